```python
import math
import jax
import jax.numpy as jnp
from jax import lax
import numpy as np

D_MODEL = 2048
BATCH = 8
SEQ = 2048
DEPTH = 1

MEM_LEN = 256
RMS_EPS = 1e-6
DN_HEAD_DIM = 128
DN_WIDTH = D_MODEL // 2
DN_HEADS = DN_WIDTH // DN_HEAD_DIM
DN_CONV = 4
DN_CHUNK = 64
RW_HEAD_DIM = 64
RW_WIDTH = D_MODEL - DN_WIDTH
RW_HEADS = RW_WIDTH // RW_HEAD_DIM
RW_DECAY_LORA = 64
RW_AAA_LORA = 64
RW_GATE_LORA = 128
RW_GN_EPS = 64e-5
DN_COLS = 4 * DN_WIDTH + 2 * DN_HEADS
RW_COLS = 3 * RW_WIDTH + RW_DECAY_LORA + RW_AAA_LORA + RW_GATE_LORA
IN_COLS = DN_COLS + RW_COLS
XA_HEADS = 4
XA_HEAD_DIM = 128
XA_WIDTH = XA_HEADS * XA_HEAD_DIM
FFN_HIDDEN = 4 * D_MODEL

kernel_name = 'hybrid_gdn_rwkv7_memxattn_layer'


def rmsnorm(x, w):
    xf = x.astype(jnp.float32)
    y = xf * lax.rsqrt(jnp.mean(xf * xf, axis=-1, keepdims=True) + RMS_EPS)
    return (y * w.astype(jnp.float32)).astype(x.dtype)


def l2norm(x):
    return x * lax.rsqrt(jnp.sum(x * x, axis=-1, keepdims=True) + 1e-6)


def causal_depthwise_conv(x, w):
    k = w.shape[0]
    return lax.conv_general_dilated(
        x, w[:, None, :].astype(x.dtype), window_strides=(1,), padding=[(k - 1, 0)],
        dimension_numbers=('NWC', 'WIO', 'NWC'), feature_group_count=x.shape[-1])


def token_shift(p):
    return jnp.pad(p, ((0, 0), (1, 0), (0, 0)))[:, :-1]


def chunked_gated_delta_rule(q, k, v, g, beta):
    b, s, h, dk = q.shape
    dv = v.shape[-1]
    c = DN_CHUNK
    n = s // c

    def chunks(t):
        return t.reshape(b, n, c, h, -1).transpose(0, 3, 1, 2, 4)

    q = chunks(q) * (dk ** -0.5)
    k = chunks(k)
    v = chunks(v)
    g = g.reshape(b, n, c, h).transpose(0, 3, 1, 2)
    beta = beta.reshape(b, n, c, h).transpose(0, 3, 1, 2)
    gc = jnp.cumsum(g, axis=-1)
    idx = jnp.arange(c)
    causal = idx[:, None] >= idx[None, :]
    strict = idx[:, None] > idx[None, :]
    decay = jnp.exp(jnp.where(causal, gc[..., :, None] - gc[..., None, :], -jnp.inf))
    kb = k * beta[..., None]
    a_strict = jnp.where(strict, jnp.einsum('bhncd,bhnmd->bhncm', kb, k) * decay, 0.0)
    lower = a_strict + jnp.eye(c, dtype=q.dtype)
    rhs = jnp.concatenate([v * beta[..., None], kb * jnp.exp(gc)[..., None]], axis=-1)
    sol = lax.linalg.triangular_solve(lower, rhs, left_side=True, lower=True)
    u, w = sol[..., :dv], sol[..., dv:]
    attn = jnp.einsum('bhncd,bhnmd->bhncm', q, k) * decay
    qg = q * jnp.exp(gc)[..., None]
    g_last = gc[..., -1]
    kd = k * jnp.exp(g_last[..., None] - gc)[..., None]
    xs = tuple(jnp.moveaxis(t, 2, 0) for t in (u, w, qg, attn, kd, g_last))

    def step(state, xs_n):
        u_n, w_n, qg_n, attn_n, kd_n, gl_n = xs_n
        v_new = u_n - jnp.einsum('bhcd,bhde->bhce', w_n, state)
        o_n = jnp.einsum('bhcd,bhde->bhce', qg_n, state) + jnp.einsum('bhcm,bhme->bhce', attn_n, v_new)
        state = state * jnp.exp(gl_n)[..., None, None] + jnp.einsum('bhcd,bhce->bhde', kd_n, v_new)
        return state, o_n

    s0 = jnp.zeros((b, h, dk, dv), q.dtype)
    _, o = lax.scan(step, s0, xs)
    return o.transpose(1, 0, 3, 2, 4).reshape(b, s, h, dv)


def wkv7_scan(r, w, k, v, kk, a):
    b, s, h, nd = r.shape
    xs = tuple(jnp.swapaxes(t, 0, 1) for t in (r, w, k, v, kk, a))

    def step(state, xs_t):
        r_t, w_t, k_t, v_t, kk_t, a_t = xs_t
        sa = jnp.einsum('bhvk,bhk->bhv', state, -kk_t)
        state = (state * w_t[:, :, None, :] + sa[..., None] * (kk_t * a_t)[:, :, None, :]
                 + v_t[..., None] * k_t[:, :, None, :])
        return state, jnp.einsum('bhvk,bhk->bhv', state, r_t)

    s0 = jnp.zeros((b, h, nd, nd), r.dtype)
    _, y = lax.scan(step, s0, xs)
    return jnp.swapaxes(y, 0, 1)


def deltanet_group(p, conv_w, a_log, dt_bias, norm_w):
    p = p.astype(jnp.float32)
    b, s, _ = p.shape
    qkv, z, ga, gb = jnp.split(p, [3 * DN_WIDTH, 4 * DN_WIDTH, 4 * DN_WIDTH + DN_HEADS], axis=-1)
    qkv = jax.nn.silu(causal_depthwise_conv(qkv, conv_w))
    q, k, v = (t.reshape(b, s, DN_HEADS, DN_HEAD_DIM) for t in jnp.split(qkv, 3, axis=-1))
    q = l2norm(q)
    k = l2norm(k)
    beta = jax.nn.sigmoid(gb)
    g = -jnp.exp(a_log) * jax.nn.softplus(ga + dt_bias)
    o = chunked_gated_delta_rule(q, k, v, g, beta)
    o = rmsnorm(o, norm_w) * jax.nn.silu(z.reshape(b, s, DN_HEADS, DN_HEAD_DIM))
    return o.reshape(b, s, DN_WIDTH)


def rwkv7_group(p, mu, w0, w2, a0, a2, g2, k_k, k_a, r_k, ln_w, ln_b):
    p = p.astype(jnp.float32)
    b, s, _ = p.shape
    p = p + (token_shift(p) - p) * mu
    pr, pk, pv, pw, pa, pg = jnp.split(
        p, [RW_WIDTH, 2 * RW_WIDTH, 3 * RW_WIDTH, 3 * RW_WIDTH + RW_DECAY_LORA,
            3 * RW_WIDTH + RW_DECAY_LORA + RW_AAA_LORA], axis=-1)

    def heads(t):
        return t.reshape(b, s, RW_HEADS, RW_HEAD_DIM)

    log_w = -jax.nn.softplus(-(w0 + jnp.tanh(pw) @ w2)) - 0.5
    decay = jnp.exp(-jnp.exp(log_w))
    a = jax.nn.sigmoid(a0 + pa @ a2)
    gate = jax.nn.sigmoid(pg) @ g2
    kk = heads(pk * k_k)
    kk = kk / jnp.maximum(jnp.sqrt(jnp.sum(kk * kk, axis=-1, keepdims=True)), 1e-12)
    k = pk * (1.0 + (a - 1.0) * k_a)
    r_h, k_h, v_h = heads(pr), heads(k), heads(pv)
    y = wkv7_scan(r_h, heads(decay), k_h, v_h, kk, heads(a))
    mean = jnp.mean(y, axis=-1, keepdims=True)
    var = jnp.mean(jnp.square(y - mean), axis=-1, keepdims=True)
    y = ((y - mean) * lax.rsqrt(var + RW_GN_EPS)).reshape(b, s, RW_WIDTH) * ln_w + ln_b
    bonus = jnp.sum(r_h * k_h * r_k, axis=-1, keepdims=True) * v_h
    return (y + bonus.reshape(b, s, RW_WIDTH)) * gate


def memory_cross_attention(hn, mn, wq, wk, wv, wo):
    b, s, _ = hn.shape
    m = mn.shape[1]
    q = (hn @ wq).reshape(b, s, XA_HEADS, XA_HEAD_DIM)
    k = (mn @ wk).reshape(b, m, XA_HEADS, XA_HEAD_DIM)
    v = (mn @ wv).reshape(b, m, XA_HEADS, XA_HEAD_DIM)
    scores = jnp.einsum('bshd,bmhd->bhsm', q, k).astype(jnp.float32) * (XA_HEAD_DIM ** -0.5)
    probs = jax.nn.softmax(scores, axis=-1).astype(v.dtype)
    o = jnp.einsum('bhsm,bmhd->bshd', probs, v).reshape(b, s, XA_WIDTH)
    return o @ wo


def squared_relu_mlp(u, w1, w2):
    return jnp.square(jax.nn.relu(u @ w1)) @ w2


def _fwd_setup_inputs(seed: int = 0) -> dict:
    key = jax.random.key(seed)
    ks = jax.random.split(key, 32)
    L = DEPTH
    D = D_MODEL

    def normal(i, shape, scale):
        return jax.random.normal(ks[i], shape, jnp.float32) * scale

    def uniform(i, shape, lo, hi):
        return jax.random.uniform(ks[i], shape, jnp.float32, lo, hi)

    dt = jnp.exp(uniform(6, (L, DN_HEADS), math.log(1e-3), math.log(1e-1)))
    return {
        'x': normal(0, (BATCH, SEQ, D), 1.0),
        'mem': normal(1, (BATCH, MEM_LEN, D), 1.0),
        'mix_norm_w': 1.0 + normal(2, (L, D), 0.02),
        'w_in': normal(3, (L, D, IN_COLS), D ** -0.5),
        'dn_conv_w': normal(4, (L, DN_CONV, 3 * DN_WIDTH), DN_CONV ** -0.5),
        'dn_a_log': jnp.log(uniform(5, (L, DN_HEADS), 1.0, 16.0)),
        'dn_dt_bias': dt + jnp.log(-jnp.expm1(-dt)),
        'dn_norm_w': 1.0 + normal(7, (L, DN_HEAD_DIM), 0.02),
        'rw_mu': uniform(8, (L, RW_COLS), 0.0, 1.0),
        'rw_w0': uniform(9, (L, RW_WIDTH), -6.0, 1.0),
        'rw_w2': normal(10, (L, RW_DECAY_LORA, RW_WIDTH), 0.1 * RW_DECAY_LORA ** -0.5),
        'rw_a0': normal(11, (L, RW_WIDTH), 0.1),
        'rw_a2': normal(12, (L, RW_AAA_LORA, RW_WIDTH), 0.1 * RW_AAA_LORA ** -0.5),
        'rw_g2': normal(13, (L, RW_GATE_LORA, RW_WIDTH), RW_GATE_LORA ** -0.5),
        'rw_k_k': 0.85 + normal(14, (L, RW_WIDTH), 0.02),
        'rw_k_a': 1.0 + normal(15, (L, RW_WIDTH), 0.02),
        'rw_r_k': normal(16, (L, RW_HEADS, RW_HEAD_DIM), 0.1),
        'rw_ln_w': 1.0 + normal(17, (L, RW_WIDTH), 0.02),
        'rw_ln_b': normal(18, (L, RW_WIDTH), 0.02),
        'w_out': normal(19, (L, D, D), D ** -0.5),
        'xa_norm_w': 1.0 + normal(20, (L, D), 0.02),
        'mem_norm_w': 1.0 + normal(21, (L, D), 0.02),
        'xa_wq': normal(22, (L, D, XA_WIDTH), D ** -0.5),
        'xa_wk': normal(23, (L, D, XA_WIDTH), D ** -0.5),
        'xa_wv': normal(24, (L, D, XA_WIDTH), D ** -0.5),
        'xa_wo': normal(25, (L, XA_WIDTH, D), XA_WIDTH ** -0.5),
        'ffn_norm_w': 1.0 + normal(26, (L, D), 0.02),
        'ffn_w1': normal(27, (L, D, FFN_HIDDEN), D ** -0.5),
        'ffn_w2': normal(28, (L, FFN_HIDDEN, D), FFN_HIDDEN ** -0.5),
        'final_norm_w': 1.0 + normal(29, (D,), 0.02),
    }


def _fwd_reference(x, mem, mix_norm_w, w_in, dn_conv_w, dn_a_log, dn_dt_bias, dn_norm_w, rw_mu, rw_w0,
              rw_w2, rw_a0, rw_a2, rw_g2, rw_k_k, rw_k_a, rw_r_k, rw_ln_w, rw_ln_b, w_out,
              xa_norm_w, mem_norm_w, xa_wq, xa_wk, xa_wv, xa_wo, ffn_norm_w, ffn_w1, ffn_w2,
              final_norm_w):
    h = x
    for l in range(DEPTH):
        u = rmsnorm(h, mix_norm_w[l])
        p = u @ w_in[l]
        o_dn = deltanet_group(p[..., :DN_COLS], dn_conv_w[l], dn_a_log[l], dn_dt_bias[l], dn_norm_w[l])
        o_rw = rwkv7_group(p[..., DN_COLS:], rw_mu[l], rw_w0[l], rw_w2[l], rw_a0[l], rw_a2[l], rw_g2[l],
                           rw_k_k[l], rw_k_a[l], rw_r_k[l], rw_ln_w[l], rw_ln_b[l])
        h = h + jnp.concatenate([o_dn, o_rw], axis=-1).astype(h.dtype) @ w_out[l]
        h = h + memory_cross_attention(rmsnorm(h, xa_norm_w[l]), rmsnorm(mem, mem_norm_w[l]),
                                       xa_wq[l], xa_wk[l], xa_wv[l], xa_wo[l])
        h = h + squared_relu_mlp(rmsnorm(h, ffn_norm_w[l]), ffn_w1[l], ffn_w2[l])
    return rmsnorm(h, final_norm_w)


import jax as _jax
import jax.numpy as _jnp

TWIN_FORMAT = 'train_step'
FWD_PARAMS = ['x', 'mem', 'mix_norm_w', 'w_in', 'dn_conv_w', 'dn_a_log', 'dn_dt_bias', 'dn_norm_w', 'rw_mu', 'rw_w0', 'rw_w2', 'rw_a0', 'rw_a2', 'rw_g2', 'rw_k_k', 'rw_k_a', 'rw_r_k', 'rw_ln_w', 'rw_ln_b', 'w_out', 'xa_norm_w', 'mem_norm_w', 'xa_wq', 'xa_wk', 'xa_wv', 'xa_wo', 'ffn_norm_w', 'ffn_w1', 'ffn_w2', 'final_norm_w']
TWIN_WEIGHTS = ['mix_norm_w', 'w_in', 'dn_conv_w', 'dn_a_log', 'dn_dt_bias', 'dn_norm_w', 'rw_mu', 'rw_w0', 'rw_w2', 'rw_a0', 'rw_a2', 'rw_g2', 'rw_k_k', 'rw_k_a', 'rw_r_k', 'rw_ln_w', 'rw_ln_b', 'w_out', 'xa_norm_w', 'mem_norm_w', 'xa_wq', 'xa_wk', 'xa_wv', 'xa_wo', 'ffn_norm_w', 'ffn_w1', 'ffn_w2', 'final_norm_w']
TWIN_DIFF_INPUT = 'x'
TWIN_INPUTS = ['x', 'mem', 'mix_norm_w', 'w_in', 'dn_conv_w', 'dn_a_log', 'dn_dt_bias', 'dn_norm_w', 'rw_mu', 'rw_w0', 'rw_w2', 'rw_a0', 'rw_a2', 'rw_g2', 'rw_k_k', 'rw_k_a', 'rw_r_k', 'rw_ln_w', 'rw_ln_b', 'w_out', 'xa_norm_w', 'mem_norm_w', 'xa_wq', 'xa_wk', 'xa_wv', 'xa_wo', 'ffn_norm_w', 'ffn_w1', 'ffn_w2', 'final_norm_w', 'loss_target', 'm_mix_norm_w', 'm_w_in', 'm_dn_conv_w', 'm_dn_a_log', 'm_dn_dt_bias', 'm_dn_norm_w', 'm_rw_mu', 'm_rw_w0', 'm_rw_w2', 'm_rw_a0', 'm_rw_a2', 'm_rw_g2', 'm_rw_k_k', 'm_rw_k_a', 'm_rw_r_k', 'm_rw_ln_w', 'm_rw_ln_b', 'm_w_out', 'm_xa_norm_w', 'm_mem_norm_w', 'm_xa_wq', 'm_xa_wk', 'm_xa_wv', 'm_xa_wo', 'm_ffn_norm_w', 'm_ffn_w1', 'm_ffn_w2', 'm_final_norm_w', 'v_mix_norm_w', 'v_w_in', 'v_dn_conv_w', 'v_dn_a_log', 'v_dn_dt_bias', 'v_dn_norm_w', 'v_rw_mu', 'v_rw_w0', 'v_rw_w2', 'v_rw_a0', 'v_rw_a2', 'v_rw_g2', 'v_rw_k_k', 'v_rw_k_a', 'v_rw_r_k', 'v_rw_ln_w', 'v_rw_ln_b', 'v_w_out', 'v_xa_norm_w', 'v_mem_norm_w', 'v_xa_wq', 'v_xa_wk', 'v_xa_wv', 'v_xa_wo', 'v_ffn_norm_w', 'v_ffn_w1', 'v_ffn_w2', 'v_final_norm_w']
TWIN_OUTPUTS = ['loss', 'grad_x', 'grad_mix_norm_w', 'grad_w_in', 'grad_dn_conv_w', 'grad_dn_a_log', 'grad_dn_dt_bias', 'grad_dn_norm_w', 'grad_rw_mu', 'grad_rw_w0', 'grad_rw_w2', 'grad_rw_a0', 'grad_rw_a2', 'grad_rw_g2', 'grad_rw_k_k', 'grad_rw_k_a', 'grad_rw_r_k', 'grad_rw_ln_w', 'grad_rw_ln_b', 'grad_w_out', 'grad_xa_norm_w', 'grad_mem_norm_w', 'grad_xa_wq', 'grad_xa_wk', 'grad_xa_wv', 'grad_xa_wo', 'grad_ffn_norm_w', 'grad_ffn_w1', 'grad_ffn_w2', 'grad_final_norm_w', 'delta_mix_norm_w', 'delta_w_in', 'delta_dn_conv_w', 'delta_dn_a_log', 'delta_dn_dt_bias', 'delta_dn_norm_w', 'delta_rw_mu', 'delta_rw_w0', 'delta_rw_w2', 'delta_rw_a0', 'delta_rw_a2', 'delta_rw_g2', 'delta_rw_k_k', 'delta_rw_k_a', 'delta_rw_r_k', 'delta_rw_ln_w', 'delta_rw_ln_b', 'delta_w_out', 'delta_xa_norm_w', 'delta_mem_norm_w', 'delta_xa_wq', 'delta_xa_wk', 'delta_xa_wv', 'delta_xa_wo', 'delta_ffn_norm_w', 'delta_ffn_w1', 'delta_ffn_w2', 'delta_final_norm_w', 'new_m_mix_norm_w', 'new_m_w_in', 'new_m_dn_conv_w', 'new_m_dn_a_log', 'new_m_dn_dt_bias', 'new_m_dn_norm_w', 'new_m_rw_mu', 'new_m_rw_w0', 'new_m_rw_w2', 'new_m_rw_a0', 'new_m_rw_a2', 'new_m_rw_g2', 'new_m_rw_k_k', 'new_m_rw_k_a', 'new_m_rw_r_k', 'new_m_rw_ln_w', 'new_m_rw_ln_b', 'new_m_w_out', 'new_m_xa_norm_w', 'new_m_mem_norm_w', 'new_m_xa_wq', 'new_m_xa_wk', 'new_m_xa_wv', 'new_m_xa_wo', 'new_m_ffn_norm_w', 'new_m_ffn_w1', 'new_m_ffn_w2', 'new_m_final_norm_w', 'new_v_mix_norm_w', 'new_v_w_in', 'new_v_dn_conv_w', 'new_v_dn_a_log', 'new_v_dn_dt_bias', 'new_v_dn_norm_w', 'new_v_rw_mu', 'new_v_rw_w0', 'new_v_rw_w2', 'new_v_rw_a0', 'new_v_rw_a2', 'new_v_rw_g2', 'new_v_rw_k_k', 'new_v_rw_k_a', 'new_v_rw_r_k', 'new_v_rw_ln_w', 'new_v_rw_ln_b', 'new_v_w_out', 'new_v_xa_norm_w', 'new_v_mem_norm_w', 'new_v_xa_wq', 'new_v_xa_wk', 'new_v_xa_wv', 'new_v_xa_wo', 'new_v_ffn_norm_w', 'new_v_ffn_w1', 'new_v_ffn_w2', 'new_v_final_norm_w']
TWIN_LEAF_KINDS = {'loss': 'loss', 'grad_x': 'grad_x', 'grad_mix_norm_w': 'grad_w', 'grad_w_in': 'grad_w', 'grad_dn_conv_w': 'grad_w', 'grad_dn_a_log': 'grad_w', 'grad_dn_dt_bias': 'grad_w', 'grad_dn_norm_w': 'grad_w', 'grad_rw_mu': 'grad_w', 'grad_rw_w0': 'grad_w', 'grad_rw_w2': 'grad_w', 'grad_rw_a0': 'grad_w', 'grad_rw_a2': 'grad_w', 'grad_rw_g2': 'grad_w', 'grad_rw_k_k': 'grad_w', 'grad_rw_k_a': 'grad_w', 'grad_rw_r_k': 'grad_w', 'grad_rw_ln_w': 'grad_w', 'grad_rw_ln_b': 'grad_w', 'grad_w_out': 'grad_w', 'grad_xa_norm_w': 'grad_w', 'grad_mem_norm_w': 'grad_w', 'grad_xa_wq': 'grad_w', 'grad_xa_wk': 'grad_w', 'grad_xa_wv': 'grad_w', 'grad_xa_wo': 'grad_w', 'grad_ffn_norm_w': 'grad_w', 'grad_ffn_w1': 'grad_w', 'grad_ffn_w2': 'grad_w', 'grad_final_norm_w': 'grad_w', 'delta_mix_norm_w': 'delta_w', 'delta_w_in': 'delta_w', 'delta_dn_conv_w': 'delta_w', 'delta_dn_a_log': 'delta_w', 'delta_dn_dt_bias': 'delta_w', 'delta_dn_norm_w': 'delta_w', 'delta_rw_mu': 'delta_w', 'delta_rw_w0': 'delta_w', 'delta_rw_w2': 'delta_w', 'delta_rw_a0': 'delta_w', 'delta_rw_a2': 'delta_w', 'delta_rw_g2': 'delta_w', 'delta_rw_k_k': 'delta_w', 'delta_rw_k_a': 'delta_w', 'delta_rw_r_k': 'delta_w', 'delta_rw_ln_w': 'delta_w', 'delta_rw_ln_b': 'delta_w', 'delta_w_out': 'delta_w', 'delta_xa_norm_w': 'delta_w', 'delta_mem_norm_w': 'delta_w', 'delta_xa_wq': 'delta_w', 'delta_xa_wk': 'delta_w', 'delta_xa_wv': 'delta_w', 'delta_xa_wo': 'delta_w', 'delta_ffn_norm_w': 'delta_w', 'delta_ffn_w1': 'delta_w', 'delta_ffn_w2': 'delta_w', 'delta_final_norm_w': 'delta_w', 'new_m_mix_norm_w': 'new_m', 'new_m_w_in': 'new_m', 'new_m_dn_conv_w': 'new_m', 'new_m_dn_a_log': 'new_m', 'new_m_dn_dt_bias': 'new_m', 'new_m_dn_norm_w': 'new_m', 'new_m_rw_mu': 'new_m', 'new_m_rw_w0': 'new_m', 'new_m_rw_w2': 'new_m', 'new_m_rw_a0': 'new_m', 'new_m_rw_a2': 'new_m', 'new_m_rw_g2': 'new_m', 'new_m_rw_k_k': 'new_m', 'new_m_rw_k_a': 'new_m', 'new_m_rw_r_k': 'new_m', 'new_m_rw_ln_w': 'new_m', 'new_m_rw_ln_b': 'new_m', 'new_m_w_out': 'new_m', 'new_m_xa_norm_w': 'new_m', 'new_m_mem_norm_w': 'new_m', 'new_m_xa_wq': 'new_m', 'new_m_xa_wk': 'new_m', 'new_m_xa_wv': 'new_m', 'new_m_xa_wo': 'new_m', 'new_m_ffn_norm_w': 'new_m', 'new_m_ffn_w1': 'new_m', 'new_m_ffn_w2': 'new_m', 'new_m_final_norm_w': 'new_m', 'new_v_mix_norm_w': 'new_v', 'new_v_w_in': 'new_v', 'new_v_dn_conv_w': 'new_v', 'new_v_dn_a_log': 'new_v', 'new_v_dn_dt_bias': 'new_v', 'new_v_dn_norm_w': 'new_v', 'new_v_rw_mu': 'new_v', 'new_v_rw_w0': 'new_v', 'new_v_rw_w2': 'new_v', 'new_v_rw_a0': 'new_v', 'new_v_rw_a2': 'new_v', 'new_v_rw_g2': 'new_v', 'new_v_rw_k_k': 'new_v', 'new_v_rw_k_a': 'new_v', 'new_v_rw_r_k': 'new_v', 'new_v_rw_ln_w': 'new_v', 'new_v_rw_ln_b': 'new_v', 'new_v_w_out': 'new_v', 'new_v_xa_norm_w': 'new_v', 'new_v_mem_norm_w': 'new_v', 'new_v_xa_wq': 'new_v', 'new_v_xa_wk': 'new_v', 'new_v_xa_wv': 'new_v', 'new_v_xa_wo': 'new_v', 'new_v_ffn_norm_w': 'new_v', 'new_v_ffn_w1': 'new_v', 'new_v_ffn_w2': 'new_v', 'new_v_final_norm_w': 'new_v'}


def _forward(args):
    return _fwd_reference(*[args[k] for k in FWD_PARAMS])


def _output_shape():
    out = _jax.eval_shape(lambda: _forward(_fwd_setup_inputs(0)))
    return out.shape, out.dtype

N_MICROBATCH = 1
ADAM_LR = 0.001
ADAM_B1 = 0.9
ADAM_B2 = 0.999
ADAM_EPS = 1e-08
ADAM_WD = 0.01
ADAM_STEP = 10
PER_EXAMPLE_BATCH_AXIS = {'x': 0, 'mem': 0, 'loss_target': 0}
SHARED_INPUTS = []
_WEIGHT_DTYPES = {'mix_norm_w': _jnp.float32, 'w_in': _jnp.float32, 'dn_conv_w': _jnp.float32, 'dn_a_log': _jnp.float32, 'dn_dt_bias': _jnp.float32, 'dn_norm_w': _jnp.float32, 'rw_mu': _jnp.float32, 'rw_w0': _jnp.float32, 'rw_w2': _jnp.float32, 'rw_a0': _jnp.float32, 'rw_a2': _jnp.float32, 'rw_g2': _jnp.float32, 'rw_k_k': _jnp.float32, 'rw_k_a': _jnp.float32, 'rw_r_k': _jnp.float32, 'rw_ln_w': _jnp.float32, 'rw_ln_b': _jnp.float32, 'w_out': _jnp.float32, 'xa_norm_w': _jnp.float32, 'mem_norm_w': _jnp.float32, 'xa_wq': _jnp.float32, 'xa_wk': _jnp.float32, 'xa_wv': _jnp.float32, 'xa_wo': _jnp.float32, 'ffn_norm_w': _jnp.float32, 'ffn_w1': _jnp.float32, 'ffn_w2': _jnp.float32, 'final_norm_w': _jnp.float32}
MOMENT_SCALE = {'mix_norm_w': 6.045856e-02, 'w_in': 3.136742e-02, 'dn_conv_w': 2.646547e-02, 'dn_a_log': 1.590767e-01, 'dn_dt_bias': 1.535547e-01, 'dn_norm_w': 1.164484e-01, 'rw_mu': 5.657414e-02, 'rw_w0': 1.235529e-02, 'rw_w2': 1.611125e-03, 'rw_a0': 1.490082e-02, 'rw_a2': 1.394198e-02, 'rw_g2': 3.299132e-02, 'rw_k_k': 3.492401e-02, 'rw_k_a': 3.520979e-02, 'rw_r_k': 6.795281e-02, 'rw_ln_w': 3.199886e-02, 'rw_ln_b': 3.904637e-02, 'w_out': 3.326748e-02, 'xa_norm_w': 6.418485e-03, 'mem_norm_w': 9.180187e-03, 'xa_wq': 1.218868e-02, 'xa_wk': 1.228459e-02, 'xa_wv': 1.275149e-02, 'xa_wo': 6.354833e-03, 'ffn_norm_w': 5.345856e-02, 'ffn_w1': 2.640445e-02, 'ffn_w2': 4.964275e-02, 'final_norm_w': 8.057782e+00}


def _to_microbatches(a, axis):
    t = _jnp.moveaxis(a, axis, 0)
    t = t.reshape((N_MICROBATCH, t.shape[0] // N_MICROBATCH) + t.shape[1:])
    return _jnp.moveaxis(t, 1, axis + 1)


def setup_inputs(seed: int = 0) -> dict:
    inp = _fwd_setup_inputs(seed)
    key = _jax.random.fold_in(_jax.random.key(seed), 7919)
    shape, _ = _output_shape()
    out = dict(inp)
    out["loss_target"] = _jax.random.normal(_jax.random.fold_in(key, 0), shape, _jnp.float32)
    for i, name in enumerate(TWIN_WEIGHTS):
        w = inp[name].astype(_jnp.float32)
        if MOMENT_SCALE is None:
            s = _jnp.sqrt(_jnp.mean(_jnp.square(w)) + 1e-30)
        else:
            s = MOMENT_SCALE[name]
        km, kv = _jax.random.split(_jax.random.fold_in(key, i + 1))
        out[name] = w
        out["m_" + name] = s * _jax.random.normal(km, w.shape, _jnp.float32)
        out["v_" + name] = (s * s) * _jax.random.uniform(kv, w.shape, _jnp.float32, 0.5, 1.5)
    if N_MICROBATCH > 1:
        for name, axis in PER_EXAMPLE_BATCH_AXIS.items():
            out[name] = _to_microbatches(out[name], axis)
    return {'x': out['x'], 'mem': out['mem'], 'mix_norm_w': out['mix_norm_w'], 'w_in': out['w_in'], 'dn_conv_w': out['dn_conv_w'], 'dn_a_log': out['dn_a_log'], 'dn_dt_bias': out['dn_dt_bias'], 'dn_norm_w': out['dn_norm_w'], 'rw_mu': out['rw_mu'], 'rw_w0': out['rw_w0'], 'rw_w2': out['rw_w2'], 'rw_a0': out['rw_a0'], 'rw_a2': out['rw_a2'], 'rw_g2': out['rw_g2'], 'rw_k_k': out['rw_k_k'], 'rw_k_a': out['rw_k_a'], 'rw_r_k': out['rw_r_k'], 'rw_ln_w': out['rw_ln_w'], 'rw_ln_b': out['rw_ln_b'], 'w_out': out['w_out'], 'xa_norm_w': out['xa_norm_w'], 'mem_norm_w': out['mem_norm_w'], 'xa_wq': out['xa_wq'], 'xa_wk': out['xa_wk'], 'xa_wv': out['xa_wv'], 'xa_wo': out['xa_wo'], 'ffn_norm_w': out['ffn_norm_w'], 'ffn_w1': out['ffn_w1'], 'ffn_w2': out['ffn_w2'], 'final_norm_w': out['final_norm_w'], 'loss_target': out['loss_target'], 'm_mix_norm_w': out['m_mix_norm_w'], 'm_w_in': out['m_w_in'], 'm_dn_conv_w': out['m_dn_conv_w'], 'm_dn_a_log': out['m_dn_a_log'], 'm_dn_dt_bias': out['m_dn_dt_bias'], 'm_dn_norm_w': out['m_dn_norm_w'], 'm_rw_mu': out['m_rw_mu'], 'm_rw_w0': out['m_rw_w0'], 'm_rw_w2': out['m_rw_w2'], 'm_rw_a0': out['m_rw_a0'], 'm_rw_a2': out['m_rw_a2'], 'm_rw_g2': out['m_rw_g2'], 'm_rw_k_k': out['m_rw_k_k'], 'm_rw_k_a': out['m_rw_k_a'], 'm_rw_r_k': out['m_rw_r_k'], 'm_rw_ln_w': out['m_rw_ln_w'], 'm_rw_ln_b': out['m_rw_ln_b'], 'm_w_out': out['m_w_out'], 'm_xa_norm_w': out['m_xa_norm_w'], 'm_mem_norm_w': out['m_mem_norm_w'], 'm_xa_wq': out['m_xa_wq'], 'm_xa_wk': out['m_xa_wk'], 'm_xa_wv': out['m_xa_wv'], 'm_xa_wo': out['m_xa_wo'], 'm_ffn_norm_w': out['m_ffn_norm_w'], 'm_ffn_w1': out['m_ffn_w1'], 'm_ffn_w2': out['m_ffn_w2'], 'm_final_norm_w': out['m_final_norm_w'], 'v_mix_norm_w': out['v_mix_norm_w'], 'v_w_in': out['v_w_in'], 'v_dn_conv_w': out['v_dn_conv_w'], 'v_dn_a_log': out['v_dn_a_log'], 'v_dn_dt_bias': out['v_dn_dt_bias'], 'v_dn_norm_w': out['v_dn_norm_w'], 'v_rw_mu': out['v_rw_mu'], 'v_rw_w0': out['v_rw_w0'], 'v_rw_w2': out['v_rw_w2'], 'v_rw_a0': out['v_rw_a0'], 'v_rw_a2': out['v_rw_a2'], 'v_rw_g2': out['v_rw_g2'], 'v_rw_k_k': out['v_rw_k_k'], 'v_rw_k_a': out['v_rw_k_a'], 'v_rw_r_k': out['v_rw_r_k'], 'v_rw_ln_w': out['v_rw_ln_w'], 'v_rw_ln_b': out['v_rw_ln_b'], 'v_w_out': out['v_w_out'], 'v_xa_norm_w': out['v_xa_norm_w'], 'v_mem_norm_w': out['v_mem_norm_w'], 'v_xa_wq': out['v_xa_wq'], 'v_xa_wk': out['v_xa_wk'], 'v_xa_wv': out['v_xa_wv'], 'v_xa_wo': out['v_xa_wo'], 'v_ffn_norm_w': out['v_ffn_norm_w'], 'v_ffn_w1': out['v_ffn_w1'], 'v_ffn_w2': out['v_ffn_w2'], 'v_final_norm_w': out['v_final_norm_w']}


def _loss(weights, diff, rest, loss_target):
    with _jax.named_scope("forward"):
        args = {**rest, TWIN_DIFF_INPUT: diff, **{k: w.astype(_WEIGHT_DTYPES[k]) for k, w in weights.items()}}
        y = _forward(args)
    with _jax.named_scope("loss_head"):
        err = _jnp.square(y.astype(_jnp.float32) - loss_target)
        return 0.5 * _jnp.sum(_jnp.mean(err, axis=-1)) if err.ndim else 0.5 * err


def _adamw(w, g, m, v):
    m = ADAM_B1 * m + (1.0 - ADAM_B1) * g
    v = ADAM_B2 * v + (1.0 - ADAM_B2) * _jnp.square(g)
    m_hat = m / (1.0 - ADAM_B1 ** ADAM_STEP)
    v_hat = v / (1.0 - ADAM_B2 ** ADAM_STEP)
    delta = -ADAM_LR * (m_hat / (_jnp.sqrt(v_hat) + ADAM_EPS) + ADAM_WD * w)
    return delta, m, v


def reference(x, mem, mix_norm_w, w_in, dn_conv_w, dn_a_log, dn_dt_bias, dn_norm_w, rw_mu, rw_w0, rw_w2, rw_a0, rw_a2, rw_g2, rw_k_k, rw_k_a, rw_r_k, rw_ln_w, rw_ln_b, w_out, xa_norm_w, mem_norm_w, xa_wq, xa_wk, xa_wv, xa_wo, ffn_norm_w, ffn_w1, ffn_w2, final_norm_w, loss_target, m_mix_norm_w, m_w_in, m_dn_conv_w, m_dn_a_log, m_dn_dt_bias, m_dn_norm_w, m_rw_mu, m_rw_w0, m_rw_w2, m_rw_a0, m_rw_a2, m_rw_g2, m_rw_k_k, m_rw_k_a, m_rw_r_k, m_rw_ln_w, m_rw_ln_b, m_w_out, m_xa_norm_w, m_mem_norm_w, m_xa_wq, m_xa_wk, m_xa_wv, m_xa_wo, m_ffn_norm_w, m_ffn_w1, m_ffn_w2, m_final_norm_w, v_mix_norm_w, v_w_in, v_dn_conv_w, v_dn_a_log, v_dn_dt_bias, v_dn_norm_w, v_rw_mu, v_rw_w0, v_rw_w2, v_rw_a0, v_rw_a2, v_rw_g2, v_rw_k_k, v_rw_k_a, v_rw_r_k, v_rw_ln_w, v_rw_ln_b, v_w_out, v_xa_norm_w, v_mem_norm_w, v_xa_wq, v_xa_wk, v_xa_wv, v_xa_wo, v_ffn_norm_w, v_ffn_w1, v_ffn_w2, v_final_norm_w):
    given = dict(x=x, mem=mem, mix_norm_w=mix_norm_w, w_in=w_in, dn_conv_w=dn_conv_w, dn_a_log=dn_a_log, dn_dt_bias=dn_dt_bias, dn_norm_w=dn_norm_w, rw_mu=rw_mu, rw_w0=rw_w0, rw_w2=rw_w2, rw_a0=rw_a0, rw_a2=rw_a2, rw_g2=rw_g2, rw_k_k=rw_k_k, rw_k_a=rw_k_a, rw_r_k=rw_r_k, rw_ln_w=rw_ln_w, rw_ln_b=rw_ln_b, w_out=w_out, xa_norm_w=xa_norm_w, mem_norm_w=mem_norm_w, xa_wq=xa_wq, xa_wk=xa_wk, xa_wv=xa_wv, xa_wo=xa_wo, ffn_norm_w=ffn_norm_w, ffn_w1=ffn_w1, ffn_w2=ffn_w2, final_norm_w=final_norm_w, loss_target=loss_target, m_mix_norm_w=m_mix_norm_w, m_w_in=m_w_in, m_dn_conv_w=m_dn_conv_w, m_dn_a_log=m_dn_a_log, m_dn_dt_bias=m_dn_dt_bias, m_dn_norm_w=m_dn_norm_w, m_rw_mu=m_rw_mu, m_rw_w0=m_rw_w0, m_rw_w2=m_rw_w2, m_rw_a0=m_rw_a0, m_rw_a2=m_rw_a2, m_rw_g2=m_rw_g2, m_rw_k_k=m_rw_k_k, m_rw_k_a=m_rw_k_a, m_rw_r_k=m_rw_r_k, m_rw_ln_w=m_rw_ln_w, m_rw_ln_b=m_rw_ln_b, m_w_out=m_w_out, m_xa_norm_w=m_xa_norm_w, m_mem_norm_w=m_mem_norm_w, m_xa_wq=m_xa_wq, m_xa_wk=m_xa_wk, m_xa_wv=m_xa_wv, m_xa_wo=m_xa_wo, m_ffn_norm_w=m_ffn_norm_w, m_ffn_w1=m_ffn_w1, m_ffn_w2=m_ffn_w2, m_final_norm_w=m_final_norm_w, v_mix_norm_w=v_mix_norm_w, v_w_in=v_w_in, v_dn_conv_w=v_dn_conv_w, v_dn_a_log=v_dn_a_log, v_dn_dt_bias=v_dn_dt_bias, v_dn_norm_w=v_dn_norm_w, v_rw_mu=v_rw_mu, v_rw_w0=v_rw_w0, v_rw_w2=v_rw_w2, v_rw_a0=v_rw_a0, v_rw_a2=v_rw_a2, v_rw_g2=v_rw_g2, v_rw_k_k=v_rw_k_k, v_rw_k_a=v_rw_k_a, v_rw_r_k=v_rw_r_k, v_rw_ln_w=v_rw_ln_w, v_rw_ln_b=v_rw_ln_b, v_w_out=v_w_out, v_xa_norm_w=v_xa_norm_w, v_mem_norm_w=v_mem_norm_w, v_xa_wq=v_xa_wq, v_xa_wk=v_xa_wk, v_xa_wv=v_xa_wv, v_xa_wo=v_xa_wo, v_ffn_norm_w=v_ffn_norm_w, v_ffn_w1=v_ffn_w1, v_ffn_w2=v_ffn_w2, v_final_norm_w=v_final_norm_w)
    weights = {n: given[n] for n in TWIN_WEIGHTS}
    shared = {n: given[n] for n in SHARED_INPUTS}
    per_example = {n: given[n] for n in ['x', 'mem']}
    grad_fn = _jax.value_and_grad(_loss, argnums=(0, 1))

    def one_microbatch(ex, loss_target):
        ex = dict(ex)
        diff = ex.pop(TWIN_DIFF_INPUT)
        return grad_fn(weights, diff, {**shared, **ex}, loss_target)

    if N_MICROBATCH == 1:
        loss, (grad_w, grad_x) = one_microbatch(per_example, given["loss_target"])
    else:
        def body(carry, xs):
            loss_sum, grad_sum = carry
            l_k, (gw_k, gx_k) = one_microbatch(xs[0], xs[1])
            with _jax.named_scope("update"):
                return (loss_sum + l_k, _jax.tree.map(_jnp.add, grad_sum, gw_k)), gx_k

        init = (_jnp.zeros((), _jnp.float32), _jax.tree.map(_jnp.zeros_like, weights))
        (loss, grad_w), grad_x = _jax.lax.scan(body, init, (per_example, given["loss_target"]))
    with _jax.named_scope("update"):
        delta_w, new_m, new_v = {}, {}, {}
        for n in TWIN_WEIGHTS:
            delta_w[n], new_m[n], new_v[n] = _adamw(weights[n], grad_w[n], given["m_" + n], given["v_" + n])
    return (loss, grad_x, *[grad_w[n] for n in TWIN_WEIGHTS], *[delta_w[n] for n in TWIN_WEIGHTS],
            *[new_m[n] for n in TWIN_WEIGHTS], *[new_v[n] for n in TWIN_WEIGHTS])
```

```python
import functools

import jax
import jax.numpy as jnp
from jax import lax
from jax.experimental import pallas as pl
from jax.experimental.pallas import tpu as pltpu

f32 = jnp.float32
bf16 = jnp.bfloat16
HI = lax.Precision.HIGHEST
MESH = pl.DeviceIdType.MESH

LANES = 128
VMEM_LIMIT = 56 << 20
TOK_TILE = 256
MM_TILE = 512
MM_TILE_K = 1024
GDN_CHUNK = 64
WKV_CHUNK = 32
RMS_EPS = 1e-6
RW_GN_EPS = 64e-5
RW_HEAD = 64

ADAM_LR, ADAM_B1, ADAM_B2, ADAM_EPS, ADAM_WD, ADAM_STEP = 0.001, 0.9, 0.999, 1e-08, 0.01, 10


def _params(n_grid):
    return pltpu.CompilerParams(dimension_semantics=("arbitrary",) * n_grid, vmem_limit_bytes=VMEM_LIMIT)


def hdot(a, b):
    return lax.dot_general(a, b, (((1,), (0,)), ((), ())), precision=HI, preferred_element_type=f32)


def hdot_nt(a, b):
    return lax.dot_general(a, b, (((1,), (1,)), ((), ())), precision=HI, preferred_element_type=f32)


def hdot_tn(a, b):
    return lax.dot_general(a, b, (((0,), (0,)), ((), ())), precision=HI, preferred_element_type=f32)


_NN = (((1,), (0,)), ((), ()))
_NT = (((1,), (1,)), ((), ()))
_TN = (((0,), (0,)), ((), ()))


def _bd(a, b, dims):
    return lax.dot_general(a.astype(bf16), b.astype(bf16), dims, preferred_element_type=f32)


@jax.custom_vjp
def bdot(a, b):
    return _bd(a, b, _NN)


def _bdot_bwd(res, g):
    a, b = res
    return _bd(g, b, _NT).astype(a.dtype), _bd(a, g, _TN).astype(b.dtype)


bdot.defvjp(lambda a, b: (_bd(a, b, _NN), (a, b)), _bdot_bwd)


@jax.custom_vjp
def bdot_nt(a, b):
    return _bd(a, b, _NT)


def _bdot_nt_bwd(res, g):
    a, b = res
    return _bd(g, b, _NN).astype(a.dtype), _bd(g, a, _TN).astype(b.dtype)


bdot_nt.defvjp(lambda a, b: (_bd(a, b, _NT), (a, b)), _bdot_nt_bwd)


def _shift_rows(x, k):
    row = lax.broadcasted_iota(jnp.int32, x.shape, 0)
    return jnp.where(row < k, 0.0, pltpu.roll(x, k, axis=0))


def _unshift_rows(g, k):
    n = g.shape[0]
    row = lax.broadcasted_iota(jnp.int32, g.shape, 0)
    return jnp.where(row >= n - k, 0.0, pltpu.roll(g, n - k, axis=0))


@functools.partial(jax.custom_vjp, nondiff_argnums=(1,))
def tshift(x, k):
    return _shift_rows(x, k)


tshift.defvjp(lambda x, k: (_shift_rows(x, k), None), lambda k, _, g: (_unshift_rows(g, k),))


def rms(x, w):
    x = x.astype(f32)
    return x * lax.rsqrt(jnp.mean(x * x, axis=-1, keepdims=True) + RMS_EPS) * w


def softplus(x):
    return jnp.maximum(x, 0.0) + jnp.log(1.0 + jnp.exp(-jnp.abs(x)))


def seg2sum(x):
    lo = lax.broadcasted_iota(jnp.int32, x.shape, 1) < RW_HEAD
    s_lo = jnp.sum(jnp.where(lo, x, 0.0), axis=-1, keepdims=True)
    s_hi = jnp.sum(jnp.where(lo, 0.0, x), axis=-1, keepdims=True)
    return jnp.where(lo, s_lo, s_hi)


def _tile(n, pref):
    if n <= pref:
        return n
    t = pref
    while t >= LANES:
        if n % t == 0:
            return t
        t -= LANES
    return n


def mm(name, a, b, *, ta=False, tb=False, add=None, out_dtype=f32):
    (k, m) = a.shape if ta else a.shape[::-1]
    (n, kb) = b.shape if tb else b.shape[::-1]
    assert k == kb, (name, a.shape, b.shape)
    tm, tn, tk = _tile(m, MM_TILE), _tile(n, MM_TILE), _tile(k, MM_TILE_K)
    nk = k // tk
    dims = (((0,) if ta else (1,), (1,) if tb else (0,)), ((), ()))

    def body(*refs):
        if add is None:
            a_ref, b_ref, o_ref, acc = refs
        else:
            a_ref, b_ref, c_ref, o_ref, acc = refs
        kk = pl.program_id(2)

        @pl.when(kk == 0)
        def _():
            acc[...] = jnp.zeros_like(acc)

        acc[...] += lax.dot_general(a_ref[...].astype(bf16), b_ref[...].astype(bf16), dims,
                                    preferred_element_type=f32)

        @pl.when(kk == nk - 1)
        def _():
            r = acc[...]
            if add is not None:
                r = r + c_ref[...].astype(f32)
            o_ref[...] = r.astype(o_ref.dtype)

    a_spec = pl.BlockSpec((tk, tm), lambda i, j, q: (q, i)) if ta else pl.BlockSpec((tm, tk), lambda i, j, q: (i, q))
    b_spec = pl.BlockSpec((tn, tk), lambda i, j, q: (j, q)) if tb else pl.BlockSpec((tk, tn), lambda i, j, q: (q, j))
    o_spec = pl.BlockSpec((tm, tn), lambda i, j, q: (i, j))
    ins, specs = [a, b], [a_spec, b_spec]
    if add is not None:
        ins.append(add)
        specs.append(o_spec)
    return pl.pallas_call(
        body, name=name, grid=(m // tm, n // tn, nk), in_specs=specs, out_specs=o_spec,
        out_shape=jax.ShapeDtypeStruct((m, n), out_dtype), scratch_shapes=[pltpu.VMEM((tm, tn), f32)],
        compiler_params=_params(3))(*ins)


def _in_spec(a, kind, tile):
    if kind == "row":
        return pl.BlockSpec((tile, a.shape[1]), lambda j, i: (i, 0))
    if kind == "const":
        return pl.BlockSpec(a.shape, lambda j, i: (0, 0))
    tag, cw, off = kind
    if tag == "rowc":
        return pl.BlockSpec((tile, cw), lambda j, i: (i, j + off))
    assert tag == "constc", kind
    return pl.BlockSpec((a.shape[0], cw), lambda j, i: (0, j + off))


def rowcall(name, fn, ins, outs, *, rows, tile=None, ncol=1):
    tile = min(TOK_TILE, rows) if tile is None else tile
    n_in = len(ins)
    kinds = [o[3] for o in outs]

    def body(*refs):
        j, i = pl.program_id(0), pl.program_id(1)
        res = fn(*[r[...] for r in refs[:n_in]])
        for ref, val, kind in zip(refs[n_in:], res, kinds):
            if kind in ("row", "rowc"):
                ref[...] = val.astype(ref.dtype)
            else:
                first = (i == 0) if kind == "accc" else jnp.logical_and(i == 0, j == 0)

                @pl.when(first)
                def _(ref=ref, val=val):
                    ref[...] = val.astype(ref.dtype)

                @pl.when(jnp.logical_not(first))
                def _(ref=ref, val=val):
                    ref[...] += val.astype(ref.dtype)

    out_shape, out_specs = [], []
    for nr, nc, dtype, kind in outs:
        out_shape.append(jax.ShapeDtypeStruct((nr, nc), dtype))
        if kind == "row":
            out_specs.append(pl.BlockSpec((tile, nc), lambda j, i: (i, 0)))
        elif kind == "rowc":
            out_specs.append(pl.BlockSpec((tile, nc // ncol), lambda j, i: (i, j)))
        elif kind == "acc":
            out_specs.append(pl.BlockSpec((nr, nc), lambda j, i: (0, 0)))
        else:
            out_specs.append(pl.BlockSpec((nr, nc // ncol), lambda j, i: (0, j)))
    return pl.pallas_call(
        body, name=name, grid=(ncol, rows // tile), in_specs=[_in_spec(a, k, tile) for a, k in ins],
        out_specs=out_specs, out_shape=out_shape, compiler_params=_params(2))(*[a for a, _ in ins])


def rowvjp(name, fn, ins, cts, grads, *, rows, tile=None, ncol=1, adds=(), dup16=()):
    n_in = len(ins)
    ct_sizes = [len(c) for c in cts]
    flat_cts = [m for c in cts for m in c]
    n_ct = len(flat_cts)

    def wrapped(*vals):
        xs = list(vals[:n_in])
        gs = vals[n_in:n_in + n_ct]
        extra = vals[n_in + n_ct:]

        def f(*dvars):
            full = list(xs)
            for k, v in zip(grads, dvars):
                full[k] = v
            return fn(*full)

        outs, pull = jax.vjp(f, *[xs[k] for k in grads])
        cot, p = [], 0
        for o, size in zip(outs, ct_sizes):
            g = gs[p].astype(f32)
            for q in range(1, size):
                g = g + gs[p + q].astype(f32)
            cot.append(g.astype(o.dtype))
            p += size
        gv = list(pull(tuple(cot)))
        for (pos, _, _), e in zip(adds, extra):
            gv[pos] = gv[pos] + e.astype(gv[pos].dtype)
        return tuple(gv) + tuple(gv[pos] for pos in dup16)

    outs = []
    for k in grads:
        a, kind = ins[k]
        if kind == "row":
            outs.append((rows, a.shape[1] * ncol, f32, "rowc") if ncol > 1 else (rows, a.shape[1], f32, "row"))
        elif kind == "const":
            outs.append((a.shape[0], a.shape[1], f32, "acc"))
        elif kind[0] == "rowc":
            outs.append((rows, kind[1] * ncol, f32, "rowc"))
        else:
            outs.append((a.shape[0], kind[1] * ncol, f32, "accc"))
    for pos in dup16:
        nr, nc, _, kind = outs[pos]
        outs.append((nr, nc, bf16, kind))
    all_ins = list(ins) + flat_cts + [(a, kind) for _, a, kind in adds]
    return rowcall(name, wrapped, all_ins, outs, rows=rows, tile=tile, ncol=ncol)


def fn_rms(x, w):
    return (rms(x, w),)


def make_fn_gconv(n_norm_blocks):
    def fn(p, cw):
        c = cw[3:4] * p
        for jj in range(3):
            c = c + cw[jj:jj + 1] * tshift(p, 3 - jj)
        s = c * jax.nn.sigmoid(c)
        nrm = s * lax.rsqrt(jnp.sum(s * s, axis=-1, keepdims=True) + 1e-6)
        return (jnp.where(pl.program_id(0) < n_norm_blocks, nrm, s),)
    return fn


def fn_ggate(ps0, alog, dtb, e_g, e_b):
    g = -jnp.exp(alog) * softplus(ps0 + dtb)
    beta = jax.nn.sigmoid(ps0)
    return hdot(g, e_g), hdot(beta, e_b)


def fn_gpost(o, z, nw):
    return (rms(o, nw) * (z * jax.nn.sigmoid(z)),)


def fn_lerp(p, mu):
    return (p + (tshift(p, 1) - p) * mu,)


def fn_rprep(pk, psl, w0, a0, k_k, k_a, w2p, a2p, g2):
    g1, g2in = psl[:, :LANES], psl[:, LANES:]
    log_w = -softplus(-(w0 + bdot(jnp.tanh(g1), w2p))) - 0.5
    lw = -jnp.exp(log_w)
    a = jax.nn.sigmoid(a0 + bdot(g1, a2p))
    gate = bdot(jax.nn.sigmoid(g2in), g2)
    kkr = pk * k_k
    kk = kkr / jnp.maximum(jnp.sqrt(seg2sum(kkr * kkr)), 1e-12)
    kmod = pk * (1.0 + (a - 1.0) * k_a)
    return lw, kmod, kk, a, gate


def fn_rpost(y, r, kmod, v, gate, r_k, ln_w, ln_b):
    inv_n = 1.0 / RW_HEAD
    mean = seg2sum(y) * inv_n
    d = y - mean
    var = seg2sum(d * d) * inv_n
    yn = d * lax.rsqrt(var + RW_GN_EPS) * ln_w + ln_b
    bonus = seg2sum(r * kmod * r_k) * v
    return ((yn + bonus) * gate,)


def fn_xcore(q, k, v):
    s = bdot_nt(q, k) * (LANES ** -0.5)
    p = jax.nn.softmax(s, axis=-1)
    return (bdot(p, v),)


def fn_act(a1):
    r = jnp.maximum(a1, 0.0)
    return ((r * r).astype(bf16),)


def fn_actb(a1, dhid):
    return ((dhid * 2.0 * jnp.maximum(a1, 0.0)).astype(bf16),)


def fn_final(h, tgt, w):
    def loss_fn(h, w):
        return 0.5 * jnp.sum(jnp.mean(jnp.square(rms(h, w) - tgt), axis=-1))

    val, (dh, dw) = jax.value_and_grad(loss_fn, argnums=(0, 1))(h, w)
    return dh, dh.astype(bf16), dw, jnp.full((8, LANES), val, f32)


def fn_sumcols(n):
    def fn(x):
        w = x.shape[1] // n
        s = x[:, :w]
        for q in range(1, n):
            s = s + x[:, q * w:(q + 1) * w]
        return (s,)
    return fn


def _tri(c):
    ii = lax.broadcasted_iota(jnp.int32, (c, c), 0)
    jj = lax.broadcasted_iota(jnp.int32, (c, c), 1)
    return ii, jj


def _neumann_inverse(m, steps):
    c = m.shape[0]
    ii, jj = _tri(c)
    eye = (ii == jj).astype(f32)
    t, p = eye + m, m
    for _ in range(steps):
        p = hdot(p, p)
        t = hdot(t, eye + p)
    return t


def gdn_chunk(q, k, v, gb, bb, s):
    c = q.shape[0]
    ii, jj = _tri(c)
    low = ii >= jj
    gcb = hdot(low.astype(f32), gb)
    gl = hdot(jnp.ones((LANES, c), f32), gb)
    gc_col = gcb[:, :c]
    diff = gc_col - gc_col.T
    decay = jnp.where(low, jnp.exp(jnp.where(low, diff, 0.0)), 0.0)
    qs = q * (q.shape[1] ** -0.5)
    kb = k * bb
    a = jnp.where(ii > jj, hdot_nt(kb, k) * decay, 0.0)
    t = _neumann_inverse(-a, 5)
    eg = jnp.exp(gcb)
    u = hdot(t, v * bb)
    w = hdot(t, kb * eg)
    attn = hdot_nt(qs, k) * decay
    kd = k * jnp.exp(gl[:c] - gcb)
    v_new = u - hdot(w, s)
    o = hdot(qs * eg, s) + hdot(attn, v_new)
    s_new = s * jnp.exp(gl) + hdot_tn(kd, v_new)
    return o, s_new


def wkv_chunk(r, lw, k, v, kk, a, s):
    t = r.shape[0]
    ii, jj = _tri(t)
    lo = lax.broadcasted_iota(jnp.int32, r.shape, 1) < RW_HEAD
    cl = hdot((ii >= jj).astype(f32), lw)
    cl_last = hdot(jnp.ones((LANES, t), f32), lw)
    al = -kk * jnp.exp(cl - lw)
    be = (a * kk) * jnp.exp(-cl)
    kt = k * jnp.exp(-cl)
    rt = r * jnp.exp(cl)

    def pair(xa, xb, msk):
        m_lo = hdot_nt(jnp.where(lo, xa, 0.0), xb)
        m_hi = hdot_nt(jnp.where(lo, 0.0, xa), xb)
        return jnp.where(msk, m_lo, 0.0), jnp.where(msk, m_hi, 0.0)

    def sel(x_lo, x_hi):
        return jnp.where(lo, x_lo, x_hi)

    ab = pair(al, be, ii > jj)
    ak = pair(al, kt, ii > jj)
    rb = pair(rt, be, ii >= jj)
    rk = pair(rt, kt, ii >= jj)
    x = hdot_nt(al, s) + sel(hdot(ak[0], v), hdot(ak[1], v))
    u = sel(hdot(_neumann_inverse(ab[0], 4), x), hdot(_neumann_inverse(ab[1], 4), x))
    y = hdot_nt(rt, s) + sel(hdot(rb[0], u) + hdot(rk[0], v), hdot(rb[1], u) + hdot(rk[1], v))
    vi = lax.broadcasted_iota(jnp.int32, s.shape, 0) < RW_HEAD
    ki = lax.broadcasted_iota(jnp.int32, s.shape, 1) < RW_HEAD
    s_new = jnp.where(vi == ki, (s + hdot_tn(u, be) + hdot_tn(v, kt)) * jnp.exp(cl_last), 0.0)
    return y, s_new


def scan_fwd(name, chunk_fn, ins, *, rows, chunk, ncol):
    n = rows // chunk
    n_in = len(ins)

    def body(*refs):
        o_ref, st_ref, s_scr = refs[n_in:]

        @pl.when(pl.program_id(1) == 0)
        def _():
            s_scr[...] = jnp.zeros_like(s_scr)

        s = s_scr[...]
        st_ref[...] = s
        o, s_new = chunk_fn(*[r[...] for r in refs[:n_in]], s)
        o_ref[...] = o
        s_scr[...] = s_new

    def spec(off):
        return pl.BlockSpec((chunk, LANES), lambda h, c: (c, h + off))

    return pl.pallas_call(
        body, name=name, grid=(ncol, n), in_specs=[spec(off) for _, off in ins],
        out_specs=[spec(0), pl.BlockSpec((None, None, LANES, LANES), lambda h, c: (h, c, 0, 0))],
        out_shape=[jax.ShapeDtypeStruct((rows, ncol * LANES), f32),
                   jax.ShapeDtypeStruct((ncol, n, LANES, LANES), f32)],
        scratch_shapes=[pltpu.VMEM((LANES, LANES), f32)], compiler_params=_params(2))(*[a for a, _ in ins])


def scan_bwd(name, chunk_fn, ins, states, d_out, d_off, *, rows, chunk, ncol):
    n = rows // chunk
    n_in = len(ins)

    def body(*refs):
        st_ref, do_ref = refs[n_in:n_in + 2]
        g_refs = refs[n_in + 2:2 * n_in + 2]
        ds_scr = refs[-1]

        @pl.when(pl.program_id(1) == 0)
        def _():
            ds_scr[...] = jnp.zeros_like(ds_scr)

        _, pull = jax.vjp(chunk_fn, *[r[...] for r in refs[:n_in]], st_ref[...])
        gs = pull((do_ref[...], ds_scr[...]))
        for ref, g in zip(g_refs, gs[:n_in]):
            ref[...] = g
        ds_scr[...] = gs[n_in]

    def spec(off):
        return pl.BlockSpec((chunk, LANES), lambda h, c: (n - 1 - c, h + off))

    st_spec = pl.BlockSpec((None, None, LANES, LANES), lambda h, c: (h, n - 1 - c, 0, 0))
    return pl.pallas_call(
        body, name=name, grid=(ncol, n), in_specs=[spec(off) for _, off in ins] + [st_spec, spec(d_off)],
        out_specs=[spec(0)] * n_in, out_shape=[jax.ShapeDtypeStruct((rows, ncol * LANES), f32)] * n_in,
        scratch_shapes=[pltpu.VMEM((LANES, LANES), f32)], compiler_params=_params(2))(*[a for a, _ in ins], states, d_out)


def flip_exchange(name, arrs, flips, n_slots, slot_of, src_of, with_self):
    n = len(arrs)
    nf = len(flips)

    def body(*refs):
        ins, outs = refs[:n], refs[n:2 * n]
        send, recv, lsem = refs[2 * n:]
        me = (lax.axis_index("x"), lax.axis_index("y"), lax.axis_index("c"))
        copies = []
        for k in range(n):
            if with_self:
                cp = pltpu.make_async_copy(src_of(ins[k], me), outs[k].at[slot_of(me)], lsem.at[k])
                cp.start()
                copies.append(cp)
            for j, fl in enumerate(flips):
                peer = tuple(1 - m if f else m for m, f in zip(me, fl))
                cp = pltpu.make_async_remote_copy(
                    src_ref=src_of(ins[k], peer), dst_ref=outs[k].at[slot_of(me)], send_sem=send.at[k, j],
                    recv_sem=recv.at[k, j], device_id=peer, device_id_type=MESH)
                cp.start()
                copies.append(cp)
        for cp in copies:
            cp.wait()

    def out_sds(a):
        blk = src_of(jax.ShapeDtypeStruct(a.shape, a.dtype), None)
        return jax.ShapeDtypeStruct((n_slots,) + tuple(blk), a.dtype)

    any_spec = pl.BlockSpec(memory_space=pl.ANY)
    return pl.pallas_call(
        body, name=name, in_specs=[any_spec] * n, out_specs=[any_spec] * n, out_shape=[out_sds(a) for a in arrs],
        scratch_shapes=[pltpu.SemaphoreType.DMA((n, nf)), pltpu.SemaphoreType.DMA((n, nf)),
                        pltpu.SemaphoreType.DMA((n,))],
        compiler_params=pltpu.CompilerParams(has_side_effects=True))(*arrs)


_CHIP_FLIPS = ((1, 0, 0), (0, 1, 0), (1, 1, 0))
_ALL_FLIPS = ((0, 0, 1), (0, 1, 0), (0, 1, 1), (1, 0, 0), (1, 0, 1), (1, 1, 0), (1, 1, 1))


def _whole(ref, pos):
    return ref.shape if pos is None else ref


def _chip_block(ref, pos):
    return ref.shape[1:] if pos is None else ref.at[2 * pos[0] + pos[1]]


def gather_chips(arrs):
    return flip_exchange("gather_chips", arrs, _CHIP_FLIPS, 4, lambda p: 2 * p[0] + p[1], _whole, True)


def scatter_chips(arrs):
    return flip_exchange("scatter_chips", arrs, _CHIP_FLIPS, 4, lambda p: 2 * p[0] + p[1], _chip_block, True)


def swap_sibling(arrs):
    outs = flip_exchange("swap_sibling", arrs, ((0, 0, 1),), 1, lambda p: 0, _whole, False)
    return [o[0] for o in outs]


def gather_all(arrs):
    return flip_exchange("gather_all", arrs, _ALL_FLIPS, 8, lambda p: 4 * p[0] + 2 * p[1] + p[2], _whole, True)


def _row_tile(nr, nc, n_arrays):
    budget = (20 << 20) // (n_arrays * 2 * 4 * max(nc, LANES))
    t = min(nr, budget) // 16 * 16
    while t > 0 and nr % t:
        t -= 16
    return t if t > 0 else nr


def sum_slots(name, x):
    ns, nr, nc = x.shape
    tile = _row_tile(nr, nc, ns + 1)

    def body(x_ref, o_ref):
        s = x_ref[0].astype(f32)
        for q in range(1, ns):
            s = s + x_ref[q].astype(f32)
        o_ref[...] = s

    return pl.pallas_call(
        body, name=name, grid=(nr // tile,), in_specs=[pl.BlockSpec((ns, tile, nc), lambda i: (0, i, 0))],
        out_specs=pl.BlockSpec((tile, nc), lambda i: (i, 0)), out_shape=jax.ShapeDtypeStruct((nr, nc), f32),
        compiler_params=_params(1))(x)


def adamw(name, w, g_parts, m, v):
    nr, nc = w.shape
    n_g = len(g_parts)
    tile = _row_tile(nr, nc, 7 + n_g)

    def body(*refs):
        w_ref, m_ref, v_ref = refs[:3]
        g = refs[3][...]
        for r in refs[4:3 + n_g]:
            g = g + r[...]
        g_ref, d_ref, nm_ref, nv_ref = refs[3 + n_g:]
        nm = ADAM_B1 * m_ref[...] + (1.0 - ADAM_B1) * g
        nv = ADAM_B2 * v_ref[...] + (1.0 - ADAM_B2) * jnp.square(g)
        m_hat = nm / (1.0 - ADAM_B1 ** ADAM_STEP)
        v_hat = nv / (1.0 - ADAM_B2 ** ADAM_STEP)
        g_ref[...] = g
        d_ref[...] = -ADAM_LR * (m_hat / (jnp.sqrt(v_hat) + ADAM_EPS) + ADAM_WD * w_ref[...])
        nm_ref[...] = nm
        nv_ref[...] = nv

    spec = pl.BlockSpec((tile, nc), lambda i: (i, 0))
    return pl.pallas_call(
        body, name=name, grid=(nr // tile,), in_specs=[spec] * (3 + n_g), out_specs=[spec] * 4,
        out_shape=[jax.ShapeDtypeStruct((nr, nc), f32)] * 4, compiler_params=_params(1))(w, m, v, *g_parts)


def adamw_packed(w, g8, m, v):
    nr, nc = w.shape

    def body(w_ref, g_ref, m_ref, v_ref, go_ref, d_ref, nm_ref, nv_ref):
        g = g_ref[0]
        for q in range(1, 8):
            g = g + g_ref[q]
        nm = ADAM_B1 * m_ref[...] + (1.0 - ADAM_B1) * g
        nv = ADAM_B2 * v_ref[...] + (1.0 - ADAM_B2) * jnp.square(g)
        m_hat = nm / (1.0 - ADAM_B1 ** ADAM_STEP)
        v_hat = nv / (1.0 - ADAM_B2 ** ADAM_STEP)
        go_ref[...] = g
        d_ref[...] = -ADAM_LR * (m_hat / (jnp.sqrt(v_hat) + ADAM_EPS) + ADAM_WD * w_ref[...])
        nm_ref[...] = nm
        nv_ref[...] = nv

    return pl.pallas_call(body, name="adamw_packed", out_shape=[jax.ShapeDtypeStruct((nr, nc), f32)] * 4,
                          compiler_params=pltpu.CompilerParams(vmem_limit_bytes=VMEM_LIMIT))(w, g8, m, v)


def _pack(vectors):
    rows = []
    for a in vectors:
        flat = a.reshape(-1).astype(f32)
        pad = (-flat.shape[0]) % LANES
        rows.append(jnp.pad(flat, (0, pad)).reshape(-1, LANES))
    packed = jnp.concatenate(rows, axis=0)
    return jnp.pad(packed, ((0, (-packed.shape[0]) % 8), (0, 0)))


def _unpack(packed, like):
    out, r = [], 0
    for a in like:
        n = a.size
        nr = -(-n // LANES)
        out.append(packed[r:r + nr].reshape(-1)[:n].reshape(a.shape))
        r += nr
    return out


def kernel(x, mem, mix_norm_w, w_in, dn_conv_w, dn_a_log, dn_dt_bias, dn_norm_w, rw_mu, rw_w0, rw_w2, rw_a0, rw_a2, rw_g2, rw_k_k, rw_k_a, rw_r_k, rw_ln_w, rw_ln_b, w_out, xa_norm_w, mem_norm_w, xa_wq, xa_wk, xa_wv, xa_wo, ffn_norm_w, ffn_w1, ffn_w2, final_norm_w, loss_target, m_mix_norm_w, m_w_in, m_dn_conv_w, m_dn_a_log, m_dn_dt_bias, m_dn_norm_w, m_rw_mu, m_rw_w0, m_rw_w2, m_rw_a0, m_rw_a2, m_rw_g2, m_rw_k_k, m_rw_k_a, m_rw_r_k, m_rw_ln_w, m_rw_ln_b, m_w_out, m_xa_norm_w, m_mem_norm_w, m_xa_wq, m_xa_wk, m_xa_wv, m_xa_wo, m_ffn_norm_w, m_ffn_w1, m_ffn_w2, m_final_norm_w, v_mix_norm_w, v_w_in, v_dn_conv_w, v_dn_a_log, v_dn_dt_bias, v_dn_norm_w, v_rw_mu, v_rw_w0, v_rw_w2, v_rw_a0, v_rw_a2, v_rw_g2, v_rw_k_k, v_rw_k_a, v_rw_r_k, v_rw_ln_w, v_rw_ln_b, v_w_out, v_xa_norm_w, v_mem_norm_w, v_xa_wq, v_xa_wk, v_xa_wv, v_xa_wo, v_ffn_norm_w, v_ffn_w1, v_ffn_w2, v_final_norm_w):
    weights = dict(mix_norm_w=mix_norm_w, w_in=w_in, dn_conv_w=dn_conv_w, dn_a_log=dn_a_log, dn_dt_bias=dn_dt_bias, dn_norm_w=dn_norm_w, rw_mu=rw_mu, rw_w0=rw_w0, rw_w2=rw_w2, rw_a0=rw_a0, rw_a2=rw_a2, rw_g2=rw_g2, rw_k_k=rw_k_k, rw_k_a=rw_k_a, rw_r_k=rw_r_k, rw_ln_w=rw_ln_w, rw_ln_b=rw_ln_b, w_out=w_out, xa_norm_w=xa_norm_w, mem_norm_w=mem_norm_w, xa_wq=xa_wq, xa_wk=xa_wk, xa_wv=xa_wv, xa_wo=xa_wo, ffn_norm_w=ffn_norm_w, ffn_w1=ffn_w1, ffn_w2=ffn_w2, final_norm_w=final_norm_w)
    mom_m = dict(mix_norm_w=m_mix_norm_w, w_in=m_w_in, dn_conv_w=m_dn_conv_w, dn_a_log=m_dn_a_log, dn_dt_bias=m_dn_dt_bias, dn_norm_w=m_dn_norm_w, rw_mu=m_rw_mu, rw_w0=m_rw_w0, rw_w2=m_rw_w2, rw_a0=m_rw_a0, rw_a2=m_rw_a2, rw_g2=m_rw_g2, rw_k_k=m_rw_k_k, rw_k_a=m_rw_k_a, rw_r_k=m_rw_r_k, rw_ln_w=m_rw_ln_w, rw_ln_b=m_rw_ln_b, w_out=m_w_out, xa_norm_w=m_xa_norm_w, mem_norm_w=m_mem_norm_w, xa_wq=m_xa_wq, xa_wk=m_xa_wk, xa_wv=m_xa_wv, xa_wo=m_xa_wo, ffn_norm_w=m_ffn_norm_w, ffn_w1=m_ffn_w1, ffn_w2=m_ffn_w2, final_norm_w=m_final_norm_w)
    mom_v = dict(mix_norm_w=v_mix_norm_w, w_in=v_w_in, dn_conv_w=v_dn_conv_w, dn_a_log=v_dn_a_log, dn_dt_bias=v_dn_dt_bias, dn_norm_w=v_dn_norm_w, rw_mu=v_rw_mu, rw_w0=v_rw_w0, rw_w2=v_rw_w2, rw_a0=v_rw_a0, rw_a2=v_rw_a2, rw_g2=v_rw_g2, rw_k_k=v_rw_k_k, rw_k_a=v_rw_k_a, rw_r_k=v_rw_r_k, rw_ln_w=v_rw_ln_w, rw_ln_b=v_rw_ln_b, w_out=v_w_out, xa_norm_w=v_xa_norm_w, mem_norm_w=v_mem_norm_w, xa_wq=v_xa_wq, xa_wk=v_xa_wk, xa_wv=v_xa_wv, xa_wo=v_xa_wo, ffn_norm_w=v_ffn_norm_w, ffn_w1=v_ffn_w1, ffn_w2=v_ffn_w2, final_norm_w=v_final_norm_w)
    names = list(weights)

    seq, d = x.shape[1], x.shape[2]
    dnw = d // 2
    rww = d - dnw
    nh, nb = dnw // LANES, rww // LANES
    n_mem = mem.shape[1]
    lw_dim, la_dim, lg_dim = rw_w2.shape[1], rw_a2.shape[1], rw_g2.shape[1]
    assert lw_dim + la_dim == LANES and lg_dim == LANES and dnw % LANES == 0 and rww % LANES == 0
    xs, mems, tgt = x[0], mem[0], loss_target[0]

    col_sharded = ("w_in", "xa_wo", "ffn_w1", "dn_conv_w", "rw_w2", "rw_a2", "rw_g2")
    row_sharded = ("w_out", "xa_wq", "xa_wk", "xa_wv", "ffn_w2")
    f32_payload = ("dn_conv_w", "rw_w2", "rw_a2", "rw_g2")
    sharded = col_sharded + row_sharded
    payload = [weights[n][0].astype(f32 if n in f32_payload else bf16) for n in sharded]
    gathered = dict(zip(sharded, gather_chips(payload)))

    def full(n):
        g = gathered[n]
        if n in col_sharded:
            return g.transpose(1, 0, 2).reshape(g.shape[1], 4 * g.shape[2])
        return g.reshape(4 * g.shape[1], g.shape[2])

    w_in_f = full("w_in")
    c_dn = 4 * dnw
    c_rw0 = c_dn + 2 * nh
    c_rw = 3 * rww
    w_main = jnp.concatenate([w_in_f[:, :c_dn], w_in_f[:, c_rw0:c_rw0 + c_rw]], axis=1)
    w_small = jnp.concatenate([w_in_f[:, c_dn:c_rw0], jnp.zeros((d, LANES - 2 * nh), bf16),
                               w_in_f[:, c_rw0 + c_rw:]], axis=1)
    conv_w = full("dn_conv_w")
    w2p = jnp.concatenate([full("rw_w2"), jnp.zeros((la_dim, rww), f32)], axis=0)
    a2p = jnp.concatenate([jnp.zeros((lw_dim, rww), f32), full("rw_a2")], axis=0)
    g2 = full("rw_g2")
    w_out_f, wq_f, wk_f, wv_f, wo_f = full("w_out"), full("xa_wq"), full("xa_wk"), full("xa_wv"), full("xa_wo")
    w1_f, w2_f = full("ffn_w1"), full("ffn_w2")
    xaw = wq_f.shape[1]
    nxh = xaw // LANES
    ffn = w1_f.shape[1]

    def lane_row(vec):
        return jnp.pad(vec.reshape(1, -1), ((0, 0), (0, LANES - vec.size)))

    alog_row, dtb_row = lane_row(dn_a_log), lane_row(dn_dt_bias)
    head_of_col = jnp.arange(dnw)[None, :] // LANES
    e_g = (jnp.arange(LANES)[:, None] == head_of_col).astype(f32)
    e_b = (jnp.arange(LANES)[:, None] == head_of_col + nh).astype(f32)
    mu_main, mu_small = rw_mu[:, :c_rw], rw_mu[:, c_rw:]
    r_k_row = rw_r_k.reshape(1, rww)
    fnw = final_norm_w.reshape(1, d)
    qb, kb_, vb, zb = 0, nh, 2 * nh, 3 * nh
    rb0 = 4 * nh
    cc = lambda w_, o_: ("constc", w_, o_)
    rc = lambda o_: ("rowc", LANES, o_)

    (u16,) = rowcall("norm_mix", fn_rms, [(xs, "row"), (mix_norm_w, "const")], [(seq, d, bf16, "row")], rows=seq)
    p_main = mm("in_main", u16, w_main)
    p_small = mm("in_small", u16, w_small)

    fn_gconv = make_fn_gconv(2 * nh)
    (qkv,) = rowcall("gdn_conv", fn_gconv, [(p_main, rc(0)), (conv_w, cc(LANES, 0))],
                     [(seq, 3 * dnw, f32, "rowc")], rows=seq, tile=seq, ncol=3 * nh)
    gate_ins = [(p_small, rc(0)), (alog_row, "const"), (dtb_row, "const"), (e_g, "const"), (e_b, "const")]
    g_b, beta_b = rowcall("gdn_gate", fn_ggate, gate_ins, [(seq, dnw, f32, "row")] * 2, rows=seq)
    gdn_ins = [(qkv, qb), (qkv, kb_), (qkv, vb), (g_b, 0), (beta_b, 0)]
    o_raw, gdn_states = scan_fwd("gdn_scan", gdn_chunk, gdn_ins, rows=seq, chunk=GDN_CHUNK, ncol=nh)
    gpost_ins = [(o_raw, rc(0)), (p_main, rc(zb)), (dn_norm_w, "const")]
    (o_dn,) = rowcall("gdn_post", fn_gpost, gpost_ins, [(seq, dnw, bf16, "rowc")], rows=seq, ncol=nh)

    (prw,) = rowcall("rw_lerp_main", fn_lerp, [(p_main, rc(rb0)), (mu_main, cc(LANES, 0))],
                     [(seq, c_rw, f32, "rowc")], rows=seq, tile=seq, ncol=3 * nb)
    (psl,) = rowcall("rw_lerp_small", fn_lerp, [(p_small, rc(1)), (mu_small, cc(LANES, 0))],
                     [(seq, 2 * LANES, f32, "rowc")], rows=seq, tile=seq, ncol=2)
    rprep_ins = [(prw, rc(nb)), (psl, "row"), (rw_w0, cc(LANES, 0)), (rw_a0, cc(LANES, 0)), (rw_k_k, cc(LANES, 0)),
                 (rw_k_a, cc(LANES, 0)), (w2p, cc(LANES, 0)), (a2p, cc(LANES, 0)), (g2, cc(LANES, 0))]
    lw, kmod, kk, a_rw, gate = rowcall("rw_prep", fn_rprep, rprep_ins, [(seq, rww, f32, "rowc")] * 5,
                                        rows=seq, ncol=nb)
    wkv_ins = [(prw, 0), (lw, 0), (kmod, 0), (prw, 2 * nb), (kk, 0), (a_rw, 0)]
    y_rw, wkv_states = scan_fwd("wkv_scan", wkv_chunk, wkv_ins, rows=seq, chunk=WKV_CHUNK, ncol=nb)
    rpost_ins = [(y_rw, rc(0)), (prw, rc(0)), (kmod, rc(0)), (prw, rc(2 * nb)), (gate, rc(0)),
                 (r_k_row, cc(LANES, 0)), (rw_ln_w, cc(LANES, 0)), (rw_ln_b, cc(LANES, 0))]
    (o_rw,) = rowcall("rw_post", fn_rpost, rpost_ins, [(seq, rww, bf16, "rowc")], rows=seq, ncol=nb)

    o_cat = jnp.concatenate([o_dn, o_rw], axis=1)
    h1 = mm("out_proj", o_cat, w_out_f, add=xs)

    (hn16,) = rowcall("norm_xa", fn_rms, [(h1, "row"), (xa_norm_w, "const")], [(seq, d, bf16, "row")], rows=seq)
    (mn16,) = rowcall("norm_mem", fn_rms, [(mems, "row"), (mem_norm_w, "const")], [(n_mem, d, bf16, "row")],
                      rows=n_mem)
    q_xa = mm("xa_q", hn16, wq_f)
    k_xa = mm("xa_k", mn16, wk_f)
    v_xa = mm("xa_v", mn16, wv_f)
    xcore_ins = [(q_xa, rc(0)), (k_xa, cc(LANES, 0)), (v_xa, cc(LANES, 0))]
    (o_xa,) = rowcall("xa_core", fn_xcore, xcore_ins, [(seq, xaw, bf16, "rowc")], rows=seq, ncol=nxh)
    h2 = mm("xa_o", o_xa, wo_f, add=h1)

    (fn16,) = rowcall("norm_ffn", fn_rms, [(h2, "row"), (ffn_norm_w, "const")], [(seq, d, bf16, "row")], rows=seq)
    a1 = mm("ffn_up", fn16, w1_f)
    nfb = ffn // (4 * LANES)
    (hid16,) = rowcall("ffn_act", fn_act, [(a1, ("rowc", 4 * LANES, 0))], [(seq, ffn, bf16, "rowc")], rows=seq,
                       ncol=nfb)
    h3 = mm("ffn_down", hid16, w2_f, add=h2)

    dh3, dh3_16, d_fnw, loss_rows = rowcall(
        "loss_head", fn_final, [(h3, "row"), (tgt, "row"), (fnw, "const")],
        [(seq, d, f32, "row"), (seq, d, bf16, "row"), (1, d, f32, "acc"), (8, LANES, f32, "acc")], rows=seq)

    dhid = mm("ffn_down_dx", dh3_16, w2_f, tb=True)
    (da1_16,) = rowcall("ffn_act_bwd", fn_actb, [(a1, ("rowc", 4 * LANES, 0)), (dhid, ("rowc", 4 * LANES, 0))],
                        [(seq, ffn, bf16, "rowc")], rows=seq, ncol=nfb)
    g_ffn_w2 = mm("ffn_down_dw", hid16, dh3_16, ta=True, out_dtype=bf16)
    g_ffn_w1 = mm("ffn_up_dw", fn16, da1_16, ta=True, out_dtype=bf16)
    dfn = mm("ffn_up_dx", da1_16, w1_f, tb=True)
    dh2, d_ffn_nw, dh2_16 = rowvjp("norm_ffn_bwd", fn_rms, [(h2, "row"), (ffn_norm_w, "const")], [[(dfn, "row")]],
                                   [0, 1], rows=seq, adds=[(0, dh3, "row")], dup16=[0])

    do_xa = mm("xa_o_dx", dh2_16, wo_f, tb=True)
    g_xa_wo = mm("xa_o_dw", o_xa, dh2_16, ta=True, out_dtype=bf16)
    dq_xa, dk_xa, dv_xa, dq16 = rowvjp("xa_core_bwd", fn_xcore, xcore_ins, [[(do_xa, rc(0))]], [0, 1, 2],
                                       rows=seq, ncol=nxh, dup16=[0])
    g_xa_wq = mm("xa_q_dw", hn16, dq16, ta=True, out_dtype=bf16)
    dhn = mm("xa_q_dx", dq16, wq_f, tb=True)
    dh1, d_xa_nw, dh1_16 = rowvjp("norm_xa_bwd", fn_rms, [(h1, "row"), (xa_norm_w, "const")], [[(dhn, "row")]],
                                  [0, 1], rows=seq, adds=[(0, dh2, "row")], dup16=[0])
    dk16, dv16 = dk_xa.astype(bf16), dv_xa.astype(bf16)
    g_xa_wk = mm("xa_k_dw", mn16, dk16, ta=True, out_dtype=bf16)
    g_xa_wv = mm("xa_v_dw", mn16, dv16, ta=True, out_dtype=bf16)
    dmn = mm("xa_v_dx", dv16, wv_f, tb=True, add=mm("xa_k_dx", dk16, wk_f, tb=True))
    (d_mem_nw,) = rowvjp("norm_mem_bwd", fn_rms, [(mems, "row"), (mem_norm_w, "const")], [[(dmn, "row")]], [1],
                         rows=n_mem)

    g_w_out = mm("out_proj_dw", o_cat, dh1_16, ta=True, out_dtype=bf16)
    do_cat = mm("out_proj_dx", dh1_16, w_out_f, tb=True)

    dy, dr_a, dkmod_a, dv_a, dgate, d_r_k, d_ln_w, d_ln_b = rowvjp(
        "rw_post_bwd", fn_rpost, rpost_ins, [[(do_cat, rc(nh))]], [0, 1, 2, 3, 4, 5, 6, 7], rows=seq, ncol=nb)
    dr_b, dlw, dkmod_b, dv_b, dkk, da_rw = scan_bwd("wkv_scan_bwd", wkv_chunk, wkv_ins, wkv_states, dy, 0,
                                                    rows=seq, chunk=WKV_CHUNK, ncol=nb)
    rprep_cts = [[(dlw, rc(0))], [(dkmod_a, rc(0)), (dkmod_b, rc(0))], [(dkk, rc(0))], [(da_rw, rc(0))],
                 [(dgate, rc(0))]]
    dpk, dpsl_parts, d_w0, d_a0, d_k_k, d_k_a, d_w2p, d_a2p, d_g2 = rowvjp(
        "rw_prep_bwd", fn_rprep, rprep_ins, rprep_cts, [0, 1, 2, 3, 4, 5, 6, 7, 8], rows=seq, ncol=nb)
    (dpsl,) = rowcall("rw_prep_sum", fn_sumcols(nb), [(dpsl_parts, "row")], [(seq, 2 * LANES, f32, "row")], rows=seq)

    def lerp_bwd(tag, p, p_off, mu, mu_off, ct_lists, ncol):
        return rowvjp("rw_lerp_bwd_" + tag, fn_lerp, [(p, rc(p_off)), (mu, cc(LANES, mu_off))], [ct_lists], [0, 1],
                      rows=seq, tile=seq, ncol=ncol, dup16=[0])

    _, dmu_r, dpr16 = lerp_bwd("r", p_main, rb0, mu_main, 0, [(dr_a, rc(0)), (dr_b, rc(0))], nb)
    _, dmu_k, dpk16 = lerp_bwd("k", p_main, rb0 + nb, mu_main, nb, [(dpk, rc(0))], nb)
    _, dmu_v, dpv16 = lerp_bwd("v", p_main, rb0 + 2 * nb, mu_main, 2 * nb, [(dv_a, rc(0)), (dv_b, rc(0))], nb)
    _, dmu_s, dps12_16 = lerp_bwd("small", p_small, 1, mu_small, 0, [(dpsl, rc(0))], 2)

    do_raw, dz, d_dn_nw, dz16 = rowvjp("gdn_post_bwd", fn_gpost, gpost_ins, [[(do_cat, rc(0))]], [0, 1, 2],
                                       rows=seq, ncol=nh, dup16=[1])
    dq_g, dk_g, dv_g, dg_b, dbeta_b = scan_bwd("gdn_scan_bwd", gdn_chunk, gdn_ins, gdn_states, do_raw, 0,
                                               rows=seq, chunk=GDN_CHUNK, ncol=nh)
    dps0, d_alog, d_dtb, dps0_16 = rowvjp("gdn_gate_bwd", fn_ggate, gate_ins, [[(dg_b, "row")], [(dbeta_b, "row")]],
                                          [0, 1, 2], rows=seq, dup16=[0])
    dqkv = jnp.concatenate([dq_g, dk_g, dv_g], axis=1)
    _, d_conv_w, dqkv16 = rowvjp("gdn_conv_bwd", fn_gconv, [(p_main, rc(0)), (conv_w, cc(LANES, 0))],
                                 [[(dqkv, rc(0))]], [0, 1], rows=seq, tile=seq, ncol=3 * nh, dup16=[0])

    dp_main16 = jnp.concatenate([dqkv16, dz16, dpr16, dpk16, dpv16], axis=1)
    dp_small16 = jnp.concatenate([dps0_16, dps12_16], axis=1)
    du = mm("in_small_dx", dp_small16, w_small, tb=True, add=mm("in_main_dx", dp_main16, w_main, tb=True))
    g_w_main = mm("in_main_dw", u16, dp_main16, ta=True, out_dtype=bf16)
    g_w_small = mm("in_small_dw", u16, dp_small16, ta=True, out_dtype=bf16)
    grad_x, d_mix_nw = rowvjp("norm_mix_bwd", fn_rms, [(xs, "row"), (mix_norm_w, "const")], [[(du, "row")]], [0, 1],
                              rows=seq, adds=[(0, dh1, "row")])

    g_w_in = jnp.concatenate([g_w_main[:, :c_dn], g_w_small[:, :2 * nh], g_w_main[:, c_dn:],
                              g_w_small[:, LANES:]], axis=1)
    local = dict(
        w_in=g_w_in, dn_conv_w=d_conv_w, rw_w2=d_w2p[:lw_dim], rw_a2=d_a2p[lw_dim:], rw_g2=d_g2, xa_wo=g_xa_wo,
        ffn_w1=g_ffn_w1, w_out=g_w_out, xa_wq=g_xa_wq, xa_wk=g_xa_wk, xa_wv=g_xa_wv, ffn_w2=g_ffn_w2)

    def by_chip(n):
        g = local[n]
        if n in col_sharded:
            return g.reshape(g.shape[0], 4, g.shape[1] // 4).transpose(1, 0, 2)
        return g.reshape(4, g.shape[0] // 4, g.shape[1])

    received = scatter_chips([by_chip(n) for n in sharded])
    partial = [sum_slots("sum_chips_" + n, r) for n, r in zip(sharded, received)]
    other = swap_sibling(partial)

    small_names = [n for n in names if n not in sharded]
    small_local = dict(
        mix_norm_w=d_mix_nw, dn_a_log=d_alog[:, :nh], dn_dt_bias=d_dtb[:, :nh], dn_norm_w=d_dn_nw,
        rw_mu=jnp.concatenate([dmu_r, dmu_k, dmu_v, dmu_s], axis=1), rw_w0=d_w0, rw_a0=d_a0, rw_k_k=d_k_k,
        rw_k_a=d_k_a, rw_r_k=d_r_k, rw_ln_w=d_ln_w, rw_ln_b=d_ln_b, xa_norm_w=d_xa_nw, mem_norm_w=d_mem_nw,
        ffn_norm_w=d_ffn_nw, final_norm_w=d_fnw)
    loss_vec = jnp.where(jnp.arange(LANES) == 0, loss_rows[0], 0.0)
    (g8,) = gather_all([_pack([small_local[n] for n in small_names] + [loss_vec])])

    out = {}
    for n, p_mine, p_other in zip(sharded, partial, other):
        w2d, m2d, v2d = weights[n][0], mom_m[n][0], mom_v[n][0]
        res = adamw("adamw_" + n, w2d, [p_mine, p_other], m2d, v2d)
        out[n] = [r.reshape(weights[n].shape) for r in res]
    packed_like = [weights[n] for n in small_names] + [loss_vec]
    zero = jnp.zeros((LANES,), f32)
    res = adamw_packed(_pack([weights[n] for n in small_names] + [zero]), g8,
                       _pack([mom_m[n] for n in small_names] + [zero]),
                       _pack([mom_v[n] for n in small_names] + [zero]))
    unpacked = [_unpack(r, packed_like) for r in res]
    for i, n in enumerate(small_names):
        out[n] = [u[i] for u in unpacked]
    loss = unpacked[0][-1][0]

    return (loss, grad_x.reshape(x.shape), *[out[n][0] for n in names], *[out[n][1] for n in names],
            *[out[n][2] for n in names], *[out[n][3] for n in names])
```

```python
import functools

import jax
import jax.numpy as jnp
from jax import lax
from jax.experimental import pallas as pl
from jax.experimental.pallas import tpu as pltpu

f32 = jnp.float32
bf16 = jnp.bfloat16
HI = lax.Precision.HIGHEST
MESH = pl.DeviceIdType.MESH

LANES = 128
VMEM_LIMIT = 56 << 20
TOK_TILE = 256
MM_TILE = 512
MM_TILE_K = 1024
GDN_CHUNK = 64
WKV_CHUNK = 32
SCAN_GROUP = 8
P_BULK = 1
P_INV = 3
RMS_EPS = 1e-6
RW_GN_EPS = 64e-5
RW_HEAD = 64

ADAM_LR, ADAM_B1, ADAM_B2, ADAM_EPS, ADAM_WD, ADAM_STEP = 0.001, 0.9, 0.999, 1e-08, 0.01, 10


def _params(n_grid):
    return pltpu.CompilerParams(dimension_semantics=("arbitrary",) * n_grid, vmem_limit_bytes=VMEM_LIMIT)


_DIMS = {"nn": (((1,), (0,)), ((), ())), "nt": (((1,), (1,)), ((), ())), "tn": (((0,), (0,)), ((), ()))}
_DIMS_BATCHED = {"nn": (((2,), (1,)), ((0,), (0,))), "nt": (((2,), (2,)), ((0,), (0,))),
                 "tn": (((1,), (1,)), ((0,), (0,)))}


def _raw_dot(a, b, mode, passes):
    dims = (_DIMS if a.ndim == 2 else _DIMS_BATCHED)[mode]
    if passes == 6:
        return lax.dot_general(a.astype(f32), b.astype(f32), dims, precision=HI, preferred_element_type=f32)
    ah, bh = a.astype(bf16), b.astype(bf16)
    r = lax.dot_general(ah, bh, dims, preferred_element_type=f32)
    if passes == 3:
        al = (a - ah.astype(f32)).astype(bf16)
        bl = (b - bh.astype(f32)).astype(bf16)
        r = r + lax.dot_general(al, bh, dims, preferred_element_type=f32)
        r = r + lax.dot_general(ah, bl, dims, preferred_element_type=f32)
    return r


@functools.partial(jax.custom_vjp, nondiff_argnums=(2, 3))
def pdot(a, b, mode, passes):
    return _raw_dot(a, b, mode, passes)


def _pdot_bwd(mode, passes, res, g):
    a, b = res
    if mode == "nn":
        da, db = _raw_dot(g, b, "nt", passes), _raw_dot(a, g, "tn", passes)
    elif mode == "nt":
        da, db = _raw_dot(g, b, "nn", passes), _raw_dot(g, a, "tn", passes)
    else:
        da, db = _raw_dot(b, g, "nt", passes), _raw_dot(a, g, "nn", passes)
    return da.astype(a.dtype), db.astype(b.dtype)


pdot.defvjp(lambda a, b, mode, passes: (_raw_dot(a, b, mode, passes), (a, b)), _pdot_bwd)


def hdot(a, b):
    return pdot(a, b, "nn", 6)


def bdot(a, b):
    return pdot(a, b, "nn", 1)


def bdot_nt(a, b):
    return pdot(a, b, "nt", 1)


def _shift_rows(x, k):
    row = lax.broadcasted_iota(jnp.int32, x.shape, 0)
    return jnp.where(row < k, 0.0, pltpu.roll(x, k, axis=0))


def _unshift_rows(g, k):
    n = g.shape[0]
    row = lax.broadcasted_iota(jnp.int32, g.shape, 0)
    return jnp.where(row >= n - k, 0.0, pltpu.roll(g, n - k, axis=0))


@functools.partial(jax.custom_vjp, nondiff_argnums=(1,))
def tshift(x, k):
    return _shift_rows(x, k)


tshift.defvjp(lambda x, k: (_shift_rows(x, k), None), lambda k, _, g: (_unshift_rows(g, k),))


def rms(x, w):
    x = x.astype(f32)
    return x * lax.rsqrt(jnp.mean(x * x, axis=-1, keepdims=True) + RMS_EPS) * w


def softplus(x):
    return jnp.maximum(x, 0.0) + jnp.log(1.0 + jnp.exp(-jnp.abs(x)))


def seg2sum(x):
    lo = lax.broadcasted_iota(jnp.int32, x.shape, 1) < RW_HEAD
    s_lo = jnp.sum(jnp.where(lo, x, 0.0), axis=-1, keepdims=True)
    s_hi = jnp.sum(jnp.where(lo, 0.0, x), axis=-1, keepdims=True)
    return jnp.where(lo, s_lo, s_hi)


def _tile(n, pref):
    if n <= pref:
        return n
    t = pref
    while t >= LANES:
        if n % t == 0:
            return t
        t -= LANES
    return n


def mm(name, a, b, *, ta=False, tb=False, add=None, out_dtype=f32):
    (k, m) = a.shape if ta else a.shape[::-1]
    (n, kb) = b.shape if tb else b.shape[::-1]
    assert k == kb, (name, a.shape, b.shape)
    tm, tn, tk = _tile(m, MM_TILE), _tile(n, MM_TILE), _tile(k, MM_TILE_K)
    nk = k // tk
    dims = (((0,) if ta else (1,), (1,) if tb else (0,)), ((), ()))

    def body(*refs):
        if add is None:
            a_ref, b_ref, o_ref, acc = refs
        else:
            a_ref, b_ref, c_ref, o_ref, acc = refs
        kk = pl.program_id(2)

        @pl.when(kk == 0)
        def _():
            acc[...] = jnp.zeros_like(acc)

        acc[...] += lax.dot_general(a_ref[...].astype(bf16), b_ref[...].astype(bf16), dims,
                                    preferred_element_type=f32)

        @pl.when(kk == nk - 1)
        def _():
            r = acc[...]
            if add is not None:
                r = r + c_ref[...].astype(f32)
            o_ref[...] = r.astype(o_ref.dtype)

    a_spec = pl.BlockSpec((tk, tm), lambda i, j, q: (q, i)) if ta else pl.BlockSpec((tm, tk), lambda i, j, q: (i, q))
    b_spec = pl.BlockSpec((tn, tk), lambda i, j, q: (j, q)) if tb else pl.BlockSpec((tk, tn), lambda i, j, q: (q, j))
    o_spec = pl.BlockSpec((tm, tn), lambda i, j, q: (i, j))
    ins, specs = [a, b], [a_spec, b_spec]
    if add is not None:
        ins.append(add)
        specs.append(o_spec)
    return pl.pallas_call(
        body, name=name, grid=(m // tm, n // tn, nk), in_specs=specs, out_specs=o_spec,
        out_shape=jax.ShapeDtypeStruct((m, n), out_dtype), scratch_shapes=[pltpu.VMEM((tm, tn), f32)],
        compiler_params=_params(3))(*ins)


def _in_spec(a, kind, tile):
    if kind == "row":
        return pl.BlockSpec((tile, a.shape[1]), lambda j, i: (i, 0))
    if kind == "const":
        return pl.BlockSpec(a.shape, lambda j, i: (0, 0))
    tag, cw, off = kind
    if tag == "rowc":
        return pl.BlockSpec((tile, cw), lambda j, i: (i, j + off))
    assert tag == "constc", kind
    return pl.BlockSpec((a.shape[0], cw), lambda j, i: (0, j + off))


def rowcall(name, fn, ins, outs, *, rows, tile=None, ncol=1):
    tile = min(TOK_TILE, rows) if tile is None else tile
    n_in = len(ins)
    kinds = [o[3] for o in outs]

    def body(*refs):
        j, i = pl.program_id(0), pl.program_id(1)
        res = fn(*[r[...] for r in refs[:n_in]])
        for ref, val, kind in zip(refs[n_in:], res, kinds):
            if kind in ("row", "rowc"):
                ref[...] = val.astype(ref.dtype)
            else:
                first = (i == 0) if kind == "accc" else jnp.logical_and(i == 0, j == 0)

                @pl.when(first)
                def _(ref=ref, val=val):
                    ref[...] = val.astype(ref.dtype)

                @pl.when(jnp.logical_not(first))
                def _(ref=ref, val=val):
                    ref[...] += val.astype(ref.dtype)

    out_shape, out_specs = [], []
    for nr, nc, dtype, kind in outs:
        out_shape.append(jax.ShapeDtypeStruct((nr, nc), dtype))
        if kind == "row":
            out_specs.append(pl.BlockSpec((tile, nc), lambda j, i: (i, 0)))
        elif kind == "rowc":
            out_specs.append(pl.BlockSpec((tile, nc // ncol), lambda j, i: (i, j)))
        elif kind == "acc":
            out_specs.append(pl.BlockSpec((nr, nc), lambda j, i: (0, 0)))
        else:
            out_specs.append(pl.BlockSpec((nr, nc // ncol), lambda j, i: (0, j)))
    return pl.pallas_call(
        body, name=name, grid=(ncol, rows // tile), in_specs=[_in_spec(a, k, tile) for a, k in ins],
        out_specs=out_specs, out_shape=out_shape, compiler_params=_params(2))(*[a for a, _ in ins])


def rowvjp(name, fn, ins, cts, grads, *, rows, tile=None, ncol=1, adds=(), dup16=()):
    n_in = len(ins)
    ct_sizes = [len(c) for c in cts]
    flat_cts = [m for c in cts for m in c]
    n_ct = len(flat_cts)

    def wrapped(*vals):
        xs = list(vals[:n_in])
        gs = vals[n_in:n_in + n_ct]
        extra = vals[n_in + n_ct:]

        def f(*dvars):
            full = list(xs)
            for k, v in zip(grads, dvars):
                full[k] = v
            return fn(*full)

        outs, pull = jax.vjp(f, *[xs[k] for k in grads])
        cot, p = [], 0
        for o, size in zip(outs, ct_sizes):
            g = gs[p].astype(f32)
            for q in range(1, size):
                g = g + gs[p + q].astype(f32)
            cot.append(g.astype(o.dtype))
            p += size
        gv = list(pull(tuple(cot)))
        for (pos, _, _), e in zip(adds, extra):
            gv[pos] = gv[pos] + e.astype(gv[pos].dtype)
        return tuple(gv) + tuple(gv[pos] for pos in dup16)

    outs = []
    for k in grads:
        a, kind = ins[k]
        if kind == "row":
            outs.append((rows, a.shape[1] * ncol, f32, "rowc") if ncol > 1 else (rows, a.shape[1], f32, "row"))
        elif kind == "const":
            outs.append((a.shape[0], a.shape[1], f32, "acc"))
        elif kind[0] == "rowc":
            outs.append((rows, kind[1] * ncol, f32, "rowc"))
        else:
            outs.append((a.shape[0], kind[1] * ncol, f32, "accc"))
    for pos in dup16:
        nr, nc, _, kind = outs[pos]
        outs.append((nr, nc, bf16, kind))
    all_ins = list(ins) + flat_cts + [(a, kind) for _, a, kind in adds]
    return rowcall(name, wrapped, all_ins, outs, rows=rows, tile=tile, ncol=ncol)


def fn_rms(x, w):
    return (rms(x, w),)


def make_fn_gconv(n_norm_blocks):
    def fn(p, cw):
        c = cw[3:4] * p
        for jj in range(3):
            c = c + cw[jj:jj + 1] * tshift(p, 3 - jj)
        s = c * jax.nn.sigmoid(c)
        nrm = s * lax.rsqrt(jnp.sum(s * s, axis=-1, keepdims=True) + 1e-6)
        return (jnp.where(pl.program_id(0) < n_norm_blocks, nrm, s),)
    return fn


def fn_ggate(ps0, alog, dtb, e_g, e_b):
    g = -jnp.exp(alog) * softplus(ps0 + dtb)
    beta = jax.nn.sigmoid(ps0)
    return hdot(g, e_g), hdot(beta, e_b)


def fn_gpost(o, z, nw):
    return (rms(o, nw) * (z * jax.nn.sigmoid(z)),)


def fn_lerp(p, mu):
    return (p + (tshift(p, 1) - p) * mu,)


def fn_rprep(pk, psl, w0, a0, k_k, k_a, w2p, a2p, g2):
    g1, g2in = psl[:, :LANES], psl[:, LANES:]
    log_w = -softplus(-(w0 + bdot(jnp.tanh(g1), w2p))) - 0.5
    lw = -jnp.exp(log_w)
    a = jax.nn.sigmoid(a0 + bdot(g1, a2p))
    gate = bdot(jax.nn.sigmoid(g2in), g2)
    kkr = pk * k_k
    kk = kkr / jnp.maximum(jnp.sqrt(seg2sum(kkr * kkr)), 1e-12)
    kmod = pk * (1.0 + (a - 1.0) * k_a)
    return lw, kmod, kk, a, gate


def fn_rpost(y, r, kmod, v, gate, r_k, ln_w, ln_b):
    inv_n = 1.0 / RW_HEAD
    mean = seg2sum(y) * inv_n
    d = y - mean
    var = seg2sum(d * d) * inv_n
    yn = d * lax.rsqrt(var + RW_GN_EPS) * ln_w + ln_b
    bonus = seg2sum(r * kmod * r_k) * v
    return ((yn + bonus) * gate,)


def fn_xcore(q, k, v):
    s = bdot_nt(q, k) * (LANES ** -0.5)
    p = jax.nn.softmax(s, axis=-1)
    return (bdot(p, v),)


def fn_act(a1):
    r = jnp.maximum(a1, 0.0)
    return ((r * r).astype(bf16),)


def fn_actb(a1, dhid):
    return ((dhid * 2.0 * jnp.maximum(a1, 0.0)).astype(bf16),)


def fn_final(h, tgt, w):
    def loss_fn(h, w):
        return 0.5 * jnp.sum(jnp.mean(jnp.square(rms(h, w) - tgt), axis=-1))

    val, (dh, dw) = jax.value_and_grad(loss_fn, argnums=(0, 1))(h, w)
    return dh, dh.astype(bf16), dw, jnp.full((8, LANES), val, f32)


def fn_sumcols(n):
    def fn(x):
        w = x.shape[1] // n
        s = x[:, :w]
        for q in range(1, n):
            s = s + x[:, q * w:(q + 1) * w]
        return (s,)
    return fn


def _tri(c):
    ii = lax.broadcasted_iota(jnp.int32, (c, c), 0)
    jj = lax.broadcasted_iota(jnp.int32, (c, c), 1)
    return ii, jj


def _neumann_raw(m, steps):
    c = m.shape[-1]
    ii, jj = _tri(c)
    eye = (ii == jj).astype(f32)
    t, p = eye + m, m
    for _ in range(steps):
        p = _raw_dot(p, p, "nn", P_INV)
        t = _raw_dot(t, eye + p, "nn", P_INV)
    return t


@functools.partial(jax.custom_vjp, nondiff_argnums=(1,))
def _neumann_inverse(m, steps):
    return _neumann_raw(m, steps)


def _neumann_fwd(m, steps):
    t = _neumann_raw(m, steps)
    return t, t


def _neumann_bwd(steps, t, g):
    return (_raw_dot(_raw_dot(t, g, "tn", P_INV), t, "nt", P_INV),)


_neumann_inverse.defvjp(_neumann_fwd, _neumann_bwd)


def cumsum_rows(x):
    t = x.shape[1]
    ii, jj = _tri(t)
    tri = jnp.broadcast_to((ii >= jj).astype(f32), (x.shape[0], t, t))
    return pdot(tri, x, "nn", P_INV)


def gdn_chunk(q, k, v, gb, bb, s):
    c = q.shape[1]
    ii, jj = _tri(c)
    low = ii >= jj
    gcb = cumsum_rows(gb)
    gl = jnp.sum(gb, axis=1, keepdims=True)
    gc_col = gcb[:, :, :c]
    diff = gc_col - jnp.swapaxes(gc_col, 1, 2)
    decay = jnp.where(low, jnp.exp(jnp.where(low, diff, 0.0)), 0.0)
    qs = q * (q.shape[2] ** -0.5)
    kb = k * bb
    a = jnp.where(ii > jj, pdot(kb, k, "nt", P_BULK) * decay, 0.0)
    t = _neumann_inverse(-a, 5)
    eg = jnp.exp(gcb)
    u = pdot(t, v * bb, "nn", P_BULK)
    w = pdot(t, kb * eg, "nn", P_BULK)
    attn = pdot(qs, k, "nt", P_BULK) * decay
    kd = k * jnp.exp(gl - gcb)
    v_new = u - pdot(w, s, "nn", P_BULK)
    o = pdot(qs * eg, s, "nn", P_BULK) + pdot(attn, v_new, "nn", P_BULK)
    s_new = s * jnp.exp(gl) + pdot(kd, v_new, "tn", P_BULK)
    return o, s_new


def wkv_chunk(r, lw, k, v, kk, a, s):
    t = r.shape[1]
    ii, jj = _tri(t)
    lo = lax.broadcasted_iota(jnp.int32, r.shape, 2) < RW_HEAD
    cl = cumsum_rows(lw)
    cl_last = jnp.sum(lw, axis=1, keepdims=True)
    al = -kk * jnp.exp(cl - lw)
    be = (a * kk) * jnp.exp(-cl)
    kt = k * jnp.exp(-cl)
    rt = r * jnp.exp(cl)

    def dot(xa, xb, mode="nn"):
        return pdot(xa, xb, mode, P_BULK)

    def pair(xa, xb, msk):
        m_lo = dot(jnp.where(lo, xa, 0.0), xb, "nt")
        m_hi = dot(jnp.where(lo, 0.0, xa), xb, "nt")
        return jnp.where(msk, m_lo, 0.0), jnp.where(msk, m_hi, 0.0)

    def sel(x_lo, x_hi):
        return jnp.where(lo, x_lo, x_hi)

    ab = pair(al, be, ii > jj)
    ak = pair(al, kt, ii > jj)
    rb = pair(rt, be, ii >= jj)
    rk = pair(rt, kt, ii >= jj)
    x = dot(al, s, "nt") + sel(dot(ak[0], v), dot(ak[1], v))
    u = sel(dot(_neumann_inverse(ab[0], 4), x), dot(_neumann_inverse(ab[1], 4), x))
    y = dot(rt, s, "nt") + sel(dot(rb[0], u) + dot(rk[0], v), dot(rb[1], u) + dot(rk[1], v))
    vi = lax.broadcasted_iota(jnp.int32, s.shape, 1) < RW_HEAD
    ki = lax.broadcasted_iota(jnp.int32, s.shape, 2) < RW_HEAD
    s_new = jnp.where(vi == ki, (s + dot(u, be, "tn") + dot(v, kt, "tn")) * jnp.exp(cl_last), 0.0)
    return y, s_new


def _scan_group(ncol, offs):
    g = SCAN_GROUP
    while g > 1 and (ncol % g or any(o % g for o in offs)):
        g //= 2
    return g


def scan_fwd(name, chunk_fn, ins, *, rows, chunk, ncol):
    n = rows // chunk
    n_in = len(ins)
    grp = _scan_group(ncol, [off for _, off in ins])

    def body(*refs):
        o_ref, st_ref, s_scr = refs[n_in:]

        @pl.when(pl.program_id(1) == 0)
        def _():
            s_scr[...] = jnp.zeros_like(s_scr)

        cols = [slice(b * LANES, (b + 1) * LANES) for b in range(grp)]
        s = s_scr[...]
        st_ref[...] = s
        o, s_new = chunk_fn(*[jnp.stack([r[:, c] for c in cols]) for r in refs[:n_in]], s)
        for b, c in enumerate(cols):
            o_ref[:, c] = o[b]
        s_scr[...] = s_new

    def spec(off):
        return pl.BlockSpec((chunk, grp * LANES), lambda h, c: (c, h + off // grp))

    return pl.pallas_call(
        body, name=name, grid=(ncol // grp, n), in_specs=[spec(off) for _, off in ins],
        out_specs=[spec(0), pl.BlockSpec((grp, None, LANES, LANES), lambda h, c: (h, c, 0, 0))],
        out_shape=[jax.ShapeDtypeStruct((rows, ncol * LANES), f32),
                   jax.ShapeDtypeStruct((ncol, n, LANES, LANES), f32)],
        scratch_shapes=[pltpu.VMEM((grp, LANES, LANES), f32)], compiler_params=_params(2))(*[a for a, _ in ins])


def scan_bwd(name, chunk_fn, ins, states, d_out, d_off, *, rows, chunk, ncol):
    n = rows // chunk
    n_in = len(ins)
    grp = _scan_group(ncol, [off for _, off in ins] + [d_off])

    def body(*refs):
        st_ref, do_ref = refs[n_in:n_in + 2]
        g_refs = refs[n_in + 2:2 * n_in + 2]
        ds_scr = refs[-1]

        @pl.when(pl.program_id(1) == 0)
        def _():
            ds_scr[...] = jnp.zeros_like(ds_scr)

        cols = [slice(b * LANES, (b + 1) * LANES) for b in range(grp)]

        def batch(ref):
            return jnp.stack([ref[:, c] for c in cols])

        _, pull = jax.vjp(chunk_fn, *[batch(r) for r in refs[:n_in]], st_ref[...])
        gs = pull((batch(do_ref), ds_scr[...]))
        for ref, g in zip(g_refs, gs[:n_in]):
            for b, c in enumerate(cols):
                ref[:, c] = g[b]
        ds_scr[...] = gs[n_in]

    def spec(off):
        return pl.BlockSpec((chunk, grp * LANES), lambda h, c: (n - 1 - c, h + off // grp))

    st_spec = pl.BlockSpec((grp, None, LANES, LANES), lambda h, c: (h, n - 1 - c, 0, 0))
    return pl.pallas_call(
        body, name=name, grid=(ncol // grp, n), in_specs=[spec(off) for _, off in ins] + [st_spec, spec(d_off)],
        out_specs=[spec(0)] * n_in, out_shape=[jax.ShapeDtypeStruct((rows, ncol * LANES), f32)] * n_in,
        scratch_shapes=[pltpu.VMEM((grp, LANES, LANES), f32)],
        compiler_params=_params(2))(*[a for a, _ in ins], states, d_out)


def flip_exchange(name, arrs, flips, n_slots, slot_of, src_of, with_self):
    n = len(arrs)
    nf = len(flips)

    def body(*refs):
        ins, outs = refs[:n], refs[n:2 * n]
        send, recv, lsem = refs[2 * n:]
        me = (lax.axis_index("x"), lax.axis_index("y"), lax.axis_index("c"))
        copies = []
        for k in range(n):
            if with_self:
                cp = pltpu.make_async_copy(src_of(ins[k], me), outs[k].at[slot_of(me)], lsem.at[k])
                cp.start()
                copies.append(cp)
            for j, fl in enumerate(flips):
                peer = tuple(1 - m if f else m for m, f in zip(me, fl))
                cp = pltpu.make_async_remote_copy(
                    src_ref=src_of(ins[k], peer), dst_ref=outs[k].at[slot_of(me)], send_sem=send.at[k, j],
                    recv_sem=recv.at[k, j], device_id=peer, device_id_type=MESH)
                cp.start()
                copies.append(cp)
        for cp in copies:
            cp.wait()

    def out_sds(a):
        blk = src_of(jax.ShapeDtypeStruct(a.shape, a.dtype), None)
        return jax.ShapeDtypeStruct((n_slots,) + tuple(blk), a.dtype)

    any_spec = pl.BlockSpec(memory_space=pl.ANY)
    return pl.pallas_call(
        body, name=name, in_specs=[any_spec] * n, out_specs=[any_spec] * n, out_shape=[out_sds(a) for a in arrs],
        scratch_shapes=[pltpu.SemaphoreType.DMA((n, nf)), pltpu.SemaphoreType.DMA((n, nf)),
                        pltpu.SemaphoreType.DMA((n,))],
        compiler_params=pltpu.CompilerParams(has_side_effects=True))(*arrs)


_CHIP_FLIPS = ((1, 0, 0), (0, 1, 0), (1, 1, 0))
_ALL_FLIPS = ((0, 0, 1), (0, 1, 0), (0, 1, 1), (1, 0, 0), (1, 0, 1), (1, 1, 0), (1, 1, 1))


def _whole(ref, pos):
    return ref.shape if pos is None else ref


def _chip_block(ref, pos):
    return ref.shape[1:] if pos is None else ref.at[2 * pos[0] + pos[1]]


def gather_chips(arrs):
    return flip_exchange("gather_chips", arrs, _CHIP_FLIPS, 4, lambda p: 2 * p[0] + p[1], _whole, True)


def scatter_chips(arrs):
    return flip_exchange("scatter_chips", arrs, _CHIP_FLIPS, 4, lambda p: 2 * p[0] + p[1], _chip_block, True)


def swap_sibling(arrs):
    outs = flip_exchange("swap_sibling", arrs, ((0, 0, 1),), 1, lambda p: 0, _whole, False)
    return [o[0] for o in outs]


def gather_all(arrs):
    return flip_exchange("gather_all", arrs, _ALL_FLIPS, 8, lambda p: 4 * p[0] + 2 * p[1] + p[2], _whole, True)


def _row_tile(nr, nc, n_arrays):
    budget = (20 << 20) // (n_arrays * 2 * 4 * max(nc, LANES))
    t = min(nr, budget) // 16 * 16
    while t > 0 and nr % t:
        t -= 16
    return t if t > 0 else nr


def sum_slots(name, x):
    ns, nr, nc = x.shape
    tile = _row_tile(nr, nc, ns + 1)

    def body(x_ref, o_ref):
        s = x_ref[0].astype(f32)
        for q in range(1, ns):
            s = s + x_ref[q].astype(f32)
        o_ref[...] = s

    return pl.pallas_call(
        body, name=name, grid=(nr // tile,), in_specs=[pl.BlockSpec((ns, tile, nc), lambda i: (0, i, 0))],
        out_specs=pl.BlockSpec((tile, nc), lambda i: (i, 0)), out_shape=jax.ShapeDtypeStruct((nr, nc), f32),
        compiler_params=_params(1))(x)


def adamw(name, w, g_parts, m, v):
    nr, nc = w.shape
    n_g = len(g_parts)
    tile = _row_tile(nr, nc, 7 + n_g)

    def body(*refs):
        w_ref, m_ref, v_ref = refs[:3]
        g = refs[3][...]
        for r in refs[4:3 + n_g]:
            g = g + r[...]
        g_ref, d_ref, nm_ref, nv_ref = refs[3 + n_g:]
        nm = ADAM_B1 * m_ref[...] + (1.0 - ADAM_B1) * g
        nv = ADAM_B2 * v_ref[...] + (1.0 - ADAM_B2) * jnp.square(g)
        m_hat = nm / (1.0 - ADAM_B1 ** ADAM_STEP)
        v_hat = nv / (1.0 - ADAM_B2 ** ADAM_STEP)
        g_ref[...] = g
        d_ref[...] = -ADAM_LR * (m_hat / (jnp.sqrt(v_hat) + ADAM_EPS) + ADAM_WD * w_ref[...])
        nm_ref[...] = nm
        nv_ref[...] = nv

    spec = pl.BlockSpec((tile, nc), lambda i: (i, 0))
    return pl.pallas_call(
        body, name=name, grid=(nr // tile,), in_specs=[spec] * (3 + n_g), out_specs=[spec] * 4,
        out_shape=[jax.ShapeDtypeStruct((nr, nc), f32)] * 4, compiler_params=_params(1))(w, m, v, *g_parts)


def adamw_packed(w, g8, m, v):
    nr, nc = w.shape

    def body(w_ref, g_ref, m_ref, v_ref, go_ref, d_ref, nm_ref, nv_ref):
        g = g_ref[0]
        for q in range(1, 8):
            g = g + g_ref[q]
        nm = ADAM_B1 * m_ref[...] + (1.0 - ADAM_B1) * g
        nv = ADAM_B2 * v_ref[...] + (1.0 - ADAM_B2) * jnp.square(g)
        m_hat = nm / (1.0 - ADAM_B1 ** ADAM_STEP)
        v_hat = nv / (1.0 - ADAM_B2 ** ADAM_STEP)
        go_ref[...] = g
        d_ref[...] = -ADAM_LR * (m_hat / (jnp.sqrt(v_hat) + ADAM_EPS) + ADAM_WD * w_ref[...])
        nm_ref[...] = nm
        nv_ref[...] = nv

    return pl.pallas_call(body, name="adamw_packed", out_shape=[jax.ShapeDtypeStruct((nr, nc), f32)] * 4,
                          compiler_params=pltpu.CompilerParams(vmem_limit_bytes=VMEM_LIMIT))(w, g8, m, v)


def _pack(vectors):
    rows = []
    for a in vectors:
        flat = a.reshape(-1).astype(f32)
        pad = (-flat.shape[0]) % LANES
        rows.append(jnp.pad(flat, (0, pad)).reshape(-1, LANES))
    packed = jnp.concatenate(rows, axis=0)
    return jnp.pad(packed, ((0, (-packed.shape[0]) % 8), (0, 0)))


def _unpack(packed, like):
    out, r = [], 0
    for a in like:
        n = a.size
        nr = -(-n // LANES)
        out.append(packed[r:r + nr].reshape(-1)[:n].reshape(a.shape))
        r += nr
    return out


def kernel(x, mem, mix_norm_w, w_in, dn_conv_w, dn_a_log, dn_dt_bias, dn_norm_w, rw_mu, rw_w0, rw_w2, rw_a0, rw_a2, rw_g2, rw_k_k, rw_k_a, rw_r_k, rw_ln_w, rw_ln_b, w_out, xa_norm_w, mem_norm_w, xa_wq, xa_wk, xa_wv, xa_wo, ffn_norm_w, ffn_w1, ffn_w2, final_norm_w, loss_target, m_mix_norm_w, m_w_in, m_dn_conv_w, m_dn_a_log, m_dn_dt_bias, m_dn_norm_w, m_rw_mu, m_rw_w0, m_rw_w2, m_rw_a0, m_rw_a2, m_rw_g2, m_rw_k_k, m_rw_k_a, m_rw_r_k, m_rw_ln_w, m_rw_ln_b, m_w_out, m_xa_norm_w, m_mem_norm_w, m_xa_wq, m_xa_wk, m_xa_wv, m_xa_wo, m_ffn_norm_w, m_ffn_w1, m_ffn_w2, m_final_norm_w, v_mix_norm_w, v_w_in, v_dn_conv_w, v_dn_a_log, v_dn_dt_bias, v_dn_norm_w, v_rw_mu, v_rw_w0, v_rw_w2, v_rw_a0, v_rw_a2, v_rw_g2, v_rw_k_k, v_rw_k_a, v_rw_r_k, v_rw_ln_w, v_rw_ln_b, v_w_out, v_xa_norm_w, v_mem_norm_w, v_xa_wq, v_xa_wk, v_xa_wv, v_xa_wo, v_ffn_norm_w, v_ffn_w1, v_ffn_w2, v_final_norm_w):
    weights = dict(mix_norm_w=mix_norm_w, w_in=w_in, dn_conv_w=dn_conv_w, dn_a_log=dn_a_log, dn_dt_bias=dn_dt_bias, dn_norm_w=dn_norm_w, rw_mu=rw_mu, rw_w0=rw_w0, rw_w2=rw_w2, rw_a0=rw_a0, rw_a2=rw_a2, rw_g2=rw_g2, rw_k_k=rw_k_k, rw_k_a=rw_k_a, rw_r_k=rw_r_k, rw_ln_w=rw_ln_w, rw_ln_b=rw_ln_b, w_out=w_out, xa_norm_w=xa_norm_w, mem_norm_w=mem_norm_w, xa_wq=xa_wq, xa_wk=xa_wk, xa_wv=xa_wv, xa_wo=xa_wo, ffn_norm_w=ffn_norm_w, ffn_w1=ffn_w1, ffn_w2=ffn_w2, final_norm_w=final_norm_w)
    mom_m = dict(mix_norm_w=m_mix_norm_w, w_in=m_w_in, dn_conv_w=m_dn_conv_w, dn_a_log=m_dn_a_log, dn_dt_bias=m_dn_dt_bias, dn_norm_w=m_dn_norm_w, rw_mu=m_rw_mu, rw_w0=m_rw_w0, rw_w2=m_rw_w2, rw_a0=m_rw_a0, rw_a2=m_rw_a2, rw_g2=m_rw_g2, rw_k_k=m_rw_k_k, rw_k_a=m_rw_k_a, rw_r_k=m_rw_r_k, rw_ln_w=m_rw_ln_w, rw_ln_b=m_rw_ln_b, w_out=m_w_out, xa_norm_w=m_xa_norm_w, mem_norm_w=m_mem_norm_w, xa_wq=m_xa_wq, xa_wk=m_xa_wk, xa_wv=m_xa_wv, xa_wo=m_xa_wo, ffn_norm_w=m_ffn_norm_w, ffn_w1=m_ffn_w1, ffn_w2=m_ffn_w2, final_norm_w=m_final_norm_w)
    mom_v = dict(mix_norm_w=v_mix_norm_w, w_in=v_w_in, dn_conv_w=v_dn_conv_w, dn_a_log=v_dn_a_log, dn_dt_bias=v_dn_dt_bias, dn_norm_w=v_dn_norm_w, rw_mu=v_rw_mu, rw_w0=v_rw_w0, rw_w2=v_rw_w2, rw_a0=v_rw_a0, rw_a2=v_rw_a2, rw_g2=v_rw_g2, rw_k_k=v_rw_k_k, rw_k_a=v_rw_k_a, rw_r_k=v_rw_r_k, rw_ln_w=v_rw_ln_w, rw_ln_b=v_rw_ln_b, w_out=v_w_out, xa_norm_w=v_xa_norm_w, mem_norm_w=v_mem_norm_w, xa_wq=v_xa_wq, xa_wk=v_xa_wk, xa_wv=v_xa_wv, xa_wo=v_xa_wo, ffn_norm_w=v_ffn_norm_w, ffn_w1=v_ffn_w1, ffn_w2=v_ffn_w2, final_norm_w=v_final_norm_w)
    names = list(weights)

    seq, d = x.shape[1], x.shape[2]
    dnw = d // 2
    rww = d - dnw
    nh, nb = dnw // LANES, rww // LANES
    n_mem = mem.shape[1]
    lw_dim, la_dim, lg_dim = rw_w2.shape[1], rw_a2.shape[1], rw_g2.shape[1]
    assert lw_dim + la_dim == LANES and lg_dim == LANES and dnw % LANES == 0 and rww % LANES == 0
    xs, mems, tgt = x[0], mem[0], loss_target[0]

    col_sharded = ("w_in", "xa_wo", "ffn_w1", "dn_conv_w", "rw_w2", "rw_a2", "rw_g2")
    row_sharded = ("w_out", "xa_wq", "xa_wk", "xa_wv", "ffn_w2")
    f32_payload = ("dn_conv_w", "rw_w2", "rw_a2", "rw_g2")
    sharded = col_sharded + row_sharded
    payload = [weights[n][0].astype(f32 if n in f32_payload else bf16) for n in sharded]
    gathered = dict(zip(sharded, gather_chips(payload)))

    def full(n):
        g = gathered[n]
        if n in col_sharded:
            return g.transpose(1, 0, 2).reshape(g.shape[1], 4 * g.shape[2])
        return g.reshape(4 * g.shape[1], g.shape[2])

    w_in_f = full("w_in")
    c_dn = 4 * dnw
    c_rw0 = c_dn + 2 * nh
    c_rw = 3 * rww
    w_main = jnp.concatenate([w_in_f[:, :c_dn], w_in_f[:, c_rw0:c_rw0 + c_rw]], axis=1)
    w_small = jnp.concatenate([w_in_f[:, c_dn:c_rw0], jnp.zeros((d, LANES - 2 * nh), bf16),
                               w_in_f[:, c_rw0 + c_rw:]], axis=1)
    conv_w = full("dn_conv_w")
    w2p = jnp.concatenate([full("rw_w2"), jnp.zeros((la_dim, rww), f32)], axis=0)
    a2p = jnp.concatenate([jnp.zeros((lw_dim, rww), f32), full("rw_a2")], axis=0)
    g2 = full("rw_g2")
    w_out_f, wq_f, wk_f, wv_f, wo_f = full("w_out"), full("xa_wq"), full("xa_wk"), full("xa_wv"), full("xa_wo")
    w1_f, w2_f = full("ffn_w1"), full("ffn_w2")
    xaw = wq_f.shape[1]
    nxh = xaw // LANES
    ffn = w1_f.shape[1]

    def lane_row(vec):
        return jnp.pad(vec.reshape(1, -1), ((0, 0), (0, LANES - vec.size)))

    alog_row, dtb_row = lane_row(dn_a_log), lane_row(dn_dt_bias)
    head_of_col = jnp.arange(dnw)[None, :] // LANES
    e_g = (jnp.arange(LANES)[:, None] == head_of_col).astype(f32)
    e_b = (jnp.arange(LANES)[:, None] == head_of_col + nh).astype(f32)
    mu_main, mu_small = rw_mu[:, :c_rw], rw_mu[:, c_rw:]
    r_k_row = rw_r_k.reshape(1, rww)
    fnw = final_norm_w.reshape(1, d)
    qb, kb_, vb, zb = 0, nh, 2 * nh, 3 * nh
    rb0 = 4 * nh
    cc = lambda w_, o_: ("constc", w_, o_)
    rc = lambda o_: ("rowc", LANES, o_)

    (u16,) = rowcall("norm_mix", fn_rms, [(xs, "row"), (mix_norm_w, "const")], [(seq, d, bf16, "row")], rows=seq)
    p_main = mm("in_main", u16, w_main)
    p_small = mm("in_small", u16, w_small)

    fn_gconv = make_fn_gconv(2 * nh)
    (qkv,) = rowcall("gdn_conv", fn_gconv, [(p_main, rc(0)), (conv_w, cc(LANES, 0))],
                     [(seq, 3 * dnw, f32, "rowc")], rows=seq, tile=seq, ncol=3 * nh)
    gate_ins = [(p_small, rc(0)), (alog_row, "const"), (dtb_row, "const"), (e_g, "const"), (e_b, "const")]
    g_b, beta_b = rowcall("gdn_gate", fn_ggate, gate_ins, [(seq, dnw, f32, "row")] * 2, rows=seq)
    gdn_ins = [(qkv, qb), (qkv, kb_), (qkv, vb), (g_b, 0), (beta_b, 0)]
    o_raw, gdn_states = scan_fwd("gdn_scan", gdn_chunk, gdn_ins, rows=seq, chunk=GDN_CHUNK, ncol=nh)
    gpost_ins = [(o_raw, rc(0)), (p_main, rc(zb)), (dn_norm_w, "const")]
    (o_dn,) = rowcall("gdn_post", fn_gpost, gpost_ins, [(seq, dnw, bf16, "rowc")], rows=seq, ncol=nh)

    (prw,) = rowcall("rw_lerp_main", fn_lerp, [(p_main, rc(rb0)), (mu_main, cc(LANES, 0))],
                     [(seq, c_rw, f32, "rowc")], rows=seq, tile=seq, ncol=3 * nb)
    (psl,) = rowcall("rw_lerp_small", fn_lerp, [(p_small, rc(1)), (mu_small, cc(LANES, 0))],
                     [(seq, 2 * LANES, f32, "rowc")], rows=seq, tile=seq, ncol=2)
    rprep_ins = [(prw, rc(nb)), (psl, "row"), (rw_w0, cc(LANES, 0)), (rw_a0, cc(LANES, 0)), (rw_k_k, cc(LANES, 0)),
                 (rw_k_a, cc(LANES, 0)), (w2p, cc(LANES, 0)), (a2p, cc(LANES, 0)), (g2, cc(LANES, 0))]
    lw, kmod, kk, a_rw, gate = rowcall("rw_prep", fn_rprep, rprep_ins, [(seq, rww, f32, "rowc")] * 5,
                                        rows=seq, ncol=nb)
    wkv_ins = [(prw, 0), (lw, 0), (kmod, 0), (prw, 2 * nb), (kk, 0), (a_rw, 0)]
    y_rw, wkv_states = scan_fwd("wkv_scan", wkv_chunk, wkv_ins, rows=seq, chunk=WKV_CHUNK, ncol=nb)
    rpost_ins = [(y_rw, rc(0)), (prw, rc(0)), (kmod, rc(0)), (prw, rc(2 * nb)), (gate, rc(0)),
                 (r_k_row, cc(LANES, 0)), (rw_ln_w, cc(LANES, 0)), (rw_ln_b, cc(LANES, 0))]
    (o_rw,) = rowcall("rw_post", fn_rpost, rpost_ins, [(seq, rww, bf16, "rowc")], rows=seq, ncol=nb)

    o_cat = jnp.concatenate([o_dn, o_rw], axis=1)
    h1 = mm("out_proj", o_cat, w_out_f, add=xs)

    (hn16,) = rowcall("norm_xa", fn_rms, [(h1, "row"), (xa_norm_w, "const")], [(seq, d, bf16, "row")], rows=seq)
    (mn16,) = rowcall("norm_mem", fn_rms, [(mems, "row"), (mem_norm_w, "const")], [(n_mem, d, bf16, "row")],
                      rows=n_mem)
    q_xa = mm("xa_q", hn16, wq_f)
    k_xa = mm("xa_k", mn16, wk_f)
    v_xa = mm("xa_v", mn16, wv_f)
    xcore_ins = [(q_xa, rc(0)), (k_xa, cc(LANES, 0)), (v_xa, cc(LANES, 0))]
    (o_xa,) = rowcall("xa_core", fn_xcore, xcore_ins, [(seq, xaw, bf16, "rowc")], rows=seq, ncol=nxh)
    h2 = mm("xa_o", o_xa, wo_f, add=h1)

    (fn16,) = rowcall("norm_ffn", fn_rms, [(h2, "row"), (ffn_norm_w, "const")], [(seq, d, bf16, "row")], rows=seq)
    a1 = mm("ffn_up", fn16, w1_f)
    nfb = ffn // (4 * LANES)
    (hid16,) = rowcall("ffn_act", fn_act, [(a1, ("rowc", 4 * LANES, 0))], [(seq, ffn, bf16, "rowc")], rows=seq,
                       ncol=nfb)
    h3 = mm("ffn_down", hid16, w2_f, add=h2)

    dh3, dh3_16, d_fnw, loss_rows = rowcall(
        "loss_head", fn_final, [(h3, "row"), (tgt, "row"), (fnw, "const")],
        [(seq, d, f32, "row"), (seq, d, bf16, "row"), (1, d, f32, "acc"), (8, LANES, f32, "acc")], rows=seq)

    dhid = mm("ffn_down_dx", dh3_16, w2_f, tb=True)
    (da1_16,) = rowcall("ffn_act_bwd", fn_actb, [(a1, ("rowc", 4 * LANES, 0)), (dhid, ("rowc", 4 * LANES, 0))],
                        [(seq, ffn, bf16, "rowc")], rows=seq, ncol=nfb)
    g_ffn_w2 = mm("ffn_down_dw", hid16, dh3_16, ta=True, out_dtype=bf16)
    g_ffn_w1 = mm("ffn_up_dw", fn16, da1_16, ta=True, out_dtype=bf16)
    dfn = mm("ffn_up_dx", da1_16, w1_f, tb=True)
    dh2, d_ffn_nw, dh2_16 = rowvjp("norm_ffn_bwd", fn_rms, [(h2, "row"), (ffn_norm_w, "const")], [[(dfn, "row")]],
                                   [0, 1], rows=seq, adds=[(0, dh3, "row")], dup16=[0])

    do_xa = mm("xa_o_dx", dh2_16, wo_f, tb=True)
    g_xa_wo = mm("xa_o_dw", o_xa, dh2_16, ta=True, out_dtype=bf16)
    dq_xa, dk_xa, dv_xa, dq16 = rowvjp("xa_core_bwd", fn_xcore, xcore_ins, [[(do_xa, rc(0))]], [0, 1, 2],
                                       rows=seq, ncol=nxh, dup16=[0])
    g_xa_wq = mm("xa_q_dw", hn16, dq16, ta=True, out_dtype=bf16)
    dhn = mm("xa_q_dx", dq16, wq_f, tb=True)
    dh1, d_xa_nw, dh1_16 = rowvjp("norm_xa_bwd", fn_rms, [(h1, "row"), (xa_norm_w, "const")], [[(dhn, "row")]],
                                  [0, 1], rows=seq, adds=[(0, dh2, "row")], dup16=[0])
    dk16, dv16 = dk_xa.astype(bf16), dv_xa.astype(bf16)
    g_xa_wk = mm("xa_k_dw", mn16, dk16, ta=True, out_dtype=bf16)
    g_xa_wv = mm("xa_v_dw", mn16, dv16, ta=True, out_dtype=bf16)
    dmn = mm("xa_v_dx", dv16, wv_f, tb=True, add=mm("xa_k_dx", dk16, wk_f, tb=True))
    (d_mem_nw,) = rowvjp("norm_mem_bwd", fn_rms, [(mems, "row"), (mem_norm_w, "const")], [[(dmn, "row")]], [1],
                         rows=n_mem)

    g_w_out = mm("out_proj_dw", o_cat, dh1_16, ta=True, out_dtype=bf16)
    do_cat = mm("out_proj_dx", dh1_16, w_out_f, tb=True)

    dy, dr_a, dkmod_a, dv_a, dgate, d_r_k, d_ln_w, d_ln_b = rowvjp(
        "rw_post_bwd", fn_rpost, rpost_ins, [[(do_cat, rc(nh))]], [0, 1, 2, 3, 4, 5, 6, 7], rows=seq, ncol=nb)
    dr_b, dlw, dkmod_b, dv_b, dkk, da_rw = scan_bwd("wkv_scan_bwd", wkv_chunk, wkv_ins, wkv_states, dy, 0,
                                                    rows=seq, chunk=WKV_CHUNK, ncol=nb)
    rprep_cts = [[(dlw, rc(0))], [(dkmod_a, rc(0)), (dkmod_b, rc(0))], [(dkk, rc(0))], [(da_rw, rc(0))],
                 [(dgate, rc(0))]]
    dpk, dpsl_parts, d_w0, d_a0, d_k_k, d_k_a, d_w2p, d_a2p, d_g2 = rowvjp(
        "rw_prep_bwd", fn_rprep, rprep_ins, rprep_cts, [0, 1, 2, 3, 4, 5, 6, 7, 8], rows=seq, ncol=nb)
    (dpsl,) = rowcall("rw_prep_sum", fn_sumcols(nb), [(dpsl_parts, "row")], [(seq, 2 * LANES, f32, "row")], rows=seq)

    def lerp_bwd(tag, p, p_off, mu, mu_off, ct_lists, ncol):
        return rowvjp("rw_lerp_bwd_" + tag, fn_lerp, [(p, rc(p_off)), (mu, cc(LANES, mu_off))], [ct_lists], [0, 1],
                      rows=seq, tile=seq, ncol=ncol, dup16=[0])

    _, dmu_r, dpr16 = lerp_bwd("r", p_main, rb0, mu_main, 0, [(dr_a, rc(0)), (dr_b, rc(0))], nb)
    _, dmu_k, dpk16 = lerp_bwd("k", p_main, rb0 + nb, mu_main, nb, [(dpk, rc(0))], nb)
    _, dmu_v, dpv16 = lerp_bwd("v", p_main, rb0 + 2 * nb, mu_main, 2 * nb, [(dv_a, rc(0)), (dv_b, rc(0))], nb)
    _, dmu_s, dps12_16 = lerp_bwd("small", p_small, 1, mu_small, 0, [(dpsl, rc(0))], 2)

    do_raw, dz, d_dn_nw, dz16 = rowvjp("gdn_post_bwd", fn_gpost, gpost_ins, [[(do_cat, rc(0))]], [0, 1, 2],
                                       rows=seq, ncol=nh, dup16=[1])
    dq_g, dk_g, dv_g, dg_b, dbeta_b = scan_bwd("gdn_scan_bwd", gdn_chunk, gdn_ins, gdn_states, do_raw, 0,
                                               rows=seq, chunk=GDN_CHUNK, ncol=nh)
    dps0, d_alog, d_dtb, dps0_16 = rowvjp("gdn_gate_bwd", fn_ggate, gate_ins, [[(dg_b, "row")], [(dbeta_b, "row")]],
                                          [0, 1, 2], rows=seq, dup16=[0])
    dqkv = jnp.concatenate([dq_g, dk_g, dv_g], axis=1)
    _, d_conv_w, dqkv16 = rowvjp("gdn_conv_bwd", fn_gconv, [(p_main, rc(0)), (conv_w, cc(LANES, 0))],
                                 [[(dqkv, rc(0))]], [0, 1], rows=seq, tile=seq, ncol=3 * nh, dup16=[0])

    dp_main16 = jnp.concatenate([dqkv16, dz16, dpr16, dpk16, dpv16], axis=1)
    dp_small16 = jnp.concatenate([dps0_16, dps12_16], axis=1)
    du = mm("in_small_dx", dp_small16, w_small, tb=True, add=mm("in_main_dx", dp_main16, w_main, tb=True))
    g_w_main = mm("in_main_dw", u16, dp_main16, ta=True, out_dtype=bf16)
    g_w_small = mm("in_small_dw", u16, dp_small16, ta=True, out_dtype=bf16)
    grad_x, d_mix_nw = rowvjp("norm_mix_bwd", fn_rms, [(xs, "row"), (mix_norm_w, "const")], [[(du, "row")]], [0, 1],
                              rows=seq, adds=[(0, dh1, "row")])

    g_w_in = jnp.concatenate([g_w_main[:, :c_dn], g_w_small[:, :2 * nh], g_w_main[:, c_dn:],
                              g_w_small[:, LANES:]], axis=1)
    local = dict(
        w_in=g_w_in, dn_conv_w=d_conv_w, rw_w2=d_w2p[:lw_dim], rw_a2=d_a2p[lw_dim:], rw_g2=d_g2, xa_wo=g_xa_wo,
        ffn_w1=g_ffn_w1, w_out=g_w_out, xa_wq=g_xa_wq, xa_wk=g_xa_wk, xa_wv=g_xa_wv, ffn_w2=g_ffn_w2)

    def by_chip(n):
        g = local[n]
        if n in col_sharded:
            return g.reshape(g.shape[0], 4, g.shape[1] // 4).transpose(1, 0, 2)
        return g.reshape(4, g.shape[0] // 4, g.shape[1])

    received = scatter_chips([by_chip(n) for n in sharded])
    partial = [sum_slots("sum_chips_" + n, r) for n, r in zip(sharded, received)]
    other = swap_sibling(partial)

    small_names = [n for n in names if n not in sharded]
    small_local = dict(
        mix_norm_w=d_mix_nw, dn_a_log=d_alog[:, :nh], dn_dt_bias=d_dtb[:, :nh], dn_norm_w=d_dn_nw,
        rw_mu=jnp.concatenate([dmu_r, dmu_k, dmu_v, dmu_s], axis=1), rw_w0=d_w0, rw_a0=d_a0, rw_k_k=d_k_k,
        rw_k_a=d_k_a, rw_r_k=d_r_k, rw_ln_w=d_ln_w, rw_ln_b=d_ln_b, xa_norm_w=d_xa_nw, mem_norm_w=d_mem_nw,
        ffn_norm_w=d_ffn_nw, final_norm_w=d_fnw)
    loss_vec = jnp.where(jnp.arange(LANES) == 0, loss_rows[0], 0.0)
    (g8,) = gather_all([_pack([small_local[n] for n in small_names] + [loss_vec])])

    out = {}
    for n, p_mine, p_other in zip(sharded, partial, other):
        w2d, m2d, v2d = weights[n][0], mom_m[n][0], mom_v[n][0]
        res = adamw("adamw_" + n, w2d, [p_mine, p_other], m2d, v2d)
        out[n] = [r.reshape(weights[n].shape) for r in res]
    packed_like = [weights[n] for n in small_names] + [loss_vec]
    zero = jnp.zeros((LANES,), f32)
    res = adamw_packed(_pack([weights[n] for n in small_names] + [zero]), g8,
                       _pack([mom_m[n] for n in small_names] + [zero]),
                       _pack([mom_v[n] for n in small_names] + [zero]))
    unpacked = [_unpack(r, packed_like) for r in res]
    for i, n in enumerate(small_names):
        out[n] = [u[i] for u in unpacked]
    loss = unpacked[0][-1][0]

    return (loss, grad_x.reshape(x.shape), *[out[n][0] for n in names], *[out[n][1] for n in names],
            *[out[n][2] for n in names], *[out[n][3] for n in names])
```

```python
import functools

import jax
import jax.numpy as jnp
from jax import lax
from jax.experimental import pallas as pl
from jax.experimental.pallas import tpu as pltpu

f32 = jnp.float32
bf16 = jnp.bfloat16
HI = lax.Precision.HIGHEST
MESH = pl.DeviceIdType.MESH

LANES = 128
VMEM_LIMIT = 56 << 20
TOK_TILE = 256
MM_TILE = 512
MM_TILE_K = 1024
GDN_CHUNK = 64
WKV_CHUNK = 32
SCAN_GROUP = 8
P_BULK = 1
P_INV = 3
RMS_EPS = 1e-6
RW_GN_EPS = 64e-5
RW_HEAD = 64

ADAM_LR, ADAM_B1, ADAM_B2, ADAM_EPS, ADAM_WD, ADAM_STEP = 0.001, 0.9, 0.999, 1e-08, 0.01, 10


def _params(n_grid):
    return pltpu.CompilerParams(dimension_semantics=("arbitrary",) * n_grid, vmem_limit_bytes=VMEM_LIMIT)


_DIMS = {"nn": (((1,), (0,)), ((), ())), "nt": (((1,), (1,)), ((), ())), "tn": (((0,), (0,)), ((), ()))}
_DIMS_BATCHED = {"nn": (((2,), (1,)), ((0,), (0,))), "nt": (((2,), (2,)), ((0,), (0,))),
                 "tn": (((1,), (1,)), ((0,), (0,)))}


def _raw_dot(a, b, mode, passes):
    dims = (_DIMS if a.ndim == 2 else _DIMS_BATCHED)[mode]
    if passes == 6:
        return lax.dot_general(a.astype(f32), b.astype(f32), dims, precision=HI, preferred_element_type=f32)
    ah, bh = a.astype(bf16), b.astype(bf16)
    r = lax.dot_general(ah, bh, dims, preferred_element_type=f32)
    if passes == 3:
        al = (a - ah.astype(f32)).astype(bf16)
        bl = (b - bh.astype(f32)).astype(bf16)
        r = r + lax.dot_general(al, bh, dims, preferred_element_type=f32)
        r = r + lax.dot_general(ah, bl, dims, preferred_element_type=f32)
    return r


@functools.partial(jax.custom_vjp, nondiff_argnums=(2, 3))
def pdot(a, b, mode, passes):
    return _raw_dot(a, b, mode, passes)


def _pdot_bwd(mode, passes, res, g):
    a, b = res
    if mode == "nn":
        da, db = _raw_dot(g, b, "nt", passes), _raw_dot(a, g, "tn", passes)
    elif mode == "nt":
        da, db = _raw_dot(g, b, "nn", passes), _raw_dot(g, a, "tn", passes)
    else:
        da, db = _raw_dot(b, g, "nt", passes), _raw_dot(a, g, "nn", passes)
    return da.astype(a.dtype), db.astype(b.dtype)


pdot.defvjp(lambda a, b, mode, passes: (_raw_dot(a, b, mode, passes), (a, b)), _pdot_bwd)


def hdot(a, b):
    return pdot(a, b, "nn", 6)


def bdot(a, b):
    return pdot(a, b, "nn", 1)


def bdot_nt(a, b):
    return pdot(a, b, "nt", 1)


def _shift_rows(x, k):
    row = lax.broadcasted_iota(jnp.int32, x.shape, 0)
    return jnp.where(row < k, 0.0, pltpu.roll(x, k, axis=0))


def _unshift_rows(g, k):
    n = g.shape[0]
    row = lax.broadcasted_iota(jnp.int32, g.shape, 0)
    return jnp.where(row >= n - k, 0.0, pltpu.roll(g, n - k, axis=0))


@functools.partial(jax.custom_vjp, nondiff_argnums=(1,))
def tshift(x, k):
    return _shift_rows(x, k)


tshift.defvjp(lambda x, k: (_shift_rows(x, k), None), lambda k, _, g: (_unshift_rows(g, k),))


def rms(x, w):
    x = x.astype(f32)
    return x * lax.rsqrt(jnp.mean(x * x, axis=-1, keepdims=True) + RMS_EPS) * w


def softplus(x):
    return jnp.maximum(x, 0.0) + jnp.log(1.0 + jnp.exp(-jnp.abs(x)))


def seg2sum(x):
    lo = lax.broadcasted_iota(jnp.int32, x.shape, 1) < RW_HEAD
    s_lo = jnp.sum(jnp.where(lo, x, 0.0), axis=-1, keepdims=True)
    s_hi = jnp.sum(jnp.where(lo, 0.0, x), axis=-1, keepdims=True)
    return jnp.where(lo, s_lo, s_hi)


def _tile(n, pref):
    if n <= pref:
        return n
    t = pref
    while t >= LANES:
        if n % t == 0:
            return t
        t -= LANES
    return n


def mm(name, a, b, *, ta=False, tb=False, add=None, out_dtype=f32):
    (k, m) = a.shape if ta else a.shape[::-1]
    (n, kb) = b.shape if tb else b.shape[::-1]
    assert k == kb, (name, a.shape, b.shape)
    tm, tn, tk = _tile(m, MM_TILE), _tile(n, MM_TILE), _tile(k, MM_TILE_K)
    nk = k // tk
    dims = (((0,) if ta else (1,), (1,) if tb else (0,)), ((), ()))

    def body(*refs):
        if add is None:
            a_ref, b_ref, o_ref, acc = refs
        else:
            a_ref, b_ref, c_ref, o_ref, acc = refs
        kk = pl.program_id(2)

        @pl.when(kk == 0)
        def _():
            acc[...] = jnp.zeros_like(acc)

        acc[...] += lax.dot_general(a_ref[...].astype(bf16), b_ref[...].astype(bf16), dims,
                                    preferred_element_type=f32)

        @pl.when(kk == nk - 1)
        def _():
            r = acc[...]
            if add is not None:
                r = r + c_ref[...].astype(f32)
            o_ref[...] = r.astype(o_ref.dtype)

    a_spec = pl.BlockSpec((tk, tm), lambda i, j, q: (q, i)) if ta else pl.BlockSpec((tm, tk), lambda i, j, q: (i, q))
    b_spec = pl.BlockSpec((tn, tk), lambda i, j, q: (j, q)) if tb else pl.BlockSpec((tk, tn), lambda i, j, q: (q, j))
    o_spec = pl.BlockSpec((tm, tn), lambda i, j, q: (i, j))
    ins, specs = [a, b], [a_spec, b_spec]
    if add is not None:
        ins.append(add)
        specs.append(o_spec)
    return pl.pallas_call(
        body, name=name, grid=(m // tm, n // tn, nk), in_specs=specs, out_specs=o_spec,
        out_shape=jax.ShapeDtypeStruct((m, n), out_dtype), scratch_shapes=[pltpu.VMEM((tm, tn), f32)],
        compiler_params=_params(3))(*ins)


def _in_spec(a, kind, tile):
    if kind == "row":
        return pl.BlockSpec((tile, a.shape[1]), lambda j, i: (i, 0))
    if kind == "const":
        return pl.BlockSpec(a.shape, lambda j, i: (0, 0))
    tag, cw, off = kind
    if tag == "rowc":
        return pl.BlockSpec((tile, cw), lambda j, i: (i, j + off))
    assert tag == "constc", kind
    return pl.BlockSpec((a.shape[0], cw), lambda j, i: (0, j + off))


def rowcall(name, fn, ins, outs, *, rows, tile=None, ncol=1):
    tile = min(TOK_TILE, rows) if tile is None else tile
    n_in = len(ins)
    kinds = [o[3] for o in outs]

    def body(*refs):
        j, i = pl.program_id(0), pl.program_id(1)
        res = fn(*[r[...] for r in refs[:n_in]])
        for ref, val, kind in zip(refs[n_in:], res, kinds):
            if kind in ("row", "rowc"):
                ref[...] = val.astype(ref.dtype)
            else:
                first = (i == 0) if kind == "accc" else jnp.logical_and(i == 0, j == 0)

                @pl.when(first)
                def _(ref=ref, val=val):
                    ref[...] = val.astype(ref.dtype)

                @pl.when(jnp.logical_not(first))
                def _(ref=ref, val=val):
                    ref[...] += val.astype(ref.dtype)

    out_shape, out_specs = [], []
    for nr, nc, dtype, kind in outs:
        out_shape.append(jax.ShapeDtypeStruct((nr, nc), dtype))
        if kind == "row":
            out_specs.append(pl.BlockSpec((tile, nc), lambda j, i: (i, 0)))
        elif kind == "rowc":
            out_specs.append(pl.BlockSpec((tile, nc // ncol), lambda j, i: (i, j)))
        elif kind == "acc":
            out_specs.append(pl.BlockSpec((nr, nc), lambda j, i: (0, 0)))
        else:
            out_specs.append(pl.BlockSpec((nr, nc // ncol), lambda j, i: (0, j)))
    return pl.pallas_call(
        body, name=name, grid=(ncol, rows // tile), in_specs=[_in_spec(a, k, tile) for a, k in ins],
        out_specs=out_specs, out_shape=out_shape, compiler_params=_params(2))(*[a for a, _ in ins])


def rowvjp(name, fn, ins, cts, grads, *, rows, tile=None, ncol=1, adds=(), dup16=()):
    n_in = len(ins)
    ct_sizes = [len(c) for c in cts]
    flat_cts = [m for c in cts for m in c]
    n_ct = len(flat_cts)

    def wrapped(*vals):
        xs = list(vals[:n_in])
        gs = vals[n_in:n_in + n_ct]
        extra = vals[n_in + n_ct:]

        def f(*dvars):
            full = list(xs)
            for k, v in zip(grads, dvars):
                full[k] = v
            return fn(*full)

        outs, pull = jax.vjp(f, *[xs[k] for k in grads])
        cot, p = [], 0
        for o, size in zip(outs, ct_sizes):
            g = gs[p].astype(f32)
            for q in range(1, size):
                g = g + gs[p + q].astype(f32)
            cot.append(g.astype(o.dtype))
            p += size
        gv = list(pull(tuple(cot)))
        for (pos, _, _), e in zip(adds, extra):
            gv[pos] = gv[pos] + e.astype(gv[pos].dtype)
        return tuple(gv) + tuple(gv[pos] for pos in dup16)

    outs = []
    for k in grads:
        a, kind = ins[k]
        if kind == "row":
            outs.append((rows, a.shape[1] * ncol, f32, "rowc") if ncol > 1 else (rows, a.shape[1], f32, "row"))
        elif kind == "const":
            outs.append((a.shape[0], a.shape[1], f32, "acc"))
        elif kind[0] == "rowc":
            outs.append((rows, kind[1] * ncol, f32, "rowc"))
        else:
            outs.append((a.shape[0], kind[1] * ncol, f32, "accc"))
    for pos in dup16:
        nr, nc, _, kind = outs[pos]
        outs.append((nr, nc, bf16, kind))
    all_ins = list(ins) + flat_cts + [(a, kind) for _, a, kind in adds]
    return rowcall(name, wrapped, all_ins, outs, rows=rows, tile=tile, ncol=ncol)


def fn_rms(x, w):
    return (rms(x, w),)


def make_fn_gconv(n_norm_blocks):
    def fn(p, cw):
        c = cw[3:4] * p
        for jj in range(3):
            c = c + cw[jj:jj + 1] * tshift(p, 3 - jj)
        s = c * jax.nn.sigmoid(c)
        nrm = s * lax.rsqrt(jnp.sum(s * s, axis=-1, keepdims=True) + 1e-6)
        return (jnp.where(pl.program_id(0) < n_norm_blocks, nrm, s),)
    return fn


def fn_ggate(ps0, alog, dtb, e_g, e_b):
    g = -jnp.exp(alog) * softplus(ps0 + dtb)
    beta = jax.nn.sigmoid(ps0)
    return hdot(g, e_g), hdot(beta, e_b)


def fn_gpost(o, z, nw):
    return (rms(o, nw) * (z * jax.nn.sigmoid(z)),)


def fn_lerp(p, mu):
    return (p + (tshift(p, 1) - p) * mu,)


def fn_rprep(pk, psl, w0, a0, k_k, k_a, w2p, a2p, g2):
    g1, g2in = psl[:, :LANES], psl[:, LANES:]
    log_w = -softplus(-(w0 + bdot(jnp.tanh(g1), w2p))) - 0.5
    lw = -jnp.exp(log_w)
    a = jax.nn.sigmoid(a0 + bdot(g1, a2p))
    gate = bdot(jax.nn.sigmoid(g2in), g2)
    kkr = pk * k_k
    kk = kkr / jnp.maximum(jnp.sqrt(seg2sum(kkr * kkr)), 1e-12)
    kmod = pk * (1.0 + (a - 1.0) * k_a)
    return lw, kmod, kk, a, gate


def fn_rpost(y, r, kmod, v, gate, r_k, ln_w, ln_b):
    inv_n = 1.0 / RW_HEAD
    mean = seg2sum(y) * inv_n
    d = y - mean
    var = seg2sum(d * d) * inv_n
    yn = d * lax.rsqrt(var + RW_GN_EPS) * ln_w + ln_b
    bonus = seg2sum(r * kmod * r_k) * v
    return ((yn + bonus) * gate,)


def fn_xcore(q, k, v):
    s = bdot_nt(q, k) * (LANES ** -0.5)
    p = jax.nn.softmax(s, axis=-1)
    return (bdot(p, v),)


def fn_act(a1):
    r = jnp.maximum(a1, 0.0)
    return ((r * r).astype(bf16),)


def fn_actb(a1, dhid):
    return ((dhid * 2.0 * jnp.maximum(a1, 0.0)).astype(bf16),)


def fn_final(h, tgt, w):
    def loss_fn(h, w):
        return 0.5 * jnp.sum(jnp.mean(jnp.square(rms(h, w) - tgt), axis=-1))

    val, (dh, dw) = jax.value_and_grad(loss_fn, argnums=(0, 1))(h, w)
    return dh, dh.astype(bf16), dw, jnp.full((8, LANES), val, f32)


def fn_sumcols(n):
    def fn(x):
        w = x.shape[1] // n
        s = x[:, :w]
        for q in range(1, n):
            s = s + x[:, q * w:(q + 1) * w]
        return (s,)
    return fn


def _tri(c):
    ii = lax.broadcasted_iota(jnp.int32, (c, c), 0)
    jj = lax.broadcasted_iota(jnp.int32, (c, c), 1)
    return ii, jj


def _neumann_raw(m, steps):
    c = m.shape[-1]
    ii, jj = _tri(c)
    eye = (ii == jj).astype(f32)
    t, p = eye + m, m
    for _ in range(steps):
        p = _raw_dot(p, p, "nn", P_INV)
        t = _raw_dot(t, eye + p, "nn", P_INV)
    return t


@functools.partial(jax.custom_vjp, nondiff_argnums=(1,))
def _neumann_inverse(m, steps):
    return _neumann_raw(m, steps)


def _neumann_fwd(m, steps):
    t = _neumann_raw(m, steps)
    return t, t


def _neumann_bwd(steps, t, g):
    return (_raw_dot(_raw_dot(t, g, "tn", P_INV), t, "nt", P_INV),)


_neumann_inverse.defvjp(_neumann_fwd, _neumann_bwd)


def cumsum_rows(x):
    t = x.shape[1]
    ii, jj = _tri(t)
    tri = jnp.broadcast_to((ii >= jj).astype(f32), (x.shape[0], t, t))
    return pdot(tri, x, "nn", P_INV)


def gdn_chunk(q, k, v, gb, bb, s):
    c = q.shape[1]
    ii, jj = _tri(c)
    low = ii >= jj
    gcb = cumsum_rows(gb)
    gl = jnp.sum(gb, axis=1, keepdims=True)
    gc_col = gcb[:, :, :c]
    diff = gc_col - jnp.swapaxes(gc_col, 1, 2)
    decay = jnp.where(low, jnp.exp(jnp.where(low, diff, 0.0)), 0.0)
    qs = q * (q.shape[2] ** -0.5)
    kb = k * bb
    a = jnp.where(ii > jj, pdot(kb, k, "nt", P_BULK) * decay, 0.0)
    t = _neumann_inverse(-a, 5)
    eg = jnp.exp(gcb)
    u = pdot(t, v * bb, "nn", P_BULK)
    w = pdot(t, kb * eg, "nn", P_BULK)
    attn = pdot(qs, k, "nt", P_BULK) * decay
    kd = k * jnp.exp(gl - gcb)
    v_new = u - pdot(w, s, "nn", P_BULK)
    o = pdot(qs * eg, s, "nn", P_BULK) + pdot(attn, v_new, "nn", P_BULK)
    s_new = s * jnp.exp(gl) + pdot(kd, v_new, "tn", P_BULK)
    return o, s_new


def wkv_chunk(r, lw, k, v, kk, a, s):
    t = r.shape[1]
    ii, jj = _tri(t)
    lo = lax.broadcasted_iota(jnp.int32, r.shape, 2) < RW_HEAD
    cl = cumsum_rows(lw)
    cl_last = jnp.sum(lw, axis=1, keepdims=True)
    al = -kk * jnp.exp(cl - lw)
    be = (a * kk) * jnp.exp(-cl)
    kt = k * jnp.exp(-cl)
    rt = r * jnp.exp(cl)

    def dot(xa, xb, mode="nn"):
        return pdot(xa, xb, mode, P_BULK)

    def pair(xa, xb, msk):
        m_lo = dot(jnp.where(lo, xa, 0.0), xb, "nt")
        m_hi = dot(jnp.where(lo, 0.0, xa), xb, "nt")
        return jnp.where(msk, m_lo, 0.0), jnp.where(msk, m_hi, 0.0)

    def sel(x_lo, x_hi):
        return jnp.where(lo, x_lo, x_hi)

    ab = pair(al, be, ii > jj)
    ak = pair(al, kt, ii > jj)
    rb = pair(rt, be, ii >= jj)
    rk = pair(rt, kt, ii >= jj)
    x = dot(al, s, "nt") + sel(dot(ak[0], v), dot(ak[1], v))
    u = sel(dot(_neumann_inverse(ab[0], 4), x), dot(_neumann_inverse(ab[1], 4), x))
    y = dot(rt, s, "nt") + sel(dot(rb[0], u) + dot(rk[0], v), dot(rb[1], u) + dot(rk[1], v))
    vi = lax.broadcasted_iota(jnp.int32, s.shape, 1) < RW_HEAD
    ki = lax.broadcasted_iota(jnp.int32, s.shape, 2) < RW_HEAD
    s_new = jnp.where(vi == ki, (s + dot(u, be, "tn") + dot(v, kt, "tn")) * jnp.exp(cl_last), 0.0)
    return y, s_new


def _scan_group(ncol, offs):
    g = SCAN_GROUP
    while g > 1 and (ncol % g or any(o % g for o in offs)):
        g //= 2
    return g


def scan_fwd(name, chunk_fn, ins, *, rows, chunk, ncol):
    n = rows // chunk
    n_in = len(ins)
    grp = _scan_group(ncol, [off for _, off in ins])

    def body(*refs):
        o_ref, st_ref, s_scr = refs[n_in:]

        @pl.when(pl.program_id(1) == 0)
        def _():
            s_scr[...] = jnp.zeros_like(s_scr)

        cols = [slice(b * LANES, (b + 1) * LANES) for b in range(grp)]
        s = s_scr[...]
        st_ref[...] = s
        o, s_new = chunk_fn(*[jnp.stack([r[:, c] for c in cols]) for r in refs[:n_in]], s)
        for b, c in enumerate(cols):
            o_ref[:, c] = o[b]
        s_scr[...] = s_new

    def spec(off):
        return pl.BlockSpec((chunk, grp * LANES), lambda h, c: (c, h + off // grp))

    return pl.pallas_call(
        body, name=name, grid=(ncol // grp, n), in_specs=[spec(off) for _, off in ins],
        out_specs=[spec(0), pl.BlockSpec((grp, None, LANES, LANES), lambda h, c: (h, c, 0, 0))],
        out_shape=[jax.ShapeDtypeStruct((rows, ncol * LANES), f32),
                   jax.ShapeDtypeStruct((ncol, n, LANES, LANES), f32)],
        scratch_shapes=[pltpu.VMEM((grp, LANES, LANES), f32)], compiler_params=_params(2))(*[a for a, _ in ins])


def scan_bwd(name, chunk_fn, ins, states, d_out, d_off, *, rows, chunk, ncol):
    n = rows // chunk
    n_in = len(ins)
    grp = _scan_group(ncol, [off for _, off in ins] + [d_off])

    def body(*refs):
        st_ref, do_ref = refs[n_in:n_in + 2]
        g_refs = refs[n_in + 2:2 * n_in + 2]
        ds_scr = refs[-1]

        @pl.when(pl.program_id(1) == 0)
        def _():
            ds_scr[...] = jnp.zeros_like(ds_scr)

        cols = [slice(b * LANES, (b + 1) * LANES) for b in range(grp)]

        def batch(ref):
            return jnp.stack([ref[:, c] for c in cols])

        _, pull = jax.vjp(chunk_fn, *[batch(r) for r in refs[:n_in]], st_ref[...])
        gs = pull((batch(do_ref), ds_scr[...]))
        for ref, g in zip(g_refs, gs[:n_in]):
            for b, c in enumerate(cols):
                ref[:, c] = g[b]
        ds_scr[...] = gs[n_in]

    def spec(off):
        return pl.BlockSpec((chunk, grp * LANES), lambda h, c: (n - 1 - c, h + off // grp))

    st_spec = pl.BlockSpec((grp, None, LANES, LANES), lambda h, c: (h, n - 1 - c, 0, 0))
    return pl.pallas_call(
        body, name=name, grid=(ncol // grp, n), in_specs=[spec(off) for _, off in ins] + [st_spec, spec(d_off)],
        out_specs=[spec(0)] * n_in, out_shape=[jax.ShapeDtypeStruct((rows, ncol * LANES), f32)] * n_in,
        scratch_shapes=[pltpu.VMEM((grp, LANES, LANES), f32)],
        compiler_params=_params(2))(*[a for a, _ in ins], states, d_out)


def flip_exchange(name, arrs, flips, n_slots, slot_of, src_of, with_self):
    n = len(arrs)
    nf = len(flips)

    def body(*refs):
        ins, outs = refs[:n], refs[n:2 * n]
        send, recv, lsem = refs[2 * n:]
        me = (lax.axis_index("x"), lax.axis_index("y"), lax.axis_index("c"))
        copies = []
        for k in range(n):
            if with_self:
                cp = pltpu.make_async_copy(src_of(ins[k], me), outs[k].at[slot_of(me)], lsem.at[k])
                cp.start()
                copies.append(cp)
            for j, fl in enumerate(flips):
                peer = tuple(1 - m if f else m for m, f in zip(me, fl))
                cp = pltpu.make_async_remote_copy(
                    src_ref=src_of(ins[k], peer), dst_ref=outs[k].at[slot_of(me)], send_sem=send.at[k, j],
                    recv_sem=recv.at[k, j], device_id=peer, device_id_type=MESH)
                cp.start()
                copies.append(cp)
        for cp in copies:
            cp.wait()

    def out_sds(a):
        blk = src_of(jax.ShapeDtypeStruct(a.shape, a.dtype), None)
        return jax.ShapeDtypeStruct((n_slots,) + tuple(blk), a.dtype)

    any_spec = pl.BlockSpec(memory_space=pl.ANY)
    return pl.pallas_call(
        body, name=name, in_specs=[any_spec] * n, out_specs=[any_spec] * n, out_shape=[out_sds(a) for a in arrs],
        scratch_shapes=[pltpu.SemaphoreType.DMA((n, nf)), pltpu.SemaphoreType.DMA((n, nf)),
                        pltpu.SemaphoreType.DMA((n,))],
        compiler_params=pltpu.CompilerParams(has_side_effects=True))(*arrs)


_CHIP_FLIPS = ((1, 0, 0), (0, 1, 0), (1, 1, 0))
_ALL_FLIPS = ((0, 0, 1), (0, 1, 0), (0, 1, 1), (1, 0, 0), (1, 0, 1), (1, 1, 0), (1, 1, 1))


def _whole(ref, pos):
    return ref.shape if pos is None else ref


def _chip_block(ref, pos):
    return ref.shape[1:] if pos is None else ref.at[2 * pos[0] + pos[1]]


def _chip_slot(p):
    return 2 * p[0] + p[1]


def gather_chips(name, arrs):
    return flip_exchange(name, arrs, _CHIP_FLIPS, 4, _chip_slot, _whole, True)


def scatter_chips(name, arrs):
    return flip_exchange(name, arrs, _CHIP_FLIPS, 4, _chip_slot, _chip_block, True)


_HBM = pl.BlockSpec(memory_space=pltpu.HBM)
_SEM = pl.BlockSpec(memory_space=pltpu.SEMAPHORE)
_DATAFLOW = pltpu.SideEffectType.DATAFLOW_SIDE_EFFECTING


def _chip_copies(srcs, lands, send, recv, src_of):
    me = (lax.axis_index("x"), lax.axis_index("y"), lax.axis_index("c"))
    copies = []
    for k, (src, land) in enumerate(zip(srcs, lands)):
        for j, fl in enumerate(_CHIP_FLIPS):
            peer = tuple(1 - m if f else m for m, f in zip(me, fl))
            q = k * len(_CHIP_FLIPS) + j
            copies.append(pltpu.make_async_remote_copy(
                src_ref=src_of(src, peer), dst_ref=land.at[_chip_slot(me)], send_sem=send.at[q],
                recv_sem=recv.at[q], device_id=peer, device_id_type=MESH))
    return copies


def chips_start(name, arrs, src_of):
    n = len(arrs)
    me = _chip_slot((lax.axis_index("x"), lax.axis_index("y")))
    lands = []
    for a in arrs:
        blk = tuple(src_of(jax.ShapeDtypeStruct(a.shape, a.dtype), None))
        own = a if src_of is _whole else lax.dynamic_index_in_dim(a, me, 0, keepdims=False)
        land = lax.dynamic_update_index_in_dim(lax.empty((4,) + blk, a.dtype), own, me, 0)
        lands.append(pltpu.with_memory_space_constraint(land, pltpu.HBM))
    srcs = [pltpu.with_memory_space_constraint(a, pltpu.HBM) for a in arrs]

    def body(*refs):
        send, recv = refs[2 * n], refs[2 * n + 1]
        for cp in _chip_copies(refs[:n], refs[n:2 * n], send, recv, src_of):
            cp.start()
        refs[-1][...] = jnp.zeros_like(refs[-1])

    sems = pltpu.SemaphoreType.DMA((n * len(_CHIP_FLIPS),))
    outs = pl.pallas_call(
        body, name=name, in_specs=[_HBM] * (2 * n),
        out_shape=(sems, sems, *[pltpu.HBM(a.shape, a.dtype) for a in srcs + lands],
                   jax.ShapeDtypeStruct((8, LANES), f32)),
        out_specs=(_SEM, _SEM, *[_HBM] * (2 * n), pl.BlockSpec(memory_space=pltpu.VMEM)),
        input_output_aliases={i: 2 + i for i in range(2 * n)},
        compiler_params=pltpu.CompilerParams(has_side_effects=_DATAFLOW))(*srcs, *lands)
    return (outs[0], outs[1], list(outs[2:2 + n]), list(outs[2 + n:2 + 2 * n]), src_of), outs[-1][0, 0]


def chips_wait(name, state, after):
    send, recv, srcs, lands, src_of = state
    n = len(srcs)

    def body(*refs):
        send, recv = refs[2 * n], refs[2 * n + 1]
        for cp in _chip_copies(refs[:n], refs[n:2 * n], send, recv, src_of):
            cp.wait_send()
            cp.wait_recv()

    outs = pl.pallas_call(
        body, name=name, in_specs=[_HBM] * (2 * n) + [_SEM, _SEM, pl.BlockSpec(memory_space=pl.ANY)],
        out_shape=tuple(pltpu.HBM(a.shape, a.dtype) for a in srcs + lands), out_specs=tuple([_HBM] * (2 * n)),
        input_output_aliases={i: i for i in range(2 * n)},
        compiler_params=pltpu.CompilerParams(has_side_effects=_DATAFLOW))(*srcs, *lands, send, recv, after)
    return list(outs[n:])


def swap_sibling(arrs):
    outs = flip_exchange("swap_sibling", arrs, ((0, 0, 1),), 1, lambda p: 0, _whole, False)
    return [o[0] for o in outs]


def gather_all(arrs):
    return flip_exchange("gather_all", arrs, _ALL_FLIPS, 8, lambda p: 4 * p[0] + 2 * p[1] + p[2], _whole, True)


def _row_tile(nr, nc, n_arrays):
    budget = (20 << 20) // (n_arrays * 2 * 4 * max(nc, LANES))
    t = min(nr, budget) // 16 * 16
    while t > 0 and nr % t:
        t -= 16
    return t if t > 0 else nr


def sum_slots(name, x):
    ns, nr, nc = x.shape
    tile = _row_tile(nr, nc, ns + 1)

    def body(x_ref, o_ref):
        s = x_ref[0].astype(f32)
        for q in range(1, ns):
            s = s + x_ref[q].astype(f32)
        o_ref[...] = s

    return pl.pallas_call(
        body, name=name, grid=(nr // tile,), in_specs=[pl.BlockSpec((ns, tile, nc), lambda i: (0, i, 0))],
        out_specs=pl.BlockSpec((tile, nc), lambda i: (i, 0)), out_shape=jax.ShapeDtypeStruct((nr, nc), f32),
        compiler_params=_params(1))(x)


def adamw(name, w, g_parts, m, v):
    nr, nc = w.shape
    n_g = len(g_parts)
    tile = _row_tile(nr, nc, 7 + n_g)

    def body(*refs):
        w_ref, m_ref, v_ref = refs[:3]
        g = refs[3][...]
        for r in refs[4:3 + n_g]:
            g = g + r[...]
        g_ref, d_ref, nm_ref, nv_ref = refs[3 + n_g:]
        nm = ADAM_B1 * m_ref[...] + (1.0 - ADAM_B1) * g
        nv = ADAM_B2 * v_ref[...] + (1.0 - ADAM_B2) * jnp.square(g)
        m_hat = nm / (1.0 - ADAM_B1 ** ADAM_STEP)
        v_hat = nv / (1.0 - ADAM_B2 ** ADAM_STEP)
        g_ref[...] = g
        d_ref[...] = -ADAM_LR * (m_hat / (jnp.sqrt(v_hat) + ADAM_EPS) + ADAM_WD * w_ref[...])
        nm_ref[...] = nm
        nv_ref[...] = nv

    spec = pl.BlockSpec((tile, nc), lambda i: (i, 0))
    return pl.pallas_call(
        body, name=name, grid=(nr // tile,), in_specs=[spec] * (3 + n_g), out_specs=[spec] * 4,
        out_shape=[jax.ShapeDtypeStruct((nr, nc), f32)] * 4, compiler_params=_params(1))(w, m, v, *g_parts)


def adamw_packed(w, g8, m, v):
    nr, nc = w.shape

    def body(w_ref, g_ref, m_ref, v_ref, go_ref, d_ref, nm_ref, nv_ref):
        g = g_ref[0]
        for q in range(1, 8):
            g = g + g_ref[q]
        nm = ADAM_B1 * m_ref[...] + (1.0 - ADAM_B1) * g
        nv = ADAM_B2 * v_ref[...] + (1.0 - ADAM_B2) * jnp.square(g)
        m_hat = nm / (1.0 - ADAM_B1 ** ADAM_STEP)
        v_hat = nv / (1.0 - ADAM_B2 ** ADAM_STEP)
        go_ref[...] = g
        d_ref[...] = -ADAM_LR * (m_hat / (jnp.sqrt(v_hat) + ADAM_EPS) + ADAM_WD * w_ref[...])
        nm_ref[...] = nm
        nv_ref[...] = nv

    return pl.pallas_call(body, name="adamw_packed", out_shape=[jax.ShapeDtypeStruct((nr, nc), f32)] * 4,
                          compiler_params=pltpu.CompilerParams(vmem_limit_bytes=VMEM_LIMIT))(w, g8, m, v)


def _pack(vectors):
    rows = []
    for a in vectors:
        flat = a.reshape(-1).astype(f32)
        pad = (-flat.shape[0]) % LANES
        rows.append(jnp.pad(flat, (0, pad)).reshape(-1, LANES))
    packed = jnp.concatenate(rows, axis=0)
    return jnp.pad(packed, ((0, (-packed.shape[0]) % 8), (0, 0)))


def _unpack(packed, like):
    out, r = [], 0
    for a in like:
        n = a.size
        nr = -(-n // LANES)
        out.append(packed[r:r + nr].reshape(-1)[:n].reshape(a.shape))
        r += nr
    return out


def kernel(x, mem, mix_norm_w, w_in, dn_conv_w, dn_a_log, dn_dt_bias, dn_norm_w, rw_mu, rw_w0, rw_w2, rw_a0, rw_a2, rw_g2, rw_k_k, rw_k_a, rw_r_k, rw_ln_w, rw_ln_b, w_out, xa_norm_w, mem_norm_w, xa_wq, xa_wk, xa_wv, xa_wo, ffn_norm_w, ffn_w1, ffn_w2, final_norm_w, loss_target, m_mix_norm_w, m_w_in, m_dn_conv_w, m_dn_a_log, m_dn_dt_bias, m_dn_norm_w, m_rw_mu, m_rw_w0, m_rw_w2, m_rw_a0, m_rw_a2, m_rw_g2, m_rw_k_k, m_rw_k_a, m_rw_r_k, m_rw_ln_w, m_rw_ln_b, m_w_out, m_xa_norm_w, m_mem_norm_w, m_xa_wq, m_xa_wk, m_xa_wv, m_xa_wo, m_ffn_norm_w, m_ffn_w1, m_ffn_w2, m_final_norm_w, v_mix_norm_w, v_w_in, v_dn_conv_w, v_dn_a_log, v_dn_dt_bias, v_dn_norm_w, v_rw_mu, v_rw_w0, v_rw_w2, v_rw_a0, v_rw_a2, v_rw_g2, v_rw_k_k, v_rw_k_a, v_rw_r_k, v_rw_ln_w, v_rw_ln_b, v_w_out, v_xa_norm_w, v_mem_norm_w, v_xa_wq, v_xa_wk, v_xa_wv, v_xa_wo, v_ffn_norm_w, v_ffn_w1, v_ffn_w2, v_final_norm_w):
    weights = dict(mix_norm_w=mix_norm_w, w_in=w_in, dn_conv_w=dn_conv_w, dn_a_log=dn_a_log, dn_dt_bias=dn_dt_bias, dn_norm_w=dn_norm_w, rw_mu=rw_mu, rw_w0=rw_w0, rw_w2=rw_w2, rw_a0=rw_a0, rw_a2=rw_a2, rw_g2=rw_g2, rw_k_k=rw_k_k, rw_k_a=rw_k_a, rw_r_k=rw_r_k, rw_ln_w=rw_ln_w, rw_ln_b=rw_ln_b, w_out=w_out, xa_norm_w=xa_norm_w, mem_norm_w=mem_norm_w, xa_wq=xa_wq, xa_wk=xa_wk, xa_wv=xa_wv, xa_wo=xa_wo, ffn_norm_w=ffn_norm_w, ffn_w1=ffn_w1, ffn_w2=ffn_w2, final_norm_w=final_norm_w)
    mom_m = dict(mix_norm_w=m_mix_norm_w, w_in=m_w_in, dn_conv_w=m_dn_conv_w, dn_a_log=m_dn_a_log, dn_dt_bias=m_dn_dt_bias, dn_norm_w=m_dn_norm_w, rw_mu=m_rw_mu, rw_w0=m_rw_w0, rw_w2=m_rw_w2, rw_a0=m_rw_a0, rw_a2=m_rw_a2, rw_g2=m_rw_g2, rw_k_k=m_rw_k_k, rw_k_a=m_rw_k_a, rw_r_k=m_rw_r_k, rw_ln_w=m_rw_ln_w, rw_ln_b=m_rw_ln_b, w_out=m_w_out, xa_norm_w=m_xa_norm_w, mem_norm_w=m_mem_norm_w, xa_wq=m_xa_wq, xa_wk=m_xa_wk, xa_wv=m_xa_wv, xa_wo=m_xa_wo, ffn_norm_w=m_ffn_norm_w, ffn_w1=m_ffn_w1, ffn_w2=m_ffn_w2, final_norm_w=m_final_norm_w)
    mom_v = dict(mix_norm_w=v_mix_norm_w, w_in=v_w_in, dn_conv_w=v_dn_conv_w, dn_a_log=v_dn_a_log, dn_dt_bias=v_dn_dt_bias, dn_norm_w=v_dn_norm_w, rw_mu=v_rw_mu, rw_w0=v_rw_w0, rw_w2=v_rw_w2, rw_a0=v_rw_a0, rw_a2=v_rw_a2, rw_g2=v_rw_g2, rw_k_k=v_rw_k_k, rw_k_a=v_rw_k_a, rw_r_k=v_rw_r_k, rw_ln_w=v_rw_ln_w, rw_ln_b=v_rw_ln_b, w_out=v_w_out, xa_norm_w=v_xa_norm_w, mem_norm_w=v_mem_norm_w, xa_wq=v_xa_wq, xa_wk=v_xa_wk, xa_wv=v_xa_wv, xa_wo=v_xa_wo, ffn_norm_w=v_ffn_norm_w, ffn_w1=v_ffn_w1, ffn_w2=v_ffn_w2, final_norm_w=v_final_norm_w)
    names = list(weights)

    seq, d = x.shape[1], x.shape[2]
    dnw = d // 2
    rww = d - dnw
    nh, nb = dnw // LANES, rww // LANES
    n_mem = mem.shape[1]
    lw_dim, la_dim, lg_dim = rw_w2.shape[1], rw_a2.shape[1], rw_g2.shape[1]
    assert lw_dim + la_dim == LANES and lg_dim == LANES and dnw % LANES == 0 and rww % LANES == 0
    xs, mems, tgt = x[0], mem[0], loss_target[0]

    col_sharded = ("w_in", "xa_wo", "ffn_w1", "dn_conv_w", "rw_w2", "rw_a2", "rw_g2")
    row_sharded = ("w_out", "xa_wq", "xa_wk", "xa_wv", "ffn_w2")
    f32_payload = ("dn_conv_w", "rw_w2", "rw_a2", "rw_g2")
    sharded = col_sharded + row_sharded
    payload = {n: weights[n][0].astype(f32 if n in f32_payload else bf16) for n in sharded}
    first = ("w_in", "dn_conv_w", "rw_w2", "rw_a2", "rw_g2")
    mid = ("w_out", "xa_wq", "xa_wk", "xa_wv", "xa_wo")
    late = ("ffn_w1", "ffn_w2")
    gathered = dict(zip(first, gather_chips("gather_first", [payload[n] for n in first])))
    ordered = lax.optimization_barrier(([gathered[n] for n in first], [payload[n] for n in mid + late]))
    gathered = dict(zip(first, ordered[0]))
    payload.update(zip(mid + late, ordered[1]))
    mid_state, tok_mid = chips_start("gather_mid_start", [payload[n] for n in mid], _whole)
    late_state, tok_late = chips_start("gather_late_start", [payload[n] for n in late], _whole)
    mix_norm_w_in = mix_norm_w + (tok_mid + tok_late)

    def full(n):
        g = gathered[n]
        if n in col_sharded:
            return g.transpose(1, 0, 2).reshape(g.shape[1], 4 * g.shape[2])
        return g.reshape(4 * g.shape[1], g.shape[2])

    w_in_f = full("w_in")
    c_dn = 4 * dnw
    c_rw0 = c_dn + 2 * nh
    c_rw = 3 * rww
    w_main = jnp.concatenate([w_in_f[:, :c_dn], w_in_f[:, c_rw0:c_rw0 + c_rw]], axis=1)
    w_small = jnp.concatenate([w_in_f[:, c_dn:c_rw0], jnp.zeros((d, LANES - 2 * nh), bf16),
                               w_in_f[:, c_rw0 + c_rw:]], axis=1)
    conv_w = full("dn_conv_w")
    w2p = jnp.concatenate([full("rw_w2"), jnp.zeros((la_dim, rww), f32)], axis=0)
    a2p = jnp.concatenate([jnp.zeros((lw_dim, rww), f32), full("rw_a2")], axis=0)
    g2 = full("rw_g2")
    xaw = xa_wq.shape[2]
    nxh = xaw // LANES
    ffn = 4 * ffn_w1.shape[2]

    def lane_row(vec):
        return jnp.pad(vec.reshape(1, -1), ((0, 0), (0, LANES - vec.size)))

    alog_row, dtb_row = lane_row(dn_a_log), lane_row(dn_dt_bias)
    head_of_col = jnp.arange(dnw)[None, :] // LANES
    e_g = (jnp.arange(LANES)[:, None] == head_of_col).astype(f32)
    e_b = (jnp.arange(LANES)[:, None] == head_of_col + nh).astype(f32)
    mu_main, mu_small = rw_mu[:, :c_rw], rw_mu[:, c_rw:]
    r_k_row = rw_r_k.reshape(1, rww)
    fnw = final_norm_w.reshape(1, d)
    qb, kb_, vb, zb = 0, nh, 2 * nh, 3 * nh
    rb0 = 4 * nh
    cc = lambda w_, o_: ("constc", w_, o_)
    rc = lambda o_: ("rowc", LANES, o_)

    (u16,) = rowcall("norm_mix", fn_rms, [(xs, "row"), (mix_norm_w_in, "const")], [(seq, d, bf16, "row")], rows=seq)
    p_main = mm("in_main", u16, w_main)
    p_small = mm("in_small", u16, w_small)

    fn_gconv = make_fn_gconv(2 * nh)
    (qkv,) = rowcall("gdn_conv", fn_gconv, [(p_main, rc(0)), (conv_w, cc(LANES, 0))],
                     [(seq, 3 * dnw, f32, "rowc")], rows=seq, tile=seq, ncol=3 * nh)
    gate_ins = [(p_small, rc(0)), (alog_row, "const"), (dtb_row, "const"), (e_g, "const"), (e_b, "const")]
    g_b, beta_b = rowcall("gdn_gate", fn_ggate, gate_ins, [(seq, dnw, f32, "row")] * 2, rows=seq)
    gdn_ins = [(qkv, qb), (qkv, kb_), (qkv, vb), (g_b, 0), (beta_b, 0)]
    o_raw, gdn_states = scan_fwd("gdn_scan", gdn_chunk, gdn_ins, rows=seq, chunk=GDN_CHUNK, ncol=nh)
    gpost_ins = [(o_raw, rc(0)), (p_main, rc(zb)), (dn_norm_w, "const")]
    (o_dn,) = rowcall("gdn_post", fn_gpost, gpost_ins, [(seq, dnw, bf16, "rowc")], rows=seq, ncol=nh)

    (prw,) = rowcall("rw_lerp_main", fn_lerp, [(p_main, rc(rb0)), (mu_main, cc(LANES, 0))],
                     [(seq, c_rw, f32, "rowc")], rows=seq, tile=seq, ncol=3 * nb)
    (psl,) = rowcall("rw_lerp_small", fn_lerp, [(p_small, rc(1)), (mu_small, cc(LANES, 0))],
                     [(seq, 2 * LANES, f32, "rowc")], rows=seq, tile=seq, ncol=2)
    rprep_ins = [(prw, rc(nb)), (psl, "row"), (rw_w0, cc(LANES, 0)), (rw_a0, cc(LANES, 0)), (rw_k_k, cc(LANES, 0)),
                 (rw_k_a, cc(LANES, 0)), (w2p, cc(LANES, 0)), (a2p, cc(LANES, 0)), (g2, cc(LANES, 0))]
    lw, kmod, kk, a_rw, gate = rowcall("rw_prep", fn_rprep, rprep_ins, [(seq, rww, f32, "rowc")] * 5,
                                        rows=seq, ncol=nb)
    wkv_ins = [(prw, 0), (lw, 0), (kmod, 0), (prw, 2 * nb), (kk, 0), (a_rw, 0)]
    y_rw, wkv_states = scan_fwd("wkv_scan", wkv_chunk, wkv_ins, rows=seq, chunk=WKV_CHUNK, ncol=nb)
    rpost_ins = [(y_rw, rc(0)), (prw, rc(0)), (kmod, rc(0)), (prw, rc(2 * nb)), (gate, rc(0)),
                 (r_k_row, cc(LANES, 0)), (rw_ln_w, cc(LANES, 0)), (rw_ln_b, cc(LANES, 0))]
    (o_rw,) = rowcall("rw_post", fn_rpost, rpost_ins, [(seq, rww, bf16, "rowc")], rows=seq, ncol=nb)

    o_cat = jnp.concatenate([o_dn, o_rw], axis=1)
    gathered.update(zip(mid, chips_wait("gather_mid_wait", mid_state, o_cat)))
    w_out_f, wq_f, wk_f, wv_f, wo_f = full("w_out"), full("xa_wq"), full("xa_wk"), full("xa_wv"), full("xa_wo")
    h1 = mm("out_proj", o_cat, w_out_f, add=xs)

    (hn16,) = rowcall("norm_xa", fn_rms, [(h1, "row"), (xa_norm_w, "const")], [(seq, d, bf16, "row")], rows=seq)
    (mn16,) = rowcall("norm_mem", fn_rms, [(mems, "row"), (mem_norm_w, "const")], [(n_mem, d, bf16, "row")],
                      rows=n_mem)
    q_xa = mm("xa_q", hn16, wq_f)
    k_xa = mm("xa_k", mn16, wk_f)
    v_xa = mm("xa_v", mn16, wv_f)
    xcore_ins = [(q_xa, rc(0)), (k_xa, cc(LANES, 0)), (v_xa, cc(LANES, 0))]
    (o_xa,) = rowcall("xa_core", fn_xcore, xcore_ins, [(seq, xaw, bf16, "rowc")], rows=seq, ncol=nxh)
    h2 = mm("xa_o", o_xa, wo_f, add=h1)

    (fn16,) = rowcall("norm_ffn", fn_rms, [(h2, "row"), (ffn_norm_w, "const")], [(seq, d, bf16, "row")], rows=seq)
    gathered.update(zip(late, chips_wait("gather_late_wait", late_state, fn16)))
    w1_f, w2_f = full("ffn_w1"), full("ffn_w2")
    a1 = mm("ffn_up", fn16, w1_f)
    nfb = ffn // (4 * LANES)
    (hid16,) = rowcall("ffn_act", fn_act, [(a1, ("rowc", 4 * LANES, 0))], [(seq, ffn, bf16, "rowc")], rows=seq,
                       ncol=nfb)
    h3 = mm("ffn_down", hid16, w2_f, add=h2)

    dh3, dh3_16, d_fnw, loss_rows = rowcall(
        "loss_head", fn_final, [(h3, "row"), (tgt, "row"), (fnw, "const")],
        [(seq, d, f32, "row"), (seq, d, bf16, "row"), (1, d, f32, "acc"), (8, LANES, f32, "acc")], rows=seq)

    dhid = mm("ffn_down_dx", dh3_16, w2_f, tb=True)
    (da1_16,) = rowcall("ffn_act_bwd", fn_actb, [(a1, ("rowc", 4 * LANES, 0)), (dhid, ("rowc", 4 * LANES, 0))],
                        [(seq, ffn, bf16, "rowc")], rows=seq, ncol=nfb)
    g_ffn_w2 = mm("ffn_down_dw", hid16, dh3_16, ta=True, out_dtype=bf16)
    g_ffn_w1 = mm("ffn_up_dw", fn16, da1_16, ta=True, out_dtype=bf16)
    dfn = mm("ffn_up_dx", da1_16, w1_f, tb=True)

    def by_chip(n, g):
        if n in col_sharded:
            return g.reshape(g.shape[0], 4, g.shape[1] // 4).transpose(1, 0, 2)
        return g.reshape(4, g.shape[0] // 4, g.shape[1])

    late_g, tok = chips_start("scatter_late_start", [by_chip("ffn_w1", g_ffn_w1), by_chip("ffn_w2", g_ffn_w2)],
                              _chip_block)
    dh2, d_ffn_nw, dh2_16 = rowvjp("norm_ffn_bwd", fn_rms, [(h2, "row"), (ffn_norm_w + tok, "const")],
                                   [[(dfn, "row")]], [0, 1], rows=seq, adds=[(0, dh3, "row")], dup16=[0])

    do_xa = mm("xa_o_dx", dh2_16, wo_f, tb=True)
    g_xa_wo = mm("xa_o_dw", o_xa, dh2_16, ta=True, out_dtype=bf16)
    dq_xa, dk_xa, dv_xa, dq16 = rowvjp("xa_core_bwd", fn_xcore, xcore_ins, [[(do_xa, rc(0))]], [0, 1, 2],
                                       rows=seq, ncol=nxh, dup16=[0])
    g_xa_wq = mm("xa_q_dw", hn16, dq16, ta=True, out_dtype=bf16)
    dhn = mm("xa_q_dx", dq16, wq_f, tb=True)
    dh1, d_xa_nw, dh1_16 = rowvjp("norm_xa_bwd", fn_rms, [(h1, "row"), (xa_norm_w, "const")], [[(dhn, "row")]],
                                  [0, 1], rows=seq, adds=[(0, dh2, "row")], dup16=[0])
    dk16, dv16 = dk_xa.astype(bf16), dv_xa.astype(bf16)
    g_xa_wk = mm("xa_k_dw", mn16, dk16, ta=True, out_dtype=bf16)
    g_xa_wv = mm("xa_v_dw", mn16, dv16, ta=True, out_dtype=bf16)
    dmn = mm("xa_v_dx", dv16, wv_f, tb=True, add=mm("xa_k_dx", dk16, wk_f, tb=True))
    (d_mem_nw,) = rowvjp("norm_mem_bwd", fn_rms, [(mems, "row"), (mem_norm_w, "const")], [[(dmn, "row")]], [1],
                         rows=n_mem)

    g_w_out = mm("out_proj_dw", o_cat, dh1_16, ta=True, out_dtype=bf16)
    do_cat = mm("out_proj_dx", dh1_16, w_out_f, tb=True)
    mid_grads = dict(w_out=g_w_out, xa_wq=g_xa_wq, xa_wk=g_xa_wk, xa_wv=g_xa_wv, xa_wo=g_xa_wo)
    mid_g, tok = chips_start("scatter_mid_start", [by_chip(n, mid_grads[n]) for n in mid], _chip_block)
    rpost_ins_b = rpost_ins[:5] + [(r_k_row + tok, cc(LANES, 0))] + rpost_ins[6:]

    dy, dr_a, dkmod_a, dv_a, dgate, d_r_k, d_ln_w, d_ln_b = rowvjp(
        "rw_post_bwd", fn_rpost, rpost_ins_b, [[(do_cat, rc(nh))]], [0, 1, 2, 3, 4, 5, 6, 7], rows=seq, ncol=nb)
    dr_b, dlw, dkmod_b, dv_b, dkk, da_rw = scan_bwd("wkv_scan_bwd", wkv_chunk, wkv_ins, wkv_states, dy, 0,
                                                    rows=seq, chunk=WKV_CHUNK, ncol=nb)
    rprep_cts = [[(dlw, rc(0))], [(dkmod_a, rc(0)), (dkmod_b, rc(0))], [(dkk, rc(0))], [(da_rw, rc(0))],
                 [(dgate, rc(0))]]
    dpk, dpsl_parts, d_w0, d_a0, d_k_k, d_k_a, d_w2p, d_a2p, d_g2 = rowvjp(
        "rw_prep_bwd", fn_rprep, rprep_ins, rprep_cts, [0, 1, 2, 3, 4, 5, 6, 7, 8], rows=seq, ncol=nb)
    (dpsl,) = rowcall("rw_prep_sum", fn_sumcols(nb), [(dpsl_parts, "row")], [(seq, 2 * LANES, f32, "row")], rows=seq)

    def lerp_bwd(tag, p, p_off, mu, mu_off, ct_lists, ncol):
        return rowvjp("rw_lerp_bwd_" + tag, fn_lerp, [(p, rc(p_off)), (mu, cc(LANES, mu_off))], [ct_lists], [0, 1],
                      rows=seq, tile=seq, ncol=ncol, dup16=[0])

    _, dmu_r, dpr16 = lerp_bwd("r", p_main, rb0, mu_main, 0, [(dr_a, rc(0)), (dr_b, rc(0))], nb)
    _, dmu_k, dpk16 = lerp_bwd("k", p_main, rb0 + nb, mu_main, nb, [(dpk, rc(0))], nb)
    _, dmu_v, dpv16 = lerp_bwd("v", p_main, rb0 + 2 * nb, mu_main, 2 * nb, [(dv_a, rc(0)), (dv_b, rc(0))], nb)
    _, dmu_s, dps12_16 = lerp_bwd("small", p_small, 1, mu_small, 0, [(dpsl, rc(0))], 2)

    do_raw, dz, d_dn_nw, dz16 = rowvjp("gdn_post_bwd", fn_gpost, gpost_ins, [[(do_cat, rc(0))]], [0, 1, 2],
                                       rows=seq, ncol=nh, dup16=[1])
    dq_g, dk_g, dv_g, dg_b, dbeta_b = scan_bwd("gdn_scan_bwd", gdn_chunk, gdn_ins, gdn_states, do_raw, 0,
                                               rows=seq, chunk=GDN_CHUNK, ncol=nh)
    dps0, d_alog, d_dtb, dps0_16 = rowvjp("gdn_gate_bwd", fn_ggate, gate_ins, [[(dg_b, "row")], [(dbeta_b, "row")]],
                                          [0, 1, 2], rows=seq, dup16=[0])
    dqkv = jnp.concatenate([dq_g, dk_g, dv_g], axis=1)
    _, d_conv_w, dqkv16 = rowvjp("gdn_conv_bwd", fn_gconv, [(p_main, rc(0)), (conv_w, cc(LANES, 0))],
                                 [[(dqkv, rc(0))]], [0, 1], rows=seq, tile=seq, ncol=3 * nh, dup16=[0])

    dp_main16 = jnp.concatenate([dqkv16, dz16, dpr16, dpk16, dpv16], axis=1)
    dp_small16 = jnp.concatenate([dps0_16, dps12_16], axis=1)
    du = mm("in_small_dx", dp_small16, w_small, tb=True, add=mm("in_main_dx", dp_main16, w_main, tb=True))
    g_w_main = mm("in_main_dw", u16, dp_main16, ta=True, out_dtype=bf16)
    g_w_small = mm("in_small_dw", u16, dp_small16, ta=True, out_dtype=bf16)
    grad_x, d_mix_nw = rowvjp("norm_mix_bwd", fn_rms, [(xs, "row"), (mix_norm_w, "const")], [[(du, "row")]], [0, 1],
                              rows=seq, adds=[(0, dh1, "row")])

    g_w_in = jnp.concatenate([g_w_main[:, :c_dn], g_w_small[:, :2 * nh], g_w_main[:, c_dn:],
                              g_w_small[:, LANES:]], axis=1)
    first_grads = dict(w_in=g_w_in, dn_conv_w=d_conv_w, rw_w2=d_w2p[:lw_dim], rw_a2=d_a2p[lw_dim:], rw_g2=d_g2)
    received = dict(zip(first, scatter_chips("scatter_first", [by_chip(n, first_grads[n]) for n in first])))
    received.update(zip(mid, chips_wait("scatter_mid_wait", mid_g, received["w_in"])))
    received.update(zip(late, chips_wait("scatter_late_wait", late_g, received["w_in"])))
    partial = [sum_slots("sum_chips_" + n, received[n]) for n in sharded]
    other = swap_sibling(partial)

    small_names = [n for n in names if n not in sharded]
    small_local = dict(
        mix_norm_w=d_mix_nw, dn_a_log=d_alog[:, :nh], dn_dt_bias=d_dtb[:, :nh], dn_norm_w=d_dn_nw,
        rw_mu=jnp.concatenate([dmu_r, dmu_k, dmu_v, dmu_s], axis=1), rw_w0=d_w0, rw_a0=d_a0, rw_k_k=d_k_k,
        rw_k_a=d_k_a, rw_r_k=d_r_k, rw_ln_w=d_ln_w, rw_ln_b=d_ln_b, xa_norm_w=d_xa_nw, mem_norm_w=d_mem_nw,
        ffn_norm_w=d_ffn_nw, final_norm_w=d_fnw)
    loss_vec = jnp.where(jnp.arange(LANES) == 0, loss_rows[0], 0.0)
    (g8,) = gather_all([_pack([small_local[n] for n in small_names] + [loss_vec])])

    out = {}
    for n, p_mine, p_other in zip(sharded, partial, other):
        w2d, m2d, v2d = weights[n][0], mom_m[n][0], mom_v[n][0]
        res = adamw("adamw_" + n, w2d, [p_mine, p_other], m2d, v2d)
        out[n] = [r.reshape(weights[n].shape) for r in res]
    packed_like = [weights[n] for n in small_names] + [loss_vec]
    zero = jnp.zeros((LANES,), f32)
    res = adamw_packed(_pack([weights[n] for n in small_names] + [zero]), g8,
                       _pack([mom_m[n] for n in small_names] + [zero]),
                       _pack([mom_v[n] for n in small_names] + [zero]))
    unpacked = [_unpack(r, packed_like) for r in res]
    for i, n in enumerate(small_names):
        out[n] = [u[i] for u in unpacked]
    loss = unpacked[0][-1][0]

    return (loss, grad_x.reshape(x.shape), *[out[n][0] for n in names], *[out[n][1] for n in names],
            *[out[n][2] for n in names], *[out[n][3] for n in names])
```

```python
import functools

import jax
import jax.numpy as jnp
from jax import lax
from jax.experimental import pallas as pl
from jax.experimental.pallas import tpu as pltpu

f32 = jnp.float32
bf16 = jnp.bfloat16
HI = lax.Precision.HIGHEST
MESH = pl.DeviceIdType.MESH

LANES = 128
VMEM_LIMIT = 56 << 20
TOK_TILE = 256
MM_TILE = 1024
MM_TILE_K = 2048
GDN_CHUNK = 64
WKV_CHUNK = 32
SCAN_GROUP = 8
P_BULK = 1
P_INV = 3
RMS_EPS = 1e-6
RW_GN_EPS = 64e-5
RW_HEAD = 64

ADAM_LR, ADAM_B1, ADAM_B2, ADAM_EPS, ADAM_WD, ADAM_STEP = 0.001, 0.9, 0.999, 1e-08, 0.01, 10


def _params(n_grid):
    return pltpu.CompilerParams(dimension_semantics=("arbitrary",) * n_grid, vmem_limit_bytes=VMEM_LIMIT)


_DIMS = {"nn": (((1,), (0,)), ((), ())), "nt": (((1,), (1,)), ((), ())), "tn": (((0,), (0,)), ((), ()))}
_DIMS_BATCHED = {"nn": (((2,), (1,)), ((0,), (0,))), "nt": (((2,), (2,)), ((0,), (0,))),
                 "tn": (((1,), (1,)), ((0,), (0,)))}


def _raw_dot(a, b, mode, passes):
    dims = (_DIMS if a.ndim == 2 else _DIMS_BATCHED)[mode]
    if passes == 6:
        return lax.dot_general(a.astype(f32), b.astype(f32), dims, precision=HI, preferred_element_type=f32)
    ah, bh = a.astype(bf16), b.astype(bf16)
    r = lax.dot_general(ah, bh, dims, preferred_element_type=f32)
    if passes == 3:
        al = (a - ah.astype(f32)).astype(bf16)
        bl = (b - bh.astype(f32)).astype(bf16)
        r = r + lax.dot_general(al, bh, dims, preferred_element_type=f32)
        r = r + lax.dot_general(ah, bl, dims, preferred_element_type=f32)
    return r


@functools.partial(jax.custom_vjp, nondiff_argnums=(2, 3))
def pdot(a, b, mode, passes):
    return _raw_dot(a, b, mode, passes)


def _pdot_bwd(mode, passes, res, g):
    a, b = res
    if mode == "nn":
        da, db = _raw_dot(g, b, "nt", passes), _raw_dot(a, g, "tn", passes)
    elif mode == "nt":
        da, db = _raw_dot(g, b, "nn", passes), _raw_dot(g, a, "tn", passes)
    else:
        da, db = _raw_dot(b, g, "nt", passes), _raw_dot(a, g, "nn", passes)
    return da.astype(a.dtype), db.astype(b.dtype)


pdot.defvjp(lambda a, b, mode, passes: (_raw_dot(a, b, mode, passes), (a, b)), _pdot_bwd)


def hdot(a, b):
    return pdot(a, b, "nn", 6)


def bdot(a, b):
    return pdot(a, b, "nn", 1)


def bdot_nt(a, b):
    return pdot(a, b, "nt", 1)


def _shift_rows(x, k):
    row = lax.broadcasted_iota(jnp.int32, x.shape, 0)
    return jnp.where(row < k, 0.0, pltpu.roll(x, k, axis=0))


def _unshift_rows(g, k):
    n = g.shape[0]
    row = lax.broadcasted_iota(jnp.int32, g.shape, 0)
    return jnp.where(row >= n - k, 0.0, pltpu.roll(g, n - k, axis=0))


@functools.partial(jax.custom_vjp, nondiff_argnums=(1,))
def tshift(x, k):
    return _shift_rows(x, k)


tshift.defvjp(lambda x, k: (_shift_rows(x, k), None), lambda k, _, g: (_unshift_rows(g, k),))


def rms(x, w):
    x = x.astype(f32)
    return x * lax.rsqrt(jnp.mean(x * x, axis=-1, keepdims=True) + RMS_EPS) * w


def softplus(x):
    return jnp.maximum(x, 0.0) + jnp.log(1.0 + jnp.exp(-jnp.abs(x)))


def seg2sum(x):
    lo = lax.broadcasted_iota(jnp.int32, x.shape, 1) < RW_HEAD
    s_lo = jnp.sum(jnp.where(lo, x, 0.0), axis=-1, keepdims=True)
    s_hi = jnp.sum(jnp.where(lo, 0.0, x), axis=-1, keepdims=True)
    return jnp.where(lo, s_lo, s_hi)


def _tile(n, pref):
    if n <= pref:
        return n
    t = pref
    while t >= LANES:
        if n % t == 0:
            return t
        t -= LANES
    return n


def mm(name, a, b, *, ta=False, tb=False, add=None, out_dtype=f32):
    (k, m) = a.shape if ta else a.shape[::-1]
    (n, kb) = b.shape if tb else b.shape[::-1]
    assert k == kb, (name, a.shape, b.shape)
    tm, tn, tk = _tile(m, MM_TILE), _tile(n, MM_TILE), _tile(k, MM_TILE_K)
    nk = k // tk
    dims = (((0,) if ta else (1,), (1,) if tb else (0,)), ((), ()))

    def body(*refs):
        a_ref, b_ref = refs[:2]
        c_ref = None if add is None else refs[2]
        o_ref = refs[2 if add is None else 3]
        part = lax.dot_general(a_ref[...].astype(bf16), b_ref[...].astype(bf16), dims, preferred_element_type=f32)

        def finish(r):
            if add is not None:
                r = r + c_ref[...].astype(f32)
            o_ref[...] = r.astype(o_ref.dtype)

        if nk == 1:
            finish(part)
            return
        acc = refs[-1]
        kk = pl.program_id(2)

        @pl.when(kk == 0)
        def _():
            acc[...] = part

        @pl.when(kk > 0)
        def _():
            acc[...] += part

        @pl.when(kk == nk - 1)
        def _():
            finish(acc[...])

    a_spec = pl.BlockSpec((tk, tm), lambda i, j, q: (q, i)) if ta else pl.BlockSpec((tm, tk), lambda i, j, q: (i, q))
    b_spec = pl.BlockSpec((tn, tk), lambda i, j, q: (j, q)) if tb else pl.BlockSpec((tk, tn), lambda i, j, q: (q, j))
    o_spec = pl.BlockSpec((tm, tn), lambda i, j, q: (i, j))
    ins, specs = [a, b], [a_spec, b_spec]
    if add is not None:
        ins.append(add)
        specs.append(o_spec)
    return pl.pallas_call(
        body, name=name, grid=(m // tm, n // tn, nk), in_specs=specs, out_specs=o_spec,
        out_shape=jax.ShapeDtypeStruct((m, n), out_dtype),
        scratch_shapes=[pltpu.VMEM((tm, tn), f32)] if nk > 1 else [], compiler_params=_params(3))(*ins)


def _in_spec(a, kind, tile):
    if kind == "row":
        return pl.BlockSpec((tile, a.shape[1]), lambda j, i: (i, 0))
    if kind == "const":
        return pl.BlockSpec(a.shape, lambda j, i: (0, 0))
    tag, cw, off = kind
    if tag == "rowc":
        return pl.BlockSpec((tile, cw), lambda j, i: (i, j + off))
    assert tag == "constc", kind
    return pl.BlockSpec((a.shape[0], cw), lambda j, i: (0, j + off))


def rowcall(name, fn, ins, outs, *, rows, tile=None, ncol=1):
    tile = min(TOK_TILE, rows) if tile is None else tile
    n_in = len(ins)
    kinds = [o[3] for o in outs]

    def body(*refs):
        j, i = pl.program_id(0), pl.program_id(1)
        res = fn(*[r[...] for r in refs[:n_in]])
        for ref, val, kind in zip(refs[n_in:], res, kinds):
            if kind in ("row", "rowc"):
                ref[...] = val.astype(ref.dtype)
            else:
                first = (i == 0) if kind == "accc" else jnp.logical_and(i == 0, j == 0)

                @pl.when(first)
                def _(ref=ref, val=val):
                    ref[...] = val.astype(ref.dtype)

                @pl.when(jnp.logical_not(first))
                def _(ref=ref, val=val):
                    ref[...] += val.astype(ref.dtype)

    out_shape, out_specs = [], []
    for nr, nc, dtype, kind in outs:
        out_shape.append(jax.ShapeDtypeStruct((nr, nc), dtype))
        if kind == "row":
            out_specs.append(pl.BlockSpec((tile, nc), lambda j, i: (i, 0)))
        elif kind == "rowc":
            out_specs.append(pl.BlockSpec((tile, nc // ncol), lambda j, i: (i, j)))
        elif kind == "acc":
            out_specs.append(pl.BlockSpec((nr, nc), lambda j, i: (0, 0)))
        else:
            out_specs.append(pl.BlockSpec((nr, nc // ncol), lambda j, i: (0, j)))
    return pl.pallas_call(
        body, name=name, grid=(ncol, rows // tile), in_specs=[_in_spec(a, k, tile) for a, k in ins],
        out_specs=out_specs, out_shape=out_shape, compiler_params=_params(2))(*[a for a, _ in ins])


def rowvjp(name, fn, ins, cts, grads, *, rows, tile=None, ncol=1, adds=(), dup16=()):
    n_in = len(ins)
    ct_sizes = [len(c) for c in cts]
    flat_cts = [m for c in cts for m in c]
    n_ct = len(flat_cts)

    def wrapped(*vals):
        xs = list(vals[:n_in])
        gs = vals[n_in:n_in + n_ct]
        extra = vals[n_in + n_ct:]

        def f(*dvars):
            full = list(xs)
            for k, v in zip(grads, dvars):
                full[k] = v
            return fn(*full)

        outs, pull = jax.vjp(f, *[xs[k] for k in grads])
        cot, p = [], 0
        for o, size in zip(outs, ct_sizes):
            g = gs[p].astype(f32)
            for q in range(1, size):
                g = g + gs[p + q].astype(f32)
            cot.append(g.astype(o.dtype))
            p += size
        gv = list(pull(tuple(cot)))
        for (pos, _, _), e in zip(adds, extra):
            gv[pos] = gv[pos] + e.astype(gv[pos].dtype)
        return tuple(gv) + tuple(gv[pos] for pos in dup16)

    outs = []
    for k in grads:
        a, kind = ins[k]
        if kind == "row":
            outs.append((rows, a.shape[1] * ncol, f32, "rowc") if ncol > 1 else (rows, a.shape[1], f32, "row"))
        elif kind == "const":
            outs.append((a.shape[0], a.shape[1], f32, "acc"))
        elif kind[0] == "rowc":
            outs.append((rows, kind[1] * ncol, f32, "rowc"))
        else:
            outs.append((a.shape[0], kind[1] * ncol, f32, "accc"))
    for pos in dup16:
        nr, nc, _, kind = outs[pos]
        outs.append((nr, nc, bf16, kind))
    all_ins = list(ins) + flat_cts + [(a, kind) for _, a, kind in adds]
    return rowcall(name, wrapped, all_ins, outs, rows=rows, tile=tile, ncol=ncol)


def fn_rms(x, w):
    return (rms(x, w),)


def make_fn_gconv(n_norm_blocks):
    def fn(p, cw):
        c = cw[3:4] * p
        for jj in range(3):
            c = c + cw[jj:jj + 1] * tshift(p, 3 - jj)
        s = c * jax.nn.sigmoid(c)
        nrm = s * lax.rsqrt(jnp.sum(s * s, axis=-1, keepdims=True) + 1e-6)
        return (jnp.where(pl.program_id(0) < n_norm_blocks, nrm, s),)
    return fn


def fn_ggate(ps0, alog, dtb, e_g, e_b):
    g = -jnp.exp(alog) * softplus(ps0 + dtb)
    beta = jax.nn.sigmoid(ps0)
    return hdot(g, e_g), hdot(beta, e_b)


def fn_gpost(o, z, nw):
    return (rms(o, nw) * (z * jax.nn.sigmoid(z)),)


def fn_lerp(p, mu):
    return (p + (tshift(p, 1) - p) * mu,)


def fn_rprep(pk, psl, w0, a0, k_k, k_a, w2p, a2p, g2):
    g1, g2in = psl[:, :LANES], psl[:, LANES:]
    log_w = -softplus(-(w0 + bdot(jnp.tanh(g1), w2p))) - 0.5
    lw = -jnp.exp(log_w)
    a = jax.nn.sigmoid(a0 + bdot(g1, a2p))
    gate = bdot(jax.nn.sigmoid(g2in), g2)
    kkr = pk * k_k
    kk = kkr / jnp.maximum(jnp.sqrt(seg2sum(kkr * kkr)), 1e-12)
    kmod = pk * (1.0 + (a - 1.0) * k_a)
    return lw, kmod, kk, a, gate


def fn_rpost(y, r, kmod, v, gate, r_k, ln_w, ln_b):
    inv_n = 1.0 / RW_HEAD
    mean = seg2sum(y) * inv_n
    d = y - mean
    var = seg2sum(d * d) * inv_n
    yn = d * lax.rsqrt(var + RW_GN_EPS) * ln_w + ln_b
    bonus = seg2sum(r * kmod * r_k) * v
    return ((yn + bonus) * gate,)


def fn_xcore(q, k, v):
    s = bdot_nt(q, k) * (LANES ** -0.5)
    p = jax.nn.softmax(s, axis=-1)
    return (bdot(p, v),)


def fn_act(a1):
    r = jnp.maximum(a1, 0.0)
    return ((r * r).astype(bf16),)


def fn_actb(a1, dhid):
    return ((dhid * 2.0 * jnp.maximum(a1, 0.0)).astype(bf16),)


def fn_final(h, tgt, w):
    def loss_fn(h, w):
        return 0.5 * jnp.sum(jnp.mean(jnp.square(rms(h, w) - tgt), axis=-1))

    val, (dh, dw) = jax.value_and_grad(loss_fn, argnums=(0, 1))(h, w)
    return dh, dh.astype(bf16), dw, jnp.full((8, LANES), val, f32)


def fn_sumcols(n):
    def fn(x):
        w = x.shape[1] // n
        s = x[:, :w]
        for q in range(1, n):
            s = s + x[:, q * w:(q + 1) * w]
        return (s,)
    return fn


def _tri(c):
    ii = lax.broadcasted_iota(jnp.int32, (c, c), 0)
    jj = lax.broadcasted_iota(jnp.int32, (c, c), 1)
    return ii, jj


def _neumann_raw(m, steps):
    c = m.shape[-1]
    ii, jj = _tri(c)
    eye = (ii == jj).astype(f32)
    t, p = eye + m, m
    for _ in range(steps):
        p = _raw_dot(p, p, "nn", P_INV)
        t = _raw_dot(t, eye + p, "nn", P_INV)
    return t


@functools.partial(jax.custom_vjp, nondiff_argnums=(1,))
def _neumann_inverse(m, steps):
    return _neumann_raw(m, steps)


def _neumann_fwd(m, steps):
    t = _neumann_raw(m, steps)
    return t, t


def _neumann_bwd(steps, t, g):
    return (_raw_dot(_raw_dot(t, g, "tn", P_INV), t, "nt", P_INV),)


_neumann_inverse.defvjp(_neumann_fwd, _neumann_bwd)


def cumsum_rows(x):
    t = x.shape[1]
    ii, jj = _tri(t)
    tri = jnp.broadcast_to((ii >= jj).astype(f32), (x.shape[0], t, t))
    return pdot(tri, x, "nn", P_INV)


def gdn_chunk(q, k, v, gb, bb, s):
    c = q.shape[1]
    ii, jj = _tri(c)
    low = ii >= jj
    gcb = cumsum_rows(gb)
    gl = jnp.sum(gb, axis=1, keepdims=True)
    gc_col = gcb[:, :, :c]
    diff = gc_col - jnp.swapaxes(gc_col, 1, 2)
    decay = jnp.where(low, jnp.exp(jnp.where(low, diff, 0.0)), 0.0)
    qs = q * (q.shape[2] ** -0.5)
    kb = k * bb
    a = jnp.where(ii > jj, pdot(kb, k, "nt", P_BULK) * decay, 0.0)
    t = _neumann_inverse(-a, 5)
    eg = jnp.exp(gcb)
    u = pdot(t, v * bb, "nn", P_BULK)
    w = pdot(t, kb * eg, "nn", P_BULK)
    attn = pdot(qs, k, "nt", P_BULK) * decay
    kd = k * jnp.exp(gl - gcb)
    v_new = u - pdot(w, s, "nn", P_BULK)
    o = pdot(qs * eg, s, "nn", P_BULK) + pdot(attn, v_new, "nn", P_BULK)
    s_new = s * jnp.exp(gl) + pdot(kd, v_new, "tn", P_BULK)
    return o, s_new


def wkv_chunk(r, lw, k, v, kk, a, s):
    t = r.shape[1]
    ii, jj = _tri(t)
    lo = lax.broadcasted_iota(jnp.int32, r.shape, 2) < RW_HEAD
    cl = cumsum_rows(lw)
    cl_last = jnp.sum(lw, axis=1, keepdims=True)
    al = -kk * jnp.exp(cl - lw)
    be = (a * kk) * jnp.exp(-cl)
    kt = k * jnp.exp(-cl)
    rt = r * jnp.exp(cl)

    def dot(xa, xb, mode="nn"):
        return pdot(xa, xb, mode, P_BULK)

    def pair(xa, xb, msk):
        m_lo = dot(jnp.where(lo, xa, 0.0), xb, "nt")
        m_hi = dot(jnp.where(lo, 0.0, xa), xb, "nt")
        return jnp.where(msk, m_lo, 0.0), jnp.where(msk, m_hi, 0.0)

    def sel(x_lo, x_hi):
        return jnp.where(lo, x_lo, x_hi)

    ab = pair(al, be, ii > jj)
    ak = pair(al, kt, ii > jj)
    rb = pair(rt, be, ii >= jj)
    rk = pair(rt, kt, ii >= jj)
    x = dot(al, s, "nt") + sel(dot(ak[0], v), dot(ak[1], v))
    u = sel(dot(_neumann_inverse(ab[0], 4), x), dot(_neumann_inverse(ab[1], 4), x))
    y = dot(rt, s, "nt") + sel(dot(rb[0], u) + dot(rk[0], v), dot(rb[1], u) + dot(rk[1], v))
    vi = lax.broadcasted_iota(jnp.int32, s.shape, 1) < RW_HEAD
    ki = lax.broadcasted_iota(jnp.int32, s.shape, 2) < RW_HEAD
    s_new = jnp.where(vi == ki, (s + dot(u, be, "tn") + dot(v, kt, "tn")) * jnp.exp(cl_last), 0.0)
    return y, s_new


def _scan_group(ncol, offs):
    g = SCAN_GROUP
    while g > 1 and (ncol % g or any(o % g for o in offs)):
        g //= 2
    return g


def scan_fwd(name, chunk_fn, ins, *, rows, chunk, ncol):
    n = rows // chunk
    n_in = len(ins)
    grp = _scan_group(ncol, [off for _, off in ins])

    def body(*refs):
        o_ref, st_ref, s_scr = refs[n_in:]

        @pl.when(pl.program_id(1) == 0)
        def _():
            s_scr[...] = jnp.zeros_like(s_scr)

        cols = [slice(b * LANES, (b + 1) * LANES) for b in range(grp)]
        s = s_scr[...]
        st_ref[...] = s
        o, s_new = chunk_fn(*[jnp.stack([r[:, c] for c in cols]) for r in refs[:n_in]], s)
        for b, c in enumerate(cols):
            o_ref[:, c] = o[b]
        s_scr[...] = s_new

    def spec(off):
        return pl.BlockSpec((chunk, grp * LANES), lambda h, c: (c, h + off // grp))

    return pl.pallas_call(
        body, name=name, grid=(ncol // grp, n), in_specs=[spec(off) for _, off in ins],
        out_specs=[spec(0), pl.BlockSpec((grp, None, LANES, LANES), lambda h, c: (h, c, 0, 0))],
        out_shape=[jax.ShapeDtypeStruct((rows, ncol * LANES), f32),
                   jax.ShapeDtypeStruct((ncol, n, LANES, LANES), f32)],
        scratch_shapes=[pltpu.VMEM((grp, LANES, LANES), f32)], compiler_params=_params(2))(*[a for a, _ in ins])


def scan_bwd(name, chunk_fn, ins, states, d_out, d_off, *, rows, chunk, ncol):
    n = rows // chunk
    n_in = len(ins)
    grp = _scan_group(ncol, [off for _, off in ins] + [d_off])

    def body(*refs):
        st_ref, do_ref = refs[n_in:n_in + 2]
        g_refs = refs[n_in + 2:2 * n_in + 2]
        ds_scr = refs[-1]

        @pl.when(pl.program_id(1) == 0)
        def _():
            ds_scr[...] = jnp.zeros_like(ds_scr)

        cols = [slice(b * LANES, (b + 1) * LANES) for b in range(grp)]

        def batch(ref):
            return jnp.stack([ref[:, c] for c in cols])

        _, pull = jax.vjp(chunk_fn, *[batch(r) for r in refs[:n_in]], st_ref[...])
        gs = pull((batch(do_ref), ds_scr[...]))
        for ref, g in zip(g_refs, gs[:n_in]):
            for b, c in enumerate(cols):
                ref[:, c] = g[b]
        ds_scr[...] = gs[n_in]

    def spec(off):
        return pl.BlockSpec((chunk, grp * LANES), lambda h, c: (n - 1 - c, h + off // grp))

    st_spec = pl.BlockSpec((grp, None, LANES, LANES), lambda h, c: (h, n - 1 - c, 0, 0))
    return pl.pallas_call(
        body, name=name, grid=(ncol // grp, n), in_specs=[spec(off) for _, off in ins] + [st_spec, spec(d_off)],
        out_specs=[spec(0)] * n_in, out_shape=[jax.ShapeDtypeStruct((rows, ncol * LANES), f32)] * n_in,
        scratch_shapes=[pltpu.VMEM((grp, LANES, LANES), f32)],
        compiler_params=_params(2))(*[a for a, _ in ins], states, d_out)


def flip_exchange(name, arrs, flips, n_slots, slot_of, src_of, with_self):
    n = len(arrs)
    nf = len(flips)

    def body(*refs):
        ins, outs = refs[:n], refs[n:2 * n]
        send, recv, lsem = refs[2 * n:]
        me = (lax.axis_index("x"), lax.axis_index("y"), lax.axis_index("c"))
        copies = []
        for k in range(n):
            if with_self:
                cp = pltpu.make_async_copy(src_of(ins[k], me), outs[k].at[slot_of(me)], lsem.at[k])
                cp.start()
                copies.append(cp)
            for j, fl in enumerate(flips):
                peer = tuple(1 - m if f else m for m, f in zip(me, fl))
                cp = pltpu.make_async_remote_copy(
                    src_ref=src_of(ins[k], peer), dst_ref=outs[k].at[slot_of(me)], send_sem=send.at[k, j],
                    recv_sem=recv.at[k, j], device_id=peer, device_id_type=MESH)
                cp.start()
                copies.append(cp)
        for cp in copies:
            cp.wait()

    def out_sds(a):
        blk = src_of(jax.ShapeDtypeStruct(a.shape, a.dtype), None)
        return jax.ShapeDtypeStruct((n_slots,) + tuple(blk), a.dtype)

    any_spec = pl.BlockSpec(memory_space=pl.ANY)
    return pl.pallas_call(
        body, name=name, in_specs=[any_spec] * n, out_specs=[any_spec] * n, out_shape=[out_sds(a) for a in arrs],
        scratch_shapes=[pltpu.SemaphoreType.DMA((n, nf)), pltpu.SemaphoreType.DMA((n, nf)),
                        pltpu.SemaphoreType.DMA((n,))],
        compiler_params=pltpu.CompilerParams(has_side_effects=True))(*arrs)


_CHIP_FLIPS = ((1, 0, 0), (0, 1, 0), (1, 1, 0))
_ALL_FLIPS = ((0, 0, 1), (0, 1, 0), (0, 1, 1), (1, 0, 0), (1, 0, 1), (1, 1, 0), (1, 1, 1))


def _whole(ref, pos):
    return ref.shape if pos is None else ref


def _chip_block(ref, pos):
    return ref.shape[1:] if pos is None else ref.at[2 * pos[0] + pos[1]]


def _chip_slot(p):
    return 2 * p[0] + p[1]


def gather_chips(name, arrs):
    return flip_exchange(name, arrs, _CHIP_FLIPS, 4, _chip_slot, _whole, True)


def scatter_chips(name, arrs):
    return flip_exchange(name, arrs, _CHIP_FLIPS, 4, _chip_slot, _chip_block, True)


_HBM = pl.BlockSpec(memory_space=pltpu.HBM)
_SEM = pl.BlockSpec(memory_space=pltpu.SEMAPHORE)
_DATAFLOW = pltpu.SideEffectType.DATAFLOW_SIDE_EFFECTING


def _chip_copies(srcs, lands, send, recv, src_of):
    me = (lax.axis_index("x"), lax.axis_index("y"), lax.axis_index("c"))
    copies = []
    for k, (src, land) in enumerate(zip(srcs, lands)):
        for j, fl in enumerate(_CHIP_FLIPS):
            peer = tuple(1 - m if f else m for m, f in zip(me, fl))
            q = k * len(_CHIP_FLIPS) + j
            copies.append(pltpu.make_async_remote_copy(
                src_ref=src_of(src, peer), dst_ref=land.at[_chip_slot(me)], send_sem=send.at[q],
                recv_sem=recv.at[q], device_id=peer, device_id_type=MESH))
    return copies


def chips_start(name, arrs, src_of):
    n = len(arrs)
    me = _chip_slot((lax.axis_index("x"), lax.axis_index("y")))
    lands = []
    for a in arrs:
        blk = tuple(src_of(jax.ShapeDtypeStruct(a.shape, a.dtype), None))
        own = a if src_of is _whole else lax.dynamic_index_in_dim(a, me, 0, keepdims=False)
        land = lax.dynamic_update_index_in_dim(lax.empty((4,) + blk, a.dtype), own, me, 0)
        lands.append(pltpu.with_memory_space_constraint(land, pltpu.HBM))
    srcs = [pltpu.with_memory_space_constraint(a, pltpu.HBM) for a in arrs]

    def body(*refs):
        send, recv = refs[2 * n], refs[2 * n + 1]
        for cp in _chip_copies(refs[:n], refs[n:2 * n], send, recv, src_of):
            cp.start()
        refs[-1][...] = jnp.zeros_like(refs[-1])

    sems = pltpu.SemaphoreType.DMA((n * len(_CHIP_FLIPS),))
    outs = pl.pallas_call(
        body, name=name, in_specs=[_HBM] * (2 * n),
        out_shape=(sems, sems, *[pltpu.HBM(a.shape, a.dtype) for a in srcs + lands],
                   jax.ShapeDtypeStruct((8, LANES), f32)),
        out_specs=(_SEM, _SEM, *[_HBM] * (2 * n), pl.BlockSpec(memory_space=pltpu.VMEM)),
        input_output_aliases={i: 2 + i for i in range(2 * n)},
        compiler_params=pltpu.CompilerParams(has_side_effects=_DATAFLOW))(*srcs, *lands)
    return (outs[0], outs[1], list(outs[2:2 + n]), list(outs[2 + n:2 + 2 * n]), src_of), outs[-1][0, 0]


def chips_wait(name, state, after):
    send, recv, srcs, lands, src_of = state
    n = len(srcs)

    def body(*refs):
        send, recv = refs[2 * n], refs[2 * n + 1]
        for cp in _chip_copies(refs[:n], refs[n:2 * n], send, recv, src_of):
            cp.wait_send()
            cp.wait_recv()

    outs = pl.pallas_call(
        body, name=name, in_specs=[_HBM] * (2 * n) + [_SEM, _SEM, pl.BlockSpec(memory_space=pl.ANY)],
        out_shape=tuple(pltpu.HBM(a.shape, a.dtype) for a in srcs + lands), out_specs=tuple([_HBM] * (2 * n)),
        input_output_aliases={i: i for i in range(2 * n)},
        compiler_params=pltpu.CompilerParams(has_side_effects=_DATAFLOW))(*srcs, *lands, send, recv, after)
    return list(outs[n:])


def swap_sibling(name, arrs):
    outs = flip_exchange(name, arrs, ((0, 0, 1),), 1, lambda p: 0, _whole, False)
    return [o[0] for o in outs]


def gather_all(arrs):
    return flip_exchange("gather_all", arrs, _ALL_FLIPS, 8, lambda p: 4 * p[0] + 2 * p[1] + p[2], _whole, True)


def _row_tile(nr, nc, n_arrays):
    budget = (20 << 20) // (n_arrays * 2 * 4 * max(nc, LANES))
    t = min(nr, budget) // 16 * 16
    while t > 0 and nr % t:
        t -= 16
    return t if t > 0 else nr


def sum_slots(name, x):
    ns, nr, nc = x.shape
    tile = _row_tile(nr, nc, ns + 1)

    def body(x_ref, o_ref):
        s = x_ref[0].astype(f32)
        for q in range(1, ns):
            s = s + x_ref[q].astype(f32)
        o_ref[...] = s

    return pl.pallas_call(
        body, name=name, grid=(nr // tile,), in_specs=[pl.BlockSpec((ns, tile, nc), lambda i: (0, i, 0))],
        out_specs=pl.BlockSpec((tile, nc), lambda i: (i, 0)), out_shape=jax.ShapeDtypeStruct((nr, nc), f32),
        compiler_params=_params(1))(x)


def adamw(name, w, g_parts, m, v):
    nr, nc = w.shape
    n_g = len(g_parts)
    tile = _row_tile(nr, nc, 7 + n_g)

    def body(*refs):
        w_ref, m_ref, v_ref = refs[:3]
        g = refs[3][...]
        for r in refs[4:3 + n_g]:
            g = g + r[...]
        g_ref, d_ref, nm_ref, nv_ref = refs[3 + n_g:]
        nm = ADAM_B1 * m_ref[...] + (1.0 - ADAM_B1) * g
        nv = ADAM_B2 * v_ref[...] + (1.0 - ADAM_B2) * jnp.square(g)
        m_hat = nm / (1.0 - ADAM_B1 ** ADAM_STEP)
        v_hat = nv / (1.0 - ADAM_B2 ** ADAM_STEP)
        g_ref[...] = g
        d_ref[...] = -ADAM_LR * (m_hat / (jnp.sqrt(v_hat) + ADAM_EPS) + ADAM_WD * w_ref[...])
        nm_ref[...] = nm
        nv_ref[...] = nv

    spec = pl.BlockSpec((tile, nc), lambda i: (i, 0))
    return pl.pallas_call(
        body, name=name, grid=(nr // tile,), in_specs=[spec] * (3 + n_g), out_specs=[spec] * 4,
        out_shape=[jax.ShapeDtypeStruct((nr, nc), f32)] * 4, compiler_params=_params(1))(w, m, v, *g_parts)


def adamw_packed(w, g8, m, v):
    nr, nc = w.shape

    def body(w_ref, g_ref, m_ref, v_ref, go_ref, d_ref, nm_ref, nv_ref):
        g = g_ref[0]
        for q in range(1, 8):
            g = g + g_ref[q]
        nm = ADAM_B1 * m_ref[...] + (1.0 - ADAM_B1) * g
        nv = ADAM_B2 * v_ref[...] + (1.0 - ADAM_B2) * jnp.square(g)
        m_hat = nm / (1.0 - ADAM_B1 ** ADAM_STEP)
        v_hat = nv / (1.0 - ADAM_B2 ** ADAM_STEP)
        go_ref[...] = g
        d_ref[...] = -ADAM_LR * (m_hat / (jnp.sqrt(v_hat) + ADAM_EPS) + ADAM_WD * w_ref[...])
        nm_ref[...] = nm
        nv_ref[...] = nv

    return pl.pallas_call(body, name="adamw_packed", out_shape=[jax.ShapeDtypeStruct((nr, nc), f32)] * 4,
                          compiler_params=pltpu.CompilerParams(vmem_limit_bytes=VMEM_LIMIT))(w, g8, m, v)


def _pack(vectors):
    rows = []
    for a in vectors:
        flat = a.reshape(-1).astype(f32)
        pad = (-flat.shape[0]) % LANES
        rows.append(jnp.pad(flat, (0, pad)).reshape(-1, LANES))
    packed = jnp.concatenate(rows, axis=0)
    return jnp.pad(packed, ((0, (-packed.shape[0]) % 8), (0, 0)))


def _unpack(packed, like):
    out, r = [], 0
    for a in like:
        n = a.size
        nr = -(-n // LANES)
        out.append(packed[r:r + nr].reshape(-1)[:n].reshape(a.shape))
        r += nr
    return out


def kernel(x, mem, mix_norm_w, w_in, dn_conv_w, dn_a_log, dn_dt_bias, dn_norm_w, rw_mu, rw_w0, rw_w2, rw_a0, rw_a2, rw_g2, rw_k_k, rw_k_a, rw_r_k, rw_ln_w, rw_ln_b, w_out, xa_norm_w, mem_norm_w, xa_wq, xa_wk, xa_wv, xa_wo, ffn_norm_w, ffn_w1, ffn_w2, final_norm_w, loss_target, m_mix_norm_w, m_w_in, m_dn_conv_w, m_dn_a_log, m_dn_dt_bias, m_dn_norm_w, m_rw_mu, m_rw_w0, m_rw_w2, m_rw_a0, m_rw_a2, m_rw_g2, m_rw_k_k, m_rw_k_a, m_rw_r_k, m_rw_ln_w, m_rw_ln_b, m_w_out, m_xa_norm_w, m_mem_norm_w, m_xa_wq, m_xa_wk, m_xa_wv, m_xa_wo, m_ffn_norm_w, m_ffn_w1, m_ffn_w2, m_final_norm_w, v_mix_norm_w, v_w_in, v_dn_conv_w, v_dn_a_log, v_dn_dt_bias, v_dn_norm_w, v_rw_mu, v_rw_w0, v_rw_w2, v_rw_a0, v_rw_a2, v_rw_g2, v_rw_k_k, v_rw_k_a, v_rw_r_k, v_rw_ln_w, v_rw_ln_b, v_w_out, v_xa_norm_w, v_mem_norm_w, v_xa_wq, v_xa_wk, v_xa_wv, v_xa_wo, v_ffn_norm_w, v_ffn_w1, v_ffn_w2, v_final_norm_w):
    weights = dict(mix_norm_w=mix_norm_w, w_in=w_in, dn_conv_w=dn_conv_w, dn_a_log=dn_a_log, dn_dt_bias=dn_dt_bias, dn_norm_w=dn_norm_w, rw_mu=rw_mu, rw_w0=rw_w0, rw_w2=rw_w2, rw_a0=rw_a0, rw_a2=rw_a2, rw_g2=rw_g2, rw_k_k=rw_k_k, rw_k_a=rw_k_a, rw_r_k=rw_r_k, rw_ln_w=rw_ln_w, rw_ln_b=rw_ln_b, w_out=w_out, xa_norm_w=xa_norm_w, mem_norm_w=mem_norm_w, xa_wq=xa_wq, xa_wk=xa_wk, xa_wv=xa_wv, xa_wo=xa_wo, ffn_norm_w=ffn_norm_w, ffn_w1=ffn_w1, ffn_w2=ffn_w2, final_norm_w=final_norm_w)
    mom_m = dict(mix_norm_w=m_mix_norm_w, w_in=m_w_in, dn_conv_w=m_dn_conv_w, dn_a_log=m_dn_a_log, dn_dt_bias=m_dn_dt_bias, dn_norm_w=m_dn_norm_w, rw_mu=m_rw_mu, rw_w0=m_rw_w0, rw_w2=m_rw_w2, rw_a0=m_rw_a0, rw_a2=m_rw_a2, rw_g2=m_rw_g2, rw_k_k=m_rw_k_k, rw_k_a=m_rw_k_a, rw_r_k=m_rw_r_k, rw_ln_w=m_rw_ln_w, rw_ln_b=m_rw_ln_b, w_out=m_w_out, xa_norm_w=m_xa_norm_w, mem_norm_w=m_mem_norm_w, xa_wq=m_xa_wq, xa_wk=m_xa_wk, xa_wv=m_xa_wv, xa_wo=m_xa_wo, ffn_norm_w=m_ffn_norm_w, ffn_w1=m_ffn_w1, ffn_w2=m_ffn_w2, final_norm_w=m_final_norm_w)
    mom_v = dict(mix_norm_w=v_mix_norm_w, w_in=v_w_in, dn_conv_w=v_dn_conv_w, dn_a_log=v_dn_a_log, dn_dt_bias=v_dn_dt_bias, dn_norm_w=v_dn_norm_w, rw_mu=v_rw_mu, rw_w0=v_rw_w0, rw_w2=v_rw_w2, rw_a0=v_rw_a0, rw_a2=v_rw_a2, rw_g2=v_rw_g2, rw_k_k=v_rw_k_k, rw_k_a=v_rw_k_a, rw_r_k=v_rw_r_k, rw_ln_w=v_rw_ln_w, rw_ln_b=v_rw_ln_b, w_out=v_w_out, xa_norm_w=v_xa_norm_w, mem_norm_w=v_mem_norm_w, xa_wq=v_xa_wq, xa_wk=v_xa_wk, xa_wv=v_xa_wv, xa_wo=v_xa_wo, ffn_norm_w=v_ffn_norm_w, ffn_w1=v_ffn_w1, ffn_w2=v_ffn_w2, final_norm_w=v_final_norm_w)
    names = list(weights)

    seq, d = x.shape[1], x.shape[2]
    dnw = d // 2
    rww = d - dnw
    nh, nb = dnw // LANES, rww // LANES
    n_mem = mem.shape[1]
    lw_dim, la_dim, lg_dim = rw_w2.shape[1], rw_a2.shape[1], rw_g2.shape[1]
    assert lw_dim + la_dim == LANES and lg_dim == LANES and dnw % LANES == 0 and rww % LANES == 0
    xs, mems, tgt = x[0], mem[0], loss_target[0]

    col_sharded = ("w_in", "xa_wo", "ffn_w1", "dn_conv_w", "rw_w2", "rw_a2", "rw_g2")
    row_sharded = ("w_out", "xa_wq", "xa_wk", "xa_wv", "ffn_w2")
    f32_payload = ("dn_conv_w", "rw_w2", "rw_a2", "rw_g2")
    sharded = col_sharded + row_sharded
    payload = {n: weights[n][0].astype(f32 if n in f32_payload else bf16) for n in sharded}
    first = ("w_in", "dn_conv_w", "rw_w2", "rw_a2", "rw_g2")
    mid = ("w_out", "xa_wq", "xa_wk", "xa_wv", "xa_wo")
    late = ("ffn_w1", "ffn_w2")
    gathered = dict(zip(first, gather_chips("gather_first", [payload[n] for n in first])))
    ordered = lax.optimization_barrier(([gathered[n] for n in first], [payload[n] for n in mid + late]))
    gathered = dict(zip(first, ordered[0]))
    payload.update(zip(mid + late, ordered[1]))
    mid_state, tok_mid = chips_start("gather_mid_start", [payload[n] for n in mid], _whole)
    late_state, tok_late = chips_start("gather_late_start", [payload[n] for n in late], _whole)
    mix_norm_w_in = mix_norm_w + (tok_mid + tok_late)

    def full(n):
        g = gathered[n]
        if n in col_sharded:
            return g.transpose(1, 0, 2).reshape(g.shape[1], 4 * g.shape[2])
        return g.reshape(4 * g.shape[1], g.shape[2])

    w_in_f = full("w_in")
    c_dn = 4 * dnw
    c_rw0 = c_dn + 2 * nh
    c_rw = 3 * rww
    w_main = jnp.concatenate([w_in_f[:, :c_dn], w_in_f[:, c_rw0:c_rw0 + c_rw]], axis=1)
    w_small = jnp.concatenate([w_in_f[:, c_dn:c_rw0], jnp.zeros((d, LANES - 2 * nh), bf16),
                               w_in_f[:, c_rw0 + c_rw:]], axis=1)
    conv_w = full("dn_conv_w")
    w2p = jnp.concatenate([full("rw_w2"), jnp.zeros((la_dim, rww), f32)], axis=0)
    a2p = jnp.concatenate([jnp.zeros((lw_dim, rww), f32), full("rw_a2")], axis=0)
    g2 = full("rw_g2")
    xaw = xa_wq.shape[2]
    nxh = xaw // LANES
    ffn = 4 * ffn_w1.shape[2]

    def lane_row(vec):
        return jnp.pad(vec.reshape(1, -1), ((0, 0), (0, LANES - vec.size)))

    alog_row, dtb_row = lane_row(dn_a_log), lane_row(dn_dt_bias)
    head_of_col = jnp.arange(dnw)[None, :] // LANES
    e_g = (jnp.arange(LANES)[:, None] == head_of_col).astype(f32)
    e_b = (jnp.arange(LANES)[:, None] == head_of_col + nh).astype(f32)
    mu_main, mu_small = rw_mu[:, :c_rw], rw_mu[:, c_rw:]
    r_k_row = rw_r_k.reshape(1, rww)
    fnw = final_norm_w.reshape(1, d)
    qb, kb_, vb, zb = 0, nh, 2 * nh, 3 * nh
    rb0 = 4 * nh
    cc = lambda w_, o_: ("constc", w_, o_)
    rc = lambda o_: ("rowc", LANES, o_)

    (u16,) = rowcall("norm_mix", fn_rms, [(xs, "row"), (mix_norm_w_in, "const")], [(seq, d, bf16, "row")], rows=seq)
    p_main = mm("in_main", u16, w_main)
    p_small = mm("in_small", u16, w_small)

    fn_gconv = make_fn_gconv(2 * nh)
    (qkv,) = rowcall("gdn_conv", fn_gconv, [(p_main, rc(0)), (conv_w, cc(LANES, 0))],
                     [(seq, 3 * dnw, f32, "rowc")], rows=seq, tile=seq, ncol=3 * nh)
    gate_ins = [(p_small, rc(0)), (alog_row, "const"), (dtb_row, "const"), (e_g, "const"), (e_b, "const")]
    g_b, beta_b = rowcall("gdn_gate", fn_ggate, gate_ins, [(seq, dnw, f32, "row")] * 2, rows=seq)
    gdn_ins = [(qkv, qb), (qkv, kb_), (qkv, vb), (g_b, 0), (beta_b, 0)]
    o_raw, gdn_states = scan_fwd("gdn_scan", gdn_chunk, gdn_ins, rows=seq, chunk=GDN_CHUNK, ncol=nh)
    gpost_ins = [(o_raw, rc(0)), (p_main, rc(zb)), (dn_norm_w, "const")]
    (o_dn,) = rowcall("gdn_post", fn_gpost, gpost_ins, [(seq, dnw, bf16, "rowc")], rows=seq, ncol=nh)

    (prw,) = rowcall("rw_lerp_main", fn_lerp, [(p_main, rc(rb0)), (mu_main, cc(LANES, 0))],
                     [(seq, c_rw, f32, "rowc")], rows=seq, tile=seq, ncol=3 * nb)
    (psl,) = rowcall("rw_lerp_small", fn_lerp, [(p_small, rc(1)), (mu_small, cc(LANES, 0))],
                     [(seq, 2 * LANES, f32, "rowc")], rows=seq, tile=seq, ncol=2)
    rprep_ins = [(prw, rc(nb)), (psl, "row"), (rw_w0, cc(LANES, 0)), (rw_a0, cc(LANES, 0)), (rw_k_k, cc(LANES, 0)),
                 (rw_k_a, cc(LANES, 0)), (w2p, cc(LANES, 0)), (a2p, cc(LANES, 0)), (g2, cc(LANES, 0))]
    lw, kmod, kk, a_rw, gate = rowcall("rw_prep", fn_rprep, rprep_ins, [(seq, rww, f32, "rowc")] * 5,
                                        rows=seq, ncol=nb)
    wkv_ins = [(prw, 0), (lw, 0), (kmod, 0), (prw, 2 * nb), (kk, 0), (a_rw, 0)]
    y_rw, wkv_states = scan_fwd("wkv_scan", wkv_chunk, wkv_ins, rows=seq, chunk=WKV_CHUNK, ncol=nb)
    rpost_ins = [(y_rw, rc(0)), (prw, rc(0)), (kmod, rc(0)), (prw, rc(2 * nb)), (gate, rc(0)),
                 (r_k_row, cc(LANES, 0)), (rw_ln_w, cc(LANES, 0)), (rw_ln_b, cc(LANES, 0))]
    (o_rw,) = rowcall("rw_post", fn_rpost, rpost_ins, [(seq, rww, bf16, "rowc")], rows=seq, ncol=nb)

    o_cat = jnp.concatenate([o_dn, o_rw], axis=1)
    gathered.update(zip(mid, chips_wait("gather_mid_wait", mid_state, o_cat)))
    w_out_f, wq_f, wk_f, wv_f, wo_f = full("w_out"), full("xa_wq"), full("xa_wk"), full("xa_wv"), full("xa_wo")
    h1 = mm("out_proj", o_cat, w_out_f, add=xs)

    (hn16,) = rowcall("norm_xa", fn_rms, [(h1, "row"), (xa_norm_w, "const")], [(seq, d, bf16, "row")], rows=seq)
    (mn16,) = rowcall("norm_mem", fn_rms, [(mems, "row"), (mem_norm_w, "const")], [(n_mem, d, bf16, "row")],
                      rows=n_mem)
    q_xa = mm("xa_q", hn16, wq_f)
    k_xa = mm("xa_k", mn16, wk_f)
    v_xa = mm("xa_v", mn16, wv_f)
    xcore_ins = [(q_xa, rc(0)), (k_xa, cc(LANES, 0)), (v_xa, cc(LANES, 0))]
    (o_xa,) = rowcall("xa_core", fn_xcore, xcore_ins, [(seq, xaw, bf16, "rowc")], rows=seq, ncol=nxh)
    h2 = mm("xa_o", o_xa, wo_f, add=h1)

    (fn16,) = rowcall("norm_ffn", fn_rms, [(h2, "row"), (ffn_norm_w, "const")], [(seq, d, bf16, "row")], rows=seq)
    gathered.update(zip(late, chips_wait("gather_late_wait", late_state, fn16)))
    w1_f, w2_f = full("ffn_w1"), full("ffn_w2")
    a1 = mm("ffn_up", fn16, w1_f)
    nfb = ffn // (4 * LANES)
    (hid16,) = rowcall("ffn_act", fn_act, [(a1, ("rowc", 4 * LANES, 0))], [(seq, ffn, bf16, "rowc")], rows=seq,
                       ncol=nfb)
    h3 = mm("ffn_down", hid16, w2_f, add=h2)

    dh3, dh3_16, d_fnw, loss_rows = rowcall(
        "loss_head", fn_final, [(h3, "row"), (tgt, "row"), (fnw, "const")],
        [(seq, d, f32, "row"), (seq, d, bf16, "row"), (1, d, f32, "acc"), (8, LANES, f32, "acc")], rows=seq)

    dhid = mm("ffn_down_dx", dh3_16, w2_f, tb=True)
    (da1_16,) = rowcall("ffn_act_bwd", fn_actb, [(a1, ("rowc", 4 * LANES, 0)), (dhid, ("rowc", 4 * LANES, 0))],
                        [(seq, ffn, bf16, "rowc")], rows=seq, ncol=nfb)
    g_ffn_w2 = mm("ffn_down_dw", hid16, dh3_16, ta=True, out_dtype=bf16)
    g_ffn_w1 = mm("ffn_up_dw", fn16, da1_16, ta=True, out_dtype=bf16)
    dfn = mm("ffn_up_dx", da1_16, w1_f, tb=True)

    def by_chip(n, g):
        if n in col_sharded:
            return g.reshape(g.shape[0], 4, g.shape[1] // 4).transpose(1, 0, 2)
        return g.reshape(4, g.shape[0] // 4, g.shape[1])

    late_g, tok = chips_start("scatter_late_start", [by_chip("ffn_w1", g_ffn_w1), by_chip("ffn_w2", g_ffn_w2)],
                              _chip_block)
    dh2, d_ffn_nw, dh2_16 = rowvjp("norm_ffn_bwd", fn_rms, [(h2, "row"), (ffn_norm_w + tok, "const")],
                                   [[(dfn, "row")]], [0, 1], rows=seq, adds=[(0, dh3, "row")], dup16=[0])

    do_xa = mm("xa_o_dx", dh2_16, wo_f, tb=True)
    g_xa_wo = mm("xa_o_dw", o_xa, dh2_16, ta=True, out_dtype=bf16)
    dq_xa, dk_xa, dv_xa, dq16 = rowvjp("xa_core_bwd", fn_xcore, xcore_ins, [[(do_xa, rc(0))]], [0, 1, 2],
                                       rows=seq, ncol=nxh, dup16=[0])
    g_xa_wq = mm("xa_q_dw", hn16, dq16, ta=True, out_dtype=bf16)
    dhn = mm("xa_q_dx", dq16, wq_f, tb=True)
    dh1, d_xa_nw, dh1_16 = rowvjp("norm_xa_bwd", fn_rms, [(h1, "row"), (xa_norm_w, "const")], [[(dhn, "row")]],
                                  [0, 1], rows=seq, adds=[(0, dh2, "row")], dup16=[0])
    dk16, dv16 = dk_xa.astype(bf16), dv_xa.astype(bf16)
    g_xa_wk = mm("xa_k_dw", mn16, dk16, ta=True, out_dtype=bf16)
    g_xa_wv = mm("xa_v_dw", mn16, dv16, ta=True, out_dtype=bf16)
    dmn = mm("xa_v_dx", dv16, wv_f, tb=True, add=mm("xa_k_dx", dk16, wk_f, tb=True))
    (d_mem_nw,) = rowvjp("norm_mem_bwd", fn_rms, [(mems, "row"), (mem_norm_w, "const")], [[(dmn, "row")]], [1],
                         rows=n_mem)

    g_w_out = mm("out_proj_dw", o_cat, dh1_16, ta=True, out_dtype=bf16)
    do_cat = mm("out_proj_dx", dh1_16, w_out_f, tb=True)
    mid_grads = dict(w_out=g_w_out, xa_wq=g_xa_wq, xa_wk=g_xa_wk, xa_wv=g_xa_wv, xa_wo=g_xa_wo)
    mid_g, tok = chips_start("scatter_mid_start", [by_chip(n, mid_grads[n]) for n in mid], _chip_block)
    rpost_ins_b = rpost_ins[:5] + [(r_k_row + tok, cc(LANES, 0))] + rpost_ins[6:]

    dy, dr_a, dkmod_a, dv_a, dgate, d_r_k, d_ln_w, d_ln_b = rowvjp(
        "rw_post_bwd", fn_rpost, rpost_ins_b, [[(do_cat, rc(nh))]], [0, 1, 2, 3, 4, 5, 6, 7], rows=seq, ncol=nb)
    dr_b, dlw, dkmod_b, dv_b, dkk, da_rw = scan_bwd("wkv_scan_bwd", wkv_chunk, wkv_ins, wkv_states, dy, 0,
                                                    rows=seq, chunk=WKV_CHUNK, ncol=nb)
    rprep_cts = [[(dlw, rc(0))], [(dkmod_a, rc(0)), (dkmod_b, rc(0))], [(dkk, rc(0))], [(da_rw, rc(0))],
                 [(dgate, rc(0))]]
    dpk, dpsl_parts, d_w0, d_a0, d_k_k, d_k_a, d_w2p, d_a2p, d_g2 = rowvjp(
        "rw_prep_bwd", fn_rprep, rprep_ins, rprep_cts, [0, 1, 2, 3, 4, 5, 6, 7, 8], rows=seq, ncol=nb)
    (dpsl,) = rowcall("rw_prep_sum", fn_sumcols(nb), [(dpsl_parts, "row")], [(seq, 2 * LANES, f32, "row")], rows=seq)

    def lerp_bwd(tag, p, p_off, mu, mu_off, ct_lists, ncol):
        return rowvjp("rw_lerp_bwd_" + tag, fn_lerp, [(p, rc(p_off)), (mu, cc(LANES, mu_off))], [ct_lists], [0, 1],
                      rows=seq, tile=seq, ncol=ncol, dup16=[0])

    _, dmu_r, dpr16 = lerp_bwd("r", p_main, rb0, mu_main, 0, [(dr_a, rc(0)), (dr_b, rc(0))], nb)
    _, dmu_k, dpk16 = lerp_bwd("k", p_main, rb0 + nb, mu_main, nb, [(dpk, rc(0))], nb)
    _, dmu_v, dpv16 = lerp_bwd("v", p_main, rb0 + 2 * nb, mu_main, 2 * nb, [(dv_a, rc(0)), (dv_b, rc(0))], nb)
    _, dmu_s, dps12_16 = lerp_bwd("small", p_small, 1, mu_small, 0, [(dpsl, rc(0))], 2)

    do_raw, dz, d_dn_nw, dz16 = rowvjp("gdn_post_bwd", fn_gpost, gpost_ins, [[(do_cat, rc(0))]], [0, 1, 2],
                                       rows=seq, ncol=nh, dup16=[1])
    dq_g, dk_g, dv_g, dg_b, dbeta_b = scan_bwd("gdn_scan_bwd", gdn_chunk, gdn_ins, gdn_states, do_raw, 0,
                                               rows=seq, chunk=GDN_CHUNK, ncol=nh)
    dps0, d_alog, d_dtb, dps0_16 = rowvjp("gdn_gate_bwd", fn_ggate, gate_ins, [[(dg_b, "row")], [(dbeta_b, "row")]],
                                          [0, 1, 2], rows=seq, dup16=[0])
    dqkv = jnp.concatenate([dq_g, dk_g, dv_g], axis=1)
    _, d_conv_w, dqkv16 = rowvjp("gdn_conv_bwd", fn_gconv, [(p_main, rc(0)), (conv_w, cc(LANES, 0))],
                                 [[(dqkv, rc(0))]], [0, 1], rows=seq, tile=seq, ncol=3 * nh, dup16=[0])

    dp_main16 = jnp.concatenate([dqkv16, dz16, dpr16, dpk16, dpv16], axis=1)
    dp_small16 = jnp.concatenate([dps0_16, dps12_16], axis=1)
    g_w_main = mm("in_main_dw", u16, dp_main16, ta=True, out_dtype=bf16)
    g_w_small = mm("in_small_dw", u16, dp_small16, ta=True, out_dtype=bf16)
    g_w_in = jnp.concatenate([g_w_main[:, :c_dn], g_w_small[:, :2 * nh], g_w_main[:, c_dn:],
                              g_w_small[:, LANES:]], axis=1)
    first_grads = dict(w_in=g_w_in, dn_conv_w=d_conv_w, rw_w2=d_w2p[:lw_dim], rw_a2=d_a2p[lw_dim:], rw_g2=d_g2)
    first_g, tok = chips_start("scatter_first_start", [by_chip(n, first_grads[n]) for n in first], _chip_block)

    du = mm("in_small_dx", dp_small16, w_small, tb=True, add=mm("in_main_dx", dp_main16, w_main, tb=True))
    grad_x, d_mix_nw = rowvjp("norm_mix_bwd", fn_rms, [(xs, "row"), (mix_norm_w + tok, "const")], [[(du, "row")]],
                              [0, 1], rows=seq, adds=[(0, dh1, "row")])
    received = dict(zip(mid, chips_wait("scatter_mid_wait", mid_g, grad_x)))
    received.update(zip(late, chips_wait("scatter_late_wait", late_g, grad_x)))

    out = {}

    def reduce_and_update(tag, group):
        partial = [sum_slots("sum_chips_" + n, received[n]) for n in group]
        other = swap_sibling("swap_sibling_" + tag, partial)
        for n, p_mine, p_other in zip(group, partial, other):
            res = adamw("adamw_" + n, weights[n][0], [p_mine, p_other], mom_m[n][0], mom_v[n][0])
            out[n] = [r.reshape(weights[n].shape) for r in res]

    reduce_and_update("rest", mid + late)
    received.update(zip(first, chips_wait("scatter_first_wait", first_g, out["ffn_w2"][1])))
    reduce_and_update("first", first)

    small_names = [n for n in names if n not in sharded]
    small_local = dict(
        mix_norm_w=d_mix_nw, dn_a_log=d_alog[:, :nh], dn_dt_bias=d_dtb[:, :nh], dn_norm_w=d_dn_nw,
        rw_mu=jnp.concatenate([dmu_r, dmu_k, dmu_v, dmu_s], axis=1), rw_w0=d_w0, rw_a0=d_a0, rw_k_k=d_k_k,
        rw_k_a=d_k_a, rw_r_k=d_r_k, rw_ln_w=d_ln_w, rw_ln_b=d_ln_b, xa_norm_w=d_xa_nw, mem_norm_w=d_mem_nw,
        ffn_norm_w=d_ffn_nw, final_norm_w=d_fnw)
    loss_vec = jnp.where(jnp.arange(LANES) == 0, loss_rows[0], 0.0)
    (g8,) = gather_all([_pack([small_local[n] for n in small_names] + [loss_vec])])

    packed_like = [weights[n] for n in small_names] + [loss_vec]
    zero = jnp.zeros((LANES,), f32)
    res = adamw_packed(_pack([weights[n] for n in small_names] + [zero]), g8,
                       _pack([mom_m[n] for n in small_names] + [zero]),
                       _pack([mom_v[n] for n in small_names] + [zero]))
    unpacked = [_unpack(r, packed_like) for r in res]
    for i, n in enumerate(small_names):
        out[n] = [u[i] for u in unpacked]
    loss = unpacked[0][-1][0]

    return (loss, grad_x.reshape(x.shape), *[out[n][0] for n in names], *[out[n][1] for n in names],
            *[out[n][2] for n in names], *[out[n][3] for n in names])
```

```python
import functools

import jax
import jax.numpy as jnp
from jax import lax
from jax.experimental import pallas as pl
from jax.experimental.pallas import tpu as pltpu

f32 = jnp.float32
bf16 = jnp.bfloat16
HI = lax.Precision.HIGHEST
MESH = pl.DeviceIdType.MESH

LANES = 128
VMEM_LIMIT = 56 << 20
TOK_TILE = 256
MM_TILE = 1024
MM_TILE_K = 2048
GDN_CHUNK = 128
WKV_CHUNK = 64
SCAN_GROUP = 8
P_BULK = 1
P_INV = 3
RMS_EPS = 1e-6
RW_GN_EPS = 64e-5
RW_HEAD = 64

ADAM_LR, ADAM_B1, ADAM_B2, ADAM_EPS, ADAM_WD, ADAM_STEP = 0.001, 0.9, 0.999, 1e-08, 0.01, 10


def _params(n_grid):
    return pltpu.CompilerParams(dimension_semantics=("arbitrary",) * n_grid, vmem_limit_bytes=VMEM_LIMIT)


_DIMS = {"nn": (((1,), (0,)), ((), ())), "nt": (((1,), (1,)), ((), ())), "tn": (((0,), (0,)), ((), ()))}
_DIMS_BATCHED = {"nn": (((2,), (1,)), ((0,), (0,))), "nt": (((2,), (2,)), ((0,), (0,))),
                 "tn": (((1,), (1,)), ((0,), (0,)))}


def _raw_dot(a, b, mode, passes):
    dims = (_DIMS if a.ndim == 2 else _DIMS_BATCHED)[mode]
    if passes == 6:
        return lax.dot_general(a.astype(f32), b.astype(f32), dims, precision=HI, preferred_element_type=f32)
    ah, bh = a.astype(bf16), b.astype(bf16)
    r = lax.dot_general(ah, bh, dims, preferred_element_type=f32)
    if passes == 3:
        al = (a - ah.astype(f32)).astype(bf16)
        bl = (b - bh.astype(f32)).astype(bf16)
        r = r + lax.dot_general(al, bh, dims, preferred_element_type=f32)
        r = r + lax.dot_general(ah, bl, dims, preferred_element_type=f32)
    return r


@functools.partial(jax.custom_vjp, nondiff_argnums=(2, 3))
def pdot(a, b, mode, passes):
    return _raw_dot(a, b, mode, passes)


def _pdot_bwd(mode, passes, res, g):
    a, b = res
    if mode == "nn":
        da, db = _raw_dot(g, b, "nt", passes), _raw_dot(a, g, "tn", passes)
    elif mode == "nt":
        da, db = _raw_dot(g, b, "nn", passes), _raw_dot(g, a, "tn", passes)
    else:
        da, db = _raw_dot(b, g, "nt", passes), _raw_dot(a, g, "nn", passes)
    return da.astype(a.dtype), db.astype(b.dtype)


pdot.defvjp(lambda a, b, mode, passes: (_raw_dot(a, b, mode, passes), (a, b)), _pdot_bwd)


def hdot(a, b):
    return pdot(a, b, "nn", 6)


def bdot(a, b):
    return pdot(a, b, "nn", 1)


def bdot_nt(a, b):
    return pdot(a, b, "nt", 1)


def _shift_rows(x, k):
    row = lax.broadcasted_iota(jnp.int32, x.shape, 0)
    return jnp.where(row < k, 0.0, pltpu.roll(x, k, axis=0))


def _unshift_rows(g, k):
    n = g.shape[0]
    row = lax.broadcasted_iota(jnp.int32, g.shape, 0)
    return jnp.where(row >= n - k, 0.0, pltpu.roll(g, n - k, axis=0))


@functools.partial(jax.custom_vjp, nondiff_argnums=(1,))
def tshift(x, k):
    return _shift_rows(x, k)


tshift.defvjp(lambda x, k: (_shift_rows(x, k), None), lambda k, _, g: (_unshift_rows(g, k),))


def rms(x, w):
    x = x.astype(f32)
    return x * lax.rsqrt(jnp.mean(x * x, axis=-1, keepdims=True) + RMS_EPS) * w


def softplus(x):
    return jnp.maximum(x, 0.0) + jnp.log(1.0 + jnp.exp(-jnp.abs(x)))


def seg2sum(x):
    lo = lax.broadcasted_iota(jnp.int32, x.shape, 1) < RW_HEAD
    s_lo = jnp.sum(jnp.where(lo, x, 0.0), axis=-1, keepdims=True)
    s_hi = jnp.sum(jnp.where(lo, 0.0, x), axis=-1, keepdims=True)
    return jnp.where(lo, s_lo, s_hi)


def _tile(n, pref):
    if n <= pref:
        return n
    t = pref
    while t >= LANES:
        if n % t == 0:
            return t
        t -= LANES
    return n


def mm(name, a, b, *, ta=False, tb=False, add=None, out_dtype=f32, by_chip=False, epilogue=None, extra=(),
       out_dtypes=None):
    (k, m) = a.shape if ta else a.shape[::-1]
    (n, kb) = b.shape if tb else b.shape[::-1]
    assert k == kb, (name, a.shape, b.shape)
    tm, tk = _tile(m, MM_TILE), _tile(k, MM_TILE_K)
    tn = _tile(n // 4, MM_TILE) if by_chip else _tile(n, MM_TILE)
    nk = k // tk
    dims = (((0,) if ta else (1,), (1,) if tb else (0,)), ((), ()))
    extra = list(extra) + ([] if add is None else [add])
    out_dtypes = [out_dtype] if out_dtypes is None else list(out_dtypes)
    n_extra, n_out = len(extra), len(out_dtypes)

    def body(*refs):
        a_ref, b_ref = refs[:2]
        x_refs = refs[2:2 + n_extra]
        o_refs = refs[2 + n_extra:2 + n_extra + n_out]
        part = lax.dot_general(a_ref[...].astype(bf16), b_ref[...].astype(bf16), dims, preferred_element_type=f32)

        def finish(r):
            xs = [x[...] for x in x_refs]
            if add is not None:
                r = r + xs.pop().astype(f32)
            outs = (r,) if epilogue is None else epilogue(r, *xs)
            for o_ref, o in zip(o_refs, outs):
                o_ref[...] = o.astype(o_ref.dtype)

        if nk == 1:
            finish(part)
            return
        acc = refs[-1]
        kk = pl.program_id(2)

        @pl.when(kk == 0)
        def _():
            acc[...] = part

        @pl.when(kk > 0)
        def _():
            acc[...] += part

        @pl.when(kk == nk - 1)
        def _():
            finish(acc[...])

    a_spec = pl.BlockSpec((tk, tm), lambda i, j, q: (q, i)) if ta else pl.BlockSpec((tm, tk), lambda i, j, q: (i, q))
    b_spec = pl.BlockSpec((tn, tk), lambda i, j, q: (j, q)) if tb else pl.BlockSpec((tk, tn), lambda i, j, q: (q, j))
    x_spec = pl.BlockSpec((tm, tn), lambda i, j, q: (i, j))
    if by_chip:
        per_chip = n // 4 // tn
        o_spec = pl.BlockSpec((None, tm, tn), lambda i, j, q: (j // per_chip, i, j % per_chip))
        o_shape = (4, m, n // 4)
    else:
        o_spec, o_shape = x_spec, (m, n)
    res = pl.pallas_call(
        body, name=name, grid=(m // tm, n // tn, nk), in_specs=[a_spec, b_spec] + [x_spec] * n_extra,
        out_specs=[o_spec] * n_out, out_shape=[jax.ShapeDtypeStruct(o_shape, dt) for dt in out_dtypes],
        scratch_shapes=[pltpu.VMEM((tm, tn), f32)] if nk > 1 else [], compiler_params=_params(3))(a, b, *extra)
    return res[0] if n_out == 1 else res


def _in_spec(a, kind, tile):
    if kind == "row":
        return pl.BlockSpec((tile, a.shape[1]), lambda j, i: (i, 0))
    if kind == "const":
        return pl.BlockSpec(a.shape, lambda j, i: (0, 0))
    tag, cw, off = kind
    if tag == "rowc":
        return pl.BlockSpec((tile, cw), lambda j, i: (i, j + off))
    assert tag == "constc", kind
    return pl.BlockSpec((a.shape[0], cw), lambda j, i: (0, j + off))


def rowcall(name, fn, ins, outs, *, rows, tile=None, ncol=1):
    tile = min(TOK_TILE, rows) if tile is None else tile
    n_in = len(ins)
    kinds = [o[3] for o in outs]

    def body(*refs):
        j, i = pl.program_id(0), pl.program_id(1)
        res = fn(*[r[...] for r in refs[:n_in]])
        for ref, val, kind in zip(refs[n_in:], res, kinds):
            if kind in ("row", "rowc"):
                ref[...] = val.astype(ref.dtype)
            else:
                first = (i == 0) if kind == "accc" else jnp.logical_and(i == 0, j == 0)

                @pl.when(first)
                def _(ref=ref, val=val):
                    ref[...] = val.astype(ref.dtype)

                @pl.when(jnp.logical_not(first))
                def _(ref=ref, val=val):
                    ref[...] += val.astype(ref.dtype)

    out_shape, out_specs = [], []
    for nr, nc, dtype, kind in outs:
        out_shape.append(jax.ShapeDtypeStruct((nr, nc), dtype))
        if kind == "row":
            out_specs.append(pl.BlockSpec((tile, nc), lambda j, i: (i, 0)))
        elif kind == "rowc":
            out_specs.append(pl.BlockSpec((tile, nc // ncol), lambda j, i: (i, j)))
        elif kind == "acc":
            out_specs.append(pl.BlockSpec((nr, nc), lambda j, i: (0, 0)))
        else:
            out_specs.append(pl.BlockSpec((nr, nc // ncol), lambda j, i: (0, j)))
    return pl.pallas_call(
        body, name=name, grid=(ncol, rows // tile), in_specs=[_in_spec(a, k, tile) for a, k in ins],
        out_specs=out_specs, out_shape=out_shape, compiler_params=_params(2))(*[a for a, _ in ins])


def rowvjp(name, fn, ins, cts, grads, *, rows, tile=None, ncol=1, adds=(), dup16=()):
    n_in = len(ins)
    ct_sizes = [len(c) for c in cts]
    flat_cts = [m for c in cts for m in c]
    n_ct = len(flat_cts)

    def wrapped(*vals):
        xs = list(vals[:n_in])
        gs = vals[n_in:n_in + n_ct]
        extra = vals[n_in + n_ct:]

        def f(*dvars):
            full = list(xs)
            for k, v in zip(grads, dvars):
                full[k] = v
            return fn(*full)

        outs, pull = jax.vjp(f, *[xs[k] for k in grads])
        cot, p = [], 0
        for o, size in zip(outs, ct_sizes):
            g = gs[p].astype(f32)
            for q in range(1, size):
                g = g + gs[p + q].astype(f32)
            cot.append(g.astype(o.dtype))
            p += size
        gv = list(pull(tuple(cot)))
        for (pos, _, _), e in zip(adds, extra):
            gv[pos] = gv[pos] + e.astype(gv[pos].dtype)
        return tuple(gv) + tuple(gv[pos] for pos in dup16)

    outs = []
    for k in grads:
        a, kind = ins[k]
        if kind == "row":
            outs.append((rows, a.shape[1] * ncol, f32, "rowc") if ncol > 1 else (rows, a.shape[1], f32, "row"))
        elif kind == "const":
            outs.append((a.shape[0], a.shape[1], f32, "acc"))
        elif kind[0] == "rowc":
            outs.append((rows, kind[1] * ncol, f32, "rowc"))
        else:
            outs.append((a.shape[0], kind[1] * ncol, f32, "accc"))
    for pos in dup16:
        nr, nc, _, kind = outs[pos]
        outs.append((nr, nc, bf16, kind))
    all_ins = list(ins) + flat_cts + [(a, kind) for _, a, kind in adds]
    return rowcall(name, wrapped, all_ins, outs, rows=rows, tile=tile, ncol=ncol)


def fn_rms(x, w):
    return (rms(x, w),)


def make_fn_gconv(n_norm_blocks):
    def fn(p, cw):
        c = cw[3:4] * p
        for jj in range(3):
            c = c + cw[jj:jj + 1] * tshift(p, 3 - jj)
        s = c * jax.nn.sigmoid(c)
        nrm = s * lax.rsqrt(jnp.sum(s * s, axis=-1, keepdims=True) + 1e-6)
        return (jnp.where(pl.program_id(0) < n_norm_blocks, nrm, s),)
    return fn


def fn_ggate(ps0, alog, dtb, e_g, e_b):
    g = -jnp.exp(alog) * softplus(ps0 + dtb)
    beta = jax.nn.sigmoid(ps0)
    return hdot(g, e_g), hdot(beta, e_b)


def fn_gpost(o, z, nw):
    return (rms(o, nw) * (z * jax.nn.sigmoid(z)),)


def fn_lerp(p, mu):
    return (p + (tshift(p, 1) - p) * mu,)


def fn_rprep(pk, psl, w0, a0, k_k, k_a, w2p, a2p, g2):
    g1, g2in = psl[:, :LANES], psl[:, LANES:]
    log_w = -softplus(-(w0 + bdot(jnp.tanh(g1), w2p))) - 0.5
    lw = -jnp.exp(log_w)
    a = jax.nn.sigmoid(a0 + bdot(g1, a2p))
    gate = bdot(jax.nn.sigmoid(g2in), g2)
    kkr = pk * k_k
    kk = kkr / jnp.maximum(jnp.sqrt(seg2sum(kkr * kkr)), 1e-12)
    kmod = pk * (1.0 + (a - 1.0) * k_a)
    return lw, kmod, kk, a, gate


def fn_rpost(y, r, kmod, v, gate, r_k, ln_w, ln_b):
    inv_n = 1.0 / RW_HEAD
    mean = seg2sum(y) * inv_n
    d = y - mean
    var = seg2sum(d * d) * inv_n
    yn = d * lax.rsqrt(var + RW_GN_EPS) * ln_w + ln_b
    bonus = seg2sum(r * kmod * r_k) * v
    return ((yn + bonus) * gate,)


def fn_xcore(q, k, v):
    s = bdot_nt(q, k) * (LANES ** -0.5)
    p = jax.nn.softmax(s, axis=-1)
    return (bdot(p, v),)


def fn_final(h, tgt, w):
    def loss_fn(h, w):
        return 0.5 * jnp.sum(jnp.mean(jnp.square(rms(h, w) - tgt), axis=-1))

    val, (dh, dw) = jax.value_and_grad(loss_fn, argnums=(0, 1))(h, w)
    return dh, dh.astype(bf16), dw, jnp.full((8, LANES), val, f32)


def fn_sumcols(n):
    def fn(x):
        w = x.shape[1] // n
        s = x[:, :w]
        for q in range(1, n):
            s = s + x[:, q * w:(q + 1) * w]
        return (s,)
    return fn


def _tri(c):
    ii = lax.broadcasted_iota(jnp.int32, (c, c), 0)
    jj = lax.broadcasted_iota(jnp.int32, (c, c), 1)
    return ii, jj


def _neumann_raw(m, steps):
    c = m.shape[-1]
    ii, jj = _tri(c)
    eye = (ii == jj).astype(f32)
    t, p = eye + m, m
    for _ in range(steps):
        p = _raw_dot(p, p, "nn", P_INV)
        t = _raw_dot(t, eye + p, "nn", P_INV)
    return t


@functools.partial(jax.custom_vjp, nondiff_argnums=(1,))
def _neumann_inverse(m, steps):
    return _neumann_raw(m, steps)


def _neumann_fwd(m, steps):
    t = _neumann_raw(m, steps)
    return t, t


def _neumann_bwd(steps, t, g):
    return (_raw_dot(_raw_dot(t, g, "tn", P_INV), t, "nt", P_INV),)


_neumann_inverse.defvjp(_neumann_fwd, _neumann_bwd)


def cumsum_rows(x):
    t = x.shape[1]
    ii, jj = _tri(t)
    tri = jnp.broadcast_to((ii >= jj).astype(f32), (x.shape[0], t, t))
    return pdot(tri, x, "nn", P_INV)


def gdn_chunk(q, k, v, gb, bb, s):
    c = q.shape[1]
    ii, jj = _tri(c)
    low = ii >= jj
    gcb = cumsum_rows(gb)
    gl = jnp.sum(gb, axis=1, keepdims=True)
    gc_col = gcb[:, :, :c]
    diff = gc_col - jnp.swapaxes(gc_col, 1, 2)
    decay = jnp.where(low, jnp.exp(jnp.where(low, diff, 0.0)), 0.0)
    qs = q * (q.shape[2] ** -0.5)
    kb = k * bb
    a = jnp.where(ii > jj, pdot(kb, k, "nt", P_BULK) * decay, 0.0)
    t = _neumann_inverse(-a, c.bit_length() - 2)
    eg = jnp.exp(gcb)
    u = pdot(t, v * bb, "nn", P_BULK)
    w = pdot(t, kb * eg, "nn", P_BULK)
    attn = pdot(qs, k, "nt", P_BULK) * decay
    kd = k * jnp.exp(gl - gcb)
    v_new = u - pdot(w, s, "nn", P_BULK)
    o = pdot(qs * eg, s, "nn", P_BULK) + pdot(attn, v_new, "nn", P_BULK)
    s_new = s * jnp.exp(gl) + pdot(kd, v_new, "tn", P_BULK)
    return o, s_new


def wkv_chunk(r, lw, k, v, kk, a, s):
    t = r.shape[1]
    ii, jj = _tri(t)
    lo = lax.broadcasted_iota(jnp.int32, r.shape, 2) < RW_HEAD
    cl = cumsum_rows(lw)
    cl_last = jnp.sum(lw, axis=1, keepdims=True)
    al = -kk * jnp.exp(cl - lw)
    be = (a * kk) * jnp.exp(-cl)
    kt = k * jnp.exp(-cl)
    rt = r * jnp.exp(cl)

    def dot(xa, xb, mode="nn"):
        return pdot(xa, xb, mode, P_BULK)

    def pair(xa, xb, msk):
        m_lo = dot(jnp.where(lo, xa, 0.0), xb, "nt")
        m_hi = dot(jnp.where(lo, 0.0, xa), xb, "nt")
        return jnp.where(msk, m_lo, 0.0), jnp.where(msk, m_hi, 0.0)

    def sel(x_lo, x_hi):
        return jnp.where(lo, x_lo, x_hi)

    ab = pair(al, be, ii > jj)
    ak = pair(al, kt, ii > jj)
    rb = pair(rt, be, ii >= jj)
    rk = pair(rt, kt, ii >= jj)
    x = dot(al, s, "nt") + sel(dot(ak[0], v), dot(ak[1], v))
    steps = t.bit_length() - 2
    u = sel(dot(_neumann_inverse(ab[0], steps), x), dot(_neumann_inverse(ab[1], steps), x))
    y = dot(rt, s, "nt") + sel(dot(rb[0], u) + dot(rk[0], v), dot(rb[1], u) + dot(rk[1], v))
    vi = lax.broadcasted_iota(jnp.int32, s.shape, 1) < RW_HEAD
    ki = lax.broadcasted_iota(jnp.int32, s.shape, 2) < RW_HEAD
    s_new = jnp.where(vi == ki, (s + dot(u, be, "tn") + dot(v, kt, "tn")) * jnp.exp(cl_last), 0.0)
    return y, s_new


def _scan_group(ncol, offs):
    g = SCAN_GROUP
    while g > 1 and (ncol % g or any(o % g for o in offs)):
        g //= 2
    return g


def scan_fwd(name, chunk_fn, ins, *, rows, chunk, ncol):
    n = rows // chunk
    n_in = len(ins)
    grp = _scan_group(ncol, [off for _, off in ins])

    def body(*refs):
        o_ref, st_ref, s_scr = refs[n_in:]

        @pl.when(pl.program_id(1) == 0)
        def _():
            s_scr[...] = jnp.zeros_like(s_scr)

        cols = [slice(b * LANES, (b + 1) * LANES) for b in range(grp)]
        s = s_scr[...]
        st_ref[...] = s
        o, s_new = chunk_fn(*[jnp.stack([r[:, c] for c in cols]) for r in refs[:n_in]], s)
        for b, c in enumerate(cols):
            o_ref[:, c] = o[b]
        s_scr[...] = s_new

    def spec(off):
        return pl.BlockSpec((chunk, grp * LANES), lambda h, c: (c, h + off // grp))

    return pl.pallas_call(
        body, name=name, grid=(ncol // grp, n), in_specs=[spec(off) for _, off in ins],
        out_specs=[spec(0), pl.BlockSpec((grp, None, LANES, LANES), lambda h, c: (h, c, 0, 0))],
        out_shape=[jax.ShapeDtypeStruct((rows, ncol * LANES), f32),
                   jax.ShapeDtypeStruct((ncol, n, LANES, LANES), f32)],
        scratch_shapes=[pltpu.VMEM((grp, LANES, LANES), f32)], compiler_params=_params(2))(*[a for a, _ in ins])


def scan_bwd(name, chunk_fn, ins, states, d_out, d_off, *, rows, chunk, ncol):
    n = rows // chunk
    n_in = len(ins)
    grp = _scan_group(ncol, [off for _, off in ins] + [d_off])

    def body(*refs):
        st_ref, do_ref = refs[n_in:n_in + 2]
        g_refs = refs[n_in + 2:2 * n_in + 2]
        ds_scr = refs[-1]

        @pl.when(pl.program_id(1) == 0)
        def _():
            ds_scr[...] = jnp.zeros_like(ds_scr)

        cols = [slice(b * LANES, (b + 1) * LANES) for b in range(grp)]

        def batch(ref):
            return jnp.stack([ref[:, c] for c in cols])

        _, pull = jax.vjp(chunk_fn, *[batch(r) for r in refs[:n_in]], st_ref[...])
        gs = pull((batch(do_ref), ds_scr[...]))
        for ref, g in zip(g_refs, gs[:n_in]):
            for b, c in enumerate(cols):
                ref[:, c] = g[b]
        ds_scr[...] = gs[n_in]

    def spec(off):
        return pl.BlockSpec((chunk, grp * LANES), lambda h, c: (n - 1 - c, h + off // grp))

    st_spec = pl.BlockSpec((grp, None, LANES, LANES), lambda h, c: (h, n - 1 - c, 0, 0))
    return pl.pallas_call(
        body, name=name, grid=(ncol // grp, n), in_specs=[spec(off) for _, off in ins] + [st_spec, spec(d_off)],
        out_specs=[spec(0)] * n_in, out_shape=[jax.ShapeDtypeStruct((rows, ncol * LANES), f32)] * n_in,
        scratch_shapes=[pltpu.VMEM((grp, LANES, LANES), f32)],
        compiler_params=_params(2))(*[a for a, _ in ins], states, d_out)


def flip_exchange(name, arrs, flips, n_slots, slot_of, src_of, with_self):
    n = len(arrs)
    nf = len(flips)

    def body(*refs):
        ins, outs = refs[:n], refs[n:2 * n]
        send, recv, lsem = refs[2 * n:]
        me = (lax.axis_index("x"), lax.axis_index("y"), lax.axis_index("c"))
        copies = []
        for k in range(n):
            if with_self:
                cp = pltpu.make_async_copy(src_of(ins[k], me), outs[k].at[slot_of(me)], lsem.at[k])
                cp.start()
                copies.append(cp)
            for j, fl in enumerate(flips):
                peer = tuple(1 - m if f else m for m, f in zip(me, fl))
                cp = pltpu.make_async_remote_copy(
                    src_ref=src_of(ins[k], peer), dst_ref=outs[k].at[slot_of(me)], send_sem=send.at[k, j],
                    recv_sem=recv.at[k, j], device_id=peer, device_id_type=MESH)
                cp.start()
                copies.append(cp)
        for cp in copies:
            cp.wait()

    def out_sds(a):
        blk = src_of(jax.ShapeDtypeStruct(a.shape, a.dtype), None)
        return jax.ShapeDtypeStruct((n_slots,) + tuple(blk), a.dtype)

    any_spec = pl.BlockSpec(memory_space=pl.ANY)
    return pl.pallas_call(
        body, name=name, in_specs=[any_spec] * n, out_specs=[any_spec] * n, out_shape=[out_sds(a) for a in arrs],
        scratch_shapes=[pltpu.SemaphoreType.DMA((n, nf)), pltpu.SemaphoreType.DMA((n, nf)),
                        pltpu.SemaphoreType.DMA((n,))],
        compiler_params=pltpu.CompilerParams(has_side_effects=True))(*arrs)


_CHIP_FLIPS = ((1, 0, 0), (0, 1, 0), (1, 1, 0))
_ALL_FLIPS = ((0, 0, 1), (0, 1, 0), (0, 1, 1), (1, 0, 0), (1, 0, 1), (1, 1, 0), (1, 1, 1))


def _whole(ref, pos):
    return ref.shape if pos is None else ref


def _chip_block(ref, pos):
    return ref.shape[1:] if pos is None else ref.at[2 * pos[0] + pos[1]]


def _chip_slot(p):
    return 2 * p[0] + p[1]


def gather_chips(name, arrs):
    return flip_exchange(name, arrs, _CHIP_FLIPS, 4, _chip_slot, _whole, True)


def scatter_chips(name, arrs):
    return flip_exchange(name, arrs, _CHIP_FLIPS, 4, _chip_slot, _chip_block, True)


_HBM = pl.BlockSpec(memory_space=pltpu.HBM)
_SEM = pl.BlockSpec(memory_space=pltpu.SEMAPHORE)
_DATAFLOW = pltpu.SideEffectType.DATAFLOW_SIDE_EFFECTING


def _chip_copies(srcs, lands, send, recv, src_of):
    me = (lax.axis_index("x"), lax.axis_index("y"), lax.axis_index("c"))
    copies = []
    for k, (src, land) in enumerate(zip(srcs, lands)):
        for j, fl in enumerate(_CHIP_FLIPS):
            peer = tuple(1 - m if f else m for m, f in zip(me, fl))
            q = k * len(_CHIP_FLIPS) + j
            copies.append(pltpu.make_async_remote_copy(
                src_ref=src_of(src, peer), dst_ref=land.at[_chip_slot(me)], send_sem=send.at[q],
                recv_sem=recv.at[q], device_id=peer, device_id_type=MESH))
    return copies


def chips_start(name, arrs, src_of):
    n = len(arrs)
    me = _chip_slot((lax.axis_index("x"), lax.axis_index("y")))
    lands = []
    for a in arrs:
        blk = tuple(src_of(jax.ShapeDtypeStruct(a.shape, a.dtype), None))
        own = a if src_of is _whole else lax.dynamic_index_in_dim(a, me, 0, keepdims=False)
        land = lax.dynamic_update_index_in_dim(lax.empty((4,) + blk, a.dtype), own, me, 0)
        lands.append(pltpu.with_memory_space_constraint(land, pltpu.HBM))
    srcs = [pltpu.with_memory_space_constraint(a, pltpu.HBM) for a in arrs]

    def body(*refs):
        send, recv = refs[2 * n], refs[2 * n + 1]
        for cp in _chip_copies(refs[:n], refs[n:2 * n], send, recv, src_of):
            cp.start()
        refs[-1][...] = jnp.zeros_like(refs[-1])

    sems = pltpu.SemaphoreType.DMA((n * len(_CHIP_FLIPS),))
    outs = pl.pallas_call(
        body, name=name, in_specs=[_HBM] * (2 * n),
        out_shape=(sems, sems, *[pltpu.HBM(a.shape, a.dtype) for a in srcs + lands],
                   jax.ShapeDtypeStruct((8, LANES), f32)),
        out_specs=(_SEM, _SEM, *[_HBM] * (2 * n), pl.BlockSpec(memory_space=pltpu.VMEM)),
        input_output_aliases={i: 2 + i for i in range(2 * n)},
        compiler_params=pltpu.CompilerParams(has_side_effects=_DATAFLOW))(*srcs, *lands)
    return (outs[0], outs[1], list(outs[2:2 + n]), list(outs[2 + n:2 + 2 * n]), src_of), outs[-1][0, 0]


def chips_wait(name, state, after):
    send, recv, srcs, lands, src_of = state
    n = len(srcs)

    def body(*refs):
        send, recv = refs[2 * n], refs[2 * n + 1]
        for cp in _chip_copies(refs[:n], refs[n:2 * n], send, recv, src_of):
            cp.wait_send()
            cp.wait_recv()

    outs = pl.pallas_call(
        body, name=name, in_specs=[_HBM] * (2 * n) + [_SEM, _SEM, pl.BlockSpec(memory_space=pl.ANY)],
        out_shape=tuple(pltpu.HBM(a.shape, a.dtype) for a in srcs + lands), out_specs=tuple([_HBM] * (2 * n)),
        input_output_aliases={i: i for i in range(2 * n)},
        compiler_params=pltpu.CompilerParams(has_side_effects=_DATAFLOW))(*srcs, *lands, send, recv, after)
    return list(outs[n:])


def swap_sibling(name, arrs):
    outs = flip_exchange(name, arrs, ((0, 0, 1),), 1, lambda p: 0, _whole, False)
    return [o[0] for o in outs]


def gather_all(arrs):
    return flip_exchange("gather_all", arrs, _ALL_FLIPS, 8, lambda p: 4 * p[0] + 2 * p[1] + p[2], _whole, True)


def _row_tile(nr, nc, n_arrays):
    budget = (20 << 20) // (n_arrays * 2 * 4 * max(nc, LANES))
    t = min(nr, budget) // 16 * 16
    while t > 0 and nr % t:
        t -= 16
    return t if t > 0 else nr


def sum_slots(name, x):
    ns, nr, nc = x.shape
    tile = _row_tile(nr, nc, ns + 1)

    def body(x_ref, o_ref):
        s = x_ref[0].astype(f32)
        for q in range(1, ns):
            s = s + x_ref[q].astype(f32)
        o_ref[...] = s

    return pl.pallas_call(
        body, name=name, grid=(nr // tile,), in_specs=[pl.BlockSpec((ns, tile, nc), lambda i: (0, i, 0))],
        out_specs=pl.BlockSpec((tile, nc), lambda i: (i, 0)), out_shape=jax.ShapeDtypeStruct((nr, nc), f32),
        compiler_params=_params(1))(x)


def adamw(name, w, g_parts, m, v):
    _, nr, nc = w.shape
    n_g = len(g_parts)
    tile = _row_tile(nr, nc, 7 + n_g)

    def body(*refs):
        w_ref, m_ref, v_ref = refs[:3]
        g = refs[3][...]
        for r in refs[4:3 + n_g]:
            g = g + r[...]
        g_ref, d_ref, nm_ref, nv_ref = refs[3 + n_g:]
        nm = ADAM_B1 * m_ref[...] + (1.0 - ADAM_B1) * g
        nv = ADAM_B2 * v_ref[...] + (1.0 - ADAM_B2) * jnp.square(g)
        m_hat = nm / (1.0 - ADAM_B1 ** ADAM_STEP)
        v_hat = nv / (1.0 - ADAM_B2 ** ADAM_STEP)
        g_ref[...] = g
        d_ref[...] = -ADAM_LR * (m_hat / (jnp.sqrt(v_hat) + ADAM_EPS) + ADAM_WD * w_ref[...])
        nm_ref[...] = nm
        nv_ref[...] = nv

    spec = pl.BlockSpec((tile, nc), lambda i: (i, 0))
    spec3 = pl.BlockSpec((None, tile, nc), lambda i: (0, i, 0))
    return pl.pallas_call(
        body, name=name, grid=(nr // tile,), in_specs=[spec3] * 3 + [spec] * n_g, out_specs=[spec3] * 4,
        out_shape=[jax.ShapeDtypeStruct((1, nr, nc), f32)] * 4, compiler_params=_params(1))(w, m, v, *g_parts)


def adamw_packed(w, g8, m, v):
    nr, nc = w.shape

    def body(w_ref, g_ref, m_ref, v_ref, go_ref, d_ref, nm_ref, nv_ref):
        g = g_ref[0]
        for q in range(1, 8):
            g = g + g_ref[q]
        nm = ADAM_B1 * m_ref[...] + (1.0 - ADAM_B1) * g
        nv = ADAM_B2 * v_ref[...] + (1.0 - ADAM_B2) * jnp.square(g)
        m_hat = nm / (1.0 - ADAM_B1 ** ADAM_STEP)
        v_hat = nv / (1.0 - ADAM_B2 ** ADAM_STEP)
        go_ref[...] = g
        d_ref[...] = -ADAM_LR * (m_hat / (jnp.sqrt(v_hat) + ADAM_EPS) + ADAM_WD * w_ref[...])
        nm_ref[...] = nm
        nv_ref[...] = nv

    return pl.pallas_call(body, name="adamw_packed", out_shape=[jax.ShapeDtypeStruct((nr, nc), f32)] * 4,
                          compiler_params=pltpu.CompilerParams(vmem_limit_bytes=VMEM_LIMIT))(w, g8, m, v)


def _pack(vectors):
    rows = []
    for a in vectors:
        flat = a.reshape(-1).astype(f32)
        pad = (-flat.shape[0]) % LANES
        rows.append(jnp.pad(flat, (0, pad)).reshape(-1, LANES))
    packed = jnp.concatenate(rows, axis=0)
    return jnp.pad(packed, ((0, (-packed.shape[0]) % 8), (0, 0)))


def _unpack(packed, like):
    out, r = [], 0
    for a in like:
        n = a.size
        nr = -(-n // LANES)
        out.append(packed[r:r + nr].reshape(-1)[:n].reshape(a.shape))
        r += nr
    return out


def kernel(x, mem, mix_norm_w, w_in, dn_conv_w, dn_a_log, dn_dt_bias, dn_norm_w, rw_mu, rw_w0, rw_w2, rw_a0, rw_a2, rw_g2, rw_k_k, rw_k_a, rw_r_k, rw_ln_w, rw_ln_b, w_out, xa_norm_w, mem_norm_w, xa_wq, xa_wk, xa_wv, xa_wo, ffn_norm_w, ffn_w1, ffn_w2, final_norm_w, loss_target, m_mix_norm_w, m_w_in, m_dn_conv_w, m_dn_a_log, m_dn_dt_bias, m_dn_norm_w, m_rw_mu, m_rw_w0, m_rw_w2, m_rw_a0, m_rw_a2, m_rw_g2, m_rw_k_k, m_rw_k_a, m_rw_r_k, m_rw_ln_w, m_rw_ln_b, m_w_out, m_xa_norm_w, m_mem_norm_w, m_xa_wq, m_xa_wk, m_xa_wv, m_xa_wo, m_ffn_norm_w, m_ffn_w1, m_ffn_w2, m_final_norm_w, v_mix_norm_w, v_w_in, v_dn_conv_w, v_dn_a_log, v_dn_dt_bias, v_dn_norm_w, v_rw_mu, v_rw_w0, v_rw_w2, v_rw_a0, v_rw_a2, v_rw_g2, v_rw_k_k, v_rw_k_a, v_rw_r_k, v_rw_ln_w, v_rw_ln_b, v_w_out, v_xa_norm_w, v_mem_norm_w, v_xa_wq, v_xa_wk, v_xa_wv, v_xa_wo, v_ffn_norm_w, v_ffn_w1, v_ffn_w2, v_final_norm_w):
    weights = dict(mix_norm_w=mix_norm_w, w_in=w_in, dn_conv_w=dn_conv_w, dn_a_log=dn_a_log, dn_dt_bias=dn_dt_bias, dn_norm_w=dn_norm_w, rw_mu=rw_mu, rw_w0=rw_w0, rw_w2=rw_w2, rw_a0=rw_a0, rw_a2=rw_a2, rw_g2=rw_g2, rw_k_k=rw_k_k, rw_k_a=rw_k_a, rw_r_k=rw_r_k, rw_ln_w=rw_ln_w, rw_ln_b=rw_ln_b, w_out=w_out, xa_norm_w=xa_norm_w, mem_norm_w=mem_norm_w, xa_wq=xa_wq, xa_wk=xa_wk, xa_wv=xa_wv, xa_wo=xa_wo, ffn_norm_w=ffn_norm_w, ffn_w1=ffn_w1, ffn_w2=ffn_w2, final_norm_w=final_norm_w)
    mom_m = dict(mix_norm_w=m_mix_norm_w, w_in=m_w_in, dn_conv_w=m_dn_conv_w, dn_a_log=m_dn_a_log, dn_dt_bias=m_dn_dt_bias, dn_norm_w=m_dn_norm_w, rw_mu=m_rw_mu, rw_w0=m_rw_w0, rw_w2=m_rw_w2, rw_a0=m_rw_a0, rw_a2=m_rw_a2, rw_g2=m_rw_g2, rw_k_k=m_rw_k_k, rw_k_a=m_rw_k_a, rw_r_k=m_rw_r_k, rw_ln_w=m_rw_ln_w, rw_ln_b=m_rw_ln_b, w_out=m_w_out, xa_norm_w=m_xa_norm_w, mem_norm_w=m_mem_norm_w, xa_wq=m_xa_wq, xa_wk=m_xa_wk, xa_wv=m_xa_wv, xa_wo=m_xa_wo, ffn_norm_w=m_ffn_norm_w, ffn_w1=m_ffn_w1, ffn_w2=m_ffn_w2, final_norm_w=m_final_norm_w)
    mom_v = dict(mix_norm_w=v_mix_norm_w, w_in=v_w_in, dn_conv_w=v_dn_conv_w, dn_a_log=v_dn_a_log, dn_dt_bias=v_dn_dt_bias, dn_norm_w=v_dn_norm_w, rw_mu=v_rw_mu, rw_w0=v_rw_w0, rw_w2=v_rw_w2, rw_a0=v_rw_a0, rw_a2=v_rw_a2, rw_g2=v_rw_g2, rw_k_k=v_rw_k_k, rw_k_a=v_rw_k_a, rw_r_k=v_rw_r_k, rw_ln_w=v_rw_ln_w, rw_ln_b=v_rw_ln_b, w_out=v_w_out, xa_norm_w=v_xa_norm_w, mem_norm_w=v_mem_norm_w, xa_wq=v_xa_wq, xa_wk=v_xa_wk, xa_wv=v_xa_wv, xa_wo=v_xa_wo, ffn_norm_w=v_ffn_norm_w, ffn_w1=v_ffn_w1, ffn_w2=v_ffn_w2, final_norm_w=v_final_norm_w)
    names = list(weights)

    seq, d = x.shape[1], x.shape[2]
    dnw = d // 2
    rww = d - dnw
    nh, nb = dnw // LANES, rww // LANES
    n_mem = mem.shape[1]
    lw_dim, la_dim, lg_dim = rw_w2.shape[1], rw_a2.shape[1], rw_g2.shape[1]
    assert lw_dim + la_dim == LANES and lg_dim == LANES and dnw % LANES == 0 and rww % LANES == 0
    xs, mems, tgt = x[0], mem[0], loss_target[0]

    col_sharded = ("w_in", "xa_wo", "ffn_w1", "dn_conv_w", "rw_w2", "rw_a2", "rw_g2")
    row_sharded = ("w_out", "xa_wq", "xa_wk", "xa_wv", "ffn_w2")
    f32_payload = ("dn_conv_w", "rw_w2", "rw_a2", "rw_g2")
    sharded = col_sharded + row_sharded
    payload = {n: weights[n][0].astype(f32 if n in f32_payload else bf16) for n in sharded}
    first = ("w_in", "dn_conv_w", "rw_w2", "rw_a2", "rw_g2")
    mid = ("w_out", "xa_wq", "xa_wk", "xa_wv", "xa_wo")
    late = ("ffn_w1", "ffn_w2")
    gathered = dict(zip(first, gather_chips("gather_first", [payload[n] for n in first])))
    ordered = lax.optimization_barrier(([gathered[n] for n in first], [payload[n] for n in mid + late]))
    gathered = dict(zip(first, ordered[0]))
    payload.update(zip(mid + late, ordered[1]))
    mid_state, tok_mid = chips_start("gather_mid_start", [payload[n] for n in mid], _whole)
    late_state, tok_late = chips_start("gather_late_start", [payload[n] for n in late], _whole)
    mix_norm_w_in = mix_norm_w + (tok_mid + tok_late)

    def full(n):
        g = gathered[n]
        if n in col_sharded:
            return g.transpose(1, 0, 2).reshape(g.shape[1], 4 * g.shape[2])
        return g.reshape(4 * g.shape[1], g.shape[2])

    w_in_f = full("w_in")
    c_dn = 4 * dnw
    c_rw0 = c_dn + 2 * nh
    c_rw = 3 * rww
    w_main = jnp.concatenate([w_in_f[:, :c_dn], w_in_f[:, c_rw0:c_rw0 + c_rw]], axis=1)
    w_small = jnp.concatenate([w_in_f[:, c_dn:c_rw0], jnp.zeros((d, LANES - 2 * nh), bf16),
                               w_in_f[:, c_rw0 + c_rw:]], axis=1)
    conv_w = full("dn_conv_w")
    w2p = jnp.concatenate([full("rw_w2"), jnp.zeros((la_dim, rww), f32)], axis=0)
    a2p = jnp.concatenate([jnp.zeros((lw_dim, rww), f32), full("rw_a2")], axis=0)
    g2 = full("rw_g2")
    xaw = xa_wq.shape[2]
    nxh = xaw // LANES
    ffn = 4 * ffn_w1.shape[2]

    def lane_row(vec):
        return jnp.pad(vec.reshape(1, -1), ((0, 0), (0, LANES - vec.size)))

    alog_row, dtb_row = lane_row(dn_a_log), lane_row(dn_dt_bias)
    head_of_col = jnp.arange(dnw)[None, :] // LANES
    e_g = (jnp.arange(LANES)[:, None] == head_of_col).astype(f32)
    e_b = (jnp.arange(LANES)[:, None] == head_of_col + nh).astype(f32)
    mu_main, mu_small = rw_mu[:, :c_rw], rw_mu[:, c_rw:]
    r_k_row = rw_r_k.reshape(1, rww)
    fnw = final_norm_w.reshape(1, d)
    qb, kb_, vb, zb = 0, nh, 2 * nh, 3 * nh
    rb0 = 4 * nh
    cc = lambda w_, o_: ("constc", w_, o_)
    rc = lambda o_: ("rowc", LANES, o_)

    (u16,) = rowcall("norm_mix", fn_rms, [(xs, "row"), (mix_norm_w_in, "const")], [(seq, d, bf16, "row")], rows=seq)
    p_main = mm("in_main", u16, w_main)
    p_small = mm("in_small", u16, w_small)

    fn_gconv = make_fn_gconv(2 * nh)
    (qkv,) = rowcall("gdn_conv", fn_gconv, [(p_main, rc(0)), (conv_w, cc(LANES, 0))],
                     [(seq, 3 * dnw, f32, "rowc")], rows=seq, tile=seq, ncol=3 * nh)
    gate_ins = [(p_small, rc(0)), (alog_row, "const"), (dtb_row, "const"), (e_g, "const"), (e_b, "const")]
    g_b, beta_b = rowcall("gdn_gate", fn_ggate, gate_ins, [(seq, dnw, f32, "row")] * 2, rows=seq)
    gdn_ins = [(qkv, qb), (qkv, kb_), (qkv, vb), (g_b, 0), (beta_b, 0)]
    o_raw, gdn_states = scan_fwd("gdn_scan", gdn_chunk, gdn_ins, rows=seq, chunk=GDN_CHUNK, ncol=nh)
    gpost_ins = [(o_raw, rc(0)), (p_main, rc(zb)), (dn_norm_w, "const")]
    (o_dn,) = rowcall("gdn_post", fn_gpost, gpost_ins, [(seq, dnw, bf16, "rowc")], rows=seq, ncol=nh)

    (prw,) = rowcall("rw_lerp_main", fn_lerp, [(p_main, rc(rb0)), (mu_main, cc(LANES, 0))],
                     [(seq, c_rw, f32, "rowc")], rows=seq, tile=seq, ncol=3 * nb)
    (psl,) = rowcall("rw_lerp_small", fn_lerp, [(p_small, rc(1)), (mu_small, cc(LANES, 0))],
                     [(seq, 2 * LANES, f32, "rowc")], rows=seq, tile=seq, ncol=2)
    rprep_ins = [(prw, rc(nb)), (psl, "row"), (rw_w0, cc(LANES, 0)), (rw_a0, cc(LANES, 0)), (rw_k_k, cc(LANES, 0)),
                 (rw_k_a, cc(LANES, 0)), (w2p, cc(LANES, 0)), (a2p, cc(LANES, 0)), (g2, cc(LANES, 0))]
    lw, kmod, kk, a_rw, gate = rowcall("rw_prep", fn_rprep, rprep_ins, [(seq, rww, f32, "rowc")] * 5,
                                        rows=seq, ncol=nb)
    wkv_ins = [(prw, 0), (lw, 0), (kmod, 0), (prw, 2 * nb), (kk, 0), (a_rw, 0)]
    y_rw, wkv_states = scan_fwd("wkv_scan", wkv_chunk, wkv_ins, rows=seq, chunk=WKV_CHUNK, ncol=nb)
    rpost_ins = [(y_rw, rc(0)), (prw, rc(0)), (kmod, rc(0)), (prw, rc(2 * nb)), (gate, rc(0)),
                 (r_k_row, cc(LANES, 0)), (rw_ln_w, cc(LANES, 0)), (rw_ln_b, cc(LANES, 0))]
    (o_rw,) = rowcall("rw_post", fn_rpost, rpost_ins, [(seq, rww, bf16, "rowc")], rows=seq, ncol=nb)

    o_cat = jnp.concatenate([o_dn, o_rw], axis=1)
    gathered.update(zip(mid, chips_wait("gather_mid_wait", mid_state, o_cat)))
    w_out_f, wq_f, wk_f, wv_f, wo_f = full("w_out"), full("xa_wq"), full("xa_wk"), full("xa_wv"), full("xa_wo")
    h1 = mm("out_proj", o_cat, w_out_f, add=xs)

    (hn16,) = rowcall("norm_xa", fn_rms, [(h1, "row"), (xa_norm_w, "const")], [(seq, d, bf16, "row")], rows=seq)
    (mn16,) = rowcall("norm_mem", fn_rms, [(mems, "row"), (mem_norm_w, "const")], [(n_mem, d, bf16, "row")],
                      rows=n_mem)
    q_xa = mm("xa_q", hn16, wq_f)
    k_xa = mm("xa_k", mn16, wk_f)
    v_xa = mm("xa_v", mn16, wv_f)
    xcore_ins = [(q_xa, rc(0)), (k_xa, cc(LANES, 0)), (v_xa, cc(LANES, 0))]
    (o_xa,) = rowcall("xa_core", fn_xcore, xcore_ins, [(seq, xaw, bf16, "rowc")], rows=seq, ncol=nxh)
    h2 = mm("xa_o", o_xa, wo_f, add=h1)

    (fn16,) = rowcall("norm_ffn", fn_rms, [(h2, "row"), (ffn_norm_w, "const")], [(seq, d, bf16, "row")], rows=seq)
    gathered.update(zip(late, chips_wait("gather_late_wait", late_state, fn16)))
    w1_f, w2_f = full("ffn_w1"), full("ffn_w2")
    a1_16, hid16 = mm("ffn_up", fn16, w1_f, epilogue=lambda r: (r, jnp.square(jnp.maximum(r, 0.0))),
                      out_dtypes=(bf16, bf16))
    h3 = mm("ffn_down", hid16, w2_f, add=h2)

    dh3, dh3_16, d_fnw, loss_rows = rowcall(
        "loss_head", fn_final, [(h3, "row"), (tgt, "row"), (fnw, "const")],
        [(seq, d, f32, "row"), (seq, d, bf16, "row"), (1, d, f32, "acc"), (8, LANES, f32, "acc")], rows=seq)

    da1_16 = mm("ffn_down_dx", dh3_16, w2_f, tb=True, extra=[a1_16], out_dtype=bf16,
                epilogue=lambda r, a1: (r * (2.0 * jnp.maximum(a1.astype(f32), 0.0)),))
    g_ffn_w2 = mm("ffn_down_dw", hid16, dh3_16, ta=True, out_dtype=bf16)
    g_ffn_w1 = mm("ffn_up_dw", fn16, da1_16, ta=True, out_dtype=bf16, by_chip=True)
    dfn = mm("ffn_up_dx", da1_16, w1_f, tb=True)

    def by_chip(n, g):
        if g.ndim == 3:
            return g
        if n in col_sharded:
            return g.reshape(g.shape[0], 4, g.shape[1] // 4).transpose(1, 0, 2)
        return g.reshape(4, g.shape[0] // 4, g.shape[1])

    late_g, tok = chips_start("scatter_late_start", [by_chip("ffn_w1", g_ffn_w1), by_chip("ffn_w2", g_ffn_w2)],
                              _chip_block)
    dh2, d_ffn_nw, dh2_16 = rowvjp("norm_ffn_bwd", fn_rms, [(h2, "row"), (ffn_norm_w + tok, "const")],
                                   [[(dfn, "row")]], [0, 1], rows=seq, adds=[(0, dh3, "row")], dup16=[0])

    do_xa = mm("xa_o_dx", dh2_16, wo_f, tb=True)
    g_xa_wo = mm("xa_o_dw", o_xa, dh2_16, ta=True, out_dtype=bf16, by_chip=True)
    dq_xa, dk_xa, dv_xa, dq16 = rowvjp("xa_core_bwd", fn_xcore, xcore_ins, [[(do_xa, rc(0))]], [0, 1, 2],
                                       rows=seq, ncol=nxh, dup16=[0])
    g_xa_wq = mm("xa_q_dw", hn16, dq16, ta=True, out_dtype=bf16)
    dhn = mm("xa_q_dx", dq16, wq_f, tb=True)
    dh1, d_xa_nw, dh1_16 = rowvjp("norm_xa_bwd", fn_rms, [(h1, "row"), (xa_norm_w, "const")], [[(dhn, "row")]],
                                  [0, 1], rows=seq, adds=[(0, dh2, "row")], dup16=[0])
    dk16, dv16 = dk_xa.astype(bf16), dv_xa.astype(bf16)
    g_xa_wk = mm("xa_k_dw", mn16, dk16, ta=True, out_dtype=bf16)
    g_xa_wv = mm("xa_v_dw", mn16, dv16, ta=True, out_dtype=bf16)
    dmn = mm("xa_v_dx", dv16, wv_f, tb=True, add=mm("xa_k_dx", dk16, wk_f, tb=True))
    (d_mem_nw,) = rowvjp("norm_mem_bwd", fn_rms, [(mems, "row"), (mem_norm_w, "const")], [[(dmn, "row")]], [1],
                         rows=n_mem)

    g_w_out = mm("out_proj_dw", o_cat, dh1_16, ta=True, out_dtype=bf16)
    do_cat = mm("out_proj_dx", dh1_16, w_out_f, tb=True)
    mid_grads = dict(w_out=g_w_out, xa_wq=g_xa_wq, xa_wk=g_xa_wk, xa_wv=g_xa_wv, xa_wo=g_xa_wo)
    mid_g, tok = chips_start("scatter_mid_start", [by_chip(n, mid_grads[n]) for n in mid], _chip_block)
    rpost_ins_b = rpost_ins[:5] + [(r_k_row + tok, cc(LANES, 0))] + rpost_ins[6:]

    dy, dr_a, dkmod_a, dv_a, dgate, d_r_k, d_ln_w, d_ln_b = rowvjp(
        "rw_post_bwd", fn_rpost, rpost_ins_b, [[(do_cat, rc(nh))]], [0, 1, 2, 3, 4, 5, 6, 7], rows=seq, ncol=nb)
    dr_b, dlw, dkmod_b, dv_b, dkk, da_rw = scan_bwd("wkv_scan_bwd", wkv_chunk, wkv_ins, wkv_states, dy, 0,
                                                    rows=seq, chunk=WKV_CHUNK, ncol=nb)
    rprep_cts = [[(dlw, rc(0))], [(dkmod_a, rc(0)), (dkmod_b, rc(0))], [(dkk, rc(0))], [(da_rw, rc(0))],
                 [(dgate, rc(0))]]
    dpk, dpsl_parts, d_w0, d_a0, d_k_k, d_k_a, d_w2p, d_a2p, d_g2 = rowvjp(
        "rw_prep_bwd", fn_rprep, rprep_ins, rprep_cts, [0, 1, 2, 3, 4, 5, 6, 7, 8], rows=seq, ncol=nb)
    (dpsl,) = rowcall("rw_prep_sum", fn_sumcols(nb), [(dpsl_parts, "row")], [(seq, 2 * LANES, f32, "row")], rows=seq)

    def lerp_bwd(tag, p, p_off, mu, mu_off, ct_lists, ncol):
        return rowvjp("rw_lerp_bwd_" + tag, fn_lerp, [(p, rc(p_off)), (mu, cc(LANES, mu_off))], [ct_lists], [0, 1],
                      rows=seq, tile=seq, ncol=ncol, dup16=[0])

    _, dmu_r, dpr16 = lerp_bwd("r", p_main, rb0, mu_main, 0, [(dr_a, rc(0)), (dr_b, rc(0))], nb)
    _, dmu_k, dpk16 = lerp_bwd("k", p_main, rb0 + nb, mu_main, nb, [(dpk, rc(0))], nb)
    _, dmu_v, dpv16 = lerp_bwd("v", p_main, rb0 + 2 * nb, mu_main, 2 * nb, [(dv_a, rc(0)), (dv_b, rc(0))], nb)
    _, dmu_s, dps12_16 = lerp_bwd("small", p_small, 1, mu_small, 0, [(dpsl, rc(0))], 2)

    do_raw, dz, d_dn_nw, dz16 = rowvjp("gdn_post_bwd", fn_gpost, gpost_ins, [[(do_cat, rc(0))]], [0, 1, 2],
                                       rows=seq, ncol=nh, dup16=[1])
    dq_g, dk_g, dv_g, dg_b, dbeta_b = scan_bwd("gdn_scan_bwd", gdn_chunk, gdn_ins, gdn_states, do_raw, 0,
                                               rows=seq, chunk=GDN_CHUNK, ncol=nh)
    dps0, d_alog, d_dtb, dps0_16 = rowvjp("gdn_gate_bwd", fn_ggate, gate_ins, [[(dg_b, "row")], [(dbeta_b, "row")]],
                                          [0, 1, 2], rows=seq, dup16=[0])
    dqkv = jnp.concatenate([dq_g, dk_g, dv_g], axis=1)
    _, d_conv_w, dqkv16 = rowvjp("gdn_conv_bwd", fn_gconv, [(p_main, rc(0)), (conv_w, cc(LANES, 0))],
                                 [[(dqkv, rc(0))]], [0, 1], rows=seq, tile=seq, ncol=3 * nh, dup16=[0])

    dp_main16 = jnp.concatenate([dqkv16, dz16, dpr16, dpk16, dpv16], axis=1)
    dp_small16 = jnp.concatenate([dps0_16, dps12_16], axis=1)
    g_w_main = mm("in_main_dw", u16, dp_main16, ta=True, out_dtype=bf16)
    g_w_small = mm("in_small_dw", u16, dp_small16, ta=True, out_dtype=bf16)
    g_w_in = jnp.concatenate([g_w_main[:, :c_dn], g_w_small[:, :2 * nh], g_w_main[:, c_dn:],
                              g_w_small[:, LANES:]], axis=1)
    first_grads = dict(w_in=g_w_in, dn_conv_w=d_conv_w, rw_w2=d_w2p[:lw_dim], rw_a2=d_a2p[lw_dim:], rw_g2=d_g2)
    first_g, tok = chips_start("scatter_first_start", [by_chip(n, first_grads[n]) for n in first], _chip_block)

    du = mm("in_small_dx", dp_small16, w_small, tb=True, add=mm("in_main_dx", dp_main16, w_main, tb=True))
    grad_x, d_mix_nw = rowvjp("norm_mix_bwd", fn_rms, [(xs, "row"), (mix_norm_w + tok, "const")], [[(du, "row")]],
                              [0, 1], rows=seq, adds=[(0, dh1, "row")])
    received = dict(zip(mid, chips_wait("scatter_mid_wait", mid_g, grad_x)))
    received.update(zip(late, chips_wait("scatter_late_wait", late_g, grad_x)))

    out = {}

    def reduce_and_update(tag, group):
        partial = [sum_slots("sum_chips_" + n, received[n]) for n in group]
        other = swap_sibling("swap_sibling_" + tag, partial)
        for n, p_mine, p_other in zip(group, partial, other):
            out[n] = adamw("adamw_" + n, weights[n], [p_mine, p_other], mom_m[n], mom_v[n])

    reduce_and_update("rest", mid + late)
    received.update(zip(first, chips_wait("scatter_first_wait", first_g, out["ffn_w2"][1])))
    reduce_and_update("first", first)

    small_names = [n for n in names if n not in sharded]
    small_local = dict(
        mix_norm_w=d_mix_nw, dn_a_log=d_alog[:, :nh], dn_dt_bias=d_dtb[:, :nh], dn_norm_w=d_dn_nw,
        rw_mu=jnp.concatenate([dmu_r, dmu_k, dmu_v, dmu_s], axis=1), rw_w0=d_w0, rw_a0=d_a0, rw_k_k=d_k_k,
        rw_k_a=d_k_a, rw_r_k=d_r_k, rw_ln_w=d_ln_w, rw_ln_b=d_ln_b, xa_norm_w=d_xa_nw, mem_norm_w=d_mem_nw,
        ffn_norm_w=d_ffn_nw, final_norm_w=d_fnw)
    loss_vec = jnp.where(jnp.arange(LANES) == 0, loss_rows[0], 0.0)
    (g8,) = gather_all([_pack([small_local[n] for n in small_names] + [loss_vec])])

    packed_like = [weights[n] for n in small_names] + [loss_vec]
    zero = jnp.zeros((LANES,), f32)
    res = adamw_packed(_pack([weights[n] for n in small_names] + [zero]), g8,
                       _pack([mom_m[n] for n in small_names] + [zero]),
                       _pack([mom_v[n] for n in small_names] + [zero]))
    unpacked = [_unpack(r, packed_like) for r in res]
    for i, n in enumerate(small_names):
        out[n] = [u[i] for u in unpacked]
    loss = unpacked[0][-1][0]

    return (loss, grad_x.reshape(x.shape), *[out[n][0] for n in names], *[out[n][1] for n in names],
            *[out[n][2] for n in names], *[out[n][3] for n in names])
```

```python
import functools

import jax
import jax.numpy as jnp
from jax import lax
from jax.experimental import pallas as pl
from jax.experimental.pallas import tpu as pltpu

f32 = jnp.float32
bf16 = jnp.bfloat16
HI = lax.Precision.HIGHEST
MESH = pl.DeviceIdType.MESH

LANES = 128
VMEM_LIMIT = 56 << 20
TOK_TILE = 256
MM_TILE = 1024
MM_TILE_K = 2048
GDN_CHUNK = 128
WKV_CHUNK = 64
SCAN_GROUP = 8
P_BULK = 1
P_INV = 3
RMS_EPS = 1e-6
RW_GN_EPS = 64e-5
RW_HEAD = 64

ADAM_LR, ADAM_B1, ADAM_B2, ADAM_EPS, ADAM_WD, ADAM_STEP = 0.001, 0.9, 0.999, 1e-08, 0.01, 10


def _params(n_grid):
    return pltpu.CompilerParams(dimension_semantics=("arbitrary",) * n_grid, vmem_limit_bytes=VMEM_LIMIT)


_DIMS = {"nn": (((1,), (0,)), ((), ())), "nt": (((1,), (1,)), ((), ())), "tn": (((0,), (0,)), ((), ()))}
_DIMS_BATCHED = {"nn": (((2,), (1,)), ((0,), (0,))), "nt": (((2,), (2,)), ((0,), (0,))),
                 "tn": (((1,), (1,)), ((0,), (0,)))}


def _raw_dot(a, b, mode, passes):
    dims = (_DIMS if a.ndim == 2 else _DIMS_BATCHED)[mode]
    if passes == 6:
        return lax.dot_general(a.astype(f32), b.astype(f32), dims, precision=HI, preferred_element_type=f32)
    ah, bh = a.astype(bf16), b.astype(bf16)
    r = lax.dot_general(ah, bh, dims, preferred_element_type=f32)
    if passes == 3:
        al = (a - ah.astype(f32)).astype(bf16)
        bl = (b - bh.astype(f32)).astype(bf16)
        r = r + lax.dot_general(al, bh, dims, preferred_element_type=f32)
        r = r + lax.dot_general(ah, bl, dims, preferred_element_type=f32)
    return r


@functools.partial(jax.custom_vjp, nondiff_argnums=(2, 3))
def pdot(a, b, mode, passes):
    return _raw_dot(a, b, mode, passes)


def _pdot_bwd(mode, passes, res, g):
    a, b = res
    if mode == "nn":
        da, db = _raw_dot(g, b, "nt", passes), _raw_dot(a, g, "tn", passes)
    elif mode == "nt":
        da, db = _raw_dot(g, b, "nn", passes), _raw_dot(g, a, "tn", passes)
    else:
        da, db = _raw_dot(b, g, "nt", passes), _raw_dot(a, g, "nn", passes)
    return da.astype(a.dtype), db.astype(b.dtype)


pdot.defvjp(lambda a, b, mode, passes: (_raw_dot(a, b, mode, passes), (a, b)), _pdot_bwd)


def hdot(a, b):
    return pdot(a, b, "nn", 6)


def bdot(a, b):
    return pdot(a, b, "nn", 1)


def bdot_nt(a, b):
    return pdot(a, b, "nt", 1)


def _shift_rows(x, k):
    row = lax.broadcasted_iota(jnp.int32, x.shape, 0)
    return jnp.where(row < k, 0.0, pltpu.roll(x, k, axis=0))


def _unshift_rows(g, k):
    n = g.shape[0]
    row = lax.broadcasted_iota(jnp.int32, g.shape, 0)
    return jnp.where(row >= n - k, 0.0, pltpu.roll(g, n - k, axis=0))


@functools.partial(jax.custom_vjp, nondiff_argnums=(1,))
def tshift(x, k):
    return _shift_rows(x, k)


tshift.defvjp(lambda x, k: (_shift_rows(x, k), None), lambda k, _, g: (_unshift_rows(g, k),))


def rms(x, w):
    x = x.astype(f32)
    return x * lax.rsqrt(jnp.mean(x * x, axis=-1, keepdims=True) + RMS_EPS) * w


def softplus(x):
    return jnp.maximum(x, 0.0) + jnp.log(1.0 + jnp.exp(-jnp.abs(x)))


def seg2sum(x):
    lo = lax.broadcasted_iota(jnp.int32, x.shape, 1) < RW_HEAD
    s_lo = jnp.sum(jnp.where(lo, x, 0.0), axis=-1, keepdims=True)
    s_hi = jnp.sum(jnp.where(lo, 0.0, x), axis=-1, keepdims=True)
    return jnp.where(lo, s_lo, s_hi)


def _tile(n, pref):
    if n <= pref:
        return n
    t = pref
    while t >= LANES:
        if n % t == 0:
            return t
        t -= LANES
    return n


def mm(name, a, b, *, ta=False, tb=False, add=None, out_dtype=f32, by_chip=False, epilogue=None, extra=(),
       out_dtypes=None):
    (k, m) = a.shape if ta else a.shape[::-1]
    (n, kb) = b.shape if tb else b.shape[::-1]
    assert k == kb, (name, a.shape, b.shape)
    tm, tk = _tile(m, MM_TILE), _tile(k, MM_TILE_K)
    tn = _tile(n // 4, MM_TILE) if by_chip else _tile(n, MM_TILE)
    nk = k // tk
    dims = (((0,) if ta else (1,), (1,) if tb else (0,)), ((), ()))
    extra = list(extra) + ([] if add is None else [add])
    out_dtypes = [out_dtype] if out_dtypes is None else list(out_dtypes)
    n_extra, n_out = len(extra), len(out_dtypes)

    def body(*refs):
        a_ref, b_ref = refs[:2]
        x_refs = refs[2:2 + n_extra]
        o_refs = refs[2 + n_extra:2 + n_extra + n_out]
        part = lax.dot_general(a_ref[...].astype(bf16), b_ref[...].astype(bf16), dims, preferred_element_type=f32)

        def finish(r):
            xs = [x[...] for x in x_refs]
            if add is not None:
                r = r + xs.pop().astype(f32)
            outs = (r,) if epilogue is None else epilogue(r, *xs)
            for o_ref, o in zip(o_refs, outs):
                o_ref[...] = o.astype(o_ref.dtype)

        if nk == 1:
            finish(part)
            return
        acc = refs[-1]
        kk = pl.program_id(2)

        @pl.when(kk == 0)
        def _():
            acc[...] = part

        @pl.when(kk > 0)
        def _():
            acc[...] += part

        @pl.when(kk == nk - 1)
        def _():
            finish(acc[...])

    a_spec = pl.BlockSpec((tk, tm), lambda i, j, q: (q, i)) if ta else pl.BlockSpec((tm, tk), lambda i, j, q: (i, q))
    b_spec = pl.BlockSpec((tn, tk), lambda i, j, q: (j, q)) if tb else pl.BlockSpec((tk, tn), lambda i, j, q: (q, j))
    x_spec = pl.BlockSpec((tm, tn), lambda i, j, q: (i, j))
    if by_chip:
        per_chip = n // 4 // tn
        o_spec = pl.BlockSpec((None, tm, tn), lambda i, j, q: (j // per_chip, i, j % per_chip))
        o_shape = (4, m, n // 4)
    else:
        o_spec, o_shape = x_spec, (m, n)
    res = pl.pallas_call(
        body, name=name, grid=(m // tm, n // tn, nk), in_specs=[a_spec, b_spec] + [x_spec] * n_extra,
        out_specs=[o_spec] * n_out, out_shape=[jax.ShapeDtypeStruct(o_shape, dt) for dt in out_dtypes],
        scratch_shapes=[pltpu.VMEM((tm, tn), f32)] if nk > 1 else [], compiler_params=_params(3))(a, b, *extra)
    return res[0] if n_out == 1 else res


def _in_spec(a, kind, tile):
    if kind == "row":
        return pl.BlockSpec((tile, a.shape[1]), lambda j, i: (i, 0))
    if kind == "const":
        return pl.BlockSpec(a.shape, lambda j, i: (0, 0))
    tag, cw, off = kind
    if tag == "rowc":
        return pl.BlockSpec((tile, cw), lambda j, i: (i, j + off))
    assert tag == "constc", kind
    return pl.BlockSpec((a.shape[0], cw), lambda j, i: (0, j + off))


def rowcall(name, fn, ins, outs, *, rows, tile=None, ncol=1):
    tile = min(TOK_TILE, rows) if tile is None else tile
    n_in = len(ins)
    kinds = [o[3] for o in outs]

    def body(*refs):
        j, i = pl.program_id(0), pl.program_id(1)
        res = fn(*[r[...] for r in refs[:n_in]])
        for ref, val, kind in zip(refs[n_in:], res, kinds):
            if kind in ("row", "rowc"):
                ref[...] = val.astype(ref.dtype)
            else:
                first = (i == 0) if kind == "accc" else jnp.logical_and(i == 0, j == 0)

                @pl.when(first)
                def _(ref=ref, val=val):
                    ref[...] = val.astype(ref.dtype)

                @pl.when(jnp.logical_not(first))
                def _(ref=ref, val=val):
                    ref[...] += val.astype(ref.dtype)

    out_shape, out_specs = [], []
    for nr, nc, dtype, kind in outs:
        out_shape.append(jax.ShapeDtypeStruct((nr, nc), dtype))
        if kind == "row":
            out_specs.append(pl.BlockSpec((tile, nc), lambda j, i: (i, 0)))
        elif kind == "rowc":
            out_specs.append(pl.BlockSpec((tile, nc // ncol), lambda j, i: (i, j)))
        elif kind == "acc":
            out_specs.append(pl.BlockSpec((nr, nc), lambda j, i: (0, 0)))
        else:
            out_specs.append(pl.BlockSpec((nr, nc // ncol), lambda j, i: (0, j)))
    return pl.pallas_call(
        body, name=name, grid=(ncol, rows // tile), in_specs=[_in_spec(a, k, tile) for a, k in ins],
        out_specs=out_specs, out_shape=out_shape, compiler_params=_params(2))(*[a for a, _ in ins])


def rowvjp(name, fn, ins, cts, grads, *, rows, tile=None, ncol=1, adds=(), dup16=()):
    n_in = len(ins)
    ct_sizes = [len(c) for c in cts]
    flat_cts = [m for c in cts for m in c]
    n_ct = len(flat_cts)

    def wrapped(*vals):
        xs = list(vals[:n_in])
        gs = vals[n_in:n_in + n_ct]
        extra = vals[n_in + n_ct:]

        def f(*dvars):
            full = list(xs)
            for k, v in zip(grads, dvars):
                full[k] = v
            return fn(*full)

        outs, pull = jax.vjp(f, *[xs[k] for k in grads])
        cot, p = [], 0
        for o, size in zip(outs, ct_sizes):
            g = gs[p].astype(f32)
            for q in range(1, size):
                g = g + gs[p + q].astype(f32)
            cot.append(g.astype(o.dtype))
            p += size
        gv = list(pull(tuple(cot)))
        for (pos, _, _), e in zip(adds, extra):
            gv[pos] = gv[pos] + e.astype(gv[pos].dtype)
        return tuple(gv) + tuple(gv[pos] for pos in dup16)

    outs = []
    for k in grads:
        a, kind = ins[k]
        if kind == "row":
            outs.append((rows, a.shape[1] * ncol, f32, "rowc") if ncol > 1 else (rows, a.shape[1], f32, "row"))
        elif kind == "const":
            outs.append((a.shape[0], a.shape[1], f32, "acc"))
        elif kind[0] == "rowc":
            outs.append((rows, kind[1] * ncol, f32, "rowc"))
        else:
            outs.append((a.shape[0], kind[1] * ncol, f32, "accc"))
    for pos in dup16:
        nr, nc, _, kind = outs[pos]
        outs.append((nr, nc, bf16, kind))
    all_ins = list(ins) + flat_cts + [(a, kind) for _, a, kind in adds]
    return rowcall(name, wrapped, all_ins, outs, rows=rows, tile=tile, ncol=ncol)


def fn_rms(x, w):
    return (rms(x, w),)


def make_fn_gconv(n_norm_blocks):
    def fn(p, cw):
        c = cw[3:4] * p
        for jj in range(3):
            c = c + cw[jj:jj + 1] * tshift(p, 3 - jj)
        s = c * jax.nn.sigmoid(c)
        nrm = s * lax.rsqrt(jnp.sum(s * s, axis=-1, keepdims=True) + 1e-6)
        return (jnp.where(pl.program_id(0) < n_norm_blocks, nrm, s),)
    return fn


def fn_ggate(ps0, alog, dtb, e_g, e_b):
    g = -jnp.exp(alog) * softplus(ps0 + dtb)
    beta = jax.nn.sigmoid(ps0)
    return hdot(g, e_g), hdot(beta, e_b)


def fn_gpost(o, z, nw):
    return (rms(o, nw) * (z * jax.nn.sigmoid(z)),)


def fn_lerp(p, mu):
    return (p + (tshift(p, 1) - p) * mu,)


def fn_rprep(pk, psl, w0, a0, k_k, k_a, w2p, a2p, g2):
    g1, g2in = psl[:, :LANES], psl[:, LANES:]
    log_w = -softplus(-(w0 + bdot(jnp.tanh(g1), w2p))) - 0.5
    lw = -jnp.exp(log_w)
    a = jax.nn.sigmoid(a0 + bdot(g1, a2p))
    gate = bdot(jax.nn.sigmoid(g2in), g2)
    kkr = pk * k_k
    kk = kkr / jnp.maximum(jnp.sqrt(seg2sum(kkr * kkr)), 1e-12)
    kmod = pk * (1.0 + (a - 1.0) * k_a)
    return lw, kmod, kk, a, gate


def fn_rpost(y, r, kmod, v, gate, r_k, ln_w, ln_b):
    inv_n = 1.0 / RW_HEAD
    mean = seg2sum(y) * inv_n
    d = y - mean
    var = seg2sum(d * d) * inv_n
    yn = d * lax.rsqrt(var + RW_GN_EPS) * ln_w + ln_b
    bonus = seg2sum(r * kmod * r_k) * v
    return ((yn + bonus) * gate,)


def fn_xcore(q, k, v):
    s = bdot_nt(q, k) * (LANES ** -0.5)
    p = jax.nn.softmax(s, axis=-1)
    return (bdot(p, v),)


def fn_final(h, tgt, w):
    def loss_fn(h, w):
        return 0.5 * jnp.sum(jnp.mean(jnp.square(rms(h, w) - tgt), axis=-1))

    val, (dh, dw) = jax.value_and_grad(loss_fn, argnums=(0, 1))(h, w)
    return dh, dh.astype(bf16), dw, jnp.full((8, LANES), val, f32)


def fn_sumcols(n):
    def fn(x):
        w = x.shape[1] // n
        s = x[:, :w]
        for q in range(1, n):
            s = s + x[:, q * w:(q + 1) * w]
        return (s,)
    return fn


def _tri(c):
    ii = lax.broadcasted_iota(jnp.int32, (c, c), 0)
    jj = lax.broadcasted_iota(jnp.int32, (c, c), 1)
    return ii, jj


def _neumann_raw(m, steps):
    c = m.shape[-1]
    ii, jj = _tri(c)
    eye = (ii == jj).astype(f32)
    t, p = eye + m, m
    for _ in range(steps):
        p = _raw_dot(p, p, "nn", P_INV)
        t = _raw_dot(t, eye + p, "nn", P_INV)
    return t


@functools.partial(jax.custom_vjp, nondiff_argnums=(1,))
def _neumann_inverse(m, steps):
    return _neumann_raw(m, steps)


def _neumann_fwd(m, steps):
    t = _neumann_raw(m, steps)
    return t, t


def _neumann_bwd(steps, t, g):
    return (_raw_dot(_raw_dot(t, g, "tn", P_INV), t, "nt", P_INV),)


_neumann_inverse.defvjp(_neumann_fwd, _neumann_bwd)


def cumsum_rows(x):
    t = x.shape[1]
    ii, jj = _tri(t)
    tri = jnp.broadcast_to((ii >= jj).astype(f32), (x.shape[0], t, t))
    return pdot(tri, x, "nn", P_INV)


def gdn_chunk(q, k, v, gb, bb, s):
    c = q.shape[1]
    ii, jj = _tri(c)
    low = ii >= jj
    gcb = cumsum_rows(gb)
    gl = jnp.sum(gb, axis=1, keepdims=True)
    gc_col = gcb[:, :, :c]
    diff = gc_col - jnp.swapaxes(gc_col, 1, 2)
    decay = jnp.where(low, jnp.exp(jnp.where(low, diff, 0.0)), 0.0)
    qs = q * (q.shape[2] ** -0.5)
    kb = k * bb
    a = jnp.where(ii > jj, pdot(kb, k, "nt", P_BULK) * decay, 0.0)
    t = _neumann_inverse(-a, c.bit_length() - 2)
    eg = jnp.exp(gcb)
    u = pdot(t, v * bb, "nn", P_BULK)
    w = pdot(t, kb * eg, "nn", P_BULK)
    attn = pdot(qs, k, "nt", P_BULK) * decay
    kd = k * jnp.exp(gl - gcb)
    v_new = u - pdot(w, s, "nn", P_BULK)
    o = pdot(qs * eg, s, "nn", P_BULK) + pdot(attn, v_new, "nn", P_BULK)
    s_new = s * jnp.exp(gl) + pdot(kd, v_new, "tn", P_BULK)
    return o, s_new


def wkv_chunk(r, lw, k, v, kk, a, s):
    t = r.shape[1]
    ii, jj = _tri(t)
    lo = lax.broadcasted_iota(jnp.int32, r.shape, 2) < RW_HEAD
    cl = cumsum_rows(lw)
    cl_last = jnp.sum(lw, axis=1, keepdims=True)
    al = -kk * jnp.exp(cl - lw)
    be = (a * kk) * jnp.exp(-cl)
    kt = k * jnp.exp(-cl)
    rt = r * jnp.exp(cl)

    def dot(xa, xb, mode="nn"):
        return pdot(xa, xb, mode, P_BULK)

    def pair(xa, xb, msk):
        m_lo = dot(jnp.where(lo, xa, 0.0), xb, "nt")
        m_hi = dot(jnp.where(lo, 0.0, xa), xb, "nt")
        return jnp.where(msk, m_lo, 0.0), jnp.where(msk, m_hi, 0.0)

    def sel(x_lo, x_hi):
        return jnp.where(lo, x_lo, x_hi)

    ab = pair(al, be, ii > jj)
    ak = pair(al, kt, ii > jj)
    rb = pair(rt, be, ii >= jj)
    rk = pair(rt, kt, ii >= jj)
    x = dot(al, s, "nt") + sel(dot(ak[0], v), dot(ak[1], v))
    steps = t.bit_length() - 2
    u = sel(dot(_neumann_inverse(ab[0], steps), x), dot(_neumann_inverse(ab[1], steps), x))
    y = dot(rt, s, "nt") + sel(dot(rb[0], u) + dot(rk[0], v), dot(rb[1], u) + dot(rk[1], v))
    vi = lax.broadcasted_iota(jnp.int32, s.shape, 1) < RW_HEAD
    ki = lax.broadcasted_iota(jnp.int32, s.shape, 2) < RW_HEAD
    s_new = jnp.where(vi == ki, (s + dot(u, be, "tn") + dot(v, kt, "tn")) * jnp.exp(cl_last), 0.0)
    return y, s_new


def _scan_group(ncol, offs):
    g = SCAN_GROUP
    while g > 1 and (ncol % g or any(o % g for o in offs)):
        g //= 2
    return g


def scan_fwd(name, chunk_fn, ins, *, rows, chunk, ncol):
    n = rows // chunk
    n_in = len(ins)
    grp = _scan_group(ncol, [off for _, off in ins])

    def body(*refs):
        o_ref, st_ref, s_scr = refs[n_in:]

        @pl.when(pl.program_id(1) == 0)
        def _():
            s_scr[...] = jnp.zeros_like(s_scr)

        cols = [slice(b * LANES, (b + 1) * LANES) for b in range(grp)]
        s = s_scr[...]
        st_ref[...] = s
        o, s_new = chunk_fn(*[jnp.stack([r[:, c] for c in cols]) for r in refs[:n_in]], s)
        for b, c in enumerate(cols):
            o_ref[:, c] = o[b]
        s_scr[...] = s_new

    def spec(off):
        return pl.BlockSpec((chunk, grp * LANES), lambda h, c: (c, h + off // grp))

    return pl.pallas_call(
        body, name=name, grid=(ncol // grp, n), in_specs=[spec(off) for _, off in ins],
        out_specs=[spec(0), pl.BlockSpec((grp, None, LANES, LANES), lambda h, c: (h, c, 0, 0))],
        out_shape=[jax.ShapeDtypeStruct((rows, ncol * LANES), f32),
                   jax.ShapeDtypeStruct((ncol, n, LANES, LANES), f32)],
        scratch_shapes=[pltpu.VMEM((grp, LANES, LANES), f32)], compiler_params=_params(2))(*[a for a, _ in ins])


def scan_bwd(name, chunk_fn, ins, states, d_out, d_off, *, rows, chunk, ncol):
    n = rows // chunk
    n_in = len(ins)
    grp = _scan_group(ncol, [off for _, off in ins] + [d_off])

    def body(*refs):
        st_ref, do_ref = refs[n_in:n_in + 2]
        g_refs = refs[n_in + 2:2 * n_in + 2]
        ds_scr = refs[-1]

        @pl.when(pl.program_id(1) == 0)
        def _():
            ds_scr[...] = jnp.zeros_like(ds_scr)

        cols = [slice(b * LANES, (b + 1) * LANES) for b in range(grp)]

        def batch(ref):
            return jnp.stack([ref[:, c] for c in cols])

        _, pull = jax.vjp(chunk_fn, *[batch(r) for r in refs[:n_in]], st_ref[...])
        gs = pull((batch(do_ref), ds_scr[...]))
        for ref, g in zip(g_refs, gs[:n_in]):
            for b, c in enumerate(cols):
                ref[:, c] = g[b]
        ds_scr[...] = gs[n_in]

    def spec(off):
        return pl.BlockSpec((chunk, grp * LANES), lambda h, c: (n - 1 - c, h + off // grp))

    st_spec = pl.BlockSpec((grp, None, LANES, LANES), lambda h, c: (h, n - 1 - c, 0, 0))
    return pl.pallas_call(
        body, name=name, grid=(ncol // grp, n), in_specs=[spec(off) for _, off in ins] + [st_spec, spec(d_off)],
        out_specs=[spec(0)] * n_in, out_shape=[jax.ShapeDtypeStruct((rows, ncol * LANES), f32)] * n_in,
        scratch_shapes=[pltpu.VMEM((grp, LANES, LANES), f32)],
        compiler_params=_params(2))(*[a for a, _ in ins], states, d_out)


def flip_exchange(name, arrs, flips, n_slots, slot_of, src_of, with_self):
    n = len(arrs)
    nf = len(flips)

    def body(*refs):
        ins, outs = refs[:n], refs[n:2 * n]
        send, recv, lsem = refs[2 * n:]
        me = (lax.axis_index("x"), lax.axis_index("y"), lax.axis_index("c"))
        copies = []
        for k in range(n):
            if with_self:
                cp = pltpu.make_async_copy(src_of(ins[k], me), outs[k].at[slot_of(me)], lsem.at[k])
                cp.start()
                copies.append(cp)
            for j, fl in enumerate(flips):
                peer = tuple(1 - m if f else m for m, f in zip(me, fl))
                cp = pltpu.make_async_remote_copy(
                    src_ref=src_of(ins[k], peer), dst_ref=outs[k].at[slot_of(me)], send_sem=send.at[k, j],
                    recv_sem=recv.at[k, j], device_id=peer, device_id_type=MESH)
                cp.start()
                copies.append(cp)
        for cp in copies:
            cp.wait()

    def out_sds(a):
        blk = src_of(jax.ShapeDtypeStruct(a.shape, a.dtype), None)
        return jax.ShapeDtypeStruct((n_slots,) + tuple(blk), a.dtype)

    any_spec = pl.BlockSpec(memory_space=pl.ANY)
    return pl.pallas_call(
        body, name=name, in_specs=[any_spec] * n, out_specs=[any_spec] * n, out_shape=[out_sds(a) for a in arrs],
        scratch_shapes=[pltpu.SemaphoreType.DMA((n, nf)), pltpu.SemaphoreType.DMA((n, nf)),
                        pltpu.SemaphoreType.DMA((n,))],
        compiler_params=pltpu.CompilerParams(has_side_effects=True))(*arrs)


_CHIP_FLIPS = ((1, 0, 0), (0, 1, 0), (1, 1, 0))
_ALL_FLIPS = ((0, 0, 1), (0, 1, 0), (0, 1, 1), (1, 0, 0), (1, 0, 1), (1, 1, 0), (1, 1, 1))


def _whole(ref, pos):
    return ref.shape if pos is None else ref


def _chip_block(ref, pos):
    return ref.shape[1:] if pos is None else ref.at[2 * pos[0] + pos[1]]


def _chip_slot(p):
    return 2 * p[0] + p[1]


def gather_chips(name, arrs):
    return flip_exchange(name, arrs, _CHIP_FLIPS, 4, _chip_slot, _whole, True)


def scatter_chips(name, arrs):
    return flip_exchange(name, arrs, _CHIP_FLIPS, 4, _chip_slot, _chip_block, True)


_HBM = pl.BlockSpec(memory_space=pltpu.HBM)
_SEM = pl.BlockSpec(memory_space=pltpu.SEMAPHORE)
_DATAFLOW = pltpu.SideEffectType.DATAFLOW_SIDE_EFFECTING


def _chip_copies(srcs, lands, send, recv, src_of):
    me = (lax.axis_index("x"), lax.axis_index("y"), lax.axis_index("c"))
    copies = []
    for k, (src, land) in enumerate(zip(srcs, lands)):
        for j, fl in enumerate(_CHIP_FLIPS):
            peer = tuple(1 - m if f else m for m, f in zip(me, fl))
            q = k * len(_CHIP_FLIPS) + j
            copies.append(pltpu.make_async_remote_copy(
                src_ref=src_of(src, peer), dst_ref=land.at[_chip_slot(me)], send_sem=send.at[q],
                recv_sem=recv.at[q], device_id=peer, device_id_type=MESH))
    return copies


def chips_start(name, arrs, src_of):
    n = len(arrs)
    me = _chip_slot((lax.axis_index("x"), lax.axis_index("y")))
    lands = []
    for a in arrs:
        blk = tuple(src_of(jax.ShapeDtypeStruct(a.shape, a.dtype), None))
        own = a if src_of is _whole else lax.dynamic_index_in_dim(a, me, 0, keepdims=False)
        land = lax.dynamic_update_index_in_dim(lax.empty((4,) + blk, a.dtype), own, me, 0)
        lands.append(pltpu.with_memory_space_constraint(land, pltpu.HBM))
    srcs = [pltpu.with_memory_space_constraint(a, pltpu.HBM) for a in arrs]

    def body(*refs):
        send, recv = refs[2 * n], refs[2 * n + 1]
        for cp in _chip_copies(refs[:n], refs[n:2 * n], send, recv, src_of):
            cp.start()
        refs[-1][...] = jnp.zeros_like(refs[-1])

    sems = pltpu.SemaphoreType.DMA((n * len(_CHIP_FLIPS),))
    outs = pl.pallas_call(
        body, name=name, in_specs=[_HBM] * (2 * n),
        out_shape=(sems, sems, *[pltpu.HBM(a.shape, a.dtype) for a in srcs + lands],
                   jax.ShapeDtypeStruct((8, LANES), f32)),
        out_specs=(_SEM, _SEM, *[_HBM] * (2 * n), pl.BlockSpec(memory_space=pltpu.VMEM)),
        input_output_aliases={i: 2 + i for i in range(2 * n)},
        compiler_params=pltpu.CompilerParams(has_side_effects=_DATAFLOW))(*srcs, *lands)
    return (outs[0], outs[1], list(outs[2:2 + n]), list(outs[2 + n:2 + 2 * n]), src_of), outs[-1][0, 0]


def chips_wait(name, state, after):
    send, recv, srcs, lands, src_of = state
    n = len(srcs)

    def body(*refs):
        send, recv = refs[2 * n], refs[2 * n + 1]
        for cp in _chip_copies(refs[:n], refs[n:2 * n], send, recv, src_of):
            cp.wait_send()
            cp.wait_recv()

    outs = pl.pallas_call(
        body, name=name, in_specs=[_HBM] * (2 * n) + [_SEM, _SEM, pl.BlockSpec(memory_space=pl.ANY)],
        out_shape=tuple(pltpu.HBM(a.shape, a.dtype) for a in srcs + lands), out_specs=tuple([_HBM] * (2 * n)),
        input_output_aliases={i: i for i in range(2 * n)},
        compiler_params=pltpu.CompilerParams(has_side_effects=_DATAFLOW))(*srcs, *lands, send, recv, after)
    return list(outs[n:])


def gather_chips_halves(name, arrs):
    n = len(arrs)
    nf = len(_CHIP_FLIPS)
    split = [a.shape[0] % 32 == 0 for a in arrs]

    def body(*refs):
        ins, outs = refs[:n], refs[n:2 * n]
        send1, recv1, send2, recv2, lsem = refs[2 * n:]
        me = (lax.axis_index("x"), lax.axis_index("y"), lax.axis_index("c"))
        sib = (me[0], me[1], 1 - me[2])
        peers = [tuple(1 - m if f else m for m, f in zip(me, fl)) for fl in _CHIP_FLIPS]
        local, first, second = [], [], []
        for k in range(n):
            cp = pltpu.make_async_copy(ins[k], outs[k].at[_chip_slot(me)], lsem.at[k])
            cp.start()
            local.append(cp)
            half = ins[k].shape[0] // 2
            rows = pl.ds(pl.multiple_of(me[2] * half, 16), half) if split[k] else pl.ds(0, ins[k].shape[0])
            for j, peer in enumerate(peers):
                cp = pltpu.make_async_remote_copy(
                    src_ref=ins[k].at[rows], dst_ref=outs[k].at[_chip_slot(me), rows], send_sem=send1.at[k, j],
                    recv_sem=recv1.at[k, j], device_id=peer, device_id_type=MESH)
                cp.start()
                first.append((k, j, rows, cp))
        for k, j, rows, cp in first:
            cp.wait_recv()
            if split[k]:
                got = outs[k].at[_chip_slot(peers[j]), rows]
                fwd = pltpu.make_async_remote_copy(src_ref=got, dst_ref=got, send_sem=send2.at[k, j],
                                                   recv_sem=recv2.at[k, j], device_id=sib, device_id_type=MESH)
                fwd.start()
                second.append(fwd)
        for _, _, _, cp in first:
            cp.wait_send()
        for cp in second:
            cp.wait()
        for cp in local:
            cp.wait()

    any_spec = pl.BlockSpec(memory_space=pl.ANY)
    sems = pltpu.SemaphoreType.DMA((n, nf))
    return pl.pallas_call(
        body, name=name, in_specs=[any_spec] * n, out_specs=[any_spec] * n,
        out_shape=[jax.ShapeDtypeStruct((4,) + a.shape, a.dtype) for a in arrs],
        scratch_shapes=[sems, sems, sems, sems, pltpu.SemaphoreType.DMA((n,))],
        compiler_params=pltpu.CompilerParams(has_side_effects=True))(*arrs)


def swap_sibling(name, arrs):
    outs = flip_exchange(name, arrs, ((0, 0, 1),), 1, lambda p: 0, _whole, False)
    return [o[0] for o in outs]


def gather_all(arrs):
    return flip_exchange("gather_all", arrs, _ALL_FLIPS, 8, lambda p: 4 * p[0] + 2 * p[1] + p[2], _whole, True)


def _row_tile(nr, nc, n_arrays):
    budget = (20 << 20) // (n_arrays * 2 * 4 * max(nc, LANES))
    t = min(nr, budget) // 16 * 16
    while t > 0 and nr % t:
        t -= 16
    return t if t > 0 else nr


def sum_slots(name, x):
    ns, nr, nc = x.shape
    tile = _row_tile(nr, nc, ns + 1)

    def body(x_ref, o_ref):
        s = x_ref[0].astype(f32)
        for q in range(1, ns):
            s = s + x_ref[q].astype(f32)
        o_ref[...] = s

    return pl.pallas_call(
        body, name=name, grid=(nr // tile,), in_specs=[pl.BlockSpec((ns, tile, nc), lambda i: (0, i, 0))],
        out_specs=pl.BlockSpec((tile, nc), lambda i: (i, 0)), out_shape=jax.ShapeDtypeStruct((nr, nc), f32),
        compiler_params=_params(1))(x)


def adamw(name, w, g_parts, m, v):
    _, nr, nc = w.shape
    n_g = len(g_parts)
    tile = _row_tile(nr, nc, 7 + n_g)

    def body(*refs):
        w_ref, m_ref, v_ref = refs[:3]
        g = refs[3][...]
        for r in refs[4:3 + n_g]:
            g = g + r[...]
        g_ref, d_ref, nm_ref, nv_ref = refs[3 + n_g:]
        nm = ADAM_B1 * m_ref[...] + (1.0 - ADAM_B1) * g
        nv = ADAM_B2 * v_ref[...] + (1.0 - ADAM_B2) * jnp.square(g)
        m_hat = nm / (1.0 - ADAM_B1 ** ADAM_STEP)
        v_hat = nv / (1.0 - ADAM_B2 ** ADAM_STEP)
        g_ref[...] = g
        d_ref[...] = -ADAM_LR * (m_hat / (jnp.sqrt(v_hat) + ADAM_EPS) + ADAM_WD * w_ref[...])
        nm_ref[...] = nm
        nv_ref[...] = nv

    spec = pl.BlockSpec((tile, nc), lambda i: (i, 0))
    spec3 = pl.BlockSpec((None, tile, nc), lambda i: (0, i, 0))
    return pl.pallas_call(
        body, name=name, grid=(nr // tile,), in_specs=[spec3] * 3 + [spec] * n_g, out_specs=[spec3] * 4,
        out_shape=[jax.ShapeDtypeStruct((1, nr, nc), f32)] * 4, compiler_params=_params(1))(w, m, v, *g_parts)


def adamw_packed(w, g8, m, v):
    nr, nc = w.shape

    def body(w_ref, g_ref, m_ref, v_ref, go_ref, d_ref, nm_ref, nv_ref):
        g = g_ref[0]
        for q in range(1, 8):
            g = g + g_ref[q]
        nm = ADAM_B1 * m_ref[...] + (1.0 - ADAM_B1) * g
        nv = ADAM_B2 * v_ref[...] + (1.0 - ADAM_B2) * jnp.square(g)
        m_hat = nm / (1.0 - ADAM_B1 ** ADAM_STEP)
        v_hat = nv / (1.0 - ADAM_B2 ** ADAM_STEP)
        go_ref[...] = g
        d_ref[...] = -ADAM_LR * (m_hat / (jnp.sqrt(v_hat) + ADAM_EPS) + ADAM_WD * w_ref[...])
        nm_ref[...] = nm
        nv_ref[...] = nv

    return pl.pallas_call(body, name="adamw_packed", out_shape=[jax.ShapeDtypeStruct((nr, nc), f32)] * 4,
                          compiler_params=pltpu.CompilerParams(vmem_limit_bytes=VMEM_LIMIT))(w, g8, m, v)


def _pack(vectors):
    rows = []
    for a in vectors:
        flat = a.reshape(-1).astype(f32)
        pad = (-flat.shape[0]) % LANES
        rows.append(jnp.pad(flat, (0, pad)).reshape(-1, LANES))
    packed = jnp.concatenate(rows, axis=0)
    return jnp.pad(packed, ((0, (-packed.shape[0]) % 8), (0, 0)))


def _unpack(packed, like):
    out, r = [], 0
    for a in like:
        n = a.size
        nr = -(-n // LANES)
        out.append(packed[r:r + nr].reshape(-1)[:n].reshape(a.shape))
        r += nr
    return out


def kernel(x, mem, mix_norm_w, w_in, dn_conv_w, dn_a_log, dn_dt_bias, dn_norm_w, rw_mu, rw_w0, rw_w2, rw_a0, rw_a2, rw_g2, rw_k_k, rw_k_a, rw_r_k, rw_ln_w, rw_ln_b, w_out, xa_norm_w, mem_norm_w, xa_wq, xa_wk, xa_wv, xa_wo, ffn_norm_w, ffn_w1, ffn_w2, final_norm_w, loss_target, m_mix_norm_w, m_w_in, m_dn_conv_w, m_dn_a_log, m_dn_dt_bias, m_dn_norm_w, m_rw_mu, m_rw_w0, m_rw_w2, m_rw_a0, m_rw_a2, m_rw_g2, m_rw_k_k, m_rw_k_a, m_rw_r_k, m_rw_ln_w, m_rw_ln_b, m_w_out, m_xa_norm_w, m_mem_norm_w, m_xa_wq, m_xa_wk, m_xa_wv, m_xa_wo, m_ffn_norm_w, m_ffn_w1, m_ffn_w2, m_final_norm_w, v_mix_norm_w, v_w_in, v_dn_conv_w, v_dn_a_log, v_dn_dt_bias, v_dn_norm_w, v_rw_mu, v_rw_w0, v_rw_w2, v_rw_a0, v_rw_a2, v_rw_g2, v_rw_k_k, v_rw_k_a, v_rw_r_k, v_rw_ln_w, v_rw_ln_b, v_w_out, v_xa_norm_w, v_mem_norm_w, v_xa_wq, v_xa_wk, v_xa_wv, v_xa_wo, v_ffn_norm_w, v_ffn_w1, v_ffn_w2, v_final_norm_w):
    weights = dict(mix_norm_w=mix_norm_w, w_in=w_in, dn_conv_w=dn_conv_w, dn_a_log=dn_a_log, dn_dt_bias=dn_dt_bias, dn_norm_w=dn_norm_w, rw_mu=rw_mu, rw_w0=rw_w0, rw_w2=rw_w2, rw_a0=rw_a0, rw_a2=rw_a2, rw_g2=rw_g2, rw_k_k=rw_k_k, rw_k_a=rw_k_a, rw_r_k=rw_r_k, rw_ln_w=rw_ln_w, rw_ln_b=rw_ln_b, w_out=w_out, xa_norm_w=xa_norm_w, mem_norm_w=mem_norm_w, xa_wq=xa_wq, xa_wk=xa_wk, xa_wv=xa_wv, xa_wo=xa_wo, ffn_norm_w=ffn_norm_w, ffn_w1=ffn_w1, ffn_w2=ffn_w2, final_norm_w=final_norm_w)
    mom_m = dict(mix_norm_w=m_mix_norm_w, w_in=m_w_in, dn_conv_w=m_dn_conv_w, dn_a_log=m_dn_a_log, dn_dt_bias=m_dn_dt_bias, dn_norm_w=m_dn_norm_w, rw_mu=m_rw_mu, rw_w0=m_rw_w0, rw_w2=m_rw_w2, rw_a0=m_rw_a0, rw_a2=m_rw_a2, rw_g2=m_rw_g2, rw_k_k=m_rw_k_k, rw_k_a=m_rw_k_a, rw_r_k=m_rw_r_k, rw_ln_w=m_rw_ln_w, rw_ln_b=m_rw_ln_b, w_out=m_w_out, xa_norm_w=m_xa_norm_w, mem_norm_w=m_mem_norm_w, xa_wq=m_xa_wq, xa_wk=m_xa_wk, xa_wv=m_xa_wv, xa_wo=m_xa_wo, ffn_norm_w=m_ffn_norm_w, ffn_w1=m_ffn_w1, ffn_w2=m_ffn_w2, final_norm_w=m_final_norm_w)
    mom_v = dict(mix_norm_w=v_mix_norm_w, w_in=v_w_in, dn_conv_w=v_dn_conv_w, dn_a_log=v_dn_a_log, dn_dt_bias=v_dn_dt_bias, dn_norm_w=v_dn_norm_w, rw_mu=v_rw_mu, rw_w0=v_rw_w0, rw_w2=v_rw_w2, rw_a0=v_rw_a0, rw_a2=v_rw_a2, rw_g2=v_rw_g2, rw_k_k=v_rw_k_k, rw_k_a=v_rw_k_a, rw_r_k=v_rw_r_k, rw_ln_w=v_rw_ln_w, rw_ln_b=v_rw_ln_b, w_out=v_w_out, xa_norm_w=v_xa_norm_w, mem_norm_w=v_mem_norm_w, xa_wq=v_xa_wq, xa_wk=v_xa_wk, xa_wv=v_xa_wv, xa_wo=v_xa_wo, ffn_norm_w=v_ffn_norm_w, ffn_w1=v_ffn_w1, ffn_w2=v_ffn_w2, final_norm_w=v_final_norm_w)
    names = list(weights)

    seq, d = x.shape[1], x.shape[2]
    dnw = d // 2
    rww = d - dnw
    nh, nb = dnw // LANES, rww // LANES
    n_mem = mem.shape[1]
    lw_dim, la_dim, lg_dim = rw_w2.shape[1], rw_a2.shape[1], rw_g2.shape[1]
    assert lw_dim + la_dim == LANES and lg_dim == LANES and dnw % LANES == 0 and rww % LANES == 0
    xs, mems, tgt = x[0], mem[0], loss_target[0]

    col_sharded = ("w_in", "xa_wo", "ffn_w1", "dn_conv_w", "rw_w2", "rw_a2", "rw_g2")
    row_sharded = ("w_out", "xa_wq", "xa_wk", "xa_wv", "ffn_w2")
    f32_payload = ("dn_conv_w", "rw_w2", "rw_a2", "rw_g2")
    sharded = col_sharded + row_sharded
    payload = {n: weights[n][0].astype(f32 if n in f32_payload else bf16) for n in sharded}
    first = ("w_in", "dn_conv_w", "rw_w2", "rw_a2", "rw_g2")
    mid = ("w_out", "xa_wq", "xa_wk", "xa_wv", "xa_wo")
    late = ("ffn_w1", "ffn_w2")
    gathered = dict(zip(first, gather_chips_halves("gather_first", [payload[n] for n in first])))
    ordered = lax.optimization_barrier(([gathered[n] for n in first], [payload[n] for n in mid + late]))
    gathered = dict(zip(first, ordered[0]))
    payload.update(zip(mid + late, ordered[1]))
    mid_state, tok_mid = chips_start("gather_mid_start", [payload[n] for n in mid], _whole)
    late_state, tok_late = chips_start("gather_late_start", [payload[n] for n in late], _whole)
    mix_norm_w_in = mix_norm_w + (tok_mid + tok_late)

    def full(n):
        g = gathered[n]
        if n in col_sharded:
            return g.transpose(1, 0, 2).reshape(g.shape[1], 4 * g.shape[2])
        return g.reshape(4 * g.shape[1], g.shape[2])

    w_in_f = full("w_in")
    c_dn = 4 * dnw
    c_rw0 = c_dn + 2 * nh
    c_rw = 3 * rww
    w_main = jnp.concatenate([w_in_f[:, :c_dn], w_in_f[:, c_rw0:c_rw0 + c_rw]], axis=1)
    w_small = jnp.concatenate([w_in_f[:, c_dn:c_rw0], jnp.zeros((d, LANES - 2 * nh), bf16),
                               w_in_f[:, c_rw0 + c_rw:]], axis=1)
    conv_w = full("dn_conv_w")
    w2p = jnp.concatenate([full("rw_w2"), jnp.zeros((la_dim, rww), f32)], axis=0)
    a2p = jnp.concatenate([jnp.zeros((lw_dim, rww), f32), full("rw_a2")], axis=0)
    g2 = full("rw_g2")
    xaw = xa_wq.shape[2]
    nxh = xaw // LANES
    ffn = 4 * ffn_w1.shape[2]

    def lane_row(vec):
        return jnp.pad(vec.reshape(1, -1), ((0, 0), (0, LANES - vec.size)))

    alog_row, dtb_row = lane_row(dn_a_log), lane_row(dn_dt_bias)
    head_of_col = jnp.arange(dnw)[None, :] // LANES
    e_g = (jnp.arange(LANES)[:, None] == head_of_col).astype(f32)
    e_b = (jnp.arange(LANES)[:, None] == head_of_col + nh).astype(f32)
    mu_main, mu_small = rw_mu[:, :c_rw], rw_mu[:, c_rw:]
    r_k_row = rw_r_k.reshape(1, rww)
    fnw = final_norm_w.reshape(1, d)
    qb, kb_, vb, zb = 0, nh, 2 * nh, 3 * nh
    rb0 = 4 * nh
    cc = lambda w_, o_: ("constc", w_, o_)
    rc = lambda o_: ("rowc", LANES, o_)

    (u16,) = rowcall("norm_mix", fn_rms, [(xs, "row"), (mix_norm_w_in, "const")], [(seq, d, bf16, "row")], rows=seq)
    p_main = mm("in_main", u16, w_main)
    p_small = mm("in_small", u16, w_small)

    fn_gconv = make_fn_gconv(2 * nh)
    (qkv,) = rowcall("gdn_conv", fn_gconv, [(p_main, rc(0)), (conv_w, cc(LANES, 0))],
                     [(seq, 3 * dnw, f32, "rowc")], rows=seq, tile=seq, ncol=3 * nh)
    gate_ins = [(p_small, rc(0)), (alog_row, "const"), (dtb_row, "const"), (e_g, "const"), (e_b, "const")]
    g_b, beta_b = rowcall("gdn_gate", fn_ggate, gate_ins, [(seq, dnw, f32, "row")] * 2, rows=seq)
    gdn_ins = [(qkv, qb), (qkv, kb_), (qkv, vb), (g_b, 0), (beta_b, 0)]
    o_raw, gdn_states = scan_fwd("gdn_scan", gdn_chunk, gdn_ins, rows=seq, chunk=GDN_CHUNK, ncol=nh)
    gpost_ins = [(o_raw, rc(0)), (p_main, rc(zb)), (dn_norm_w, "const")]
    (o_dn,) = rowcall("gdn_post", fn_gpost, gpost_ins, [(seq, dnw, bf16, "rowc")], rows=seq, ncol=nh)

    (prw,) = rowcall("rw_lerp_main", fn_lerp, [(p_main, rc(rb0)), (mu_main, cc(LANES, 0))],
                     [(seq, c_rw, f32, "rowc")], rows=seq, tile=seq, ncol=3 * nb)
    (psl,) = rowcall("rw_lerp_small", fn_lerp, [(p_small, rc(1)), (mu_small, cc(LANES, 0))],
                     [(seq, 2 * LANES, f32, "rowc")], rows=seq, tile=seq, ncol=2)
    rprep_ins = [(prw, rc(nb)), (psl, "row"), (rw_w0, cc(LANES, 0)), (rw_a0, cc(LANES, 0)), (rw_k_k, cc(LANES, 0)),
                 (rw_k_a, cc(LANES, 0)), (w2p, cc(LANES, 0)), (a2p, cc(LANES, 0)), (g2, cc(LANES, 0))]
    lw, kmod, kk, a_rw, gate = rowcall("rw_prep", fn_rprep, rprep_ins, [(seq, rww, f32, "rowc")] * 5,
                                        rows=seq, ncol=nb)
    wkv_ins = [(prw, 0), (lw, 0), (kmod, 0), (prw, 2 * nb), (kk, 0), (a_rw, 0)]
    y_rw, wkv_states = scan_fwd("wkv_scan", wkv_chunk, wkv_ins, rows=seq, chunk=WKV_CHUNK, ncol=nb)
    rpost_ins = [(y_rw, rc(0)), (prw, rc(0)), (kmod, rc(0)), (prw, rc(2 * nb)), (gate, rc(0)),
                 (r_k_row, cc(LANES, 0)), (rw_ln_w, cc(LANES, 0)), (rw_ln_b, cc(LANES, 0))]
    (o_rw,) = rowcall("rw_post", fn_rpost, rpost_ins, [(seq, rww, bf16, "rowc")], rows=seq, ncol=nb)

    o_cat = jnp.concatenate([o_dn, o_rw], axis=1)
    gathered.update(zip(mid, chips_wait("gather_mid_wait", mid_state, o_cat)))
    w_out_f, wq_f, wk_f, wv_f, wo_f = full("w_out"), full("xa_wq"), full("xa_wk"), full("xa_wv"), full("xa_wo")
    h1 = mm("out_proj", o_cat, w_out_f, add=xs)

    (hn16,) = rowcall("norm_xa", fn_rms, [(h1, "row"), (xa_norm_w, "const")], [(seq, d, bf16, "row")], rows=seq)
    (mn16,) = rowcall("norm_mem", fn_rms, [(mems, "row"), (mem_norm_w, "const")], [(n_mem, d, bf16, "row")],
                      rows=n_mem)
    q_xa = mm("xa_q", hn16, wq_f)
    k_xa = mm("xa_k", mn16, wk_f)
    v_xa = mm("xa_v", mn16, wv_f)
    xcore_ins = [(q_xa, rc(0)), (k_xa, cc(LANES, 0)), (v_xa, cc(LANES, 0))]
    (o_xa,) = rowcall("xa_core", fn_xcore, xcore_ins, [(seq, xaw, bf16, "rowc")], rows=seq, ncol=nxh)
    h2 = mm("xa_o", o_xa, wo_f, add=h1)

    (fn16,) = rowcall("norm_ffn", fn_rms, [(h2, "row"), (ffn_norm_w, "const")], [(seq, d, bf16, "row")], rows=seq)
    gathered.update(zip(late, chips_wait("gather_late_wait", late_state, fn16)))
    w1_f, w2_f = full("ffn_w1"), full("ffn_w2")
    a1_16, hid16 = mm("ffn_up", fn16, w1_f, epilogue=lambda r: (r, jnp.square(jnp.maximum(r, 0.0))),
                      out_dtypes=(bf16, bf16))
    h3 = mm("ffn_down", hid16, w2_f, add=h2)

    dh3, dh3_16, d_fnw, loss_rows = rowcall(
        "loss_head", fn_final, [(h3, "row"), (tgt, "row"), (fnw, "const")],
        [(seq, d, f32, "row"), (seq, d, bf16, "row"), (1, d, f32, "acc"), (8, LANES, f32, "acc")], rows=seq)

    da1_16 = mm("ffn_down_dx", dh3_16, w2_f, tb=True, extra=[a1_16], out_dtype=bf16,
                epilogue=lambda r, a1: (r * (2.0 * jnp.maximum(a1.astype(f32), 0.0)),))
    g_ffn_w2 = mm("ffn_down_dw", hid16, dh3_16, ta=True, out_dtype=bf16)
    g_ffn_w1 = mm("ffn_up_dw", fn16, da1_16, ta=True, out_dtype=bf16, by_chip=True)
    dfn = mm("ffn_up_dx", da1_16, w1_f, tb=True)

    def by_chip(n, g):
        if g.ndim == 3:
            return g
        if n in col_sharded:
            return g.reshape(g.shape[0], 4, g.shape[1] // 4).transpose(1, 0, 2)
        return g.reshape(4, g.shape[0] // 4, g.shape[1])

    late_g, tok = chips_start("scatter_late_start", [by_chip("ffn_w1", g_ffn_w1), by_chip("ffn_w2", g_ffn_w2)],
                              _chip_block)
    dh2, d_ffn_nw, dh2_16 = rowvjp("norm_ffn_bwd", fn_rms, [(h2, "row"), (ffn_norm_w + tok, "const")],
                                   [[(dfn, "row")]], [0, 1], rows=seq, adds=[(0, dh3, "row")], dup16=[0])

    do_xa = mm("xa_o_dx", dh2_16, wo_f, tb=True)
    g_xa_wo = mm("xa_o_dw", o_xa, dh2_16, ta=True, out_dtype=bf16, by_chip=True)
    dq_xa, dk_xa, dv_xa, dq16 = rowvjp("xa_core_bwd", fn_xcore, xcore_ins, [[(do_xa, rc(0))]], [0, 1, 2],
                                       rows=seq, ncol=nxh, dup16=[0])
    g_xa_wq = mm("xa_q_dw", hn16, dq16, ta=True, out_dtype=bf16)
    dhn = mm("xa_q_dx", dq16, wq_f, tb=True)
    dh1, d_xa_nw, dh1_16 = rowvjp("norm_xa_bwd", fn_rms, [(h1, "row"), (xa_norm_w, "const")], [[(dhn, "row")]],
                                  [0, 1], rows=seq, adds=[(0, dh2, "row")], dup16=[0])
    dk16, dv16 = dk_xa.astype(bf16), dv_xa.astype(bf16)
    g_xa_wk = mm("xa_k_dw", mn16, dk16, ta=True, out_dtype=bf16)
    g_xa_wv = mm("xa_v_dw", mn16, dv16, ta=True, out_dtype=bf16)
    dmn = mm("xa_v_dx", dv16, wv_f, tb=True, add=mm("xa_k_dx", dk16, wk_f, tb=True))
    (d_mem_nw,) = rowvjp("norm_mem_bwd", fn_rms, [(mems, "row"), (mem_norm_w, "const")], [[(dmn, "row")]], [1],
                         rows=n_mem)

    g_w_out = mm("out_proj_dw", o_cat, dh1_16, ta=True, out_dtype=bf16)
    do_cat = mm("out_proj_dx", dh1_16, w_out_f, tb=True)
    mid_grads = dict(w_out=g_w_out, xa_wq=g_xa_wq, xa_wk=g_xa_wk, xa_wv=g_xa_wv, xa_wo=g_xa_wo)
    mid_g, tok = chips_start("scatter_mid_start", [by_chip(n, mid_grads[n]) for n in mid], _chip_block)
    rpost_ins_b = rpost_ins[:5] + [(r_k_row + tok, cc(LANES, 0))] + rpost_ins[6:]

    dy, dr_a, dkmod_a, dv_a, dgate, d_r_k, d_ln_w, d_ln_b = rowvjp(
        "rw_post_bwd", fn_rpost, rpost_ins_b, [[(do_cat, rc(nh))]], [0, 1, 2, 3, 4, 5, 6, 7], rows=seq, ncol=nb)
    dr_b, dlw, dkmod_b, dv_b, dkk, da_rw = scan_bwd("wkv_scan_bwd", wkv_chunk, wkv_ins, wkv_states, dy, 0,
                                                    rows=seq, chunk=WKV_CHUNK, ncol=nb)
    rprep_cts = [[(dlw, rc(0))], [(dkmod_a, rc(0)), (dkmod_b, rc(0))], [(dkk, rc(0))], [(da_rw, rc(0))],
                 [(dgate, rc(0))]]
    dpk, dpsl_parts, d_w0, d_a0, d_k_k, d_k_a, d_w2p, d_a2p, d_g2 = rowvjp(
        "rw_prep_bwd", fn_rprep, rprep_ins, rprep_cts, [0, 1, 2, 3, 4, 5, 6, 7, 8], rows=seq, ncol=nb)
    (dpsl,) = rowcall("rw_prep_sum", fn_sumcols(nb), [(dpsl_parts, "row")], [(seq, 2 * LANES, f32, "row")], rows=seq)

    def lerp_bwd(tag, p, p_off, mu, mu_off, ct_lists, ncol):
        return rowvjp("rw_lerp_bwd_" + tag, fn_lerp, [(p, rc(p_off)), (mu, cc(LANES, mu_off))], [ct_lists], [0, 1],
                      rows=seq, tile=seq, ncol=ncol, dup16=[0])

    _, dmu_r, dpr16 = lerp_bwd("r", p_main, rb0, mu_main, 0, [(dr_a, rc(0)), (dr_b, rc(0))], nb)
    _, dmu_k, dpk16 = lerp_bwd("k", p_main, rb0 + nb, mu_main, nb, [(dpk, rc(0))], nb)
    _, dmu_v, dpv16 = lerp_bwd("v", p_main, rb0 + 2 * nb, mu_main, 2 * nb, [(dv_a, rc(0)), (dv_b, rc(0))], nb)
    _, dmu_s, dps12_16 = lerp_bwd("small", p_small, 1, mu_small, 0, [(dpsl, rc(0))], 2)

    do_raw, dz, d_dn_nw, dz16 = rowvjp("gdn_post_bwd", fn_gpost, gpost_ins, [[(do_cat, rc(0))]], [0, 1, 2],
                                       rows=seq, ncol=nh, dup16=[1])
    dq_g, dk_g, dv_g, dg_b, dbeta_b = scan_bwd("gdn_scan_bwd", gdn_chunk, gdn_ins, gdn_states, do_raw, 0,
                                               rows=seq, chunk=GDN_CHUNK, ncol=nh)
    dps0, d_alog, d_dtb, dps0_16 = rowvjp("gdn_gate_bwd", fn_ggate, gate_ins, [[(dg_b, "row")], [(dbeta_b, "row")]],
                                          [0, 1, 2], rows=seq, dup16=[0])
    dqkv = jnp.concatenate([dq_g, dk_g, dv_g], axis=1)
    _, d_conv_w, dqkv16 = rowvjp("gdn_conv_bwd", fn_gconv, [(p_main, rc(0)), (conv_w, cc(LANES, 0))],
                                 [[(dqkv, rc(0))]], [0, 1], rows=seq, tile=seq, ncol=3 * nh, dup16=[0])

    dp_main16 = jnp.concatenate([dqkv16, dz16, dpr16, dpk16, dpv16], axis=1)
    dp_small16 = jnp.concatenate([dps0_16, dps12_16], axis=1)
    g_w_main = mm("in_main_dw", u16, dp_main16, ta=True, out_dtype=bf16)
    g_w_small = mm("in_small_dw", u16, dp_small16, ta=True, out_dtype=bf16)
    g_w_in = jnp.concatenate([g_w_main[:, :c_dn], g_w_small[:, :2 * nh], g_w_main[:, c_dn:],
                              g_w_small[:, LANES:]], axis=1)
    first_grads = dict(w_in=g_w_in, dn_conv_w=d_conv_w, rw_w2=d_w2p[:lw_dim], rw_a2=d_a2p[lw_dim:], rw_g2=d_g2)
    first_g, tok = chips_start("scatter_first_start", [by_chip(n, first_grads[n]) for n in first], _chip_block)

    du = mm("in_small_dx", dp_small16, w_small, tb=True, add=mm("in_main_dx", dp_main16, w_main, tb=True))
    grad_x, d_mix_nw = rowvjp("norm_mix_bwd", fn_rms, [(xs, "row"), (mix_norm_w + tok, "const")], [[(du, "row")]],
                              [0, 1], rows=seq, adds=[(0, dh1, "row")])
    received = dict(zip(mid, chips_wait("scatter_mid_wait", mid_g, grad_x)))
    received.update(zip(late, chips_wait("scatter_late_wait", late_g, grad_x)))

    out = {}

    def reduce_and_update(tag, group):
        partial = [sum_slots("sum_chips_" + n, received[n]) for n in group]
        other = swap_sibling("swap_sibling_" + tag, partial)
        for n, p_mine, p_other in zip(group, partial, other):
            out[n] = adamw("adamw_" + n, weights[n], [p_mine, p_other], mom_m[n], mom_v[n])

    reduce_and_update("rest", mid + late)
    received.update(zip(first, chips_wait("scatter_first_wait", first_g, out["ffn_w2"][1])))
    reduce_and_update("first", first)

    small_names = [n for n in names if n not in sharded]
    small_local = dict(
        mix_norm_w=d_mix_nw, dn_a_log=d_alog[:, :nh], dn_dt_bias=d_dtb[:, :nh], dn_norm_w=d_dn_nw,
        rw_mu=jnp.concatenate([dmu_r, dmu_k, dmu_v, dmu_s], axis=1), rw_w0=d_w0, rw_a0=d_a0, rw_k_k=d_k_k,
        rw_k_a=d_k_a, rw_r_k=d_r_k, rw_ln_w=d_ln_w, rw_ln_b=d_ln_b, xa_norm_w=d_xa_nw, mem_norm_w=d_mem_nw,
        ffn_norm_w=d_ffn_nw, final_norm_w=d_fnw)
    loss_vec = jnp.where(jnp.arange(LANES) == 0, loss_rows[0], 0.0)
    packed, _ = lax.optimization_barrier((_pack([small_local[n] for n in small_names] + [loss_vec]), out["w_in"][0]))
    (g8,) = gather_all([packed])

    packed_like = [weights[n] for n in small_names] + [loss_vec]
    zero = jnp.zeros((LANES,), f32)
    res = adamw_packed(_pack([weights[n] for n in small_names] + [zero]), g8,
                       _pack([mom_m[n] for n in small_names] + [zero]),
                       _pack([mom_v[n] for n in small_names] + [zero]))
    unpacked = [_unpack(r, packed_like) for r in res]
    for i, n in enumerate(small_names):
        out[n] = [u[i] for u in unpacked]
    loss = unpacked[0][-1][0]

    return (loss, grad_x.reshape(x.shape), *[out[n][0] for n in names], *[out[n][1] for n in names],
            *[out[n][2] for n in names], *[out[n][3] for n in names])
```

```python
import functools

import jax
import jax.numpy as jnp
from jax import lax
from jax.experimental import pallas as pl
from jax.experimental.pallas import tpu as pltpu

f32 = jnp.float32
bf16 = jnp.bfloat16
HI = lax.Precision.HIGHEST
MESH = pl.DeviceIdType.MESH

LANES = 128
VMEM_LIMIT = 56 << 20
TOK_TILE = 256
MM_TILE = 1024
MM_TILE_K = 2048
GDN_CHUNK = 128
WKV_CHUNK = 64
SCAN_GROUP = 8
P_BULK = 1
P_INV = 3
RMS_EPS = 1e-6
RW_GN_EPS = 64e-5
RW_HEAD = 64

ADAM_LR, ADAM_B1, ADAM_B2, ADAM_EPS, ADAM_WD, ADAM_STEP = 0.001, 0.9, 0.999, 1e-08, 0.01, 10


def _params(n_grid):
    return pltpu.CompilerParams(dimension_semantics=("arbitrary",) * n_grid, vmem_limit_bytes=VMEM_LIMIT)


_DIMS = {"nn": (((1,), (0,)), ((), ())), "nt": (((1,), (1,)), ((), ())), "tn": (((0,), (0,)), ((), ()))}
_DIMS_BATCHED = {"nn": (((2,), (1,)), ((0,), (0,))), "nt": (((2,), (2,)), ((0,), (0,))),
                 "tn": (((1,), (1,)), ((0,), (0,)))}


def _raw_dot(a, b, mode, passes):
    dims = (_DIMS if a.ndim == 2 else _DIMS_BATCHED)[mode]
    if passes == 6:
        return lax.dot_general(a.astype(f32), b.astype(f32), dims, precision=HI, preferred_element_type=f32)
    ah, bh = a.astype(bf16), b.astype(bf16)
    r = lax.dot_general(ah, bh, dims, preferred_element_type=f32)
    if passes == 3:
        al = (a - ah.astype(f32)).astype(bf16)
        bl = (b - bh.astype(f32)).astype(bf16)
        r = r + lax.dot_general(al, bh, dims, preferred_element_type=f32)
        r = r + lax.dot_general(ah, bl, dims, preferred_element_type=f32)
    return r


@functools.partial(jax.custom_vjp, nondiff_argnums=(2, 3))
def pdot(a, b, mode, passes):
    return _raw_dot(a, b, mode, passes)


def _pdot_bwd(mode, passes, res, g):
    a, b = res
    if mode == "nn":
        da, db = _raw_dot(g, b, "nt", passes), _raw_dot(a, g, "tn", passes)
    elif mode == "nt":
        da, db = _raw_dot(g, b, "nn", passes), _raw_dot(g, a, "tn", passes)
    else:
        da, db = _raw_dot(b, g, "nt", passes), _raw_dot(a, g, "nn", passes)
    return da.astype(a.dtype), db.astype(b.dtype)


pdot.defvjp(lambda a, b, mode, passes: (_raw_dot(a, b, mode, passes), (a, b)), _pdot_bwd)


def hdot(a, b):
    return pdot(a, b, "nn", 6)


def bdot(a, b):
    return pdot(a, b, "nn", 1)


def bdot_nt(a, b):
    return pdot(a, b, "nt", 1)


def _shift_rows(x, k):
    row = lax.broadcasted_iota(jnp.int32, x.shape, 0)
    return jnp.where(row < k, 0.0, pltpu.roll(x, k, axis=0))


def _unshift_rows(g, k):
    n = g.shape[0]
    row = lax.broadcasted_iota(jnp.int32, g.shape, 0)
    return jnp.where(row >= n - k, 0.0, pltpu.roll(g, n - k, axis=0))


@functools.partial(jax.custom_vjp, nondiff_argnums=(1,))
def tshift(x, k):
    return _shift_rows(x, k)


tshift.defvjp(lambda x, k: (_shift_rows(x, k), None), lambda k, _, g: (_unshift_rows(g, k),))


def rms(x, w):
    x = x.astype(f32)
    return x * lax.rsqrt(jnp.mean(x * x, axis=-1, keepdims=True) + RMS_EPS) * w


def softplus(x):
    return jnp.maximum(x, 0.0) + jnp.log(1.0 + jnp.exp(-jnp.abs(x)))


def seg2sum(x):
    lo = lax.broadcasted_iota(jnp.int32, x.shape, 1) < RW_HEAD
    s_lo = jnp.sum(jnp.where(lo, x, 0.0), axis=-1, keepdims=True)
    s_hi = jnp.sum(jnp.where(lo, 0.0, x), axis=-1, keepdims=True)
    return jnp.where(lo, s_lo, s_hi)


def _tile(n, pref):
    if n <= pref:
        return n
    t = pref
    while t >= LANES:
        if n % t == 0:
            return t
        t -= LANES
    return n


def mm(name, a, b, *, ta=False, tb=False, add=None, out_dtype=f32, by_chip=False, epilogue=None, extra=(),
       out_dtypes=None):
    (k, m) = a.shape if ta else a.shape[::-1]
    (n, kb) = b.shape if tb else b.shape[::-1]
    assert k == kb, (name, a.shape, b.shape)
    tm, tk = _tile(m, MM_TILE), _tile(k, MM_TILE_K)
    tn = _tile(n // 4, MM_TILE) if by_chip else _tile(n, MM_TILE)
    nk = k // tk
    dims = (((0,) if ta else (1,), (1,) if tb else (0,)), ((), ()))
    extra = list(extra) + ([] if add is None else [add])
    out_dtypes = [out_dtype] if out_dtypes is None else list(out_dtypes)
    n_extra, n_out = len(extra), len(out_dtypes)

    def body(*refs):
        a_ref, b_ref = refs[:2]
        x_refs = refs[2:2 + n_extra]
        o_refs = refs[2 + n_extra:2 + n_extra + n_out]
        part = lax.dot_general(a_ref[...].astype(bf16), b_ref[...].astype(bf16), dims, preferred_element_type=f32)

        def finish(r):
            xs = [x[...] for x in x_refs]
            if add is not None:
                r = r + xs.pop().astype(f32)
            outs = (r,) if epilogue is None else epilogue(r, *xs)
            for o_ref, o in zip(o_refs, outs):
                o_ref[...] = o.astype(o_ref.dtype)

        if nk == 1:
            finish(part)
            return
        acc = refs[-1]
        kk = pl.program_id(2)

        @pl.when(kk == 0)
        def _():
            acc[...] = part

        @pl.when(kk > 0)
        def _():
            acc[...] += part

        @pl.when(kk == nk - 1)
        def _():
            finish(acc[...])

    a_spec = pl.BlockSpec((tk, tm), lambda i, j, q: (q, i)) if ta else pl.BlockSpec((tm, tk), lambda i, j, q: (i, q))
    b_spec = pl.BlockSpec((tn, tk), lambda i, j, q: (j, q)) if tb else pl.BlockSpec((tk, tn), lambda i, j, q: (q, j))
    x_spec = pl.BlockSpec((tm, tn), lambda i, j, q: (i, j))
    if by_chip:
        per_chip = n // 4 // tn
        o_spec = pl.BlockSpec((None, tm, tn), lambda i, j, q: (j // per_chip, i, j % per_chip))
        o_shape = (4, m, n // 4)
    else:
        o_spec, o_shape = x_spec, (m, n)
    res = pl.pallas_call(
        body, name=name, grid=(m // tm, n // tn, nk), in_specs=[a_spec, b_spec] + [x_spec] * n_extra,
        out_specs=[o_spec] * n_out, out_shape=[jax.ShapeDtypeStruct(o_shape, dt) for dt in out_dtypes],
        scratch_shapes=[pltpu.VMEM((tm, tn), f32)] if nk > 1 else [], compiler_params=_params(3))(a, b, *extra)
    return res[0] if n_out == 1 else res


REGROUP_PIECES = 4


REGROUP_FIELDS = 5


def _regroup_table(n_out, sources_of):
    import numpy as np
    tbl = np.zeros((REGROUP_FIELDS * REGROUP_PIECES, n_out), np.int32)
    for j in range(n_out):
        groups = sorted(sources_of(j).items())
        assert len(groups) <= REGROUP_PIECES, (j, len(groups))
        for p in range(REGROUP_PIECES):
            if p < len(groups):
                key, lanes = groups[p]
                shifts = {q - s for s, q in lanes}
                qs = sorted(q for _, q in lanes)
                assert len(shifts) == 1 and qs == list(range(qs[0], qs[-1] + 1)), (j, key)
                row = (key[0], key[1], shifts.pop(), qs[0], qs[-1] + 1)
            else:
                key = groups[0][0] if groups else (0, 0)
                row = (key[0], key[1], 0, 0, 0)
            tbl[REGROUP_FIELDS * p:REGROUP_FIELDS * (p + 1), j] = row
    return jnp.asarray(tbl)


def lane_regroup(name, src, table, src_spec, out_spec, out_shape, grid, src_width):
    def body(tbl, *refs):
        o_ref = refs[REGROUP_PIECES]
        j = pl.program_id(1)
        rows = o_ref.shape[0]
        lane = lax.broadcasted_iota(jnp.int32, (rows, LANES), 1)
        pi = lax.broadcasted_iota(jnp.int32, (LANES, LANES), 0)
        qi = lax.broadcasted_iota(jnp.int32, (LANES, LANES), 1)
        acc = jnp.zeros((rows, LANES), f32)
        for p in range(REGROUP_PIECES):
            blk, shift, lo, hi = (tbl[REGROUP_FIELDS * p + f, j] for f in range(1, REGROUP_FIELDS))
            x = jnp.where(lane < src_width - blk * LANES, refs[p][...], jnp.zeros((), src.dtype))
            sel = jnp.logical_and(qi - pi == shift, jnp.logical_and(qi >= lo, qi < hi))
            acc = acc + jnp.dot(x, sel.astype(src.dtype), preferred_element_type=f32)
        o_ref[...] = acc.astype(o_ref.dtype)

    return pl.pallas_call(
        body, name=name, out_shape=jax.ShapeDtypeStruct(out_shape, src.dtype),
        grid_spec=pltpu.PrefetchScalarGridSpec(
            num_scalar_prefetch=1, grid=grid, in_specs=[src_spec(p) for p in range(REGROUP_PIECES)],
            out_specs=out_spec),
        compiler_params=_params(2))(table, *[src] * REGROUP_PIECES)


def w_in_layout(d_cols, shard_w, c_split, gap, n_blocks):
    def ext_of(c):
        return c if c < c_split else c + gap

    def fwd_sources(j):
        groups = {}
        for q in range(LANES):
            e = j * LANES + q
            c = e if e < c_split else e - gap
            if (c_split <= e < c_split + gap) or c >= d_cols:
                continue
            s, l = divmod(c, shard_w)
            groups.setdefault((s, l // LANES), []).append((l % LANES, q))
        return groups

    per_shard = -(-shard_w // LANES)

    def bwd_sources(j):
        s, b = divmod(j, per_shard)
        groups = {}
        for q in range(LANES):
            l = b * LANES + q
            if l >= shard_w:
                continue
            e = ext_of(s * shard_w + l)
            groups.setdefault((0, e // LANES), []).append((e % LANES, q))
        return groups

    return _regroup_table(n_blocks, fwd_sources), _regroup_table(4 * per_shard, bwd_sources), per_shard


def _in_spec(a, kind, tile):
    if kind == "row":
        return pl.BlockSpec((tile, a.shape[1]), lambda j, i: (i, 0))
    if kind == "const":
        return pl.BlockSpec(a.shape, lambda j, i: (0, 0))
    tag, cw, off = kind
    if tag == "rowc":
        return pl.BlockSpec((tile, cw), lambda j, i: (i, j + off))
    assert tag == "constc", kind
    return pl.BlockSpec((a.shape[0], cw), lambda j, i: (0, j + off))


def rowcall(name, fn, ins, outs, *, rows, tile=None, ncol=1):
    tile = min(TOK_TILE, rows) if tile is None else tile
    n_in = len(ins)
    kinds = [o[3] for o in outs]

    def body(*refs):
        j, i = pl.program_id(0), pl.program_id(1)
        res = fn(*[r[...] for r in refs[:n_in]])
        for ref, val, kind in zip(refs[n_in:], res, kinds):
            if kind in ("row", "rowc"):
                ref[...] = val.astype(ref.dtype)
            else:
                first = (i == 0) if kind == "accc" else jnp.logical_and(i == 0, j == 0)

                @pl.when(first)
                def _(ref=ref, val=val):
                    ref[...] = val.astype(ref.dtype)

                @pl.when(jnp.logical_not(first))
                def _(ref=ref, val=val):
                    ref[...] += val.astype(ref.dtype)

    out_shape, out_specs = [], []
    for nr, nc, dtype, kind in outs:
        out_shape.append(jax.ShapeDtypeStruct((nr, nc), dtype))
        if kind == "row":
            out_specs.append(pl.BlockSpec((tile, nc), lambda j, i: (i, 0)))
        elif kind == "rowc":
            out_specs.append(pl.BlockSpec((tile, nc // ncol), lambda j, i: (i, j)))
        elif kind == "acc":
            out_specs.append(pl.BlockSpec((nr, nc), lambda j, i: (0, 0)))
        else:
            out_specs.append(pl.BlockSpec((nr, nc // ncol), lambda j, i: (0, j)))
    return pl.pallas_call(
        body, name=name, grid=(ncol, rows // tile), in_specs=[_in_spec(a, k, tile) for a, k in ins],
        out_specs=out_specs, out_shape=out_shape, compiler_params=_params(2))(*[a for a, _ in ins])


def rowvjp(name, fn, ins, cts, grads, *, rows, tile=None, ncol=1, adds=(), dup16=()):
    n_in = len(ins)
    ct_sizes = [len(c) for c in cts]
    flat_cts = [m for c in cts for m in c]
    n_ct = len(flat_cts)

    def wrapped(*vals):
        xs = list(vals[:n_in])
        gs = vals[n_in:n_in + n_ct]
        extra = vals[n_in + n_ct:]

        def f(*dvars):
            full = list(xs)
            for k, v in zip(grads, dvars):
                full[k] = v
            return fn(*full)

        outs, pull = jax.vjp(f, *[xs[k] for k in grads])
        cot, p = [], 0
        for o, size in zip(outs, ct_sizes):
            g = gs[p].astype(f32)
            for q in range(1, size):
                g = g + gs[p + q].astype(f32)
            cot.append(g.astype(o.dtype))
            p += size
        gv = list(pull(tuple(cot)))
        for (pos, _, _), e in zip(adds, extra):
            gv[pos] = gv[pos] + e.astype(gv[pos].dtype)
        return tuple(gv) + tuple(gv[pos] for pos in dup16)

    outs = []
    for k in grads:
        a, kind = ins[k]
        if kind == "row":
            outs.append((rows, a.shape[1] * ncol, f32, "rowc") if ncol > 1 else (rows, a.shape[1], f32, "row"))
        elif kind == "const":
            outs.append((a.shape[0], a.shape[1], f32, "acc"))
        elif kind[0] == "rowc":
            outs.append((rows, kind[1] * ncol, f32, "rowc"))
        else:
            outs.append((a.shape[0], kind[1] * ncol, f32, "accc"))
    for pos in dup16:
        nr, nc, _, kind = outs[pos]
        outs.append((nr, nc, bf16, kind))
    all_ins = list(ins) + flat_cts + [(a, kind) for _, a, kind in adds]
    return rowcall(name, wrapped, all_ins, outs, rows=rows, tile=tile, ncol=ncol)


def fn_rms(x, w):
    return (rms(x, w),)


def make_fn_gconv(n_norm_blocks):
    def fn(p, cw):
        c = cw[3:4] * p
        for jj in range(3):
            c = c + cw[jj:jj + 1] * tshift(p, 3 - jj)
        s = c * jax.nn.sigmoid(c)
        nrm = s * lax.rsqrt(jnp.sum(s * s, axis=-1, keepdims=True) + 1e-6)
        return (jnp.where(pl.program_id(0) < n_norm_blocks, nrm, s),)
    return fn


def fn_ggate(ps0, alog, dtb, e_g, e_b):
    g = -jnp.exp(alog) * softplus(ps0 + dtb)
    beta = jax.nn.sigmoid(ps0)
    return hdot(g, e_g), hdot(beta, e_b)


def fn_gpost(o, z, nw):
    return (rms(o, nw) * (z * jax.nn.sigmoid(z)),)


def fn_lerp(p, mu):
    return (p + (tshift(p, 1) - p) * mu,)


def fn_rprep(pk, psl, w0, a0, k_k, k_a, w2p, a2p, g2):
    g1, g2in = psl[:, :LANES], psl[:, LANES:]
    log_w = -softplus(-(w0 + bdot(jnp.tanh(g1), w2p))) - 0.5
    lw = -jnp.exp(log_w)
    a = jax.nn.sigmoid(a0 + bdot(g1, a2p))
    gate = bdot(jax.nn.sigmoid(g2in), g2)
    kkr = pk * k_k
    kk = kkr / jnp.maximum(jnp.sqrt(seg2sum(kkr * kkr)), 1e-12)
    kmod = pk * (1.0 + (a - 1.0) * k_a)
    return lw, kmod, kk, a, gate


def fn_rpost(y, r, kmod, v, gate, r_k, ln_w, ln_b):
    inv_n = 1.0 / RW_HEAD
    mean = seg2sum(y) * inv_n
    d = y - mean
    var = seg2sum(d * d) * inv_n
    yn = d * lax.rsqrt(var + RW_GN_EPS) * ln_w + ln_b
    bonus = seg2sum(r * kmod * r_k) * v
    return ((yn + bonus) * gate,)


def fn_xcore(q, k, v):
    s = bdot_nt(q, k) * (LANES ** -0.5)
    p = jax.nn.softmax(s, axis=-1)
    return (bdot(p, v),)


def fn_final(h, tgt, w):
    def loss_fn(h, w):
        return 0.5 * jnp.sum(jnp.mean(jnp.square(rms(h, w) - tgt), axis=-1))

    val, (dh, dw) = jax.value_and_grad(loss_fn, argnums=(0, 1))(h, w)
    return dh, dh.astype(bf16), dw, jnp.full((8, LANES), val, f32)


def fn_sumcols(n):
    def fn(x):
        w = x.shape[1] // n
        s = x[:, :w]
        for q in range(1, n):
            s = s + x[:, q * w:(q + 1) * w]
        return (s,)
    return fn


def _tri(c):
    ii = lax.broadcasted_iota(jnp.int32, (c, c), 0)
    jj = lax.broadcasted_iota(jnp.int32, (c, c), 1)
    return ii, jj


def _neumann_raw(m, steps):
    c = m.shape[-1]
    ii, jj = _tri(c)
    eye = (ii == jj).astype(f32)
    t, p = eye + m, m
    for _ in range(steps):
        p = _raw_dot(p, p, "nn", P_INV)
        t = _raw_dot(t, eye + p, "nn", P_INV)
    return t


@functools.partial(jax.custom_vjp, nondiff_argnums=(1,))
def _neumann_inverse(m, steps):
    return _neumann_raw(m, steps)


def _neumann_fwd(m, steps):
    t = _neumann_raw(m, steps)
    return t, t


def _neumann_bwd(steps, t, g):
    return (_raw_dot(_raw_dot(t, g, "tn", P_INV), t, "nt", P_INV),)


_neumann_inverse.defvjp(_neumann_fwd, _neumann_bwd)


def cumsum_rows(x):
    t = x.shape[1]
    ii, jj = _tri(t)
    tri = jnp.broadcast_to((ii >= jj).astype(f32), (x.shape[0], t, t))
    return pdot(tri, x, "nn", P_INV)


def gdn_chunk(q, k, v, gb, bb, s):
    c = q.shape[1]
    ii, jj = _tri(c)
    low = ii >= jj
    gcb = cumsum_rows(gb)
    gl = jnp.sum(gb, axis=1, keepdims=True)
    gc_col = gcb[:, :, :c]
    diff = gc_col - jnp.swapaxes(gc_col, 1, 2)
    decay = jnp.where(low, jnp.exp(jnp.where(low, diff, 0.0)), 0.0)
    qs = q * (q.shape[2] ** -0.5)
    kb = k * bb
    a = jnp.where(ii > jj, pdot(kb, k, "nt", P_BULK) * decay, 0.0)
    t = _neumann_inverse(-a, c.bit_length() - 2)
    eg = jnp.exp(gcb)
    u = pdot(t, v * bb, "nn", P_BULK)
    w = pdot(t, kb * eg, "nn", P_BULK)
    attn = pdot(qs, k, "nt", P_BULK) * decay
    kd = k * jnp.exp(gl - gcb)
    v_new = u - pdot(w, s, "nn", P_BULK)
    o = pdot(qs * eg, s, "nn", P_BULK) + pdot(attn, v_new, "nn", P_BULK)
    s_new = s * jnp.exp(gl) + pdot(kd, v_new, "tn", P_BULK)
    return o, s_new


def wkv_chunk(r, lw, k, v, kk, a, s):
    t = r.shape[1]
    ii, jj = _tri(t)
    lo = lax.broadcasted_iota(jnp.int32, r.shape, 2) < RW_HEAD
    cl = cumsum_rows(lw)
    cl_last = jnp.sum(lw, axis=1, keepdims=True)
    al = -kk * jnp.exp(cl - lw)
    be = (a * kk) * jnp.exp(-cl)
    kt = k * jnp.exp(-cl)
    rt = r * jnp.exp(cl)

    def dot(xa, xb, mode="nn"):
        return pdot(xa, xb, mode, P_BULK)

    def pair(xa, xb, msk):
        m_lo = dot(jnp.where(lo, xa, 0.0), xb, "nt")
        m_hi = dot(jnp.where(lo, 0.0, xa), xb, "nt")
        return jnp.where(msk, m_lo, 0.0), jnp.where(msk, m_hi, 0.0)

    def sel(x_lo, x_hi):
        return jnp.where(lo, x_lo, x_hi)

    ab = pair(al, be, ii > jj)
    ak = pair(al, kt, ii > jj)
    rb = pair(rt, be, ii >= jj)
    rk = pair(rt, kt, ii >= jj)
    x = dot(al, s, "nt") + sel(dot(ak[0], v), dot(ak[1], v))
    steps = t.bit_length() - 2
    u = sel(dot(_neumann_inverse(ab[0], steps), x), dot(_neumann_inverse(ab[1], steps), x))
    y = dot(rt, s, "nt") + sel(dot(rb[0], u) + dot(rk[0], v), dot(rb[1], u) + dot(rk[1], v))
    vi = lax.broadcasted_iota(jnp.int32, s.shape, 1) < RW_HEAD
    ki = lax.broadcasted_iota(jnp.int32, s.shape, 2) < RW_HEAD
    s_new = jnp.where(vi == ki, (s + dot(u, be, "tn") + dot(v, kt, "tn")) * jnp.exp(cl_last), 0.0)
    return y, s_new


def _scan_group(ncol, offs):
    g = SCAN_GROUP
    while g > 1 and (ncol % g or any(o % g for o in offs)):
        g //= 2
    return g


def scan_fwd(name, chunk_fn, ins, *, rows, chunk, ncol):
    n = rows // chunk
    n_in = len(ins)
    grp = _scan_group(ncol, [off for _, off in ins])

    def body(*refs):
        o_ref, st_ref, s_scr = refs[n_in:]

        @pl.when(pl.program_id(1) == 0)
        def _():
            s_scr[...] = jnp.zeros_like(s_scr)

        cols = [slice(b * LANES, (b + 1) * LANES) for b in range(grp)]
        s = s_scr[...]
        st_ref[...] = s
        o, s_new = chunk_fn(*[jnp.stack([r[:, c] for c in cols]) for r in refs[:n_in]], s)
        for b, c in enumerate(cols):
            o_ref[:, c] = o[b]
        s_scr[...] = s_new

    def spec(off):
        return pl.BlockSpec((chunk, grp * LANES), lambda h, c: (c, h + off // grp))

    return pl.pallas_call(
        body, name=name, grid=(ncol // grp, n), in_specs=[spec(off) for _, off in ins],
        out_specs=[spec(0), pl.BlockSpec((grp, None, LANES, LANES), lambda h, c: (h, c, 0, 0))],
        out_shape=[jax.ShapeDtypeStruct((rows, ncol * LANES), f32),
                   jax.ShapeDtypeStruct((ncol, n, LANES, LANES), f32)],
        scratch_shapes=[pltpu.VMEM((grp, LANES, LANES), f32)], compiler_params=_params(2))(*[a for a, _ in ins])


def scan_bwd(name, chunk_fn, ins, states, d_out, d_off, *, rows, chunk, ncol):
    n = rows // chunk
    n_in = len(ins)
    grp = _scan_group(ncol, [off for _, off in ins] + [d_off])

    def body(*refs):
        st_ref, do_ref = refs[n_in:n_in + 2]
        g_refs = refs[n_in + 2:2 * n_in + 2]
        ds_scr = refs[-1]

        @pl.when(pl.program_id(1) == 0)
        def _():
            ds_scr[...] = jnp.zeros_like(ds_scr)

        cols = [slice(b * LANES, (b + 1) * LANES) for b in range(grp)]

        def batch(ref):
            return jnp.stack([ref[:, c] for c in cols])

        _, pull = jax.vjp(chunk_fn, *[batch(r) for r in refs[:n_in]], st_ref[...])
        gs = pull((batch(do_ref), ds_scr[...]))
        for ref, g in zip(g_refs, gs[:n_in]):
            for b, c in enumerate(cols):
                ref[:, c] = g[b]
        ds_scr[...] = gs[n_in]

    def spec(off):
        return pl.BlockSpec((chunk, grp * LANES), lambda h, c: (n - 1 - c, h + off // grp))

    st_spec = pl.BlockSpec((grp, None, LANES, LANES), lambda h, c: (h, n - 1 - c, 0, 0))
    return pl.pallas_call(
        body, name=name, grid=(ncol // grp, n), in_specs=[spec(off) for _, off in ins] + [st_spec, spec(d_off)],
        out_specs=[spec(0)] * n_in, out_shape=[jax.ShapeDtypeStruct((rows, ncol * LANES), f32)] * n_in,
        scratch_shapes=[pltpu.VMEM((grp, LANES, LANES), f32)],
        compiler_params=_params(2))(*[a for a, _ in ins], states, d_out)


def flip_exchange(name, arrs, flips, n_slots, slot_of, src_of, with_self):
    n = len(arrs)
    nf = len(flips)

    def body(*refs):
        ins, outs = refs[:n], refs[n:2 * n]
        send, recv, lsem = refs[2 * n:]
        me = (lax.axis_index("x"), lax.axis_index("y"), lax.axis_index("c"))
        copies = []
        for k in range(n):
            if with_self:
                cp = pltpu.make_async_copy(src_of(ins[k], me), outs[k].at[slot_of(me)], lsem.at[k])
                cp.start()
                copies.append(cp)
            for j, fl in enumerate(flips):
                peer = tuple(1 - m if f else m for m, f in zip(me, fl))
                cp = pltpu.make_async_remote_copy(
                    src_ref=src_of(ins[k], peer), dst_ref=outs[k].at[slot_of(me)], send_sem=send.at[k, j],
                    recv_sem=recv.at[k, j], device_id=peer, device_id_type=MESH)
                cp.start()
                copies.append(cp)
        for cp in copies:
            cp.wait()

    def out_sds(a):
        blk = src_of(jax.ShapeDtypeStruct(a.shape, a.dtype), None)
        return jax.ShapeDtypeStruct((n_slots,) + tuple(blk), a.dtype)

    any_spec = pl.BlockSpec(memory_space=pl.ANY)
    return pl.pallas_call(
        body, name=name, in_specs=[any_spec] * n, out_specs=[any_spec] * n, out_shape=[out_sds(a) for a in arrs],
        scratch_shapes=[pltpu.SemaphoreType.DMA((n, nf)), pltpu.SemaphoreType.DMA((n, nf)),
                        pltpu.SemaphoreType.DMA((n,))],
        compiler_params=pltpu.CompilerParams(has_side_effects=True))(*arrs)


_CHIP_FLIPS = ((1, 0, 0), (0, 1, 0), (1, 1, 0))
_ALL_FLIPS = ((0, 0, 1), (0, 1, 0), (0, 1, 1), (1, 0, 0), (1, 0, 1), (1, 1, 0), (1, 1, 1))


def _whole(ref, pos):
    return ref.shape if pos is None else ref


def _chip_block(ref, pos):
    return ref.shape[1:] if pos is None else ref.at[2 * pos[0] + pos[1]]


def _chip_slot(p):
    return 2 * p[0] + p[1]


def gather_chips(name, arrs):
    return flip_exchange(name, arrs, _CHIP_FLIPS, 4, _chip_slot, _whole, True)


def scatter_chips(name, arrs):
    return flip_exchange(name, arrs, _CHIP_FLIPS, 4, _chip_slot, _chip_block, True)


_HBM = pl.BlockSpec(memory_space=pltpu.HBM)
_SEM = pl.BlockSpec(memory_space=pltpu.SEMAPHORE)
_DATAFLOW = pltpu.SideEffectType.DATAFLOW_SIDE_EFFECTING


def _chip_copies(srcs, lands, send, recv, src_of):
    me = (lax.axis_index("x"), lax.axis_index("y"), lax.axis_index("c"))
    copies = []
    for k, (src, land) in enumerate(zip(srcs, lands)):
        for j, fl in enumerate(_CHIP_FLIPS):
            peer = tuple(1 - m if f else m for m, f in zip(me, fl))
            q = k * len(_CHIP_FLIPS) + j
            copies.append(pltpu.make_async_remote_copy(
                src_ref=src_of(src, peer), dst_ref=land.at[_chip_slot(me)], send_sem=send.at[q],
                recv_sem=recv.at[q], device_id=peer, device_id_type=MESH))
    return copies


def chips_start(name, arrs, src_of):
    n = len(arrs)
    me = _chip_slot((lax.axis_index("x"), lax.axis_index("y")))
    lands = []
    for a in arrs:
        blk = tuple(src_of(jax.ShapeDtypeStruct(a.shape, a.dtype), None))
        own = a if src_of is _whole else lax.dynamic_index_in_dim(a, me, 0, keepdims=False)
        land = lax.dynamic_update_index_in_dim(lax.empty((4,) + blk, a.dtype), own, me, 0)
        lands.append(pltpu.with_memory_space_constraint(land, pltpu.HBM))
    srcs = [pltpu.with_memory_space_constraint(a, pltpu.HBM) for a in arrs]

    def body(*refs):
        send, recv = refs[2 * n], refs[2 * n + 1]
        for cp in _chip_copies(refs[:n], refs[n:2 * n], send, recv, src_of):
            cp.start()
        refs[-1][...] = jnp.zeros_like(refs[-1])

    sems = pltpu.SemaphoreType.DMA((n * len(_CHIP_FLIPS),))
    outs = pl.pallas_call(
        body, name=name, in_specs=[_HBM] * (2 * n),
        out_shape=(sems, sems, *[pltpu.HBM(a.shape, a.dtype) for a in srcs + lands],
                   jax.ShapeDtypeStruct((8, LANES), f32)),
        out_specs=(_SEM, _SEM, *[_HBM] * (2 * n), pl.BlockSpec(memory_space=pltpu.VMEM)),
        input_output_aliases={i: 2 + i for i in range(2 * n)},
        compiler_params=pltpu.CompilerParams(has_side_effects=_DATAFLOW))(*srcs, *lands)
    return (outs[0], outs[1], list(outs[2:2 + n]), list(outs[2 + n:2 + 2 * n]), src_of), outs[-1][0, 0]


def chips_wait(name, state, after):
    send, recv, srcs, lands, src_of = state
    n = len(srcs)

    def body(*refs):
        send, recv = refs[2 * n], refs[2 * n + 1]
        for cp in _chip_copies(refs[:n], refs[n:2 * n], send, recv, src_of):
            cp.wait_send()
            cp.wait_recv()

    outs = pl.pallas_call(
        body, name=name, in_specs=[_HBM] * (2 * n) + [_SEM, _SEM, pl.BlockSpec(memory_space=pl.ANY)],
        out_shape=tuple(pltpu.HBM(a.shape, a.dtype) for a in srcs + lands), out_specs=tuple([_HBM] * (2 * n)),
        input_output_aliases={i: i for i in range(2 * n)},
        compiler_params=pltpu.CompilerParams(has_side_effects=_DATAFLOW))(*srcs, *lands, send, recv, after)
    return list(outs[n:])


def gather_chips_halves(name, arrs):
    n = len(arrs)
    nf = len(_CHIP_FLIPS)
    split = [a.shape[0] % 32 == 0 for a in arrs]

    def body(*refs):
        ins, outs = refs[:n], refs[n:2 * n]
        send1, recv1, send2, recv2, lsem = refs[2 * n:]
        me = (lax.axis_index("x"), lax.axis_index("y"), lax.axis_index("c"))
        sib = (me[0], me[1], 1 - me[2])
        peers = [tuple(1 - m if f else m for m, f in zip(me, fl)) for fl in _CHIP_FLIPS]
        local, first, second = [], [], []
        for k in range(n):
            cp = pltpu.make_async_copy(ins[k], outs[k].at[_chip_slot(me)], lsem.at[k])
            cp.start()
            local.append(cp)
            half = ins[k].shape[0] // 2
            rows = pl.ds(pl.multiple_of(me[2] * half, 16), half) if split[k] else pl.ds(0, ins[k].shape[0])
            for j, peer in enumerate(peers):
                cp = pltpu.make_async_remote_copy(
                    src_ref=ins[k].at[rows], dst_ref=outs[k].at[_chip_slot(me), rows], send_sem=send1.at[k, j],
                    recv_sem=recv1.at[k, j], device_id=peer, device_id_type=MESH)
                cp.start()
                first.append((k, j, rows, cp))
        for k, j, rows, cp in first:
            cp.wait_recv()
            if split[k]:
                got = outs[k].at[_chip_slot(peers[j]), rows]
                fwd = pltpu.make_async_remote_copy(src_ref=got, dst_ref=got, send_sem=send2.at[k, j],
                                                   recv_sem=recv2.at[k, j], device_id=sib, device_id_type=MESH)
                fwd.start()
                second.append(fwd)
        for _, _, _, cp in first:
            cp.wait_send()
        for cp in second:
            cp.wait()
        for cp in local:
            cp.wait()

    any_spec = pl.BlockSpec(memory_space=pl.ANY)
    sems = pltpu.SemaphoreType.DMA((n, nf))
    return pl.pallas_call(
        body, name=name, in_specs=[any_spec] * n, out_specs=[any_spec] * n,
        out_shape=[jax.ShapeDtypeStruct((4,) + a.shape, a.dtype) for a in arrs],
        scratch_shapes=[sems, sems, sems, sems, pltpu.SemaphoreType.DMA((n,))],
        compiler_params=pltpu.CompilerParams(has_side_effects=True))(*arrs)


def swap_sibling(name, arrs):
    outs = flip_exchange(name, arrs, ((0, 0, 1),), 1, lambda p: 0, _whole, False)
    return [o[0] for o in outs]


def gather_all(arrs):
    return flip_exchange("gather_all", arrs, _ALL_FLIPS, 8, lambda p: 4 * p[0] + 2 * p[1] + p[2], _whole, True)


def _row_tile(nr, nc, n_arrays):
    budget = (20 << 20) // (n_arrays * 2 * 4 * max(nc, LANES))
    t = min(nr, budget) // 16 * 16
    while t > 0 and nr % t:
        t -= 16
    return t if t > 0 else nr


def sum_slots(name, x):
    ns, nr, nc = x.shape
    tile = _row_tile(nr, nc, ns + 1)

    def body(x_ref, o_ref):
        s = x_ref[0].astype(f32)
        for q in range(1, ns):
            s = s + x_ref[q].astype(f32)
        o_ref[...] = s

    return pl.pallas_call(
        body, name=name, grid=(nr // tile,), in_specs=[pl.BlockSpec((ns, tile, nc), lambda i: (0, i, 0))],
        out_specs=pl.BlockSpec((tile, nc), lambda i: (i, 0)), out_shape=jax.ShapeDtypeStruct((nr, nc), f32),
        compiler_params=_params(1))(x)


def adamw(name, w, g_parts, m, v):
    nr, nc = w.shape[-2:]
    n_g = len(g_parts)
    tile = _row_tile(nr, nc, 7 + n_g)

    def body(*refs):
        w_ref, m_ref, v_ref = refs[:3]
        g = refs[3][...]
        for r in refs[4:3 + n_g]:
            g = g + r[...]
        g_ref, d_ref, nm_ref, nv_ref = refs[3 + n_g:]
        nm = ADAM_B1 * m_ref[...] + (1.0 - ADAM_B1) * g
        nv = ADAM_B2 * v_ref[...] + (1.0 - ADAM_B2) * jnp.square(g)
        m_hat = nm / (1.0 - ADAM_B1 ** ADAM_STEP)
        v_hat = nv / (1.0 - ADAM_B2 ** ADAM_STEP)
        g_ref[...] = g
        d_ref[...] = -ADAM_LR * (m_hat / (jnp.sqrt(v_hat) + ADAM_EPS) + ADAM_WD * w_ref[...])
        nm_ref[...] = nm
        nv_ref[...] = nv

    spec = pl.BlockSpec((tile, nc), lambda i: (i, 0))
    spec3 = pl.BlockSpec((None, tile, nc), lambda i: (0, i, 0)) if w.ndim == 3 else spec
    return pl.pallas_call(
        body, name=name, grid=(nr // tile,), in_specs=[spec3] * 3 + [spec] * n_g, out_specs=[spec3] * 4,
        out_shape=[jax.ShapeDtypeStruct(w.shape, f32)] * 4, compiler_params=_params(1))(w, m, v, *g_parts)


def adamw_packed(w, g8, m, v):
    nr, nc = w.shape

    def body(w_ref, g_ref, m_ref, v_ref, go_ref, d_ref, nm_ref, nv_ref):
        g = g_ref[0]
        for q in range(1, 8):
            g = g + g_ref[q]
        nm = ADAM_B1 * m_ref[...] + (1.0 - ADAM_B1) * g
        nv = ADAM_B2 * v_ref[...] + (1.0 - ADAM_B2) * jnp.square(g)
        m_hat = nm / (1.0 - ADAM_B1 ** ADAM_STEP)
        v_hat = nv / (1.0 - ADAM_B2 ** ADAM_STEP)
        go_ref[...] = g
        d_ref[...] = -ADAM_LR * (m_hat / (jnp.sqrt(v_hat) + ADAM_EPS) + ADAM_WD * w_ref[...])
        nm_ref[...] = nm
        nv_ref[...] = nv

    return pl.pallas_call(body, name="adamw_packed", out_shape=[jax.ShapeDtypeStruct((nr, nc), f32)] * 4,
                          compiler_params=pltpu.CompilerParams(vmem_limit_bytes=VMEM_LIMIT))(w, g8, m, v)


def _pack(vectors):
    rows = []
    for a in vectors:
        flat = a.reshape(-1).astype(f32)
        pad = (-flat.shape[0]) % LANES
        rows.append(jnp.pad(flat, (0, pad)).reshape(-1, LANES))
    packed = jnp.concatenate(rows, axis=0)
    return jnp.pad(packed, ((0, (-packed.shape[0]) % 8), (0, 0)))


def _unpack(packed, like):
    out, r = [], 0
    for a in like:
        n = a.size
        nr = -(-n // LANES)
        out.append(packed[r:r + nr].reshape(-1)[:n].reshape(a.shape))
        r += nr
    return out


def kernel(x, mem, mix_norm_w, w_in, dn_conv_w, dn_a_log, dn_dt_bias, dn_norm_w, rw_mu, rw_w0, rw_w2, rw_a0, rw_a2, rw_g2, rw_k_k, rw_k_a, rw_r_k, rw_ln_w, rw_ln_b, w_out, xa_norm_w, mem_norm_w, xa_wq, xa_wk, xa_wv, xa_wo, ffn_norm_w, ffn_w1, ffn_w2, final_norm_w, loss_target, m_mix_norm_w, m_w_in, m_dn_conv_w, m_dn_a_log, m_dn_dt_bias, m_dn_norm_w, m_rw_mu, m_rw_w0, m_rw_w2, m_rw_a0, m_rw_a2, m_rw_g2, m_rw_k_k, m_rw_k_a, m_rw_r_k, m_rw_ln_w, m_rw_ln_b, m_w_out, m_xa_norm_w, m_mem_norm_w, m_xa_wq, m_xa_wk, m_xa_wv, m_xa_wo, m_ffn_norm_w, m_ffn_w1, m_ffn_w2, m_final_norm_w, v_mix_norm_w, v_w_in, v_dn_conv_w, v_dn_a_log, v_dn_dt_bias, v_dn_norm_w, v_rw_mu, v_rw_w0, v_rw_w2, v_rw_a0, v_rw_a2, v_rw_g2, v_rw_k_k, v_rw_k_a, v_rw_r_k, v_rw_ln_w, v_rw_ln_b, v_w_out, v_xa_norm_w, v_mem_norm_w, v_xa_wq, v_xa_wk, v_xa_wv, v_xa_wo, v_ffn_norm_w, v_ffn_w1, v_ffn_w2, v_final_norm_w):
    weights = dict(mix_norm_w=mix_norm_w, w_in=w_in, dn_conv_w=dn_conv_w, dn_a_log=dn_a_log, dn_dt_bias=dn_dt_bias, dn_norm_w=dn_norm_w, rw_mu=rw_mu, rw_w0=rw_w0, rw_w2=rw_w2, rw_a0=rw_a0, rw_a2=rw_a2, rw_g2=rw_g2, rw_k_k=rw_k_k, rw_k_a=rw_k_a, rw_r_k=rw_r_k, rw_ln_w=rw_ln_w, rw_ln_b=rw_ln_b, w_out=w_out, xa_norm_w=xa_norm_w, mem_norm_w=mem_norm_w, xa_wq=xa_wq, xa_wk=xa_wk, xa_wv=xa_wv, xa_wo=xa_wo, ffn_norm_w=ffn_norm_w, ffn_w1=ffn_w1, ffn_w2=ffn_w2, final_norm_w=final_norm_w)
    mom_m = dict(mix_norm_w=m_mix_norm_w, w_in=m_w_in, dn_conv_w=m_dn_conv_w, dn_a_log=m_dn_a_log, dn_dt_bias=m_dn_dt_bias, dn_norm_w=m_dn_norm_w, rw_mu=m_rw_mu, rw_w0=m_rw_w0, rw_w2=m_rw_w2, rw_a0=m_rw_a0, rw_a2=m_rw_a2, rw_g2=m_rw_g2, rw_k_k=m_rw_k_k, rw_k_a=m_rw_k_a, rw_r_k=m_rw_r_k, rw_ln_w=m_rw_ln_w, rw_ln_b=m_rw_ln_b, w_out=m_w_out, xa_norm_w=m_xa_norm_w, mem_norm_w=m_mem_norm_w, xa_wq=m_xa_wq, xa_wk=m_xa_wk, xa_wv=m_xa_wv, xa_wo=m_xa_wo, ffn_norm_w=m_ffn_norm_w, ffn_w1=m_ffn_w1, ffn_w2=m_ffn_w2, final_norm_w=m_final_norm_w)
    mom_v = dict(mix_norm_w=v_mix_norm_w, w_in=v_w_in, dn_conv_w=v_dn_conv_w, dn_a_log=v_dn_a_log, dn_dt_bias=v_dn_dt_bias, dn_norm_w=v_dn_norm_w, rw_mu=v_rw_mu, rw_w0=v_rw_w0, rw_w2=v_rw_w2, rw_a0=v_rw_a0, rw_a2=v_rw_a2, rw_g2=v_rw_g2, rw_k_k=v_rw_k_k, rw_k_a=v_rw_k_a, rw_r_k=v_rw_r_k, rw_ln_w=v_rw_ln_w, rw_ln_b=v_rw_ln_b, w_out=v_w_out, xa_norm_w=v_xa_norm_w, mem_norm_w=v_mem_norm_w, xa_wq=v_xa_wq, xa_wk=v_xa_wk, xa_wv=v_xa_wv, xa_wo=v_xa_wo, ffn_norm_w=v_ffn_norm_w, ffn_w1=v_ffn_w1, ffn_w2=v_ffn_w2, final_norm_w=v_final_norm_w)
    names = list(weights)

    seq, d = x.shape[1], x.shape[2]
    dnw = d // 2
    rww = d - dnw
    nh, nb = dnw // LANES, rww // LANES
    n_mem = mem.shape[1]
    lw_dim, la_dim, lg_dim = rw_w2.shape[1], rw_a2.shape[1], rw_g2.shape[1]
    assert lw_dim + la_dim == LANES and lg_dim == LANES and dnw % LANES == 0 and rww % LANES == 0
    xs, mems, tgt = x[0], mem[0], loss_target[0]

    col_sharded = ("w_in", "xa_wo", "ffn_w1", "dn_conv_w", "rw_w2", "rw_a2", "rw_g2")
    row_sharded = ("w_out", "xa_wq", "xa_wk", "xa_wv", "ffn_w2")
    f32_payload = ("dn_conv_w", "rw_w2", "rw_a2", "rw_g2")
    sharded = col_sharded + row_sharded
    payload = {n: weights[n][0].astype(f32 if n in f32_payload else bf16) for n in sharded}
    first = ("w_in", "dn_conv_w", "rw_w2", "rw_a2", "rw_g2")
    mid = ("w_out", "xa_wq", "xa_wk", "xa_wv", "xa_wo")
    late = ("ffn_w1", "ffn_w2")
    gathered = dict(zip(first, gather_chips_halves("gather_first", [payload[n] for n in first])))
    ordered = lax.optimization_barrier(([gathered[n] for n in first], [payload[n] for n in mid + late]))
    gathered = dict(zip(first, ordered[0]))
    payload.update(zip(mid + late, ordered[1]))
    mid_state, tok_mid = chips_start("gather_mid_start", [payload[n] for n in mid], _whole)
    late_state, tok_late = chips_start("gather_late_start", [payload[n] for n in late], _whole)
    mix_norm_w_in = mix_norm_w + (tok_mid + tok_late)

    def full(n):
        g = gathered[n]
        if n in col_sharded:
            return g.transpose(1, 0, 2).reshape(g.shape[1], 4 * g.shape[2])
        return g.reshape(4 * g.shape[1], g.shape[2])

    c_rw0 = 4 * dnw + 2 * nh
    c_rw = 3 * rww
    eb_ab, eb_r = 4 * nh, 4 * nh + 1
    eb_l1 = eb_r + 3 * nb
    n_ext = -(-(eb_l1 + 2) // 4) * 4
    shard_w = w_in.shape[2]
    tbl_fwd, tbl_bwd, per_shard = w_in_layout(4 * shard_w, shard_w, c_rw0, LANES - 2 * nh, n_ext)
    w_ext = lane_regroup(
        "w_in_regroup", gathered["w_in"], tbl_fwd,
        lambda p: pl.BlockSpec((None, d, LANES),
                               lambda i, j, t: (t[REGROUP_FIELDS * p, j], 0, t[REGROUP_FIELDS * p + 1, j])),
        pl.BlockSpec((d, LANES), lambda i, j, t: (0, j)), (d, n_ext * LANES), (1, n_ext), shard_w)
    conv_w = full("dn_conv_w")
    w2p = jnp.concatenate([full("rw_w2"), jnp.zeros((la_dim, rww), f32)], axis=0)
    a2p = jnp.concatenate([jnp.zeros((lw_dim, rww), f32), full("rw_a2")], axis=0)
    g2 = full("rw_g2")
    xaw = xa_wq.shape[2]
    nxh = xaw // LANES
    ffn = 4 * ffn_w1.shape[2]

    def lane_row(vec):
        return jnp.pad(vec.reshape(1, -1), ((0, 0), (0, LANES - vec.size)))

    alog_row, dtb_row = lane_row(dn_a_log), lane_row(dn_dt_bias)
    head_of_col = jnp.arange(dnw)[None, :] // LANES
    e_g = (jnp.arange(LANES)[:, None] == head_of_col).astype(f32)
    e_b = (jnp.arange(LANES)[:, None] == head_of_col + nh).astype(f32)
    mu_main, mu_small = rw_mu[:, :c_rw], rw_mu[:, c_rw:]
    r_k_row = rw_r_k.reshape(1, rww)
    fnw = final_norm_w.reshape(1, d)
    qb, kb_, vb, zb = 0, nh, 2 * nh, 3 * nh
    rb0 = eb_r
    cc = lambda w_, o_: ("constc", w_, o_)
    rc = lambda o_: ("rowc", LANES, o_)

    (u16,) = rowcall("norm_mix", fn_rms, [(xs, "row"), (mix_norm_w_in, "const")], [(seq, d, bf16, "row")], rows=seq)
    p_main = p_small = mm("in_proj", u16, w_ext)

    fn_gconv = make_fn_gconv(2 * nh)
    (qkv,) = rowcall("gdn_conv", fn_gconv, [(p_main, rc(0)), (conv_w, cc(LANES, 0))],
                     [(seq, 3 * dnw, f32, "rowc")], rows=seq, tile=seq, ncol=3 * nh)
    gate_ins = [(p_small, rc(eb_ab)), (alog_row, "const"), (dtb_row, "const"), (e_g, "const"), (e_b, "const")]
    g_b, beta_b = rowcall("gdn_gate", fn_ggate, gate_ins, [(seq, dnw, f32, "row")] * 2, rows=seq)
    gdn_ins = [(qkv, qb), (qkv, kb_), (qkv, vb), (g_b, 0), (beta_b, 0)]
    o_raw, gdn_states = scan_fwd("gdn_scan", gdn_chunk, gdn_ins, rows=seq, chunk=GDN_CHUNK, ncol=nh)
    gpost_ins = [(o_raw, rc(0)), (p_main, rc(zb)), (dn_norm_w, "const")]
    (o_dn,) = rowcall("gdn_post", fn_gpost, gpost_ins, [(seq, dnw, bf16, "rowc")], rows=seq, ncol=nh)

    (prw,) = rowcall("rw_lerp_main", fn_lerp, [(p_main, rc(rb0)), (mu_main, cc(LANES, 0))],
                     [(seq, c_rw, f32, "rowc")], rows=seq, tile=seq, ncol=3 * nb)
    (psl,) = rowcall("rw_lerp_small", fn_lerp, [(p_small, rc(eb_l1)), (mu_small, cc(LANES, 0))],
                     [(seq, 2 * LANES, f32, "rowc")], rows=seq, tile=seq, ncol=2)
    rprep_ins = [(prw, rc(nb)), (psl, "row"), (rw_w0, cc(LANES, 0)), (rw_a0, cc(LANES, 0)), (rw_k_k, cc(LANES, 0)),
                 (rw_k_a, cc(LANES, 0)), (w2p, cc(LANES, 0)), (a2p, cc(LANES, 0)), (g2, cc(LANES, 0))]
    lw, kmod, kk, a_rw, gate = rowcall("rw_prep", fn_rprep, rprep_ins, [(seq, rww, f32, "rowc")] * 5,
                                        rows=seq, ncol=nb)
    wkv_ins = [(prw, 0), (lw, 0), (kmod, 0), (prw, 2 * nb), (kk, 0), (a_rw, 0)]
    y_rw, wkv_states = scan_fwd("wkv_scan", wkv_chunk, wkv_ins, rows=seq, chunk=WKV_CHUNK, ncol=nb)
    rpost_ins = [(y_rw, rc(0)), (prw, rc(0)), (kmod, rc(0)), (prw, rc(2 * nb)), (gate, rc(0)),
                 (r_k_row, cc(LANES, 0)), (rw_ln_w, cc(LANES, 0)), (rw_ln_b, cc(LANES, 0))]
    (o_rw,) = rowcall("rw_post", fn_rpost, rpost_ins, [(seq, rww, bf16, "rowc")], rows=seq, ncol=nb)

    o_cat = jnp.concatenate([o_dn, o_rw], axis=1)
    gathered.update(zip(mid, chips_wait("gather_mid_wait", mid_state, o_cat)))
    w_out_f, wq_f, wk_f, wv_f, wo_f = full("w_out"), full("xa_wq"), full("xa_wk"), full("xa_wv"), full("xa_wo")
    h1 = mm("out_proj", o_cat, w_out_f, add=xs)

    (hn16,) = rowcall("norm_xa", fn_rms, [(h1, "row"), (xa_norm_w, "const")], [(seq, d, bf16, "row")], rows=seq)
    (mn16,) = rowcall("norm_mem", fn_rms, [(mems, "row"), (mem_norm_w, "const")], [(n_mem, d, bf16, "row")],
                      rows=n_mem)
    q_xa = mm("xa_q", hn16, wq_f)
    k_xa = mm("xa_k", mn16, wk_f)
    v_xa = mm("xa_v", mn16, wv_f)
    xcore_ins = [(q_xa, rc(0)), (k_xa, cc(LANES, 0)), (v_xa, cc(LANES, 0))]
    (o_xa,) = rowcall("xa_core", fn_xcore, xcore_ins, [(seq, xaw, bf16, "rowc")], rows=seq, ncol=nxh)
    h2 = mm("xa_o", o_xa, wo_f, add=h1)

    (fn16,) = rowcall("norm_ffn", fn_rms, [(h2, "row"), (ffn_norm_w, "const")], [(seq, d, bf16, "row")], rows=seq)
    gathered.update(zip(late, chips_wait("gather_late_wait", late_state, fn16)))
    w1_f, w2_f = full("ffn_w1"), full("ffn_w2")
    a1_16, hid16 = mm("ffn_up", fn16, w1_f, epilogue=lambda r: (r, jnp.square(jnp.maximum(r, 0.0))),
                      out_dtypes=(bf16, bf16))
    h3 = mm("ffn_down", hid16, w2_f, add=h2)

    dh3, dh3_16, d_fnw, loss_rows = rowcall(
        "loss_head", fn_final, [(h3, "row"), (tgt, "row"), (fnw, "const")],
        [(seq, d, f32, "row"), (seq, d, bf16, "row"), (1, d, f32, "acc"), (8, LANES, f32, "acc")], rows=seq)

    da1_16 = mm("ffn_down_dx", dh3_16, w2_f, tb=True, extra=[a1_16], out_dtype=bf16,
                epilogue=lambda r, a1: (r * (2.0 * jnp.maximum(a1.astype(f32), 0.0)),))
    g_ffn_w2 = mm("ffn_down_dw", hid16, dh3_16, ta=True, out_dtype=bf16)
    g_ffn_w1 = mm("ffn_up_dw", fn16, da1_16, ta=True, out_dtype=bf16, by_chip=True)
    dfn = mm("ffn_up_dx", da1_16, w1_f, tb=True)

    def by_chip(n, g):
        if g.ndim == 3:
            return g
        if n in col_sharded:
            return g.reshape(g.shape[0], 4, g.shape[1] // 4).transpose(1, 0, 2)
        return g.reshape(4, g.shape[0] // 4, g.shape[1])

    late_g, tok = chips_start("scatter_late_start", [by_chip("ffn_w1", g_ffn_w1), by_chip("ffn_w2", g_ffn_w2)],
                              _chip_block)
    dh2, d_ffn_nw, dh2_16 = rowvjp("norm_ffn_bwd", fn_rms, [(h2, "row"), (ffn_norm_w + tok, "const")],
                                   [[(dfn, "row")]], [0, 1], rows=seq, adds=[(0, dh3, "row")], dup16=[0])

    do_xa = mm("xa_o_dx", dh2_16, wo_f, tb=True)
    g_xa_wo = mm("xa_o_dw", o_xa, dh2_16, ta=True, out_dtype=bf16, by_chip=True)
    dq_xa, dk_xa, dv_xa, dq16 = rowvjp("xa_core_bwd", fn_xcore, xcore_ins, [[(do_xa, rc(0))]], [0, 1, 2],
                                       rows=seq, ncol=nxh, dup16=[0])
    g_xa_wq = mm("xa_q_dw", hn16, dq16, ta=True, out_dtype=bf16)
    dhn = mm("xa_q_dx", dq16, wq_f, tb=True)
    dh1, d_xa_nw, dh1_16 = rowvjp("norm_xa_bwd", fn_rms, [(h1, "row"), (xa_norm_w, "const")], [[(dhn, "row")]],
                                  [0, 1], rows=seq, adds=[(0, dh2, "row")], dup16=[0])
    dk16, dv16 = dk_xa.astype(bf16), dv_xa.astype(bf16)
    g_xa_wk = mm("xa_k_dw", mn16, dk16, ta=True, out_dtype=bf16)
    g_xa_wv = mm("xa_v_dw", mn16, dv16, ta=True, out_dtype=bf16)
    dmn = mm("xa_v_dx", dv16, wv_f, tb=True, add=mm("xa_k_dx", dk16, wk_f, tb=True))
    (d_mem_nw,) = rowvjp("norm_mem_bwd", fn_rms, [(mems, "row"), (mem_norm_w, "const")], [[(dmn, "row")]], [1],
                         rows=n_mem)

    g_w_out = mm("out_proj_dw", o_cat, dh1_16, ta=True, out_dtype=bf16)
    do_cat = mm("out_proj_dx", dh1_16, w_out_f, tb=True)
    mid_grads = dict(w_out=g_w_out, xa_wq=g_xa_wq, xa_wk=g_xa_wk, xa_wv=g_xa_wv, xa_wo=g_xa_wo)
    mid_g, tok = chips_start("scatter_mid_start", [by_chip(n, mid_grads[n]) for n in mid], _chip_block)
    rpost_ins_b = rpost_ins[:5] + [(r_k_row + tok, cc(LANES, 0))] + rpost_ins[6:]

    dy, dr_a, dkmod_a, dv_a, dgate, d_r_k, d_ln_w, d_ln_b = rowvjp(
        "rw_post_bwd", fn_rpost, rpost_ins_b, [[(do_cat, rc(nh))]], [0, 1, 2, 3, 4, 5, 6, 7], rows=seq, ncol=nb)
    dr_b, dlw, dkmod_b, dv_b, dkk, da_rw = scan_bwd("wkv_scan_bwd", wkv_chunk, wkv_ins, wkv_states, dy, 0,
                                                    rows=seq, chunk=WKV_CHUNK, ncol=nb)
    rprep_cts = [[(dlw, rc(0))], [(dkmod_a, rc(0)), (dkmod_b, rc(0))], [(dkk, rc(0))], [(da_rw, rc(0))],
                 [(dgate, rc(0))]]
    dpk, dpsl_parts, d_w0, d_a0, d_k_k, d_k_a, d_w2p, d_a2p, d_g2 = rowvjp(
        "rw_prep_bwd", fn_rprep, rprep_ins, rprep_cts, [0, 1, 2, 3, 4, 5, 6, 7, 8], rows=seq, ncol=nb)
    (dpsl,) = rowcall("rw_prep_sum", fn_sumcols(nb), [(dpsl_parts, "row")], [(seq, 2 * LANES, f32, "row")], rows=seq)

    def lerp_bwd(tag, p, p_off, mu, mu_off, ct_lists, ncol):
        return rowvjp("rw_lerp_bwd_" + tag, fn_lerp, [(p, rc(p_off)), (mu, cc(LANES, mu_off))], [ct_lists], [0, 1],
                      rows=seq, tile=seq, ncol=ncol, dup16=[0])

    _, dmu_r, dpr16 = lerp_bwd("r", p_main, rb0, mu_main, 0, [(dr_a, rc(0)), (dr_b, rc(0))], nb)
    _, dmu_k, dpk16 = lerp_bwd("k", p_main, rb0 + nb, mu_main, nb, [(dpk, rc(0))], nb)
    _, dmu_v, dpv16 = lerp_bwd("v", p_main, rb0 + 2 * nb, mu_main, 2 * nb, [(dv_a, rc(0)), (dv_b, rc(0))], nb)
    _, dmu_s, dps12_16 = lerp_bwd("small", p_small, eb_l1, mu_small, 0, [(dpsl, rc(0))], 2)

    do_raw, dz, d_dn_nw, dz16 = rowvjp("gdn_post_bwd", fn_gpost, gpost_ins, [[(do_cat, rc(0))]], [0, 1, 2],
                                       rows=seq, ncol=nh, dup16=[1])
    dq_g, dk_g, dv_g, dg_b, dbeta_b = scan_bwd("gdn_scan_bwd", gdn_chunk, gdn_ins, gdn_states, do_raw, 0,
                                               rows=seq, chunk=GDN_CHUNK, ncol=nh)
    dps0, d_alog, d_dtb, dps0_16 = rowvjp("gdn_gate_bwd", fn_ggate, gate_ins, [[(dg_b, "row")], [(dbeta_b, "row")]],
                                          [0, 1, 2], rows=seq, dup16=[0])
    dqkv = jnp.concatenate([dq_g, dk_g, dv_g], axis=1)
    _, d_conv_w, dqkv16 = rowvjp("gdn_conv_bwd", fn_gconv, [(p_main, rc(0)), (conv_w, cc(LANES, 0))],
                                 [[(dqkv, rc(0))]], [0, 1], rows=seq, tile=seq, ncol=3 * nh, dup16=[0])

    dp16 = jnp.concatenate([dqkv16, dz16, dps0_16, dpr16, dpk16, dpv16, dps12_16,
                            jnp.zeros((seq, (n_ext - eb_l1 - 2) * LANES), bf16)], axis=1)
    g_w_ext = mm("in_proj_dw", u16, dp16, ta=True, out_dtype=bf16)
    g_w_in = lane_regroup(
        "w_in_grad_regroup", g_w_ext, tbl_bwd,
        lambda p: pl.BlockSpec((d, LANES), lambda i, j, t: (0, t[REGROUP_FIELDS * p + 1, j])),
        pl.BlockSpec((None, d, LANES), lambda i, j, t: (j // per_shard, 0, j % per_shard)),
        (4, d, shard_w), (1, 4 * per_shard), n_ext * LANES)
    first_grads = dict(w_in=g_w_in, dn_conv_w=d_conv_w, rw_w2=d_w2p[:lw_dim], rw_a2=d_a2p[lw_dim:], rw_g2=d_g2)
    first_g, tok = chips_start("scatter_first_start", [by_chip(n, first_grads[n]) for n in first], _chip_block)

    du = mm("in_proj_dx", dp16, w_ext, tb=True)
    grad_x, d_mix_nw = rowvjp("norm_mix_bwd", fn_rms, [(xs, "row"), (mix_norm_w + tok, "const")], [[(du, "row")]],
                              [0, 1], rows=seq, adds=[(0, dh1, "row")])
    received = dict(zip(mid, chips_wait("scatter_mid_wait", mid_g, grad_x)))
    received.update(zip(late, chips_wait("scatter_late_wait", late_g, grad_x)))

    out = {}

    def reduce_and_update(tag, group):
        partial = [sum_slots("sum_chips_" + n, received[n]) for n in group]
        other = swap_sibling("swap_sibling_" + tag, partial)
        for n, p_mine, p_other in zip(group, partial, other):
            if weights[n].shape[2] % LANES:
                lin = lambda a: jnp.swapaxes(a, 1, 2).reshape(-1, LANES)
                res = adamw("adamw_" + n, lin(weights[n]), [p_mine.T.reshape(-1, LANES), p_other.T.reshape(-1, LANES)],
                            lin(mom_m[n]), lin(mom_v[n]))
                rows, cols = weights[n].shape[1:]
                out[n] = [jnp.swapaxes(r.reshape(1, cols, rows), 1, 2) for r in res]
            else:
                out[n] = adamw("adamw_" + n, weights[n], [p_mine, p_other], mom_m[n], mom_v[n])

    reduce_and_update("rest", mid + late)
    received.update(zip(first, chips_wait("scatter_first_wait", first_g, out["ffn_w2"][1])))
    reduce_and_update("first", first)

    small_names = [n for n in names if n not in sharded]
    small_local = dict(
        mix_norm_w=d_mix_nw, dn_a_log=d_alog[:, :nh], dn_dt_bias=d_dtb[:, :nh], dn_norm_w=d_dn_nw,
        rw_mu=jnp.concatenate([dmu_r, dmu_k, dmu_v, dmu_s], axis=1), rw_w0=d_w0, rw_a0=d_a0, rw_k_k=d_k_k,
        rw_k_a=d_k_a, rw_r_k=d_r_k, rw_ln_w=d_ln_w, rw_ln_b=d_ln_b, xa_norm_w=d_xa_nw, mem_norm_w=d_mem_nw,
        ffn_norm_w=d_ffn_nw, final_norm_w=d_fnw)
    loss_vec = jnp.where(jnp.arange(LANES) == 0, loss_rows[0], 0.0)
    packed, _ = lax.optimization_barrier((_pack([small_local[n] for n in small_names] + [loss_vec]), out["w_in"][0]))
    (g8,) = gather_all([packed])

    packed_like = [weights[n] for n in small_names] + [loss_vec]
    zero = jnp.zeros((LANES,), f32)
    res = adamw_packed(_pack([weights[n] for n in small_names] + [zero]), g8,
                       _pack([mom_m[n] for n in small_names] + [zero]),
                       _pack([mom_v[n] for n in small_names] + [zero]))
    unpacked = [_unpack(r, packed_like) for r in res]
    for i, n in enumerate(small_names):
        out[n] = [u[i] for u in unpacked]
    loss = unpacked[0][-1][0]

    return (loss, grad_x.reshape(x.shape), *[out[n][0] for n in names], *[out[n][1] for n in names],
            *[out[n][2] for n in names], *[out[n][3] for n in names])
```

```python
import functools

import jax
import jax.numpy as jnp
from jax import lax
from jax.experimental import pallas as pl
from jax.experimental.pallas import tpu as pltpu

f32 = jnp.float32
bf16 = jnp.bfloat16
HI = lax.Precision.HIGHEST
MESH = pl.DeviceIdType.MESH

LANES = 128
VMEM_LIMIT = 56 << 20
TOK_TILE = 256
TOK_TILE_BLOCKED = 1024
MM_TILE = 1024
MM_TILE_K = 2048
GDN_CHUNK = 128
WKV_CHUNK = 64
SCAN_GROUP = 8
P_BULK = 1
P_INV = 1
P_CUMSUM = 3
RMS_EPS = 1e-6
RW_GN_EPS = 64e-5
RW_HEAD = 64

ADAM_LR, ADAM_B1, ADAM_B2, ADAM_EPS, ADAM_WD, ADAM_STEP = 0.001, 0.9, 0.999, 1e-08, 0.01, 10


def _params(n_grid):
    return pltpu.CompilerParams(dimension_semantics=("arbitrary",) * n_grid, vmem_limit_bytes=VMEM_LIMIT)


_DIMS = {"nn": (((1,), (0,)), ((), ())), "nt": (((1,), (1,)), ((), ())), "tn": (((0,), (0,)), ((), ()))}
_DIMS_BATCHED = {"nn": (((2,), (1,)), ((0,), (0,))), "nt": (((2,), (2,)), ((0,), (0,))),
                 "tn": (((1,), (1,)), ((0,), (0,)))}


def _raw_dot(a, b, mode, passes):
    dims = (_DIMS if a.ndim == 2 else _DIMS_BATCHED)[mode]
    if passes == 6:
        return lax.dot_general(a.astype(f32), b.astype(f32), dims, precision=HI, preferred_element_type=f32)
    ah, bh = a.astype(bf16), b.astype(bf16)
    r = lax.dot_general(ah, bh, dims, preferred_element_type=f32)
    if passes == 3:
        al = (a - ah.astype(f32)).astype(bf16)
        bl = (b - bh.astype(f32)).astype(bf16)
        r = r + lax.dot_general(al, bh, dims, preferred_element_type=f32)
        r = r + lax.dot_general(ah, bl, dims, preferred_element_type=f32)
    return r


@functools.partial(jax.custom_vjp, nondiff_argnums=(2, 3))
def pdot(a, b, mode, passes):
    return _raw_dot(a, b, mode, passes)


def _pdot_bwd(mode, passes, res, g):
    a, b = res
    if mode == "nn":
        da, db = _raw_dot(g, b, "nt", passes), _raw_dot(a, g, "tn", passes)
    elif mode == "nt":
        da, db = _raw_dot(g, b, "nn", passes), _raw_dot(g, a, "tn", passes)
    else:
        da, db = _raw_dot(b, g, "nt", passes), _raw_dot(a, g, "nn", passes)
    return da.astype(a.dtype), db.astype(b.dtype)


pdot.defvjp(lambda a, b, mode, passes: (_raw_dot(a, b, mode, passes), (a, b)), _pdot_bwd)


def hdot(a, b):
    return pdot(a, b, "nn", 6)


def bdot(a, b):
    return pdot(a, b, "nn", 1)


def bdot_nt(a, b):
    return pdot(a, b, "nt", 1)


def _shift_rows(x, k):
    row = lax.broadcasted_iota(jnp.int32, x.shape, 0)
    return jnp.where(row < k, 0.0, pltpu.roll(x, k, axis=0))


def _unshift_rows(g, k):
    n = g.shape[0]
    row = lax.broadcasted_iota(jnp.int32, g.shape, 0)
    return jnp.where(row >= n - k, 0.0, pltpu.roll(g, n - k, axis=0))


@functools.partial(jax.custom_vjp, nondiff_argnums=(1,))
def tshift(x, k):
    return _shift_rows(x, k)


tshift.defvjp(lambda x, k: (_shift_rows(x, k), None), lambda k, _, g: (_unshift_rows(g, k),))


def rms(x, w):
    x = x.astype(f32)
    return x * lax.rsqrt(jnp.mean(x * x, axis=-1, keepdims=True) + RMS_EPS) * w


def softplus(x):
    return jnp.maximum(x, 0.0) + jnp.log(1.0 + jnp.exp(-jnp.abs(x)))


def seg2sum(x):
    lo = lax.broadcasted_iota(jnp.int32, x.shape, 1) < RW_HEAD
    s_lo = jnp.sum(jnp.where(lo, x, 0.0), axis=-1, keepdims=True)
    s_hi = jnp.sum(jnp.where(lo, 0.0, x), axis=-1, keepdims=True)
    return jnp.where(lo, s_lo, s_hi)


def _tile(n, pref):
    if n <= pref:
        return n
    t = pref
    while t >= LANES:
        if n % t == 0:
            return t
        t -= LANES
    return n


def mm(name, a, b, *, ta=False, tb=False, add=None, out_dtype=f32, by_chip=False, epilogue=None, extra=(),
       out_dtypes=None):
    (k, m) = a.shape if ta else a.shape[::-1]
    (n, kb) = b.shape if tb else b.shape[::-1]
    assert k == kb, (name, a.shape, b.shape)
    tm, tk = _tile(m, MM_TILE), _tile(k, MM_TILE_K)
    tn = _tile(n // 4, MM_TILE) if by_chip else _tile(n, MM_TILE)
    nk = k // tk
    dims = (((0,) if ta else (1,), (1,) if tb else (0,)), ((), ()))
    extra = list(extra) + ([] if add is None else [add])
    out_dtypes = [out_dtype] if out_dtypes is None else list(out_dtypes)
    n_extra, n_out = len(extra), len(out_dtypes)

    def body(*refs):
        a_ref, b_ref = refs[:2]
        x_refs = refs[2:2 + n_extra]
        o_refs = refs[2 + n_extra:2 + n_extra + n_out]
        part = lax.dot_general(a_ref[...].astype(bf16), b_ref[...].astype(bf16), dims, preferred_element_type=f32)

        def finish(r):
            xs = [x[...] for x in x_refs]
            if add is not None:
                r = r + xs.pop().astype(f32)
            outs = (r,) if epilogue is None else epilogue(r, *xs)
            for o_ref, o in zip(o_refs, outs):
                o_ref[...] = o.astype(o_ref.dtype)

        if nk == 1:
            finish(part)
            return
        acc = refs[-1]
        kk = pl.program_id(2)

        @pl.when(kk == 0)
        def _():
            acc[...] = part

        @pl.when(kk > 0)
        def _():
            acc[...] += part

        @pl.when(kk == nk - 1)
        def _():
            finish(acc[...])

    a_spec = pl.BlockSpec((tk, tm), lambda i, j, q: (q, i)) if ta else pl.BlockSpec((tm, tk), lambda i, j, q: (i, q))
    b_spec = pl.BlockSpec((tn, tk), lambda i, j, q: (j, q)) if tb else pl.BlockSpec((tk, tn), lambda i, j, q: (q, j))
    x_spec = pl.BlockSpec((tm, tn), lambda i, j, q: (i, j))
    if by_chip:
        per_chip = n // 4 // tn
        o_spec = pl.BlockSpec((None, tm, tn), lambda i, j, q: (j // per_chip, i, j % per_chip))
        o_shape = (4, m, n // 4)
    else:
        o_spec, o_shape = x_spec, (m, n)
    res = pl.pallas_call(
        body, name=name, grid=(m // tm, n // tn, nk), in_specs=[a_spec, b_spec] + [x_spec] * n_extra,
        out_specs=[o_spec] * n_out, out_shape=[jax.ShapeDtypeStruct(o_shape, dt) for dt in out_dtypes],
        scratch_shapes=[pltpu.VMEM((tm, tn), f32)] if nk > 1 else [], compiler_params=_params(3))(a, b, *extra)
    return res[0] if n_out == 1 else res


REGROUP_PIECES = 4


REGROUP_FIELDS = 5


def _regroup_table(n_out, sources_of):
    import numpy as np
    tbl = np.zeros((REGROUP_FIELDS * REGROUP_PIECES, n_out), np.int32)
    for j in range(n_out):
        groups = sorted(sources_of(j).items())
        assert len(groups) <= REGROUP_PIECES, (j, len(groups))
        for p in range(REGROUP_PIECES):
            if p < len(groups):
                key, lanes = groups[p]
                shifts = {q - s for s, q in lanes}
                qs = sorted(q for _, q in lanes)
                assert len(shifts) == 1 and qs == list(range(qs[0], qs[-1] + 1)), (j, key)
                row = (key[0], key[1], shifts.pop(), qs[0], qs[-1] + 1)
            else:
                row = (0, 0, 0, 0, 0)
            tbl[REGROUP_FIELDS * p:REGROUP_FIELDS * (p + 1), j] = row
    return jnp.asarray(tbl)


def lane_regroup(name, src, table, src_spec, out_spec, out_shape, grid, src_width):
    rows = src.shape[-2]

    def body(tbl, *refs):
        o_ref, acc = refs[REGROUP_PIECES:]
        j = pl.program_id(1)
        acc[...] = jnp.zeros_like(acc)
        for p in range(REGROUP_PIECES):
            blk, shift, lo, hi = (tbl[REGROUP_FIELDS * p + f, j] for f in range(1, REGROUP_FIELDS))

            @pl.when(hi > lo)
            def _(p=p, blk=blk, shift=shift, lo=lo, hi=hi):
                lane = lax.broadcasted_iota(jnp.int32, (rows, LANES), 1)
                pi = lax.broadcasted_iota(jnp.int32, (LANES, LANES), 0)
                qi = lax.broadcasted_iota(jnp.int32, (LANES, LANES), 1)
                x = jnp.where(lane < src_width - blk * LANES, refs[p][...], jnp.zeros((), src.dtype))
                sel = jnp.logical_and(qi - pi == shift, jnp.logical_and(qi >= lo, qi < hi))
                acc[...] += jnp.dot(x, sel.astype(src.dtype), preferred_element_type=f32)

        o_ref[...] = acc[...].astype(o_ref.dtype)

    return pl.pallas_call(
        body, name=name, out_shape=jax.ShapeDtypeStruct(out_shape, src.dtype),
        grid_spec=pltpu.PrefetchScalarGridSpec(
            num_scalar_prefetch=1, grid=grid, in_specs=[src_spec(p) for p in range(REGROUP_PIECES)],
            out_specs=out_spec, scratch_shapes=[pltpu.VMEM((rows, LANES), f32)]),
        compiler_params=_params(2))(table, *[src] * REGROUP_PIECES)


def w_in_layout(d_cols, shard_w, c_split, gap, n_blocks):
    def ext_of(c):
        return c if c < c_split else c + gap

    def fwd_sources(j):
        groups = {}
        for q in range(LANES):
            e = j * LANES + q
            c = e if e < c_split else e - gap
            if (c_split <= e < c_split + gap) or c >= d_cols:
                continue
            s, l = divmod(c, shard_w)
            groups.setdefault((s, l // LANES), []).append((l % LANES, q))
        return groups

    per_shard = -(-shard_w // LANES)

    def bwd_sources(j):
        s, b = divmod(j, per_shard)
        groups = {}
        for q in range(LANES):
            l = b * LANES + q
            if l >= shard_w:
                continue
            e = ext_of(s * shard_w + l)
            groups.setdefault((0, e // LANES), []).append((e % LANES, q))
        return groups

    return _regroup_table(n_blocks, fwd_sources), _regroup_table(4 * per_shard, bwd_sources), per_shard


def _in_spec(a, kind, tile):
    if kind == "row":
        return pl.BlockSpec((tile, a.shape[1]), lambda j, i: (i, 0))
    if kind == "const":
        return pl.BlockSpec(a.shape, lambda j, i: (0, 0))
    tag, cw, off = kind
    if tag == "rowc":
        return pl.BlockSpec((tile, cw), lambda j, i: (i, j + off))
    assert tag == "constc", kind
    return pl.BlockSpec((a.shape[0], cw), lambda j, i: (0, j + off))


def rowcall(name, fn, ins, outs, *, rows, tile=None, ncol=1):
    if tile is None:
        tile = min(TOK_TILE_BLOCKED if ncol > 1 else TOK_TILE, rows)
    n_in = len(ins)
    kinds = [o[3] for o in outs]

    def body(*refs):
        j, i = pl.program_id(0), pl.program_id(1)
        res = fn(*[r[...] for r in refs[:n_in]])
        for ref, val, kind in zip(refs[n_in:], res, kinds):
            if kind in ("row", "rowc"):
                ref[...] = val.astype(ref.dtype)
            else:
                first = (i == 0) if kind == "accc" else jnp.logical_and(i == 0, j == 0)

                @pl.when(first)
                def _(ref=ref, val=val):
                    ref[...] = val.astype(ref.dtype)

                @pl.when(jnp.logical_not(first))
                def _(ref=ref, val=val):
                    ref[...] += val.astype(ref.dtype)

    out_shape, out_specs = [], []
    for nr, nc, dtype, kind in outs:
        out_shape.append(jax.ShapeDtypeStruct((nr, nc), dtype))
        if kind == "row":
            out_specs.append(pl.BlockSpec((tile, nc), lambda j, i: (i, 0)))
        elif kind == "rowc":
            out_specs.append(pl.BlockSpec((tile, nc // ncol), lambda j, i: (i, j)))
        elif kind == "acc":
            out_specs.append(pl.BlockSpec((nr, nc), lambda j, i: (0, 0)))
        else:
            out_specs.append(pl.BlockSpec((nr, nc // ncol), lambda j, i: (0, j)))
    return pl.pallas_call(
        body, name=name, grid=(ncol, rows // tile), in_specs=[_in_spec(a, k, tile) for a, k in ins],
        out_specs=out_specs, out_shape=out_shape, compiler_params=_params(2))(*[a for a, _ in ins])


def rowvjp(name, fn, ins, cts, grads, *, rows, tile=None, ncol=1, adds=(), dup16=()):
    n_in = len(ins)
    ct_sizes = [len(c) for c in cts]
    flat_cts = [m for c in cts for m in c]
    n_ct = len(flat_cts)

    def wrapped(*vals):
        xs = list(vals[:n_in])
        gs = vals[n_in:n_in + n_ct]
        extra = vals[n_in + n_ct:]

        def f(*dvars):
            full = list(xs)
            for k, v in zip(grads, dvars):
                full[k] = v
            return fn(*full)

        outs, pull = jax.vjp(f, *[xs[k] for k in grads])
        cot, p = [], 0
        for o, size in zip(outs, ct_sizes):
            g = gs[p].astype(f32)
            for q in range(1, size):
                g = g + gs[p + q].astype(f32)
            cot.append(g.astype(o.dtype))
            p += size
        gv = list(pull(tuple(cot)))
        for (pos, _, _), e in zip(adds, extra):
            gv[pos] = gv[pos] + e.astype(gv[pos].dtype)
        return tuple(gv) + tuple(gv[pos] for pos in dup16)

    outs = []
    for k in grads:
        a, kind = ins[k]
        if kind == "row":
            outs.append((rows, a.shape[1] * ncol, f32, "rowc") if ncol > 1 else (rows, a.shape[1], f32, "row"))
        elif kind == "const":
            outs.append((a.shape[0], a.shape[1], f32, "acc"))
        elif kind[0] == "rowc":
            outs.append((rows, kind[1] * ncol, f32, "rowc"))
        else:
            outs.append((a.shape[0], kind[1] * ncol, f32, "accc"))
    for pos in dup16:
        nr, nc, _, kind = outs[pos]
        outs.append((nr, nc, bf16, kind))
    all_ins = list(ins) + flat_cts + [(a, kind) for _, a, kind in adds]
    return rowcall(name, wrapped, all_ins, outs, rows=rows, tile=tile, ncol=ncol)


def fn_rms(x, w):
    return (rms(x, w),)


def make_fn_gconv(n_norm_blocks):
    def fn(p, cw):
        c = cw[3:4] * p
        for jj in range(3):
            c = c + cw[jj:jj + 1] * tshift(p, 3 - jj)
        s = c * jax.nn.sigmoid(c)
        nrm = s * lax.rsqrt(jnp.sum(s * s, axis=-1, keepdims=True) + 1e-6)
        return (jnp.where(pl.program_id(0) < n_norm_blocks, nrm, s),)
    return fn


def fn_ggate(ps0, alog, dtb, e_g, e_b):
    g = -jnp.exp(alog) * softplus(ps0 + dtb)
    beta = jax.nn.sigmoid(ps0)
    return hdot(g, e_g), hdot(beta, e_b)


def fn_gpost(o, z, nw):
    return (rms(o, nw) * (z * jax.nn.sigmoid(z)),)


def fn_lerp(p, mu):
    return (p + (tshift(p, 1) - p) * mu,)


def fn_rprep(pk, psl, w0, a0, k_k, k_a, w2p, a2p, g2):
    g1, g2in = psl[:, :LANES], psl[:, LANES:]
    log_w = -softplus(-(w0 + bdot(jnp.tanh(g1), w2p))) - 0.5
    lw = -jnp.exp(log_w)
    a = jax.nn.sigmoid(a0 + bdot(g1, a2p))
    gate = bdot(jax.nn.sigmoid(g2in), g2)
    kkr = pk * k_k
    kk = kkr / jnp.maximum(jnp.sqrt(seg2sum(kkr * kkr)), 1e-12)
    kmod = pk * (1.0 + (a - 1.0) * k_a)
    return lw, kmod, kk, a, gate


def fn_rpost(y, r, kmod, v, gate, r_k, ln_w, ln_b):
    inv_n = 1.0 / RW_HEAD
    mean = seg2sum(y) * inv_n
    d = y - mean
    var = seg2sum(d * d) * inv_n
    yn = d * lax.rsqrt(var + RW_GN_EPS) * ln_w + ln_b
    bonus = seg2sum(r * kmod * r_k) * v
    return ((yn + bonus) * gate,)


def fn_xcore(q, k, v):
    s = bdot_nt(q, k) * (LANES ** -0.5)
    p = jax.nn.softmax(s, axis=-1)
    return (bdot(p, v),)


def fn_final(h, tgt, w):
    def loss_fn(h, w):
        return 0.5 * jnp.sum(jnp.mean(jnp.square(rms(h, w) - tgt), axis=-1))

    val, (dh, dw) = jax.value_and_grad(loss_fn, argnums=(0, 1))(h, w)
    return dh, dh.astype(bf16), dw, jnp.full((8, LANES), val, f32)


def fn_sumcols(n):
    def fn(x):
        w = x.shape[1] // n
        s = x[:, :w]
        for q in range(1, n):
            s = s + x[:, q * w:(q + 1) * w]
        return (s,)
    return fn


def _tri(c):
    ii = lax.broadcasted_iota(jnp.int32, (c, c), 0)
    jj = lax.broadcasted_iota(jnp.int32, (c, c), 1)
    return ii, jj


def _neumann_raw(m, steps):
    c = m.shape[-1]
    ii, jj = _tri(c)
    eye = (ii == jj).astype(f32)
    t, p = eye + m, m
    for _ in range(steps):
        p = _raw_dot(p, p, "nn", P_INV)
        t = _raw_dot(t, eye + p, "nn", P_INV)
    return t


@functools.partial(jax.custom_vjp, nondiff_argnums=(1,))
def _neumann_inverse(m, steps):
    return _neumann_raw(m, steps)


def _neumann_fwd(m, steps):
    t = _neumann_raw(m, steps)
    return t, t


def _neumann_bwd(steps, t, g):
    return (_raw_dot(_raw_dot(t, g, "tn", P_INV), t, "nt", P_INV),)


_neumann_inverse.defvjp(_neumann_fwd, _neumann_bwd)


def cumsum_rows(x):
    t = x.shape[1]
    ii, jj = _tri(t)
    tri = jnp.broadcast_to((ii >= jj).astype(f32), (x.shape[0], t, t))
    return pdot(tri, x, "nn", P_CUMSUM)


def gdn_chunk(q, k, v, gb, bb, s):
    c = q.shape[1]
    ii, jj = _tri(c)
    low = ii >= jj
    gcb = cumsum_rows(gb)
    gl = jnp.sum(gb, axis=1, keepdims=True)
    gc_col = gcb[:, :, :c]
    diff = gc_col - jnp.swapaxes(gc_col, 1, 2)
    decay = jnp.where(low, jnp.exp(jnp.where(low, diff, 0.0)), 0.0)
    qs = q * (q.shape[2] ** -0.5)
    kb = k * bb
    a = jnp.where(ii > jj, pdot(kb, k, "nt", P_BULK) * decay, 0.0)
    t = _neumann_inverse(-a, c.bit_length() - 2)
    eg = jnp.exp(gcb)
    u = pdot(t, v * bb, "nn", P_BULK)
    w = pdot(t, kb * eg, "nn", P_BULK)
    attn = pdot(qs, k, "nt", P_BULK) * decay
    kd = k * jnp.exp(gl - gcb)
    v_new = u - pdot(w, s, "nn", P_BULK)
    o = pdot(qs * eg, s, "nn", P_BULK) + pdot(attn, v_new, "nn", P_BULK)
    s_new = s * jnp.exp(gl) + pdot(kd, v_new, "tn", P_BULK)
    return o, s_new


def wkv_chunk(r, lw, k, v, kk, a, s):
    t = r.shape[1]
    ii, jj = _tri(t)
    lo = lax.broadcasted_iota(jnp.int32, r.shape, 2) < RW_HEAD
    cl = cumsum_rows(lw)
    cl_last = jnp.sum(lw, axis=1, keepdims=True)
    al = -kk * jnp.exp(cl - lw)
    be = (a * kk) * jnp.exp(-cl)
    kt = k * jnp.exp(-cl)
    rt = r * jnp.exp(cl)

    def dot(xa, xb, mode="nn"):
        return pdot(xa, xb, mode, P_BULK)

    def pair(xa, xb, msk):
        m_lo = dot(jnp.where(lo, xa, 0.0), xb, "nt")
        m_hi = dot(jnp.where(lo, 0.0, xa), xb, "nt")
        return jnp.where(msk, m_lo, 0.0), jnp.where(msk, m_hi, 0.0)

    def sel(x_lo, x_hi):
        return jnp.where(lo, x_lo, x_hi)

    ab = pair(al, be, ii > jj)
    ak = pair(al, kt, ii > jj)
    rb = pair(rt, be, ii >= jj)
    rk = pair(rt, kt, ii >= jj)
    x = dot(al, s, "nt") + sel(dot(ak[0], v), dot(ak[1], v))
    steps = t.bit_length() - 2
    u = sel(dot(_neumann_inverse(ab[0], steps), x), dot(_neumann_inverse(ab[1], steps), x))
    y = dot(rt, s, "nt") + sel(dot(rb[0], u) + dot(rk[0], v), dot(rb[1], u) + dot(rk[1], v))
    vi = lax.broadcasted_iota(jnp.int32, s.shape, 1) < RW_HEAD
    ki = lax.broadcasted_iota(jnp.int32, s.shape, 2) < RW_HEAD
    s_new = jnp.where(vi == ki, (s + dot(u, be, "tn") + dot(v, kt, "tn")) * jnp.exp(cl_last), 0.0)
    return y, s_new


def _scan_group(ncol, offs):
    g = SCAN_GROUP
    while g > 1 and (ncol % g or any(o % g for o in offs)):
        g //= 2
    return g


def scan_fwd(name, chunk_fn, ins, *, rows, chunk, ncol):
    n = rows // chunk
    n_in = len(ins)
    grp = _scan_group(ncol, [off for _, off in ins])

    def body(*refs):
        o_ref, st_ref, s_scr = refs[n_in:]

        @pl.when(pl.program_id(1) == 0)
        def _():
            s_scr[...] = jnp.zeros_like(s_scr)

        cols = [slice(b * LANES, (b + 1) * LANES) for b in range(grp)]
        s = s_scr[...]
        st_ref[...] = s
        o, s_new = chunk_fn(*[jnp.stack([r[:, c] for c in cols]) for r in refs[:n_in]], s)
        for b, c in enumerate(cols):
            o_ref[:, c] = o[b]
        s_scr[...] = s_new

    def spec(off):
        return pl.BlockSpec((chunk, grp * LANES), lambda h, c: (c, h + off // grp))

    return pl.pallas_call(
        body, name=name, grid=(ncol // grp, n), in_specs=[spec(off) for _, off in ins],
        out_specs=[spec(0), pl.BlockSpec((grp, None, LANES, LANES), lambda h, c: (h, c, 0, 0))],
        out_shape=[jax.ShapeDtypeStruct((rows, ncol * LANES), f32),
                   jax.ShapeDtypeStruct((ncol, n, LANES, LANES), f32)],
        scratch_shapes=[pltpu.VMEM((grp, LANES, LANES), f32)], compiler_params=_params(2))(*[a for a, _ in ins])


def scan_bwd(name, chunk_fn, ins, states, d_out, d_off, *, rows, chunk, ncol):
    n = rows // chunk
    n_in = len(ins)
    grp = _scan_group(ncol, [off for _, off in ins] + [d_off])

    def body(*refs):
        st_ref, do_ref = refs[n_in:n_in + 2]
        g_refs = refs[n_in + 2:2 * n_in + 2]
        ds_scr = refs[-1]

        @pl.when(pl.program_id(1) == 0)
        def _():
            ds_scr[...] = jnp.zeros_like(ds_scr)

        cols = [slice(b * LANES, (b + 1) * LANES) for b in range(grp)]

        def batch(ref):
            return jnp.stack([ref[:, c] for c in cols])

        _, pull = jax.vjp(chunk_fn, *[batch(r) for r in refs[:n_in]], st_ref[...])
        gs = pull((batch(do_ref), ds_scr[...]))
        for ref, g in zip(g_refs, gs[:n_in]):
            for b, c in enumerate(cols):
                ref[:, c] = g[b]
        ds_scr[...] = gs[n_in]

    def spec(off):
        return pl.BlockSpec((chunk, grp * LANES), lambda h, c: (n - 1 - c, h + off // grp))

    st_spec = pl.BlockSpec((grp, None, LANES, LANES), lambda h, c: (h, n - 1 - c, 0, 0))
    return pl.pallas_call(
        body, name=name, grid=(ncol // grp, n), in_specs=[spec(off) for _, off in ins] + [st_spec, spec(d_off)],
        out_specs=[spec(0)] * n_in, out_shape=[jax.ShapeDtypeStruct((rows, ncol * LANES), f32)] * n_in,
        scratch_shapes=[pltpu.VMEM((grp, LANES, LANES), f32)],
        compiler_params=_params(2))(*[a for a, _ in ins], states, d_out)


def flip_exchange(name, arrs, flips, n_slots, slot_of, src_of, with_self):
    n = len(arrs)
    nf = len(flips)

    def body(*refs):
        ins, outs = refs[:n], refs[n:2 * n]
        send, recv, lsem = refs[2 * n:]
        me = (lax.axis_index("x"), lax.axis_index("y"), lax.axis_index("c"))
        copies = []
        for k in range(n):
            if with_self:
                cp = pltpu.make_async_copy(src_of(ins[k], me), outs[k].at[slot_of(me)], lsem.at[k])
                cp.start()
                copies.append(cp)
            for j, fl in enumerate(flips):
                peer = tuple(1 - m if f else m for m, f in zip(me, fl))
                cp = pltpu.make_async_remote_copy(
                    src_ref=src_of(ins[k], peer), dst_ref=outs[k].at[slot_of(me)], send_sem=send.at[k, j],
                    recv_sem=recv.at[k, j], device_id=peer, device_id_type=MESH)
                cp.start()
                copies.append(cp)
        for cp in copies:
            cp.wait()

    def out_sds(a):
        blk = src_of(jax.ShapeDtypeStruct(a.shape, a.dtype), None)
        return jax.ShapeDtypeStruct((n_slots,) + tuple(blk), a.dtype)

    any_spec = pl.BlockSpec(memory_space=pl.ANY)
    return pl.pallas_call(
        body, name=name, in_specs=[any_spec] * n, out_specs=[any_spec] * n, out_shape=[out_sds(a) for a in arrs],
        scratch_shapes=[pltpu.SemaphoreType.DMA((n, nf)), pltpu.SemaphoreType.DMA((n, nf)),
                        pltpu.SemaphoreType.DMA((n,))],
        compiler_params=pltpu.CompilerParams(has_side_effects=True))(*arrs)


_CHIP_FLIPS = ((1, 0, 0), (0, 1, 0), (1, 1, 0))
_ALL_FLIPS = ((0, 0, 1), (0, 1, 0), (0, 1, 1), (1, 0, 0), (1, 0, 1), (1, 1, 0), (1, 1, 1))


def _whole(ref, pos):
    return ref.shape if pos is None else ref


def _chip_block(ref, pos):
    return ref.shape[1:] if pos is None else ref.at[2 * pos[0] + pos[1]]


def _chip_slot(p):
    return 2 * p[0] + p[1]


def gather_chips(name, arrs):
    return flip_exchange(name, arrs, _CHIP_FLIPS, 4, _chip_slot, _whole, True)


def scatter_chips(name, arrs):
    return flip_exchange(name, arrs, _CHIP_FLIPS, 4, _chip_slot, _chip_block, True)


_HBM = pl.BlockSpec(memory_space=pltpu.HBM)
_SEM = pl.BlockSpec(memory_space=pltpu.SEMAPHORE)
_DATAFLOW = pltpu.SideEffectType.DATAFLOW_SIDE_EFFECTING


def _chip_copies(srcs, lands, send, recv, src_of):
    me = (lax.axis_index("x"), lax.axis_index("y"), lax.axis_index("c"))
    copies = []
    for k, (src, land) in enumerate(zip(srcs, lands)):
        for j, fl in enumerate(_CHIP_FLIPS):
            peer = tuple(1 - m if f else m for m, f in zip(me, fl))
            q = k * len(_CHIP_FLIPS) + j
            copies.append(pltpu.make_async_remote_copy(
                src_ref=src_of(src, peer), dst_ref=land.at[_chip_slot(me)], send_sem=send.at[q],
                recv_sem=recv.at[q], device_id=peer, device_id_type=MESH))
    return copies


def chips_start(name, arrs, src_of):
    n = len(arrs)
    me = _chip_slot((lax.axis_index("x"), lax.axis_index("y")))
    lands = []
    for a in arrs:
        blk = tuple(src_of(jax.ShapeDtypeStruct(a.shape, a.dtype), None))
        own = a if src_of is _whole else lax.dynamic_index_in_dim(a, me, 0, keepdims=False)
        land = lax.dynamic_update_index_in_dim(lax.empty((4,) + blk, a.dtype), own, me, 0)
        lands.append(pltpu.with_memory_space_constraint(land, pltpu.HBM))
    srcs = [pltpu.with_memory_space_constraint(a, pltpu.HBM) for a in arrs]

    def body(*refs):
        send, recv = refs[2 * n], refs[2 * n + 1]
        for cp in _chip_copies(refs[:n], refs[n:2 * n], send, recv, src_of):
            cp.start()
        refs[-1][...] = jnp.zeros_like(refs[-1])

    sems = pltpu.SemaphoreType.DMA((n * len(_CHIP_FLIPS),))
    outs = pl.pallas_call(
        body, name=name, in_specs=[_HBM] * (2 * n),
        out_shape=(sems, sems, *[pltpu.HBM(a.shape, a.dtype) for a in srcs + lands],
                   jax.ShapeDtypeStruct((8, LANES), f32)),
        out_specs=(_SEM, _SEM, *[_HBM] * (2 * n), pl.BlockSpec(memory_space=pltpu.VMEM)),
        input_output_aliases={i: 2 + i for i in range(2 * n)},
        compiler_params=pltpu.CompilerParams(has_side_effects=_DATAFLOW))(*srcs, *lands)
    return (outs[0], outs[1], list(outs[2:2 + n]), list(outs[2 + n:2 + 2 * n]), src_of), outs[-1][0, 0]


def chips_wait(name, state, after):
    send, recv, srcs, lands, src_of = state
    n = len(srcs)

    def body(*refs):
        send, recv = refs[2 * n], refs[2 * n + 1]
        for cp in _chip_copies(refs[:n], refs[n:2 * n], send, recv, src_of):
            cp.wait_send()
            cp.wait_recv()

    outs = pl.pallas_call(
        body, name=name, in_specs=[_HBM] * (2 * n) + [_SEM, _SEM, pl.BlockSpec(memory_space=pl.ANY)],
        out_shape=tuple(pltpu.HBM(a.shape, a.dtype) for a in srcs + lands), out_specs=tuple([_HBM] * (2 * n)),
        input_output_aliases={i: i for i in range(2 * n)},
        compiler_params=pltpu.CompilerParams(has_side_effects=_DATAFLOW))(*srcs, *lands, send, recv, after)
    return list(outs[n:])


def gather_chips_halves(name, arrs):
    n = len(arrs)
    nf = len(_CHIP_FLIPS)
    split = [a.shape[0] % 32 == 0 for a in arrs]

    def body(*refs):
        ins, outs = refs[:n], refs[n:2 * n]
        send1, recv1, send2, recv2, lsem = refs[2 * n:]
        me = (lax.axis_index("x"), lax.axis_index("y"), lax.axis_index("c"))
        sib = (me[0], me[1], 1 - me[2])
        peers = [tuple(1 - m if f else m for m, f in zip(me, fl)) for fl in _CHIP_FLIPS]
        local, first, second = [], [], []
        for k in range(n):
            cp = pltpu.make_async_copy(ins[k], outs[k].at[_chip_slot(me)], lsem.at[k])
            cp.start()
            local.append(cp)
            half = ins[k].shape[0] // 2
            rows = pl.ds(pl.multiple_of(me[2] * half, 16), half) if split[k] else pl.ds(0, ins[k].shape[0])
            for j, peer in enumerate(peers):
                cp = pltpu.make_async_remote_copy(
                    src_ref=ins[k].at[rows], dst_ref=outs[k].at[_chip_slot(me), rows], send_sem=send1.at[k, j],
                    recv_sem=recv1.at[k, j], device_id=peer, device_id_type=MESH)
                cp.start()
                first.append((k, j, rows, cp))
        for k, j, rows, cp in first:
            cp.wait_recv()
            if split[k]:
                got = outs[k].at[_chip_slot(peers[j]), rows]
                fwd = pltpu.make_async_remote_copy(src_ref=got, dst_ref=got, send_sem=send2.at[k, j],
                                                   recv_sem=recv2.at[k, j], device_id=sib, device_id_type=MESH)
                fwd.start()
                second.append(fwd)
        for _, _, _, cp in first:
            cp.wait_send()
        for cp in second:
            cp.wait()
        for cp in local:
            cp.wait()

    any_spec = pl.BlockSpec(memory_space=pl.ANY)
    sems = pltpu.SemaphoreType.DMA((n, nf))
    return pl.pallas_call(
        body, name=name, in_specs=[any_spec] * n, out_specs=[any_spec] * n,
        out_shape=[jax.ShapeDtypeStruct((4,) + a.shape, a.dtype) for a in arrs],
        scratch_shapes=[sems, sems, sems, sems, pltpu.SemaphoreType.DMA((n,))],
        compiler_params=pltpu.CompilerParams(has_side_effects=True))(*arrs)


def swap_sibling(name, arrs):
    outs = flip_exchange(name, arrs, ((0, 0, 1),), 1, lambda p: 0, _whole, False)
    return [o[0] for o in outs]


def gather_all(arrs):
    return flip_exchange("gather_all", arrs, _ALL_FLIPS, 8, lambda p: 4 * p[0] + 2 * p[1] + p[2], _whole, True)


def _row_tile(nr, nc, n_arrays):
    budget = (20 << 20) // (n_arrays * 2 * 4 * max(nc, LANES))
    t = min(nr, budget) // 16 * 16
    while t > 0 and nr % t:
        t -= 16
    return t if t > 0 else nr


def sum_slots(name, x):
    ns, nr, nc = x.shape
    tile = _row_tile(nr, nc, ns + 1)

    def body(x_ref, o_ref):
        s = x_ref[0].astype(f32)
        for q in range(1, ns):
            s = s + x_ref[q].astype(f32)
        o_ref[...] = s

    return pl.pallas_call(
        body, name=name, grid=(nr // tile,), in_specs=[pl.BlockSpec((ns, tile, nc), lambda i: (0, i, 0))],
        out_specs=pl.BlockSpec((tile, nc), lambda i: (i, 0)), out_shape=jax.ShapeDtypeStruct((nr, nc), f32),
        compiler_params=_params(1))(x)


def adamw(name, w, g_parts, m, v):
    nr, nc = w.shape[-2:]
    n_g = len(g_parts)
    tile = _row_tile(nr, nc, 7 + n_g)

    def body(*refs):
        w_ref, m_ref, v_ref = refs[:3]
        g = refs[3][...]
        for r in refs[4:3 + n_g]:
            g = g + r[...]
        g_ref, d_ref, nm_ref, nv_ref = refs[3 + n_g:]
        nm = ADAM_B1 * m_ref[...] + (1.0 - ADAM_B1) * g
        nv = ADAM_B2 * v_ref[...] + (1.0 - ADAM_B2) * jnp.square(g)
        m_hat = nm / (1.0 - ADAM_B1 ** ADAM_STEP)
        v_hat = nv / (1.0 - ADAM_B2 ** ADAM_STEP)
        g_ref[...] = g
        d_ref[...] = -ADAM_LR * (m_hat / (jnp.sqrt(v_hat) + ADAM_EPS) + ADAM_WD * w_ref[...])
        nm_ref[...] = nm
        nv_ref[...] = nv

    spec = pl.BlockSpec((tile, nc), lambda i: (i, 0))
    spec3 = pl.BlockSpec((None, tile, nc), lambda i: (0, i, 0)) if w.ndim == 3 else spec
    return pl.pallas_call(
        body, name=name, grid=(nr // tile,), in_specs=[spec3] * 3 + [spec] * n_g, out_specs=[spec3] * 4,
        out_shape=[jax.ShapeDtypeStruct(w.shape, f32)] * 4, compiler_params=_params(1))(w, m, v, *g_parts)


def adamw_packed(w, g8, m, v):
    nr, nc = w.shape

    def body(w_ref, g_ref, m_ref, v_ref, go_ref, d_ref, nm_ref, nv_ref):
        g = g_ref[0]
        for q in range(1, 8):
            g = g + g_ref[q]
        nm = ADAM_B1 * m_ref[...] + (1.0 - ADAM_B1) * g
        nv = ADAM_B2 * v_ref[...] + (1.0 - ADAM_B2) * jnp.square(g)
        m_hat = nm / (1.0 - ADAM_B1 ** ADAM_STEP)
        v_hat = nv / (1.0 - ADAM_B2 ** ADAM_STEP)
        go_ref[...] = g
        d_ref[...] = -ADAM_LR * (m_hat / (jnp.sqrt(v_hat) + ADAM_EPS) + ADAM_WD * w_ref[...])
        nm_ref[...] = nm
        nv_ref[...] = nv

    return pl.pallas_call(body, name="adamw_packed", out_shape=[jax.ShapeDtypeStruct((nr, nc), f32)] * 4,
                          compiler_params=pltpu.CompilerParams(vmem_limit_bytes=VMEM_LIMIT))(w, g8, m, v)


def _pack(vectors):
    rows = []
    for a in vectors:
        flat = a.reshape(-1).astype(f32)
        pad = (-flat.shape[0]) % LANES
        rows.append(jnp.pad(flat, (0, pad)).reshape(-1, LANES))
    packed = jnp.concatenate(rows, axis=0)
    return jnp.pad(packed, ((0, (-packed.shape[0]) % 8), (0, 0)))


def _unpack(packed, like):
    out, r = [], 0
    for a in like:
        n = a.size
        nr = -(-n // LANES)
        out.append(packed[r:r + nr].reshape(-1)[:n].reshape(a.shape))
        r += nr
    return out


def kernel(x, mem, mix_norm_w, w_in, dn_conv_w, dn_a_log, dn_dt_bias, dn_norm_w, rw_mu, rw_w0, rw_w2, rw_a0, rw_a2, rw_g2, rw_k_k, rw_k_a, rw_r_k, rw_ln_w, rw_ln_b, w_out, xa_norm_w, mem_norm_w, xa_wq, xa_wk, xa_wv, xa_wo, ffn_norm_w, ffn_w1, ffn_w2, final_norm_w, loss_target, m_mix_norm_w, m_w_in, m_dn_conv_w, m_dn_a_log, m_dn_dt_bias, m_dn_norm_w, m_rw_mu, m_rw_w0, m_rw_w2, m_rw_a0, m_rw_a2, m_rw_g2, m_rw_k_k, m_rw_k_a, m_rw_r_k, m_rw_ln_w, m_rw_ln_b, m_w_out, m_xa_norm_w, m_mem_norm_w, m_xa_wq, m_xa_wk, m_xa_wv, m_xa_wo, m_ffn_norm_w, m_ffn_w1, m_ffn_w2, m_final_norm_w, v_mix_norm_w, v_w_in, v_dn_conv_w, v_dn_a_log, v_dn_dt_bias, v_dn_norm_w, v_rw_mu, v_rw_w0, v_rw_w2, v_rw_a0, v_rw_a2, v_rw_g2, v_rw_k_k, v_rw_k_a, v_rw_r_k, v_rw_ln_w, v_rw_ln_b, v_w_out, v_xa_norm_w, v_mem_norm_w, v_xa_wq, v_xa_wk, v_xa_wv, v_xa_wo, v_ffn_norm_w, v_ffn_w1, v_ffn_w2, v_final_norm_w):
    weights = dict(mix_norm_w=mix_norm_w, w_in=w_in, dn_conv_w=dn_conv_w, dn_a_log=dn_a_log, dn_dt_bias=dn_dt_bias, dn_norm_w=dn_norm_w, rw_mu=rw_mu, rw_w0=rw_w0, rw_w2=rw_w2, rw_a0=rw_a0, rw_a2=rw_a2, rw_g2=rw_g2, rw_k_k=rw_k_k, rw_k_a=rw_k_a, rw_r_k=rw_r_k, rw_ln_w=rw_ln_w, rw_ln_b=rw_ln_b, w_out=w_out, xa_norm_w=xa_norm_w, mem_norm_w=mem_norm_w, xa_wq=xa_wq, xa_wk=xa_wk, xa_wv=xa_wv, xa_wo=xa_wo, ffn_norm_w=ffn_norm_w, ffn_w1=ffn_w1, ffn_w2=ffn_w2, final_norm_w=final_norm_w)
    mom_m = dict(mix_norm_w=m_mix_norm_w, w_in=m_w_in, dn_conv_w=m_dn_conv_w, dn_a_log=m_dn_a_log, dn_dt_bias=m_dn_dt_bias, dn_norm_w=m_dn_norm_w, rw_mu=m_rw_mu, rw_w0=m_rw_w0, rw_w2=m_rw_w2, rw_a0=m_rw_a0, rw_a2=m_rw_a2, rw_g2=m_rw_g2, rw_k_k=m_rw_k_k, rw_k_a=m_rw_k_a, rw_r_k=m_rw_r_k, rw_ln_w=m_rw_ln_w, rw_ln_b=m_rw_ln_b, w_out=m_w_out, xa_norm_w=m_xa_norm_w, mem_norm_w=m_mem_norm_w, xa_wq=m_xa_wq, xa_wk=m_xa_wk, xa_wv=m_xa_wv, xa_wo=m_xa_wo, ffn_norm_w=m_ffn_norm_w, ffn_w1=m_ffn_w1, ffn_w2=m_ffn_w2, final_norm_w=m_final_norm_w)
    mom_v = dict(mix_norm_w=v_mix_norm_w, w_in=v_w_in, dn_conv_w=v_dn_conv_w, dn_a_log=v_dn_a_log, dn_dt_bias=v_dn_dt_bias, dn_norm_w=v_dn_norm_w, rw_mu=v_rw_mu, rw_w0=v_rw_w0, rw_w2=v_rw_w2, rw_a0=v_rw_a0, rw_a2=v_rw_a2, rw_g2=v_rw_g2, rw_k_k=v_rw_k_k, rw_k_a=v_rw_k_a, rw_r_k=v_rw_r_k, rw_ln_w=v_rw_ln_w, rw_ln_b=v_rw_ln_b, w_out=v_w_out, xa_norm_w=v_xa_norm_w, mem_norm_w=v_mem_norm_w, xa_wq=v_xa_wq, xa_wk=v_xa_wk, xa_wv=v_xa_wv, xa_wo=v_xa_wo, ffn_norm_w=v_ffn_norm_w, ffn_w1=v_ffn_w1, ffn_w2=v_ffn_w2, final_norm_w=v_final_norm_w)
    names = list(weights)

    seq, d = x.shape[1], x.shape[2]
    dnw = d // 2
    rww = d - dnw
    nh, nb = dnw // LANES, rww // LANES
    n_mem = mem.shape[1]
    lw_dim, la_dim, lg_dim = rw_w2.shape[1], rw_a2.shape[1], rw_g2.shape[1]
    assert lw_dim + la_dim == LANES and lg_dim == LANES and dnw % LANES == 0 and rww % LANES == 0
    xs, mems, tgt = x[0], mem[0], loss_target[0]

    col_sharded = ("w_in", "xa_wo", "ffn_w1", "dn_conv_w", "rw_w2", "rw_a2", "rw_g2")
    row_sharded = ("w_out", "xa_wq", "xa_wk", "xa_wv", "ffn_w2")
    f32_payload = ("dn_conv_w", "rw_w2", "rw_a2", "rw_g2")
    sharded = col_sharded + row_sharded
    payload = {n: weights[n][0].astype(f32 if n in f32_payload else bf16) for n in sharded}
    shard_w = w_in.shape[2]
    pad_w = -(-shard_w // LANES) * LANES
    payload["w_in"] = jnp.pad(payload["w_in"], ((0, 0), (0, pad_w - shard_w)))
    first = ("w_in", "dn_conv_w", "rw_w2", "rw_a2", "rw_g2")
    mid = ("w_out", "xa_wq", "xa_wk", "xa_wv", "xa_wo")
    late = ("ffn_w1", "ffn_w2")
    gathered = dict(zip(first, gather_chips_halves("gather_first", [payload[n] for n in first])))
    ordered = lax.optimization_barrier(([gathered[n] for n in first], [payload[n] for n in mid + late]))
    gathered = dict(zip(first, ordered[0]))
    payload.update(zip(mid + late, ordered[1]))
    mid_state, tok_mid = chips_start("gather_mid_start", [payload[n] for n in mid], _whole)
    late_state, tok_late = chips_start("gather_late_start", [payload[n] for n in late], _whole)
    mix_norm_w_in = mix_norm_w + (tok_mid + tok_late)

    def full(n):
        g = gathered[n]
        if n in col_sharded:
            return g.transpose(1, 0, 2).reshape(g.shape[1], 4 * g.shape[2])
        return g.reshape(4 * g.shape[1], g.shape[2])

    c_rw0 = 4 * dnw + 2 * nh
    c_rw = 3 * rww
    eb_ab, eb_r = 4 * nh, 4 * nh + 1
    eb_l1 = eb_r + 3 * nb
    n_ext = -(-(eb_l1 + 2) // 4) * 4
    tbl_fwd, tbl_bwd, per_shard = w_in_layout(4 * shard_w, shard_w, c_rw0, LANES - 2 * nh, n_ext)
    w_ext = lane_regroup(
        "w_in_regroup", gathered["w_in"], tbl_fwd,
        lambda p: pl.BlockSpec((None, d, LANES),
                               lambda i, j, t: (t[REGROUP_FIELDS * p, j], 0, t[REGROUP_FIELDS * p + 1, j])),
        pl.BlockSpec((d, LANES), lambda i, j, t: (0, j)), (d, n_ext * LANES), (1, n_ext), pad_w)
    conv_w = full("dn_conv_w")
    w2p = jnp.concatenate([full("rw_w2"), jnp.zeros((la_dim, rww), f32)], axis=0)
    a2p = jnp.concatenate([jnp.zeros((lw_dim, rww), f32), full("rw_a2")], axis=0)
    g2 = full("rw_g2")
    xaw = xa_wq.shape[2]
    nxh = xaw // LANES
    ffn = 4 * ffn_w1.shape[2]

    def lane_row(vec):
        return jnp.pad(vec.reshape(1, -1), ((0, 0), (0, LANES - vec.size)))

    alog_row, dtb_row = lane_row(dn_a_log), lane_row(dn_dt_bias)
    head_of_col = jnp.arange(dnw)[None, :] // LANES
    e_g = (jnp.arange(LANES)[:, None] == head_of_col).astype(f32)
    e_b = (jnp.arange(LANES)[:, None] == head_of_col + nh).astype(f32)
    mu_main, mu_small = rw_mu[:, :c_rw], rw_mu[:, c_rw:]
    r_k_row = rw_r_k.reshape(1, rww)
    fnw = final_norm_w.reshape(1, d)
    qb, kb_, vb, zb = 0, nh, 2 * nh, 3 * nh
    rb0 = eb_r
    cc = lambda w_, o_: ("constc", w_, o_)
    rc = lambda o_: ("rowc", LANES, o_)

    (u16,) = rowcall("norm_mix", fn_rms, [(xs, "row"), (mix_norm_w_in, "const")], [(seq, d, bf16, "row")], rows=seq)
    p_main = p_small = mm("in_proj", u16, w_ext)

    fn_gconv = make_fn_gconv(2 * nh)
    (qkv,) = rowcall("gdn_conv", fn_gconv, [(p_main, rc(0)), (conv_w, cc(LANES, 0))],
                     [(seq, 3 * dnw, f32, "rowc")], rows=seq, tile=seq, ncol=3 * nh)
    gate_ins = [(p_small, rc(eb_ab)), (alog_row, "const"), (dtb_row, "const"), (e_g, "const"), (e_b, "const")]
    g_b, beta_b = rowcall("gdn_gate", fn_ggate, gate_ins, [(seq, dnw, f32, "row")] * 2, rows=seq)
    gdn_ins = [(qkv, qb), (qkv, kb_), (qkv, vb), (g_b, 0), (beta_b, 0)]
    o_raw, gdn_states = scan_fwd("gdn_scan", gdn_chunk, gdn_ins, rows=seq, chunk=GDN_CHUNK, ncol=nh)
    gpost_ins = [(o_raw, rc(0)), (p_main, rc(zb)), (dn_norm_w, "const")]
    (o_dn,) = rowcall("gdn_post", fn_gpost, gpost_ins, [(seq, dnw, bf16, "rowc")], rows=seq, ncol=nh)

    (prw,) = rowcall("rw_lerp_main", fn_lerp, [(p_main, rc(rb0)), (mu_main, cc(LANES, 0))],
                     [(seq, c_rw, f32, "rowc")], rows=seq, tile=seq, ncol=3 * nb)
    (psl,) = rowcall("rw_lerp_small", fn_lerp, [(p_small, rc(eb_l1)), (mu_small, cc(LANES, 0))],
                     [(seq, 2 * LANES, f32, "rowc")], rows=seq, tile=seq, ncol=2)
    rprep_ins = [(prw, rc(nb)), (psl, "row"), (rw_w0, cc(LANES, 0)), (rw_a0, cc(LANES, 0)), (rw_k_k, cc(LANES, 0)),
                 (rw_k_a, cc(LANES, 0)), (w2p, cc(LANES, 0)), (a2p, cc(LANES, 0)), (g2, cc(LANES, 0))]
    lw, kmod, kk, a_rw, gate = rowcall("rw_prep", fn_rprep, rprep_ins, [(seq, rww, f32, "rowc")] * 5,
                                        rows=seq, ncol=nb)
    wkv_ins = [(prw, 0), (lw, 0), (kmod, 0), (prw, 2 * nb), (kk, 0), (a_rw, 0)]
    y_rw, wkv_states = scan_fwd("wkv_scan", wkv_chunk, wkv_ins, rows=seq, chunk=WKV_CHUNK, ncol=nb)
    rpost_ins = [(y_rw, rc(0)), (prw, rc(0)), (kmod, rc(0)), (prw, rc(2 * nb)), (gate, rc(0)),
                 (r_k_row, cc(LANES, 0)), (rw_ln_w, cc(LANES, 0)), (rw_ln_b, cc(LANES, 0))]
    (o_rw,) = rowcall("rw_post", fn_rpost, rpost_ins, [(seq, rww, bf16, "rowc")], rows=seq, ncol=nb)

    o_cat = jnp.concatenate([o_dn, o_rw], axis=1)
    gathered.update(zip(mid, chips_wait("gather_mid_wait", mid_state, o_cat)))
    w_out_f, wq_f, wk_f, wv_f, wo_f = full("w_out"), full("xa_wq"), full("xa_wk"), full("xa_wv"), full("xa_wo")
    h1 = mm("out_proj", o_cat, w_out_f, add=xs)

    (hn16,) = rowcall("norm_xa", fn_rms, [(h1, "row"), (xa_norm_w, "const")], [(seq, d, bf16, "row")], rows=seq)
    (mn16,) = rowcall("norm_mem", fn_rms, [(mems, "row"), (mem_norm_w, "const")], [(n_mem, d, bf16, "row")],
                      rows=n_mem)
    q_xa = mm("xa_q", hn16, wq_f)
    k_xa = mm("xa_k", mn16, wk_f)
    v_xa = mm("xa_v", mn16, wv_f)
    xcore_ins = [(q_xa, rc(0)), (k_xa, cc(LANES, 0)), (v_xa, cc(LANES, 0))]
    (o_xa,) = rowcall("xa_core", fn_xcore, xcore_ins, [(seq, xaw, bf16, "rowc")], rows=seq, ncol=nxh)
    h2 = mm("xa_o", o_xa, wo_f, add=h1)

    (fn16,) = rowcall("norm_ffn", fn_rms, [(h2, "row"), (ffn_norm_w, "const")], [(seq, d, bf16, "row")], rows=seq)
    gathered.update(zip(late, chips_wait("gather_late_wait", late_state, fn16)))
    w1_f, w2_f = full("ffn_w1"), full("ffn_w2")
    a1_16, hid16 = mm("ffn_up", fn16, w1_f, epilogue=lambda r: (r, jnp.square(jnp.maximum(r, 0.0))),
                      out_dtypes=(bf16, bf16))
    h3 = mm("ffn_down", hid16, w2_f, add=h2)

    dh3, dh3_16, d_fnw, loss_rows = rowcall(
        "loss_head", fn_final, [(h3, "row"), (tgt, "row"), (fnw, "const")],
        [(seq, d, f32, "row"), (seq, d, bf16, "row"), (1, d, f32, "acc"), (8, LANES, f32, "acc")], rows=seq)

    da1_16 = mm("ffn_down_dx", dh3_16, w2_f, tb=True, extra=[a1_16], out_dtype=bf16,
                epilogue=lambda r, a1: (r * (2.0 * jnp.maximum(a1.astype(f32), 0.0)),))
    g_ffn_w2 = mm("ffn_down_dw", hid16, dh3_16, ta=True, out_dtype=bf16)
    g_ffn_w1 = mm("ffn_up_dw", fn16, da1_16, ta=True, out_dtype=bf16, by_chip=True)
    dfn = mm("ffn_up_dx", da1_16, w1_f, tb=True)

    def by_chip(n, g):
        if g.ndim == 3:
            return g
        if n in col_sharded:
            return g.reshape(g.shape[0], 4, g.shape[1] // 4).transpose(1, 0, 2)
        return g.reshape(4, g.shape[0] // 4, g.shape[1])

    late_g, tok = chips_start("scatter_late_start", [by_chip("ffn_w1", g_ffn_w1), by_chip("ffn_w2", g_ffn_w2)],
                              _chip_block)
    dh2, d_ffn_nw, dh2_16 = rowvjp("norm_ffn_bwd", fn_rms, [(h2, "row"), (ffn_norm_w + tok, "const")],
                                   [[(dfn, "row")]], [0, 1], rows=seq, adds=[(0, dh3, "row")], dup16=[0])

    do_xa = mm("xa_o_dx", dh2_16, wo_f, tb=True)
    g_xa_wo = mm("xa_o_dw", o_xa, dh2_16, ta=True, out_dtype=bf16, by_chip=True)
    dq_xa, dk_xa, dv_xa, dq16 = rowvjp("xa_core_bwd", fn_xcore, xcore_ins, [[(do_xa, rc(0))]], [0, 1, 2],
                                       rows=seq, ncol=nxh, dup16=[0])
    g_xa_wq = mm("xa_q_dw", hn16, dq16, ta=True, out_dtype=bf16)
    dhn = mm("xa_q_dx", dq16, wq_f, tb=True)
    dh1, d_xa_nw, dh1_16 = rowvjp("norm_xa_bwd", fn_rms, [(h1, "row"), (xa_norm_w, "const")], [[(dhn, "row")]],
                                  [0, 1], rows=seq, adds=[(0, dh2, "row")], dup16=[0])
    dk16, dv16 = dk_xa.astype(bf16), dv_xa.astype(bf16)
    g_xa_wk = mm("xa_k_dw", mn16, dk16, ta=True, out_dtype=bf16)
    g_xa_wv = mm("xa_v_dw", mn16, dv16, ta=True, out_dtype=bf16)
    dmn = mm("xa_v_dx", dv16, wv_f, tb=True, add=mm("xa_k_dx", dk16, wk_f, tb=True))
    (d_mem_nw,) = rowvjp("norm_mem_bwd", fn_rms, [(mems, "row"), (mem_norm_w, "const")], [[(dmn, "row")]], [1],
                         rows=n_mem)

    g_w_out = mm("out_proj_dw", o_cat, dh1_16, ta=True, out_dtype=bf16)
    do_cat = mm("out_proj_dx", dh1_16, w_out_f, tb=True)
    mid_grads = dict(w_out=g_w_out, xa_wq=g_xa_wq, xa_wk=g_xa_wk, xa_wv=g_xa_wv, xa_wo=g_xa_wo)
    mid_g, tok = chips_start("scatter_mid_start", [by_chip(n, mid_grads[n]) for n in mid], _chip_block)
    rpost_ins_b = rpost_ins[:5] + [(r_k_row + tok, cc(LANES, 0))] + rpost_ins[6:]

    dy, dr_a, dkmod_a, dv_a, dgate, d_r_k, d_ln_w, d_ln_b = rowvjp(
        "rw_post_bwd", fn_rpost, rpost_ins_b, [[(do_cat, rc(nh))]], [0, 1, 2, 3, 4, 5, 6, 7], rows=seq, ncol=nb)
    dr_b, dlw, dkmod_b, dv_b, dkk, da_rw = scan_bwd("wkv_scan_bwd", wkv_chunk, wkv_ins, wkv_states, dy, 0,
                                                    rows=seq, chunk=WKV_CHUNK, ncol=nb)
    rprep_cts = [[(dlw, rc(0))], [(dkmod_a, rc(0)), (dkmod_b, rc(0))], [(dkk, rc(0))], [(da_rw, rc(0))],
                 [(dgate, rc(0))]]
    dpk, dpsl_parts, d_w0, d_a0, d_k_k, d_k_a, d_w2p, d_a2p, d_g2 = rowvjp(
        "rw_prep_bwd", fn_rprep, rprep_ins, rprep_cts, [0, 1, 2, 3, 4, 5, 6, 7, 8], rows=seq, ncol=nb)
    (dpsl,) = rowcall("rw_prep_sum", fn_sumcols(nb), [(dpsl_parts, "row")], [(seq, 2 * LANES, f32, "row")], rows=seq)

    def lerp_bwd(tag, p, p_off, mu, mu_off, ct_lists, ncol):
        return rowvjp("rw_lerp_bwd_" + tag, fn_lerp, [(p, rc(p_off)), (mu, cc(LANES, mu_off))], [ct_lists], [0, 1],
                      rows=seq, tile=seq, ncol=ncol, dup16=[0])

    _, dmu_r, dpr16 = lerp_bwd("r", p_main, rb0, mu_main, 0, [(dr_a, rc(0)), (dr_b, rc(0))], nb)
    _, dmu_k, dpk16 = lerp_bwd("k", p_main, rb0 + nb, mu_main, nb, [(dpk, rc(0))], nb)
    _, dmu_v, dpv16 = lerp_bwd("v", p_main, rb0 + 2 * nb, mu_main, 2 * nb, [(dv_a, rc(0)), (dv_b, rc(0))], nb)
    _, dmu_s, dps12_16 = lerp_bwd("small", p_small, eb_l1, mu_small, 0, [(dpsl, rc(0))], 2)

    do_raw, dz, d_dn_nw, dz16 = rowvjp("gdn_post_bwd", fn_gpost, gpost_ins, [[(do_cat, rc(0))]], [0, 1, 2],
                                       rows=seq, ncol=nh, dup16=[1])
    dq_g, dk_g, dv_g, dg_b, dbeta_b = scan_bwd("gdn_scan_bwd", gdn_chunk, gdn_ins, gdn_states, do_raw, 0,
                                               rows=seq, chunk=GDN_CHUNK, ncol=nh)
    dps0, d_alog, d_dtb, dps0_16 = rowvjp("gdn_gate_bwd", fn_ggate, gate_ins, [[(dg_b, "row")], [(dbeta_b, "row")]],
                                          [0, 1, 2], rows=seq, dup16=[0])
    dqkv = jnp.concatenate([dq_g, dk_g, dv_g], axis=1)
    _, d_conv_w, dqkv16 = rowvjp("gdn_conv_bwd", fn_gconv, [(p_main, rc(0)), (conv_w, cc(LANES, 0))],
                                 [[(dqkv, rc(0))]], [0, 1], rows=seq, tile=seq, ncol=3 * nh, dup16=[0])

    dp16 = jnp.concatenate([dqkv16, dz16, dps0_16, dpr16, dpk16, dpv16, dps12_16,
                            jnp.zeros((seq, (n_ext - eb_l1 - 2) * LANES), bf16)], axis=1)
    g_w_ext = mm("in_proj_dw", u16, dp16, ta=True, out_dtype=bf16)
    g_w_in = lane_regroup(
        "w_in_grad_regroup", g_w_ext, tbl_bwd,
        lambda p: pl.BlockSpec((d, LANES), lambda i, j, t: (0, t[REGROUP_FIELDS * p + 1, j])),
        pl.BlockSpec((None, d, LANES), lambda i, j, t: (j // per_shard, 0, j % per_shard)),
        (4, d, pad_w), (1, 4 * per_shard), n_ext * LANES)
    first_grads = dict(w_in=g_w_in, dn_conv_w=d_conv_w, rw_w2=d_w2p[:lw_dim], rw_a2=d_a2p[lw_dim:], rw_g2=d_g2)
    first_g, tok = chips_start("scatter_first_start", [by_chip(n, first_grads[n]) for n in first], _chip_block)

    du = mm("in_proj_dx", dp16, w_ext, tb=True)
    grad_x, d_mix_nw = rowvjp("norm_mix_bwd", fn_rms, [(xs, "row"), (mix_norm_w + tok, "const")], [[(du, "row")]],
                              [0, 1], rows=seq, adds=[(0, dh1, "row")])
    received = dict(zip(mid, chips_wait("scatter_mid_wait", mid_g, grad_x)))
    received.update(zip(late, chips_wait("scatter_late_wait", late_g, grad_x)))

    out = {}

    def reduce_and_update(tag, group):
        partial = [sum_slots("sum_chips_" + n, received[n]) for n in group]
        other = swap_sibling("swap_sibling_" + tag, partial)
        for n, p_mine, p_other in zip(group, partial, other):
            if weights[n].shape[2] % LANES:
                rows, cols = weights[n].shape[1:]
                lin = lambda a: jnp.swapaxes(a, 1, 2).reshape(-1, LANES)
                lin_g = lambda p: p.T[:cols].reshape(-1, LANES)
                res = adamw("adamw_" + n, lin(weights[n]), [lin_g(p_mine), lin_g(p_other)], lin(mom_m[n]),
                            lin(mom_v[n]))
                out[n] = [jnp.swapaxes(r.reshape(1, cols, rows), 1, 2) for r in res]
            else:
                out[n] = adamw("adamw_" + n, weights[n], [p_mine, p_other], mom_m[n], mom_v[n])

    reduce_and_update("rest", mid + late)
    received.update(zip(first, chips_wait("scatter_first_wait", first_g, out["ffn_w2"][1])))
    reduce_and_update("first", first)

    small_names = [n for n in names if n not in sharded]
    small_local = dict(
        mix_norm_w=d_mix_nw, dn_a_log=d_alog[:, :nh], dn_dt_bias=d_dtb[:, :nh], dn_norm_w=d_dn_nw,
        rw_mu=jnp.concatenate([dmu_r, dmu_k, dmu_v, dmu_s], axis=1), rw_w0=d_w0, rw_a0=d_a0, rw_k_k=d_k_k,
        rw_k_a=d_k_a, rw_r_k=d_r_k, rw_ln_w=d_ln_w, rw_ln_b=d_ln_b, xa_norm_w=d_xa_nw, mem_norm_w=d_mem_nw,
        ffn_norm_w=d_ffn_nw, final_norm_w=d_fnw)
    loss_vec = jnp.where(jnp.arange(LANES) == 0, loss_rows[0], 0.0)
    packed, _ = lax.optimization_barrier((_pack([small_local[n] for n in small_names] + [loss_vec]), out["w_in"][0]))
    (g8,) = gather_all([packed])

    packed_like = [weights[n] for n in small_names] + [loss_vec]
    zero = jnp.zeros((LANES,), f32)
    res = adamw_packed(_pack([weights[n] for n in small_names] + [zero]), g8,
                       _pack([mom_m[n] for n in small_names] + [zero]),
                       _pack([mom_v[n] for n in small_names] + [zero]))
    unpacked = [_unpack(r, packed_like) for r in res]
    for i, n in enumerate(small_names):
        out[n] = [u[i] for u in unpacked]
    loss = unpacked[0][-1][0]

    return (loss, grad_x.reshape(x.shape), *[out[n][0] for n in names], *[out[n][1] for n in names],
            *[out[n][2] for n in names], *[out[n][3] for n in names])
```

```python
import functools

import jax
import jax.numpy as jnp
from jax import lax
from jax.experimental import pallas as pl
from jax.experimental.pallas import tpu as pltpu

f32 = jnp.float32
bf16 = jnp.bfloat16
HI = lax.Precision.HIGHEST
MESH = pl.DeviceIdType.MESH

LANES = 128
VMEM_LIMIT = 56 << 20
TOK_TILE = 256
TOK_TILE_BLOCKED = 1024
MM_TILE = 1024
MM_TILE_K = 2048
GDN_CHUNK = 128
WKV_CHUNK = 64
SCAN_GROUP = 8
P_BULK = 1
P_INV = 1
P_RESID = 3
P_CUMSUM = 3
RMS_EPS = 1e-6
RW_GN_EPS = 64e-5
RW_HEAD = 64

ADAM_LR, ADAM_B1, ADAM_B2, ADAM_EPS, ADAM_WD, ADAM_STEP = 0.001, 0.9, 0.999, 1e-08, 0.01, 10


def _params(n_grid):
    return pltpu.CompilerParams(dimension_semantics=("arbitrary",) * n_grid, vmem_limit_bytes=VMEM_LIMIT)


_DIMS = {"nn": (((1,), (0,)), ((), ())), "nt": (((1,), (1,)), ((), ())), "tn": (((0,), (0,)), ((), ()))}
_DIMS_BATCHED = {"nn": (((2,), (1,)), ((0,), (0,))), "nt": (((2,), (2,)), ((0,), (0,))),
                 "tn": (((1,), (1,)), ((0,), (0,)))}


def _raw_dot(a, b, mode, passes):
    dims = (_DIMS if a.ndim == 2 else _DIMS_BATCHED)[mode]
    if passes == 6:
        return lax.dot_general(a.astype(f32), b.astype(f32), dims, precision=HI, preferred_element_type=f32)
    ah, bh = a.astype(bf16), b.astype(bf16)
    r = lax.dot_general(ah, bh, dims, preferred_element_type=f32)
    if passes == 3:
        al = (a - ah.astype(f32)).astype(bf16)
        bl = (b - bh.astype(f32)).astype(bf16)
        r = r + lax.dot_general(al, bh, dims, preferred_element_type=f32)
        r = r + lax.dot_general(ah, bl, dims, preferred_element_type=f32)
    return r


@functools.partial(jax.custom_vjp, nondiff_argnums=(2, 3))
def pdot(a, b, mode, passes):
    return _raw_dot(a, b, mode, passes)


def _pdot_bwd(mode, passes, res, g):
    a, b = res
    if mode == "nn":
        da, db = _raw_dot(g, b, "nt", passes), _raw_dot(a, g, "tn", passes)
    elif mode == "nt":
        da, db = _raw_dot(g, b, "nn", passes), _raw_dot(g, a, "tn", passes)
    else:
        da, db = _raw_dot(b, g, "nt", passes), _raw_dot(a, g, "nn", passes)
    return da.astype(a.dtype), db.astype(b.dtype)


pdot.defvjp(lambda a, b, mode, passes: (_raw_dot(a, b, mode, passes), (a, b)), _pdot_bwd)


def hdot(a, b):
    return pdot(a, b, "nn", 6)


def bdot(a, b):
    return pdot(a, b, "nn", 1)


def bdot_nt(a, b):
    return pdot(a, b, "nt", 1)


def _shift_rows(x, k):
    row = lax.broadcasted_iota(jnp.int32, x.shape, 0)
    return jnp.where(row < k, 0.0, pltpu.roll(x, k, axis=0))


def _unshift_rows(g, k):
    n = g.shape[0]
    row = lax.broadcasted_iota(jnp.int32, g.shape, 0)
    return jnp.where(row >= n - k, 0.0, pltpu.roll(g, n - k, axis=0))


@functools.partial(jax.custom_vjp, nondiff_argnums=(1,))
def tshift(x, k):
    return _shift_rows(x, k)


tshift.defvjp(lambda x, k: (_shift_rows(x, k), None), lambda k, _, g: (_unshift_rows(g, k),))


def rms(x, w):
    x = x.astype(f32)
    return x * lax.rsqrt(jnp.mean(x * x, axis=-1, keepdims=True) + RMS_EPS) * w


def softplus(x):
    return jnp.maximum(x, 0.0) + jnp.log(1.0 + jnp.exp(-jnp.abs(x)))


def seg2sum(x):
    lo = lax.broadcasted_iota(jnp.int32, x.shape, 1) < RW_HEAD
    s_lo = jnp.sum(jnp.where(lo, x, 0.0), axis=-1, keepdims=True)
    s_hi = jnp.sum(jnp.where(lo, 0.0, x), axis=-1, keepdims=True)
    return jnp.where(lo, s_lo, s_hi)


def _tile(n, pref):
    if n <= pref:
        return n
    t = pref
    while t >= LANES:
        if n % t == 0:
            return t
        t -= LANES
    return n


def mm(name, a, b, *, ta=False, tb=False, add=None, out_dtype=f32, by_chip=False, epilogue=None, extra=(),
       out_dtypes=None):
    (k, m) = a.shape if ta else a.shape[::-1]
    (n, kb) = b.shape if tb else b.shape[::-1]
    assert k == kb, (name, a.shape, b.shape)
    tm, tk = _tile(m, MM_TILE), _tile(k, MM_TILE_K)
    tn = _tile(n // 4, MM_TILE) if by_chip else _tile(n, MM_TILE)
    nk = k // tk
    dims = (((0,) if ta else (1,), (1,) if tb else (0,)), ((), ()))
    extra = list(extra) + ([] if add is None else [add])
    out_dtypes = [out_dtype] if out_dtypes is None else list(out_dtypes)
    n_extra, n_out = len(extra), len(out_dtypes)

    def body(*refs):
        a_ref, b_ref = refs[:2]
        x_refs = refs[2:2 + n_extra]
        o_refs = refs[2 + n_extra:2 + n_extra + n_out]
        part = lax.dot_general(a_ref[...].astype(bf16), b_ref[...].astype(bf16), dims, preferred_element_type=f32)

        def finish(r):
            xs = [x[...] for x in x_refs]
            if add is not None:
                r = r + xs.pop().astype(f32)
            outs = (r,) if epilogue is None else epilogue(r, *xs)
            for o_ref, o in zip(o_refs, outs):
                o_ref[...] = o.astype(o_ref.dtype)

        if nk == 1:
            finish(part)
            return
        acc = refs[-1]
        kk = pl.program_id(2)

        @pl.when(kk == 0)
        def _():
            acc[...] = part

        @pl.when(kk > 0)
        def _():
            acc[...] += part

        @pl.when(kk == nk - 1)
        def _():
            finish(acc[...])

    a_spec = pl.BlockSpec((tk, tm), lambda i, j, q: (q, i)) if ta else pl.BlockSpec((tm, tk), lambda i, j, q: (i, q))
    b_spec = pl.BlockSpec((tn, tk), lambda i, j, q: (j, q)) if tb else pl.BlockSpec((tk, tn), lambda i, j, q: (q, j))
    x_spec = pl.BlockSpec((tm, tn), lambda i, j, q: (i, j))
    if by_chip:
        per_chip = n // 4 // tn
        o_spec = pl.BlockSpec((None, tm, tn), lambda i, j, q: (j // per_chip, i, j % per_chip))
        o_shape = (4, m, n // 4)
    else:
        o_spec, o_shape = x_spec, (m, n)
    res = pl.pallas_call(
        body, name=name, grid=(m // tm, n // tn, nk), in_specs=[a_spec, b_spec] + [x_spec] * n_extra,
        out_specs=[o_spec] * n_out, out_shape=[jax.ShapeDtypeStruct(o_shape, dt) for dt in out_dtypes],
        scratch_shapes=[pltpu.VMEM((tm, tn), f32)] if nk > 1 else [], compiler_params=_params(3))(a, b, *extra)
    return res[0] if n_out == 1 else res


REGROUP_PIECES = 4


REGROUP_FIELDS = 5


def _regroup_table(n_out, sources_of):
    import numpy as np
    tbl = np.zeros((REGROUP_FIELDS * REGROUP_PIECES, n_out), np.int32)
    for j in range(n_out):
        groups = sorted(sources_of(j).items())
        assert len(groups) <= REGROUP_PIECES, (j, len(groups))
        for p in range(REGROUP_PIECES):
            if p < len(groups):
                key, lanes = groups[p]
                shifts = {q - s for s, q in lanes}
                qs = sorted(q for _, q in lanes)
                assert len(shifts) == 1 and qs == list(range(qs[0], qs[-1] + 1)), (j, key)
                row = (key[0], key[1], shifts.pop(), qs[0], qs[-1] + 1)
            else:
                row = (0, 0, 0, 0, 0)
            tbl[REGROUP_FIELDS * p:REGROUP_FIELDS * (p + 1), j] = row
    return jnp.asarray(tbl)


def lane_regroup(name, src, table, src_spec, out_spec, out_shape, grid, src_width):
    rows = src.shape[-2]

    def body(tbl, *refs):
        o_ref, acc = refs[REGROUP_PIECES:]
        j = pl.program_id(1)
        acc[...] = jnp.zeros_like(acc)
        for p in range(REGROUP_PIECES):
            blk, shift, lo, hi = (tbl[REGROUP_FIELDS * p + f, j] for f in range(1, REGROUP_FIELDS))

            @pl.when(hi > lo)
            def _(p=p, blk=blk, shift=shift, lo=lo, hi=hi):
                lane = lax.broadcasted_iota(jnp.int32, (rows, LANES), 1)
                pi = lax.broadcasted_iota(jnp.int32, (LANES, LANES), 0)
                qi = lax.broadcasted_iota(jnp.int32, (LANES, LANES), 1)
                x = jnp.where(lane < src_width - blk * LANES, refs[p][...], jnp.zeros((), src.dtype))
                sel = jnp.logical_and(qi - pi == shift, jnp.logical_and(qi >= lo, qi < hi))
                acc[...] += jnp.dot(x, sel.astype(src.dtype), preferred_element_type=f32)

        o_ref[...] = acc[...].astype(o_ref.dtype)

    return pl.pallas_call(
        body, name=name, out_shape=jax.ShapeDtypeStruct(out_shape, src.dtype),
        grid_spec=pltpu.PrefetchScalarGridSpec(
            num_scalar_prefetch=1, grid=grid, in_specs=[src_spec(p) for p in range(REGROUP_PIECES)],
            out_specs=out_spec, scratch_shapes=[pltpu.VMEM((rows, LANES), f32)]),
        compiler_params=_params(2))(table, *[src] * REGROUP_PIECES)


def w_in_layout(d_cols, shard_w, c_split, gap, n_blocks):
    def ext_of(c):
        return c if c < c_split else c + gap

    def fwd_sources(j):
        groups = {}
        for q in range(LANES):
            e = j * LANES + q
            c = e if e < c_split else e - gap
            if (c_split <= e < c_split + gap) or c >= d_cols:
                continue
            s, l = divmod(c, shard_w)
            groups.setdefault((s, l // LANES), []).append((l % LANES, q))
        return groups

    per_shard = -(-shard_w // LANES)

    def bwd_sources(j):
        s, b = divmod(j, per_shard)
        groups = {}
        for q in range(LANES):
            l = b * LANES + q
            if l >= shard_w:
                continue
            e = ext_of(s * shard_w + l)
            groups.setdefault((0, e // LANES), []).append((e % LANES, q))
        return groups

    return _regroup_table(n_blocks, fwd_sources), _regroup_table(4 * per_shard, bwd_sources), per_shard


def _in_spec(a, kind, tile):
    if kind == "row":
        return pl.BlockSpec((tile, a.shape[1]), lambda j, i: (i, 0))
    if kind == "const":
        return pl.BlockSpec(a.shape, lambda j, i: (0, 0))
    tag, cw, off = kind
    if tag == "rowc":
        return pl.BlockSpec((tile, cw), lambda j, i: (i, j + off))
    assert tag == "constc", kind
    return pl.BlockSpec((a.shape[0], cw), lambda j, i: (0, j + off))


def rowcall(name, fn, ins, outs, *, rows, tile=None, ncol=1):
    if tile is None:
        tile = min(TOK_TILE_BLOCKED if ncol > 1 else TOK_TILE, rows)
    n_in = len(ins)
    kinds = [o[3] for o in outs]

    def body(*refs):
        j, i = pl.program_id(0), pl.program_id(1)
        res = fn(*[r[...] for r in refs[:n_in]])
        for ref, val, kind in zip(refs[n_in:], res, kinds):
            if kind in ("row", "rowc"):
                ref[...] = val.astype(ref.dtype)
            else:
                first = (i == 0) if kind == "accc" else jnp.logical_and(i == 0, j == 0)

                @pl.when(first)
                def _(ref=ref, val=val):
                    ref[...] = val.astype(ref.dtype)

                @pl.when(jnp.logical_not(first))
                def _(ref=ref, val=val):
                    ref[...] += val.astype(ref.dtype)

    out_shape, out_specs = [], []
    for nr, nc, dtype, kind in outs:
        out_shape.append(jax.ShapeDtypeStruct((nr, nc), dtype))
        if kind == "row":
            out_specs.append(pl.BlockSpec((tile, nc), lambda j, i: (i, 0)))
        elif kind == "rowc":
            out_specs.append(pl.BlockSpec((tile, nc // ncol), lambda j, i: (i, j)))
        elif kind == "acc":
            out_specs.append(pl.BlockSpec((nr, nc), lambda j, i: (0, 0)))
        else:
            out_specs.append(pl.BlockSpec((nr, nc // ncol), lambda j, i: (0, j)))
    return pl.pallas_call(
        body, name=name, grid=(ncol, rows // tile), in_specs=[_in_spec(a, k, tile) for a, k in ins],
        out_specs=out_specs, out_shape=out_shape, compiler_params=_params(2))(*[a for a, _ in ins])


def rowvjp(name, fn, ins, cts, grads, *, rows, tile=None, ncol=1, adds=(), dup16=()):
    n_in = len(ins)
    ct_sizes = [len(c) for c in cts]
    flat_cts = [m for c in cts for m in c]
    n_ct = len(flat_cts)

    def wrapped(*vals):
        xs = list(vals[:n_in])
        gs = vals[n_in:n_in + n_ct]
        extra = vals[n_in + n_ct:]

        def f(*dvars):
            full = list(xs)
            for k, v in zip(grads, dvars):
                full[k] = v
            return fn(*full)

        outs, pull = jax.vjp(f, *[xs[k] for k in grads])
        cot, p = [], 0
        for o, size in zip(outs, ct_sizes):
            g = gs[p].astype(f32)
            for q in range(1, size):
                g = g + gs[p + q].astype(f32)
            cot.append(g.astype(o.dtype))
            p += size
        gv = list(pull(tuple(cot)))
        for (pos, _, _), e in zip(adds, extra):
            gv[pos] = gv[pos] + e.astype(gv[pos].dtype)
        return tuple(gv) + tuple(gv[pos] for pos in dup16)

    outs = []
    for k in grads:
        a, kind = ins[k]
        if kind == "row":
            outs.append((rows, a.shape[1] * ncol, f32, "rowc") if ncol > 1 else (rows, a.shape[1], f32, "row"))
        elif kind == "const":
            outs.append((a.shape[0], a.shape[1], f32, "acc"))
        elif kind[0] == "rowc":
            outs.append((rows, kind[1] * ncol, f32, "rowc"))
        else:
            outs.append((a.shape[0], kind[1] * ncol, f32, "accc"))
    for pos in dup16:
        nr, nc, _, kind = outs[pos]
        outs.append((nr, nc, bf16, kind))
    all_ins = list(ins) + flat_cts + [(a, kind) for _, a, kind in adds]
    return rowcall(name, wrapped, all_ins, outs, rows=rows, tile=tile, ncol=ncol)


def fn_rms(x, w):
    return (rms(x, w),)


def make_fn_gconv(n_norm_blocks):
    def fn(p, cw):
        c = cw[3:4] * p
        for jj in range(3):
            c = c + cw[jj:jj + 1] * tshift(p, 3 - jj)
        s = c * jax.nn.sigmoid(c)
        nrm = s * lax.rsqrt(jnp.sum(s * s, axis=-1, keepdims=True) + 1e-6)
        return (jnp.where(pl.program_id(0) < n_norm_blocks, nrm, s),)
    return fn


def fn_ggate(ps0, alog, dtb, e_g, e_b):
    g = -jnp.exp(alog) * softplus(ps0 + dtb)
    beta = jax.nn.sigmoid(ps0)
    return hdot(g, e_g), hdot(beta, e_b)


def fn_gpost(o, z, nw):
    return (rms(o, nw) * (z * jax.nn.sigmoid(z)),)


def fn_lerp(p, mu):
    return (p + (tshift(p, 1) - p) * mu,)


def fn_rprep(pk, psl, w0, a0, k_k, k_a, w2p, a2p, g2):
    g1, g2in = psl[:, :LANES], psl[:, LANES:]
    log_w = -softplus(-(w0 + bdot(jnp.tanh(g1), w2p))) - 0.5
    lw = -jnp.exp(log_w)
    a = jax.nn.sigmoid(a0 + bdot(g1, a2p))
    gate = bdot(jax.nn.sigmoid(g2in), g2)
    kkr = pk * k_k
    kk = kkr / jnp.maximum(jnp.sqrt(seg2sum(kkr * kkr)), 1e-12)
    kmod = pk * (1.0 + (a - 1.0) * k_a)
    return lw, kmod, kk, a, gate


def fn_rpost(y, r, kmod, v, gate, r_k, ln_w, ln_b):
    inv_n = 1.0 / RW_HEAD
    mean = seg2sum(y) * inv_n
    d = y - mean
    var = seg2sum(d * d) * inv_n
    yn = d * lax.rsqrt(var + RW_GN_EPS) * ln_w + ln_b
    bonus = seg2sum(r * kmod * r_k) * v
    return ((yn + bonus) * gate,)


def fn_xcore(q, k, v):
    s = bdot_nt(q, k) * (LANES ** -0.5)
    p = jax.nn.softmax(s, axis=-1)
    return (bdot(p, v),)


def fn_final(h, tgt, w):
    def loss_fn(h, w):
        return 0.5 * jnp.sum(jnp.mean(jnp.square(rms(h, w) - tgt), axis=-1))

    val, (dh, dw) = jax.value_and_grad(loss_fn, argnums=(0, 1))(h, w)
    return dh, dh.astype(bf16), dw, jnp.full((8, LANES), val, f32)


def fn_sumcols(n):
    def fn(x):
        w = x.shape[1] // n
        s = x[:, :w]
        for q in range(1, n):
            s = s + x[:, q * w:(q + 1) * w]
        return (s,)
    return fn


def _tri(c):
    ii = lax.broadcasted_iota(jnp.int32, (c, c), 0)
    jj = lax.broadcasted_iota(jnp.int32, (c, c), 1)
    return ii, jj


def _neumann_raw(m, steps):
    c = m.shape[-1]
    ii, jj = _tri(c)
    eye = (ii == jj).astype(f32)
    t, p = eye + m, m
    for _ in range(steps):
        p = _raw_dot(p, p, "nn", P_INV)
        t = _raw_dot(t, eye + p, "nn", P_INV)
    resid = eye - t + _raw_dot(m, t, "nn", P_RESID)
    return t + _raw_dot(t, resid, "nn", P_INV)


@functools.partial(jax.custom_vjp, nondiff_argnums=(1,))
def _neumann_inverse(m, steps):
    return _neumann_raw(m, steps)


def _neumann_fwd(m, steps):
    t = _neumann_raw(m, steps)
    return t, t


def _neumann_bwd(steps, t, g):
    return (_raw_dot(_raw_dot(t, g, "tn", P_RESID), t, "nt", P_RESID),)


_neumann_inverse.defvjp(_neumann_fwd, _neumann_bwd)


def cumsum_rows(x):
    t = x.shape[1]
    ii, jj = _tri(t)
    tri = jnp.broadcast_to((ii >= jj).astype(f32), (x.shape[0], t, t))
    return pdot(tri, x, "nn", P_CUMSUM)


def gdn_chunk(q, k, v, gb, bb, s):
    c = q.shape[1]
    ii, jj = _tri(c)
    low = ii >= jj
    gcb = cumsum_rows(gb)
    gl = jnp.sum(gb, axis=1, keepdims=True)
    gc_col = gcb[:, :, :c]
    diff = gc_col - jnp.swapaxes(gc_col, 1, 2)
    decay = jnp.where(low, jnp.exp(jnp.where(low, diff, 0.0)), 0.0)
    qs = q * (q.shape[2] ** -0.5)
    kb = k * bb
    a = jnp.where(ii > jj, pdot(kb, k, "nt", P_BULK) * decay, 0.0)
    t = _neumann_inverse(-a, c.bit_length() - 2)
    eg = jnp.exp(gcb)
    u = pdot(t, v * bb, "nn", P_BULK)
    w = pdot(t, kb * eg, "nn", P_BULK)
    attn = pdot(qs, k, "nt", P_BULK) * decay
    kd = k * jnp.exp(gl - gcb)
    v_new = u - pdot(w, s, "nn", P_BULK)
    o = pdot(qs * eg, s, "nn", P_BULK) + pdot(attn, v_new, "nn", P_BULK)
    s_new = s * jnp.exp(gl) + pdot(kd, v_new, "tn", P_BULK)
    return o, s_new


def wkv_chunk(r, lw, k, v, kk, a, s):
    t = r.shape[1]
    ii, jj = _tri(t)
    lo = lax.broadcasted_iota(jnp.int32, r.shape, 2) < RW_HEAD
    cl = cumsum_rows(lw)
    cl_last = jnp.sum(lw, axis=1, keepdims=True)
    al = -kk * jnp.exp(cl - lw)
    be = (a * kk) * jnp.exp(-cl)
    kt = k * jnp.exp(-cl)
    rt = r * jnp.exp(cl)

    def dot(xa, xb, mode="nn"):
        return pdot(xa, xb, mode, P_BULK)

    def pair(xa, xb, msk):
        m_lo = dot(jnp.where(lo, xa, 0.0), xb, "nt")
        m_hi = dot(jnp.where(lo, 0.0, xa), xb, "nt")
        return jnp.where(msk, m_lo, 0.0), jnp.where(msk, m_hi, 0.0)

    def sel(x_lo, x_hi):
        return jnp.where(lo, x_lo, x_hi)

    ab = pair(al, be, ii > jj)
    ak = pair(al, kt, ii > jj)
    rb = pair(rt, be, ii >= jj)
    rk = pair(rt, kt, ii >= jj)
    x = dot(al, s, "nt") + sel(dot(ak[0], v), dot(ak[1], v))
    steps = t.bit_length() - 2
    u = sel(dot(_neumann_inverse(ab[0], steps), x), dot(_neumann_inverse(ab[1], steps), x))
    y = dot(rt, s, "nt") + sel(dot(rb[0], u) + dot(rk[0], v), dot(rb[1], u) + dot(rk[1], v))
    vi = lax.broadcasted_iota(jnp.int32, s.shape, 1) < RW_HEAD
    ki = lax.broadcasted_iota(jnp.int32, s.shape, 2) < RW_HEAD
    s_new = jnp.where(vi == ki, (s + dot(u, be, "tn") + dot(v, kt, "tn")) * jnp.exp(cl_last), 0.0)
    return y, s_new


def _scan_group(ncol, offs):
    g = SCAN_GROUP
    while g > 1 and (ncol % g or any(o % g for o in offs)):
        g //= 2
    return g


def scan_fwd(name, chunk_fn, ins, *, rows, chunk, ncol):
    n = rows // chunk
    n_in = len(ins)
    grp = _scan_group(ncol, [off for _, off in ins])

    def body(*refs):
        o_ref, st_ref, s_scr = refs[n_in:]

        @pl.when(pl.program_id(1) == 0)
        def _():
            s_scr[...] = jnp.zeros_like(s_scr)

        cols = [slice(b * LANES, (b + 1) * LANES) for b in range(grp)]
        s = s_scr[...]
        st_ref[...] = s
        o, s_new = chunk_fn(*[jnp.stack([r[:, c] for c in cols]) for r in refs[:n_in]], s)
        for b, c in enumerate(cols):
            o_ref[:, c] = o[b]
        s_scr[...] = s_new

    def spec(off):
        return pl.BlockSpec((chunk, grp * LANES), lambda h, c: (c, h + off // grp))

    return pl.pallas_call(
        body, name=name, grid=(ncol // grp, n), in_specs=[spec(off) for _, off in ins],
        out_specs=[spec(0), pl.BlockSpec((grp, None, LANES, LANES), lambda h, c: (h, c, 0, 0))],
        out_shape=[jax.ShapeDtypeStruct((rows, ncol * LANES), f32),
                   jax.ShapeDtypeStruct((ncol, n, LANES, LANES), f32)],
        scratch_shapes=[pltpu.VMEM((grp, LANES, LANES), f32)], compiler_params=_params(2))(*[a for a, _ in ins])


def scan_bwd(name, chunk_fn, ins, states, d_out, d_off, *, rows, chunk, ncol):
    n = rows // chunk
    n_in = len(ins)
    grp = _scan_group(ncol, [off for _, off in ins] + [d_off])

    def body(*refs):
        st_ref, do_ref = refs[n_in:n_in + 2]
        g_refs = refs[n_in + 2:2 * n_in + 2]
        ds_scr = refs[-1]

        @pl.when(pl.program_id(1) == 0)
        def _():
            ds_scr[...] = jnp.zeros_like(ds_scr)

        cols = [slice(b * LANES, (b + 1) * LANES) for b in range(grp)]

        def batch(ref):
            return jnp.stack([ref[:, c] for c in cols])

        _, pull = jax.vjp(chunk_fn, *[batch(r) for r in refs[:n_in]], st_ref[...])
        gs = pull((batch(do_ref), ds_scr[...]))
        for ref, g in zip(g_refs, gs[:n_in]):
            for b, c in enumerate(cols):
                ref[:, c] = g[b]
        ds_scr[...] = gs[n_in]

    def spec(off):
        return pl.BlockSpec((chunk, grp * LANES), lambda h, c: (n - 1 - c, h + off // grp))

    st_spec = pl.BlockSpec((grp, None, LANES, LANES), lambda h, c: (h, n - 1 - c, 0, 0))
    return pl.pallas_call(
        body, name=name, grid=(ncol // grp, n), in_specs=[spec(off) for _, off in ins] + [st_spec, spec(d_off)],
        out_specs=[spec(0)] * n_in, out_shape=[jax.ShapeDtypeStruct((rows, ncol * LANES), f32)] * n_in,
        scratch_shapes=[pltpu.VMEM((grp, LANES, LANES), f32)],
        compiler_params=_params(2))(*[a for a, _ in ins], states, d_out)


def flip_exchange(name, arrs, flips, n_slots, slot_of, src_of, with_self):
    n = len(arrs)
    nf = len(flips)

    def body(*refs):
        ins, outs = refs[:n], refs[n:2 * n]
        send, recv, lsem = refs[2 * n:]
        me = (lax.axis_index("x"), lax.axis_index("y"), lax.axis_index("c"))
        copies = []
        for k in range(n):
            if with_self:
                cp = pltpu.make_async_copy(src_of(ins[k], me), outs[k].at[slot_of(me)], lsem.at[k])
                cp.start()
                copies.append(cp)
            for j, fl in enumerate(flips):
                peer = tuple(1 - m if f else m for m, f in zip(me, fl))
                cp = pltpu.make_async_remote_copy(
                    src_ref=src_of(ins[k], peer), dst_ref=outs[k].at[slot_of(me)], send_sem=send.at[k, j],
                    recv_sem=recv.at[k, j], device_id=peer, device_id_type=MESH)
                cp.start()
                copies.append(cp)
        for cp in copies:
            cp.wait()

    def out_sds(a):
        blk = src_of(jax.ShapeDtypeStruct(a.shape, a.dtype), None)
        return jax.ShapeDtypeStruct((n_slots,) + tuple(blk), a.dtype)

    any_spec = pl.BlockSpec(memory_space=pl.ANY)
    return pl.pallas_call(
        body, name=name, in_specs=[any_spec] * n, out_specs=[any_spec] * n, out_shape=[out_sds(a) for a in arrs],
        scratch_shapes=[pltpu.SemaphoreType.DMA((n, nf)), pltpu.SemaphoreType.DMA((n, nf)),
                        pltpu.SemaphoreType.DMA((n,))],
        compiler_params=pltpu.CompilerParams(has_side_effects=True))(*arrs)


_CHIP_FLIPS = ((1, 0, 0), (0, 1, 0), (1, 1, 0))
_ALL_FLIPS = ((0, 0, 1), (0, 1, 0), (0, 1, 1), (1, 0, 0), (1, 0, 1), (1, 1, 0), (1, 1, 1))


def _whole(ref, pos):
    return ref.shape if pos is None else ref


def _chip_block(ref, pos):
    return ref.shape[1:] if pos is None else ref.at[2 * pos[0] + pos[1]]


def _chip_slot(p):
    return 2 * p[0] + p[1]


def gather_chips(name, arrs):
    return flip_exchange(name, arrs, _CHIP_FLIPS, 4, _chip_slot, _whole, True)


def scatter_chips(name, arrs):
    return flip_exchange(name, arrs, _CHIP_FLIPS, 4, _chip_slot, _chip_block, True)


_HBM = pl.BlockSpec(memory_space=pltpu.HBM)
_SEM = pl.BlockSpec(memory_space=pltpu.SEMAPHORE)
_DATAFLOW = pltpu.SideEffectType.DATAFLOW_SIDE_EFFECTING


def _split_copies(mode, refs, n, send, recv):
    me = (lax.axis_index("x"), lax.axis_index("y"), lax.axis_index("c"))
    sib = (me[0], me[1], 1 - me[2])
    lands = refs[:n] if mode == "handover" else refs[n:2 * n]
    copies = []
    for k, land in enumerate(lands):
        half = land.shape[1] // 2
        mine = pl.ds(pl.multiple_of(me[2] * half, 16), half)
        for j, fl in enumerate(_CHIP_FLIPS):
            peer = tuple(1 - m if f else m for m, f in zip(me, fl))
            if mode == "gather":
                src, dst, to = refs[k], land.at[_chip_slot(me)], peer
            elif mode == "scatter":
                src, dst, to = refs[k].at[_chip_slot(peer)], land.at[_chip_slot(me)], peer
            elif mode == "gather_half":
                src, dst, to = refs[k].at[mine], land.at[_chip_slot(me), mine], peer
            else:
                src = dst = land.at[_chip_slot(peer), mine]
                to = sib
            q = k * len(_CHIP_FLIPS) + j
            copies.append(pltpu.make_async_remote_copy(src_ref=src, dst_ref=dst, send_sem=send.at[q],
                                                       recv_sem=recv.at[q], device_id=to, device_id_type=MESH))
    return copies


def split_start(name, mode, ops, n):
    ops = [pltpu.with_memory_space_constraint(a, pltpu.HBM) for a in ops]
    m = len(ops)

    def body(*refs):
        for cp in _split_copies(mode, refs[:m], n, refs[m], refs[m + 1]):
            cp.start()
        refs[-1][...] = jnp.zeros_like(refs[-1])

    sems = pltpu.SemaphoreType.DMA((n * len(_CHIP_FLIPS),))
    outs = pl.pallas_call(
        body, name=name, in_specs=[_HBM] * m,
        out_shape=(sems, sems, *[pltpu.HBM(a.shape, a.dtype) for a in ops], jax.ShapeDtypeStruct((8, LANES), f32)),
        out_specs=(_SEM, _SEM, *[_HBM] * m, pl.BlockSpec(memory_space=pltpu.VMEM)),
        input_output_aliases={i: 2 + i for i in range(m)},
        compiler_params=pltpu.CompilerParams(has_side_effects=_DATAFLOW))(*ops)
    return (outs[0], outs[1], list(outs[2:2 + m]), mode, n), outs[-1][0, 0]


def split_wait(name, state, after):
    send, recv, ops, mode, n = state
    m = len(ops)

    def body(*refs):
        for cp in _split_copies(mode, refs[:m], n, refs[m], refs[m + 1]):
            cp.wait_send()
            cp.wait_recv()

    outs = pl.pallas_call(
        body, name=name, in_specs=[_HBM] * m + [_SEM, _SEM, pl.BlockSpec(memory_space=pl.ANY)],
        out_shape=tuple(pltpu.HBM(a.shape, a.dtype) for a in ops), out_specs=tuple([_HBM] * m),
        input_output_aliases={i: i for i in range(m)},
        compiler_params=pltpu.CompilerParams(has_side_effects=_DATAFLOW))(*ops, send, recv, after)
    return list(outs[m - n:])


def chips_start(name, arrs, src_of, halves=False):
    me = _chip_slot((lax.axis_index("x"), lax.axis_index("y")))
    lands = []
    for a in arrs:
        blk = tuple(src_of(jax.ShapeDtypeStruct(a.shape, a.dtype), None))
        own = a if src_of is _whole else lax.dynamic_index_in_dim(a, me, 0, keepdims=False)
        lands.append(lax.dynamic_update_index_in_dim(lax.empty((4,) + blk, a.dtype), own, me, 0))
    mode = "scatter" if src_of is _chip_block else ("gather_half" if halves else "gather")
    return split_start(name, mode, list(arrs) + lands, len(arrs))


chips_wait = split_wait


def handover_start(name, lands):
    return split_start(name, "handover", lands, len(lands))


def gather_chips_halves(name, arrs):
    n = len(arrs)
    nf = len(_CHIP_FLIPS)
    split = [a.shape[0] % 32 == 0 for a in arrs]

    def body(*refs):
        ins, outs = refs[:n], refs[n:2 * n]
        send1, recv1, send2, recv2, lsem = refs[2 * n:]
        me = (lax.axis_index("x"), lax.axis_index("y"), lax.axis_index("c"))
        sib = (me[0], me[1], 1 - me[2])
        peers = [tuple(1 - m if f else m for m, f in zip(me, fl)) for fl in _CHIP_FLIPS]
        local, first, second = [], [], []
        for k in range(n):
            cp = pltpu.make_async_copy(ins[k], outs[k].at[_chip_slot(me)], lsem.at[k])
            cp.start()
            local.append(cp)
            half = ins[k].shape[0] // 2
            rows = pl.ds(pl.multiple_of(me[2] * half, 16), half) if split[k] else pl.ds(0, ins[k].shape[0])
            for j, peer in enumerate(peers):
                cp = pltpu.make_async_remote_copy(
                    src_ref=ins[k].at[rows], dst_ref=outs[k].at[_chip_slot(me), rows], send_sem=send1.at[k, j],
                    recv_sem=recv1.at[k, j], device_id=peer, device_id_type=MESH)
                cp.start()
                first.append((k, j, rows, cp))
        for k, j, rows, cp in first:
            cp.wait_recv()
            if split[k]:
                got = outs[k].at[_chip_slot(peers[j]), rows]
                fwd = pltpu.make_async_remote_copy(src_ref=got, dst_ref=got, send_sem=send2.at[k, j],
                                                   recv_sem=recv2.at[k, j], device_id=sib, device_id_type=MESH)
                fwd.start()
                second.append(fwd)
        for _, _, _, cp in first:
            cp.wait_send()
        for cp in second:
            cp.wait()
        for cp in local:
            cp.wait()

    any_spec = pl.BlockSpec(memory_space=pl.ANY)
    sems = pltpu.SemaphoreType.DMA((n, nf))
    return pl.pallas_call(
        body, name=name, in_specs=[any_spec] * n, out_specs=[any_spec] * n,
        out_shape=[jax.ShapeDtypeStruct((4,) + a.shape, a.dtype) for a in arrs],
        scratch_shapes=[sems, sems, sems, sems, pltpu.SemaphoreType.DMA((n,))],
        compiler_params=pltpu.CompilerParams(has_side_effects=True))(*arrs)


def swap_sibling(name, arrs):
    outs = flip_exchange(name, arrs, ((0, 0, 1),), 1, lambda p: 0, _whole, False)
    return [o[0] for o in outs]


def gather_all(arrs):
    return flip_exchange("gather_all", arrs, _ALL_FLIPS, 8, lambda p: 4 * p[0] + 2 * p[1] + p[2], _whole, True)


def _row_tile(nr, nc, n_arrays):
    budget = (20 << 20) // (n_arrays * 2 * 4 * max(nc, LANES))
    t = min(nr, budget) // 16 * 16
    while t > 0 and nr % t:
        t -= 16
    return t if t > 0 else nr


def sum_slots(name, x):
    ns, nr, nc = x.shape
    tile = _row_tile(nr, nc, ns + 1)

    def body(x_ref, o_ref):
        s = x_ref[0].astype(f32)
        for q in range(1, ns):
            s = s + x_ref[q].astype(f32)
        o_ref[...] = s

    return pl.pallas_call(
        body, name=name, grid=(nr // tile,), in_specs=[pl.BlockSpec((ns, tile, nc), lambda i: (0, i, 0))],
        out_specs=pl.BlockSpec((tile, nc), lambda i: (i, 0)), out_shape=jax.ShapeDtypeStruct((nr, nc), f32),
        compiler_params=_params(1))(x)


def adamw(name, w, g_parts, m, v):
    nr, nc = w.shape[-2:]
    n_g = len(g_parts)
    tile = _row_tile(nr, nc, 7 + n_g)

    def body(*refs):
        w_ref, m_ref, v_ref = refs[:3]
        g = refs[3][...]
        for r in refs[4:3 + n_g]:
            g = g + r[...]
        g_ref, d_ref, nm_ref, nv_ref = refs[3 + n_g:]
        nm = ADAM_B1 * m_ref[...] + (1.0 - ADAM_B1) * g
        nv = ADAM_B2 * v_ref[...] + (1.0 - ADAM_B2) * jnp.square(g)
        m_hat = nm / (1.0 - ADAM_B1 ** ADAM_STEP)
        v_hat = nv / (1.0 - ADAM_B2 ** ADAM_STEP)
        g_ref[...] = g
        d_ref[...] = -ADAM_LR * (m_hat / (jnp.sqrt(v_hat) + ADAM_EPS) + ADAM_WD * w_ref[...])
        nm_ref[...] = nm
        nv_ref[...] = nv

    spec = pl.BlockSpec((tile, nc), lambda i: (i, 0))
    spec3 = pl.BlockSpec((None, tile, nc), lambda i: (0, i, 0)) if w.ndim == 3 else spec
    return pl.pallas_call(
        body, name=name, grid=(nr // tile,), in_specs=[spec3] * 3 + [spec] * n_g, out_specs=[spec3] * 4,
        out_shape=[jax.ShapeDtypeStruct(w.shape, f32)] * 4, compiler_params=_params(1))(w, m, v, *g_parts)


def adamw_packed(w, g8, m, v):
    nr, nc = w.shape

    def body(w_ref, g_ref, m_ref, v_ref, go_ref, d_ref, nm_ref, nv_ref):
        g = g_ref[0]
        for q in range(1, 8):
            g = g + g_ref[q]
        nm = ADAM_B1 * m_ref[...] + (1.0 - ADAM_B1) * g
        nv = ADAM_B2 * v_ref[...] + (1.0 - ADAM_B2) * jnp.square(g)
        m_hat = nm / (1.0 - ADAM_B1 ** ADAM_STEP)
        v_hat = nv / (1.0 - ADAM_B2 ** ADAM_STEP)
        go_ref[...] = g
        d_ref[...] = -ADAM_LR * (m_hat / (jnp.sqrt(v_hat) + ADAM_EPS) + ADAM_WD * w_ref[...])
        nm_ref[...] = nm
        nv_ref[...] = nv

    return pl.pallas_call(body, name="adamw_packed", out_shape=[jax.ShapeDtypeStruct((nr, nc), f32)] * 4,
                          compiler_params=pltpu.CompilerParams(vmem_limit_bytes=VMEM_LIMIT))(w, g8, m, v)


def _pack(vectors):
    rows = []
    for a in vectors:
        flat = a.reshape(-1).astype(f32)
        pad = (-flat.shape[0]) % LANES
        rows.append(jnp.pad(flat, (0, pad)).reshape(-1, LANES))
    packed = jnp.concatenate(rows, axis=0)
    return jnp.pad(packed, ((0, (-packed.shape[0]) % 8), (0, 0)))


def _unpack(packed, like):
    out, r = [], 0
    for a in like:
        n = a.size
        nr = -(-n // LANES)
        out.append(packed[r:r + nr].reshape(-1)[:n].reshape(a.shape))
        r += nr
    return out


def kernel(x, mem, mix_norm_w, w_in, dn_conv_w, dn_a_log, dn_dt_bias, dn_norm_w, rw_mu, rw_w0, rw_w2, rw_a0, rw_a2, rw_g2, rw_k_k, rw_k_a, rw_r_k, rw_ln_w, rw_ln_b, w_out, xa_norm_w, mem_norm_w, xa_wq, xa_wk, xa_wv, xa_wo, ffn_norm_w, ffn_w1, ffn_w2, final_norm_w, loss_target, m_mix_norm_w, m_w_in, m_dn_conv_w, m_dn_a_log, m_dn_dt_bias, m_dn_norm_w, m_rw_mu, m_rw_w0, m_rw_w2, m_rw_a0, m_rw_a2, m_rw_g2, m_rw_k_k, m_rw_k_a, m_rw_r_k, m_rw_ln_w, m_rw_ln_b, m_w_out, m_xa_norm_w, m_mem_norm_w, m_xa_wq, m_xa_wk, m_xa_wv, m_xa_wo, m_ffn_norm_w, m_ffn_w1, m_ffn_w2, m_final_norm_w, v_mix_norm_w, v_w_in, v_dn_conv_w, v_dn_a_log, v_dn_dt_bias, v_dn_norm_w, v_rw_mu, v_rw_w0, v_rw_w2, v_rw_a0, v_rw_a2, v_rw_g2, v_rw_k_k, v_rw_k_a, v_rw_r_k, v_rw_ln_w, v_rw_ln_b, v_w_out, v_xa_norm_w, v_mem_norm_w, v_xa_wq, v_xa_wk, v_xa_wv, v_xa_wo, v_ffn_norm_w, v_ffn_w1, v_ffn_w2, v_final_norm_w):
    weights = dict(mix_norm_w=mix_norm_w, w_in=w_in, dn_conv_w=dn_conv_w, dn_a_log=dn_a_log, dn_dt_bias=dn_dt_bias, dn_norm_w=dn_norm_w, rw_mu=rw_mu, rw_w0=rw_w0, rw_w2=rw_w2, rw_a0=rw_a0, rw_a2=rw_a2, rw_g2=rw_g2, rw_k_k=rw_k_k, rw_k_a=rw_k_a, rw_r_k=rw_r_k, rw_ln_w=rw_ln_w, rw_ln_b=rw_ln_b, w_out=w_out, xa_norm_w=xa_norm_w, mem_norm_w=mem_norm_w, xa_wq=xa_wq, xa_wk=xa_wk, xa_wv=xa_wv, xa_wo=xa_wo, ffn_norm_w=ffn_norm_w, ffn_w1=ffn_w1, ffn_w2=ffn_w2, final_norm_w=final_norm_w)
    mom_m = dict(mix_norm_w=m_mix_norm_w, w_in=m_w_in, dn_conv_w=m_dn_conv_w, dn_a_log=m_dn_a_log, dn_dt_bias=m_dn_dt_bias, dn_norm_w=m_dn_norm_w, rw_mu=m_rw_mu, rw_w0=m_rw_w0, rw_w2=m_rw_w2, rw_a0=m_rw_a0, rw_a2=m_rw_a2, rw_g2=m_rw_g2, rw_k_k=m_rw_k_k, rw_k_a=m_rw_k_a, rw_r_k=m_rw_r_k, rw_ln_w=m_rw_ln_w, rw_ln_b=m_rw_ln_b, w_out=m_w_out, xa_norm_w=m_xa_norm_w, mem_norm_w=m_mem_norm_w, xa_wq=m_xa_wq, xa_wk=m_xa_wk, xa_wv=m_xa_wv, xa_wo=m_xa_wo, ffn_norm_w=m_ffn_norm_w, ffn_w1=m_ffn_w1, ffn_w2=m_ffn_w2, final_norm_w=m_final_norm_w)
    mom_v = dict(mix_norm_w=v_mix_norm_w, w_in=v_w_in, dn_conv_w=v_dn_conv_w, dn_a_log=v_dn_a_log, dn_dt_bias=v_dn_dt_bias, dn_norm_w=v_dn_norm_w, rw_mu=v_rw_mu, rw_w0=v_rw_w0, rw_w2=v_rw_w2, rw_a0=v_rw_a0, rw_a2=v_rw_a2, rw_g2=v_rw_g2, rw_k_k=v_rw_k_k, rw_k_a=v_rw_k_a, rw_r_k=v_rw_r_k, rw_ln_w=v_rw_ln_w, rw_ln_b=v_rw_ln_b, w_out=v_w_out, xa_norm_w=v_xa_norm_w, mem_norm_w=v_mem_norm_w, xa_wq=v_xa_wq, xa_wk=v_xa_wk, xa_wv=v_xa_wv, xa_wo=v_xa_wo, ffn_norm_w=v_ffn_norm_w, ffn_w1=v_ffn_w1, ffn_w2=v_ffn_w2, final_norm_w=v_final_norm_w)
    names = list(weights)

    seq, d = x.shape[1], x.shape[2]
    dnw = d // 2
    rww = d - dnw
    nh, nb = dnw // LANES, rww // LANES
    n_mem = mem.shape[1]
    lw_dim, la_dim, lg_dim = rw_w2.shape[1], rw_a2.shape[1], rw_g2.shape[1]
    assert lw_dim + la_dim == LANES and lg_dim == LANES and dnw % LANES == 0 and rww % LANES == 0
    xs, mems, tgt = x[0], mem[0], loss_target[0]

    col_sharded = ("w_in", "xa_wo", "ffn_w1", "dn_conv_w", "rw_w2", "rw_a2", "rw_g2")
    row_sharded = ("w_out", "xa_wq", "xa_wk", "xa_wv", "ffn_w2")
    f32_payload = ("dn_conv_w", "rw_w2", "rw_a2", "rw_g2")
    sharded = col_sharded + row_sharded
    payload = {n: weights[n][0].astype(f32 if n in f32_payload else bf16) for n in sharded}
    shard_w = w_in.shape[2]
    pad_w = -(-shard_w // LANES) * LANES
    payload["w_in"] = jnp.pad(payload["w_in"], ((0, 0), (0, pad_w - shard_w)))
    first = ("w_in", "dn_conv_w", "rw_w2", "rw_a2", "rw_g2")
    mid = ("w_out", "xa_wq", "xa_wk", "xa_wv", "xa_wo")
    late = ("ffn_w1", "ffn_w2")
    gathered = dict(zip(first, gather_chips_halves("gather_first", [payload[n] for n in first])))
    ordered = lax.optimization_barrier(([gathered[n] for n in first], [payload[n] for n in mid + late]))
    gathered = dict(zip(first, ordered[0]))
    payload.update(zip(mid + late, ordered[1]))
    mid_state, tok_mid = chips_start("gather_mid_start", [payload[n] for n in mid], _whole, halves=True)
    late_state, tok_late = chips_start("gather_late_start", [payload[n] for n in late], _whole, halves=True)
    mix_norm_w_in = mix_norm_w + (tok_mid + tok_late)

    def full(n):
        g = gathered[n]
        if n in col_sharded:
            return g.transpose(1, 0, 2).reshape(g.shape[1], 4 * g.shape[2])
        return g.reshape(4 * g.shape[1], g.shape[2])

    c_rw0 = 4 * dnw + 2 * nh
    c_rw = 3 * rww
    eb_ab, eb_r = 4 * nh, 4 * nh + 1
    eb_l1 = eb_r + 3 * nb
    n_ext = -(-(eb_l1 + 2) // 4) * 4
    tbl_fwd, tbl_bwd, per_shard = w_in_layout(4 * shard_w, shard_w, c_rw0, LANES - 2 * nh, n_ext)
    w_ext = lane_regroup(
        "w_in_regroup", gathered["w_in"], tbl_fwd,
        lambda p: pl.BlockSpec((None, d, LANES),
                               lambda i, j, t: (t[REGROUP_FIELDS * p, j], 0, t[REGROUP_FIELDS * p + 1, j])),
        pl.BlockSpec((d, LANES), lambda i, j, t: (0, j)), (d, n_ext * LANES), (1, n_ext), pad_w)
    conv_w = full("dn_conv_w")
    w2p = jnp.concatenate([full("rw_w2"), jnp.zeros((la_dim, rww), f32)], axis=0)
    a2p = jnp.concatenate([jnp.zeros((lw_dim, rww), f32), full("rw_a2")], axis=0)
    g2 = full("rw_g2")
    xaw = xa_wq.shape[2]
    nxh = xaw // LANES
    ffn = 4 * ffn_w1.shape[2]

    def lane_row(vec):
        return jnp.pad(vec.reshape(1, -1), ((0, 0), (0, LANES - vec.size)))

    alog_row, dtb_row = lane_row(dn_a_log), lane_row(dn_dt_bias)
    head_of_col = jnp.arange(dnw)[None, :] // LANES
    e_g = (jnp.arange(LANES)[:, None] == head_of_col).astype(f32)
    e_b = (jnp.arange(LANES)[:, None] == head_of_col + nh).astype(f32)
    mu_main, mu_small = rw_mu[:, :c_rw], rw_mu[:, c_rw:]
    r_k_row = rw_r_k.reshape(1, rww)
    fnw = final_norm_w.reshape(1, d)
    qb, kb_, vb, zb = 0, nh, 2 * nh, 3 * nh
    rb0 = eb_r
    cc = lambda w_, o_: ("constc", w_, o_)
    rc = lambda o_: ("rowc", LANES, o_)

    (u16,) = rowcall("norm_mix", fn_rms, [(xs, "row"), (mix_norm_w_in, "const")], [(seq, d, bf16, "row")], rows=seq)
    p_main = p_small = mm("in_proj", u16, w_ext)

    fn_gconv = make_fn_gconv(2 * nh)
    (qkv,) = rowcall("gdn_conv", fn_gconv, [(p_main, rc(0)), (conv_w, cc(LANES, 0))],
                     [(seq, 3 * dnw, f32, "rowc")], rows=seq, tile=seq, ncol=3 * nh)
    gate_ins = [(p_small, rc(eb_ab)), (alog_row, "const"), (dtb_row, "const"), (e_g, "const"), (e_b, "const")]
    g_b, beta_b = rowcall("gdn_gate", fn_ggate, gate_ins, [(seq, dnw, f32, "row")] * 2, rows=seq)
    gdn_ins = [(qkv, qb), (qkv, kb_), (qkv, vb), (g_b, 0), (beta_b, 0)]
    o_raw, gdn_states = scan_fwd("gdn_scan", gdn_chunk, gdn_ins, rows=seq, chunk=GDN_CHUNK, ncol=nh)
    mid_state, tok = handover_start("gather_mid_pass", chips_wait("gather_mid_wait", mid_state, o_raw))
    gpost_ins = [(o_raw, rc(0)), (p_main, rc(zb)), (dn_norm_w + tok, "const")]
    (o_dn,) = rowcall("gdn_post", fn_gpost, gpost_ins, [(seq, dnw, bf16, "rowc")], rows=seq, ncol=nh)

    (prw,) = rowcall("rw_lerp_main", fn_lerp, [(p_main, rc(rb0)), (mu_main, cc(LANES, 0))],
                     [(seq, c_rw, f32, "rowc")], rows=seq, tile=seq, ncol=3 * nb)
    (psl,) = rowcall("rw_lerp_small", fn_lerp, [(p_small, rc(eb_l1)), (mu_small, cc(LANES, 0))],
                     [(seq, 2 * LANES, f32, "rowc")], rows=seq, tile=seq, ncol=2)
    rprep_ins = [(prw, rc(nb)), (psl, "row"), (rw_w0, cc(LANES, 0)), (rw_a0, cc(LANES, 0)), (rw_k_k, cc(LANES, 0)),
                 (rw_k_a, cc(LANES, 0)), (w2p, cc(LANES, 0)), (a2p, cc(LANES, 0)), (g2, cc(LANES, 0))]
    lw, kmod, kk, a_rw, gate = rowcall("rw_prep", fn_rprep, rprep_ins, [(seq, rww, f32, "rowc")] * 5,
                                        rows=seq, ncol=nb)
    wkv_ins = [(prw, 0), (lw, 0), (kmod, 0), (prw, 2 * nb), (kk, 0), (a_rw, 0)]
    y_rw, wkv_states = scan_fwd("wkv_scan", wkv_chunk, wkv_ins, rows=seq, chunk=WKV_CHUNK, ncol=nb)
    rpost_ins = [(y_rw, rc(0)), (prw, rc(0)), (kmod, rc(0)), (prw, rc(2 * nb)), (gate, rc(0)),
                 (r_k_row, cc(LANES, 0)), (rw_ln_w, cc(LANES, 0)), (rw_ln_b, cc(LANES, 0))]
    (o_rw,) = rowcall("rw_post", fn_rpost, rpost_ins, [(seq, rww, bf16, "rowc")], rows=seq, ncol=nb)

    o_cat = jnp.concatenate([o_dn, o_rw], axis=1)
    late_state, tok = handover_start("gather_late_pass", chips_wait("gather_late_wait", late_state, o_cat))
    gathered.update(zip(mid, chips_wait("gather_mid_pass_wait", mid_state, o_cat)))
    w_out_f, wq_f, wk_f, wv_f, wo_f = full("w_out"), full("xa_wq"), full("xa_wk"), full("xa_wv"), full("xa_wo")
    h1 = mm("out_proj", o_cat, w_out_f, add=xs)

    (hn16,) = rowcall("norm_xa", fn_rms, [(h1, "row"), (xa_norm_w + tok, "const")], [(seq, d, bf16, "row")],
                      rows=seq)
    (mn16,) = rowcall("norm_mem", fn_rms, [(mems, "row"), (mem_norm_w, "const")], [(n_mem, d, bf16, "row")],
                      rows=n_mem)
    q_xa = mm("xa_q", hn16, wq_f)
    k_xa = mm("xa_k", mn16, wk_f)
    v_xa = mm("xa_v", mn16, wv_f)
    xcore_ins = [(q_xa, rc(0)), (k_xa, cc(LANES, 0)), (v_xa, cc(LANES, 0))]
    (o_xa,) = rowcall("xa_core", fn_xcore, xcore_ins, [(seq, xaw, bf16, "rowc")], rows=seq, ncol=nxh)
    h2 = mm("xa_o", o_xa, wo_f, add=h1)

    (fn16,) = rowcall("norm_ffn", fn_rms, [(h2, "row"), (ffn_norm_w, "const")], [(seq, d, bf16, "row")], rows=seq)
    gathered.update(zip(late, chips_wait("gather_late_pass_wait", late_state, fn16)))
    w1_f, w2_f = full("ffn_w1"), full("ffn_w2")
    a1_16, hid16 = mm("ffn_up", fn16, w1_f, epilogue=lambda r: (r, jnp.square(jnp.maximum(r, 0.0))),
                      out_dtypes=(bf16, bf16))
    h3 = mm("ffn_down", hid16, w2_f, add=h2)

    dh3, dh3_16, d_fnw, loss_rows = rowcall(
        "loss_head", fn_final, [(h3, "row"), (tgt, "row"), (fnw, "const")],
        [(seq, d, f32, "row"), (seq, d, bf16, "row"), (1, d, f32, "acc"), (8, LANES, f32, "acc")], rows=seq)

    da1_16 = mm("ffn_down_dx", dh3_16, w2_f, tb=True, extra=[a1_16], out_dtype=bf16,
                epilogue=lambda r, a1: (r * (2.0 * jnp.maximum(a1.astype(f32), 0.0)),))
    g_ffn_w2 = mm("ffn_down_dw", hid16, dh3_16, ta=True, out_dtype=bf16)
    g_ffn_w1 = mm("ffn_up_dw", fn16, da1_16, ta=True, out_dtype=bf16, by_chip=True)
    dfn = mm("ffn_up_dx", da1_16, w1_f, tb=True)

    def by_chip(n, g):
        if g.ndim == 3:
            return g
        if n in col_sharded:
            return g.reshape(g.shape[0], 4, g.shape[1] // 4).transpose(1, 0, 2)
        return g.reshape(4, g.shape[0] // 4, g.shape[1])

    late_g, tok = chips_start("scatter_late_start", [by_chip("ffn_w1", g_ffn_w1), by_chip("ffn_w2", g_ffn_w2)],
                              _chip_block)
    dh2, d_ffn_nw, dh2_16 = rowvjp("norm_ffn_bwd", fn_rms, [(h2, "row"), (ffn_norm_w + tok, "const")],
                                   [[(dfn, "row")]], [0, 1], rows=seq, adds=[(0, dh3, "row")], dup16=[0])

    do_xa = mm("xa_o_dx", dh2_16, wo_f, tb=True)
    g_xa_wo = mm("xa_o_dw", o_xa, dh2_16, ta=True, out_dtype=bf16, by_chip=True)
    dq_xa, dk_xa, dv_xa, dq16 = rowvjp("xa_core_bwd", fn_xcore, xcore_ins, [[(do_xa, rc(0))]], [0, 1, 2],
                                       rows=seq, ncol=nxh, dup16=[0])
    g_xa_wq = mm("xa_q_dw", hn16, dq16, ta=True, out_dtype=bf16)
    dhn = mm("xa_q_dx", dq16, wq_f, tb=True)
    dh1, d_xa_nw, dh1_16 = rowvjp("norm_xa_bwd", fn_rms, [(h1, "row"), (xa_norm_w, "const")], [[(dhn, "row")]],
                                  [0, 1], rows=seq, adds=[(0, dh2, "row")], dup16=[0])
    dk16, dv16 = dk_xa.astype(bf16), dv_xa.astype(bf16)
    g_xa_wk = mm("xa_k_dw", mn16, dk16, ta=True, out_dtype=bf16)
    g_xa_wv = mm("xa_v_dw", mn16, dv16, ta=True, out_dtype=bf16)
    dmn = mm("xa_v_dx", dv16, wv_f, tb=True, add=mm("xa_k_dx", dk16, wk_f, tb=True))
    (d_mem_nw,) = rowvjp("norm_mem_bwd", fn_rms, [(mems, "row"), (mem_norm_w, "const")], [[(dmn, "row")]], [1],
                         rows=n_mem)

    g_w_out = mm("out_proj_dw", o_cat, dh1_16, ta=True, out_dtype=bf16)
    do_cat = mm("out_proj_dx", dh1_16, w_out_f, tb=True)
    mid_grads = dict(w_out=g_w_out, xa_wq=g_xa_wq, xa_wk=g_xa_wk, xa_wv=g_xa_wv, xa_wo=g_xa_wo)
    mid_g, tok = chips_start("scatter_mid_start", [by_chip(n, mid_grads[n]) for n in mid], _chip_block)
    rpost_ins_b = rpost_ins[:5] + [(r_k_row + tok, cc(LANES, 0))] + rpost_ins[6:]

    dy, dr_a, dkmod_a, dv_a, dgate, d_r_k, d_ln_w, d_ln_b = rowvjp(
        "rw_post_bwd", fn_rpost, rpost_ins_b, [[(do_cat, rc(nh))]], [0, 1, 2, 3, 4, 5, 6, 7], rows=seq, ncol=nb)
    dr_b, dlw, dkmod_b, dv_b, dkk, da_rw = scan_bwd("wkv_scan_bwd", wkv_chunk, wkv_ins, wkv_states, dy, 0,
                                                    rows=seq, chunk=WKV_CHUNK, ncol=nb)
    rprep_cts = [[(dlw, rc(0))], [(dkmod_a, rc(0)), (dkmod_b, rc(0))], [(dkk, rc(0))], [(da_rw, rc(0))],
                 [(dgate, rc(0))]]
    dpk, dpsl_parts, d_w0, d_a0, d_k_k, d_k_a, d_w2p, d_a2p, d_g2 = rowvjp(
        "rw_prep_bwd", fn_rprep, rprep_ins, rprep_cts, [0, 1, 2, 3, 4, 5, 6, 7, 8], rows=seq, ncol=nb)
    (dpsl,) = rowcall("rw_prep_sum", fn_sumcols(nb), [(dpsl_parts, "row")], [(seq, 2 * LANES, f32, "row")], rows=seq)

    def lerp_bwd(tag, p, p_off, mu, mu_off, ct_lists, ncol):
        return rowvjp("rw_lerp_bwd_" + tag, fn_lerp, [(p, rc(p_off)), (mu, cc(LANES, mu_off))], [ct_lists], [0, 1],
                      rows=seq, tile=seq, ncol=ncol, dup16=[0])

    _, dmu_r, dpr16 = lerp_bwd("r", p_main, rb0, mu_main, 0, [(dr_a, rc(0)), (dr_b, rc(0))], nb)
    _, dmu_k, dpk16 = lerp_bwd("k", p_main, rb0 + nb, mu_main, nb, [(dpk, rc(0))], nb)
    _, dmu_v, dpv16 = lerp_bwd("v", p_main, rb0 + 2 * nb, mu_main, 2 * nb, [(dv_a, rc(0)), (dv_b, rc(0))], nb)
    _, dmu_s, dps12_16 = lerp_bwd("small", p_small, eb_l1, mu_small, 0, [(dpsl, rc(0))], 2)

    do_raw, dz, d_dn_nw, dz16 = rowvjp("gdn_post_bwd", fn_gpost, gpost_ins, [[(do_cat, rc(0))]], [0, 1, 2],
                                       rows=seq, ncol=nh, dup16=[1])
    dq_g, dk_g, dv_g, dg_b, dbeta_b = scan_bwd("gdn_scan_bwd", gdn_chunk, gdn_ins, gdn_states, do_raw, 0,
                                               rows=seq, chunk=GDN_CHUNK, ncol=nh)
    dps0, d_alog, d_dtb, dps0_16 = rowvjp("gdn_gate_bwd", fn_ggate, gate_ins, [[(dg_b, "row")], [(dbeta_b, "row")]],
                                          [0, 1, 2], rows=seq, dup16=[0])
    dqkv = jnp.concatenate([dq_g, dk_g, dv_g], axis=1)
    _, d_conv_w, dqkv16 = rowvjp("gdn_conv_bwd", fn_gconv, [(p_main, rc(0)), (conv_w, cc(LANES, 0))],
                                 [[(dqkv, rc(0))]], [0, 1], rows=seq, tile=seq, ncol=3 * nh, dup16=[0])

    dp16 = jnp.concatenate([dqkv16, dz16, dps0_16, dpr16, dpk16, dpv16, dps12_16,
                            jnp.zeros((seq, (n_ext - eb_l1 - 2) * LANES), bf16)], axis=1)
    g_w_ext = mm("in_proj_dw", u16, dp16, ta=True, out_dtype=bf16)
    g_w_in = lane_regroup(
        "w_in_grad_regroup", g_w_ext, tbl_bwd,
        lambda p: pl.BlockSpec((d, LANES), lambda i, j, t: (0, t[REGROUP_FIELDS * p + 1, j])),
        pl.BlockSpec((None, d, LANES), lambda i, j, t: (j // per_shard, 0, j % per_shard)),
        (4, d, pad_w), (1, 4 * per_shard), n_ext * LANES)
    first_grads = dict(w_in=g_w_in, dn_conv_w=d_conv_w, rw_w2=d_w2p[:lw_dim], rw_a2=d_a2p[lw_dim:], rw_g2=d_g2)
    first_g, tok = chips_start("scatter_first_start", [by_chip(n, first_grads[n]) for n in first], _chip_block)

    du = mm("in_proj_dx", dp16, w_ext, tb=True)
    grad_x, d_mix_nw = rowvjp("norm_mix_bwd", fn_rms, [(xs, "row"), (mix_norm_w + tok, "const")], [[(du, "row")]],
                              [0, 1], rows=seq, adds=[(0, dh1, "row")])
    received = dict(zip(mid, chips_wait("scatter_mid_wait", mid_g, grad_x)))
    received.update(zip(late, chips_wait("scatter_late_wait", late_g, grad_x)))

    out = {}

    def reduce_and_update(tag, group):
        partial = [sum_slots("sum_chips_" + n, received[n]) for n in group]
        other = swap_sibling("swap_sibling_" + tag, partial)
        for n, p_mine, p_other in zip(group, partial, other):
            if weights[n].shape[2] % LANES:
                rows, cols = weights[n].shape[1:]
                lin = lambda a: jnp.swapaxes(a, 1, 2).reshape(-1, LANES)
                lin_g = lambda p: p.T[:cols].reshape(-1, LANES)
                res = adamw("adamw_" + n, lin(weights[n]), [lin_g(p_mine), lin_g(p_other)], lin(mom_m[n]),
                            lin(mom_v[n]))
                out[n] = [jnp.swapaxes(r.reshape(1, cols, rows), 1, 2) for r in res]
            else:
                out[n] = adamw("adamw_" + n, weights[n], [p_mine, p_other], mom_m[n], mom_v[n])

    reduce_and_update("rest", mid + late)
    received.update(zip(first, chips_wait("scatter_first_wait", first_g, out["ffn_w2"][1])))
    reduce_and_update("first", first)

    small_names = [n for n in names if n not in sharded]
    small_local = dict(
        mix_norm_w=d_mix_nw, dn_a_log=d_alog[:, :nh], dn_dt_bias=d_dtb[:, :nh], dn_norm_w=d_dn_nw,
        rw_mu=jnp.concatenate([dmu_r, dmu_k, dmu_v, dmu_s], axis=1), rw_w0=d_w0, rw_a0=d_a0, rw_k_k=d_k_k,
        rw_k_a=d_k_a, rw_r_k=d_r_k, rw_ln_w=d_ln_w, rw_ln_b=d_ln_b, xa_norm_w=d_xa_nw, mem_norm_w=d_mem_nw,
        ffn_norm_w=d_ffn_nw, final_norm_w=d_fnw)
    loss_vec = jnp.where(jnp.arange(LANES) == 0, loss_rows[0], 0.0)
    packed, _ = lax.optimization_barrier((_pack([small_local[n] for n in small_names] + [loss_vec]), out["w_in"][0]))
    (g8,) = gather_all([packed])

    packed_like = [weights[n] for n in small_names] + [loss_vec]
    zero = jnp.zeros((LANES,), f32)
    res = adamw_packed(_pack([weights[n] for n in small_names] + [zero]), g8,
                       _pack([mom_m[n] for n in small_names] + [zero]),
                       _pack([mom_v[n] for n in small_names] + [zero]))
    unpacked = [_unpack(r, packed_like) for r in res]
    for i, n in enumerate(small_names):
        out[n] = [u[i] for u in unpacked]
    loss = unpacked[0][-1][0]

    return (loss, grad_x.reshape(x.shape), *[out[n][0] for n in names], *[out[n][1] for n in names],
            *[out[n][2] for n in names], *[out[n][3] for n in names])
```

```python
import functools

import jax
import jax.numpy as jnp
from jax import lax
from jax.experimental import pallas as pl
from jax.experimental.pallas import tpu as pltpu

f32 = jnp.float32
bf16 = jnp.bfloat16
HI = lax.Precision.HIGHEST
MESH = pl.DeviceIdType.MESH

LANES = 128
VMEM_LIMIT = 56 << 20
TOK_TILE = 256
TOK_TILE_BLOCKED = 1024
MM_TILE = 1024
MM_TILE_K = 2048
GDN_CHUNK = 128
WKV_CHUNK = 64
SCAN_GROUP = 8
P_BULK = 1
P_INV = 1
P_RESID = 3
P_CUMSUM = 3
RMS_EPS = 1e-6
RW_GN_EPS = 64e-5
RW_HEAD = 64

ADAM_LR, ADAM_B1, ADAM_B2, ADAM_EPS, ADAM_WD, ADAM_STEP = 0.001, 0.9, 0.999, 1e-08, 0.01, 10


def _params(n_grid):
    return pltpu.CompilerParams(dimension_semantics=("arbitrary",) * n_grid, vmem_limit_bytes=VMEM_LIMIT)


_DIMS = {"nn": (((1,), (0,)), ((), ())), "nt": (((1,), (1,)), ((), ())), "tn": (((0,), (0,)), ((), ()))}
_DIMS_BATCHED = {"nn": (((2,), (1,)), ((0,), (0,))), "nt": (((2,), (2,)), ((0,), (0,))),
                 "tn": (((1,), (1,)), ((0,), (0,)))}


def _raw_dot(a, b, mode, passes):
    dims = (_DIMS if a.ndim == 2 else _DIMS_BATCHED)[mode]
    if passes == 6:
        return lax.dot_general(a.astype(f32), b.astype(f32), dims, precision=HI, preferred_element_type=f32)
    ah, bh = a.astype(bf16), b.astype(bf16)
    r = lax.dot_general(ah, bh, dims, preferred_element_type=f32)
    if passes == 3:
        al = (a - ah.astype(f32)).astype(bf16)
        bl = (b - bh.astype(f32)).astype(bf16)
        r = r + lax.dot_general(al, bh, dims, preferred_element_type=f32)
        r = r + lax.dot_general(ah, bl, dims, preferred_element_type=f32)
    return r


@functools.partial(jax.custom_vjp, nondiff_argnums=(2, 3))
def pdot(a, b, mode, passes):
    return _raw_dot(a, b, mode, passes)


def _pdot_bwd(mode, passes, res, g):
    a, b = res
    if mode == "nn":
        da, db = _raw_dot(g, b, "nt", passes), _raw_dot(a, g, "tn", passes)
    elif mode == "nt":
        da, db = _raw_dot(g, b, "nn", passes), _raw_dot(g, a, "tn", passes)
    else:
        da, db = _raw_dot(b, g, "nt", passes), _raw_dot(a, g, "nn", passes)
    return da.astype(a.dtype), db.astype(b.dtype)


pdot.defvjp(lambda a, b, mode, passes: (_raw_dot(a, b, mode, passes), (a, b)), _pdot_bwd)


def hdot(a, b):
    return pdot(a, b, "nn", 6)


def bdot(a, b):
    return pdot(a, b, "nn", 1)


def bdot_nt(a, b):
    return pdot(a, b, "nt", 1)


def _shift_rows(x, k):
    row = lax.broadcasted_iota(jnp.int32, x.shape, 0)
    return jnp.where(row < k, 0.0, pltpu.roll(x, k, axis=0))


def _unshift_rows(g, k):
    n = g.shape[0]
    row = lax.broadcasted_iota(jnp.int32, g.shape, 0)
    return jnp.where(row >= n - k, 0.0, pltpu.roll(g, n - k, axis=0))


@functools.partial(jax.custom_vjp, nondiff_argnums=(1,))
def tshift(x, k):
    return _shift_rows(x, k)


tshift.defvjp(lambda x, k: (_shift_rows(x, k), None), lambda k, _, g: (_unshift_rows(g, k),))


def rms(x, w):
    x = x.astype(f32)
    return x * lax.rsqrt(jnp.mean(x * x, axis=-1, keepdims=True) + RMS_EPS) * w


def softplus(x):
    return jnp.maximum(x, 0.0) + jnp.log(1.0 + jnp.exp(-jnp.abs(x)))


def seg2sum(x):
    lo = lax.broadcasted_iota(jnp.int32, x.shape, 1) < RW_HEAD
    s_lo = jnp.sum(jnp.where(lo, x, 0.0), axis=-1, keepdims=True)
    s_hi = jnp.sum(jnp.where(lo, 0.0, x), axis=-1, keepdims=True)
    return jnp.where(lo, s_lo, s_hi)


def _tile(n, pref):
    if n <= pref:
        return n
    t = pref
    while t >= LANES:
        if n % t == 0:
            return t
        t -= LANES
    return n


def mm(name, a, b, *, ta=False, tb=False, add=None, out_dtype=f32, by_chip=False, epilogue=None, extra=(),
       out_dtypes=None):
    (k, m) = a.shape if ta else a.shape[::-1]
    (n, kb) = b.shape if tb else b.shape[::-1]
    assert k == kb, (name, a.shape, b.shape)
    tm, tk = _tile(m, MM_TILE), _tile(k, MM_TILE_K)
    tn = _tile(n // 4, MM_TILE) if by_chip else _tile(n, MM_TILE)
    nk = k // tk
    dims = (((0,) if ta else (1,), (1,) if tb else (0,)), ((), ()))
    extra = list(extra) + ([] if add is None else [add])
    out_dtypes = [out_dtype] if out_dtypes is None else list(out_dtypes)
    n_extra, n_out = len(extra), len(out_dtypes)

    def body(*refs):
        a_ref, b_ref = refs[:2]
        x_refs = refs[2:2 + n_extra]
        o_refs = refs[2 + n_extra:2 + n_extra + n_out]
        part = lax.dot_general(a_ref[...].astype(bf16), b_ref[...].astype(bf16), dims, preferred_element_type=f32)

        def finish(r):
            xs = [x[...] for x in x_refs]
            if add is not None:
                r = r + xs.pop().astype(f32)
            outs = (r,) if epilogue is None else epilogue(r, *xs)
            for o_ref, o in zip(o_refs, outs):
                o_ref[...] = o.astype(o_ref.dtype)

        if nk == 1:
            finish(part)
            return
        acc = refs[-1]
        kk = pl.program_id(2)

        @pl.when(kk == 0)
        def _():
            acc[...] = part

        @pl.when(kk > 0)
        def _():
            acc[...] += part

        @pl.when(kk == nk - 1)
        def _():
            finish(acc[...])

    a_spec = pl.BlockSpec((tk, tm), lambda i, j, q: (q, i)) if ta else pl.BlockSpec((tm, tk), lambda i, j, q: (i, q))
    b_spec = pl.BlockSpec((tn, tk), lambda i, j, q: (j, q)) if tb else pl.BlockSpec((tk, tn), lambda i, j, q: (q, j))
    x_spec = pl.BlockSpec((tm, tn), lambda i, j, q: (i, j))
    if by_chip:
        per_chip = n // 4 // tn
        o_spec = pl.BlockSpec((None, tm, tn), lambda i, j, q: (j // per_chip, i, j % per_chip))
        o_shape = (4, m, n // 4)
    else:
        o_spec, o_shape = x_spec, (m, n)
    res = pl.pallas_call(
        body, name=name, grid=(m // tm, n // tn, nk), in_specs=[a_spec, b_spec] + [x_spec] * n_extra,
        out_specs=[o_spec] * n_out, out_shape=[jax.ShapeDtypeStruct(o_shape, dt) for dt in out_dtypes],
        scratch_shapes=[pltpu.VMEM((tm, tn), f32)] if nk > 1 else [], compiler_params=_params(3))(a, b, *extra)
    return res[0] if n_out == 1 else res


REGROUP_PIECES = 4


REGROUP_FIELDS = 5


def _regroup_table(n_out, sources_of):
    import numpy as np
    tbl = np.zeros((REGROUP_FIELDS * REGROUP_PIECES, n_out), np.int32)
    for j in range(n_out):
        groups = sorted(sources_of(j).items())
        assert len(groups) <= REGROUP_PIECES, (j, len(groups))
        for p in range(REGROUP_PIECES):
            if p < len(groups):
                key, lanes = groups[p]
                shifts = {q - s for s, q in lanes}
                qs = sorted(q for _, q in lanes)
                assert len(shifts) == 1 and qs == list(range(qs[0], qs[-1] + 1)), (j, key)
                row = (key[0], key[1], shifts.pop(), qs[0], qs[-1] + 1)
            else:
                row = (0, 0, 0, 0, 0)
            tbl[REGROUP_FIELDS * p:REGROUP_FIELDS * (p + 1), j] = row
    return jnp.asarray(tbl)


def lane_regroup(name, src, table, src_spec, out_spec, out_shape, grid, src_width):
    rows = src.shape[-2]

    def body(tbl, *refs):
        o_ref, acc = refs[REGROUP_PIECES:]
        j = pl.program_id(1)
        acc[...] = jnp.zeros_like(acc)
        for p in range(REGROUP_PIECES):
            blk, shift, lo, hi = (tbl[REGROUP_FIELDS * p + f, j] for f in range(1, REGROUP_FIELDS))

            @pl.when(hi > lo)
            def _(p=p, blk=blk, shift=shift, lo=lo, hi=hi):
                lane = lax.broadcasted_iota(jnp.int32, (rows, LANES), 1)
                pi = lax.broadcasted_iota(jnp.int32, (LANES, LANES), 0)
                qi = lax.broadcasted_iota(jnp.int32, (LANES, LANES), 1)
                x = jnp.where(lane < src_width - blk * LANES, refs[p][...], jnp.zeros((), src.dtype))
                sel = jnp.logical_and(qi - pi == shift, jnp.logical_and(qi >= lo, qi < hi))
                acc[...] += jnp.dot(x, sel.astype(src.dtype), preferred_element_type=f32)

        o_ref[...] = acc[...].astype(o_ref.dtype)

    return pl.pallas_call(
        body, name=name, out_shape=jax.ShapeDtypeStruct(out_shape, src.dtype),
        grid_spec=pltpu.PrefetchScalarGridSpec(
            num_scalar_prefetch=1, grid=grid, in_specs=[src_spec(p) for p in range(REGROUP_PIECES)],
            out_specs=out_spec, scratch_shapes=[pltpu.VMEM((rows, LANES), f32)]),
        compiler_params=_params(2))(table, *[src] * REGROUP_PIECES)


def w_in_layout(d_cols, shard_w, c_split, gap, n_blocks):
    def ext_of(c):
        return c if c < c_split else c + gap

    def fwd_sources(j):
        groups = {}
        for q in range(LANES):
            e = j * LANES + q
            c = e if e < c_split else e - gap
            if (c_split <= e < c_split + gap) or c >= d_cols:
                continue
            s, l = divmod(c, shard_w)
            groups.setdefault((s, l // LANES), []).append((l % LANES, q))
        return groups

    per_shard = -(-shard_w // LANES)

    def bwd_sources(j):
        s, b = divmod(j, per_shard)
        groups = {}
        for q in range(LANES):
            l = b * LANES + q
            if l >= shard_w:
                continue
            e = ext_of(s * shard_w + l)
            groups.setdefault((0, e // LANES), []).append((e % LANES, q))
        return groups

    return _regroup_table(n_blocks, fwd_sources), _regroup_table(4 * per_shard, bwd_sources), per_shard


def _in_spec(a, kind, tile):
    if kind == "row":
        return pl.BlockSpec((tile, a.shape[1]), lambda j, i: (i, 0))
    if kind == "const":
        return pl.BlockSpec(a.shape, lambda j, i: (0, 0))
    tag, cw, off = kind
    if tag == "rowc":
        return pl.BlockSpec((tile, cw), lambda j, i: (i, j + off))
    assert tag == "constc", kind
    return pl.BlockSpec((a.shape[0], cw), lambda j, i: (0, j + off))


def rowcall(name, fn, ins, outs, *, rows, tile=None, ncol=1):
    if tile is None:
        tile = min(TOK_TILE_BLOCKED if ncol > 1 else TOK_TILE, rows)
    n_in = len(ins)
    kinds = [o[3] for o in outs]

    def body(*refs):
        j, i = pl.program_id(0), pl.program_id(1)
        res = fn(*[r[...] for r in refs[:n_in]])
        for ref, val, kind in zip(refs[n_in:], res, kinds):
            if kind in ("row", "rowc"):
                ref[...] = val.astype(ref.dtype)
            else:
                first = (i == 0) if kind == "accc" else jnp.logical_and(i == 0, j == 0)

                @pl.when(first)
                def _(ref=ref, val=val):
                    ref[...] = val.astype(ref.dtype)

                @pl.when(jnp.logical_not(first))
                def _(ref=ref, val=val):
                    ref[...] += val.astype(ref.dtype)

    out_shape, out_specs = [], []
    for nr, nc, dtype, kind in outs:
        out_shape.append(jax.ShapeDtypeStruct((nr, nc), dtype))
        if kind == "row":
            out_specs.append(pl.BlockSpec((tile, nc), lambda j, i: (i, 0)))
        elif kind == "rowc":
            out_specs.append(pl.BlockSpec((tile, nc // ncol), lambda j, i: (i, j)))
        elif kind == "acc":
            out_specs.append(pl.BlockSpec((nr, nc), lambda j, i: (0, 0)))
        else:
            out_specs.append(pl.BlockSpec((nr, nc // ncol), lambda j, i: (0, j)))
    return pl.pallas_call(
        body, name=name, grid=(ncol, rows // tile), in_specs=[_in_spec(a, k, tile) for a, k in ins],
        out_specs=out_specs, out_shape=out_shape, compiler_params=_params(2))(*[a for a, _ in ins])


def rowvjp(name, fn, ins, cts, grads, *, rows, tile=None, ncol=1, adds=(), dup16=()):
    n_in = len(ins)
    ct_sizes = [len(c) for c in cts]
    flat_cts = [m for c in cts for m in c]
    n_ct = len(flat_cts)

    def wrapped(*vals):
        xs = list(vals[:n_in])
        gs = vals[n_in:n_in + n_ct]
        extra = vals[n_in + n_ct:]

        def f(*dvars):
            full = list(xs)
            for k, v in zip(grads, dvars):
                full[k] = v
            return fn(*full)

        outs, pull = jax.vjp(f, *[xs[k] for k in grads])
        cot, p = [], 0
        for o, size in zip(outs, ct_sizes):
            g = gs[p].astype(f32)
            for q in range(1, size):
                g = g + gs[p + q].astype(f32)
            cot.append(g.astype(o.dtype))
            p += size
        gv = list(pull(tuple(cot)))
        for (pos, _, _), e in zip(adds, extra):
            gv[pos] = gv[pos] + e.astype(gv[pos].dtype)
        return tuple(gv) + tuple(gv[pos] for pos in dup16)

    outs = []
    for k in grads:
        a, kind = ins[k]
        if kind == "row":
            outs.append((rows, a.shape[1] * ncol, f32, "rowc") if ncol > 1 else (rows, a.shape[1], f32, "row"))
        elif kind == "const":
            outs.append((a.shape[0], a.shape[1], f32, "acc"))
        elif kind[0] == "rowc":
            outs.append((rows, kind[1] * ncol, f32, "rowc"))
        else:
            outs.append((a.shape[0], kind[1] * ncol, f32, "accc"))
    for pos in dup16:
        nr, nc, _, kind = outs[pos]
        outs.append((nr, nc, bf16, kind))
    all_ins = list(ins) + flat_cts + [(a, kind) for _, a, kind in adds]
    return rowcall(name, wrapped, all_ins, outs, rows=rows, tile=tile, ncol=ncol)


def fn_rms(x, w):
    return (rms(x, w),)


def make_fn_gconv(n_norm_blocks):
    def fn(p, cw):
        c = cw[3:4] * p
        for jj in range(3):
            c = c + cw[jj:jj + 1] * tshift(p, 3 - jj)
        s = c * jax.nn.sigmoid(c)
        nrm = s * lax.rsqrt(jnp.sum(s * s, axis=-1, keepdims=True) + 1e-6)
        return (jnp.where(pl.program_id(0) < n_norm_blocks, nrm, s),)
    return fn


def fn_ggate(ps0, alog, dtb, e_g, e_b):
    g = -jnp.exp(alog) * softplus(ps0 + dtb)
    beta = jax.nn.sigmoid(ps0)
    return hdot(g, e_g), hdot(beta, e_b)


def fn_gpost(o, z, nw):
    return (rms(o, nw) * (z * jax.nn.sigmoid(z)),)


def fn_lerp(p, mu):
    return (p + (tshift(p, 1) - p) * mu,)


def fn_rprep(pk, psl, w0, a0, k_k, k_a, w2p, a2p, g2):
    g1, g2in = psl[:, :LANES], psl[:, LANES:]
    log_w = -softplus(-(w0 + bdot(jnp.tanh(g1), w2p))) - 0.5
    lw = -jnp.exp(log_w)
    a = jax.nn.sigmoid(a0 + bdot(g1, a2p))
    gate = bdot(jax.nn.sigmoid(g2in), g2)
    kkr = pk * k_k
    kk = kkr / jnp.maximum(jnp.sqrt(seg2sum(kkr * kkr)), 1e-12)
    kmod = pk * (1.0 + (a - 1.0) * k_a)
    return lw, kmod, kk, a, gate


def fn_rpost(y, r, kmod, v, gate, r_k, ln_w, ln_b):
    inv_n = 1.0 / RW_HEAD
    mean = seg2sum(y) * inv_n
    d = y - mean
    var = seg2sum(d * d) * inv_n
    yn = d * lax.rsqrt(var + RW_GN_EPS) * ln_w + ln_b
    bonus = seg2sum(r * kmod * r_k) * v
    return ((yn + bonus) * gate,)


def fn_xcore(q, k, v):
    s = bdot_nt(q, k) * (LANES ** -0.5)
    p = jax.nn.softmax(s, axis=-1)
    return (bdot(p, v),)


def fn_final(h, tgt, w):
    def loss_fn(h, w):
        return 0.5 * jnp.sum(jnp.mean(jnp.square(rms(h, w) - tgt), axis=-1))

    val, (dh, dw) = jax.value_and_grad(loss_fn, argnums=(0, 1))(h, w)
    return dh, dh.astype(bf16), dw, jnp.full((8, LANES), val, f32)


def fn_sumcols(n):
    def fn(x):
        w = x.shape[1] // n
        s = x[:, :w]
        for q in range(1, n):
            s = s + x[:, q * w:(q + 1) * w]
        return (s,)
    return fn


def _tri(c):
    ii = lax.broadcasted_iota(jnp.int32, (c, c), 0)
    jj = lax.broadcasted_iota(jnp.int32, (c, c), 1)
    return ii, jj


def _neumann_raw(m, steps):
    c = m.shape[-1]
    ii, jj = _tri(c)
    eye = (ii == jj).astype(f32)
    t, p = eye + m, m
    for _ in range(steps):
        p = _raw_dot(p, p, "nn", P_INV)
        t = _raw_dot(t, eye + p, "nn", P_INV)
    resid = eye - t + _raw_dot(m, t, "nn", P_RESID)
    return t + _raw_dot(t, resid, "nn", P_INV)


@functools.partial(jax.custom_vjp, nondiff_argnums=(1,))
def _neumann_inverse(m, steps):
    return _neumann_raw(m, steps)


def _neumann_fwd(m, steps):
    t = _neumann_raw(m, steps)
    return t, t


def _neumann_bwd(steps, t, g):
    return (_raw_dot(_raw_dot(t, g, "tn", P_RESID), t, "nt", P_RESID),)


_neumann_inverse.defvjp(_neumann_fwd, _neumann_bwd)


def cumsum_rows(x):
    t = x.shape[1]
    ii, jj = _tri(t)
    tri = jnp.broadcast_to((ii >= jj).astype(f32), (x.shape[0], t, t))
    return pdot(tri, x, "nn", P_CUMSUM)


def gdn_chunk(q, k, v, gb, bb, s):
    c = q.shape[1]
    ii, jj = _tri(c)
    low = ii >= jj
    gcb = cumsum_rows(gb)
    gl = jnp.sum(gb, axis=1, keepdims=True)
    gc_col = gcb[:, :, :c]
    diff = gc_col - jnp.swapaxes(gc_col, 1, 2)
    decay = jnp.where(low, jnp.exp(jnp.where(low, diff, 0.0)), 0.0)
    qs = q * (q.shape[2] ** -0.5)
    kb = k * bb
    a = jnp.where(ii > jj, pdot(kb, k, "nt", P_BULK) * decay, 0.0)
    t = _neumann_inverse(-a, c.bit_length() - 2)
    eg = jnp.exp(gcb)
    u = pdot(t, v * bb, "nn", P_BULK)
    w = pdot(t, kb * eg, "nn", P_BULK)
    attn = pdot(qs, k, "nt", P_BULK) * decay
    kd = k * jnp.exp(gl - gcb)
    v_new = u - pdot(w, s, "nn", P_BULK)
    o = pdot(qs * eg, s, "nn", P_BULK) + pdot(attn, v_new, "nn", P_BULK)
    s_new = s * jnp.exp(gl) + pdot(kd, v_new, "tn", P_BULK)
    return o, s_new


def wkv_chunk(r, lw, k, v, kk, a, s):
    t = r.shape[1]
    ii, jj = _tri(t)
    lo = lax.broadcasted_iota(jnp.int32, r.shape, 2) < RW_HEAD
    cl = cumsum_rows(lw)
    cl_last = jnp.sum(lw, axis=1, keepdims=True)
    al = -kk * jnp.exp(cl - lw)
    be = (a * kk) * jnp.exp(-cl)
    kt = k * jnp.exp(-cl)
    rt = r * jnp.exp(cl)

    def dot(xa, xb, mode="nn"):
        return pdot(xa, xb, mode, P_BULK)

    def pair(xa, xb, msk):
        m_lo = dot(jnp.where(lo, xa, 0.0), xb, "nt")
        m_hi = dot(jnp.where(lo, 0.0, xa), xb, "nt")
        return jnp.where(msk, m_lo, 0.0), jnp.where(msk, m_hi, 0.0)

    def sel(x_lo, x_hi):
        return jnp.where(lo, x_lo, x_hi)

    ab = pair(al, be, ii > jj)
    ak = pair(al, kt, ii > jj)
    rb = pair(rt, be, ii >= jj)
    rk = pair(rt, kt, ii >= jj)
    x = dot(al, s, "nt") + sel(dot(ak[0], v), dot(ak[1], v))
    steps = t.bit_length() - 2
    u = sel(dot(_neumann_inverse(ab[0], steps), x), dot(_neumann_inverse(ab[1], steps), x))
    y = dot(rt, s, "nt") + sel(dot(rb[0], u) + dot(rk[0], v), dot(rb[1], u) + dot(rk[1], v))
    vi = lax.broadcasted_iota(jnp.int32, s.shape, 1) < RW_HEAD
    ki = lax.broadcasted_iota(jnp.int32, s.shape, 2) < RW_HEAD
    s_new = jnp.where(vi == ki, (s + dot(u, be, "tn") + dot(v, kt, "tn")) * jnp.exp(cl_last), 0.0)
    return y, s_new


def _scan_group(ncol, offs):
    g = SCAN_GROUP
    while g > 1 and (ncol % g or any(o % g for o in offs)):
        g //= 2
    return g


def scan_fwd(name, chunk_fn, ins, *, rows, chunk, ncol):
    n = rows // chunk
    n_in = len(ins)
    grp = _scan_group(ncol, [off for _, off in ins])

    def body(*refs):
        o_ref, st_ref, s_scr = refs[n_in:]

        @pl.when(pl.program_id(1) == 0)
        def _():
            s_scr[...] = jnp.zeros_like(s_scr)

        cols = [slice(b * LANES, (b + 1) * LANES) for b in range(grp)]
        s = s_scr[...]
        st_ref[...] = s
        o, s_new = chunk_fn(*[jnp.stack([r[:, c] for c in cols]) for r in refs[:n_in]], s)
        for b, c in enumerate(cols):
            o_ref[:, c] = o[b]
        s_scr[...] = s_new

    def spec(off):
        return pl.BlockSpec((chunk, grp * LANES), lambda h, c: (c, h + off // grp))

    return pl.pallas_call(
        body, name=name, grid=(ncol // grp, n), in_specs=[spec(off) for _, off in ins],
        out_specs=[spec(0), pl.BlockSpec((grp, None, LANES, LANES), lambda h, c: (h, c, 0, 0))],
        out_shape=[jax.ShapeDtypeStruct((rows, ncol * LANES), f32),
                   jax.ShapeDtypeStruct((ncol, n, LANES, LANES), f32)],
        scratch_shapes=[pltpu.VMEM((grp, LANES, LANES), f32)], compiler_params=_params(2))(*[a for a, _ in ins])


def scan_bwd(name, chunk_fn, ins, states, d_out, d_off, *, rows, chunk, ncol):
    n = rows // chunk
    n_in = len(ins)
    grp = _scan_group(ncol, [off for _, off in ins] + [d_off])

    def body(*refs):
        st_ref, do_ref = refs[n_in:n_in + 2]
        g_refs = refs[n_in + 2:2 * n_in + 2]
        ds_scr = refs[-1]

        @pl.when(pl.program_id(1) == 0)
        def _():
            ds_scr[...] = jnp.zeros_like(ds_scr)

        cols = [slice(b * LANES, (b + 1) * LANES) for b in range(grp)]

        def batch(ref):
            return jnp.stack([ref[:, c] for c in cols])

        _, pull = jax.vjp(chunk_fn, *[batch(r) for r in refs[:n_in]], st_ref[...])
        gs = pull((batch(do_ref), ds_scr[...]))
        for ref, g in zip(g_refs, gs[:n_in]):
            for b, c in enumerate(cols):
                ref[:, c] = g[b]
        ds_scr[...] = gs[n_in]

    def spec(off):
        return pl.BlockSpec((chunk, grp * LANES), lambda h, c: (n - 1 - c, h + off // grp))

    st_spec = pl.BlockSpec((grp, None, LANES, LANES), lambda h, c: (h, n - 1 - c, 0, 0))
    return pl.pallas_call(
        body, name=name, grid=(ncol // grp, n), in_specs=[spec(off) for _, off in ins] + [st_spec, spec(d_off)],
        out_specs=[spec(0)] * n_in, out_shape=[jax.ShapeDtypeStruct((rows, ncol * LANES), f32)] * n_in,
        scratch_shapes=[pltpu.VMEM((grp, LANES, LANES), f32)],
        compiler_params=_params(2))(*[a for a, _ in ins], states, d_out)


def flip_exchange(name, arrs, flips, n_slots, slot_of, src_of, with_self):
    n = len(arrs)
    nf = len(flips)

    def body(*refs):
        ins, outs = refs[:n], refs[n:2 * n]
        send, recv, lsem = refs[2 * n:]
        me = (lax.axis_index("x"), lax.axis_index("y"), lax.axis_index("c"))
        copies = []
        for k in range(n):
            if with_self:
                cp = pltpu.make_async_copy(src_of(ins[k], me), outs[k].at[slot_of(me)], lsem.at[k])
                cp.start()
                copies.append(cp)
            for j, fl in enumerate(flips):
                peer = tuple(1 - m if f else m for m, f in zip(me, fl))
                cp = pltpu.make_async_remote_copy(
                    src_ref=src_of(ins[k], peer), dst_ref=outs[k].at[slot_of(me)], send_sem=send.at[k, j],
                    recv_sem=recv.at[k, j], device_id=peer, device_id_type=MESH)
                cp.start()
                copies.append(cp)
        for cp in copies:
            cp.wait()

    def out_sds(a):
        blk = src_of(jax.ShapeDtypeStruct(a.shape, a.dtype), None)
        return jax.ShapeDtypeStruct((n_slots,) + tuple(blk), a.dtype)

    any_spec = pl.BlockSpec(memory_space=pl.ANY)
    return pl.pallas_call(
        body, name=name, in_specs=[any_spec] * n, out_specs=[any_spec] * n, out_shape=[out_sds(a) for a in arrs],
        scratch_shapes=[pltpu.SemaphoreType.DMA((n, nf)), pltpu.SemaphoreType.DMA((n, nf)),
                        pltpu.SemaphoreType.DMA((n,))],
        compiler_params=pltpu.CompilerParams(has_side_effects=True))(*arrs)


_CHIP_FLIPS = ((1, 0, 0), (0, 1, 0), (1, 1, 0))
_ALL_FLIPS = ((0, 0, 1), (0, 1, 0), (0, 1, 1), (1, 0, 0), (1, 0, 1), (1, 1, 0), (1, 1, 1))


def _whole(ref, pos):
    return ref.shape if pos is None else ref


def _chip_block(ref, pos):
    return ref.shape[1:] if pos is None else ref.at[2 * pos[0] + pos[1]]


def _chip_slot(p):
    return 2 * p[0] + p[1]


def gather_chips(name, arrs):
    return flip_exchange(name, arrs, _CHIP_FLIPS, 4, _chip_slot, _whole, True)


def scatter_chips(name, arrs):
    return flip_exchange(name, arrs, _CHIP_FLIPS, 4, _chip_slot, _chip_block, True)


_HBM = pl.BlockSpec(memory_space=pltpu.HBM)
_SEM = pl.BlockSpec(memory_space=pltpu.SEMAPHORE)
_DATAFLOW = pltpu.SideEffectType.DATAFLOW_SIDE_EFFECTING


def _split_copies(mode, refs, n, send, recv):
    me = (lax.axis_index("x"), lax.axis_index("y"), lax.axis_index("c"))
    sib = (me[0], me[1], 1 - me[2])
    lands = refs[:n] if mode == "handover" else refs[n:2 * n]
    copies = []
    for k, land in enumerate(lands):
        half = land.shape[1] // 2
        mine = pl.ds(pl.multiple_of(me[2] * half, 16), half)
        for j, fl in enumerate(_CHIP_FLIPS):
            peer = tuple(1 - m if f else m for m, f in zip(me, fl))
            if mode == "gather":
                src, dst, to = refs[k], land.at[_chip_slot(me)], peer
            elif mode == "scatter":
                src, dst, to = refs[k].at[_chip_slot(peer)], land.at[_chip_slot(me)], peer
            elif mode == "gather_half":
                src, dst, to = refs[k].at[mine], land.at[_chip_slot(me), mine], peer
            else:
                src = dst = land.at[_chip_slot(peer), mine]
                to = sib
            q = k * len(_CHIP_FLIPS) + j
            copies.append(pltpu.make_async_remote_copy(src_ref=src, dst_ref=dst, send_sem=send.at[q],
                                                       recv_sem=recv.at[q], device_id=to, device_id_type=MESH))
    return copies


def split_start(name, mode, ops, n):
    ops = [pltpu.with_memory_space_constraint(a, pltpu.HBM) for a in ops]
    m = len(ops)

    def body(*refs):
        for cp in _split_copies(mode, refs[:m], n, refs[m], refs[m + 1]):
            cp.start()
        refs[-1][...] = jnp.zeros_like(refs[-1])

    sems = pltpu.SemaphoreType.DMA((n * len(_CHIP_FLIPS),))
    outs = pl.pallas_call(
        body, name=name, in_specs=[_HBM] * m,
        out_shape=(sems, sems, *[pltpu.HBM(a.shape, a.dtype) for a in ops], jax.ShapeDtypeStruct((8, LANES), f32)),
        out_specs=(_SEM, _SEM, *[_HBM] * m, pl.BlockSpec(memory_space=pltpu.VMEM)),
        input_output_aliases={i: 2 + i for i in range(m)},
        compiler_params=pltpu.CompilerParams(has_side_effects=_DATAFLOW))(*ops)
    return (outs[0], outs[1], list(outs[2:2 + m]), mode, n), outs[-1][0, 0]


def split_wait(name, state, after):
    send, recv, ops, mode, n = state
    m = len(ops)

    def body(*refs):
        for cp in _split_copies(mode, refs[:m], n, refs[m], refs[m + 1]):
            cp.wait_send()
            cp.wait_recv()

    outs = pl.pallas_call(
        body, name=name, in_specs=[_HBM] * m + [_SEM, _SEM, pl.BlockSpec(memory_space=pl.ANY)],
        out_shape=tuple(pltpu.HBM(a.shape, a.dtype) for a in ops), out_specs=tuple([_HBM] * m),
        input_output_aliases={i: i for i in range(m)},
        compiler_params=pltpu.CompilerParams(has_side_effects=_DATAFLOW))(*ops, send, recv, after)
    return list(outs[m - n:])


def chips_start(name, arrs, src_of, halves=False):
    me = _chip_slot((lax.axis_index("x"), lax.axis_index("y")))
    lands = []
    for a in arrs:
        blk = tuple(src_of(jax.ShapeDtypeStruct(a.shape, a.dtype), None))
        own = a if src_of is _whole else lax.dynamic_index_in_dim(a, me, 0, keepdims=False)
        lands.append(lax.dynamic_update_index_in_dim(lax.empty((4,) + blk, a.dtype), own, me, 0))
    mode = "scatter" if src_of is _chip_block else ("gather_half" if halves else "gather")
    return split_start(name, mode, list(arrs) + lands, len(arrs))


chips_wait = split_wait


def handover_start(name, lands):
    return split_start(name, "handover", lands, len(lands))


def gather_chips_halves(name, arrs):
    n = len(arrs)
    nf = len(_CHIP_FLIPS)
    split = [a.shape[0] % 32 == 0 for a in arrs]

    def body(*refs):
        ins, outs = refs[:n], refs[n:2 * n]
        send1, recv1, send2, recv2, lsem = refs[2 * n:]
        me = (lax.axis_index("x"), lax.axis_index("y"), lax.axis_index("c"))
        sib = (me[0], me[1], 1 - me[2])
        peers = [tuple(1 - m if f else m for m, f in zip(me, fl)) for fl in _CHIP_FLIPS]
        local, first, second = [], [], []
        for k in range(n):
            cp = pltpu.make_async_copy(ins[k], outs[k].at[_chip_slot(me)], lsem.at[k])
            cp.start()
            local.append(cp)
            half = ins[k].shape[0] // 2
            rows = pl.ds(pl.multiple_of(me[2] * half, 16), half) if split[k] else pl.ds(0, ins[k].shape[0])
            for j, peer in enumerate(peers):
                cp = pltpu.make_async_remote_copy(
                    src_ref=ins[k].at[rows], dst_ref=outs[k].at[_chip_slot(me), rows], send_sem=send1.at[k, j],
                    recv_sem=recv1.at[k, j], device_id=peer, device_id_type=MESH)
                cp.start()
                first.append((k, j, rows, cp))
        for k, j, rows, cp in first:
            cp.wait_recv()
            if split[k]:
                got = outs[k].at[_chip_slot(peers[j]), rows]
                fwd = pltpu.make_async_remote_copy(src_ref=got, dst_ref=got, send_sem=send2.at[k, j],
                                                   recv_sem=recv2.at[k, j], device_id=sib, device_id_type=MESH)
                fwd.start()
                second.append(fwd)
        for _, _, _, cp in first:
            cp.wait_send()
        for cp in second:
            cp.wait()
        for cp in local:
            cp.wait()

    any_spec = pl.BlockSpec(memory_space=pl.ANY)
    sems = pltpu.SemaphoreType.DMA((n, nf))
    return pl.pallas_call(
        body, name=name, in_specs=[any_spec] * n, out_specs=[any_spec] * n,
        out_shape=[jax.ShapeDtypeStruct((4,) + a.shape, a.dtype) for a in arrs],
        scratch_shapes=[sems, sems, sems, sems, pltpu.SemaphoreType.DMA((n,))],
        compiler_params=pltpu.CompilerParams(has_side_effects=True))(*arrs)


def swap_sibling(name, arrs):
    outs = flip_exchange(name, arrs, ((0, 0, 1),), 1, lambda p: 0, _whole, False)
    return [o[0] for o in outs]


def gather_all(arrs):
    return flip_exchange("gather_all", arrs, _ALL_FLIPS, 8, lambda p: 4 * p[0] + 2 * p[1] + p[2], _whole, True)


def _row_tile(nr, nc, n_arrays):
    budget = (20 << 20) // (n_arrays * 2 * 4 * max(nc, LANES))
    t = min(nr, budget) // 16 * 16
    while t > 0 and nr % t:
        t -= 16
    return t if t > 0 else nr


def sum_slots(name, x):
    ns, nr, nc = x.shape
    tile = _row_tile(nr, nc, ns + 1)

    def body(x_ref, o_ref):
        s = x_ref[0].astype(f32)
        for q in range(1, ns):
            s = s + x_ref[q].astype(f32)
        o_ref[...] = s

    return pl.pallas_call(
        body, name=name, grid=(nr // tile,), in_specs=[pl.BlockSpec((ns, tile, nc), lambda i: (0, i, 0))],
        out_specs=pl.BlockSpec((tile, nc), lambda i: (i, 0)), out_shape=jax.ShapeDtypeStruct((nr, nc), f32),
        compiler_params=_params(1))(x)


def adamw(name, w, g_parts, m, v):
    nr, nc = w.shape[-2:]
    n_g = len(g_parts)
    tile = _row_tile(nr, nc, 7 + n_g)

    def body(*refs):
        w_ref, m_ref, v_ref = refs[:3]
        g = refs[3][...]
        for r in refs[4:3 + n_g]:
            g = g + r[...]
        g_ref, d_ref, nm_ref, nv_ref = refs[3 + n_g:]
        nm = ADAM_B1 * m_ref[...] + (1.0 - ADAM_B1) * g
        nv = ADAM_B2 * v_ref[...] + (1.0 - ADAM_B2) * jnp.square(g)
        m_hat = nm / (1.0 - ADAM_B1 ** ADAM_STEP)
        v_hat = nv / (1.0 - ADAM_B2 ** ADAM_STEP)
        g_ref[...] = g
        d_ref[...] = -ADAM_LR * (m_hat / (jnp.sqrt(v_hat) + ADAM_EPS) + ADAM_WD * w_ref[...])
        nm_ref[...] = nm
        nv_ref[...] = nv

    spec = pl.BlockSpec((tile, nc), lambda i: (i, 0))
    spec3 = pl.BlockSpec((None, tile, nc), lambda i: (0, i, 0)) if w.ndim == 3 else spec
    return pl.pallas_call(
        body, name=name, grid=(nr // tile,), in_specs=[spec3] * 3 + [spec] * n_g, out_specs=[spec3] * 4,
        out_shape=[jax.ShapeDtypeStruct(w.shape, f32)] * 4, compiler_params=_params(1))(w, m, v, *g_parts)


def adamw_packed(w, g8, m, v):
    nr, nc = w.shape

    def body(w_ref, g_ref, m_ref, v_ref, go_ref, d_ref, nm_ref, nv_ref):
        g = g_ref[0]
        for q in range(1, 8):
            g = g + g_ref[q]
        nm = ADAM_B1 * m_ref[...] + (1.0 - ADAM_B1) * g
        nv = ADAM_B2 * v_ref[...] + (1.0 - ADAM_B2) * jnp.square(g)
        m_hat = nm / (1.0 - ADAM_B1 ** ADAM_STEP)
        v_hat = nv / (1.0 - ADAM_B2 ** ADAM_STEP)
        go_ref[...] = g
        d_ref[...] = -ADAM_LR * (m_hat / (jnp.sqrt(v_hat) + ADAM_EPS) + ADAM_WD * w_ref[...])
        nm_ref[...] = nm
        nv_ref[...] = nv

    return pl.pallas_call(body, name="adamw_packed", out_shape=[jax.ShapeDtypeStruct((nr, nc), f32)] * 4,
                          compiler_params=pltpu.CompilerParams(vmem_limit_bytes=VMEM_LIMIT))(w, g8, m, v)


def _pack(vectors):
    rows = []
    for a in vectors:
        flat = a.reshape(-1).astype(f32)
        pad = (-flat.shape[0]) % LANES
        rows.append(jnp.pad(flat, (0, pad)).reshape(-1, LANES))
    packed = jnp.concatenate(rows, axis=0)
    return jnp.pad(packed, ((0, (-packed.shape[0]) % 8), (0, 0)))


def _unpack(packed, like):
    out, r = [], 0
    for a in like:
        n = a.size
        nr = -(-n // LANES)
        out.append(packed[r:r + nr].reshape(-1)[:n].reshape(a.shape))
        r += nr
    return out


def kernel(x, mem, mix_norm_w, w_in, dn_conv_w, dn_a_log, dn_dt_bias, dn_norm_w, rw_mu, rw_w0, rw_w2, rw_a0, rw_a2, rw_g2, rw_k_k, rw_k_a, rw_r_k, rw_ln_w, rw_ln_b, w_out, xa_norm_w, mem_norm_w, xa_wq, xa_wk, xa_wv, xa_wo, ffn_norm_w, ffn_w1, ffn_w2, final_norm_w, loss_target, m_mix_norm_w, m_w_in, m_dn_conv_w, m_dn_a_log, m_dn_dt_bias, m_dn_norm_w, m_rw_mu, m_rw_w0, m_rw_w2, m_rw_a0, m_rw_a2, m_rw_g2, m_rw_k_k, m_rw_k_a, m_rw_r_k, m_rw_ln_w, m_rw_ln_b, m_w_out, m_xa_norm_w, m_mem_norm_w, m_xa_wq, m_xa_wk, m_xa_wv, m_xa_wo, m_ffn_norm_w, m_ffn_w1, m_ffn_w2, m_final_norm_w, v_mix_norm_w, v_w_in, v_dn_conv_w, v_dn_a_log, v_dn_dt_bias, v_dn_norm_w, v_rw_mu, v_rw_w0, v_rw_w2, v_rw_a0, v_rw_a2, v_rw_g2, v_rw_k_k, v_rw_k_a, v_rw_r_k, v_rw_ln_w, v_rw_ln_b, v_w_out, v_xa_norm_w, v_mem_norm_w, v_xa_wq, v_xa_wk, v_xa_wv, v_xa_wo, v_ffn_norm_w, v_ffn_w1, v_ffn_w2, v_final_norm_w):
    weights = dict(mix_norm_w=mix_norm_w, w_in=w_in, dn_conv_w=dn_conv_w, dn_a_log=dn_a_log, dn_dt_bias=dn_dt_bias, dn_norm_w=dn_norm_w, rw_mu=rw_mu, rw_w0=rw_w0, rw_w2=rw_w2, rw_a0=rw_a0, rw_a2=rw_a2, rw_g2=rw_g2, rw_k_k=rw_k_k, rw_k_a=rw_k_a, rw_r_k=rw_r_k, rw_ln_w=rw_ln_w, rw_ln_b=rw_ln_b, w_out=w_out, xa_norm_w=xa_norm_w, mem_norm_w=mem_norm_w, xa_wq=xa_wq, xa_wk=xa_wk, xa_wv=xa_wv, xa_wo=xa_wo, ffn_norm_w=ffn_norm_w, ffn_w1=ffn_w1, ffn_w2=ffn_w2, final_norm_w=final_norm_w)
    mom_m = dict(mix_norm_w=m_mix_norm_w, w_in=m_w_in, dn_conv_w=m_dn_conv_w, dn_a_log=m_dn_a_log, dn_dt_bias=m_dn_dt_bias, dn_norm_w=m_dn_norm_w, rw_mu=m_rw_mu, rw_w0=m_rw_w0, rw_w2=m_rw_w2, rw_a0=m_rw_a0, rw_a2=m_rw_a2, rw_g2=m_rw_g2, rw_k_k=m_rw_k_k, rw_k_a=m_rw_k_a, rw_r_k=m_rw_r_k, rw_ln_w=m_rw_ln_w, rw_ln_b=m_rw_ln_b, w_out=m_w_out, xa_norm_w=m_xa_norm_w, mem_norm_w=m_mem_norm_w, xa_wq=m_xa_wq, xa_wk=m_xa_wk, xa_wv=m_xa_wv, xa_wo=m_xa_wo, ffn_norm_w=m_ffn_norm_w, ffn_w1=m_ffn_w1, ffn_w2=m_ffn_w2, final_norm_w=m_final_norm_w)
    mom_v = dict(mix_norm_w=v_mix_norm_w, w_in=v_w_in, dn_conv_w=v_dn_conv_w, dn_a_log=v_dn_a_log, dn_dt_bias=v_dn_dt_bias, dn_norm_w=v_dn_norm_w, rw_mu=v_rw_mu, rw_w0=v_rw_w0, rw_w2=v_rw_w2, rw_a0=v_rw_a0, rw_a2=v_rw_a2, rw_g2=v_rw_g2, rw_k_k=v_rw_k_k, rw_k_a=v_rw_k_a, rw_r_k=v_rw_r_k, rw_ln_w=v_rw_ln_w, rw_ln_b=v_rw_ln_b, w_out=v_w_out, xa_norm_w=v_xa_norm_w, mem_norm_w=v_mem_norm_w, xa_wq=v_xa_wq, xa_wk=v_xa_wk, xa_wv=v_xa_wv, xa_wo=v_xa_wo, ffn_norm_w=v_ffn_norm_w, ffn_w1=v_ffn_w1, ffn_w2=v_ffn_w2, final_norm_w=v_final_norm_w)
    names = list(weights)

    seq, d = x.shape[1], x.shape[2]
    dnw = d // 2
    rww = d - dnw
    nh, nb = dnw // LANES, rww // LANES
    n_mem = mem.shape[1]
    lw_dim, la_dim, lg_dim = rw_w2.shape[1], rw_a2.shape[1], rw_g2.shape[1]
    assert lw_dim + la_dim == LANES and lg_dim == LANES and dnw % LANES == 0 and rww % LANES == 0
    xs, mems, tgt = x[0], mem[0], loss_target[0]

    col_sharded = ("w_in", "xa_wo", "ffn_w1", "dn_conv_w", "rw_w2", "rw_a2", "rw_g2")
    row_sharded = ("w_out", "xa_wq", "xa_wk", "xa_wv", "ffn_w2")
    f32_payload = ("dn_conv_w", "rw_w2", "rw_a2", "rw_g2")
    sharded = col_sharded + row_sharded
    payload = {n: weights[n][0].astype(f32 if n in f32_payload else bf16) for n in sharded}
    shard_w = w_in.shape[2]
    pad_w = -(-shard_w // LANES) * LANES
    payload["w_in"] = jnp.pad(payload["w_in"], ((0, 0), (0, pad_w - shard_w)))
    first = ("w_in", "dn_conv_w", "rw_w2", "rw_a2", "rw_g2")
    mid = ("w_out", "xa_wq", "xa_wk", "xa_wv", "xa_wo")
    late = ("ffn_w1", "ffn_w2")
    gathered = dict(zip(first, gather_chips_halves("gather_first", [payload[n] for n in first])))
    ordered = lax.optimization_barrier(([gathered[n] for n in first], [payload[n] for n in mid + late]))
    gathered = dict(zip(first, ordered[0]))
    payload.update(zip(mid + late, ordered[1]))
    mid_state, tok_mid = chips_start("gather_mid_start", [payload[n] for n in mid], _whole, halves=True)
    late_state, tok_late = chips_start("gather_late_start", [payload[n] for n in late], _whole, halves=True)
    mix_norm_w_in = mix_norm_w + (tok_mid + tok_late)

    def full(n):
        g = gathered[n]
        if n in col_sharded:
            return g.transpose(1, 0, 2).reshape(g.shape[1], 4 * g.shape[2])
        return g.reshape(4 * g.shape[1], g.shape[2])

    c_rw0 = 4 * dnw + 2 * nh
    c_rw = 3 * rww
    eb_ab, eb_r = 4 * nh, 4 * nh + 1
    eb_l1 = eb_r + 3 * nb
    n_ext = -(-(eb_l1 + 2) // 4) * 4
    tbl_fwd, tbl_bwd, per_shard = w_in_layout(4 * shard_w, shard_w, c_rw0, LANES - 2 * nh, n_ext)
    w_ext = lane_regroup(
        "w_in_regroup", gathered["w_in"], tbl_fwd,
        lambda p: pl.BlockSpec((None, d, LANES),
                               lambda i, j, t: (t[REGROUP_FIELDS * p, j], 0, t[REGROUP_FIELDS * p + 1, j])),
        pl.BlockSpec((d, LANES), lambda i, j, t: (0, j)), (d, n_ext * LANES), (1, n_ext), pad_w)
    conv_w = full("dn_conv_w")
    w2p = jnp.concatenate([full("rw_w2"), jnp.zeros((la_dim, rww), f32)], axis=0)
    a2p = jnp.concatenate([jnp.zeros((lw_dim, rww), f32), full("rw_a2")], axis=0)
    g2 = full("rw_g2")
    xaw = xa_wq.shape[2]
    nxh = xaw // LANES
    ffn = 4 * ffn_w1.shape[2]

    def lane_row(vec):
        return jnp.pad(vec.reshape(1, -1), ((0, 0), (0, LANES - vec.size)))

    alog_row, dtb_row = lane_row(dn_a_log), lane_row(dn_dt_bias)
    head_of_col = jnp.arange(dnw)[None, :] // LANES
    e_g = (jnp.arange(LANES)[:, None] == head_of_col).astype(f32)
    e_b = (jnp.arange(LANES)[:, None] == head_of_col + nh).astype(f32)
    mu_main, mu_small = rw_mu[:, :c_rw], rw_mu[:, c_rw:]
    r_k_row = rw_r_k.reshape(1, rww)
    fnw = final_norm_w.reshape(1, d)
    qb, kb_, vb, zb = 0, nh, 2 * nh, 3 * nh
    rb0 = eb_r
    cc = lambda w_, o_: ("constc", w_, o_)
    rc = lambda o_: ("rowc", LANES, o_)

    (u16,) = rowcall("norm_mix", fn_rms, [(xs, "row"), (mix_norm_w_in, "const")], [(seq, d, bf16, "row")], rows=seq)
    p_main = p_small = mm("in_proj", u16, w_ext)

    fn_gconv = make_fn_gconv(2 * nh)
    (qkv,) = rowcall("gdn_conv", fn_gconv, [(p_main, rc(0)), (conv_w, cc(LANES, 0))],
                     [(seq, 3 * dnw, f32, "rowc")], rows=seq, tile=seq, ncol=3 * nh)
    gate_ins = [(p_small, rc(eb_ab)), (alog_row, "const"), (dtb_row, "const"), (e_g, "const"), (e_b, "const")]
    g_b, beta_b = rowcall("gdn_gate", fn_ggate, gate_ins, [(seq, dnw, f32, "row")] * 2, rows=seq)
    gdn_ins = [(qkv, qb), (qkv, kb_), (qkv, vb), (g_b, 0), (beta_b, 0)]
    o_raw, gdn_states = scan_fwd("gdn_scan", gdn_chunk, gdn_ins, rows=seq, chunk=GDN_CHUNK, ncol=nh)
    mid_state, tok = handover_start("gather_mid_pass", chips_wait("gather_mid_wait", mid_state, o_raw))
    gpost_ins = [(o_raw, rc(0)), (p_main, rc(zb)), (dn_norm_w + tok, "const")]
    (o_dn,) = rowcall("gdn_post", fn_gpost, gpost_ins, [(seq, dnw, bf16, "rowc")], rows=seq, ncol=nh)

    (prw,) = rowcall("rw_lerp_main", fn_lerp, [(p_main, rc(rb0)), (mu_main, cc(LANES, 0))],
                     [(seq, c_rw, f32, "rowc")], rows=seq, tile=seq, ncol=3 * nb)
    (psl,) = rowcall("rw_lerp_small", fn_lerp, [(p_small, rc(eb_l1)), (mu_small, cc(LANES, 0))],
                     [(seq, 2 * LANES, f32, "rowc")], rows=seq, tile=seq, ncol=2)
    rprep_ins = [(prw, rc(nb)), (psl, "row"), (rw_w0, cc(LANES, 0)), (rw_a0, cc(LANES, 0)), (rw_k_k, cc(LANES, 0)),
                 (rw_k_a, cc(LANES, 0)), (w2p, cc(LANES, 0)), (a2p, cc(LANES, 0)), (g2, cc(LANES, 0))]
    lw, kmod, kk, a_rw, gate = rowcall("rw_prep", fn_rprep, rprep_ins, [(seq, rww, f32, "rowc")] * 5,
                                        rows=seq, ncol=nb)
    wkv_ins = [(prw, 0), (lw, 0), (kmod, 0), (prw, 2 * nb), (kk, 0), (a_rw, 0)]
    y_rw, wkv_states = scan_fwd("wkv_scan", wkv_chunk, wkv_ins, rows=seq, chunk=WKV_CHUNK, ncol=nb)
    rpost_ins = [(y_rw, rc(0)), (prw, rc(0)), (kmod, rc(0)), (prw, rc(2 * nb)), (gate, rc(0)),
                 (r_k_row, cc(LANES, 0)), (rw_ln_w, cc(LANES, 0)), (rw_ln_b, cc(LANES, 0))]
    (o_rw,) = rowcall("rw_post", fn_rpost, rpost_ins, [(seq, rww, bf16, "rowc")], rows=seq, ncol=nb)

    o_cat = jnp.concatenate([o_dn, o_rw], axis=1)
    late_state, tok = handover_start("gather_late_pass", chips_wait("gather_late_wait", late_state, o_cat))
    gathered.update(zip(mid, chips_wait("gather_mid_pass_wait", mid_state, o_cat)))
    w_out_f, wq_f, wk_f, wv_f, wo_f = full("w_out"), full("xa_wq"), full("xa_wk"), full("xa_wv"), full("xa_wo")
    h1 = mm("out_proj", o_cat, w_out_f, add=xs)

    (hn16,) = rowcall("norm_xa", fn_rms, [(h1, "row"), (xa_norm_w + tok, "const")], [(seq, d, bf16, "row")],
                      rows=seq)
    (mn16,) = rowcall("norm_mem", fn_rms, [(mems, "row"), (mem_norm_w, "const")], [(n_mem, d, bf16, "row")],
                      rows=n_mem)
    q_xa = mm("xa_q", hn16, wq_f)
    k_xa = mm("xa_k", mn16, wk_f)
    v_xa = mm("xa_v", mn16, wv_f)
    xcore_ins = [(q_xa, rc(0)), (k_xa, cc(LANES, 0)), (v_xa, cc(LANES, 0))]
    (o_xa,) = rowcall("xa_core", fn_xcore, xcore_ins, [(seq, xaw, bf16, "rowc")], rows=seq, ncol=nxh)
    h2 = mm("xa_o", o_xa, wo_f, add=h1)

    (fn16,) = rowcall("norm_ffn", fn_rms, [(h2, "row"), (ffn_norm_w, "const")], [(seq, d, bf16, "row")], rows=seq)
    gathered.update(zip(late, chips_wait("gather_late_pass_wait", late_state, fn16)))
    w1_f, w2_f = full("ffn_w1"), full("ffn_w2")
    a1_16, hid16 = mm("ffn_up", fn16, w1_f, epilogue=lambda r: (r, jnp.square(jnp.maximum(r, 0.0))),
                      out_dtypes=(bf16, bf16))
    h3 = mm("ffn_down", hid16, w2_f, add=h2)

    dh3, dh3_16, d_fnw, loss_rows = rowcall(
        "loss_head", fn_final, [(h3, "row"), (tgt, "row"), (fnw, "const")],
        [(seq, d, f32, "row"), (seq, d, bf16, "row"), (1, d, f32, "acc"), (8, LANES, f32, "acc")], rows=seq)

    da1_16 = mm("ffn_down_dx", dh3_16, w2_f, tb=True, extra=[a1_16], out_dtype=bf16,
                epilogue=lambda r, a1: (r * (2.0 * jnp.maximum(a1.astype(f32), 0.0)),))
    def by_chip(n, g):
        if g.ndim == 3:
            return g
        if n in col_sharded:
            return g.reshape(g.shape[0], 4, g.shape[1] // 4).transpose(1, 0, 2)
        return g.reshape(4, g.shape[0] // 4, g.shape[1])

    def behind(x, *deps):
        return lax.optimization_barrier((x, deps))[0]

    g_ffn_w2 = mm("ffn_down_dw", hid16, dh3_16, ta=True, out_dtype=bf16)
    ffn_w2_g, tok = chips_start("scatter_ffn_w2_start", [by_chip("ffn_w2", g_ffn_w2)], _chip_block)
    da1_16 = behind(da1_16, tok)
    g_ffn_w1 = mm("ffn_up_dw", fn16, da1_16, ta=True, out_dtype=bf16, by_chip=True)
    ffn_w1_g, tok = chips_start("scatter_ffn_w1_start", [g_ffn_w1], _chip_block)
    dfn = mm("ffn_up_dx", behind(da1_16, tok), w1_f, tb=True)
    dh2, d_ffn_nw, dh2_16 = rowvjp("norm_ffn_bwd", fn_rms, [(h2, "row"), (ffn_norm_w, "const")],
                                   [[(dfn, "row")]], [0, 1], rows=seq, adds=[(0, dh3, "row")], dup16=[0])

    do_xa = mm("xa_o_dx", dh2_16, wo_f, tb=True)
    g_xa_wo = mm("xa_o_dw", o_xa, dh2_16, ta=True, out_dtype=bf16, by_chip=True)
    dq_xa, dk_xa, dv_xa, dq16 = rowvjp("xa_core_bwd", fn_xcore, xcore_ins, [[(do_xa, rc(0))]], [0, 1, 2],
                                       rows=seq, ncol=nxh, dup16=[0])
    g_xa_wq = mm("xa_q_dw", hn16, dq16, ta=True, out_dtype=bf16)
    dhn = mm("xa_q_dx", dq16, wq_f, tb=True)
    dh1, d_xa_nw, dh1_16 = rowvjp("norm_xa_bwd", fn_rms, [(h1, "row"), (xa_norm_w, "const")], [[(dhn, "row")]],
                                  [0, 1], rows=seq, adds=[(0, dh2, "row")], dup16=[0])
    dk16, dv16 = dk_xa.astype(bf16), dv_xa.astype(bf16)
    g_xa_wk = mm("xa_k_dw", mn16, dk16, ta=True, out_dtype=bf16)
    g_xa_wv = mm("xa_v_dw", mn16, dv16, ta=True, out_dtype=bf16)
    dmn = mm("xa_v_dx", dv16, wv_f, tb=True, add=mm("xa_k_dx", dk16, wk_f, tb=True))
    (d_mem_nw,) = rowvjp("norm_mem_bwd", fn_rms, [(mems, "row"), (mem_norm_w, "const")], [[(dmn, "row")]], [1],
                         rows=n_mem)

    g_w_out = mm("out_proj_dw", o_cat, dh1_16, ta=True, out_dtype=bf16)
    do_cat = mm("out_proj_dx", dh1_16, w_out_f, tb=True)
    mid_grads = dict(w_out=g_w_out, xa_wq=g_xa_wq, xa_wk=g_xa_wk, xa_wv=g_xa_wv, xa_wo=g_xa_wo)
    mid_g, tok = chips_start("scatter_mid_start", [by_chip(n, mid_grads[n]) for n in mid], _chip_block)
    rpost_ins_b = rpost_ins[:5] + [(r_k_row + tok, cc(LANES, 0))] + rpost_ins[6:]

    dy, dr_a, dkmod_a, dv_a, dgate, d_r_k, d_ln_w, d_ln_b = rowvjp(
        "rw_post_bwd", fn_rpost, rpost_ins_b, [[(do_cat, rc(nh))]], [0, 1, 2, 3, 4, 5, 6, 7], rows=seq, ncol=nb)
    dr_b, dlw, dkmod_b, dv_b, dkk, da_rw = scan_bwd("wkv_scan_bwd", wkv_chunk, wkv_ins, wkv_states, dy, 0,
                                                    rows=seq, chunk=WKV_CHUNK, ncol=nb)
    rprep_cts = [[(dlw, rc(0))], [(dkmod_a, rc(0)), (dkmod_b, rc(0))], [(dkk, rc(0))], [(da_rw, rc(0))],
                 [(dgate, rc(0))]]
    dpk, dpsl_parts, d_w0, d_a0, d_k_k, d_k_a, d_w2p, d_a2p, d_g2 = rowvjp(
        "rw_prep_bwd", fn_rprep, rprep_ins, rprep_cts, [0, 1, 2, 3, 4, 5, 6, 7, 8], rows=seq, ncol=nb)
    (dpsl,) = rowcall("rw_prep_sum", fn_sumcols(nb), [(dpsl_parts, "row")], [(seq, 2 * LANES, f32, "row")], rows=seq)

    def lerp_bwd(tag, p, p_off, mu, mu_off, ct_lists, ncol):
        return rowvjp("rw_lerp_bwd_" + tag, fn_lerp, [(p, rc(p_off)), (mu, cc(LANES, mu_off))], [ct_lists], [0, 1],
                      rows=seq, tile=seq, ncol=ncol, dup16=[0])

    _, dmu_r, dpr16 = lerp_bwd("r", p_main, rb0, mu_main, 0, [(dr_a, rc(0)), (dr_b, rc(0))], nb)
    _, dmu_k, dpk16 = lerp_bwd("k", p_main, rb0 + nb, mu_main, nb, [(dpk, rc(0))], nb)
    _, dmu_v, dpv16 = lerp_bwd("v", p_main, rb0 + 2 * nb, mu_main, 2 * nb, [(dv_a, rc(0)), (dv_b, rc(0))], nb)
    _, dmu_s, dps12_16 = lerp_bwd("small", p_small, eb_l1, mu_small, 0, [(dpsl, rc(0))], 2)

    do_raw, dz, d_dn_nw, dz16 = rowvjp("gdn_post_bwd", fn_gpost, gpost_ins, [[(do_cat, rc(0))]], [0, 1, 2],
                                       rows=seq, ncol=nh, dup16=[1])
    dq_g, dk_g, dv_g, dg_b, dbeta_b = scan_bwd("gdn_scan_bwd", gdn_chunk, gdn_ins, gdn_states, do_raw, 0,
                                               rows=seq, chunk=GDN_CHUNK, ncol=nh)
    dps0, d_alog, d_dtb, dps0_16 = rowvjp("gdn_gate_bwd", fn_ggate, gate_ins, [[(dg_b, "row")], [(dbeta_b, "row")]],
                                          [0, 1, 2], rows=seq, dup16=[0])
    dqkv = jnp.concatenate([dq_g, dk_g, dv_g], axis=1)
    _, d_conv_w, dqkv16 = rowvjp("gdn_conv_bwd", fn_gconv, [(p_main, rc(0)), (conv_w, cc(LANES, 0))],
                                 [[(dqkv, rc(0))]], [0, 1], rows=seq, tile=seq, ncol=3 * nh, dup16=[0])

    dp16 = jnp.concatenate([dqkv16, dz16, dps0_16, dpr16, dpk16, dpv16, dps12_16,
                            jnp.zeros((seq, (n_ext - eb_l1 - 2) * LANES), bf16)], axis=1)
    g_w_ext = mm("in_proj_dw", u16, dp16, ta=True, out_dtype=bf16)
    g_w_in = lane_regroup(
        "w_in_grad_regroup", g_w_ext, tbl_bwd,
        lambda p: pl.BlockSpec((d, LANES), lambda i, j, t: (0, t[REGROUP_FIELDS * p + 1, j])),
        pl.BlockSpec((None, d, LANES), lambda i, j, t: (j // per_shard, 0, j % per_shard)),
        (4, d, pad_w), (1, 4 * per_shard), n_ext * LANES)
    first_grads = dict(w_in=g_w_in, dn_conv_w=d_conv_w, rw_w2=d_w2p[:lw_dim], rw_a2=d_a2p[lw_dim:], rw_g2=d_g2)
    first_g, tok = chips_start("scatter_first_start", [by_chip(n, first_grads[n]) for n in first], _chip_block)

    du = mm("in_proj_dx", behind(dp16, tok), w_ext, tb=True)
    grad_x, d_mix_nw = rowvjp("norm_mix_bwd", fn_rms, [(xs, "row"), (mix_norm_w, "const")], [[(du, "row")]],
                              [0, 1], rows=seq, adds=[(0, dh1, "row")])
    received = dict(zip(mid, chips_wait("scatter_mid_wait", mid_g, grad_x)))
    received["ffn_w2"] = chips_wait("scatter_ffn_w2_wait", ffn_w2_g, grad_x)[0]
    received["ffn_w1"] = chips_wait("scatter_ffn_w1_wait", ffn_w1_g, grad_x)[0]

    out = {}

    def reduce_and_update(tag, group):
        partial = [sum_slots("sum_chips_" + n, received[n]) for n in group]
        other = swap_sibling("swap_sibling_" + tag, partial)
        for n, p_mine, p_other in zip(group, partial, other):
            if weights[n].shape[2] % LANES:
                rows, cols = weights[n].shape[1:]
                lin = lambda a: jnp.swapaxes(a, 1, 2).reshape(-1, LANES)
                lin_g = lambda p: p.T[:cols].reshape(-1, LANES)
                res = adamw("adamw_" + n, lin(weights[n]), [lin_g(p_mine), lin_g(p_other)], lin(mom_m[n]),
                            lin(mom_v[n]))
                out[n] = [jnp.swapaxes(r.reshape(1, cols, rows), 1, 2) for r in res]
            else:
                out[n] = adamw("adamw_" + n, weights[n], [p_mine, p_other], mom_m[n], mom_v[n])

    reduce_and_update("rest", mid + late)
    updated = lax.optimization_barrier(tuple(out[n][1] for n in mid + late))
    received.update(zip(first, chips_wait("scatter_first_wait", first_g, updated[0])))
    reduce_and_update("first", first)

    small_names = [n for n in names if n not in sharded]
    small_local = dict(
        mix_norm_w=d_mix_nw, dn_a_log=d_alog[:, :nh], dn_dt_bias=d_dtb[:, :nh], dn_norm_w=d_dn_nw,
        rw_mu=jnp.concatenate([dmu_r, dmu_k, dmu_v, dmu_s], axis=1), rw_w0=d_w0, rw_a0=d_a0, rw_k_k=d_k_k,
        rw_k_a=d_k_a, rw_r_k=d_r_k, rw_ln_w=d_ln_w, rw_ln_b=d_ln_b, xa_norm_w=d_xa_nw, mem_norm_w=d_mem_nw,
        ffn_norm_w=d_ffn_nw, final_norm_w=d_fnw)
    loss_vec = jnp.where(jnp.arange(LANES) == 0, loss_rows[0], 0.0)
    packed, _ = lax.optimization_barrier((_pack([small_local[n] for n in small_names] + [loss_vec]), out["w_in"][0]))
    (g8,) = gather_all([packed])

    packed_like = [weights[n] for n in small_names] + [loss_vec]
    zero = jnp.zeros((LANES,), f32)
    res = adamw_packed(_pack([weights[n] for n in small_names] + [zero]), g8,
                       _pack([mom_m[n] for n in small_names] + [zero]),
                       _pack([mom_v[n] for n in small_names] + [zero]))
    unpacked = [_unpack(r, packed_like) for r in res]
    for i, n in enumerate(small_names):
        out[n] = [u[i] for u in unpacked]
    loss = unpacked[0][-1][0]

    return (loss, grad_x.reshape(x.shape), *[out[n][0] for n in names], *[out[n][1] for n in names],
            *[out[n][2] for n in names], *[out[n][3] for n in names])
```

```python
import functools

import jax
import jax.numpy as jnp
from jax import lax
from jax.experimental import pallas as pl
from jax.experimental.pallas import tpu as pltpu

f32 = jnp.float32
bf16 = jnp.bfloat16
HI = lax.Precision.HIGHEST
MESH = pl.DeviceIdType.MESH

LANES = 128
VMEM_LIMIT = 56 << 20
TOK_TILE = 256
TOK_TILE_BLOCKED = 1024
MM_TILE = 1024
MM_TILE_K = 2048
GDN_CHUNK = 128
WKV_CHUNK = 64
SCAN_GROUP = 8
P_BULK = 1
P_INV = 1
P_RESID = 3
P_CUMSUM = 3
RMS_EPS = 1e-6
RW_GN_EPS = 64e-5
RW_HEAD = 64

ADAM_LR, ADAM_B1, ADAM_B2, ADAM_EPS, ADAM_WD, ADAM_STEP = 0.001, 0.9, 0.999, 1e-08, 0.01, 10


def _params(n_grid):
    return pltpu.CompilerParams(dimension_semantics=("arbitrary",) * n_grid, vmem_limit_bytes=VMEM_LIMIT)


_DIMS = {"nn": (((1,), (0,)), ((), ())), "nt": (((1,), (1,)), ((), ())), "tn": (((0,), (0,)), ((), ()))}
_DIMS_BATCHED = {"nn": (((2,), (1,)), ((0,), (0,))), "nt": (((2,), (2,)), ((0,), (0,))),
                 "tn": (((1,), (1,)), ((0,), (0,)))}


def _raw_dot(a, b, mode, passes):
    dims = (_DIMS if a.ndim == 2 else _DIMS_BATCHED)[mode]
    if passes == 6:
        return lax.dot_general(a.astype(f32), b.astype(f32), dims, precision=HI, preferred_element_type=f32)
    ah, bh = a.astype(bf16), b.astype(bf16)
    r = lax.dot_general(ah, bh, dims, preferred_element_type=f32)
    if passes == 3:
        al = (a - ah.astype(f32)).astype(bf16)
        bl = (b - bh.astype(f32)).astype(bf16)
        r = r + lax.dot_general(al, bh, dims, preferred_element_type=f32)
        r = r + lax.dot_general(ah, bl, dims, preferred_element_type=f32)
    return r


@functools.partial(jax.custom_vjp, nondiff_argnums=(2, 3))
def pdot(a, b, mode, passes):
    return _raw_dot(a, b, mode, passes)


def _pdot_bwd(mode, passes, res, g):
    a, b = res
    if mode == "nn":
        da, db = _raw_dot(g, b, "nt", passes), _raw_dot(a, g, "tn", passes)
    elif mode == "nt":
        da, db = _raw_dot(g, b, "nn", passes), _raw_dot(g, a, "tn", passes)
    else:
        da, db = _raw_dot(b, g, "nt", passes), _raw_dot(a, g, "nn", passes)
    return da.astype(a.dtype), db.astype(b.dtype)


pdot.defvjp(lambda a, b, mode, passes: (_raw_dot(a, b, mode, passes), (a, b)), _pdot_bwd)


def hdot(a, b):
    return pdot(a, b, "nn", 6)


def bdot(a, b):
    return pdot(a, b, "nn", 1)


def bdot_nt(a, b):
    return pdot(a, b, "nt", 1)


def _shift_rows(x, k):
    row = lax.broadcasted_iota(jnp.int32, x.shape, 0)
    return jnp.where(row < k, 0.0, pltpu.roll(x, k, axis=0))


def _unshift_rows(g, k):
    n = g.shape[0]
    row = lax.broadcasted_iota(jnp.int32, g.shape, 0)
    return jnp.where(row >= n - k, 0.0, pltpu.roll(g, n - k, axis=0))


@functools.partial(jax.custom_vjp, nondiff_argnums=(1,))
def tshift(x, k):
    return _shift_rows(x, k)


tshift.defvjp(lambda x, k: (_shift_rows(x, k), None), lambda k, _, g: (_unshift_rows(g, k),))


def rms(x, w):
    x = x.astype(f32)
    return x * lax.rsqrt(jnp.mean(x * x, axis=-1, keepdims=True) + RMS_EPS) * w


def softplus(x):
    return jnp.maximum(x, 0.0) + jnp.log(1.0 + jnp.exp(-jnp.abs(x)))


def seg2sum(x):
    lo = lax.broadcasted_iota(jnp.int32, x.shape, 1) < RW_HEAD
    s_lo = jnp.sum(jnp.where(lo, x, 0.0), axis=-1, keepdims=True)
    s_hi = jnp.sum(jnp.where(lo, 0.0, x), axis=-1, keepdims=True)
    return jnp.where(lo, s_lo, s_hi)


def _tile(n, pref):
    if n <= pref:
        return n
    t = pref
    while t >= LANES:
        if n % t == 0:
            return t
        t -= LANES
    return n


def mm(name, a, b, *, ta=False, tb=False, add=None, out_dtype=f32, by_chip=False, epilogue=None, extra=(),
       out_dtypes=None, after=None):
    (k, m) = a.shape if ta else a.shape[::-1]
    (n, kb) = b.shape if tb else b.shape[::-1]
    assert k == kb, (name, a.shape, b.shape)
    tm, tk = _tile(m, MM_TILE), _tile(k, MM_TILE_K)
    tn = _tile(n // 4, MM_TILE) if by_chip else _tile(n, MM_TILE)
    nk = k // tk
    dims = (((0,) if ta else (1,), (1,) if tb else (0,)), ((), ()))
    extra = list(extra) + ([] if add is None else [add])
    out_dtypes = [out_dtype] if out_dtypes is None else list(out_dtypes)
    n_extra, n_out = len(extra), len(out_dtypes)
    n_after = 0 if after is None else 1

    def body(*refs):
        a_ref, b_ref = refs[:2]
        x_refs = refs[2:2 + n_extra]
        o_refs = refs[2 + n_extra + n_after:2 + n_extra + n_after + n_out]
        part = lax.dot_general(a_ref[...].astype(bf16), b_ref[...].astype(bf16), dims, preferred_element_type=f32)

        def finish(r):
            xs = [x[...] for x in x_refs]
            if add is not None:
                r = r + xs.pop().astype(f32)
            outs = (r,) if epilogue is None else epilogue(r, *xs)
            for o_ref, o in zip(o_refs, outs):
                o_ref[...] = o.astype(o_ref.dtype)

        if nk == 1:
            finish(part)
            return
        acc = refs[-1]
        kk = pl.program_id(2)

        @pl.when(kk == 0)
        def _():
            acc[...] = part

        @pl.when(kk > 0)
        def _():
            acc[...] += part

        @pl.when(kk == nk - 1)
        def _():
            finish(acc[...])

    a_spec = pl.BlockSpec((tk, tm), lambda i, j, q: (q, i)) if ta else pl.BlockSpec((tm, tk), lambda i, j, q: (i, q))
    b_spec = pl.BlockSpec((tn, tk), lambda i, j, q: (j, q)) if tb else pl.BlockSpec((tk, tn), lambda i, j, q: (q, j))
    x_spec = pl.BlockSpec((tm, tn), lambda i, j, q: (i, j))
    if by_chip:
        per_chip = n // 4 // tn
        o_spec = pl.BlockSpec((None, tm, tn), lambda i, j, q: (j // per_chip, i, j % per_chip))
        o_shape = (4, m, n // 4)
    else:
        o_spec, o_shape = x_spec, (m, n)
    afters = [] if after is None else [jnp.reshape(after, (1, 1))]
    res = pl.pallas_call(
        body, name=name, grid=(m // tm, n // tn, nk),
        in_specs=[a_spec, b_spec] + [x_spec] * n_extra + [pl.BlockSpec(memory_space=pl.ANY)] * n_after,
        out_specs=[o_spec] * n_out, out_shape=[jax.ShapeDtypeStruct(o_shape, dt) for dt in out_dtypes],
        scratch_shapes=[pltpu.VMEM((tm, tn), f32)] if nk > 1 else [],
        compiler_params=_params(3))(a, b, *extra, *afters)
    return res[0] if n_out == 1 else res


REGROUP_PIECES = 4


REGROUP_FIELDS = 5


def _regroup_table(n_out, sources_of):
    import numpy as np
    tbl = np.zeros((REGROUP_FIELDS * REGROUP_PIECES, n_out), np.int32)
    for j in range(n_out):
        groups = sorted(sources_of(j).items())
        assert len(groups) <= REGROUP_PIECES, (j, len(groups))
        for p in range(REGROUP_PIECES):
            if p < len(groups):
                key, lanes = groups[p]
                shifts = {q - s for s, q in lanes}
                qs = sorted(q for _, q in lanes)
                assert len(shifts) == 1 and qs == list(range(qs[0], qs[-1] + 1)), (j, key)
                row = (key[0], key[1], shifts.pop(), qs[0], qs[-1] + 1)
            else:
                row = (0, 0, 0, 0, 0)
            tbl[REGROUP_FIELDS * p:REGROUP_FIELDS * (p + 1), j] = row
    return jnp.asarray(tbl)


def lane_regroup(name, src, table, src_spec, out_spec, out_shape, grid, src_width):
    rows = src.shape[-2]

    def body(tbl, *refs):
        o_ref, acc = refs[REGROUP_PIECES:]
        j = pl.program_id(1)
        acc[...] = jnp.zeros_like(acc)
        for p in range(REGROUP_PIECES):
            blk, shift, lo, hi = (tbl[REGROUP_FIELDS * p + f, j] for f in range(1, REGROUP_FIELDS))

            @pl.when(hi > lo)
            def _(p=p, blk=blk, shift=shift, lo=lo, hi=hi):
                lane = lax.broadcasted_iota(jnp.int32, (rows, LANES), 1)
                pi = lax.broadcasted_iota(jnp.int32, (LANES, LANES), 0)
                qi = lax.broadcasted_iota(jnp.int32, (LANES, LANES), 1)
                x = jnp.where(lane < src_width - blk * LANES, refs[p][...], jnp.zeros((), src.dtype))
                sel = jnp.logical_and(qi - pi == shift, jnp.logical_and(qi >= lo, qi < hi))
                acc[...] += jnp.dot(x, sel.astype(src.dtype), preferred_element_type=f32)

        o_ref[...] = acc[...].astype(o_ref.dtype)

    return pl.pallas_call(
        body, name=name, out_shape=jax.ShapeDtypeStruct(out_shape, src.dtype),
        grid_spec=pltpu.PrefetchScalarGridSpec(
            num_scalar_prefetch=1, grid=grid, in_specs=[src_spec(p) for p in range(REGROUP_PIECES)],
            out_specs=out_spec, scratch_shapes=[pltpu.VMEM((rows, LANES), f32)]),
        compiler_params=_params(2))(table, *[src] * REGROUP_PIECES)


def w_in_layout(d_cols, shard_w, c_split, gap, n_blocks):
    def ext_of(c):
        return c if c < c_split else c + gap

    def fwd_sources(j):
        groups = {}
        for q in range(LANES):
            e = j * LANES + q
            c = e if e < c_split else e - gap
            if (c_split <= e < c_split + gap) or c >= d_cols:
                continue
            s, l = divmod(c, shard_w)
            groups.setdefault((s, l // LANES), []).append((l % LANES, q))
        return groups

    per_shard = -(-shard_w // LANES)

    def bwd_sources(j):
        s, b = divmod(j, per_shard)
        groups = {}
        for q in range(LANES):
            l = b * LANES + q
            if l >= shard_w:
                continue
            e = ext_of(s * shard_w + l)
            groups.setdefault((0, e // LANES), []).append((e % LANES, q))
        return groups

    return _regroup_table(n_blocks, fwd_sources), _regroup_table(4 * per_shard, bwd_sources), per_shard


def _in_spec(a, kind, tile):
    if kind == "row":
        return pl.BlockSpec((tile, a.shape[1]), lambda j, i: (i, 0))
    if kind == "const":
        return pl.BlockSpec(a.shape, lambda j, i: (0, 0))
    tag, cw, off = kind
    if tag == "rowc":
        return pl.BlockSpec((tile, cw), lambda j, i: (i, j + off))
    assert tag == "constc", kind
    return pl.BlockSpec((a.shape[0], cw), lambda j, i: (0, j + off))


def rowcall(name, fn, ins, outs, *, rows, tile=None, ncol=1):
    if tile is None:
        tile = min(TOK_TILE_BLOCKED if ncol > 1 else TOK_TILE, rows)
    n_in = len(ins)
    kinds = [o[3] for o in outs]

    def body(*refs):
        j, i = pl.program_id(0), pl.program_id(1)
        res = fn(*[r[...] for r in refs[:n_in]])
        for ref, val, kind in zip(refs[n_in:], res, kinds):
            if kind in ("row", "rowc"):
                ref[...] = val.astype(ref.dtype)
            else:
                first = (i == 0) if kind == "accc" else jnp.logical_and(i == 0, j == 0)

                @pl.when(first)
                def _(ref=ref, val=val):
                    ref[...] = val.astype(ref.dtype)

                @pl.when(jnp.logical_not(first))
                def _(ref=ref, val=val):
                    ref[...] += val.astype(ref.dtype)

    out_shape, out_specs = [], []
    for nr, nc, dtype, kind in outs:
        out_shape.append(jax.ShapeDtypeStruct((nr, nc), dtype))
        if kind == "row":
            out_specs.append(pl.BlockSpec((tile, nc), lambda j, i: (i, 0)))
        elif kind == "rowc":
            out_specs.append(pl.BlockSpec((tile, nc // ncol), lambda j, i: (i, j)))
        elif kind == "acc":
            out_specs.append(pl.BlockSpec((nr, nc), lambda j, i: (0, 0)))
        else:
            out_specs.append(pl.BlockSpec((nr, nc // ncol), lambda j, i: (0, j)))
    return pl.pallas_call(
        body, name=name, grid=(ncol, rows // tile), in_specs=[_in_spec(a, k, tile) for a, k in ins],
        out_specs=out_specs, out_shape=out_shape, compiler_params=_params(2))(*[a for a, _ in ins])


def rowvjp(name, fn, ins, cts, grads, *, rows, tile=None, ncol=1, adds=(), dup16=()):
    n_in = len(ins)
    ct_sizes = [len(c) for c in cts]
    flat_cts = [m for c in cts for m in c]
    n_ct = len(flat_cts)

    def wrapped(*vals):
        xs = list(vals[:n_in])
        gs = vals[n_in:n_in + n_ct]
        extra = vals[n_in + n_ct:]

        def f(*dvars):
            full = list(xs)
            for k, v in zip(grads, dvars):
                full[k] = v
            return fn(*full)

        outs, pull = jax.vjp(f, *[xs[k] for k in grads])
        cot, p = [], 0
        for o, size in zip(outs, ct_sizes):
            g = gs[p].astype(f32)
            for q in range(1, size):
                g = g + gs[p + q].astype(f32)
            cot.append(g.astype(o.dtype))
            p += size
        gv = list(pull(tuple(cot)))
        for (pos, _, _), e in zip(adds, extra):
            gv[pos] = gv[pos] + e.astype(gv[pos].dtype)
        return tuple(gv) + tuple(gv[pos] for pos in dup16)

    outs = []
    for k in grads:
        a, kind = ins[k]
        if kind == "row":
            outs.append((rows, a.shape[1] * ncol, f32, "rowc") if ncol > 1 else (rows, a.shape[1], f32, "row"))
        elif kind == "const":
            outs.append((a.shape[0], a.shape[1], f32, "acc"))
        elif kind[0] == "rowc":
            outs.append((rows, kind[1] * ncol, f32, "rowc"))
        else:
            outs.append((a.shape[0], kind[1] * ncol, f32, "accc"))
    for pos in dup16:
        nr, nc, _, kind = outs[pos]
        outs.append((nr, nc, bf16, kind))
    all_ins = list(ins) + flat_cts + [(a, kind) for _, a, kind in adds]
    return rowcall(name, wrapped, all_ins, outs, rows=rows, tile=tile, ncol=ncol)


def fn_rms(x, w):
    return (rms(x, w),)


def make_fn_gconv(n_norm_blocks):
    def fn(p, cw):
        c = cw[3:4] * p
        for jj in range(3):
            c = c + cw[jj:jj + 1] * tshift(p, 3 - jj)
        s = c * jax.nn.sigmoid(c)
        nrm = s * lax.rsqrt(jnp.sum(s * s, axis=-1, keepdims=True) + 1e-6)
        return (jnp.where(pl.program_id(0) < n_norm_blocks, nrm, s),)
    return fn


def fn_ggate(ps0, alog, dtb, e_g, e_b):
    g = -jnp.exp(alog) * softplus(ps0 + dtb)
    beta = jax.nn.sigmoid(ps0)
    return hdot(g, e_g), hdot(beta, e_b)


def fn_gpost(o, z, nw):
    return (rms(o, nw) * (z * jax.nn.sigmoid(z)),)


def fn_lerp(p, mu):
    return (p + (tshift(p, 1) - p) * mu,)


def fn_rprep(pk, psl, w0, a0, k_k, k_a, w2p, a2p, g2):
    g1, g2in = psl[:, :LANES], psl[:, LANES:]
    log_w = -softplus(-(w0 + bdot(jnp.tanh(g1), w2p))) - 0.5
    lw = -jnp.exp(log_w)
    a = jax.nn.sigmoid(a0 + bdot(g1, a2p))
    gate = bdot(jax.nn.sigmoid(g2in), g2)
    kkr = pk * k_k
    kk = kkr / jnp.maximum(jnp.sqrt(seg2sum(kkr * kkr)), 1e-12)
    kmod = pk * (1.0 + (a - 1.0) * k_a)
    return lw, kmod, kk, a, gate


def fn_rpost(y, r, kmod, v, gate, r_k, ln_w, ln_b):
    inv_n = 1.0 / RW_HEAD
    mean = seg2sum(y) * inv_n
    d = y - mean
    var = seg2sum(d * d) * inv_n
    yn = d * lax.rsqrt(var + RW_GN_EPS) * ln_w + ln_b
    bonus = seg2sum(r * kmod * r_k) * v
    return ((yn + bonus) * gate,)


def fn_xcore(q, k, v):
    s = bdot_nt(q, k) * (LANES ** -0.5)
    p = jax.nn.softmax(s, axis=-1)
    return (bdot(p, v),)


def fn_final(h, tgt, w):
    def loss_fn(h, w):
        return 0.5 * jnp.sum(jnp.mean(jnp.square(rms(h, w) - tgt), axis=-1))

    val, (dh, dw) = jax.value_and_grad(loss_fn, argnums=(0, 1))(h, w)
    return dh, dh.astype(bf16), dw, jnp.full((8, LANES), val, f32)


def fn_sumcols(n):
    def fn(x):
        w = x.shape[1] // n
        s = x[:, :w]
        for q in range(1, n):
            s = s + x[:, q * w:(q + 1) * w]
        return (s,)
    return fn


def _tri(c):
    ii = lax.broadcasted_iota(jnp.int32, (c, c), 0)
    jj = lax.broadcasted_iota(jnp.int32, (c, c), 1)
    return ii, jj


def _neumann_raw(m, steps):
    c = m.shape[-1]
    ii, jj = _tri(c)
    eye = (ii == jj).astype(f32)
    t, p = eye + m, m
    for _ in range(steps):
        p = _raw_dot(p, p, "nn", P_INV)
        t = _raw_dot(t, eye + p, "nn", P_INV)
    resid = eye - t + _raw_dot(m, t, "nn", P_RESID)
    return t + _raw_dot(t, resid, "nn", P_INV)


@functools.partial(jax.custom_vjp, nondiff_argnums=(1,))
def _neumann_inverse(m, steps):
    return _neumann_raw(m, steps)


def _neumann_fwd(m, steps):
    t = _neumann_raw(m, steps)
    return t, t


def _neumann_bwd(steps, t, g):
    return (_raw_dot(_raw_dot(t, g, "tn", P_RESID), t, "nt", P_RESID),)


_neumann_inverse.defvjp(_neumann_fwd, _neumann_bwd)


def cumsum_rows(x):
    t = x.shape[1]
    ii, jj = _tri(t)
    tri = jnp.broadcast_to((ii >= jj).astype(f32), (x.shape[0], t, t))
    return pdot(tri, x, "nn", P_CUMSUM)


def gdn_chunk(q, k, v, gb, bb, s):
    c = q.shape[1]
    ii, jj = _tri(c)
    low = ii >= jj
    gcb = cumsum_rows(gb)
    gl = jnp.sum(gb, axis=1, keepdims=True)
    gc_col = gcb[:, :, :c]
    diff = gc_col - jnp.swapaxes(gc_col, 1, 2)
    decay = jnp.where(low, jnp.exp(jnp.where(low, diff, 0.0)), 0.0)
    qs = q * (q.shape[2] ** -0.5)
    kb = k * bb
    a = jnp.where(ii > jj, pdot(kb, k, "nt", P_BULK) * decay, 0.0)
    t = _neumann_inverse(-a, c.bit_length() - 2)
    eg = jnp.exp(gcb)
    u = pdot(t, v * bb, "nn", P_BULK)
    w = pdot(t, kb * eg, "nn", P_BULK)
    attn = pdot(qs, k, "nt", P_BULK) * decay
    kd = k * jnp.exp(gl - gcb)
    v_new = u - pdot(w, s, "nn", P_BULK)
    o = pdot(qs * eg, s, "nn", P_BULK) + pdot(attn, v_new, "nn", P_BULK)
    s_new = s * jnp.exp(gl) + pdot(kd, v_new, "tn", P_BULK)
    return o, s_new


def wkv_chunk(r, lw, k, v, kk, a, s):
    t = r.shape[1]
    ii, jj = _tri(t)
    lo = lax.broadcasted_iota(jnp.int32, r.shape, 2) < RW_HEAD
    cl = cumsum_rows(lw)
    cl_last = jnp.sum(lw, axis=1, keepdims=True)
    al = -kk * jnp.exp(cl - lw)
    be = (a * kk) * jnp.exp(-cl)
    kt = k * jnp.exp(-cl)
    rt = r * jnp.exp(cl)

    def dot(xa, xb, mode="nn"):
        return pdot(xa, xb, mode, P_BULK)

    def pair(xa, xb, msk):
        m_lo = dot(jnp.where(lo, xa, 0.0), xb, "nt")
        m_hi = dot(jnp.where(lo, 0.0, xa), xb, "nt")
        return jnp.where(msk, m_lo, 0.0), jnp.where(msk, m_hi, 0.0)

    def sel(x_lo, x_hi):
        return jnp.where(lo, x_lo, x_hi)

    ab = pair(al, be, ii > jj)
    ak = pair(al, kt, ii > jj)
    rb = pair(rt, be, ii >= jj)
    rk = pair(rt, kt, ii >= jj)
    x = dot(al, s, "nt") + sel(dot(ak[0], v), dot(ak[1], v))
    steps = t.bit_length() - 2
    u = sel(dot(_neumann_inverse(ab[0], steps), x), dot(_neumann_inverse(ab[1], steps), x))
    y = dot(rt, s, "nt") + sel(dot(rb[0], u) + dot(rk[0], v), dot(rb[1], u) + dot(rk[1], v))
    vi = lax.broadcasted_iota(jnp.int32, s.shape, 1) < RW_HEAD
    ki = lax.broadcasted_iota(jnp.int32, s.shape, 2) < RW_HEAD
    s_new = jnp.where(vi == ki, (s + dot(u, be, "tn") + dot(v, kt, "tn")) * jnp.exp(cl_last), 0.0)
    return y, s_new


def _scan_group(ncol, offs):
    g = SCAN_GROUP
    while g > 1 and (ncol % g or any(o % g for o in offs)):
        g //= 2
    return g


def scan_fwd(name, chunk_fn, ins, *, rows, chunk, ncol):
    n = rows // chunk
    n_in = len(ins)
    grp = _scan_group(ncol, [off for _, off in ins])

    def body(*refs):
        o_ref, st_ref, s_scr = refs[n_in:]

        @pl.when(pl.program_id(1) == 0)
        def _():
            s_scr[...] = jnp.zeros_like(s_scr)

        cols = [slice(b * LANES, (b + 1) * LANES) for b in range(grp)]
        s = s_scr[...]
        st_ref[...] = s
        o, s_new = chunk_fn(*[jnp.stack([r[:, c] for c in cols]) for r in refs[:n_in]], s)
        for b, c in enumerate(cols):
            o_ref[:, c] = o[b]
        s_scr[...] = s_new

    def spec(off):
        return pl.BlockSpec((chunk, grp * LANES), lambda h, c: (c, h + off // grp))

    return pl.pallas_call(
        body, name=name, grid=(ncol // grp, n), in_specs=[spec(off) for _, off in ins],
        out_specs=[spec(0), pl.BlockSpec((grp, None, LANES, LANES), lambda h, c: (h, c, 0, 0))],
        out_shape=[jax.ShapeDtypeStruct((rows, ncol * LANES), f32),
                   jax.ShapeDtypeStruct((ncol, n, LANES, LANES), f32)],
        scratch_shapes=[pltpu.VMEM((grp, LANES, LANES), f32)], compiler_params=_params(2))(*[a for a, _ in ins])


def scan_bwd(name, chunk_fn, ins, states, d_out, d_off, *, rows, chunk, ncol):
    n = rows // chunk
    n_in = len(ins)
    grp = _scan_group(ncol, [off for _, off in ins] + [d_off])

    def body(*refs):
        st_ref, do_ref = refs[n_in:n_in + 2]
        g_refs = refs[n_in + 2:2 * n_in + 2]
        ds_scr = refs[-1]

        @pl.when(pl.program_id(1) == 0)
        def _():
            ds_scr[...] = jnp.zeros_like(ds_scr)

        cols = [slice(b * LANES, (b + 1) * LANES) for b in range(grp)]

        def batch(ref):
            return jnp.stack([ref[:, c] for c in cols])

        _, pull = jax.vjp(chunk_fn, *[batch(r) for r in refs[:n_in]], st_ref[...])
        gs = pull((batch(do_ref), ds_scr[...]))
        for ref, g in zip(g_refs, gs[:n_in]):
            for b, c in enumerate(cols):
                ref[:, c] = g[b]
        ds_scr[...] = gs[n_in]

    def spec(off):
        return pl.BlockSpec((chunk, grp * LANES), lambda h, c: (n - 1 - c, h + off // grp))

    st_spec = pl.BlockSpec((grp, None, LANES, LANES), lambda h, c: (h, n - 1 - c, 0, 0))
    return pl.pallas_call(
        body, name=name, grid=(ncol // grp, n), in_specs=[spec(off) for _, off in ins] + [st_spec, spec(d_off)],
        out_specs=[spec(0)] * n_in, out_shape=[jax.ShapeDtypeStruct((rows, ncol * LANES), f32)] * n_in,
        scratch_shapes=[pltpu.VMEM((grp, LANES, LANES), f32)],
        compiler_params=_params(2))(*[a for a, _ in ins], states, d_out)


def flip_exchange(name, arrs, flips, n_slots, slot_of, src_of, with_self, after=()):
    n = len(arrs)
    nf = len(flips)
    n_after = len(after)

    def body(*refs):
        ins, outs = refs[:n], refs[n + n_after:2 * n + n_after]
        send, recv, lsem = refs[2 * n + n_after:]
        me = (lax.axis_index("x"), lax.axis_index("y"), lax.axis_index("c"))
        copies = []
        for k in range(n):
            if with_self:
                cp = pltpu.make_async_copy(src_of(ins[k], me), outs[k].at[slot_of(me)], lsem.at[k])
                cp.start()
                copies.append(cp)
            for j, fl in enumerate(flips):
                peer = tuple(1 - m if f else m for m, f in zip(me, fl))
                cp = pltpu.make_async_remote_copy(
                    src_ref=src_of(ins[k], peer), dst_ref=outs[k].at[slot_of(me)], send_sem=send.at[k, j],
                    recv_sem=recv.at[k, j], device_id=peer, device_id_type=MESH)
                cp.start()
                copies.append(cp)
        for cp in copies:
            cp.wait()

    def out_sds(a):
        blk = src_of(jax.ShapeDtypeStruct(a.shape, a.dtype), None)
        return jax.ShapeDtypeStruct((n_slots,) + tuple(blk), a.dtype)

    any_spec = pl.BlockSpec(memory_space=pl.ANY)
    return pl.pallas_call(
        body, name=name, in_specs=[any_spec] * (n + n_after), out_specs=[any_spec] * n,
        out_shape=[out_sds(a) for a in arrs],
        scratch_shapes=[pltpu.SemaphoreType.DMA((n, nf)), pltpu.SemaphoreType.DMA((n, nf)),
                        pltpu.SemaphoreType.DMA((n,))],
        compiler_params=pltpu.CompilerParams(has_side_effects=True))(*arrs, *after)


_CHIP_FLIPS = ((1, 0, 0), (0, 1, 0), (1, 1, 0))
_ALL_FLIPS = ((0, 0, 1), (0, 1, 0), (0, 1, 1), (1, 0, 0), (1, 0, 1), (1, 1, 0), (1, 1, 1))


def _whole(ref, pos):
    return ref.shape if pos is None else ref


def _chip_block(ref, pos):
    return ref.shape[1:] if pos is None else ref.at[2 * pos[0] + pos[1]]


def _chip_slot(p):
    return 2 * p[0] + p[1]


def gather_chips(name, arrs):
    return flip_exchange(name, arrs, _CHIP_FLIPS, 4, _chip_slot, _whole, True)


def scatter_chips(name, arrs):
    return flip_exchange(name, arrs, _CHIP_FLIPS, 4, _chip_slot, _chip_block, True)


_HBM = pl.BlockSpec(memory_space=pltpu.HBM)
_SEM = pl.BlockSpec(memory_space=pltpu.SEMAPHORE)
_DATAFLOW = pltpu.SideEffectType.DATAFLOW_SIDE_EFFECTING


def _split_copies(mode, refs, n, send, recv):
    me = (lax.axis_index("x"), lax.axis_index("y"), lax.axis_index("c"))
    sib = (me[0], me[1], 1 - me[2])
    lands = refs[:n] if mode == "handover" else refs[n:2 * n]
    copies = []
    for k, land in enumerate(lands):
        half = land.shape[1] // 2
        mine = pl.ds(pl.multiple_of(me[2] * half, 16), half)
        for j, fl in enumerate(_CHIP_FLIPS):
            peer = tuple(1 - m if f else m for m, f in zip(me, fl))
            if mode == "gather":
                src, dst, to = refs[k], land.at[_chip_slot(me)], peer
            elif mode == "scatter":
                src, dst, to = refs[k].at[_chip_slot(peer)], land.at[_chip_slot(me)], peer
            elif mode == "gather_half":
                src, dst, to = refs[k].at[mine], land.at[_chip_slot(me), mine], peer
            else:
                src = dst = land.at[_chip_slot(peer), mine]
                to = sib
            q = k * len(_CHIP_FLIPS) + j
            copies.append(pltpu.make_async_remote_copy(src_ref=src, dst_ref=dst, send_sem=send.at[q],
                                                       recv_sem=recv.at[q], device_id=to, device_id_type=MESH))
    return copies


def split_start(name, mode, ops, n):
    ops = [pltpu.with_memory_space_constraint(a, pltpu.HBM) for a in ops]
    m = len(ops)

    def body(*refs):
        for cp in _split_copies(mode, refs[:m], n, refs[m], refs[m + 1]):
            cp.start()
        refs[-1][...] = jnp.zeros_like(refs[-1])

    sems = pltpu.SemaphoreType.DMA((n * len(_CHIP_FLIPS),))
    outs = pl.pallas_call(
        body, name=name, in_specs=[_HBM] * m,
        out_shape=(sems, sems, *[pltpu.HBM(a.shape, a.dtype) for a in ops], jax.ShapeDtypeStruct((8, LANES), f32)),
        out_specs=(_SEM, _SEM, *[_HBM] * m, pl.BlockSpec(memory_space=pltpu.VMEM)),
        input_output_aliases={i: 2 + i for i in range(m)},
        compiler_params=pltpu.CompilerParams(has_side_effects=_DATAFLOW))(*ops)
    return (outs[0], outs[1], list(outs[2:2 + m]), mode, n), outs[-1][0, 0]


def split_wait(name, state, after):
    send, recv, ops, mode, n = state
    m = len(ops)
    afters = list(after) if isinstance(after, (list, tuple)) else [after]

    def body(*refs):
        for cp in _split_copies(mode, refs[:m], n, refs[m], refs[m + 1]):
            cp.wait_send()
            cp.wait_recv()

    outs = pl.pallas_call(
        body, name=name, in_specs=[_HBM] * m + [_SEM, _SEM] + [pl.BlockSpec(memory_space=pl.ANY)] * len(afters),
        out_shape=tuple(pltpu.HBM(a.shape, a.dtype) for a in ops), out_specs=tuple([_HBM] * m),
        input_output_aliases={i: i for i in range(m)},
        compiler_params=pltpu.CompilerParams(has_side_effects=_DATAFLOW))(*ops, send, recv, *afters)
    return list(outs[m - n:])


def chips_start(name, arrs, src_of, halves=False):
    me = _chip_slot((lax.axis_index("x"), lax.axis_index("y")))
    lands = []
    for a in arrs:
        blk = tuple(src_of(jax.ShapeDtypeStruct(a.shape, a.dtype), None))
        own = a if src_of is _whole else lax.dynamic_index_in_dim(a, me, 0, keepdims=False)
        lands.append(lax.dynamic_update_index_in_dim(lax.empty((4,) + blk, a.dtype), own, me, 0))
    mode = "scatter" if src_of is _chip_block else ("gather_half" if halves else "gather")
    return split_start(name, mode, list(arrs) + lands, len(arrs))


chips_wait = split_wait


def handover_start(name, lands):
    return split_start(name, "handover", lands, len(lands))


def gather_chips_halves(name, arrs):
    n = len(arrs)
    nf = len(_CHIP_FLIPS)
    split = [a.shape[0] % 32 == 0 for a in arrs]

    def body(*refs):
        ins, outs = refs[:n], refs[n:2 * n]
        send1, recv1, send2, recv2, lsem = refs[2 * n:]
        me = (lax.axis_index("x"), lax.axis_index("y"), lax.axis_index("c"))
        sib = (me[0], me[1], 1 - me[2])
        peers = [tuple(1 - m if f else m for m, f in zip(me, fl)) for fl in _CHIP_FLIPS]
        local, first, second = [], [], []
        for k in range(n):
            cp = pltpu.make_async_copy(ins[k], outs[k].at[_chip_slot(me)], lsem.at[k])
            cp.start()
            local.append(cp)
            half = ins[k].shape[0] // 2
            rows = pl.ds(pl.multiple_of(me[2] * half, 16), half) if split[k] else pl.ds(0, ins[k].shape[0])
            for j, peer in enumerate(peers):
                cp = pltpu.make_async_remote_copy(
                    src_ref=ins[k].at[rows], dst_ref=outs[k].at[_chip_slot(me), rows], send_sem=send1.at[k, j],
                    recv_sem=recv1.at[k, j], device_id=peer, device_id_type=MESH)
                cp.start()
                first.append((k, j, rows, cp))
        for k, j, rows, cp in first:
            cp.wait_recv()
            if split[k]:
                got = outs[k].at[_chip_slot(peers[j]), rows]
                fwd = pltpu.make_async_remote_copy(src_ref=got, dst_ref=got, send_sem=send2.at[k, j],
                                                   recv_sem=recv2.at[k, j], device_id=sib, device_id_type=MESH)
                fwd.start()
                second.append(fwd)
        for _, _, _, cp in first:
            cp.wait_send()
        for cp in second:
            cp.wait()
        for cp in local:
            cp.wait()

    any_spec = pl.BlockSpec(memory_space=pl.ANY)
    sems = pltpu.SemaphoreType.DMA((n, nf))
    return pl.pallas_call(
        body, name=name, in_specs=[any_spec] * n, out_specs=[any_spec] * n,
        out_shape=[jax.ShapeDtypeStruct((4,) + a.shape, a.dtype) for a in arrs],
        scratch_shapes=[sems, sems, sems, sems, pltpu.SemaphoreType.DMA((n,))],
        compiler_params=pltpu.CompilerParams(has_side_effects=True))(*arrs)


def swap_sibling(name, arrs):
    outs = flip_exchange(name, arrs, ((0, 0, 1),), 1, lambda p: 0, _whole, False)
    return [o[0] for o in outs]


def gather_all(arrs, after=()):
    return flip_exchange("gather_all", arrs, _ALL_FLIPS, 8, lambda p: 4 * p[0] + 2 * p[1] + p[2], _whole, True,
                         after=after)


def _row_tile(nr, nc, n_arrays):
    budget = (20 << 20) // (n_arrays * 2 * 4 * max(nc, LANES))
    t = min(nr, budget) // 16 * 16
    while t > 0 and nr % t:
        t -= 16
    return t if t > 0 else nr


def sum_slots(name, x):
    ns, nr, nc = x.shape
    tile = _row_tile(nr, nc, ns + 1)

    def body(x_ref, o_ref):
        s = x_ref[0].astype(f32)
        for q in range(1, ns):
            s = s + x_ref[q].astype(f32)
        o_ref[...] = s

    return pl.pallas_call(
        body, name=name, grid=(nr // tile,), in_specs=[pl.BlockSpec((ns, tile, nc), lambda i: (0, i, 0))],
        out_specs=pl.BlockSpec((tile, nc), lambda i: (i, 0)), out_shape=jax.ShapeDtypeStruct((nr, nc), f32),
        compiler_params=_params(1))(x)


def adamw(name, w, g_parts, m, v):
    nr, nc = w.shape[-2:]
    n_g = len(g_parts)
    tile = _row_tile(nr, nc, 7 + n_g)

    def body(*refs):
        w_ref, m_ref, v_ref = refs[:3]
        g = refs[3][...]
        for r in refs[4:3 + n_g]:
            g = g + r[...]
        g_ref, d_ref, nm_ref, nv_ref = refs[3 + n_g:]
        nm = ADAM_B1 * m_ref[...] + (1.0 - ADAM_B1) * g
        nv = ADAM_B2 * v_ref[...] + (1.0 - ADAM_B2) * jnp.square(g)
        m_hat = nm / (1.0 - ADAM_B1 ** ADAM_STEP)
        v_hat = nv / (1.0 - ADAM_B2 ** ADAM_STEP)
        g_ref[...] = g
        d_ref[...] = -ADAM_LR * (m_hat / (jnp.sqrt(v_hat) + ADAM_EPS) + ADAM_WD * w_ref[...])
        nm_ref[...] = nm
        nv_ref[...] = nv

    spec = pl.BlockSpec((tile, nc), lambda i: (i, 0))
    spec3 = pl.BlockSpec((None, tile, nc), lambda i: (0, i, 0)) if w.ndim == 3 else spec
    return pl.pallas_call(
        body, name=name, grid=(nr // tile,), in_specs=[spec3] * 3 + [spec] * n_g, out_specs=[spec3] * 4,
        out_shape=[jax.ShapeDtypeStruct(w.shape, f32)] * 4, compiler_params=_params(1))(w, m, v, *g_parts)


def adamw_packed(w, g8, m, v):
    nr, nc = w.shape

    def body(w_ref, g_ref, m_ref, v_ref, go_ref, d_ref, nm_ref, nv_ref):
        g = g_ref[0]
        for q in range(1, 8):
            g = g + g_ref[q]
        nm = ADAM_B1 * m_ref[...] + (1.0 - ADAM_B1) * g
        nv = ADAM_B2 * v_ref[...] + (1.0 - ADAM_B2) * jnp.square(g)
        m_hat = nm / (1.0 - ADAM_B1 ** ADAM_STEP)
        v_hat = nv / (1.0 - ADAM_B2 ** ADAM_STEP)
        go_ref[...] = g
        d_ref[...] = -ADAM_LR * (m_hat / (jnp.sqrt(v_hat) + ADAM_EPS) + ADAM_WD * w_ref[...])
        nm_ref[...] = nm
        nv_ref[...] = nv

    return pl.pallas_call(body, name="adamw_packed", out_shape=[jax.ShapeDtypeStruct((nr, nc), f32)] * 4,
                          compiler_params=pltpu.CompilerParams(vmem_limit_bytes=VMEM_LIMIT))(w, g8, m, v)


def _pack(vectors):
    rows = []
    for a in vectors:
        flat = a.reshape(-1).astype(f32)
        pad = (-flat.shape[0]) % LANES
        rows.append(jnp.pad(flat, (0, pad)).reshape(-1, LANES))
    packed = jnp.concatenate(rows, axis=0)
    return jnp.pad(packed, ((0, (-packed.shape[0]) % 8), (0, 0)))


def _unpack(packed, like):
    out, r = [], 0
    for a in like:
        n = a.size
        nr = -(-n // LANES)
        out.append(packed[r:r + nr].reshape(-1)[:n].reshape(a.shape))
        r += nr
    return out


def kernel(x, mem, mix_norm_w, w_in, dn_conv_w, dn_a_log, dn_dt_bias, dn_norm_w, rw_mu, rw_w0, rw_w2, rw_a0, rw_a2, rw_g2, rw_k_k, rw_k_a, rw_r_k, rw_ln_w, rw_ln_b, w_out, xa_norm_w, mem_norm_w, xa_wq, xa_wk, xa_wv, xa_wo, ffn_norm_w, ffn_w1, ffn_w2, final_norm_w, loss_target, m_mix_norm_w, m_w_in, m_dn_conv_w, m_dn_a_log, m_dn_dt_bias, m_dn_norm_w, m_rw_mu, m_rw_w0, m_rw_w2, m_rw_a0, m_rw_a2, m_rw_g2, m_rw_k_k, m_rw_k_a, m_rw_r_k, m_rw_ln_w, m_rw_ln_b, m_w_out, m_xa_norm_w, m_mem_norm_w, m_xa_wq, m_xa_wk, m_xa_wv, m_xa_wo, m_ffn_norm_w, m_ffn_w1, m_ffn_w2, m_final_norm_w, v_mix_norm_w, v_w_in, v_dn_conv_w, v_dn_a_log, v_dn_dt_bias, v_dn_norm_w, v_rw_mu, v_rw_w0, v_rw_w2, v_rw_a0, v_rw_a2, v_rw_g2, v_rw_k_k, v_rw_k_a, v_rw_r_k, v_rw_ln_w, v_rw_ln_b, v_w_out, v_xa_norm_w, v_mem_norm_w, v_xa_wq, v_xa_wk, v_xa_wv, v_xa_wo, v_ffn_norm_w, v_ffn_w1, v_ffn_w2, v_final_norm_w):
    weights = dict(mix_norm_w=mix_norm_w, w_in=w_in, dn_conv_w=dn_conv_w, dn_a_log=dn_a_log, dn_dt_bias=dn_dt_bias, dn_norm_w=dn_norm_w, rw_mu=rw_mu, rw_w0=rw_w0, rw_w2=rw_w2, rw_a0=rw_a0, rw_a2=rw_a2, rw_g2=rw_g2, rw_k_k=rw_k_k, rw_k_a=rw_k_a, rw_r_k=rw_r_k, rw_ln_w=rw_ln_w, rw_ln_b=rw_ln_b, w_out=w_out, xa_norm_w=xa_norm_w, mem_norm_w=mem_norm_w, xa_wq=xa_wq, xa_wk=xa_wk, xa_wv=xa_wv, xa_wo=xa_wo, ffn_norm_w=ffn_norm_w, ffn_w1=ffn_w1, ffn_w2=ffn_w2, final_norm_w=final_norm_w)
    mom_m = dict(mix_norm_w=m_mix_norm_w, w_in=m_w_in, dn_conv_w=m_dn_conv_w, dn_a_log=m_dn_a_log, dn_dt_bias=m_dn_dt_bias, dn_norm_w=m_dn_norm_w, rw_mu=m_rw_mu, rw_w0=m_rw_w0, rw_w2=m_rw_w2, rw_a0=m_rw_a0, rw_a2=m_rw_a2, rw_g2=m_rw_g2, rw_k_k=m_rw_k_k, rw_k_a=m_rw_k_a, rw_r_k=m_rw_r_k, rw_ln_w=m_rw_ln_w, rw_ln_b=m_rw_ln_b, w_out=m_w_out, xa_norm_w=m_xa_norm_w, mem_norm_w=m_mem_norm_w, xa_wq=m_xa_wq, xa_wk=m_xa_wk, xa_wv=m_xa_wv, xa_wo=m_xa_wo, ffn_norm_w=m_ffn_norm_w, ffn_w1=m_ffn_w1, ffn_w2=m_ffn_w2, final_norm_w=m_final_norm_w)
    mom_v = dict(mix_norm_w=v_mix_norm_w, w_in=v_w_in, dn_conv_w=v_dn_conv_w, dn_a_log=v_dn_a_log, dn_dt_bias=v_dn_dt_bias, dn_norm_w=v_dn_norm_w, rw_mu=v_rw_mu, rw_w0=v_rw_w0, rw_w2=v_rw_w2, rw_a0=v_rw_a0, rw_a2=v_rw_a2, rw_g2=v_rw_g2, rw_k_k=v_rw_k_k, rw_k_a=v_rw_k_a, rw_r_k=v_rw_r_k, rw_ln_w=v_rw_ln_w, rw_ln_b=v_rw_ln_b, w_out=v_w_out, xa_norm_w=v_xa_norm_w, mem_norm_w=v_mem_norm_w, xa_wq=v_xa_wq, xa_wk=v_xa_wk, xa_wv=v_xa_wv, xa_wo=v_xa_wo, ffn_norm_w=v_ffn_norm_w, ffn_w1=v_ffn_w1, ffn_w2=v_ffn_w2, final_norm_w=v_final_norm_w)
    names = list(weights)

    seq, d = x.shape[1], x.shape[2]
    dnw = d // 2
    rww = d - dnw
    nh, nb = dnw // LANES, rww // LANES
    n_mem = mem.shape[1]
    lw_dim, la_dim, lg_dim = rw_w2.shape[1], rw_a2.shape[1], rw_g2.shape[1]
    assert lw_dim + la_dim == LANES and lg_dim == LANES and dnw % LANES == 0 and rww % LANES == 0
    xs, mems, tgt = x[0], mem[0], loss_target[0]

    col_sharded = ("w_in", "xa_wo", "ffn_w1", "dn_conv_w", "rw_w2", "rw_a2", "rw_g2")
    row_sharded = ("w_out", "xa_wq", "xa_wk", "xa_wv", "ffn_w2")
    f32_payload = ("dn_conv_w", "rw_w2", "rw_a2", "rw_g2")
    sharded = col_sharded + row_sharded
    payload = {n: weights[n][0].astype(f32 if n in f32_payload else bf16) for n in sharded}
    shard_w = w_in.shape[2]
    pad_w = -(-shard_w // LANES) * LANES
    payload["w_in"] = jnp.pad(payload["w_in"], ((0, 0), (0, pad_w - shard_w)))
    first = ("w_in", "dn_conv_w", "rw_w2", "rw_a2", "rw_g2")
    mid = ("w_out", "xa_wq", "xa_wk", "xa_wv", "xa_wo")
    late = ("ffn_w1", "ffn_w2")
    gathered = dict(zip(first, gather_chips_halves("gather_first", [payload[n] for n in first])))
    ordered = lax.optimization_barrier(([gathered[n] for n in first], [payload[n] for n in mid + late]))
    gathered = dict(zip(first, ordered[0]))
    payload.update(zip(mid + late, ordered[1]))
    mid_state, tok_mid = chips_start("gather_mid_start", [payload[n] for n in mid], _whole, halves=True)
    late_state, tok_late = chips_start("gather_late_start", [payload[n] for n in late], _whole, halves=True)
    mix_norm_w_in = mix_norm_w + (tok_mid + tok_late)

    def full(n):
        g = gathered[n]
        if n in col_sharded:
            return g.transpose(1, 0, 2).reshape(g.shape[1], 4 * g.shape[2])
        return g.reshape(4 * g.shape[1], g.shape[2])

    c_rw0 = 4 * dnw + 2 * nh
    c_rw = 3 * rww
    eb_ab, eb_r = 4 * nh, 4 * nh + 1
    eb_l1 = eb_r + 3 * nb
    n_ext = -(-(eb_l1 + 2) // 4) * 4
    tbl_fwd, tbl_bwd, per_shard = w_in_layout(4 * shard_w, shard_w, c_rw0, LANES - 2 * nh, n_ext)
    w_ext = lane_regroup(
        "w_in_regroup", gathered["w_in"], tbl_fwd,
        lambda p: pl.BlockSpec((None, d, LANES),
                               lambda i, j, t: (t[REGROUP_FIELDS * p, j], 0, t[REGROUP_FIELDS * p + 1, j])),
        pl.BlockSpec((d, LANES), lambda i, j, t: (0, j)), (d, n_ext * LANES), (1, n_ext), pad_w)
    conv_w = full("dn_conv_w")
    w2p = jnp.concatenate([full("rw_w2"), jnp.zeros((la_dim, rww), f32)], axis=0)
    a2p = jnp.concatenate([jnp.zeros((lw_dim, rww), f32), full("rw_a2")], axis=0)
    g2 = full("rw_g2")
    xaw = xa_wq.shape[2]
    nxh = xaw // LANES
    ffn = 4 * ffn_w1.shape[2]

    def lane_row(vec):
        return jnp.pad(vec.reshape(1, -1), ((0, 0), (0, LANES - vec.size)))

    alog_row, dtb_row = lane_row(dn_a_log), lane_row(dn_dt_bias)
    head_of_col = jnp.arange(dnw)[None, :] // LANES
    e_g = (jnp.arange(LANES)[:, None] == head_of_col).astype(f32)
    e_b = (jnp.arange(LANES)[:, None] == head_of_col + nh).astype(f32)
    mu_main, mu_small = rw_mu[:, :c_rw], rw_mu[:, c_rw:]
    r_k_row = rw_r_k.reshape(1, rww)
    fnw = final_norm_w.reshape(1, d)
    qb, kb_, vb, zb = 0, nh, 2 * nh, 3 * nh
    rb0 = eb_r
    cc = lambda w_, o_: ("constc", w_, o_)
    rc = lambda o_: ("rowc", LANES, o_)

    (u16,) = rowcall("norm_mix", fn_rms, [(xs, "row"), (mix_norm_w_in, "const")], [(seq, d, bf16, "row")], rows=seq)
    p_main = p_small = mm("in_proj", u16, w_ext)

    fn_gconv = make_fn_gconv(2 * nh)
    (qkv,) = rowcall("gdn_conv", fn_gconv, [(p_main, rc(0)), (conv_w, cc(LANES, 0))],
                     [(seq, 3 * dnw, f32, "rowc")], rows=seq, tile=seq, ncol=3 * nh)
    gate_ins = [(p_small, rc(eb_ab)), (alog_row, "const"), (dtb_row, "const"), (e_g, "const"), (e_b, "const")]
    g_b, beta_b = rowcall("gdn_gate", fn_ggate, gate_ins, [(seq, dnw, f32, "row")] * 2, rows=seq)
    gdn_ins = [(qkv, qb), (qkv, kb_), (qkv, vb), (g_b, 0), (beta_b, 0)]
    o_raw, gdn_states = scan_fwd("gdn_scan", gdn_chunk, gdn_ins, rows=seq, chunk=GDN_CHUNK, ncol=nh)
    mid_state, tok = handover_start("gather_mid_pass", chips_wait("gather_mid_wait", mid_state, o_raw))
    gpost_ins = [(o_raw, rc(0)), (p_main, rc(zb)), (dn_norm_w + tok, "const")]
    (o_dn,) = rowcall("gdn_post", fn_gpost, gpost_ins, [(seq, dnw, bf16, "rowc")], rows=seq, ncol=nh)

    (prw,) = rowcall("rw_lerp_main", fn_lerp, [(p_main, rc(rb0)), (mu_main, cc(LANES, 0))],
                     [(seq, c_rw, f32, "rowc")], rows=seq, tile=seq, ncol=3 * nb)
    (psl,) = rowcall("rw_lerp_small", fn_lerp, [(p_small, rc(eb_l1)), (mu_small, cc(LANES, 0))],
                     [(seq, 2 * LANES, f32, "rowc")], rows=seq, tile=seq, ncol=2)
    rprep_ins = [(prw, rc(nb)), (psl, "row"), (rw_w0, cc(LANES, 0)), (rw_a0, cc(LANES, 0)), (rw_k_k, cc(LANES, 0)),
                 (rw_k_a, cc(LANES, 0)), (w2p, cc(LANES, 0)), (a2p, cc(LANES, 0)), (g2, cc(LANES, 0))]
    lw, kmod, kk, a_rw, gate = rowcall("rw_prep", fn_rprep, rprep_ins, [(seq, rww, f32, "rowc")] * 5,
                                        rows=seq, ncol=nb)
    wkv_ins = [(prw, 0), (lw, 0), (kmod, 0), (prw, 2 * nb), (kk, 0), (a_rw, 0)]
    y_rw, wkv_states = scan_fwd("wkv_scan", wkv_chunk, wkv_ins, rows=seq, chunk=WKV_CHUNK, ncol=nb)
    rpost_ins = [(y_rw, rc(0)), (prw, rc(0)), (kmod, rc(0)), (prw, rc(2 * nb)), (gate, rc(0)),
                 (r_k_row, cc(LANES, 0)), (rw_ln_w, cc(LANES, 0)), (rw_ln_b, cc(LANES, 0))]
    (o_rw,) = rowcall("rw_post", fn_rpost, rpost_ins, [(seq, rww, bf16, "rowc")], rows=seq, ncol=nb)

    o_cat = jnp.concatenate([o_dn, o_rw], axis=1)
    late_state, tok = handover_start("gather_late_pass", chips_wait("gather_late_wait", late_state, o_cat))
    gathered.update(zip(mid, chips_wait("gather_mid_pass_wait", mid_state, o_cat)))
    w_out_f, wq_f, wk_f, wv_f, wo_f = full("w_out"), full("xa_wq"), full("xa_wk"), full("xa_wv"), full("xa_wo")
    h1 = mm("out_proj", o_cat, w_out_f, add=xs)

    (hn16,) = rowcall("norm_xa", fn_rms, [(h1, "row"), (xa_norm_w + tok, "const")], [(seq, d, bf16, "row")],
                      rows=seq)
    (mn16,) = rowcall("norm_mem", fn_rms, [(mems, "row"), (mem_norm_w, "const")], [(n_mem, d, bf16, "row")],
                      rows=n_mem)
    q_xa = mm("xa_q", hn16, wq_f)
    k_xa = mm("xa_k", mn16, wk_f)
    v_xa = mm("xa_v", mn16, wv_f)
    xcore_ins = [(q_xa, rc(0)), (k_xa, cc(LANES, 0)), (v_xa, cc(LANES, 0))]
    (o_xa,) = rowcall("xa_core", fn_xcore, xcore_ins, [(seq, xaw, bf16, "rowc")], rows=seq, ncol=nxh)
    h2 = mm("xa_o", o_xa, wo_f, add=h1)

    (fn16,) = rowcall("norm_ffn", fn_rms, [(h2, "row"), (ffn_norm_w, "const")], [(seq, d, bf16, "row")], rows=seq)
    gathered.update(zip(late, chips_wait("gather_late_pass_wait", late_state, fn16)))
    w1_f, w2_f = full("ffn_w1"), full("ffn_w2")
    a1_16, hid16 = mm("ffn_up", fn16, w1_f, epilogue=lambda r: (r, jnp.square(jnp.maximum(r, 0.0))),
                      out_dtypes=(bf16, bf16))
    h3 = mm("ffn_down", hid16, w2_f, add=h2)

    dh3, dh3_16, d_fnw, loss_rows = rowcall(
        "loss_head", fn_final, [(h3, "row"), (tgt, "row"), (fnw, "const")],
        [(seq, d, f32, "row"), (seq, d, bf16, "row"), (1, d, f32, "acc"), (8, LANES, f32, "acc")], rows=seq)

    da1_16 = mm("ffn_down_dx", dh3_16, w2_f, tb=True, extra=[a1_16], out_dtype=bf16,
                epilogue=lambda r, a1: (r * (2.0 * jnp.maximum(a1.astype(f32), 0.0)),))
    def by_chip(n, g):
        if g.ndim == 3:
            return g
        if n in col_sharded:
            return g.reshape(g.shape[0], 4, g.shape[1] // 4).transpose(1, 0, 2)
        return g.reshape(4, g.shape[0] // 4, g.shape[1])

    g_ffn_w2 = mm("ffn_down_dw", hid16, dh3_16, ta=True, out_dtype=bf16)
    ffn_w2_g, tok = chips_start("scatter_ffn_w2_start", [by_chip("ffn_w2", g_ffn_w2)], _chip_block)
    g_ffn_w1 = mm("ffn_up_dw", fn16, da1_16, ta=True, out_dtype=bf16, by_chip=True, after=tok)
    ffn_w1_g, tok = chips_start("scatter_ffn_w1_start", [g_ffn_w1], _chip_block)
    dfn = mm("ffn_up_dx", da1_16, w1_f, tb=True, after=tok)
    dh2, d_ffn_nw, dh2_16 = rowvjp("norm_ffn_bwd", fn_rms, [(h2, "row"), (ffn_norm_w, "const")],
                                   [[(dfn, "row")]], [0, 1], rows=seq, adds=[(0, dh3, "row")], dup16=[0])

    do_xa = mm("xa_o_dx", dh2_16, wo_f, tb=True)
    g_xa_wo = mm("xa_o_dw", o_xa, dh2_16, ta=True, out_dtype=bf16, by_chip=True)
    dq_xa, dk_xa, dv_xa, dq16 = rowvjp("xa_core_bwd", fn_xcore, xcore_ins, [[(do_xa, rc(0))]], [0, 1, 2],
                                       rows=seq, ncol=nxh, dup16=[0])
    g_xa_wq = mm("xa_q_dw", hn16, dq16, ta=True, out_dtype=bf16)
    dhn = mm("xa_q_dx", dq16, wq_f, tb=True)
    dh1, d_xa_nw, dh1_16 = rowvjp("norm_xa_bwd", fn_rms, [(h1, "row"), (xa_norm_w, "const")], [[(dhn, "row")]],
                                  [0, 1], rows=seq, adds=[(0, dh2, "row")], dup16=[0])
    dk16, dv16 = dk_xa.astype(bf16), dv_xa.astype(bf16)
    g_xa_wk = mm("xa_k_dw", mn16, dk16, ta=True, out_dtype=bf16)
    g_xa_wv = mm("xa_v_dw", mn16, dv16, ta=True, out_dtype=bf16)
    dmn = mm("xa_v_dx", dv16, wv_f, tb=True, add=mm("xa_k_dx", dk16, wk_f, tb=True))
    (d_mem_nw,) = rowvjp("norm_mem_bwd", fn_rms, [(mems, "row"), (mem_norm_w, "const")], [[(dmn, "row")]], [1],
                         rows=n_mem)

    g_w_out = mm("out_proj_dw", o_cat, dh1_16, ta=True, out_dtype=bf16)
    mid_grads = dict(w_out=g_w_out, xa_wq=g_xa_wq, xa_wk=g_xa_wk, xa_wv=g_xa_wv, xa_wo=g_xa_wo)
    mid_g, tok = chips_start("scatter_mid_start", [by_chip(n, mid_grads[n]) for n in mid], _chip_block)
    do_cat = mm("out_proj_dx", dh1_16, w_out_f, tb=True, after=tok)

    dy, dr_a, dkmod_a, dv_a, dgate, d_r_k, d_ln_w, d_ln_b = rowvjp(
        "rw_post_bwd", fn_rpost, rpost_ins, [[(do_cat, rc(nh))]], [0, 1, 2, 3, 4, 5, 6, 7], rows=seq, ncol=nb)
    dr_b, dlw, dkmod_b, dv_b, dkk, da_rw = scan_bwd("wkv_scan_bwd", wkv_chunk, wkv_ins, wkv_states, dy, 0,
                                                    rows=seq, chunk=WKV_CHUNK, ncol=nb)
    rprep_cts = [[(dlw, rc(0))], [(dkmod_a, rc(0)), (dkmod_b, rc(0))], [(dkk, rc(0))], [(da_rw, rc(0))],
                 [(dgate, rc(0))]]
    dpk, dpsl_parts, d_w0, d_a0, d_k_k, d_k_a, d_w2p, d_a2p, d_g2 = rowvjp(
        "rw_prep_bwd", fn_rprep, rprep_ins, rprep_cts, [0, 1, 2, 3, 4, 5, 6, 7, 8], rows=seq, ncol=nb)
    (dpsl,) = rowcall("rw_prep_sum", fn_sumcols(nb), [(dpsl_parts, "row")], [(seq, 2 * LANES, f32, "row")], rows=seq)

    def lerp_bwd(tag, p, p_off, mu, mu_off, ct_lists, ncol):
        return rowvjp("rw_lerp_bwd_" + tag, fn_lerp, [(p, rc(p_off)), (mu, cc(LANES, mu_off))], [ct_lists], [0, 1],
                      rows=seq, tile=seq, ncol=ncol, dup16=[0])

    _, dmu_r, dpr16 = lerp_bwd("r", p_main, rb0, mu_main, 0, [(dr_a, rc(0)), (dr_b, rc(0))], nb)
    _, dmu_k, dpk16 = lerp_bwd("k", p_main, rb0 + nb, mu_main, nb, [(dpk, rc(0))], nb)
    _, dmu_v, dpv16 = lerp_bwd("v", p_main, rb0 + 2 * nb, mu_main, 2 * nb, [(dv_a, rc(0)), (dv_b, rc(0))], nb)
    _, dmu_s, dps12_16 = lerp_bwd("small", p_small, eb_l1, mu_small, 0, [(dpsl, rc(0))], 2)

    do_raw, dz, d_dn_nw, dz16 = rowvjp("gdn_post_bwd", fn_gpost, gpost_ins, [[(do_cat, rc(0))]], [0, 1, 2],
                                       rows=seq, ncol=nh, dup16=[1])
    dq_g, dk_g, dv_g, dg_b, dbeta_b = scan_bwd("gdn_scan_bwd", gdn_chunk, gdn_ins, gdn_states, do_raw, 0,
                                               rows=seq, chunk=GDN_CHUNK, ncol=nh)
    dps0, d_alog, d_dtb, dps0_16 = rowvjp("gdn_gate_bwd", fn_ggate, gate_ins, [[(dg_b, "row")], [(dbeta_b, "row")]],
                                          [0, 1, 2], rows=seq, dup16=[0])
    dqkv = jnp.concatenate([dq_g, dk_g, dv_g], axis=1)
    _, d_conv_w, dqkv16 = rowvjp("gdn_conv_bwd", fn_gconv, [(p_main, rc(0)), (conv_w, cc(LANES, 0))],
                                 [[(dqkv, rc(0))]], [0, 1], rows=seq, tile=seq, ncol=3 * nh, dup16=[0])

    dp16 = jnp.concatenate([dqkv16, dz16, dps0_16, dpr16, dpk16, dpv16, dps12_16,
                            jnp.zeros((seq, (n_ext - eb_l1 - 2) * LANES), bf16)], axis=1)
    g_w_ext = mm("in_proj_dw", u16, dp16, ta=True, out_dtype=bf16)
    g_w_in = lane_regroup(
        "w_in_grad_regroup", g_w_ext, tbl_bwd,
        lambda p: pl.BlockSpec((d, LANES), lambda i, j, t: (0, t[REGROUP_FIELDS * p + 1, j])),
        pl.BlockSpec((None, d, LANES), lambda i, j, t: (j // per_shard, 0, j % per_shard)),
        (4, d, pad_w), (1, 4 * per_shard), n_ext * LANES)
    first_grads = dict(w_in=g_w_in, dn_conv_w=d_conv_w, rw_w2=d_w2p[:lw_dim], rw_a2=d_a2p[lw_dim:], rw_g2=d_g2)
    first_g, tok = chips_start("scatter_first_start", [by_chip(n, first_grads[n]) for n in first], _chip_block)

    du = mm("in_proj_dx", dp16, w_ext, tb=True, after=tok)
    grad_x, d_mix_nw = rowvjp("norm_mix_bwd", fn_rms, [(xs, "row"), (mix_norm_w, "const")], [[(du, "row")]],
                              [0, 1], rows=seq, adds=[(0, dh1, "row")])
    received = dict(zip(mid, chips_wait("scatter_mid_wait", mid_g, grad_x)))
    received["ffn_w2"] = chips_wait("scatter_ffn_w2_wait", ffn_w2_g, grad_x)[0]
    received["ffn_w1"] = chips_wait("scatter_ffn_w1_wait", ffn_w1_g, grad_x)[0]

    out = {}

    def reduce_and_update(tag, group):
        partial = [sum_slots("sum_chips_" + n, received[n]) for n in group]
        other = swap_sibling("swap_sibling_" + tag, partial)
        for n, p_mine, p_other in zip(group, partial, other):
            if weights[n].shape[2] % LANES:
                rows, cols = weights[n].shape[1:]
                lin = lambda a: jnp.swapaxes(a, 1, 2).reshape(-1, LANES)
                lin_g = lambda p: p.T[:cols].reshape(-1, LANES)
                res = adamw("adamw_" + n, lin(weights[n]), [lin_g(p_mine), lin_g(p_other)], lin(mom_m[n]),
                            lin(mom_v[n]))
                out[n] = [jnp.swapaxes(r.reshape(1, cols, rows), 1, 2) for r in res]
            else:
                out[n] = adamw("adamw_" + n, weights[n], [p_mine, p_other], mom_m[n], mom_v[n])

    reduce_and_update("rest", mid + late)
    received.update(zip(first, chips_wait("scatter_first_wait", first_g, [out[n][1] for n in mid + late])))
    reduce_and_update("first", first)

    small_names = [n for n in names if n not in sharded]
    small_local = dict(
        mix_norm_w=d_mix_nw, dn_a_log=d_alog[:, :nh], dn_dt_bias=d_dtb[:, :nh], dn_norm_w=d_dn_nw,
        rw_mu=jnp.concatenate([dmu_r, dmu_k, dmu_v, dmu_s], axis=1), rw_w0=d_w0, rw_a0=d_a0, rw_k_k=d_k_k,
        rw_k_a=d_k_a, rw_r_k=d_r_k, rw_ln_w=d_ln_w, rw_ln_b=d_ln_b, xa_norm_w=d_xa_nw, mem_norm_w=d_mem_nw,
        ffn_norm_w=d_ffn_nw, final_norm_w=d_fnw)
    loss_vec = jnp.where(jnp.arange(LANES) == 0, loss_rows[0], 0.0)
    (g8,) = gather_all([_pack([small_local[n] for n in small_names] + [loss_vec])], after=[out["w_in"][1]])

    packed_like = [weights[n] for n in small_names] + [loss_vec]
    zero = jnp.zeros((LANES,), f32)
    res = adamw_packed(_pack([weights[n] for n in small_names] + [zero]), g8,
                       _pack([mom_m[n] for n in small_names] + [zero]),
                       _pack([mom_v[n] for n in small_names] + [zero]))
    unpacked = [_unpack(r, packed_like) for r in res]
    for i, n in enumerate(small_names):
        out[n] = [u[i] for u in unpacked]
    loss = unpacked[0][-1][0]

    return (loss, grad_x.reshape(x.shape), *[out[n][0] for n in names], *[out[n][1] for n in names],
            *[out[n][2] for n in names], *[out[n][3] for n in names])
```

```python
import functools

import jax
import jax.numpy as jnp
from jax import lax
from jax.experimental import pallas as pl
from jax.experimental.pallas import tpu as pltpu

f32 = jnp.float32
bf16 = jnp.bfloat16
HI = lax.Precision.HIGHEST
MESH = pl.DeviceIdType.MESH

LANES = 128
VMEM_LIMIT = 56 << 20
TOK_TILE = 256
TOK_TILE_BLOCKED = 1024
MM_TILE = 1024
MM_TILE_K = 2048
GDN_CHUNK = 128
WKV_CHUNK = 64
SCAN_GROUP = 8
P_BULK = 1
P_INV = 1
P_RESID = 3
P_CUMSUM = 3
RMS_EPS = 1e-6
RW_GN_EPS = 64e-5
RW_HEAD = 64

ADAM_LR, ADAM_B1, ADAM_B2, ADAM_EPS, ADAM_WD, ADAM_STEP = 0.001, 0.9, 0.999, 1e-08, 0.01, 10


def _params(n_grid):
    return pltpu.CompilerParams(dimension_semantics=("arbitrary",) * n_grid, vmem_limit_bytes=VMEM_LIMIT)


_DIMS = {"nn": (((1,), (0,)), ((), ())), "nt": (((1,), (1,)), ((), ())), "tn": (((0,), (0,)), ((), ()))}
_DIMS_BATCHED = {"nn": (((2,), (1,)), ((0,), (0,))), "nt": (((2,), (2,)), ((0,), (0,))),
                 "tn": (((1,), (1,)), ((0,), (0,)))}


def _raw_dot(a, b, mode, passes):
    dims = (_DIMS if a.ndim == 2 else _DIMS_BATCHED)[mode]
    if passes == 6:
        return lax.dot_general(a.astype(f32), b.astype(f32), dims, precision=HI, preferred_element_type=f32)
    ah, bh = a.astype(bf16), b.astype(bf16)
    r = lax.dot_general(ah, bh, dims, preferred_element_type=f32)
    if passes == 3:
        al = (a - ah.astype(f32)).astype(bf16)
        bl = (b - bh.astype(f32)).astype(bf16)
        r = r + lax.dot_general(al, bh, dims, preferred_element_type=f32)
        r = r + lax.dot_general(ah, bl, dims, preferred_element_type=f32)
    return r


@functools.partial(jax.custom_vjp, nondiff_argnums=(2, 3))
def pdot(a, b, mode, passes):
    return _raw_dot(a, b, mode, passes)


def _pdot_bwd(mode, passes, res, g):
    a, b = res
    if mode == "nn":
        da, db = _raw_dot(g, b, "nt", passes), _raw_dot(a, g, "tn", passes)
    elif mode == "nt":
        da, db = _raw_dot(g, b, "nn", passes), _raw_dot(g, a, "tn", passes)
    else:
        da, db = _raw_dot(b, g, "nt", passes), _raw_dot(a, g, "nn", passes)
    return da.astype(a.dtype), db.astype(b.dtype)


pdot.defvjp(lambda a, b, mode, passes: (_raw_dot(a, b, mode, passes), (a, b)), _pdot_bwd)


def hdot(a, b):
    return pdot(a, b, "nn", 6)


def bdot(a, b):
    return pdot(a, b, "nn", 1)


def bdot_nt(a, b):
    return pdot(a, b, "nt", 1)


def _shift_rows(x, k):
    row = lax.broadcasted_iota(jnp.int32, x.shape, 0)
    return jnp.where(row < k, 0.0, pltpu.roll(x, k, axis=0))


def _unshift_rows(g, k):
    n = g.shape[0]
    row = lax.broadcasted_iota(jnp.int32, g.shape, 0)
    return jnp.where(row >= n - k, 0.0, pltpu.roll(g, n - k, axis=0))


@functools.partial(jax.custom_vjp, nondiff_argnums=(1,))
def tshift(x, k):
    return _shift_rows(x, k)


tshift.defvjp(lambda x, k: (_shift_rows(x, k), None), lambda k, _, g: (_unshift_rows(g, k),))


def rms(x, w):
    x = x.astype(f32)
    return x * lax.rsqrt(jnp.mean(x * x, axis=-1, keepdims=True) + RMS_EPS) * w


def softplus(x):
    return jnp.maximum(x, 0.0) + jnp.log(1.0 + jnp.exp(-jnp.abs(x)))


def seg2sum(x):
    lo = lax.broadcasted_iota(jnp.int32, x.shape, 1) < RW_HEAD
    s_lo = jnp.sum(jnp.where(lo, x, 0.0), axis=-1, keepdims=True)
    s_hi = jnp.sum(jnp.where(lo, 0.0, x), axis=-1, keepdims=True)
    return jnp.where(lo, s_lo, s_hi)


def _tile(n, pref):
    if n <= pref:
        return n
    t = pref
    while t >= LANES:
        if n % t == 0:
            return t
        t -= LANES
    return n


def mm(name, a, b, *, ta=False, tb=False, add=None, out_dtype=f32, by_chip=False, epilogue=None, extra=(),
       out_dtypes=None, after=None):
    (k, m) = a.shape if ta else a.shape[::-1]
    (n, kb) = b.shape if tb else b.shape[::-1]
    assert k == kb, (name, a.shape, b.shape)
    tm, tk = _tile(m, MM_TILE), _tile(k, MM_TILE_K)
    tn = _tile(n // 4, MM_TILE) if by_chip else _tile(n, MM_TILE)
    nk = k // tk
    dims = (((0,) if ta else (1,), (1,) if tb else (0,)), ((), ()))
    extra = list(extra) + ([] if add is None else [add])
    out_dtypes = [out_dtype] if out_dtypes is None else list(out_dtypes)
    n_extra, n_out = len(extra), len(out_dtypes)
    n_after = 0 if after is None else 1

    def body(*refs):
        a_ref, b_ref = refs[:2]
        x_refs = refs[2:2 + n_extra]
        o_refs = refs[2 + n_extra + n_after:2 + n_extra + n_after + n_out]
        part = lax.dot_general(a_ref[...].astype(bf16), b_ref[...].astype(bf16), dims, preferred_element_type=f32)

        def finish(r):
            xs = [x[...] for x in x_refs]
            if add is not None:
                r = r + xs.pop().astype(f32)
            outs = (r,) if epilogue is None else epilogue(r, *xs)
            for o_ref, o in zip(o_refs, outs):
                o_ref[...] = o.astype(o_ref.dtype)

        if nk == 1:
            finish(part)
            return
        acc = refs[-1]
        kk = pl.program_id(2)

        @pl.when(kk == 0)
        def _():
            acc[...] = part

        @pl.when(kk > 0)
        def _():
            acc[...] += part

        @pl.when(kk == nk - 1)
        def _():
            finish(acc[...])

    a_spec = pl.BlockSpec((tk, tm), lambda i, j, q: (q, i)) if ta else pl.BlockSpec((tm, tk), lambda i, j, q: (i, q))
    b_spec = pl.BlockSpec((tn, tk), lambda i, j, q: (j, q)) if tb else pl.BlockSpec((tk, tn), lambda i, j, q: (q, j))
    x_spec = pl.BlockSpec((tm, tn), lambda i, j, q: (i, j))
    if by_chip:
        per_chip = n // 4 // tn
        o_spec = pl.BlockSpec((None, tm, tn), lambda i, j, q: (j // per_chip, i, j % per_chip))
        o_shape = (4, m, n // 4)
    else:
        o_spec, o_shape = x_spec, (m, n)
    afters = [] if after is None else [jnp.reshape(after, (1, 1))]
    res = pl.pallas_call(
        body, name=name, grid=(m // tm, n // tn, nk),
        in_specs=[a_spec, b_spec] + [x_spec] * n_extra + [pl.BlockSpec(memory_space=pl.ANY)] * n_after,
        out_specs=[o_spec] * n_out, out_shape=[jax.ShapeDtypeStruct(o_shape, dt) for dt in out_dtypes],
        scratch_shapes=[pltpu.VMEM((tm, tn), f32)] if nk > 1 else [],
        compiler_params=_params(3))(a, b, *extra, *afters)
    return res[0] if n_out == 1 else res


REGROUP_PIECES = 4


REGROUP_FIELDS = 5


def _regroup_table(n_out, sources_of):
    import numpy as np
    tbl = np.zeros((REGROUP_FIELDS * REGROUP_PIECES, n_out), np.int32)
    for j in range(n_out):
        groups = sorted(sources_of(j).items())
        assert len(groups) <= REGROUP_PIECES, (j, len(groups))
        for p in range(REGROUP_PIECES):
            if p < len(groups):
                key, lanes = groups[p]
                shifts = {q - s for s, q in lanes}
                qs = sorted(q for _, q in lanes)
                assert len(shifts) == 1 and qs == list(range(qs[0], qs[-1] + 1)), (j, key)
                row = (key[0], key[1], shifts.pop(), qs[0], qs[-1] + 1)
            else:
                row = (0, 0, 0, 0, 0)
            tbl[REGROUP_FIELDS * p:REGROUP_FIELDS * (p + 1), j] = row
    return jnp.asarray(tbl)


def lane_regroup(name, src, table, src_spec, out_spec, out_shape, grid, src_width):
    rows = src.shape[-2]

    def body(tbl, *refs):
        o_ref, acc = refs[REGROUP_PIECES:]
        j = pl.program_id(1)
        acc[...] = jnp.zeros_like(acc)
        for p in range(REGROUP_PIECES):
            blk, shift, lo, hi = (tbl[REGROUP_FIELDS * p + f, j] for f in range(1, REGROUP_FIELDS))

            @pl.when(hi > lo)
            def _(p=p, blk=blk, shift=shift, lo=lo, hi=hi):
                lane = lax.broadcasted_iota(jnp.int32, (rows, LANES), 1)
                pi = lax.broadcasted_iota(jnp.int32, (LANES, LANES), 0)
                qi = lax.broadcasted_iota(jnp.int32, (LANES, LANES), 1)
                x = jnp.where(lane < src_width - blk * LANES, refs[p][...], jnp.zeros((), src.dtype))
                sel = jnp.logical_and(qi - pi == shift, jnp.logical_and(qi >= lo, qi < hi))
                acc[...] += jnp.dot(x, sel.astype(src.dtype), preferred_element_type=f32)

        o_ref[...] = acc[...].astype(o_ref.dtype)

    return pl.pallas_call(
        body, name=name, out_shape=jax.ShapeDtypeStruct(out_shape, src.dtype),
        grid_spec=pltpu.PrefetchScalarGridSpec(
            num_scalar_prefetch=1, grid=grid, in_specs=[src_spec(p) for p in range(REGROUP_PIECES)],
            out_specs=out_spec, scratch_shapes=[pltpu.VMEM((rows, LANES), f32)]),
        compiler_params=_params(2))(table, *[src] * REGROUP_PIECES)


def w_in_layout(d_cols, shard_w, c_split, gap, n_blocks):
    def ext_of(c):
        return c if c < c_split else c + gap

    def fwd_sources(j):
        groups = {}
        for q in range(LANES):
            e = j * LANES + q
            c = e if e < c_split else e - gap
            if (c_split <= e < c_split + gap) or c >= d_cols:
                continue
            s, l = divmod(c, shard_w)
            groups.setdefault((s, l // LANES), []).append((l % LANES, q))
        return groups

    per_shard = -(-shard_w // LANES)

    def bwd_sources(j):
        s, b = divmod(j, per_shard)
        groups = {}
        for q in range(LANES):
            l = b * LANES + q
            if l >= shard_w:
                continue
            e = ext_of(s * shard_w + l)
            groups.setdefault((0, e // LANES), []).append((e % LANES, q))
        return groups

    return _regroup_table(n_blocks, fwd_sources), _regroup_table(4 * per_shard, bwd_sources), per_shard


def _in_spec(a, kind, tile):
    if kind == "row":
        return pl.BlockSpec((tile, a.shape[1]), lambda j, i: (i, 0))
    if kind == "const":
        return pl.BlockSpec(a.shape, lambda j, i: (0, 0))
    tag, cw, off = kind
    if tag == "rowc":
        return pl.BlockSpec((tile, cw), lambda j, i: (i, j + off))
    assert tag == "constc", kind
    return pl.BlockSpec((a.shape[0], cw), lambda j, i: (0, j + off))


def rowcall(name, fn, ins, outs, *, rows, tile=None, ncol=1):
    if tile is None:
        tile = min(TOK_TILE_BLOCKED if ncol > 1 else TOK_TILE, rows)
    n_in = len(ins)
    kinds = [o[3] for o in outs]

    def body(*refs):
        j, i = pl.program_id(0), pl.program_id(1)
        res = fn(*[r[...] for r in refs[:n_in]])
        for ref, val, kind in zip(refs[n_in:], res, kinds):
            if kind in ("row", "rowc"):
                ref[...] = val.astype(ref.dtype)
            else:
                first = (i == 0) if kind == "accc" else jnp.logical_and(i == 0, j == 0)

                @pl.when(first)
                def _(ref=ref, val=val):
                    ref[...] = val.astype(ref.dtype)

                @pl.when(jnp.logical_not(first))
                def _(ref=ref, val=val):
                    ref[...] += val.astype(ref.dtype)

    out_shape, out_specs = [], []
    for nr, nc, dtype, kind in outs:
        out_shape.append(jax.ShapeDtypeStruct((nr, nc), dtype))
        if kind == "row":
            out_specs.append(pl.BlockSpec((tile, nc), lambda j, i: (i, 0)))
        elif kind == "rowc":
            out_specs.append(pl.BlockSpec((tile, nc // ncol), lambda j, i: (i, j)))
        elif kind == "acc":
            out_specs.append(pl.BlockSpec((nr, nc), lambda j, i: (0, 0)))
        else:
            out_specs.append(pl.BlockSpec((nr, nc // ncol), lambda j, i: (0, j)))
    return pl.pallas_call(
        body, name=name, grid=(ncol, rows // tile), in_specs=[_in_spec(a, k, tile) for a, k in ins],
        out_specs=out_specs, out_shape=out_shape, compiler_params=_params(2))(*[a for a, _ in ins])


def rowvjp(name, fn, ins, cts, grads, *, rows, tile=None, ncol=1, adds=(), dup16=()):
    n_in = len(ins)
    ct_sizes = [len(c) for c in cts]
    flat_cts = [m for c in cts for m in c]
    n_ct = len(flat_cts)

    def wrapped(*vals):
        xs = list(vals[:n_in])
        gs = vals[n_in:n_in + n_ct]
        extra = vals[n_in + n_ct:]

        def f(*dvars):
            full = list(xs)
            for k, v in zip(grads, dvars):
                full[k] = v
            return fn(*full)

        outs, pull = jax.vjp(f, *[xs[k] for k in grads])
        cot, p = [], 0
        for o, size in zip(outs, ct_sizes):
            g = gs[p].astype(f32)
            for q in range(1, size):
                g = g + gs[p + q].astype(f32)
            cot.append(g.astype(o.dtype))
            p += size
        gv = list(pull(tuple(cot)))
        for (pos, _, _), e in zip(adds, extra):
            gv[pos] = gv[pos] + e.astype(gv[pos].dtype)
        return tuple(gv) + tuple(gv[pos] for pos in dup16)

    outs = []
    for k in grads:
        a, kind = ins[k]
        if kind == "row":
            outs.append((rows, a.shape[1] * ncol, f32, "rowc") if ncol > 1 else (rows, a.shape[1], f32, "row"))
        elif kind == "const":
            outs.append((a.shape[0], a.shape[1], f32, "acc"))
        elif kind[0] == "rowc":
            outs.append((rows, kind[1] * ncol, f32, "rowc"))
        else:
            outs.append((a.shape[0], kind[1] * ncol, f32, "accc"))
    for pos in dup16:
        nr, nc, _, kind = outs[pos]
        outs.append((nr, nc, bf16, kind))
    all_ins = list(ins) + flat_cts + [(a, kind) for _, a, kind in adds]
    return rowcall(name, wrapped, all_ins, outs, rows=rows, tile=tile, ncol=ncol)


def fn_rms(x, w):
    return (rms(x, w),)


def make_fn_gconv(n_norm_blocks):
    def fn(p, cw):
        c = cw[3:4] * p
        for jj in range(3):
            c = c + cw[jj:jj + 1] * tshift(p, 3 - jj)
        s = c * jax.nn.sigmoid(c)
        nrm = s * lax.rsqrt(jnp.sum(s * s, axis=-1, keepdims=True) + 1e-6)
        return (jnp.where(pl.program_id(0) < n_norm_blocks, nrm, s),)
    return fn


def fn_ggate(ps0, alog, dtb, e_g, e_b):
    g = -jnp.exp(alog) * softplus(ps0 + dtb)
    beta = jax.nn.sigmoid(ps0)
    return hdot(g, e_g), hdot(beta, e_b)


def fn_gpost(o, z, nw):
    return (rms(o, nw) * (z * jax.nn.sigmoid(z)),)


def fn_lerp(p, mu):
    return (p + (tshift(p, 1) - p) * mu,)


def fn_rprep(pk, psl, w0, a0, k_k, k_a, w2p, a2p, g2):
    g1, g2in = psl[:, :LANES], psl[:, LANES:]
    log_w = -softplus(-(w0 + bdot(jnp.tanh(g1), w2p))) - 0.5
    lw = -jnp.exp(log_w)
    a = jax.nn.sigmoid(a0 + bdot(g1, a2p))
    gate = bdot(jax.nn.sigmoid(g2in), g2)
    kkr = pk * k_k
    kk = kkr / jnp.maximum(jnp.sqrt(seg2sum(kkr * kkr)), 1e-12)
    kmod = pk * (1.0 + (a - 1.0) * k_a)
    return lw, kmod, kk, a, gate


def fn_rpost(y, r, kmod, v, gate, r_k, ln_w, ln_b):
    inv_n = 1.0 / RW_HEAD
    mean = seg2sum(y) * inv_n
    d = y - mean
    var = seg2sum(d * d) * inv_n
    yn = d * lax.rsqrt(var + RW_GN_EPS) * ln_w + ln_b
    bonus = seg2sum(r * kmod * r_k) * v
    return ((yn + bonus) * gate,)


def fn_xcore(q, k, v):
    s = bdot_nt(q, k) * (LANES ** -0.5)
    p = jax.nn.softmax(s, axis=-1)
    return (bdot(p, v),)


def fn_final(h, tgt, w):
    def loss_fn(h, w):
        return 0.5 * jnp.sum(jnp.mean(jnp.square(rms(h, w) - tgt), axis=-1))

    val, (dh, dw) = jax.value_and_grad(loss_fn, argnums=(0, 1))(h, w)
    return dh, dh.astype(bf16), dw, jnp.full((8, LANES), val, f32)


def fn_sumcols(n):
    def fn(x):
        w = x.shape[1] // n
        s = x[:, :w]
        for q in range(1, n):
            s = s + x[:, q * w:(q + 1) * w]
        return (s,)
    return fn


def _tri(c):
    ii = lax.broadcasted_iota(jnp.int32, (c, c), 0)
    jj = lax.broadcasted_iota(jnp.int32, (c, c), 1)
    return ii, jj


def _neumann_raw(m, steps):
    c = m.shape[-1]
    ii, jj = _tri(c)
    eye = (ii == jj).astype(f32)
    t, p = eye + m, m
    for _ in range(steps):
        p = _raw_dot(p, p, "nn", P_INV)
        t = _raw_dot(t, eye + p, "nn", P_INV)
    resid = eye - t + _raw_dot(m, t, "nn", P_RESID)
    return t + _raw_dot(t, resid, "nn", P_INV)


@functools.partial(jax.custom_vjp, nondiff_argnums=(1,))
def _neumann_inverse(m, steps):
    return _neumann_raw(m, steps)


def _neumann_fwd(m, steps):
    t = _neumann_raw(m, steps)
    return t, t


def _neumann_bwd(steps, t, g):
    return (_raw_dot(_raw_dot(t, g, "tn", P_RESID), t, "nt", P_RESID),)


_neumann_inverse.defvjp(_neumann_fwd, _neumann_bwd)


def cumsum_rows(x):
    t = x.shape[1]
    ii, jj = _tri(t)
    tri = jnp.broadcast_to((ii >= jj).astype(f32), (x.shape[0], t, t))
    return pdot(tri, x, "nn", P_CUMSUM)


def gdn_chunk(q, k, v, gb, bb, s):
    c = q.shape[1]
    ii, jj = _tri(c)
    low = ii >= jj
    gcb = cumsum_rows(gb)
    gl = jnp.sum(gb, axis=1, keepdims=True)
    gc_col = gcb[:, :, :c]
    diff = gc_col - jnp.swapaxes(gc_col, 1, 2)
    decay = jnp.where(low, jnp.exp(jnp.where(low, diff, 0.0)), 0.0)
    qs = q * (q.shape[2] ** -0.5)
    kb = k * bb
    with_k = pdot(jnp.concatenate([kb, qs], axis=1), k, "nt", P_BULK)
    a = jnp.where(ii > jj, with_k[:, :c] * decay, 0.0)
    attn = with_k[:, c:] * decay
    t = _neumann_inverse(-a, c.bit_length() - 2)
    eg = jnp.exp(gcb)
    uw = pdot(t, jnp.concatenate([v * bb, kb * eg], axis=2), "nn", P_BULK)
    u, w = uw[:, :, :LANES], uw[:, :, LANES:]
    kd = k * jnp.exp(gl - gcb)
    from_state = pdot(jnp.concatenate([w, qs * eg], axis=1), s, "nn", P_BULK)
    v_new = u - from_state[:, :c]
    o = from_state[:, c:] + pdot(attn, v_new, "nn", P_BULK)
    s_new = s * jnp.exp(gl) + pdot(kd, v_new, "tn", P_BULK)
    return o, s_new


def wkv_chunk(r, lw, k, v, kk, a, s):
    t = r.shape[1]
    ii, jj = _tri(t)
    lo = lax.broadcasted_iota(jnp.int32, r.shape, 2) < RW_HEAD
    cl = cumsum_rows(lw)
    cl_last = jnp.sum(lw, axis=1, keepdims=True)
    al = -kk * jnp.exp(cl - lw)
    be = (a * kk) * jnp.exp(-cl)
    kt = k * jnp.exp(-cl)
    rt = r * jnp.exp(cl)

    def dot(xa, xb, mode="nn"):
        return pdot(xa, xb, mode, P_BULK)

    def sel(x_lo, x_hi):
        return jnp.where(lo, x_lo, x_hi)

    left = jnp.concatenate([jnp.where(lo, al, 0.0), jnp.where(lo, 0.0, al),
                            jnp.where(lo, rt, 0.0), jnp.where(lo, 0.0, rt)], axis=1)
    with_be, with_kt = dot(left, be, "nt"), dot(left, kt, "nt")

    def blocks(prod):
        below, upto = ii > jj, ii >= jj
        return [jnp.where(msk, prod[:, q * t:(q + 1) * t], 0.0) for q, msk in enumerate((below, below, upto, upto))]

    ab_lo, ab_hi, rb_lo, rb_hi = blocks(with_be)
    ak_lo, ak_hi, rk_lo, rk_hi = blocks(with_kt)
    from_state = dot(jnp.concatenate([al, rt], axis=1), s, "nt")
    x = from_state[:, :t] + sel(dot(ak_lo, v), dot(ak_hi, v))
    steps = t.bit_length() - 2
    u = sel(dot(_neumann_inverse(ab_lo, steps), x), dot(_neumann_inverse(ab_hi, steps), x))
    y = from_state[:, t:] + sel(dot(rb_lo, u) + dot(rk_lo, v), dot(rb_hi, u) + dot(rk_hi, v))
    vi = lax.broadcasted_iota(jnp.int32, s.shape, 1) < RW_HEAD
    ki = lax.broadcasted_iota(jnp.int32, s.shape, 2) < RW_HEAD
    s_new = jnp.where(vi == ki, (s + dot(u, be, "tn") + dot(v, kt, "tn")) * jnp.exp(cl_last), 0.0)
    return y, s_new


def _scan_group(ncol, offs):
    g = SCAN_GROUP
    while g > 1 and (ncol % g or any(o % g for o in offs)):
        g //= 2
    return g


def scan_fwd(name, chunk_fn, ins, *, rows, chunk, ncol):
    n = rows // chunk
    n_in = len(ins)
    grp = _scan_group(ncol, [off for _, off in ins])

    def body(*refs):
        o_ref, st_ref, s_scr = refs[n_in:]

        @pl.when(pl.program_id(1) == 0)
        def _():
            s_scr[...] = jnp.zeros_like(s_scr)

        cols = [slice(b * LANES, (b + 1) * LANES) for b in range(grp)]
        s = s_scr[...]
        st_ref[...] = s
        o, s_new = chunk_fn(*[jnp.stack([r[:, c] for c in cols]) for r in refs[:n_in]], s)
        for b, c in enumerate(cols):
            o_ref[:, c] = o[b]
        s_scr[...] = s_new

    def spec(off):
        return pl.BlockSpec((chunk, grp * LANES), lambda h, c: (c, h + off // grp))

    return pl.pallas_call(
        body, name=name, grid=(ncol // grp, n), in_specs=[spec(off) for _, off in ins],
        out_specs=[spec(0), pl.BlockSpec((grp, None, LANES, LANES), lambda h, c: (h, c, 0, 0))],
        out_shape=[jax.ShapeDtypeStruct((rows, ncol * LANES), f32),
                   jax.ShapeDtypeStruct((ncol, n, LANES, LANES), f32)],
        scratch_shapes=[pltpu.VMEM((grp, LANES, LANES), f32)], compiler_params=_params(2))(*[a for a, _ in ins])


def scan_bwd(name, chunk_fn, ins, states, d_out, d_off, *, rows, chunk, ncol):
    n = rows // chunk
    n_in = len(ins)
    grp = _scan_group(ncol, [off for _, off in ins] + [d_off])

    def body(*refs):
        st_ref, do_ref = refs[n_in:n_in + 2]
        g_refs = refs[n_in + 2:2 * n_in + 2]
        ds_scr = refs[-1]

        @pl.when(pl.program_id(1) == 0)
        def _():
            ds_scr[...] = jnp.zeros_like(ds_scr)

        cols = [slice(b * LANES, (b + 1) * LANES) for b in range(grp)]

        def batch(ref):
            return jnp.stack([ref[:, c] for c in cols])

        _, pull = jax.vjp(chunk_fn, *[batch(r) for r in refs[:n_in]], st_ref[...])
        gs = pull((batch(do_ref), ds_scr[...]))
        for ref, g in zip(g_refs, gs[:n_in]):
            for b, c in enumerate(cols):
                ref[:, c] = g[b]
        ds_scr[...] = gs[n_in]

    def spec(off):
        return pl.BlockSpec((chunk, grp * LANES), lambda h, c: (n - 1 - c, h + off // grp))

    st_spec = pl.BlockSpec((grp, None, LANES, LANES), lambda h, c: (h, n - 1 - c, 0, 0))
    return pl.pallas_call(
        body, name=name, grid=(ncol // grp, n), in_specs=[spec(off) for _, off in ins] + [st_spec, spec(d_off)],
        out_specs=[spec(0)] * n_in, out_shape=[jax.ShapeDtypeStruct((rows, ncol * LANES), f32)] * n_in,
        scratch_shapes=[pltpu.VMEM((grp, LANES, LANES), f32)],
        compiler_params=_params(2))(*[a for a, _ in ins], states, d_out)


def flip_exchange(name, arrs, flips, n_slots, slot_of, src_of, with_self, after=()):
    n = len(arrs)
    nf = len(flips)
    n_after = len(after)

    def body(*refs):
        ins, outs = refs[:n], refs[n + n_after:2 * n + n_after]
        send, recv, lsem = refs[2 * n + n_after:]
        me = (lax.axis_index("x"), lax.axis_index("y"), lax.axis_index("c"))
        copies = []
        for k in range(n):
            if with_self:
                cp = pltpu.make_async_copy(src_of(ins[k], me), outs[k].at[slot_of(me)], lsem.at[k])
                cp.start()
                copies.append(cp)
            for j, fl in enumerate(flips):
                peer = tuple(1 - m if f else m for m, f in zip(me, fl))
                cp = pltpu.make_async_remote_copy(
                    src_ref=src_of(ins[k], peer), dst_ref=outs[k].at[slot_of(me)], send_sem=send.at[k, j],
                    recv_sem=recv.at[k, j], device_id=peer, device_id_type=MESH)
                cp.start()
                copies.append(cp)
        for cp in copies:
            cp.wait()

    def out_sds(a):
        blk = src_of(jax.ShapeDtypeStruct(a.shape, a.dtype), None)
        return jax.ShapeDtypeStruct((n_slots,) + tuple(blk), a.dtype)

    any_spec = pl.BlockSpec(memory_space=pl.ANY)
    return pl.pallas_call(
        body, name=name, in_specs=[any_spec] * (n + n_after), out_specs=[any_spec] * n,
        out_shape=[out_sds(a) for a in arrs],
        scratch_shapes=[pltpu.SemaphoreType.DMA((n, nf)), pltpu.SemaphoreType.DMA((n, nf)),
                        pltpu.SemaphoreType.DMA((n,))],
        compiler_params=pltpu.CompilerParams(has_side_effects=True))(*arrs, *after)


_CHIP_FLIPS = ((1, 0, 0), (0, 1, 0), (1, 1, 0))
_ALL_FLIPS = ((0, 0, 1), (0, 1, 0), (0, 1, 1), (1, 0, 0), (1, 0, 1), (1, 1, 0), (1, 1, 1))


def _whole(ref, pos):
    return ref.shape if pos is None else ref


def _chip_block(ref, pos):
    return ref.shape[1:] if pos is None else ref.at[2 * pos[0] + pos[1]]


def _chip_slot(p):
    return 2 * p[0] + p[1]


def gather_chips(name, arrs):
    return flip_exchange(name, arrs, _CHIP_FLIPS, 4, _chip_slot, _whole, True)


def scatter_chips(name, arrs):
    return flip_exchange(name, arrs, _CHIP_FLIPS, 4, _chip_slot, _chip_block, True)


_HBM = pl.BlockSpec(memory_space=pltpu.HBM)
_SEM = pl.BlockSpec(memory_space=pltpu.SEMAPHORE)
_DATAFLOW = pltpu.SideEffectType.DATAFLOW_SIDE_EFFECTING


def _split_copies(mode, refs, n, send, recv):
    me = (lax.axis_index("x"), lax.axis_index("y"), lax.axis_index("c"))
    sib = (me[0], me[1], 1 - me[2])
    lands = refs[:n] if mode == "handover" else refs[n:2 * n]
    copies = []
    for k, land in enumerate(lands):
        half = land.shape[1] // 2
        mine = pl.ds(pl.multiple_of(me[2] * half, 16), half)
        for j, fl in enumerate(_CHIP_FLIPS):
            peer = tuple(1 - m if f else m for m, f in zip(me, fl))
            if mode == "gather":
                src, dst, to = refs[k], land.at[_chip_slot(me)], peer
            elif mode == "scatter":
                src, dst, to = refs[k].at[_chip_slot(peer)], land.at[_chip_slot(me)], peer
            elif mode == "gather_half":
                src, dst, to = refs[k].at[mine], land.at[_chip_slot(me), mine], peer
            else:
                src = dst = land.at[_chip_slot(peer), mine]
                to = sib
            q = k * len(_CHIP_FLIPS) + j
            copies.append(pltpu.make_async_remote_copy(src_ref=src, dst_ref=dst, send_sem=send.at[q],
                                                       recv_sem=recv.at[q], device_id=to, device_id_type=MESH))
    return copies


def split_start(name, mode, ops, n):
    ops = [pltpu.with_memory_space_constraint(a, pltpu.HBM) for a in ops]
    m = len(ops)

    def body(*refs):
        for cp in _split_copies(mode, refs[:m], n, refs[m], refs[m + 1]):
            cp.start()
        refs[-1][...] = jnp.zeros_like(refs[-1])

    sems = pltpu.SemaphoreType.DMA((n * len(_CHIP_FLIPS),))
    outs = pl.pallas_call(
        body, name=name, in_specs=[_HBM] * m,
        out_shape=(sems, sems, *[pltpu.HBM(a.shape, a.dtype) for a in ops], jax.ShapeDtypeStruct((8, LANES), f32)),
        out_specs=(_SEM, _SEM, *[_HBM] * m, pl.BlockSpec(memory_space=pltpu.VMEM)),
        input_output_aliases={i: 2 + i for i in range(m)},
        compiler_params=pltpu.CompilerParams(has_side_effects=_DATAFLOW))(*ops)
    return (outs[0], outs[1], list(outs[2:2 + m]), mode, n), outs[-1][0, 0]


def split_wait(name, state, after):
    send, recv, ops, mode, n = state
    m = len(ops)
    afters = list(after) if isinstance(after, (list, tuple)) else [after]

    def body(*refs):
        for cp in _split_copies(mode, refs[:m], n, refs[m], refs[m + 1]):
            cp.wait_send()
            cp.wait_recv()

    outs = pl.pallas_call(
        body, name=name, in_specs=[_HBM] * m + [_SEM, _SEM] + [pl.BlockSpec(memory_space=pl.ANY)] * len(afters),
        out_shape=tuple(pltpu.HBM(a.shape, a.dtype) for a in ops), out_specs=tuple([_HBM] * m),
        input_output_aliases={i: i for i in range(m)},
        compiler_params=pltpu.CompilerParams(has_side_effects=_DATAFLOW))(*ops, send, recv, *afters)
    return list(outs[m - n:])


def chips_start(name, arrs, src_of, halves=False):
    me = _chip_slot((lax.axis_index("x"), lax.axis_index("y")))
    lands = []
    for a in arrs:
        blk = tuple(src_of(jax.ShapeDtypeStruct(a.shape, a.dtype), None))
        own = a if src_of is _whole else lax.dynamic_index_in_dim(a, me, 0, keepdims=False)
        lands.append(lax.dynamic_update_index_in_dim(lax.empty((4,) + blk, a.dtype), own, me, 0))
    mode = "scatter" if src_of is _chip_block else ("gather_half" if halves else "gather")
    return split_start(name, mode, list(arrs) + lands, len(arrs))


chips_wait = split_wait


def handover_start(name, lands):
    return split_start(name, "handover", lands, len(lands))


def gather_chips_halves(name, arrs):
    n = len(arrs)
    nf = len(_CHIP_FLIPS)
    split = [a.shape[0] % 32 == 0 for a in arrs]

    def body(*refs):
        ins, outs = refs[:n], refs[n:2 * n]
        send1, recv1, send2, recv2, lsem = refs[2 * n:]
        me = (lax.axis_index("x"), lax.axis_index("y"), lax.axis_index("c"))
        sib = (me[0], me[1], 1 - me[2])
        peers = [tuple(1 - m if f else m for m, f in zip(me, fl)) for fl in _CHIP_FLIPS]
        local, first, second = [], [], []
        for k in range(n):
            cp = pltpu.make_async_copy(ins[k], outs[k].at[_chip_slot(me)], lsem.at[k])
            cp.start()
            local.append(cp)
            half = ins[k].shape[0] // 2
            rows = pl.ds(pl.multiple_of(me[2] * half, 16), half) if split[k] else pl.ds(0, ins[k].shape[0])
            for j, peer in enumerate(peers):
                cp = pltpu.make_async_remote_copy(
                    src_ref=ins[k].at[rows], dst_ref=outs[k].at[_chip_slot(me), rows], send_sem=send1.at[k, j],
                    recv_sem=recv1.at[k, j], device_id=peer, device_id_type=MESH)
                cp.start()
                first.append((k, j, rows, cp))
        for k, j, rows, cp in first:
            cp.wait_recv()
            if split[k]:
                got = outs[k].at[_chip_slot(peers[j]), rows]
                fwd = pltpu.make_async_remote_copy(src_ref=got, dst_ref=got, send_sem=send2.at[k, j],
                                                   recv_sem=recv2.at[k, j], device_id=sib, device_id_type=MESH)
                fwd.start()
                second.append(fwd)
        for _, _, _, cp in first:
            cp.wait_send()
        for cp in second:
            cp.wait()
        for cp in local:
            cp.wait()

    any_spec = pl.BlockSpec(memory_space=pl.ANY)
    sems = pltpu.SemaphoreType.DMA((n, nf))
    return pl.pallas_call(
        body, name=name, in_specs=[any_spec] * n, out_specs=[any_spec] * n,
        out_shape=[jax.ShapeDtypeStruct((4,) + a.shape, a.dtype) for a in arrs],
        scratch_shapes=[sems, sems, sems, sems, pltpu.SemaphoreType.DMA((n,))],
        compiler_params=pltpu.CompilerParams(has_side_effects=True))(*arrs)


def swap_sibling(name, arrs):
    outs = flip_exchange(name, arrs, ((0, 0, 1),), 1, lambda p: 0, _whole, False)
    return [o[0] for o in outs]


def gather_all(arrs, after=()):
    return flip_exchange("gather_all", arrs, _ALL_FLIPS, 8, lambda p: 4 * p[0] + 2 * p[1] + p[2], _whole, True,
                         after=after)


def _row_tile(nr, nc, n_arrays):
    budget = (20 << 20) // (n_arrays * 2 * 4 * max(nc, LANES))
    t = min(nr, budget) // 16 * 16
    while t > 0 and nr % t:
        t -= 16
    return t if t > 0 else nr


def sum_slots(name, x):
    ns, nr, nc = x.shape
    tile = _row_tile(nr, nc, ns + 1)

    def body(x_ref, o_ref):
        s = x_ref[0].astype(f32)
        for q in range(1, ns):
            s = s + x_ref[q].astype(f32)
        o_ref[...] = s

    return pl.pallas_call(
        body, name=name, grid=(nr // tile,), in_specs=[pl.BlockSpec((ns, tile, nc), lambda i: (0, i, 0))],
        out_specs=pl.BlockSpec((tile, nc), lambda i: (i, 0)), out_shape=jax.ShapeDtypeStruct((nr, nc), f32),
        compiler_params=_params(1))(x)


def adamw(name, w, g_parts, m, v):
    nr, nc = w.shape[-2:]
    n_g = len(g_parts)
    tile = _row_tile(nr, nc, 7 + n_g)

    def body(*refs):
        w_ref, m_ref, v_ref = refs[:3]
        g = refs[3][...]
        for r in refs[4:3 + n_g]:
            g = g + r[...]
        g_ref, d_ref, nm_ref, nv_ref = refs[3 + n_g:]
        nm = ADAM_B1 * m_ref[...] + (1.0 - ADAM_B1) * g
        nv = ADAM_B2 * v_ref[...] + (1.0 - ADAM_B2) * jnp.square(g)
        m_hat = nm / (1.0 - ADAM_B1 ** ADAM_STEP)
        v_hat = nv / (1.0 - ADAM_B2 ** ADAM_STEP)
        g_ref[...] = g
        d_ref[...] = -ADAM_LR * (m_hat / (jnp.sqrt(v_hat) + ADAM_EPS) + ADAM_WD * w_ref[...])
        nm_ref[...] = nm
        nv_ref[...] = nv

    spec = pl.BlockSpec((tile, nc), lambda i: (i, 0))
    spec3 = pl.BlockSpec((None, tile, nc), lambda i: (0, i, 0)) if w.ndim == 3 else spec
    return pl.pallas_call(
        body, name=name, grid=(nr // tile,), in_specs=[spec3] * 3 + [spec] * n_g, out_specs=[spec3] * 4,
        out_shape=[jax.ShapeDtypeStruct(w.shape, f32)] * 4, compiler_params=_params(1))(w, m, v, *g_parts)


def adamw_packed(w, g8, m, v):
    nr, nc = w.shape

    def body(w_ref, g_ref, m_ref, v_ref, go_ref, d_ref, nm_ref, nv_ref):
        g = g_ref[0]
        for q in range(1, 8):
            g = g + g_ref[q]
        nm = ADAM_B1 * m_ref[...] + (1.0 - ADAM_B1) * g
        nv = ADAM_B2 * v_ref[...] + (1.0 - ADAM_B2) * jnp.square(g)
        m_hat = nm / (1.0 - ADAM_B1 ** ADAM_STEP)
        v_hat = nv / (1.0 - ADAM_B2 ** ADAM_STEP)
        go_ref[...] = g
        d_ref[...] = -ADAM_LR * (m_hat / (jnp.sqrt(v_hat) + ADAM_EPS) + ADAM_WD * w_ref[...])
        nm_ref[...] = nm
        nv_ref[...] = nv

    return pl.pallas_call(body, name="adamw_packed", out_shape=[jax.ShapeDtypeStruct((nr, nc), f32)] * 4,
                          compiler_params=pltpu.CompilerParams(vmem_limit_bytes=VMEM_LIMIT))(w, g8, m, v)


def _pack(vectors):
    rows = []
    for a in vectors:
        flat = a.reshape(-1).astype(f32)
        pad = (-flat.shape[0]) % LANES
        rows.append(jnp.pad(flat, (0, pad)).reshape(-1, LANES))
    packed = jnp.concatenate(rows, axis=0)
    return jnp.pad(packed, ((0, (-packed.shape[0]) % 8), (0, 0)))


def _unpack(packed, like):
    out, r = [], 0
    for a in like:
        n = a.size
        nr = -(-n // LANES)
        out.append(packed[r:r + nr].reshape(-1)[:n].reshape(a.shape))
        r += nr
    return out


def kernel(x, mem, mix_norm_w, w_in, dn_conv_w, dn_a_log, dn_dt_bias, dn_norm_w, rw_mu, rw_w0, rw_w2, rw_a0, rw_a2, rw_g2, rw_k_k, rw_k_a, rw_r_k, rw_ln_w, rw_ln_b, w_out, xa_norm_w, mem_norm_w, xa_wq, xa_wk, xa_wv, xa_wo, ffn_norm_w, ffn_w1, ffn_w2, final_norm_w, loss_target, m_mix_norm_w, m_w_in, m_dn_conv_w, m_dn_a_log, m_dn_dt_bias, m_dn_norm_w, m_rw_mu, m_rw_w0, m_rw_w2, m_rw_a0, m_rw_a2, m_rw_g2, m_rw_k_k, m_rw_k_a, m_rw_r_k, m_rw_ln_w, m_rw_ln_b, m_w_out, m_xa_norm_w, m_mem_norm_w, m_xa_wq, m_xa_wk, m_xa_wv, m_xa_wo, m_ffn_norm_w, m_ffn_w1, m_ffn_w2, m_final_norm_w, v_mix_norm_w, v_w_in, v_dn_conv_w, v_dn_a_log, v_dn_dt_bias, v_dn_norm_w, v_rw_mu, v_rw_w0, v_rw_w2, v_rw_a0, v_rw_a2, v_rw_g2, v_rw_k_k, v_rw_k_a, v_rw_r_k, v_rw_ln_w, v_rw_ln_b, v_w_out, v_xa_norm_w, v_mem_norm_w, v_xa_wq, v_xa_wk, v_xa_wv, v_xa_wo, v_ffn_norm_w, v_ffn_w1, v_ffn_w2, v_final_norm_w):
    weights = dict(mix_norm_w=mix_norm_w, w_in=w_in, dn_conv_w=dn_conv_w, dn_a_log=dn_a_log, dn_dt_bias=dn_dt_bias, dn_norm_w=dn_norm_w, rw_mu=rw_mu, rw_w0=rw_w0, rw_w2=rw_w2, rw_a0=rw_a0, rw_a2=rw_a2, rw_g2=rw_g2, rw_k_k=rw_k_k, rw_k_a=rw_k_a, rw_r_k=rw_r_k, rw_ln_w=rw_ln_w, rw_ln_b=rw_ln_b, w_out=w_out, xa_norm_w=xa_norm_w, mem_norm_w=mem_norm_w, xa_wq=xa_wq, xa_wk=xa_wk, xa_wv=xa_wv, xa_wo=xa_wo, ffn_norm_w=ffn_norm_w, ffn_w1=ffn_w1, ffn_w2=ffn_w2, final_norm_w=final_norm_w)
    mom_m = dict(mix_norm_w=m_mix_norm_w, w_in=m_w_in, dn_conv_w=m_dn_conv_w, dn_a_log=m_dn_a_log, dn_dt_bias=m_dn_dt_bias, dn_norm_w=m_dn_norm_w, rw_mu=m_rw_mu, rw_w0=m_rw_w0, rw_w2=m_rw_w2, rw_a0=m_rw_a0, rw_a2=m_rw_a2, rw_g2=m_rw_g2, rw_k_k=m_rw_k_k, rw_k_a=m_rw_k_a, rw_r_k=m_rw_r_k, rw_ln_w=m_rw_ln_w, rw_ln_b=m_rw_ln_b, w_out=m_w_out, xa_norm_w=m_xa_norm_w, mem_norm_w=m_mem_norm_w, xa_wq=m_xa_wq, xa_wk=m_xa_wk, xa_wv=m_xa_wv, xa_wo=m_xa_wo, ffn_norm_w=m_ffn_norm_w, ffn_w1=m_ffn_w1, ffn_w2=m_ffn_w2, final_norm_w=m_final_norm_w)
    mom_v = dict(mix_norm_w=v_mix_norm_w, w_in=v_w_in, dn_conv_w=v_dn_conv_w, dn_a_log=v_dn_a_log, dn_dt_bias=v_dn_dt_bias, dn_norm_w=v_dn_norm_w, rw_mu=v_rw_mu, rw_w0=v_rw_w0, rw_w2=v_rw_w2, rw_a0=v_rw_a0, rw_a2=v_rw_a2, rw_g2=v_rw_g2, rw_k_k=v_rw_k_k, rw_k_a=v_rw_k_a, rw_r_k=v_rw_r_k, rw_ln_w=v_rw_ln_w, rw_ln_b=v_rw_ln_b, w_out=v_w_out, xa_norm_w=v_xa_norm_w, mem_norm_w=v_mem_norm_w, xa_wq=v_xa_wq, xa_wk=v_xa_wk, xa_wv=v_xa_wv, xa_wo=v_xa_wo, ffn_norm_w=v_ffn_norm_w, ffn_w1=v_ffn_w1, ffn_w2=v_ffn_w2, final_norm_w=v_final_norm_w)
    names = list(weights)

    seq, d = x.shape[1], x.shape[2]
    dnw = d // 2
    rww = d - dnw
    nh, nb = dnw // LANES, rww // LANES
    n_mem = mem.shape[1]
    lw_dim, la_dim, lg_dim = rw_w2.shape[1], rw_a2.shape[1], rw_g2.shape[1]
    assert lw_dim + la_dim == LANES and lg_dim == LANES and dnw % LANES == 0 and rww % LANES == 0
    xs, mems, tgt = x[0], mem[0], loss_target[0]

    col_sharded = ("w_in", "xa_wo", "ffn_w1", "dn_conv_w", "rw_w2", "rw_a2", "rw_g2")
    row_sharded = ("w_out", "xa_wq", "xa_wk", "xa_wv", "ffn_w2")
    f32_payload = ("dn_conv_w", "rw_w2", "rw_a2", "rw_g2")
    sharded = col_sharded + row_sharded
    payload = {n: weights[n][0].astype(f32 if n in f32_payload else bf16) for n in sharded}
    shard_w = w_in.shape[2]
    pad_w = -(-shard_w // LANES) * LANES
    payload["w_in"] = jnp.pad(payload["w_in"], ((0, 0), (0, pad_w - shard_w)))
    first = ("w_in", "dn_conv_w", "rw_w2", "rw_a2", "rw_g2")
    mid = ("w_out", "xa_wq", "xa_wk", "xa_wv", "xa_wo")
    late = ("ffn_w1", "ffn_w2")
    gathered = dict(zip(first, gather_chips_halves("gather_first", [payload[n] for n in first])))
    ordered = lax.optimization_barrier(([gathered[n] for n in first], [payload[n] for n in mid + late]))
    gathered = dict(zip(first, ordered[0]))
    payload.update(zip(mid + late, ordered[1]))
    mid_state, tok_mid = chips_start("gather_mid_start", [payload[n] for n in mid], _whole, halves=True)
    late_state, tok_late = chips_start("gather_late_start", [payload[n] for n in late], _whole, halves=True)
    mix_norm_w_in = mix_norm_w + (tok_mid + tok_late)

    def full(n):
        g = gathered[n]
        if n in col_sharded:
            return g.transpose(1, 0, 2).reshape(g.shape[1], 4 * g.shape[2])
        return g.reshape(4 * g.shape[1], g.shape[2])

    c_rw0 = 4 * dnw + 2 * nh
    c_rw = 3 * rww
    eb_ab, eb_r = 4 * nh, 4 * nh + 1
    eb_l1 = eb_r + 3 * nb
    n_ext = -(-(eb_l1 + 2) // 4) * 4
    tbl_fwd, tbl_bwd, per_shard = w_in_layout(4 * shard_w, shard_w, c_rw0, LANES - 2 * nh, n_ext)
    w_ext = lane_regroup(
        "w_in_regroup", gathered["w_in"], tbl_fwd,
        lambda p: pl.BlockSpec((None, d, LANES),
                               lambda i, j, t: (t[REGROUP_FIELDS * p, j], 0, t[REGROUP_FIELDS * p + 1, j])),
        pl.BlockSpec((d, LANES), lambda i, j, t: (0, j)), (d, n_ext * LANES), (1, n_ext), pad_w)
    conv_w = full("dn_conv_w")
    w2p = jnp.concatenate([full("rw_w2"), jnp.zeros((la_dim, rww), f32)], axis=0)
    a2p = jnp.concatenate([jnp.zeros((lw_dim, rww), f32), full("rw_a2")], axis=0)
    g2 = full("rw_g2")
    xaw = xa_wq.shape[2]
    nxh = xaw // LANES
    ffn = 4 * ffn_w1.shape[2]

    def lane_row(vec):
        return jnp.pad(vec.reshape(1, -1), ((0, 0), (0, LANES - vec.size)))

    alog_row, dtb_row = lane_row(dn_a_log), lane_row(dn_dt_bias)
    head_of_col = jnp.arange(dnw)[None, :] // LANES
    e_g = (jnp.arange(LANES)[:, None] == head_of_col).astype(f32)
    e_b = (jnp.arange(LANES)[:, None] == head_of_col + nh).astype(f32)
    mu_main, mu_small = rw_mu[:, :c_rw], rw_mu[:, c_rw:]
    r_k_row = rw_r_k.reshape(1, rww)
    fnw = final_norm_w.reshape(1, d)
    qb, kb_, vb, zb = 0, nh, 2 * nh, 3 * nh
    rb0 = eb_r
    cc = lambda w_, o_: ("constc", w_, o_)
    rc = lambda o_: ("rowc", LANES, o_)

    (u16,) = rowcall("norm_mix", fn_rms, [(xs, "row"), (mix_norm_w_in, "const")], [(seq, d, bf16, "row")], rows=seq)
    p_main = p_small = mm("in_proj", u16, w_ext)

    fn_gconv = make_fn_gconv(2 * nh)
    (qkv,) = rowcall("gdn_conv", fn_gconv, [(p_main, rc(0)), (conv_w, cc(LANES, 0))],
                     [(seq, 3 * dnw, f32, "rowc")], rows=seq, tile=seq, ncol=3 * nh)
    gate_ins = [(p_small, rc(eb_ab)), (alog_row, "const"), (dtb_row, "const"), (e_g, "const"), (e_b, "const")]
    g_b, beta_b = rowcall("gdn_gate", fn_ggate, gate_ins, [(seq, dnw, f32, "row")] * 2, rows=seq)
    gdn_ins = [(qkv, qb), (qkv, kb_), (qkv, vb), (g_b, 0), (beta_b, 0)]
    o_raw, gdn_states = scan_fwd("gdn_scan", gdn_chunk, gdn_ins, rows=seq, chunk=GDN_CHUNK, ncol=nh)
    mid_state, tok = handover_start("gather_mid_pass", chips_wait("gather_mid_wait", mid_state, o_raw))
    gpost_ins = [(o_raw, rc(0)), (p_main, rc(zb)), (dn_norm_w + tok, "const")]
    (o_dn,) = rowcall("gdn_post", fn_gpost, gpost_ins, [(seq, dnw, bf16, "rowc")], rows=seq, ncol=nh)

    (prw,) = rowcall("rw_lerp_main", fn_lerp, [(p_main, rc(rb0)), (mu_main, cc(LANES, 0))],
                     [(seq, c_rw, f32, "rowc")], rows=seq, tile=seq, ncol=3 * nb)
    (psl,) = rowcall("rw_lerp_small", fn_lerp, [(p_small, rc(eb_l1)), (mu_small, cc(LANES, 0))],
                     [(seq, 2 * LANES, f32, "rowc")], rows=seq, tile=seq, ncol=2)
    rprep_ins = [(prw, rc(nb)), (psl, "row"), (rw_w0, cc(LANES, 0)), (rw_a0, cc(LANES, 0)), (rw_k_k, cc(LANES, 0)),
                 (rw_k_a, cc(LANES, 0)), (w2p, cc(LANES, 0)), (a2p, cc(LANES, 0)), (g2, cc(LANES, 0))]
    lw, kmod, kk, a_rw, gate = rowcall("rw_prep", fn_rprep, rprep_ins, [(seq, rww, f32, "rowc")] * 5,
                                        rows=seq, ncol=nb)
    wkv_ins = [(prw, 0), (lw, 0), (kmod, 0), (prw, 2 * nb), (kk, 0), (a_rw, 0)]
    y_rw, wkv_states = scan_fwd("wkv_scan", wkv_chunk, wkv_ins, rows=seq, chunk=WKV_CHUNK, ncol=nb)
    rpost_ins = [(y_rw, rc(0)), (prw, rc(0)), (kmod, rc(0)), (prw, rc(2 * nb)), (gate, rc(0)),
                 (r_k_row, cc(LANES, 0)), (rw_ln_w, cc(LANES, 0)), (rw_ln_b, cc(LANES, 0))]
    (o_rw,) = rowcall("rw_post", fn_rpost, rpost_ins, [(seq, rww, bf16, "rowc")], rows=seq, ncol=nb)

    o_cat = jnp.concatenate([o_dn, o_rw], axis=1)
    late_state, tok = handover_start("gather_late_pass", chips_wait("gather_late_wait", late_state, o_cat))
    gathered.update(zip(mid, chips_wait("gather_mid_pass_wait", mid_state, o_cat)))
    w_out_f, wq_f, wk_f, wv_f, wo_f = full("w_out"), full("xa_wq"), full("xa_wk"), full("xa_wv"), full("xa_wo")
    h1 = mm("out_proj", o_cat, w_out_f, add=xs)

    (hn16,) = rowcall("norm_xa", fn_rms, [(h1, "row"), (xa_norm_w + tok, "const")], [(seq, d, bf16, "row")],
                      rows=seq)
    (mn16,) = rowcall("norm_mem", fn_rms, [(mems, "row"), (mem_norm_w, "const")], [(n_mem, d, bf16, "row")],
                      rows=n_mem)
    q_xa = mm("xa_q", hn16, wq_f)
    k_xa = mm("xa_k", mn16, wk_f)
    v_xa = mm("xa_v", mn16, wv_f)
    xcore_ins = [(q_xa, rc(0)), (k_xa, cc(LANES, 0)), (v_xa, cc(LANES, 0))]
    (o_xa,) = rowcall("xa_core", fn_xcore, xcore_ins, [(seq, xaw, bf16, "rowc")], rows=seq, ncol=nxh)
    h2 = mm("xa_o", o_xa, wo_f, add=h1)

    (fn16,) = rowcall("norm_ffn", fn_rms, [(h2, "row"), (ffn_norm_w, "const")], [(seq, d, bf16, "row")], rows=seq)
    gathered.update(zip(late, chips_wait("gather_late_pass_wait", late_state, fn16)))
    w1_f, w2_f = full("ffn_w1"), full("ffn_w2")
    a1_16, hid16 = mm("ffn_up", fn16, w1_f, epilogue=lambda r: (r, jnp.square(jnp.maximum(r, 0.0))),
                      out_dtypes=(bf16, bf16))
    h3 = mm("ffn_down", hid16, w2_f, add=h2)

    dh3, dh3_16, d_fnw, loss_rows = rowcall(
        "loss_head", fn_final, [(h3, "row"), (tgt, "row"), (fnw, "const")],
        [(seq, d, f32, "row"), (seq, d, bf16, "row"), (1, d, f32, "acc"), (8, LANES, f32, "acc")], rows=seq)

    da1_16 = mm("ffn_down_dx", dh3_16, w2_f, tb=True, extra=[a1_16], out_dtype=bf16,
                epilogue=lambda r, a1: (r * (2.0 * jnp.maximum(a1.astype(f32), 0.0)),))
    def by_chip(n, g):
        if g.ndim == 3:
            return g
        if n in col_sharded:
            return g.reshape(g.shape[0], 4, g.shape[1] // 4).transpose(1, 0, 2)
        return g.reshape(4, g.shape[0] // 4, g.shape[1])

    g_ffn_w2 = mm("ffn_down_dw", hid16, dh3_16, ta=True, out_dtype=bf16)
    ffn_w2_g, tok = chips_start("scatter_ffn_w2_start", [by_chip("ffn_w2", g_ffn_w2)], _chip_block)
    g_ffn_w1 = mm("ffn_up_dw", fn16, da1_16, ta=True, out_dtype=bf16, by_chip=True, after=tok)
    ffn_w1_g, tok = chips_start("scatter_ffn_w1_start", [g_ffn_w1], _chip_block)
    dfn = mm("ffn_up_dx", da1_16, w1_f, tb=True, after=tok)
    dh2, d_ffn_nw, dh2_16 = rowvjp("norm_ffn_bwd", fn_rms, [(h2, "row"), (ffn_norm_w, "const")],
                                   [[(dfn, "row")]], [0, 1], rows=seq, adds=[(0, dh3, "row")], dup16=[0])

    do_xa = mm("xa_o_dx", dh2_16, wo_f, tb=True)
    g_xa_wo = mm("xa_o_dw", o_xa, dh2_16, ta=True, out_dtype=bf16, by_chip=True)
    dq_xa, dk_xa, dv_xa, dq16 = rowvjp("xa_core_bwd", fn_xcore, xcore_ins, [[(do_xa, rc(0))]], [0, 1, 2],
                                       rows=seq, ncol=nxh, dup16=[0])
    g_xa_wq = mm("xa_q_dw", hn16, dq16, ta=True, out_dtype=bf16)
    dhn = mm("xa_q_dx", dq16, wq_f, tb=True)
    dh1, d_xa_nw, dh1_16 = rowvjp("norm_xa_bwd", fn_rms, [(h1, "row"), (xa_norm_w, "const")], [[(dhn, "row")]],
                                  [0, 1], rows=seq, adds=[(0, dh2, "row")], dup16=[0])
    dk16, dv16 = dk_xa.astype(bf16), dv_xa.astype(bf16)
    g_xa_wk = mm("xa_k_dw", mn16, dk16, ta=True, out_dtype=bf16)
    g_xa_wv = mm("xa_v_dw", mn16, dv16, ta=True, out_dtype=bf16)
    dmn = mm("xa_v_dx", dv16, wv_f, tb=True, add=mm("xa_k_dx", dk16, wk_f, tb=True))
    (d_mem_nw,) = rowvjp("norm_mem_bwd", fn_rms, [(mems, "row"), (mem_norm_w, "const")], [[(dmn, "row")]], [1],
                         rows=n_mem)

    g_w_out = mm("out_proj_dw", o_cat, dh1_16, ta=True, out_dtype=bf16)
    mid_grads = dict(w_out=g_w_out, xa_wq=g_xa_wq, xa_wk=g_xa_wk, xa_wv=g_xa_wv, xa_wo=g_xa_wo)
    mid_g, tok = chips_start("scatter_mid_start", [by_chip(n, mid_grads[n]) for n in mid], _chip_block)
    do_cat = mm("out_proj_dx", dh1_16, w_out_f, tb=True, after=tok)

    dy, dr_a, dkmod_a, dv_a, dgate, d_r_k, d_ln_w, d_ln_b = rowvjp(
        "rw_post_bwd", fn_rpost, rpost_ins, [[(do_cat, rc(nh))]], [0, 1, 2, 3, 4, 5, 6, 7], rows=seq, ncol=nb)
    dr_b, dlw, dkmod_b, dv_b, dkk, da_rw = scan_bwd("wkv_scan_bwd", wkv_chunk, wkv_ins, wkv_states, dy, 0,
                                                    rows=seq, chunk=WKV_CHUNK, ncol=nb)
    rprep_cts = [[(dlw, rc(0))], [(dkmod_a, rc(0)), (dkmod_b, rc(0))], [(dkk, rc(0))], [(da_rw, rc(0))],
                 [(dgate, rc(0))]]
    dpk, dpsl_parts, d_w0, d_a0, d_k_k, d_k_a, d_w2p, d_a2p, d_g2 = rowvjp(
        "rw_prep_bwd", fn_rprep, rprep_ins, rprep_cts, [0, 1, 2, 3, 4, 5, 6, 7, 8], rows=seq, ncol=nb)
    (dpsl,) = rowcall("rw_prep_sum", fn_sumcols(nb), [(dpsl_parts, "row")], [(seq, 2 * LANES, f32, "row")], rows=seq)

    def lerp_bwd(tag, p, p_off, mu, mu_off, ct_lists, ncol):
        return rowvjp("rw_lerp_bwd_" + tag, fn_lerp, [(p, rc(p_off)), (mu, cc(LANES, mu_off))], [ct_lists], [0, 1],
                      rows=seq, tile=seq, ncol=ncol, dup16=[0])

    _, dmu_r, dpr16 = lerp_bwd("r", p_main, rb0, mu_main, 0, [(dr_a, rc(0)), (dr_b, rc(0))], nb)
    _, dmu_k, dpk16 = lerp_bwd("k", p_main, rb0 + nb, mu_main, nb, [(dpk, rc(0))], nb)
    _, dmu_v, dpv16 = lerp_bwd("v", p_main, rb0 + 2 * nb, mu_main, 2 * nb, [(dv_a, rc(0)), (dv_b, rc(0))], nb)
    _, dmu_s, dps12_16 = lerp_bwd("small", p_small, eb_l1, mu_small, 0, [(dpsl, rc(0))], 2)

    do_raw, dz, d_dn_nw, dz16 = rowvjp("gdn_post_bwd", fn_gpost, gpost_ins, [[(do_cat, rc(0))]], [0, 1, 2],
                                       rows=seq, ncol=nh, dup16=[1])
    dq_g, dk_g, dv_g, dg_b, dbeta_b = scan_bwd("gdn_scan_bwd", gdn_chunk, gdn_ins, gdn_states, do_raw, 0,
                                               rows=seq, chunk=GDN_CHUNK, ncol=nh)
    dps0, d_alog, d_dtb, dps0_16 = rowvjp("gdn_gate_bwd", fn_ggate, gate_ins, [[(dg_b, "row")], [(dbeta_b, "row")]],
                                          [0, 1, 2], rows=seq, dup16=[0])
    dqkv = jnp.concatenate([dq_g, dk_g, dv_g], axis=1)
    _, d_conv_w, dqkv16 = rowvjp("gdn_conv_bwd", fn_gconv, [(p_main, rc(0)), (conv_w, cc(LANES, 0))],
                                 [[(dqkv, rc(0))]], [0, 1], rows=seq, tile=seq, ncol=3 * nh, dup16=[0])

    dp16 = jnp.concatenate([dqkv16, dz16, dps0_16, dpr16, dpk16, dpv16, dps12_16,
                            jnp.zeros((seq, (n_ext - eb_l1 - 2) * LANES), bf16)], axis=1)
    g_w_ext = mm("in_proj_dw", u16, dp16, ta=True, out_dtype=bf16)
    g_w_in = lane_regroup(
        "w_in_grad_regroup", g_w_ext, tbl_bwd,
        lambda p: pl.BlockSpec((d, LANES), lambda i, j, t: (0, t[REGROUP_FIELDS * p + 1, j])),
        pl.BlockSpec((None, d, LANES), lambda i, j, t: (j // per_shard, 0, j % per_shard)),
        (4, d, pad_w), (1, 4 * per_shard), n_ext * LANES)
    first_grads = dict(w_in=g_w_in, dn_conv_w=d_conv_w, rw_w2=d_w2p[:lw_dim], rw_a2=d_a2p[lw_dim:], rw_g2=d_g2)
    first_g, tok = chips_start("scatter_first_start", [by_chip(n, first_grads[n]) for n in first], _chip_block)

    du = mm("in_proj_dx", dp16, w_ext, tb=True, after=tok)
    grad_x, d_mix_nw = rowvjp("norm_mix_bwd", fn_rms, [(xs, "row"), (mix_norm_w, "const")], [[(du, "row")]],
                              [0, 1], rows=seq, adds=[(0, dh1, "row")])
    received = dict(zip(mid, chips_wait("scatter_mid_wait", mid_g, grad_x)))
    received["ffn_w2"] = chips_wait("scatter_ffn_w2_wait", ffn_w2_g, grad_x)[0]
    received["ffn_w1"] = chips_wait("scatter_ffn_w1_wait", ffn_w1_g, grad_x)[0]

    out = {}

    def reduce_and_update(tag, group):
        partial = [sum_slots("sum_chips_" + n, received[n]) for n in group]
        other = swap_sibling("swap_sibling_" + tag, partial)
        for n, p_mine, p_other in zip(group, partial, other):
            if weights[n].shape[2] % LANES:
                rows, cols = weights[n].shape[1:]
                lin = lambda a: jnp.swapaxes(a, 1, 2).reshape(-1, LANES)
                lin_g = lambda p: p.T[:cols].reshape(-1, LANES)
                res = adamw("adamw_" + n, lin(weights[n]), [lin_g(p_mine), lin_g(p_other)], lin(mom_m[n]),
                            lin(mom_v[n]))
                out[n] = [jnp.swapaxes(r.reshape(1, cols, rows), 1, 2) for r in res]
            else:
                out[n] = adamw("adamw_" + n, weights[n], [p_mine, p_other], mom_m[n], mom_v[n])

    reduce_and_update("rest", mid + late)
    received.update(zip(first, chips_wait("scatter_first_wait", first_g, [out[n][1] for n in mid + late])))
    reduce_and_update("first", first)

    small_names = [n for n in names if n not in sharded]
    small_local = dict(
        mix_norm_w=d_mix_nw, dn_a_log=d_alog[:, :nh], dn_dt_bias=d_dtb[:, :nh], dn_norm_w=d_dn_nw,
        rw_mu=jnp.concatenate([dmu_r, dmu_k, dmu_v, dmu_s], axis=1), rw_w0=d_w0, rw_a0=d_a0, rw_k_k=d_k_k,
        rw_k_a=d_k_a, rw_r_k=d_r_k, rw_ln_w=d_ln_w, rw_ln_b=d_ln_b, xa_norm_w=d_xa_nw, mem_norm_w=d_mem_nw,
        ffn_norm_w=d_ffn_nw, final_norm_w=d_fnw)
    loss_vec = jnp.where(jnp.arange(LANES) == 0, loss_rows[0], 0.0)
    (g8,) = gather_all([_pack([small_local[n] for n in small_names] + [loss_vec])], after=[out["w_in"][1]])

    packed_like = [weights[n] for n in small_names] + [loss_vec]
    zero = jnp.zeros((LANES,), f32)
    res = adamw_packed(_pack([weights[n] for n in small_names] + [zero]), g8,
                       _pack([mom_m[n] for n in small_names] + [zero]),
                       _pack([mom_v[n] for n in small_names] + [zero]))
    unpacked = [_unpack(r, packed_like) for r in res]
    for i, n in enumerate(small_names):
        out[n] = [u[i] for u in unpacked]
    loss = unpacked[0][-1][0]

    return (loss, grad_x.reshape(x.shape), *[out[n][0] for n in names], *[out[n][1] for n in names],
            *[out[n][2] for n in names], *[out[n][3] for n in names])
```

```python
import functools

import jax
import jax.numpy as jnp
from jax import lax
from jax.experimental import pallas as pl
from jax.experimental.pallas import tpu as pltpu

f32 = jnp.float32
bf16 = jnp.bfloat16
HI = lax.Precision.HIGHEST
MESH = pl.DeviceIdType.MESH

LANES = 128
VMEM_LIMIT = 56 << 20
TOK_TILE = 256
TOK_TILE_BLOCKED = 1024
MM_TILE = 1024
MM_TILE_K = 2048
GDN_CHUNK = 128
WKV_CHUNK = 64
SCAN_GROUP = 8
P_BULK = 1
P_INV = 1
P_RESID = 3
P_CUMSUM = 3
RMS_EPS = 1e-6
RW_GN_EPS = 64e-5
RW_HEAD = 64

ADAM_LR, ADAM_B1, ADAM_B2, ADAM_EPS, ADAM_WD, ADAM_STEP = 0.001, 0.9, 0.999, 1e-08, 0.01, 10


def _params(n_grid):
    return pltpu.CompilerParams(dimension_semantics=("arbitrary",) * n_grid, vmem_limit_bytes=VMEM_LIMIT)


_DIMS = {"nn": (((1,), (0,)), ((), ())), "nt": (((1,), (1,)), ((), ())), "tn": (((0,), (0,)), ((), ()))}
_DIMS_BATCHED = {"nn": (((2,), (1,)), ((0,), (0,))), "nt": (((2,), (2,)), ((0,), (0,))),
                 "tn": (((1,), (1,)), ((0,), (0,)))}


def _raw_dot(a, b, mode, passes):
    dims = (_DIMS if a.ndim == 2 else _DIMS_BATCHED)[mode]
    if passes == 6:
        return lax.dot_general(a.astype(f32), b.astype(f32), dims, precision=HI, preferred_element_type=f32)
    ah, bh = a.astype(bf16), b.astype(bf16)
    r = lax.dot_general(ah, bh, dims, preferred_element_type=f32)
    if passes == 3:
        al = (a - ah.astype(f32)).astype(bf16)
        bl = (b - bh.astype(f32)).astype(bf16)
        r = r + lax.dot_general(al, bh, dims, preferred_element_type=f32)
        r = r + lax.dot_general(ah, bl, dims, preferred_element_type=f32)
    return r


@functools.partial(jax.custom_vjp, nondiff_argnums=(2, 3))
def pdot(a, b, mode, passes):
    return _raw_dot(a, b, mode, passes)


def _pdot_bwd(mode, passes, res, g):
    a, b = res
    if mode == "nn":
        da, db = _raw_dot(g, b, "nt", passes), _raw_dot(a, g, "tn", passes)
    elif mode == "nt":
        da, db = _raw_dot(g, b, "nn", passes), _raw_dot(g, a, "tn", passes)
    else:
        da, db = _raw_dot(b, g, "nt", passes), _raw_dot(a, g, "nn", passes)
    return da.astype(a.dtype), db.astype(b.dtype)


pdot.defvjp(lambda a, b, mode, passes: (_raw_dot(a, b, mode, passes), (a, b)), _pdot_bwd)


def hdot(a, b):
    return pdot(a, b, "nn", 6)


def bdot(a, b):
    return pdot(a, b, "nn", 1)


def bdot_nt(a, b):
    return pdot(a, b, "nt", 1)


def _shift_rows(x, k):
    row = lax.broadcasted_iota(jnp.int32, x.shape, 0)
    return jnp.where(row < k, 0.0, pltpu.roll(x, k, axis=0))


def _unshift_rows(g, k):
    n = g.shape[0]
    row = lax.broadcasted_iota(jnp.int32, g.shape, 0)
    return jnp.where(row >= n - k, 0.0, pltpu.roll(g, n - k, axis=0))


@functools.partial(jax.custom_vjp, nondiff_argnums=(1,))
def tshift(x, k):
    return _shift_rows(x, k)


tshift.defvjp(lambda x, k: (_shift_rows(x, k), None), lambda k, _, g: (_unshift_rows(g, k),))


def rms(x, w):
    x = x.astype(f32)
    return x * lax.rsqrt(jnp.mean(x * x, axis=-1, keepdims=True) + RMS_EPS) * w


def softplus(x):
    return jnp.maximum(x, 0.0) + jnp.log(1.0 + jnp.exp(-jnp.abs(x)))


def seg2sum(x):
    lo = lax.broadcasted_iota(jnp.int32, x.shape, 1) < RW_HEAD
    s_lo = jnp.sum(jnp.where(lo, x, 0.0), axis=-1, keepdims=True)
    s_hi = jnp.sum(jnp.where(lo, 0.0, x), axis=-1, keepdims=True)
    return jnp.where(lo, s_lo, s_hi)


def _tile(n, pref):
    if n <= pref:
        return n
    t = pref
    while t >= LANES:
        if n % t == 0:
            return t
        t -= LANES
    return n


def mm(name, a, b, *, ta=False, tb=False, add=None, out_dtype=f32, by_chip=False, epilogue=None, extra=(),
       out_dtypes=None, after=None):
    (k, m) = a.shape if ta else a.shape[::-1]
    (n, kb) = b.shape if tb else b.shape[::-1]
    assert k == kb, (name, a.shape, b.shape)
    tm, tk = _tile(m, MM_TILE), _tile(k, MM_TILE_K)
    tn = _tile(n // 4, MM_TILE) if by_chip else _tile(n, MM_TILE)
    nk = k // tk
    dims = (((0,) if ta else (1,), (1,) if tb else (0,)), ((), ()))
    extra = list(extra) + ([] if add is None else [add])
    out_dtypes = [out_dtype] if out_dtypes is None else list(out_dtypes)
    n_extra, n_out = len(extra), len(out_dtypes)
    n_after = 0 if after is None else 1

    def body(*refs):
        a_ref, b_ref = refs[:2]
        x_refs = refs[2:2 + n_extra]
        o_refs = refs[2 + n_extra + n_after:2 + n_extra + n_after + n_out]
        part = lax.dot_general(a_ref[...].astype(bf16), b_ref[...].astype(bf16), dims, preferred_element_type=f32)

        def finish(r):
            xs = [x[...] for x in x_refs]
            if add is not None:
                r = r + xs.pop().astype(f32)
            outs = (r,) if epilogue is None else epilogue(r, *xs)
            for o_ref, o in zip(o_refs, outs):
                o_ref[...] = o.astype(o_ref.dtype)

        if nk == 1:
            finish(part)
            return
        acc = refs[-1]
        kk = pl.program_id(2)

        @pl.when(kk == 0)
        def _():
            acc[...] = part

        @pl.when(kk > 0)
        def _():
            acc[...] += part

        @pl.when(kk == nk - 1)
        def _():
            finish(acc[...])

    a_spec = pl.BlockSpec((tk, tm), lambda i, j, q: (q, i)) if ta else pl.BlockSpec((tm, tk), lambda i, j, q: (i, q))
    b_spec = pl.BlockSpec((tn, tk), lambda i, j, q: (j, q)) if tb else pl.BlockSpec((tk, tn), lambda i, j, q: (q, j))
    x_spec = pl.BlockSpec((tm, tn), lambda i, j, q: (i, j))
    if by_chip:
        per_chip = n // 4 // tn
        o_spec = pl.BlockSpec((None, tm, tn), lambda i, j, q: (j // per_chip, i, j % per_chip))
        o_shape = (4, m, n // 4)
    else:
        o_spec, o_shape = x_spec, (m, n)
    afters = [] if after is None else [jnp.reshape(after, (1, 1))]
    res = pl.pallas_call(
        body, name=name, grid=(m // tm, n // tn, nk),
        in_specs=[a_spec, b_spec] + [x_spec] * n_extra + [pl.BlockSpec(memory_space=pl.ANY)] * n_after,
        out_specs=[o_spec] * n_out, out_shape=[jax.ShapeDtypeStruct(o_shape, dt) for dt in out_dtypes],
        scratch_shapes=[pltpu.VMEM((tm, tn), f32)] if nk > 1 else [],
        compiler_params=_params(3))(a, b, *extra, *afters)
    return res[0] if n_out == 1 else res


REGROUP_PIECES = 4


REGROUP_FIELDS = 5


def _regroup_table(n_out, sources_of):
    import numpy as np
    tbl = np.zeros((REGROUP_FIELDS * REGROUP_PIECES, n_out), np.int32)
    for j in range(n_out):
        groups = sorted(sources_of(j).items())
        assert len(groups) <= REGROUP_PIECES, (j, len(groups))
        for p in range(REGROUP_PIECES):
            if p < len(groups):
                key, lanes = groups[p]
                shifts = {q - s for s, q in lanes}
                qs = sorted(q for _, q in lanes)
                assert len(shifts) == 1 and qs == list(range(qs[0], qs[-1] + 1)), (j, key)
                row = (key[0], key[1], shifts.pop(), qs[0], qs[-1] + 1)
            else:
                row = (0, 0, 0, 0, 0)
            tbl[REGROUP_FIELDS * p:REGROUP_FIELDS * (p + 1), j] = row
    return jnp.asarray(tbl)


def lane_regroup(name, src, table, src_spec, out_spec, out_shape, grid, src_width):
    rows = src.shape[-2]

    def body(tbl, *refs):
        o_ref, acc = refs[REGROUP_PIECES:]
        j = pl.program_id(1)
        acc[...] = jnp.zeros_like(acc)
        for p in range(REGROUP_PIECES):
            blk, shift, lo, hi = (tbl[REGROUP_FIELDS * p + f, j] for f in range(1, REGROUP_FIELDS))

            @pl.when(hi > lo)
            def _(p=p, blk=blk, shift=shift, lo=lo, hi=hi):
                lane = lax.broadcasted_iota(jnp.int32, (rows, LANES), 1)
                pi = lax.broadcasted_iota(jnp.int32, (LANES, LANES), 0)
                qi = lax.broadcasted_iota(jnp.int32, (LANES, LANES), 1)
                x = jnp.where(lane < src_width - blk * LANES, refs[p][...], jnp.zeros((), src.dtype))
                sel = jnp.logical_and(qi - pi == shift, jnp.logical_and(qi >= lo, qi < hi))
                acc[...] += jnp.dot(x, sel.astype(src.dtype), preferred_element_type=f32)

        o_ref[...] = acc[...].astype(o_ref.dtype)

    return pl.pallas_call(
        body, name=name, out_shape=jax.ShapeDtypeStruct(out_shape, src.dtype),
        grid_spec=pltpu.PrefetchScalarGridSpec(
            num_scalar_prefetch=1, grid=grid, in_specs=[src_spec(p) for p in range(REGROUP_PIECES)],
            out_specs=out_spec, scratch_shapes=[pltpu.VMEM((rows, LANES), f32)]),
        compiler_params=_params(2))(table, *[src] * REGROUP_PIECES)


def w_in_layout(d_cols, shard_w, c_split, gap, n_blocks):
    def ext_of(c):
        return c if c < c_split else c + gap

    def fwd_sources(j):
        groups = {}
        for q in range(LANES):
            e = j * LANES + q
            c = e if e < c_split else e - gap
            if (c_split <= e < c_split + gap) or c >= d_cols:
                continue
            s, l = divmod(c, shard_w)
            groups.setdefault((s, l // LANES), []).append((l % LANES, q))
        return groups

    per_shard = -(-shard_w // LANES)

    def bwd_sources(j):
        s, b = divmod(j, per_shard)
        groups = {}
        for q in range(LANES):
            l = b * LANES + q
            if l >= shard_w:
                continue
            e = ext_of(s * shard_w + l)
            groups.setdefault((0, e // LANES), []).append((e % LANES, q))
        return groups

    return _regroup_table(n_blocks, fwd_sources), _regroup_table(4 * per_shard, bwd_sources), per_shard


def _in_spec(a, kind, tile):
    if kind == "row":
        return pl.BlockSpec((tile, a.shape[1]), lambda j, i: (i, 0))
    if kind == "const":
        return pl.BlockSpec(a.shape, lambda j, i: (0, 0))
    tag, cw, off = kind
    if tag == "rowc":
        return pl.BlockSpec((tile, cw), lambda j, i: (i, j + off))
    assert tag == "constc", kind
    return pl.BlockSpec((a.shape[0], cw), lambda j, i: (0, j + off))


def rowcall(name, fn, ins, outs, *, rows, tile=None, ncol=1):
    if tile is None:
        tile = min(TOK_TILE_BLOCKED if ncol > 1 else TOK_TILE, rows)
    n_in = len(ins)
    kinds = [o[3] for o in outs]

    def body(*refs):
        j, i = pl.program_id(0), pl.program_id(1)
        res = fn(*[r[...] for r in refs[:n_in]])
        for ref, val, kind in zip(refs[n_in:], res, kinds):
            if kind in ("row", "rowc"):
                ref[...] = val.astype(ref.dtype)
            else:
                first = (i == 0) if kind == "accc" else jnp.logical_and(i == 0, j == 0)

                @pl.when(first)
                def _(ref=ref, val=val):
                    ref[...] = val.astype(ref.dtype)

                @pl.when(jnp.logical_not(first))
                def _(ref=ref, val=val):
                    ref[...] += val.astype(ref.dtype)

    out_shape, out_specs = [], []
    for nr, nc, dtype, kind in outs:
        out_shape.append(jax.ShapeDtypeStruct((nr, nc), dtype))
        if kind == "row":
            out_specs.append(pl.BlockSpec((tile, nc), lambda j, i: (i, 0)))
        elif kind == "rowc":
            out_specs.append(pl.BlockSpec((tile, nc // ncol), lambda j, i: (i, j)))
        elif kind == "acc":
            out_specs.append(pl.BlockSpec((nr, nc), lambda j, i: (0, 0)))
        else:
            out_specs.append(pl.BlockSpec((nr, nc // ncol), lambda j, i: (0, j)))
    return pl.pallas_call(
        body, name=name, grid=(ncol, rows // tile), in_specs=[_in_spec(a, k, tile) for a, k in ins],
        out_specs=out_specs, out_shape=out_shape, compiler_params=_params(2))(*[a for a, _ in ins])


def rowvjp(name, fn, ins, cts, grads, *, rows, tile=None, ncol=1, adds=(), dup16=()):
    n_in = len(ins)
    ct_sizes = [len(c) for c in cts]
    flat_cts = [m for c in cts for m in c]
    n_ct = len(flat_cts)

    def wrapped(*vals):
        xs = list(vals[:n_in])
        gs = vals[n_in:n_in + n_ct]
        extra = vals[n_in + n_ct:]

        def f(*dvars):
            full = list(xs)
            for k, v in zip(grads, dvars):
                full[k] = v
            return fn(*full)

        outs, pull = jax.vjp(f, *[xs[k] for k in grads])
        cot, p = [], 0
        for o, size in zip(outs, ct_sizes):
            g = gs[p].astype(f32)
            for q in range(1, size):
                g = g + gs[p + q].astype(f32)
            cot.append(g.astype(o.dtype))
            p += size
        gv = list(pull(tuple(cot)))
        for (pos, _, _), e in zip(adds, extra):
            gv[pos] = gv[pos] + e.astype(gv[pos].dtype)
        return tuple(gv) + tuple(gv[pos] for pos in dup16)

    outs = []
    for k in grads:
        a, kind = ins[k]
        if kind == "row":
            outs.append((rows, a.shape[1] * ncol, f32, "rowc") if ncol > 1 else (rows, a.shape[1], f32, "row"))
        elif kind == "const":
            outs.append((a.shape[0], a.shape[1], f32, "acc"))
        elif kind[0] == "rowc":
            outs.append((rows, kind[1] * ncol, f32, "rowc"))
        else:
            outs.append((a.shape[0], kind[1] * ncol, f32, "accc"))
    for pos in dup16:
        nr, nc, _, kind = outs[pos]
        outs.append((nr, nc, bf16, kind))
    all_ins = list(ins) + flat_cts + [(a, kind) for _, a, kind in adds]
    return rowcall(name, wrapped, all_ins, outs, rows=rows, tile=tile, ncol=ncol)


def fn_rms(x, w):
    return (rms(x, w),)


def make_fn_gconv(n_norm_blocks):
    def fn(p, cw):
        c = cw[3:4] * p
        for jj in range(3):
            c = c + cw[jj:jj + 1] * tshift(p, 3 - jj)
        s = c * jax.nn.sigmoid(c)
        nrm = s * lax.rsqrt(jnp.sum(s * s, axis=-1, keepdims=True) + 1e-6)
        return (jnp.where(pl.program_id(0) < n_norm_blocks, nrm, s),)
    return fn


def fn_ggate(ps0, alog, dtb, e_g, e_b):
    g = -jnp.exp(alog) * softplus(ps0 + dtb)
    beta = jax.nn.sigmoid(ps0)
    return hdot(g, e_g), hdot(beta, e_b)


def fn_gpost(o, z, nw):
    return (rms(o, nw) * (z * jax.nn.sigmoid(z)),)


def fn_lerp(p, mu):
    return (p + (tshift(p, 1) - p) * mu,)


def fn_rprep(pk, psl, w0, a0, k_k, k_a, w2p, a2p, g2):
    g1, g2in = psl[:, :LANES], psl[:, LANES:]
    log_w = -softplus(-(w0 + bdot(jnp.tanh(g1), w2p))) - 0.5
    lw = -jnp.exp(log_w)
    a = jax.nn.sigmoid(a0 + bdot(g1, a2p))
    gate = bdot(jax.nn.sigmoid(g2in), g2)
    kkr = pk * k_k
    kk = kkr / jnp.maximum(jnp.sqrt(seg2sum(kkr * kkr)), 1e-12)
    kmod = pk * (1.0 + (a - 1.0) * k_a)
    return lw, kmod, kk, a, gate


def fn_rpost(y, r, kmod, v, gate, r_k, ln_w, ln_b):
    inv_n = 1.0 / RW_HEAD
    mean = seg2sum(y) * inv_n
    d = y - mean
    var = seg2sum(d * d) * inv_n
    yn = d * lax.rsqrt(var + RW_GN_EPS) * ln_w + ln_b
    bonus = seg2sum(r * kmod * r_k) * v
    return ((yn + bonus) * gate,)


def fn_xcore(q, k, v):
    s = bdot_nt(q, k) * (LANES ** -0.5)
    p = jax.nn.softmax(s, axis=-1)
    return (bdot(p, v),)


def fn_final(h, tgt, w):
    def loss_fn(h, w):
        return 0.5 * jnp.sum(jnp.mean(jnp.square(rms(h, w) - tgt), axis=-1))

    val, (dh, dw) = jax.value_and_grad(loss_fn, argnums=(0, 1))(h, w)
    return dh, dh.astype(bf16), dw, jnp.full((8, LANES), val, f32)


def fn_sumcols(n):
    def fn(x):
        w = x.shape[1] // n
        s = x[:, :w]
        for q in range(1, n):
            s = s + x[:, q * w:(q + 1) * w]
        return (s,)
    return fn


def _tri(c):
    ii = lax.broadcasted_iota(jnp.int32, (c, c), 0)
    jj = lax.broadcasted_iota(jnp.int32, (c, c), 1)
    return ii, jj


def _neumann_raw(m, steps):
    c = m.shape[-1]
    ii, jj = _tri(c)
    eye = (ii == jj).astype(f32)
    t, p = eye + m, m
    for _ in range(steps):
        p = _raw_dot(p, p, "nn", P_INV)
        t = _raw_dot(t, eye + p, "nn", P_INV)
    resid = eye - t + _raw_dot(m, t, "nn", P_RESID)
    return t + _raw_dot(t, resid, "nn", P_INV)


@functools.partial(jax.custom_vjp, nondiff_argnums=(1,))
def _neumann_inverse(m, steps):
    return _neumann_raw(m, steps)


def _neumann_fwd(m, steps):
    t = _neumann_raw(m, steps)
    return t, t


def _neumann_bwd(steps, t, g):
    return (_raw_dot(_raw_dot(t, g, "tn", P_RESID), t, "nt", P_RESID),)


_neumann_inverse.defvjp(_neumann_fwd, _neumann_bwd)


def cumsum_rows(x):
    t = x.shape[1]
    ii, jj = _tri(t)
    tri = jnp.broadcast_to((ii >= jj).astype(f32), (x.shape[0], t, t))
    return pdot(tri, x, "nn", P_CUMSUM)


def gdn_chunk(q, k, v, gb, bb, s):
    c = q.shape[1]
    ii, jj = _tri(c)
    low = ii >= jj
    gcb = cumsum_rows(gb)
    gl = jnp.sum(gb, axis=1, keepdims=True)
    gc_col = gcb[:, :, :c]
    diff = gc_col - jnp.swapaxes(gc_col, 1, 2)
    decay = jnp.where(low, jnp.exp(jnp.where(low, diff, 0.0)), 0.0)
    qs = q * (q.shape[2] ** -0.5)
    kb = k * bb
    with_k = pdot(jnp.concatenate([kb, qs], axis=1), k, "nt", P_BULK)
    a = jnp.where(ii > jj, with_k[:, :c] * decay, 0.0)
    attn = with_k[:, c:] * decay
    t = _neumann_inverse(-a, c.bit_length() - 2)
    eg = jnp.exp(gcb)
    uw = pdot(t, jnp.concatenate([v * bb, kb * eg], axis=2), "nn", P_BULK)
    u, w = uw[:, :, :LANES], uw[:, :, LANES:]
    kd = k * jnp.exp(gl - gcb)
    from_state = pdot(jnp.concatenate([w, qs * eg], axis=1), s, "nn", P_BULK)
    v_new = u - from_state[:, :c]
    o = from_state[:, c:] + pdot(attn, v_new, "nn", P_BULK)
    s_new = s * jnp.exp(gl) + pdot(kd, v_new, "tn", P_BULK)
    return o, s_new


def wkv_chunk(r, lw, k, v, kk, a, s):
    t = r.shape[1]
    ii, jj = _tri(t)
    lo = lax.broadcasted_iota(jnp.int32, r.shape, 2) < RW_HEAD
    cl = cumsum_rows(lw)
    cl_last = jnp.sum(lw, axis=1, keepdims=True)
    al = -kk * jnp.exp(cl - lw)
    be = (a * kk) * jnp.exp(-cl)
    kt = k * jnp.exp(-cl)
    rt = r * jnp.exp(cl)

    def dot(xa, xb, mode="nn"):
        return pdot(xa, xb, mode, P_BULK)

    def sel(x_lo, x_hi):
        return jnp.where(lo, x_lo, x_hi)

    left = jnp.concatenate([jnp.where(lo, al, 0.0), jnp.where(lo, 0.0, al),
                            jnp.where(lo, rt, 0.0), jnp.where(lo, 0.0, rt)], axis=1)
    with_be, with_kt = dot(left, be, "nt"), dot(left, kt, "nt")

    def blocks(prod):
        below, upto = ii > jj, ii >= jj
        return [jnp.where(msk, prod[:, q * t:(q + 1) * t], 0.0) for q, msk in enumerate((below, below, upto, upto))]

    ab_lo, ab_hi, rb_lo, rb_hi = blocks(with_be)
    ak_lo, ak_hi, rk_lo, rk_hi = blocks(with_kt)
    from_state = dot(jnp.concatenate([al, rt], axis=1), s, "nt")
    x = from_state[:, :t] + sel(dot(ak_lo, v), dot(ak_hi, v))
    steps = t.bit_length() - 2
    u = sel(dot(_neumann_inverse(ab_lo, steps), x), dot(_neumann_inverse(ab_hi, steps), x))
    y = from_state[:, t:] + sel(dot(rb_lo, u) + dot(rk_lo, v), dot(rb_hi, u) + dot(rk_hi, v))
    vi = lax.broadcasted_iota(jnp.int32, s.shape, 1) < RW_HEAD
    ki = lax.broadcasted_iota(jnp.int32, s.shape, 2) < RW_HEAD
    s_new = jnp.where(vi == ki, (s + dot(u, be, "tn") + dot(v, kt, "tn")) * jnp.exp(cl_last), 0.0)
    return y, s_new


def _scan_group(ncol, offs):
    g = SCAN_GROUP
    while g > 1 and (ncol % g or any(o % g for o in offs)):
        g //= 2
    return g


def scan_fwd(name, chunk_fn, ins, *, rows, chunk, ncol):
    n = rows // chunk
    n_in = len(ins)
    grp = _scan_group(ncol, [off for _, off in ins])

    def body(*refs):
        o_ref, st_ref, s_scr = refs[n_in:]

        @pl.when(pl.program_id(1) == 0)
        def _():
            s_scr[...] = jnp.zeros_like(s_scr)

        cols = [slice(b * LANES, (b + 1) * LANES) for b in range(grp)]
        s = s_scr[...]
        st_ref[...] = s
        o, s_new = chunk_fn(*[jnp.stack([r[:, c] for c in cols]) for r in refs[:n_in]], s)
        for b, c in enumerate(cols):
            o_ref[:, c] = o[b]
        s_scr[...] = s_new

    def spec(off):
        return pl.BlockSpec((chunk, grp * LANES), lambda h, c: (c, h + off // grp))

    return pl.pallas_call(
        body, name=name, grid=(ncol // grp, n), in_specs=[spec(off) for _, off in ins],
        out_specs=[spec(0), pl.BlockSpec((grp, None, LANES, LANES), lambda h, c: (h, c, 0, 0))],
        out_shape=[jax.ShapeDtypeStruct((rows, ncol * LANES), f32),
                   jax.ShapeDtypeStruct((ncol, n, LANES, LANES), f32)],
        scratch_shapes=[pltpu.VMEM((grp, LANES, LANES), f32)], compiler_params=_params(2))(*[a for a, _ in ins])


def scan_bwd(name, chunk_fn, ins, states, d_out, d_off, *, rows, chunk, ncol):
    n = rows // chunk
    n_in = len(ins)
    grp = _scan_group(ncol, [off for _, off in ins] + [d_off])

    def body(*refs):
        st_ref, do_ref = refs[n_in:n_in + 2]
        g_refs = refs[n_in + 2:2 * n_in + 2]
        ds_scr = refs[-1]

        @pl.when(pl.program_id(1) == 0)
        def _():
            ds_scr[...] = jnp.zeros_like(ds_scr)

        cols = [slice(b * LANES, (b + 1) * LANES) for b in range(grp)]

        def batch(ref):
            return jnp.stack([ref[:, c] for c in cols])

        _, pull = jax.vjp(chunk_fn, *[batch(r) for r in refs[:n_in]], st_ref[...])
        gs = pull((batch(do_ref), ds_scr[...]))
        for ref, g in zip(g_refs, gs[:n_in]):
            for b, c in enumerate(cols):
                ref[:, c] = g[b]
        ds_scr[...] = gs[n_in]

    def spec(off):
        return pl.BlockSpec((chunk, grp * LANES), lambda h, c: (n - 1 - c, h + off // grp))

    st_spec = pl.BlockSpec((grp, None, LANES, LANES), lambda h, c: (h, n - 1 - c, 0, 0))
    return pl.pallas_call(
        body, name=name, grid=(ncol // grp, n), in_specs=[spec(off) for _, off in ins] + [st_spec, spec(d_off)],
        out_specs=[spec(0)] * n_in, out_shape=[jax.ShapeDtypeStruct((rows, ncol * LANES), f32)] * n_in,
        scratch_shapes=[pltpu.VMEM((grp, LANES, LANES), f32)],
        compiler_params=_params(2))(*[a for a, _ in ins], states, d_out)


def flip_exchange(name, arrs, flips, n_slots, slot_of, src_of, with_self, after=()):
    n = len(arrs)
    nf = len(flips)
    n_after = len(after)

    def body(*refs):
        ins, outs = refs[:n], refs[n + n_after:2 * n + n_after]
        send, recv, lsem = refs[2 * n + n_after:]
        me = (lax.axis_index("x"), lax.axis_index("y"), lax.axis_index("c"))
        copies = []
        for k in range(n):
            if with_self:
                cp = pltpu.make_async_copy(src_of(ins[k], me), outs[k].at[slot_of(me)], lsem.at[k])
                cp.start()
                copies.append(cp)
            for j, fl in enumerate(flips):
                peer = tuple(1 - m if f else m for m, f in zip(me, fl))
                cp = pltpu.make_async_remote_copy(
                    src_ref=src_of(ins[k], peer), dst_ref=outs[k].at[slot_of(me)], send_sem=send.at[k, j],
                    recv_sem=recv.at[k, j], device_id=peer, device_id_type=MESH)
                cp.start()
                copies.append(cp)
        for cp in copies:
            cp.wait()

    def out_sds(a):
        blk = src_of(jax.ShapeDtypeStruct(a.shape, a.dtype), None)
        return jax.ShapeDtypeStruct((n_slots,) + tuple(blk), a.dtype)

    any_spec = pl.BlockSpec(memory_space=pl.ANY)
    return pl.pallas_call(
        body, name=name, in_specs=[any_spec] * (n + n_after), out_specs=[any_spec] * n,
        out_shape=[out_sds(a) for a in arrs],
        scratch_shapes=[pltpu.SemaphoreType.DMA((n, nf)), pltpu.SemaphoreType.DMA((n, nf)),
                        pltpu.SemaphoreType.DMA((n,))],
        compiler_params=pltpu.CompilerParams(has_side_effects=True))(*arrs, *after)


_CHIP_FLIPS = ((1, 0, 0), (0, 1, 0), (1, 1, 0))
_ALL_FLIPS = ((0, 0, 1), (0, 1, 0), (0, 1, 1), (1, 0, 0), (1, 0, 1), (1, 1, 0), (1, 1, 1))


def _whole(ref, pos):
    return ref.shape if pos is None else ref


def _chip_block(ref, pos):
    return ref.shape[1:] if pos is None else ref.at[2 * pos[0] + pos[1]]


def _chip_slot(p):
    return 2 * p[0] + p[1]


def gather_chips(name, arrs):
    return flip_exchange(name, arrs, _CHIP_FLIPS, 4, _chip_slot, _whole, True)


def scatter_chips(name, arrs):
    return flip_exchange(name, arrs, _CHIP_FLIPS, 4, _chip_slot, _chip_block, True)


_HBM = pl.BlockSpec(memory_space=pltpu.HBM)
_SEM = pl.BlockSpec(memory_space=pltpu.SEMAPHORE)
_DATAFLOW = pltpu.SideEffectType.DATAFLOW_SIDE_EFFECTING


def _split_copies(mode, refs, n, send, recv):
    me = (lax.axis_index("x"), lax.axis_index("y"), lax.axis_index("c"))
    sib = (me[0], me[1], 1 - me[2])
    lands = refs[:n] if mode == "handover" else refs[n:2 * n]
    copies = []
    for k, land in enumerate(lands):
        half = land.shape[1] // 2
        mine = pl.ds(pl.multiple_of(me[2] * half, 16), half)
        for j, fl in enumerate(_CHIP_FLIPS):
            peer = tuple(1 - m if f else m for m, f in zip(me, fl))
            if mode == "gather":
                src, dst, to = refs[k], land.at[_chip_slot(me)], peer
            elif mode == "scatter":
                src, dst, to = refs[k].at[_chip_slot(peer)], land.at[_chip_slot(me)], peer
            elif mode == "gather_half":
                src, dst, to = refs[k].at[mine], land.at[_chip_slot(me), mine], peer
            else:
                src = dst = land.at[_chip_slot(peer), mine]
                to = sib
            q = k * len(_CHIP_FLIPS) + j
            copies.append(pltpu.make_async_remote_copy(src_ref=src, dst_ref=dst, send_sem=send.at[q],
                                                       recv_sem=recv.at[q], device_id=to, device_id_type=MESH))
    return copies


def split_start(name, mode, ops, n):
    ops = [pltpu.with_memory_space_constraint(a, pltpu.HBM) for a in ops]
    m = len(ops)

    def body(*refs):
        for cp in _split_copies(mode, refs[:m], n, refs[m], refs[m + 1]):
            cp.start()
        refs[-1][...] = jnp.zeros_like(refs[-1])

    sems = pltpu.SemaphoreType.DMA((n * len(_CHIP_FLIPS),))
    outs = pl.pallas_call(
        body, name=name, in_specs=[_HBM] * m,
        out_shape=(sems, sems, *[pltpu.HBM(a.shape, a.dtype) for a in ops], jax.ShapeDtypeStruct((8, LANES), f32)),
        out_specs=(_SEM, _SEM, *[_HBM] * m, pl.BlockSpec(memory_space=pltpu.VMEM)),
        input_output_aliases={i: 2 + i for i in range(m)},
        compiler_params=pltpu.CompilerParams(has_side_effects=_DATAFLOW))(*ops)
    return (outs[0], outs[1], list(outs[2:2 + m]), mode, n), outs[-1][0, 0]


def split_wait(name, state, after):
    send, recv, ops, mode, n = state
    m = len(ops)
    afters = list(after) if isinstance(after, (list, tuple)) else [after]

    def body(*refs):
        for cp in _split_copies(mode, refs[:m], n, refs[m], refs[m + 1]):
            cp.wait_send()
            cp.wait_recv()

    outs = pl.pallas_call(
        body, name=name, in_specs=[_HBM] * m + [_SEM, _SEM] + [pl.BlockSpec(memory_space=pl.ANY)] * len(afters),
        out_shape=tuple(pltpu.HBM(a.shape, a.dtype) for a in ops), out_specs=tuple([_HBM] * m),
        input_output_aliases={i: i for i in range(m)},
        compiler_params=pltpu.CompilerParams(has_side_effects=_DATAFLOW))(*ops, send, recv, *afters)
    return list(outs[m - n:])


def chips_start(name, arrs, src_of, halves=False):
    me = _chip_slot((lax.axis_index("x"), lax.axis_index("y")))
    lands = []
    for a in arrs:
        blk = tuple(src_of(jax.ShapeDtypeStruct(a.shape, a.dtype), None))
        own = a if src_of is _whole else lax.dynamic_index_in_dim(a, me, 0, keepdims=False)
        lands.append(lax.dynamic_update_index_in_dim(lax.empty((4,) + blk, a.dtype), own, me, 0))
    mode = "scatter" if src_of is _chip_block else ("gather_half" if halves else "gather")
    return split_start(name, mode, list(arrs) + lands, len(arrs))


chips_wait = split_wait


def handover_start(name, lands):
    return split_start(name, "handover", lands, len(lands))


def gather_chips_halves(name, arrs):
    n = len(arrs)
    nf = len(_CHIP_FLIPS)
    split = [a.shape[0] % 32 == 0 for a in arrs]

    def body(*refs):
        ins, outs = refs[:n], refs[n:2 * n]
        send1, recv1, send2, recv2, lsem = refs[2 * n:]
        me = (lax.axis_index("x"), lax.axis_index("y"), lax.axis_index("c"))
        sib = (me[0], me[1], 1 - me[2])
        peers = [tuple(1 - m if f else m for m, f in zip(me, fl)) for fl in _CHIP_FLIPS]
        local, first, second = [], [], []
        for k in range(n):
            cp = pltpu.make_async_copy(ins[k], outs[k].at[_chip_slot(me)], lsem.at[k])
            cp.start()
            local.append(cp)
            half = ins[k].shape[0] // 2
            rows = pl.ds(pl.multiple_of(me[2] * half, 16), half) if split[k] else pl.ds(0, ins[k].shape[0])
            for j, peer in enumerate(peers):
                cp = pltpu.make_async_remote_copy(
                    src_ref=ins[k].at[rows], dst_ref=outs[k].at[_chip_slot(me), rows], send_sem=send1.at[k, j],
                    recv_sem=recv1.at[k, j], device_id=peer, device_id_type=MESH)
                cp.start()
                first.append((k, j, rows, cp))
        for k, j, rows, cp in first:
            cp.wait_recv()
            if split[k]:
                got = outs[k].at[_chip_slot(peers[j]), rows]
                fwd = pltpu.make_async_remote_copy(src_ref=got, dst_ref=got, send_sem=send2.at[k, j],
                                                   recv_sem=recv2.at[k, j], device_id=sib, device_id_type=MESH)
                fwd.start()
                second.append(fwd)
        for _, _, _, cp in first:
            cp.wait_send()
        for cp in second:
            cp.wait()
        for cp in local:
            cp.wait()

    any_spec = pl.BlockSpec(memory_space=pl.ANY)
    sems = pltpu.SemaphoreType.DMA((n, nf))
    return pl.pallas_call(
        body, name=name, in_specs=[any_spec] * n, out_specs=[any_spec] * n,
        out_shape=[jax.ShapeDtypeStruct((4,) + a.shape, a.dtype) for a in arrs],
        scratch_shapes=[sems, sems, sems, sems, pltpu.SemaphoreType.DMA((n,))],
        compiler_params=pltpu.CompilerParams(has_side_effects=True))(*arrs)


def swap_sibling(name, arrs):
    outs = flip_exchange(name, arrs, ((0, 0, 1),), 1, lambda p: 0, _whole, False)
    return [o[0] for o in outs]


def gather_all(arrs, after=()):
    return flip_exchange("gather_all", arrs, _ALL_FLIPS, 8, lambda p: 4 * p[0] + 2 * p[1] + p[2], _whole, True,
                         after=after)


def _row_tile(nr, nc, n_arrays):
    budget = (20 << 20) // (n_arrays * 2 * 4 * max(nc, LANES))
    t = min(nr, budget) // 16 * 16
    while t > 0 and nr % t:
        t -= 16
    return t if t > 0 else nr


def sum_slots(name, x):
    ns, nr, nc = x.shape
    tile = _row_tile(nr, nc, ns + 1)

    def body(x_ref, o_ref):
        s = x_ref[0].astype(f32)
        for q in range(1, ns):
            s = s + x_ref[q].astype(f32)
        o_ref[...] = s.astype(o_ref.dtype)

    return pl.pallas_call(
        body, name=name, grid=(nr // tile,), in_specs=[pl.BlockSpec((ns, tile, nc), lambda i: (0, i, 0))],
        out_specs=pl.BlockSpec((tile, nc), lambda i: (i, 0)), out_shape=jax.ShapeDtypeStruct((nr, nc), x.dtype),
        compiler_params=_params(1))(x)


def adamw(name, w, g_parts, m, v):
    nr, nc = w.shape[-2:]
    n_g = len(g_parts)
    tile = _row_tile(nr, nc, 7 + n_g)

    def body(*refs):
        w_ref, m_ref, v_ref = refs[:3]
        g = refs[3][...].astype(f32)
        for r in refs[4:3 + n_g]:
            g = g + r[...].astype(f32)
        g_ref, d_ref, nm_ref, nv_ref = refs[3 + n_g:]
        nm = ADAM_B1 * m_ref[...] + (1.0 - ADAM_B1) * g
        nv = ADAM_B2 * v_ref[...] + (1.0 - ADAM_B2) * jnp.square(g)
        m_hat = nm / (1.0 - ADAM_B1 ** ADAM_STEP)
        v_hat = nv / (1.0 - ADAM_B2 ** ADAM_STEP)
        g_ref[...] = g
        d_ref[...] = -ADAM_LR * (m_hat / (jnp.sqrt(v_hat) + ADAM_EPS) + ADAM_WD * w_ref[...])
        nm_ref[...] = nm
        nv_ref[...] = nv

    spec = pl.BlockSpec((tile, nc), lambda i: (i, 0))
    spec3 = pl.BlockSpec((None, tile, nc), lambda i: (0, i, 0)) if w.ndim == 3 else spec
    return pl.pallas_call(
        body, name=name, grid=(nr // tile,), in_specs=[spec3] * 3 + [spec] * n_g, out_specs=[spec3] * 4,
        out_shape=[jax.ShapeDtypeStruct(w.shape, f32)] * 4, compiler_params=_params(1))(w, m, v, *g_parts)


def adamw_packed(w, g8, m, v):
    nr, nc = w.shape

    def body(w_ref, g_ref, m_ref, v_ref, go_ref, d_ref, nm_ref, nv_ref):
        g = g_ref[0]
        for q in range(1, 8):
            g = g + g_ref[q]
        nm = ADAM_B1 * m_ref[...] + (1.0 - ADAM_B1) * g
        nv = ADAM_B2 * v_ref[...] + (1.0 - ADAM_B2) * jnp.square(g)
        m_hat = nm / (1.0 - ADAM_B1 ** ADAM_STEP)
        v_hat = nv / (1.0 - ADAM_B2 ** ADAM_STEP)
        go_ref[...] = g
        d_ref[...] = -ADAM_LR * (m_hat / (jnp.sqrt(v_hat) + ADAM_EPS) + ADAM_WD * w_ref[...])
        nm_ref[...] = nm
        nv_ref[...] = nv

    return pl.pallas_call(body, name="adamw_packed", out_shape=[jax.ShapeDtypeStruct((nr, nc), f32)] * 4,
                          compiler_params=pltpu.CompilerParams(vmem_limit_bytes=VMEM_LIMIT))(w, g8, m, v)


def _pack(vectors):
    rows = []
    for a in vectors:
        flat = a.reshape(-1).astype(f32)
        pad = (-flat.shape[0]) % LANES
        rows.append(jnp.pad(flat, (0, pad)).reshape(-1, LANES))
    packed = jnp.concatenate(rows, axis=0)
    return jnp.pad(packed, ((0, (-packed.shape[0]) % 8), (0, 0)))


def _unpack(packed, like):
    out, r = [], 0
    for a in like:
        n = a.size
        nr = -(-n // LANES)
        out.append(packed[r:r + nr].reshape(-1)[:n].reshape(a.shape))
        r += nr
    return out


def kernel(x, mem, mix_norm_w, w_in, dn_conv_w, dn_a_log, dn_dt_bias, dn_norm_w, rw_mu, rw_w0, rw_w2, rw_a0, rw_a2, rw_g2, rw_k_k, rw_k_a, rw_r_k, rw_ln_w, rw_ln_b, w_out, xa_norm_w, mem_norm_w, xa_wq, xa_wk, xa_wv, xa_wo, ffn_norm_w, ffn_w1, ffn_w2, final_norm_w, loss_target, m_mix_norm_w, m_w_in, m_dn_conv_w, m_dn_a_log, m_dn_dt_bias, m_dn_norm_w, m_rw_mu, m_rw_w0, m_rw_w2, m_rw_a0, m_rw_a2, m_rw_g2, m_rw_k_k, m_rw_k_a, m_rw_r_k, m_rw_ln_w, m_rw_ln_b, m_w_out, m_xa_norm_w, m_mem_norm_w, m_xa_wq, m_xa_wk, m_xa_wv, m_xa_wo, m_ffn_norm_w, m_ffn_w1, m_ffn_w2, m_final_norm_w, v_mix_norm_w, v_w_in, v_dn_conv_w, v_dn_a_log, v_dn_dt_bias, v_dn_norm_w, v_rw_mu, v_rw_w0, v_rw_w2, v_rw_a0, v_rw_a2, v_rw_g2, v_rw_k_k, v_rw_k_a, v_rw_r_k, v_rw_ln_w, v_rw_ln_b, v_w_out, v_xa_norm_w, v_mem_norm_w, v_xa_wq, v_xa_wk, v_xa_wv, v_xa_wo, v_ffn_norm_w, v_ffn_w1, v_ffn_w2, v_final_norm_w):
    weights = dict(mix_norm_w=mix_norm_w, w_in=w_in, dn_conv_w=dn_conv_w, dn_a_log=dn_a_log, dn_dt_bias=dn_dt_bias, dn_norm_w=dn_norm_w, rw_mu=rw_mu, rw_w0=rw_w0, rw_w2=rw_w2, rw_a0=rw_a0, rw_a2=rw_a2, rw_g2=rw_g2, rw_k_k=rw_k_k, rw_k_a=rw_k_a, rw_r_k=rw_r_k, rw_ln_w=rw_ln_w, rw_ln_b=rw_ln_b, w_out=w_out, xa_norm_w=xa_norm_w, mem_norm_w=mem_norm_w, xa_wq=xa_wq, xa_wk=xa_wk, xa_wv=xa_wv, xa_wo=xa_wo, ffn_norm_w=ffn_norm_w, ffn_w1=ffn_w1, ffn_w2=ffn_w2, final_norm_w=final_norm_w)
    mom_m = dict(mix_norm_w=m_mix_norm_w, w_in=m_w_in, dn_conv_w=m_dn_conv_w, dn_a_log=m_dn_a_log, dn_dt_bias=m_dn_dt_bias, dn_norm_w=m_dn_norm_w, rw_mu=m_rw_mu, rw_w0=m_rw_w0, rw_w2=m_rw_w2, rw_a0=m_rw_a0, rw_a2=m_rw_a2, rw_g2=m_rw_g2, rw_k_k=m_rw_k_k, rw_k_a=m_rw_k_a, rw_r_k=m_rw_r_k, rw_ln_w=m_rw_ln_w, rw_ln_b=m_rw_ln_b, w_out=m_w_out, xa_norm_w=m_xa_norm_w, mem_norm_w=m_mem_norm_w, xa_wq=m_xa_wq, xa_wk=m_xa_wk, xa_wv=m_xa_wv, xa_wo=m_xa_wo, ffn_norm_w=m_ffn_norm_w, ffn_w1=m_ffn_w1, ffn_w2=m_ffn_w2, final_norm_w=m_final_norm_w)
    mom_v = dict(mix_norm_w=v_mix_norm_w, w_in=v_w_in, dn_conv_w=v_dn_conv_w, dn_a_log=v_dn_a_log, dn_dt_bias=v_dn_dt_bias, dn_norm_w=v_dn_norm_w, rw_mu=v_rw_mu, rw_w0=v_rw_w0, rw_w2=v_rw_w2, rw_a0=v_rw_a0, rw_a2=v_rw_a2, rw_g2=v_rw_g2, rw_k_k=v_rw_k_k, rw_k_a=v_rw_k_a, rw_r_k=v_rw_r_k, rw_ln_w=v_rw_ln_w, rw_ln_b=v_rw_ln_b, w_out=v_w_out, xa_norm_w=v_xa_norm_w, mem_norm_w=v_mem_norm_w, xa_wq=v_xa_wq, xa_wk=v_xa_wk, xa_wv=v_xa_wv, xa_wo=v_xa_wo, ffn_norm_w=v_ffn_norm_w, ffn_w1=v_ffn_w1, ffn_w2=v_ffn_w2, final_norm_w=v_final_norm_w)
    names = list(weights)

    seq, d = x.shape[1], x.shape[2]
    dnw = d // 2
    rww = d - dnw
    nh, nb = dnw // LANES, rww // LANES
    n_mem = mem.shape[1]
    lw_dim, la_dim, lg_dim = rw_w2.shape[1], rw_a2.shape[1], rw_g2.shape[1]
    assert lw_dim + la_dim == LANES and lg_dim == LANES and dnw % LANES == 0 and rww % LANES == 0
    xs, mems, tgt = x[0], mem[0], loss_target[0]

    col_sharded = ("w_in", "xa_wo", "ffn_w1", "dn_conv_w", "rw_w2", "rw_a2", "rw_g2")
    row_sharded = ("w_out", "xa_wq", "xa_wk", "xa_wv", "ffn_w2")
    f32_payload = ("dn_conv_w", "rw_w2", "rw_a2", "rw_g2")
    sharded = col_sharded + row_sharded
    payload = {n: weights[n][0].astype(f32 if n in f32_payload else bf16) for n in sharded}
    shard_w = w_in.shape[2]
    pad_w = -(-shard_w // LANES) * LANES
    payload["w_in"] = jnp.pad(payload["w_in"], ((0, 0), (0, pad_w - shard_w)))
    first = ("w_in", "dn_conv_w", "rw_w2", "rw_a2", "rw_g2")
    mid = ("w_out", "xa_wq", "xa_wk", "xa_wv", "xa_wo")
    late = ("ffn_w1", "ffn_w2")
    gathered = dict(zip(first, gather_chips_halves("gather_first", [payload[n] for n in first])))
    ordered = lax.optimization_barrier(([gathered[n] for n in first], [payload[n] for n in mid + late]))
    gathered = dict(zip(first, ordered[0]))
    payload.update(zip(mid + late, ordered[1]))
    mid_state, tok_mid = chips_start("gather_mid_start", [payload[n] for n in mid], _whole, halves=True)
    late_state, tok_late = chips_start("gather_late_start", [payload[n] for n in late], _whole, halves=True)
    mix_norm_w_in = mix_norm_w + (tok_mid + tok_late)

    def full(n):
        g = gathered[n]
        if n in col_sharded:
            return g.transpose(1, 0, 2).reshape(g.shape[1], 4 * g.shape[2])
        return g.reshape(4 * g.shape[1], g.shape[2])

    c_rw0 = 4 * dnw + 2 * nh
    c_rw = 3 * rww
    eb_ab, eb_r = 4 * nh, 4 * nh + 1
    eb_l1 = eb_r + 3 * nb
    n_ext = -(-(eb_l1 + 2) // 4) * 4
    tbl_fwd, tbl_bwd, per_shard = w_in_layout(4 * shard_w, shard_w, c_rw0, LANES - 2 * nh, n_ext)
    w_ext = lane_regroup(
        "w_in_regroup", gathered["w_in"], tbl_fwd,
        lambda p: pl.BlockSpec((None, d, LANES),
                               lambda i, j, t: (t[REGROUP_FIELDS * p, j], 0, t[REGROUP_FIELDS * p + 1, j])),
        pl.BlockSpec((d, LANES), lambda i, j, t: (0, j)), (d, n_ext * LANES), (1, n_ext), pad_w)
    conv_w = full("dn_conv_w")
    w2p = jnp.concatenate([full("rw_w2"), jnp.zeros((la_dim, rww), f32)], axis=0)
    a2p = jnp.concatenate([jnp.zeros((lw_dim, rww), f32), full("rw_a2")], axis=0)
    g2 = full("rw_g2")
    xaw = xa_wq.shape[2]
    nxh = xaw // LANES
    ffn = 4 * ffn_w1.shape[2]

    def lane_row(vec):
        return jnp.pad(vec.reshape(1, -1), ((0, 0), (0, LANES - vec.size)))

    alog_row, dtb_row = lane_row(dn_a_log), lane_row(dn_dt_bias)
    head_of_col = jnp.arange(dnw)[None, :] // LANES
    e_g = (jnp.arange(LANES)[:, None] == head_of_col).astype(f32)
    e_b = (jnp.arange(LANES)[:, None] == head_of_col + nh).astype(f32)
    mu_main, mu_small = rw_mu[:, :c_rw], rw_mu[:, c_rw:]
    r_k_row = rw_r_k.reshape(1, rww)
    fnw = final_norm_w.reshape(1, d)
    qb, kb_, vb, zb = 0, nh, 2 * nh, 3 * nh
    rb0 = eb_r
    cc = lambda w_, o_: ("constc", w_, o_)
    rc = lambda o_: ("rowc", LANES, o_)

    (u16,) = rowcall("norm_mix", fn_rms, [(xs, "row"), (mix_norm_w_in, "const")], [(seq, d, bf16, "row")], rows=seq)
    p_main = p_small = mm("in_proj", u16, w_ext)

    fn_gconv = make_fn_gconv(2 * nh)
    (qkv,) = rowcall("gdn_conv", fn_gconv, [(p_main, rc(0)), (conv_w, cc(LANES, 0))],
                     [(seq, 3 * dnw, f32, "rowc")], rows=seq, tile=seq, ncol=3 * nh)
    gate_ins = [(p_small, rc(eb_ab)), (alog_row, "const"), (dtb_row, "const"), (e_g, "const"), (e_b, "const")]
    g_b, beta_b = rowcall("gdn_gate", fn_ggate, gate_ins, [(seq, dnw, f32, "row")] * 2, rows=seq)
    gdn_ins = [(qkv, qb), (qkv, kb_), (qkv, vb), (g_b, 0), (beta_b, 0)]
    o_raw, gdn_states = scan_fwd("gdn_scan", gdn_chunk, gdn_ins, rows=seq, chunk=GDN_CHUNK, ncol=nh)
    mid_state, tok = handover_start("gather_mid_pass", chips_wait("gather_mid_wait", mid_state, o_raw))
    gpost_ins = [(o_raw, rc(0)), (p_main, rc(zb)), (dn_norm_w + tok, "const")]
    (o_dn,) = rowcall("gdn_post", fn_gpost, gpost_ins, [(seq, dnw, bf16, "rowc")], rows=seq, ncol=nh)

    (prw,) = rowcall("rw_lerp_main", fn_lerp, [(p_main, rc(rb0)), (mu_main, cc(LANES, 0))],
                     [(seq, c_rw, f32, "rowc")], rows=seq, tile=seq, ncol=3 * nb)
    (psl,) = rowcall("rw_lerp_small", fn_lerp, [(p_small, rc(eb_l1)), (mu_small, cc(LANES, 0))],
                     [(seq, 2 * LANES, f32, "rowc")], rows=seq, tile=seq, ncol=2)
    rprep_ins = [(prw, rc(nb)), (psl, "row"), (rw_w0, cc(LANES, 0)), (rw_a0, cc(LANES, 0)), (rw_k_k, cc(LANES, 0)),
                 (rw_k_a, cc(LANES, 0)), (w2p, cc(LANES, 0)), (a2p, cc(LANES, 0)), (g2, cc(LANES, 0))]
    lw, kmod, kk, a_rw, gate = rowcall("rw_prep", fn_rprep, rprep_ins, [(seq, rww, f32, "rowc")] * 5,
                                        rows=seq, ncol=nb)
    wkv_ins = [(prw, 0), (lw, 0), (kmod, 0), (prw, 2 * nb), (kk, 0), (a_rw, 0)]
    y_rw, wkv_states = scan_fwd("wkv_scan", wkv_chunk, wkv_ins, rows=seq, chunk=WKV_CHUNK, ncol=nb)
    rpost_ins = [(y_rw, rc(0)), (prw, rc(0)), (kmod, rc(0)), (prw, rc(2 * nb)), (gate, rc(0)),
                 (r_k_row, cc(LANES, 0)), (rw_ln_w, cc(LANES, 0)), (rw_ln_b, cc(LANES, 0))]
    (o_rw,) = rowcall("rw_post", fn_rpost, rpost_ins, [(seq, rww, bf16, "rowc")], rows=seq, ncol=nb)

    o_cat = jnp.concatenate([o_dn, o_rw], axis=1)
    late_state, tok = handover_start("gather_late_pass", chips_wait("gather_late_wait", late_state, o_cat))
    gathered.update(zip(mid, chips_wait("gather_mid_pass_wait", mid_state, o_cat)))
    w_out_f, wq_f, wk_f, wv_f, wo_f = full("w_out"), full("xa_wq"), full("xa_wk"), full("xa_wv"), full("xa_wo")
    h1 = mm("out_proj", o_cat, w_out_f, add=xs)

    (hn16,) = rowcall("norm_xa", fn_rms, [(h1, "row"), (xa_norm_w + tok, "const")], [(seq, d, bf16, "row")],
                      rows=seq)
    (mn16,) = rowcall("norm_mem", fn_rms, [(mems, "row"), (mem_norm_w, "const")], [(n_mem, d, bf16, "row")],
                      rows=n_mem)
    q_xa = mm("xa_q", hn16, wq_f)
    k_xa = mm("xa_k", mn16, wk_f)
    v_xa = mm("xa_v", mn16, wv_f)
    xcore_ins = [(q_xa, rc(0)), (k_xa, cc(LANES, 0)), (v_xa, cc(LANES, 0))]
    (o_xa,) = rowcall("xa_core", fn_xcore, xcore_ins, [(seq, xaw, bf16, "rowc")], rows=seq, ncol=nxh)
    h2 = mm("xa_o", o_xa, wo_f, add=h1)

    (fn16,) = rowcall("norm_ffn", fn_rms, [(h2, "row"), (ffn_norm_w, "const")], [(seq, d, bf16, "row")], rows=seq)
    gathered.update(zip(late, chips_wait("gather_late_pass_wait", late_state, fn16)))
    w1_f, w2_f = full("ffn_w1"), full("ffn_w2")
    a1_16, hid16 = mm("ffn_up", fn16, w1_f, epilogue=lambda r: (r, jnp.square(jnp.maximum(r, 0.0))),
                      out_dtypes=(bf16, bf16))
    h3 = mm("ffn_down", hid16, w2_f, add=h2)

    dh3, dh3_16, d_fnw, loss_rows = rowcall(
        "loss_head", fn_final, [(h3, "row"), (tgt, "row"), (fnw, "const")],
        [(seq, d, f32, "row"), (seq, d, bf16, "row"), (1, d, f32, "acc"), (8, LANES, f32, "acc")], rows=seq)

    da1_16 = mm("ffn_down_dx", dh3_16, w2_f, tb=True, extra=[a1_16], out_dtype=bf16,
                epilogue=lambda r, a1: (r * (2.0 * jnp.maximum(a1.astype(f32), 0.0)),))
    def by_chip(n, g):
        if g.ndim == 3:
            return g
        if n in col_sharded:
            return g.reshape(g.shape[0], 4, g.shape[1] // 4).transpose(1, 0, 2)
        return g.reshape(4, g.shape[0] // 4, g.shape[1])

    g_ffn_w2 = mm("ffn_down_dw", hid16, dh3_16, ta=True, out_dtype=bf16)
    ffn_w2_g, tok = chips_start("scatter_ffn_w2_start", [by_chip("ffn_w2", g_ffn_w2)], _chip_block)
    g_ffn_w1 = mm("ffn_up_dw", fn16, da1_16, ta=True, out_dtype=bf16, by_chip=True, after=tok)
    ffn_w1_g, tok = chips_start("scatter_ffn_w1_start", [g_ffn_w1], _chip_block)
    dfn = mm("ffn_up_dx", da1_16, w1_f, tb=True, after=tok)
    dh2, d_ffn_nw, dh2_16 = rowvjp("norm_ffn_bwd", fn_rms, [(h2, "row"), (ffn_norm_w, "const")],
                                   [[(dfn, "row")]], [0, 1], rows=seq, adds=[(0, dh3, "row")], dup16=[0])

    do_xa = mm("xa_o_dx", dh2_16, wo_f, tb=True)
    g_xa_wo = mm("xa_o_dw", o_xa, dh2_16, ta=True, out_dtype=bf16, by_chip=True)
    dq_xa, dk_xa, dv_xa, dq16 = rowvjp("xa_core_bwd", fn_xcore, xcore_ins, [[(do_xa, rc(0))]], [0, 1, 2],
                                       rows=seq, ncol=nxh, dup16=[0])
    g_xa_wq = mm("xa_q_dw", hn16, dq16, ta=True, out_dtype=bf16)
    dhn = mm("xa_q_dx", dq16, wq_f, tb=True)
    dh1, d_xa_nw, dh1_16 = rowvjp("norm_xa_bwd", fn_rms, [(h1, "row"), (xa_norm_w, "const")], [[(dhn, "row")]],
                                  [0, 1], rows=seq, adds=[(0, dh2, "row")], dup16=[0])
    dk16, dv16 = dk_xa.astype(bf16), dv_xa.astype(bf16)
    g_xa_wk = mm("xa_k_dw", mn16, dk16, ta=True, out_dtype=bf16)
    g_xa_wv = mm("xa_v_dw", mn16, dv16, ta=True, out_dtype=bf16)
    dmn = mm("xa_v_dx", dv16, wv_f, tb=True, add=mm("xa_k_dx", dk16, wk_f, tb=True))
    (d_mem_nw,) = rowvjp("norm_mem_bwd", fn_rms, [(mems, "row"), (mem_norm_w, "const")], [[(dmn, "row")]], [1],
                         rows=n_mem)

    g_w_out = mm("out_proj_dw", o_cat, dh1_16, ta=True, out_dtype=bf16)
    mid_grads = dict(w_out=g_w_out, xa_wq=g_xa_wq, xa_wk=g_xa_wk, xa_wv=g_xa_wv, xa_wo=g_xa_wo)
    mid_g, tok = chips_start("scatter_mid_start", [by_chip(n, mid_grads[n]) for n in mid], _chip_block)
    do_cat = mm("out_proj_dx", dh1_16, w_out_f, tb=True, after=tok)

    dy, dr_a, dkmod_a, dv_a, dgate, d_r_k, d_ln_w, d_ln_b = rowvjp(
        "rw_post_bwd", fn_rpost, rpost_ins, [[(do_cat, rc(nh))]], [0, 1, 2, 3, 4, 5, 6, 7], rows=seq, ncol=nb)
    dr_b, dlw, dkmod_b, dv_b, dkk, da_rw = scan_bwd("wkv_scan_bwd", wkv_chunk, wkv_ins, wkv_states, dy, 0,
                                                    rows=seq, chunk=WKV_CHUNK, ncol=nb)
    rprep_cts = [[(dlw, rc(0))], [(dkmod_a, rc(0)), (dkmod_b, rc(0))], [(dkk, rc(0))], [(da_rw, rc(0))],
                 [(dgate, rc(0))]]
    dpk, dpsl_parts, d_w0, d_a0, d_k_k, d_k_a, d_w2p, d_a2p, d_g2 = rowvjp(
        "rw_prep_bwd", fn_rprep, rprep_ins, rprep_cts, [0, 1, 2, 3, 4, 5, 6, 7, 8], rows=seq, ncol=nb)
    (dpsl,) = rowcall("rw_prep_sum", fn_sumcols(nb), [(dpsl_parts, "row")], [(seq, 2 * LANES, f32, "row")], rows=seq)

    def lerp_bwd(tag, p, p_off, mu, mu_off, ct_lists, ncol):
        return rowvjp("rw_lerp_bwd_" + tag, fn_lerp, [(p, rc(p_off)), (mu, cc(LANES, mu_off))], [ct_lists], [0, 1],
                      rows=seq, tile=seq, ncol=ncol, dup16=[0])

    _, dmu_r, dpr16 = lerp_bwd("r", p_main, rb0, mu_main, 0, [(dr_a, rc(0)), (dr_b, rc(0))], nb)
    _, dmu_k, dpk16 = lerp_bwd("k", p_main, rb0 + nb, mu_main, nb, [(dpk, rc(0))], nb)
    _, dmu_v, dpv16 = lerp_bwd("v", p_main, rb0 + 2 * nb, mu_main, 2 * nb, [(dv_a, rc(0)), (dv_b, rc(0))], nb)
    _, dmu_s, dps12_16 = lerp_bwd("small", p_small, eb_l1, mu_small, 0, [(dpsl, rc(0))], 2)

    do_raw, dz, d_dn_nw, dz16 = rowvjp("gdn_post_bwd", fn_gpost, gpost_ins, [[(do_cat, rc(0))]], [0, 1, 2],
                                       rows=seq, ncol=nh, dup16=[1])
    dq_g, dk_g, dv_g, dg_b, dbeta_b = scan_bwd("gdn_scan_bwd", gdn_chunk, gdn_ins, gdn_states, do_raw, 0,
                                               rows=seq, chunk=GDN_CHUNK, ncol=nh)
    dps0, d_alog, d_dtb, dps0_16 = rowvjp("gdn_gate_bwd", fn_ggate, gate_ins, [[(dg_b, "row")], [(dbeta_b, "row")]],
                                          [0, 1, 2], rows=seq, dup16=[0])
    dqkv = jnp.concatenate([dq_g, dk_g, dv_g], axis=1)
    _, d_conv_w, dqkv16 = rowvjp("gdn_conv_bwd", fn_gconv, [(p_main, rc(0)), (conv_w, cc(LANES, 0))],
                                 [[(dqkv, rc(0))]], [0, 1], rows=seq, tile=seq, ncol=3 * nh, dup16=[0])

    dp16 = jnp.concatenate([dqkv16, dz16, dps0_16, dpr16, dpk16, dpv16, dps12_16,
                            jnp.zeros((seq, (n_ext - eb_l1 - 2) * LANES), bf16)], axis=1)
    g_w_ext = mm("in_proj_dw", u16, dp16, ta=True, out_dtype=bf16)
    g_w_in = lane_regroup(
        "w_in_grad_regroup", g_w_ext, tbl_bwd,
        lambda p: pl.BlockSpec((d, LANES), lambda i, j, t: (0, t[REGROUP_FIELDS * p + 1, j])),
        pl.BlockSpec((None, d, LANES), lambda i, j, t: (j // per_shard, 0, j % per_shard)),
        (4, d, pad_w), (1, 4 * per_shard), n_ext * LANES)
    first_grads = dict(w_in=g_w_in, dn_conv_w=d_conv_w, rw_w2=d_w2p[:lw_dim], rw_a2=d_a2p[lw_dim:], rw_g2=d_g2)
    first_g, tok = chips_start("scatter_first_start", [by_chip(n, first_grads[n]) for n in first], _chip_block)

    du = mm("in_proj_dx", dp16, w_ext, tb=True, after=tok)
    grad_x, d_mix_nw = rowvjp("norm_mix_bwd", fn_rms, [(xs, "row"), (mix_norm_w, "const")], [[(du, "row")]],
                              [0, 1], rows=seq, adds=[(0, dh1, "row")])
    received = dict(zip(mid, chips_wait("scatter_mid_wait", mid_g, grad_x)))
    received["ffn_w2"] = chips_wait("scatter_ffn_w2_wait", ffn_w2_g, grad_x)[0]
    received["ffn_w1"] = chips_wait("scatter_ffn_w1_wait", ffn_w1_g, grad_x)[0]

    out = {}

    def reduce_and_update(tag, group):
        partial = [sum_slots("sum_chips_" + n, received[n]) for n in group]
        other = swap_sibling("swap_sibling_" + tag, partial)
        for n, p_mine, p_other in zip(group, partial, other):
            if weights[n].shape[2] % LANES:
                rows, cols = weights[n].shape[1:]
                lin = lambda a: jnp.swapaxes(a, 1, 2).reshape(-1, LANES)
                lin_g = lambda p: p.T[:cols].reshape(-1, LANES)
                res = adamw("adamw_" + n, lin(weights[n]), [lin_g(p_mine), lin_g(p_other)], lin(mom_m[n]),
                            lin(mom_v[n]))
                out[n] = [jnp.swapaxes(r.reshape(1, cols, rows), 1, 2) for r in res]
            else:
                out[n] = adamw("adamw_" + n, weights[n], [p_mine, p_other], mom_m[n], mom_v[n])

    reduce_and_update("rest", mid + late)
    received.update(zip(first, chips_wait("scatter_first_wait", first_g, [out[n][1] for n in mid + late])))
    reduce_and_update("first", first)

    small_names = [n for n in names if n not in sharded]
    small_local = dict(
        mix_norm_w=d_mix_nw, dn_a_log=d_alog[:, :nh], dn_dt_bias=d_dtb[:, :nh], dn_norm_w=d_dn_nw,
        rw_mu=jnp.concatenate([dmu_r, dmu_k, dmu_v, dmu_s], axis=1), rw_w0=d_w0, rw_a0=d_a0, rw_k_k=d_k_k,
        rw_k_a=d_k_a, rw_r_k=d_r_k, rw_ln_w=d_ln_w, rw_ln_b=d_ln_b, xa_norm_w=d_xa_nw, mem_norm_w=d_mem_nw,
        ffn_norm_w=d_ffn_nw, final_norm_w=d_fnw)
    loss_vec = jnp.where(jnp.arange(LANES) == 0, loss_rows[0], 0.0)
    (g8,) = gather_all([_pack([small_local[n] for n in small_names] + [loss_vec])], after=[out["w_in"][1]])

    packed_like = [weights[n] for n in small_names] + [loss_vec]
    zero = jnp.zeros((LANES,), f32)
    res = adamw_packed(_pack([weights[n] for n in small_names] + [zero]), g8,
                       _pack([mom_m[n] for n in small_names] + [zero]),
                       _pack([mom_v[n] for n in small_names] + [zero]))
    unpacked = [_unpack(r, packed_like) for r in res]
    for i, n in enumerate(small_names):
        out[n] = [u[i] for u in unpacked]
    loss = unpacked[0][-1][0]

    return (loss, grad_x.reshape(x.shape), *[out[n][0] for n in names], *[out[n][1] for n in names],
            *[out[n][2] for n in names], *[out[n][3] for n in names])
```

```python
import functools

import jax
import jax.numpy as jnp
from jax import lax
from jax.experimental import pallas as pl
from jax.experimental.pallas import tpu as pltpu

f32 = jnp.float32
bf16 = jnp.bfloat16
HI = lax.Precision.HIGHEST
MESH = pl.DeviceIdType.MESH

LANES = 128
VMEM_LIMIT = 56 << 20
TOK_TILE = 256
TOK_TILE_BLOCKED = 1024
MM_TILE = 1024
MM_TILE_K = 2048
GDN_CHUNK = 128
WKV_CHUNK = 64
SCAN_GROUP = 8
P_BULK = 1
P_INV = 1
P_RESID = 3
P_CUMSUM = 3
RMS_EPS = 1e-6
RW_GN_EPS = 64e-5
RW_HEAD = 64

ADAM_LR, ADAM_B1, ADAM_B2, ADAM_EPS, ADAM_WD, ADAM_STEP = 0.001, 0.9, 0.999, 1e-08, 0.01, 10


def _params(n_grid):
    return pltpu.CompilerParams(dimension_semantics=("arbitrary",) * n_grid, vmem_limit_bytes=VMEM_LIMIT)


_DIMS = {"nn": (((1,), (0,)), ((), ())), "nt": (((1,), (1,)), ((), ())), "tn": (((0,), (0,)), ((), ()))}
_DIMS_BATCHED = {"nn": (((2,), (1,)), ((0,), (0,))), "nt": (((2,), (2,)), ((0,), (0,))),
                 "tn": (((1,), (1,)), ((0,), (0,)))}


def _raw_dot(a, b, mode, passes):
    dims = (_DIMS if a.ndim == 2 else _DIMS_BATCHED)[mode]
    if passes == 6:
        return lax.dot_general(a.astype(f32), b.astype(f32), dims, precision=HI, preferred_element_type=f32)
    ah, bh = a.astype(bf16), b.astype(bf16)
    r = lax.dot_general(ah, bh, dims, preferred_element_type=f32)
    if passes == 3:
        al = (a - ah.astype(f32)).astype(bf16)
        bl = (b - bh.astype(f32)).astype(bf16)
        r = r + lax.dot_general(al, bh, dims, preferred_element_type=f32)
        r = r + lax.dot_general(ah, bl, dims, preferred_element_type=f32)
    return r


@functools.partial(jax.custom_vjp, nondiff_argnums=(2, 3))
def pdot(a, b, mode, passes):
    return _raw_dot(a, b, mode, passes)


def _pdot_bwd(mode, passes, res, g):
    a, b = res
    if mode == "nn":
        da, db = _raw_dot(g, b, "nt", passes), _raw_dot(a, g, "tn", passes)
    elif mode == "nt":
        da, db = _raw_dot(g, b, "nn", passes), _raw_dot(g, a, "tn", passes)
    else:
        da, db = _raw_dot(b, g, "nt", passes), _raw_dot(a, g, "nn", passes)
    return da.astype(a.dtype), db.astype(b.dtype)


pdot.defvjp(lambda a, b, mode, passes: (_raw_dot(a, b, mode, passes), (a, b)), _pdot_bwd)


def hdot(a, b):
    return pdot(a, b, "nn", 6)


def bdot(a, b):
    return pdot(a, b, "nn", 1)


def bdot_nt(a, b):
    return pdot(a, b, "nt", 1)


def _shift_rows(x, k):
    row = lax.broadcasted_iota(jnp.int32, x.shape, 0)
    return jnp.where(row < k, 0.0, pltpu.roll(x, k, axis=0))


def _unshift_rows(g, k):
    n = g.shape[0]
    row = lax.broadcasted_iota(jnp.int32, g.shape, 0)
    return jnp.where(row >= n - k, 0.0, pltpu.roll(g, n - k, axis=0))


@functools.partial(jax.custom_vjp, nondiff_argnums=(1,))
def tshift(x, k):
    return _shift_rows(x, k)


tshift.defvjp(lambda x, k: (_shift_rows(x, k), None), lambda k, _, g: (_unshift_rows(g, k),))


def rms(x, w):
    x = x.astype(f32)
    return x * lax.rsqrt(jnp.mean(x * x, axis=-1, keepdims=True) + RMS_EPS) * w


def softplus(x):
    return jnp.maximum(x, 0.0) + jnp.log(1.0 + jnp.exp(-jnp.abs(x)))


def seg2sum(x):
    lo = lax.broadcasted_iota(jnp.int32, x.shape, 1) < RW_HEAD
    s_lo = jnp.sum(jnp.where(lo, x, 0.0), axis=-1, keepdims=True)
    s_hi = jnp.sum(jnp.where(lo, 0.0, x), axis=-1, keepdims=True)
    return jnp.where(lo, s_lo, s_hi)


def _tile(n, pref):
    if n <= pref:
        return n
    t = pref
    while t >= LANES:
        if n % t == 0:
            return t
        t -= LANES
    return n


def mm(name, a, b, *, ta=False, tb=False, add=None, out_dtype=f32, by_chip=False, epilogue=None, extra=(),
       out_dtypes=None, after=None):
    (k, m) = a.shape if ta else a.shape[::-1]
    (n, kb) = b.shape if tb else b.shape[::-1]
    assert k == kb, (name, a.shape, b.shape)
    tm, tk = _tile(m, MM_TILE), _tile(k, MM_TILE_K)
    tn = _tile(n // 4, MM_TILE) if by_chip else _tile(n, MM_TILE)
    nk = k // tk
    dims = (((0,) if ta else (1,), (1,) if tb else (0,)), ((), ()))
    extra = list(extra) + ([] if add is None else [add])
    out_dtypes = [out_dtype] if out_dtypes is None else list(out_dtypes)
    n_extra, n_out = len(extra), len(out_dtypes)
    n_after = 0 if after is None else 1

    def body(*refs):
        a_ref, b_ref = refs[:2]
        x_refs = refs[2:2 + n_extra]
        o_refs = refs[2 + n_extra + n_after:2 + n_extra + n_after + n_out]
        part = lax.dot_general(a_ref[...].astype(bf16), b_ref[...].astype(bf16), dims, preferred_element_type=f32)

        def finish(r):
            xs = [x[...] for x in x_refs]
            if add is not None:
                r = r + xs.pop().astype(f32)
            outs = (r,) if epilogue is None else epilogue(r, *xs)
            for o_ref, o in zip(o_refs, outs):
                o_ref[...] = o.astype(o_ref.dtype)

        if nk == 1:
            finish(part)
            return
        acc = refs[-1]
        kk = pl.program_id(2)

        @pl.when(kk == 0)
        def _():
            acc[...] = part

        @pl.when(kk > 0)
        def _():
            acc[...] += part

        @pl.when(kk == nk - 1)
        def _():
            finish(acc[...])

    a_spec = pl.BlockSpec((tk, tm), lambda i, j, q: (q, i)) if ta else pl.BlockSpec((tm, tk), lambda i, j, q: (i, q))
    b_spec = pl.BlockSpec((tn, tk), lambda i, j, q: (j, q)) if tb else pl.BlockSpec((tk, tn), lambda i, j, q: (q, j))
    x_spec = pl.BlockSpec((tm, tn), lambda i, j, q: (i, j))
    if by_chip:
        per_chip = n // 4 // tn
        o_spec = pl.BlockSpec((None, tm, tn), lambda i, j, q: (j // per_chip, i, j % per_chip))
        o_shape = (4, m, n // 4)
    else:
        o_spec, o_shape = x_spec, (m, n)
    afters = [] if after is None else [jnp.reshape(after, (1, 1))]
    res = pl.pallas_call(
        body, name=name, grid=(m // tm, n // tn, nk),
        in_specs=[a_spec, b_spec] + [x_spec] * n_extra + [pl.BlockSpec(memory_space=pl.ANY)] * n_after,
        out_specs=[o_spec] * n_out, out_shape=[jax.ShapeDtypeStruct(o_shape, dt) for dt in out_dtypes],
        scratch_shapes=[pltpu.VMEM((tm, tn), f32)] if nk > 1 else [],
        compiler_params=_params(3))(a, b, *extra, *afters)
    return res[0] if n_out == 1 else res


REGROUP_PIECES = 4


REGROUP_FIELDS = 5


def _regroup_table(n_out, sources_of):
    import numpy as np
    tbl = np.zeros((REGROUP_FIELDS * REGROUP_PIECES, n_out), np.int32)
    for j in range(n_out):
        groups = sorted(sources_of(j).items())
        assert len(groups) <= REGROUP_PIECES, (j, len(groups))
        for p in range(REGROUP_PIECES):
            if p < len(groups):
                key, lanes = groups[p]
                shifts = {q - s for s, q in lanes}
                qs = sorted(q for _, q in lanes)
                assert len(shifts) == 1 and qs == list(range(qs[0], qs[-1] + 1)), (j, key)
                row = (key[0], key[1], shifts.pop(), qs[0], qs[-1] + 1)
            else:
                row = (0, 0, 0, 0, 0)
            tbl[REGROUP_FIELDS * p:REGROUP_FIELDS * (p + 1), j] = row
    return jnp.asarray(tbl)


def lane_regroup(name, src, table, src_spec, out_spec, out_shape, grid, src_width):
    rows = src.shape[-2]

    def body(tbl, *refs):
        o_ref = refs[REGROUP_PIECES]
        j = pl.program_id(1)

        def moved(p):
            blk, shift, lo, hi = (tbl[REGROUP_FIELDS * p + f, j] for f in range(1, REGROUP_FIELDS))
            x = refs[p][...]
            if src_width % LANES:
                lane = lax.broadcasted_iota(jnp.int32, (rows, LANES), 1)
                x = jnp.where(lane < src_width - blk * LANES, x, jnp.zeros((), src.dtype))
            pi = lax.broadcasted_iota(jnp.int32, (LANES, LANES), 0)
            qi = lax.broadcasted_iota(jnp.int32, (LANES, LANES), 1)
            sel = jnp.logical_and(qi - pi == shift, jnp.logical_and(qi >= lo, qi < hi))
            return jnp.dot(x, sel.astype(src.dtype), preferred_element_type=f32).astype(o_ref.dtype)

        o_ref[...] = moved(0)
        for p in range(1, REGROUP_PIECES):
            @pl.when(tbl[REGROUP_FIELDS * p + 4, j] > tbl[REGROUP_FIELDS * p + 3, j])
            def _(p=p):
                o_ref[...] += moved(p)

    return pl.pallas_call(
        body, name=name, out_shape=jax.ShapeDtypeStruct(out_shape, src.dtype),
        grid_spec=pltpu.PrefetchScalarGridSpec(
            num_scalar_prefetch=1, grid=grid, in_specs=[src_spec(p) for p in range(REGROUP_PIECES)],
            out_specs=out_spec),
        compiler_params=_params(2))(table, *[src] * REGROUP_PIECES)


def w_in_layout(d_cols, shard_w, c_split, gap, n_blocks):
    def ext_of(c):
        return c if c < c_split else c + gap

    def fwd_sources(j):
        groups = {}
        for q in range(LANES):
            e = j * LANES + q
            c = e if e < c_split else e - gap
            if (c_split <= e < c_split + gap) or c >= d_cols:
                continue
            s, l = divmod(c, shard_w)
            groups.setdefault((s, l // LANES), []).append((l % LANES, q))
        return groups

    per_shard = -(-shard_w // LANES)

    def bwd_sources(j):
        s, b = divmod(j, per_shard)
        groups = {}
        for q in range(LANES):
            l = b * LANES + q
            if l >= shard_w:
                continue
            e = ext_of(s * shard_w + l)
            groups.setdefault((0, e // LANES), []).append((e % LANES, q))
        return groups

    return _regroup_table(n_blocks, fwd_sources), _regroup_table(4 * per_shard, bwd_sources), per_shard


def _in_spec(a, kind, tile):
    if kind == "row":
        return pl.BlockSpec((tile, a.shape[1]), lambda j, i: (i, 0))
    if kind == "const":
        return pl.BlockSpec(a.shape, lambda j, i: (0, 0))
    tag, cw, off = kind
    if tag == "rowc":
        return pl.BlockSpec((tile, cw), lambda j, i: (i, j + off))
    assert tag == "constc", kind
    return pl.BlockSpec((a.shape[0], cw), lambda j, i: (0, j + off))


def rowcall(name, fn, ins, outs, *, rows, tile=None, ncol=1):
    if tile is None:
        tile = min(TOK_TILE_BLOCKED if ncol > 1 else TOK_TILE, rows)
    n_in = len(ins)
    kinds = [o[3] for o in outs]

    def body(*refs):
        j, i = pl.program_id(0), pl.program_id(1)
        res = fn(*[r[...] for r in refs[:n_in]])
        for ref, val, kind in zip(refs[n_in:], res, kinds):
            if kind in ("row", "rowc"):
                ref[...] = val.astype(ref.dtype)
            else:
                first = (i == 0) if kind == "accc" else jnp.logical_and(i == 0, j == 0)

                @pl.when(first)
                def _(ref=ref, val=val):
                    ref[...] = val.astype(ref.dtype)

                @pl.when(jnp.logical_not(first))
                def _(ref=ref, val=val):
                    ref[...] += val.astype(ref.dtype)

    out_shape, out_specs = [], []
    for nr, nc, dtype, kind in outs:
        out_shape.append(jax.ShapeDtypeStruct((nr, nc), dtype))
        if kind == "row":
            out_specs.append(pl.BlockSpec((tile, nc), lambda j, i: (i, 0)))
        elif kind == "rowc":
            out_specs.append(pl.BlockSpec((tile, nc // ncol), lambda j, i: (i, j)))
        elif kind == "acc":
            out_specs.append(pl.BlockSpec((nr, nc), lambda j, i: (0, 0)))
        else:
            out_specs.append(pl.BlockSpec((nr, nc // ncol), lambda j, i: (0, j)))
    return pl.pallas_call(
        body, name=name, grid=(ncol, rows // tile), in_specs=[_in_spec(a, k, tile) for a, k in ins],
        out_specs=out_specs, out_shape=out_shape, compiler_params=_params(2))(*[a for a, _ in ins])


def rowvjp(name, fn, ins, cts, grads, *, rows, tile=None, ncol=1, adds=(), dup16=()):
    n_in = len(ins)
    ct_sizes = [len(c) for c in cts]
    flat_cts = [m for c in cts for m in c]
    n_ct = len(flat_cts)

    def wrapped(*vals):
        xs = list(vals[:n_in])
        gs = vals[n_in:n_in + n_ct]
        extra = vals[n_in + n_ct:]

        def f(*dvars):
            full = list(xs)
            for k, v in zip(grads, dvars):
                full[k] = v
            return fn(*full)

        outs, pull = jax.vjp(f, *[xs[k] for k in grads])
        cot, p = [], 0
        for o, size in zip(outs, ct_sizes):
            g = gs[p].astype(f32)
            for q in range(1, size):
                g = g + gs[p + q].astype(f32)
            cot.append(g.astype(o.dtype))
            p += size
        gv = list(pull(tuple(cot)))
        for (pos, _, _), e in zip(adds, extra):
            gv[pos] = gv[pos] + e.astype(gv[pos].dtype)
        return tuple(gv) + tuple(gv[pos] for pos in dup16)

    outs = []
    for k in grads:
        a, kind = ins[k]
        if kind == "row":
            outs.append((rows, a.shape[1] * ncol, f32, "rowc") if ncol > 1 else (rows, a.shape[1], f32, "row"))
        elif kind == "const":
            outs.append((a.shape[0], a.shape[1], f32, "acc"))
        elif kind[0] == "rowc":
            outs.append((rows, kind[1] * ncol, f32, "rowc"))
        else:
            outs.append((a.shape[0], kind[1] * ncol, f32, "accc"))
    for pos in dup16:
        nr, nc, _, kind = outs[pos]
        outs.append((nr, nc, bf16, kind))
    all_ins = list(ins) + flat_cts + [(a, kind) for _, a, kind in adds]
    return rowcall(name, wrapped, all_ins, outs, rows=rows, tile=tile, ncol=ncol)


def fn_rms(x, w):
    return (rms(x, w),)


def make_fn_gconv(n_norm_blocks):
    def fn(p, cw):
        c = cw[3:4] * p
        for jj in range(3):
            c = c + cw[jj:jj + 1] * tshift(p, 3 - jj)
        s = c * jax.nn.sigmoid(c)
        nrm = s * lax.rsqrt(jnp.sum(s * s, axis=-1, keepdims=True) + 1e-6)
        return (jnp.where(pl.program_id(0) < n_norm_blocks, nrm, s),)
    return fn


def fn_ggate(ps0, alog, dtb, e_g, e_b):
    g = -jnp.exp(alog) * softplus(ps0 + dtb)
    beta = jax.nn.sigmoid(ps0)
    return hdot(g, e_g), hdot(beta, e_b)


def fn_gpost(o, z, nw):
    return (rms(o, nw) * (z * jax.nn.sigmoid(z)),)


def fn_lerp(p, mu):
    return (p + (tshift(p, 1) - p) * mu,)


def fn_rprep(pk, psl, w0, a0, k_k, k_a, w2p, a2p, g2):
    g1, g2in = psl[:, :LANES], psl[:, LANES:]
    log_w = -softplus(-(w0 + bdot(jnp.tanh(g1), w2p))) - 0.5
    lw = -jnp.exp(log_w)
    a = jax.nn.sigmoid(a0 + bdot(g1, a2p))
    gate = bdot(jax.nn.sigmoid(g2in), g2)
    kkr = pk * k_k
    kk = kkr / jnp.maximum(jnp.sqrt(seg2sum(kkr * kkr)), 1e-12)
    kmod = pk * (1.0 + (a - 1.0) * k_a)
    return lw, kmod, kk, a, gate


def fn_rpost(y, r, kmod, v, gate, r_k, ln_w, ln_b):
    inv_n = 1.0 / RW_HEAD
    mean = seg2sum(y) * inv_n
    d = y - mean
    var = seg2sum(d * d) * inv_n
    yn = d * lax.rsqrt(var + RW_GN_EPS) * ln_w + ln_b
    bonus = seg2sum(r * kmod * r_k) * v
    return ((yn + bonus) * gate,)


def fn_xcore(q, k, v):
    s = bdot_nt(q, k) * (LANES ** -0.5)
    p = jax.nn.softmax(s, axis=-1)
    return (bdot(p, v),)


def fn_final(h, tgt, w):
    def loss_fn(h, w):
        return 0.5 * jnp.sum(jnp.mean(jnp.square(rms(h, w) - tgt), axis=-1))

    val, (dh, dw) = jax.value_and_grad(loss_fn, argnums=(0, 1))(h, w)
    return dh, dh.astype(bf16), dw, jnp.full((8, LANES), val, f32)


def fn_sumcols(n):
    def fn(x):
        w = x.shape[1] // n
        s = x[:, :w]
        for q in range(1, n):
            s = s + x[:, q * w:(q + 1) * w]
        return (s,)
    return fn


def _tri(c):
    ii = lax.broadcasted_iota(jnp.int32, (c, c), 0)
    jj = lax.broadcasted_iota(jnp.int32, (c, c), 1)
    return ii, jj


def _neumann_raw(m, steps):
    c = m.shape[-1]
    ii, jj = _tri(c)
    eye = (ii == jj).astype(f32)
    t, p = eye + m, m
    for _ in range(steps):
        p = _raw_dot(p, p, "nn", P_INV)
        t = _raw_dot(t, eye + p, "nn", P_INV)
    resid = eye - t + _raw_dot(m, t, "nn", P_RESID)
    return t + _raw_dot(t, resid, "nn", P_INV)


@functools.partial(jax.custom_vjp, nondiff_argnums=(1,))
def _neumann_inverse(m, steps):
    return _neumann_raw(m, steps)


def _neumann_fwd(m, steps):
    t = _neumann_raw(m, steps)
    return t, t


def _neumann_bwd(steps, t, g):
    return (_raw_dot(_raw_dot(t, g, "tn", P_RESID), t, "nt", P_RESID),)


_neumann_inverse.defvjp(_neumann_fwd, _neumann_bwd)


@jax.custom_vjp
def _known_inverse(m, t):
    return t


_known_inverse.defvjp(lambda m, t: (t, t), lambda t, g: (_neumann_bwd(0, t, g)[0], jnp.zeros_like(t)))


def _inverse(m, steps, kept):
    return _neumann_inverse(m, steps) if kept is None else _known_inverse(m, kept)


def cumsum_rows(x):
    t = x.shape[1]
    ii, jj = _tri(t)
    tri = jnp.broadcast_to((ii >= jj).astype(f32), (x.shape[0], t, t))
    return pdot(tri, x, "nn", P_CUMSUM)


def gdn_chunk(q, k, v, gb, bb, s, kept=None):
    c = q.shape[1]
    ii, jj = _tri(c)
    low = ii >= jj
    gcb = cumsum_rows(gb)
    gl = jnp.sum(gb, axis=1, keepdims=True)
    gc_col = gcb[:, :, :c]
    diff = gc_col - jnp.swapaxes(gc_col, 1, 2)
    decay = jnp.where(low, jnp.exp(jnp.where(low, diff, 0.0)), 0.0)
    qs = q * (q.shape[2] ** -0.5)
    kb = k * bb
    with_k = pdot(jnp.concatenate([kb, qs], axis=1), k, "nt", P_BULK)
    a = jnp.where(ii > jj, with_k[:, :c] * decay, 0.0)
    attn = with_k[:, c:] * decay
    t = _inverse(-a, c.bit_length() - 2, kept)
    eg = jnp.exp(gcb)
    uw = pdot(t, jnp.concatenate([v * bb, kb * eg], axis=2), "nn", P_BULK)
    u, w = uw[:, :, :LANES], uw[:, :, LANES:]
    kd = k * jnp.exp(gl - gcb)
    from_state = pdot(jnp.concatenate([w, qs * eg], axis=1), s, "nn", P_BULK)
    v_new = u - from_state[:, :c]
    o = from_state[:, c:] + pdot(attn, v_new, "nn", P_BULK)
    s_new = s * jnp.exp(gl) + pdot(kd, v_new, "tn", P_BULK)
    return o, s_new, t


def wkv_chunk(r, lw, k, v, kk, a, s, kept=None):
    t = r.shape[1]
    ii, jj = _tri(t)
    lo = lax.broadcasted_iota(jnp.int32, r.shape, 2) < RW_HEAD
    cl = cumsum_rows(lw)
    cl_last = jnp.sum(lw, axis=1, keepdims=True)
    al = -kk * jnp.exp(cl - lw)
    be = (a * kk) * jnp.exp(-cl)
    kt = k * jnp.exp(-cl)
    rt = r * jnp.exp(cl)

    def dot(xa, xb, mode="nn"):
        return pdot(xa, xb, mode, P_BULK)

    def sel(x_lo, x_hi):
        return jnp.where(lo, x_lo, x_hi)

    left = jnp.concatenate([jnp.where(lo, al, 0.0), jnp.where(lo, 0.0, al),
                            jnp.where(lo, rt, 0.0), jnp.where(lo, 0.0, rt)], axis=1)
    with_be, with_kt = dot(left, be, "nt"), dot(left, kt, "nt")

    def blocks(prod):
        below, upto = ii > jj, ii >= jj
        return [jnp.where(msk, prod[:, q * t:(q + 1) * t], 0.0) for q, msk in enumerate((below, below, upto, upto))]

    ab_lo, ab_hi, rb_lo, rb_hi = blocks(with_be)
    ak_lo, ak_hi, rk_lo, rk_hi = blocks(with_kt)
    from_state = dot(jnp.concatenate([al, rt], axis=1), s, "nt")
    x = from_state[:, :t] + sel(dot(ak_lo, v), dot(ak_hi, v))
    steps = t.bit_length() - 2
    inv_lo = _inverse(ab_lo, steps, None if kept is None else kept[:, :t])
    inv_hi = _inverse(ab_hi, steps, None if kept is None else kept[:, t:])
    u = sel(dot(inv_lo, x), dot(inv_hi, x))
    y = from_state[:, t:] + sel(dot(rb_lo, u) + dot(rk_lo, v), dot(rb_hi, u) + dot(rk_hi, v))
    vi = lax.broadcasted_iota(jnp.int32, s.shape, 1) < RW_HEAD
    ki = lax.broadcasted_iota(jnp.int32, s.shape, 2) < RW_HEAD
    s_new = jnp.where(vi == ki, (s + dot(u, be, "tn") + dot(v, kt, "tn")) * jnp.exp(cl_last), 0.0)
    return y, s_new, jnp.concatenate([inv_lo, inv_hi], axis=1)


def _scan_group(ncol, offs):
    g = SCAN_GROUP
    while g > 1 and (ncol % g or any(o % g for o in offs)):
        g //= 2
    return g


def scan_fwd(name, chunk_fn, ins, *, rows, chunk, ncol):
    n = rows // chunk
    n_in = len(ins)
    grp = _scan_group(ncol, [off for _, off in ins])
    tile = jax.ShapeDtypeStruct((grp, chunk, LANES), f32)
    kept = jax.eval_shape(chunk_fn, *[tile] * n_in, jax.ShapeDtypeStruct((grp, LANES, LANES), f32))[2].shape[1:]

    def body(*refs):
        o_ref, st_ref, kept_ref, s_scr = refs[n_in:]

        @pl.when(pl.program_id(1) == 0)
        def _():
            s_scr[...] = jnp.zeros_like(s_scr)

        cols = [slice(b * LANES, (b + 1) * LANES) for b in range(grp)]
        s = s_scr[...]
        st_ref[...] = s
        o, s_new, inv = chunk_fn(*[jnp.stack([r[:, c] for c in cols]) for r in refs[:n_in]], s)
        for b, c in enumerate(cols):
            o_ref[:, c] = o[b]
        kept_ref[...] = inv
        s_scr[...] = s_new

    def spec(off):
        return pl.BlockSpec((chunk, grp * LANES), lambda h, c: (c, h + off // grp))

    def per_chunk(shape):
        return pl.BlockSpec((grp, None) + tuple(shape), lambda h, c: (h, c, 0, 0))

    return pl.pallas_call(
        body, name=name, grid=(ncol // grp, n), in_specs=[spec(off) for _, off in ins],
        out_specs=[spec(0), per_chunk((LANES, LANES)), per_chunk(kept)],
        out_shape=[jax.ShapeDtypeStruct((rows, ncol * LANES), f32),
                   jax.ShapeDtypeStruct((ncol, n, LANES, LANES), f32),
                   jax.ShapeDtypeStruct((ncol, n) + tuple(kept), f32)],
        scratch_shapes=[pltpu.VMEM((grp, LANES, LANES), f32)], compiler_params=_params(2))(*[a for a, _ in ins])


def scan_bwd(name, chunk_fn, ins, states, kept, d_out, d_off, *, rows, chunk, ncol):
    n = rows // chunk
    n_in = len(ins)
    grp = _scan_group(ncol, [off for _, off in ins] + [d_off])

    def body(*refs):
        st_ref, kept_ref, do_ref = refs[n_in:n_in + 3]
        g_refs = refs[n_in + 3:2 * n_in + 3]
        ds_scr = refs[-1]

        @pl.when(pl.program_id(1) == 0)
        def _():
            ds_scr[...] = jnp.zeros_like(ds_scr)

        cols = [slice(b * LANES, (b + 1) * LANES) for b in range(grp)]

        def batch(ref):
            return jnp.stack([ref[:, c] for c in cols])

        inv = kept_ref[...]
        _, pull = jax.vjp(lambda *a: chunk_fn(*a, kept=inv)[:2], *[batch(r) for r in refs[:n_in]], st_ref[...])
        gs = pull((batch(do_ref), ds_scr[...]))
        for ref, g in zip(g_refs, gs[:n_in]):
            for b, c in enumerate(cols):
                ref[:, c] = g[b]
        ds_scr[...] = gs[n_in]

    def spec(off):
        return pl.BlockSpec((chunk, grp * LANES), lambda h, c: (n - 1 - c, h + off // grp))

    def per_chunk(shape):
        return pl.BlockSpec((grp, None) + tuple(shape), lambda h, c: (h, n - 1 - c, 0, 0))

    return pl.pallas_call(
        body, name=name, grid=(ncol // grp, n),
        in_specs=[spec(off) for _, off in ins] + [per_chunk(states.shape[2:]), per_chunk(kept.shape[2:]), spec(d_off)],
        out_specs=[spec(0)] * n_in, out_shape=[jax.ShapeDtypeStruct((rows, ncol * LANES), f32)] * n_in,
        scratch_shapes=[pltpu.VMEM((grp, LANES, LANES), f32)],
        compiler_params=_params(2))(*[a for a, _ in ins], states, kept, d_out)


def flip_exchange(name, arrs, flips, n_slots, slot_of, src_of, with_self, after=()):
    n = len(arrs)
    nf = len(flips)
    n_after = len(after)

    def body(*refs):
        ins, outs = refs[:n], refs[n + n_after:2 * n + n_after]
        send, recv, lsem = refs[2 * n + n_after:]
        me = (lax.axis_index("x"), lax.axis_index("y"), lax.axis_index("c"))
        copies = []
        for k in range(n):
            if with_self:
                cp = pltpu.make_async_copy(src_of(ins[k], me), outs[k].at[slot_of(me)], lsem.at[k])
                cp.start()
                copies.append(cp)
            for j, fl in enumerate(flips):
                peer = tuple(1 - m if f else m for m, f in zip(me, fl))
                cp = pltpu.make_async_remote_copy(
                    src_ref=src_of(ins[k], peer), dst_ref=outs[k].at[slot_of(me)], send_sem=send.at[k, j],
                    recv_sem=recv.at[k, j], device_id=peer, device_id_type=MESH)
                cp.start()
                copies.append(cp)
        for cp in copies:
            cp.wait()

    def out_sds(a):
        blk = src_of(jax.ShapeDtypeStruct(a.shape, a.dtype), None)
        return jax.ShapeDtypeStruct((n_slots,) + tuple(blk), a.dtype)

    any_spec = pl.BlockSpec(memory_space=pl.ANY)
    return pl.pallas_call(
        body, name=name, in_specs=[any_spec] * (n + n_after), out_specs=[any_spec] * n,
        out_shape=[out_sds(a) for a in arrs],
        scratch_shapes=[pltpu.SemaphoreType.DMA((n, nf)), pltpu.SemaphoreType.DMA((n, nf)),
                        pltpu.SemaphoreType.DMA((n,))],
        compiler_params=pltpu.CompilerParams(has_side_effects=True))(*arrs, *after)


_CHIP_FLIPS = ((1, 0, 0), (0, 1, 0), (1, 1, 0))
_ALL_FLIPS = ((0, 0, 1), (0, 1, 0), (0, 1, 1), (1, 0, 0), (1, 0, 1), (1, 1, 0), (1, 1, 1))


def _whole(ref, pos):
    return ref.shape if pos is None else ref


def _chip_block(ref, pos):
    return ref.shape[1:] if pos is None else ref.at[2 * pos[0] + pos[1]]


def _chip_slot(p):
    return 2 * p[0] + p[1]


def gather_chips(name, arrs):
    return flip_exchange(name, arrs, _CHIP_FLIPS, 4, _chip_slot, _whole, True)


def scatter_chips(name, arrs):
    return flip_exchange(name, arrs, _CHIP_FLIPS, 4, _chip_slot, _chip_block, True)


_HBM = pl.BlockSpec(memory_space=pltpu.HBM)
_SEM = pl.BlockSpec(memory_space=pltpu.SEMAPHORE)
_DATAFLOW = pltpu.SideEffectType.DATAFLOW_SIDE_EFFECTING


def _split_copies(mode, refs, n, send, recv):
    me = (lax.axis_index("x"), lax.axis_index("y"), lax.axis_index("c"))
    sib = (me[0], me[1], 1 - me[2])
    lands = refs[:n] if mode == "handover" else refs[n:2 * n]
    copies = []
    for k, land in enumerate(lands):
        half = land.shape[1] // 2
        mine = pl.ds(pl.multiple_of(me[2] * half, 16), half)
        for j, fl in enumerate(_CHIP_FLIPS):
            peer = tuple(1 - m if f else m for m, f in zip(me, fl))
            if mode == "gather":
                src, dst, to = refs[k], land.at[_chip_slot(me)], peer
            elif mode == "scatter":
                src, dst, to = refs[k].at[_chip_slot(peer)], land.at[_chip_slot(me)], peer
            elif mode == "gather_half":
                src, dst, to = refs[k].at[mine], land.at[_chip_slot(me), mine], peer
            else:
                src = dst = land.at[_chip_slot(peer), mine]
                to = sib
            q = k * len(_CHIP_FLIPS) + j
            copies.append(pltpu.make_async_remote_copy(src_ref=src, dst_ref=dst, send_sem=send.at[q],
                                                       recv_sem=recv.at[q], device_id=to, device_id_type=MESH))
    return copies


def split_start(name, mode, ops, n):
    ops = [pltpu.with_memory_space_constraint(a, pltpu.HBM) for a in ops]
    m = len(ops)

    def body(*refs):
        for cp in _split_copies(mode, refs[:m], n, refs[m], refs[m + 1]):
            cp.start()
        refs[-1][...] = jnp.zeros_like(refs[-1])

    sems = pltpu.SemaphoreType.DMA((n * len(_CHIP_FLIPS),))
    outs = pl.pallas_call(
        body, name=name, in_specs=[_HBM] * m,
        out_shape=(sems, sems, *[pltpu.HBM(a.shape, a.dtype) for a in ops], jax.ShapeDtypeStruct((8, LANES), f32)),
        out_specs=(_SEM, _SEM, *[_HBM] * m, pl.BlockSpec(memory_space=pltpu.VMEM)),
        input_output_aliases={i: 2 + i for i in range(m)},
        compiler_params=pltpu.CompilerParams(has_side_effects=_DATAFLOW))(*ops)
    return (outs[0], outs[1], list(outs[2:2 + m]), mode, n), outs[-1][0, 0]


def split_wait(name, state, after):
    send, recv, ops, mode, n = state
    m = len(ops)
    afters = list(after) if isinstance(after, (list, tuple)) else [after]

    def body(*refs):
        for cp in _split_copies(mode, refs[:m], n, refs[m], refs[m + 1]):
            cp.wait_send()
            cp.wait_recv()

    outs = pl.pallas_call(
        body, name=name, in_specs=[_HBM] * m + [_SEM, _SEM] + [pl.BlockSpec(memory_space=pl.ANY)] * len(afters),
        out_shape=tuple(pltpu.HBM(a.shape, a.dtype) for a in ops), out_specs=tuple([_HBM] * m),
        input_output_aliases={i: i for i in range(m)},
        compiler_params=pltpu.CompilerParams(has_side_effects=_DATAFLOW))(*ops, send, recv, *afters)
    return list(outs[m - n:])


def chips_start(name, arrs, src_of, halves=False):
    me = _chip_slot((lax.axis_index("x"), lax.axis_index("y")))
    lands = []
    for a in arrs:
        blk = tuple(src_of(jax.ShapeDtypeStruct(a.shape, a.dtype), None))
        own = a if src_of is _whole else lax.dynamic_index_in_dim(a, me, 0, keepdims=False)
        lands.append(lax.dynamic_update_index_in_dim(lax.empty((4,) + blk, a.dtype), own, me, 0))
    mode = "scatter" if src_of is _chip_block else ("gather_half" if halves else "gather")
    return split_start(name, mode, list(arrs) + lands, len(arrs))


chips_wait = split_wait


def handover_start(name, lands):
    return split_start(name, "handover", lands, len(lands))


def gather_chips_halves(name, arrs):
    n = len(arrs)
    nf = len(_CHIP_FLIPS)
    split = [a.shape[0] % 32 == 0 for a in arrs]

    def body(*refs):
        ins, outs = refs[:n], refs[n:2 * n]
        send1, recv1, send2, recv2, lsem = refs[2 * n:]
        me = (lax.axis_index("x"), lax.axis_index("y"), lax.axis_index("c"))
        sib = (me[0], me[1], 1 - me[2])
        peers = [tuple(1 - m if f else m for m, f in zip(me, fl)) for fl in _CHIP_FLIPS]
        local, first, second = [], [], []
        for k in range(n):
            cp = pltpu.make_async_copy(ins[k], outs[k].at[_chip_slot(me)], lsem.at[k])
            cp.start()
            local.append(cp)
            half = ins[k].shape[0] // 2
            rows = pl.ds(pl.multiple_of(me[2] * half, 16), half) if split[k] else pl.ds(0, ins[k].shape[0])
            for j, peer in enumerate(peers):
                cp = pltpu.make_async_remote_copy(
                    src_ref=ins[k].at[rows], dst_ref=outs[k].at[_chip_slot(me), rows], send_sem=send1.at[k, j],
                    recv_sem=recv1.at[k, j], device_id=peer, device_id_type=MESH)
                cp.start()
                first.append((k, j, rows, cp))
        for k, j, rows, cp in first:
            cp.wait_recv()
            if split[k]:
                got = outs[k].at[_chip_slot(peers[j]), rows]
                fwd = pltpu.make_async_remote_copy(src_ref=got, dst_ref=got, send_sem=send2.at[k, j],
                                                   recv_sem=recv2.at[k, j], device_id=sib, device_id_type=MESH)
                fwd.start()
                second.append(fwd)
        for _, _, _, cp in first:
            cp.wait_send()
        for cp in second:
            cp.wait()
        for cp in local:
            cp.wait()

    any_spec = pl.BlockSpec(memory_space=pl.ANY)
    sems = pltpu.SemaphoreType.DMA((n, nf))
    return pl.pallas_call(
        body, name=name, in_specs=[any_spec] * n, out_specs=[any_spec] * n,
        out_shape=[jax.ShapeDtypeStruct((4,) + a.shape, a.dtype) for a in arrs],
        scratch_shapes=[sems, sems, sems, sems, pltpu.SemaphoreType.DMA((n,))],
        compiler_params=pltpu.CompilerParams(has_side_effects=True))(*arrs)


def swap_sibling(name, arrs):
    outs = flip_exchange(name, arrs, ((0, 0, 1),), 1, lambda p: 0, _whole, False)
    return [o[0] for o in outs]


def gather_all(arrs, after=()):
    return flip_exchange("gather_all", arrs, _ALL_FLIPS, 8, lambda p: 4 * p[0] + 2 * p[1] + p[2], _whole, True,
                         after=after)


def _row_tile(nr, nc, n_arrays):
    budget = (20 << 20) // (n_arrays * 2 * 4 * max(nc, LANES))
    t = min(nr, budget) // 16 * 16
    while t > 0 and nr % t:
        t -= 16
    return t if t > 0 else nr


def sum_slots(name, x):
    ns, nr, nc = x.shape
    tile = _row_tile(nr, nc, ns + 1)

    def body(x_ref, o_ref):
        s = x_ref[0].astype(f32)
        for q in range(1, ns):
            s = s + x_ref[q].astype(f32)
        o_ref[...] = s.astype(o_ref.dtype)

    return pl.pallas_call(
        body, name=name, grid=(nr // tile,), in_specs=[pl.BlockSpec((ns, tile, nc), lambda i: (0, i, 0))],
        out_specs=pl.BlockSpec((tile, nc), lambda i: (i, 0)), out_shape=jax.ShapeDtypeStruct((nr, nc), x.dtype),
        compiler_params=_params(1))(x)


def adamw(name, w, g_parts, m, v):
    nr, nc = w.shape[-2:]
    n_g = len(g_parts)
    tile = _row_tile(nr, nc, 7 + n_g)

    def body(*refs):
        w_ref, m_ref, v_ref = refs[:3]
        g = refs[3][...].astype(f32)
        for r in refs[4:3 + n_g]:
            g = g + r[...].astype(f32)
        g_ref, d_ref, nm_ref, nv_ref = refs[3 + n_g:]
        nm = ADAM_B1 * m_ref[...] + (1.0 - ADAM_B1) * g
        nv = ADAM_B2 * v_ref[...] + (1.0 - ADAM_B2) * jnp.square(g)
        m_hat = nm / (1.0 - ADAM_B1 ** ADAM_STEP)
        v_hat = nv / (1.0 - ADAM_B2 ** ADAM_STEP)
        g_ref[...] = g
        d_ref[...] = -ADAM_LR * (m_hat / (jnp.sqrt(v_hat) + ADAM_EPS) + ADAM_WD * w_ref[...])
        nm_ref[...] = nm
        nv_ref[...] = nv

    spec = pl.BlockSpec((tile, nc), lambda i: (i, 0))
    spec3 = pl.BlockSpec((None, tile, nc), lambda i: (0, i, 0)) if w.ndim == 3 else spec
    return pl.pallas_call(
        body, name=name, grid=(nr // tile,), in_specs=[spec3] * 3 + [spec] * n_g, out_specs=[spec3] * 4,
        out_shape=[jax.ShapeDtypeStruct(w.shape, f32)] * 4, compiler_params=_params(1))(w, m, v, *g_parts)


def adamw_packed(w, g8, m, v):
    nr, nc = w.shape

    def body(w_ref, g_ref, m_ref, v_ref, go_ref, d_ref, nm_ref, nv_ref):
        g = g_ref[0]
        for q in range(1, 8):
            g = g + g_ref[q]
        nm = ADAM_B1 * m_ref[...] + (1.0 - ADAM_B1) * g
        nv = ADAM_B2 * v_ref[...] + (1.0 - ADAM_B2) * jnp.square(g)
        m_hat = nm / (1.0 - ADAM_B1 ** ADAM_STEP)
        v_hat = nv / (1.0 - ADAM_B2 ** ADAM_STEP)
        go_ref[...] = g
        d_ref[...] = -ADAM_LR * (m_hat / (jnp.sqrt(v_hat) + ADAM_EPS) + ADAM_WD * w_ref[...])
        nm_ref[...] = nm
        nv_ref[...] = nv

    return pl.pallas_call(body, name="adamw_packed", out_shape=[jax.ShapeDtypeStruct((nr, nc), f32)] * 4,
                          compiler_params=pltpu.CompilerParams(vmem_limit_bytes=VMEM_LIMIT))(w, g8, m, v)


def _pack(vectors):
    rows = []
    for a in vectors:
        flat = a.reshape(-1).astype(f32)
        pad = (-flat.shape[0]) % LANES
        rows.append(jnp.pad(flat, (0, pad)).reshape(-1, LANES))
    packed = jnp.concatenate(rows, axis=0)
    return jnp.pad(packed, ((0, (-packed.shape[0]) % 8), (0, 0)))


def _unpack(packed, like):
    out, r = [], 0
    for a in like:
        n = a.size
        nr = -(-n // LANES)
        out.append(packed[r:r + nr].reshape(-1)[:n].reshape(a.shape))
        r += nr
    return out


def kernel(x, mem, mix_norm_w, w_in, dn_conv_w, dn_a_log, dn_dt_bias, dn_norm_w, rw_mu, rw_w0, rw_w2, rw_a0, rw_a2, rw_g2, rw_k_k, rw_k_a, rw_r_k, rw_ln_w, rw_ln_b, w_out, xa_norm_w, mem_norm_w, xa_wq, xa_wk, xa_wv, xa_wo, ffn_norm_w, ffn_w1, ffn_w2, final_norm_w, loss_target, m_mix_norm_w, m_w_in, m_dn_conv_w, m_dn_a_log, m_dn_dt_bias, m_dn_norm_w, m_rw_mu, m_rw_w0, m_rw_w2, m_rw_a0, m_rw_a2, m_rw_g2, m_rw_k_k, m_rw_k_a, m_rw_r_k, m_rw_ln_w, m_rw_ln_b, m_w_out, m_xa_norm_w, m_mem_norm_w, m_xa_wq, m_xa_wk, m_xa_wv, m_xa_wo, m_ffn_norm_w, m_ffn_w1, m_ffn_w2, m_final_norm_w, v_mix_norm_w, v_w_in, v_dn_conv_w, v_dn_a_log, v_dn_dt_bias, v_dn_norm_w, v_rw_mu, v_rw_w0, v_rw_w2, v_rw_a0, v_rw_a2, v_rw_g2, v_rw_k_k, v_rw_k_a, v_rw_r_k, v_rw_ln_w, v_rw_ln_b, v_w_out, v_xa_norm_w, v_mem_norm_w, v_xa_wq, v_xa_wk, v_xa_wv, v_xa_wo, v_ffn_norm_w, v_ffn_w1, v_ffn_w2, v_final_norm_w):
    weights = dict(mix_norm_w=mix_norm_w, w_in=w_in, dn_conv_w=dn_conv_w, dn_a_log=dn_a_log, dn_dt_bias=dn_dt_bias, dn_norm_w=dn_norm_w, rw_mu=rw_mu, rw_w0=rw_w0, rw_w2=rw_w2, rw_a0=rw_a0, rw_a2=rw_a2, rw_g2=rw_g2, rw_k_k=rw_k_k, rw_k_a=rw_k_a, rw_r_k=rw_r_k, rw_ln_w=rw_ln_w, rw_ln_b=rw_ln_b, w_out=w_out, xa_norm_w=xa_norm_w, mem_norm_w=mem_norm_w, xa_wq=xa_wq, xa_wk=xa_wk, xa_wv=xa_wv, xa_wo=xa_wo, ffn_norm_w=ffn_norm_w, ffn_w1=ffn_w1, ffn_w2=ffn_w2, final_norm_w=final_norm_w)
    mom_m = dict(mix_norm_w=m_mix_norm_w, w_in=m_w_in, dn_conv_w=m_dn_conv_w, dn_a_log=m_dn_a_log, dn_dt_bias=m_dn_dt_bias, dn_norm_w=m_dn_norm_w, rw_mu=m_rw_mu, rw_w0=m_rw_w0, rw_w2=m_rw_w2, rw_a0=m_rw_a0, rw_a2=m_rw_a2, rw_g2=m_rw_g2, rw_k_k=m_rw_k_k, rw_k_a=m_rw_k_a, rw_r_k=m_rw_r_k, rw_ln_w=m_rw_ln_w, rw_ln_b=m_rw_ln_b, w_out=m_w_out, xa_norm_w=m_xa_norm_w, mem_norm_w=m_mem_norm_w, xa_wq=m_xa_wq, xa_wk=m_xa_wk, xa_wv=m_xa_wv, xa_wo=m_xa_wo, ffn_norm_w=m_ffn_norm_w, ffn_w1=m_ffn_w1, ffn_w2=m_ffn_w2, final_norm_w=m_final_norm_w)
    mom_v = dict(mix_norm_w=v_mix_norm_w, w_in=v_w_in, dn_conv_w=v_dn_conv_w, dn_a_log=v_dn_a_log, dn_dt_bias=v_dn_dt_bias, dn_norm_w=v_dn_norm_w, rw_mu=v_rw_mu, rw_w0=v_rw_w0, rw_w2=v_rw_w2, rw_a0=v_rw_a0, rw_a2=v_rw_a2, rw_g2=v_rw_g2, rw_k_k=v_rw_k_k, rw_k_a=v_rw_k_a, rw_r_k=v_rw_r_k, rw_ln_w=v_rw_ln_w, rw_ln_b=v_rw_ln_b, w_out=v_w_out, xa_norm_w=v_xa_norm_w, mem_norm_w=v_mem_norm_w, xa_wq=v_xa_wq, xa_wk=v_xa_wk, xa_wv=v_xa_wv, xa_wo=v_xa_wo, ffn_norm_w=v_ffn_norm_w, ffn_w1=v_ffn_w1, ffn_w2=v_ffn_w2, final_norm_w=v_final_norm_w)
    names = list(weights)

    seq, d = x.shape[1], x.shape[2]
    dnw = d // 2
    rww = d - dnw
    nh, nb = dnw // LANES, rww // LANES
    n_mem = mem.shape[1]
    lw_dim, la_dim, lg_dim = rw_w2.shape[1], rw_a2.shape[1], rw_g2.shape[1]
    assert lw_dim + la_dim == LANES and lg_dim == LANES and dnw % LANES == 0 and rww % LANES == 0
    xs, mems, tgt = x[0], mem[0], loss_target[0]

    col_sharded = ("w_in", "xa_wo", "ffn_w1", "dn_conv_w", "rw_w2", "rw_a2", "rw_g2")
    row_sharded = ("w_out", "xa_wq", "xa_wk", "xa_wv", "ffn_w2")
    f32_payload = ("dn_conv_w", "rw_w2", "rw_a2", "rw_g2")
    sharded = col_sharded + row_sharded
    payload = {n: weights[n][0].astype(f32 if n in f32_payload else bf16) for n in sharded}
    shard_w = w_in.shape[2]
    pad_w = -(-shard_w // LANES) * LANES
    payload["w_in"] = jnp.pad(payload["w_in"], ((0, 0), (0, pad_w - shard_w)))
    first = ("w_in", "dn_conv_w", "rw_w2", "rw_a2", "rw_g2")
    mid = ("w_out", "xa_wq", "xa_wk", "xa_wv", "xa_wo")
    late = ("ffn_w1", "ffn_w2")
    gathered = dict(zip(first, gather_chips_halves("gather_first", [payload[n] for n in first])))
    ordered = lax.optimization_barrier(([gathered[n] for n in first], [payload[n] for n in mid + late]))
    gathered = dict(zip(first, ordered[0]))
    payload.update(zip(mid + late, ordered[1]))
    mid_state, tok_mid = chips_start("gather_mid_start", [payload[n] for n in mid], _whole, halves=True)
    late_state, tok_late = chips_start("gather_late_start", [payload[n] for n in late], _whole, halves=True)
    mix_norm_w_in = mix_norm_w + (tok_mid + tok_late)

    def full(n):
        g = gathered[n]
        if n in col_sharded:
            return g.transpose(1, 0, 2).reshape(g.shape[1], 4 * g.shape[2])
        return g.reshape(4 * g.shape[1], g.shape[2])

    c_rw0 = 4 * dnw + 2 * nh
    c_rw = 3 * rww
    eb_ab, eb_r = 4 * nh, 4 * nh + 1
    eb_l1 = eb_r + 3 * nb
    n_ext = -(-(eb_l1 + 2) // 4) * 4
    tbl_fwd, tbl_bwd, per_shard = w_in_layout(4 * shard_w, shard_w, c_rw0, LANES - 2 * nh, n_ext)
    w_ext = lane_regroup(
        "w_in_regroup", gathered["w_in"], tbl_fwd,
        lambda p: pl.BlockSpec((None, d, LANES),
                               lambda i, j, t: (t[REGROUP_FIELDS * p, j], 0, t[REGROUP_FIELDS * p + 1, j])),
        pl.BlockSpec((d, LANES), lambda i, j, t: (0, j)), (d, n_ext * LANES), (1, n_ext), pad_w)
    conv_w = full("dn_conv_w")
    w2p = jnp.concatenate([full("rw_w2"), jnp.zeros((la_dim, rww), f32)], axis=0)
    a2p = jnp.concatenate([jnp.zeros((lw_dim, rww), f32), full("rw_a2")], axis=0)
    g2 = full("rw_g2")
    xaw = xa_wq.shape[2]
    nxh = xaw // LANES
    ffn = 4 * ffn_w1.shape[2]

    def lane_row(vec):
        return jnp.pad(vec.reshape(1, -1), ((0, 0), (0, LANES - vec.size)))

    alog_row, dtb_row = lane_row(dn_a_log), lane_row(dn_dt_bias)
    head_of_col = jnp.arange(dnw)[None, :] // LANES
    e_g = (jnp.arange(LANES)[:, None] == head_of_col).astype(f32)
    e_b = (jnp.arange(LANES)[:, None] == head_of_col + nh).astype(f32)
    mu_main, mu_small = rw_mu[:, :c_rw], rw_mu[:, c_rw:]
    r_k_row = rw_r_k.reshape(1, rww)
    fnw = final_norm_w.reshape(1, d)
    qb, kb_, vb, zb = 0, nh, 2 * nh, 3 * nh
    rb0 = eb_r
    cc = lambda w_, o_: ("constc", w_, o_)
    rc = lambda o_: ("rowc", LANES, o_)

    (u16,) = rowcall("norm_mix", fn_rms, [(xs, "row"), (mix_norm_w_in, "const")], [(seq, d, bf16, "row")], rows=seq)
    p_main = p_small = mm("in_proj", u16, w_ext)

    fn_gconv = make_fn_gconv(2 * nh)
    (qkv,) = rowcall("gdn_conv", fn_gconv, [(p_main, rc(0)), (conv_w, cc(LANES, 0))],
                     [(seq, 3 * dnw, f32, "rowc")], rows=seq, tile=seq, ncol=3 * nh)
    gate_ins = [(p_small, rc(eb_ab)), (alog_row, "const"), (dtb_row, "const"), (e_g, "const"), (e_b, "const")]
    g_b, beta_b = rowcall("gdn_gate", fn_ggate, gate_ins, [(seq, dnw, f32, "row")] * 2, rows=seq)
    gdn_ins = [(qkv, qb), (qkv, kb_), (qkv, vb), (g_b, 0), (beta_b, 0)]
    o_raw, gdn_states, gdn_kept = scan_fwd("gdn_scan", gdn_chunk, gdn_ins, rows=seq, chunk=GDN_CHUNK, ncol=nh)
    mid_state, tok = handover_start("gather_mid_pass", chips_wait("gather_mid_wait", mid_state, o_raw))
    gpost_ins = [(o_raw, rc(0)), (p_main, rc(zb)), (dn_norm_w + tok, "const")]
    (o_dn,) = rowcall("gdn_post", fn_gpost, gpost_ins, [(seq, dnw, bf16, "rowc")], rows=seq, ncol=nh)

    (prw,) = rowcall("rw_lerp_main", fn_lerp, [(p_main, rc(rb0)), (mu_main, cc(LANES, 0))],
                     [(seq, c_rw, f32, "rowc")], rows=seq, tile=seq, ncol=3 * nb)
    (psl,) = rowcall("rw_lerp_small", fn_lerp, [(p_small, rc(eb_l1)), (mu_small, cc(LANES, 0))],
                     [(seq, 2 * LANES, f32, "rowc")], rows=seq, tile=seq, ncol=2)
    rprep_ins = [(prw, rc(nb)), (psl, "row"), (rw_w0, cc(LANES, 0)), (rw_a0, cc(LANES, 0)), (rw_k_k, cc(LANES, 0)),
                 (rw_k_a, cc(LANES, 0)), (w2p, cc(LANES, 0)), (a2p, cc(LANES, 0)), (g2, cc(LANES, 0))]
    lw, kmod, kk, a_rw, gate = rowcall("rw_prep", fn_rprep, rprep_ins, [(seq, rww, f32, "rowc")] * 5,
                                        rows=seq, ncol=nb)
    wkv_ins = [(prw, 0), (lw, 0), (kmod, 0), (prw, 2 * nb), (kk, 0), (a_rw, 0)]
    y_rw, wkv_states, wkv_kept = scan_fwd("wkv_scan", wkv_chunk, wkv_ins, rows=seq, chunk=WKV_CHUNK, ncol=nb)
    rpost_ins = [(y_rw, rc(0)), (prw, rc(0)), (kmod, rc(0)), (prw, rc(2 * nb)), (gate, rc(0)),
                 (r_k_row, cc(LANES, 0)), (rw_ln_w, cc(LANES, 0)), (rw_ln_b, cc(LANES, 0))]
    (o_rw,) = rowcall("rw_post", fn_rpost, rpost_ins, [(seq, rww, bf16, "rowc")], rows=seq, ncol=nb)

    o_cat = jnp.concatenate([o_dn, o_rw], axis=1)
    late_state, tok = handover_start("gather_late_pass", chips_wait("gather_late_wait", late_state, o_cat))
    gathered.update(zip(mid, chips_wait("gather_mid_pass_wait", mid_state, o_cat)))
    w_out_f, wq_f, wk_f, wv_f, wo_f = full("w_out"), full("xa_wq"), full("xa_wk"), full("xa_wv"), full("xa_wo")
    h1 = mm("out_proj", o_cat, w_out_f, add=xs)

    (hn16,) = rowcall("norm_xa", fn_rms, [(h1, "row"), (xa_norm_w + tok, "const")], [(seq, d, bf16, "row")],
                      rows=seq)
    (mn16,) = rowcall("norm_mem", fn_rms, [(mems, "row"), (mem_norm_w, "const")], [(n_mem, d, bf16, "row")],
                      rows=n_mem)
    q_xa = mm("xa_q", hn16, wq_f)
    k_xa = mm("xa_k", mn16, wk_f)
    v_xa = mm("xa_v", mn16, wv_f)
    xcore_ins = [(q_xa, rc(0)), (k_xa, cc(LANES, 0)), (v_xa, cc(LANES, 0))]
    (o_xa,) = rowcall("xa_core", fn_xcore, xcore_ins, [(seq, xaw, bf16, "rowc")], rows=seq, ncol=nxh)
    h2 = mm("xa_o", o_xa, wo_f, add=h1)

    (fn16,) = rowcall("norm_ffn", fn_rms, [(h2, "row"), (ffn_norm_w, "const")], [(seq, d, bf16, "row")], rows=seq)
    gathered.update(zip(late, chips_wait("gather_late_pass_wait", late_state, fn16)))
    w1_f, w2_f = full("ffn_w1"), full("ffn_w2")
    a1_16, hid16 = mm("ffn_up", fn16, w1_f, epilogue=lambda r: (r, jnp.square(jnp.maximum(r, 0.0))),
                      out_dtypes=(bf16, bf16))
    h3 = mm("ffn_down", hid16, w2_f, add=h2)

    dh3, dh3_16, d_fnw, loss_rows = rowcall(
        "loss_head", fn_final, [(h3, "row"), (tgt, "row"), (fnw, "const")],
        [(seq, d, f32, "row"), (seq, d, bf16, "row"), (1, d, f32, "acc"), (8, LANES, f32, "acc")], rows=seq)

    da1_16 = mm("ffn_down_dx", dh3_16, w2_f, tb=True, extra=[a1_16], out_dtype=bf16,
                epilogue=lambda r, a1: (r * (2.0 * jnp.maximum(a1.astype(f32), 0.0)),))
    def by_chip(n, g):
        if g.ndim == 3:
            return g
        if n in col_sharded:
            return g.reshape(g.shape[0], 4, g.shape[1] // 4).transpose(1, 0, 2)
        return g.reshape(4, g.shape[0] // 4, g.shape[1])

    g_ffn_w2 = mm("ffn_down_dw", hid16, dh3_16, ta=True, out_dtype=bf16)
    ffn_w2_g, tok = chips_start("scatter_ffn_w2_start", [by_chip("ffn_w2", g_ffn_w2)], _chip_block)
    g_ffn_w1 = mm("ffn_up_dw", fn16, da1_16, ta=True, out_dtype=bf16, by_chip=True, after=tok)
    ffn_w1_g, tok = chips_start("scatter_ffn_w1_start", [g_ffn_w1], _chip_block)
    dfn = mm("ffn_up_dx", da1_16, w1_f, tb=True, after=tok)
    dh2, d_ffn_nw, dh2_16 = rowvjp("norm_ffn_bwd", fn_rms, [(h2, "row"), (ffn_norm_w, "const")],
                                   [[(dfn, "row")]], [0, 1], rows=seq, adds=[(0, dh3, "row")], dup16=[0])

    do_xa = mm("xa_o_dx", dh2_16, wo_f, tb=True)
    g_xa_wo = mm("xa_o_dw", o_xa, dh2_16, ta=True, out_dtype=bf16, by_chip=True)
    dq_xa, dk_xa, dv_xa, dq16 = rowvjp("xa_core_bwd", fn_xcore, xcore_ins, [[(do_xa, rc(0))]], [0, 1, 2],
                                       rows=seq, ncol=nxh, dup16=[0])
    g_xa_wq = mm("xa_q_dw", hn16, dq16, ta=True, out_dtype=bf16)
    dhn = mm("xa_q_dx", dq16, wq_f, tb=True)
    dh1, d_xa_nw, dh1_16 = rowvjp("norm_xa_bwd", fn_rms, [(h1, "row"), (xa_norm_w, "const")], [[(dhn, "row")]],
                                  [0, 1], rows=seq, adds=[(0, dh2, "row")], dup16=[0])
    dk16, dv16 = dk_xa.astype(bf16), dv_xa.astype(bf16)
    g_xa_wk = mm("xa_k_dw", mn16, dk16, ta=True, out_dtype=bf16)
    g_xa_wv = mm("xa_v_dw", mn16, dv16, ta=True, out_dtype=bf16)
    dmn = mm("xa_v_dx", dv16, wv_f, tb=True, add=mm("xa_k_dx", dk16, wk_f, tb=True))
    (d_mem_nw,) = rowvjp("norm_mem_bwd", fn_rms, [(mems, "row"), (mem_norm_w, "const")], [[(dmn, "row")]], [1],
                         rows=n_mem)

    g_w_out = mm("out_proj_dw", o_cat, dh1_16, ta=True, out_dtype=bf16)
    mid_grads = dict(w_out=g_w_out, xa_wq=g_xa_wq, xa_wk=g_xa_wk, xa_wv=g_xa_wv, xa_wo=g_xa_wo)
    mid_g, tok = chips_start("scatter_mid_start", [by_chip(n, mid_grads[n]) for n in mid], _chip_block)
    do_cat = mm("out_proj_dx", dh1_16, w_out_f, tb=True, after=tok)

    dy, dr_a, dkmod_a, dv_a, dgate, d_r_k, d_ln_w, d_ln_b = rowvjp(
        "rw_post_bwd", fn_rpost, rpost_ins, [[(do_cat, rc(nh))]], [0, 1, 2, 3, 4, 5, 6, 7], rows=seq, ncol=nb)
    dr_b, dlw, dkmod_b, dv_b, dkk, da_rw = scan_bwd("wkv_scan_bwd", wkv_chunk, wkv_ins, wkv_states, wkv_kept, dy, 0,
                                                    rows=seq, chunk=WKV_CHUNK, ncol=nb)
    rprep_cts = [[(dlw, rc(0))], [(dkmod_a, rc(0)), (dkmod_b, rc(0))], [(dkk, rc(0))], [(da_rw, rc(0))],
                 [(dgate, rc(0))]]
    dpk, dpsl_parts, d_w0, d_a0, d_k_k, d_k_a, d_w2p, d_a2p, d_g2 = rowvjp(
        "rw_prep_bwd", fn_rprep, rprep_ins, rprep_cts, [0, 1, 2, 3, 4, 5, 6, 7, 8], rows=seq, ncol=nb)
    (dpsl,) = rowcall("rw_prep_sum", fn_sumcols(nb), [(dpsl_parts, "row")], [(seq, 2 * LANES, f32, "row")], rows=seq)

    def lerp_bwd(tag, p, p_off, mu, mu_off, ct_lists, ncol):
        return rowvjp("rw_lerp_bwd_" + tag, fn_lerp, [(p, rc(p_off)), (mu, cc(LANES, mu_off))], [ct_lists], [0, 1],
                      rows=seq, tile=seq, ncol=ncol, dup16=[0])

    _, dmu_r, dpr16 = lerp_bwd("r", p_main, rb0, mu_main, 0, [(dr_a, rc(0)), (dr_b, rc(0))], nb)
    _, dmu_k, dpk16 = lerp_bwd("k", p_main, rb0 + nb, mu_main, nb, [(dpk, rc(0))], nb)
    _, dmu_v, dpv16 = lerp_bwd("v", p_main, rb0 + 2 * nb, mu_main, 2 * nb, [(dv_a, rc(0)), (dv_b, rc(0))], nb)
    _, dmu_s, dps12_16 = lerp_bwd("small", p_small, eb_l1, mu_small, 0, [(dpsl, rc(0))], 2)

    do_raw, dz, d_dn_nw, dz16 = rowvjp("gdn_post_bwd", fn_gpost, gpost_ins, [[(do_cat, rc(0))]], [0, 1, 2],
                                       rows=seq, ncol=nh, dup16=[1])
    dq_g, dk_g, dv_g, dg_b, dbeta_b = scan_bwd("gdn_scan_bwd", gdn_chunk, gdn_ins, gdn_states, gdn_kept, do_raw, 0,
                                               rows=seq, chunk=GDN_CHUNK, ncol=nh)
    dps0, d_alog, d_dtb, dps0_16 = rowvjp("gdn_gate_bwd", fn_ggate, gate_ins, [[(dg_b, "row")], [(dbeta_b, "row")]],
                                          [0, 1, 2], rows=seq, dup16=[0])
    dqkv = jnp.concatenate([dq_g, dk_g, dv_g], axis=1)
    _, d_conv_w, dqkv16 = rowvjp("gdn_conv_bwd", fn_gconv, [(p_main, rc(0)), (conv_w, cc(LANES, 0))],
                                 [[(dqkv, rc(0))]], [0, 1], rows=seq, tile=seq, ncol=3 * nh, dup16=[0])

    dp16 = jnp.concatenate([dqkv16, dz16, dps0_16, dpr16, dpk16, dpv16, dps12_16,
                            jnp.zeros((seq, (n_ext - eb_l1 - 2) * LANES), bf16)], axis=1)
    g_w_ext = mm("in_proj_dw", u16, dp16, ta=True, out_dtype=bf16)
    g_w_in = lane_regroup(
        "w_in_grad_regroup", g_w_ext, tbl_bwd,
        lambda p: pl.BlockSpec((d, LANES), lambda i, j, t: (0, t[REGROUP_FIELDS * p + 1, j])),
        pl.BlockSpec((None, d, LANES), lambda i, j, t: (j // per_shard, 0, j % per_shard)),
        (4, d, pad_w), (1, 4 * per_shard), n_ext * LANES)
    first_grads = dict(w_in=g_w_in, dn_conv_w=d_conv_w, rw_w2=d_w2p[:lw_dim], rw_a2=d_a2p[lw_dim:], rw_g2=d_g2)
    first_g, tok = chips_start("scatter_first_start", [by_chip(n, first_grads[n]) for n in first], _chip_block)

    du = mm("in_proj_dx", dp16, w_ext, tb=True, after=tok)
    grad_x, d_mix_nw = rowvjp("norm_mix_bwd", fn_rms, [(xs, "row"), (mix_norm_w, "const")], [[(du, "row")]],
                              [0, 1], rows=seq, adds=[(0, dh1, "row")])
    received = dict(zip(mid, chips_wait("scatter_mid_wait", mid_g, grad_x)))
    received["ffn_w2"] = chips_wait("scatter_ffn_w2_wait", ffn_w2_g, grad_x)[0]
    received["ffn_w1"] = chips_wait("scatter_ffn_w1_wait", ffn_w1_g, grad_x)[0]

    out = {}

    def reduce_and_update(tag, group):
        partial = [sum_slots("sum_chips_" + n, received[n]) for n in group]
        other = swap_sibling("swap_sibling_" + tag, partial)
        for n, p_mine, p_other in zip(group, partial, other):
            if weights[n].shape[2] % LANES:
                rows, cols = weights[n].shape[1:]
                lin = lambda a: jnp.swapaxes(a, 1, 2).reshape(-1, LANES)
                lin_g = lambda p: p.T[:cols].reshape(-1, LANES)
                res = adamw("adamw_" + n, lin(weights[n]), [lin_g(p_mine), lin_g(p_other)], lin(mom_m[n]),
                            lin(mom_v[n]))
                out[n] = [jnp.swapaxes(r.reshape(1, cols, rows), 1, 2) for r in res]
            else:
                out[n] = adamw("adamw_" + n, weights[n], [p_mine, p_other], mom_m[n], mom_v[n])

    reduce_and_update("rest", mid + late)
    received.update(zip(first, chips_wait("scatter_first_wait", first_g, [out[n][1] for n in mid + late])))
    reduce_and_update("first", first)

    small_names = [n for n in names if n not in sharded]
    small_local = dict(
        mix_norm_w=d_mix_nw, dn_a_log=d_alog[:, :nh], dn_dt_bias=d_dtb[:, :nh], dn_norm_w=d_dn_nw,
        rw_mu=jnp.concatenate([dmu_r, dmu_k, dmu_v, dmu_s], axis=1), rw_w0=d_w0, rw_a0=d_a0, rw_k_k=d_k_k,
        rw_k_a=d_k_a, rw_r_k=d_r_k, rw_ln_w=d_ln_w, rw_ln_b=d_ln_b, xa_norm_w=d_xa_nw, mem_norm_w=d_mem_nw,
        ffn_norm_w=d_ffn_nw, final_norm_w=d_fnw)
    loss_vec = jnp.where(jnp.arange(LANES) == 0, loss_rows[0], 0.0)
    (g8,) = gather_all([_pack([small_local[n] for n in small_names] + [loss_vec])], after=[out["w_in"][1]])

    packed_like = [weights[n] for n in small_names] + [loss_vec]
    zero = jnp.zeros((LANES,), f32)
    res = adamw_packed(_pack([weights[n] for n in small_names] + [zero]), g8,
                       _pack([mom_m[n] for n in small_names] + [zero]),
                       _pack([mom_v[n] for n in small_names] + [zero]))
    unpacked = [_unpack(r, packed_like) for r in res]
    for i, n in enumerate(small_names):
        out[n] = [u[i] for u in unpacked]
    loss = unpacked[0][-1][0]

    return (loss, grad_x.reshape(x.shape), *[out[n][0] for n in names], *[out[n][1] for n in names],
            *[out[n][2] for n in names], *[out[n][3] for n in names])
```

```python
import functools

import jax
import jax.numpy as jnp
from jax import lax
from jax.experimental import pallas as pl
from jax.experimental.pallas import tpu as pltpu

f32 = jnp.float32
bf16 = jnp.bfloat16
HI = lax.Precision.HIGHEST
MESH = pl.DeviceIdType.MESH

LANES = 128
VMEM_LIMIT = 56 << 20
TOK_TILE = 256
TOK_TILE_BLOCKED = 1024
MM_TILE = 1024
MM_TILE_K = 2048
GDN_CHUNK = 128
WKV_CHUNK = 64
SCAN_GROUP = 8
P_BULK = 1
P_INV = 1
P_RESID = 3
P_CUMSUM = 3
RMS_EPS = 1e-6
RW_GN_EPS = 64e-5
RW_HEAD = 64

ADAM_LR, ADAM_B1, ADAM_B2, ADAM_EPS, ADAM_WD, ADAM_STEP = 0.001, 0.9, 0.999, 1e-08, 0.01, 10


def _params(n_grid):
    return pltpu.CompilerParams(dimension_semantics=("arbitrary",) * n_grid, vmem_limit_bytes=VMEM_LIMIT)


_DIMS = {"nn": (((1,), (0,)), ((), ())), "nt": (((1,), (1,)), ((), ())), "tn": (((0,), (0,)), ((), ()))}
_DIMS_BATCHED = {"nn": (((2,), (1,)), ((0,), (0,))), "nt": (((2,), (2,)), ((0,), (0,))),
                 "tn": (((1,), (1,)), ((0,), (0,)))}


def _raw_dot(a, b, mode, passes):
    dims = (_DIMS if a.ndim == 2 else _DIMS_BATCHED)[mode]
    if passes == 6:
        return lax.dot_general(a.astype(f32), b.astype(f32), dims, precision=HI, preferred_element_type=f32)
    ah, bh = a.astype(bf16), b.astype(bf16)
    r = lax.dot_general(ah, bh, dims, preferred_element_type=f32)
    if passes == 3:
        al = (a - ah.astype(f32)).astype(bf16)
        bl = (b - bh.astype(f32)).astype(bf16)
        r = r + lax.dot_general(al, bh, dims, preferred_element_type=f32)
        r = r + lax.dot_general(ah, bl, dims, preferred_element_type=f32)
    return r


@functools.partial(jax.custom_vjp, nondiff_argnums=(2, 3))
def pdot(a, b, mode, passes):
    return _raw_dot(a, b, mode, passes)


def _pdot_bwd(mode, passes, res, g):
    a, b = res
    if mode == "nn":
        da, db = _raw_dot(g, b, "nt", passes), _raw_dot(a, g, "tn", passes)
    elif mode == "nt":
        da, db = _raw_dot(g, b, "nn", passes), _raw_dot(g, a, "tn", passes)
    else:
        da, db = _raw_dot(b, g, "nt", passes), _raw_dot(a, g, "nn", passes)
    return da.astype(a.dtype), db.astype(b.dtype)


pdot.defvjp(lambda a, b, mode, passes: (_raw_dot(a, b, mode, passes), (a, b)), _pdot_bwd)


def hdot(a, b):
    return pdot(a, b, "nn", 6)


def bdot(a, b):
    return pdot(a, b, "nn", 1)


def bdot_nt(a, b):
    return pdot(a, b, "nt", 1)


def _shift_rows(x, k):
    row = lax.broadcasted_iota(jnp.int32, x.shape, 0)
    return jnp.where(row < k, 0.0, pltpu.roll(x, k, axis=0))


def _unshift_rows(g, k):
    n = g.shape[0]
    row = lax.broadcasted_iota(jnp.int32, g.shape, 0)
    return jnp.where(row >= n - k, 0.0, pltpu.roll(g, n - k, axis=0))


@functools.partial(jax.custom_vjp, nondiff_argnums=(1,))
def tshift(x, k):
    return _shift_rows(x, k)


tshift.defvjp(lambda x, k: (_shift_rows(x, k), None), lambda k, _, g: (_unshift_rows(g, k),))


def rms(x, w):
    x = x.astype(f32)
    return x * lax.rsqrt(jnp.mean(x * x, axis=-1, keepdims=True) + RMS_EPS) * w


def softplus(x):
    return jnp.maximum(x, 0.0) + jnp.log(1.0 + jnp.exp(-jnp.abs(x)))


def seg2sum(x):
    lo = lax.broadcasted_iota(jnp.int32, x.shape, 1) < RW_HEAD
    s_lo = jnp.sum(jnp.where(lo, x, 0.0), axis=-1, keepdims=True)
    s_hi = jnp.sum(jnp.where(lo, 0.0, x), axis=-1, keepdims=True)
    return jnp.where(lo, s_lo, s_hi)


def _tile(n, pref):
    if n <= pref:
        return n
    t = pref
    while t >= LANES:
        if n % t == 0:
            return t
        t -= LANES
    return n


def mm(name, a, b, *, ta=False, tb=False, add=None, out_dtype=f32, by_chip=False, epilogue=None, extra=(),
       out_dtypes=None, after=None):
    (k, m) = a.shape if ta else a.shape[::-1]
    (n, kb) = b.shape if tb else b.shape[::-1]
    assert k == kb, (name, a.shape, b.shape)
    tm, tk = _tile(m, MM_TILE), _tile(k, MM_TILE_K)
    tn = _tile(n // 4, MM_TILE) if by_chip else _tile(n, MM_TILE)
    nk = k // tk
    dims = (((0,) if ta else (1,), (1,) if tb else (0,)), ((), ()))
    extra = list(extra) + ([] if add is None else [add])
    out_dtypes = [out_dtype] if out_dtypes is None else list(out_dtypes)
    n_extra, n_out = len(extra), len(out_dtypes)
    n_after = 0 if after is None else 1

    def body(*refs):
        a_ref, b_ref = refs[:2]
        x_refs = refs[2:2 + n_extra]
        o_refs = refs[2 + n_extra + n_after:2 + n_extra + n_after + n_out]
        part = lax.dot_general(a_ref[...].astype(bf16), b_ref[...].astype(bf16), dims, preferred_element_type=f32)

        def finish(r):
            xs = [x[...] for x in x_refs]
            if add is not None:
                r = r + xs.pop().astype(f32)
            outs = (r,) if epilogue is None else epilogue(r, *xs)
            for o_ref, o in zip(o_refs, outs):
                o_ref[...] = o.astype(o_ref.dtype)

        if nk == 1:
            finish(part)
            return
        acc = refs[-1]
        kk = pl.program_id(2)

        @pl.when(kk == 0)
        def _():
            acc[...] = part

        @pl.when(kk > 0)
        def _():
            acc[...] += part

        @pl.when(kk == nk - 1)
        def _():
            finish(acc[...])

    a_spec = pl.BlockSpec((tk, tm), lambda i, j, q: (q, i)) if ta else pl.BlockSpec((tm, tk), lambda i, j, q: (i, q))
    b_spec = pl.BlockSpec((tn, tk), lambda i, j, q: (j, q)) if tb else pl.BlockSpec((tk, tn), lambda i, j, q: (q, j))
    x_spec = pl.BlockSpec((tm, tn), lambda i, j, q: (i, j))
    if by_chip:
        per_chip = n // 4 // tn
        o_spec = pl.BlockSpec((None, tm, tn), lambda i, j, q: (j // per_chip, i, j % per_chip))
        o_shape = (4, m, n // 4)
    else:
        o_spec, o_shape = x_spec, (m, n)
    afters = [] if after is None else [jnp.reshape(after, (1, 1))]
    res = pl.pallas_call(
        body, name=name, grid=(m // tm, n // tn, nk),
        in_specs=[a_spec, b_spec] + [x_spec] * n_extra + [pl.BlockSpec(memory_space=pl.ANY)] * n_after,
        out_specs=[o_spec] * n_out, out_shape=[jax.ShapeDtypeStruct(o_shape, dt) for dt in out_dtypes],
        scratch_shapes=[pltpu.VMEM((tm, tn), f32)] if nk > 1 else [],
        compiler_params=_params(3))(a, b, *extra, *afters)
    return res[0] if n_out == 1 else res


REGROUP_PIECES = 4


REGROUP_FIELDS = 5


def _regroup_table(n_out, sources_of):
    import numpy as np
    tbl = np.zeros((REGROUP_FIELDS * REGROUP_PIECES, n_out), np.int32)
    for j in range(n_out):
        groups = sorted(sources_of(j).items())
        assert len(groups) <= REGROUP_PIECES, (j, len(groups))
        for p in range(REGROUP_PIECES):
            if p < len(groups):
                key, lanes = groups[p]
                shifts = {q - s for s, q in lanes}
                qs = sorted(q for _, q in lanes)
                assert len(shifts) == 1 and qs == list(range(qs[0], qs[-1] + 1)), (j, key)
                row = (key[0], key[1], shifts.pop(), qs[0], qs[-1] + 1)
            else:
                row = (0, 0, 0, 0, 0)
            tbl[REGROUP_FIELDS * p:REGROUP_FIELDS * (p + 1), j] = row
    return jnp.asarray(tbl)


def lane_regroup(name, src, table, src_spec, out_spec, out_shape, grid, src_width):
    rows = src.shape[-2]

    def body(tbl, *refs):
        o_ref = refs[REGROUP_PIECES]
        j = pl.program_id(1)

        def moved(p):
            blk, shift, lo, hi = (tbl[REGROUP_FIELDS * p + f, j] for f in range(1, REGROUP_FIELDS))
            x = refs[p][...]
            if src_width % LANES:
                lane = lax.broadcasted_iota(jnp.int32, (rows, LANES), 1)
                x = jnp.where(lane < src_width - blk * LANES, x, jnp.zeros((), src.dtype))
            pi = lax.broadcasted_iota(jnp.int32, (LANES, LANES), 0)
            qi = lax.broadcasted_iota(jnp.int32, (LANES, LANES), 1)
            sel = jnp.logical_and(qi - pi == shift, jnp.logical_and(qi >= lo, qi < hi))
            return jnp.dot(x, sel.astype(src.dtype), preferred_element_type=f32).astype(o_ref.dtype)

        o_ref[...] = moved(0)
        for p in range(1, REGROUP_PIECES):
            @pl.when(tbl[REGROUP_FIELDS * p + 4, j] > tbl[REGROUP_FIELDS * p + 3, j])
            def _(p=p):
                o_ref[...] += moved(p)

    return pl.pallas_call(
        body, name=name, out_shape=jax.ShapeDtypeStruct(out_shape, src.dtype),
        grid_spec=pltpu.PrefetchScalarGridSpec(
            num_scalar_prefetch=1, grid=grid, in_specs=[src_spec(p) for p in range(REGROUP_PIECES)],
            out_specs=out_spec),
        compiler_params=_params(2))(table, *[src] * REGROUP_PIECES)


def w_in_layout(d_cols, shard_w, c_split, gap, n_blocks):
    def ext_of(c):
        return c if c < c_split else c + gap

    def fwd_sources(j):
        groups = {}
        for q in range(LANES):
            e = j * LANES + q
            c = e if e < c_split else e - gap
            if (c_split <= e < c_split + gap) or c >= d_cols:
                continue
            s, l = divmod(c, shard_w)
            groups.setdefault((s, l // LANES), []).append((l % LANES, q))
        return groups

    per_shard = -(-shard_w // LANES)

    def bwd_sources(j):
        s, b = divmod(j, per_shard)
        groups = {}
        for q in range(LANES):
            l = b * LANES + q
            if l >= shard_w:
                continue
            e = ext_of(s * shard_w + l)
            groups.setdefault((0, e // LANES), []).append((e % LANES, q))
        return groups

    return _regroup_table(n_blocks, fwd_sources), _regroup_table(4 * per_shard, bwd_sources), per_shard


def _in_spec(a, kind, tile):
    if kind == "row":
        return pl.BlockSpec((tile, a.shape[1]), lambda j, i: (i, 0))
    if kind == "const":
        return pl.BlockSpec(a.shape, lambda j, i: (0, 0))
    tag, cw, off = kind
    if tag == "rowc":
        return pl.BlockSpec((tile, cw), lambda j, i: (i, j + off))
    assert tag == "constc", kind
    return pl.BlockSpec((a.shape[0], cw), lambda j, i: (0, j + off))


def rowcall(name, fn, ins, outs, *, rows, tile=None, ncol=1):
    if tile is None:
        tile = min(TOK_TILE_BLOCKED if ncol > 1 else TOK_TILE, rows)
    n_in = len(ins)
    kinds = [o[3] for o in outs]

    def body(*refs):
        j, i = pl.program_id(0), pl.program_id(1)
        res = fn(*[r[...] for r in refs[:n_in]])
        for ref, val, kind in zip(refs[n_in:], res, kinds):
            if kind in ("row", "rowc"):
                ref[...] = val.astype(ref.dtype)
            else:
                first = (i == 0) if kind == "accc" else jnp.logical_and(i == 0, j == 0)

                @pl.when(first)
                def _(ref=ref, val=val):
                    ref[...] = val.astype(ref.dtype)

                @pl.when(jnp.logical_not(first))
                def _(ref=ref, val=val):
                    ref[...] += val.astype(ref.dtype)

    out_shape, out_specs = [], []
    for nr, nc, dtype, kind in outs:
        out_shape.append(jax.ShapeDtypeStruct((nr, nc), dtype))
        if kind == "row":
            out_specs.append(pl.BlockSpec((tile, nc), lambda j, i: (i, 0)))
        elif kind == "rowc":
            out_specs.append(pl.BlockSpec((tile, nc // ncol), lambda j, i: (i, j)))
        elif kind == "acc":
            out_specs.append(pl.BlockSpec((nr, nc), lambda j, i: (0, 0)))
        else:
            out_specs.append(pl.BlockSpec((nr, nc // ncol), lambda j, i: (0, j)))
    return pl.pallas_call(
        body, name=name, grid=(ncol, rows // tile), in_specs=[_in_spec(a, k, tile) for a, k in ins],
        out_specs=out_specs, out_shape=out_shape, compiler_params=_params(2))(*[a for a, _ in ins])


def rowvjp(name, fn, ins, cts, grads, *, rows, tile=None, ncol=1, adds=(), dup16=()):
    n_in = len(ins)
    ct_sizes = [len(c) for c in cts]
    flat_cts = [m for c in cts for m in c]
    n_ct = len(flat_cts)

    def wrapped(*vals):
        xs = list(vals[:n_in])
        gs = vals[n_in:n_in + n_ct]
        extra = vals[n_in + n_ct:]

        def f(*dvars):
            full = list(xs)
            for k, v in zip(grads, dvars):
                full[k] = v
            return fn(*full)

        outs, pull = jax.vjp(f, *[xs[k] for k in grads])
        cot, p = [], 0
        for o, size in zip(outs, ct_sizes):
            g = gs[p].astype(f32)
            for q in range(1, size):
                g = g + gs[p + q].astype(f32)
            cot.append(g.astype(o.dtype))
            p += size
        gv = list(pull(tuple(cot)))
        for (pos, _, _), e in zip(adds, extra):
            gv[pos] = gv[pos] + e.astype(gv[pos].dtype)
        return tuple(gv) + tuple(gv[pos] for pos in dup16)

    outs = []
    for k in grads:
        a, kind = ins[k]
        if kind == "row":
            outs.append((rows, a.shape[1] * ncol, f32, "rowc") if ncol > 1 else (rows, a.shape[1], f32, "row"))
        elif kind == "const":
            outs.append((a.shape[0], a.shape[1], f32, "acc"))
        elif kind[0] == "rowc":
            outs.append((rows, kind[1] * ncol, f32, "rowc"))
        else:
            outs.append((a.shape[0], kind[1] * ncol, f32, "accc"))
    for pos in dup16:
        nr, nc, _, kind = outs[pos]
        outs.append((nr, nc, bf16, kind))
    all_ins = list(ins) + flat_cts + [(a, kind) for _, a, kind in adds]
    return rowcall(name, wrapped, all_ins, outs, rows=rows, tile=tile, ncol=ncol)


def fn_rms(x, w):
    return (rms(x, w),)


def make_fn_gconv(n_norm_blocks):
    def fn(p, cw):
        c = cw[3:4] * p
        for jj in range(3):
            c = c + cw[jj:jj + 1] * tshift(p, 3 - jj)
        s = c * jax.nn.sigmoid(c)
        nrm = s * lax.rsqrt(jnp.sum(s * s, axis=-1, keepdims=True) + 1e-6)
        return (jnp.where(pl.program_id(0) < n_norm_blocks, nrm, s),)
    return fn


def fn_ggate(ps0, alog, dtb, e_g, e_b):
    g = -jnp.exp(alog) * softplus(ps0 + dtb)
    beta = jax.nn.sigmoid(ps0)
    return pdot(g, e_g, "nn", 3), pdot(beta, e_b, "nn", 3)


def fn_gpost(o, z, nw):
    return (rms(o, nw) * (z * jax.nn.sigmoid(z)),)


def fn_lerp(p, mu):
    return (p + (tshift(p, 1) - p) * mu,)


def fn_rprep(pk, psl, w0, a0, k_k, k_a, w2p, a2p, g2):
    g1, g2in = psl[:, :LANES], psl[:, LANES:]
    log_w = -softplus(-(w0 + bdot(jnp.tanh(g1), w2p))) - 0.5
    lw = -jnp.exp(log_w)
    a = jax.nn.sigmoid(a0 + bdot(g1, a2p))
    gate = bdot(jax.nn.sigmoid(g2in), g2)
    kkr = pk * k_k
    kk = kkr / jnp.maximum(jnp.sqrt(seg2sum(kkr * kkr)), 1e-12)
    kmod = pk * (1.0 + (a - 1.0) * k_a)
    return lw, kmod, kk, a, gate


def fn_rpost(y, r, kmod, v, gate, r_k, ln_w, ln_b):
    inv_n = 1.0 / RW_HEAD
    mean = seg2sum(y) * inv_n
    d = y - mean
    var = seg2sum(d * d) * inv_n
    yn = d * lax.rsqrt(var + RW_GN_EPS) * ln_w + ln_b
    bonus = seg2sum(r * kmod * r_k) * v
    return ((yn + bonus) * gate,)


def fn_xcore(q, k, v):
    s = bdot_nt(q, k) * (LANES ** -0.5)
    p = jax.nn.softmax(s, axis=-1)
    return (bdot(p, v),)


def fn_final(h, tgt, w):
    def loss_fn(h, w):
        return 0.5 * jnp.sum(jnp.mean(jnp.square(rms(h, w) - tgt), axis=-1))

    val, (dh, dw) = jax.value_and_grad(loss_fn, argnums=(0, 1))(h, w)
    return dh, dh.astype(bf16), dw, jnp.full((8, LANES), val, f32)


def fn_sumcols(n):
    def fn(x):
        w = x.shape[1] // n
        s = x[:, :w]
        for q in range(1, n):
            s = s + x[:, q * w:(q + 1) * w]
        return (s,)
    return fn


def _tri(c):
    ii = lax.broadcasted_iota(jnp.int32, (c, c), 0)
    jj = lax.broadcasted_iota(jnp.int32, (c, c), 1)
    return ii, jj


def _neumann_raw(m, steps):
    c = m.shape[-1]
    ii, jj = _tri(c)
    eye = (ii == jj).astype(f32)
    t, p = eye + m, m
    for _ in range(steps):
        p = _raw_dot(p, p, "nn", P_INV)
        t = _raw_dot(t, eye + p, "nn", P_INV)
    resid = eye - t + _raw_dot(m, t, "nn", P_RESID)
    return t + _raw_dot(t, resid, "nn", P_INV)


@functools.partial(jax.custom_vjp, nondiff_argnums=(1,))
def _neumann_inverse(m, steps):
    return _neumann_raw(m, steps)


def _neumann_fwd(m, steps):
    t = _neumann_raw(m, steps)
    return t, t


def _neumann_bwd(steps, t, g):
    return (_raw_dot(_raw_dot(t, g, "tn", P_RESID), t, "nt", P_RESID),)


_neumann_inverse.defvjp(_neumann_fwd, _neumann_bwd)


@jax.custom_vjp
def _known_inverse(m, t):
    return t


_known_inverse.defvjp(lambda m, t: (t, t), lambda t, g: (_neumann_bwd(0, t, g)[0], jnp.zeros_like(t)))


def _inverse(m, steps, kept):
    return _neumann_inverse(m, steps) if kept is None else _known_inverse(m, kept)


def cumsum_rows(x):
    t = x.shape[1]
    ii, jj = _tri(t)
    tri = jnp.broadcast_to((ii >= jj).astype(f32), (x.shape[0], t, t))
    return pdot(tri, x, "nn", P_CUMSUM)


def gdn_chunk(q, k, v, gb, bb, s, kept=None):
    c = q.shape[1]
    ii, jj = _tri(c)
    low = ii >= jj
    gcb = cumsum_rows(gb)
    gl = jnp.sum(gb, axis=1, keepdims=True)
    gc_col = gcb[:, :, :c]
    diff = gc_col - jnp.swapaxes(gc_col, 1, 2)
    decay = jnp.where(low, jnp.exp(jnp.where(low, diff, 0.0)), 0.0)
    qs = q * (q.shape[2] ** -0.5)
    kb = k * bb
    with_k = pdot(jnp.concatenate([kb, qs], axis=1), k, "nt", P_BULK)
    a = jnp.where(ii > jj, with_k[:, :c] * decay, 0.0)
    attn = with_k[:, c:] * decay
    t = _inverse(-a, c.bit_length() - 2, kept)
    eg = jnp.exp(gcb)
    uw = pdot(t, jnp.concatenate([v * bb, kb * eg], axis=2), "nn", P_BULK)
    u, w = uw[:, :, :LANES], uw[:, :, LANES:]
    kd = k * jnp.exp(gl - gcb)
    from_state = pdot(jnp.concatenate([w, qs * eg], axis=1), s, "nn", P_BULK)
    v_new = u - from_state[:, :c]
    o = from_state[:, c:] + pdot(attn, v_new, "nn", P_BULK)
    s_new = s * jnp.exp(gl) + pdot(kd, v_new, "tn", P_BULK)
    return o, s_new, t


def wkv_chunk(r, lw, k, v, kk, a, s, kept=None):
    t = r.shape[1]
    ii, jj = _tri(t)
    lo = lax.broadcasted_iota(jnp.int32, r.shape, 2) < RW_HEAD
    cl = cumsum_rows(lw)
    cl_last = jnp.sum(lw, axis=1, keepdims=True)
    al = -kk * jnp.exp(cl - lw)
    be = (a * kk) * jnp.exp(-cl)
    kt = k * jnp.exp(-cl)
    rt = r * jnp.exp(cl)

    def dot(xa, xb, mode="nn"):
        return pdot(xa, xb, mode, P_BULK)

    def sel(x_lo, x_hi):
        return jnp.where(lo, x_lo, x_hi)

    left = jnp.concatenate([jnp.where(lo, al, 0.0), jnp.where(lo, 0.0, al),
                            jnp.where(lo, rt, 0.0), jnp.where(lo, 0.0, rt)], axis=1)
    with_be, with_kt = dot(left, be, "nt"), dot(left, kt, "nt")

    def blocks(prod):
        below, upto = ii > jj, ii >= jj
        return [jnp.where(msk, prod[:, q * t:(q + 1) * t], 0.0) for q, msk in enumerate((below, below, upto, upto))]

    ab_lo, ab_hi, rb_lo, rb_hi = blocks(with_be)
    ak_lo, ak_hi, rk_lo, rk_hi = blocks(with_kt)
    from_state = dot(jnp.concatenate([al, rt], axis=1), s, "nt")
    x = from_state[:, :t] + sel(dot(ak_lo, v), dot(ak_hi, v))
    steps = t.bit_length() - 2
    inv_lo = _inverse(ab_lo, steps, None if kept is None else kept[:, :t])
    inv_hi = _inverse(ab_hi, steps, None if kept is None else kept[:, t:])
    u = sel(dot(inv_lo, x), dot(inv_hi, x))
    y = from_state[:, t:] + sel(dot(rb_lo, u) + dot(rk_lo, v), dot(rb_hi, u) + dot(rk_hi, v))
    vi = lax.broadcasted_iota(jnp.int32, s.shape, 1) < RW_HEAD
    ki = lax.broadcasted_iota(jnp.int32, s.shape, 2) < RW_HEAD
    s_new = jnp.where(vi == ki, (s + dot(u, be, "tn") + dot(v, kt, "tn")) * jnp.exp(cl_last), 0.0)
    return y, s_new, jnp.concatenate([inv_lo, inv_hi], axis=1)


def _scan_group(ncol, offs):
    g = SCAN_GROUP
    while g > 1 and (ncol % g or any(o % g for o in offs)):
        g //= 2
    return g


def scan_fwd(name, chunk_fn, ins, *, rows, chunk, ncol):
    n = rows // chunk
    n_in = len(ins)
    grp = _scan_group(ncol, [off for _, off in ins])
    tile = jax.ShapeDtypeStruct((grp, chunk, LANES), f32)
    kept = jax.eval_shape(chunk_fn, *[tile] * n_in, jax.ShapeDtypeStruct((grp, LANES, LANES), f32))[2].shape[1:]

    def body(*refs):
        o_ref, st_ref, kept_ref, s_scr = refs[n_in:]

        @pl.when(pl.program_id(1) == 0)
        def _():
            s_scr[...] = jnp.zeros_like(s_scr)

        cols = [slice(b * LANES, (b + 1) * LANES) for b in range(grp)]
        s = s_scr[...]
        st_ref[...] = s
        o, s_new, inv = chunk_fn(*[jnp.stack([r[:, c] for c in cols]) for r in refs[:n_in]], s)
        for b, c in enumerate(cols):
            o_ref[:, c] = o[b]
        kept_ref[...] = inv
        s_scr[...] = s_new

    def spec(off):
        return pl.BlockSpec((chunk, grp * LANES), lambda h, c: (c, h + off // grp))

    def per_chunk(shape):
        return pl.BlockSpec((grp, None) + tuple(shape), lambda h, c: (h, c, 0, 0))

    return pl.pallas_call(
        body, name=name, grid=(ncol // grp, n), in_specs=[spec(off) for _, off in ins],
        out_specs=[spec(0), per_chunk((LANES, LANES)), per_chunk(kept)],
        out_shape=[jax.ShapeDtypeStruct((rows, ncol * LANES), f32),
                   jax.ShapeDtypeStruct((ncol, n, LANES, LANES), f32),
                   jax.ShapeDtypeStruct((ncol, n) + tuple(kept), f32)],
        scratch_shapes=[pltpu.VMEM((grp, LANES, LANES), f32)], compiler_params=_params(2))(*[a for a, _ in ins])


def scan_bwd(name, chunk_fn, ins, states, kept, d_out, d_off, *, rows, chunk, ncol):
    n = rows // chunk
    n_in = len(ins)
    grp = _scan_group(ncol, [off for _, off in ins] + [d_off])

    def body(*refs):
        st_ref, kept_ref, do_ref = refs[n_in:n_in + 3]
        g_refs = refs[n_in + 3:2 * n_in + 3]
        ds_scr = refs[-1]

        @pl.when(pl.program_id(1) == 0)
        def _():
            ds_scr[...] = jnp.zeros_like(ds_scr)

        cols = [slice(b * LANES, (b + 1) * LANES) for b in range(grp)]

        def batch(ref):
            return jnp.stack([ref[:, c] for c in cols])

        inv = kept_ref[...]
        _, pull = jax.vjp(lambda *a: chunk_fn(*a, kept=inv)[:2], *[batch(r) for r in refs[:n_in]], st_ref[...])
        gs = pull((batch(do_ref), ds_scr[...]))
        for ref, g in zip(g_refs, gs[:n_in]):
            for b, c in enumerate(cols):
                ref[:, c] = g[b]
        ds_scr[...] = gs[n_in]

    def spec(off):
        return pl.BlockSpec((chunk, grp * LANES), lambda h, c: (n - 1 - c, h + off // grp))

    def per_chunk(shape):
        return pl.BlockSpec((grp, None) + tuple(shape), lambda h, c: (h, n - 1 - c, 0, 0))

    return pl.pallas_call(
        body, name=name, grid=(ncol // grp, n),
        in_specs=[spec(off) for _, off in ins] + [per_chunk(states.shape[2:]), per_chunk(kept.shape[2:]), spec(d_off)],
        out_specs=[spec(0)] * n_in, out_shape=[jax.ShapeDtypeStruct((rows, ncol * LANES), f32)] * n_in,
        scratch_shapes=[pltpu.VMEM((grp, LANES, LANES), f32)],
        compiler_params=_params(2))(*[a for a, _ in ins], states, kept, d_out)


def flip_exchange(name, arrs, flips, n_slots, slot_of, src_of, with_self, after=()):
    n = len(arrs)
    nf = len(flips)
    n_after = len(after)

    def body(*refs):
        ins, outs = refs[:n], refs[n + n_after:2 * n + n_after]
        send, recv, lsem = refs[2 * n + n_after:]
        me = (lax.axis_index("x"), lax.axis_index("y"), lax.axis_index("c"))
        copies = []
        for k in range(n):
            if with_self:
                cp = pltpu.make_async_copy(src_of(ins[k], me), outs[k].at[slot_of(me)], lsem.at[k])
                cp.start()
                copies.append(cp)
            for j, fl in enumerate(flips):
                peer = tuple(1 - m if f else m for m, f in zip(me, fl))
                cp = pltpu.make_async_remote_copy(
                    src_ref=src_of(ins[k], peer), dst_ref=outs[k].at[slot_of(me)], send_sem=send.at[k, j],
                    recv_sem=recv.at[k, j], device_id=peer, device_id_type=MESH)
                cp.start()
                copies.append(cp)
        for cp in copies:
            cp.wait()

    def out_sds(a):
        blk = src_of(jax.ShapeDtypeStruct(a.shape, a.dtype), None)
        return jax.ShapeDtypeStruct((n_slots,) + tuple(blk), a.dtype)

    any_spec = pl.BlockSpec(memory_space=pl.ANY)
    return pl.pallas_call(
        body, name=name, in_specs=[any_spec] * (n + n_after), out_specs=[any_spec] * n,
        out_shape=[out_sds(a) for a in arrs],
        scratch_shapes=[pltpu.SemaphoreType.DMA((n, nf)), pltpu.SemaphoreType.DMA((n, nf)),
                        pltpu.SemaphoreType.DMA((n,))],
        compiler_params=pltpu.CompilerParams(has_side_effects=True))(*arrs, *after)


_CHIP_FLIPS = ((1, 0, 0), (0, 1, 0), (1, 1, 0))
_ALL_FLIPS = ((0, 0, 1), (0, 1, 0), (0, 1, 1), (1, 0, 0), (1, 0, 1), (1, 1, 0), (1, 1, 1))


def _whole(ref, pos):
    return ref.shape if pos is None else ref


def _chip_block(ref, pos):
    return ref.shape[1:] if pos is None else ref.at[2 * pos[0] + pos[1]]


def _chip_slot(p):
    return 2 * p[0] + p[1]


def gather_chips(name, arrs):
    return flip_exchange(name, arrs, _CHIP_FLIPS, 4, _chip_slot, _whole, True)


def scatter_chips(name, arrs):
    return flip_exchange(name, arrs, _CHIP_FLIPS, 4, _chip_slot, _chip_block, True)


_HBM = pl.BlockSpec(memory_space=pltpu.HBM)
_SEM = pl.BlockSpec(memory_space=pltpu.SEMAPHORE)
_DATAFLOW = pltpu.SideEffectType.DATAFLOW_SIDE_EFFECTING


def _split_copies(mode, refs, n, send, recv):
    me = (lax.axis_index("x"), lax.axis_index("y"), lax.axis_index("c"))
    sib = (me[0], me[1], 1 - me[2])
    lands = refs[:n] if mode == "handover" else refs[n:2 * n]
    copies = []
    for k, land in enumerate(lands):
        half = land.shape[1] // 2
        mine = pl.ds(pl.multiple_of(me[2] * half, 16), half)
        for j, fl in enumerate(_CHIP_FLIPS):
            peer = tuple(1 - m if f else m for m, f in zip(me, fl))
            if mode == "gather":
                src, dst, to = refs[k], land.at[_chip_slot(me)], peer
            elif mode == "scatter":
                src, dst, to = refs[k].at[_chip_slot(peer)], land.at[_chip_slot(me)], peer
            elif mode == "gather_half":
                src, dst, to = refs[k].at[mine], land.at[_chip_slot(me), mine], peer
            else:
                src = dst = land.at[_chip_slot(peer), mine]
                to = sib
            q = k * len(_CHIP_FLIPS) + j
            copies.append(pltpu.make_async_remote_copy(src_ref=src, dst_ref=dst, send_sem=send.at[q],
                                                       recv_sem=recv.at[q], device_id=to, device_id_type=MESH))
    return copies


def split_start(name, mode, ops, n):
    ops = [pltpu.with_memory_space_constraint(a, pltpu.HBM) for a in ops]
    m = len(ops)

    def body(*refs):
        for cp in _split_copies(mode, refs[:m], n, refs[m], refs[m + 1]):
            cp.start()
        refs[-1][...] = jnp.zeros_like(refs[-1])

    sems = pltpu.SemaphoreType.DMA((n * len(_CHIP_FLIPS),))
    outs = pl.pallas_call(
        body, name=name, in_specs=[_HBM] * m,
        out_shape=(sems, sems, *[pltpu.HBM(a.shape, a.dtype) for a in ops], jax.ShapeDtypeStruct((8, LANES), f32)),
        out_specs=(_SEM, _SEM, *[_HBM] * m, pl.BlockSpec(memory_space=pltpu.VMEM)),
        input_output_aliases={i: 2 + i for i in range(m)},
        compiler_params=pltpu.CompilerParams(has_side_effects=_DATAFLOW))(*ops)
    return (outs[0], outs[1], list(outs[2:2 + m]), mode, n), outs[-1][0, 0]


def split_wait(name, state, after):
    send, recv, ops, mode, n = state
    m = len(ops)
    afters = list(after) if isinstance(after, (list, tuple)) else [after]

    def body(*refs):
        for cp in _split_copies(mode, refs[:m], n, refs[m], refs[m + 1]):
            cp.wait_send()
            cp.wait_recv()

    outs = pl.pallas_call(
        body, name=name, in_specs=[_HBM] * m + [_SEM, _SEM] + [pl.BlockSpec(memory_space=pl.ANY)] * len(afters),
        out_shape=tuple(pltpu.HBM(a.shape, a.dtype) for a in ops), out_specs=tuple([_HBM] * m),
        input_output_aliases={i: i for i in range(m)},
        compiler_params=pltpu.CompilerParams(has_side_effects=_DATAFLOW))(*ops, send, recv, *afters)
    return list(outs[m - n:])


def chips_start(name, arrs, src_of, halves=False):
    me = _chip_slot((lax.axis_index("x"), lax.axis_index("y")))
    lands = []
    for a in arrs:
        blk = tuple(src_of(jax.ShapeDtypeStruct(a.shape, a.dtype), None))
        own = a if src_of is _whole else lax.dynamic_index_in_dim(a, me, 0, keepdims=False)
        lands.append(lax.dynamic_update_index_in_dim(lax.empty((4,) + blk, a.dtype), own, me, 0))
    mode = "scatter" if src_of is _chip_block else ("gather_half" if halves else "gather")
    return split_start(name, mode, list(arrs) + lands, len(arrs))


chips_wait = split_wait


def handover_start(name, lands):
    return split_start(name, "handover", lands, len(lands))


def gather_chips_halves(name, arrs):
    n = len(arrs)
    nf = len(_CHIP_FLIPS)
    split = [a.shape[0] % 32 == 0 for a in arrs]

    def body(*refs):
        ins, outs = refs[:n], refs[n:2 * n]
        send1, recv1, send2, recv2, lsem = refs[2 * n:]
        me = (lax.axis_index("x"), lax.axis_index("y"), lax.axis_index("c"))
        sib = (me[0], me[1], 1 - me[2])
        peers = [tuple(1 - m if f else m for m, f in zip(me, fl)) for fl in _CHIP_FLIPS]
        local, first, second = [], [], []
        for k in range(n):
            cp = pltpu.make_async_copy(ins[k], outs[k].at[_chip_slot(me)], lsem.at[k])
            cp.start()
            local.append(cp)
            half = ins[k].shape[0] // 2
            rows = pl.ds(pl.multiple_of(me[2] * half, 16), half) if split[k] else pl.ds(0, ins[k].shape[0])
            for j, peer in enumerate(peers):
                cp = pltpu.make_async_remote_copy(
                    src_ref=ins[k].at[rows], dst_ref=outs[k].at[_chip_slot(me), rows], send_sem=send1.at[k, j],
                    recv_sem=recv1.at[k, j], device_id=peer, device_id_type=MESH)
                cp.start()
                first.append((k, j, rows, cp))
        for k, j, rows, cp in first:
            cp.wait_recv()
            if split[k]:
                got = outs[k].at[_chip_slot(peers[j]), rows]
                fwd = pltpu.make_async_remote_copy(src_ref=got, dst_ref=got, send_sem=send2.at[k, j],
                                                   recv_sem=recv2.at[k, j], device_id=sib, device_id_type=MESH)
                fwd.start()
                second.append(fwd)
        for _, _, _, cp in first:
            cp.wait_send()
        for cp in second:
            cp.wait()
        for cp in local:
            cp.wait()

    any_spec = pl.BlockSpec(memory_space=pl.ANY)
    sems = pltpu.SemaphoreType.DMA((n, nf))
    return pl.pallas_call(
        body, name=name, in_specs=[any_spec] * n, out_specs=[any_spec] * n,
        out_shape=[jax.ShapeDtypeStruct((4,) + a.shape, a.dtype) for a in arrs],
        scratch_shapes=[sems, sems, sems, sems, pltpu.SemaphoreType.DMA((n,))],
        compiler_params=pltpu.CompilerParams(has_side_effects=True))(*arrs)


def swap_sibling(name, arrs):
    outs = flip_exchange(name, arrs, ((0, 0, 1),), 1, lambda p: 0, _whole, False)
    return [o[0] for o in outs]


def gather_all(arrs, after=()):
    return flip_exchange("gather_all", arrs, _ALL_FLIPS, 8, lambda p: 4 * p[0] + 2 * p[1] + p[2], _whole, True,
                         after=after)


def _row_tile(nr, nc, n_arrays):
    budget = (20 << 20) // (n_arrays * 2 * 4 * max(nc, LANES))
    t = min(nr, budget) // 16 * 16
    while t > 0 and nr % t:
        t -= 16
    return t if t > 0 else nr


def sum_slots(name, x):
    ns, nr, nc = x.shape
    tile = _row_tile(nr, nc, ns + 1)

    def body(x_ref, o_ref):
        s = x_ref[0].astype(f32)
        for q in range(1, ns):
            s = s + x_ref[q].astype(f32)
        o_ref[...] = s.astype(o_ref.dtype)

    return pl.pallas_call(
        body, name=name, grid=(nr // tile,), in_specs=[pl.BlockSpec((ns, tile, nc), lambda i: (0, i, 0))],
        out_specs=pl.BlockSpec((tile, nc), lambda i: (i, 0)), out_shape=jax.ShapeDtypeStruct((nr, nc), x.dtype),
        compiler_params=_params(1))(x)


def adamw(name, w, g_parts, m, v):
    nr, nc = w.shape[-2:]
    n_g = len(g_parts)
    tile = _row_tile(nr, nc, 7 + n_g)

    def body(*refs):
        w_ref, m_ref, v_ref = refs[:3]
        g = refs[3][...].astype(f32)
        for r in refs[4:3 + n_g]:
            g = g + r[...].astype(f32)
        g_ref, d_ref, nm_ref, nv_ref = refs[3 + n_g:]
        nm = ADAM_B1 * m_ref[...] + (1.0 - ADAM_B1) * g
        nv = ADAM_B2 * v_ref[...] + (1.0 - ADAM_B2) * jnp.square(g)
        m_hat = nm / (1.0 - ADAM_B1 ** ADAM_STEP)
        v_hat = nv / (1.0 - ADAM_B2 ** ADAM_STEP)
        g_ref[...] = g
        d_ref[...] = -ADAM_LR * (m_hat / (jnp.sqrt(v_hat) + ADAM_EPS) + ADAM_WD * w_ref[...])
        nm_ref[...] = nm
        nv_ref[...] = nv

    spec = pl.BlockSpec((tile, nc), lambda i: (i, 0))
    spec3 = pl.BlockSpec((None, tile, nc), lambda i: (0, i, 0)) if w.ndim == 3 else spec
    return pl.pallas_call(
        body, name=name, grid=(nr // tile,), in_specs=[spec3] * 3 + [spec] * n_g, out_specs=[spec3] * 4,
        out_shape=[jax.ShapeDtypeStruct(w.shape, f32)] * 4, compiler_params=_params(1))(w, m, v, *g_parts)


def adamw_packed(w, g8, m, v):
    nr, nc = w.shape

    def body(w_ref, g_ref, m_ref, v_ref, go_ref, d_ref, nm_ref, nv_ref):
        g = g_ref[0]
        for q in range(1, 8):
            g = g + g_ref[q]
        nm = ADAM_B1 * m_ref[...] + (1.0 - ADAM_B1) * g
        nv = ADAM_B2 * v_ref[...] + (1.0 - ADAM_B2) * jnp.square(g)
        m_hat = nm / (1.0 - ADAM_B1 ** ADAM_STEP)
        v_hat = nv / (1.0 - ADAM_B2 ** ADAM_STEP)
        go_ref[...] = g
        d_ref[...] = -ADAM_LR * (m_hat / (jnp.sqrt(v_hat) + ADAM_EPS) + ADAM_WD * w_ref[...])
        nm_ref[...] = nm
        nv_ref[...] = nv

    return pl.pallas_call(body, name="adamw_packed", out_shape=[jax.ShapeDtypeStruct((nr, nc), f32)] * 4,
                          compiler_params=pltpu.CompilerParams(vmem_limit_bytes=VMEM_LIMIT))(w, g8, m, v)


def _pack(vectors):
    rows = []
    for a in vectors:
        flat = a.reshape(-1).astype(f32)
        pad = (-flat.shape[0]) % LANES
        rows.append(jnp.pad(flat, (0, pad)).reshape(-1, LANES))
    packed = jnp.concatenate(rows, axis=0)
    return jnp.pad(packed, ((0, (-packed.shape[0]) % 8), (0, 0)))


def _unpack(packed, like):
    out, r = [], 0
    for a in like:
        n = a.size
        nr = -(-n // LANES)
        out.append(packed[r:r + nr].reshape(-1)[:n].reshape(a.shape))
        r += nr
    return out


def kernel(x, mem, mix_norm_w, w_in, dn_conv_w, dn_a_log, dn_dt_bias, dn_norm_w, rw_mu, rw_w0, rw_w2, rw_a0, rw_a2, rw_g2, rw_k_k, rw_k_a, rw_r_k, rw_ln_w, rw_ln_b, w_out, xa_norm_w, mem_norm_w, xa_wq, xa_wk, xa_wv, xa_wo, ffn_norm_w, ffn_w1, ffn_w2, final_norm_w, loss_target, m_mix_norm_w, m_w_in, m_dn_conv_w, m_dn_a_log, m_dn_dt_bias, m_dn_norm_w, m_rw_mu, m_rw_w0, m_rw_w2, m_rw_a0, m_rw_a2, m_rw_g2, m_rw_k_k, m_rw_k_a, m_rw_r_k, m_rw_ln_w, m_rw_ln_b, m_w_out, m_xa_norm_w, m_mem_norm_w, m_xa_wq, m_xa_wk, m_xa_wv, m_xa_wo, m_ffn_norm_w, m_ffn_w1, m_ffn_w2, m_final_norm_w, v_mix_norm_w, v_w_in, v_dn_conv_w, v_dn_a_log, v_dn_dt_bias, v_dn_norm_w, v_rw_mu, v_rw_w0, v_rw_w2, v_rw_a0, v_rw_a2, v_rw_g2, v_rw_k_k, v_rw_k_a, v_rw_r_k, v_rw_ln_w, v_rw_ln_b, v_w_out, v_xa_norm_w, v_mem_norm_w, v_xa_wq, v_xa_wk, v_xa_wv, v_xa_wo, v_ffn_norm_w, v_ffn_w1, v_ffn_w2, v_final_norm_w):
    weights = dict(mix_norm_w=mix_norm_w, w_in=w_in, dn_conv_w=dn_conv_w, dn_a_log=dn_a_log, dn_dt_bias=dn_dt_bias, dn_norm_w=dn_norm_w, rw_mu=rw_mu, rw_w0=rw_w0, rw_w2=rw_w2, rw_a0=rw_a0, rw_a2=rw_a2, rw_g2=rw_g2, rw_k_k=rw_k_k, rw_k_a=rw_k_a, rw_r_k=rw_r_k, rw_ln_w=rw_ln_w, rw_ln_b=rw_ln_b, w_out=w_out, xa_norm_w=xa_norm_w, mem_norm_w=mem_norm_w, xa_wq=xa_wq, xa_wk=xa_wk, xa_wv=xa_wv, xa_wo=xa_wo, ffn_norm_w=ffn_norm_w, ffn_w1=ffn_w1, ffn_w2=ffn_w2, final_norm_w=final_norm_w)
    mom_m = dict(mix_norm_w=m_mix_norm_w, w_in=m_w_in, dn_conv_w=m_dn_conv_w, dn_a_log=m_dn_a_log, dn_dt_bias=m_dn_dt_bias, dn_norm_w=m_dn_norm_w, rw_mu=m_rw_mu, rw_w0=m_rw_w0, rw_w2=m_rw_w2, rw_a0=m_rw_a0, rw_a2=m_rw_a2, rw_g2=m_rw_g2, rw_k_k=m_rw_k_k, rw_k_a=m_rw_k_a, rw_r_k=m_rw_r_k, rw_ln_w=m_rw_ln_w, rw_ln_b=m_rw_ln_b, w_out=m_w_out, xa_norm_w=m_xa_norm_w, mem_norm_w=m_mem_norm_w, xa_wq=m_xa_wq, xa_wk=m_xa_wk, xa_wv=m_xa_wv, xa_wo=m_xa_wo, ffn_norm_w=m_ffn_norm_w, ffn_w1=m_ffn_w1, ffn_w2=m_ffn_w2, final_norm_w=m_final_norm_w)
    mom_v = dict(mix_norm_w=v_mix_norm_w, w_in=v_w_in, dn_conv_w=v_dn_conv_w, dn_a_log=v_dn_a_log, dn_dt_bias=v_dn_dt_bias, dn_norm_w=v_dn_norm_w, rw_mu=v_rw_mu, rw_w0=v_rw_w0, rw_w2=v_rw_w2, rw_a0=v_rw_a0, rw_a2=v_rw_a2, rw_g2=v_rw_g2, rw_k_k=v_rw_k_k, rw_k_a=v_rw_k_a, rw_r_k=v_rw_r_k, rw_ln_w=v_rw_ln_w, rw_ln_b=v_rw_ln_b, w_out=v_w_out, xa_norm_w=v_xa_norm_w, mem_norm_w=v_mem_norm_w, xa_wq=v_xa_wq, xa_wk=v_xa_wk, xa_wv=v_xa_wv, xa_wo=v_xa_wo, ffn_norm_w=v_ffn_norm_w, ffn_w1=v_ffn_w1, ffn_w2=v_ffn_w2, final_norm_w=v_final_norm_w)
    names = list(weights)

    seq, d = x.shape[1], x.shape[2]
    dnw = d // 2
    rww = d - dnw
    nh, nb = dnw // LANES, rww // LANES
    n_mem = mem.shape[1]
    lw_dim, la_dim, lg_dim = rw_w2.shape[1], rw_a2.shape[1], rw_g2.shape[1]
    assert lw_dim + la_dim == LANES and lg_dim == LANES and dnw % LANES == 0 and rww % LANES == 0
    xs, mems, tgt = x[0], mem[0], loss_target[0]

    col_sharded = ("w_in", "xa_wo", "ffn_w1", "dn_conv_w", "rw_w2", "rw_a2", "rw_g2")
    row_sharded = ("w_out", "xa_wq", "xa_wk", "xa_wv", "ffn_w2")
    f32_payload = ("dn_conv_w", "rw_w2", "rw_a2", "rw_g2")
    sharded = col_sharded + row_sharded
    payload = {n: weights[n][0].astype(f32 if n in f32_payload else bf16) for n in sharded}
    shard_w = w_in.shape[2]
    pad_w = -(-shard_w // LANES) * LANES
    first = ("w_in", "dn_conv_w", "rw_w2", "rw_a2", "rw_g2")
    mid = ("w_out", "xa_wq", "xa_wk", "xa_wv", "xa_wo")
    late = ("ffn_w1", "ffn_w2")
    gathered = dict(zip(first, gather_chips_halves("gather_first", [payload[n] for n in first])))
    ordered = lax.optimization_barrier(([gathered[n] for n in first], [payload[n] for n in mid + late]))
    gathered = dict(zip(first, ordered[0]))
    payload.update(zip(mid + late, ordered[1]))
    mid_state, tok_mid = chips_start("gather_mid_start", [payload[n] for n in mid], _whole, halves=True)
    late_state, tok_late = chips_start("gather_late_start", [payload[n] for n in late], _whole, halves=True)
    mix_norm_w_in = mix_norm_w + (tok_mid + tok_late)

    def full(n):
        g = gathered[n]
        if n in col_sharded:
            return g.transpose(1, 0, 2).reshape(g.shape[1], 4 * g.shape[2])
        return g.reshape(4 * g.shape[1], g.shape[2])

    c_rw0 = 4 * dnw + 2 * nh
    c_rw = 3 * rww
    eb_ab, eb_r = 4 * nh, 4 * nh + 1
    eb_l1 = eb_r + 3 * nb
    n_ext = -(-(eb_l1 + 2) // 4) * 4
    tbl_fwd, tbl_bwd, per_shard = w_in_layout(4 * shard_w, shard_w, c_rw0, LANES - 2 * nh, n_ext)
    w_ext = lane_regroup(
        "w_in_regroup", gathered["w_in"], tbl_fwd,
        lambda p: pl.BlockSpec((None, d, LANES),
                               lambda i, j, t: (t[REGROUP_FIELDS * p, j], 0, t[REGROUP_FIELDS * p + 1, j])),
        pl.BlockSpec((d, LANES), lambda i, j, t: (0, j)), (d, n_ext * LANES), (1, n_ext), shard_w)
    conv_w = full("dn_conv_w")
    w2p = jnp.concatenate([full("rw_w2"), jnp.zeros((la_dim, rww), f32)], axis=0)
    a2p = jnp.concatenate([jnp.zeros((lw_dim, rww), f32), full("rw_a2")], axis=0)
    g2 = full("rw_g2")
    xaw = xa_wq.shape[2]
    nxh = xaw // LANES
    ffn = 4 * ffn_w1.shape[2]

    def lane_row(vec):
        return jnp.pad(vec.reshape(1, -1), ((0, 0), (0, LANES - vec.size)))

    alog_row, dtb_row = lane_row(dn_a_log), lane_row(dn_dt_bias)
    head_of_col = jnp.arange(dnw)[None, :] // LANES
    e_g = (jnp.arange(LANES)[:, None] == head_of_col).astype(f32)
    e_b = (jnp.arange(LANES)[:, None] == head_of_col + nh).astype(f32)
    mu_main, mu_small = rw_mu[:, :c_rw], rw_mu[:, c_rw:]
    r_k_row = rw_r_k.reshape(1, rww)
    fnw = final_norm_w.reshape(1, d)
    qb, kb_, vb, zb = 0, nh, 2 * nh, 3 * nh
    rb0 = eb_r
    cc = lambda w_, o_: ("constc", w_, o_)
    rc = lambda o_: ("rowc", LANES, o_)

    (u16,) = rowcall("norm_mix", fn_rms, [(xs, "row"), (mix_norm_w_in, "const")], [(seq, d, bf16, "row")], rows=seq)
    p_main = p_small = mm("in_proj", u16, w_ext)

    fn_gconv = make_fn_gconv(2 * nh)
    (qkv,) = rowcall("gdn_conv", fn_gconv, [(p_main, rc(0)), (conv_w, cc(LANES, 0))],
                     [(seq, 3 * dnw, f32, "rowc")], rows=seq, tile=seq, ncol=3 * nh)
    gate_ins = [(p_small, rc(eb_ab)), (alog_row, "const"), (dtb_row, "const"), (e_g, "const"), (e_b, "const")]
    g_b, beta_b = rowcall("gdn_gate", fn_ggate, gate_ins, [(seq, dnw, f32, "row")] * 2, rows=seq)
    gdn_ins = [(qkv, qb), (qkv, kb_), (qkv, vb), (g_b, 0), (beta_b, 0)]
    o_raw, gdn_states, gdn_kept = scan_fwd("gdn_scan", gdn_chunk, gdn_ins, rows=seq, chunk=GDN_CHUNK, ncol=nh)
    mid_state, tok = handover_start("gather_mid_pass", chips_wait("gather_mid_wait", mid_state, o_raw))
    gpost_ins = [(o_raw, rc(0)), (p_main, rc(zb)), (dn_norm_w + tok, "const")]
    (o_dn,) = rowcall("gdn_post", fn_gpost, gpost_ins, [(seq, dnw, bf16, "rowc")], rows=seq, ncol=nh)

    (prw,) = rowcall("rw_lerp_main", fn_lerp, [(p_main, rc(rb0)), (mu_main, cc(LANES, 0))],
                     [(seq, c_rw, f32, "rowc")], rows=seq, tile=seq, ncol=3 * nb)
    (psl,) = rowcall("rw_lerp_small", fn_lerp, [(p_small, rc(eb_l1)), (mu_small, cc(LANES, 0))],
                     [(seq, 2 * LANES, f32, "rowc")], rows=seq, tile=seq, ncol=2)
    rprep_ins = [(prw, rc(nb)), (psl, "row"), (rw_w0, cc(LANES, 0)), (rw_a0, cc(LANES, 0)), (rw_k_k, cc(LANES, 0)),
                 (rw_k_a, cc(LANES, 0)), (w2p, cc(LANES, 0)), (a2p, cc(LANES, 0)), (g2, cc(LANES, 0))]
    lw, kmod, kk, a_rw, gate = rowcall("rw_prep", fn_rprep, rprep_ins, [(seq, rww, f32, "rowc")] * 5,
                                        rows=seq, ncol=nb)
    wkv_ins = [(prw, 0), (lw, 0), (kmod, 0), (prw, 2 * nb), (kk, 0), (a_rw, 0)]
    y_rw, wkv_states, wkv_kept = scan_fwd("wkv_scan", wkv_chunk, wkv_ins, rows=seq, chunk=WKV_CHUNK, ncol=nb)
    rpost_ins = [(y_rw, rc(0)), (prw, rc(0)), (kmod, rc(0)), (prw, rc(2 * nb)), (gate, rc(0)),
                 (r_k_row, cc(LANES, 0)), (rw_ln_w, cc(LANES, 0)), (rw_ln_b, cc(LANES, 0))]
    (o_rw,) = rowcall("rw_post", fn_rpost, rpost_ins, [(seq, rww, bf16, "rowc")], rows=seq, ncol=nb)

    o_cat = jnp.concatenate([o_dn, o_rw], axis=1)
    late_state, tok = handover_start("gather_late_pass", chips_wait("gather_late_wait", late_state, o_cat))
    gathered.update(zip(mid, chips_wait("gather_mid_pass_wait", mid_state, o_cat)))
    w_out_f, wq_f, wk_f, wv_f, wo_f = full("w_out"), full("xa_wq"), full("xa_wk"), full("xa_wv"), full("xa_wo")
    h1 = mm("out_proj", o_cat, w_out_f, add=xs)

    (hn16,) = rowcall("norm_xa", fn_rms, [(h1, "row"), (xa_norm_w + tok, "const")], [(seq, d, bf16, "row")],
                      rows=seq)
    (mn16,) = rowcall("norm_mem", fn_rms, [(mems, "row"), (mem_norm_w, "const")], [(n_mem, d, bf16, "row")],
                      rows=n_mem)
    q_xa = mm("xa_q", hn16, wq_f)
    k_xa = mm("xa_k", mn16, wk_f)
    v_xa = mm("xa_v", mn16, wv_f)
    xcore_ins = [(q_xa, rc(0)), (k_xa, cc(LANES, 0)), (v_xa, cc(LANES, 0))]
    (o_xa,) = rowcall("xa_core", fn_xcore, xcore_ins, [(seq, xaw, bf16, "rowc")], rows=seq, ncol=nxh)
    h2 = mm("xa_o", o_xa, wo_f, add=h1)

    (fn16,) = rowcall("norm_ffn", fn_rms, [(h2, "row"), (ffn_norm_w, "const")], [(seq, d, bf16, "row")], rows=seq)
    gathered.update(zip(late, chips_wait("gather_late_pass_wait", late_state, fn16)))
    w1_f, w2_f = full("ffn_w1"), full("ffn_w2")
    a1_16, hid16 = mm("ffn_up", fn16, w1_f, epilogue=lambda r: (r, jnp.square(jnp.maximum(r, 0.0))),
                      out_dtypes=(bf16, bf16))
    h3 = mm("ffn_down", hid16, w2_f, add=h2)

    dh3, dh3_16, d_fnw, loss_rows = rowcall(
        "loss_head", fn_final, [(h3, "row"), (tgt, "row"), (fnw, "const")],
        [(seq, d, f32, "row"), (seq, d, bf16, "row"), (1, d, f32, "acc"), (8, LANES, f32, "acc")], rows=seq)

    da1_16 = mm("ffn_down_dx", dh3_16, w2_f, tb=True, extra=[a1_16], out_dtype=bf16,
                epilogue=lambda r, a1: (r * (2.0 * jnp.maximum(a1.astype(f32), 0.0)),))
    def by_chip(n, g):
        if g.ndim == 3:
            return g
        if n in col_sharded:
            return g.reshape(g.shape[0], 4, g.shape[1] // 4).transpose(1, 0, 2)
        return g.reshape(4, g.shape[0] // 4, g.shape[1])

    g_ffn_w2 = mm("ffn_down_dw", hid16, dh3_16, ta=True, out_dtype=bf16)
    ffn_w2_g, tok = chips_start("scatter_ffn_w2_start", [by_chip("ffn_w2", g_ffn_w2)], _chip_block)
    g_ffn_w1 = mm("ffn_up_dw", fn16, da1_16, ta=True, out_dtype=bf16, by_chip=True, after=tok)
    ffn_w1_g, tok = chips_start("scatter_ffn_w1_start", [g_ffn_w1], _chip_block)
    dfn = mm("ffn_up_dx", da1_16, w1_f, tb=True, after=tok)
    dh2, d_ffn_nw, dh2_16 = rowvjp("norm_ffn_bwd", fn_rms, [(h2, "row"), (ffn_norm_w, "const")],
                                   [[(dfn, "row")]], [0, 1], rows=seq, adds=[(0, dh3, "row")], dup16=[0])

    do_xa = mm("xa_o_dx", dh2_16, wo_f, tb=True)
    g_xa_wo = mm("xa_o_dw", o_xa, dh2_16, ta=True, out_dtype=bf16, by_chip=True)
    dq_xa, dk_xa, dv_xa, dq16 = rowvjp("xa_core_bwd", fn_xcore, xcore_ins, [[(do_xa, rc(0))]], [0, 1, 2],
                                       rows=seq, ncol=nxh, dup16=[0])
    g_xa_wq = mm("xa_q_dw", hn16, dq16, ta=True, out_dtype=bf16)
    dhn = mm("xa_q_dx", dq16, wq_f, tb=True)
    dh1, d_xa_nw, dh1_16 = rowvjp("norm_xa_bwd", fn_rms, [(h1, "row"), (xa_norm_w, "const")], [[(dhn, "row")]],
                                  [0, 1], rows=seq, adds=[(0, dh2, "row")], dup16=[0])
    dk16, dv16 = dk_xa.astype(bf16), dv_xa.astype(bf16)
    g_xa_wk = mm("xa_k_dw", mn16, dk16, ta=True, out_dtype=bf16)
    g_xa_wv = mm("xa_v_dw", mn16, dv16, ta=True, out_dtype=bf16)
    dmn = mm("xa_v_dx", dv16, wv_f, tb=True, add=mm("xa_k_dx", dk16, wk_f, tb=True))
    (d_mem_nw,) = rowvjp("norm_mem_bwd", fn_rms, [(mems, "row"), (mem_norm_w, "const")], [[(dmn, "row")]], [1],
                         rows=n_mem)

    g_w_out = mm("out_proj_dw", o_cat, dh1_16, ta=True, out_dtype=bf16)
    mid_grads = dict(w_out=g_w_out, xa_wq=g_xa_wq, xa_wk=g_xa_wk, xa_wv=g_xa_wv, xa_wo=g_xa_wo)
    mid_g, tok = chips_start("scatter_mid_start", [by_chip(n, mid_grads[n]) for n in mid], _chip_block)
    do_cat = mm("out_proj_dx", dh1_16, w_out_f, tb=True, after=tok)

    dy, dr_a, dkmod_a, dv_a, dgate, d_r_k, d_ln_w, d_ln_b = rowvjp(
        "rw_post_bwd", fn_rpost, rpost_ins, [[(do_cat, rc(nh))]], [0, 1, 2, 3, 4, 5, 6, 7], rows=seq, ncol=nb)
    dr_b, dlw, dkmod_b, dv_b, dkk, da_rw = scan_bwd("wkv_scan_bwd", wkv_chunk, wkv_ins, wkv_states, wkv_kept, dy, 0,
                                                    rows=seq, chunk=WKV_CHUNK, ncol=nb)
    rprep_cts = [[(dlw, rc(0))], [(dkmod_a, rc(0)), (dkmod_b, rc(0))], [(dkk, rc(0))], [(da_rw, rc(0))],
                 [(dgate, rc(0))]]
    dpk, dpsl_parts, d_w0, d_a0, d_k_k, d_k_a, d_w2p, d_a2p, d_g2 = rowvjp(
        "rw_prep_bwd", fn_rprep, rprep_ins, rprep_cts, [0, 1, 2, 3, 4, 5, 6, 7, 8], rows=seq, ncol=nb)
    (dpsl,) = rowcall("rw_prep_sum", fn_sumcols(nb), [(dpsl_parts, "row")], [(seq, 2 * LANES, f32, "row")], rows=seq)

    def lerp_bwd(tag, p, p_off, mu, mu_off, ct_lists, ncol):
        return rowvjp("rw_lerp_bwd_" + tag, fn_lerp, [(p, rc(p_off)), (mu, cc(LANES, mu_off))], [ct_lists], [0, 1],
                      rows=seq, tile=seq, ncol=ncol, dup16=[0])

    _, dmu_r, dpr16 = lerp_bwd("r", p_main, rb0, mu_main, 0, [(dr_a, rc(0)), (dr_b, rc(0))], nb)
    _, dmu_k, dpk16 = lerp_bwd("k", p_main, rb0 + nb, mu_main, nb, [(dpk, rc(0))], nb)
    _, dmu_v, dpv16 = lerp_bwd("v", p_main, rb0 + 2 * nb, mu_main, 2 * nb, [(dv_a, rc(0)), (dv_b, rc(0))], nb)
    _, dmu_s, dps12_16 = lerp_bwd("small", p_small, eb_l1, mu_small, 0, [(dpsl, rc(0))], 2)

    do_raw, dz, d_dn_nw, dz16 = rowvjp("gdn_post_bwd", fn_gpost, gpost_ins, [[(do_cat, rc(0))]], [0, 1, 2],
                                       rows=seq, ncol=nh, dup16=[1])
    dq_g, dk_g, dv_g, dg_b, dbeta_b = scan_bwd("gdn_scan_bwd", gdn_chunk, gdn_ins, gdn_states, gdn_kept, do_raw, 0,
                                               rows=seq, chunk=GDN_CHUNK, ncol=nh)
    dps0, d_alog, d_dtb, dps0_16 = rowvjp("gdn_gate_bwd", fn_ggate, gate_ins, [[(dg_b, "row")], [(dbeta_b, "row")]],
                                          [0, 1, 2], rows=seq, dup16=[0])
    dqkv = jnp.concatenate([dq_g, dk_g, dv_g], axis=1)
    _, d_conv_w, dqkv16 = rowvjp("gdn_conv_bwd", fn_gconv, [(p_main, rc(0)), (conv_w, cc(LANES, 0))],
                                 [[(dqkv, rc(0))]], [0, 1], rows=seq, tile=seq, ncol=3 * nh, dup16=[0])

    dp16 = jnp.concatenate([dqkv16, dz16, dps0_16, dpr16, dpk16, dpv16, dps12_16,
                            jnp.zeros((seq, (n_ext - eb_l1 - 2) * LANES), bf16)], axis=1)
    g_w_ext = mm("in_proj_dw", u16, dp16, ta=True, out_dtype=bf16)
    g_w_in = lane_regroup(
        "w_in_grad_regroup", g_w_ext, tbl_bwd,
        lambda p: pl.BlockSpec((d, LANES), lambda i, j, t: (0, t[REGROUP_FIELDS * p + 1, j])),
        pl.BlockSpec((None, d, LANES), lambda i, j, t: (j // per_shard, 0, j % per_shard)),
        (4, d, pad_w), (1, 4 * per_shard), n_ext * LANES)
    first_grads = dict(w_in=g_w_in, dn_conv_w=d_conv_w, rw_w2=d_w2p[:lw_dim], rw_a2=d_a2p[lw_dim:], rw_g2=d_g2)
    first_g, tok = chips_start("scatter_first_start", [by_chip(n, first_grads[n]) for n in first], _chip_block)

    du = mm("in_proj_dx", dp16, w_ext, tb=True, after=tok)
    grad_x, d_mix_nw = rowvjp("norm_mix_bwd", fn_rms, [(xs, "row"), (mix_norm_w, "const")], [[(du, "row")]],
                              [0, 1], rows=seq, adds=[(0, dh1, "row")])
    received = dict(zip(mid, chips_wait("scatter_mid_wait", mid_g, grad_x)))
    received["ffn_w2"] = chips_wait("scatter_ffn_w2_wait", ffn_w2_g, grad_x)[0]
    received["ffn_w1"] = chips_wait("scatter_ffn_w1_wait", ffn_w1_g, grad_x)[0]

    out = {}

    def reduce_and_update(tag, group):
        partial = [sum_slots("sum_chips_" + n, received[n]) for n in group]
        other = swap_sibling("swap_sibling_" + tag, partial)
        for n, p_mine, p_other in zip(group, partial, other):
            if weights[n].shape[2] % LANES:
                rows, cols = weights[n].shape[1:]
                lin = lambda a: jnp.swapaxes(a, 1, 2).reshape(-1, LANES)
                lin_g = lambda p: p.T[:cols].reshape(-1, LANES)
                res = adamw("adamw_" + n, lin(weights[n]), [lin_g(p_mine), lin_g(p_other)], lin(mom_m[n]),
                            lin(mom_v[n]))
                out[n] = [jnp.swapaxes(r.reshape(1, cols, rows), 1, 2) for r in res]
            else:
                out[n] = adamw("adamw_" + n, weights[n], [p_mine, p_other], mom_m[n], mom_v[n])

    reduce_and_update("rest", mid + late)
    received.update(zip(first, chips_wait("scatter_first_wait", first_g, [out[n][1] for n in mid + late])))
    reduce_and_update("first", first)

    small_names = [n for n in names if n not in sharded]
    small_local = dict(
        mix_norm_w=d_mix_nw, dn_a_log=d_alog[:, :nh], dn_dt_bias=d_dtb[:, :nh], dn_norm_w=d_dn_nw,
        rw_mu=jnp.concatenate([dmu_r, dmu_k, dmu_v, dmu_s], axis=1), rw_w0=d_w0, rw_a0=d_a0, rw_k_k=d_k_k,
        rw_k_a=d_k_a, rw_r_k=d_r_k, rw_ln_w=d_ln_w, rw_ln_b=d_ln_b, xa_norm_w=d_xa_nw, mem_norm_w=d_mem_nw,
        ffn_norm_w=d_ffn_nw, final_norm_w=d_fnw)
    loss_vec = jnp.where(jnp.arange(LANES) == 0, loss_rows[0], 0.0)
    (g8,) = gather_all([_pack([small_local[n] for n in small_names] + [loss_vec])], after=[out["w_in"][1]])

    packed_like = [weights[n] for n in small_names] + [loss_vec]
    zero = jnp.zeros((LANES,), f32)
    res = adamw_packed(_pack([weights[n] for n in small_names] + [zero]), g8,
                       _pack([mom_m[n] for n in small_names] + [zero]),
                       _pack([mom_v[n] for n in small_names] + [zero]))
    unpacked = [_unpack(r, packed_like) for r in res]
    for i, n in enumerate(small_names):
        out[n] = [u[i] for u in unpacked]
    loss = unpacked[0][-1][0]

    return (loss, grad_x.reshape(x.shape), *[out[n][0] for n in names], *[out[n][1] for n in names],
            *[out[n][2] for n in names], *[out[n][3] for n in names])
```

```python
import functools

import jax
import jax.numpy as jnp
from jax import lax
from jax.experimental import pallas as pl
from jax.experimental.pallas import tpu as pltpu

f32 = jnp.float32
bf16 = jnp.bfloat16
HI = lax.Precision.HIGHEST
MESH = pl.DeviceIdType.MESH

LANES = 128
VMEM_LIMIT = 56 << 20
TOK_TILE = 256
TOK_TILE_BLOCKED = 1024
MM_TILE = 1024
MM_TILE_K = 2048
GDN_CHUNK = 128
WKV_CHUNK = 64
SCAN_GROUP = 8
P_BULK = 1
P_INV = 1
P_RESID = 3
P_CUMSUM = 3
RMS_EPS = 1e-6
RW_GN_EPS = 64e-5
RW_HEAD = 64

ADAM_LR, ADAM_B1, ADAM_B2, ADAM_EPS, ADAM_WD, ADAM_STEP = 0.001, 0.9, 0.999, 1e-08, 0.01, 10


def _params(n_grid):
    return pltpu.CompilerParams(dimension_semantics=("arbitrary",) * n_grid, vmem_limit_bytes=VMEM_LIMIT)


_DIMS = {"nn": (((1,), (0,)), ((), ())), "nt": (((1,), (1,)), ((), ())), "tn": (((0,), (0,)), ((), ()))}
_DIMS_BATCHED = {"nn": (((2,), (1,)), ((0,), (0,))), "nt": (((2,), (2,)), ((0,), (0,))),
                 "tn": (((1,), (1,)), ((0,), (0,)))}


def _raw_dot(a, b, mode, passes):
    dims = (_DIMS if a.ndim == 2 else _DIMS_BATCHED)[mode]
    if passes == 6:
        return lax.dot_general(a.astype(f32), b.astype(f32), dims, precision=HI, preferred_element_type=f32)
    ah, bh = a.astype(bf16), b.astype(bf16)
    r = lax.dot_general(ah, bh, dims, preferred_element_type=f32)
    if passes == 3:
        al = (a - ah.astype(f32)).astype(bf16)
        bl = (b - bh.astype(f32)).astype(bf16)
        r = r + lax.dot_general(al, bh, dims, preferred_element_type=f32)
        r = r + lax.dot_general(ah, bl, dims, preferred_element_type=f32)
    return r


@functools.partial(jax.custom_vjp, nondiff_argnums=(2, 3))
def pdot(a, b, mode, passes):
    return _raw_dot(a, b, mode, passes)


def _pdot_bwd(mode, passes, res, g):
    a, b = res
    if mode == "nn":
        da, db = _raw_dot(g, b, "nt", passes), _raw_dot(a, g, "tn", passes)
    elif mode == "nt":
        da, db = _raw_dot(g, b, "nn", passes), _raw_dot(g, a, "tn", passes)
    else:
        da, db = _raw_dot(b, g, "nt", passes), _raw_dot(a, g, "nn", passes)
    return da.astype(a.dtype), db.astype(b.dtype)


pdot.defvjp(lambda a, b, mode, passes: (_raw_dot(a, b, mode, passes), (a, b)), _pdot_bwd)


def bdot(a, b):
    return pdot(a, b, "nn", 1)


def bdot_nt(a, b):
    return pdot(a, b, "nt", 1)


def _shift_rows(x, k):
    row = lax.broadcasted_iota(jnp.int32, x.shape, 0)
    return jnp.where(row < k, 0.0, pltpu.roll(x, k, axis=0))


def _unshift_rows(g, k):
    n = g.shape[0]
    row = lax.broadcasted_iota(jnp.int32, g.shape, 0)
    return jnp.where(row >= n - k, 0.0, pltpu.roll(g, n - k, axis=0))


@functools.partial(jax.custom_vjp, nondiff_argnums=(1,))
def tshift(x, k):
    return _shift_rows(x, k)


tshift.defvjp(lambda x, k: (_shift_rows(x, k), None), lambda k, _, g: (_unshift_rows(g, k),))


def rms(x, w):
    x = x.astype(f32)
    return x * lax.rsqrt(jnp.mean(x * x, axis=-1, keepdims=True) + RMS_EPS) * w


def softplus(x):
    return jnp.maximum(x, 0.0) + jnp.log(1.0 + jnp.exp(-jnp.abs(x)))


def seg2sum(x):
    lo = lax.broadcasted_iota(jnp.int32, x.shape, 1) < RW_HEAD
    s_lo = jnp.sum(jnp.where(lo, x, 0.0), axis=-1, keepdims=True)
    s_hi = jnp.sum(jnp.where(lo, 0.0, x), axis=-1, keepdims=True)
    return jnp.where(lo, s_lo, s_hi)


def _tile(n, pref):
    if n <= pref:
        return n
    t = pref
    while t >= LANES:
        if n % t == 0:
            return t
        t -= LANES
    return n


def mm(name, a, b, *, ta=False, tb=False, add=None, out_dtype=f32, by_chip=False, epilogue=None, extra=(),
       out_dtypes=None, after=None):
    (k, m) = a.shape if ta else a.shape[::-1]
    (n, kb) = b.shape if tb else b.shape[::-1]
    assert k == kb, (name, a.shape, b.shape)
    tm, tk = _tile(m, MM_TILE), _tile(k, MM_TILE_K)
    tn = _tile(n // 4, MM_TILE) if by_chip else _tile(n, MM_TILE)
    nk = k // tk
    dims = (((0,) if ta else (1,), (1,) if tb else (0,)), ((), ()))
    extra = list(extra) + ([] if add is None else [add])
    out_dtypes = [out_dtype] if out_dtypes is None else list(out_dtypes)
    n_extra, n_out = len(extra), len(out_dtypes)
    n_after = 0 if after is None else 1

    def body(*refs):
        a_ref, b_ref = refs[:2]
        x_refs = refs[2:2 + n_extra]
        o_refs = refs[2 + n_extra + n_after:2 + n_extra + n_after + n_out]
        part = lax.dot_general(a_ref[...].astype(bf16), b_ref[...].astype(bf16), dims, preferred_element_type=f32)

        def finish(r):
            xs = [x[...] for x in x_refs]
            if add is not None:
                r = r + xs.pop().astype(f32)
            outs = (r,) if epilogue is None else epilogue(r, *xs)
            for o_ref, o in zip(o_refs, outs):
                o_ref[...] = o.astype(o_ref.dtype)

        if nk == 1:
            finish(part)
            return
        acc = refs[-1]
        kk = pl.program_id(2)

        @pl.when(kk == 0)
        def _():
            acc[...] = part

        @pl.when(kk > 0)
        def _():
            acc[...] += part

        @pl.when(kk == nk - 1)
        def _():
            finish(acc[...])

    a_spec = pl.BlockSpec((tk, tm), lambda i, j, q: (q, i)) if ta else pl.BlockSpec((tm, tk), lambda i, j, q: (i, q))
    b_spec = pl.BlockSpec((tn, tk), lambda i, j, q: (j, q)) if tb else pl.BlockSpec((tk, tn), lambda i, j, q: (q, j))
    x_spec = pl.BlockSpec((tm, tn), lambda i, j, q: (i, j))
    if by_chip:
        per_chip = n // 4 // tn
        o_spec = pl.BlockSpec((None, tm, tn), lambda i, j, q: (j // per_chip, i, j % per_chip))
        o_shape = (4, m, n // 4)
    else:
        o_spec, o_shape = x_spec, (m, n)
    afters = [] if after is None else [jnp.reshape(after, (1, 1))]
    res = pl.pallas_call(
        body, name=name, grid=(m // tm, n // tn, nk),
        in_specs=[a_spec, b_spec] + [x_spec] * n_extra + [pl.BlockSpec(memory_space=pl.ANY)] * n_after,
        out_specs=[o_spec] * n_out, out_shape=[jax.ShapeDtypeStruct(o_shape, dt) for dt in out_dtypes],
        scratch_shapes=[pltpu.VMEM((tm, tn), f32)] if nk > 1 else [],
        compiler_params=_params(3))(a, b, *extra, *afters)
    return res[0] if n_out == 1 else res


REGROUP_PIECES = 4


REGROUP_SUB = 5
REGROUP_FIELDS = 5


def _regroup_table(n_out, sources_of):
    import numpy as np
    tbl = np.zeros((REGROUP_FIELDS * REGROUP_PIECES, n_out), np.int32)
    for j in range(n_out):
        groups = sorted(sources_of(j).items())
        assert len(groups) <= REGROUP_PIECES, (j, len(groups))
        for p in range(REGROUP_PIECES):
            if p < len(groups):
                key, lanes = groups[p]
                shifts = {q - s for s, q in lanes}
                qs = sorted(q for _, q in lanes)
                assert len(shifts) == 1 and qs == list(range(qs[0], qs[-1] + 1)), (j, key)
                row = (key[0], key[1], shifts.pop(), qs[0], qs[-1] + 1)
            else:
                row = (0, 0, 0, 0, 0)
            tbl[REGROUP_FIELDS * p:REGROUP_FIELDS * (p + 1), j] = row
    return jnp.asarray(tbl)


def lane_regroup(name, src, table, src_spec, out_spec, out_shape, grid, src_width, sub):
    rows = src.shape[-2]
    n_src = REGROUP_PIECES * sub

    def body(tbl, *refs):
        o_ref = refs[n_src]
        step = pl.program_id(1)

        def moved(b, p):
            j = step * sub + b
            blk, shift, lo, hi = (tbl[REGROUP_FIELDS * p + f, j] for f in range(1, REGROUP_FIELDS))
            x = refs[b * REGROUP_PIECES + p][...]
            if src_width % LANES:
                lane = lax.broadcasted_iota(jnp.int32, (rows, LANES), 1)
                x = jnp.where(lane < src_width - blk * LANES, x, jnp.zeros((), src.dtype))
            pi = lax.broadcasted_iota(jnp.int32, (LANES, LANES), 0)
            qi = lax.broadcasted_iota(jnp.int32, (LANES, LANES), 1)
            sel = jnp.logical_and(qi - pi == shift, jnp.logical_and(qi >= lo, qi < hi))
            return jnp.dot(x, sel.astype(src.dtype), preferred_element_type=f32).astype(o_ref.dtype)

        for b in range(sub):
            lanes = slice(b * LANES, (b + 1) * LANES)
            o_ref[:, lanes] = moved(b, 0)
            for p in range(1, REGROUP_PIECES):
                j = step * sub + b

                @pl.when(tbl[REGROUP_FIELDS * p + 4, j] > tbl[REGROUP_FIELDS * p + 3, j])
                def _(b=b, p=p, lanes=lanes):
                    o_ref[:, lanes] += moved(b, p)

    return pl.pallas_call(
        body, name=name, out_shape=jax.ShapeDtypeStruct(out_shape, src.dtype),
        grid_spec=pltpu.PrefetchScalarGridSpec(
            num_scalar_prefetch=1, grid=grid,
            in_specs=[src_spec(p, b) for b in range(sub) for p in range(REGROUP_PIECES)], out_specs=out_spec),
        compiler_params=_params(2))(table, *[src] * n_src)


def w_in_layout(d_cols, shard_w, c_split, gap, n_blocks):
    def ext_of(c):
        return c if c < c_split else c + gap

    def fwd_sources(j):
        groups = {}
        for q in range(LANES):
            e = j * LANES + q
            c = e if e < c_split else e - gap
            if (c_split <= e < c_split + gap) or c >= d_cols:
                continue
            s, l = divmod(c, shard_w)
            groups.setdefault((s, l // LANES), []).append((l % LANES, q))
        return groups

    per_shard = -(-shard_w // LANES)

    def bwd_sources(j):
        s, b = divmod(j, per_shard)
        groups = {}
        for q in range(LANES):
            l = b * LANES + q
            if l >= shard_w:
                continue
            e = ext_of(s * shard_w + l)
            groups.setdefault((0, e // LANES), []).append((e % LANES, q))
        return groups

    return _regroup_table(n_blocks, fwd_sources), _regroup_table(4 * per_shard, bwd_sources), per_shard


def _in_spec(a, kind, tile):
    if kind == "row":
        return pl.BlockSpec((tile, a.shape[1]), lambda j, i: (i, 0))
    if kind == "const":
        return pl.BlockSpec(a.shape, lambda j, i: (0, 0))
    tag, cw, off = kind
    if tag == "rowc":
        return pl.BlockSpec((tile, cw), lambda j, i: (i, j + off))
    assert tag == "constc", kind
    return pl.BlockSpec((a.shape[0], cw), lambda j, i: (0, j + off))


def rowcall(name, fn, ins, outs, *, rows, tile=None, ncol=1):
    if tile is None:
        tile = min(TOK_TILE_BLOCKED if ncol > 1 else TOK_TILE, rows)
    n_in = len(ins)
    kinds = [o[3] for o in outs]

    def body(*refs):
        j, i = pl.program_id(0), pl.program_id(1)
        res = fn(*[r[...] for r in refs[:n_in]])
        for ref, val, kind in zip(refs[n_in:], res, kinds):
            if kind in ("row", "rowc"):
                ref[...] = val.astype(ref.dtype)
            else:
                first = (i == 0) if kind == "accc" else jnp.logical_and(i == 0, j == 0)

                @pl.when(first)
                def _(ref=ref, val=val):
                    ref[...] = val.astype(ref.dtype)

                @pl.when(jnp.logical_not(first))
                def _(ref=ref, val=val):
                    ref[...] += val.astype(ref.dtype)

    out_shape, out_specs = [], []
    for nr, nc, dtype, kind in outs:
        out_shape.append(jax.ShapeDtypeStruct((nr, nc), dtype))
        if kind == "row":
            out_specs.append(pl.BlockSpec((tile, nc), lambda j, i: (i, 0)))
        elif kind == "rowc":
            out_specs.append(pl.BlockSpec((tile, nc // ncol), lambda j, i: (i, j)))
        elif kind == "acc":
            out_specs.append(pl.BlockSpec((nr, nc), lambda j, i: (0, 0)))
        else:
            out_specs.append(pl.BlockSpec((nr, nc // ncol), lambda j, i: (0, j)))
    return pl.pallas_call(
        body, name=name, grid=(ncol, rows // tile), in_specs=[_in_spec(a, k, tile) for a, k in ins],
        out_specs=out_specs, out_shape=out_shape, compiler_params=_params(2))(*[a for a, _ in ins])


def rowvjp(name, fn, ins, cts, grads, *, rows, tile=None, ncol=1, adds=(), dup16=()):
    n_in = len(ins)
    ct_sizes = [len(c) for c in cts]
    flat_cts = [m for c in cts for m in c]
    n_ct = len(flat_cts)

    def wrapped(*vals):
        xs = list(vals[:n_in])
        gs = vals[n_in:n_in + n_ct]
        extra = vals[n_in + n_ct:]

        def f(*dvars):
            full = list(xs)
            for k, v in zip(grads, dvars):
                full[k] = v
            return fn(*full)

        outs, pull = jax.vjp(f, *[xs[k] for k in grads])
        cot, p = [], 0
        for o, size in zip(outs, ct_sizes):
            g = gs[p].astype(f32)
            for q in range(1, size):
                g = g + gs[p + q].astype(f32)
            cot.append(g.astype(o.dtype))
            p += size
        gv = list(pull(tuple(cot)))
        for (pos, _, _), e in zip(adds, extra):
            gv[pos] = gv[pos] + e.astype(gv[pos].dtype)
        return tuple(gv) + tuple(gv[pos] for pos in dup16)

    outs = []
    for k in grads:
        a, kind = ins[k]
        if kind == "row":
            outs.append((rows, a.shape[1] * ncol, f32, "rowc") if ncol > 1 else (rows, a.shape[1], f32, "row"))
        elif kind == "const":
            outs.append((a.shape[0], a.shape[1], f32, "acc"))
        elif kind[0] == "rowc":
            outs.append((rows, kind[1] * ncol, f32, "rowc"))
        else:
            outs.append((a.shape[0], kind[1] * ncol, f32, "accc"))
    for pos in dup16:
        nr, nc, _, kind = outs[pos]
        outs.append((nr, nc, bf16, kind))
    all_ins = list(ins) + flat_cts + [(a, kind) for _, a, kind in adds]
    return rowcall(name, wrapped, all_ins, outs, rows=rows, tile=tile, ncol=ncol)


def fn_rms(x, w):
    return (rms(x, w),)


def make_fn_gconv(n_norm_blocks):
    def fn(p, cw):
        c = cw[3:4] * p
        for jj in range(3):
            c = c + cw[jj:jj + 1] * tshift(p, 3 - jj)
        s = c * jax.nn.sigmoid(c)
        nrm = s * lax.rsqrt(jnp.sum(s * s, axis=-1, keepdims=True) + 1e-6)
        return (jnp.where(pl.program_id(0) < n_norm_blocks, nrm, s),)
    return fn


def fn_ggate(ps0, alog, dtb, e_g, e_b):
    g = -jnp.exp(alog) * softplus(ps0 + dtb)
    beta = jax.nn.sigmoid(ps0)
    return pdot(g, e_g, "nn", 3), pdot(beta, e_b, "nn", 3)


def fn_gpost(o, z, nw):
    return (rms(o, nw) * (z * jax.nn.sigmoid(z)),)


def fn_lerp(p, mu):
    return (p + (tshift(p, 1) - p) * mu,)


def fn_rprep(pk, psl, w0, a0, k_k, k_a, w2p, a2p, g2):
    g1, g2in = psl[:, :LANES], psl[:, LANES:]
    log_w = -softplus(-(w0 + bdot(jnp.tanh(g1), w2p))) - 0.5
    lw = -jnp.exp(log_w)
    a = jax.nn.sigmoid(a0 + bdot(g1, a2p))
    gate = bdot(jax.nn.sigmoid(g2in), g2)
    kkr = pk * k_k
    kk = kkr / jnp.maximum(jnp.sqrt(seg2sum(kkr * kkr)), 1e-12)
    kmod = pk * (1.0 + (a - 1.0) * k_a)
    return lw, kmod, kk, a, gate


def fn_rpost(y, r, kmod, v, gate, r_k, ln_w, ln_b):
    inv_n = 1.0 / RW_HEAD
    mean = seg2sum(y) * inv_n
    d = y - mean
    var = seg2sum(d * d) * inv_n
    yn = d * lax.rsqrt(var + RW_GN_EPS) * ln_w + ln_b
    bonus = seg2sum(r * kmod * r_k) * v
    return ((yn + bonus) * gate,)


def fn_xcore(q, k, v):
    s = bdot_nt(q, k) * (LANES ** -0.5)
    p = jax.nn.softmax(s, axis=-1)
    return (bdot(p, v),)


def fn_final(h, tgt, w):
    def loss_fn(h, w):
        return 0.5 * jnp.sum(jnp.mean(jnp.square(rms(h, w) - tgt), axis=-1))

    val, (dh, dw) = jax.value_and_grad(loss_fn, argnums=(0, 1))(h, w)
    return dh, dh.astype(bf16), dw, jnp.full((8, LANES), val, f32)


def fn_sumcols(n):
    def fn(x):
        w = x.shape[1] // n
        s = x[:, :w]
        for q in range(1, n):
            s = s + x[:, q * w:(q + 1) * w]
        return (s,)
    return fn


def _tri(c):
    ii = lax.broadcasted_iota(jnp.int32, (c, c), 0)
    jj = lax.broadcasted_iota(jnp.int32, (c, c), 1)
    return ii, jj


def _neumann_raw(m, steps):
    c = m.shape[-1]
    ii, jj = _tri(c)
    eye = (ii == jj).astype(f32)
    t, p = eye + m, m
    for _ in range(steps):
        p = _raw_dot(p, p, "nn", P_INV)
        t = _raw_dot(t, eye + p, "nn", P_INV)
    resid = eye - t + _raw_dot(m, t, "nn", P_RESID)
    return t + _raw_dot(t, resid, "nn", P_INV)


@functools.partial(jax.custom_vjp, nondiff_argnums=(1,))
def _neumann_inverse(m, steps):
    return _neumann_raw(m, steps)


def _neumann_fwd(m, steps):
    t = _neumann_raw(m, steps)
    return t, t


def _neumann_bwd(steps, t, g):
    return (_raw_dot(_raw_dot(t, g, "tn", P_RESID), t, "nt", P_RESID),)


_neumann_inverse.defvjp(_neumann_fwd, _neumann_bwd)


@jax.custom_vjp
def _known_inverse(m, t):
    return t


_known_inverse.defvjp(lambda m, t: (t, t), lambda t, g: (_neumann_bwd(0, t, g)[0], jnp.zeros_like(t)))


def _inverse(m, steps, kept):
    return _neumann_inverse(m, steps) if kept is None else _known_inverse(m, kept)


def cumsum_rows(x):
    t = x.shape[1]
    ii, jj = _tri(t)
    tri = jnp.broadcast_to((ii >= jj).astype(f32), (x.shape[0], t, t))
    return pdot(tri, x, "nn", P_CUMSUM)


def gdn_chunk(q, k, v, gb, bb, s, kept=None):
    c = q.shape[1]
    ii, jj = _tri(c)
    low = ii >= jj
    gcb = cumsum_rows(gb)
    gl = jnp.sum(gb, axis=1, keepdims=True)
    gc_col = gcb[:, :, :c]
    diff = gc_col - jnp.swapaxes(gc_col, 1, 2)
    decay = jnp.where(low, jnp.exp(jnp.where(low, diff, 0.0)), 0.0)
    qs = q * (q.shape[2] ** -0.5)
    kb = k * bb
    with_k = pdot(jnp.concatenate([kb, qs], axis=1), k, "nt", P_BULK)
    a = jnp.where(ii > jj, with_k[:, :c] * decay, 0.0)
    attn = with_k[:, c:] * decay
    t = _inverse(-a, c.bit_length() - 2, kept)
    eg = jnp.exp(gcb)
    uw = pdot(t, jnp.concatenate([v * bb, kb * eg], axis=2), "nn", P_BULK)
    u, w = uw[:, :, :LANES], uw[:, :, LANES:]
    kd = k * jnp.exp(gl - gcb)
    from_state = pdot(jnp.concatenate([w, qs * eg], axis=1), s, "nn", P_BULK)
    v_new = u - from_state[:, :c]
    o = from_state[:, c:] + pdot(attn, v_new, "nn", P_BULK)
    s_new = s * jnp.exp(gl) + pdot(kd, v_new, "tn", P_BULK)
    return o, s_new, t


def wkv_chunk(r, lw, k, v, kk, a, s, kept=None):
    t = r.shape[1]
    ii, jj = _tri(t)
    lo = lax.broadcasted_iota(jnp.int32, r.shape, 2) < RW_HEAD
    cl = cumsum_rows(lw)
    cl_last = jnp.sum(lw, axis=1, keepdims=True)
    al = -kk * jnp.exp(cl - lw)
    be = (a * kk) * jnp.exp(-cl)
    kt = k * jnp.exp(-cl)
    rt = r * jnp.exp(cl)

    def dot(xa, xb, mode="nn"):
        return pdot(xa, xb, mode, P_BULK)

    def sel(x_lo, x_hi):
        return jnp.where(lo, x_lo, x_hi)

    left = jnp.concatenate([jnp.where(lo, al, 0.0), jnp.where(lo, 0.0, al),
                            jnp.where(lo, rt, 0.0), jnp.where(lo, 0.0, rt)], axis=1)
    with_be, with_kt = dot(left, be, "nt"), dot(left, kt, "nt")

    def blocks(prod):
        below, upto = ii > jj, ii >= jj
        return [jnp.where(msk, prod[:, q * t:(q + 1) * t], 0.0) for q, msk in enumerate((below, below, upto, upto))]

    ab_lo, ab_hi, rb_lo, rb_hi = blocks(with_be)
    ak_lo, ak_hi, rk_lo, rk_hi = blocks(with_kt)
    from_state = dot(jnp.concatenate([al, rt], axis=1), s, "nt")
    x = from_state[:, :t] + sel(dot(ak_lo, v), dot(ak_hi, v))
    steps = t.bit_length() - 2
    inv_lo = _inverse(ab_lo, steps, None if kept is None else kept[:, :t])
    inv_hi = _inverse(ab_hi, steps, None if kept is None else kept[:, t:])
    u = sel(dot(inv_lo, x), dot(inv_hi, x))
    y = from_state[:, t:] + sel(dot(rb_lo, u) + dot(rk_lo, v), dot(rb_hi, u) + dot(rk_hi, v))
    vi = lax.broadcasted_iota(jnp.int32, s.shape, 1) < RW_HEAD
    ki = lax.broadcasted_iota(jnp.int32, s.shape, 2) < RW_HEAD
    s_new = jnp.where(vi == ki, (s + dot(u, be, "tn") + dot(v, kt, "tn")) * jnp.exp(cl_last), 0.0)
    return y, s_new, jnp.concatenate([inv_lo, inv_hi], axis=1)


def _scan_group(ncol, offs):
    g = SCAN_GROUP
    while g > 1 and (ncol % g or any(o % g for o in offs)):
        g //= 2
    return g


def scan_fwd(name, chunk_fn, ins, *, rows, chunk, ncol):
    n = rows // chunk
    n_in = len(ins)
    grp = _scan_group(ncol, [off for _, off in ins])
    tile = jax.ShapeDtypeStruct((grp, chunk, LANES), f32)
    kept = jax.eval_shape(chunk_fn, *[tile] * n_in, jax.ShapeDtypeStruct((grp, LANES, LANES), f32))[2].shape[1:]

    def body(*refs):
        o_ref, st_ref, kept_ref, s_scr = refs[n_in:]

        @pl.when(pl.program_id(1) == 0)
        def _():
            s_scr[...] = jnp.zeros_like(s_scr)

        cols = [slice(b * LANES, (b + 1) * LANES) for b in range(grp)]
        s = s_scr[...]
        st_ref[...] = s
        o, s_new, inv = chunk_fn(*[jnp.stack([r[:, c] for c in cols]) for r in refs[:n_in]], s)
        for b, c in enumerate(cols):
            o_ref[:, c] = o[b]
        kept_ref[...] = inv
        s_scr[...] = s_new

    def spec(off):
        return pl.BlockSpec((chunk, grp * LANES), lambda h, c: (c, h + off // grp))

    def per_chunk(shape):
        return pl.BlockSpec((grp, None) + tuple(shape), lambda h, c: (h, c, 0, 0))

    return pl.pallas_call(
        body, name=name, grid=(ncol // grp, n), in_specs=[spec(off) for _, off in ins],
        out_specs=[spec(0), per_chunk((LANES, LANES)), per_chunk(kept)],
        out_shape=[jax.ShapeDtypeStruct((rows, ncol * LANES), f32),
                   jax.ShapeDtypeStruct((ncol, n, LANES, LANES), f32),
                   jax.ShapeDtypeStruct((ncol, n) + tuple(kept), f32)],
        scratch_shapes=[pltpu.VMEM((grp, LANES, LANES), f32)], compiler_params=_params(2))(*[a for a, _ in ins])


def scan_bwd(name, chunk_fn, ins, states, kept, d_out, d_off, *, rows, chunk, ncol):
    n = rows // chunk
    n_in = len(ins)
    grp = _scan_group(ncol, [off for _, off in ins] + [d_off])

    def body(*refs):
        st_ref, kept_ref, do_ref = refs[n_in:n_in + 3]
        g_refs = refs[n_in + 3:2 * n_in + 3]
        ds_scr = refs[-1]

        @pl.when(pl.program_id(1) == 0)
        def _():
            ds_scr[...] = jnp.zeros_like(ds_scr)

        cols = [slice(b * LANES, (b + 1) * LANES) for b in range(grp)]

        def batch(ref):
            return jnp.stack([ref[:, c] for c in cols])

        inv = kept_ref[...]
        _, pull = jax.vjp(lambda *a: chunk_fn(*a, kept=inv)[:2], *[batch(r) for r in refs[:n_in]], st_ref[...])
        gs = pull((batch(do_ref), ds_scr[...]))
        for ref, g in zip(g_refs, gs[:n_in]):
            for b, c in enumerate(cols):
                ref[:, c] = g[b]
        ds_scr[...] = gs[n_in]

    def spec(off):
        return pl.BlockSpec((chunk, grp * LANES), lambda h, c: (n - 1 - c, h + off // grp))

    def per_chunk(shape):
        return pl.BlockSpec((grp, None) + tuple(shape), lambda h, c: (h, n - 1 - c, 0, 0))

    return pl.pallas_call(
        body, name=name, grid=(ncol // grp, n),
        in_specs=[spec(off) for _, off in ins] + [per_chunk(states.shape[2:]), per_chunk(kept.shape[2:]), spec(d_off)],
        out_specs=[spec(0)] * n_in, out_shape=[jax.ShapeDtypeStruct((rows, ncol * LANES), f32)] * n_in,
        scratch_shapes=[pltpu.VMEM((grp, LANES, LANES), f32)],
        compiler_params=_params(2))(*[a for a, _ in ins], states, kept, d_out)


def flip_exchange(name, arrs, flips, n_slots, slot_of, src_of, with_self, after=()):
    n = len(arrs)
    nf = len(flips)
    n_after = len(after)

    def body(*refs):
        ins, outs = refs[:n], refs[n + n_after:2 * n + n_after]
        send, recv, lsem = refs[2 * n + n_after:]
        me = (lax.axis_index("x"), lax.axis_index("y"), lax.axis_index("c"))
        copies = []
        for k in range(n):
            if with_self:
                cp = pltpu.make_async_copy(src_of(ins[k], me), outs[k].at[slot_of(me)], lsem.at[k])
                cp.start()
                copies.append(cp)
            for j, fl in enumerate(flips):
                peer = tuple(1 - m if f else m for m, f in zip(me, fl))
                cp = pltpu.make_async_remote_copy(
                    src_ref=src_of(ins[k], peer), dst_ref=outs[k].at[slot_of(me)], send_sem=send.at[k, j],
                    recv_sem=recv.at[k, j], device_id=peer, device_id_type=MESH)
                cp.start()
                copies.append(cp)
        for cp in copies:
            cp.wait()

    def out_sds(a):
        blk = src_of(jax.ShapeDtypeStruct(a.shape, a.dtype), None)
        return jax.ShapeDtypeStruct((n_slots,) + tuple(blk), a.dtype)

    any_spec = pl.BlockSpec(memory_space=pl.ANY)
    return pl.pallas_call(
        body, name=name, in_specs=[any_spec] * (n + n_after), out_specs=[any_spec] * n,
        out_shape=[out_sds(a) for a in arrs],
        scratch_shapes=[pltpu.SemaphoreType.DMA((n, nf)), pltpu.SemaphoreType.DMA((n, nf)),
                        pltpu.SemaphoreType.DMA((n,))],
        compiler_params=pltpu.CompilerParams(has_side_effects=True))(*arrs, *after)


_CHIP_FLIPS = ((1, 0, 0), (0, 1, 0), (1, 1, 0))
_ALL_FLIPS = ((0, 0, 1), (0, 1, 0), (0, 1, 1), (1, 0, 0), (1, 0, 1), (1, 1, 0), (1, 1, 1))


def _whole(ref, pos):
    return ref.shape if pos is None else ref


def _chip_block(ref, pos):
    return ref.shape[1:] if pos is None else ref.at[2 * pos[0] + pos[1]]


def _chip_slot(p):
    return 2 * p[0] + p[1]


_HBM = pl.BlockSpec(memory_space=pltpu.HBM)
_SEM = pl.BlockSpec(memory_space=pltpu.SEMAPHORE)
_DATAFLOW = pltpu.SideEffectType.DATAFLOW_SIDE_EFFECTING


def _split_copies(mode, refs, n, send, recv):
    me = (lax.axis_index("x"), lax.axis_index("y"), lax.axis_index("c"))
    sib = (me[0], me[1], 1 - me[2])
    lands = refs[:n] if mode == "handover" else refs[n:2 * n]
    copies = []
    for k, land in enumerate(lands):
        half = land.shape[1] // 2
        mine = pl.ds(pl.multiple_of(me[2] * half, 16), half)
        for j, fl in enumerate(_CHIP_FLIPS):
            peer = tuple(1 - m if f else m for m, f in zip(me, fl))
            if mode == "gather":
                src, dst, to = refs[k], land.at[_chip_slot(me)], peer
            elif mode == "scatter":
                src, dst, to = refs[k].at[_chip_slot(peer)], land.at[_chip_slot(me)], peer
            elif mode == "gather_half":
                src, dst, to = refs[k].at[mine], land.at[_chip_slot(me), mine], peer
            else:
                src = dst = land.at[_chip_slot(peer), mine]
                to = sib
            q = k * len(_CHIP_FLIPS) + j
            copies.append(pltpu.make_async_remote_copy(src_ref=src, dst_ref=dst, send_sem=send.at[q],
                                                       recv_sem=recv.at[q], device_id=to, device_id_type=MESH))
    return copies


def split_start(name, mode, ops, n):
    ops = [pltpu.with_memory_space_constraint(a, pltpu.HBM) for a in ops]
    m = len(ops)

    def body(*refs):
        for cp in _split_copies(mode, refs[:m], n, refs[m], refs[m + 1]):
            cp.start()
        refs[-1][...] = jnp.zeros_like(refs[-1])

    sems = pltpu.SemaphoreType.DMA((n * len(_CHIP_FLIPS),))
    outs = pl.pallas_call(
        body, name=name, in_specs=[_HBM] * m,
        out_shape=(sems, sems, *[pltpu.HBM(a.shape, a.dtype) for a in ops], jax.ShapeDtypeStruct((8, LANES), f32)),
        out_specs=(_SEM, _SEM, *[_HBM] * m, pl.BlockSpec(memory_space=pltpu.VMEM)),
        input_output_aliases={i: 2 + i for i in range(m)},
        compiler_params=pltpu.CompilerParams(has_side_effects=_DATAFLOW))(*ops)
    return (outs[0], outs[1], list(outs[2:2 + m]), mode, n), outs[-1][0, 0]


def split_wait(name, state, after):
    send, recv, ops, mode, n = state
    m = len(ops)
    afters = list(after) if isinstance(after, (list, tuple)) else [after]

    def body(*refs):
        for cp in _split_copies(mode, refs[:m], n, refs[m], refs[m + 1]):
            cp.wait_send()
            cp.wait_recv()

    outs = pl.pallas_call(
        body, name=name, in_specs=[_HBM] * m + [_SEM, _SEM] + [pl.BlockSpec(memory_space=pl.ANY)] * len(afters),
        out_shape=tuple(pltpu.HBM(a.shape, a.dtype) for a in ops), out_specs=tuple([_HBM] * m),
        input_output_aliases={i: i for i in range(m)},
        compiler_params=pltpu.CompilerParams(has_side_effects=_DATAFLOW))(*ops, send, recv, *afters)
    return list(outs[m - n:])


def chips_start(name, arrs, src_of, halves=False):
    me = _chip_slot((lax.axis_index("x"), lax.axis_index("y")))
    lands = []
    for a in arrs:
        blk = tuple(src_of(jax.ShapeDtypeStruct(a.shape, a.dtype), None))
        own = a if src_of is _whole else lax.dynamic_index_in_dim(a, me, 0, keepdims=False)
        lands.append(lax.dynamic_update_index_in_dim(lax.empty((4,) + blk, a.dtype), own, me, 0))
    mode = "scatter" if src_of is _chip_block else ("gather_half" if halves else "gather")
    return split_start(name, mode, list(arrs) + lands, len(arrs))


chips_wait = split_wait


def handover_start(name, lands):
    return split_start(name, "handover", lands, len(lands))


def gather_chips_halves(name, arrs):
    n = len(arrs)
    nf = len(_CHIP_FLIPS)
    split = [a.shape[0] % 32 == 0 for a in arrs]

    def body(*refs):
        ins, outs = refs[:n], refs[n:2 * n]
        send1, recv1, send2, recv2, lsem = refs[2 * n:]
        me = (lax.axis_index("x"), lax.axis_index("y"), lax.axis_index("c"))
        sib = (me[0], me[1], 1 - me[2])
        peers = [tuple(1 - m if f else m for m, f in zip(me, fl)) for fl in _CHIP_FLIPS]
        local, first, second = [], [], []
        for k in range(n):
            cp = pltpu.make_async_copy(ins[k], outs[k].at[_chip_slot(me)], lsem.at[k])
            cp.start()
            local.append(cp)
            half = ins[k].shape[0] // 2
            rows = pl.ds(pl.multiple_of(me[2] * half, 16), half) if split[k] else pl.ds(0, ins[k].shape[0])
            for j, peer in enumerate(peers):
                cp = pltpu.make_async_remote_copy(
                    src_ref=ins[k].at[rows], dst_ref=outs[k].at[_chip_slot(me), rows], send_sem=send1.at[k, j],
                    recv_sem=recv1.at[k, j], device_id=peer, device_id_type=MESH)
                cp.start()
                first.append((k, j, rows, cp))
        for k, j, rows, cp in first:
            cp.wait_recv()
            if split[k]:
                got = outs[k].at[_chip_slot(peers[j]), rows]
                fwd = pltpu.make_async_remote_copy(src_ref=got, dst_ref=got, send_sem=send2.at[k, j],
                                                   recv_sem=recv2.at[k, j], device_id=sib, device_id_type=MESH)
                fwd.start()
                second.append(fwd)
        for _, _, _, cp in first:
            cp.wait_send()
        for cp in second:
            cp.wait()
        for cp in local:
            cp.wait()

    any_spec = pl.BlockSpec(memory_space=pl.ANY)
    sems = pltpu.SemaphoreType.DMA((n, nf))
    return pl.pallas_call(
        body, name=name, in_specs=[any_spec] * n, out_specs=[any_spec] * n,
        out_shape=[jax.ShapeDtypeStruct((4,) + a.shape, a.dtype) for a in arrs],
        scratch_shapes=[sems, sems, sems, sems, pltpu.SemaphoreType.DMA((n,))],
        compiler_params=pltpu.CompilerParams(has_side_effects=True))(*arrs)


def swap_sibling(name, arrs):
    outs = flip_exchange(name, arrs, ((0, 0, 1),), 1, lambda p: 0, _whole, False)
    return [o[0] for o in outs]


def gather_all(arrs, after=()):
    return flip_exchange("gather_all", arrs, _ALL_FLIPS, 8, lambda p: 4 * p[0] + 2 * p[1] + p[2], _whole, True,
                         after=after)


def _row_tile(nr, nc, n_arrays):
    budget = (20 << 20) // (n_arrays * 2 * 4 * max(nc, LANES))
    t = min(nr, budget) // 16 * 16
    while t > 0 and nr % t:
        t -= 16
    return t if t > 0 else nr


def sum_slots(name, x):
    ns, nr, nc = x.shape
    tile = _row_tile(nr, nc, ns + 1)

    def body(x_ref, o_ref):
        s = x_ref[0].astype(f32)
        for q in range(1, ns):
            s = s + x_ref[q].astype(f32)
        o_ref[...] = s.astype(o_ref.dtype)

    return pl.pallas_call(
        body, name=name, grid=(nr // tile,), in_specs=[pl.BlockSpec((ns, tile, nc), lambda i: (0, i, 0))],
        out_specs=pl.BlockSpec((tile, nc), lambda i: (i, 0)), out_shape=jax.ShapeDtypeStruct((nr, nc), x.dtype),
        compiler_params=_params(1))(x)


def adamw(name, w, g_parts, m, v):
    nr, nc = w.shape[-2:]
    n_g = len(g_parts)
    tile = _row_tile(nr, nc, 7 + n_g)

    def body(*refs):
        w_ref, m_ref, v_ref = refs[:3]
        g = refs[3][...].astype(f32)
        for r in refs[4:3 + n_g]:
            g = g + r[...].astype(f32)
        g_ref, d_ref, nm_ref, nv_ref = refs[3 + n_g:]
        nm = ADAM_B1 * m_ref[...] + (1.0 - ADAM_B1) * g
        nv = ADAM_B2 * v_ref[...] + (1.0 - ADAM_B2) * jnp.square(g)
        m_hat = nm / (1.0 - ADAM_B1 ** ADAM_STEP)
        v_hat = nv / (1.0 - ADAM_B2 ** ADAM_STEP)
        g_ref[...] = g
        d_ref[...] = -ADAM_LR * (m_hat / (jnp.sqrt(v_hat) + ADAM_EPS) + ADAM_WD * w_ref[...])
        nm_ref[...] = nm
        nv_ref[...] = nv

    spec = pl.BlockSpec((tile, nc), lambda i: (i, 0))
    spec3 = pl.BlockSpec((None, tile, nc), lambda i: (0, i, 0)) if w.ndim == 3 else spec
    return pl.pallas_call(
        body, name=name, grid=(nr // tile,), in_specs=[spec3] * 3 + [spec] * n_g, out_specs=[spec3] * 4,
        out_shape=[jax.ShapeDtypeStruct(w.shape, f32)] * 4, compiler_params=_params(1))(w, m, v, *g_parts)


def adamw_packed(w, g8, m, v):
    nr, nc = w.shape

    def body(w_ref, g_ref, m_ref, v_ref, go_ref, d_ref, nm_ref, nv_ref):
        g = g_ref[0]
        for q in range(1, 8):
            g = g + g_ref[q]
        nm = ADAM_B1 * m_ref[...] + (1.0 - ADAM_B1) * g
        nv = ADAM_B2 * v_ref[...] + (1.0 - ADAM_B2) * jnp.square(g)
        m_hat = nm / (1.0 - ADAM_B1 ** ADAM_STEP)
        v_hat = nv / (1.0 - ADAM_B2 ** ADAM_STEP)
        go_ref[...] = g
        d_ref[...] = -ADAM_LR * (m_hat / (jnp.sqrt(v_hat) + ADAM_EPS) + ADAM_WD * w_ref[...])
        nm_ref[...] = nm
        nv_ref[...] = nv

    return pl.pallas_call(body, name="adamw_packed", out_shape=[jax.ShapeDtypeStruct((nr, nc), f32)] * 4,
                          compiler_params=pltpu.CompilerParams(vmem_limit_bytes=VMEM_LIMIT))(w, g8, m, v)


def _pack(vectors):
    rows = []
    for a in vectors:
        flat = a.reshape(-1).astype(f32)
        pad = (-flat.shape[0]) % LANES
        rows.append(jnp.pad(flat, (0, pad)).reshape(-1, LANES))
    packed = jnp.concatenate(rows, axis=0)
    return jnp.pad(packed, ((0, (-packed.shape[0]) % 8), (0, 0)))


def _unpack(packed, like):
    out, r = [], 0
    for a in like:
        n = a.size
        nr = -(-n // LANES)
        out.append(packed[r:r + nr].reshape(-1)[:n].reshape(a.shape))
        r += nr
    return out


def kernel(x, mem, mix_norm_w, w_in, dn_conv_w, dn_a_log, dn_dt_bias, dn_norm_w, rw_mu, rw_w0, rw_w2, rw_a0, rw_a2, rw_g2, rw_k_k, rw_k_a, rw_r_k, rw_ln_w, rw_ln_b, w_out, xa_norm_w, mem_norm_w, xa_wq, xa_wk, xa_wv, xa_wo, ffn_norm_w, ffn_w1, ffn_w2, final_norm_w, loss_target, m_mix_norm_w, m_w_in, m_dn_conv_w, m_dn_a_log, m_dn_dt_bias, m_dn_norm_w, m_rw_mu, m_rw_w0, m_rw_w2, m_rw_a0, m_rw_a2, m_rw_g2, m_rw_k_k, m_rw_k_a, m_rw_r_k, m_rw_ln_w, m_rw_ln_b, m_w_out, m_xa_norm_w, m_mem_norm_w, m_xa_wq, m_xa_wk, m_xa_wv, m_xa_wo, m_ffn_norm_w, m_ffn_w1, m_ffn_w2, m_final_norm_w, v_mix_norm_w, v_w_in, v_dn_conv_w, v_dn_a_log, v_dn_dt_bias, v_dn_norm_w, v_rw_mu, v_rw_w0, v_rw_w2, v_rw_a0, v_rw_a2, v_rw_g2, v_rw_k_k, v_rw_k_a, v_rw_r_k, v_rw_ln_w, v_rw_ln_b, v_w_out, v_xa_norm_w, v_mem_norm_w, v_xa_wq, v_xa_wk, v_xa_wv, v_xa_wo, v_ffn_norm_w, v_ffn_w1, v_ffn_w2, v_final_norm_w):
    weights = dict(mix_norm_w=mix_norm_w, w_in=w_in, dn_conv_w=dn_conv_w, dn_a_log=dn_a_log, dn_dt_bias=dn_dt_bias, dn_norm_w=dn_norm_w, rw_mu=rw_mu, rw_w0=rw_w0, rw_w2=rw_w2, rw_a0=rw_a0, rw_a2=rw_a2, rw_g2=rw_g2, rw_k_k=rw_k_k, rw_k_a=rw_k_a, rw_r_k=rw_r_k, rw_ln_w=rw_ln_w, rw_ln_b=rw_ln_b, w_out=w_out, xa_norm_w=xa_norm_w, mem_norm_w=mem_norm_w, xa_wq=xa_wq, xa_wk=xa_wk, xa_wv=xa_wv, xa_wo=xa_wo, ffn_norm_w=ffn_norm_w, ffn_w1=ffn_w1, ffn_w2=ffn_w2, final_norm_w=final_norm_w)
    mom_m = dict(mix_norm_w=m_mix_norm_w, w_in=m_w_in, dn_conv_w=m_dn_conv_w, dn_a_log=m_dn_a_log, dn_dt_bias=m_dn_dt_bias, dn_norm_w=m_dn_norm_w, rw_mu=m_rw_mu, rw_w0=m_rw_w0, rw_w2=m_rw_w2, rw_a0=m_rw_a0, rw_a2=m_rw_a2, rw_g2=m_rw_g2, rw_k_k=m_rw_k_k, rw_k_a=m_rw_k_a, rw_r_k=m_rw_r_k, rw_ln_w=m_rw_ln_w, rw_ln_b=m_rw_ln_b, w_out=m_w_out, xa_norm_w=m_xa_norm_w, mem_norm_w=m_mem_norm_w, xa_wq=m_xa_wq, xa_wk=m_xa_wk, xa_wv=m_xa_wv, xa_wo=m_xa_wo, ffn_norm_w=m_ffn_norm_w, ffn_w1=m_ffn_w1, ffn_w2=m_ffn_w2, final_norm_w=m_final_norm_w)
    mom_v = dict(mix_norm_w=v_mix_norm_w, w_in=v_w_in, dn_conv_w=v_dn_conv_w, dn_a_log=v_dn_a_log, dn_dt_bias=v_dn_dt_bias, dn_norm_w=v_dn_norm_w, rw_mu=v_rw_mu, rw_w0=v_rw_w0, rw_w2=v_rw_w2, rw_a0=v_rw_a0, rw_a2=v_rw_a2, rw_g2=v_rw_g2, rw_k_k=v_rw_k_k, rw_k_a=v_rw_k_a, rw_r_k=v_rw_r_k, rw_ln_w=v_rw_ln_w, rw_ln_b=v_rw_ln_b, w_out=v_w_out, xa_norm_w=v_xa_norm_w, mem_norm_w=v_mem_norm_w, xa_wq=v_xa_wq, xa_wk=v_xa_wk, xa_wv=v_xa_wv, xa_wo=v_xa_wo, ffn_norm_w=v_ffn_norm_w, ffn_w1=v_ffn_w1, ffn_w2=v_ffn_w2, final_norm_w=v_final_norm_w)
    names = list(weights)

    seq, d = x.shape[1], x.shape[2]
    dnw = d // 2
    rww = d - dnw
    nh, nb = dnw // LANES, rww // LANES
    n_mem = mem.shape[1]
    lw_dim, la_dim, lg_dim = rw_w2.shape[1], rw_a2.shape[1], rw_g2.shape[1]
    assert lw_dim + la_dim == LANES and lg_dim == LANES and dnw % LANES == 0 and rww % LANES == 0
    xs, mems, tgt = x[0], mem[0], loss_target[0]

    col_sharded = ("w_in", "xa_wo", "ffn_w1", "dn_conv_w", "rw_w2", "rw_a2", "rw_g2")
    row_sharded = ("w_out", "xa_wq", "xa_wk", "xa_wv", "ffn_w2")
    f32_payload = ("dn_conv_w", "rw_w2", "rw_a2", "rw_g2")
    sharded = col_sharded + row_sharded
    payload = {n: weights[n][0].astype(f32 if n in f32_payload else bf16) for n in sharded}
    shard_w = w_in.shape[2]
    pad_w = -(-shard_w // LANES) * LANES
    first = ("w_in", "dn_conv_w", "rw_w2", "rw_a2", "rw_g2")
    mid = ("w_out", "xa_wq", "xa_wk", "xa_wv", "xa_wo")
    late = ("ffn_w1", "ffn_w2")
    gathered = dict(zip(first, gather_chips_halves("gather_first", [payload[n] for n in first])))
    ordered = lax.optimization_barrier(([gathered[n] for n in first], [payload[n] for n in mid + late]))
    gathered = dict(zip(first, ordered[0]))
    payload.update(zip(mid + late, ordered[1]))
    mid_state, tok_mid = chips_start("gather_mid_start", [payload[n] for n in mid], _whole, halves=True)
    late_state, tok_late = chips_start("gather_late_start", [payload[n] for n in late], _whole, halves=True)
    mix_norm_w_in = mix_norm_w + (tok_mid + tok_late)

    def full(n):
        g = gathered[n]
        if n in col_sharded:
            return g.transpose(1, 0, 2).reshape(g.shape[1], 4 * g.shape[2])
        return g.reshape(4 * g.shape[1], g.shape[2])

    c_rw0 = 4 * dnw + 2 * nh
    c_rw = 3 * rww
    eb_ab, eb_r = 4 * nh, 4 * nh + 1
    eb_l1 = eb_r + 3 * nb
    n_ext = -(-(eb_l1 + 2) // 4) * 4
    tbl_fwd, tbl_bwd, per_shard = w_in_layout(4 * shard_w, shard_w, c_rw0, LANES - 2 * nh, n_ext)
    sub_f = max(s for s in range(1, REGROUP_SUB + 1) if n_ext % s == 0)
    sub_b = max(s for s in range(1, REGROUP_SUB + 1) if per_shard % s == 0)
    w_ext = lane_regroup(
        "w_in_regroup", gathered["w_in"], tbl_fwd,
        lambda p, b: pl.BlockSpec(
            (None, d, LANES), lambda i, j, t: (t[REGROUP_FIELDS * p, j * sub_f + b], 0,
                                               t[REGROUP_FIELDS * p + 1, j * sub_f + b])),
        pl.BlockSpec((d, sub_f * LANES), lambda i, j, t: (0, j)), (d, n_ext * LANES), (1, n_ext // sub_f), shard_w,
        sub_f)
    conv_w = full("dn_conv_w")
    w2p = jnp.concatenate([full("rw_w2"), jnp.zeros((la_dim, rww), f32)], axis=0)
    a2p = jnp.concatenate([jnp.zeros((lw_dim, rww), f32), full("rw_a2")], axis=0)
    g2 = full("rw_g2")
    xaw = xa_wq.shape[2]
    nxh = xaw // LANES
    ffn = 4 * ffn_w1.shape[2]

    def lane_row(vec):
        return jnp.pad(vec.reshape(1, -1), ((0, 0), (0, LANES - vec.size)))

    alog_row, dtb_row = lane_row(dn_a_log), lane_row(dn_dt_bias)
    head_of_col = jnp.arange(dnw)[None, :] // LANES
    e_g = (jnp.arange(LANES)[:, None] == head_of_col).astype(f32)
    e_b = (jnp.arange(LANES)[:, None] == head_of_col + nh).astype(f32)
    mu_main, mu_small = rw_mu[:, :c_rw], rw_mu[:, c_rw:]
    r_k_row = rw_r_k.reshape(1, rww)
    fnw = final_norm_w.reshape(1, d)
    qb, kb_, vb, zb = 0, nh, 2 * nh, 3 * nh
    rb0 = eb_r
    cc = lambda w_, o_: ("constc", w_, o_)
    rc = lambda o_: ("rowc", LANES, o_)

    (u16,) = rowcall("norm_mix", fn_rms, [(xs, "row"), (mix_norm_w_in, "const")], [(seq, d, bf16, "row")], rows=seq)
    p_main = p_small = mm("in_proj", u16, w_ext)

    fn_gconv = make_fn_gconv(2 * nh)
    (qkv,) = rowcall("gdn_conv", fn_gconv, [(p_main, rc(0)), (conv_w, cc(LANES, 0))],
                     [(seq, 3 * dnw, f32, "rowc")], rows=seq, tile=seq, ncol=3 * nh)
    gate_ins = [(p_small, rc(eb_ab)), (alog_row, "const"), (dtb_row, "const"), (e_g, "const"), (e_b, "const")]
    g_b, beta_b = rowcall("gdn_gate", fn_ggate, gate_ins, [(seq, dnw, f32, "row")] * 2, rows=seq)
    gdn_ins = [(qkv, qb), (qkv, kb_), (qkv, vb), (g_b, 0), (beta_b, 0)]
    o_raw, gdn_states, gdn_kept = scan_fwd("gdn_scan", gdn_chunk, gdn_ins, rows=seq, chunk=GDN_CHUNK, ncol=nh)
    mid_state, tok = handover_start("gather_mid_pass", chips_wait("gather_mid_wait", mid_state, o_raw))
    gpost_ins = [(o_raw, rc(0)), (p_main, rc(zb)), (dn_norm_w + tok, "const")]
    (o_dn,) = rowcall("gdn_post", fn_gpost, gpost_ins, [(seq, dnw, bf16, "rowc")], rows=seq, ncol=nh)

    (prw,) = rowcall("rw_lerp_main", fn_lerp, [(p_main, rc(rb0)), (mu_main, cc(LANES, 0))],
                     [(seq, c_rw, f32, "rowc")], rows=seq, tile=seq, ncol=3 * nb)
    (psl,) = rowcall("rw_lerp_small", fn_lerp, [(p_small, rc(eb_l1)), (mu_small, cc(LANES, 0))],
                     [(seq, 2 * LANES, f32, "rowc")], rows=seq, tile=seq, ncol=2)
    rprep_ins = [(prw, rc(nb)), (psl, "row"), (rw_w0, cc(LANES, 0)), (rw_a0, cc(LANES, 0)), (rw_k_k, cc(LANES, 0)),
                 (rw_k_a, cc(LANES, 0)), (w2p, cc(LANES, 0)), (a2p, cc(LANES, 0)), (g2, cc(LANES, 0))]
    lw, kmod, kk, a_rw, gate = rowcall("rw_prep", fn_rprep, rprep_ins, [(seq, rww, f32, "rowc")] * 5,
                                        rows=seq, ncol=nb)
    wkv_ins = [(prw, 0), (lw, 0), (kmod, 0), (prw, 2 * nb), (kk, 0), (a_rw, 0)]
    y_rw, wkv_states, wkv_kept = scan_fwd("wkv_scan", wkv_chunk, wkv_ins, rows=seq, chunk=WKV_CHUNK, ncol=nb)
    rpost_ins = [(y_rw, rc(0)), (prw, rc(0)), (kmod, rc(0)), (prw, rc(2 * nb)), (gate, rc(0)),
                 (r_k_row, cc(LANES, 0)), (rw_ln_w, cc(LANES, 0)), (rw_ln_b, cc(LANES, 0))]
    (o_rw,) = rowcall("rw_post", fn_rpost, rpost_ins, [(seq, rww, bf16, "rowc")], rows=seq, ncol=nb)

    o_cat = jnp.concatenate([o_dn, o_rw], axis=1)
    late_state, tok = handover_start("gather_late_pass", chips_wait("gather_late_wait", late_state, o_cat))
    gathered.update(zip(mid, chips_wait("gather_mid_pass_wait", mid_state, o_cat)))
    w_out_f, wq_f, wk_f, wv_f, wo_f = full("w_out"), full("xa_wq"), full("xa_wk"), full("xa_wv"), full("xa_wo")
    h1 = mm("out_proj", o_cat, w_out_f, add=xs)

    (hn16,) = rowcall("norm_xa", fn_rms, [(h1, "row"), (xa_norm_w + tok, "const")], [(seq, d, bf16, "row")],
                      rows=seq)
    (mn16,) = rowcall("norm_mem", fn_rms, [(mems, "row"), (mem_norm_w, "const")], [(n_mem, d, bf16, "row")],
                      rows=n_mem)
    q_xa = mm("xa_q", hn16, wq_f)
    k_xa = mm("xa_k", mn16, wk_f)
    v_xa = mm("xa_v", mn16, wv_f)
    xcore_ins = [(q_xa, rc(0)), (k_xa, cc(LANES, 0)), (v_xa, cc(LANES, 0))]
    (o_xa,) = rowcall("xa_core", fn_xcore, xcore_ins, [(seq, xaw, bf16, "rowc")], rows=seq, ncol=nxh)
    h2 = mm("xa_o", o_xa, wo_f, add=h1)

    (fn16,) = rowcall("norm_ffn", fn_rms, [(h2, "row"), (ffn_norm_w, "const")], [(seq, d, bf16, "row")], rows=seq)
    gathered.update(zip(late, chips_wait("gather_late_pass_wait", late_state, fn16)))
    w1_f, w2_f = full("ffn_w1"), full("ffn_w2")
    a1_16, hid16 = mm("ffn_up", fn16, w1_f, epilogue=lambda r: (r, jnp.square(jnp.maximum(r, 0.0))),
                      out_dtypes=(bf16, bf16))
    h3 = mm("ffn_down", hid16, w2_f, add=h2)

    dh3, dh3_16, d_fnw, loss_rows = rowcall(
        "loss_head", fn_final, [(h3, "row"), (tgt, "row"), (fnw, "const")],
        [(seq, d, f32, "row"), (seq, d, bf16, "row"), (1, d, f32, "acc"), (8, LANES, f32, "acc")], rows=seq)

    da1_16 = mm("ffn_down_dx", dh3_16, w2_f, tb=True, extra=[a1_16], out_dtype=bf16,
                epilogue=lambda r, a1: (r * (2.0 * jnp.maximum(a1.astype(f32), 0.0)),))
    def by_chip(n, g):
        if g.ndim == 3:
            return g
        if n in col_sharded:
            return g.reshape(g.shape[0], 4, g.shape[1] // 4).transpose(1, 0, 2)
        return g.reshape(4, g.shape[0] // 4, g.shape[1])

    g_ffn_w2 = mm("ffn_down_dw", hid16, dh3_16, ta=True, out_dtype=bf16)
    ffn_w2_g, tok = chips_start("scatter_ffn_w2_start", [by_chip("ffn_w2", g_ffn_w2)], _chip_block)
    g_ffn_w1 = mm("ffn_up_dw", fn16, da1_16, ta=True, out_dtype=bf16, by_chip=True, after=tok)
    ffn_w1_g, tok = chips_start("scatter_ffn_w1_start", [g_ffn_w1], _chip_block)
    dfn = mm("ffn_up_dx", da1_16, w1_f, tb=True, after=tok)
    dh2, d_ffn_nw, dh2_16 = rowvjp("norm_ffn_bwd", fn_rms, [(h2, "row"), (ffn_norm_w, "const")],
                                   [[(dfn, "row")]], [0, 1], rows=seq, adds=[(0, dh3, "row")], dup16=[0])

    do_xa = mm("xa_o_dx", dh2_16, wo_f, tb=True)
    g_xa_wo = mm("xa_o_dw", o_xa, dh2_16, ta=True, out_dtype=bf16, by_chip=True)
    dq_xa, dk_xa, dv_xa, dq16 = rowvjp("xa_core_bwd", fn_xcore, xcore_ins, [[(do_xa, rc(0))]], [0, 1, 2],
                                       rows=seq, ncol=nxh, dup16=[0])
    g_xa_wq = mm("xa_q_dw", hn16, dq16, ta=True, out_dtype=bf16)
    dhn = mm("xa_q_dx", dq16, wq_f, tb=True)
    dh1, d_xa_nw, dh1_16 = rowvjp("norm_xa_bwd", fn_rms, [(h1, "row"), (xa_norm_w, "const")], [[(dhn, "row")]],
                                  [0, 1], rows=seq, adds=[(0, dh2, "row")], dup16=[0])
    dk16, dv16 = dk_xa.astype(bf16), dv_xa.astype(bf16)
    g_xa_wk = mm("xa_k_dw", mn16, dk16, ta=True, out_dtype=bf16)
    g_xa_wv = mm("xa_v_dw", mn16, dv16, ta=True, out_dtype=bf16)
    dmn = mm("xa_v_dx", dv16, wv_f, tb=True, add=mm("xa_k_dx", dk16, wk_f, tb=True))
    (d_mem_nw,) = rowvjp("norm_mem_bwd", fn_rms, [(mems, "row"), (mem_norm_w, "const")], [[(dmn, "row")]], [1],
                         rows=n_mem)

    g_w_out = mm("out_proj_dw", o_cat, dh1_16, ta=True, out_dtype=bf16)
    mid_grads = dict(w_out=g_w_out, xa_wq=g_xa_wq, xa_wk=g_xa_wk, xa_wv=g_xa_wv, xa_wo=g_xa_wo)
    mid_g, tok = chips_start("scatter_mid_start", [by_chip(n, mid_grads[n]) for n in mid], _chip_block)
    do_cat = mm("out_proj_dx", dh1_16, w_out_f, tb=True, after=tok)

    dy, dr_a, dkmod_a, dv_a, dgate, d_r_k, d_ln_w, d_ln_b = rowvjp(
        "rw_post_bwd", fn_rpost, rpost_ins, [[(do_cat, rc(nh))]], [0, 1, 2, 3, 4, 5, 6, 7], rows=seq, ncol=nb)
    dr_b, dlw, dkmod_b, dv_b, dkk, da_rw = scan_bwd("wkv_scan_bwd", wkv_chunk, wkv_ins, wkv_states, wkv_kept, dy, 0,
                                                    rows=seq, chunk=WKV_CHUNK, ncol=nb)
    rprep_cts = [[(dlw, rc(0))], [(dkmod_a, rc(0)), (dkmod_b, rc(0))], [(dkk, rc(0))], [(da_rw, rc(0))],
                 [(dgate, rc(0))]]
    dpk, dpsl_parts, d_w0, d_a0, d_k_k, d_k_a, d_w2p, d_a2p, d_g2 = rowvjp(
        "rw_prep_bwd", fn_rprep, rprep_ins, rprep_cts, [0, 1, 2, 3, 4, 5, 6, 7, 8], rows=seq, ncol=nb)
    (dpsl,) = rowcall("rw_prep_sum", fn_sumcols(nb), [(dpsl_parts, "row")], [(seq, 2 * LANES, f32, "row")], rows=seq)

    def lerp_bwd(tag, p, p_off, mu, mu_off, ct_lists, ncol):
        return rowvjp("rw_lerp_bwd_" + tag, fn_lerp, [(p, rc(p_off)), (mu, cc(LANES, mu_off))], [ct_lists], [0, 1],
                      rows=seq, tile=seq, ncol=ncol, dup16=[0])

    _, dmu_r, dpr16 = lerp_bwd("r", p_main, rb0, mu_main, 0, [(dr_a, rc(0)), (dr_b, rc(0))], nb)
    _, dmu_k, dpk16 = lerp_bwd("k", p_main, rb0 + nb, mu_main, nb, [(dpk, rc(0))], nb)
    _, dmu_v, dpv16 = lerp_bwd("v", p_main, rb0 + 2 * nb, mu_main, 2 * nb, [(dv_a, rc(0)), (dv_b, rc(0))], nb)
    _, dmu_s, dps12_16 = lerp_bwd("small", p_small, eb_l1, mu_small, 0, [(dpsl, rc(0))], 2)

    do_raw, dz, d_dn_nw, dz16 = rowvjp("gdn_post_bwd", fn_gpost, gpost_ins, [[(do_cat, rc(0))]], [0, 1, 2],
                                       rows=seq, ncol=nh, dup16=[1])
    dq_g, dk_g, dv_g, dg_b, dbeta_b = scan_bwd("gdn_scan_bwd", gdn_chunk, gdn_ins, gdn_states, gdn_kept, do_raw, 0,
                                               rows=seq, chunk=GDN_CHUNK, ncol=nh)
    dps0, d_alog, d_dtb, dps0_16 = rowvjp("gdn_gate_bwd", fn_ggate, gate_ins, [[(dg_b, "row")], [(dbeta_b, "row")]],
                                          [0, 1, 2], rows=seq, dup16=[0])
    dqkv = jnp.concatenate([dq_g, dk_g, dv_g], axis=1)
    _, d_conv_w, dqkv16 = rowvjp("gdn_conv_bwd", fn_gconv, [(p_main, rc(0)), (conv_w, cc(LANES, 0))],
                                 [[(dqkv, rc(0))]], [0, 1], rows=seq, tile=seq, ncol=3 * nh, dup16=[0])

    dp16 = jnp.concatenate([dqkv16, dz16, dps0_16, dpr16, dpk16, dpv16, dps12_16,
                            jnp.zeros((seq, (n_ext - eb_l1 - 2) * LANES), bf16)], axis=1)
    g_w_ext = mm("in_proj_dw", u16, dp16, ta=True, out_dtype=bf16)
    g_w_in = lane_regroup(
        "w_in_grad_regroup", g_w_ext, tbl_bwd,
        lambda p, b: pl.BlockSpec((d, LANES), lambda i, j, t: (0, t[REGROUP_FIELDS * p + 1, j * sub_b + b])),
        pl.BlockSpec((None, d, sub_b * LANES),
                     lambda i, j, t: (j // (per_shard // sub_b), 0, j % (per_shard // sub_b))),
        (4, d, pad_w), (1, 4 * per_shard // sub_b), n_ext * LANES, sub_b)
    first_grads = dict(w_in=g_w_in, dn_conv_w=d_conv_w, rw_w2=d_w2p[:lw_dim], rw_a2=d_a2p[lw_dim:], rw_g2=d_g2)
    first_g, tok = chips_start("scatter_first_start", [by_chip(n, first_grads[n]) for n in first], _chip_block)

    du = mm("in_proj_dx", dp16, w_ext, tb=True, after=tok)
    grad_x, d_mix_nw = rowvjp("norm_mix_bwd", fn_rms, [(xs, "row"), (mix_norm_w, "const")], [[(du, "row")]],
                              [0, 1], rows=seq, adds=[(0, dh1, "row")])
    received = dict(zip(mid, chips_wait("scatter_mid_wait", mid_g, grad_x)))
    received["ffn_w2"] = chips_wait("scatter_ffn_w2_wait", ffn_w2_g, grad_x)[0]
    received["ffn_w1"] = chips_wait("scatter_ffn_w1_wait", ffn_w1_g, grad_x)[0]

    out = {}

    def reduce_and_update(tag, group):
        partial = [sum_slots("sum_chips_" + n, received[n]) for n in group]
        other = swap_sibling("swap_sibling_" + tag, partial)
        for n, p_mine, p_other in zip(group, partial, other):
            if weights[n].shape[2] % LANES:
                rows, cols = weights[n].shape[1:]
                lin = lambda a: jnp.swapaxes(a, 1, 2).reshape(-1, LANES)
                lin_g = lambda p: p.T[:cols].reshape(-1, LANES)
                res = adamw("adamw_" + n, lin(weights[n]), [lin_g(p_mine), lin_g(p_other)], lin(mom_m[n]),
                            lin(mom_v[n]))
                out[n] = [jnp.swapaxes(r.reshape(1, cols, rows), 1, 2) for r in res]
            else:
                out[n] = adamw("adamw_" + n, weights[n], [p_mine, p_other], mom_m[n], mom_v[n])

    reduce_and_update("rest", mid + late)
    received.update(zip(first, chips_wait("scatter_first_wait", first_g, [out[n][1] for n in mid + late])))
    reduce_and_update("first", first)

    small_names = [n for n in names if n not in sharded]
    small_local = dict(
        mix_norm_w=d_mix_nw, dn_a_log=d_alog[:, :nh], dn_dt_bias=d_dtb[:, :nh], dn_norm_w=d_dn_nw,
        rw_mu=jnp.concatenate([dmu_r, dmu_k, dmu_v, dmu_s], axis=1), rw_w0=d_w0, rw_a0=d_a0, rw_k_k=d_k_k,
        rw_k_a=d_k_a, rw_r_k=d_r_k, rw_ln_w=d_ln_w, rw_ln_b=d_ln_b, xa_norm_w=d_xa_nw, mem_norm_w=d_mem_nw,
        ffn_norm_w=d_ffn_nw, final_norm_w=d_fnw)
    loss_vec = jnp.where(jnp.arange(LANES) == 0, loss_rows[0], 0.0)
    (g8,) = gather_all([_pack([small_local[n] for n in small_names] + [loss_vec])], after=[out["w_in"][1]])

    packed_like = [weights[n] for n in small_names] + [loss_vec]
    zero = jnp.zeros((LANES,), f32)
    res = adamw_packed(_pack([weights[n] for n in small_names] + [zero]), g8,
                       _pack([mom_m[n] for n in small_names] + [zero]),
                       _pack([mom_v[n] for n in small_names] + [zero]))
    unpacked = [_unpack(r, packed_like) for r in res]
    for i, n in enumerate(small_names):
        out[n] = [u[i] for u in unpacked]
    loss = unpacked[0][-1][0]

    return (loss, grad_x.reshape(x.shape), *[out[n][0] for n in names], *[out[n][1] for n in names],
            *[out[n][2] for n in names], *[out[n][3] for n in names])
```

```python
import functools

import jax
import jax.numpy as jnp
from jax import lax
from jax.experimental import pallas as pl
from jax.experimental.pallas import tpu as pltpu

f32 = jnp.float32
bf16 = jnp.bfloat16
HI = lax.Precision.HIGHEST
MESH = pl.DeviceIdType.MESH

LANES = 128
VMEM_LIMIT = 56 << 20
TOK_TILE = 256
TOK_TILE_BLOCKED = 1024
MM_TILE = 1024
MM_TILE_K = 2048
GDN_CHUNK = 128
WKV_CHUNK = 64
SCAN_GROUP = 8
P_BULK = 1
P_INV = 1
P_RESID = 3
P_CUMSUM = 3
RMS_EPS = 1e-6
RW_GN_EPS = 64e-5
RW_HEAD = 64

ADAM_LR, ADAM_B1, ADAM_B2, ADAM_EPS, ADAM_WD, ADAM_STEP = 0.001, 0.9, 0.999, 1e-08, 0.01, 10


def _params(n_grid):
    return pltpu.CompilerParams(dimension_semantics=("arbitrary",) * n_grid, vmem_limit_bytes=VMEM_LIMIT)


_DIMS = {"nn": (((1,), (0,)), ((), ())), "nt": (((1,), (1,)), ((), ())), "tn": (((0,), (0,)), ((), ()))}
_DIMS_BATCHED = {"nn": (((2,), (1,)), ((0,), (0,))), "nt": (((2,), (2,)), ((0,), (0,))),
                 "tn": (((1,), (1,)), ((0,), (0,)))}


def _raw_dot(a, b, mode, passes):
    dims = (_DIMS if a.ndim == 2 else _DIMS_BATCHED)[mode]
    if passes == 6:
        return lax.dot_general(a.astype(f32), b.astype(f32), dims, precision=HI, preferred_element_type=f32)
    ah, bh = a.astype(bf16), b.astype(bf16)
    r = lax.dot_general(ah, bh, dims, preferred_element_type=f32)
    if passes == 3:
        al = (a - ah.astype(f32)).astype(bf16)
        bl = (b - bh.astype(f32)).astype(bf16)
        r = r + lax.dot_general(al, bh, dims, preferred_element_type=f32)
        r = r + lax.dot_general(ah, bl, dims, preferred_element_type=f32)
    return r


@functools.partial(jax.custom_vjp, nondiff_argnums=(2, 3))
def pdot(a, b, mode, passes):
    return _raw_dot(a, b, mode, passes)


def _pdot_bwd(mode, passes, res, g):
    a, b = res
    if mode == "nn":
        da, db = _raw_dot(g, b, "nt", passes), _raw_dot(a, g, "tn", passes)
    elif mode == "nt":
        da, db = _raw_dot(g, b, "nn", passes), _raw_dot(g, a, "tn", passes)
    else:
        da, db = _raw_dot(b, g, "nt", passes), _raw_dot(a, g, "nn", passes)
    return da.astype(a.dtype), db.astype(b.dtype)


pdot.defvjp(lambda a, b, mode, passes: (_raw_dot(a, b, mode, passes), (a, b)), _pdot_bwd)


def bdot(a, b):
    return pdot(a, b, "nn", 1)


def bdot_nt(a, b):
    return pdot(a, b, "nt", 1)


def _shift_rows(x, k):
    row = lax.broadcasted_iota(jnp.int32, x.shape, 0)
    return jnp.where(row < k, 0.0, pltpu.roll(x, k, axis=0))


def _unshift_rows(g, k):
    n = g.shape[0]
    row = lax.broadcasted_iota(jnp.int32, g.shape, 0)
    return jnp.where(row >= n - k, 0.0, pltpu.roll(g, n - k, axis=0))


@functools.partial(jax.custom_vjp, nondiff_argnums=(1,))
def tshift(x, k):
    return _shift_rows(x, k)


tshift.defvjp(lambda x, k: (_shift_rows(x, k), None), lambda k, _, g: (_unshift_rows(g, k),))


def rms(x, w):
    x = x.astype(f32)
    return x * lax.rsqrt(jnp.mean(x * x, axis=-1, keepdims=True) + RMS_EPS) * w


def softplus(x):
    return jnp.maximum(x, 0.0) + jnp.log(1.0 + jnp.exp(-jnp.abs(x)))


def seg2sum(x):
    lo = lax.broadcasted_iota(jnp.int32, x.shape, 1) < RW_HEAD
    s_lo = jnp.sum(jnp.where(lo, x, 0.0), axis=-1, keepdims=True)
    s_hi = jnp.sum(jnp.where(lo, 0.0, x), axis=-1, keepdims=True)
    return jnp.where(lo, s_lo, s_hi)


def _tile(n, pref):
    if n <= pref:
        return n
    t = pref
    while t >= LANES:
        if n % t == 0:
            return t
        t -= LANES
    return n


def mm(name, a, b, *, ta=False, tb=False, add=None, out_dtype=f32, by_chip=False, epilogue=None, extra=(),
       out_dtypes=None, after=None):
    (k, m) = a.shape if ta else a.shape[::-1]
    (n, kb) = b.shape if tb else b.shape[::-1]
    assert k == kb, (name, a.shape, b.shape)
    tm, tk = _tile(m, MM_TILE), _tile(k, MM_TILE_K)
    if k > MM_TILE_K:
        tm, tk = _tile(m, MM_TILE // 2), _tile(k, 2 * MM_TILE_K)
    tn = _tile(n // 4, MM_TILE) if by_chip else _tile(n, MM_TILE)
    nk = k // tk
    dims = (((0,) if ta else (1,), (1,) if tb else (0,)), ((), ()))
    extra = list(extra) + ([] if add is None else [add])
    out_dtypes = [out_dtype] if out_dtypes is None else list(out_dtypes)
    n_extra, n_out = len(extra), len(out_dtypes)
    n_after = 0 if after is None else 1

    def body(*refs):
        a_ref, b_ref = refs[:2]
        x_refs = refs[2:2 + n_extra]
        o_refs = refs[2 + n_extra + n_after:2 + n_extra + n_after + n_out]
        part = lax.dot_general(a_ref[...].astype(bf16), b_ref[...].astype(bf16), dims, preferred_element_type=f32)

        def finish(r):
            xs = [x[...] for x in x_refs]
            if add is not None:
                r = r + xs.pop().astype(f32)
            outs = (r,) if epilogue is None else epilogue(r, *xs)
            for o_ref, o in zip(o_refs, outs):
                o_ref[...] = o.astype(o_ref.dtype)

        if nk == 1:
            finish(part)
            return
        acc = refs[-1]
        kk = pl.program_id(2)

        @pl.when(kk == 0)
        def _():
            acc[...] = part

        @pl.when(kk > 0)
        def _():
            acc[...] += part

        @pl.when(kk == nk - 1)
        def _():
            finish(acc[...])

    a_spec = pl.BlockSpec((tk, tm), lambda i, j, q: (q, i)) if ta else pl.BlockSpec((tm, tk), lambda i, j, q: (i, q))
    b_spec = pl.BlockSpec((tn, tk), lambda i, j, q: (j, q)) if tb else pl.BlockSpec((tk, tn), lambda i, j, q: (q, j))
    x_spec = pl.BlockSpec((tm, tn), lambda i, j, q: (i, j))
    if by_chip:
        per_chip = n // 4 // tn
        o_spec = pl.BlockSpec((None, tm, tn), lambda i, j, q: (j // per_chip, i, j % per_chip))
        o_shape = (4, m, n // 4)
    else:
        o_spec, o_shape = x_spec, (m, n)
    afters = [] if after is None else [jnp.reshape(after, (1, 1))]
    res = pl.pallas_call(
        body, name=name, grid=(m // tm, n // tn, nk),
        in_specs=[a_spec, b_spec] + [x_spec] * n_extra + [pl.BlockSpec(memory_space=pl.ANY)] * n_after,
        out_specs=[o_spec] * n_out, out_shape=[jax.ShapeDtypeStruct(o_shape, dt) for dt in out_dtypes],
        scratch_shapes=[pltpu.VMEM((tm, tn), f32)] if nk > 1 else [],
        compiler_params=_params(3))(a, b, *extra, *afters)
    return res[0] if n_out == 1 else res


REGROUP_PIECES = 4


REGROUP_SUB = 5
REGROUP_FIELDS = 5


def _regroup_table(n_out, sources_of):
    import numpy as np
    tbl = np.zeros((REGROUP_FIELDS * REGROUP_PIECES, n_out), np.int32)
    for j in range(n_out):
        groups = sorted(sources_of(j).items())
        assert len(groups) <= REGROUP_PIECES, (j, len(groups))
        for p in range(REGROUP_PIECES):
            if p < len(groups):
                key, lanes = groups[p]
                shifts = {q - s for s, q in lanes}
                qs = sorted(q for _, q in lanes)
                assert len(shifts) == 1 and qs == list(range(qs[0], qs[-1] + 1)), (j, key)
                row = (key[0], key[1], shifts.pop(), qs[0], qs[-1] + 1)
            else:
                row = (0, 0, 0, 0, 0)
            tbl[REGROUP_FIELDS * p:REGROUP_FIELDS * (p + 1), j] = row
    return jnp.asarray(tbl)


def lane_regroup(name, src, table, src_spec, out_spec, out_shape, grid, src_width, sub):
    rows = src.shape[-2]
    n_src = REGROUP_PIECES * sub

    def body(tbl, *refs):
        o_ref = refs[n_src]
        step = pl.program_id(1)

        def moved(b, p):
            j = step * sub + b
            blk, shift, lo, hi = (tbl[REGROUP_FIELDS * p + f, j] for f in range(1, REGROUP_FIELDS))
            x = refs[b * REGROUP_PIECES + p][...]
            if src_width % LANES:
                lane = lax.broadcasted_iota(jnp.int32, (rows, LANES), 1)
                x = jnp.where(lane < src_width - blk * LANES, x, jnp.zeros((), src.dtype))
            pi = lax.broadcasted_iota(jnp.int32, (LANES, LANES), 0)
            qi = lax.broadcasted_iota(jnp.int32, (LANES, LANES), 1)
            sel = jnp.logical_and(qi - pi == shift, jnp.logical_and(qi >= lo, qi < hi))
            return jnp.dot(x, sel.astype(src.dtype), preferred_element_type=f32).astype(o_ref.dtype)

        for b in range(sub):
            lanes = slice(b * LANES, (b + 1) * LANES)
            o_ref[:, lanes] = moved(b, 0)
            for p in range(1, REGROUP_PIECES):
                j = step * sub + b

                @pl.when(tbl[REGROUP_FIELDS * p + 4, j] > tbl[REGROUP_FIELDS * p + 3, j])
                def _(b=b, p=p, lanes=lanes):
                    o_ref[:, lanes] += moved(b, p)

    return pl.pallas_call(
        body, name=name, out_shape=jax.ShapeDtypeStruct(out_shape, src.dtype),
        grid_spec=pltpu.PrefetchScalarGridSpec(
            num_scalar_prefetch=1, grid=grid,
            in_specs=[src_spec(p, b) for b in range(sub) for p in range(REGROUP_PIECES)], out_specs=out_spec),
        compiler_params=_params(2))(table, *[src] * n_src)


def w_in_layout(d_cols, shard_w, c_split, gap, n_blocks):
    def ext_of(c):
        return c if c < c_split else c + gap

    def fwd_sources(j):
        groups = {}
        for q in range(LANES):
            e = j * LANES + q
            c = e if e < c_split else e - gap
            if (c_split <= e < c_split + gap) or c >= d_cols:
                continue
            s, l = divmod(c, shard_w)
            groups.setdefault((s, l // LANES), []).append((l % LANES, q))
        return groups

    per_shard = -(-shard_w // LANES)

    def bwd_sources(j):
        s, b = divmod(j, per_shard)
        groups = {}
        for q in range(LANES):
            l = b * LANES + q
            if l >= shard_w:
                continue
            e = ext_of(s * shard_w + l)
            groups.setdefault((0, e // LANES), []).append((e % LANES, q))
        return groups

    return _regroup_table(n_blocks, fwd_sources), _regroup_table(4 * per_shard, bwd_sources), per_shard


def _in_spec(a, kind, tile):
    if kind == "row":
        return pl.BlockSpec((tile, a.shape[1]), lambda j, i: (i, 0))
    if kind == "const":
        return pl.BlockSpec(a.shape, lambda j, i: (0, 0))
    tag, cw, off = kind
    if tag == "rowc":
        return pl.BlockSpec((tile, cw), lambda j, i: (i, j + off))
    assert tag == "constc", kind
    return pl.BlockSpec((a.shape[0], cw), lambda j, i: (0, j + off))


def rowcall(name, fn, ins, outs, *, rows, tile=None, ncol=1):
    if tile is None:
        tile = min(TOK_TILE_BLOCKED if ncol > 1 else TOK_TILE, rows)
    n_in = len(ins)
    kinds = [o[3] for o in outs]

    def body(*refs):
        j, i = pl.program_id(0), pl.program_id(1)
        res = fn(*[r[...] for r in refs[:n_in]])
        for ref, val, kind in zip(refs[n_in:], res, kinds):
            if kind in ("row", "rowc"):
                ref[...] = val.astype(ref.dtype)
            else:
                first = (i == 0) if kind == "accc" else jnp.logical_and(i == 0, j == 0)

                @pl.when(first)
                def _(ref=ref, val=val):
                    ref[...] = val.astype(ref.dtype)

                @pl.when(jnp.logical_not(first))
                def _(ref=ref, val=val):
                    ref[...] += val.astype(ref.dtype)

    out_shape, out_specs = [], []
    for nr, nc, dtype, kind in outs:
        out_shape.append(jax.ShapeDtypeStruct((nr, nc), dtype))
        if kind == "row":
            out_specs.append(pl.BlockSpec((tile, nc), lambda j, i: (i, 0)))
        elif kind == "rowc":
            out_specs.append(pl.BlockSpec((tile, nc // ncol), lambda j, i: (i, j)))
        elif kind == "acc":
            out_specs.append(pl.BlockSpec((nr, nc), lambda j, i: (0, 0)))
        else:
            out_specs.append(pl.BlockSpec((nr, nc // ncol), lambda j, i: (0, j)))
    return pl.pallas_call(
        body, name=name, grid=(ncol, rows // tile), in_specs=[_in_spec(a, k, tile) for a, k in ins],
        out_specs=out_specs, out_shape=out_shape, compiler_params=_params(2))(*[a for a, _ in ins])


def rowvjp(name, fn, ins, cts, grads, *, rows, tile=None, ncol=1, adds=(), dup16=()):
    n_in = len(ins)
    ct_sizes = [len(c) for c in cts]
    flat_cts = [m for c in cts for m in c]
    n_ct = len(flat_cts)

    def wrapped(*vals):
        xs = list(vals[:n_in])
        gs = vals[n_in:n_in + n_ct]
        extra = vals[n_in + n_ct:]

        def f(*dvars):
            full = list(xs)
            for k, v in zip(grads, dvars):
                full[k] = v
            return fn(*full)

        outs, pull = jax.vjp(f, *[xs[k] for k in grads])
        cot, p = [], 0
        for o, size in zip(outs, ct_sizes):
            g = gs[p].astype(f32)
            for q in range(1, size):
                g = g + gs[p + q].astype(f32)
            cot.append(g.astype(o.dtype))
            p += size
        gv = list(pull(tuple(cot)))
        for (pos, _, _), e in zip(adds, extra):
            gv[pos] = gv[pos] + e.astype(gv[pos].dtype)
        return tuple(gv) + tuple(gv[pos] for pos in dup16)

    outs = []
    for k in grads:
        a, kind = ins[k]
        if kind == "row":
            outs.append((rows, a.shape[1] * ncol, f32, "rowc") if ncol > 1 else (rows, a.shape[1], f32, "row"))
        elif kind == "const":
            outs.append((a.shape[0], a.shape[1], f32, "acc"))
        elif kind[0] == "rowc":
            outs.append((rows, kind[1] * ncol, f32, "rowc"))
        else:
            outs.append((a.shape[0], kind[1] * ncol, f32, "accc"))
    for pos in dup16:
        nr, nc, _, kind = outs[pos]
        outs.append((nr, nc, bf16, kind))
    all_ins = list(ins) + flat_cts + [(a, kind) for _, a, kind in adds]
    return rowcall(name, wrapped, all_ins, outs, rows=rows, tile=tile, ncol=ncol)


def fn_rms(x, w):
    return (rms(x, w),)


def make_fn_gconv(n_norm_blocks):
    def fn(p, cw):
        c = cw[3:4] * p
        for jj in range(3):
            c = c + cw[jj:jj + 1] * tshift(p, 3 - jj)
        s = c * jax.nn.sigmoid(c)
        nrm = s * lax.rsqrt(jnp.sum(s * s, axis=-1, keepdims=True) + 1e-6)
        return (jnp.where(pl.program_id(0) < n_norm_blocks, nrm, s),)
    return fn


def fn_ggate(ps0, alog, dtb, e_g, e_b):
    g = -jnp.exp(alog) * softplus(ps0 + dtb)
    beta = jax.nn.sigmoid(ps0)
    return pdot(g, e_g, "nn", 3), pdot(beta, e_b, "nn", 3)


def fn_gpost(o, z, nw):
    return (rms(o, nw) * (z * jax.nn.sigmoid(z)),)


def fn_lerp(p, mu):
    return (p + (tshift(p, 1) - p) * mu,)


def fn_rprep(pk, psl, w0, a0, k_k, k_a, w2p, a2p, g2):
    g1, g2in = psl[:, :LANES], psl[:, LANES:]
    log_w = -softplus(-(w0 + bdot(jnp.tanh(g1), w2p))) - 0.5
    lw = -jnp.exp(log_w)
    a = jax.nn.sigmoid(a0 + bdot(g1, a2p))
    gate = bdot(jax.nn.sigmoid(g2in), g2)
    kkr = pk * k_k
    kk = kkr / jnp.maximum(jnp.sqrt(seg2sum(kkr * kkr)), 1e-12)
    kmod = pk * (1.0 + (a - 1.0) * k_a)
    return lw, kmod, kk, a, gate


def fn_rpost(y, r, kmod, v, gate, r_k, ln_w, ln_b):
    inv_n = 1.0 / RW_HEAD
    mean = seg2sum(y) * inv_n
    d = y - mean
    var = seg2sum(d * d) * inv_n
    yn = d * lax.rsqrt(var + RW_GN_EPS) * ln_w + ln_b
    bonus = seg2sum(r * kmod * r_k) * v
    return ((yn + bonus) * gate,)


def fn_xcore(q, k, v):
    s = bdot_nt(q, k) * (LANES ** -0.5)
    p = jax.nn.softmax(s, axis=-1)
    return (bdot(p, v),)


def fn_final(h, tgt, w):
    def loss_fn(h, w):
        return 0.5 * jnp.sum(jnp.mean(jnp.square(rms(h, w) - tgt), axis=-1))

    val, (dh, dw) = jax.value_and_grad(loss_fn, argnums=(0, 1))(h, w)
    return dh, dh.astype(bf16), dw, jnp.full((8, LANES), val, f32)


def fn_sumcols(n):
    def fn(x):
        w = x.shape[1] // n
        s = x[:, :w]
        for q in range(1, n):
            s = s + x[:, q * w:(q + 1) * w]
        return (s,)
    return fn


def _tri(c):
    ii = lax.broadcasted_iota(jnp.int32, (c, c), 0)
    jj = lax.broadcasted_iota(jnp.int32, (c, c), 1)
    return ii, jj


def _neumann_raw(m, steps):
    c = m.shape[-1]
    ii, jj = _tri(c)
    eye = (ii == jj).astype(f32)
    t, p = eye + m, m
    for _ in range(steps):
        p = _raw_dot(p, p, "nn", P_INV)
        t = _raw_dot(t, eye + p, "nn", P_INV)
    resid = eye - t + _raw_dot(m, t, "nn", P_RESID)
    return t + _raw_dot(t, resid, "nn", P_INV)


@functools.partial(jax.custom_vjp, nondiff_argnums=(1,))
def _neumann_inverse(m, steps):
    return _neumann_raw(m, steps)


def _neumann_fwd(m, steps):
    t = _neumann_raw(m, steps)
    return t, t


def _neumann_bwd(steps, t, g):
    return (_raw_dot(_raw_dot(t, g, "tn", P_RESID), t, "nt", P_RESID),)


_neumann_inverse.defvjp(_neumann_fwd, _neumann_bwd)


@jax.custom_vjp
def _known_inverse(m, t):
    return t


_known_inverse.defvjp(lambda m, t: (t, t), lambda t, g: (_neumann_bwd(0, t, g)[0], jnp.zeros_like(t)))


def _inverse(m, steps, kept):
    return _neumann_inverse(m, steps) if kept is None else _known_inverse(m, kept)


def cumsum_rows(x):
    t = x.shape[1]
    ii, jj = _tri(t)
    tri = jnp.broadcast_to((ii >= jj).astype(f32), (x.shape[0], t, t))
    return pdot(tri, x, "nn", P_CUMSUM)


def gdn_chunk(q, k, v, gb, bb, s, kept=None):
    c = q.shape[1]
    ii, jj = _tri(c)
    low = ii >= jj
    gcb = cumsum_rows(gb)
    gl = jnp.sum(gb, axis=1, keepdims=True)
    gc_col = gcb[:, :, :c]
    diff = gc_col - jnp.swapaxes(gc_col, 1, 2)
    decay = jnp.where(low, jnp.exp(jnp.where(low, diff, 0.0)), 0.0)
    qs = q * (q.shape[2] ** -0.5)
    kb = k * bb
    with_k = pdot(jnp.concatenate([kb, qs], axis=1), k, "nt", P_BULK)
    a = jnp.where(ii > jj, with_k[:, :c] * decay, 0.0)
    attn = with_k[:, c:] * decay
    t = _inverse(-a, c.bit_length() - 2, kept)
    eg = jnp.exp(gcb)
    uw = pdot(t, jnp.concatenate([v * bb, kb * eg], axis=2), "nn", P_BULK)
    u, w = uw[:, :, :LANES], uw[:, :, LANES:]
    kd = k * jnp.exp(gl - gcb)
    from_state = pdot(jnp.concatenate([w, qs * eg], axis=1), s, "nn", P_BULK)
    v_new = u - from_state[:, :c]
    o = from_state[:, c:] + pdot(attn, v_new, "nn", P_BULK)
    s_new = s * jnp.exp(gl) + pdot(kd, v_new, "tn", P_BULK)
    return o, s_new, t


def wkv_chunk(r, lw, k, v, kk, a, s, kept=None):
    t = r.shape[1]
    ii, jj = _tri(t)
    lo = lax.broadcasted_iota(jnp.int32, r.shape, 2) < RW_HEAD
    cl = cumsum_rows(lw)
    cl_last = jnp.sum(lw, axis=1, keepdims=True)
    al = -kk * jnp.exp(cl - lw)
    be = (a * kk) * jnp.exp(-cl)
    kt = k * jnp.exp(-cl)
    rt = r * jnp.exp(cl)

    def dot(xa, xb, mode="nn"):
        return pdot(xa, xb, mode, P_BULK)

    def sel(x_lo, x_hi):
        return jnp.where(lo, x_lo, x_hi)

    left = jnp.concatenate([jnp.where(lo, al, 0.0), jnp.where(lo, 0.0, al),
                            jnp.where(lo, rt, 0.0), jnp.where(lo, 0.0, rt)], axis=1)
    with_be, with_kt = dot(left, be, "nt"), dot(left, kt, "nt")

    def blocks(prod):
        below, upto = ii > jj, ii >= jj
        return [jnp.where(msk, prod[:, q * t:(q + 1) * t], 0.0) for q, msk in enumerate((below, below, upto, upto))]

    ab_lo, ab_hi, rb_lo, rb_hi = blocks(with_be)
    ak_lo, ak_hi, rk_lo, rk_hi = blocks(with_kt)
    from_state = dot(jnp.concatenate([al, rt], axis=1), s, "nt")
    x = from_state[:, :t] + sel(dot(ak_lo, v), dot(ak_hi, v))
    steps = t.bit_length() - 2
    inv_lo = _inverse(ab_lo, steps, None if kept is None else kept[:, :t])
    inv_hi = _inverse(ab_hi, steps, None if kept is None else kept[:, t:])
    u = sel(dot(inv_lo, x), dot(inv_hi, x))
    y = from_state[:, t:] + sel(dot(rb_lo, u) + dot(rk_lo, v), dot(rb_hi, u) + dot(rk_hi, v))
    vi = lax.broadcasted_iota(jnp.int32, s.shape, 1) < RW_HEAD
    ki = lax.broadcasted_iota(jnp.int32, s.shape, 2) < RW_HEAD
    s_new = jnp.where(vi == ki, (s + dot(u, be, "tn") + dot(v, kt, "tn")) * jnp.exp(cl_last), 0.0)
    return y, s_new, jnp.concatenate([inv_lo, inv_hi], axis=1)


def _scan_group(ncol, offs):
    g = SCAN_GROUP
    while g > 1 and (ncol % g or any(o % g for o in offs)):
        g //= 2
    return g


def scan_fwd(name, chunk_fn, ins, *, rows, chunk, ncol):
    n = rows // chunk
    n_in = len(ins)
    grp = _scan_group(ncol, [off for _, off in ins])
    tile = jax.ShapeDtypeStruct((grp, chunk, LANES), f32)
    kept = jax.eval_shape(chunk_fn, *[tile] * n_in, jax.ShapeDtypeStruct((grp, LANES, LANES), f32))[2].shape[1:]

    def body(*refs):
        o_ref, st_ref, kept_ref, s_scr = refs[n_in:]

        @pl.when(pl.program_id(1) == 0)
        def _():
            s_scr[...] = jnp.zeros_like(s_scr)

        cols = [slice(b * LANES, (b + 1) * LANES) for b in range(grp)]
        s = s_scr[...]
        st_ref[...] = s
        o, s_new, inv = chunk_fn(*[jnp.stack([r[:, c] for c in cols]) for r in refs[:n_in]], s)
        for b, c in enumerate(cols):
            o_ref[:, c] = o[b]
        kept_ref[...] = inv
        s_scr[...] = s_new

    def spec(off):
        return pl.BlockSpec((chunk, grp * LANES), lambda h, c: (c, h + off // grp))

    def per_chunk(shape):
        return pl.BlockSpec((grp, None) + tuple(shape), lambda h, c: (h, c, 0, 0))

    return pl.pallas_call(
        body, name=name, grid=(ncol // grp, n), in_specs=[spec(off) for _, off in ins],
        out_specs=[spec(0), per_chunk((LANES, LANES)), per_chunk(kept)],
        out_shape=[jax.ShapeDtypeStruct((rows, ncol * LANES), f32),
                   jax.ShapeDtypeStruct((ncol, n, LANES, LANES), f32),
                   jax.ShapeDtypeStruct((ncol, n) + tuple(kept), f32)],
        scratch_shapes=[pltpu.VMEM((grp, LANES, LANES), f32)], compiler_params=_params(2))(*[a for a, _ in ins])


def scan_bwd(name, chunk_fn, ins, states, kept, d_out, d_off, *, rows, chunk, ncol):
    n = rows // chunk
    n_in = len(ins)
    grp = _scan_group(ncol, [off for _, off in ins] + [d_off])

    def body(*refs):
        st_ref, kept_ref, do_ref = refs[n_in:n_in + 3]
        g_refs = refs[n_in + 3:2 * n_in + 3]
        ds_scr = refs[-1]

        @pl.when(pl.program_id(1) == 0)
        def _():
            ds_scr[...] = jnp.zeros_like(ds_scr)

        cols = [slice(b * LANES, (b + 1) * LANES) for b in range(grp)]

        def batch(ref):
            return jnp.stack([ref[:, c] for c in cols])

        inv = kept_ref[...]
        _, pull = jax.vjp(lambda *a: chunk_fn(*a, kept=inv)[:2], *[batch(r) for r in refs[:n_in]], st_ref[...])
        gs = pull((batch(do_ref), ds_scr[...]))
        for ref, g in zip(g_refs, gs[:n_in]):
            for b, c in enumerate(cols):
                ref[:, c] = g[b]
        ds_scr[...] = gs[n_in]

    def spec(off):
        return pl.BlockSpec((chunk, grp * LANES), lambda h, c: (n - 1 - c, h + off // grp))

    def per_chunk(shape):
        return pl.BlockSpec((grp, None) + tuple(shape), lambda h, c: (h, n - 1 - c, 0, 0))

    return pl.pallas_call(
        body, name=name, grid=(ncol // grp, n),
        in_specs=[spec(off) for _, off in ins] + [per_chunk(states.shape[2:]), per_chunk(kept.shape[2:]), spec(d_off)],
        out_specs=[spec(0)] * n_in, out_shape=[jax.ShapeDtypeStruct((rows, ncol * LANES), f32)] * n_in,
        scratch_shapes=[pltpu.VMEM((grp, LANES, LANES), f32)],
        compiler_params=_params(2))(*[a for a, _ in ins], states, kept, d_out)


def flip_exchange(name, arrs, flips, n_slots, slot_of, src_of, with_self, after=()):
    n = len(arrs)
    nf = len(flips)
    n_after = len(after)

    def body(*refs):
        ins, outs = refs[:n], refs[n + n_after:2 * n + n_after]
        send, recv, lsem = refs[2 * n + n_after:]
        me = (lax.axis_index("x"), lax.axis_index("y"), lax.axis_index("c"))
        copies = []
        for k in range(n):
            if with_self:
                cp = pltpu.make_async_copy(src_of(ins[k], me), outs[k].at[slot_of(me)], lsem.at[k])
                cp.start()
                copies.append(cp)
            for j, fl in enumerate(flips):
                peer = tuple(1 - m if f else m for m, f in zip(me, fl))
                cp = pltpu.make_async_remote_copy(
                    src_ref=src_of(ins[k], peer), dst_ref=outs[k].at[slot_of(me)], send_sem=send.at[k, j],
                    recv_sem=recv.at[k, j], device_id=peer, device_id_type=MESH)
                cp.start()
                copies.append(cp)
        for cp in copies:
            cp.wait()

    def out_sds(a):
        blk = src_of(jax.ShapeDtypeStruct(a.shape, a.dtype), None)
        return jax.ShapeDtypeStruct((n_slots,) + tuple(blk), a.dtype)

    any_spec = pl.BlockSpec(memory_space=pl.ANY)
    return pl.pallas_call(
        body, name=name, in_specs=[any_spec] * (n + n_after), out_specs=[any_spec] * n,
        out_shape=[out_sds(a) for a in arrs],
        scratch_shapes=[pltpu.SemaphoreType.DMA((n, nf)), pltpu.SemaphoreType.DMA((n, nf)),
                        pltpu.SemaphoreType.DMA((n,))],
        compiler_params=pltpu.CompilerParams(has_side_effects=True))(*arrs, *after)


_CHIP_FLIPS = ((1, 0, 0), (0, 1, 0), (1, 1, 0))
_ALL_FLIPS = ((0, 0, 1), (0, 1, 0), (0, 1, 1), (1, 0, 0), (1, 0, 1), (1, 1, 0), (1, 1, 1))


def _whole(ref, pos):
    return ref.shape if pos is None else ref


def _chip_block(ref, pos):
    return ref.shape[1:] if pos is None else ref.at[2 * pos[0] + pos[1]]


def _chip_slot(p):
    return 2 * p[0] + p[1]


_HBM = pl.BlockSpec(memory_space=pltpu.HBM)
_SEM = pl.BlockSpec(memory_space=pltpu.SEMAPHORE)
_DATAFLOW = pltpu.SideEffectType.DATAFLOW_SIDE_EFFECTING


def _split_copies(mode, refs, n, send, recv):
    me = (lax.axis_index("x"), lax.axis_index("y"), lax.axis_index("c"))
    sib = (me[0], me[1], 1 - me[2])
    lands = refs[:n] if mode == "handover" else refs[n:2 * n]
    copies = []
    for k, land in enumerate(lands):
        half = land.shape[1] // 2
        mine = pl.ds(pl.multiple_of(me[2] * half, 16), half)
        for j, fl in enumerate(_CHIP_FLIPS):
            peer = tuple(1 - m if f else m for m, f in zip(me, fl))
            if mode == "gather":
                src, dst, to = refs[k], land.at[_chip_slot(me)], peer
            elif mode == "scatter":
                src, dst, to = refs[k].at[_chip_slot(peer)], land.at[_chip_slot(me)], peer
            elif mode == "gather_half":
                src, dst, to = refs[k].at[mine], land.at[_chip_slot(me), mine], peer
            else:
                src = dst = land.at[_chip_slot(peer), mine]
                to = sib
            q = k * len(_CHIP_FLIPS) + j
            copies.append(pltpu.make_async_remote_copy(src_ref=src, dst_ref=dst, send_sem=send.at[q],
                                                       recv_sem=recv.at[q], device_id=to, device_id_type=MESH))
    return copies


def split_start(name, mode, ops, n):
    ops = [pltpu.with_memory_space_constraint(a, pltpu.HBM) for a in ops]
    m = len(ops)

    def body(*refs):
        for cp in _split_copies(mode, refs[:m], n, refs[m], refs[m + 1]):
            cp.start()
        refs[-1][...] = jnp.zeros_like(refs[-1])

    sems = pltpu.SemaphoreType.DMA((n * len(_CHIP_FLIPS),))
    outs = pl.pallas_call(
        body, name=name, in_specs=[_HBM] * m,
        out_shape=(sems, sems, *[pltpu.HBM(a.shape, a.dtype) for a in ops], jax.ShapeDtypeStruct((8, LANES), f32)),
        out_specs=(_SEM, _SEM, *[_HBM] * m, pl.BlockSpec(memory_space=pltpu.VMEM)),
        input_output_aliases={i: 2 + i for i in range(m)},
        compiler_params=pltpu.CompilerParams(has_side_effects=_DATAFLOW))(*ops)
    return (outs[0], outs[1], list(outs[2:2 + m]), mode, n), outs[-1][0, 0]


def split_wait(name, state, after):
    send, recv, ops, mode, n = state
    m = len(ops)
    afters = list(after) if isinstance(after, (list, tuple)) else [after]

    def body(*refs):
        for cp in _split_copies(mode, refs[:m], n, refs[m], refs[m + 1]):
            cp.wait_send()
            cp.wait_recv()

    outs = pl.pallas_call(
        body, name=name, in_specs=[_HBM] * m + [_SEM, _SEM] + [pl.BlockSpec(memory_space=pl.ANY)] * len(afters),
        out_shape=tuple(pltpu.HBM(a.shape, a.dtype) for a in ops), out_specs=tuple([_HBM] * m),
        input_output_aliases={i: i for i in range(m)},
        compiler_params=pltpu.CompilerParams(has_side_effects=_DATAFLOW))(*ops, send, recv, *afters)
    return list(outs[m - n:])


def chips_start(name, arrs, src_of, halves=False):
    me = _chip_slot((lax.axis_index("x"), lax.axis_index("y")))
    lands = []
    for a in arrs:
        blk = tuple(src_of(jax.ShapeDtypeStruct(a.shape, a.dtype), None))
        own = a if src_of is _whole else lax.dynamic_index_in_dim(a, me, 0, keepdims=False)
        lands.append(lax.dynamic_update_index_in_dim(lax.empty((4,) + blk, a.dtype), own, me, 0))
    mode = "scatter" if src_of is _chip_block else ("gather_half" if halves else "gather")
    return split_start(name, mode, list(arrs) + lands, len(arrs))


chips_wait = split_wait


def handover_start(name, lands):
    return split_start(name, "handover", lands, len(lands))


def gather_chips_halves(name, arrs):
    n = len(arrs)
    nf = len(_CHIP_FLIPS)
    split = [a.shape[0] % 32 == 0 for a in arrs]

    def body(*refs):
        ins, outs = refs[:n], refs[n:2 * n]
        send1, recv1, send2, recv2, lsem = refs[2 * n:]
        me = (lax.axis_index("x"), lax.axis_index("y"), lax.axis_index("c"))
        sib = (me[0], me[1], 1 - me[2])
        peers = [tuple(1 - m if f else m for m, f in zip(me, fl)) for fl in _CHIP_FLIPS]
        local, first, second = [], [], []
        for k in range(n):
            cp = pltpu.make_async_copy(ins[k], outs[k].at[_chip_slot(me)], lsem.at[k])
            cp.start()
            local.append(cp)
            half = ins[k].shape[0] // 2
            rows = pl.ds(pl.multiple_of(me[2] * half, 16), half) if split[k] else pl.ds(0, ins[k].shape[0])
            for j, peer in enumerate(peers):
                cp = pltpu.make_async_remote_copy(
                    src_ref=ins[k].at[rows], dst_ref=outs[k].at[_chip_slot(me), rows], send_sem=send1.at[k, j],
                    recv_sem=recv1.at[k, j], device_id=peer, device_id_type=MESH)
                cp.start()
                first.append((k, j, rows, cp))
        for k, j, rows, cp in first:
            cp.wait_recv()
            if split[k]:
                got = outs[k].at[_chip_slot(peers[j]), rows]
                fwd = pltpu.make_async_remote_copy(src_ref=got, dst_ref=got, send_sem=send2.at[k, j],
                                                   recv_sem=recv2.at[k, j], device_id=sib, device_id_type=MESH)
                fwd.start()
                second.append(fwd)
        for _, _, _, cp in first:
            cp.wait_send()
        for cp in second:
            cp.wait()
        for cp in local:
            cp.wait()

    any_spec = pl.BlockSpec(memory_space=pl.ANY)
    sems = pltpu.SemaphoreType.DMA((n, nf))
    return pl.pallas_call(
        body, name=name, in_specs=[any_spec] * n, out_specs=[any_spec] * n,
        out_shape=[jax.ShapeDtypeStruct((4,) + a.shape, a.dtype) for a in arrs],
        scratch_shapes=[sems, sems, sems, sems, pltpu.SemaphoreType.DMA((n,))],
        compiler_params=pltpu.CompilerParams(has_side_effects=True))(*arrs)


def swap_sibling(name, arrs):
    outs = flip_exchange(name, arrs, ((0, 0, 1),), 1, lambda p: 0, _whole, False)
    return [o[0] for o in outs]


def gather_all(arrs, after=()):
    return flip_exchange("gather_all", arrs, _ALL_FLIPS, 8, lambda p: 4 * p[0] + 2 * p[1] + p[2], _whole, True,
                         after=after)


def _row_tile(nr, nc, n_arrays):
    budget = (20 << 20) // (n_arrays * 2 * 4 * max(nc, LANES))
    t = min(nr, budget) // 16 * 16
    while t > 0 and nr % t:
        t -= 16
    return t if t > 0 else nr


def sum_slots(name, x):
    ns, nr, nc = x.shape
    tile = _row_tile(nr, nc, ns + 1)

    def body(x_ref, o_ref):
        s = x_ref[0].astype(f32)
        for q in range(1, ns):
            s = s + x_ref[q].astype(f32)
        o_ref[...] = s.astype(o_ref.dtype)

    return pl.pallas_call(
        body, name=name, grid=(nr // tile,), in_specs=[pl.BlockSpec((ns, tile, nc), lambda i: (0, i, 0))],
        out_specs=pl.BlockSpec((tile, nc), lambda i: (i, 0)), out_shape=jax.ShapeDtypeStruct((nr, nc), x.dtype),
        compiler_params=_params(1))(x)


def adamw(name, w, g_parts, m, v):
    nr, nc = w.shape[-2:]
    n_g = len(g_parts)
    tile = _row_tile(nr, nc, 7 + n_g)

    def body(*refs):
        w_ref, m_ref, v_ref = refs[:3]
        g = refs[3][...].astype(f32)
        for r in refs[4:3 + n_g]:
            g = g + r[...].astype(f32)
        g_ref, d_ref, nm_ref, nv_ref = refs[3 + n_g:]
        nm = ADAM_B1 * m_ref[...] + (1.0 - ADAM_B1) * g
        nv = ADAM_B2 * v_ref[...] + (1.0 - ADAM_B2) * jnp.square(g)
        m_hat = nm / (1.0 - ADAM_B1 ** ADAM_STEP)
        v_hat = nv / (1.0 - ADAM_B2 ** ADAM_STEP)
        g_ref[...] = g
        d_ref[...] = -ADAM_LR * (m_hat / (jnp.sqrt(v_hat) + ADAM_EPS) + ADAM_WD * w_ref[...])
        nm_ref[...] = nm
        nv_ref[...] = nv

    spec = pl.BlockSpec((tile, nc), lambda i: (i, 0))
    spec3 = pl.BlockSpec((None, tile, nc), lambda i: (0, i, 0)) if w.ndim == 3 else spec
    return pl.pallas_call(
        body, name=name, grid=(nr // tile,), in_specs=[spec3] * 3 + [spec] * n_g, out_specs=[spec3] * 4,
        out_shape=[jax.ShapeDtypeStruct(w.shape, f32)] * 4, compiler_params=_params(1))(w, m, v, *g_parts)


def adamw_packed(w, g8, m, v):
    nr, nc = w.shape

    def body(w_ref, g_ref, m_ref, v_ref, go_ref, d_ref, nm_ref, nv_ref):
        g = g_ref[0]
        for q in range(1, 8):
            g = g + g_ref[q]
        nm = ADAM_B1 * m_ref[...] + (1.0 - ADAM_B1) * g
        nv = ADAM_B2 * v_ref[...] + (1.0 - ADAM_B2) * jnp.square(g)
        m_hat = nm / (1.0 - ADAM_B1 ** ADAM_STEP)
        v_hat = nv / (1.0 - ADAM_B2 ** ADAM_STEP)
        go_ref[...] = g
        d_ref[...] = -ADAM_LR * (m_hat / (jnp.sqrt(v_hat) + ADAM_EPS) + ADAM_WD * w_ref[...])
        nm_ref[...] = nm
        nv_ref[...] = nv

    return pl.pallas_call(body, name="adamw_packed", out_shape=[jax.ShapeDtypeStruct((nr, nc), f32)] * 4,
                          compiler_params=pltpu.CompilerParams(vmem_limit_bytes=VMEM_LIMIT))(w, g8, m, v)


def _pack(vectors):
    rows = []
    for a in vectors:
        flat = a.reshape(-1).astype(f32)
        pad = (-flat.shape[0]) % LANES
        rows.append(jnp.pad(flat, (0, pad)).reshape(-1, LANES))
    packed = jnp.concatenate(rows, axis=0)
    return jnp.pad(packed, ((0, (-packed.shape[0]) % 8), (0, 0)))


def _unpack(packed, like):
    out, r = [], 0
    for a in like:
        n = a.size
        nr = -(-n // LANES)
        out.append(packed[r:r + nr].reshape(-1)[:n].reshape(a.shape))
        r += nr
    return out


def kernel(x, mem, mix_norm_w, w_in, dn_conv_w, dn_a_log, dn_dt_bias, dn_norm_w, rw_mu, rw_w0, rw_w2, rw_a0, rw_a2, rw_g2, rw_k_k, rw_k_a, rw_r_k, rw_ln_w, rw_ln_b, w_out, xa_norm_w, mem_norm_w, xa_wq, xa_wk, xa_wv, xa_wo, ffn_norm_w, ffn_w1, ffn_w2, final_norm_w, loss_target, m_mix_norm_w, m_w_in, m_dn_conv_w, m_dn_a_log, m_dn_dt_bias, m_dn_norm_w, m_rw_mu, m_rw_w0, m_rw_w2, m_rw_a0, m_rw_a2, m_rw_g2, m_rw_k_k, m_rw_k_a, m_rw_r_k, m_rw_ln_w, m_rw_ln_b, m_w_out, m_xa_norm_w, m_mem_norm_w, m_xa_wq, m_xa_wk, m_xa_wv, m_xa_wo, m_ffn_norm_w, m_ffn_w1, m_ffn_w2, m_final_norm_w, v_mix_norm_w, v_w_in, v_dn_conv_w, v_dn_a_log, v_dn_dt_bias, v_dn_norm_w, v_rw_mu, v_rw_w0, v_rw_w2, v_rw_a0, v_rw_a2, v_rw_g2, v_rw_k_k, v_rw_k_a, v_rw_r_k, v_rw_ln_w, v_rw_ln_b, v_w_out, v_xa_norm_w, v_mem_norm_w, v_xa_wq, v_xa_wk, v_xa_wv, v_xa_wo, v_ffn_norm_w, v_ffn_w1, v_ffn_w2, v_final_norm_w):
    weights = dict(mix_norm_w=mix_norm_w, w_in=w_in, dn_conv_w=dn_conv_w, dn_a_log=dn_a_log, dn_dt_bias=dn_dt_bias, dn_norm_w=dn_norm_w, rw_mu=rw_mu, rw_w0=rw_w0, rw_w2=rw_w2, rw_a0=rw_a0, rw_a2=rw_a2, rw_g2=rw_g2, rw_k_k=rw_k_k, rw_k_a=rw_k_a, rw_r_k=rw_r_k, rw_ln_w=rw_ln_w, rw_ln_b=rw_ln_b, w_out=w_out, xa_norm_w=xa_norm_w, mem_norm_w=mem_norm_w, xa_wq=xa_wq, xa_wk=xa_wk, xa_wv=xa_wv, xa_wo=xa_wo, ffn_norm_w=ffn_norm_w, ffn_w1=ffn_w1, ffn_w2=ffn_w2, final_norm_w=final_norm_w)
    mom_m = dict(mix_norm_w=m_mix_norm_w, w_in=m_w_in, dn_conv_w=m_dn_conv_w, dn_a_log=m_dn_a_log, dn_dt_bias=m_dn_dt_bias, dn_norm_w=m_dn_norm_w, rw_mu=m_rw_mu, rw_w0=m_rw_w0, rw_w2=m_rw_w2, rw_a0=m_rw_a0, rw_a2=m_rw_a2, rw_g2=m_rw_g2, rw_k_k=m_rw_k_k, rw_k_a=m_rw_k_a, rw_r_k=m_rw_r_k, rw_ln_w=m_rw_ln_w, rw_ln_b=m_rw_ln_b, w_out=m_w_out, xa_norm_w=m_xa_norm_w, mem_norm_w=m_mem_norm_w, xa_wq=m_xa_wq, xa_wk=m_xa_wk, xa_wv=m_xa_wv, xa_wo=m_xa_wo, ffn_norm_w=m_ffn_norm_w, ffn_w1=m_ffn_w1, ffn_w2=m_ffn_w2, final_norm_w=m_final_norm_w)
    mom_v = dict(mix_norm_w=v_mix_norm_w, w_in=v_w_in, dn_conv_w=v_dn_conv_w, dn_a_log=v_dn_a_log, dn_dt_bias=v_dn_dt_bias, dn_norm_w=v_dn_norm_w, rw_mu=v_rw_mu, rw_w0=v_rw_w0, rw_w2=v_rw_w2, rw_a0=v_rw_a0, rw_a2=v_rw_a2, rw_g2=v_rw_g2, rw_k_k=v_rw_k_k, rw_k_a=v_rw_k_a, rw_r_k=v_rw_r_k, rw_ln_w=v_rw_ln_w, rw_ln_b=v_rw_ln_b, w_out=v_w_out, xa_norm_w=v_xa_norm_w, mem_norm_w=v_mem_norm_w, xa_wq=v_xa_wq, xa_wk=v_xa_wk, xa_wv=v_xa_wv, xa_wo=v_xa_wo, ffn_norm_w=v_ffn_norm_w, ffn_w1=v_ffn_w1, ffn_w2=v_ffn_w2, final_norm_w=v_final_norm_w)
    names = list(weights)

    seq, d = x.shape[1], x.shape[2]
    dnw = d // 2
    rww = d - dnw
    nh, nb = dnw // LANES, rww // LANES
    n_mem = mem.shape[1]
    lw_dim, la_dim, lg_dim = rw_w2.shape[1], rw_a2.shape[1], rw_g2.shape[1]
    assert lw_dim + la_dim == LANES and lg_dim == LANES and dnw % LANES == 0 and rww % LANES == 0
    xs, mems, tgt = x[0], mem[0], loss_target[0]

    col_sharded = ("w_in", "xa_wo", "ffn_w1", "dn_conv_w", "rw_w2", "rw_a2", "rw_g2")
    row_sharded = ("w_out", "xa_wq", "xa_wk", "xa_wv", "ffn_w2")
    f32_payload = ("dn_conv_w", "rw_w2", "rw_a2", "rw_g2")
    sharded = col_sharded + row_sharded
    payload = {n: weights[n][0].astype(f32 if n in f32_payload else bf16) for n in sharded}
    shard_w = w_in.shape[2]
    pad_w = -(-shard_w // LANES) * LANES
    first = ("w_in", "dn_conv_w", "rw_w2", "rw_a2", "rw_g2")
    mid = ("w_out", "xa_wq", "xa_wk", "xa_wv", "xa_wo")
    late = ("ffn_w1", "ffn_w2")
    gathered = dict(zip(first, gather_chips_halves("gather_first", [payload[n] for n in first])))
    ordered = lax.optimization_barrier(([gathered[n] for n in first], [payload[n] for n in mid + late]))
    gathered = dict(zip(first, ordered[0]))
    payload.update(zip(mid + late, ordered[1]))
    mid_state, tok_mid = chips_start("gather_mid_start", [payload[n] for n in mid], _whole, halves=True)
    late_state, tok_late = chips_start("gather_late_start", [payload[n] for n in late], _whole, halves=True)
    mix_norm_w_in = mix_norm_w + (tok_mid + tok_late)

    def full(n):
        g = gathered[n]
        if n in col_sharded:
            return g.transpose(1, 0, 2).reshape(g.shape[1], 4 * g.shape[2])
        return g.reshape(4 * g.shape[1], g.shape[2])

    c_rw0 = 4 * dnw + 2 * nh
    c_rw = 3 * rww
    eb_ab, eb_r = 4 * nh, 4 * nh + 1
    eb_l1 = eb_r + 3 * nb
    n_ext = -(-(eb_l1 + 2) // 4) * 4
    tbl_fwd, tbl_bwd, per_shard = w_in_layout(4 * shard_w, shard_w, c_rw0, LANES - 2 * nh, n_ext)
    sub_f = max(s for s in range(1, REGROUP_SUB + 1) if n_ext % s == 0)
    sub_b = max(s for s in range(1, REGROUP_SUB + 1) if per_shard % s == 0)
    w_ext = lane_regroup(
        "w_in_regroup", gathered["w_in"], tbl_fwd,
        lambda p, b: pl.BlockSpec(
            (None, d, LANES), lambda i, j, t: (t[REGROUP_FIELDS * p, j * sub_f + b], 0,
                                               t[REGROUP_FIELDS * p + 1, j * sub_f + b])),
        pl.BlockSpec((d, sub_f * LANES), lambda i, j, t: (0, j)), (d, n_ext * LANES), (1, n_ext // sub_f), shard_w,
        sub_f)
    conv_w = full("dn_conv_w")
    w2p = jnp.concatenate([full("rw_w2"), jnp.zeros((la_dim, rww), f32)], axis=0)
    a2p = jnp.concatenate([jnp.zeros((lw_dim, rww), f32), full("rw_a2")], axis=0)
    g2 = full("rw_g2")
    xaw = xa_wq.shape[2]
    nxh = xaw // LANES
    ffn = 4 * ffn_w1.shape[2]

    def lane_row(vec):
        return jnp.pad(vec.reshape(1, -1), ((0, 0), (0, LANES - vec.size)))

    alog_row, dtb_row = lane_row(dn_a_log), lane_row(dn_dt_bias)
    head_of_col = jnp.arange(dnw)[None, :] // LANES
    e_g = (jnp.arange(LANES)[:, None] == head_of_col).astype(f32)
    e_b = (jnp.arange(LANES)[:, None] == head_of_col + nh).astype(f32)
    mu_main, mu_small = rw_mu[:, :c_rw], rw_mu[:, c_rw:]
    r_k_row = rw_r_k.reshape(1, rww)
    fnw = final_norm_w.reshape(1, d)
    qb, kb_, vb, zb = 0, nh, 2 * nh, 3 * nh
    rb0 = eb_r
    cc = lambda w_, o_: ("constc", w_, o_)
    rc = lambda o_: ("rowc", LANES, o_)

    (u16,) = rowcall("norm_mix", fn_rms, [(xs, "row"), (mix_norm_w_in, "const")], [(seq, d, bf16, "row")], rows=seq)
    p_main = p_small = mm("in_proj", u16, w_ext)

    fn_gconv = make_fn_gconv(2 * nh)
    (qkv,) = rowcall("gdn_conv", fn_gconv, [(p_main, rc(0)), (conv_w, cc(LANES, 0))],
                     [(seq, 3 * dnw, f32, "rowc")], rows=seq, tile=seq, ncol=3 * nh)
    gate_ins = [(p_small, rc(eb_ab)), (alog_row, "const"), (dtb_row, "const"), (e_g, "const"), (e_b, "const")]
    g_b, beta_b = rowcall("gdn_gate", fn_ggate, gate_ins, [(seq, dnw, f32, "row")] * 2, rows=seq)
    gdn_ins = [(qkv, qb), (qkv, kb_), (qkv, vb), (g_b, 0), (beta_b, 0)]
    o_raw, gdn_states, gdn_kept = scan_fwd("gdn_scan", gdn_chunk, gdn_ins, rows=seq, chunk=GDN_CHUNK, ncol=nh)
    mid_state, tok = handover_start("gather_mid_pass", chips_wait("gather_mid_wait", mid_state, o_raw))
    gpost_ins = [(o_raw, rc(0)), (p_main, rc(zb)), (dn_norm_w + tok, "const")]
    (o_dn,) = rowcall("gdn_post", fn_gpost, gpost_ins, [(seq, dnw, bf16, "rowc")], rows=seq, ncol=nh)

    (prw,) = rowcall("rw_lerp_main", fn_lerp, [(p_main, rc(rb0)), (mu_main, cc(LANES, 0))],
                     [(seq, c_rw, f32, "rowc")], rows=seq, tile=seq, ncol=3 * nb)
    (psl,) = rowcall("rw_lerp_small", fn_lerp, [(p_small, rc(eb_l1)), (mu_small, cc(LANES, 0))],
                     [(seq, 2 * LANES, f32, "rowc")], rows=seq, tile=seq, ncol=2)
    rprep_ins = [(prw, rc(nb)), (psl, "row"), (rw_w0, cc(LANES, 0)), (rw_a0, cc(LANES, 0)), (rw_k_k, cc(LANES, 0)),
                 (rw_k_a, cc(LANES, 0)), (w2p, cc(LANES, 0)), (a2p, cc(LANES, 0)), (g2, cc(LANES, 0))]
    lw, kmod, kk, a_rw, gate = rowcall("rw_prep", fn_rprep, rprep_ins, [(seq, rww, f32, "rowc")] * 5,
                                        rows=seq, ncol=nb)
    wkv_ins = [(prw, 0), (lw, 0), (kmod, 0), (prw, 2 * nb), (kk, 0), (a_rw, 0)]
    y_rw, wkv_states, wkv_kept = scan_fwd("wkv_scan", wkv_chunk, wkv_ins, rows=seq, chunk=WKV_CHUNK, ncol=nb)
    rpost_ins = [(y_rw, rc(0)), (prw, rc(0)), (kmod, rc(0)), (prw, rc(2 * nb)), (gate, rc(0)),
                 (r_k_row, cc(LANES, 0)), (rw_ln_w, cc(LANES, 0)), (rw_ln_b, cc(LANES, 0))]
    (o_rw,) = rowcall("rw_post", fn_rpost, rpost_ins, [(seq, rww, bf16, "rowc")], rows=seq, ncol=nb)

    o_cat = jnp.concatenate([o_dn, o_rw], axis=1)
    late_state, tok = handover_start("gather_late_pass", chips_wait("gather_late_wait", late_state, o_cat))
    gathered.update(zip(mid, chips_wait("gather_mid_pass_wait", mid_state, o_cat)))
    w_out_f, wq_f, wk_f, wv_f, wo_f = full("w_out"), full("xa_wq"), full("xa_wk"), full("xa_wv"), full("xa_wo")
    h1 = mm("out_proj", o_cat, w_out_f, add=xs)

    (hn16,) = rowcall("norm_xa", fn_rms, [(h1, "row"), (xa_norm_w + tok, "const")], [(seq, d, bf16, "row")],
                      rows=seq)
    (mn16,) = rowcall("norm_mem", fn_rms, [(mems, "row"), (mem_norm_w, "const")], [(n_mem, d, bf16, "row")],
                      rows=n_mem)
    q_xa = mm("xa_q", hn16, wq_f)
    k_xa = mm("xa_k", mn16, wk_f)
    v_xa = mm("xa_v", mn16, wv_f)
    xcore_ins = [(q_xa, rc(0)), (k_xa, cc(LANES, 0)), (v_xa, cc(LANES, 0))]
    (o_xa,) = rowcall("xa_core", fn_xcore, xcore_ins, [(seq, xaw, bf16, "rowc")], rows=seq, ncol=nxh)
    h2 = mm("xa_o", o_xa, wo_f, add=h1)

    (fn16,) = rowcall("norm_ffn", fn_rms, [(h2, "row"), (ffn_norm_w, "const")], [(seq, d, bf16, "row")], rows=seq)
    gathered.update(zip(late, chips_wait("gather_late_pass_wait", late_state, fn16)))
    w1_f, w2_f = full("ffn_w1"), full("ffn_w2")
    a1_16, hid16 = mm("ffn_up", fn16, w1_f, epilogue=lambda r: (r, jnp.square(jnp.maximum(r, 0.0))),
                      out_dtypes=(bf16, bf16))
    h3 = mm("ffn_down", hid16, w2_f, add=h2)

    dh3, dh3_16, d_fnw, loss_rows = rowcall(
        "loss_head", fn_final, [(h3, "row"), (tgt, "row"), (fnw, "const")],
        [(seq, d, f32, "row"), (seq, d, bf16, "row"), (1, d, f32, "acc"), (8, LANES, f32, "acc")], rows=seq)

    da1_16 = mm("ffn_down_dx", dh3_16, w2_f, tb=True, extra=[a1_16], out_dtype=bf16,
                epilogue=lambda r, a1: (r * (2.0 * jnp.maximum(a1.astype(f32), 0.0)),))
    def by_chip(n, g):
        if g.ndim == 3:
            return g
        if n in col_sharded:
            return g.reshape(g.shape[0], 4, g.shape[1] // 4).transpose(1, 0, 2)
        return g.reshape(4, g.shape[0] // 4, g.shape[1])

    g_ffn_w2 = mm("ffn_down_dw", hid16, dh3_16, ta=True, out_dtype=bf16)
    ffn_w2_g, tok = chips_start("scatter_ffn_w2_start", [by_chip("ffn_w2", g_ffn_w2)], _chip_block)
    g_ffn_w1 = mm("ffn_up_dw", fn16, da1_16, ta=True, out_dtype=bf16, by_chip=True, after=tok)
    ffn_w1_g, tok = chips_start("scatter_ffn_w1_start", [g_ffn_w1], _chip_block)
    dfn = mm("ffn_up_dx", da1_16, w1_f, tb=True, after=tok)
    dh2, d_ffn_nw, dh2_16 = rowvjp("norm_ffn_bwd", fn_rms, [(h2, "row"), (ffn_norm_w, "const")],
                                   [[(dfn, "row")]], [0, 1], rows=seq, adds=[(0, dh3, "row")], dup16=[0])

    do_xa = mm("xa_o_dx", dh2_16, wo_f, tb=True)
    g_xa_wo = mm("xa_o_dw", o_xa, dh2_16, ta=True, out_dtype=bf16, by_chip=True)
    dq_xa, dk_xa, dv_xa, dq16 = rowvjp("xa_core_bwd", fn_xcore, xcore_ins, [[(do_xa, rc(0))]], [0, 1, 2],
                                       rows=seq, ncol=nxh, dup16=[0])
    g_xa_wq = mm("xa_q_dw", hn16, dq16, ta=True, out_dtype=bf16)
    dhn = mm("xa_q_dx", dq16, wq_f, tb=True)
    dh1, d_xa_nw, dh1_16 = rowvjp("norm_xa_bwd", fn_rms, [(h1, "row"), (xa_norm_w, "const")], [[(dhn, "row")]],
                                  [0, 1], rows=seq, adds=[(0, dh2, "row")], dup16=[0])
    dk16, dv16 = dk_xa.astype(bf16), dv_xa.astype(bf16)
    g_xa_wk = mm("xa_k_dw", mn16, dk16, ta=True, out_dtype=bf16)
    g_xa_wv = mm("xa_v_dw", mn16, dv16, ta=True, out_dtype=bf16)
    dmn = mm("xa_v_dx", dv16, wv_f, tb=True, add=mm("xa_k_dx", dk16, wk_f, tb=True))
    (d_mem_nw,) = rowvjp("norm_mem_bwd", fn_rms, [(mems, "row"), (mem_norm_w, "const")], [[(dmn, "row")]], [1],
                         rows=n_mem)

    g_w_out = mm("out_proj_dw", o_cat, dh1_16, ta=True, out_dtype=bf16)
    mid_grads = dict(w_out=g_w_out, xa_wq=g_xa_wq, xa_wk=g_xa_wk, xa_wv=g_xa_wv, xa_wo=g_xa_wo)
    mid_g, tok = chips_start("scatter_mid_start", [by_chip(n, mid_grads[n]) for n in mid], _chip_block)
    do_cat = mm("out_proj_dx", dh1_16, w_out_f, tb=True, after=tok)

    dy, dr_a, dkmod_a, dv_a, dgate, d_r_k, d_ln_w, d_ln_b = rowvjp(
        "rw_post_bwd", fn_rpost, rpost_ins, [[(do_cat, rc(nh))]], [0, 1, 2, 3, 4, 5, 6, 7], rows=seq, ncol=nb)
    dr_b, dlw, dkmod_b, dv_b, dkk, da_rw = scan_bwd("wkv_scan_bwd", wkv_chunk, wkv_ins, wkv_states, wkv_kept, dy, 0,
                                                    rows=seq, chunk=WKV_CHUNK, ncol=nb)
    rprep_cts = [[(dlw, rc(0))], [(dkmod_a, rc(0)), (dkmod_b, rc(0))], [(dkk, rc(0))], [(da_rw, rc(0))],
                 [(dgate, rc(0))]]
    dpk, dpsl_parts, d_w0, d_a0, d_k_k, d_k_a, d_w2p, d_a2p, d_g2 = rowvjp(
        "rw_prep_bwd", fn_rprep, rprep_ins, rprep_cts, [0, 1, 2, 3, 4, 5, 6, 7, 8], rows=seq, ncol=nb)
    (dpsl,) = rowcall("rw_prep_sum", fn_sumcols(nb), [(dpsl_parts, "row")], [(seq, 2 * LANES, f32, "row")], rows=seq)

    def lerp_bwd(tag, p, p_off, mu, mu_off, ct_lists, ncol):
        return rowvjp("rw_lerp_bwd_" + tag, fn_lerp, [(p, rc(p_off)), (mu, cc(LANES, mu_off))], [ct_lists], [0, 1],
                      rows=seq, tile=seq, ncol=ncol, dup16=[0])

    _, dmu_r, dpr16 = lerp_bwd("r", p_main, rb0, mu_main, 0, [(dr_a, rc(0)), (dr_b, rc(0))], nb)
    _, dmu_k, dpk16 = lerp_bwd("k", p_main, rb0 + nb, mu_main, nb, [(dpk, rc(0))], nb)
    _, dmu_v, dpv16 = lerp_bwd("v", p_main, rb0 + 2 * nb, mu_main, 2 * nb, [(dv_a, rc(0)), (dv_b, rc(0))], nb)
    _, dmu_s, dps12_16 = lerp_bwd("small", p_small, eb_l1, mu_small, 0, [(dpsl, rc(0))], 2)

    do_raw, dz, d_dn_nw, dz16 = rowvjp("gdn_post_bwd", fn_gpost, gpost_ins, [[(do_cat, rc(0))]], [0, 1, 2],
                                       rows=seq, ncol=nh, dup16=[1])
    dq_g, dk_g, dv_g, dg_b, dbeta_b = scan_bwd("gdn_scan_bwd", gdn_chunk, gdn_ins, gdn_states, gdn_kept, do_raw, 0,
                                               rows=seq, chunk=GDN_CHUNK, ncol=nh)
    dps0, d_alog, d_dtb, dps0_16 = rowvjp("gdn_gate_bwd", fn_ggate, gate_ins, [[(dg_b, "row")], [(dbeta_b, "row")]],
                                          [0, 1, 2], rows=seq, dup16=[0])
    dqkv = jnp.concatenate([dq_g, dk_g, dv_g], axis=1)
    _, d_conv_w, dqkv16 = rowvjp("gdn_conv_bwd", fn_gconv, [(p_main, rc(0)), (conv_w, cc(LANES, 0))],
                                 [[(dqkv, rc(0))]], [0, 1], rows=seq, tile=seq, ncol=3 * nh, dup16=[0])

    dp16 = jnp.concatenate([dqkv16, dz16, dps0_16, dpr16, dpk16, dpv16, dps12_16,
                            jnp.zeros((seq, (n_ext - eb_l1 - 2) * LANES), bf16)], axis=1)
    g_w_ext = mm("in_proj_dw", u16, dp16, ta=True, out_dtype=bf16)
    g_w_in = lane_regroup(
        "w_in_grad_regroup", g_w_ext, tbl_bwd,
        lambda p, b: pl.BlockSpec((d, LANES), lambda i, j, t: (0, t[REGROUP_FIELDS * p + 1, j * sub_b + b])),
        pl.BlockSpec((None, d, sub_b * LANES),
                     lambda i, j, t: (j // (per_shard // sub_b), 0, j % (per_shard // sub_b))),
        (4, d, pad_w), (1, 4 * per_shard // sub_b), n_ext * LANES, sub_b)
    first_grads = dict(w_in=g_w_in, dn_conv_w=d_conv_w, rw_w2=d_w2p[:lw_dim], rw_a2=d_a2p[lw_dim:], rw_g2=d_g2)
    first_g, tok = chips_start("scatter_first_start", [by_chip(n, first_grads[n]) for n in first], _chip_block)

    du = mm("in_proj_dx", dp16, w_ext, tb=True, after=tok)
    grad_x, d_mix_nw = rowvjp("norm_mix_bwd", fn_rms, [(xs, "row"), (mix_norm_w, "const")], [[(du, "row")]],
                              [0, 1], rows=seq, adds=[(0, dh1, "row")])
    received = dict(zip(mid, chips_wait("scatter_mid_wait", mid_g, grad_x)))
    received["ffn_w2"] = chips_wait("scatter_ffn_w2_wait", ffn_w2_g, grad_x)[0]
    received["ffn_w1"] = chips_wait("scatter_ffn_w1_wait", ffn_w1_g, grad_x)[0]

    out = {}

    def reduce_and_update(tag, group):
        partial = [sum_slots("sum_chips_" + n, received[n]) for n in group]
        other = swap_sibling("swap_sibling_" + tag, partial)
        for n, p_mine, p_other in zip(group, partial, other):
            if weights[n].shape[2] % LANES:
                rows, cols = weights[n].shape[1:]
                lin = lambda a: jnp.swapaxes(a, 1, 2).reshape(-1, LANES)
                lin_g = lambda p: p.T[:cols].reshape(-1, LANES)
                res = adamw("adamw_" + n, lin(weights[n]), [lin_g(p_mine), lin_g(p_other)], lin(mom_m[n]),
                            lin(mom_v[n]))
                out[n] = [jnp.swapaxes(r.reshape(1, cols, rows), 1, 2) for r in res]
            else:
                out[n] = adamw("adamw_" + n, weights[n], [p_mine, p_other], mom_m[n], mom_v[n])

    reduce_and_update("rest", mid + late)
    received.update(zip(first, chips_wait("scatter_first_wait", first_g, [out[n][1] for n in mid + late])))
    reduce_and_update("first", first)

    small_names = [n for n in names if n not in sharded]
    small_local = dict(
        mix_norm_w=d_mix_nw, dn_a_log=d_alog[:, :nh], dn_dt_bias=d_dtb[:, :nh], dn_norm_w=d_dn_nw,
        rw_mu=jnp.concatenate([dmu_r, dmu_k, dmu_v, dmu_s], axis=1), rw_w0=d_w0, rw_a0=d_a0, rw_k_k=d_k_k,
        rw_k_a=d_k_a, rw_r_k=d_r_k, rw_ln_w=d_ln_w, rw_ln_b=d_ln_b, xa_norm_w=d_xa_nw, mem_norm_w=d_mem_nw,
        ffn_norm_w=d_ffn_nw, final_norm_w=d_fnw)
    loss_vec = jnp.where(jnp.arange(LANES) == 0, loss_rows[0], 0.0)
    (g8,) = gather_all([_pack([small_local[n] for n in small_names] + [loss_vec])], after=[out["w_in"][1]])

    packed_like = [weights[n] for n in small_names] + [loss_vec]
    zero = jnp.zeros((LANES,), f32)
    res = adamw_packed(_pack([weights[n] for n in small_names] + [zero]), g8,
                       _pack([mom_m[n] for n in small_names] + [zero]),
                       _pack([mom_v[n] for n in small_names] + [zero]))
    unpacked = [_unpack(r, packed_like) for r in res]
    for i, n in enumerate(small_names):
        out[n] = [u[i] for u in unpacked]
    loss = unpacked[0][-1][0]

    return (loss, grad_x.reshape(x.shape), *[out[n][0] for n in names], *[out[n][1] for n in names],
            *[out[n][2] for n in names], *[out[n][3] for n in names])
```

```python
import functools

import jax
import jax.numpy as jnp
from jax import lax
from jax.experimental import pallas as pl
from jax.experimental.pallas import tpu as pltpu

f32 = jnp.float32
bf16 = jnp.bfloat16
HI = lax.Precision.HIGHEST
MESH = pl.DeviceIdType.MESH

LANES = 128
VMEM_LIMIT = 56 << 20
TOK_TILE = 256
TOK_TILE_BLOCKED = 1024
MM_TILE = 1024
MM_TILE_K = 2048
GDN_CHUNK = 128
WKV_CHUNK = 64
SCAN_GROUP = 8
P_BULK = 1
P_INV = 1
P_RESID = 3
P_CUMSUM = 3
RMS_EPS = 1e-6
RW_GN_EPS = 64e-5
RW_HEAD = 64

ADAM_LR, ADAM_B1, ADAM_B2, ADAM_EPS, ADAM_WD, ADAM_STEP = 0.001, 0.9, 0.999, 1e-08, 0.01, 10


def _params(n_grid):
    return pltpu.CompilerParams(dimension_semantics=("arbitrary",) * n_grid, vmem_limit_bytes=VMEM_LIMIT)


_DIMS = {"nn": (((1,), (0,)), ((), ())), "nt": (((1,), (1,)), ((), ())), "tn": (((0,), (0,)), ((), ()))}
_DIMS_BATCHED = {"nn": (((2,), (1,)), ((0,), (0,))), "nt": (((2,), (2,)), ((0,), (0,))),
                 "tn": (((1,), (1,)), ((0,), (0,)))}


def _raw_dot(a, b, mode, passes):
    dims = (_DIMS if a.ndim == 2 else _DIMS_BATCHED)[mode]
    if passes == 6:
        return lax.dot_general(a.astype(f32), b.astype(f32), dims, precision=HI, preferred_element_type=f32)
    ah, bh = a.astype(bf16), b.astype(bf16)
    r = lax.dot_general(ah, bh, dims, preferred_element_type=f32)
    if passes == 3:
        al = (a - ah.astype(f32)).astype(bf16)
        bl = (b - bh.astype(f32)).astype(bf16)
        r = r + lax.dot_general(al, bh, dims, preferred_element_type=f32)
        r = r + lax.dot_general(ah, bl, dims, preferred_element_type=f32)
    return r


@functools.partial(jax.custom_vjp, nondiff_argnums=(2, 3))
def pdot(a, b, mode, passes):
    return _raw_dot(a, b, mode, passes)


def _pdot_bwd(mode, passes, res, g):
    a, b = res
    if mode == "nn":
        da, db = _raw_dot(g, b, "nt", passes), _raw_dot(a, g, "tn", passes)
    elif mode == "nt":
        da, db = _raw_dot(g, b, "nn", passes), _raw_dot(g, a, "tn", passes)
    else:
        da, db = _raw_dot(b, g, "nt", passes), _raw_dot(a, g, "nn", passes)
    return da.astype(a.dtype), db.astype(b.dtype)


pdot.defvjp(lambda a, b, mode, passes: (_raw_dot(a, b, mode, passes), (a, b)), _pdot_bwd)


def bdot(a, b):
    return pdot(a, b, "nn", 1)


def bdot_nt(a, b):
    return pdot(a, b, "nt", 1)


def _shift_rows(x, k):
    row = lax.broadcasted_iota(jnp.int32, x.shape, 0)
    return jnp.where(row < k, 0.0, pltpu.roll(x, k, axis=0))


def _unshift_rows(g, k):
    n = g.shape[0]
    row = lax.broadcasted_iota(jnp.int32, g.shape, 0)
    return jnp.where(row >= n - k, 0.0, pltpu.roll(g, n - k, axis=0))


@functools.partial(jax.custom_vjp, nondiff_argnums=(1,))
def tshift(x, k):
    return _shift_rows(x, k)


tshift.defvjp(lambda x, k: (_shift_rows(x, k), None), lambda k, _, g: (_unshift_rows(g, k),))


def rms(x, w):
    x = x.astype(f32)
    return x * lax.rsqrt(jnp.mean(x * x, axis=-1, keepdims=True) + RMS_EPS) * w


def softplus(x):
    return jnp.maximum(x, 0.0) + jnp.log(1.0 + jnp.exp(-jnp.abs(x)))


def seg2sum(x):
    lo = lax.broadcasted_iota(jnp.int32, x.shape, 1) < RW_HEAD
    s_lo = jnp.sum(jnp.where(lo, x, 0.0), axis=-1, keepdims=True)
    s_hi = jnp.sum(jnp.where(lo, 0.0, x), axis=-1, keepdims=True)
    return jnp.where(lo, s_lo, s_hi)


def _tile(n, pref):
    if n <= pref:
        return n
    t = pref
    while t >= LANES:
        if n % t == 0:
            return t
        t -= LANES
    return n


def mm(name, a, b, *, ta=False, tb=False, add=None, out_dtype=f32, by_chip=False, epilogue=None, extra=(),
       out_dtypes=None, after=None):
    (k, m) = a.shape if ta else a.shape[::-1]
    (n, kb) = b.shape if tb else b.shape[::-1]
    assert k == kb, (name, a.shape, b.shape)
    tm, tk = _tile(m, MM_TILE), _tile(k, MM_TILE_K)
    tn = _tile(n // 4, MM_TILE) if by_chip else _tile(n, MM_TILE)
    nk = k // tk
    dims = (((0,) if ta else (1,), (1,) if tb else (0,)), ((), ()))
    extra = list(extra) + ([] if add is None else [add])
    out_dtypes = [out_dtype] if out_dtypes is None else list(out_dtypes)
    n_extra, n_out = len(extra), len(out_dtypes)
    n_after = 0 if after is None else 1

    def body(*refs):
        a_ref, b_ref = refs[:2]
        x_refs = refs[2:2 + n_extra]
        o_refs = refs[2 + n_extra + n_after:2 + n_extra + n_after + n_out]
        part = lax.dot_general(a_ref[...].astype(bf16), b_ref[...].astype(bf16), dims, preferred_element_type=f32)

        def finish(r):
            xs = [x[...] for x in x_refs]
            if add is not None:
                r = r + xs.pop().astype(f32)
            outs = (r,) if epilogue is None else epilogue(r, *xs)
            for o_ref, o in zip(o_refs, outs):
                o_ref[...] = o.astype(o_ref.dtype)

        if nk == 1:
            finish(part)
            return
        acc = refs[-1]
        kk = pl.program_id(2)

        @pl.when(kk == 0)
        def _():
            acc[...] = part

        @pl.when(kk > 0)
        def _():
            acc[...] += part

        @pl.when(kk == nk - 1)
        def _():
            finish(acc[...])

    a_spec = pl.BlockSpec((tk, tm), lambda i, j, q: (q, i)) if ta else pl.BlockSpec((tm, tk), lambda i, j, q: (i, q))
    b_spec = pl.BlockSpec((tn, tk), lambda i, j, q: (j, q)) if tb else pl.BlockSpec((tk, tn), lambda i, j, q: (q, j))
    x_spec = pl.BlockSpec((tm, tn), lambda i, j, q: (i, j))
    if by_chip:
        per_chip = n // 4 // tn
        o_spec = pl.BlockSpec((None, tm, tn), lambda i, j, q: (j // per_chip, i, j % per_chip))
        o_shape = (4, m, n // 4)
    else:
        o_spec, o_shape = x_spec, (m, n)
    afters = [] if after is None else [jnp.reshape(after, (1, 1))]
    res = pl.pallas_call(
        body, name=name, grid=(m // tm, n // tn, nk),
        in_specs=[a_spec, b_spec] + [x_spec] * n_extra + [pl.BlockSpec(memory_space=pl.ANY)] * n_after,
        out_specs=[o_spec] * n_out, out_shape=[jax.ShapeDtypeStruct(o_shape, dt) for dt in out_dtypes],
        scratch_shapes=[pltpu.VMEM((tm, tn), f32)] if nk > 1 else [],
        compiler_params=_params(3))(a, b, *extra, *afters)
    return res[0] if n_out == 1 else res


REGROUP_PIECES = 4


REGROUP_SUB = 5
REGROUP_FIELDS = 5


def _regroup_table(n_out, sources_of):
    import numpy as np
    tbl = np.zeros((REGROUP_FIELDS * REGROUP_PIECES, n_out), np.int32)
    for j in range(n_out):
        groups = sorted(sources_of(j).items())
        assert len(groups) <= REGROUP_PIECES, (j, len(groups))
        for p in range(REGROUP_PIECES):
            if p < len(groups):
                key, lanes = groups[p]
                shifts = {q - s for s, q in lanes}
                qs = sorted(q for _, q in lanes)
                assert len(shifts) == 1 and qs == list(range(qs[0], qs[-1] + 1)), (j, key)
                row = (key[0], key[1], shifts.pop(), qs[0], qs[-1] + 1)
            else:
                row = (0, 0, 0, 0, 0)
            tbl[REGROUP_FIELDS * p:REGROUP_FIELDS * (p + 1), j] = row
    return jnp.asarray(tbl)


def lane_regroup(name, src, table, src_spec, out_spec, out_shape, grid, src_width, sub):
    rows = src.shape[-2]
    n_src = REGROUP_PIECES * sub

    def body(tbl, *refs):
        o_ref = refs[n_src]
        step = pl.program_id(1)

        def moved(b, p):
            j = step * sub + b
            blk, shift, lo, hi = (tbl[REGROUP_FIELDS * p + f, j] for f in range(1, REGROUP_FIELDS))
            x = refs[b * REGROUP_PIECES + p][...]
            if src_width % LANES:
                lane = lax.broadcasted_iota(jnp.int32, (rows, LANES), 1)
                x = jnp.where(lane < src_width - blk * LANES, x, jnp.zeros((), src.dtype))
            pi = lax.broadcasted_iota(jnp.int32, (LANES, LANES), 0)
            qi = lax.broadcasted_iota(jnp.int32, (LANES, LANES), 1)
            sel = jnp.logical_and(qi - pi == shift, jnp.logical_and(qi >= lo, qi < hi))
            return jnp.dot(x, sel.astype(src.dtype), preferred_element_type=f32).astype(o_ref.dtype)

        for b in range(sub):
            lanes = slice(b * LANES, (b + 1) * LANES)
            o_ref[:, lanes] = moved(b, 0)
            for p in range(1, REGROUP_PIECES):
                j = step * sub + b

                @pl.when(tbl[REGROUP_FIELDS * p + 4, j] > tbl[REGROUP_FIELDS * p + 3, j])
                def _(b=b, p=p, lanes=lanes):
                    o_ref[:, lanes] += moved(b, p)

    return pl.pallas_call(
        body, name=name, out_shape=jax.ShapeDtypeStruct(out_shape, src.dtype),
        grid_spec=pltpu.PrefetchScalarGridSpec(
            num_scalar_prefetch=1, grid=grid,
            in_specs=[src_spec(p, b) for b in range(sub) for p in range(REGROUP_PIECES)], out_specs=out_spec),
        compiler_params=_params(2))(table, *[src] * n_src)


def w_in_layout(d_cols, shard_w, c_split, gap, n_blocks):
    def ext_of(c):
        return c if c < c_split else c + gap

    def fwd_sources(j):
        groups = {}
        for q in range(LANES):
            e = j * LANES + q
            c = e if e < c_split else e - gap
            if (c_split <= e < c_split + gap) or c >= d_cols:
                continue
            s, l = divmod(c, shard_w)
            groups.setdefault((s, l // LANES), []).append((l % LANES, q))
        return groups

    per_shard = -(-shard_w // LANES)

    def bwd_sources(j):
        s, b = divmod(j, per_shard)
        groups = {}
        for q in range(LANES):
            l = b * LANES + q
            if l >= shard_w:
                continue
            e = ext_of(s * shard_w + l)
            groups.setdefault((0, e // LANES), []).append((e % LANES, q))
        return groups

    return _regroup_table(n_blocks, fwd_sources), _regroup_table(4 * per_shard, bwd_sources), per_shard


def _in_spec(a, kind, tile):
    if kind == "row":
        return pl.BlockSpec((tile, a.shape[1]), lambda j, i: (i, 0))
    if kind == "const":
        return pl.BlockSpec(a.shape, lambda j, i: (0, 0))
    tag, cw, off = kind
    if tag == "rowc":
        return pl.BlockSpec((tile, cw), lambda j, i: (i, j + off))
    assert tag == "constc", kind
    return pl.BlockSpec((a.shape[0], cw), lambda j, i: (0, j + off))


def rowcall(name, fn, ins, outs, *, rows, tile=None, ncol=1):
    if tile is None:
        tile = min(TOK_TILE_BLOCKED if ncol > 1 else TOK_TILE, rows)
    n_in = len(ins)
    kinds = [o[3] for o in outs]

    def body(*refs):
        j, i = pl.program_id(0), pl.program_id(1)
        res = fn(*[r[...] for r in refs[:n_in]])
        for ref, val, kind in zip(refs[n_in:], res, kinds):
            if kind in ("row", "rowc"):
                ref[...] = val.astype(ref.dtype)
            else:
                first = (i == 0) if kind == "accc" else jnp.logical_and(i == 0, j == 0)

                @pl.when(first)
                def _(ref=ref, val=val):
                    ref[...] = val.astype(ref.dtype)

                @pl.when(jnp.logical_not(first))
                def _(ref=ref, val=val):
                    ref[...] += val.astype(ref.dtype)

    out_shape, out_specs = [], []
    for nr, nc, dtype, kind in outs:
        out_shape.append(jax.ShapeDtypeStruct((nr, nc), dtype))
        if kind == "row":
            out_specs.append(pl.BlockSpec((tile, nc), lambda j, i: (i, 0)))
        elif kind == "rowc":
            out_specs.append(pl.BlockSpec((tile, nc // ncol), lambda j, i: (i, j)))
        elif kind == "acc":
            out_specs.append(pl.BlockSpec((nr, nc), lambda j, i: (0, 0)))
        else:
            out_specs.append(pl.BlockSpec((nr, nc // ncol), lambda j, i: (0, j)))
    return pl.pallas_call(
        body, name=name, grid=(ncol, rows // tile), in_specs=[_in_spec(a, k, tile) for a, k in ins],
        out_specs=out_specs, out_shape=out_shape, compiler_params=_params(2))(*[a for a, _ in ins])


def rowvjp(name, fn, ins, cts, grads, *, rows, tile=None, ncol=1, adds=(), dup16=()):
    n_in = len(ins)
    ct_sizes = [len(c) for c in cts]
    flat_cts = [m for c in cts for m in c]
    n_ct = len(flat_cts)

    def wrapped(*vals):
        xs = list(vals[:n_in])
        gs = vals[n_in:n_in + n_ct]
        extra = vals[n_in + n_ct:]

        def f(*dvars):
            full = list(xs)
            for k, v in zip(grads, dvars):
                full[k] = v
            return fn(*full)

        outs, pull = jax.vjp(f, *[xs[k] for k in grads])
        cot, p = [], 0
        for o, size in zip(outs, ct_sizes):
            g = gs[p].astype(f32)
            for q in range(1, size):
                g = g + gs[p + q].astype(f32)
            cot.append(g.astype(o.dtype))
            p += size
        gv = list(pull(tuple(cot)))
        for (pos, _, _), e in zip(adds, extra):
            gv[pos] = gv[pos] + e.astype(gv[pos].dtype)
        return tuple(gv) + tuple(gv[pos] for pos in dup16)

    outs = []
    for k in grads:
        a, kind = ins[k]
        if kind == "row":
            outs.append((rows, a.shape[1] * ncol, f32, "rowc") if ncol > 1 else (rows, a.shape[1], f32, "row"))
        elif kind == "const":
            outs.append((a.shape[0], a.shape[1], f32, "acc"))
        elif kind[0] == "rowc":
            outs.append((rows, kind[1] * ncol, f32, "rowc"))
        else:
            outs.append((a.shape[0], kind[1] * ncol, f32, "accc"))
    for pos in dup16:
        nr, nc, _, kind = outs[pos]
        outs.append((nr, nc, bf16, kind))
    all_ins = list(ins) + flat_cts + [(a, kind) for _, a, kind in adds]
    return rowcall(name, wrapped, all_ins, outs, rows=rows, tile=tile, ncol=ncol)


def fn_rms(x, w):
    return (rms(x, w),)


def make_fn_gconv(n_norm_blocks):
    def fn(p, cw):
        c = cw[3:4] * p
        for jj in range(3):
            c = c + cw[jj:jj + 1] * tshift(p, 3 - jj)
        s = c * jax.nn.sigmoid(c)
        nrm = s * lax.rsqrt(jnp.sum(s * s, axis=-1, keepdims=True) + 1e-6)
        return (jnp.where(pl.program_id(0) < n_norm_blocks, nrm, s),)
    return fn


def fn_ggate(ps0, alog, dtb, e_g, e_b):
    g = -jnp.exp(alog) * softplus(ps0 + dtb)
    beta = jax.nn.sigmoid(ps0)
    return pdot(g, e_g, "nn", 6), pdot(beta, e_b, "nn", 6)


def fn_gpost(o, z, nw):
    return (rms(o, nw) * (z * jax.nn.sigmoid(z)),)


def fn_lerp(p, mu):
    return (p + (tshift(p, 1) - p) * mu,)


def fn_rprep(pk, psl, w0, a0, k_k, k_a, w2p, a2p, g2):
    g1, g2in = psl[:, :LANES], psl[:, LANES:]
    log_w = -softplus(-(w0 + bdot(jnp.tanh(g1), w2p))) - 0.5
    lw = -jnp.exp(log_w)
    a = jax.nn.sigmoid(a0 + bdot(g1, a2p))
    gate = bdot(jax.nn.sigmoid(g2in), g2)
    kkr = pk * k_k
    kk = kkr / jnp.maximum(jnp.sqrt(seg2sum(kkr * kkr)), 1e-12)
    kmod = pk * (1.0 + (a - 1.0) * k_a)
    return lw, kmod, kk, a, gate


def fn_rpost(y, r, kmod, v, gate, r_k, ln_w, ln_b):
    inv_n = 1.0 / RW_HEAD
    mean = seg2sum(y) * inv_n
    d = y - mean
    var = seg2sum(d * d) * inv_n
    yn = d * lax.rsqrt(var + RW_GN_EPS) * ln_w + ln_b
    bonus = seg2sum(r * kmod * r_k) * v
    return ((yn + bonus) * gate,)


def fn_xcore(q, k, v):
    s = bdot_nt(q, k) * (LANES ** -0.5)
    p = jax.nn.softmax(s, axis=-1)
    return (bdot(p, v),)


def fn_final(h, tgt, w):
    def loss_fn(h, w):
        return 0.5 * jnp.sum(jnp.mean(jnp.square(rms(h, w) - tgt), axis=-1))

    val, (dh, dw) = jax.value_and_grad(loss_fn, argnums=(0, 1))(h, w)
    return dh, dh.astype(bf16), dw, jnp.full((8, LANES), val, f32)


def fn_sumcols(n):
    def fn(x):
        w = x.shape[1] // n
        s = x[:, :w]
        for q in range(1, n):
            s = s + x[:, q * w:(q + 1) * w]
        return (s,)
    return fn


def _tri(c):
    ii = lax.broadcasted_iota(jnp.int32, (c, c), 0)
    jj = lax.broadcasted_iota(jnp.int32, (c, c), 1)
    return ii, jj


def _neumann_raw(m, steps):
    c = m.shape[-1]
    ii, jj = _tri(c)
    eye = (ii == jj).astype(f32)
    t, p = eye + m, m
    for _ in range(steps):
        p = _raw_dot(p, p, "nn", P_INV)
        t = _raw_dot(t, eye + p, "nn", P_INV)
    resid = eye - t + _raw_dot(m, t, "nn", P_RESID)
    return t + _raw_dot(t, resid, "nn", P_INV)


@functools.partial(jax.custom_vjp, nondiff_argnums=(1,))
def _neumann_inverse(m, steps):
    return _neumann_raw(m, steps)


def _neumann_fwd(m, steps):
    t = _neumann_raw(m, steps)
    return t, t


def _neumann_bwd(steps, t, g):
    return (_raw_dot(_raw_dot(t, g, "tn", P_RESID), t, "nt", P_RESID),)


_neumann_inverse.defvjp(_neumann_fwd, _neumann_bwd)


@jax.custom_vjp
def _known_inverse(m, t):
    return t


_known_inverse.defvjp(lambda m, t: (t, t), lambda t, g: (_neumann_bwd(0, t, g)[0], jnp.zeros_like(t)))


def _inverse(m, steps, kept):
    return _neumann_inverse(m, steps) if kept is None else _known_inverse(m, kept)


def cumsum_rows(x):
    t = x.shape[1]
    ii, jj = _tri(t)
    tri = jnp.broadcast_to((ii >= jj).astype(f32), (x.shape[0], t, t))
    return pdot(tri, x, "nn", P_CUMSUM)


def gdn_chunk(q, k, v, gb, bb, s, kept=None):
    c = q.shape[1]
    ii, jj = _tri(c)
    low = ii >= jj
    gcb = cumsum_rows(gb)
    gl = jnp.sum(gb, axis=1, keepdims=True)
    gc_col = gcb[:, :, :c]
    diff = gc_col - jnp.swapaxes(gc_col, 1, 2)
    decay = jnp.where(low, jnp.exp(jnp.where(low, diff, 0.0)), 0.0)
    qs = q * (q.shape[2] ** -0.5)
    kb = k * bb
    with_k = pdot(jnp.concatenate([kb, qs], axis=1), k, "nt", P_BULK)
    a = jnp.where(ii > jj, with_k[:, :c] * decay, 0.0)
    attn = with_k[:, c:] * decay
    t = _inverse(-a, c.bit_length() - 2, kept)
    eg = jnp.exp(gcb)
    uw = pdot(t, jnp.concatenate([v * bb, kb * eg], axis=2), "nn", P_BULK)
    u, w = uw[:, :, :LANES], uw[:, :, LANES:]
    kd = k * jnp.exp(gl - gcb)
    from_state = pdot(jnp.concatenate([w, qs * eg], axis=1), s, "nn", P_BULK)
    v_new = u - from_state[:, :c]
    o = from_state[:, c:] + pdot(attn, v_new, "nn", P_BULK)
    s_new = s * jnp.exp(gl) + pdot(kd, v_new, "tn", P_BULK)
    return o, s_new, t


def wkv_chunk(r, lw, k, v, kk, a, s, kept=None):
    t = r.shape[1]
    ii, jj = _tri(t)
    lo = lax.broadcasted_iota(jnp.int32, r.shape, 2) < RW_HEAD
    cl = cumsum_rows(lw)
    cl_last = jnp.sum(lw, axis=1, keepdims=True)
    al = -kk * jnp.exp(cl - lw)
    be = (a * kk) * jnp.exp(-cl)
    kt = k * jnp.exp(-cl)
    rt = r * jnp.exp(cl)

    def dot(xa, xb, mode="nn"):
        return pdot(xa, xb, mode, P_BULK)

    def sel(x_lo, x_hi):
        return jnp.where(lo, x_lo, x_hi)

    left = jnp.concatenate([jnp.where(lo, al, 0.0), jnp.where(lo, 0.0, al),
                            jnp.where(lo, rt, 0.0), jnp.where(lo, 0.0, rt)], axis=1)
    with_be, with_kt = dot(left, be, "nt"), dot(left, kt, "nt")

    def blocks(prod):
        below, upto = ii > jj, ii >= jj
        return [jnp.where(msk, prod[:, q * t:(q + 1) * t], 0.0) for q, msk in enumerate((below, below, upto, upto))]

    ab_lo, ab_hi, rb_lo, rb_hi = blocks(with_be)
    ak_lo, ak_hi, rk_lo, rk_hi = blocks(with_kt)
    from_state = dot(jnp.concatenate([al, rt], axis=1), s, "nt")
    x = from_state[:, :t] + sel(dot(ak_lo, v), dot(ak_hi, v))
    steps = t.bit_length() - 2
    inv_lo = _inverse(ab_lo, steps, None if kept is None else kept[:, :t])
    inv_hi = _inverse(ab_hi, steps, None if kept is None else kept[:, t:])
    u = sel(dot(inv_lo, x), dot(inv_hi, x))
    y = from_state[:, t:] + sel(dot(rb_lo, u) + dot(rk_lo, v), dot(rb_hi, u) + dot(rk_hi, v))
    vi = lax.broadcasted_iota(jnp.int32, s.shape, 1) < RW_HEAD
    ki = lax.broadcasted_iota(jnp.int32, s.shape, 2) < RW_HEAD
    s_new = jnp.where(vi == ki, (s + dot(u, be, "tn") + dot(v, kt, "tn")) * jnp.exp(cl_last), 0.0)
    return y, s_new, jnp.concatenate([inv_lo, inv_hi], axis=1)


def _scan_group(ncol, offs):
    g = SCAN_GROUP
    while g > 1 and (ncol % g or any(o % g for o in offs)):
        g //= 2
    return g


def scan_fwd(name, chunk_fn, ins, *, rows, chunk, ncol):
    n = rows // chunk
    n_in = len(ins)
    grp = _scan_group(ncol, [off for _, off in ins])
    tile = jax.ShapeDtypeStruct((grp, chunk, LANES), f32)
    kept = jax.eval_shape(chunk_fn, *[tile] * n_in, jax.ShapeDtypeStruct((grp, LANES, LANES), f32))[2].shape[1:]

    def body(*refs):
        o_ref, st_ref, kept_ref, s_scr = refs[n_in:]

        @pl.when(pl.program_id(1) == 0)
        def _():
            s_scr[...] = jnp.zeros_like(s_scr)

        cols = [slice(b * LANES, (b + 1) * LANES) for b in range(grp)]
        s = s_scr[...]
        st_ref[...] = s
        o, s_new, inv = chunk_fn(*[jnp.stack([r[:, c] for c in cols]) for r in refs[:n_in]], s)
        for b, c in enumerate(cols):
            o_ref[:, c] = o[b]
        kept_ref[...] = inv
        s_scr[...] = s_new

    def spec(off):
        return pl.BlockSpec((chunk, grp * LANES), lambda h, c: (c, h + off // grp))

    def per_chunk(shape):
        return pl.BlockSpec((grp, None) + tuple(shape), lambda h, c: (h, c, 0, 0))

    return pl.pallas_call(
        body, name=name, grid=(ncol // grp, n), in_specs=[spec(off) for _, off in ins],
        out_specs=[spec(0), per_chunk((LANES, LANES)), per_chunk(kept)],
        out_shape=[jax.ShapeDtypeStruct((rows, ncol * LANES), f32),
                   jax.ShapeDtypeStruct((ncol, n, LANES, LANES), f32),
                   jax.ShapeDtypeStruct((ncol, n) + tuple(kept), f32)],
        scratch_shapes=[pltpu.VMEM((grp, LANES, LANES), f32)], compiler_params=_params(2))(*[a for a, _ in ins])


def scan_bwd(name, chunk_fn, ins, states, kept, d_out, d_off, *, rows, chunk, ncol):
    n = rows // chunk
    n_in = len(ins)
    grp = _scan_group(ncol, [off for _, off in ins] + [d_off])

    def body(*refs):
        st_ref, kept_ref, do_ref = refs[n_in:n_in + 3]
        g_refs = refs[n_in + 3:2 * n_in + 3]
        ds_scr = refs[-1]

        @pl.when(pl.program_id(1) == 0)
        def _():
            ds_scr[...] = jnp.zeros_like(ds_scr)

        cols = [slice(b * LANES, (b + 1) * LANES) for b in range(grp)]

        def batch(ref):
            return jnp.stack([ref[:, c] for c in cols])

        inv = kept_ref[...]
        _, pull = jax.vjp(lambda *a: chunk_fn(*a, kept=inv)[:2], *[batch(r) for r in refs[:n_in]], st_ref[...])
        gs = pull((batch(do_ref), ds_scr[...]))
        for ref, g in zip(g_refs, gs[:n_in]):
            for b, c in enumerate(cols):
                ref[:, c] = g[b]
        ds_scr[...] = gs[n_in]

    def spec(off):
        return pl.BlockSpec((chunk, grp * LANES), lambda h, c: (n - 1 - c, h + off // grp))

    def per_chunk(shape):
        return pl.BlockSpec((grp, None) + tuple(shape), lambda h, c: (h, n - 1 - c, 0, 0))

    return pl.pallas_call(
        body, name=name, grid=(ncol // grp, n),
        in_specs=[spec(off) for _, off in ins] + [per_chunk(states.shape[2:]), per_chunk(kept.shape[2:]), spec(d_off)],
        out_specs=[spec(0)] * n_in, out_shape=[jax.ShapeDtypeStruct((rows, ncol * LANES), f32)] * n_in,
        scratch_shapes=[pltpu.VMEM((grp, LANES, LANES), f32)],
        compiler_params=_params(2))(*[a for a, _ in ins], states, kept, d_out)


def flip_exchange(name, arrs, flips, n_slots, slot_of, src_of, with_self, after=()):
    n = len(arrs)
    nf = len(flips)
    n_after = len(after)

    def body(*refs):
        ins, outs = refs[:n], refs[n + n_after:2 * n + n_after]
        send, recv, lsem = refs[2 * n + n_after:]
        me = (lax.axis_index("x"), lax.axis_index("y"), lax.axis_index("c"))
        copies = []
        for k in range(n):
            if with_self:
                cp = pltpu.make_async_copy(src_of(ins[k], me), outs[k].at[slot_of(me)], lsem.at[k])
                cp.start()
                copies.append(cp)
            for j, fl in enumerate(flips):
                peer = tuple(1 - m if f else m for m, f in zip(me, fl))
                cp = pltpu.make_async_remote_copy(
                    src_ref=src_of(ins[k], peer), dst_ref=outs[k].at[slot_of(me)], send_sem=send.at[k, j],
                    recv_sem=recv.at[k, j], device_id=peer, device_id_type=MESH)
                cp.start()
                copies.append(cp)
        for cp in copies:
            cp.wait()

    def out_sds(a):
        blk = src_of(jax.ShapeDtypeStruct(a.shape, a.dtype), None)
        return jax.ShapeDtypeStruct((n_slots,) + tuple(blk), a.dtype)

    any_spec = pl.BlockSpec(memory_space=pl.ANY)
    return pl.pallas_call(
        body, name=name, in_specs=[any_spec] * (n + n_after), out_specs=[any_spec] * n,
        out_shape=[out_sds(a) for a in arrs],
        scratch_shapes=[pltpu.SemaphoreType.DMA((n, nf)), pltpu.SemaphoreType.DMA((n, nf)),
                        pltpu.SemaphoreType.DMA((n,))],
        compiler_params=pltpu.CompilerParams(has_side_effects=True))(*arrs, *after)


_CHIP_FLIPS = ((1, 0, 0), (0, 1, 0), (1, 1, 0))
_ALL_FLIPS = ((0, 0, 1), (0, 1, 0), (0, 1, 1), (1, 0, 0), (1, 0, 1), (1, 1, 0), (1, 1, 1))


def _whole(ref, pos):
    return ref.shape if pos is None else ref


def _chip_block(ref, pos):
    return ref.shape[1:] if pos is None else ref.at[2 * pos[0] + pos[1]]


def _chip_slot(p):
    return 2 * p[0] + p[1]


_HBM = pl.BlockSpec(memory_space=pltpu.HBM)
_SEM = pl.BlockSpec(memory_space=pltpu.SEMAPHORE)
_DATAFLOW = pltpu.SideEffectType.DATAFLOW_SIDE_EFFECTING


def _split_copies(mode, refs, n, send, recv):
    me = (lax.axis_index("x"), lax.axis_index("y"), lax.axis_index("c"))
    sib = (me[0], me[1], 1 - me[2])
    lands = refs[:n] if mode == "handover" else refs[n:2 * n]
    copies = []
    for k, land in enumerate(lands):
        half = land.shape[1] // 2
        mine = pl.ds(pl.multiple_of(me[2] * half, 16), half)
        for j, fl in enumerate(_CHIP_FLIPS):
            peer = tuple(1 - m if f else m for m, f in zip(me, fl))
            if mode == "gather":
                src, dst, to = refs[k], land.at[_chip_slot(me)], peer
            elif mode == "scatter":
                src, dst, to = refs[k].at[_chip_slot(peer)], land.at[_chip_slot(me)], peer
            elif mode == "gather_half":
                src, dst, to = refs[k].at[mine], land.at[_chip_slot(me), mine], peer
            else:
                src = dst = land.at[_chip_slot(peer), mine]
                to = sib
            q = k * len(_CHIP_FLIPS) + j
            copies.append(pltpu.make_async_remote_copy(src_ref=src, dst_ref=dst, send_sem=send.at[q],
                                                       recv_sem=recv.at[q], device_id=to, device_id_type=MESH))
    return copies


def split_start(name, mode, ops, n):
    ops = [pltpu.with_memory_space_constraint(a, pltpu.HBM) for a in ops]
    m = len(ops)

    def body(*refs):
        for cp in _split_copies(mode, refs[:m], n, refs[m], refs[m + 1]):
            cp.start()
        refs[-1][...] = jnp.zeros_like(refs[-1])

    sems = pltpu.SemaphoreType.DMA((n * len(_CHIP_FLIPS),))
    outs = pl.pallas_call(
        body, name=name, in_specs=[_HBM] * m,
        out_shape=(sems, sems, *[pltpu.HBM(a.shape, a.dtype) for a in ops], jax.ShapeDtypeStruct((8, LANES), f32)),
        out_specs=(_SEM, _SEM, *[_HBM] * m, pl.BlockSpec(memory_space=pltpu.VMEM)),
        input_output_aliases={i: 2 + i for i in range(m)},
        compiler_params=pltpu.CompilerParams(has_side_effects=_DATAFLOW))(*ops)
    return (outs[0], outs[1], list(outs[2:2 + m]), mode, n), outs[-1][0, 0]


def split_wait(name, state, after):
    send, recv, ops, mode, n = state
    m = len(ops)
    afters = list(after) if isinstance(after, (list, tuple)) else [after]

    def body(*refs):
        for cp in _split_copies(mode, refs[:m], n, refs[m], refs[m + 1]):
            cp.wait_send()
            cp.wait_recv()

    outs = pl.pallas_call(
        body, name=name, in_specs=[_HBM] * m + [_SEM, _SEM] + [pl.BlockSpec(memory_space=pl.ANY)] * len(afters),
        out_shape=tuple(pltpu.HBM(a.shape, a.dtype) for a in ops), out_specs=tuple([_HBM] * m),
        input_output_aliases={i: i for i in range(m)},
        compiler_params=pltpu.CompilerParams(has_side_effects=_DATAFLOW))(*ops, send, recv, *afters)
    return list(outs[m - n:])


def chips_start(name, arrs, src_of, halves=False):
    me = _chip_slot((lax.axis_index("x"), lax.axis_index("y")))
    lands = []
    for a in arrs:
        blk = tuple(src_of(jax.ShapeDtypeStruct(a.shape, a.dtype), None))
        own = a if src_of is _whole else lax.dynamic_index_in_dim(a, me, 0, keepdims=False)
        lands.append(lax.dynamic_update_index_in_dim(lax.empty((4,) + blk, a.dtype), own, me, 0))
    mode = "scatter" if src_of is _chip_block else ("gather_half" if halves else "gather")
    return split_start(name, mode, list(arrs) + lands, len(arrs))


chips_wait = split_wait


def handover_start(name, lands):
    return split_start(name, "handover", lands, len(lands))


def gather_chips_halves(name, arrs):
    n = len(arrs)
    nf = len(_CHIP_FLIPS)
    split = [a.shape[0] % 32 == 0 for a in arrs]

    def body(*refs):
        ins, outs = refs[:n], refs[n:2 * n]
        send1, recv1, send2, recv2, lsem = refs[2 * n:]
        me = (lax.axis_index("x"), lax.axis_index("y"), lax.axis_index("c"))
        sib = (me[0], me[1], 1 - me[2])
        peers = [tuple(1 - m if f else m for m, f in zip(me, fl)) for fl in _CHIP_FLIPS]
        local, first, second = [], [], []
        for k in range(n):
            cp = pltpu.make_async_copy(ins[k], outs[k].at[_chip_slot(me)], lsem.at[k])
            cp.start()
            local.append(cp)
            half = ins[k].shape[0] // 2
            rows = pl.ds(pl.multiple_of(me[2] * half, 16), half) if split[k] else pl.ds(0, ins[k].shape[0])
            for j, peer in enumerate(peers):
                cp = pltpu.make_async_remote_copy(
                    src_ref=ins[k].at[rows], dst_ref=outs[k].at[_chip_slot(me), rows], send_sem=send1.at[k, j],
                    recv_sem=recv1.at[k, j], device_id=peer, device_id_type=MESH)
                cp.start()
                first.append((k, j, rows, cp))
        for k, j, rows, cp in first:
            cp.wait_recv()
            if split[k]:
                got = outs[k].at[_chip_slot(peers[j]), rows]
                fwd = pltpu.make_async_remote_copy(src_ref=got, dst_ref=got, send_sem=send2.at[k, j],
                                                   recv_sem=recv2.at[k, j], device_id=sib, device_id_type=MESH)
                fwd.start()
                second.append(fwd)
        for _, _, _, cp in first:
            cp.wait_send()
        for cp in second:
            cp.wait()
        for cp in local:
            cp.wait()

    any_spec = pl.BlockSpec(memory_space=pl.ANY)
    sems = pltpu.SemaphoreType.DMA((n, nf))
    return pl.pallas_call(
        body, name=name, in_specs=[any_spec] * n, out_specs=[any_spec] * n,
        out_shape=[jax.ShapeDtypeStruct((4,) + a.shape, a.dtype) for a in arrs],
        scratch_shapes=[sems, sems, sems, sems, pltpu.SemaphoreType.DMA((n,))],
        compiler_params=pltpu.CompilerParams(has_side_effects=True))(*arrs)


def swap_sibling(name, arrs):
    outs = flip_exchange(name, arrs, ((0, 0, 1),), 1, lambda p: 0, _whole, False)
    return [o[0] for o in outs]


def gather_all(arrs, after=()):
    return flip_exchange("gather_all", arrs, _ALL_FLIPS, 8, lambda p: 4 * p[0] + 2 * p[1] + p[2], _whole, True,
                         after=after)


def _row_tile(nr, nc, n_arrays):
    budget = (20 << 20) // (n_arrays * 2 * 4 * max(nc, LANES))
    t = min(nr, budget) // 16 * 16
    while t > 0 and nr % t:
        t -= 16
    return t if t > 0 else nr


def sum_slots(name, x):
    ns, nr, nc = x.shape
    tile = _row_tile(nr, nc, ns + 1)

    def body(x_ref, o_ref):
        s = x_ref[0].astype(f32)
        for q in range(1, ns):
            s = s + x_ref[q].astype(f32)
        o_ref[...] = s.astype(o_ref.dtype)

    return pl.pallas_call(
        body, name=name, grid=(nr // tile,), in_specs=[pl.BlockSpec((ns, tile, nc), lambda i: (0, i, 0))],
        out_specs=pl.BlockSpec((tile, nc), lambda i: (i, 0)), out_shape=jax.ShapeDtypeStruct((nr, nc), x.dtype),
        compiler_params=_params(1))(x)


def adamw(name, w, g_parts, m, v):
    nr, nc = w.shape[-2:]
    n_g = len(g_parts)
    tile = _row_tile(nr, nc, 7 + n_g)

    def body(*refs):
        w_ref, m_ref, v_ref = refs[:3]
        g = refs[3][...].astype(f32)
        for r in refs[4:3 + n_g]:
            g = g + r[...].astype(f32)
        g_ref, d_ref, nm_ref, nv_ref = refs[3 + n_g:]
        nm = ADAM_B1 * m_ref[...] + (1.0 - ADAM_B1) * g
        nv = ADAM_B2 * v_ref[...] + (1.0 - ADAM_B2) * jnp.square(g)
        m_hat = nm / (1.0 - ADAM_B1 ** ADAM_STEP)
        v_hat = nv / (1.0 - ADAM_B2 ** ADAM_STEP)
        g_ref[...] = g
        d_ref[...] = -ADAM_LR * (m_hat / (jnp.sqrt(v_hat) + ADAM_EPS) + ADAM_WD * w_ref[...])
        nm_ref[...] = nm
        nv_ref[...] = nv

    spec = pl.BlockSpec((tile, nc), lambda i: (i, 0))
    spec3 = pl.BlockSpec((None, tile, nc), lambda i: (0, i, 0)) if w.ndim == 3 else spec
    return pl.pallas_call(
        body, name=name, grid=(nr // tile,), in_specs=[spec3] * 3 + [spec] * n_g, out_specs=[spec3] * 4,
        out_shape=[jax.ShapeDtypeStruct(w.shape, f32)] * 4, compiler_params=_params(1))(w, m, v, *g_parts)


def adamw_packed(w, g8, m, v):
    nr, nc = w.shape

    def body(w_ref, g_ref, m_ref, v_ref, go_ref, d_ref, nm_ref, nv_ref):
        g = g_ref[0]
        for q in range(1, 8):
            g = g + g_ref[q]
        nm = ADAM_B1 * m_ref[...] + (1.0 - ADAM_B1) * g
        nv = ADAM_B2 * v_ref[...] + (1.0 - ADAM_B2) * jnp.square(g)
        m_hat = nm / (1.0 - ADAM_B1 ** ADAM_STEP)
        v_hat = nv / (1.0 - ADAM_B2 ** ADAM_STEP)
        go_ref[...] = g
        d_ref[...] = -ADAM_LR * (m_hat / (jnp.sqrt(v_hat) + ADAM_EPS) + ADAM_WD * w_ref[...])
        nm_ref[...] = nm
        nv_ref[...] = nv

    return pl.pallas_call(body, name="adamw_packed", out_shape=[jax.ShapeDtypeStruct((nr, nc), f32)] * 4,
                          compiler_params=pltpu.CompilerParams(vmem_limit_bytes=VMEM_LIMIT))(w, g8, m, v)


def _pack(vectors):
    rows = []
    for a in vectors:
        flat = a.reshape(-1).astype(f32)
        pad = (-flat.shape[0]) % LANES
        rows.append(jnp.pad(flat, (0, pad)).reshape(-1, LANES))
    packed = jnp.concatenate(rows, axis=0)
    return jnp.pad(packed, ((0, (-packed.shape[0]) % 8), (0, 0)))


def _unpack(packed, like):
    out, r = [], 0
    for a in like:
        n = a.size
        nr = -(-n // LANES)
        out.append(packed[r:r + nr].reshape(-1)[:n].reshape(a.shape))
        r += nr
    return out


def kernel(x, mem, mix_norm_w, w_in, dn_conv_w, dn_a_log, dn_dt_bias, dn_norm_w, rw_mu, rw_w0, rw_w2, rw_a0, rw_a2, rw_g2, rw_k_k, rw_k_a, rw_r_k, rw_ln_w, rw_ln_b, w_out, xa_norm_w, mem_norm_w, xa_wq, xa_wk, xa_wv, xa_wo, ffn_norm_w, ffn_w1, ffn_w2, final_norm_w, loss_target, m_mix_norm_w, m_w_in, m_dn_conv_w, m_dn_a_log, m_dn_dt_bias, m_dn_norm_w, m_rw_mu, m_rw_w0, m_rw_w2, m_rw_a0, m_rw_a2, m_rw_g2, m_rw_k_k, m_rw_k_a, m_rw_r_k, m_rw_ln_w, m_rw_ln_b, m_w_out, m_xa_norm_w, m_mem_norm_w, m_xa_wq, m_xa_wk, m_xa_wv, m_xa_wo, m_ffn_norm_w, m_ffn_w1, m_ffn_w2, m_final_norm_w, v_mix_norm_w, v_w_in, v_dn_conv_w, v_dn_a_log, v_dn_dt_bias, v_dn_norm_w, v_rw_mu, v_rw_w0, v_rw_w2, v_rw_a0, v_rw_a2, v_rw_g2, v_rw_k_k, v_rw_k_a, v_rw_r_k, v_rw_ln_w, v_rw_ln_b, v_w_out, v_xa_norm_w, v_mem_norm_w, v_xa_wq, v_xa_wk, v_xa_wv, v_xa_wo, v_ffn_norm_w, v_ffn_w1, v_ffn_w2, v_final_norm_w):
    weights = dict(mix_norm_w=mix_norm_w, w_in=w_in, dn_conv_w=dn_conv_w, dn_a_log=dn_a_log, dn_dt_bias=dn_dt_bias, dn_norm_w=dn_norm_w, rw_mu=rw_mu, rw_w0=rw_w0, rw_w2=rw_w2, rw_a0=rw_a0, rw_a2=rw_a2, rw_g2=rw_g2, rw_k_k=rw_k_k, rw_k_a=rw_k_a, rw_r_k=rw_r_k, rw_ln_w=rw_ln_w, rw_ln_b=rw_ln_b, w_out=w_out, xa_norm_w=xa_norm_w, mem_norm_w=mem_norm_w, xa_wq=xa_wq, xa_wk=xa_wk, xa_wv=xa_wv, xa_wo=xa_wo, ffn_norm_w=ffn_norm_w, ffn_w1=ffn_w1, ffn_w2=ffn_w2, final_norm_w=final_norm_w)
    mom_m = dict(mix_norm_w=m_mix_norm_w, w_in=m_w_in, dn_conv_w=m_dn_conv_w, dn_a_log=m_dn_a_log, dn_dt_bias=m_dn_dt_bias, dn_norm_w=m_dn_norm_w, rw_mu=m_rw_mu, rw_w0=m_rw_w0, rw_w2=m_rw_w2, rw_a0=m_rw_a0, rw_a2=m_rw_a2, rw_g2=m_rw_g2, rw_k_k=m_rw_k_k, rw_k_a=m_rw_k_a, rw_r_k=m_rw_r_k, rw_ln_w=m_rw_ln_w, rw_ln_b=m_rw_ln_b, w_out=m_w_out, xa_norm_w=m_xa_norm_w, mem_norm_w=m_mem_norm_w, xa_wq=m_xa_wq, xa_wk=m_xa_wk, xa_wv=m_xa_wv, xa_wo=m_xa_wo, ffn_norm_w=m_ffn_norm_w, ffn_w1=m_ffn_w1, ffn_w2=m_ffn_w2, final_norm_w=m_final_norm_w)
    mom_v = dict(mix_norm_w=v_mix_norm_w, w_in=v_w_in, dn_conv_w=v_dn_conv_w, dn_a_log=v_dn_a_log, dn_dt_bias=v_dn_dt_bias, dn_norm_w=v_dn_norm_w, rw_mu=v_rw_mu, rw_w0=v_rw_w0, rw_w2=v_rw_w2, rw_a0=v_rw_a0, rw_a2=v_rw_a2, rw_g2=v_rw_g2, rw_k_k=v_rw_k_k, rw_k_a=v_rw_k_a, rw_r_k=v_rw_r_k, rw_ln_w=v_rw_ln_w, rw_ln_b=v_rw_ln_b, w_out=v_w_out, xa_norm_w=v_xa_norm_w, mem_norm_w=v_mem_norm_w, xa_wq=v_xa_wq, xa_wk=v_xa_wk, xa_wv=v_xa_wv, xa_wo=v_xa_wo, ffn_norm_w=v_ffn_norm_w, ffn_w1=v_ffn_w1, ffn_w2=v_ffn_w2, final_norm_w=v_final_norm_w)
    names = list(weights)

    seq, d = x.shape[1], x.shape[2]
    dnw = d // 2
    rww = d - dnw
    nh, nb = dnw // LANES, rww // LANES
    n_mem = mem.shape[1]
    lw_dim, la_dim, lg_dim = rw_w2.shape[1], rw_a2.shape[1], rw_g2.shape[1]
    assert lw_dim + la_dim == LANES and lg_dim == LANES and dnw % LANES == 0 and rww % LANES == 0
    xs, mems, tgt = x[0], mem[0], loss_target[0]

    col_sharded = ("w_in", "xa_wo", "ffn_w1", "dn_conv_w", "rw_w2", "rw_a2", "rw_g2")
    row_sharded = ("w_out", "xa_wq", "xa_wk", "xa_wv", "ffn_w2")
    f32_payload = ("dn_conv_w", "rw_w2", "rw_a2", "rw_g2")
    sharded = col_sharded + row_sharded
    payload = {n: weights[n][0].astype(f32 if n in f32_payload else bf16) for n in sharded}
    shard_w = w_in.shape[2]
    pad_w = -(-shard_w // LANES) * LANES
    first = ("w_in", "dn_conv_w", "rw_w2", "rw_a2", "rw_g2")
    mid = ("w_out", "xa_wq", "xa_wk", "xa_wv", "xa_wo")
    late = ("ffn_w1", "ffn_w2")
    gathered = dict(zip(first, gather_chips_halves("gather_first", [payload[n] for n in first])))
    ordered = lax.optimization_barrier(([gathered[n] for n in first], [payload[n] for n in mid + late]))
    gathered = dict(zip(first, ordered[0]))
    payload.update(zip(mid + late, ordered[1]))
    mid_state, tok_mid = chips_start("gather_mid_start", [payload[n] for n in mid], _whole, halves=True)
    late_state, tok_late = chips_start("gather_late_start", [payload[n] for n in late], _whole, halves=True)
    mix_norm_w_in = mix_norm_w + (tok_mid + tok_late)

    def full(n):
        g = gathered[n]
        if n in col_sharded:
            return g.transpose(1, 0, 2).reshape(g.shape[1], 4 * g.shape[2])
        return g.reshape(4 * g.shape[1], g.shape[2])

    c_rw0 = 4 * dnw + 2 * nh
    c_rw = 3 * rww
    eb_ab, eb_r = 4 * nh, 4 * nh + 1
    eb_l1 = eb_r + 3 * nb
    n_ext = -(-(eb_l1 + 2) // 4) * 4
    tbl_fwd, tbl_bwd, per_shard = w_in_layout(4 * shard_w, shard_w, c_rw0, LANES - 2 * nh, n_ext)
    sub_f = max(s for s in range(1, REGROUP_SUB + 1) if n_ext % s == 0)
    sub_b = max(s for s in range(1, REGROUP_SUB + 1) if per_shard % s == 0)
    w_ext = lane_regroup(
        "w_in_regroup", gathered["w_in"], tbl_fwd,
        lambda p, b: pl.BlockSpec(
            (None, d, LANES), lambda i, j, t: (t[REGROUP_FIELDS * p, j * sub_f + b], 0,
                                               t[REGROUP_FIELDS * p + 1, j * sub_f + b])),
        pl.BlockSpec((d, sub_f * LANES), lambda i, j, t: (0, j)), (d, n_ext * LANES), (1, n_ext // sub_f), shard_w,
        sub_f)
    conv_w = full("dn_conv_w")
    w2p = jnp.concatenate([full("rw_w2"), jnp.zeros((la_dim, rww), f32)], axis=0)
    a2p = jnp.concatenate([jnp.zeros((lw_dim, rww), f32), full("rw_a2")], axis=0)
    g2 = full("rw_g2")
    xaw = xa_wq.shape[2]
    nxh = xaw // LANES
    ffn = 4 * ffn_w1.shape[2]

    def lane_row(vec):
        return jnp.pad(vec.reshape(1, -1), ((0, 0), (0, LANES - vec.size)))

    alog_row, dtb_row = lane_row(dn_a_log), lane_row(dn_dt_bias)
    head_of_col = jnp.arange(dnw)[None, :] // LANES
    e_g = (jnp.arange(LANES)[:, None] == head_of_col).astype(f32)
    e_b = (jnp.arange(LANES)[:, None] == head_of_col + nh).astype(f32)
    mu_main, mu_small = rw_mu[:, :c_rw], rw_mu[:, c_rw:]
    r_k_row = rw_r_k.reshape(1, rww)
    fnw = final_norm_w.reshape(1, d)
    qb, kb_, vb, zb = 0, nh, 2 * nh, 3 * nh
    rb0 = eb_r
    cc = lambda w_, o_: ("constc", w_, o_)
    rc = lambda o_: ("rowc", LANES, o_)

    (u16,) = rowcall("norm_mix", fn_rms, [(xs, "row"), (mix_norm_w_in, "const")], [(seq, d, bf16, "row")], rows=seq)
    p_main = p_small = mm("in_proj", u16, w_ext)

    fn_gconv = make_fn_gconv(2 * nh)
    (qkv,) = rowcall("gdn_conv", fn_gconv, [(p_main, rc(0)), (conv_w, cc(LANES, 0))],
                     [(seq, 3 * dnw, f32, "rowc")], rows=seq, tile=seq, ncol=3 * nh)
    gate_ins = [(p_small, rc(eb_ab)), (alog_row, "const"), (dtb_row, "const"), (e_g, "const"), (e_b, "const")]
    g_b, beta_b = rowcall("gdn_gate", fn_ggate, gate_ins, [(seq, dnw, f32, "row")] * 2, rows=seq)
    gdn_ins = [(qkv, qb), (qkv, kb_), (qkv, vb), (g_b, 0), (beta_b, 0)]
    o_raw, gdn_states, gdn_kept = scan_fwd("gdn_scan", gdn_chunk, gdn_ins, rows=seq, chunk=GDN_CHUNK, ncol=nh)
    mid_state, tok = handover_start("gather_mid_pass", chips_wait("gather_mid_wait", mid_state, o_raw))
    gpost_ins = [(o_raw, rc(0)), (p_main, rc(zb)), (dn_norm_w + tok, "const")]
    (o_dn,) = rowcall("gdn_post", fn_gpost, gpost_ins, [(seq, dnw, bf16, "rowc")], rows=seq, ncol=nh)

    (prw,) = rowcall("rw_lerp_main", fn_lerp, [(p_main, rc(rb0)), (mu_main, cc(LANES, 0))],
                     [(seq, c_rw, f32, "rowc")], rows=seq, tile=seq, ncol=3 * nb)
    (psl,) = rowcall("rw_lerp_small", fn_lerp, [(p_small, rc(eb_l1)), (mu_small, cc(LANES, 0))],
                     [(seq, 2 * LANES, f32, "rowc")], rows=seq, tile=seq, ncol=2)
    rprep_ins = [(prw, rc(nb)), (psl, "row"), (rw_w0, cc(LANES, 0)), (rw_a0, cc(LANES, 0)), (rw_k_k, cc(LANES, 0)),
                 (rw_k_a, cc(LANES, 0)), (w2p, cc(LANES, 0)), (a2p, cc(LANES, 0)), (g2, cc(LANES, 0))]
    lw, kmod, kk, a_rw, gate = rowcall("rw_prep", fn_rprep, rprep_ins, [(seq, rww, f32, "rowc")] * 5,
                                        rows=seq, ncol=nb)
    wkv_ins = [(prw, 0), (lw, 0), (kmod, 0), (prw, 2 * nb), (kk, 0), (a_rw, 0)]
    y_rw, wkv_states, wkv_kept = scan_fwd("wkv_scan", wkv_chunk, wkv_ins, rows=seq, chunk=WKV_CHUNK, ncol=nb)
    rpost_ins = [(y_rw, rc(0)), (prw, rc(0)), (kmod, rc(0)), (prw, rc(2 * nb)), (gate, rc(0)),
                 (r_k_row, cc(LANES, 0)), (rw_ln_w, cc(LANES, 0)), (rw_ln_b, cc(LANES, 0))]
    (o_rw,) = rowcall("rw_post", fn_rpost, rpost_ins, [(seq, rww, bf16, "rowc")], rows=seq, ncol=nb)

    o_cat = jnp.concatenate([o_dn, o_rw], axis=1)
    late_state, tok = handover_start("gather_late_pass", chips_wait("gather_late_wait", late_state, o_cat))
    gathered.update(zip(mid, chips_wait("gather_mid_pass_wait", mid_state, o_cat)))
    w_out_f, wq_f, wk_f, wv_f, wo_f = full("w_out"), full("xa_wq"), full("xa_wk"), full("xa_wv"), full("xa_wo")
    h1 = mm("out_proj", o_cat, w_out_f, add=xs)

    (hn16,) = rowcall("norm_xa", fn_rms, [(h1, "row"), (xa_norm_w + tok, "const")], [(seq, d, bf16, "row")],
                      rows=seq)
    (mn16,) = rowcall("norm_mem", fn_rms, [(mems, "row"), (mem_norm_w, "const")], [(n_mem, d, bf16, "row")],
                      rows=n_mem)
    q_xa = mm("xa_q", hn16, wq_f)
    k_xa = mm("xa_k", mn16, wk_f)
    v_xa = mm("xa_v", mn16, wv_f)
    xcore_ins = [(q_xa, rc(0)), (k_xa, cc(LANES, 0)), (v_xa, cc(LANES, 0))]
    (o_xa,) = rowcall("xa_core", fn_xcore, xcore_ins, [(seq, xaw, bf16, "rowc")], rows=seq, ncol=nxh)
    h2 = mm("xa_o", o_xa, wo_f, add=h1)

    (fn16,) = rowcall("norm_ffn", fn_rms, [(h2, "row"), (ffn_norm_w, "const")], [(seq, d, bf16, "row")], rows=seq)
    gathered.update(zip(late, chips_wait("gather_late_pass_wait", late_state, fn16)))
    w1_f, w2_f = full("ffn_w1"), full("ffn_w2")
    a1_16, hid16 = mm("ffn_up", fn16, w1_f, epilogue=lambda r: (r, jnp.square(jnp.maximum(r, 0.0))),
                      out_dtypes=(bf16, bf16))
    h3 = mm("ffn_down", hid16, w2_f, add=h2)

    dh3, dh3_16, d_fnw, loss_rows = rowcall(
        "loss_head", fn_final, [(h3, "row"), (tgt, "row"), (fnw, "const")],
        [(seq, d, f32, "row"), (seq, d, bf16, "row"), (1, d, f32, "acc"), (8, LANES, f32, "acc")], rows=seq)

    da1_16 = mm("ffn_down_dx", dh3_16, w2_f, tb=True, extra=[a1_16], out_dtype=bf16,
                epilogue=lambda r, a1: (r * (2.0 * jnp.maximum(a1.astype(f32), 0.0)),))
    def by_chip(n, g):
        if g.ndim == 3:
            return g
        if n in col_sharded:
            return g.reshape(g.shape[0], 4, g.shape[1] // 4).transpose(1, 0, 2)
        return g.reshape(4, g.shape[0] // 4, g.shape[1])

    g_ffn_w2 = mm("ffn_down_dw", hid16, dh3_16, ta=True, out_dtype=bf16)
    ffn_w2_g, tok = chips_start("scatter_ffn_w2_start", [by_chip("ffn_w2", g_ffn_w2)], _chip_block)
    g_ffn_w1 = mm("ffn_up_dw", fn16, da1_16, ta=True, out_dtype=bf16, by_chip=True, after=tok)
    ffn_w1_g, tok = chips_start("scatter_ffn_w1_start", [g_ffn_w1], _chip_block)
    dfn = mm("ffn_up_dx", da1_16, w1_f, tb=True, after=tok)
    dh2, d_ffn_nw, dh2_16 = rowvjp("norm_ffn_bwd", fn_rms, [(h2, "row"), (ffn_norm_w, "const")],
                                   [[(dfn, "row")]], [0, 1], rows=seq, adds=[(0, dh3, "row")], dup16=[0])

    do_xa = mm("xa_o_dx", dh2_16, wo_f, tb=True)
    g_xa_wo = mm("xa_o_dw", o_xa, dh2_16, ta=True, out_dtype=bf16, by_chip=True)
    dq_xa, dk_xa, dv_xa, dq16 = rowvjp("xa_core_bwd", fn_xcore, xcore_ins, [[(do_xa, rc(0))]], [0, 1, 2],
                                       rows=seq, ncol=nxh, dup16=[0])
    g_xa_wq = mm("xa_q_dw", hn16, dq16, ta=True, out_dtype=bf16)
    dhn = mm("xa_q_dx", dq16, wq_f, tb=True)
    dh1, d_xa_nw, dh1_16 = rowvjp("norm_xa_bwd", fn_rms, [(h1, "row"), (xa_norm_w, "const")], [[(dhn, "row")]],
                                  [0, 1], rows=seq, adds=[(0, dh2, "row")], dup16=[0])
    dk16, dv16 = dk_xa.astype(bf16), dv_xa.astype(bf16)
    g_xa_wk = mm("xa_k_dw", mn16, dk16, ta=True, out_dtype=bf16)
    g_xa_wv = mm("xa_v_dw", mn16, dv16, ta=True, out_dtype=bf16)
    dmn = mm("xa_v_dx", dv16, wv_f, tb=True, add=mm("xa_k_dx", dk16, wk_f, tb=True))
    (d_mem_nw,) = rowvjp("norm_mem_bwd", fn_rms, [(mems, "row"), (mem_norm_w, "const")], [[(dmn, "row")]], [1],
                         rows=n_mem)

    g_w_out = mm("out_proj_dw", o_cat, dh1_16, ta=True, out_dtype=bf16)
    mid_grads = dict(w_out=g_w_out, xa_wq=g_xa_wq, xa_wk=g_xa_wk, xa_wv=g_xa_wv, xa_wo=g_xa_wo)
    mid_g, tok = chips_start("scatter_mid_start", [by_chip(n, mid_grads[n]) for n in mid], _chip_block)
    do_cat = mm("out_proj_dx", dh1_16, w_out_f, tb=True, after=tok)

    dy, dr_a, dkmod_a, dv_a, dgate, d_r_k, d_ln_w, d_ln_b = rowvjp(
        "rw_post_bwd", fn_rpost, rpost_ins, [[(do_cat, rc(nh))]], [0, 1, 2, 3, 4, 5, 6, 7], rows=seq, ncol=nb)
    dr_b, dlw, dkmod_b, dv_b, dkk, da_rw = scan_bwd("wkv_scan_bwd", wkv_chunk, wkv_ins, wkv_states, wkv_kept, dy, 0,
                                                    rows=seq, chunk=WKV_CHUNK, ncol=nb)
    rprep_cts = [[(dlw, rc(0))], [(dkmod_a, rc(0)), (dkmod_b, rc(0))], [(dkk, rc(0))], [(da_rw, rc(0))],
                 [(dgate, rc(0))]]
    dpk, dpsl_parts, d_w0, d_a0, d_k_k, d_k_a, d_w2p, d_a2p, d_g2 = rowvjp(
        "rw_prep_bwd", fn_rprep, rprep_ins, rprep_cts, [0, 1, 2, 3, 4, 5, 6, 7, 8], rows=seq, ncol=nb)
    (dpsl,) = rowcall("rw_prep_sum", fn_sumcols(nb), [(dpsl_parts, "row")], [(seq, 2 * LANES, f32, "row")], rows=seq)

    def lerp_bwd(tag, p, p_off, mu, mu_off, ct_lists, ncol):
        return rowvjp("rw_lerp_bwd_" + tag, fn_lerp, [(p, rc(p_off)), (mu, cc(LANES, mu_off))], [ct_lists], [0, 1],
                      rows=seq, tile=seq, ncol=ncol, dup16=[0])

    _, dmu_r, dpr16 = lerp_bwd("r", p_main, rb0, mu_main, 0, [(dr_a, rc(0)), (dr_b, rc(0))], nb)
    _, dmu_k, dpk16 = lerp_bwd("k", p_main, rb0 + nb, mu_main, nb, [(dpk, rc(0))], nb)
    _, dmu_v, dpv16 = lerp_bwd("v", p_main, rb0 + 2 * nb, mu_main, 2 * nb, [(dv_a, rc(0)), (dv_b, rc(0))], nb)
    _, dmu_s, dps12_16 = lerp_bwd("small", p_small, eb_l1, mu_small, 0, [(dpsl, rc(0))], 2)

    do_raw, dz, d_dn_nw, dz16 = rowvjp("gdn_post_bwd", fn_gpost, gpost_ins, [[(do_cat, rc(0))]], [0, 1, 2],
                                       rows=seq, ncol=nh, dup16=[1])
    dq_g, dk_g, dv_g, dg_b, dbeta_b = scan_bwd("gdn_scan_bwd", gdn_chunk, gdn_ins, gdn_states, gdn_kept, do_raw, 0,
                                               rows=seq, chunk=GDN_CHUNK, ncol=nh)
    dps0, d_alog, d_dtb, dps0_16 = rowvjp("gdn_gate_bwd", fn_ggate, gate_ins, [[(dg_b, "row")], [(dbeta_b, "row")]],
                                          [0, 1, 2], rows=seq, dup16=[0])
    dqkv = jnp.concatenate([dq_g, dk_g, dv_g], axis=1)
    _, d_conv_w, dqkv16 = rowvjp("gdn_conv_bwd", fn_gconv, [(p_main, rc(0)), (conv_w, cc(LANES, 0))],
                                 [[(dqkv, rc(0))]], [0, 1], rows=seq, tile=seq, ncol=3 * nh, dup16=[0])

    dp16 = jnp.concatenate([dqkv16, dz16, dps0_16, dpr16, dpk16, dpv16, dps12_16,
                            jnp.zeros((seq, (n_ext - eb_l1 - 2) * LANES), bf16)], axis=1)
    g_w_ext = mm("in_proj_dw", u16, dp16, ta=True, out_dtype=bf16)
    g_w_in = lane_regroup(
        "w_in_grad_regroup", g_w_ext, tbl_bwd,
        lambda p, b: pl.BlockSpec((d, LANES), lambda i, j, t: (0, t[REGROUP_FIELDS * p + 1, j * sub_b + b])),
        pl.BlockSpec((None, d, sub_b * LANES),
                     lambda i, j, t: (j // (per_shard // sub_b), 0, j % (per_shard // sub_b))),
        (4, d, pad_w), (1, 4 * per_shard // sub_b), n_ext * LANES, sub_b)
    first_grads = dict(w_in=g_w_in, dn_conv_w=d_conv_w, rw_w2=d_w2p[:lw_dim], rw_a2=d_a2p[lw_dim:], rw_g2=d_g2)
    first_g, tok = chips_start("scatter_first_start", [by_chip(n, first_grads[n]) for n in first], _chip_block)

    du = mm("in_proj_dx", dp16, w_ext, tb=True, after=tok)
    grad_x, d_mix_nw = rowvjp("norm_mix_bwd", fn_rms, [(xs, "row"), (mix_norm_w, "const")], [[(du, "row")]],
                              [0, 1], rows=seq, adds=[(0, dh1, "row")])
    received = dict(zip(mid, chips_wait("scatter_mid_wait", mid_g, grad_x)))
    received["ffn_w2"] = chips_wait("scatter_ffn_w2_wait", ffn_w2_g, grad_x)[0]
    received["ffn_w1"] = chips_wait("scatter_ffn_w1_wait", ffn_w1_g, grad_x)[0]

    out = {}

    def reduce_and_update(tag, group):
        partial = [sum_slots("sum_chips_" + n, received[n]) for n in group]
        other = swap_sibling("swap_sibling_" + tag, partial)
        for n, p_mine, p_other in zip(group, partial, other):
            if weights[n].shape[2] % LANES:
                rows, cols = weights[n].shape[1:]
                lin = lambda a: jnp.swapaxes(a, 1, 2).reshape(-1, LANES)
                lin_g = lambda p: p.T[:cols].reshape(-1, LANES)
                res = adamw("adamw_" + n, lin(weights[n]), [lin_g(p_mine), lin_g(p_other)], lin(mom_m[n]),
                            lin(mom_v[n]))
                out[n] = [jnp.swapaxes(r.reshape(1, cols, rows), 1, 2) for r in res]
            else:
                out[n] = adamw("adamw_" + n, weights[n], [p_mine, p_other], mom_m[n], mom_v[n])

    reduce_and_update("rest", mid + late)
    received.update(zip(first, chips_wait("scatter_first_wait", first_g, [out[n][1] for n in mid + late])))
    reduce_and_update("first", first)

    small_names = [n for n in names if n not in sharded]
    small_local = dict(
        mix_norm_w=d_mix_nw, dn_a_log=d_alog[:, :nh], dn_dt_bias=d_dtb[:, :nh], dn_norm_w=d_dn_nw,
        rw_mu=jnp.concatenate([dmu_r, dmu_k, dmu_v, dmu_s], axis=1), rw_w0=d_w0, rw_a0=d_a0, rw_k_k=d_k_k,
        rw_k_a=d_k_a, rw_r_k=d_r_k, rw_ln_w=d_ln_w, rw_ln_b=d_ln_b, xa_norm_w=d_xa_nw, mem_norm_w=d_mem_nw,
        ffn_norm_w=d_ffn_nw, final_norm_w=d_fnw)
    loss_vec = jnp.where(jnp.arange(LANES) == 0, loss_rows[0], 0.0)
    (g8,) = gather_all([_pack([small_local[n] for n in small_names] + [loss_vec])], after=[out["w_in"][1]])

    packed_like = [weights[n] for n in small_names] + [loss_vec]
    zero = jnp.zeros((LANES,), f32)
    res = adamw_packed(_pack([weights[n] for n in small_names] + [zero]), g8,
                       _pack([mom_m[n] for n in small_names] + [zero]),
                       _pack([mom_v[n] for n in small_names] + [zero]))
    unpacked = [_unpack(r, packed_like) for r in res]
    for i, n in enumerate(small_names):
        out[n] = [u[i] for u in unpacked]
    loss = unpacked[0][-1][0]

    return (loss, grad_x.reshape(x.shape), *[out[n][0] for n in names], *[out[n][1] for n in names],
            *[out[n][2] for n in names], *[out[n][3] for n in names])
```

```python
import functools

import jax
import jax.numpy as jnp
from jax import lax
from jax.experimental import pallas as pl
from jax.experimental.pallas import tpu as pltpu

f32 = jnp.float32
bf16 = jnp.bfloat16
HI = lax.Precision.HIGHEST
MESH = pl.DeviceIdType.MESH

LANES = 128
VMEM_LIMIT = 56 << 20
TOK_TILE = 256
TOK_TILE_BLOCKED = 1024
MM_TILE = 1024
MM_TILE_K = 2048
GDN_CHUNK = 128
WKV_CHUNK = 64
SCAN_GROUP = 8
P_BULK = 1
P_INV = 1
P_RESID = 3
P_CUMSUM = 3
RMS_EPS = 1e-6
RW_GN_EPS = 64e-5
RW_HEAD = 64

ADAM_LR, ADAM_B1, ADAM_B2, ADAM_EPS, ADAM_WD, ADAM_STEP = 0.001, 0.9, 0.999, 1e-08, 0.01, 10


def _params(n_grid):
    return pltpu.CompilerParams(dimension_semantics=("arbitrary",) * n_grid, vmem_limit_bytes=VMEM_LIMIT)


_DIMS = {"nn": (((1,), (0,)), ((), ())), "nt": (((1,), (1,)), ((), ())), "tn": (((0,), (0,)), ((), ()))}
_DIMS_BATCHED = {"nn": (((2,), (1,)), ((0,), (0,))), "nt": (((2,), (2,)), ((0,), (0,))),
                 "tn": (((1,), (1,)), ((0,), (0,)))}


def _raw_dot(a, b, mode, passes):
    dims = (_DIMS if a.ndim == 2 else _DIMS_BATCHED)[mode]
    if passes == 6:
        return lax.dot_general(a.astype(f32), b.astype(f32), dims, precision=HI, preferred_element_type=f32)
    ah, bh = a.astype(bf16), b.astype(bf16)
    r = lax.dot_general(ah, bh, dims, preferred_element_type=f32)
    if passes == 3:
        al = (a - ah.astype(f32)).astype(bf16)
        bl = (b - bh.astype(f32)).astype(bf16)
        r = r + lax.dot_general(al, bh, dims, preferred_element_type=f32)
        r = r + lax.dot_general(ah, bl, dims, preferred_element_type=f32)
    return r


@functools.partial(jax.custom_vjp, nondiff_argnums=(2, 3))
def pdot(a, b, mode, passes):
    return _raw_dot(a, b, mode, passes)


def _pdot_bwd(mode, passes, res, g):
    a, b = res
    if mode == "nn":
        da, db = _raw_dot(g, b, "nt", passes), _raw_dot(a, g, "tn", passes)
    elif mode == "nt":
        da, db = _raw_dot(g, b, "nn", passes), _raw_dot(g, a, "tn", passes)
    else:
        da, db = _raw_dot(b, g, "nt", passes), _raw_dot(a, g, "nn", passes)
    return da.astype(a.dtype), db.astype(b.dtype)


pdot.defvjp(lambda a, b, mode, passes: (_raw_dot(a, b, mode, passes), (a, b)), _pdot_bwd)


def bdot(a, b):
    return pdot(a, b, "nn", 1)


def bdot_nt(a, b):
    return pdot(a, b, "nt", 1)


def _shift_rows(x, k):
    row = lax.broadcasted_iota(jnp.int32, x.shape, 0)
    return jnp.where(row < k, 0.0, pltpu.roll(x, k, axis=0))


def _unshift_rows(g, k):
    n = g.shape[0]
    row = lax.broadcasted_iota(jnp.int32, g.shape, 0)
    return jnp.where(row >= n - k, 0.0, pltpu.roll(g, n - k, axis=0))


@functools.partial(jax.custom_vjp, nondiff_argnums=(1,))
def tshift(x, k):
    return _shift_rows(x, k)


tshift.defvjp(lambda x, k: (_shift_rows(x, k), None), lambda k, _, g: (_unshift_rows(g, k),))


def rms(x, w):
    x = x.astype(f32)
    return x * lax.rsqrt(jnp.mean(x * x, axis=-1, keepdims=True) + RMS_EPS) * w


def softplus(x):
    return jnp.maximum(x, 0.0) + jnp.log(1.0 + jnp.exp(-jnp.abs(x)))


def seg2sum(x):
    lo = lax.broadcasted_iota(jnp.int32, x.shape, 1) < RW_HEAD
    s_lo = jnp.sum(jnp.where(lo, x, 0.0), axis=-1, keepdims=True)
    s_hi = jnp.sum(jnp.where(lo, 0.0, x), axis=-1, keepdims=True)
    return jnp.where(lo, s_lo, s_hi)


def _tile(n, pref):
    if n <= pref:
        return n
    t = pref
    while t >= LANES:
        if n % t == 0:
            return t
        t -= LANES
    return n


def mm(name, a, b, *, ta=False, tb=False, add=None, out_dtype=f32, by_chip=False, epilogue=None, extra=(),
       out_dtypes=None, after=None):
    (k, m) = a.shape if ta else a.shape[::-1]
    (n, kb) = b.shape if tb else b.shape[::-1]
    assert k == kb, (name, a.shape, b.shape)
    tm, tk = _tile(m, MM_TILE), _tile(k, MM_TILE_K)
    tn = _tile(n // 4, MM_TILE) if by_chip else _tile(n, MM_TILE)
    nk = k // tk
    dims = (((0,) if ta else (1,), (1,) if tb else (0,)), ((), ()))
    extra = list(extra) + ([] if add is None else [add])
    out_dtypes = [out_dtype] if out_dtypes is None else list(out_dtypes)
    n_extra, n_out = len(extra), len(out_dtypes)
    n_after = 0 if after is None else 1

    def body(*refs):
        a_ref, b_ref = refs[:2]
        x_refs = refs[2:2 + n_extra]
        o_refs = refs[2 + n_extra + n_after:2 + n_extra + n_after + n_out]
        part = lax.dot_general(a_ref[...].astype(bf16), b_ref[...].astype(bf16), dims, preferred_element_type=f32)

        def finish(r):
            xs = [x[...] for x in x_refs]
            if add is not None:
                r = r + xs.pop().astype(f32)
            outs = (r,) if epilogue is None else epilogue(r, *xs)
            for o_ref, o in zip(o_refs, outs):
                o_ref[...] = o.astype(o_ref.dtype)

        if nk == 1:
            finish(part)
            return
        acc = refs[-1]
        kk = pl.program_id(2)

        @pl.when(kk == 0)
        def _():
            acc[...] = part

        @pl.when(kk > 0)
        def _():
            acc[...] += part

        @pl.when(kk == nk - 1)
        def _():
            finish(acc[...])

    a_spec = pl.BlockSpec((tk, tm), lambda i, j, q: (q, i)) if ta else pl.BlockSpec((tm, tk), lambda i, j, q: (i, q))
    b_spec = pl.BlockSpec((tn, tk), lambda i, j, q: (j, q)) if tb else pl.BlockSpec((tk, tn), lambda i, j, q: (q, j))
    x_spec = pl.BlockSpec((tm, tn), lambda i, j, q: (i, j))
    if by_chip:
        per_chip = n // 4 // tn
        o_spec = pl.BlockSpec((None, tm, tn), lambda i, j, q: (j // per_chip, i, j % per_chip))
        o_shape = (4, m, n // 4)
    else:
        o_spec, o_shape = x_spec, (m, n)
    afters = [] if after is None else [jnp.reshape(after, (1, 1))]
    res = pl.pallas_call(
        body, name=name, grid=(m // tm, n // tn, nk),
        in_specs=[a_spec, b_spec] + [x_spec] * n_extra + [pl.BlockSpec(memory_space=pl.ANY)] * n_after,
        out_specs=[o_spec] * n_out, out_shape=[jax.ShapeDtypeStruct(o_shape, dt) for dt in out_dtypes],
        scratch_shapes=[pltpu.VMEM((tm, tn), f32)] if nk > 1 else [],
        compiler_params=_params(3))(a, b, *extra, *afters)
    return res[0] if n_out == 1 else res


REGROUP_PIECES = 4


REGROUP_SUB = 5
REGROUP_FIELDS = 5


def _regroup_table(n_out, sources_of):
    import numpy as np
    tbl = np.zeros((REGROUP_FIELDS * REGROUP_PIECES, n_out), np.int32)
    for j in range(n_out):
        groups = sorted(sources_of(j).items())
        assert len(groups) <= REGROUP_PIECES, (j, len(groups))
        for p in range(REGROUP_PIECES):
            if p < len(groups):
                key, lanes = groups[p]
                shifts = {q - s for s, q in lanes}
                qs = sorted(q for _, q in lanes)
                assert len(shifts) == 1 and qs == list(range(qs[0], qs[-1] + 1)), (j, key)
                row = (key[0], key[1], shifts.pop(), qs[0], qs[-1] + 1)
            else:
                row = (0, 0, 0, 0, 0)
            tbl[REGROUP_FIELDS * p:REGROUP_FIELDS * (p + 1), j] = row
    return jnp.asarray(tbl)


def lane_regroup(name, src, table, src_spec, out_spec, out_shape, grid, src_width, sub):
    rows = src.shape[-2]
    n_src = REGROUP_PIECES * sub

    def body(tbl, *refs):
        o_ref = refs[n_src]
        step = pl.program_id(1)

        def moved(b, p):
            j = step * sub + b
            blk, shift, lo, hi = (tbl[REGROUP_FIELDS * p + f, j] for f in range(1, REGROUP_FIELDS))
            x = refs[b * REGROUP_PIECES + p][...]
            if src_width % LANES:
                lane = lax.broadcasted_iota(jnp.int32, (rows, LANES), 1)
                x = jnp.where(lane < src_width - blk * LANES, x, jnp.zeros((), src.dtype))
            pi = lax.broadcasted_iota(jnp.int32, (LANES, LANES), 0)
            qi = lax.broadcasted_iota(jnp.int32, (LANES, LANES), 1)
            sel = jnp.logical_and(qi - pi == shift, jnp.logical_and(qi >= lo, qi < hi))
            return jnp.dot(x, sel.astype(src.dtype), preferred_element_type=f32).astype(o_ref.dtype)

        for b in range(sub):
            lanes = slice(b * LANES, (b + 1) * LANES)
            o_ref[:, lanes] = moved(b, 0)
            for p in range(1, REGROUP_PIECES):
                j = step * sub + b

                @pl.when(tbl[REGROUP_FIELDS * p + 4, j] > tbl[REGROUP_FIELDS * p + 3, j])
                def _(b=b, p=p, lanes=lanes):
                    o_ref[:, lanes] += moved(b, p)

    return pl.pallas_call(
        body, name=name, out_shape=jax.ShapeDtypeStruct(out_shape, src.dtype),
        grid_spec=pltpu.PrefetchScalarGridSpec(
            num_scalar_prefetch=1, grid=grid,
            in_specs=[src_spec(p, b) for b in range(sub) for p in range(REGROUP_PIECES)], out_specs=out_spec),
        compiler_params=_params(2))(table, *[src] * n_src)


def w_in_layout(d_cols, shard_w, c_split, gap, n_blocks):
    def ext_of(c):
        return c if c < c_split else c + gap

    def fwd_sources(j):
        groups = {}
        for q in range(LANES):
            e = j * LANES + q
            c = e if e < c_split else e - gap
            if (c_split <= e < c_split + gap) or c >= d_cols:
                continue
            s, l = divmod(c, shard_w)
            groups.setdefault((s, l // LANES), []).append((l % LANES, q))
        return groups

    per_shard = -(-shard_w // LANES)

    def bwd_sources(j):
        s, b = divmod(j, per_shard)
        groups = {}
        for q in range(LANES):
            l = b * LANES + q
            if l >= shard_w:
                continue
            e = ext_of(s * shard_w + l)
            groups.setdefault((0, e // LANES), []).append((e % LANES, q))
        return groups

    return _regroup_table(n_blocks, fwd_sources), _regroup_table(4 * per_shard, bwd_sources), per_shard


def _in_spec(a, kind, tile):
    if kind == "row":
        return pl.BlockSpec((tile, a.shape[1]), lambda j, i: (i, 0))
    if kind == "const":
        return pl.BlockSpec(a.shape, lambda j, i: (0, 0))
    tag, cw, off = kind
    if tag == "rowc":
        return pl.BlockSpec((tile, cw), lambda j, i: (i, j + off))
    assert tag == "constc", kind
    return pl.BlockSpec((a.shape[0], cw), lambda j, i: (0, j + off))


def rowcall(name, fn, ins, outs, *, rows, tile=None, ncol=1):
    if tile is None:
        tile = min(TOK_TILE_BLOCKED if ncol > 1 else TOK_TILE, rows)
    n_in = len(ins)
    kinds = [o[3] for o in outs]

    def body(*refs):
        j, i = pl.program_id(0), pl.program_id(1)
        res = fn(*[r[...] for r in refs[:n_in]])
        for ref, val, kind in zip(refs[n_in:], res, kinds):
            if kind in ("row", "rowc"):
                ref[...] = val.astype(ref.dtype)
            else:
                first = (i == 0) if kind == "accc" else jnp.logical_and(i == 0, j == 0)

                @pl.when(first)
                def _(ref=ref, val=val):
                    ref[...] = val.astype(ref.dtype)

                @pl.when(jnp.logical_not(first))
                def _(ref=ref, val=val):
                    ref[...] += val.astype(ref.dtype)

    out_shape, out_specs = [], []
    for nr, nc, dtype, kind in outs:
        out_shape.append(jax.ShapeDtypeStruct((nr, nc), dtype))
        if kind == "row":
            out_specs.append(pl.BlockSpec((tile, nc), lambda j, i: (i, 0)))
        elif kind == "rowc":
            out_specs.append(pl.BlockSpec((tile, nc // ncol), lambda j, i: (i, j)))
        elif kind == "acc":
            out_specs.append(pl.BlockSpec((nr, nc), lambda j, i: (0, 0)))
        else:
            out_specs.append(pl.BlockSpec((nr, nc // ncol), lambda j, i: (0, j)))
    return pl.pallas_call(
        body, name=name, grid=(ncol, rows // tile), in_specs=[_in_spec(a, k, tile) for a, k in ins],
        out_specs=out_specs, out_shape=out_shape, compiler_params=_params(2))(*[a for a, _ in ins])


def rowvjp(name, fn, ins, cts, grads, *, rows, tile=None, ncol=1, adds=(), dup16=()):
    n_in = len(ins)
    ct_sizes = [len(c) for c in cts]
    flat_cts = [m for c in cts for m in c]
    n_ct = len(flat_cts)

    def wrapped(*vals):
        xs = list(vals[:n_in])
        gs = vals[n_in:n_in + n_ct]
        extra = vals[n_in + n_ct:]

        def f(*dvars):
            full = list(xs)
            for k, v in zip(grads, dvars):
                full[k] = v
            return fn(*full)

        outs, pull = jax.vjp(f, *[xs[k] for k in grads])
        cot, p = [], 0
        for o, size in zip(outs, ct_sizes):
            g = gs[p].astype(f32)
            for q in range(1, size):
                g = g + gs[p + q].astype(f32)
            cot.append(g.astype(o.dtype))
            p += size
        gv = list(pull(tuple(cot)))
        for (pos, _, _), e in zip(adds, extra):
            gv[pos] = gv[pos] + e.astype(gv[pos].dtype)
        return tuple(gv) + tuple(gv[pos] for pos in dup16)

    outs = []
    for k in grads:
        a, kind = ins[k]
        if kind == "row":
            outs.append((rows, a.shape[1] * ncol, f32, "rowc") if ncol > 1 else (rows, a.shape[1], f32, "row"))
        elif kind == "const":
            outs.append((a.shape[0], a.shape[1], f32, "acc"))
        elif kind[0] == "rowc":
            outs.append((rows, kind[1] * ncol, f32, "rowc"))
        else:
            outs.append((a.shape[0], kind[1] * ncol, f32, "accc"))
    for pos in dup16:
        nr, nc, _, kind = outs[pos]
        outs.append((nr, nc, bf16, kind))
    all_ins = list(ins) + flat_cts + [(a, kind) for _, a, kind in adds]
    return rowcall(name, wrapped, all_ins, outs, rows=rows, tile=tile, ncol=ncol)


def fn_rms(x, w):
    return (rms(x, w),)


def make_fn_gconv(n_norm_blocks):
    def fn(p, cw):
        c = cw[3:4] * p
        for jj in range(3):
            c = c + cw[jj:jj + 1] * tshift(p, 3 - jj)
        s = c * jax.nn.sigmoid(c)
        nrm = s * lax.rsqrt(jnp.sum(s * s, axis=-1, keepdims=True) + 1e-6)
        return (jnp.where(pl.program_id(0) < n_norm_blocks, nrm, s),)
    return fn


def fn_ggate(ps0, alog, dtb, e_g, e_b):
    g = -jnp.exp(alog) * softplus(ps0 + dtb)
    beta = jax.nn.sigmoid(ps0)
    return pdot(g, e_g, "nn", 6), pdot(beta, e_b, "nn", 6)


def fn_gpost(o, z, nw):
    return (rms(o, nw) * (z * jax.nn.sigmoid(z)),)


def fn_lerp(p, mu):
    return (p + (tshift(p, 1) - p) * mu,)


def fn_rprep(pk, psl, w0, a0, k_k, k_a, w2p, a2p, g2):
    g1, g2in = psl[:, :LANES], psl[:, LANES:]
    log_w = -softplus(-(w0 + bdot(jnp.tanh(g1), w2p))) - 0.5
    lw = -jnp.exp(log_w)
    a = jax.nn.sigmoid(a0 + bdot(g1, a2p))
    gate = bdot(jax.nn.sigmoid(g2in), g2)
    kkr = pk * k_k
    kk = kkr / jnp.maximum(jnp.sqrt(seg2sum(kkr * kkr)), 1e-12)
    kmod = pk * (1.0 + (a - 1.0) * k_a)
    return lw, kmod, kk, a, gate


def fn_rpost(y, r, kmod, v, gate, r_k, ln_w, ln_b):
    inv_n = 1.0 / RW_HEAD
    mean = seg2sum(y) * inv_n
    d = y - mean
    var = seg2sum(d * d) * inv_n
    yn = d * lax.rsqrt(var + RW_GN_EPS) * ln_w + ln_b
    bonus = seg2sum(r * kmod * r_k) * v
    return ((yn + bonus) * gate,)


def fn_xcore(q, k, v):
    s = bdot_nt(q, k) * (LANES ** -0.5)
    p = jax.nn.softmax(s, axis=-1)
    return (bdot(p, v),)


def fn_final(h, tgt, w):
    def loss_fn(h, w):
        return 0.5 * jnp.sum(jnp.mean(jnp.square(rms(h, w) - tgt), axis=-1))

    val, (dh, dw) = jax.value_and_grad(loss_fn, argnums=(0, 1))(h, w)
    return dh, dh.astype(bf16), dw, jnp.full((8, LANES), val, f32)


def fn_sumcols(n):
    def fn(x):
        w = x.shape[1] // n
        s = x[:, :w]
        for q in range(1, n):
            s = s + x[:, q * w:(q + 1) * w]
        return (s,)
    return fn


def _tri(c):
    ii = lax.broadcasted_iota(jnp.int32, (c, c), 0)
    jj = lax.broadcasted_iota(jnp.int32, (c, c), 1)
    return ii, jj


def _neumann_raw(m, steps):
    c = m.shape[-1]
    ii, jj = _tri(c)
    eye = (ii == jj).astype(f32)
    t, p = eye + m, m
    for _ in range(steps):
        p = _raw_dot(p, p, "nn", P_INV)
        t = _raw_dot(t, eye + p, "nn", P_INV)
    resid = eye - t + _raw_dot(m, t, "nn", P_RESID)
    return t + _raw_dot(t, resid, "nn", P_INV)


@functools.partial(jax.custom_vjp, nondiff_argnums=(1,))
def _neumann_inverse(m, steps):
    return _neumann_raw(m, steps)


def _neumann_fwd(m, steps):
    t = _neumann_raw(m, steps)
    return t, t


def _neumann_bwd(steps, t, g):
    return (_raw_dot(_raw_dot(t, g, "tn", P_RESID), t, "nt", P_RESID),)


_neumann_inverse.defvjp(_neumann_fwd, _neumann_bwd)


@jax.custom_vjp
def _known_inverse(m, t):
    return t


_known_inverse.defvjp(lambda m, t: (t, t), lambda t, g: (_neumann_bwd(0, t, g)[0], jnp.zeros_like(t)))


def _inverse(m, steps, kept):
    return _neumann_inverse(m, steps) if kept is None else _known_inverse(m, kept)


def cumsum_rows(x):
    t = x.shape[1]
    ii, jj = _tri(t)
    tri = jnp.broadcast_to((ii >= jj).astype(f32), (x.shape[0], t, t))
    return pdot(tri, x, "nn", P_CUMSUM)


def gdn_chunk(q, k, v, gb, bb, s, kept=None):
    c = q.shape[1]
    ii, jj = _tri(c)
    low = ii >= jj
    gcb = cumsum_rows(gb)
    gl = jnp.sum(gb, axis=1, keepdims=True)
    gc_col = gcb[:, :, :c]
    diff = gc_col - jnp.swapaxes(gc_col, 1, 2)
    decay = jnp.where(low, jnp.exp(jnp.where(low, diff, 0.0)), 0.0)
    qs = q * (q.shape[2] ** -0.5)
    kb = k * bb
    with_k = pdot(jnp.concatenate([kb, qs], axis=1), k, "nt", P_BULK)
    a = jnp.where(ii > jj, with_k[:, :c] * decay, 0.0)
    attn = with_k[:, c:] * decay
    t = _inverse(-a, c.bit_length() - 2, kept)
    eg = jnp.exp(gcb)
    uw = pdot(t, jnp.concatenate([v * bb, kb * eg], axis=2), "nn", P_BULK)
    u, w = uw[:, :, :LANES], uw[:, :, LANES:]
    kd = k * jnp.exp(gl - gcb)
    from_state = pdot(jnp.concatenate([w, qs * eg], axis=1), s, "nn", P_BULK)
    v_new = u - from_state[:, :c]
    o = from_state[:, c:] + pdot(attn, v_new, "nn", P_BULK)
    s_new = s * jnp.exp(gl) + pdot(kd, v_new, "tn", P_BULK)
    return o, s_new, t


def wkv_chunk(r, lw, k, v, kk, a, s, kept=None):
    t = r.shape[1]
    ii, jj = _tri(t)
    lo = lax.broadcasted_iota(jnp.int32, r.shape, 2) < RW_HEAD
    cl = cumsum_rows(lw)
    cl_last = jnp.sum(lw, axis=1, keepdims=True)
    al = -kk * jnp.exp(cl - lw)
    be = (a * kk) * jnp.exp(-cl)
    kt = k * jnp.exp(-cl)
    rt = r * jnp.exp(cl)

    def dot(xa, xb, mode="nn"):
        return pdot(xa, xb, mode, P_BULK)

    def sel(x_lo, x_hi):
        return jnp.where(lo, x_lo, x_hi)

    left = jnp.concatenate([jnp.where(lo, al, 0.0), jnp.where(lo, 0.0, al),
                            jnp.where(lo, rt, 0.0), jnp.where(lo, 0.0, rt)], axis=1)
    with_be, with_kt = dot(left, be, "nt"), dot(left, kt, "nt")

    def blocks(prod):
        below, upto = ii > jj, ii >= jj
        return [jnp.where(msk, prod[:, q * t:(q + 1) * t], 0.0) for q, msk in enumerate((below, below, upto, upto))]

    ab_lo, ab_hi, rb_lo, rb_hi = blocks(with_be)
    ak_lo, ak_hi, rk_lo, rk_hi = blocks(with_kt)
    from_state = dot(jnp.concatenate([al, rt], axis=1), s, "nt")
    x = from_state[:, :t] + sel(dot(ak_lo, v), dot(ak_hi, v))
    steps = t.bit_length() - 2
    inv_lo = _inverse(ab_lo, steps, None if kept is None else kept[:, :t])
    inv_hi = _inverse(ab_hi, steps, None if kept is None else kept[:, t:])
    u = sel(dot(inv_lo, x), dot(inv_hi, x))
    y = from_state[:, t:] + sel(dot(rb_lo, u) + dot(rk_lo, v), dot(rb_hi, u) + dot(rk_hi, v))
    vi = lax.broadcasted_iota(jnp.int32, s.shape, 1) < RW_HEAD
    ki = lax.broadcasted_iota(jnp.int32, s.shape, 2) < RW_HEAD
    s_new = jnp.where(vi == ki, (s + dot(u, be, "tn") + dot(v, kt, "tn")) * jnp.exp(cl_last), 0.0)
    return y, s_new, jnp.concatenate([inv_lo, inv_hi], axis=1)


def _scan_group(ncol, offs):
    g = SCAN_GROUP
    while g > 1 and (ncol % g or any(o % g for o in offs)):
        g //= 2
    return g


def scan_fwd(name, chunk_fn, ins, *, rows, chunk, ncol):
    n = rows // chunk
    n_in = len(ins)
    grp = _scan_group(ncol, [off for _, off in ins])
    tile = jax.ShapeDtypeStruct((grp, chunk, LANES), f32)
    kept = jax.eval_shape(chunk_fn, *[tile] * n_in, jax.ShapeDtypeStruct((grp, LANES, LANES), f32))[2].shape[1:]

    def body(*refs):
        o_ref, st_ref, kept_ref, s_scr = refs[n_in:]

        @pl.when(pl.program_id(1) == 0)
        def _():
            s_scr[...] = jnp.zeros_like(s_scr)

        cols = [slice(b * LANES, (b + 1) * LANES) for b in range(grp)]
        s = s_scr[...]
        st_ref[...] = s
        o, s_new, inv = chunk_fn(*[jnp.stack([r[:, c] for c in cols]) for r in refs[:n_in]], s)
        for b, c in enumerate(cols):
            o_ref[:, c] = o[b]
        kept_ref[...] = inv
        s_scr[...] = s_new

    def spec(off):
        return pl.BlockSpec((chunk, grp * LANES), lambda h, c: (c, h + off // grp))

    def per_chunk(shape):
        return pl.BlockSpec((grp, None) + tuple(shape), lambda h, c: (h, c, 0, 0))

    return pl.pallas_call(
        body, name=name, grid=(ncol // grp, n), in_specs=[spec(off) for _, off in ins],
        out_specs=[spec(0), per_chunk((LANES, LANES)), per_chunk(kept)],
        out_shape=[jax.ShapeDtypeStruct((rows, ncol * LANES), f32),
                   jax.ShapeDtypeStruct((ncol, n, LANES, LANES), f32),
                   jax.ShapeDtypeStruct((ncol, n) + tuple(kept), f32)],
        scratch_shapes=[pltpu.VMEM((grp, LANES, LANES), f32)], compiler_params=_params(2))(*[a for a, _ in ins])


def scan_bwd(name, chunk_fn, ins, states, kept, d_out, d_off, *, rows, chunk, ncol):
    n = rows // chunk
    n_in = len(ins)
    grp = _scan_group(ncol, [off for _, off in ins] + [d_off])

    def body(*refs):
        st_ref, kept_ref, do_ref = refs[n_in:n_in + 3]
        g_refs = refs[n_in + 3:2 * n_in + 3]
        ds_scr = refs[-1]

        @pl.when(pl.program_id(1) == 0)
        def _():
            ds_scr[...] = jnp.zeros_like(ds_scr)

        cols = [slice(b * LANES, (b + 1) * LANES) for b in range(grp)]

        def batch(ref):
            return jnp.stack([ref[:, c] for c in cols])

        inv = kept_ref[...]
        _, pull = jax.vjp(lambda *a: chunk_fn(*a, kept=inv)[:2], *[batch(r) for r in refs[:n_in]], st_ref[...])
        gs = pull((batch(do_ref), ds_scr[...]))
        for ref, g in zip(g_refs, gs[:n_in]):
            for b, c in enumerate(cols):
                ref[:, c] = g[b]
        ds_scr[...] = gs[n_in]

    def spec(off):
        return pl.BlockSpec((chunk, grp * LANES), lambda h, c: (n - 1 - c, h + off // grp))

    def per_chunk(shape):
        return pl.BlockSpec((grp, None) + tuple(shape), lambda h, c: (h, n - 1 - c, 0, 0))

    return pl.pallas_call(
        body, name=name, grid=(ncol // grp, n),
        in_specs=[spec(off) for _, off in ins] + [per_chunk(states.shape[2:]), per_chunk(kept.shape[2:]), spec(d_off)],
        out_specs=[spec(0)] * n_in, out_shape=[jax.ShapeDtypeStruct((rows, ncol * LANES), f32)] * n_in,
        scratch_shapes=[pltpu.VMEM((grp, LANES, LANES), f32)],
        compiler_params=_params(2))(*[a for a, _ in ins], states, kept, d_out)


def flip_exchange(name, arrs, flips, n_slots, slot_of, src_of, with_self, after=()):
    n = len(arrs)
    nf = len(flips)
    n_after = len(after)

    def body(*refs):
        ins, outs = refs[:n], refs[n + n_after:2 * n + n_after]
        send, recv, lsem = refs[2 * n + n_after:]
        me = (lax.axis_index("x"), lax.axis_index("y"), lax.axis_index("c"))
        copies = []
        for k in range(n):
            if with_self:
                cp = pltpu.make_async_copy(src_of(ins[k], me), outs[k].at[slot_of(me)], lsem.at[k])
                cp.start()
                copies.append(cp)
            for j, fl in enumerate(flips):
                peer = tuple(1 - m if f else m for m, f in zip(me, fl))
                cp = pltpu.make_async_remote_copy(
                    src_ref=src_of(ins[k], peer), dst_ref=outs[k].at[slot_of(me)], send_sem=send.at[k, j],
                    recv_sem=recv.at[k, j], device_id=peer, device_id_type=MESH)
                cp.start()
                copies.append(cp)
        for cp in copies:
            cp.wait()

    def out_sds(a):
        blk = src_of(jax.ShapeDtypeStruct(a.shape, a.dtype), None)
        return jax.ShapeDtypeStruct((n_slots,) + tuple(blk), a.dtype)

    any_spec = pl.BlockSpec(memory_space=pl.ANY)
    return pl.pallas_call(
        body, name=name, in_specs=[any_spec] * (n + n_after), out_specs=[any_spec] * n,
        out_shape=[out_sds(a) for a in arrs],
        scratch_shapes=[pltpu.SemaphoreType.DMA((n, nf)), pltpu.SemaphoreType.DMA((n, nf)),
                        pltpu.SemaphoreType.DMA((n,))],
        compiler_params=pltpu.CompilerParams(has_side_effects=True))(*arrs, *after)


_CHIP_FLIPS = ((1, 0, 0), (0, 1, 0), (1, 1, 0))
_ALL_FLIPS = ((0, 0, 1), (0, 1, 0), (0, 1, 1), (1, 0, 0), (1, 0, 1), (1, 1, 0), (1, 1, 1))


def _whole(ref, pos):
    return ref.shape if pos is None else ref


def _chip_block(ref, pos):
    return ref.shape[1:] if pos is None else ref.at[2 * pos[0] + pos[1]]


def _chip_slot(p):
    return 2 * p[0] + p[1]


_HBM = pl.BlockSpec(memory_space=pltpu.HBM)
_SEM = pl.BlockSpec(memory_space=pltpu.SEMAPHORE)
_DATAFLOW = pltpu.SideEffectType.DATAFLOW_SIDE_EFFECTING


def _split_copies(mode, refs, n, send, recv):
    me = (lax.axis_index("x"), lax.axis_index("y"), lax.axis_index("c"))
    sib = (me[0], me[1], 1 - me[2])
    lands = refs[:n] if mode == "handover" else refs[n:2 * n]
    copies = []
    for k, land in enumerate(lands):
        half = land.shape[1] // 2
        mine = pl.ds(pl.multiple_of(me[2] * half, 16), half)
        for j, fl in enumerate(_CHIP_FLIPS):
            peer = tuple(1 - m if f else m for m, f in zip(me, fl))
            if mode == "gather":
                src, dst, to = refs[k], land.at[_chip_slot(me)], peer
            elif mode == "scatter":
                src, dst, to = refs[k].at[_chip_slot(peer)], land.at[_chip_slot(me)], peer
            elif mode == "gather_half":
                src, dst, to = refs[k].at[mine], land.at[_chip_slot(me), mine], peer
            else:
                src = dst = land.at[_chip_slot(peer), mine]
                to = sib
            q = k * len(_CHIP_FLIPS) + j
            copies.append(pltpu.make_async_remote_copy(src_ref=src, dst_ref=dst, send_sem=send.at[q],
                                                       recv_sem=recv.at[q], device_id=to, device_id_type=MESH))
    return copies


def split_start(name, mode, ops, n):
    ops = [pltpu.with_memory_space_constraint(a, pltpu.HBM) for a in ops]
    m = len(ops)

    def body(*refs):
        for cp in _split_copies(mode, refs[:m], n, refs[m], refs[m + 1]):
            cp.start()
        refs[-1][...] = jnp.zeros_like(refs[-1])

    sems = pltpu.SemaphoreType.DMA((n * len(_CHIP_FLIPS),))
    outs = pl.pallas_call(
        body, name=name, in_specs=[_HBM] * m,
        out_shape=(sems, sems, *[pltpu.HBM(a.shape, a.dtype) for a in ops], jax.ShapeDtypeStruct((8, LANES), f32)),
        out_specs=(_SEM, _SEM, *[_HBM] * m, pl.BlockSpec(memory_space=pltpu.VMEM)),
        input_output_aliases={i: 2 + i for i in range(m)},
        compiler_params=pltpu.CompilerParams(has_side_effects=_DATAFLOW))(*ops)
    return (outs[0], outs[1], list(outs[2:2 + m]), mode, n), outs[-1][0, 0]


def split_wait(name, state, after, with_sources=False):
    send, recv, ops, mode, n = state
    m = len(ops)
    afters = list(after) if isinstance(after, (list, tuple)) else [after]

    def body(*refs):
        for cp in _split_copies(mode, refs[:m], n, refs[m], refs[m + 1]):
            cp.wait_send()
            cp.wait_recv()

    outs = pl.pallas_call(
        body, name=name, in_specs=[_HBM] * m + [_SEM, _SEM] + [pl.BlockSpec(memory_space=pl.ANY)] * len(afters),
        out_shape=tuple(pltpu.HBM(a.shape, a.dtype) for a in ops), out_specs=tuple([_HBM] * m),
        input_output_aliases={i: i for i in range(m)},
        compiler_params=pltpu.CompilerParams(has_side_effects=_DATAFLOW))(*ops, send, recv, *afters)
    return (list(outs[m - n:]), list(outs[:m - n])) if with_sources else list(outs[m - n:])


def chips_start(name, arrs, src_of, halves=False):
    me = _chip_slot((lax.axis_index("x"), lax.axis_index("y")))
    lands = []
    for a in arrs:
        blk = tuple(src_of(jax.ShapeDtypeStruct(a.shape, a.dtype), None))
        land = lax.empty((4,) + blk, a.dtype)
        lands.append(lax.dynamic_update_index_in_dim(land, a, me, 0) if src_of is _whole else land)
    mode = "scatter" if src_of is _chip_block else ("gather_half" if halves else "gather")
    return split_start(name, mode, list(arrs) + lands, len(arrs))


chips_wait = split_wait


def handover_start(name, lands):
    return split_start(name, "handover", lands, len(lands))


def gather_chips_halves(name, arrs):
    n = len(arrs)
    nf = len(_CHIP_FLIPS)
    split = [a.shape[0] % 32 == 0 for a in arrs]

    def body(*refs):
        ins, outs = refs[:n], refs[n:2 * n]
        send1, recv1, send2, recv2, lsem = refs[2 * n:]
        me = (lax.axis_index("x"), lax.axis_index("y"), lax.axis_index("c"))
        sib = (me[0], me[1], 1 - me[2])
        peers = [tuple(1 - m if f else m for m, f in zip(me, fl)) for fl in _CHIP_FLIPS]
        local, first, second = [], [], []
        for k in range(n):
            cp = pltpu.make_async_copy(ins[k], outs[k].at[_chip_slot(me)], lsem.at[k])
            cp.start()
            local.append(cp)
            half = ins[k].shape[0] // 2
            rows = pl.ds(pl.multiple_of(me[2] * half, 16), half) if split[k] else pl.ds(0, ins[k].shape[0])
            for j, peer in enumerate(peers):
                cp = pltpu.make_async_remote_copy(
                    src_ref=ins[k].at[rows], dst_ref=outs[k].at[_chip_slot(me), rows], send_sem=send1.at[k, j],
                    recv_sem=recv1.at[k, j], device_id=peer, device_id_type=MESH)
                cp.start()
                first.append((k, j, rows, cp))
        for k, j, rows, cp in first:
            cp.wait_recv()
            if split[k]:
                got = outs[k].at[_chip_slot(peers[j]), rows]
                fwd = pltpu.make_async_remote_copy(src_ref=got, dst_ref=got, send_sem=send2.at[k, j],
                                                   recv_sem=recv2.at[k, j], device_id=sib, device_id_type=MESH)
                fwd.start()
                second.append(fwd)
        for _, _, _, cp in first:
            cp.wait_send()
        for cp in second:
            cp.wait()
        for cp in local:
            cp.wait()

    any_spec = pl.BlockSpec(memory_space=pl.ANY)
    sems = pltpu.SemaphoreType.DMA((n, nf))
    return pl.pallas_call(
        body, name=name, in_specs=[any_spec] * n, out_specs=[any_spec] * n,
        out_shape=[jax.ShapeDtypeStruct((4,) + a.shape, a.dtype) for a in arrs],
        scratch_shapes=[sems, sems, sems, sems, pltpu.SemaphoreType.DMA((n,))],
        compiler_params=pltpu.CompilerParams(has_side_effects=True))(*arrs)


def swap_sibling(name, arrs):
    outs = flip_exchange(name, arrs, ((0, 0, 1),), 1, lambda p: 0, _whole, False)
    return [o[0] for o in outs]


def gather_all(arrs, after=()):
    return flip_exchange("gather_all", arrs, _ALL_FLIPS, 8, lambda p: 4 * p[0] + 2 * p[1] + p[2], _whole, True,
                         after=after)


def _row_tile(nr, nc, n_arrays):
    budget = (20 << 20) // (n_arrays * 2 * 4 * max(nc, LANES))
    t = min(nr, budget) // 16 * 16
    while t > 0 and nr % t:
        t -= 16
    return t if t > 0 else nr


def sum_slots(name, x, own):
    ns, nr, nc = x.shape
    tile = _row_tile(nr, nc, ns + 1)
    me = jnp.reshape(_chip_slot((lax.axis_index("x"), lax.axis_index("y"))), (1,)).astype(jnp.int32)

    def body(me_ref, own_ref, *refs):
        o_ref = refs[-1]
        s = own_ref[...].astype(f32)
        for r in refs[:-1]:
            s = s + r[...].astype(f32)
        o_ref[...] = s.astype(o_ref.dtype)

    def slot(q):
        return pl.BlockSpec((None, tile, nc), lambda i, t: ((t[0] + q) % ns, i, 0))

    return pl.pallas_call(
        body, name=name, out_shape=jax.ShapeDtypeStruct((nr, nc), x.dtype),
        grid_spec=pltpu.PrefetchScalarGridSpec(
            num_scalar_prefetch=1, grid=(nr // tile,), in_specs=[slot(q) for q in range(ns)],
            out_specs=pl.BlockSpec((tile, nc), lambda i, t: (i, 0))),
        compiler_params=_params(1))(me, own, *[x] * (ns - 1))


def adamw(name, w, g_parts, m, v):
    nr, nc = w.shape[-2:]
    n_g = len(g_parts)
    tile = _row_tile(nr, nc, 7 + n_g)

    def body(*refs):
        w_ref, m_ref, v_ref = refs[:3]
        g = refs[3][...].astype(f32)
        for r in refs[4:3 + n_g]:
            g = g + r[...].astype(f32)
        g_ref, d_ref, nm_ref, nv_ref = refs[3 + n_g:]
        nm = ADAM_B1 * m_ref[...] + (1.0 - ADAM_B1) * g
        nv = ADAM_B2 * v_ref[...] + (1.0 - ADAM_B2) * jnp.square(g)
        m_hat = nm / (1.0 - ADAM_B1 ** ADAM_STEP)
        v_hat = nv / (1.0 - ADAM_B2 ** ADAM_STEP)
        g_ref[...] = g
        d_ref[...] = -ADAM_LR * (m_hat / (jnp.sqrt(v_hat) + ADAM_EPS) + ADAM_WD * w_ref[...])
        nm_ref[...] = nm
        nv_ref[...] = nv

    spec = pl.BlockSpec((tile, nc), lambda i: (i, 0))
    spec3 = pl.BlockSpec((None, tile, nc), lambda i: (0, i, 0)) if w.ndim == 3 else spec
    return pl.pallas_call(
        body, name=name, grid=(nr // tile,), in_specs=[spec3] * 3 + [spec] * n_g, out_specs=[spec3] * 4,
        out_shape=[jax.ShapeDtypeStruct(w.shape, f32)] * 4, compiler_params=_params(1))(w, m, v, *g_parts)


def adamw_packed(w, g8, m, v):
    nr, nc = w.shape

    def body(w_ref, g_ref, m_ref, v_ref, go_ref, d_ref, nm_ref, nv_ref):
        g = g_ref[0]
        for q in range(1, 8):
            g = g + g_ref[q]
        nm = ADAM_B1 * m_ref[...] + (1.0 - ADAM_B1) * g
        nv = ADAM_B2 * v_ref[...] + (1.0 - ADAM_B2) * jnp.square(g)
        m_hat = nm / (1.0 - ADAM_B1 ** ADAM_STEP)
        v_hat = nv / (1.0 - ADAM_B2 ** ADAM_STEP)
        go_ref[...] = g
        d_ref[...] = -ADAM_LR * (m_hat / (jnp.sqrt(v_hat) + ADAM_EPS) + ADAM_WD * w_ref[...])
        nm_ref[...] = nm
        nv_ref[...] = nv

    return pl.pallas_call(body, name="adamw_packed", out_shape=[jax.ShapeDtypeStruct((nr, nc), f32)] * 4,
                          compiler_params=pltpu.CompilerParams(vmem_limit_bytes=VMEM_LIMIT))(w, g8, m, v)


def _pack(vectors):
    rows = []
    for a in vectors:
        flat = a.reshape(-1).astype(f32)
        pad = (-flat.shape[0]) % LANES
        rows.append(jnp.pad(flat, (0, pad)).reshape(-1, LANES))
    packed = jnp.concatenate(rows, axis=0)
    return jnp.pad(packed, ((0, (-packed.shape[0]) % 8), (0, 0)))


def _unpack(packed, like):
    out, r = [], 0
    for a in like:
        n = a.size
        nr = -(-n // LANES)
        out.append(packed[r:r + nr].reshape(-1)[:n].reshape(a.shape))
        r += nr
    return out


def kernel(x, mem, mix_norm_w, w_in, dn_conv_w, dn_a_log, dn_dt_bias, dn_norm_w, rw_mu, rw_w0, rw_w2, rw_a0, rw_a2, rw_g2, rw_k_k, rw_k_a, rw_r_k, rw_ln_w, rw_ln_b, w_out, xa_norm_w, mem_norm_w, xa_wq, xa_wk, xa_wv, xa_wo, ffn_norm_w, ffn_w1, ffn_w2, final_norm_w, loss_target, m_mix_norm_w, m_w_in, m_dn_conv_w, m_dn_a_log, m_dn_dt_bias, m_dn_norm_w, m_rw_mu, m_rw_w0, m_rw_w2, m_rw_a0, m_rw_a2, m_rw_g2, m_rw_k_k, m_rw_k_a, m_rw_r_k, m_rw_ln_w, m_rw_ln_b, m_w_out, m_xa_norm_w, m_mem_norm_w, m_xa_wq, m_xa_wk, m_xa_wv, m_xa_wo, m_ffn_norm_w, m_ffn_w1, m_ffn_w2, m_final_norm_w, v_mix_norm_w, v_w_in, v_dn_conv_w, v_dn_a_log, v_dn_dt_bias, v_dn_norm_w, v_rw_mu, v_rw_w0, v_rw_w2, v_rw_a0, v_rw_a2, v_rw_g2, v_rw_k_k, v_rw_k_a, v_rw_r_k, v_rw_ln_w, v_rw_ln_b, v_w_out, v_xa_norm_w, v_mem_norm_w, v_xa_wq, v_xa_wk, v_xa_wv, v_xa_wo, v_ffn_norm_w, v_ffn_w1, v_ffn_w2, v_final_norm_w):
    weights = dict(mix_norm_w=mix_norm_w, w_in=w_in, dn_conv_w=dn_conv_w, dn_a_log=dn_a_log, dn_dt_bias=dn_dt_bias, dn_norm_w=dn_norm_w, rw_mu=rw_mu, rw_w0=rw_w0, rw_w2=rw_w2, rw_a0=rw_a0, rw_a2=rw_a2, rw_g2=rw_g2, rw_k_k=rw_k_k, rw_k_a=rw_k_a, rw_r_k=rw_r_k, rw_ln_w=rw_ln_w, rw_ln_b=rw_ln_b, w_out=w_out, xa_norm_w=xa_norm_w, mem_norm_w=mem_norm_w, xa_wq=xa_wq, xa_wk=xa_wk, xa_wv=xa_wv, xa_wo=xa_wo, ffn_norm_w=ffn_norm_w, ffn_w1=ffn_w1, ffn_w2=ffn_w2, final_norm_w=final_norm_w)
    mom_m = dict(mix_norm_w=m_mix_norm_w, w_in=m_w_in, dn_conv_w=m_dn_conv_w, dn_a_log=m_dn_a_log, dn_dt_bias=m_dn_dt_bias, dn_norm_w=m_dn_norm_w, rw_mu=m_rw_mu, rw_w0=m_rw_w0, rw_w2=m_rw_w2, rw_a0=m_rw_a0, rw_a2=m_rw_a2, rw_g2=m_rw_g2, rw_k_k=m_rw_k_k, rw_k_a=m_rw_k_a, rw_r_k=m_rw_r_k, rw_ln_w=m_rw_ln_w, rw_ln_b=m_rw_ln_b, w_out=m_w_out, xa_norm_w=m_xa_norm_w, mem_norm_w=m_mem_norm_w, xa_wq=m_xa_wq, xa_wk=m_xa_wk, xa_wv=m_xa_wv, xa_wo=m_xa_wo, ffn_norm_w=m_ffn_norm_w, ffn_w1=m_ffn_w1, ffn_w2=m_ffn_w2, final_norm_w=m_final_norm_w)
    mom_v = dict(mix_norm_w=v_mix_norm_w, w_in=v_w_in, dn_conv_w=v_dn_conv_w, dn_a_log=v_dn_a_log, dn_dt_bias=v_dn_dt_bias, dn_norm_w=v_dn_norm_w, rw_mu=v_rw_mu, rw_w0=v_rw_w0, rw_w2=v_rw_w2, rw_a0=v_rw_a0, rw_a2=v_rw_a2, rw_g2=v_rw_g2, rw_k_k=v_rw_k_k, rw_k_a=v_rw_k_a, rw_r_k=v_rw_r_k, rw_ln_w=v_rw_ln_w, rw_ln_b=v_rw_ln_b, w_out=v_w_out, xa_norm_w=v_xa_norm_w, mem_norm_w=v_mem_norm_w, xa_wq=v_xa_wq, xa_wk=v_xa_wk, xa_wv=v_xa_wv, xa_wo=v_xa_wo, ffn_norm_w=v_ffn_norm_w, ffn_w1=v_ffn_w1, ffn_w2=v_ffn_w2, final_norm_w=v_final_norm_w)
    names = list(weights)

    seq, d = x.shape[1], x.shape[2]
    dnw = d // 2
    rww = d - dnw
    nh, nb = dnw // LANES, rww // LANES
    n_mem = mem.shape[1]
    lw_dim, la_dim, lg_dim = rw_w2.shape[1], rw_a2.shape[1], rw_g2.shape[1]
    assert lw_dim + la_dim == LANES and lg_dim == LANES and dnw % LANES == 0 and rww % LANES == 0
    xs, mems, tgt = x[0], mem[0], loss_target[0]

    col_sharded = ("w_in", "xa_wo", "ffn_w1", "dn_conv_w", "rw_w2", "rw_a2", "rw_g2")
    row_sharded = ("w_out", "xa_wq", "xa_wk", "xa_wv", "ffn_w2")
    f32_payload = ("dn_conv_w", "rw_w2", "rw_a2", "rw_g2")
    sharded = col_sharded + row_sharded
    payload = {n: weights[n][0].astype(f32 if n in f32_payload else bf16) for n in sharded}
    shard_w = w_in.shape[2]
    pad_w = -(-shard_w // LANES) * LANES
    first = ("w_in", "dn_conv_w", "rw_w2", "rw_a2", "rw_g2")
    mid = ("w_out", "xa_wq", "xa_wk", "xa_wv", "xa_wo")
    late = ("ffn_w1", "ffn_w2")
    gathered = dict(zip(first, gather_chips_halves("gather_first", [payload[n] for n in first])))
    ordered = lax.optimization_barrier(([gathered[n] for n in first], [payload[n] for n in mid + late]))
    gathered = dict(zip(first, ordered[0]))
    payload.update(zip(mid + late, ordered[1]))
    mid_state, tok_mid = chips_start("gather_mid_start", [payload[n] for n in mid], _whole, halves=True)
    late_state, tok_late = chips_start("gather_late_start", [payload[n] for n in late], _whole, halves=True)
    mix_norm_w_in = mix_norm_w + (tok_mid + tok_late)

    def full(n):
        g = gathered[n]
        if n in col_sharded:
            return g.transpose(1, 0, 2).reshape(g.shape[1], 4 * g.shape[2])
        return g.reshape(4 * g.shape[1], g.shape[2])

    c_rw0 = 4 * dnw + 2 * nh
    c_rw = 3 * rww
    eb_ab, eb_r = 4 * nh, 4 * nh + 1
    eb_l1 = eb_r + 3 * nb
    n_ext = -(-(eb_l1 + 2) // 4) * 4
    tbl_fwd, tbl_bwd, per_shard = w_in_layout(4 * shard_w, shard_w, c_rw0, LANES - 2 * nh, n_ext)
    sub_f = max(s for s in range(1, REGROUP_SUB + 1) if n_ext % s == 0)
    sub_b = max(s for s in range(1, REGROUP_SUB + 1) if per_shard % s == 0)
    w_ext = lane_regroup(
        "w_in_regroup", gathered["w_in"], tbl_fwd,
        lambda p, b: pl.BlockSpec(
            (None, d, LANES), lambda i, j, t: (t[REGROUP_FIELDS * p, j * sub_f + b], 0,
                                               t[REGROUP_FIELDS * p + 1, j * sub_f + b])),
        pl.BlockSpec((d, sub_f * LANES), lambda i, j, t: (0, j)), (d, n_ext * LANES), (1, n_ext // sub_f), shard_w,
        sub_f)
    conv_w = full("dn_conv_w")
    w2p = jnp.concatenate([full("rw_w2"), jnp.zeros((la_dim, rww), f32)], axis=0)
    a2p = jnp.concatenate([jnp.zeros((lw_dim, rww), f32), full("rw_a2")], axis=0)
    g2 = full("rw_g2")
    xaw = xa_wq.shape[2]
    nxh = xaw // LANES
    ffn = 4 * ffn_w1.shape[2]

    def lane_row(vec):
        return jnp.pad(vec.reshape(1, -1), ((0, 0), (0, LANES - vec.size)))

    alog_row, dtb_row = lane_row(dn_a_log), lane_row(dn_dt_bias)
    head_of_col = jnp.arange(dnw)[None, :] // LANES
    e_g = (jnp.arange(LANES)[:, None] == head_of_col).astype(f32)
    e_b = (jnp.arange(LANES)[:, None] == head_of_col + nh).astype(f32)
    mu_main, mu_small = rw_mu[:, :c_rw], rw_mu[:, c_rw:]
    r_k_row = rw_r_k.reshape(1, rww)
    fnw = final_norm_w.reshape(1, d)
    qb, kb_, vb, zb = 0, nh, 2 * nh, 3 * nh
    rb0 = eb_r
    cc = lambda w_, o_: ("constc", w_, o_)
    rc = lambda o_: ("rowc", LANES, o_)

    (u16,) = rowcall("norm_mix", fn_rms, [(xs, "row"), (mix_norm_w_in, "const")], [(seq, d, bf16, "row")], rows=seq)
    p_main = p_small = mm("in_proj", u16, w_ext)

    fn_gconv = make_fn_gconv(2 * nh)
    (qkv,) = rowcall("gdn_conv", fn_gconv, [(p_main, rc(0)), (conv_w, cc(LANES, 0))],
                     [(seq, 3 * dnw, f32, "rowc")], rows=seq, tile=seq, ncol=3 * nh)
    gate_ins = [(p_small, rc(eb_ab)), (alog_row, "const"), (dtb_row, "const"), (e_g, "const"), (e_b, "const")]
    g_b, beta_b = rowcall("gdn_gate", fn_ggate, gate_ins, [(seq, dnw, f32, "row")] * 2, rows=seq)
    gdn_ins = [(qkv, qb), (qkv, kb_), (qkv, vb), (g_b, 0), (beta_b, 0)]
    o_raw, gdn_states, gdn_kept = scan_fwd("gdn_scan", gdn_chunk, gdn_ins, rows=seq, chunk=GDN_CHUNK, ncol=nh)
    mid_state, tok = handover_start("gather_mid_pass", chips_wait("gather_mid_wait", mid_state, o_raw))
    gpost_ins = [(o_raw, rc(0)), (p_main, rc(zb)), (dn_norm_w + tok, "const")]
    (o_dn,) = rowcall("gdn_post", fn_gpost, gpost_ins, [(seq, dnw, bf16, "rowc")], rows=seq, ncol=nh)

    (prw,) = rowcall("rw_lerp_main", fn_lerp, [(p_main, rc(rb0)), (mu_main, cc(LANES, 0))],
                     [(seq, c_rw, f32, "rowc")], rows=seq, tile=seq, ncol=3 * nb)
    (psl,) = rowcall("rw_lerp_small", fn_lerp, [(p_small, rc(eb_l1)), (mu_small, cc(LANES, 0))],
                     [(seq, 2 * LANES, f32, "rowc")], rows=seq, tile=seq, ncol=2)
    rprep_ins = [(prw, rc(nb)), (psl, "row"), (rw_w0, cc(LANES, 0)), (rw_a0, cc(LANES, 0)), (rw_k_k, cc(LANES, 0)),
                 (rw_k_a, cc(LANES, 0)), (w2p, cc(LANES, 0)), (a2p, cc(LANES, 0)), (g2, cc(LANES, 0))]
    lw, kmod, kk, a_rw, gate = rowcall("rw_prep", fn_rprep, rprep_ins, [(seq, rww, f32, "rowc")] * 5,
                                        rows=seq, ncol=nb)
    wkv_ins = [(prw, 0), (lw, 0), (kmod, 0), (prw, 2 * nb), (kk, 0), (a_rw, 0)]
    y_rw, wkv_states, wkv_kept = scan_fwd("wkv_scan", wkv_chunk, wkv_ins, rows=seq, chunk=WKV_CHUNK, ncol=nb)
    rpost_ins = [(y_rw, rc(0)), (prw, rc(0)), (kmod, rc(0)), (prw, rc(2 * nb)), (gate, rc(0)),
                 (r_k_row, cc(LANES, 0)), (rw_ln_w, cc(LANES, 0)), (rw_ln_b, cc(LANES, 0))]
    (o_rw,) = rowcall("rw_post", fn_rpost, rpost_ins, [(seq, rww, bf16, "rowc")], rows=seq, ncol=nb)

    o_cat = jnp.concatenate([o_dn, o_rw], axis=1)
    late_state, tok = handover_start("gather_late_pass", chips_wait("gather_late_wait", late_state, o_cat))
    gathered.update(zip(mid, chips_wait("gather_mid_pass_wait", mid_state, o_cat)))
    w_out_f, wq_f, wk_f, wv_f, wo_f = full("w_out"), full("xa_wq"), full("xa_wk"), full("xa_wv"), full("xa_wo")
    h1 = mm("out_proj", o_cat, w_out_f, add=xs)

    (hn16,) = rowcall("norm_xa", fn_rms, [(h1, "row"), (xa_norm_w + tok, "const")], [(seq, d, bf16, "row")],
                      rows=seq)
    (mn16,) = rowcall("norm_mem", fn_rms, [(mems, "row"), (mem_norm_w, "const")], [(n_mem, d, bf16, "row")],
                      rows=n_mem)
    q_xa = mm("xa_q", hn16, wq_f)
    k_xa = mm("xa_k", mn16, wk_f)
    v_xa = mm("xa_v", mn16, wv_f)
    xcore_ins = [(q_xa, rc(0)), (k_xa, cc(LANES, 0)), (v_xa, cc(LANES, 0))]
    (o_xa,) = rowcall("xa_core", fn_xcore, xcore_ins, [(seq, xaw, bf16, "rowc")], rows=seq, ncol=nxh)
    h2 = mm("xa_o", o_xa, wo_f, add=h1)

    (fn16,) = rowcall("norm_ffn", fn_rms, [(h2, "row"), (ffn_norm_w, "const")], [(seq, d, bf16, "row")], rows=seq)
    gathered.update(zip(late, chips_wait("gather_late_pass_wait", late_state, fn16)))
    w1_f, w2_f = full("ffn_w1"), full("ffn_w2")
    a1_16, hid16 = mm("ffn_up", fn16, w1_f, epilogue=lambda r: (r, jnp.square(jnp.maximum(r, 0.0))),
                      out_dtypes=(bf16, bf16))
    h3 = mm("ffn_down", hid16, w2_f, add=h2)

    dh3, dh3_16, d_fnw, loss_rows = rowcall(
        "loss_head", fn_final, [(h3, "row"), (tgt, "row"), (fnw, "const")],
        [(seq, d, f32, "row"), (seq, d, bf16, "row"), (1, d, f32, "acc"), (8, LANES, f32, "acc")], rows=seq)

    da1_16 = mm("ffn_down_dx", dh3_16, w2_f, tb=True, extra=[a1_16], out_dtype=bf16,
                epilogue=lambda r, a1: (r * (2.0 * jnp.maximum(a1.astype(f32), 0.0)),))
    def by_chip(n, g):
        if g.ndim == 3:
            return g
        if n in col_sharded:
            return g.reshape(g.shape[0], 4, g.shape[1] // 4).transpose(1, 0, 2)
        return g.reshape(4, g.shape[0] // 4, g.shape[1])

    g_ffn_w2 = mm("ffn_down_dw", hid16, dh3_16, ta=True, out_dtype=bf16)
    ffn_w2_g, tok = chips_start("scatter_ffn_w2_start", [by_chip("ffn_w2", g_ffn_w2)], _chip_block)
    g_ffn_w1 = mm("ffn_up_dw", fn16, da1_16, ta=True, out_dtype=bf16, by_chip=True, after=tok)
    ffn_w1_g, tok = chips_start("scatter_ffn_w1_start", [g_ffn_w1], _chip_block)
    dfn = mm("ffn_up_dx", da1_16, w1_f, tb=True, after=tok)
    dh2, d_ffn_nw, dh2_16 = rowvjp("norm_ffn_bwd", fn_rms, [(h2, "row"), (ffn_norm_w, "const")],
                                   [[(dfn, "row")]], [0, 1], rows=seq, adds=[(0, dh3, "row")], dup16=[0])

    do_xa = mm("xa_o_dx", dh2_16, wo_f, tb=True)
    g_xa_wo = mm("xa_o_dw", o_xa, dh2_16, ta=True, out_dtype=bf16, by_chip=True)
    dq_xa, dk_xa, dv_xa, dq16 = rowvjp("xa_core_bwd", fn_xcore, xcore_ins, [[(do_xa, rc(0))]], [0, 1, 2],
                                       rows=seq, ncol=nxh, dup16=[0])
    g_xa_wq = mm("xa_q_dw", hn16, dq16, ta=True, out_dtype=bf16)
    dhn = mm("xa_q_dx", dq16, wq_f, tb=True)
    dh1, d_xa_nw, dh1_16 = rowvjp("norm_xa_bwd", fn_rms, [(h1, "row"), (xa_norm_w, "const")], [[(dhn, "row")]],
                                  [0, 1], rows=seq, adds=[(0, dh2, "row")], dup16=[0])
    dk16, dv16 = dk_xa.astype(bf16), dv_xa.astype(bf16)
    g_xa_wk = mm("xa_k_dw", mn16, dk16, ta=True, out_dtype=bf16)
    g_xa_wv = mm("xa_v_dw", mn16, dv16, ta=True, out_dtype=bf16)
    dmn = mm("xa_v_dx", dv16, wv_f, tb=True, add=mm("xa_k_dx", dk16, wk_f, tb=True))
    (d_mem_nw,) = rowvjp("norm_mem_bwd", fn_rms, [(mems, "row"), (mem_norm_w, "const")], [[(dmn, "row")]], [1],
                         rows=n_mem)

    g_w_out = mm("out_proj_dw", o_cat, dh1_16, ta=True, out_dtype=bf16)
    mid_grads = dict(w_out=g_w_out, xa_wq=g_xa_wq, xa_wk=g_xa_wk, xa_wv=g_xa_wv, xa_wo=g_xa_wo)
    mid_g, tok = chips_start("scatter_mid_start", [by_chip(n, mid_grads[n]) for n in mid], _chip_block)
    do_cat = mm("out_proj_dx", dh1_16, w_out_f, tb=True, after=tok)

    dy, dr_a, dkmod_a, dv_a, dgate, d_r_k, d_ln_w, d_ln_b = rowvjp(
        "rw_post_bwd", fn_rpost, rpost_ins, [[(do_cat, rc(nh))]], [0, 1, 2, 3, 4, 5, 6, 7], rows=seq, ncol=nb)
    dr_b, dlw, dkmod_b, dv_b, dkk, da_rw = scan_bwd("wkv_scan_bwd", wkv_chunk, wkv_ins, wkv_states, wkv_kept, dy, 0,
                                                    rows=seq, chunk=WKV_CHUNK, ncol=nb)
    rprep_cts = [[(dlw, rc(0))], [(dkmod_a, rc(0)), (dkmod_b, rc(0))], [(dkk, rc(0))], [(da_rw, rc(0))],
                 [(dgate, rc(0))]]
    dpk, dpsl_parts, d_w0, d_a0, d_k_k, d_k_a, d_w2p, d_a2p, d_g2 = rowvjp(
        "rw_prep_bwd", fn_rprep, rprep_ins, rprep_cts, [0, 1, 2, 3, 4, 5, 6, 7, 8], rows=seq, ncol=nb)
    (dpsl,) = rowcall("rw_prep_sum", fn_sumcols(nb), [(dpsl_parts, "row")], [(seq, 2 * LANES, f32, "row")], rows=seq)

    def lerp_bwd(tag, p, p_off, mu, mu_off, ct_lists, ncol):
        return rowvjp("rw_lerp_bwd_" + tag, fn_lerp, [(p, rc(p_off)), (mu, cc(LANES, mu_off))], [ct_lists], [0, 1],
                      rows=seq, tile=seq, ncol=ncol, dup16=[0])

    _, dmu_r, dpr16 = lerp_bwd("r", p_main, rb0, mu_main, 0, [(dr_a, rc(0)), (dr_b, rc(0))], nb)
    _, dmu_k, dpk16 = lerp_bwd("k", p_main, rb0 + nb, mu_main, nb, [(dpk, rc(0))], nb)
    _, dmu_v, dpv16 = lerp_bwd("v", p_main, rb0 + 2 * nb, mu_main, 2 * nb, [(dv_a, rc(0)), (dv_b, rc(0))], nb)
    _, dmu_s, dps12_16 = lerp_bwd("small", p_small, eb_l1, mu_small, 0, [(dpsl, rc(0))], 2)

    do_raw, dz, d_dn_nw, dz16 = rowvjp("gdn_post_bwd", fn_gpost, gpost_ins, [[(do_cat, rc(0))]], [0, 1, 2],
                                       rows=seq, ncol=nh, dup16=[1])
    dq_g, dk_g, dv_g, dg_b, dbeta_b = scan_bwd("gdn_scan_bwd", gdn_chunk, gdn_ins, gdn_states, gdn_kept, do_raw, 0,
                                               rows=seq, chunk=GDN_CHUNK, ncol=nh)
    dps0, d_alog, d_dtb, dps0_16 = rowvjp("gdn_gate_bwd", fn_ggate, gate_ins, [[(dg_b, "row")], [(dbeta_b, "row")]],
                                          [0, 1, 2], rows=seq, dup16=[0])
    dqkv = jnp.concatenate([dq_g, dk_g, dv_g], axis=1)
    _, d_conv_w, dqkv16 = rowvjp("gdn_conv_bwd", fn_gconv, [(p_main, rc(0)), (conv_w, cc(LANES, 0))],
                                 [[(dqkv, rc(0))]], [0, 1], rows=seq, tile=seq, ncol=3 * nh, dup16=[0])

    dp16 = jnp.concatenate([dqkv16, dz16, dps0_16, dpr16, dpk16, dpv16, dps12_16,
                            jnp.zeros((seq, (n_ext - eb_l1 - 2) * LANES), bf16)], axis=1)
    g_w_ext = mm("in_proj_dw", u16, dp16, ta=True, out_dtype=bf16)
    g_w_in = lane_regroup(
        "w_in_grad_regroup", g_w_ext, tbl_bwd,
        lambda p, b: pl.BlockSpec((d, LANES), lambda i, j, t: (0, t[REGROUP_FIELDS * p + 1, j * sub_b + b])),
        pl.BlockSpec((None, d, sub_b * LANES),
                     lambda i, j, t: (j // (per_shard // sub_b), 0, j % (per_shard // sub_b))),
        (4, d, pad_w), (1, 4 * per_shard // sub_b), n_ext * LANES, sub_b)
    first_grads = dict(w_in=g_w_in, dn_conv_w=d_conv_w, rw_w2=d_w2p[:lw_dim], rw_a2=d_a2p[lw_dim:], rw_g2=d_g2)
    first_g, tok = chips_start("scatter_first_start", [by_chip(n, first_grads[n]) for n in first], _chip_block)

    du = mm("in_proj_dx", dp16, w_ext, tb=True, after=tok)
    grad_x, d_mix_nw = rowvjp("norm_mix_bwd", fn_rms, [(xs, "row"), (mix_norm_w, "const")], [[(du, "row")]],
                              [0, 1], rows=seq, adds=[(0, dh1, "row")])
    received, sent = {}, {}

    def arrived(group, name, state, after):
        lands, sources = chips_wait(name, state, after, with_sources=True)
        received.update(zip(group, lands))
        sent.update(zip(group, sources))

    arrived(mid, "scatter_mid_wait", mid_g, grad_x)
    arrived(("ffn_w2",), "scatter_ffn_w2_wait", ffn_w2_g, grad_x)
    arrived(("ffn_w1",), "scatter_ffn_w1_wait", ffn_w1_g, grad_x)

    out = {}

    def reduce_and_update(tag, group):
        partial = [sum_slots("sum_chips_" + n, received[n], sent[n]) for n in group]
        other = swap_sibling("swap_sibling_" + tag, partial)
        for n, p_mine, p_other in zip(group, partial, other):
            if weights[n].shape[2] % LANES:
                rows, cols = weights[n].shape[1:]
                lin = lambda a: jnp.swapaxes(a, 1, 2).reshape(-1, LANES)
                lin_g = lambda p: p.T[:cols].reshape(-1, LANES)
                res = adamw("adamw_" + n, lin(weights[n]), [lin_g(p_mine), lin_g(p_other)], lin(mom_m[n]),
                            lin(mom_v[n]))
                out[n] = [jnp.swapaxes(r.reshape(1, cols, rows), 1, 2) for r in res]
            else:
                out[n] = adamw("adamw_" + n, weights[n], [p_mine, p_other], mom_m[n], mom_v[n])

    reduce_and_update("rest", mid + late)
    arrived(first, "scatter_first_wait", first_g, [out[n][1] for n in mid + late])
    reduce_and_update("first", first)

    small_names = [n for n in names if n not in sharded]
    small_local = dict(
        mix_norm_w=d_mix_nw, dn_a_log=d_alog[:, :nh], dn_dt_bias=d_dtb[:, :nh], dn_norm_w=d_dn_nw,
        rw_mu=jnp.concatenate([dmu_r, dmu_k, dmu_v, dmu_s], axis=1), rw_w0=d_w0, rw_a0=d_a0, rw_k_k=d_k_k,
        rw_k_a=d_k_a, rw_r_k=d_r_k, rw_ln_w=d_ln_w, rw_ln_b=d_ln_b, xa_norm_w=d_xa_nw, mem_norm_w=d_mem_nw,
        ffn_norm_w=d_ffn_nw, final_norm_w=d_fnw)
    loss_vec = jnp.where(jnp.arange(LANES) == 0, loss_rows[0], 0.0)
    (g8,) = gather_all([_pack([small_local[n] for n in small_names] + [loss_vec])], after=[out["w_in"][1]])

    packed_like = [weights[n] for n in small_names] + [loss_vec]
    zero = jnp.zeros((LANES,), f32)
    res = adamw_packed(_pack([weights[n] for n in small_names] + [zero]), g8,
                       _pack([mom_m[n] for n in small_names] + [zero]),
                       _pack([mom_v[n] for n in small_names] + [zero]))
    unpacked = [_unpack(r, packed_like) for r in res]
    for i, n in enumerate(small_names):
        out[n] = [u[i] for u in unpacked]
    loss = unpacked[0][-1][0]

    return (loss, grad_x.reshape(x.shape), *[out[n][0] for n in names], *[out[n][1] for n in names],
            *[out[n][2] for n in names], *[out[n][3] for n in names])
```

```python
import functools

import jax
import jax.numpy as jnp
from jax import lax
from jax.experimental import pallas as pl
from jax.experimental.pallas import tpu as pltpu

f32 = jnp.float32
bf16 = jnp.bfloat16
HI = lax.Precision.HIGHEST
MESH = pl.DeviceIdType.MESH

LANES = 128
VMEM_LIMIT = 56 << 20
TOK_TILE = 256
TOK_TILE_BLOCKED = 1024
MM_TILE = 1024
MM_TILE_K = 2048
MM_WHOLE_K_BYTES = 52 << 20
GDN_CHUNK = 128
WKV_CHUNK = 64
SCAN_GROUP = 8
P_BULK = 1
P_INV = 1
P_RESID = 3
P_CUMSUM = 3
RMS_EPS = 1e-6
RW_GN_EPS = 64e-5
RW_HEAD = 64

ADAM_LR, ADAM_B1, ADAM_B2, ADAM_EPS, ADAM_WD, ADAM_STEP = 0.001, 0.9, 0.999, 1e-08, 0.01, 10


def _params(n_grid):
    return pltpu.CompilerParams(dimension_semantics=("arbitrary",) * n_grid, vmem_limit_bytes=VMEM_LIMIT)


_DIMS = {"nn": (((1,), (0,)), ((), ())), "nt": (((1,), (1,)), ((), ())), "tn": (((0,), (0,)), ((), ()))}
_DIMS_BATCHED = {"nn": (((2,), (1,)), ((0,), (0,))), "nt": (((2,), (2,)), ((0,), (0,))),
                 "tn": (((1,), (1,)), ((0,), (0,)))}


def _raw_dot(a, b, mode, passes):
    dims = (_DIMS if a.ndim == 2 else _DIMS_BATCHED)[mode]
    if passes == 6:
        return lax.dot_general(a.astype(f32), b.astype(f32), dims, precision=HI, preferred_element_type=f32)
    ah, bh = a.astype(bf16), b.astype(bf16)
    r = lax.dot_general(ah, bh, dims, preferred_element_type=f32)
    if passes == 3:
        al = (a - ah.astype(f32)).astype(bf16)
        bl = (b - bh.astype(f32)).astype(bf16)
        r = r + lax.dot_general(al, bh, dims, preferred_element_type=f32)
        r = r + lax.dot_general(ah, bl, dims, preferred_element_type=f32)
    return r


@functools.partial(jax.custom_vjp, nondiff_argnums=(2, 3))
def pdot(a, b, mode, passes):
    return _raw_dot(a, b, mode, passes)


def _pdot_bwd(mode, passes, res, g):
    a, b = res
    if mode == "nn":
        da, db = _raw_dot(g, b, "nt", passes), _raw_dot(a, g, "tn", passes)
    elif mode == "nt":
        da, db = _raw_dot(g, b, "nn", passes), _raw_dot(g, a, "tn", passes)
    else:
        da, db = _raw_dot(b, g, "nt", passes), _raw_dot(a, g, "nn", passes)
    return da.astype(a.dtype), db.astype(b.dtype)


pdot.defvjp(lambda a, b, mode, passes: (_raw_dot(a, b, mode, passes), (a, b)), _pdot_bwd)


def bdot(a, b):
    return pdot(a, b, "nn", 1)


def bdot_nt(a, b):
    return pdot(a, b, "nt", 1)


def _shift_rows(x, k):
    row = lax.broadcasted_iota(jnp.int32, x.shape, 0)
    return jnp.where(row < k, 0.0, pltpu.roll(x, k, axis=0))


def _unshift_rows(g, k):
    n = g.shape[0]
    row = lax.broadcasted_iota(jnp.int32, g.shape, 0)
    return jnp.where(row >= n - k, 0.0, pltpu.roll(g, n - k, axis=0))


@functools.partial(jax.custom_vjp, nondiff_argnums=(1,))
def tshift(x, k):
    return _shift_rows(x, k)


tshift.defvjp(lambda x, k: (_shift_rows(x, k), None), lambda k, _, g: (_unshift_rows(g, k),))


def rms(x, w):
    x = x.astype(f32)
    return x * lax.rsqrt(jnp.mean(x * x, axis=-1, keepdims=True) + RMS_EPS) * w


def softplus(x):
    return jnp.maximum(x, 0.0) + jnp.log(1.0 + jnp.exp(-jnp.abs(x)))


def seg2sum(x):
    lo = lax.broadcasted_iota(jnp.int32, x.shape, 1) < RW_HEAD
    s_lo = jnp.sum(jnp.where(lo, x, 0.0), axis=-1, keepdims=True)
    s_hi = jnp.sum(jnp.where(lo, 0.0, x), axis=-1, keepdims=True)
    return jnp.where(lo, s_lo, s_hi)


def _tile(n, pref):
    if n <= pref:
        return n
    t = pref
    while t >= LANES:
        if n % t == 0:
            return t
        t -= LANES
    return n


def mm(name, a, b, *, ta=False, tb=False, add=None, out_dtype=f32, by_chip=False, epilogue=None, extra=(),
       out_dtypes=None, after=None):
    (k, m) = a.shape if ta else a.shape[::-1]
    (n, kb) = b.shape if tb else b.shape[::-1]
    assert k == kb, (name, a.shape, b.shape)
    tm, tk = _tile(m, MM_TILE), _tile(k, MM_TILE_K)
    tn = _tile(n // 4, MM_TILE) if by_chip else _tile(n, MM_TILE)
    vmem = VMEM_LIMIT
    narrow = _tile(n, MM_TILE // 2)
    whole_k = 2 * 2 * (tm + narrow) * k + 2 * 4 * tm * narrow * (1 + len(extra) + (add is not None))
    if k > tk and not by_chip and whole_k <= MM_WHOLE_K_BYTES:
        tn, tk, vmem = narrow, k, MM_WHOLE_K_BYTES + (8 << 20)
    nk = k // tk
    dims = (((0,) if ta else (1,), (1,) if tb else (0,)), ((), ()))
    extra = list(extra) + ([] if add is None else [add])
    out_dtypes = [out_dtype] if out_dtypes is None else list(out_dtypes)
    n_extra, n_out = len(extra), len(out_dtypes)
    n_after = 0 if after is None else 1

    def body(*refs):
        a_ref, b_ref = refs[:2]
        x_refs = refs[2:2 + n_extra]
        o_refs = refs[2 + n_extra + n_after:2 + n_extra + n_after + n_out]
        part = lax.dot_general(a_ref[...].astype(bf16), b_ref[...].astype(bf16), dims, preferred_element_type=f32)

        def finish(r):
            xs = [x[...] for x in x_refs]
            if add is not None:
                r = r + xs.pop().astype(f32)
            outs = (r,) if epilogue is None else epilogue(r, *xs)
            for o_ref, o in zip(o_refs, outs):
                o_ref[...] = o.astype(o_ref.dtype)

        if nk == 1:
            finish(part)
            return
        acc = refs[-1]
        kk = pl.program_id(2)

        @pl.when(kk == 0)
        def _():
            acc[...] = part

        @pl.when(kk > 0)
        def _():
            acc[...] += part

        @pl.when(kk == nk - 1)
        def _():
            finish(acc[...])

    a_spec = pl.BlockSpec((tk, tm), lambda i, j, q: (q, i)) if ta else pl.BlockSpec((tm, tk), lambda i, j, q: (i, q))
    b_spec = pl.BlockSpec((tn, tk), lambda i, j, q: (j, q)) if tb else pl.BlockSpec((tk, tn), lambda i, j, q: (q, j))
    x_spec = pl.BlockSpec((tm, tn), lambda i, j, q: (i, j))
    if by_chip:
        per_chip = n // 4 // tn
        o_spec = pl.BlockSpec((None, tm, tn), lambda i, j, q: (j // per_chip, i, j % per_chip))
        o_shape = (4, m, n // 4)
    else:
        o_spec, o_shape = x_spec, (m, n)
    afters = [] if after is None else [jnp.reshape(after, (1, 1))]
    res = pl.pallas_call(
        body, name=name, grid=(m // tm, n // tn, nk),
        in_specs=[a_spec, b_spec] + [x_spec] * n_extra + [pl.BlockSpec(memory_space=pl.ANY)] * n_after,
        out_specs=[o_spec] * n_out, out_shape=[jax.ShapeDtypeStruct(o_shape, dt) for dt in out_dtypes],
        scratch_shapes=[pltpu.VMEM((tm, tn), f32)] if nk > 1 else [],
        compiler_params=pltpu.CompilerParams(dimension_semantics=("arbitrary",) * 3, vmem_limit_bytes=vmem))(
            a, b, *extra, *afters)
    return res[0] if n_out == 1 else res


REGROUP_PIECES = 4


REGROUP_SUB = 5
REGROUP_FIELDS = 5


def _regroup_table(n_out, sources_of):
    import numpy as np
    tbl = np.zeros((REGROUP_FIELDS * REGROUP_PIECES, n_out), np.int32)
    for j in range(n_out):
        groups = sorted(sources_of(j).items())
        assert len(groups) <= REGROUP_PIECES, (j, len(groups))
        for p in range(REGROUP_PIECES):
            if p < len(groups):
                key, lanes = groups[p]
                shifts = {q - s for s, q in lanes}
                qs = sorted(q for _, q in lanes)
                assert len(shifts) == 1 and qs == list(range(qs[0], qs[-1] + 1)), (j, key)
                row = (key[0], key[1], shifts.pop(), qs[0], qs[-1] + 1)
            else:
                row = (0, 0, 0, 0, 0)
            tbl[REGROUP_FIELDS * p:REGROUP_FIELDS * (p + 1), j] = row
    return jnp.asarray(tbl)


def lane_regroup(name, src, table, src_spec, out_spec, out_shape, grid, src_width, sub):
    rows = src.shape[-2]
    n_src = REGROUP_PIECES * sub

    def body(tbl, *refs):
        o_ref = refs[n_src]
        step = pl.program_id(1)

        def moved(b, p):
            j = step * sub + b
            blk, shift, lo, hi = (tbl[REGROUP_FIELDS * p + f, j] for f in range(1, REGROUP_FIELDS))
            x = refs[b * REGROUP_PIECES + p][...]
            if src_width % LANES:
                lane = lax.broadcasted_iota(jnp.int32, (rows, LANES), 1)
                x = jnp.where(lane < src_width - blk * LANES, x, jnp.zeros((), src.dtype))
            pi = lax.broadcasted_iota(jnp.int32, (LANES, LANES), 0)
            qi = lax.broadcasted_iota(jnp.int32, (LANES, LANES), 1)
            sel = jnp.logical_and(qi - pi == shift, jnp.logical_and(qi >= lo, qi < hi))
            return jnp.dot(x, sel.astype(src.dtype), preferred_element_type=f32).astype(o_ref.dtype)

        for b in range(sub):
            lanes = slice(b * LANES, (b + 1) * LANES)
            o_ref[:, lanes] = moved(b, 0)
            for p in range(1, REGROUP_PIECES):
                j = step * sub + b

                @pl.when(tbl[REGROUP_FIELDS * p + 4, j] > tbl[REGROUP_FIELDS * p + 3, j])
                def _(b=b, p=p, lanes=lanes):
                    o_ref[:, lanes] += moved(b, p)

    return pl.pallas_call(
        body, name=name, out_shape=jax.ShapeDtypeStruct(out_shape, src.dtype),
        grid_spec=pltpu.PrefetchScalarGridSpec(
            num_scalar_prefetch=1, grid=grid,
            in_specs=[src_spec(p, b) for b in range(sub) for p in range(REGROUP_PIECES)], out_specs=out_spec),
        compiler_params=_params(2))(table, *[src] * n_src)


def w_in_layout(d_cols, shard_w, c_split, gap, n_blocks):
    def ext_of(c):
        return c if c < c_split else c + gap

    def fwd_sources(j):
        groups = {}
        for q in range(LANES):
            e = j * LANES + q
            c = e if e < c_split else e - gap
            if (c_split <= e < c_split + gap) or c >= d_cols:
                continue
            s, l = divmod(c, shard_w)
            groups.setdefault((s, l // LANES), []).append((l % LANES, q))
        return groups

    per_shard = -(-shard_w // LANES)

    def bwd_sources(j):
        s, b = divmod(j, per_shard)
        groups = {}
        for q in range(LANES):
            l = b * LANES + q
            if l >= shard_w:
                continue
            e = ext_of(s * shard_w + l)
            groups.setdefault((0, e // LANES), []).append((e % LANES, q))
        return groups

    return _regroup_table(n_blocks, fwd_sources), _regroup_table(4 * per_shard, bwd_sources), per_shard


def _in_spec(a, kind, tile):
    if kind == "row":
        return pl.BlockSpec((tile, a.shape[1]), lambda j, i: (i, 0))
    if kind == "const":
        return pl.BlockSpec(a.shape, lambda j, i: (0, 0))
    tag, cw, off = kind
    if tag == "rowc":
        return pl.BlockSpec((tile, cw), lambda j, i: (i, j + off))
    assert tag == "constc", kind
    return pl.BlockSpec((a.shape[0], cw), lambda j, i: (0, j + off))


def rowcall(name, fn, ins, outs, *, rows, tile=None, ncol=1):
    if tile is None:
        tile = min(TOK_TILE_BLOCKED if ncol > 1 else TOK_TILE, rows)
    n_in = len(ins)
    kinds = [o[3] for o in outs]

    def body(*refs):
        j, i = pl.program_id(0), pl.program_id(1)
        res = fn(*[r[...] for r in refs[:n_in]])
        for ref, val, kind in zip(refs[n_in:], res, kinds):
            if kind in ("row", "rowc"):
                ref[...] = val.astype(ref.dtype)
            else:
                first = (i == 0) if kind == "accc" else jnp.logical_and(i == 0, j == 0)

                @pl.when(first)
                def _(ref=ref, val=val):
                    ref[...] = val.astype(ref.dtype)

                @pl.when(jnp.logical_not(first))
                def _(ref=ref, val=val):
                    ref[...] += val.astype(ref.dtype)

    out_shape, out_specs = [], []
    for nr, nc, dtype, kind in outs:
        out_shape.append(jax.ShapeDtypeStruct((nr, nc), dtype))
        if kind == "row":
            out_specs.append(pl.BlockSpec((tile, nc), lambda j, i: (i, 0)))
        elif kind == "rowc":
            out_specs.append(pl.BlockSpec((tile, nc // ncol), lambda j, i: (i, j)))
        elif kind == "acc":
            out_specs.append(pl.BlockSpec((nr, nc), lambda j, i: (0, 0)))
        else:
            out_specs.append(pl.BlockSpec((nr, nc // ncol), lambda j, i: (0, j)))
    return pl.pallas_call(
        body, name=name, grid=(ncol, rows // tile), in_specs=[_in_spec(a, k, tile) for a, k in ins],
        out_specs=out_specs, out_shape=out_shape, compiler_params=_params(2))(*[a for a, _ in ins])


def rowvjp(name, fn, ins, cts, grads, *, rows, tile=None, ncol=1, adds=(), dup16=()):
    n_in = len(ins)
    ct_sizes = [len(c) for c in cts]
    flat_cts = [m for c in cts for m in c]
    n_ct = len(flat_cts)

    def wrapped(*vals):
        xs = list(vals[:n_in])
        gs = vals[n_in:n_in + n_ct]
        extra = vals[n_in + n_ct:]

        def f(*dvars):
            full = list(xs)
            for k, v in zip(grads, dvars):
                full[k] = v
            return fn(*full)

        outs, pull = jax.vjp(f, *[xs[k] for k in grads])
        cot, p = [], 0
        for o, size in zip(outs, ct_sizes):
            g = gs[p].astype(f32)
            for q in range(1, size):
                g = g + gs[p + q].astype(f32)
            cot.append(g.astype(o.dtype))
            p += size
        gv = list(pull(tuple(cot)))
        for (pos, _, _), e in zip(adds, extra):
            gv[pos] = gv[pos] + e.astype(gv[pos].dtype)
        return tuple(gv) + tuple(gv[pos] for pos in dup16)

    outs = []
    for k in grads:
        a, kind = ins[k]
        if kind == "row":
            outs.append((rows, a.shape[1] * ncol, f32, "rowc") if ncol > 1 else (rows, a.shape[1], f32, "row"))
        elif kind == "const":
            outs.append((a.shape[0], a.shape[1], f32, "acc"))
        elif kind[0] == "rowc":
            outs.append((rows, kind[1] * ncol, f32, "rowc"))
        else:
            outs.append((a.shape[0], kind[1] * ncol, f32, "accc"))
    for pos in dup16:
        nr, nc, _, kind = outs[pos]
        outs.append((nr, nc, bf16, kind))
    all_ins = list(ins) + flat_cts + [(a, kind) for _, a, kind in adds]
    return rowcall(name, wrapped, all_ins, outs, rows=rows, tile=tile, ncol=ncol)


def fn_rms(x, w):
    return (rms(x, w),)


def make_fn_gconv(n_norm_blocks):
    def fn(p, cw):
        c = cw[3:4] * p
        for jj in range(3):
            c = c + cw[jj:jj + 1] * tshift(p, 3 - jj)
        s = c * jax.nn.sigmoid(c)
        nrm = s * lax.rsqrt(jnp.sum(s * s, axis=-1, keepdims=True) + 1e-6)
        return (jnp.where(pl.program_id(0) < n_norm_blocks, nrm, s),)
    return fn


def fn_ggate(ps0, alog, dtb, e_g, e_b):
    g = -jnp.exp(alog) * softplus(ps0 + dtb)
    beta = jax.nn.sigmoid(ps0)
    return pdot(g, e_g, "nn", 6), pdot(beta, e_b, "nn", 6)


def fn_gpost(o, z, nw):
    return (rms(o, nw) * (z * jax.nn.sigmoid(z)),)


def fn_lerp(p, mu):
    return (p + (tshift(p, 1) - p) * mu,)


def fn_rprep(pk, psl, w0, a0, k_k, k_a, w2p, a2p, g2):
    g1, g2in = psl[:, :LANES], psl[:, LANES:]
    log_w = -softplus(-(w0 + bdot(jnp.tanh(g1), w2p))) - 0.5
    lw = -jnp.exp(log_w)
    a = jax.nn.sigmoid(a0 + bdot(g1, a2p))
    gate = bdot(jax.nn.sigmoid(g2in), g2)
    kkr = pk * k_k
    kk = kkr / jnp.maximum(jnp.sqrt(seg2sum(kkr * kkr)), 1e-12)
    kmod = pk * (1.0 + (a - 1.0) * k_a)
    return lw, kmod, kk, a, gate


def fn_rpost(y, r, kmod, v, gate, r_k, ln_w, ln_b):
    inv_n = 1.0 / RW_HEAD
    mean = seg2sum(y) * inv_n
    d = y - mean
    var = seg2sum(d * d) * inv_n
    yn = d * lax.rsqrt(var + RW_GN_EPS) * ln_w + ln_b
    bonus = seg2sum(r * kmod * r_k) * v
    return ((yn + bonus) * gate,)


def fn_xcore(q, k, v):
    s = bdot_nt(q, k) * (LANES ** -0.5)
    p = jax.nn.softmax(s, axis=-1)
    return (bdot(p, v),)


def fn_final(h, tgt, w):
    def loss_fn(h, w):
        return 0.5 * jnp.sum(jnp.mean(jnp.square(rms(h, w) - tgt), axis=-1))

    val, (dh, dw) = jax.value_and_grad(loss_fn, argnums=(0, 1))(h, w)
    return dh, dh.astype(bf16), dw, jnp.full((8, LANES), val, f32)


def fn_sumcols(n):
    def fn(x):
        w = x.shape[1] // n
        s = x[:, :w]
        for q in range(1, n):
            s = s + x[:, q * w:(q + 1) * w]
        return (s,)
    return fn


def _tri(c):
    ii = lax.broadcasted_iota(jnp.int32, (c, c), 0)
    jj = lax.broadcasted_iota(jnp.int32, (c, c), 1)
    return ii, jj


def _neumann_raw(m, steps):
    c = m.shape[-1]
    ii, jj = _tri(c)
    eye = (ii == jj).astype(f32)
    t, p = eye + m, m
    for _ in range(steps):
        p = _raw_dot(p, p, "nn", P_INV)
        t = _raw_dot(t, eye + p, "nn", P_INV)
    resid = eye - t + _raw_dot(m, t, "nn", P_RESID)
    return t + _raw_dot(t, resid, "nn", P_INV)


@functools.partial(jax.custom_vjp, nondiff_argnums=(1,))
def _neumann_inverse(m, steps):
    return _neumann_raw(m, steps)


def _neumann_fwd(m, steps):
    t = _neumann_raw(m, steps)
    return t, t


def _neumann_bwd(steps, t, g):
    return (_raw_dot(_raw_dot(t, g, "tn", P_RESID), t, "nt", P_RESID),)


_neumann_inverse.defvjp(_neumann_fwd, _neumann_bwd)


@jax.custom_vjp
def _known_inverse(m, t):
    return t


_known_inverse.defvjp(lambda m, t: (t, t), lambda t, g: (_neumann_bwd(0, t, g)[0], jnp.zeros_like(t)))


def _inverse(m, steps, kept):
    return _neumann_inverse(m, steps) if kept is None else _known_inverse(m, kept)


def cumsum_rows(x):
    t = x.shape[1]
    ii, jj = _tri(t)
    tri = jnp.broadcast_to((ii >= jj).astype(f32), (x.shape[0], t, t))
    return pdot(tri, x, "nn", P_CUMSUM)


def gdn_chunk(q, k, v, gb, bb, s, kept=None):
    c = q.shape[1]
    ii, jj = _tri(c)
    low = ii >= jj
    gcb = cumsum_rows(gb)
    gl = jnp.sum(gb, axis=1, keepdims=True)
    gc_col = gcb[:, :, :c]
    diff = gc_col - jnp.swapaxes(gc_col, 1, 2)
    decay = jnp.where(low, jnp.exp(jnp.where(low, diff, 0.0)), 0.0)
    qs = q * (q.shape[2] ** -0.5)
    kb = k * bb
    with_k = pdot(jnp.concatenate([kb, qs], axis=1), k, "nt", P_BULK)
    a = jnp.where(ii > jj, with_k[:, :c] * decay, 0.0)
    attn = with_k[:, c:] * decay
    t = _inverse(-a, c.bit_length() - 2, kept)
    eg = jnp.exp(gcb)
    uw = pdot(t, jnp.concatenate([v * bb, kb * eg], axis=2), "nn", P_BULK)
    u, w = uw[:, :, :LANES], uw[:, :, LANES:]
    kd = k * jnp.exp(gl - gcb)
    from_state = pdot(jnp.concatenate([w, qs * eg], axis=1), s, "nn", P_BULK)
    v_new = u - from_state[:, :c]
    o = from_state[:, c:] + pdot(attn, v_new, "nn", P_BULK)
    s_new = s * jnp.exp(gl) + pdot(kd, v_new, "tn", P_BULK)
    return o, s_new, t


def wkv_chunk(r, lw, k, v, kk, a, s, kept=None):
    t = r.shape[1]
    ii, jj = _tri(t)
    lo = lax.broadcasted_iota(jnp.int32, r.shape, 2) < RW_HEAD
    cl = cumsum_rows(lw)
    cl_last = jnp.sum(lw, axis=1, keepdims=True)
    al = -kk * jnp.exp(cl - lw)
    be = (a * kk) * jnp.exp(-cl)
    kt = k * jnp.exp(-cl)
    rt = r * jnp.exp(cl)

    def dot(xa, xb, mode="nn"):
        return pdot(xa, xb, mode, P_BULK)

    def sel(x_lo, x_hi):
        return jnp.where(lo, x_lo, x_hi)

    left = jnp.concatenate([jnp.where(lo, al, 0.0), jnp.where(lo, 0.0, al),
                            jnp.where(lo, rt, 0.0), jnp.where(lo, 0.0, rt)], axis=1)
    with_be, with_kt = dot(left, be, "nt"), dot(left, kt, "nt")

    def blocks(prod):
        below, upto = ii > jj, ii >= jj
        return [jnp.where(msk, prod[:, q * t:(q + 1) * t], 0.0) for q, msk in enumerate((below, below, upto, upto))]

    ab_lo, ab_hi, rb_lo, rb_hi = blocks(with_be)
    ak_lo, ak_hi, rk_lo, rk_hi = blocks(with_kt)
    from_state = dot(jnp.concatenate([al, rt], axis=1), s, "nt")
    x = from_state[:, :t] + sel(dot(ak_lo, v), dot(ak_hi, v))
    steps = t.bit_length() - 2
    inv_lo = _inverse(ab_lo, steps, None if kept is None else kept[:, :t])
    inv_hi = _inverse(ab_hi, steps, None if kept is None else kept[:, t:])
    u = sel(dot(inv_lo, x), dot(inv_hi, x))
    y = from_state[:, t:] + sel(dot(rb_lo, u) + dot(rk_lo, v), dot(rb_hi, u) + dot(rk_hi, v))
    vi = lax.broadcasted_iota(jnp.int32, s.shape, 1) < RW_HEAD
    ki = lax.broadcasted_iota(jnp.int32, s.shape, 2) < RW_HEAD
    s_new = jnp.where(vi == ki, (s + dot(u, be, "tn") + dot(v, kt, "tn")) * jnp.exp(cl_last), 0.0)
    return y, s_new, jnp.concatenate([inv_lo, inv_hi], axis=1)


def _scan_group(ncol, offs):
    g = SCAN_GROUP
    while g > 1 and (ncol % g or any(o % g for o in offs)):
        g //= 2
    return g


def scan_fwd(name, chunk_fn, ins, *, rows, chunk, ncol):
    n = rows // chunk
    n_in = len(ins)
    grp = _scan_group(ncol, [off for _, off in ins])
    tile = jax.ShapeDtypeStruct((grp, chunk, LANES), f32)
    kept = jax.eval_shape(chunk_fn, *[tile] * n_in, jax.ShapeDtypeStruct((grp, LANES, LANES), f32))[2].shape[1:]

    def body(*refs):
        o_ref, st_ref, kept_ref, s_scr = refs[n_in:]

        @pl.when(pl.program_id(1) == 0)
        def _():
            s_scr[...] = jnp.zeros_like(s_scr)

        cols = [slice(b * LANES, (b + 1) * LANES) for b in range(grp)]
        s = s_scr[...]
        st_ref[...] = s
        o, s_new, inv = chunk_fn(*[jnp.stack([r[:, c] for c in cols]) for r in refs[:n_in]], s)
        for b, c in enumerate(cols):
            o_ref[:, c] = o[b]
        kept_ref[...] = inv
        s_scr[...] = s_new

    def spec(off):
        return pl.BlockSpec((chunk, grp * LANES), lambda h, c: (c, h + off // grp))

    def per_chunk(shape):
        return pl.BlockSpec((grp, None) + tuple(shape), lambda h, c: (h, c, 0, 0))

    return pl.pallas_call(
        body, name=name, grid=(ncol // grp, n), in_specs=[spec(off) for _, off in ins],
        out_specs=[spec(0), per_chunk((LANES, LANES)), per_chunk(kept)],
        out_shape=[jax.ShapeDtypeStruct((rows, ncol * LANES), f32),
                   jax.ShapeDtypeStruct((ncol, n, LANES, LANES), f32),
                   jax.ShapeDtypeStruct((ncol, n) + tuple(kept), f32)],
        scratch_shapes=[pltpu.VMEM((grp, LANES, LANES), f32)], compiler_params=_params(2))(*[a for a, _ in ins])


def scan_bwd(name, chunk_fn, ins, states, kept, d_out, d_off, *, rows, chunk, ncol):
    n = rows // chunk
    n_in = len(ins)
    grp = _scan_group(ncol, [off for _, off in ins] + [d_off])

    def body(*refs):
        st_ref, kept_ref, do_ref = refs[n_in:n_in + 3]
        g_refs = refs[n_in + 3:2 * n_in + 3]
        ds_scr = refs[-1]

        @pl.when(pl.program_id(1) == 0)
        def _():
            ds_scr[...] = jnp.zeros_like(ds_scr)

        cols = [slice(b * LANES, (b + 1) * LANES) for b in range(grp)]

        def batch(ref):
            return jnp.stack([ref[:, c] for c in cols])

        inv = kept_ref[...]
        _, pull = jax.vjp(lambda *a: chunk_fn(*a, kept=inv)[:2], *[batch(r) for r in refs[:n_in]], st_ref[...])
        gs = pull((batch(do_ref), ds_scr[...]))
        for ref, g in zip(g_refs, gs[:n_in]):
            for b, c in enumerate(cols):
                ref[:, c] = g[b]
        ds_scr[...] = gs[n_in]

    def spec(off):
        return pl.BlockSpec((chunk, grp * LANES), lambda h, c: (n - 1 - c, h + off // grp))

    def per_chunk(shape):
        return pl.BlockSpec((grp, None) + tuple(shape), lambda h, c: (h, n - 1 - c, 0, 0))

    return pl.pallas_call(
        body, name=name, grid=(ncol // grp, n),
        in_specs=[spec(off) for _, off in ins] + [per_chunk(states.shape[2:]), per_chunk(kept.shape[2:]), spec(d_off)],
        out_specs=[spec(0)] * n_in, out_shape=[jax.ShapeDtypeStruct((rows, ncol * LANES), f32)] * n_in,
        scratch_shapes=[pltpu.VMEM((grp, LANES, LANES), f32)],
        compiler_params=_params(2))(*[a for a, _ in ins], states, kept, d_out)


def flip_exchange(name, arrs, flips, n_slots, slot_of, src_of, with_self, after=()):
    n = len(arrs)
    nf = len(flips)
    n_after = len(after)

    def body(*refs):
        ins, outs = refs[:n], refs[n + n_after:2 * n + n_after]
        send, recv, lsem = refs[2 * n + n_after:]
        me = (lax.axis_index("x"), lax.axis_index("y"), lax.axis_index("c"))
        copies = []
        for k in range(n):
            if with_self:
                cp = pltpu.make_async_copy(src_of(ins[k], me), outs[k].at[slot_of(me)], lsem.at[k])
                cp.start()
                copies.append(cp)
            for j, fl in enumerate(flips):
                peer = tuple(1 - m if f else m for m, f in zip(me, fl))
                cp = pltpu.make_async_remote_copy(
                    src_ref=src_of(ins[k], peer), dst_ref=outs[k].at[slot_of(me)], send_sem=send.at[k, j],
                    recv_sem=recv.at[k, j], device_id=peer, device_id_type=MESH)
                cp.start()
                copies.append(cp)
        for cp in copies:
            cp.wait()

    def out_sds(a):
        blk = src_of(jax.ShapeDtypeStruct(a.shape, a.dtype), None)
        return jax.ShapeDtypeStruct((n_slots,) + tuple(blk), a.dtype)

    any_spec = pl.BlockSpec(memory_space=pl.ANY)
    return pl.pallas_call(
        body, name=name, in_specs=[any_spec] * (n + n_after), out_specs=[any_spec] * n,
        out_shape=[out_sds(a) for a in arrs],
        scratch_shapes=[pltpu.SemaphoreType.DMA((n, nf)), pltpu.SemaphoreType.DMA((n, nf)),
                        pltpu.SemaphoreType.DMA((n,))],
        compiler_params=pltpu.CompilerParams(has_side_effects=True))(*arrs, *after)


_CHIP_FLIPS = ((1, 0, 0), (0, 1, 0), (1, 1, 0))
_ALL_FLIPS = ((0, 0, 1), (0, 1, 0), (0, 1, 1), (1, 0, 0), (1, 0, 1), (1, 1, 0), (1, 1, 1))


def _whole(ref, pos):
    return ref.shape if pos is None else ref


def _chip_block(ref, pos):
    return ref.shape[1:] if pos is None else ref.at[2 * pos[0] + pos[1]]


def _chip_slot(p):
    return 2 * p[0] + p[1]


_HBM = pl.BlockSpec(memory_space=pltpu.HBM)
_SEM = pl.BlockSpec(memory_space=pltpu.SEMAPHORE)
_DATAFLOW = pltpu.SideEffectType.DATAFLOW_SIDE_EFFECTING


def _split_copies(mode, refs, n, send, recv):
    me = (lax.axis_index("x"), lax.axis_index("y"), lax.axis_index("c"))
    sib = (me[0], me[1], 1 - me[2])
    lands = refs[:n] if mode == "handover" else refs[n:2 * n]
    copies = []
    for k, land in enumerate(lands):
        half = land.shape[1] // 2
        mine = pl.ds(pl.multiple_of(me[2] * half, 16), half)
        for j, fl in enumerate(_CHIP_FLIPS):
            peer = tuple(1 - m if f else m for m, f in zip(me, fl))
            if mode == "gather":
                src, dst, to = refs[k], land.at[_chip_slot(me)], peer
            elif mode == "scatter":
                src, dst, to = refs[k].at[_chip_slot(peer)], land.at[_chip_slot(me)], peer
            elif mode == "gather_half":
                src, dst, to = refs[k].at[mine], land.at[_chip_slot(me), mine], peer
            else:
                src = dst = land.at[_chip_slot(peer), mine]
                to = sib
            q = k * len(_CHIP_FLIPS) + j
            copies.append(pltpu.make_async_remote_copy(src_ref=src, dst_ref=dst, send_sem=send.at[q],
                                                       recv_sem=recv.at[q], device_id=to, device_id_type=MESH))
    return copies


def split_start(name, mode, ops, n):
    ops = [pltpu.with_memory_space_constraint(a, pltpu.HBM) for a in ops]
    m = len(ops)

    def body(*refs):
        for cp in _split_copies(mode, refs[:m], n, refs[m], refs[m + 1]):
            cp.start()
        refs[-1][...] = jnp.zeros_like(refs[-1])

    sems = pltpu.SemaphoreType.DMA((n * len(_CHIP_FLIPS),))
    outs = pl.pallas_call(
        body, name=name, in_specs=[_HBM] * m,
        out_shape=(sems, sems, *[pltpu.HBM(a.shape, a.dtype) for a in ops], jax.ShapeDtypeStruct((8, LANES), f32)),
        out_specs=(_SEM, _SEM, *[_HBM] * m, pl.BlockSpec(memory_space=pltpu.VMEM)),
        input_output_aliases={i: 2 + i for i in range(m)},
        compiler_params=pltpu.CompilerParams(has_side_effects=_DATAFLOW))(*ops)
    return (outs[0], outs[1], list(outs[2:2 + m]), mode, n), outs[-1][0, 0]


def split_wait(name, state, after, with_sources=False):
    send, recv, ops, mode, n = state
    m = len(ops)
    afters = list(after) if isinstance(after, (list, tuple)) else [after]

    def body(*refs):
        for cp in _split_copies(mode, refs[:m], n, refs[m], refs[m + 1]):
            cp.wait_send()
            cp.wait_recv()

    outs = pl.pallas_call(
        body, name=name, in_specs=[_HBM] * m + [_SEM, _SEM] + [pl.BlockSpec(memory_space=pl.ANY)] * len(afters),
        out_shape=tuple(pltpu.HBM(a.shape, a.dtype) for a in ops), out_specs=tuple([_HBM] * m),
        input_output_aliases={i: i for i in range(m)},
        compiler_params=pltpu.CompilerParams(has_side_effects=_DATAFLOW))(*ops, send, recv, *afters)
    return (list(outs[m - n:]), list(outs[:m - n])) if with_sources else list(outs[m - n:])


def chips_start(name, arrs, src_of, halves=False):
    me = _chip_slot((lax.axis_index("x"), lax.axis_index("y")))
    lands = []
    for a in arrs:
        blk = tuple(src_of(jax.ShapeDtypeStruct(a.shape, a.dtype), None))
        land = lax.empty((4,) + blk, a.dtype)
        lands.append(lax.dynamic_update_index_in_dim(land, a, me, 0) if src_of is _whole else land)
    mode = "scatter" if src_of is _chip_block else ("gather_half" if halves else "gather")
    return split_start(name, mode, list(arrs) + lands, len(arrs))


chips_wait = split_wait


def handover_start(name, lands):
    return split_start(name, "handover", lands, len(lands))


def gather_chips_halves(name, arrs):
    n = len(arrs)
    nf = len(_CHIP_FLIPS)
    split = [a.shape[0] % 32 == 0 for a in arrs]

    def body(*refs):
        ins, outs = refs[:n], refs[n:2 * n]
        send1, recv1, send2, recv2, lsem = refs[2 * n:]
        me = (lax.axis_index("x"), lax.axis_index("y"), lax.axis_index("c"))
        sib = (me[0], me[1], 1 - me[2])
        peers = [tuple(1 - m if f else m for m, f in zip(me, fl)) for fl in _CHIP_FLIPS]
        local, first, second = [], [], []
        for k in range(n):
            cp = pltpu.make_async_copy(ins[k], outs[k].at[_chip_slot(me)], lsem.at[k])
            cp.start()
            local.append(cp)
            half = ins[k].shape[0] // 2
            rows = pl.ds(pl.multiple_of(me[2] * half, 16), half) if split[k] else pl.ds(0, ins[k].shape[0])
            for j, peer in enumerate(peers):
                cp = pltpu.make_async_remote_copy(
                    src_ref=ins[k].at[rows], dst_ref=outs[k].at[_chip_slot(me), rows], send_sem=send1.at[k, j],
                    recv_sem=recv1.at[k, j], device_id=peer, device_id_type=MESH)
                cp.start()
                first.append((k, j, rows, cp))
        for k, j, rows, cp in first:
            cp.wait_recv()
            if split[k]:
                got = outs[k].at[_chip_slot(peers[j]), rows]
                fwd = pltpu.make_async_remote_copy(src_ref=got, dst_ref=got, send_sem=send2.at[k, j],
                                                   recv_sem=recv2.at[k, j], device_id=sib, device_id_type=MESH)
                fwd.start()
                second.append(fwd)
        for _, _, _, cp in first:
            cp.wait_send()
        for cp in second:
            cp.wait()
        for cp in local:
            cp.wait()

    any_spec = pl.BlockSpec(memory_space=pl.ANY)
    sems = pltpu.SemaphoreType.DMA((n, nf))
    return pl.pallas_call(
        body, name=name, in_specs=[any_spec] * n, out_specs=[any_spec] * n,
        out_shape=[jax.ShapeDtypeStruct((4,) + a.shape, a.dtype) for a in arrs],
        scratch_shapes=[sems, sems, sems, sems, pltpu.SemaphoreType.DMA((n,))],
        compiler_params=pltpu.CompilerParams(has_side_effects=True))(*arrs)


def swap_sibling(name, arrs):
    outs = flip_exchange(name, arrs, ((0, 0, 1),), 1, lambda p: 0, _whole, False)
    return [o[0] for o in outs]


def gather_all(arrs, after=()):
    return flip_exchange("gather_all", arrs, _ALL_FLIPS, 8, lambda p: 4 * p[0] + 2 * p[1] + p[2], _whole, True,
                         after=after)


def _row_tile(nr, nc, n_arrays):
    budget = (20 << 20) // (n_arrays * 2 * 4 * max(nc, LANES))
    t = min(nr, budget) // 16 * 16
    while t > 0 and nr % t:
        t -= 16
    return t if t > 0 else nr


def sum_slots(name, x, own):
    ns, nr, nc = x.shape
    tile = _row_tile(nr, nc, ns + 1)
    me = jnp.reshape(_chip_slot((lax.axis_index("x"), lax.axis_index("y"))), (1,)).astype(jnp.int32)

    def body(me_ref, own_ref, *refs):
        o_ref = refs[-1]
        s = own_ref[...].astype(f32)
        for r in refs[:-1]:
            s = s + r[...].astype(f32)
        o_ref[...] = s.astype(o_ref.dtype)

    def slot(q):
        return pl.BlockSpec((None, tile, nc), lambda i, t: ((t[0] + q) % ns, i, 0))

    return pl.pallas_call(
        body, name=name, out_shape=jax.ShapeDtypeStruct((nr, nc), x.dtype),
        grid_spec=pltpu.PrefetchScalarGridSpec(
            num_scalar_prefetch=1, grid=(nr // tile,), in_specs=[slot(q) for q in range(ns)],
            out_specs=pl.BlockSpec((tile, nc), lambda i, t: (i, 0))),
        compiler_params=_params(1))(me, own, *[x] * (ns - 1))


def adamw(name, w, g_parts, m, v):
    nr, nc = w.shape[-2:]
    n_g = len(g_parts)
    tile = _row_tile(nr, nc, 7 + n_g)

    def body(*refs):
        w_ref, m_ref, v_ref = refs[:3]
        g = refs[3][...].astype(f32)
        for r in refs[4:3 + n_g]:
            g = g + r[...].astype(f32)
        g_ref, d_ref, nm_ref, nv_ref = refs[3 + n_g:]
        nm = ADAM_B1 * m_ref[...] + (1.0 - ADAM_B1) * g
        nv = ADAM_B2 * v_ref[...] + (1.0 - ADAM_B2) * jnp.square(g)
        m_hat = nm / (1.0 - ADAM_B1 ** ADAM_STEP)
        v_hat = nv / (1.0 - ADAM_B2 ** ADAM_STEP)
        g_ref[...] = g
        d_ref[...] = -ADAM_LR * (m_hat / (jnp.sqrt(v_hat) + ADAM_EPS) + ADAM_WD * w_ref[...])
        nm_ref[...] = nm
        nv_ref[...] = nv

    spec = pl.BlockSpec((tile, nc), lambda i: (i, 0))
    spec3 = pl.BlockSpec((None, tile, nc), lambda i: (0, i, 0)) if w.ndim == 3 else spec
    return pl.pallas_call(
        body, name=name, grid=(nr // tile,), in_specs=[spec3] * 3 + [spec] * n_g, out_specs=[spec3] * 4,
        out_shape=[jax.ShapeDtypeStruct(w.shape, f32)] * 4, compiler_params=_params(1))(w, m, v, *g_parts)


def adamw_packed(w, g8, m, v):
    nr, nc = w.shape

    def body(w_ref, g_ref, m_ref, v_ref, go_ref, d_ref, nm_ref, nv_ref):
        g = g_ref[0]
        for q in range(1, 8):
            g = g + g_ref[q]
        nm = ADAM_B1 * m_ref[...] + (1.0 - ADAM_B1) * g
        nv = ADAM_B2 * v_ref[...] + (1.0 - ADAM_B2) * jnp.square(g)
        m_hat = nm / (1.0 - ADAM_B1 ** ADAM_STEP)
        v_hat = nv / (1.0 - ADAM_B2 ** ADAM_STEP)
        go_ref[...] = g
        d_ref[...] = -ADAM_LR * (m_hat / (jnp.sqrt(v_hat) + ADAM_EPS) + ADAM_WD * w_ref[...])
        nm_ref[...] = nm
        nv_ref[...] = nv

    return pl.pallas_call(body, name="adamw_packed", out_shape=[jax.ShapeDtypeStruct((nr, nc), f32)] * 4,
                          compiler_params=pltpu.CompilerParams(vmem_limit_bytes=VMEM_LIMIT))(w, g8, m, v)


def _pack(vectors):
    rows = []
    for a in vectors:
        flat = a.reshape(-1).astype(f32)
        pad = (-flat.shape[0]) % LANES
        rows.append(jnp.pad(flat, (0, pad)).reshape(-1, LANES))
    packed = jnp.concatenate(rows, axis=0)
    return jnp.pad(packed, ((0, (-packed.shape[0]) % 8), (0, 0)))


def _unpack(packed, like):
    out, r = [], 0
    for a in like:
        n = a.size
        nr = -(-n // LANES)
        out.append(packed[r:r + nr].reshape(-1)[:n].reshape(a.shape))
        r += nr
    return out


def kernel(x, mem, mix_norm_w, w_in, dn_conv_w, dn_a_log, dn_dt_bias, dn_norm_w, rw_mu, rw_w0, rw_w2, rw_a0, rw_a2, rw_g2, rw_k_k, rw_k_a, rw_r_k, rw_ln_w, rw_ln_b, w_out, xa_norm_w, mem_norm_w, xa_wq, xa_wk, xa_wv, xa_wo, ffn_norm_w, ffn_w1, ffn_w2, final_norm_w, loss_target, m_mix_norm_w, m_w_in, m_dn_conv_w, m_dn_a_log, m_dn_dt_bias, m_dn_norm_w, m_rw_mu, m_rw_w0, m_rw_w2, m_rw_a0, m_rw_a2, m_rw_g2, m_rw_k_k, m_rw_k_a, m_rw_r_k, m_rw_ln_w, m_rw_ln_b, m_w_out, m_xa_norm_w, m_mem_norm_w, m_xa_wq, m_xa_wk, m_xa_wv, m_xa_wo, m_ffn_norm_w, m_ffn_w1, m_ffn_w2, m_final_norm_w, v_mix_norm_w, v_w_in, v_dn_conv_w, v_dn_a_log, v_dn_dt_bias, v_dn_norm_w, v_rw_mu, v_rw_w0, v_rw_w2, v_rw_a0, v_rw_a2, v_rw_g2, v_rw_k_k, v_rw_k_a, v_rw_r_k, v_rw_ln_w, v_rw_ln_b, v_w_out, v_xa_norm_w, v_mem_norm_w, v_xa_wq, v_xa_wk, v_xa_wv, v_xa_wo, v_ffn_norm_w, v_ffn_w1, v_ffn_w2, v_final_norm_w):
    weights = dict(mix_norm_w=mix_norm_w, w_in=w_in, dn_conv_w=dn_conv_w, dn_a_log=dn_a_log, dn_dt_bias=dn_dt_bias, dn_norm_w=dn_norm_w, rw_mu=rw_mu, rw_w0=rw_w0, rw_w2=rw_w2, rw_a0=rw_a0, rw_a2=rw_a2, rw_g2=rw_g2, rw_k_k=rw_k_k, rw_k_a=rw_k_a, rw_r_k=rw_r_k, rw_ln_w=rw_ln_w, rw_ln_b=rw_ln_b, w_out=w_out, xa_norm_w=xa_norm_w, mem_norm_w=mem_norm_w, xa_wq=xa_wq, xa_wk=xa_wk, xa_wv=xa_wv, xa_wo=xa_wo, ffn_norm_w=ffn_norm_w, ffn_w1=ffn_w1, ffn_w2=ffn_w2, final_norm_w=final_norm_w)
    mom_m = dict(mix_norm_w=m_mix_norm_w, w_in=m_w_in, dn_conv_w=m_dn_conv_w, dn_a_log=m_dn_a_log, dn_dt_bias=m_dn_dt_bias, dn_norm_w=m_dn_norm_w, rw_mu=m_rw_mu, rw_w0=m_rw_w0, rw_w2=m_rw_w2, rw_a0=m_rw_a0, rw_a2=m_rw_a2, rw_g2=m_rw_g2, rw_k_k=m_rw_k_k, rw_k_a=m_rw_k_a, rw_r_k=m_rw_r_k, rw_ln_w=m_rw_ln_w, rw_ln_b=m_rw_ln_b, w_out=m_w_out, xa_norm_w=m_xa_norm_w, mem_norm_w=m_mem_norm_w, xa_wq=m_xa_wq, xa_wk=m_xa_wk, xa_wv=m_xa_wv, xa_wo=m_xa_wo, ffn_norm_w=m_ffn_norm_w, ffn_w1=m_ffn_w1, ffn_w2=m_ffn_w2, final_norm_w=m_final_norm_w)
    mom_v = dict(mix_norm_w=v_mix_norm_w, w_in=v_w_in, dn_conv_w=v_dn_conv_w, dn_a_log=v_dn_a_log, dn_dt_bias=v_dn_dt_bias, dn_norm_w=v_dn_norm_w, rw_mu=v_rw_mu, rw_w0=v_rw_w0, rw_w2=v_rw_w2, rw_a0=v_rw_a0, rw_a2=v_rw_a2, rw_g2=v_rw_g2, rw_k_k=v_rw_k_k, rw_k_a=v_rw_k_a, rw_r_k=v_rw_r_k, rw_ln_w=v_rw_ln_w, rw_ln_b=v_rw_ln_b, w_out=v_w_out, xa_norm_w=v_xa_norm_w, mem_norm_w=v_mem_norm_w, xa_wq=v_xa_wq, xa_wk=v_xa_wk, xa_wv=v_xa_wv, xa_wo=v_xa_wo, ffn_norm_w=v_ffn_norm_w, ffn_w1=v_ffn_w1, ffn_w2=v_ffn_w2, final_norm_w=v_final_norm_w)
    names = list(weights)

    seq, d = x.shape[1], x.shape[2]
    dnw = d // 2
    rww = d - dnw
    nh, nb = dnw // LANES, rww // LANES
    n_mem = mem.shape[1]
    lw_dim, la_dim, lg_dim = rw_w2.shape[1], rw_a2.shape[1], rw_g2.shape[1]
    assert lw_dim + la_dim == LANES and lg_dim == LANES and dnw % LANES == 0 and rww % LANES == 0
    xs, mems, tgt = x[0], mem[0], loss_target[0]

    col_sharded = ("w_in", "xa_wo", "ffn_w1", "dn_conv_w", "rw_w2", "rw_a2", "rw_g2")
    row_sharded = ("w_out", "xa_wq", "xa_wk", "xa_wv", "ffn_w2")
    f32_payload = ("dn_conv_w", "rw_w2", "rw_a2", "rw_g2")
    sharded = col_sharded + row_sharded
    payload = {n: weights[n][0].astype(f32 if n in f32_payload else bf16) for n in sharded}
    shard_w = w_in.shape[2]
    pad_w = -(-shard_w // LANES) * LANES
    first = ("w_in", "dn_conv_w", "rw_w2", "rw_a2", "rw_g2")
    mid = ("w_out", "xa_wq", "xa_wk", "xa_wv", "xa_wo")
    late = ("ffn_w1", "ffn_w2")
    gathered = dict(zip(first, gather_chips_halves("gather_first", [payload[n] for n in first])))
    ordered = lax.optimization_barrier(([gathered[n] for n in first], [payload[n] for n in mid + late]))
    gathered = dict(zip(first, ordered[0]))
    payload.update(zip(mid + late, ordered[1]))
    mid_state, tok_mid = chips_start("gather_mid_start", [payload[n] for n in mid], _whole, halves=True)
    late_state, tok_late = chips_start("gather_late_start", [payload[n] for n in late], _whole, halves=True)
    mix_norm_w_in = mix_norm_w + (tok_mid + tok_late)

    def full(n):
        g = gathered[n]
        if n in col_sharded:
            return g.transpose(1, 0, 2).reshape(g.shape[1], 4 * g.shape[2])
        return g.reshape(4 * g.shape[1], g.shape[2])

    c_rw0 = 4 * dnw + 2 * nh
    c_rw = 3 * rww
    eb_ab, eb_r = 4 * nh, 4 * nh + 1
    eb_l1 = eb_r + 3 * nb
    n_ext = -(-(eb_l1 + 2) // 4) * 4
    tbl_fwd, tbl_bwd, per_shard = w_in_layout(4 * shard_w, shard_w, c_rw0, LANES - 2 * nh, n_ext)
    sub_f = max(s for s in range(1, REGROUP_SUB + 1) if n_ext % s == 0)
    sub_b = max(s for s in range(1, REGROUP_SUB + 1) if per_shard % s == 0)
    w_ext = lane_regroup(
        "w_in_regroup", gathered["w_in"], tbl_fwd,
        lambda p, b: pl.BlockSpec(
            (None, d, LANES), lambda i, j, t: (t[REGROUP_FIELDS * p, j * sub_f + b], 0,
                                               t[REGROUP_FIELDS * p + 1, j * sub_f + b])),
        pl.BlockSpec((d, sub_f * LANES), lambda i, j, t: (0, j)), (d, n_ext * LANES), (1, n_ext // sub_f), shard_w,
        sub_f)
    conv_w = full("dn_conv_w")
    w2p = jnp.concatenate([full("rw_w2"), jnp.zeros((la_dim, rww), f32)], axis=0)
    a2p = jnp.concatenate([jnp.zeros((lw_dim, rww), f32), full("rw_a2")], axis=0)
    g2 = full("rw_g2")
    xaw = xa_wq.shape[2]
    nxh = xaw // LANES
    ffn = 4 * ffn_w1.shape[2]

    def lane_row(vec):
        return jnp.pad(vec.reshape(1, -1), ((0, 0), (0, LANES - vec.size)))

    alog_row, dtb_row = lane_row(dn_a_log), lane_row(dn_dt_bias)
    head_of_col = jnp.arange(dnw)[None, :] // LANES
    e_g = (jnp.arange(LANES)[:, None] == head_of_col).astype(f32)
    e_b = (jnp.arange(LANES)[:, None] == head_of_col + nh).astype(f32)
    mu_main, mu_small = rw_mu[:, :c_rw], rw_mu[:, c_rw:]
    r_k_row = rw_r_k.reshape(1, rww)
    fnw = final_norm_w.reshape(1, d)
    qb, kb_, vb, zb = 0, nh, 2 * nh, 3 * nh
    rb0 = eb_r
    cc = lambda w_, o_: ("constc", w_, o_)
    rc = lambda o_: ("rowc", LANES, o_)

    (u16,) = rowcall("norm_mix", fn_rms, [(xs, "row"), (mix_norm_w_in, "const")], [(seq, d, bf16, "row")], rows=seq)
    p_main = p_small = mm("in_proj", u16, w_ext)

    fn_gconv = make_fn_gconv(2 * nh)
    (qkv,) = rowcall("gdn_conv", fn_gconv, [(p_main, rc(0)), (conv_w, cc(LANES, 0))],
                     [(seq, 3 * dnw, f32, "rowc")], rows=seq, tile=seq, ncol=3 * nh)
    gate_ins = [(p_small, rc(eb_ab)), (alog_row, "const"), (dtb_row, "const"), (e_g, "const"), (e_b, "const")]
    g_b, beta_b = rowcall("gdn_gate", fn_ggate, gate_ins, [(seq, dnw, f32, "row")] * 2, rows=seq)
    gdn_ins = [(qkv, qb), (qkv, kb_), (qkv, vb), (g_b, 0), (beta_b, 0)]
    o_raw, gdn_states, gdn_kept = scan_fwd("gdn_scan", gdn_chunk, gdn_ins, rows=seq, chunk=GDN_CHUNK, ncol=nh)
    mid_state, tok = handover_start("gather_mid_pass", chips_wait("gather_mid_wait", mid_state, o_raw))
    gpost_ins = [(o_raw, rc(0)), (p_main, rc(zb)), (dn_norm_w + tok, "const")]
    (o_dn,) = rowcall("gdn_post", fn_gpost, gpost_ins, [(seq, dnw, bf16, "rowc")], rows=seq, ncol=nh)

    (prw,) = rowcall("rw_lerp_main", fn_lerp, [(p_main, rc(rb0)), (mu_main, cc(LANES, 0))],
                     [(seq, c_rw, f32, "rowc")], rows=seq, tile=seq, ncol=3 * nb)
    (psl,) = rowcall("rw_lerp_small", fn_lerp, [(p_small, rc(eb_l1)), (mu_small, cc(LANES, 0))],
                     [(seq, 2 * LANES, f32, "rowc")], rows=seq, tile=seq, ncol=2)
    rprep_ins = [(prw, rc(nb)), (psl, "row"), (rw_w0, cc(LANES, 0)), (rw_a0, cc(LANES, 0)), (rw_k_k, cc(LANES, 0)),
                 (rw_k_a, cc(LANES, 0)), (w2p, cc(LANES, 0)), (a2p, cc(LANES, 0)), (g2, cc(LANES, 0))]
    lw, kmod, kk, a_rw, gate = rowcall("rw_prep", fn_rprep, rprep_ins, [(seq, rww, f32, "rowc")] * 5,
                                        rows=seq, ncol=nb)
    wkv_ins = [(prw, 0), (lw, 0), (kmod, 0), (prw, 2 * nb), (kk, 0), (a_rw, 0)]
    y_rw, wkv_states, wkv_kept = scan_fwd("wkv_scan", wkv_chunk, wkv_ins, rows=seq, chunk=WKV_CHUNK, ncol=nb)
    rpost_ins = [(y_rw, rc(0)), (prw, rc(0)), (kmod, rc(0)), (prw, rc(2 * nb)), (gate, rc(0)),
                 (r_k_row, cc(LANES, 0)), (rw_ln_w, cc(LANES, 0)), (rw_ln_b, cc(LANES, 0))]
    (o_rw,) = rowcall("rw_post", fn_rpost, rpost_ins, [(seq, rww, bf16, "rowc")], rows=seq, ncol=nb)

    o_cat = jnp.concatenate([o_dn, o_rw], axis=1)
    late_state, tok = handover_start("gather_late_pass", chips_wait("gather_late_wait", late_state, o_cat))
    gathered.update(zip(mid, chips_wait("gather_mid_pass_wait", mid_state, o_cat)))
    w_out_f, wq_f, wk_f, wv_f, wo_f = full("w_out"), full("xa_wq"), full("xa_wk"), full("xa_wv"), full("xa_wo")
    h1 = mm("out_proj", o_cat, w_out_f, add=xs)

    (hn16,) = rowcall("norm_xa", fn_rms, [(h1, "row"), (xa_norm_w + tok, "const")], [(seq, d, bf16, "row")],
                      rows=seq)
    (mn16,) = rowcall("norm_mem", fn_rms, [(mems, "row"), (mem_norm_w, "const")], [(n_mem, d, bf16, "row")],
                      rows=n_mem)
    q_xa = mm("xa_q", hn16, wq_f)
    k_xa = mm("xa_k", mn16, wk_f)
    v_xa = mm("xa_v", mn16, wv_f)
    xcore_ins = [(q_xa, rc(0)), (k_xa, cc(LANES, 0)), (v_xa, cc(LANES, 0))]
    (o_xa,) = rowcall("xa_core", fn_xcore, xcore_ins, [(seq, xaw, bf16, "rowc")], rows=seq, ncol=nxh)
    h2 = mm("xa_o", o_xa, wo_f, add=h1)

    (fn16,) = rowcall("norm_ffn", fn_rms, [(h2, "row"), (ffn_norm_w, "const")], [(seq, d, bf16, "row")], rows=seq)
    gathered.update(zip(late, chips_wait("gather_late_pass_wait", late_state, fn16)))
    w1_f, w2_f = full("ffn_w1"), full("ffn_w2")
    a1_16, hid16 = mm("ffn_up", fn16, w1_f, epilogue=lambda r: (r, jnp.square(jnp.maximum(r, 0.0))),
                      out_dtypes=(bf16, bf16))
    h3 = mm("ffn_down", hid16, w2_f, add=h2)

    dh3, dh3_16, d_fnw, loss_rows = rowcall(
        "loss_head", fn_final, [(h3, "row"), (tgt, "row"), (fnw, "const")],
        [(seq, d, f32, "row"), (seq, d, bf16, "row"), (1, d, f32, "acc"), (8, LANES, f32, "acc")], rows=seq)

    da1_16 = mm("ffn_down_dx", dh3_16, w2_f, tb=True, extra=[a1_16], out_dtype=bf16,
                epilogue=lambda r, a1: (r * (2.0 * jnp.maximum(a1.astype(f32), 0.0)),))
    def by_chip(n, g):
        if g.ndim == 3:
            return g
        if n in col_sharded:
            return g.reshape(g.shape[0], 4, g.shape[1] // 4).transpose(1, 0, 2)
        return g.reshape(4, g.shape[0] // 4, g.shape[1])

    g_ffn_w2 = mm("ffn_down_dw", hid16, dh3_16, ta=True, out_dtype=bf16)
    ffn_w2_g, tok = chips_start("scatter_ffn_w2_start", [by_chip("ffn_w2", g_ffn_w2)], _chip_block)
    g_ffn_w1 = mm("ffn_up_dw", fn16, da1_16, ta=True, out_dtype=bf16, by_chip=True, after=tok)
    ffn_w1_g, tok = chips_start("scatter_ffn_w1_start", [g_ffn_w1], _chip_block)
    dfn = mm("ffn_up_dx", da1_16, w1_f, tb=True, after=tok)
    dh2, d_ffn_nw, dh2_16 = rowvjp("norm_ffn_bwd", fn_rms, [(h2, "row"), (ffn_norm_w, "const")],
                                   [[(dfn, "row")]], [0, 1], rows=seq, adds=[(0, dh3, "row")], dup16=[0])

    do_xa = mm("xa_o_dx", dh2_16, wo_f, tb=True)
    g_xa_wo = mm("xa_o_dw", o_xa, dh2_16, ta=True, out_dtype=bf16, by_chip=True)
    dq_xa, dk_xa, dv_xa, dq16 = rowvjp("xa_core_bwd", fn_xcore, xcore_ins, [[(do_xa, rc(0))]], [0, 1, 2],
                                       rows=seq, ncol=nxh, dup16=[0])
    g_xa_wq = mm("xa_q_dw", hn16, dq16, ta=True, out_dtype=bf16)
    dhn = mm("xa_q_dx", dq16, wq_f, tb=True)
    dh1, d_xa_nw, dh1_16 = rowvjp("norm_xa_bwd", fn_rms, [(h1, "row"), (xa_norm_w, "const")], [[(dhn, "row")]],
                                  [0, 1], rows=seq, adds=[(0, dh2, "row")], dup16=[0])
    dk16, dv16 = dk_xa.astype(bf16), dv_xa.astype(bf16)
    g_xa_wk = mm("xa_k_dw", mn16, dk16, ta=True, out_dtype=bf16)
    g_xa_wv = mm("xa_v_dw", mn16, dv16, ta=True, out_dtype=bf16)
    dmn = mm("xa_v_dx", dv16, wv_f, tb=True, add=mm("xa_k_dx", dk16, wk_f, tb=True))
    (d_mem_nw,) = rowvjp("norm_mem_bwd", fn_rms, [(mems, "row"), (mem_norm_w, "const")], [[(dmn, "row")]], [1],
                         rows=n_mem)

    g_w_out = mm("out_proj_dw", o_cat, dh1_16, ta=True, out_dtype=bf16)
    mid_grads = dict(w_out=g_w_out, xa_wq=g_xa_wq, xa_wk=g_xa_wk, xa_wv=g_xa_wv, xa_wo=g_xa_wo)
    mid_g, tok = chips_start("scatter_mid_start", [by_chip(n, mid_grads[n]) for n in mid], _chip_block)
    do_cat = mm("out_proj_dx", dh1_16, w_out_f, tb=True, after=tok)

    dy, dr_a, dkmod_a, dv_a, dgate, d_r_k, d_ln_w, d_ln_b = rowvjp(
        "rw_post_bwd", fn_rpost, rpost_ins, [[(do_cat, rc(nh))]], [0, 1, 2, 3, 4, 5, 6, 7], rows=seq, ncol=nb)
    dr_b, dlw, dkmod_b, dv_b, dkk, da_rw = scan_bwd("wkv_scan_bwd", wkv_chunk, wkv_ins, wkv_states, wkv_kept, dy, 0,
                                                    rows=seq, chunk=WKV_CHUNK, ncol=nb)
    rprep_cts = [[(dlw, rc(0))], [(dkmod_a, rc(0)), (dkmod_b, rc(0))], [(dkk, rc(0))], [(da_rw, rc(0))],
                 [(dgate, rc(0))]]
    dpk, dpsl_parts, d_w0, d_a0, d_k_k, d_k_a, d_w2p, d_a2p, d_g2 = rowvjp(
        "rw_prep_bwd", fn_rprep, rprep_ins, rprep_cts, [0, 1, 2, 3, 4, 5, 6, 7, 8], rows=seq, ncol=nb)
    (dpsl,) = rowcall("rw_prep_sum", fn_sumcols(nb), [(dpsl_parts, "row")], [(seq, 2 * LANES, f32, "row")], rows=seq)

    def lerp_bwd(tag, p, p_off, mu, mu_off, ct_lists, ncol):
        return rowvjp("rw_lerp_bwd_" + tag, fn_lerp, [(p, rc(p_off)), (mu, cc(LANES, mu_off))], [ct_lists], [0, 1],
                      rows=seq, tile=seq, ncol=ncol, dup16=[0])

    _, dmu_r, dpr16 = lerp_bwd("r", p_main, rb0, mu_main, 0, [(dr_a, rc(0)), (dr_b, rc(0))], nb)
    _, dmu_k, dpk16 = lerp_bwd("k", p_main, rb0 + nb, mu_main, nb, [(dpk, rc(0))], nb)
    _, dmu_v, dpv16 = lerp_bwd("v", p_main, rb0 + 2 * nb, mu_main, 2 * nb, [(dv_a, rc(0)), (dv_b, rc(0))], nb)
    _, dmu_s, dps12_16 = lerp_bwd("small", p_small, eb_l1, mu_small, 0, [(dpsl, rc(0))], 2)

    do_raw, dz, d_dn_nw, dz16 = rowvjp("gdn_post_bwd", fn_gpost, gpost_ins, [[(do_cat, rc(0))]], [0, 1, 2],
                                       rows=seq, ncol=nh, dup16=[1])
    dq_g, dk_g, dv_g, dg_b, dbeta_b = scan_bwd("gdn_scan_bwd", gdn_chunk, gdn_ins, gdn_states, gdn_kept, do_raw, 0,
                                               rows=seq, chunk=GDN_CHUNK, ncol=nh)
    dps0, d_alog, d_dtb, dps0_16 = rowvjp("gdn_gate_bwd", fn_ggate, gate_ins, [[(dg_b, "row")], [(dbeta_b, "row")]],
                                          [0, 1, 2], rows=seq, dup16=[0])
    dqkv = jnp.concatenate([dq_g, dk_g, dv_g], axis=1)
    _, d_conv_w, dqkv16 = rowvjp("gdn_conv_bwd", fn_gconv, [(p_main, rc(0)), (conv_w, cc(LANES, 0))],
                                 [[(dqkv, rc(0))]], [0, 1], rows=seq, tile=seq, ncol=3 * nh, dup16=[0])

    dp16 = jnp.concatenate([dqkv16, dz16, dps0_16, dpr16, dpk16, dpv16, dps12_16,
                            jnp.zeros((seq, (n_ext - eb_l1 - 2) * LANES), bf16)], axis=1)
    g_w_ext = mm("in_proj_dw", u16, dp16, ta=True, out_dtype=bf16)
    g_w_in = lane_regroup(
        "w_in_grad_regroup", g_w_ext, tbl_bwd,
        lambda p, b: pl.BlockSpec((d, LANES), lambda i, j, t: (0, t[REGROUP_FIELDS * p + 1, j * sub_b + b])),
        pl.BlockSpec((None, d, sub_b * LANES),
                     lambda i, j, t: (j // (per_shard // sub_b), 0, j % (per_shard // sub_b))),
        (4, d, pad_w), (1, 4 * per_shard // sub_b), n_ext * LANES, sub_b)
    first_grads = dict(w_in=g_w_in, dn_conv_w=d_conv_w, rw_w2=d_w2p[:lw_dim], rw_a2=d_a2p[lw_dim:], rw_g2=d_g2)
    first_g, tok = chips_start("scatter_first_start", [by_chip(n, first_grads[n]) for n in first], _chip_block)

    du = mm("in_proj_dx", dp16, w_ext, tb=True, after=tok)
    grad_x, d_mix_nw = rowvjp("norm_mix_bwd", fn_rms, [(xs, "row"), (mix_norm_w, "const")], [[(du, "row")]],
                              [0, 1], rows=seq, adds=[(0, dh1, "row")])
    received, sent = {}, {}

    def arrived(group, name, state, after):
        lands, sources = chips_wait(name, state, after, with_sources=True)
        received.update(zip(group, lands))
        sent.update(zip(group, sources))

    arrived(mid, "scatter_mid_wait", mid_g, grad_x)
    arrived(("ffn_w2",), "scatter_ffn_w2_wait", ffn_w2_g, grad_x)
    arrived(("ffn_w1",), "scatter_ffn_w1_wait", ffn_w1_g, grad_x)

    out = {}

    def reduce_and_update(tag, group):
        partial = [sum_slots("sum_chips_" + n, received[n], sent[n]) for n in group]
        other = swap_sibling("swap_sibling_" + tag, partial)
        for n, p_mine, p_other in zip(group, partial, other):
            if weights[n].shape[2] % LANES:
                rows, cols = weights[n].shape[1:]
                lin = lambda a: jnp.swapaxes(a, 1, 2).reshape(-1, LANES)
                lin_g = lambda p: p.T[:cols].reshape(-1, LANES)
                res = adamw("adamw_" + n, lin(weights[n]), [lin_g(p_mine), lin_g(p_other)], lin(mom_m[n]),
                            lin(mom_v[n]))
                out[n] = [jnp.swapaxes(r.reshape(1, cols, rows), 1, 2) for r in res]
            else:
                out[n] = adamw("adamw_" + n, weights[n], [p_mine, p_other], mom_m[n], mom_v[n])

    reduce_and_update("rest", mid + late)
    arrived(first, "scatter_first_wait", first_g, [out[n][1] for n in mid + late])
    reduce_and_update("first", first)

    small_names = [n for n in names if n not in sharded]
    small_local = dict(
        mix_norm_w=d_mix_nw, dn_a_log=d_alog[:, :nh], dn_dt_bias=d_dtb[:, :nh], dn_norm_w=d_dn_nw,
        rw_mu=jnp.concatenate([dmu_r, dmu_k, dmu_v, dmu_s], axis=1), rw_w0=d_w0, rw_a0=d_a0, rw_k_k=d_k_k,
        rw_k_a=d_k_a, rw_r_k=d_r_k, rw_ln_w=d_ln_w, rw_ln_b=d_ln_b, xa_norm_w=d_xa_nw, mem_norm_w=d_mem_nw,
        ffn_norm_w=d_ffn_nw, final_norm_w=d_fnw)
    loss_vec = jnp.where(jnp.arange(LANES) == 0, loss_rows[0], 0.0)
    (g8,) = gather_all([_pack([small_local[n] for n in small_names] + [loss_vec])], after=[out["w_in"][1]])

    packed_like = [weights[n] for n in small_names] + [loss_vec]
    zero = jnp.zeros((LANES,), f32)
    res = adamw_packed(_pack([weights[n] for n in small_names] + [zero]), g8,
                       _pack([mom_m[n] for n in small_names] + [zero]),
                       _pack([mom_v[n] for n in small_names] + [zero]))
    unpacked = [_unpack(r, packed_like) for r in res]
    for i, n in enumerate(small_names):
        out[n] = [u[i] for u in unpacked]
    loss = unpacked[0][-1][0]

    return (loss, grad_x.reshape(x.shape), *[out[n][0] for n in names], *[out[n][1] for n in names],
            *[out[n][2] for n in names], *[out[n][3] for n in names])
```

```python
import functools

import jax
import jax.numpy as jnp
from jax import lax
from jax.experimental import pallas as pl
from jax.experimental.pallas import tpu as pltpu

f32 = jnp.float32
bf16 = jnp.bfloat16
HI = lax.Precision.HIGHEST
MESH = pl.DeviceIdType.MESH

LANES = 128
VMEM_LIMIT = 56 << 20
TOK_TILE = 256
TOK_TILE_BLOCKED = 1024
MM_TILE = 1024
MM_TILE_K = 2048
MM_WHOLE_K_BYTES = 55 << 20
GDN_CHUNK = 128
WKV_CHUNK = 64
SCAN_GROUP = 8
P_BULK = 1
P_INV = 1
P_RESID = 3
P_CUMSUM = 3
RMS_EPS = 1e-6
RW_GN_EPS = 64e-5
RW_HEAD = 64

ADAM_LR, ADAM_B1, ADAM_B2, ADAM_EPS, ADAM_WD, ADAM_STEP = 0.001, 0.9, 0.999, 1e-08, 0.01, 10


def _params(n_grid):
    return pltpu.CompilerParams(dimension_semantics=("arbitrary",) * n_grid, vmem_limit_bytes=VMEM_LIMIT)


_DIMS = {"nn": (((1,), (0,)), ((), ())), "nt": (((1,), (1,)), ((), ())), "tn": (((0,), (0,)), ((), ()))}
_DIMS_BATCHED = {"nn": (((2,), (1,)), ((0,), (0,))), "nt": (((2,), (2,)), ((0,), (0,))),
                 "tn": (((1,), (1,)), ((0,), (0,)))}


def _raw_dot(a, b, mode, passes):
    dims = (_DIMS if a.ndim == 2 else _DIMS_BATCHED)[mode]
    if passes == 6:
        return lax.dot_general(a.astype(f32), b.astype(f32), dims, precision=HI, preferred_element_type=f32)
    ah, bh = a.astype(bf16), b.astype(bf16)
    r = lax.dot_general(ah, bh, dims, preferred_element_type=f32)
    if passes == 3:
        al = (a - ah.astype(f32)).astype(bf16)
        bl = (b - bh.astype(f32)).astype(bf16)
        r = r + lax.dot_general(al, bh, dims, preferred_element_type=f32)
        r = r + lax.dot_general(ah, bl, dims, preferred_element_type=f32)
    return r


@functools.partial(jax.custom_vjp, nondiff_argnums=(2, 3))
def pdot(a, b, mode, passes):
    return _raw_dot(a, b, mode, passes)


def _pdot_bwd(mode, passes, res, g):
    a, b = res
    if mode == "nn":
        da, db = _raw_dot(g, b, "nt", passes), _raw_dot(a, g, "tn", passes)
    elif mode == "nt":
        da, db = _raw_dot(g, b, "nn", passes), _raw_dot(g, a, "tn", passes)
    else:
        da, db = _raw_dot(b, g, "nt", passes), _raw_dot(a, g, "nn", passes)
    return da.astype(a.dtype), db.astype(b.dtype)


pdot.defvjp(lambda a, b, mode, passes: (_raw_dot(a, b, mode, passes), (a, b)), _pdot_bwd)


def bdot(a, b):
    return pdot(a, b, "nn", 1)


def bdot_nt(a, b):
    return pdot(a, b, "nt", 1)


def _shift_rows(x, k):
    row = lax.broadcasted_iota(jnp.int32, x.shape, 0)
    return jnp.where(row < k, 0.0, pltpu.roll(x, k, axis=0))


def _unshift_rows(g, k):
    n = g.shape[0]
    row = lax.broadcasted_iota(jnp.int32, g.shape, 0)
    return jnp.where(row >= n - k, 0.0, pltpu.roll(g, n - k, axis=0))


@functools.partial(jax.custom_vjp, nondiff_argnums=(1,))
def tshift(x, k):
    return _shift_rows(x, k)


tshift.defvjp(lambda x, k: (_shift_rows(x, k), None), lambda k, _, g: (_unshift_rows(g, k),))


def rms(x, w):
    x = x.astype(f32)
    return x * lax.rsqrt(jnp.mean(x * x, axis=-1, keepdims=True) + RMS_EPS) * w


def softplus(x):
    return jnp.maximum(x, 0.0) + jnp.log(1.0 + jnp.exp(-jnp.abs(x)))


def seg2sum(x):
    lo = lax.broadcasted_iota(jnp.int32, x.shape, 1) < RW_HEAD
    s_lo = jnp.sum(jnp.where(lo, x, 0.0), axis=-1, keepdims=True)
    s_hi = jnp.sum(jnp.where(lo, 0.0, x), axis=-1, keepdims=True)
    return jnp.where(lo, s_lo, s_hi)


def _tile(n, pref):
    if n <= pref:
        return n
    t = pref
    while t >= LANES:
        if n % t == 0:
            return t
        t -= LANES
    return n


def mm(name, a, b, *, ta=False, tb=False, add=None, out_dtype=f32, by_chip=False, epilogue=None, extra=(),
       out_dtypes=None, after=None):
    (k, m) = a.shape if ta else a.shape[::-1]
    (n, kb) = b.shape if tb else b.shape[::-1]
    assert k == kb, (name, a.shape, b.shape)
    tm, tk = _tile(m, MM_TILE), _tile(k, MM_TILE_K)
    tn = _tile(n // 4, MM_TILE) if by_chip else _tile(n, MM_TILE)
    vmem = VMEM_LIMIT
    narrow = _tile(n, MM_TILE // 2)
    whole_k = 2 * 2 * (tm + narrow) * k + 2 * 4 * tm * narrow * (1 + len(extra) + (add is not None))
    if k > tk and not by_chip and whole_k <= MM_WHOLE_K_BYTES:
        tn, tk, vmem = narrow, k, MM_WHOLE_K_BYTES + (7 << 20)
    nk = k // tk
    dims = (((0,) if ta else (1,), (1,) if tb else (0,)), ((), ()))
    extra = list(extra) + ([] if add is None else [add])
    out_dtypes = [out_dtype] if out_dtypes is None else list(out_dtypes)
    n_extra, n_out = len(extra), len(out_dtypes)
    n_after = 0 if after is None else 1

    def body(*refs):
        a_ref, b_ref = refs[:2]
        x_refs = refs[2:2 + n_extra]
        o_refs = refs[2 + n_extra + n_after:2 + n_extra + n_after + n_out]
        part = lax.dot_general(a_ref[...].astype(bf16), b_ref[...].astype(bf16), dims, preferred_element_type=f32)

        def finish(r):
            xs = [x[...] for x in x_refs]
            if add is not None:
                r = r + xs.pop().astype(f32)
            outs = (r,) if epilogue is None else epilogue(r, *xs)
            for o_ref, o in zip(o_refs, outs):
                o_ref[...] = o.astype(o_ref.dtype)

        if nk == 1:
            finish(part)
            return
        acc = refs[-1]
        kk = pl.program_id(2)

        @pl.when(kk == 0)
        def _():
            acc[...] = part

        @pl.when(kk > 0)
        def _():
            acc[...] += part

        @pl.when(kk == nk - 1)
        def _():
            finish(acc[...])

    a_spec = pl.BlockSpec((tk, tm), lambda i, j, q: (q, i)) if ta else pl.BlockSpec((tm, tk), lambda i, j, q: (i, q))
    b_spec = pl.BlockSpec((tn, tk), lambda i, j, q: (j, q)) if tb else pl.BlockSpec((tk, tn), lambda i, j, q: (q, j))
    x_spec = pl.BlockSpec((tm, tn), lambda i, j, q: (i, j))
    if by_chip:
        per_chip = n // 4 // tn
        o_spec = pl.BlockSpec((None, tm, tn), lambda i, j, q: (j // per_chip, i, j % per_chip))
        o_shape = (4, m, n // 4)
    else:
        o_spec, o_shape = x_spec, (m, n)
    afters = [] if after is None else [jnp.reshape(after, (1, 1))]
    res = pl.pallas_call(
        body, name=name, grid=(m // tm, n // tn, nk),
        in_specs=[a_spec, b_spec] + [x_spec] * n_extra + [pl.BlockSpec(memory_space=pl.ANY)] * n_after,
        out_specs=[o_spec] * n_out, out_shape=[jax.ShapeDtypeStruct(o_shape, dt) for dt in out_dtypes],
        scratch_shapes=[pltpu.VMEM((tm, tn), f32)] if nk > 1 else [],
        compiler_params=pltpu.CompilerParams(dimension_semantics=("arbitrary",) * 3, vmem_limit_bytes=vmem))(
            a, b, *extra, *afters)
    return res[0] if n_out == 1 else res


REGROUP_PIECES = 4


REGROUP_SUB = 5
REGROUP_FIELDS = 5


def _regroup_table(n_out, sources_of):
    import numpy as np
    tbl = np.zeros((REGROUP_FIELDS * REGROUP_PIECES, n_out), np.int32)
    for j in range(n_out):
        groups = sorted(sources_of(j).items())
        assert len(groups) <= REGROUP_PIECES, (j, len(groups))
        for p in range(REGROUP_PIECES):
            if p < len(groups):
                key, lanes = groups[p]
                shifts = {q - s for s, q in lanes}
                qs = sorted(q for _, q in lanes)
                assert len(shifts) == 1 and qs == list(range(qs[0], qs[-1] + 1)), (j, key)
                row = (key[0], key[1], shifts.pop(), qs[0], qs[-1] + 1)
            else:
                row = (0, 0, 0, 0, 0)
            tbl[REGROUP_FIELDS * p:REGROUP_FIELDS * (p + 1), j] = row
    return jnp.asarray(tbl)


def lane_regroup(name, src, table, src_spec, out_spec, out_shape, grid, src_width, sub):
    rows = src.shape[-2]
    n_src = REGROUP_PIECES * sub

    def body(tbl, *refs):
        o_ref = refs[n_src]
        step = pl.program_id(1)

        def moved(b, p):
            j = step * sub + b
            blk, shift, lo, hi = (tbl[REGROUP_FIELDS * p + f, j] for f in range(1, REGROUP_FIELDS))
            x = refs[b * REGROUP_PIECES + p][...]
            if src_width % LANES:
                lane = lax.broadcasted_iota(jnp.int32, (rows, LANES), 1)
                x = jnp.where(lane < src_width - blk * LANES, x, jnp.zeros((), src.dtype))
            pi = lax.broadcasted_iota(jnp.int32, (LANES, LANES), 0)
            qi = lax.broadcasted_iota(jnp.int32, (LANES, LANES), 1)
            sel = jnp.logical_and(qi - pi == shift, jnp.logical_and(qi >= lo, qi < hi))
            return jnp.dot(x, sel.astype(src.dtype), preferred_element_type=f32).astype(o_ref.dtype)

        for b in range(sub):
            lanes = slice(b * LANES, (b + 1) * LANES)
            o_ref[:, lanes] = moved(b, 0)
            for p in range(1, REGROUP_PIECES):
                j = step * sub + b

                @pl.when(tbl[REGROUP_FIELDS * p + 4, j] > tbl[REGROUP_FIELDS * p + 3, j])
                def _(b=b, p=p, lanes=lanes):
                    o_ref[:, lanes] += moved(b, p)

    return pl.pallas_call(
        body, name=name, out_shape=jax.ShapeDtypeStruct(out_shape, src.dtype),
        grid_spec=pltpu.PrefetchScalarGridSpec(
            num_scalar_prefetch=1, grid=grid,
            in_specs=[src_spec(p, b) for b in range(sub) for p in range(REGROUP_PIECES)], out_specs=out_spec),
        compiler_params=_params(2))(table, *[src] * n_src)


def w_in_layout(d_cols, shard_w, c_split, gap, n_blocks):
    def ext_of(c):
        return c if c < c_split else c + gap

    def fwd_sources(j):
        groups = {}
        for q in range(LANES):
            e = j * LANES + q
            c = e if e < c_split else e - gap
            if (c_split <= e < c_split + gap) or c >= d_cols:
                continue
            s, l = divmod(c, shard_w)
            groups.setdefault((s, l // LANES), []).append((l % LANES, q))
        return groups

    per_shard = -(-shard_w // LANES)

    def bwd_sources(j):
        s, b = divmod(j, per_shard)
        groups = {}
        for q in range(LANES):
            l = b * LANES + q
            if l >= shard_w:
                continue
            e = ext_of(s * shard_w + l)
            groups.setdefault((0, e // LANES), []).append((e % LANES, q))
        return groups

    return _regroup_table(n_blocks, fwd_sources), _regroup_table(4 * per_shard, bwd_sources), per_shard


def _in_spec(a, kind, tile):
    if kind == "row":
        return pl.BlockSpec((tile, a.shape[1]), lambda j, i: (i, 0))
    if kind == "const":
        return pl.BlockSpec(a.shape, lambda j, i: (0, 0))
    tag, cw, off = kind
    if tag == "rowc":
        return pl.BlockSpec((tile, cw), lambda j, i: (i, j + off))
    assert tag == "constc", kind
    return pl.BlockSpec((a.shape[0], cw), lambda j, i: (0, j + off))


def rowcall(name, fn, ins, outs, *, rows, tile=None, ncol=1):
    if tile is None:
        tile = min(TOK_TILE_BLOCKED if ncol > 1 else TOK_TILE, rows)
    n_in = len(ins)
    kinds = [o[3] for o in outs]

    def body(*refs):
        j, i = pl.program_id(0), pl.program_id(1)
        res = fn(*[r[...] for r in refs[:n_in]])
        for ref, val, kind in zip(refs[n_in:], res, kinds):
            if kind in ("row", "rowc"):
                ref[...] = val.astype(ref.dtype)
            else:
                first = (i == 0) if kind == "accc" else jnp.logical_and(i == 0, j == 0)

                @pl.when(first)
                def _(ref=ref, val=val):
                    ref[...] = val.astype(ref.dtype)

                @pl.when(jnp.logical_not(first))
                def _(ref=ref, val=val):
                    ref[...] += val.astype(ref.dtype)

    out_shape, out_specs = [], []
    for nr, nc, dtype, kind in outs:
        out_shape.append(jax.ShapeDtypeStruct((nr, nc), dtype))
        if kind == "row":
            out_specs.append(pl.BlockSpec((tile, nc), lambda j, i: (i, 0)))
        elif kind == "rowc":
            out_specs.append(pl.BlockSpec((tile, nc // ncol), lambda j, i: (i, j)))
        elif kind == "acc":
            out_specs.append(pl.BlockSpec((nr, nc), lambda j, i: (0, 0)))
        else:
            out_specs.append(pl.BlockSpec((nr, nc // ncol), lambda j, i: (0, j)))
    return pl.pallas_call(
        body, name=name, grid=(ncol, rows // tile), in_specs=[_in_spec(a, k, tile) for a, k in ins],
        out_specs=out_specs, out_shape=out_shape, compiler_params=_params(2))(*[a for a, _ in ins])


def rowvjp(name, fn, ins, cts, grads, *, rows, tile=None, ncol=1, adds=(), dup16=()):
    n_in = len(ins)
    ct_sizes = [len(c) for c in cts]
    flat_cts = [m for c in cts for m in c]
    n_ct = len(flat_cts)

    def wrapped(*vals):
        xs = list(vals[:n_in])
        gs = vals[n_in:n_in + n_ct]
        extra = vals[n_in + n_ct:]

        def f(*dvars):
            full = list(xs)
            for k, v in zip(grads, dvars):
                full[k] = v
            return fn(*full)

        outs, pull = jax.vjp(f, *[xs[k] for k in grads])
        cot, p = [], 0
        for o, size in zip(outs, ct_sizes):
            g = gs[p].astype(f32)
            for q in range(1, size):
                g = g + gs[p + q].astype(f32)
            cot.append(g.astype(o.dtype))
            p += size
        gv = list(pull(tuple(cot)))
        for (pos, _, _), e in zip(adds, extra):
            gv[pos] = gv[pos] + e.astype(gv[pos].dtype)
        return tuple(gv) + tuple(gv[pos] for pos in dup16)

    outs = []
    for k in grads:
        a, kind = ins[k]
        if kind == "row":
            outs.append((rows, a.shape[1] * ncol, f32, "rowc") if ncol > 1 else (rows, a.shape[1], f32, "row"))
        elif kind == "const":
            outs.append((a.shape[0], a.shape[1], f32, "acc"))
        elif kind[0] == "rowc":
            outs.append((rows, kind[1] * ncol, f32, "rowc"))
        else:
            outs.append((a.shape[0], kind[1] * ncol, f32, "accc"))
    for pos in dup16:
        nr, nc, _, kind = outs[pos]
        outs.append((nr, nc, bf16, kind))
    all_ins = list(ins) + flat_cts + [(a, kind) for _, a, kind in adds]
    return rowcall(name, wrapped, all_ins, outs, rows=rows, tile=tile, ncol=ncol)


def fn_rms(x, w):
    return (rms(x, w),)


def make_fn_gconv(n_norm_blocks):
    def fn(p, cw):
        c = cw[3:4] * p
        for jj in range(3):
            c = c + cw[jj:jj + 1] * tshift(p, 3 - jj)
        s = c * jax.nn.sigmoid(c)
        nrm = s * lax.rsqrt(jnp.sum(s * s, axis=-1, keepdims=True) + 1e-6)
        return (jnp.where(pl.program_id(0) < n_norm_blocks, nrm, s),)
    return fn


def fn_ggate(ps0, alog, dtb, e_g, e_b):
    g = -jnp.exp(alog) * softplus(ps0 + dtb)
    beta = jax.nn.sigmoid(ps0)
    return pdot(g, e_g, "nn", 6), pdot(beta, e_b, "nn", 6)


def fn_gpost(o, z, nw):
    return (rms(o, nw) * (z * jax.nn.sigmoid(z)),)


def fn_lerp(p, mu):
    return (p + (tshift(p, 1) - p) * mu,)


def fn_rprep(pk, psl, w0, a0, k_k, k_a, w2p, a2p, g2):
    g1, g2in = psl[:, :LANES], psl[:, LANES:]
    log_w = -softplus(-(w0 + bdot(jnp.tanh(g1), w2p))) - 0.5
    lw = -jnp.exp(log_w)
    a = jax.nn.sigmoid(a0 + bdot(g1, a2p))
    gate = bdot(jax.nn.sigmoid(g2in), g2)
    kkr = pk * k_k
    kk = kkr / jnp.maximum(jnp.sqrt(seg2sum(kkr * kkr)), 1e-12)
    kmod = pk * (1.0 + (a - 1.0) * k_a)
    return lw, kmod, kk, a, gate


def fn_rpost(y, r, kmod, v, gate, r_k, ln_w, ln_b):
    inv_n = 1.0 / RW_HEAD
    mean = seg2sum(y) * inv_n
    d = y - mean
    var = seg2sum(d * d) * inv_n
    yn = d * lax.rsqrt(var + RW_GN_EPS) * ln_w + ln_b
    bonus = seg2sum(r * kmod * r_k) * v
    return ((yn + bonus) * gate,)


def fn_xcore(q, k, v):
    s = bdot_nt(q, k) * (LANES ** -0.5)
    p = jax.nn.softmax(s, axis=-1)
    return (bdot(p, v),)


def fn_final(h, tgt, w):
    def loss_fn(h, w):
        return 0.5 * jnp.sum(jnp.mean(jnp.square(rms(h, w) - tgt), axis=-1))

    val, (dh, dw) = jax.value_and_grad(loss_fn, argnums=(0, 1))(h, w)
    return dh, dh.astype(bf16), dw, jnp.full((8, LANES), val, f32)


def fn_sumcols(n):
    def fn(x):
        w = x.shape[1] // n
        s = x[:, :w]
        for q in range(1, n):
            s = s + x[:, q * w:(q + 1) * w]
        return (s,)
    return fn


def _tri(c):
    ii = lax.broadcasted_iota(jnp.int32, (c, c), 0)
    jj = lax.broadcasted_iota(jnp.int32, (c, c), 1)
    return ii, jj


def _neumann_raw(m, steps):
    c = m.shape[-1]
    ii, jj = _tri(c)
    eye = (ii == jj).astype(f32)
    t, p = eye + m, m
    for _ in range(steps):
        p = _raw_dot(p, p, "nn", P_INV)
        t = _raw_dot(t, eye + p, "nn", P_INV)
    resid = eye - t + _raw_dot(m, t, "nn", P_RESID)
    return t + _raw_dot(t, resid, "nn", P_INV)


@functools.partial(jax.custom_vjp, nondiff_argnums=(1,))
def _neumann_inverse(m, steps):
    return _neumann_raw(m, steps)


def _neumann_fwd(m, steps):
    t = _neumann_raw(m, steps)
    return t, t


def _neumann_bwd(steps, t, g):
    return (_raw_dot(_raw_dot(t, g, "tn", P_RESID), t, "nt", P_RESID),)


_neumann_inverse.defvjp(_neumann_fwd, _neumann_bwd)


@jax.custom_vjp
def _known_inverse(m, t):
    return t


_known_inverse.defvjp(lambda m, t: (t, t), lambda t, g: (_neumann_bwd(0, t, g)[0], jnp.zeros_like(t)))


def _inverse(m, steps, kept):
    return _neumann_inverse(m, steps) if kept is None else _known_inverse(m, kept)


def cumsum_rows(x):
    t = x.shape[1]
    ii, jj = _tri(t)
    tri = jnp.broadcast_to((ii >= jj).astype(f32), (x.shape[0], t, t))
    return pdot(tri, x, "nn", P_CUMSUM)


def gdn_chunk(q, k, v, gb, bb, s, kept=None):
    c = q.shape[1]
    ii, jj = _tri(c)
    low = ii >= jj
    gcb = cumsum_rows(gb)
    gl = jnp.sum(gb, axis=1, keepdims=True)
    gc_col = gcb[:, :, :c]
    diff = gc_col - jnp.swapaxes(gc_col, 1, 2)
    decay = jnp.where(low, jnp.exp(jnp.where(low, diff, 0.0)), 0.0)
    qs = q * (q.shape[2] ** -0.5)
    kb = k * bb
    with_k = pdot(jnp.concatenate([kb, qs], axis=1), k, "nt", P_BULK)
    a = jnp.where(ii > jj, with_k[:, :c] * decay, 0.0)
    attn = with_k[:, c:] * decay
    t = _inverse(-a, c.bit_length() - 2, kept)
    eg = jnp.exp(gcb)
    uw = pdot(t, jnp.concatenate([v * bb, kb * eg], axis=2), "nn", P_BULK)
    u, w = uw[:, :, :LANES], uw[:, :, LANES:]
    kd = k * jnp.exp(gl - gcb)
    from_state = pdot(jnp.concatenate([w, qs * eg], axis=1), s, "nn", P_BULK)
    v_new = u - from_state[:, :c]
    o = from_state[:, c:] + pdot(attn, v_new, "nn", P_BULK)
    s_new = s * jnp.exp(gl) + pdot(kd, v_new, "tn", P_BULK)
    return o, s_new, t


def wkv_chunk(r, lw, k, v, kk, a, s, kept=None):
    t = r.shape[1]
    ii, jj = _tri(t)
    lo = lax.broadcasted_iota(jnp.int32, r.shape, 2) < RW_HEAD
    cl = cumsum_rows(lw)
    cl_last = jnp.sum(lw, axis=1, keepdims=True)
    al = -kk * jnp.exp(cl - lw)
    be = (a * kk) * jnp.exp(-cl)
    kt = k * jnp.exp(-cl)
    rt = r * jnp.exp(cl)

    def dot(xa, xb, mode="nn"):
        return pdot(xa, xb, mode, P_BULK)

    def sel(x_lo, x_hi):
        return jnp.where(lo, x_lo, x_hi)

    left = jnp.concatenate([jnp.where(lo, al, 0.0), jnp.where(lo, 0.0, al),
                            jnp.where(lo, rt, 0.0), jnp.where(lo, 0.0, rt)], axis=1)
    with_be, with_kt = dot(left, be, "nt"), dot(left, kt, "nt")

    def blocks(prod):
        below, upto = ii > jj, ii >= jj
        return [jnp.where(msk, prod[:, q * t:(q + 1) * t], 0.0) for q, msk in enumerate((below, below, upto, upto))]

    ab_lo, ab_hi, rb_lo, rb_hi = blocks(with_be)
    ak_lo, ak_hi, rk_lo, rk_hi = blocks(with_kt)
    from_state = dot(jnp.concatenate([al, rt], axis=1), s, "nt")
    x = from_state[:, :t] + sel(dot(ak_lo, v), dot(ak_hi, v))
    steps = t.bit_length() - 2
    inv_lo = _inverse(ab_lo, steps, None if kept is None else kept[:, :t])
    inv_hi = _inverse(ab_hi, steps, None if kept is None else kept[:, t:])
    u = sel(dot(inv_lo, x), dot(inv_hi, x))
    y = from_state[:, t:] + sel(dot(rb_lo, u) + dot(rk_lo, v), dot(rb_hi, u) + dot(rk_hi, v))
    vi = lax.broadcasted_iota(jnp.int32, s.shape, 1) < RW_HEAD
    ki = lax.broadcasted_iota(jnp.int32, s.shape, 2) < RW_HEAD
    s_new = jnp.where(vi == ki, (s + dot(u, be, "tn") + dot(v, kt, "tn")) * jnp.exp(cl_last), 0.0)
    return y, s_new, jnp.concatenate([inv_lo, inv_hi], axis=1)


def _scan_group(ncol, offs):
    g = SCAN_GROUP
    while g > 1 and (ncol % g or any(o % g for o in offs)):
        g //= 2
    return g


def scan_fwd(name, chunk_fn, ins, *, rows, chunk, ncol):
    n = rows // chunk
    n_in = len(ins)
    grp = _scan_group(ncol, [off for _, off in ins])
    tile = jax.ShapeDtypeStruct((grp, chunk, LANES), f32)
    kept = jax.eval_shape(chunk_fn, *[tile] * n_in, jax.ShapeDtypeStruct((grp, LANES, LANES), f32))[2].shape[1:]

    def body(*refs):
        o_ref, st_ref, kept_ref, s_scr = refs[n_in:]

        @pl.when(pl.program_id(1) == 0)
        def _():
            s_scr[...] = jnp.zeros_like(s_scr)

        cols = [slice(b * LANES, (b + 1) * LANES) for b in range(grp)]
        s = s_scr[...]
        st_ref[...] = s
        o, s_new, inv = chunk_fn(*[jnp.stack([r[:, c] for c in cols]) for r in refs[:n_in]], s)
        for b, c in enumerate(cols):
            o_ref[:, c] = o[b]
        kept_ref[...] = inv
        s_scr[...] = s_new

    def spec(off):
        return pl.BlockSpec((chunk, grp * LANES), lambda h, c: (c, h + off // grp))

    def per_chunk(shape):
        return pl.BlockSpec((grp, None) + tuple(shape), lambda h, c: (h, c, 0, 0))

    return pl.pallas_call(
        body, name=name, grid=(ncol // grp, n), in_specs=[spec(off) for _, off in ins],
        out_specs=[spec(0), per_chunk((LANES, LANES)), per_chunk(kept)],
        out_shape=[jax.ShapeDtypeStruct((rows, ncol * LANES), f32),
                   jax.ShapeDtypeStruct((ncol, n, LANES, LANES), f32),
                   jax.ShapeDtypeStruct((ncol, n) + tuple(kept), f32)],
        scratch_shapes=[pltpu.VMEM((grp, LANES, LANES), f32)], compiler_params=_params(2))(*[a for a, _ in ins])


def scan_bwd(name, chunk_fn, ins, states, kept, d_out, d_off, *, rows, chunk, ncol):
    n = rows // chunk
    n_in = len(ins)
    grp = _scan_group(ncol, [off for _, off in ins] + [d_off])

    def body(*refs):
        st_ref, kept_ref, do_ref = refs[n_in:n_in + 3]
        g_refs = refs[n_in + 3:2 * n_in + 3]
        ds_scr = refs[-1]

        @pl.when(pl.program_id(1) == 0)
        def _():
            ds_scr[...] = jnp.zeros_like(ds_scr)

        cols = [slice(b * LANES, (b + 1) * LANES) for b in range(grp)]

        def batch(ref):
            return jnp.stack([ref[:, c] for c in cols])

        inv = kept_ref[...]
        _, pull = jax.vjp(lambda *a: chunk_fn(*a, kept=inv)[:2], *[batch(r) for r in refs[:n_in]], st_ref[...])
        gs = pull((batch(do_ref), ds_scr[...]))
        for ref, g in zip(g_refs, gs[:n_in]):
            for b, c in enumerate(cols):
                ref[:, c] = g[b]
        ds_scr[...] = gs[n_in]

    def spec(off):
        return pl.BlockSpec((chunk, grp * LANES), lambda h, c: (n - 1 - c, h + off // grp))

    def per_chunk(shape):
        return pl.BlockSpec((grp, None) + tuple(shape), lambda h, c: (h, n - 1 - c, 0, 0))

    return pl.pallas_call(
        body, name=name, grid=(ncol // grp, n),
        in_specs=[spec(off) for _, off in ins] + [per_chunk(states.shape[2:]), per_chunk(kept.shape[2:]), spec(d_off)],
        out_specs=[spec(0)] * n_in, out_shape=[jax.ShapeDtypeStruct((rows, ncol * LANES), f32)] * n_in,
        scratch_shapes=[pltpu.VMEM((grp, LANES, LANES), f32)],
        compiler_params=_params(2))(*[a for a, _ in ins], states, kept, d_out)


def flip_exchange(name, arrs, flips, n_slots, slot_of, src_of, with_self, after=()):
    n = len(arrs)
    nf = len(flips)
    n_after = len(after)

    def body(*refs):
        ins, outs = refs[:n], refs[n + n_after:2 * n + n_after]
        send, recv, lsem = refs[2 * n + n_after:]
        me = (lax.axis_index("x"), lax.axis_index("y"), lax.axis_index("c"))
        copies = []
        for k in range(n):
            if with_self:
                cp = pltpu.make_async_copy(src_of(ins[k], me), outs[k].at[slot_of(me)], lsem.at[k])
                cp.start()
                copies.append(cp)
            for j, fl in enumerate(flips):
                peer = tuple(1 - m if f else m for m, f in zip(me, fl))
                cp = pltpu.make_async_remote_copy(
                    src_ref=src_of(ins[k], peer), dst_ref=outs[k].at[slot_of(me)], send_sem=send.at[k, j],
                    recv_sem=recv.at[k, j], device_id=peer, device_id_type=MESH)
                cp.start()
                copies.append(cp)
        for cp in copies:
            cp.wait()

    def out_sds(a):
        blk = src_of(jax.ShapeDtypeStruct(a.shape, a.dtype), None)
        return jax.ShapeDtypeStruct((n_slots,) + tuple(blk), a.dtype)

    any_spec = pl.BlockSpec(memory_space=pl.ANY)
    return pl.pallas_call(
        body, name=name, in_specs=[any_spec] * (n + n_after), out_specs=[any_spec] * n,
        out_shape=[out_sds(a) for a in arrs],
        scratch_shapes=[pltpu.SemaphoreType.DMA((n, nf)), pltpu.SemaphoreType.DMA((n, nf)),
                        pltpu.SemaphoreType.DMA((n,))],
        compiler_params=pltpu.CompilerParams(has_side_effects=True))(*arrs, *after)


_CHIP_FLIPS = ((1, 0, 0), (0, 1, 0), (1, 1, 0))
_ALL_FLIPS = ((0, 0, 1), (0, 1, 0), (0, 1, 1), (1, 0, 0), (1, 0, 1), (1, 1, 0), (1, 1, 1))


def _whole(ref, pos):
    return ref.shape if pos is None else ref


def _chip_block(ref, pos):
    return ref.shape[1:] if pos is None else ref.at[2 * pos[0] + pos[1]]


def _chip_slot(p):
    return 2 * p[0] + p[1]


_HBM = pl.BlockSpec(memory_space=pltpu.HBM)
_SEM = pl.BlockSpec(memory_space=pltpu.SEMAPHORE)
_DATAFLOW = pltpu.SideEffectType.DATAFLOW_SIDE_EFFECTING


def _split_copies(mode, refs, n, send, recv):
    me = (lax.axis_index("x"), lax.axis_index("y"), lax.axis_index("c"))
    sib = (me[0], me[1], 1 - me[2])
    lands = refs[:n] if mode == "handover" else refs[n:2 * n]
    copies = []
    for k, land in enumerate(lands):
        half = land.shape[1] // 2
        mine = pl.ds(pl.multiple_of(me[2] * half, 16), half)
        for j, fl in enumerate(_CHIP_FLIPS):
            peer = tuple(1 - m if f else m for m, f in zip(me, fl))
            if mode == "gather":
                src, dst, to = refs[k], land.at[_chip_slot(me)], peer
            elif mode == "scatter":
                src, dst, to = refs[k].at[_chip_slot(peer)], land.at[_chip_slot(me)], peer
            elif mode == "gather_half":
                src, dst, to = refs[k].at[mine], land.at[_chip_slot(me), mine], peer
            else:
                src = dst = land.at[_chip_slot(peer), mine]
                to = sib
            q = k * len(_CHIP_FLIPS) + j
            copies.append(pltpu.make_async_remote_copy(src_ref=src, dst_ref=dst, send_sem=send.at[q],
                                                       recv_sem=recv.at[q], device_id=to, device_id_type=MESH))
    return copies


def split_start(name, mode, ops, n):
    ops = [pltpu.with_memory_space_constraint(a, pltpu.HBM) for a in ops]
    m = len(ops)

    def body(*refs):
        for cp in _split_copies(mode, refs[:m], n, refs[m], refs[m + 1]):
            cp.start()
        refs[-1][...] = jnp.zeros_like(refs[-1])

    sems = pltpu.SemaphoreType.DMA((n * len(_CHIP_FLIPS),))
    outs = pl.pallas_call(
        body, name=name, in_specs=[_HBM] * m,
        out_shape=(sems, sems, *[pltpu.HBM(a.shape, a.dtype) for a in ops], jax.ShapeDtypeStruct((8, LANES), f32)),
        out_specs=(_SEM, _SEM, *[_HBM] * m, pl.BlockSpec(memory_space=pltpu.VMEM)),
        input_output_aliases={i: 2 + i for i in range(m)},
        compiler_params=pltpu.CompilerParams(has_side_effects=_DATAFLOW))(*ops)
    return (outs[0], outs[1], list(outs[2:2 + m]), mode, n), outs[-1][0, 0]


def split_wait(name, state, after, with_sources=False):
    send, recv, ops, mode, n = state
    m = len(ops)
    afters = list(after) if isinstance(after, (list, tuple)) else [after]

    def body(*refs):
        for cp in _split_copies(mode, refs[:m], n, refs[m], refs[m + 1]):
            cp.wait_send()
            cp.wait_recv()

    outs = pl.pallas_call(
        body, name=name, in_specs=[_HBM] * m + [_SEM, _SEM] + [pl.BlockSpec(memory_space=pl.ANY)] * len(afters),
        out_shape=tuple(pltpu.HBM(a.shape, a.dtype) for a in ops), out_specs=tuple([_HBM] * m),
        input_output_aliases={i: i for i in range(m)},
        compiler_params=pltpu.CompilerParams(has_side_effects=_DATAFLOW))(*ops, send, recv, *afters)
    return (list(outs[m - n:]), list(outs[:m - n])) if with_sources else list(outs[m - n:])


def chips_start(name, arrs, src_of, halves=False):
    me = _chip_slot((lax.axis_index("x"), lax.axis_index("y")))
    lands = []
    for a in arrs:
        blk = tuple(src_of(jax.ShapeDtypeStruct(a.shape, a.dtype), None))
        land = lax.empty((4,) + blk, a.dtype)
        lands.append(lax.dynamic_update_index_in_dim(land, a, me, 0) if src_of is _whole else land)
    mode = "scatter" if src_of is _chip_block else ("gather_half" if halves else "gather")
    return split_start(name, mode, list(arrs) + lands, len(arrs))


chips_wait = split_wait


def handover_start(name, lands):
    return split_start(name, "handover", lands, len(lands))


def gather_chips_halves(name, arrs):
    n = len(arrs)
    nf = len(_CHIP_FLIPS)
    split = [a.shape[0] % 32 == 0 for a in arrs]

    def body(*refs):
        ins, outs = refs[:n], refs[n:2 * n]
        send1, recv1, send2, recv2, lsem = refs[2 * n:]
        me = (lax.axis_index("x"), lax.axis_index("y"), lax.axis_index("c"))
        sib = (me[0], me[1], 1 - me[2])
        peers = [tuple(1 - m if f else m for m, f in zip(me, fl)) for fl in _CHIP_FLIPS]
        local, first, second = [], [], []
        for k in range(n):
            cp = pltpu.make_async_copy(ins[k], outs[k].at[_chip_slot(me)], lsem.at[k])
            cp.start()
            local.append(cp)
            half = ins[k].shape[0] // 2
            rows = pl.ds(pl.multiple_of(me[2] * half, 16), half) if split[k] else pl.ds(0, ins[k].shape[0])
            for j, peer in enumerate(peers):
                cp = pltpu.make_async_remote_copy(
                    src_ref=ins[k].at[rows], dst_ref=outs[k].at[_chip_slot(me), rows], send_sem=send1.at[k, j],
                    recv_sem=recv1.at[k, j], device_id=peer, device_id_type=MESH)
                cp.start()
                first.append((k, j, rows, cp))
        for k, j, rows, cp in first:
            cp.wait_recv()
            if split[k]:
                got = outs[k].at[_chip_slot(peers[j]), rows]
                fwd = pltpu.make_async_remote_copy(src_ref=got, dst_ref=got, send_sem=send2.at[k, j],
                                                   recv_sem=recv2.at[k, j], device_id=sib, device_id_type=MESH)
                fwd.start()
                second.append(fwd)
        for _, _, _, cp in first:
            cp.wait_send()
        for cp in second:
            cp.wait()
        for cp in local:
            cp.wait()

    any_spec = pl.BlockSpec(memory_space=pl.ANY)
    sems = pltpu.SemaphoreType.DMA((n, nf))
    return pl.pallas_call(
        body, name=name, in_specs=[any_spec] * n, out_specs=[any_spec] * n,
        out_shape=[jax.ShapeDtypeStruct((4,) + a.shape, a.dtype) for a in arrs],
        scratch_shapes=[sems, sems, sems, sems, pltpu.SemaphoreType.DMA((n,))],
        compiler_params=pltpu.CompilerParams(has_side_effects=True))(*arrs)


def swap_sibling(name, arrs):
    outs = flip_exchange(name, arrs, ((0, 0, 1),), 1, lambda p: 0, _whole, False)
    return [o[0] for o in outs]


def gather_all(arrs, after=()):
    return flip_exchange("gather_all", arrs, _ALL_FLIPS, 8, lambda p: 4 * p[0] + 2 * p[1] + p[2], _whole, True,
                         after=after)


def _row_tile(nr, nc, n_arrays):
    budget = (20 << 20) // (n_arrays * 2 * 4 * max(nc, LANES))
    t = min(nr, budget) // 16 * 16
    while t > 0 and nr % t:
        t -= 16
    return t if t > 0 else nr


def sum_slots(name, x, own):
    ns, nr, nc = x.shape
    tile = _row_tile(nr, nc, ns + 1)
    me = jnp.reshape(_chip_slot((lax.axis_index("x"), lax.axis_index("y"))), (1,)).astype(jnp.int32)

    def body(me_ref, own_ref, *refs):
        o_ref = refs[-1]
        s = own_ref[...].astype(f32)
        for r in refs[:-1]:
            s = s + r[...].astype(f32)
        o_ref[...] = s.astype(o_ref.dtype)

    def slot(q):
        return pl.BlockSpec((None, tile, nc), lambda i, t: ((t[0] + q) % ns, i, 0))

    return pl.pallas_call(
        body, name=name, out_shape=jax.ShapeDtypeStruct((nr, nc), x.dtype),
        grid_spec=pltpu.PrefetchScalarGridSpec(
            num_scalar_prefetch=1, grid=(nr // tile,), in_specs=[slot(q) for q in range(ns)],
            out_specs=pl.BlockSpec((tile, nc), lambda i, t: (i, 0))),
        compiler_params=_params(1))(me, own, *[x] * (ns - 1))


def adamw(name, w, g_parts, m, v):
    nr, nc = w.shape[-2:]
    n_g = len(g_parts)
    tile = _row_tile(nr, nc, 7 + n_g)

    def body(*refs):
        w_ref, m_ref, v_ref = refs[:3]
        g = refs[3][...].astype(f32)
        for r in refs[4:3 + n_g]:
            g = g + r[...].astype(f32)
        g_ref, d_ref, nm_ref, nv_ref = refs[3 + n_g:]
        nm = ADAM_B1 * m_ref[...] + (1.0 - ADAM_B1) * g
        nv = ADAM_B2 * v_ref[...] + (1.0 - ADAM_B2) * jnp.square(g)
        m_hat = nm / (1.0 - ADAM_B1 ** ADAM_STEP)
        v_hat = nv / (1.0 - ADAM_B2 ** ADAM_STEP)
        g_ref[...] = g
        d_ref[...] = -ADAM_LR * (m_hat / (jnp.sqrt(v_hat) + ADAM_EPS) + ADAM_WD * w_ref[...])
        nm_ref[...] = nm
        nv_ref[...] = nv

    spec = pl.BlockSpec((tile, nc), lambda i: (i, 0))
    spec3 = pl.BlockSpec((None, tile, nc), lambda i: (0, i, 0)) if w.ndim == 3 else spec
    return pl.pallas_call(
        body, name=name, grid=(nr // tile,), in_specs=[spec3] * 3 + [spec] * n_g, out_specs=[spec3] * 4,
        out_shape=[jax.ShapeDtypeStruct(w.shape, f32)] * 4, compiler_params=_params(1))(w, m, v, *g_parts)


def adamw_packed(w, g8, m, v):
    nr, nc = w.shape

    def body(w_ref, g_ref, m_ref, v_ref, go_ref, d_ref, nm_ref, nv_ref):
        g = g_ref[0]
        for q in range(1, 8):
            g = g + g_ref[q]
        nm = ADAM_B1 * m_ref[...] + (1.0 - ADAM_B1) * g
        nv = ADAM_B2 * v_ref[...] + (1.0 - ADAM_B2) * jnp.square(g)
        m_hat = nm / (1.0 - ADAM_B1 ** ADAM_STEP)
        v_hat = nv / (1.0 - ADAM_B2 ** ADAM_STEP)
        go_ref[...] = g
        d_ref[...] = -ADAM_LR * (m_hat / (jnp.sqrt(v_hat) + ADAM_EPS) + ADAM_WD * w_ref[...])
        nm_ref[...] = nm
        nv_ref[...] = nv

    return pl.pallas_call(body, name="adamw_packed", out_shape=[jax.ShapeDtypeStruct((nr, nc), f32)] * 4,
                          compiler_params=pltpu.CompilerParams(vmem_limit_bytes=VMEM_LIMIT))(w, g8, m, v)


def _pack(vectors):
    rows = []
    for a in vectors:
        flat = a.reshape(-1).astype(f32)
        pad = (-flat.shape[0]) % LANES
        rows.append(jnp.pad(flat, (0, pad)).reshape(-1, LANES))
    packed = jnp.concatenate(rows, axis=0)
    return jnp.pad(packed, ((0, (-packed.shape[0]) % 8), (0, 0)))


def _unpack(packed, like):
    out, r = [], 0
    for a in like:
        n = a.size
        nr = -(-n // LANES)
        out.append(packed[r:r + nr].reshape(-1)[:n].reshape(a.shape))
        r += nr
    return out


def kernel(x, mem, mix_norm_w, w_in, dn_conv_w, dn_a_log, dn_dt_bias, dn_norm_w, rw_mu, rw_w0, rw_w2, rw_a0, rw_a2, rw_g2, rw_k_k, rw_k_a, rw_r_k, rw_ln_w, rw_ln_b, w_out, xa_norm_w, mem_norm_w, xa_wq, xa_wk, xa_wv, xa_wo, ffn_norm_w, ffn_w1, ffn_w2, final_norm_w, loss_target, m_mix_norm_w, m_w_in, m_dn_conv_w, m_dn_a_log, m_dn_dt_bias, m_dn_norm_w, m_rw_mu, m_rw_w0, m_rw_w2, m_rw_a0, m_rw_a2, m_rw_g2, m_rw_k_k, m_rw_k_a, m_rw_r_k, m_rw_ln_w, m_rw_ln_b, m_w_out, m_xa_norm_w, m_mem_norm_w, m_xa_wq, m_xa_wk, m_xa_wv, m_xa_wo, m_ffn_norm_w, m_ffn_w1, m_ffn_w2, m_final_norm_w, v_mix_norm_w, v_w_in, v_dn_conv_w, v_dn_a_log, v_dn_dt_bias, v_dn_norm_w, v_rw_mu, v_rw_w0, v_rw_w2, v_rw_a0, v_rw_a2, v_rw_g2, v_rw_k_k, v_rw_k_a, v_rw_r_k, v_rw_ln_w, v_rw_ln_b, v_w_out, v_xa_norm_w, v_mem_norm_w, v_xa_wq, v_xa_wk, v_xa_wv, v_xa_wo, v_ffn_norm_w, v_ffn_w1, v_ffn_w2, v_final_norm_w):
    weights = dict(mix_norm_w=mix_norm_w, w_in=w_in, dn_conv_w=dn_conv_w, dn_a_log=dn_a_log, dn_dt_bias=dn_dt_bias, dn_norm_w=dn_norm_w, rw_mu=rw_mu, rw_w0=rw_w0, rw_w2=rw_w2, rw_a0=rw_a0, rw_a2=rw_a2, rw_g2=rw_g2, rw_k_k=rw_k_k, rw_k_a=rw_k_a, rw_r_k=rw_r_k, rw_ln_w=rw_ln_w, rw_ln_b=rw_ln_b, w_out=w_out, xa_norm_w=xa_norm_w, mem_norm_w=mem_norm_w, xa_wq=xa_wq, xa_wk=xa_wk, xa_wv=xa_wv, xa_wo=xa_wo, ffn_norm_w=ffn_norm_w, ffn_w1=ffn_w1, ffn_w2=ffn_w2, final_norm_w=final_norm_w)
    mom_m = dict(mix_norm_w=m_mix_norm_w, w_in=m_w_in, dn_conv_w=m_dn_conv_w, dn_a_log=m_dn_a_log, dn_dt_bias=m_dn_dt_bias, dn_norm_w=m_dn_norm_w, rw_mu=m_rw_mu, rw_w0=m_rw_w0, rw_w2=m_rw_w2, rw_a0=m_rw_a0, rw_a2=m_rw_a2, rw_g2=m_rw_g2, rw_k_k=m_rw_k_k, rw_k_a=m_rw_k_a, rw_r_k=m_rw_r_k, rw_ln_w=m_rw_ln_w, rw_ln_b=m_rw_ln_b, w_out=m_w_out, xa_norm_w=m_xa_norm_w, mem_norm_w=m_mem_norm_w, xa_wq=m_xa_wq, xa_wk=m_xa_wk, xa_wv=m_xa_wv, xa_wo=m_xa_wo, ffn_norm_w=m_ffn_norm_w, ffn_w1=m_ffn_w1, ffn_w2=m_ffn_w2, final_norm_w=m_final_norm_w)
    mom_v = dict(mix_norm_w=v_mix_norm_w, w_in=v_w_in, dn_conv_w=v_dn_conv_w, dn_a_log=v_dn_a_log, dn_dt_bias=v_dn_dt_bias, dn_norm_w=v_dn_norm_w, rw_mu=v_rw_mu, rw_w0=v_rw_w0, rw_w2=v_rw_w2, rw_a0=v_rw_a0, rw_a2=v_rw_a2, rw_g2=v_rw_g2, rw_k_k=v_rw_k_k, rw_k_a=v_rw_k_a, rw_r_k=v_rw_r_k, rw_ln_w=v_rw_ln_w, rw_ln_b=v_rw_ln_b, w_out=v_w_out, xa_norm_w=v_xa_norm_w, mem_norm_w=v_mem_norm_w, xa_wq=v_xa_wq, xa_wk=v_xa_wk, xa_wv=v_xa_wv, xa_wo=v_xa_wo, ffn_norm_w=v_ffn_norm_w, ffn_w1=v_ffn_w1, ffn_w2=v_ffn_w2, final_norm_w=v_final_norm_w)
    names = list(weights)

    seq, d = x.shape[1], x.shape[2]
    dnw = d // 2
    rww = d - dnw
    nh, nb = dnw // LANES, rww // LANES
    n_mem = mem.shape[1]
    lw_dim, la_dim, lg_dim = rw_w2.shape[1], rw_a2.shape[1], rw_g2.shape[1]
    assert lw_dim + la_dim == LANES and lg_dim == LANES and dnw % LANES == 0 and rww % LANES == 0
    xs, mems, tgt = x[0], mem[0], loss_target[0]

    col_sharded = ("w_in", "xa_wo", "ffn_w1", "dn_conv_w", "rw_w2", "rw_a2", "rw_g2")
    row_sharded = ("w_out", "xa_wq", "xa_wk", "xa_wv", "ffn_w2")
    f32_payload = ("dn_conv_w", "rw_w2", "rw_a2", "rw_g2")
    sharded = col_sharded + row_sharded
    payload = {n: weights[n][0].astype(f32 if n in f32_payload else bf16) for n in sharded}
    shard_w = w_in.shape[2]
    pad_w = -(-shard_w // LANES) * LANES
    first = ("w_in", "dn_conv_w", "rw_w2", "rw_a2", "rw_g2")
    mid = ("w_out", "xa_wq", "xa_wk", "xa_wv", "xa_wo")
    late = ("ffn_w1", "ffn_w2")
    gathered = dict(zip(first, gather_chips_halves("gather_first", [payload[n] for n in first])))
    ordered = lax.optimization_barrier(([gathered[n] for n in first], [payload[n] for n in mid + late]))
    gathered = dict(zip(first, ordered[0]))
    payload.update(zip(mid + late, ordered[1]))
    mid_state, tok_mid = chips_start("gather_mid_start", [payload[n] for n in mid], _whole, halves=True)
    late_state, tok_late = chips_start("gather_late_start", [payload[n] for n in late], _whole, halves=True)
    mix_norm_w_in = mix_norm_w + (tok_mid + tok_late)

    def full(n):
        g = gathered[n]
        if n in col_sharded:
            return g.transpose(1, 0, 2).reshape(g.shape[1], 4 * g.shape[2])
        return g.reshape(4 * g.shape[1], g.shape[2])

    c_rw0 = 4 * dnw + 2 * nh
    c_rw = 3 * rww
    eb_ab, eb_r = 4 * nh, 4 * nh + 1
    eb_l1 = eb_r + 3 * nb
    n_ext = -(-(eb_l1 + 2) // 4) * 4
    tbl_fwd, tbl_bwd, per_shard = w_in_layout(4 * shard_w, shard_w, c_rw0, LANES - 2 * nh, n_ext)
    sub_f = max(s for s in range(1, REGROUP_SUB + 1) if n_ext % s == 0)
    sub_b = max(s for s in range(1, REGROUP_SUB + 1) if per_shard % s == 0)
    w_ext = lane_regroup(
        "w_in_regroup", gathered["w_in"], tbl_fwd,
        lambda p, b: pl.BlockSpec(
            (None, d, LANES), lambda i, j, t: (t[REGROUP_FIELDS * p, j * sub_f + b], 0,
                                               t[REGROUP_FIELDS * p + 1, j * sub_f + b])),
        pl.BlockSpec((d, sub_f * LANES), lambda i, j, t: (0, j)), (d, n_ext * LANES), (1, n_ext // sub_f), shard_w,
        sub_f)
    conv_w = full("dn_conv_w")
    w2p = jnp.concatenate([full("rw_w2"), jnp.zeros((la_dim, rww), f32)], axis=0)
    a2p = jnp.concatenate([jnp.zeros((lw_dim, rww), f32), full("rw_a2")], axis=0)
    g2 = full("rw_g2")
    xaw = xa_wq.shape[2]
    nxh = xaw // LANES
    ffn = 4 * ffn_w1.shape[2]

    def lane_row(vec):
        return jnp.pad(vec.reshape(1, -1), ((0, 0), (0, LANES - vec.size)))

    alog_row, dtb_row = lane_row(dn_a_log), lane_row(dn_dt_bias)
    head_of_col = jnp.arange(dnw)[None, :] // LANES
    e_g = (jnp.arange(LANES)[:, None] == head_of_col).astype(f32)
    e_b = (jnp.arange(LANES)[:, None] == head_of_col + nh).astype(f32)
    mu_main, mu_small = rw_mu[:, :c_rw], rw_mu[:, c_rw:]
    r_k_row = rw_r_k.reshape(1, rww)
    fnw = final_norm_w.reshape(1, d)
    qb, kb_, vb, zb = 0, nh, 2 * nh, 3 * nh
    rb0 = eb_r
    cc = lambda w_, o_: ("constc", w_, o_)
    rc = lambda o_: ("rowc", LANES, o_)

    (u16,) = rowcall("norm_mix", fn_rms, [(xs, "row"), (mix_norm_w_in, "const")], [(seq, d, bf16, "row")], rows=seq)
    p_main = p_small = mm("in_proj", u16, w_ext)

    fn_gconv = make_fn_gconv(2 * nh)
    (qkv,) = rowcall("gdn_conv", fn_gconv, [(p_main, rc(0)), (conv_w, cc(LANES, 0))],
                     [(seq, 3 * dnw, f32, "rowc")], rows=seq, tile=seq, ncol=3 * nh)
    gate_ins = [(p_small, rc(eb_ab)), (alog_row, "const"), (dtb_row, "const"), (e_g, "const"), (e_b, "const")]
    g_b, beta_b = rowcall("gdn_gate", fn_ggate, gate_ins, [(seq, dnw, f32, "row")] * 2, rows=seq)
    gdn_ins = [(qkv, qb), (qkv, kb_), (qkv, vb), (g_b, 0), (beta_b, 0)]
    o_raw, gdn_states, gdn_kept = scan_fwd("gdn_scan", gdn_chunk, gdn_ins, rows=seq, chunk=GDN_CHUNK, ncol=nh)
    mid_state, tok = handover_start("gather_mid_pass", chips_wait("gather_mid_wait", mid_state, o_raw))
    gpost_ins = [(o_raw, rc(0)), (p_main, rc(zb)), (dn_norm_w + tok, "const")]
    (o_dn,) = rowcall("gdn_post", fn_gpost, gpost_ins, [(seq, dnw, bf16, "rowc")], rows=seq, ncol=nh)

    (prw,) = rowcall("rw_lerp_main", fn_lerp, [(p_main, rc(rb0)), (mu_main, cc(LANES, 0))],
                     [(seq, c_rw, f32, "rowc")], rows=seq, tile=seq, ncol=3 * nb)
    (psl,) = rowcall("rw_lerp_small", fn_lerp, [(p_small, rc(eb_l1)), (mu_small, cc(LANES, 0))],
                     [(seq, 2 * LANES, f32, "rowc")], rows=seq, tile=seq, ncol=2)
    rprep_ins = [(prw, rc(nb)), (psl, "row"), (rw_w0, cc(LANES, 0)), (rw_a0, cc(LANES, 0)), (rw_k_k, cc(LANES, 0)),
                 (rw_k_a, cc(LANES, 0)), (w2p, cc(LANES, 0)), (a2p, cc(LANES, 0)), (g2, cc(LANES, 0))]
    lw, kmod, kk, a_rw, gate = rowcall("rw_prep", fn_rprep, rprep_ins, [(seq, rww, f32, "rowc")] * 5,
                                        rows=seq, ncol=nb)
    wkv_ins = [(prw, 0), (lw, 0), (kmod, 0), (prw, 2 * nb), (kk, 0), (a_rw, 0)]
    y_rw, wkv_states, wkv_kept = scan_fwd("wkv_scan", wkv_chunk, wkv_ins, rows=seq, chunk=WKV_CHUNK, ncol=nb)
    rpost_ins = [(y_rw, rc(0)), (prw, rc(0)), (kmod, rc(0)), (prw, rc(2 * nb)), (gate, rc(0)),
                 (r_k_row, cc(LANES, 0)), (rw_ln_w, cc(LANES, 0)), (rw_ln_b, cc(LANES, 0))]
    (o_rw,) = rowcall("rw_post", fn_rpost, rpost_ins, [(seq, rww, bf16, "rowc")], rows=seq, ncol=nb)

    o_cat = jnp.concatenate([o_dn, o_rw], axis=1)
    late_state, tok = handover_start("gather_late_pass", chips_wait("gather_late_wait", late_state, o_cat))
    gathered.update(zip(mid, chips_wait("gather_mid_pass_wait", mid_state, o_cat)))
    w_out_f, wq_f, wk_f, wv_f, wo_f = full("w_out"), full("xa_wq"), full("xa_wk"), full("xa_wv"), full("xa_wo")
    h1 = mm("out_proj", o_cat, w_out_f, add=xs)

    (hn16,) = rowcall("norm_xa", fn_rms, [(h1, "row"), (xa_norm_w + tok, "const")], [(seq, d, bf16, "row")],
                      rows=seq)
    (mn16,) = rowcall("norm_mem", fn_rms, [(mems, "row"), (mem_norm_w, "const")], [(n_mem, d, bf16, "row")],
                      rows=n_mem)
    q_xa = mm("xa_q", hn16, wq_f)
    k_xa = mm("xa_k", mn16, wk_f)
    v_xa = mm("xa_v", mn16, wv_f)
    xcore_ins = [(q_xa, rc(0)), (k_xa, cc(LANES, 0)), (v_xa, cc(LANES, 0))]
    (o_xa,) = rowcall("xa_core", fn_xcore, xcore_ins, [(seq, xaw, bf16, "rowc")], rows=seq, ncol=nxh)
    h2 = mm("xa_o", o_xa, wo_f, add=h1)

    (fn16,) = rowcall("norm_ffn", fn_rms, [(h2, "row"), (ffn_norm_w, "const")], [(seq, d, bf16, "row")], rows=seq)
    gathered.update(zip(late, chips_wait("gather_late_pass_wait", late_state, fn16)))
    w1_f, w2_f = full("ffn_w1"), full("ffn_w2")
    a1_16, hid16 = mm("ffn_up", fn16, w1_f, epilogue=lambda r: (r, jnp.square(jnp.maximum(r, 0.0))),
                      out_dtypes=(bf16, bf16))
    h3 = mm("ffn_down", hid16, w2_f, add=h2)

    dh3, dh3_16, d_fnw, loss_rows = rowcall(
        "loss_head", fn_final, [(h3, "row"), (tgt, "row"), (fnw, "const")],
        [(seq, d, f32, "row"), (seq, d, bf16, "row"), (1, d, f32, "acc"), (8, LANES, f32, "acc")], rows=seq)

    da1_16 = mm("ffn_down_dx", dh3_16, w2_f, tb=True, extra=[a1_16], out_dtype=bf16,
                epilogue=lambda r, a1: (r * (2.0 * jnp.maximum(a1.astype(f32), 0.0)),))
    def by_chip(n, g):
        if g.ndim == 3:
            return g
        if n in col_sharded:
            return g.reshape(g.shape[0], 4, g.shape[1] // 4).transpose(1, 0, 2)
        return g.reshape(4, g.shape[0] // 4, g.shape[1])

    g_ffn_w2 = mm("ffn_down_dw", hid16, dh3_16, ta=True, out_dtype=bf16)
    ffn_w2_g, tok = chips_start("scatter_ffn_w2_start", [by_chip("ffn_w2", g_ffn_w2)], _chip_block)
    g_ffn_w1 = mm("ffn_up_dw", fn16, da1_16, ta=True, out_dtype=bf16, by_chip=True, after=tok)
    ffn_w1_g, tok = chips_start("scatter_ffn_w1_start", [g_ffn_w1], _chip_block)
    dfn = mm("ffn_up_dx", da1_16, w1_f, tb=True, after=tok)
    dh2, d_ffn_nw, dh2_16 = rowvjp("norm_ffn_bwd", fn_rms, [(h2, "row"), (ffn_norm_w, "const")],
                                   [[(dfn, "row")]], [0, 1], rows=seq, adds=[(0, dh3, "row")], dup16=[0])

    do_xa = mm("xa_o_dx", dh2_16, wo_f, tb=True)
    g_xa_wo = mm("xa_o_dw", o_xa, dh2_16, ta=True, out_dtype=bf16, by_chip=True)
    dq_xa, dk_xa, dv_xa, dq16 = rowvjp("xa_core_bwd", fn_xcore, xcore_ins, [[(do_xa, rc(0))]], [0, 1, 2],
                                       rows=seq, ncol=nxh, dup16=[0])
    g_xa_wq = mm("xa_q_dw", hn16, dq16, ta=True, out_dtype=bf16)
    dhn = mm("xa_q_dx", dq16, wq_f, tb=True)
    dh1, d_xa_nw, dh1_16 = rowvjp("norm_xa_bwd", fn_rms, [(h1, "row"), (xa_norm_w, "const")], [[(dhn, "row")]],
                                  [0, 1], rows=seq, adds=[(0, dh2, "row")], dup16=[0])
    dk16, dv16 = dk_xa.astype(bf16), dv_xa.astype(bf16)
    g_xa_wk = mm("xa_k_dw", mn16, dk16, ta=True, out_dtype=bf16)
    g_xa_wv = mm("xa_v_dw", mn16, dv16, ta=True, out_dtype=bf16)
    dmn = mm("xa_v_dx", dv16, wv_f, tb=True, add=mm("xa_k_dx", dk16, wk_f, tb=True))
    (d_mem_nw,) = rowvjp("norm_mem_bwd", fn_rms, [(mems, "row"), (mem_norm_w, "const")], [[(dmn, "row")]], [1],
                         rows=n_mem)

    g_w_out = mm("out_proj_dw", o_cat, dh1_16, ta=True, out_dtype=bf16)
    mid_grads = dict(w_out=g_w_out, xa_wq=g_xa_wq, xa_wk=g_xa_wk, xa_wv=g_xa_wv, xa_wo=g_xa_wo)
    mid_g, tok = chips_start("scatter_mid_start", [by_chip(n, mid_grads[n]) for n in mid], _chip_block)
    do_cat = mm("out_proj_dx", dh1_16, w_out_f, tb=True, after=tok)

    dy, dr_a, dkmod_a, dv_a, dgate, d_r_k, d_ln_w, d_ln_b = rowvjp(
        "rw_post_bwd", fn_rpost, rpost_ins, [[(do_cat, rc(nh))]], [0, 1, 2, 3, 4, 5, 6, 7], rows=seq, ncol=nb)
    dr_b, dlw, dkmod_b, dv_b, dkk, da_rw = scan_bwd("wkv_scan_bwd", wkv_chunk, wkv_ins, wkv_states, wkv_kept, dy, 0,
                                                    rows=seq, chunk=WKV_CHUNK, ncol=nb)
    rprep_cts = [[(dlw, rc(0))], [(dkmod_a, rc(0)), (dkmod_b, rc(0))], [(dkk, rc(0))], [(da_rw, rc(0))],
                 [(dgate, rc(0))]]
    dpk, dpsl_parts, d_w0, d_a0, d_k_k, d_k_a, d_w2p, d_a2p, d_g2 = rowvjp(
        "rw_prep_bwd", fn_rprep, rprep_ins, rprep_cts, [0, 1, 2, 3, 4, 5, 6, 7, 8], rows=seq, ncol=nb)
    (dpsl,) = rowcall("rw_prep_sum", fn_sumcols(nb), [(dpsl_parts, "row")], [(seq, 2 * LANES, f32, "row")], rows=seq)

    def lerp_bwd(tag, p, p_off, mu, mu_off, ct_lists, ncol):
        return rowvjp("rw_lerp_bwd_" + tag, fn_lerp, [(p, rc(p_off)), (mu, cc(LANES, mu_off))], [ct_lists], [0, 1],
                      rows=seq, tile=seq, ncol=ncol, dup16=[0])

    _, dmu_r, dpr16 = lerp_bwd("r", p_main, rb0, mu_main, 0, [(dr_a, rc(0)), (dr_b, rc(0))], nb)
    _, dmu_k, dpk16 = lerp_bwd("k", p_main, rb0 + nb, mu_main, nb, [(dpk, rc(0))], nb)
    _, dmu_v, dpv16 = lerp_bwd("v", p_main, rb0 + 2 * nb, mu_main, 2 * nb, [(dv_a, rc(0)), (dv_b, rc(0))], nb)
    _, dmu_s, dps12_16 = lerp_bwd("small", p_small, eb_l1, mu_small, 0, [(dpsl, rc(0))], 2)

    do_raw, dz, d_dn_nw, dz16 = rowvjp("gdn_post_bwd", fn_gpost, gpost_ins, [[(do_cat, rc(0))]], [0, 1, 2],
                                       rows=seq, ncol=nh, dup16=[1])
    dq_g, dk_g, dv_g, dg_b, dbeta_b = scan_bwd("gdn_scan_bwd", gdn_chunk, gdn_ins, gdn_states, gdn_kept, do_raw, 0,
                                               rows=seq, chunk=GDN_CHUNK, ncol=nh)
    dps0, d_alog, d_dtb, dps0_16 = rowvjp("gdn_gate_bwd", fn_ggate, gate_ins, [[(dg_b, "row")], [(dbeta_b, "row")]],
                                          [0, 1, 2], rows=seq, dup16=[0])
    dqkv = jnp.concatenate([dq_g, dk_g, dv_g], axis=1)
    _, d_conv_w, dqkv16 = rowvjp("gdn_conv_bwd", fn_gconv, [(p_main, rc(0)), (conv_w, cc(LANES, 0))],
                                 [[(dqkv, rc(0))]], [0, 1], rows=seq, tile=seq, ncol=3 * nh, dup16=[0])

    dp16 = jnp.concatenate([dqkv16, dz16, dps0_16, dpr16, dpk16, dpv16, dps12_16,
                            jnp.zeros((seq, (n_ext - eb_l1 - 2) * LANES), bf16)], axis=1)
    g_w_ext = mm("in_proj_dw", u16, dp16, ta=True, out_dtype=bf16)
    g_w_in = lane_regroup(
        "w_in_grad_regroup", g_w_ext, tbl_bwd,
        lambda p, b: pl.BlockSpec((d, LANES), lambda i, j, t: (0, t[REGROUP_FIELDS * p + 1, j * sub_b + b])),
        pl.BlockSpec((None, d, sub_b * LANES),
                     lambda i, j, t: (j // (per_shard // sub_b), 0, j % (per_shard // sub_b))),
        (4, d, pad_w), (1, 4 * per_shard // sub_b), n_ext * LANES, sub_b)
    first_grads = dict(w_in=g_w_in, dn_conv_w=d_conv_w, rw_w2=d_w2p[:lw_dim], rw_a2=d_a2p[lw_dim:], rw_g2=d_g2)
    first_g, tok = chips_start("scatter_first_start", [by_chip(n, first_grads[n]) for n in first], _chip_block)

    du = mm("in_proj_dx", dp16, w_ext, tb=True, after=tok)
    grad_x, d_mix_nw = rowvjp("norm_mix_bwd", fn_rms, [(xs, "row"), (mix_norm_w, "const")], [[(du, "row")]],
                              [0, 1], rows=seq, adds=[(0, dh1, "row")])
    received, sent = {}, {}

    def arrived(group, name, state, after):
        lands, sources = chips_wait(name, state, after, with_sources=True)
        received.update(zip(group, lands))
        sent.update(zip(group, sources))

    arrived(mid, "scatter_mid_wait", mid_g, grad_x)
    arrived(("ffn_w2",), "scatter_ffn_w2_wait", ffn_w2_g, grad_x)
    arrived(("ffn_w1",), "scatter_ffn_w1_wait", ffn_w1_g, grad_x)

    out = {}

    def reduce_and_update(tag, group):
        partial = [sum_slots("sum_chips_" + n, received[n], sent[n]) for n in group]
        other = swap_sibling("swap_sibling_" + tag, partial)
        for n, p_mine, p_other in zip(group, partial, other):
            if weights[n].shape[2] % LANES:
                rows, cols = weights[n].shape[1:]
                lin = lambda a: jnp.swapaxes(a, 1, 2).reshape(-1, LANES)
                lin_g = lambda p: p.T[:cols].reshape(-1, LANES)
                res = adamw("adamw_" + n, lin(weights[n]), [lin_g(p_mine), lin_g(p_other)], lin(mom_m[n]),
                            lin(mom_v[n]))
                out[n] = [jnp.swapaxes(r.reshape(1, cols, rows), 1, 2) for r in res]
            else:
                out[n] = adamw("adamw_" + n, weights[n], [p_mine, p_other], mom_m[n], mom_v[n])

    reduce_and_update("rest", mid + late)
    arrived(first, "scatter_first_wait", first_g, [out[n][1] for n in mid + late])
    reduce_and_update("first", first)

    small_names = [n for n in names if n not in sharded]
    small_local = dict(
        mix_norm_w=d_mix_nw, dn_a_log=d_alog[:, :nh], dn_dt_bias=d_dtb[:, :nh], dn_norm_w=d_dn_nw,
        rw_mu=jnp.concatenate([dmu_r, dmu_k, dmu_v, dmu_s], axis=1), rw_w0=d_w0, rw_a0=d_a0, rw_k_k=d_k_k,
        rw_k_a=d_k_a, rw_r_k=d_r_k, rw_ln_w=d_ln_w, rw_ln_b=d_ln_b, xa_norm_w=d_xa_nw, mem_norm_w=d_mem_nw,
        ffn_norm_w=d_ffn_nw, final_norm_w=d_fnw)
    loss_vec = jnp.where(jnp.arange(LANES) == 0, loss_rows[0], 0.0)
    (g8,) = gather_all([_pack([small_local[n] for n in small_names] + [loss_vec])], after=[out["w_in"][1]])

    packed_like = [weights[n] for n in small_names] + [loss_vec]
    zero = jnp.zeros((LANES,), f32)
    res = adamw_packed(_pack([weights[n] for n in small_names] + [zero]), g8,
                       _pack([mom_m[n] for n in small_names] + [zero]),
                       _pack([mom_v[n] for n in small_names] + [zero]))
    unpacked = [_unpack(r, packed_like) for r in res]
    for i, n in enumerate(small_names):
        out[n] = [u[i] for u in unpacked]
    loss = unpacked[0][-1][0]

    return (loss, grad_x.reshape(x.shape), *[out[n][0] for n in names], *[out[n][1] for n in names],
            *[out[n][2] for n in names], *[out[n][3] for n in names])
```

```python
import functools

import jax
import jax.numpy as jnp
from jax import lax
from jax.experimental import pallas as pl
from jax.experimental.pallas import tpu as pltpu

f32 = jnp.float32
bf16 = jnp.bfloat16
HI = lax.Precision.HIGHEST
MESH = pl.DeviceIdType.MESH

LANES = 128
VMEM_LIMIT = 56 << 20
TOK_TILE = 256
TOK_TILE_BLOCKED = 1024
MM_TILE = 1024
MM_TILE_K = 2048
MM_WHOLE_K_BYTES = 55 << 20
GDN_CHUNK = 128
WKV_CHUNK = 64
SCAN_GROUP = 8
P_BULK = 1
P_INV = 1
P_RESID = 3
P_CUMSUM = 3
RMS_EPS = 1e-6
RW_GN_EPS = 64e-5
RW_HEAD = 64

ADAM_LR, ADAM_B1, ADAM_B2, ADAM_EPS, ADAM_WD, ADAM_STEP = 0.001, 0.9, 0.999, 1e-08, 0.01, 10


def _params(n_grid):
    return pltpu.CompilerParams(dimension_semantics=("arbitrary",) * n_grid, vmem_limit_bytes=VMEM_LIMIT)


_DIMS = {"nn": (((1,), (0,)), ((), ())), "nt": (((1,), (1,)), ((), ())), "tn": (((0,), (0,)), ((), ()))}
_DIMS_BATCHED = {"nn": (((2,), (1,)), ((0,), (0,))), "nt": (((2,), (2,)), ((0,), (0,))),
                 "tn": (((1,), (1,)), ((0,), (0,)))}


def _raw_dot(a, b, mode, passes):
    dims = (_DIMS if a.ndim == 2 else _DIMS_BATCHED)[mode]
    if passes == 6:
        return lax.dot_general(a.astype(f32), b.astype(f32), dims, precision=HI, preferred_element_type=f32)
    ah, bh = a.astype(bf16), b.astype(bf16)
    r = lax.dot_general(ah, bh, dims, preferred_element_type=f32)
    if passes == 3:
        al = (a - ah.astype(f32)).astype(bf16)
        bl = (b - bh.astype(f32)).astype(bf16)
        r = r + lax.dot_general(al, bh, dims, preferred_element_type=f32)
        r = r + lax.dot_general(ah, bl, dims, preferred_element_type=f32)
    return r


@functools.partial(jax.custom_vjp, nondiff_argnums=(2, 3))
def pdot(a, b, mode, passes):
    return _raw_dot(a, b, mode, passes)


def _pdot_bwd(mode, passes, res, g):
    a, b = res
    if mode == "nn":
        da, db = _raw_dot(g, b, "nt", passes), _raw_dot(a, g, "tn", passes)
    elif mode == "nt":
        da, db = _raw_dot(g, b, "nn", passes), _raw_dot(g, a, "tn", passes)
    else:
        da, db = _raw_dot(b, g, "nt", passes), _raw_dot(a, g, "nn", passes)
    return da.astype(a.dtype), db.astype(b.dtype)


pdot.defvjp(lambda a, b, mode, passes: (_raw_dot(a, b, mode, passes), (a, b)), _pdot_bwd)


def bdot(a, b):
    return pdot(a, b, "nn", 1)


def bdot_nt(a, b):
    return pdot(a, b, "nt", 1)


def _shift_rows(x, k):
    row = lax.broadcasted_iota(jnp.int32, x.shape, 0)
    return jnp.where(row < k, 0.0, pltpu.roll(x, k, axis=0))


def _unshift_rows(g, k):
    n = g.shape[0]
    row = lax.broadcasted_iota(jnp.int32, g.shape, 0)
    return jnp.where(row >= n - k, 0.0, pltpu.roll(g, n - k, axis=0))


@functools.partial(jax.custom_vjp, nondiff_argnums=(1,))
def tshift(x, k):
    return _shift_rows(x, k)


tshift.defvjp(lambda x, k: (_shift_rows(x, k), None), lambda k, _, g: (_unshift_rows(g, k),))


def rms(x, w):
    x = x.astype(f32)
    return x * lax.rsqrt(jnp.mean(x * x, axis=-1, keepdims=True) + RMS_EPS) * w


def softplus(x):
    return jnp.maximum(x, 0.0) + jnp.log(1.0 + jnp.exp(-jnp.abs(x)))


def seg2sum(x):
    lo = lax.broadcasted_iota(jnp.int32, x.shape, 1) < RW_HEAD
    s_lo = jnp.sum(jnp.where(lo, x, 0.0), axis=-1, keepdims=True)
    s_hi = jnp.sum(jnp.where(lo, 0.0, x), axis=-1, keepdims=True)
    return jnp.where(lo, s_lo, s_hi)


def _tile(n, pref):
    if n <= pref:
        return n
    t = pref
    while t >= LANES:
        if n % t == 0:
            return t
        t -= LANES
    return n


def mm(name, a, b, *, ta=False, tb=False, add=None, out_dtype=f32, by_chip=False, epilogue=None, extra=(),
       out_dtypes=None, after=None):
    (k, m) = a.shape if ta else a.shape[::-1]
    (n, kb) = b.shape if tb else b.shape[::-1]
    assert k == kb, (name, a.shape, b.shape)
    tm, tk = _tile(m, MM_TILE), _tile(k, MM_TILE_K)
    tn = _tile(n // 4, MM_TILE) if by_chip else _tile(n, MM_TILE)
    vmem = VMEM_LIMIT
    narrow = _tile(n, MM_TILE // 2)
    whole_k = 2 * 2 * (tm + narrow) * k + 2 * 4 * tm * narrow * (1 + len(extra) + (add is not None))
    if k > tk and not by_chip and whole_k <= MM_WHOLE_K_BYTES:
        tn, tk, vmem = narrow, k, MM_WHOLE_K_BYTES + (7 << 20)
    nk = k // tk
    dims = (((0,) if ta else (1,), (1,) if tb else (0,)), ((), ()))
    extra = list(extra) + ([] if add is None else [add])
    out_dtypes = [out_dtype] if out_dtypes is None else list(out_dtypes)
    n_extra, n_out = len(extra), len(out_dtypes)
    n_after = 0 if after is None else 1

    def body(*refs):
        a_ref, b_ref = refs[:2]
        x_refs = refs[2:2 + n_extra]
        o_refs = refs[2 + n_extra + n_after:2 + n_extra + n_after + n_out]
        part = lax.dot_general(a_ref[...].astype(bf16), b_ref[...].astype(bf16), dims, preferred_element_type=f32)

        def finish(r):
            xs = [x[...] for x in x_refs]
            if add is not None:
                r = r + xs.pop().astype(f32)
            outs = (r,) if epilogue is None else epilogue(r, *xs)
            for o_ref, o in zip(o_refs, outs):
                o_ref[...] = o.astype(o_ref.dtype)

        if nk == 1:
            finish(part)
            return
        acc = refs[-1]
        kk = pl.program_id(2)

        @pl.when(kk == 0)
        def _():
            acc[...] = part

        @pl.when(kk > 0)
        def _():
            acc[...] += part

        @pl.when(kk == nk - 1)
        def _():
            finish(acc[...])

    a_spec = pl.BlockSpec((tk, tm), lambda i, j, q: (q, i)) if ta else pl.BlockSpec((tm, tk), lambda i, j, q: (i, q))
    b_spec = pl.BlockSpec((tn, tk), lambda i, j, q: (j, q)) if tb else pl.BlockSpec((tk, tn), lambda i, j, q: (q, j))
    x_spec = pl.BlockSpec((tm, tn), lambda i, j, q: (i, j))
    if by_chip:
        per_chip = n // 4 // tn
        o_spec = pl.BlockSpec((None, tm, tn), lambda i, j, q: (j // per_chip, i, j % per_chip))
        o_shape = (4, m, n // 4)
    else:
        o_spec, o_shape = x_spec, (m, n)
    afters = [] if after is None else [jnp.reshape(after, (1, 1))]
    res = pl.pallas_call(
        body, name=name, grid=(m // tm, n // tn, nk),
        in_specs=[a_spec, b_spec] + [x_spec] * n_extra + [pl.BlockSpec(memory_space=pl.ANY)] * n_after,
        out_specs=[o_spec] * n_out, out_shape=[jax.ShapeDtypeStruct(o_shape, dt) for dt in out_dtypes],
        scratch_shapes=[pltpu.VMEM((tm, tn), f32)] if nk > 1 else [],
        compiler_params=pltpu.CompilerParams(dimension_semantics=("arbitrary",) * 3, vmem_limit_bytes=vmem))(
            a, b, *extra, *afters)
    return res[0] if n_out == 1 else res


REGROUP_PIECES = 4


REGROUP_SUB = 5
REGROUP_FIELDS = 5


def _regroup_table(n_out, sources_of):
    import numpy as np
    tbl = np.zeros((REGROUP_FIELDS * REGROUP_PIECES, n_out), np.int32)
    for j in range(n_out):
        groups = sorted(sources_of(j).items())
        assert len(groups) <= REGROUP_PIECES, (j, len(groups))
        for p in range(REGROUP_PIECES):
            if p < len(groups):
                key, lanes = groups[p]
                shifts = {q - s for s, q in lanes}
                qs = sorted(q for _, q in lanes)
                assert len(shifts) == 1 and qs == list(range(qs[0], qs[-1] + 1)), (j, key)
                row = (key[0], key[1], shifts.pop(), qs[0], qs[-1] + 1)
            else:
                row = (0, 0, 0, 0, 0)
            tbl[REGROUP_FIELDS * p:REGROUP_FIELDS * (p + 1), j] = row
    return jnp.asarray(tbl)


def lane_regroup(name, src, table, src_spec, out_spec, out_shape, grid, src_width, sub):
    rows = src.shape[-2]
    n_src = REGROUP_PIECES * sub

    def body(tbl, *refs):
        o_ref = refs[n_src]
        step = pl.program_id(1)

        def moved(b, p):
            j = step * sub + b
            blk, shift, lo, hi = (tbl[REGROUP_FIELDS * p + f, j] for f in range(1, REGROUP_FIELDS))
            x = refs[b * REGROUP_PIECES + p][...]
            if src_width % LANES:
                lane = lax.broadcasted_iota(jnp.int32, (rows, LANES), 1)
                x = jnp.where(lane < src_width - blk * LANES, x, jnp.zeros((), src.dtype))
            pi = lax.broadcasted_iota(jnp.int32, (LANES, LANES), 0)
            qi = lax.broadcasted_iota(jnp.int32, (LANES, LANES), 1)
            sel = jnp.logical_and(qi - pi == shift, jnp.logical_and(qi >= lo, qi < hi))
            return jnp.dot(x, sel.astype(src.dtype), preferred_element_type=f32).astype(o_ref.dtype)

        for b in range(sub):
            lanes = slice(b * LANES, (b + 1) * LANES)
            o_ref[:, lanes] = moved(b, 0)
            for p in range(1, REGROUP_PIECES):
                j = step * sub + b

                @pl.when(tbl[REGROUP_FIELDS * p + 4, j] > tbl[REGROUP_FIELDS * p + 3, j])
                def _(b=b, p=p, lanes=lanes):
                    o_ref[:, lanes] += moved(b, p)

    return pl.pallas_call(
        body, name=name, out_shape=jax.ShapeDtypeStruct(out_shape, src.dtype),
        grid_spec=pltpu.PrefetchScalarGridSpec(
            num_scalar_prefetch=1, grid=grid,
            in_specs=[src_spec(p, b) for b in range(sub) for p in range(REGROUP_PIECES)], out_specs=out_spec),
        compiler_params=_params(2))(table, *[src] * n_src)


def w_in_layout(d_cols, shard_w, c_split, gap, n_blocks):
    def ext_of(c):
        return c if c < c_split else c + gap

    def fwd_sources(j):
        groups = {}
        for q in range(LANES):
            e = j * LANES + q
            c = e if e < c_split else e - gap
            if (c_split <= e < c_split + gap) or c >= d_cols:
                continue
            s, l = divmod(c, shard_w)
            groups.setdefault((s, l // LANES), []).append((l % LANES, q))
        return groups

    per_shard = -(-shard_w // LANES)

    def bwd_sources(j):
        s, b = divmod(j, per_shard)
        groups = {}
        for q in range(LANES):
            l = b * LANES + q
            if l >= shard_w:
                continue
            e = ext_of(s * shard_w + l)
            groups.setdefault((0, e // LANES), []).append((e % LANES, q))
        return groups

    return _regroup_table(n_blocks, fwd_sources), _regroup_table(4 * per_shard, bwd_sources), per_shard


def _in_spec(a, kind, tile):
    if kind == "row":
        return pl.BlockSpec((tile, a.shape[1]), lambda j, i: (i, 0))
    if kind == "const":
        return pl.BlockSpec(a.shape, lambda j, i: (0, 0))
    tag, cw, off = kind
    if tag == "rowc":
        return pl.BlockSpec((tile, cw), lambda j, i: (i, j + off))
    assert tag == "constc", kind
    return pl.BlockSpec((a.shape[0], cw), lambda j, i: (0, j + off))


def rowcall(name, fn, ins, outs, *, rows, tile=None, ncol=1):
    if tile is None:
        tile = min(TOK_TILE_BLOCKED if ncol > 1 else TOK_TILE, rows)
    n_in = len(ins)
    kinds = [o[3] for o in outs]

    def body(*refs):
        j, i = pl.program_id(0), pl.program_id(1)
        res = fn(*[r[...] for r in refs[:n_in]])
        for ref, val, kind in zip(refs[n_in:], res, kinds):
            if kind in ("row", "rowc"):
                ref[...] = val.astype(ref.dtype)
            else:
                first = (i == 0) if kind == "accc" else jnp.logical_and(i == 0, j == 0)

                @pl.when(first)
                def _(ref=ref, val=val):
                    ref[...] = val.astype(ref.dtype)

                @pl.when(jnp.logical_not(first))
                def _(ref=ref, val=val):
                    ref[...] += val.astype(ref.dtype)

    out_shape, out_specs = [], []
    for nr, nc, dtype, kind in outs:
        out_shape.append(jax.ShapeDtypeStruct((nr, nc), dtype))
        if kind == "row":
            out_specs.append(pl.BlockSpec((tile, nc), lambda j, i: (i, 0)))
        elif kind == "rowc":
            out_specs.append(pl.BlockSpec((tile, nc // ncol), lambda j, i: (i, j)))
        elif kind == "acc":
            out_specs.append(pl.BlockSpec((nr, nc), lambda j, i: (0, 0)))
        else:
            out_specs.append(pl.BlockSpec((nr, nc // ncol), lambda j, i: (0, j)))
    return pl.pallas_call(
        body, name=name, grid=(ncol, rows // tile), in_specs=[_in_spec(a, k, tile) for a, k in ins],
        out_specs=out_specs, out_shape=out_shape, compiler_params=_params(2))(*[a for a, _ in ins])


def rowvjp(name, fn, ins, cts, grads, *, rows, tile=None, ncol=1, adds=(), dup16=()):
    n_in = len(ins)
    ct_sizes = [len(c) for c in cts]
    flat_cts = [m for c in cts for m in c]
    n_ct = len(flat_cts)

    def wrapped(*vals):
        xs = list(vals[:n_in])
        gs = vals[n_in:n_in + n_ct]
        extra = vals[n_in + n_ct:]

        def f(*dvars):
            full = list(xs)
            for k, v in zip(grads, dvars):
                full[k] = v
            return fn(*full)

        outs, pull = jax.vjp(f, *[xs[k] for k in grads])
        cot, p = [], 0
        for o, size in zip(outs, ct_sizes):
            g = gs[p].astype(f32)
            for q in range(1, size):
                g = g + gs[p + q].astype(f32)
            cot.append(g.astype(o.dtype))
            p += size
        gv = list(pull(tuple(cot)))
        for (pos, _, _), e in zip(adds, extra):
            gv[pos] = gv[pos] + e.astype(gv[pos].dtype)
        return tuple(gv) + tuple(gv[pos] for pos in dup16)

    outs = []
    for k in grads:
        a, kind = ins[k]
        if kind == "row":
            outs.append((rows, a.shape[1] * ncol, f32, "rowc") if ncol > 1 else (rows, a.shape[1], f32, "row"))
        elif kind == "const":
            outs.append((a.shape[0], a.shape[1], f32, "acc"))
        elif kind[0] == "rowc":
            outs.append((rows, kind[1] * ncol, f32, "rowc"))
        else:
            outs.append((a.shape[0], kind[1] * ncol, f32, "accc"))
    for pos in dup16:
        nr, nc, _, kind = outs[pos]
        outs.append((nr, nc, bf16, kind))
    all_ins = list(ins) + flat_cts + [(a, kind) for _, a, kind in adds]
    return rowcall(name, wrapped, all_ins, outs, rows=rows, tile=tile, ncol=ncol)


def fn_rms(x, w):
    return (rms(x, w),)


def make_fn_gconv(n_norm_blocks):
    def fn(p, cw):
        c = cw[3:4] * p
        for jj in range(3):
            c = c + cw[jj:jj + 1] * tshift(p, 3 - jj)
        s = c * jax.nn.sigmoid(c)
        nrm = s * lax.rsqrt(jnp.sum(s * s, axis=-1, keepdims=True) + 1e-6)
        return (jnp.where(pl.program_id(0) < n_norm_blocks, nrm, s),)
    return fn


def fn_ggate(ps0, alog, dtb, e_g, e_b):
    g = -jnp.exp(alog) * softplus(ps0 + dtb)
    beta = jax.nn.sigmoid(ps0)
    return pdot(g, e_g, "nn", 6), pdot(beta, e_b, "nn", 6)


def fn_gpost(o, z, nw):
    return (rms(o, nw) * (z * jax.nn.sigmoid(z)),)


def fn_lerp(p, mu):
    return (p + (tshift(p, 1) - p) * mu,)


def fn_rprep(pk, psl, w0, a0, k_k, k_a, w2p, a2p, g2):
    g1, g2in = psl[:, :LANES], psl[:, LANES:]
    log_w = -softplus(-(w0 + bdot(jnp.tanh(g1), w2p))) - 0.5
    lw = -jnp.exp(log_w)
    a = jax.nn.sigmoid(a0 + bdot(g1, a2p))
    gate = bdot(jax.nn.sigmoid(g2in), g2)
    kkr = pk * k_k
    kk = kkr / jnp.maximum(jnp.sqrt(seg2sum(kkr * kkr)), 1e-12)
    kmod = pk * (1.0 + (a - 1.0) * k_a)
    return lw, kmod, kk, a, gate


def fn_rpost(y, r, kmod, v, gate, r_k, ln_w, ln_b):
    inv_n = 1.0 / RW_HEAD
    mean = seg2sum(y) * inv_n
    d = y - mean
    var = seg2sum(d * d) * inv_n
    yn = d * lax.rsqrt(var + RW_GN_EPS) * ln_w + ln_b
    bonus = seg2sum(r * kmod * r_k) * v
    return ((yn + bonus) * gate,)


def fn_xcore(q, k, v):
    s = bdot_nt(q, k) * (LANES ** -0.5)
    p = jax.nn.softmax(s, axis=-1)
    return (bdot(p, v),)


def fn_final(h, tgt, w):
    def loss_fn(h, w):
        return 0.5 * jnp.sum(jnp.mean(jnp.square(rms(h, w) - tgt), axis=-1))

    val, (dh, dw) = jax.value_and_grad(loss_fn, argnums=(0, 1))(h, w)
    return dh, dh.astype(bf16), dw, jnp.full((8, LANES), val, f32)


def fn_sumcols(n):
    def fn(x):
        w = x.shape[1] // n
        s = x[:, :w]
        for q in range(1, n):
            s = s + x[:, q * w:(q + 1) * w]
        return (s,)
    return fn


def _tri(c):
    ii = lax.broadcasted_iota(jnp.int32, (c, c), 0)
    jj = lax.broadcasted_iota(jnp.int32, (c, c), 1)
    return ii, jj


def _neumann_raw(m, steps):
    c = m.shape[-1]
    ii, jj = _tri(c)
    eye = (ii == jj).astype(f32)
    t, p = eye + m, m
    for _ in range(steps):
        p = _raw_dot(p, p, "nn", P_INV)
        t = _raw_dot(t, eye + p, "nn", P_INV)
    resid = eye - t + _raw_dot(m, t, "nn", P_RESID)
    return t + _raw_dot(t, resid, "nn", P_INV)


@functools.partial(jax.custom_vjp, nondiff_argnums=(1,))
def _neumann_inverse(m, steps):
    return _neumann_raw(m, steps)


def _neumann_fwd(m, steps):
    t = _neumann_raw(m, steps)
    return t, t


def _neumann_bwd(steps, t, g):
    return (_raw_dot(_raw_dot(t, g, "tn", P_RESID), t, "nt", P_RESID),)


_neumann_inverse.defvjp(_neumann_fwd, _neumann_bwd)


@jax.custom_vjp
def _known_inverse(m, t):
    return t


_known_inverse.defvjp(lambda m, t: (t, t), lambda t, g: (_neumann_bwd(0, t, g)[0], jnp.zeros_like(t)))


def _inverse(m, steps, kept):
    return _neumann_inverse(m, steps) if kept is None else _known_inverse(m, kept)


def cumsum_rows(x):
    t = x.shape[1]
    ii, jj = _tri(t)
    tri = jnp.broadcast_to((ii >= jj).astype(f32), (x.shape[0], t, t))
    return pdot(tri, x, "nn", P_CUMSUM)


def gdn_chunk(q, k, v, gb, bb, s, kept=None):
    c = q.shape[1]
    ii, jj = _tri(c)
    low = ii >= jj
    gcb = cumsum_rows(gb)
    gl = jnp.sum(gb, axis=1, keepdims=True)
    gc_col = gcb[:, :, :c]
    diff = gc_col - jnp.swapaxes(gc_col, 1, 2)
    decay = jnp.where(low, jnp.exp(jnp.where(low, diff, 0.0)), 0.0)
    qs = q * (q.shape[2] ** -0.5)
    kb = k * bb
    with_k = pdot(jnp.concatenate([kb, qs], axis=1), k, "nt", P_BULK)
    a = jnp.where(ii > jj, with_k[:, :c] * decay, 0.0)
    attn = with_k[:, c:] * decay
    t = _inverse(-a, c.bit_length() - 2, kept)
    eg = jnp.exp(gcb)
    uw = pdot(t, jnp.concatenate([v * bb, kb * eg], axis=2), "nn", P_BULK)
    u, w = uw[:, :, :LANES], uw[:, :, LANES:]
    kd = k * jnp.exp(gl - gcb)
    from_state = pdot(jnp.concatenate([w, qs * eg], axis=1), s, "nn", P_BULK)
    v_new = u - from_state[:, :c]
    o = from_state[:, c:] + pdot(attn, v_new, "nn", P_BULK)
    s_new = s * jnp.exp(gl) + pdot(kd, v_new, "tn", P_BULK)
    return o, s_new, t


def wkv_chunk(r, lw, k, v, kk, a, s, kept=None):
    t = r.shape[1]
    ii, jj = _tri(t)
    lo = lax.broadcasted_iota(jnp.int32, r.shape, 2) < RW_HEAD
    cl = cumsum_rows(lw)
    cl_last = jnp.sum(lw, axis=1, keepdims=True)
    al = -kk * jnp.exp(cl - lw)
    be = (a * kk) * jnp.exp(-cl)
    kt = k * jnp.exp(-cl)
    rt = r * jnp.exp(cl)

    def dot(xa, xb, mode="nn"):
        return pdot(xa, xb, mode, P_BULK)

    def sel(x_lo, x_hi):
        return jnp.where(lo, x_lo, x_hi)

    left = jnp.concatenate([jnp.where(lo, al, 0.0), jnp.where(lo, 0.0, al),
                            jnp.where(lo, rt, 0.0), jnp.where(lo, 0.0, rt)], axis=1)
    with_be, with_kt = dot(left, be, "nt"), dot(left, kt, "nt")

    def blocks(prod):
        below, upto = ii > jj, ii >= jj
        return [jnp.where(msk, prod[:, q * t:(q + 1) * t], 0.0) for q, msk in enumerate((below, below, upto, upto))]

    ab_lo, ab_hi, rb_lo, rb_hi = blocks(with_be)
    ak_lo, ak_hi, rk_lo, rk_hi = blocks(with_kt)
    from_state = dot(jnp.concatenate([al, rt], axis=1), s, "nt")
    x = from_state[:, :t] + sel(dot(ak_lo, v), dot(ak_hi, v))
    steps = t.bit_length() - 2
    inv_lo = _inverse(ab_lo, steps, None if kept is None else kept[:, :t])
    inv_hi = _inverse(ab_hi, steps, None if kept is None else kept[:, t:])
    u = sel(dot(inv_lo, x), dot(inv_hi, x))
    y = from_state[:, t:] + sel(dot(rb_lo, u) + dot(rk_lo, v), dot(rb_hi, u) + dot(rk_hi, v))
    vi = lax.broadcasted_iota(jnp.int32, s.shape, 1) < RW_HEAD
    ki = lax.broadcasted_iota(jnp.int32, s.shape, 2) < RW_HEAD
    s_new = jnp.where(vi == ki, (s + dot(u, be, "tn") + dot(v, kt, "tn")) * jnp.exp(cl_last), 0.0)
    return y, s_new, jnp.concatenate([inv_lo, inv_hi], axis=1)


def _scan_group(ncol, offs):
    g = SCAN_GROUP
    while g > 1 and (ncol % g or any(o % g for o in offs)):
        g //= 2
    return g


def scan_fwd(name, chunk_fn, ins, *, rows, chunk, ncol):
    n = rows // chunk
    n_in = len(ins)
    grp = _scan_group(ncol, [off for _, off in ins])
    tile = jax.ShapeDtypeStruct((grp, chunk, LANES), f32)
    kept = jax.eval_shape(chunk_fn, *[tile] * n_in, jax.ShapeDtypeStruct((grp, LANES, LANES), f32))[2].shape[1:]

    def body(*refs):
        o_ref, st_ref, kept_ref, s_scr = refs[n_in:]

        @pl.when(pl.program_id(1) == 0)
        def _():
            s_scr[...] = jnp.zeros_like(s_scr)

        cols = [slice(b * LANES, (b + 1) * LANES) for b in range(grp)]
        s = s_scr[...]
        st_ref[...] = s
        o, s_new, inv = chunk_fn(*[jnp.stack([r[:, c] for c in cols]) for r in refs[:n_in]], s)
        for b, c in enumerate(cols):
            o_ref[:, c] = o[b]
        kept_ref[...] = inv
        s_scr[...] = s_new

    def spec(off):
        return pl.BlockSpec((chunk, grp * LANES), lambda h, c: (c, h + off // grp))

    def per_chunk(shape):
        return pl.BlockSpec((grp, None) + tuple(shape), lambda h, c: (h, c, 0, 0))

    return pl.pallas_call(
        body, name=name, grid=(ncol // grp, n), in_specs=[spec(off) for _, off in ins],
        out_specs=[spec(0), per_chunk((LANES, LANES)), per_chunk(kept)],
        out_shape=[jax.ShapeDtypeStruct((rows, ncol * LANES), f32),
                   jax.ShapeDtypeStruct((ncol, n, LANES, LANES), f32),
                   jax.ShapeDtypeStruct((ncol, n) + tuple(kept), f32)],
        scratch_shapes=[pltpu.VMEM((grp, LANES, LANES), f32)], compiler_params=_params(2))(*[a for a, _ in ins])


def scan_bwd(name, chunk_fn, ins, states, kept, d_out, d_off, *, rows, chunk, ncol):
    n = rows // chunk
    n_in = len(ins)
    grp = _scan_group(ncol, [off for _, off in ins] + [d_off])

    def body(*refs):
        st_ref, kept_ref, do_ref = refs[n_in:n_in + 3]
        g_refs = refs[n_in + 3:2 * n_in + 3]
        ds_scr = refs[-1]

        @pl.when(pl.program_id(1) == 0)
        def _():
            ds_scr[...] = jnp.zeros_like(ds_scr)

        cols = [slice(b * LANES, (b + 1) * LANES) for b in range(grp)]

        def batch(ref):
            return jnp.stack([ref[:, c] for c in cols])

        inv = kept_ref[...]
        _, pull = jax.vjp(lambda *a: chunk_fn(*a, kept=inv)[:2], *[batch(r) for r in refs[:n_in]], st_ref[...])
        gs = pull((batch(do_ref), ds_scr[...]))
        for ref, g in zip(g_refs, gs[:n_in]):
            for b, c in enumerate(cols):
                ref[:, c] = g[b]
        ds_scr[...] = gs[n_in]

    def spec(off):
        return pl.BlockSpec((chunk, grp * LANES), lambda h, c: (n - 1 - c, h + off // grp))

    def per_chunk(shape):
        return pl.BlockSpec((grp, None) + tuple(shape), lambda h, c: (h, n - 1 - c, 0, 0))

    return pl.pallas_call(
        body, name=name, grid=(ncol // grp, n),
        in_specs=[spec(off) for _, off in ins] + [per_chunk(states.shape[2:]), per_chunk(kept.shape[2:]), spec(d_off)],
        out_specs=[spec(0)] * n_in, out_shape=[jax.ShapeDtypeStruct((rows, ncol * LANES), f32)] * n_in,
        scratch_shapes=[pltpu.VMEM((grp, LANES, LANES), f32)],
        compiler_params=_params(2))(*[a for a, _ in ins], states, kept, d_out)


def flip_exchange(name, arrs, flips, n_slots, slot_of, src_of, with_self, after=()):
    n = len(arrs)
    nf = len(flips)
    n_after = len(after)

    def body(*refs):
        ins, outs = refs[:n], refs[n + n_after:2 * n + n_after]
        send, recv, lsem = refs[2 * n + n_after:]
        me = (lax.axis_index("x"), lax.axis_index("y"), lax.axis_index("c"))
        copies = []
        for k in range(n):
            if with_self:
                cp = pltpu.make_async_copy(src_of(ins[k], me), outs[k].at[slot_of(me)], lsem.at[k])
                cp.start()
                copies.append(cp)
            for j, fl in enumerate(flips):
                peer = tuple(1 - m if f else m for m, f in zip(me, fl))
                cp = pltpu.make_async_remote_copy(
                    src_ref=src_of(ins[k], peer), dst_ref=outs[k].at[slot_of(me)], send_sem=send.at[k, j],
                    recv_sem=recv.at[k, j], device_id=peer, device_id_type=MESH)
                cp.start()
                copies.append(cp)
        for cp in copies:
            cp.wait()

    def out_sds(a):
        blk = src_of(jax.ShapeDtypeStruct(a.shape, a.dtype), None)
        return jax.ShapeDtypeStruct((n_slots,) + tuple(blk), a.dtype)

    any_spec = pl.BlockSpec(memory_space=pl.ANY)
    return pl.pallas_call(
        body, name=name, in_specs=[any_spec] * (n + n_after), out_specs=[any_spec] * n,
        out_shape=[out_sds(a) for a in arrs],
        scratch_shapes=[pltpu.SemaphoreType.DMA((n, nf)), pltpu.SemaphoreType.DMA((n, nf)),
                        pltpu.SemaphoreType.DMA((n,))],
        compiler_params=pltpu.CompilerParams(has_side_effects=True))(*arrs, *after)


_CHIP_FLIPS = ((1, 0, 0), (0, 1, 0), (1, 1, 0))
_ALL_FLIPS = ((0, 0, 1), (0, 1, 0), (0, 1, 1), (1, 0, 0), (1, 0, 1), (1, 1, 0), (1, 1, 1))


def _whole(ref, pos):
    return ref.shape if pos is None else ref


def _chip_block(ref, pos):
    return ref.shape[1:] if pos is None else ref.at[2 * pos[0] + pos[1]]


def _chip_slot(p):
    return 2 * p[0] + p[1]


_HBM = pl.BlockSpec(memory_space=pltpu.HBM)
_SEM = pl.BlockSpec(memory_space=pltpu.SEMAPHORE)
_DATAFLOW = pltpu.SideEffectType.DATAFLOW_SIDE_EFFECTING


def _split_copies(mode, refs, n, send, recv):
    me = (lax.axis_index("x"), lax.axis_index("y"), lax.axis_index("c"))
    sib = (me[0], me[1], 1 - me[2])
    lands = refs[:n] if mode == "handover" else refs[n:2 * n]
    copies = []
    for k, land in enumerate(lands):
        half = land.shape[1] // 2
        mine = pl.ds(pl.multiple_of(me[2] * half, 16), half)
        for j, fl in enumerate(_CHIP_FLIPS):
            peer = tuple(1 - m if f else m for m, f in zip(me, fl))
            if mode == "gather":
                src, dst, to = refs[k], land.at[_chip_slot(me)], peer
            elif mode == "scatter":
                src, dst, to = refs[k].at[_chip_slot(peer)], land.at[_chip_slot(me)], peer
            elif mode == "gather_half":
                src, dst, to = refs[k].at[mine], land.at[_chip_slot(me), mine], peer
            else:
                src = dst = land.at[_chip_slot(peer), mine]
                to = sib
            q = k * len(_CHIP_FLIPS) + j
            copies.append(pltpu.make_async_remote_copy(src_ref=src, dst_ref=dst, send_sem=send.at[q],
                                                       recv_sem=recv.at[q], device_id=to, device_id_type=MESH))
    return copies


def split_start(name, mode, ops, n):
    ops = [pltpu.with_memory_space_constraint(a, pltpu.HBM) for a in ops]
    m = len(ops)

    def body(*refs):
        for cp in _split_copies(mode, refs[:m], n, refs[m], refs[m + 1]):
            cp.start()
        refs[-1][...] = jnp.zeros_like(refs[-1])

    sems = pltpu.SemaphoreType.DMA((n * len(_CHIP_FLIPS),))
    outs = pl.pallas_call(
        body, name=name, in_specs=[_HBM] * m,
        out_shape=(sems, sems, *[pltpu.HBM(a.shape, a.dtype) for a in ops], jax.ShapeDtypeStruct((8, LANES), f32)),
        out_specs=(_SEM, _SEM, *[_HBM] * m, pl.BlockSpec(memory_space=pltpu.VMEM)),
        input_output_aliases={i: 2 + i for i in range(m)},
        compiler_params=pltpu.CompilerParams(has_side_effects=_DATAFLOW))(*ops)
    return (outs[0], outs[1], list(outs[2:2 + m]), mode, n), outs[-1][0, 0]


def split_wait(name, state, after, with_sources=False):
    send, recv, ops, mode, n = state
    m = len(ops)
    afters = list(after) if isinstance(after, (list, tuple)) else [after]

    def body(*refs):
        for cp in _split_copies(mode, refs[:m], n, refs[m], refs[m + 1]):
            cp.wait_send()
            cp.wait_recv()

    outs = pl.pallas_call(
        body, name=name, in_specs=[_HBM] * m + [_SEM, _SEM] + [pl.BlockSpec(memory_space=pl.ANY)] * len(afters),
        out_shape=tuple(pltpu.HBM(a.shape, a.dtype) for a in ops), out_specs=tuple([_HBM] * m),
        input_output_aliases={i: i for i in range(m)},
        compiler_params=pltpu.CompilerParams(has_side_effects=_DATAFLOW))(*ops, send, recv, *afters)
    return (list(outs[m - n:]), list(outs[:m - n])) if with_sources else list(outs[m - n:])


def chips_start(name, arrs, src_of, halves=False):
    me = _chip_slot((lax.axis_index("x"), lax.axis_index("y")))
    lands = []
    for a in arrs:
        blk = tuple(src_of(jax.ShapeDtypeStruct(a.shape, a.dtype), None))
        land = lax.empty((4,) + blk, a.dtype)
        lands.append(lax.dynamic_update_index_in_dim(land, a, me, 0) if src_of is _whole else land)
    mode = "scatter" if src_of is _chip_block else ("gather_half" if halves else "gather")
    return split_start(name, mode, list(arrs) + lands, len(arrs))


chips_wait = split_wait


def handover_start(name, lands):
    return split_start(name, "handover", lands, len(lands))


def gather_chips_halves(name, arrs):
    n = len(arrs)
    nf = len(_CHIP_FLIPS)
    split = [a.shape[0] % 32 == 0 for a in arrs]

    def body(*refs):
        ins, outs = refs[:n], refs[n:2 * n]
        tok_ref, send1, recv1, send2, recv2, lsem = refs[2 * n:]
        me = (lax.axis_index("x"), lax.axis_index("y"), lax.axis_index("c"))
        sib = (me[0], me[1], 1 - me[2])
        peers = [tuple(1 - m if f else m for m, f in zip(me, fl)) for fl in _CHIP_FLIPS]
        local, first, second = [], [], []
        for k in range(n):
            cp = pltpu.make_async_copy(ins[k], outs[k].at[_chip_slot(me)], lsem.at[k])
            cp.start()
            local.append(cp)
            half = ins[k].shape[0] // 2
            rows = pl.ds(pl.multiple_of(me[2] * half, 16), half) if split[k] else pl.ds(0, ins[k].shape[0])
            for j, peer in enumerate(peers):
                cp = pltpu.make_async_remote_copy(
                    src_ref=ins[k].at[rows], dst_ref=outs[k].at[_chip_slot(me), rows], send_sem=send1.at[k, j],
                    recv_sem=recv1.at[k, j], device_id=peer, device_id_type=MESH)
                cp.start()
                first.append((k, j, rows, cp))
        for k, j, rows, cp in first:
            cp.wait_recv()
            if split[k]:
                got = outs[k].at[_chip_slot(peers[j]), rows]
                fwd = pltpu.make_async_remote_copy(src_ref=got, dst_ref=got, send_sem=send2.at[k, j],
                                                   recv_sem=recv2.at[k, j], device_id=sib, device_id_type=MESH)
                fwd.start()
                second.append(fwd)
        for _, _, _, cp in first:
            cp.wait_send()
        for cp in second:
            cp.wait()
        for cp in local:
            cp.wait()
        tok_ref[...] = jnp.zeros_like(tok_ref)

    any_spec = pl.BlockSpec(memory_space=pl.ANY)
    sems = pltpu.SemaphoreType.DMA((n, nf))
    outs = pl.pallas_call(
        body, name=name, in_specs=[any_spec] * n,
        out_specs=[any_spec] * n + [pl.BlockSpec(memory_space=pltpu.VMEM)],
        out_shape=[jax.ShapeDtypeStruct((4,) + a.shape, a.dtype) for a in arrs] + [
            jax.ShapeDtypeStruct((8, LANES), f32)],
        scratch_shapes=[sems, sems, sems, sems, pltpu.SemaphoreType.DMA((n,))],
        compiler_params=pltpu.CompilerParams(has_side_effects=True))(*arrs)
    return list(outs[:n]), outs[n][0, 0]


def swap_sibling(name, arrs):
    outs = flip_exchange(name, arrs, ((0, 0, 1),), 1, lambda p: 0, _whole, False)
    return [o[0] for o in outs]


def gather_all(arrs, after=()):
    return flip_exchange("gather_all", arrs, _ALL_FLIPS, 8, lambda p: 4 * p[0] + 2 * p[1] + p[2], _whole, True,
                         after=after)


def _row_tile(nr, nc, n_arrays):
    budget = (20 << 20) // (n_arrays * 2 * 4 * max(nc, LANES))
    t = min(nr, budget) // 16 * 16
    while t > 0 and nr % t:
        t -= 16
    return t if t > 0 else nr


def sum_slots(name, x, own):
    ns, nr, nc = x.shape
    tile = _row_tile(nr, nc, ns + 1)
    me = jnp.reshape(_chip_slot((lax.axis_index("x"), lax.axis_index("y"))), (1,)).astype(jnp.int32)

    def body(me_ref, own_ref, *refs):
        o_ref = refs[-1]
        s = own_ref[...].astype(f32)
        for r in refs[:-1]:
            s = s + r[...].astype(f32)
        o_ref[...] = s.astype(o_ref.dtype)

    def slot(q):
        return pl.BlockSpec((None, tile, nc), lambda i, t: ((t[0] + q) % ns, i, 0))

    return pl.pallas_call(
        body, name=name, out_shape=jax.ShapeDtypeStruct((nr, nc), x.dtype),
        grid_spec=pltpu.PrefetchScalarGridSpec(
            num_scalar_prefetch=1, grid=(nr // tile,), in_specs=[slot(q) for q in range(ns)],
            out_specs=pl.BlockSpec((tile, nc), lambda i, t: (i, 0))),
        compiler_params=_params(1))(me, own, *[x] * (ns - 1))


def adamw(name, w, g_parts, m, v):
    nr, nc = w.shape[-2:]
    n_g = len(g_parts)
    tile = _row_tile(nr, nc, 7 + n_g)

    def body(*refs):
        w_ref, m_ref, v_ref = refs[:3]
        g = refs[3][...].astype(f32)
        for r in refs[4:3 + n_g]:
            g = g + r[...].astype(f32)
        g_ref, d_ref, nm_ref, nv_ref = refs[3 + n_g:]
        nm = ADAM_B1 * m_ref[...] + (1.0 - ADAM_B1) * g
        nv = ADAM_B2 * v_ref[...] + (1.0 - ADAM_B2) * jnp.square(g)
        m_hat = nm / (1.0 - ADAM_B1 ** ADAM_STEP)
        v_hat = nv / (1.0 - ADAM_B2 ** ADAM_STEP)
        g_ref[...] = g
        d_ref[...] = -ADAM_LR * (m_hat / (jnp.sqrt(v_hat) + ADAM_EPS) + ADAM_WD * w_ref[...])
        nm_ref[...] = nm
        nv_ref[...] = nv

    spec = pl.BlockSpec((tile, nc), lambda i: (i, 0))
    spec3 = pl.BlockSpec((None, tile, nc), lambda i: (0, i, 0)) if w.ndim == 3 else spec
    return pl.pallas_call(
        body, name=name, grid=(nr // tile,), in_specs=[spec3] * 3 + [spec] * n_g, out_specs=[spec3] * 4,
        out_shape=[jax.ShapeDtypeStruct(w.shape, f32)] * 4, compiler_params=_params(1))(w, m, v, *g_parts)


def adamw_packed(w, g8, m, v):
    nr, nc = w.shape

    def body(w_ref, g_ref, m_ref, v_ref, go_ref, d_ref, nm_ref, nv_ref):
        g = g_ref[0]
        for q in range(1, 8):
            g = g + g_ref[q]
        nm = ADAM_B1 * m_ref[...] + (1.0 - ADAM_B1) * g
        nv = ADAM_B2 * v_ref[...] + (1.0 - ADAM_B2) * jnp.square(g)
        m_hat = nm / (1.0 - ADAM_B1 ** ADAM_STEP)
        v_hat = nv / (1.0 - ADAM_B2 ** ADAM_STEP)
        go_ref[...] = g
        d_ref[...] = -ADAM_LR * (m_hat / (jnp.sqrt(v_hat) + ADAM_EPS) + ADAM_WD * w_ref[...])
        nm_ref[...] = nm
        nv_ref[...] = nv

    return pl.pallas_call(body, name="adamw_packed", out_shape=[jax.ShapeDtypeStruct((nr, nc), f32)] * 4,
                          compiler_params=pltpu.CompilerParams(vmem_limit_bytes=VMEM_LIMIT))(w, g8, m, v)


def _pack(vectors):
    rows = []
    for a in vectors:
        flat = a.reshape(-1).astype(f32)
        pad = (-flat.shape[0]) % LANES
        rows.append(jnp.pad(flat, (0, pad)).reshape(-1, LANES))
    packed = jnp.concatenate(rows, axis=0)
    return jnp.pad(packed, ((0, (-packed.shape[0]) % 8), (0, 0)))


def _unpack(packed, like):
    out, r = [], 0
    for a in like:
        n = a.size
        nr = -(-n // LANES)
        out.append(packed[r:r + nr].reshape(-1)[:n].reshape(a.shape))
        r += nr
    return out


def kernel(x, mem, mix_norm_w, w_in, dn_conv_w, dn_a_log, dn_dt_bias, dn_norm_w, rw_mu, rw_w0, rw_w2, rw_a0, rw_a2, rw_g2, rw_k_k, rw_k_a, rw_r_k, rw_ln_w, rw_ln_b, w_out, xa_norm_w, mem_norm_w, xa_wq, xa_wk, xa_wv, xa_wo, ffn_norm_w, ffn_w1, ffn_w2, final_norm_w, loss_target, m_mix_norm_w, m_w_in, m_dn_conv_w, m_dn_a_log, m_dn_dt_bias, m_dn_norm_w, m_rw_mu, m_rw_w0, m_rw_w2, m_rw_a0, m_rw_a2, m_rw_g2, m_rw_k_k, m_rw_k_a, m_rw_r_k, m_rw_ln_w, m_rw_ln_b, m_w_out, m_xa_norm_w, m_mem_norm_w, m_xa_wq, m_xa_wk, m_xa_wv, m_xa_wo, m_ffn_norm_w, m_ffn_w1, m_ffn_w2, m_final_norm_w, v_mix_norm_w, v_w_in, v_dn_conv_w, v_dn_a_log, v_dn_dt_bias, v_dn_norm_w, v_rw_mu, v_rw_w0, v_rw_w2, v_rw_a0, v_rw_a2, v_rw_g2, v_rw_k_k, v_rw_k_a, v_rw_r_k, v_rw_ln_w, v_rw_ln_b, v_w_out, v_xa_norm_w, v_mem_norm_w, v_xa_wq, v_xa_wk, v_xa_wv, v_xa_wo, v_ffn_norm_w, v_ffn_w1, v_ffn_w2, v_final_norm_w):
    weights = dict(mix_norm_w=mix_norm_w, w_in=w_in, dn_conv_w=dn_conv_w, dn_a_log=dn_a_log, dn_dt_bias=dn_dt_bias, dn_norm_w=dn_norm_w, rw_mu=rw_mu, rw_w0=rw_w0, rw_w2=rw_w2, rw_a0=rw_a0, rw_a2=rw_a2, rw_g2=rw_g2, rw_k_k=rw_k_k, rw_k_a=rw_k_a, rw_r_k=rw_r_k, rw_ln_w=rw_ln_w, rw_ln_b=rw_ln_b, w_out=w_out, xa_norm_w=xa_norm_w, mem_norm_w=mem_norm_w, xa_wq=xa_wq, xa_wk=xa_wk, xa_wv=xa_wv, xa_wo=xa_wo, ffn_norm_w=ffn_norm_w, ffn_w1=ffn_w1, ffn_w2=ffn_w2, final_norm_w=final_norm_w)
    mom_m = dict(mix_norm_w=m_mix_norm_w, w_in=m_w_in, dn_conv_w=m_dn_conv_w, dn_a_log=m_dn_a_log, dn_dt_bias=m_dn_dt_bias, dn_norm_w=m_dn_norm_w, rw_mu=m_rw_mu, rw_w0=m_rw_w0, rw_w2=m_rw_w2, rw_a0=m_rw_a0, rw_a2=m_rw_a2, rw_g2=m_rw_g2, rw_k_k=m_rw_k_k, rw_k_a=m_rw_k_a, rw_r_k=m_rw_r_k, rw_ln_w=m_rw_ln_w, rw_ln_b=m_rw_ln_b, w_out=m_w_out, xa_norm_w=m_xa_norm_w, mem_norm_w=m_mem_norm_w, xa_wq=m_xa_wq, xa_wk=m_xa_wk, xa_wv=m_xa_wv, xa_wo=m_xa_wo, ffn_norm_w=m_ffn_norm_w, ffn_w1=m_ffn_w1, ffn_w2=m_ffn_w2, final_norm_w=m_final_norm_w)
    mom_v = dict(mix_norm_w=v_mix_norm_w, w_in=v_w_in, dn_conv_w=v_dn_conv_w, dn_a_log=v_dn_a_log, dn_dt_bias=v_dn_dt_bias, dn_norm_w=v_dn_norm_w, rw_mu=v_rw_mu, rw_w0=v_rw_w0, rw_w2=v_rw_w2, rw_a0=v_rw_a0, rw_a2=v_rw_a2, rw_g2=v_rw_g2, rw_k_k=v_rw_k_k, rw_k_a=v_rw_k_a, rw_r_k=v_rw_r_k, rw_ln_w=v_rw_ln_w, rw_ln_b=v_rw_ln_b, w_out=v_w_out, xa_norm_w=v_xa_norm_w, mem_norm_w=v_mem_norm_w, xa_wq=v_xa_wq, xa_wk=v_xa_wk, xa_wv=v_xa_wv, xa_wo=v_xa_wo, ffn_norm_w=v_ffn_norm_w, ffn_w1=v_ffn_w1, ffn_w2=v_ffn_w2, final_norm_w=v_final_norm_w)
    names = list(weights)

    seq, d = x.shape[1], x.shape[2]
    dnw = d // 2
    rww = d - dnw
    nh, nb = dnw // LANES, rww // LANES
    n_mem = mem.shape[1]
    lw_dim, la_dim, lg_dim = rw_w2.shape[1], rw_a2.shape[1], rw_g2.shape[1]
    assert lw_dim + la_dim == LANES and lg_dim == LANES and dnw % LANES == 0 and rww % LANES == 0
    xs, mems, tgt = x[0], mem[0], loss_target[0]

    col_sharded = ("w_in", "xa_wo", "ffn_w1", "dn_conv_w", "rw_w2", "rw_a2", "rw_g2")
    row_sharded = ("w_out", "xa_wq", "xa_wk", "xa_wv", "ffn_w2")
    f32_payload = ("dn_conv_w", "rw_w2", "rw_a2", "rw_g2")
    sharded = col_sharded + row_sharded
    shard_w = w_in.shape[2]
    pad_w = -(-shard_w // LANES) * LANES
    first = ("w_in", "dn_conv_w", "rw_w2", "rw_a2", "rw_g2")
    mid = ("w_out", "xa_wq", "xa_wk", "xa_wv", "xa_wo")
    late = ("ffn_w1", "ffn_w2")
    payload = {n: weights[n][0].astype(f32 if n in f32_payload else bf16) for n in first}
    first_lands, tok = gather_chips_halves("gather_first", [payload[n] for n in first])
    gathered = dict(zip(first, first_lands))
    payload.update({n: (weights[n][0] + tok).astype(bf16) for n in mid + late})
    mid_state, tok_mid = chips_start("gather_mid_start", [payload[n] for n in mid], _whole, halves=True)
    late_state, tok_late = chips_start("gather_late_start", [payload[n] for n in late], _whole, halves=True)
    mix_norm_w_in = mix_norm_w + (tok_mid + tok_late)

    def full(n):
        g = gathered[n]
        if n in col_sharded:
            return g.transpose(1, 0, 2).reshape(g.shape[1], 4 * g.shape[2])
        return g.reshape(4 * g.shape[1], g.shape[2])

    c_rw0 = 4 * dnw + 2 * nh
    c_rw = 3 * rww
    eb_ab, eb_r = 4 * nh, 4 * nh + 1
    eb_l1 = eb_r + 3 * nb
    n_ext = -(-(eb_l1 + 2) // 4) * 4
    tbl_fwd, tbl_bwd, per_shard = w_in_layout(4 * shard_w, shard_w, c_rw0, LANES - 2 * nh, n_ext)
    sub_f = max(s for s in range(1, REGROUP_SUB + 1) if n_ext % s == 0)
    sub_b = max(s for s in range(1, REGROUP_SUB + 1) if per_shard % s == 0)
    w_ext = lane_regroup(
        "w_in_regroup", gathered["w_in"], tbl_fwd,
        lambda p, b: pl.BlockSpec(
            (None, d, LANES), lambda i, j, t: (t[REGROUP_FIELDS * p, j * sub_f + b], 0,
                                               t[REGROUP_FIELDS * p + 1, j * sub_f + b])),
        pl.BlockSpec((d, sub_f * LANES), lambda i, j, t: (0, j)), (d, n_ext * LANES), (1, n_ext // sub_f), shard_w,
        sub_f)
    conv_w = full("dn_conv_w")
    w2p = jnp.concatenate([full("rw_w2"), jnp.zeros((la_dim, rww), f32)], axis=0)
    a2p = jnp.concatenate([jnp.zeros((lw_dim, rww), f32), full("rw_a2")], axis=0)
    g2 = full("rw_g2")
    xaw = xa_wq.shape[2]
    nxh = xaw // LANES
    ffn = 4 * ffn_w1.shape[2]

    def lane_row(vec):
        return jnp.pad(vec.reshape(1, -1), ((0, 0), (0, LANES - vec.size)))

    alog_row, dtb_row = lane_row(dn_a_log), lane_row(dn_dt_bias)
    head_of_col = jnp.arange(dnw)[None, :] // LANES
    e_g = (jnp.arange(LANES)[:, None] == head_of_col).astype(f32)
    e_b = (jnp.arange(LANES)[:, None] == head_of_col + nh).astype(f32)
    mu_main, mu_small = rw_mu[:, :c_rw], rw_mu[:, c_rw:]
    r_k_row = rw_r_k.reshape(1, rww)
    fnw = final_norm_w.reshape(1, d)
    qb, kb_, vb, zb = 0, nh, 2 * nh, 3 * nh
    rb0 = eb_r
    cc = lambda w_, o_: ("constc", w_, o_)
    rc = lambda o_: ("rowc", LANES, o_)

    (u16,) = rowcall("norm_mix", fn_rms, [(xs, "row"), (mix_norm_w_in, "const")], [(seq, d, bf16, "row")], rows=seq)
    p_main = p_small = mm("in_proj", u16, w_ext)

    fn_gconv = make_fn_gconv(2 * nh)
    (qkv,) = rowcall("gdn_conv", fn_gconv, [(p_main, rc(0)), (conv_w, cc(LANES, 0))],
                     [(seq, 3 * dnw, f32, "rowc")], rows=seq, tile=seq, ncol=3 * nh)
    gate_ins = [(p_small, rc(eb_ab)), (alog_row, "const"), (dtb_row, "const"), (e_g, "const"), (e_b, "const")]
    g_b, beta_b = rowcall("gdn_gate", fn_ggate, gate_ins, [(seq, dnw, f32, "row")] * 2, rows=seq)
    gdn_ins = [(qkv, qb), (qkv, kb_), (qkv, vb), (g_b, 0), (beta_b, 0)]
    o_raw, gdn_states, gdn_kept = scan_fwd("gdn_scan", gdn_chunk, gdn_ins, rows=seq, chunk=GDN_CHUNK, ncol=nh)
    mid_state, tok = handover_start("gather_mid_pass", chips_wait("gather_mid_wait", mid_state, o_raw))
    gpost_ins = [(o_raw, rc(0)), (p_main, rc(zb)), (dn_norm_w + tok, "const")]
    (o_dn,) = rowcall("gdn_post", fn_gpost, gpost_ins, [(seq, dnw, bf16, "rowc")], rows=seq, ncol=nh)

    (prw,) = rowcall("rw_lerp_main", fn_lerp, [(p_main, rc(rb0)), (mu_main, cc(LANES, 0))],
                     [(seq, c_rw, f32, "rowc")], rows=seq, tile=seq, ncol=3 * nb)
    (psl,) = rowcall("rw_lerp_small", fn_lerp, [(p_small, rc(eb_l1)), (mu_small, cc(LANES, 0))],
                     [(seq, 2 * LANES, f32, "rowc")], rows=seq, tile=seq, ncol=2)
    rprep_ins = [(prw, rc(nb)), (psl, "row"), (rw_w0, cc(LANES, 0)), (rw_a0, cc(LANES, 0)), (rw_k_k, cc(LANES, 0)),
                 (rw_k_a, cc(LANES, 0)), (w2p, cc(LANES, 0)), (a2p, cc(LANES, 0)), (g2, cc(LANES, 0))]
    lw, kmod, kk, a_rw, gate = rowcall("rw_prep", fn_rprep, rprep_ins, [(seq, rww, f32, "rowc")] * 5,
                                        rows=seq, ncol=nb)
    wkv_ins = [(prw, 0), (lw, 0), (kmod, 0), (prw, 2 * nb), (kk, 0), (a_rw, 0)]
    y_rw, wkv_states, wkv_kept = scan_fwd("wkv_scan", wkv_chunk, wkv_ins, rows=seq, chunk=WKV_CHUNK, ncol=nb)
    rpost_ins = [(y_rw, rc(0)), (prw, rc(0)), (kmod, rc(0)), (prw, rc(2 * nb)), (gate, rc(0)),
                 (r_k_row, cc(LANES, 0)), (rw_ln_w, cc(LANES, 0)), (rw_ln_b, cc(LANES, 0))]
    (o_rw,) = rowcall("rw_post", fn_rpost, rpost_ins, [(seq, rww, bf16, "rowc")], rows=seq, ncol=nb)

    o_cat = jnp.concatenate([o_dn, o_rw], axis=1)
    late_state, tok = handover_start("gather_late_pass", chips_wait("gather_late_wait", late_state, o_cat))
    gathered.update(zip(mid, chips_wait("gather_mid_pass_wait", mid_state, o_cat)))
    w_out_f, wq_f, wk_f, wv_f, wo_f = full("w_out"), full("xa_wq"), full("xa_wk"), full("xa_wv"), full("xa_wo")
    h1 = mm("out_proj", o_cat, w_out_f, add=xs)

    (hn16,) = rowcall("norm_xa", fn_rms, [(h1, "row"), (xa_norm_w + tok, "const")], [(seq, d, bf16, "row")],
                      rows=seq)
    (mn16,) = rowcall("norm_mem", fn_rms, [(mems, "row"), (mem_norm_w, "const")], [(n_mem, d, bf16, "row")],
                      rows=n_mem)
    q_xa = mm("xa_q", hn16, wq_f)
    k_xa = mm("xa_k", mn16, wk_f)
    v_xa = mm("xa_v", mn16, wv_f)
    xcore_ins = [(q_xa, rc(0)), (k_xa, cc(LANES, 0)), (v_xa, cc(LANES, 0))]
    (o_xa,) = rowcall("xa_core", fn_xcore, xcore_ins, [(seq, xaw, bf16, "rowc")], rows=seq, ncol=nxh)
    h2 = mm("xa_o", o_xa, wo_f, add=h1)

    (fn16,) = rowcall("norm_ffn", fn_rms, [(h2, "row"), (ffn_norm_w, "const")], [(seq, d, bf16, "row")], rows=seq)
    gathered.update(zip(late, chips_wait("gather_late_pass_wait", late_state, fn16)))
    w1_f, w2_f = full("ffn_w1"), full("ffn_w2")
    a1_16, hid16 = mm("ffn_up", fn16, w1_f, epilogue=lambda r: (r, jnp.square(jnp.maximum(r, 0.0))),
                      out_dtypes=(bf16, bf16))
    h3 = mm("ffn_down", hid16, w2_f, add=h2)

    dh3, dh3_16, d_fnw, loss_rows = rowcall(
        "loss_head", fn_final, [(h3, "row"), (tgt, "row"), (fnw, "const")],
        [(seq, d, f32, "row"), (seq, d, bf16, "row"), (1, d, f32, "acc"), (8, LANES, f32, "acc")], rows=seq)

    da1_16 = mm("ffn_down_dx", dh3_16, w2_f, tb=True, extra=[a1_16], out_dtype=bf16,
                epilogue=lambda r, a1: (r * (2.0 * jnp.maximum(a1.astype(f32), 0.0)),))
    def by_chip(n, g):
        if g.ndim == 3:
            return g
        if n in col_sharded:
            return g.reshape(g.shape[0], 4, g.shape[1] // 4).transpose(1, 0, 2)
        return g.reshape(4, g.shape[0] // 4, g.shape[1])

    g_ffn_w2 = mm("ffn_down_dw", hid16, dh3_16, ta=True, out_dtype=bf16)
    ffn_w2_g, tok = chips_start("scatter_ffn_w2_start", [by_chip("ffn_w2", g_ffn_w2)], _chip_block)
    g_ffn_w1 = mm("ffn_up_dw", fn16, da1_16, ta=True, out_dtype=bf16, by_chip=True, after=tok)
    ffn_w1_g, tok = chips_start("scatter_ffn_w1_start", [g_ffn_w1], _chip_block)
    dfn = mm("ffn_up_dx", da1_16, w1_f, tb=True, after=tok)
    dh2, d_ffn_nw, dh2_16 = rowvjp("norm_ffn_bwd", fn_rms, [(h2, "row"), (ffn_norm_w, "const")],
                                   [[(dfn, "row")]], [0, 1], rows=seq, adds=[(0, dh3, "row")], dup16=[0])

    do_xa = mm("xa_o_dx", dh2_16, wo_f, tb=True)
    g_xa_wo = mm("xa_o_dw", o_xa, dh2_16, ta=True, out_dtype=bf16, by_chip=True)
    dq_xa, dk_xa, dv_xa, dq16 = rowvjp("xa_core_bwd", fn_xcore, xcore_ins, [[(do_xa, rc(0))]], [0, 1, 2],
                                       rows=seq, ncol=nxh, dup16=[0])
    g_xa_wq = mm("xa_q_dw", hn16, dq16, ta=True, out_dtype=bf16)
    dhn = mm("xa_q_dx", dq16, wq_f, tb=True)
    dh1, d_xa_nw, dh1_16 = rowvjp("norm_xa_bwd", fn_rms, [(h1, "row"), (xa_norm_w, "const")], [[(dhn, "row")]],
                                  [0, 1], rows=seq, adds=[(0, dh2, "row")], dup16=[0])
    dk16, dv16 = dk_xa.astype(bf16), dv_xa.astype(bf16)
    g_xa_wk = mm("xa_k_dw", mn16, dk16, ta=True, out_dtype=bf16)
    g_xa_wv = mm("xa_v_dw", mn16, dv16, ta=True, out_dtype=bf16)
    dmn = mm("xa_v_dx", dv16, wv_f, tb=True, add=mm("xa_k_dx", dk16, wk_f, tb=True))
    (d_mem_nw,) = rowvjp("norm_mem_bwd", fn_rms, [(mems, "row"), (mem_norm_w, "const")], [[(dmn, "row")]], [1],
                         rows=n_mem)

    g_w_out = mm("out_proj_dw", o_cat, dh1_16, ta=True, out_dtype=bf16)
    mid_grads = dict(w_out=g_w_out, xa_wq=g_xa_wq, xa_wk=g_xa_wk, xa_wv=g_xa_wv, xa_wo=g_xa_wo)
    mid_g, tok = chips_start("scatter_mid_start", [by_chip(n, mid_grads[n]) for n in mid], _chip_block)
    do_cat = mm("out_proj_dx", dh1_16, w_out_f, tb=True, after=tok)

    dy, dr_a, dkmod_a, dv_a, dgate, d_r_k, d_ln_w, d_ln_b = rowvjp(
        "rw_post_bwd", fn_rpost, rpost_ins, [[(do_cat, rc(nh))]], [0, 1, 2, 3, 4, 5, 6, 7], rows=seq, ncol=nb)
    dr_b, dlw, dkmod_b, dv_b, dkk, da_rw = scan_bwd("wkv_scan_bwd", wkv_chunk, wkv_ins, wkv_states, wkv_kept, dy, 0,
                                                    rows=seq, chunk=WKV_CHUNK, ncol=nb)
    rprep_cts = [[(dlw, rc(0))], [(dkmod_a, rc(0)), (dkmod_b, rc(0))], [(dkk, rc(0))], [(da_rw, rc(0))],
                 [(dgate, rc(0))]]
    dpk, dpsl_parts, d_w0, d_a0, d_k_k, d_k_a, d_w2p, d_a2p, d_g2 = rowvjp(
        "rw_prep_bwd", fn_rprep, rprep_ins, rprep_cts, [0, 1, 2, 3, 4, 5, 6, 7, 8], rows=seq, ncol=nb)
    (dpsl,) = rowcall("rw_prep_sum", fn_sumcols(nb), [(dpsl_parts, "row")], [(seq, 2 * LANES, f32, "row")], rows=seq)

    def lerp_bwd(tag, p, p_off, mu, mu_off, ct_lists, ncol):
        return rowvjp("rw_lerp_bwd_" + tag, fn_lerp, [(p, rc(p_off)), (mu, cc(LANES, mu_off))], [ct_lists], [0, 1],
                      rows=seq, tile=seq, ncol=ncol, dup16=[0])

    _, dmu_r, dpr16 = lerp_bwd("r", p_main, rb0, mu_main, 0, [(dr_a, rc(0)), (dr_b, rc(0))], nb)
    _, dmu_k, dpk16 = lerp_bwd("k", p_main, rb0 + nb, mu_main, nb, [(dpk, rc(0))], nb)
    _, dmu_v, dpv16 = lerp_bwd("v", p_main, rb0 + 2 * nb, mu_main, 2 * nb, [(dv_a, rc(0)), (dv_b, rc(0))], nb)
    _, dmu_s, dps12_16 = lerp_bwd("small", p_small, eb_l1, mu_small, 0, [(dpsl, rc(0))], 2)

    do_raw, dz, d_dn_nw, dz16 = rowvjp("gdn_post_bwd", fn_gpost, gpost_ins, [[(do_cat, rc(0))]], [0, 1, 2],
                                       rows=seq, ncol=nh, dup16=[1])
    dq_g, dk_g, dv_g, dg_b, dbeta_b = scan_bwd("gdn_scan_bwd", gdn_chunk, gdn_ins, gdn_states, gdn_kept, do_raw, 0,
                                               rows=seq, chunk=GDN_CHUNK, ncol=nh)
    dps0, d_alog, d_dtb, dps0_16 = rowvjp("gdn_gate_bwd", fn_ggate, gate_ins, [[(dg_b, "row")], [(dbeta_b, "row")]],
                                          [0, 1, 2], rows=seq, dup16=[0])
    dqkv = jnp.concatenate([dq_g, dk_g, dv_g], axis=1)
    _, d_conv_w, dqkv16 = rowvjp("gdn_conv_bwd", fn_gconv, [(p_main, rc(0)), (conv_w, cc(LANES, 0))],
                                 [[(dqkv, rc(0))]], [0, 1], rows=seq, tile=seq, ncol=3 * nh, dup16=[0])

    dp16 = jnp.concatenate([dqkv16, dz16, dps0_16, dpr16, dpk16, dpv16, dps12_16,
                            jnp.zeros((seq, (n_ext - eb_l1 - 2) * LANES), bf16)], axis=1)
    g_w_ext = mm("in_proj_dw", u16, dp16, ta=True, out_dtype=bf16)
    g_w_in = lane_regroup(
        "w_in_grad_regroup", g_w_ext, tbl_bwd,
        lambda p, b: pl.BlockSpec((d, LANES), lambda i, j, t: (0, t[REGROUP_FIELDS * p + 1, j * sub_b + b])),
        pl.BlockSpec((None, d, sub_b * LANES),
                     lambda i, j, t: (j // (per_shard // sub_b), 0, j % (per_shard // sub_b))),
        (4, d, pad_w), (1, 4 * per_shard // sub_b), n_ext * LANES, sub_b)
    first_grads = dict(w_in=g_w_in, dn_conv_w=d_conv_w, rw_w2=d_w2p[:lw_dim], rw_a2=d_a2p[lw_dim:], rw_g2=d_g2)
    first_g, tok = chips_start("scatter_first_start", [by_chip(n, first_grads[n]) for n in first], _chip_block)

    du = mm("in_proj_dx", dp16, w_ext, tb=True, after=tok)
    grad_x, d_mix_nw = rowvjp("norm_mix_bwd", fn_rms, [(xs, "row"), (mix_norm_w, "const")], [[(du, "row")]],
                              [0, 1], rows=seq, adds=[(0, dh1, "row")])
    received, sent = {}, {}

    def arrived(group, name, state, after):
        lands, sources = chips_wait(name, state, after, with_sources=True)
        received.update(zip(group, lands))
        sent.update(zip(group, sources))

    arrived(mid, "scatter_mid_wait", mid_g, grad_x)
    arrived(("ffn_w2",), "scatter_ffn_w2_wait", ffn_w2_g, grad_x)
    arrived(("ffn_w1",), "scatter_ffn_w1_wait", ffn_w1_g, grad_x)

    out = {}

    def reduce_and_update(tag, group):
        partial = [sum_slots("sum_chips_" + n, received[n], sent[n]) for n in group]
        other = swap_sibling("swap_sibling_" + tag, partial)
        for n, p_mine, p_other in zip(group, partial, other):
            if weights[n].shape[2] % LANES:
                rows, cols = weights[n].shape[1:]
                lin = lambda a: jnp.swapaxes(a, 1, 2).reshape(-1, LANES)
                lin_g = lambda p: p.T[:cols].reshape(-1, LANES)
                res = adamw("adamw_" + n, lin(weights[n]), [lin_g(p_mine), lin_g(p_other)], lin(mom_m[n]),
                            lin(mom_v[n]))
                out[n] = [jnp.swapaxes(r.reshape(1, cols, rows), 1, 2) for r in res]
            else:
                out[n] = adamw("adamw_" + n, weights[n], [p_mine, p_other], mom_m[n], mom_v[n])

    reduce_and_update("rest", mid + late)
    arrived(first, "scatter_first_wait", first_g, [out[n][1] for n in mid + late])
    reduce_and_update("first", first)

    small_names = [n for n in names if n not in sharded]
    small_local = dict(
        mix_norm_w=d_mix_nw, dn_a_log=d_alog[:, :nh], dn_dt_bias=d_dtb[:, :nh], dn_norm_w=d_dn_nw,
        rw_mu=jnp.concatenate([dmu_r, dmu_k, dmu_v, dmu_s], axis=1), rw_w0=d_w0, rw_a0=d_a0, rw_k_k=d_k_k,
        rw_k_a=d_k_a, rw_r_k=d_r_k, rw_ln_w=d_ln_w, rw_ln_b=d_ln_b, xa_norm_w=d_xa_nw, mem_norm_w=d_mem_nw,
        ffn_norm_w=d_ffn_nw, final_norm_w=d_fnw)
    loss_vec = jnp.where(jnp.arange(LANES) == 0, loss_rows[0], 0.0)
    (g8,) = gather_all([_pack([small_local[n] for n in small_names] + [loss_vec])], after=[out["w_in"][1]])

    packed_like = [weights[n] for n in small_names] + [loss_vec]
    zero = jnp.zeros((LANES,), f32)
    res = adamw_packed(_pack([weights[n] for n in small_names] + [zero]), g8,
                       _pack([mom_m[n] for n in small_names] + [zero]),
                       _pack([mom_v[n] for n in small_names] + [zero]))
    unpacked = [_unpack(r, packed_like) for r in res]
    for i, n in enumerate(small_names):
        out[n] = [u[i] for u in unpacked]
    loss = unpacked[0][-1][0]

    return (loss, grad_x.reshape(x.shape), *[out[n][0] for n in names], *[out[n][1] for n in names],
            *[out[n][2] for n in names], *[out[n][3] for n in names])
```

```python
import functools

import jax
import jax.numpy as jnp
from jax import lax
from jax.experimental import pallas as pl
from jax.experimental.pallas import tpu as pltpu

f32 = jnp.float32
bf16 = jnp.bfloat16
HI = lax.Precision.HIGHEST
MESH = pl.DeviceIdType.MESH

LANES = 128
VMEM_LIMIT = 56 << 20
TOK_TILE = 256
TOK_TILE_BLOCKED = 1024
MM_TILE = 1024
MM_TILE_K = 2048
MM_WHOLE_K_BYTES = 55 << 20
GDN_CHUNK = 128
WKV_CHUNK = 64
SCAN_GROUP = 8
P_BULK = 1
P_INV = 1
P_RESID = 3
P_CUMSUM = 3
RMS_EPS = 1e-6
RW_GN_EPS = 64e-5
RW_HEAD = 64

ADAM_LR, ADAM_B1, ADAM_B2, ADAM_EPS, ADAM_WD, ADAM_STEP = 0.001, 0.9, 0.999, 1e-08, 0.01, 10


def _params(n_grid):
    return pltpu.CompilerParams(dimension_semantics=("arbitrary",) * n_grid, vmem_limit_bytes=VMEM_LIMIT)


_DIMS = {"nn": (((1,), (0,)), ((), ())), "nt": (((1,), (1,)), ((), ())), "tn": (((0,), (0,)), ((), ()))}
_DIMS_BATCHED = {"nn": (((2,), (1,)), ((0,), (0,))), "nt": (((2,), (2,)), ((0,), (0,))),
                 "tn": (((1,), (1,)), ((0,), (0,)))}


def _raw_dot(a, b, mode, passes):
    dims = (_DIMS if a.ndim == 2 else _DIMS_BATCHED)[mode]
    if passes == 6:
        return lax.dot_general(a.astype(f32), b.astype(f32), dims, precision=HI, preferred_element_type=f32)
    ah, bh = a.astype(bf16), b.astype(bf16)
    r = lax.dot_general(ah, bh, dims, preferred_element_type=f32)
    if passes == 3:
        al = (a - ah.astype(f32)).astype(bf16)
        bl = (b - bh.astype(f32)).astype(bf16)
        r = r + lax.dot_general(al, bh, dims, preferred_element_type=f32)
        r = r + lax.dot_general(ah, bl, dims, preferred_element_type=f32)
    return r


@functools.partial(jax.custom_vjp, nondiff_argnums=(2, 3))
def pdot(a, b, mode, passes):
    return _raw_dot(a, b, mode, passes)


def _pdot_bwd(mode, passes, res, g):
    a, b = res
    if mode == "nn":
        da, db = _raw_dot(g, b, "nt", passes), _raw_dot(a, g, "tn", passes)
    elif mode == "nt":
        da, db = _raw_dot(g, b, "nn", passes), _raw_dot(g, a, "tn", passes)
    else:
        da, db = _raw_dot(b, g, "nt", passes), _raw_dot(a, g, "nn", passes)
    return da.astype(a.dtype), db.astype(b.dtype)


pdot.defvjp(lambda a, b, mode, passes: (_raw_dot(a, b, mode, passes), (a, b)), _pdot_bwd)


def bdot(a, b):
    return pdot(a, b, "nn", 1)


def bdot_nt(a, b):
    return pdot(a, b, "nt", 1)


def _shift_rows(x, k):
    row = lax.broadcasted_iota(jnp.int32, x.shape, 0)
    return jnp.where(row < k, 0.0, pltpu.roll(x, k, axis=0))


def _unshift_rows(g, k):
    n = g.shape[0]
    row = lax.broadcasted_iota(jnp.int32, g.shape, 0)
    return jnp.where(row >= n - k, 0.0, pltpu.roll(g, n - k, axis=0))


@functools.partial(jax.custom_vjp, nondiff_argnums=(1,))
def tshift(x, k):
    return _shift_rows(x, k)


tshift.defvjp(lambda x, k: (_shift_rows(x, k), None), lambda k, _, g: (_unshift_rows(g, k),))


def rms(x, w):
    x = x.astype(f32)
    return x * lax.rsqrt(jnp.mean(x * x, axis=-1, keepdims=True) + RMS_EPS) * w


def softplus(x):
    return jnp.maximum(x, 0.0) + jnp.log(1.0 + jnp.exp(-jnp.abs(x)))


def seg2sum(x):
    lo = lax.broadcasted_iota(jnp.int32, x.shape, 1) < RW_HEAD
    s_lo = jnp.sum(jnp.where(lo, x, 0.0), axis=-1, keepdims=True)
    s_hi = jnp.sum(jnp.where(lo, 0.0, x), axis=-1, keepdims=True)
    return jnp.where(lo, s_lo, s_hi)


def _tile(n, pref):
    if n <= pref:
        return n
    t = pref
    while t >= LANES:
        if n % t == 0:
            return t
        t -= LANES
    return n


def mm(name, a, b, *, ta=False, tb=False, add=None, out_dtype=f32, by_chip=False, epilogue=None, extra=(),
       out_dtypes=None, after=None):
    (k, m) = a.shape if ta else a.shape[::-1]
    (n, kb) = b.shape if tb else b.shape[::-1]
    assert k == kb, (name, a.shape, b.shape)
    tm, tk = _tile(m, MM_TILE), _tile(k, MM_TILE_K)
    tn = _tile(n // 4, MM_TILE) if by_chip else _tile(n, MM_TILE)
    vmem = VMEM_LIMIT
    narrow = _tile(n, MM_TILE // 2)
    whole_k = 2 * 2 * (tm + narrow) * k + 2 * 4 * tm * narrow * (1 + len(extra) + (add is not None))
    if k > tk and not by_chip and whole_k <= MM_WHOLE_K_BYTES:
        tn, tk, vmem = narrow, k, MM_WHOLE_K_BYTES + (7 << 20)
    nk = k // tk
    dims = (((0,) if ta else (1,), (1,) if tb else (0,)), ((), ()))
    extra = list(extra) + ([] if add is None else [add])
    out_dtypes = [out_dtype] if out_dtypes is None else list(out_dtypes)
    n_extra, n_out = len(extra), len(out_dtypes)
    n_after = 0 if after is None else 1

    def body(*refs):
        a_ref, b_ref = refs[:2]
        x_refs = refs[2:2 + n_extra]
        o_refs = refs[2 + n_extra + n_after:2 + n_extra + n_after + n_out]
        part = lax.dot_general(a_ref[...].astype(bf16), b_ref[...].astype(bf16), dims, preferred_element_type=f32)

        def finish(r):
            xs = [x[...] for x in x_refs]
            if add is not None:
                r = r + xs.pop().astype(f32)
            outs = (r,) if epilogue is None else epilogue(r, *xs)
            for o_ref, o in zip(o_refs, outs):
                o_ref[...] = o.astype(o_ref.dtype)

        if nk == 1:
            finish(part)
            return
        acc = refs[-1]
        kk = pl.program_id(2)

        @pl.when(kk == 0)
        def _():
            acc[...] = part

        @pl.when(kk > 0)
        def _():
            acc[...] += part

        @pl.when(kk == nk - 1)
        def _():
            finish(acc[...])

    a_spec = pl.BlockSpec((tk, tm), lambda i, j, q: (q, i)) if ta else pl.BlockSpec((tm, tk), lambda i, j, q: (i, q))
    b_spec = pl.BlockSpec((tn, tk), lambda i, j, q: (j, q)) if tb else pl.BlockSpec((tk, tn), lambda i, j, q: (q, j))
    x_spec = pl.BlockSpec((tm, tn), lambda i, j, q: (i, j))
    if by_chip:
        per_chip = n // 4 // tn
        o_spec = pl.BlockSpec((None, tm, tn), lambda i, j, q: (j // per_chip, i, j % per_chip))
        o_shape = (4, m, n // 4)
    else:
        o_spec, o_shape = x_spec, (m, n)
    afters = [] if after is None else [jnp.reshape(after, (1, 1))]
    res = pl.pallas_call(
        body, name=name, grid=(m // tm, n // tn, nk),
        in_specs=[a_spec, b_spec] + [x_spec] * n_extra + [pl.BlockSpec(memory_space=pl.ANY)] * n_after,
        out_specs=[o_spec] * n_out, out_shape=[jax.ShapeDtypeStruct(o_shape, dt) for dt in out_dtypes],
        scratch_shapes=[pltpu.VMEM((tm, tn), f32)] if nk > 1 else [],
        compiler_params=pltpu.CompilerParams(dimension_semantics=("arbitrary",) * 3, vmem_limit_bytes=vmem))(
            a, b, *extra, *afters)
    return res[0] if n_out == 1 else res


GATHER_PARTS = 4
REGROUP_PIECES = 4


REGROUP_SUB = 5
REGROUP_FIELDS = 5


def _regroup_table(n_out, sources_of):
    import numpy as np
    tbl = np.zeros((REGROUP_FIELDS * REGROUP_PIECES, n_out), np.int32)
    for j in range(n_out):
        groups = sorted(sources_of(j).items())
        assert len(groups) <= REGROUP_PIECES, (j, len(groups))
        for p in range(REGROUP_PIECES):
            if p < len(groups):
                key, lanes = groups[p]
                shifts = {q - s for s, q in lanes}
                qs = sorted(q for _, q in lanes)
                assert len(shifts) == 1 and qs == list(range(qs[0], qs[-1] + 1)), (j, key)
                row = (key[0], key[1], shifts.pop(), qs[0], qs[-1] + 1)
            else:
                row = (0, 0, 0, 0, 0)
            tbl[REGROUP_FIELDS * p:REGROUP_FIELDS * (p + 1), j] = row
    return jnp.asarray(tbl)


def lane_regroup(name, src, table, src_spec, out_spec, out_shape, grid, src_width, sub):
    rows = src.shape[-2]
    n_src = REGROUP_PIECES * sub

    def body(tbl, *refs):
        o_ref = refs[n_src]
        step = pl.program_id(1)

        def moved(b, p):
            j = step * sub + b
            blk, shift, lo, hi = (tbl[REGROUP_FIELDS * p + f, j] for f in range(1, REGROUP_FIELDS))
            x = refs[b * REGROUP_PIECES + p][...]
            if src_width % LANES:
                lane = lax.broadcasted_iota(jnp.int32, (rows, LANES), 1)
                x = jnp.where(lane < src_width - blk * LANES, x, jnp.zeros((), src.dtype))
            pi = lax.broadcasted_iota(jnp.int32, (LANES, LANES), 0)
            qi = lax.broadcasted_iota(jnp.int32, (LANES, LANES), 1)
            sel = jnp.logical_and(qi - pi == shift, jnp.logical_and(qi >= lo, qi < hi))
            return jnp.dot(x, sel.astype(src.dtype), preferred_element_type=f32).astype(o_ref.dtype)

        for b in range(sub):
            lanes = slice(b * LANES, (b + 1) * LANES)
            o_ref[:, lanes] = moved(b, 0)
            for p in range(1, REGROUP_PIECES):
                j = step * sub + b

                @pl.when(tbl[REGROUP_FIELDS * p + 4, j] > tbl[REGROUP_FIELDS * p + 3, j])
                def _(b=b, p=p, lanes=lanes):
                    o_ref[:, lanes] += moved(b, p)

    return pl.pallas_call(
        body, name=name, out_shape=jax.ShapeDtypeStruct(out_shape, src.dtype),
        grid_spec=pltpu.PrefetchScalarGridSpec(
            num_scalar_prefetch=1, grid=grid,
            in_specs=[src_spec(p, b) for b in range(sub) for p in range(REGROUP_PIECES)], out_specs=out_spec),
        compiler_params=_params(2))(table, *[src] * n_src)


def w_in_layout(d_cols, shard_w, c_split, gap, n_blocks):
    def ext_of(c):
        return c if c < c_split else c + gap

    def fwd_sources(j):
        groups = {}
        for q in range(LANES):
            e = j * LANES + q
            c = e if e < c_split else e - gap
            if (c_split <= e < c_split + gap) or c >= d_cols:
                continue
            s, l = divmod(c, shard_w)
            groups.setdefault((s, l // LANES), []).append((l % LANES, q))
        return groups

    per_shard = -(-shard_w // LANES)

    def bwd_sources(j):
        s, b = divmod(j, per_shard)
        groups = {}
        for q in range(LANES):
            l = b * LANES + q
            if l >= shard_w:
                continue
            e = ext_of(s * shard_w + l)
            groups.setdefault((0, e // LANES), []).append((e % LANES, q))
        return groups

    return _regroup_table(n_blocks, fwd_sources), _regroup_table(4 * per_shard, bwd_sources), per_shard


def _in_spec(a, kind, tile):
    if kind == "row":
        return pl.BlockSpec((tile, a.shape[1]), lambda j, i: (i, 0))
    if kind == "const":
        return pl.BlockSpec(a.shape, lambda j, i: (0, 0))
    tag, cw, off = kind
    if tag == "rowc":
        return pl.BlockSpec((tile, cw), lambda j, i: (i, j + off))
    assert tag == "constc", kind
    return pl.BlockSpec((a.shape[0], cw), lambda j, i: (0, j + off))


def rowcall(name, fn, ins, outs, *, rows, tile=None, ncol=1):
    if tile is None:
        tile = min(TOK_TILE_BLOCKED if ncol > 1 else TOK_TILE, rows)
    n_in = len(ins)
    kinds = [o[3] for o in outs]

    def body(*refs):
        j, i = pl.program_id(0), pl.program_id(1)
        res = fn(*[r[...] for r in refs[:n_in]])
        for ref, val, kind in zip(refs[n_in:], res, kinds):
            if kind in ("row", "rowc"):
                ref[...] = val.astype(ref.dtype)
            else:
                first = (i == 0) if kind == "accc" else jnp.logical_and(i == 0, j == 0)

                @pl.when(first)
                def _(ref=ref, val=val):
                    ref[...] = val.astype(ref.dtype)

                @pl.when(jnp.logical_not(first))
                def _(ref=ref, val=val):
                    ref[...] += val.astype(ref.dtype)

    out_shape, out_specs = [], []
    for nr, nc, dtype, kind in outs:
        out_shape.append(jax.ShapeDtypeStruct((nr, nc), dtype))
        if kind == "row":
            out_specs.append(pl.BlockSpec((tile, nc), lambda j, i: (i, 0)))
        elif kind == "rowc":
            out_specs.append(pl.BlockSpec((tile, nc // ncol), lambda j, i: (i, j)))
        elif kind == "acc":
            out_specs.append(pl.BlockSpec((nr, nc), lambda j, i: (0, 0)))
        else:
            out_specs.append(pl.BlockSpec((nr, nc // ncol), lambda j, i: (0, j)))
    return pl.pallas_call(
        body, name=name, grid=(ncol, rows // tile), in_specs=[_in_spec(a, k, tile) for a, k in ins],
        out_specs=out_specs, out_shape=out_shape, compiler_params=_params(2))(*[a for a, _ in ins])


def rowvjp(name, fn, ins, cts, grads, *, rows, tile=None, ncol=1, adds=(), dup16=()):
    n_in = len(ins)
    ct_sizes = [len(c) for c in cts]
    flat_cts = [m for c in cts for m in c]
    n_ct = len(flat_cts)

    def wrapped(*vals):
        xs = list(vals[:n_in])
        gs = vals[n_in:n_in + n_ct]
        extra = vals[n_in + n_ct:]

        def f(*dvars):
            full = list(xs)
            for k, v in zip(grads, dvars):
                full[k] = v
            return fn(*full)

        outs, pull = jax.vjp(f, *[xs[k] for k in grads])
        cot, p = [], 0
        for o, size in zip(outs, ct_sizes):
            g = gs[p].astype(f32)
            for q in range(1, size):
                g = g + gs[p + q].astype(f32)
            cot.append(g.astype(o.dtype))
            p += size
        gv = list(pull(tuple(cot)))
        for (pos, _, _), e in zip(adds, extra):
            gv[pos] = gv[pos] + e.astype(gv[pos].dtype)
        return tuple(gv) + tuple(gv[pos] for pos in dup16)

    outs = []
    for k in grads:
        a, kind = ins[k]
        if kind == "row":
            outs.append((rows, a.shape[1] * ncol, f32, "rowc") if ncol > 1 else (rows, a.shape[1], f32, "row"))
        elif kind == "const":
            outs.append((a.shape[0], a.shape[1], f32, "acc"))
        elif kind[0] == "rowc":
            outs.append((rows, kind[1] * ncol, f32, "rowc"))
        else:
            outs.append((a.shape[0], kind[1] * ncol, f32, "accc"))
    for pos in dup16:
        nr, nc, _, kind = outs[pos]
        outs.append((nr, nc, bf16, kind))
    all_ins = list(ins) + flat_cts + [(a, kind) for _, a, kind in adds]
    return rowcall(name, wrapped, all_ins, outs, rows=rows, tile=tile, ncol=ncol)


def fn_rms(x, w):
    return (rms(x, w),)


def make_fn_gconv(n_norm_blocks):
    def fn(p, cw):
        c = cw[3:4] * p
        for jj in range(3):
            c = c + cw[jj:jj + 1] * tshift(p, 3 - jj)
        s = c * jax.nn.sigmoid(c)
        nrm = s * lax.rsqrt(jnp.sum(s * s, axis=-1, keepdims=True) + 1e-6)
        return (jnp.where(pl.program_id(0) < n_norm_blocks, nrm, s),)
    return fn


def fn_ggate(ps0, alog, dtb, e_g, e_b):
    g = -jnp.exp(alog) * softplus(ps0 + dtb)
    beta = jax.nn.sigmoid(ps0)
    return pdot(g, e_g, "nn", 6), pdot(beta, e_b, "nn", 6)


def fn_gpost(o, z, nw):
    return (rms(o, nw) * (z * jax.nn.sigmoid(z)),)


def fn_lerp(p, mu):
    return (p + (tshift(p, 1) - p) * mu,)


def fn_rprep(pk, psl, w0, a0, k_k, k_a, w2p, a2p, g2):
    g1, g2in = psl[:, :LANES], psl[:, LANES:]
    log_w = -softplus(-(w0 + bdot(jnp.tanh(g1), w2p))) - 0.5
    lw = -jnp.exp(log_w)
    a = jax.nn.sigmoid(a0 + bdot(g1, a2p))
    gate = bdot(jax.nn.sigmoid(g2in), g2)
    kkr = pk * k_k
    kk = kkr / jnp.maximum(jnp.sqrt(seg2sum(kkr * kkr)), 1e-12)
    kmod = pk * (1.0 + (a - 1.0) * k_a)
    return lw, kmod, kk, a, gate


def fn_rpost(y, r, kmod, v, gate, r_k, ln_w, ln_b):
    inv_n = 1.0 / RW_HEAD
    mean = seg2sum(y) * inv_n
    d = y - mean
    var = seg2sum(d * d) * inv_n
    yn = d * lax.rsqrt(var + RW_GN_EPS) * ln_w + ln_b
    bonus = seg2sum(r * kmod * r_k) * v
    return ((yn + bonus) * gate,)


def fn_xcore(q, k, v):
    s = bdot_nt(q, k) * (LANES ** -0.5)
    p = jax.nn.softmax(s, axis=-1)
    return (bdot(p, v),)


def fn_final(h, tgt, w):
    def loss_fn(h, w):
        return 0.5 * jnp.sum(jnp.mean(jnp.square(rms(h, w) - tgt), axis=-1))

    val, (dh, dw) = jax.value_and_grad(loss_fn, argnums=(0, 1))(h, w)
    return dh, dh.astype(bf16), dw, jnp.full((8, LANES), val, f32)


def fn_sumcols(n):
    def fn(x):
        w = x.shape[1] // n
        s = x[:, :w]
        for q in range(1, n):
            s = s + x[:, q * w:(q + 1) * w]
        return (s,)
    return fn


def _tri(c):
    ii = lax.broadcasted_iota(jnp.int32, (c, c), 0)
    jj = lax.broadcasted_iota(jnp.int32, (c, c), 1)
    return ii, jj


def _neumann_raw(m, steps):
    c = m.shape[-1]
    ii, jj = _tri(c)
    eye = (ii == jj).astype(f32)
    t, p = eye + m, m
    for _ in range(steps):
        p = _raw_dot(p, p, "nn", P_INV)
        t = _raw_dot(t, eye + p, "nn", P_INV)
    resid = eye - t + _raw_dot(m, t, "nn", P_RESID)
    return t + _raw_dot(t, resid, "nn", P_INV)


@functools.partial(jax.custom_vjp, nondiff_argnums=(1,))
def _neumann_inverse(m, steps):
    return _neumann_raw(m, steps)


def _neumann_fwd(m, steps):
    t = _neumann_raw(m, steps)
    return t, t


def _neumann_bwd(steps, t, g):
    return (_raw_dot(_raw_dot(t, g, "tn", P_RESID), t, "nt", P_RESID),)


_neumann_inverse.defvjp(_neumann_fwd, _neumann_bwd)


@jax.custom_vjp
def _known_inverse(m, t):
    return t


_known_inverse.defvjp(lambda m, t: (t, t), lambda t, g: (_neumann_bwd(0, t, g)[0], jnp.zeros_like(t)))


def _inverse(m, steps, kept):
    return _neumann_inverse(m, steps) if kept is None else _known_inverse(m, kept)


def cumsum_rows(x):
    t = x.shape[1]
    ii, jj = _tri(t)
    tri = jnp.broadcast_to((ii >= jj).astype(f32), (x.shape[0], t, t))
    return pdot(tri, x, "nn", P_CUMSUM)


def gdn_chunk(q, k, v, gb, bb, s, kept=None):
    c = q.shape[1]
    ii, jj = _tri(c)
    low = ii >= jj
    gcb = cumsum_rows(gb)
    gl = jnp.sum(gb, axis=1, keepdims=True)
    gc_col = gcb[:, :, :c]
    diff = gc_col - jnp.swapaxes(gc_col, 1, 2)
    decay = jnp.where(low, jnp.exp(jnp.where(low, diff, 0.0)), 0.0)
    qs = q * (q.shape[2] ** -0.5)
    kb = k * bb
    with_k = pdot(jnp.concatenate([kb, qs], axis=1), k, "nt", P_BULK)
    a = jnp.where(ii > jj, with_k[:, :c] * decay, 0.0)
    attn = with_k[:, c:] * decay
    t = _inverse(-a, c.bit_length() - 2, kept)
    eg = jnp.exp(gcb)
    uw = pdot(t, jnp.concatenate([v * bb, kb * eg], axis=2), "nn", P_BULK)
    u, w = uw[:, :, :LANES], uw[:, :, LANES:]
    kd = k * jnp.exp(gl - gcb)
    from_state = pdot(jnp.concatenate([w, qs * eg], axis=1), s, "nn", P_BULK)
    v_new = u - from_state[:, :c]
    o = from_state[:, c:] + pdot(attn, v_new, "nn", P_BULK)
    s_new = s * jnp.exp(gl) + pdot(kd, v_new, "tn", P_BULK)
    return o, s_new, t


def wkv_chunk(r, lw, k, v, kk, a, s, kept=None):
    t = r.shape[1]
    ii, jj = _tri(t)
    lo = lax.broadcasted_iota(jnp.int32, r.shape, 2) < RW_HEAD
    cl = cumsum_rows(lw)
    cl_last = jnp.sum(lw, axis=1, keepdims=True)
    al = -kk * jnp.exp(cl - lw)
    be = (a * kk) * jnp.exp(-cl)
    kt = k * jnp.exp(-cl)
    rt = r * jnp.exp(cl)

    def dot(xa, xb, mode="nn"):
        return pdot(xa, xb, mode, P_BULK)

    def sel(x_lo, x_hi):
        return jnp.where(lo, x_lo, x_hi)

    left = jnp.concatenate([jnp.where(lo, al, 0.0), jnp.where(lo, 0.0, al),
                            jnp.where(lo, rt, 0.0), jnp.where(lo, 0.0, rt)], axis=1)
    with_be, with_kt = dot(left, be, "nt"), dot(left, kt, "nt")

    def blocks(prod):
        below, upto = ii > jj, ii >= jj
        return [jnp.where(msk, prod[:, q * t:(q + 1) * t], 0.0) for q, msk in enumerate((below, below, upto, upto))]

    ab_lo, ab_hi, rb_lo, rb_hi = blocks(with_be)
    ak_lo, ak_hi, rk_lo, rk_hi = blocks(with_kt)
    from_state = dot(jnp.concatenate([al, rt], axis=1), s, "nt")
    x = from_state[:, :t] + sel(dot(ak_lo, v), dot(ak_hi, v))
    steps = t.bit_length() - 2
    inv_lo = _inverse(ab_lo, steps, None if kept is None else kept[:, :t])
    inv_hi = _inverse(ab_hi, steps, None if kept is None else kept[:, t:])
    u = sel(dot(inv_lo, x), dot(inv_hi, x))
    y = from_state[:, t:] + sel(dot(rb_lo, u) + dot(rk_lo, v), dot(rb_hi, u) + dot(rk_hi, v))
    vi = lax.broadcasted_iota(jnp.int32, s.shape, 1) < RW_HEAD
    ki = lax.broadcasted_iota(jnp.int32, s.shape, 2) < RW_HEAD
    s_new = jnp.where(vi == ki, (s + dot(u, be, "tn") + dot(v, kt, "tn")) * jnp.exp(cl_last), 0.0)
    return y, s_new, jnp.concatenate([inv_lo, inv_hi], axis=1)


def _scan_group(ncol, offs):
    g = SCAN_GROUP
    while g > 1 and (ncol % g or any(o % g for o in offs)):
        g //= 2
    return g


def scan_fwd(name, chunk_fn, ins, *, rows, chunk, ncol):
    n = rows // chunk
    n_in = len(ins)
    grp = _scan_group(ncol, [off for _, off in ins])
    tile = jax.ShapeDtypeStruct((grp, chunk, LANES), f32)
    kept = jax.eval_shape(chunk_fn, *[tile] * n_in, jax.ShapeDtypeStruct((grp, LANES, LANES), f32))[2].shape[1:]

    def body(*refs):
        o_ref, st_ref, kept_ref, s_scr = refs[n_in:]

        @pl.when(pl.program_id(1) == 0)
        def _():
            s_scr[...] = jnp.zeros_like(s_scr)

        cols = [slice(b * LANES, (b + 1) * LANES) for b in range(grp)]
        s = s_scr[...]
        st_ref[...] = s
        o, s_new, inv = chunk_fn(*[jnp.stack([r[:, c] for c in cols]) for r in refs[:n_in]], s)
        for b, c in enumerate(cols):
            o_ref[:, c] = o[b]
        kept_ref[...] = inv
        s_scr[...] = s_new

    def spec(off):
        return pl.BlockSpec((chunk, grp * LANES), lambda h, c: (c, h + off // grp))

    def per_chunk(shape):
        return pl.BlockSpec((grp, None) + tuple(shape), lambda h, c: (h, c, 0, 0))

    return pl.pallas_call(
        body, name=name, grid=(ncol // grp, n), in_specs=[spec(off) for _, off in ins],
        out_specs=[spec(0), per_chunk((LANES, LANES)), per_chunk(kept)],
        out_shape=[jax.ShapeDtypeStruct((rows, ncol * LANES), f32),
                   jax.ShapeDtypeStruct((ncol, n, LANES, LANES), f32),
                   jax.ShapeDtypeStruct((ncol, n) + tuple(kept), f32)],
        scratch_shapes=[pltpu.VMEM((grp, LANES, LANES), f32)], compiler_params=_params(2))(*[a for a, _ in ins])


def scan_bwd(name, chunk_fn, ins, states, kept, d_out, d_off, *, rows, chunk, ncol):
    n = rows // chunk
    n_in = len(ins)
    grp = _scan_group(ncol, [off for _, off in ins] + [d_off])

    def body(*refs):
        st_ref, kept_ref, do_ref = refs[n_in:n_in + 3]
        g_refs = refs[n_in + 3:2 * n_in + 3]
        ds_scr = refs[-1]

        @pl.when(pl.program_id(1) == 0)
        def _():
            ds_scr[...] = jnp.zeros_like(ds_scr)

        cols = [slice(b * LANES, (b + 1) * LANES) for b in range(grp)]

        def batch(ref):
            return jnp.stack([ref[:, c] for c in cols])

        inv = kept_ref[...]
        _, pull = jax.vjp(lambda *a: chunk_fn(*a, kept=inv)[:2], *[batch(r) for r in refs[:n_in]], st_ref[...])
        gs = pull((batch(do_ref), ds_scr[...]))
        for ref, g in zip(g_refs, gs[:n_in]):
            for b, c in enumerate(cols):
                ref[:, c] = g[b]
        ds_scr[...] = gs[n_in]

    def spec(off):
        return pl.BlockSpec((chunk, grp * LANES), lambda h, c: (n - 1 - c, h + off // grp))

    def per_chunk(shape):
        return pl.BlockSpec((grp, None) + tuple(shape), lambda h, c: (h, n - 1 - c, 0, 0))

    return pl.pallas_call(
        body, name=name, grid=(ncol // grp, n),
        in_specs=[spec(off) for _, off in ins] + [per_chunk(states.shape[2:]), per_chunk(kept.shape[2:]), spec(d_off)],
        out_specs=[spec(0)] * n_in, out_shape=[jax.ShapeDtypeStruct((rows, ncol * LANES), f32)] * n_in,
        scratch_shapes=[pltpu.VMEM((grp, LANES, LANES), f32)],
        compiler_params=_params(2))(*[a for a, _ in ins], states, kept, d_out)


def flip_exchange(name, arrs, flips, n_slots, slot_of, src_of, with_self, after=()):
    n = len(arrs)
    nf = len(flips)
    n_after = len(after)

    def body(*refs):
        ins, outs = refs[:n], refs[n + n_after:2 * n + n_after]
        send, recv, lsem = refs[2 * n + n_after:]
        me = (lax.axis_index("x"), lax.axis_index("y"), lax.axis_index("c"))
        copies = []
        for k in range(n):
            if with_self:
                cp = pltpu.make_async_copy(src_of(ins[k], me), outs[k].at[slot_of(me)], lsem.at[k])
                cp.start()
                copies.append(cp)
            for j, fl in enumerate(flips):
                peer = tuple(1 - m if f else m for m, f in zip(me, fl))
                cp = pltpu.make_async_remote_copy(
                    src_ref=src_of(ins[k], peer), dst_ref=outs[k].at[slot_of(me)], send_sem=send.at[k, j],
                    recv_sem=recv.at[k, j], device_id=peer, device_id_type=MESH)
                cp.start()
                copies.append(cp)
        for cp in copies:
            cp.wait()

    def out_sds(a):
        blk = src_of(jax.ShapeDtypeStruct(a.shape, a.dtype), None)
        return jax.ShapeDtypeStruct((n_slots,) + tuple(blk), a.dtype)

    any_spec = pl.BlockSpec(memory_space=pl.ANY)
    return pl.pallas_call(
        body, name=name, in_specs=[any_spec] * (n + n_after), out_specs=[any_spec] * n,
        out_shape=[out_sds(a) for a in arrs],
        scratch_shapes=[pltpu.SemaphoreType.DMA((n, nf)), pltpu.SemaphoreType.DMA((n, nf)),
                        pltpu.SemaphoreType.DMA((n,))],
        compiler_params=pltpu.CompilerParams(has_side_effects=True))(*arrs, *after)


_CHIP_FLIPS = ((1, 0, 0), (0, 1, 0), (1, 1, 0))
_ALL_FLIPS = ((0, 0, 1), (0, 1, 0), (0, 1, 1), (1, 0, 0), (1, 0, 1), (1, 1, 0), (1, 1, 1))


def _whole(ref, pos):
    return ref.shape if pos is None else ref


def _chip_block(ref, pos):
    return ref.shape[1:] if pos is None else ref.at[2 * pos[0] + pos[1]]


def _chip_slot(p):
    return 2 * p[0] + p[1]


_HBM = pl.BlockSpec(memory_space=pltpu.HBM)
_SEM = pl.BlockSpec(memory_space=pltpu.SEMAPHORE)
_DATAFLOW = pltpu.SideEffectType.DATAFLOW_SIDE_EFFECTING


def _split_copies(mode, refs, n, send, recv):
    me = (lax.axis_index("x"), lax.axis_index("y"), lax.axis_index("c"))
    sib = (me[0], me[1], 1 - me[2])
    lands = refs[:n] if mode == "handover" else refs[n:2 * n]
    copies = []
    for k, land in enumerate(lands):
        half = land.shape[1] // 2
        mine = pl.ds(pl.multiple_of(me[2] * half, 16), half)
        for j, fl in enumerate(_CHIP_FLIPS):
            peer = tuple(1 - m if f else m for m, f in zip(me, fl))
            if mode == "gather":
                src, dst, to = refs[k], land.at[_chip_slot(me)], peer
            elif mode == "scatter":
                src, dst, to = refs[k].at[_chip_slot(peer)], land.at[_chip_slot(me)], peer
            elif mode == "gather_half":
                src, dst, to = refs[k].at[mine], land.at[_chip_slot(me), mine], peer
            else:
                src = dst = land.at[_chip_slot(peer), mine]
                to = sib
            q = k * len(_CHIP_FLIPS) + j
            copies.append(pltpu.make_async_remote_copy(src_ref=src, dst_ref=dst, send_sem=send.at[q],
                                                       recv_sem=recv.at[q], device_id=to, device_id_type=MESH))
    return copies


def split_start(name, mode, ops, n):
    ops = [pltpu.with_memory_space_constraint(a, pltpu.HBM) for a in ops]
    m = len(ops)

    def body(*refs):
        for cp in _split_copies(mode, refs[:m], n, refs[m], refs[m + 1]):
            cp.start()
        refs[-1][...] = jnp.zeros_like(refs[-1])

    sems = pltpu.SemaphoreType.DMA((n * len(_CHIP_FLIPS),))
    outs = pl.pallas_call(
        body, name=name, in_specs=[_HBM] * m,
        out_shape=(sems, sems, *[pltpu.HBM(a.shape, a.dtype) for a in ops], jax.ShapeDtypeStruct((8, LANES), f32)),
        out_specs=(_SEM, _SEM, *[_HBM] * m, pl.BlockSpec(memory_space=pltpu.VMEM)),
        input_output_aliases={i: 2 + i for i in range(m)},
        compiler_params=pltpu.CompilerParams(has_side_effects=_DATAFLOW))(*ops)
    return (outs[0], outs[1], list(outs[2:2 + m]), mode, n), outs[-1][0, 0]


def split_wait(name, state, after, with_sources=False):
    send, recv, ops, mode, n = state
    m = len(ops)
    afters = list(after) if isinstance(after, (list, tuple)) else [after]

    def body(*refs):
        for cp in _split_copies(mode, refs[:m], n, refs[m], refs[m + 1]):
            cp.wait_send()
            cp.wait_recv()

    outs = pl.pallas_call(
        body, name=name, in_specs=[_HBM] * m + [_SEM, _SEM] + [pl.BlockSpec(memory_space=pl.ANY)] * len(afters),
        out_shape=tuple(pltpu.HBM(a.shape, a.dtype) for a in ops), out_specs=tuple([_HBM] * m),
        input_output_aliases={i: i for i in range(m)},
        compiler_params=pltpu.CompilerParams(has_side_effects=_DATAFLOW))(*ops, send, recv, *afters)
    return (list(outs[m - n:]), list(outs[:m - n])) if with_sources else list(outs[m - n:])


def chips_start(name, arrs, src_of, halves=False):
    me = _chip_slot((lax.axis_index("x"), lax.axis_index("y")))
    lands = []
    for a in arrs:
        blk = tuple(src_of(jax.ShapeDtypeStruct(a.shape, a.dtype), None))
        land = lax.empty((4,) + blk, a.dtype)
        lands.append(lax.dynamic_update_index_in_dim(land, a, me, 0) if src_of is _whole else land)
    mode = "scatter" if src_of is _chip_block else ("gather_half" if halves else "gather")
    return split_start(name, mode, list(arrs) + lands, len(arrs))


chips_wait = split_wait


def handover_start(name, lands):
    return split_start(name, "handover", lands, len(lands))


def gather_chips_halves(name, arrs):
    n = len(arrs)
    nf = len(_CHIP_FLIPS)
    split = [a.shape[0] % 32 == 0 for a in arrs]
    parts = [GATHER_PARTS if s and a.shape[0] % (32 * GATHER_PARTS) == 0 and a.size >= (1 << 18) else 1
             for a, s in zip(arrs, split)]

    def body(*refs):
        ins, outs = refs[:n], refs[n:2 * n]
        send1, recv1, send2, recv2, lsem = refs[2 * n:]
        me = (lax.axis_index("x"), lax.axis_index("y"), lax.axis_index("c"))
        sib = (me[0], me[1], 1 - me[2])
        peers = [tuple(1 - m if f else m for m, f in zip(me, fl)) for fl in _CHIP_FLIPS]
        local, first, second = [], [], []
        for k in range(n):
            cp = pltpu.make_async_copy(ins[k], outs[k].at[_chip_slot(me)], lsem.at[k])
            cp.start()
            local.append(cp)
        for q in range(GATHER_PARTS):
            for k in range(n):
                if q >= parts[k]:
                    continue
                piece = ins[k].shape[0] // 2 // parts[k]
                rows = (pl.ds(pl.multiple_of(me[2] * (piece * parts[k]) + q * piece, 16), piece) if split[k]
                        else pl.ds(0, ins[k].shape[0]))
                for j, peer in enumerate(peers):
                    sem = j * GATHER_PARTS + q
                    cp = pltpu.make_async_remote_copy(
                        src_ref=ins[k].at[rows], dst_ref=outs[k].at[_chip_slot(me), rows], send_sem=send1.at[k, sem],
                        recv_sem=recv1.at[k, sem], device_id=peer, device_id_type=MESH)
                    cp.start()
                    first.append((k, j, sem, rows, cp))
        for k, j, sem, rows, cp in first:
            cp.wait_recv()
            if split[k]:
                got = outs[k].at[_chip_slot(peers[j]), rows]
                fwd = pltpu.make_async_remote_copy(src_ref=got, dst_ref=got, send_sem=send2.at[k, sem],
                                                   recv_sem=recv2.at[k, sem], device_id=sib, device_id_type=MESH)
                fwd.start()
                second.append(fwd)
        for _, _, _, _, cp in first:
            cp.wait_send()
        for cp in second:
            cp.wait()
        for cp in local:
            cp.wait()

    any_spec = pl.BlockSpec(memory_space=pl.ANY)
    sems = pltpu.SemaphoreType.DMA((n, nf * GATHER_PARTS))
    return pl.pallas_call(
        body, name=name, in_specs=[any_spec] * n, out_specs=[any_spec] * n,
        out_shape=[jax.ShapeDtypeStruct((4,) + a.shape, a.dtype) for a in arrs],
        scratch_shapes=[sems, sems, sems, sems, pltpu.SemaphoreType.DMA((n,))],
        compiler_params=pltpu.CompilerParams(has_side_effects=True))(*arrs)


def swap_sibling(name, arrs):
    outs = flip_exchange(name, arrs, ((0, 0, 1),), 1, lambda p: 0, _whole, False)
    return [o[0] for o in outs]


def gather_all(arrs, after=()):
    return flip_exchange("gather_all", arrs, _ALL_FLIPS, 8, lambda p: 4 * p[0] + 2 * p[1] + p[2], _whole, True,
                         after=after)


def _row_tile(nr, nc, n_arrays):
    budget = (20 << 20) // (n_arrays * 2 * 4 * max(nc, LANES))
    t = min(nr, budget) // 16 * 16
    while t > 0 and nr % t:
        t -= 16
    return t if t > 0 else nr


def sum_slots(name, x, own):
    ns, nr, nc = x.shape
    tile = _row_tile(nr, nc, ns + 1)
    me = jnp.reshape(_chip_slot((lax.axis_index("x"), lax.axis_index("y"))), (1,)).astype(jnp.int32)

    def body(me_ref, own_ref, *refs):
        o_ref = refs[-1]
        s = own_ref[...].astype(f32)
        for r in refs[:-1]:
            s = s + r[...].astype(f32)
        o_ref[...] = s.astype(o_ref.dtype)

    def slot(q):
        return pl.BlockSpec((None, tile, nc), lambda i, t: ((t[0] + q) % ns, i, 0))

    return pl.pallas_call(
        body, name=name, out_shape=jax.ShapeDtypeStruct((nr, nc), x.dtype),
        grid_spec=pltpu.PrefetchScalarGridSpec(
            num_scalar_prefetch=1, grid=(nr // tile,), in_specs=[slot(q) for q in range(ns)],
            out_specs=pl.BlockSpec((tile, nc), lambda i, t: (i, 0))),
        compiler_params=_params(1))(me, own, *[x] * (ns - 1))


def adamw(name, w, g_parts, m, v):
    nr, nc = w.shape[-2:]
    n_g = len(g_parts)
    tile = _row_tile(nr, nc, 7 + n_g)

    def body(*refs):
        w_ref, m_ref, v_ref = refs[:3]
        g = refs[3][...].astype(f32)
        for r in refs[4:3 + n_g]:
            g = g + r[...].astype(f32)
        g_ref, d_ref, nm_ref, nv_ref = refs[3 + n_g:]
        nm = ADAM_B1 * m_ref[...] + (1.0 - ADAM_B1) * g
        nv = ADAM_B2 * v_ref[...] + (1.0 - ADAM_B2) * jnp.square(g)
        m_hat = nm / (1.0 - ADAM_B1 ** ADAM_STEP)
        v_hat = nv / (1.0 - ADAM_B2 ** ADAM_STEP)
        g_ref[...] = g
        d_ref[...] = -ADAM_LR * (m_hat / (jnp.sqrt(v_hat) + ADAM_EPS) + ADAM_WD * w_ref[...])
        nm_ref[...] = nm
        nv_ref[...] = nv

    spec = pl.BlockSpec((tile, nc), lambda i: (i, 0))
    spec3 = pl.BlockSpec((None, tile, nc), lambda i: (0, i, 0)) if w.ndim == 3 else spec
    return pl.pallas_call(
        body, name=name, grid=(nr // tile,), in_specs=[spec3] * 3 + [spec] * n_g, out_specs=[spec3] * 4,
        out_shape=[jax.ShapeDtypeStruct(w.shape, f32)] * 4, compiler_params=_params(1))(w, m, v, *g_parts)


def adamw_packed(w, g8, m, v):
    nr, nc = w.shape

    def body(w_ref, g_ref, m_ref, v_ref, go_ref, d_ref, nm_ref, nv_ref):
        g = g_ref[0]
        for q in range(1, 8):
            g = g + g_ref[q]
        nm = ADAM_B1 * m_ref[...] + (1.0 - ADAM_B1) * g
        nv = ADAM_B2 * v_ref[...] + (1.0 - ADAM_B2) * jnp.square(g)
        m_hat = nm / (1.0 - ADAM_B1 ** ADAM_STEP)
        v_hat = nv / (1.0 - ADAM_B2 ** ADAM_STEP)
        go_ref[...] = g
        d_ref[...] = -ADAM_LR * (m_hat / (jnp.sqrt(v_hat) + ADAM_EPS) + ADAM_WD * w_ref[...])
        nm_ref[...] = nm
        nv_ref[...] = nv

    return pl.pallas_call(body, name="adamw_packed", out_shape=[jax.ShapeDtypeStruct((nr, nc), f32)] * 4,
                          compiler_params=pltpu.CompilerParams(vmem_limit_bytes=VMEM_LIMIT))(w, g8, m, v)


def _pack(vectors):
    rows = []
    for a in vectors:
        flat = a.reshape(-1).astype(f32)
        pad = (-flat.shape[0]) % LANES
        rows.append(jnp.pad(flat, (0, pad)).reshape(-1, LANES))
    packed = jnp.concatenate(rows, axis=0)
    return jnp.pad(packed, ((0, (-packed.shape[0]) % 8), (0, 0)))


def _unpack(packed, like):
    out, r = [], 0
    for a in like:
        n = a.size
        nr = -(-n // LANES)
        out.append(packed[r:r + nr].reshape(-1)[:n].reshape(a.shape))
        r += nr
    return out


def kernel(x, mem, mix_norm_w, w_in, dn_conv_w, dn_a_log, dn_dt_bias, dn_norm_w, rw_mu, rw_w0, rw_w2, rw_a0, rw_a2, rw_g2, rw_k_k, rw_k_a, rw_r_k, rw_ln_w, rw_ln_b, w_out, xa_norm_w, mem_norm_w, xa_wq, xa_wk, xa_wv, xa_wo, ffn_norm_w, ffn_w1, ffn_w2, final_norm_w, loss_target, m_mix_norm_w, m_w_in, m_dn_conv_w, m_dn_a_log, m_dn_dt_bias, m_dn_norm_w, m_rw_mu, m_rw_w0, m_rw_w2, m_rw_a0, m_rw_a2, m_rw_g2, m_rw_k_k, m_rw_k_a, m_rw_r_k, m_rw_ln_w, m_rw_ln_b, m_w_out, m_xa_norm_w, m_mem_norm_w, m_xa_wq, m_xa_wk, m_xa_wv, m_xa_wo, m_ffn_norm_w, m_ffn_w1, m_ffn_w2, m_final_norm_w, v_mix_norm_w, v_w_in, v_dn_conv_w, v_dn_a_log, v_dn_dt_bias, v_dn_norm_w, v_rw_mu, v_rw_w0, v_rw_w2, v_rw_a0, v_rw_a2, v_rw_g2, v_rw_k_k, v_rw_k_a, v_rw_r_k, v_rw_ln_w, v_rw_ln_b, v_w_out, v_xa_norm_w, v_mem_norm_w, v_xa_wq, v_xa_wk, v_xa_wv, v_xa_wo, v_ffn_norm_w, v_ffn_w1, v_ffn_w2, v_final_norm_w):
    weights = dict(mix_norm_w=mix_norm_w, w_in=w_in, dn_conv_w=dn_conv_w, dn_a_log=dn_a_log, dn_dt_bias=dn_dt_bias, dn_norm_w=dn_norm_w, rw_mu=rw_mu, rw_w0=rw_w0, rw_w2=rw_w2, rw_a0=rw_a0, rw_a2=rw_a2, rw_g2=rw_g2, rw_k_k=rw_k_k, rw_k_a=rw_k_a, rw_r_k=rw_r_k, rw_ln_w=rw_ln_w, rw_ln_b=rw_ln_b, w_out=w_out, xa_norm_w=xa_norm_w, mem_norm_w=mem_norm_w, xa_wq=xa_wq, xa_wk=xa_wk, xa_wv=xa_wv, xa_wo=xa_wo, ffn_norm_w=ffn_norm_w, ffn_w1=ffn_w1, ffn_w2=ffn_w2, final_norm_w=final_norm_w)
    mom_m = dict(mix_norm_w=m_mix_norm_w, w_in=m_w_in, dn_conv_w=m_dn_conv_w, dn_a_log=m_dn_a_log, dn_dt_bias=m_dn_dt_bias, dn_norm_w=m_dn_norm_w, rw_mu=m_rw_mu, rw_w0=m_rw_w0, rw_w2=m_rw_w2, rw_a0=m_rw_a0, rw_a2=m_rw_a2, rw_g2=m_rw_g2, rw_k_k=m_rw_k_k, rw_k_a=m_rw_k_a, rw_r_k=m_rw_r_k, rw_ln_w=m_rw_ln_w, rw_ln_b=m_rw_ln_b, w_out=m_w_out, xa_norm_w=m_xa_norm_w, mem_norm_w=m_mem_norm_w, xa_wq=m_xa_wq, xa_wk=m_xa_wk, xa_wv=m_xa_wv, xa_wo=m_xa_wo, ffn_norm_w=m_ffn_norm_w, ffn_w1=m_ffn_w1, ffn_w2=m_ffn_w2, final_norm_w=m_final_norm_w)
    mom_v = dict(mix_norm_w=v_mix_norm_w, w_in=v_w_in, dn_conv_w=v_dn_conv_w, dn_a_log=v_dn_a_log, dn_dt_bias=v_dn_dt_bias, dn_norm_w=v_dn_norm_w, rw_mu=v_rw_mu, rw_w0=v_rw_w0, rw_w2=v_rw_w2, rw_a0=v_rw_a0, rw_a2=v_rw_a2, rw_g2=v_rw_g2, rw_k_k=v_rw_k_k, rw_k_a=v_rw_k_a, rw_r_k=v_rw_r_k, rw_ln_w=v_rw_ln_w, rw_ln_b=v_rw_ln_b, w_out=v_w_out, xa_norm_w=v_xa_norm_w, mem_norm_w=v_mem_norm_w, xa_wq=v_xa_wq, xa_wk=v_xa_wk, xa_wv=v_xa_wv, xa_wo=v_xa_wo, ffn_norm_w=v_ffn_norm_w, ffn_w1=v_ffn_w1, ffn_w2=v_ffn_w2, final_norm_w=v_final_norm_w)
    names = list(weights)

    seq, d = x.shape[1], x.shape[2]
    dnw = d // 2
    rww = d - dnw
    nh, nb = dnw // LANES, rww // LANES
    n_mem = mem.shape[1]
    lw_dim, la_dim, lg_dim = rw_w2.shape[1], rw_a2.shape[1], rw_g2.shape[1]
    assert lw_dim + la_dim == LANES and lg_dim == LANES and dnw % LANES == 0 and rww % LANES == 0
    xs, mems, tgt = x[0], mem[0], loss_target[0]

    col_sharded = ("w_in", "xa_wo", "ffn_w1", "dn_conv_w", "rw_w2", "rw_a2", "rw_g2")
    row_sharded = ("w_out", "xa_wq", "xa_wk", "xa_wv", "ffn_w2")
    f32_payload = ("dn_conv_w", "rw_w2", "rw_a2", "rw_g2")
    sharded = col_sharded + row_sharded
    payload = {n: weights[n][0].astype(f32 if n in f32_payload else bf16) for n in sharded}
    shard_w = w_in.shape[2]
    pad_w = -(-shard_w // LANES) * LANES
    first = ("w_in", "dn_conv_w", "rw_w2", "rw_a2", "rw_g2")
    mid = ("w_out", "xa_wq", "xa_wk", "xa_wv", "xa_wo")
    late = ("ffn_w1", "ffn_w2")
    gathered = dict(zip(first, gather_chips_halves("gather_first", [payload[n] for n in first])))
    ordered = lax.optimization_barrier(([gathered[n] for n in first], [payload[n] for n in mid + late]))
    gathered = dict(zip(first, ordered[0]))
    payload.update(zip(mid + late, ordered[1]))
    mid_state, tok_mid = chips_start("gather_mid_start", [payload[n] for n in mid], _whole, halves=True)
    late_state, tok_late = chips_start("gather_late_start", [payload[n] for n in late], _whole, halves=True)
    mix_norm_w_in = mix_norm_w + (tok_mid + tok_late)

    def full(n):
        g = gathered[n]
        if n in col_sharded:
            return g.transpose(1, 0, 2).reshape(g.shape[1], 4 * g.shape[2])
        return g.reshape(4 * g.shape[1], g.shape[2])

    c_rw0 = 4 * dnw + 2 * nh
    c_rw = 3 * rww
    eb_ab, eb_r = 4 * nh, 4 * nh + 1
    eb_l1 = eb_r + 3 * nb
    n_ext = -(-(eb_l1 + 2) // 4) * 4
    tbl_fwd, tbl_bwd, per_shard = w_in_layout(4 * shard_w, shard_w, c_rw0, LANES - 2 * nh, n_ext)
    sub_f = max(s for s in range(1, REGROUP_SUB + 1) if n_ext % s == 0)
    sub_b = max(s for s in range(1, REGROUP_SUB + 1) if per_shard % s == 0)
    w_ext = lane_regroup(
        "w_in_regroup", gathered["w_in"], tbl_fwd,
        lambda p, b: pl.BlockSpec(
            (None, d, LANES), lambda i, j, t: (t[REGROUP_FIELDS * p, j * sub_f + b], 0,
                                               t[REGROUP_FIELDS * p + 1, j * sub_f + b])),
        pl.BlockSpec((d, sub_f * LANES), lambda i, j, t: (0, j)), (d, n_ext * LANES), (1, n_ext // sub_f), shard_w,
        sub_f)
    conv_w = full("dn_conv_w")
    w2p = jnp.concatenate([full("rw_w2"), jnp.zeros((la_dim, rww), f32)], axis=0)
    a2p = jnp.concatenate([jnp.zeros((lw_dim, rww), f32), full("rw_a2")], axis=0)
    g2 = full("rw_g2")
    xaw = xa_wq.shape[2]
    nxh = xaw // LANES
    ffn = 4 * ffn_w1.shape[2]

    def lane_row(vec):
        return jnp.pad(vec.reshape(1, -1), ((0, 0), (0, LANES - vec.size)))

    alog_row, dtb_row = lane_row(dn_a_log), lane_row(dn_dt_bias)
    head_of_col = jnp.arange(dnw)[None, :] // LANES
    e_g = (jnp.arange(LANES)[:, None] == head_of_col).astype(f32)
    e_b = (jnp.arange(LANES)[:, None] == head_of_col + nh).astype(f32)
    mu_main, mu_small = rw_mu[:, :c_rw], rw_mu[:, c_rw:]
    r_k_row = rw_r_k.reshape(1, rww)
    fnw = final_norm_w.reshape(1, d)
    qb, kb_, vb, zb = 0, nh, 2 * nh, 3 * nh
    rb0 = eb_r
    cc = lambda w_, o_: ("constc", w_, o_)
    rc = lambda o_: ("rowc", LANES, o_)

    (u16,) = rowcall("norm_mix", fn_rms, [(xs, "row"), (mix_norm_w_in, "const")], [(seq, d, bf16, "row")], rows=seq)
    p_main = p_small = mm("in_proj", u16, w_ext)

    fn_gconv = make_fn_gconv(2 * nh)
    (qkv,) = rowcall("gdn_conv", fn_gconv, [(p_main, rc(0)), (conv_w, cc(LANES, 0))],
                     [(seq, 3 * dnw, f32, "rowc")], rows=seq, tile=seq, ncol=3 * nh)
    gate_ins = [(p_small, rc(eb_ab)), (alog_row, "const"), (dtb_row, "const"), (e_g, "const"), (e_b, "const")]
    g_b, beta_b = rowcall("gdn_gate", fn_ggate, gate_ins, [(seq, dnw, f32, "row")] * 2, rows=seq)
    gdn_ins = [(qkv, qb), (qkv, kb_), (qkv, vb), (g_b, 0), (beta_b, 0)]
    o_raw, gdn_states, gdn_kept = scan_fwd("gdn_scan", gdn_chunk, gdn_ins, rows=seq, chunk=GDN_CHUNK, ncol=nh)
    mid_state, tok = handover_start("gather_mid_pass", chips_wait("gather_mid_wait", mid_state, o_raw))
    gpost_ins = [(o_raw, rc(0)), (p_main, rc(zb)), (dn_norm_w + tok, "const")]
    (o_dn,) = rowcall("gdn_post", fn_gpost, gpost_ins, [(seq, dnw, bf16, "rowc")], rows=seq, ncol=nh)

    (prw,) = rowcall("rw_lerp_main", fn_lerp, [(p_main, rc(rb0)), (mu_main, cc(LANES, 0))],
                     [(seq, c_rw, f32, "rowc")], rows=seq, tile=seq, ncol=3 * nb)
    (psl,) = rowcall("rw_lerp_small", fn_lerp, [(p_small, rc(eb_l1)), (mu_small, cc(LANES, 0))],
                     [(seq, 2 * LANES, f32, "rowc")], rows=seq, tile=seq, ncol=2)
    rprep_ins = [(prw, rc(nb)), (psl, "row"), (rw_w0, cc(LANES, 0)), (rw_a0, cc(LANES, 0)), (rw_k_k, cc(LANES, 0)),
                 (rw_k_a, cc(LANES, 0)), (w2p, cc(LANES, 0)), (a2p, cc(LANES, 0)), (g2, cc(LANES, 0))]
    lw, kmod, kk, a_rw, gate = rowcall("rw_prep", fn_rprep, rprep_ins, [(seq, rww, f32, "rowc")] * 5,
                                        rows=seq, ncol=nb)
    wkv_ins = [(prw, 0), (lw, 0), (kmod, 0), (prw, 2 * nb), (kk, 0), (a_rw, 0)]
    y_rw, wkv_states, wkv_kept = scan_fwd("wkv_scan", wkv_chunk, wkv_ins, rows=seq, chunk=WKV_CHUNK, ncol=nb)
    rpost_ins = [(y_rw, rc(0)), (prw, rc(0)), (kmod, rc(0)), (prw, rc(2 * nb)), (gate, rc(0)),
                 (r_k_row, cc(LANES, 0)), (rw_ln_w, cc(LANES, 0)), (rw_ln_b, cc(LANES, 0))]
    (o_rw,) = rowcall("rw_post", fn_rpost, rpost_ins, [(seq, rww, bf16, "rowc")], rows=seq, ncol=nb)

    o_cat = jnp.concatenate([o_dn, o_rw], axis=1)
    late_state, tok = handover_start("gather_late_pass", chips_wait("gather_late_wait", late_state, o_cat))
    gathered.update(zip(mid, chips_wait("gather_mid_pass_wait", mid_state, o_cat)))
    w_out_f, wq_f, wk_f, wv_f, wo_f = full("w_out"), full("xa_wq"), full("xa_wk"), full("xa_wv"), full("xa_wo")
    h1 = mm("out_proj", o_cat, w_out_f, add=xs)

    (hn16,) = rowcall("norm_xa", fn_rms, [(h1, "row"), (xa_norm_w + tok, "const")], [(seq, d, bf16, "row")],
                      rows=seq)
    (mn16,) = rowcall("norm_mem", fn_rms, [(mems, "row"), (mem_norm_w, "const")], [(n_mem, d, bf16, "row")],
                      rows=n_mem)
    q_xa = mm("xa_q", hn16, wq_f)
    k_xa = mm("xa_k", mn16, wk_f)
    v_xa = mm("xa_v", mn16, wv_f)
    xcore_ins = [(q_xa, rc(0)), (k_xa, cc(LANES, 0)), (v_xa, cc(LANES, 0))]
    (o_xa,) = rowcall("xa_core", fn_xcore, xcore_ins, [(seq, xaw, bf16, "rowc")], rows=seq, ncol=nxh)
    h2 = mm("xa_o", o_xa, wo_f, add=h1)

    (fn16,) = rowcall("norm_ffn", fn_rms, [(h2, "row"), (ffn_norm_w, "const")], [(seq, d, bf16, "row")], rows=seq)
    gathered.update(zip(late, chips_wait("gather_late_pass_wait", late_state, fn16)))
    w1_f, w2_f = full("ffn_w1"), full("ffn_w2")
    a1_16, hid16 = mm("ffn_up", fn16, w1_f, epilogue=lambda r: (r, jnp.square(jnp.maximum(r, 0.0))),
                      out_dtypes=(bf16, bf16))
    h3 = mm("ffn_down", hid16, w2_f, add=h2)

    dh3, dh3_16, d_fnw, loss_rows = rowcall(
        "loss_head", fn_final, [(h3, "row"), (tgt, "row"), (fnw, "const")],
        [(seq, d, f32, "row"), (seq, d, bf16, "row"), (1, d, f32, "acc"), (8, LANES, f32, "acc")], rows=seq)

    da1_16 = mm("ffn_down_dx", dh3_16, w2_f, tb=True, extra=[a1_16], out_dtype=bf16,
                epilogue=lambda r, a1: (r * (2.0 * jnp.maximum(a1.astype(f32), 0.0)),))
    def by_chip(n, g):
        if g.ndim == 3:
            return g
        if n in col_sharded:
            return g.reshape(g.shape[0], 4, g.shape[1] // 4).transpose(1, 0, 2)
        return g.reshape(4, g.shape[0] // 4, g.shape[1])

    g_ffn_w2 = mm("ffn_down_dw", hid16, dh3_16, ta=True, out_dtype=bf16)
    ffn_w2_g, tok = chips_start("scatter_ffn_w2_start", [by_chip("ffn_w2", g_ffn_w2)], _chip_block)
    g_ffn_w1 = mm("ffn_up_dw", fn16, da1_16, ta=True, out_dtype=bf16, by_chip=True, after=tok)
    ffn_w1_g, tok = chips_start("scatter_ffn_w1_start", [g_ffn_w1], _chip_block)
    dfn = mm("ffn_up_dx", da1_16, w1_f, tb=True, after=tok)
    dh2, d_ffn_nw, dh2_16 = rowvjp("norm_ffn_bwd", fn_rms, [(h2, "row"), (ffn_norm_w, "const")],
                                   [[(dfn, "row")]], [0, 1], rows=seq, adds=[(0, dh3, "row")], dup16=[0])

    do_xa = mm("xa_o_dx", dh2_16, wo_f, tb=True)
    g_xa_wo = mm("xa_o_dw", o_xa, dh2_16, ta=True, out_dtype=bf16, by_chip=True)
    dq_xa, dk_xa, dv_xa, dq16 = rowvjp("xa_core_bwd", fn_xcore, xcore_ins, [[(do_xa, rc(0))]], [0, 1, 2],
                                       rows=seq, ncol=nxh, dup16=[0])
    g_xa_wq = mm("xa_q_dw", hn16, dq16, ta=True, out_dtype=bf16)
    dhn = mm("xa_q_dx", dq16, wq_f, tb=True)
    dh1, d_xa_nw, dh1_16 = rowvjp("norm_xa_bwd", fn_rms, [(h1, "row"), (xa_norm_w, "const")], [[(dhn, "row")]],
                                  [0, 1], rows=seq, adds=[(0, dh2, "row")], dup16=[0])
    dk16, dv16 = dk_xa.astype(bf16), dv_xa.astype(bf16)
    g_xa_wk = mm("xa_k_dw", mn16, dk16, ta=True, out_dtype=bf16)
    g_xa_wv = mm("xa_v_dw", mn16, dv16, ta=True, out_dtype=bf16)
    dmn = mm("xa_v_dx", dv16, wv_f, tb=True, add=mm("xa_k_dx", dk16, wk_f, tb=True))
    (d_mem_nw,) = rowvjp("norm_mem_bwd", fn_rms, [(mems, "row"), (mem_norm_w, "const")], [[(dmn, "row")]], [1],
                         rows=n_mem)

    g_w_out = mm("out_proj_dw", o_cat, dh1_16, ta=True, out_dtype=bf16)
    mid_grads = dict(w_out=g_w_out, xa_wq=g_xa_wq, xa_wk=g_xa_wk, xa_wv=g_xa_wv, xa_wo=g_xa_wo)
    mid_g, tok = chips_start("scatter_mid_start", [by_chip(n, mid_grads[n]) for n in mid], _chip_block)
    do_cat = mm("out_proj_dx", dh1_16, w_out_f, tb=True, after=tok)

    dy, dr_a, dkmod_a, dv_a, dgate, d_r_k, d_ln_w, d_ln_b = rowvjp(
        "rw_post_bwd", fn_rpost, rpost_ins, [[(do_cat, rc(nh))]], [0, 1, 2, 3, 4, 5, 6, 7], rows=seq, ncol=nb)
    dr_b, dlw, dkmod_b, dv_b, dkk, da_rw = scan_bwd("wkv_scan_bwd", wkv_chunk, wkv_ins, wkv_states, wkv_kept, dy, 0,
                                                    rows=seq, chunk=WKV_CHUNK, ncol=nb)
    rprep_cts = [[(dlw, rc(0))], [(dkmod_a, rc(0)), (dkmod_b, rc(0))], [(dkk, rc(0))], [(da_rw, rc(0))],
                 [(dgate, rc(0))]]
    dpk, dpsl_parts, d_w0, d_a0, d_k_k, d_k_a, d_w2p, d_a2p, d_g2 = rowvjp(
        "rw_prep_bwd", fn_rprep, rprep_ins, rprep_cts, [0, 1, 2, 3, 4, 5, 6, 7, 8], rows=seq, ncol=nb)
    (dpsl,) = rowcall("rw_prep_sum", fn_sumcols(nb), [(dpsl_parts, "row")], [(seq, 2 * LANES, f32, "row")], rows=seq)

    def lerp_bwd(tag, p, p_off, mu, mu_off, ct_lists, ncol):
        return rowvjp("rw_lerp_bwd_" + tag, fn_lerp, [(p, rc(p_off)), (mu, cc(LANES, mu_off))], [ct_lists], [0, 1],
                      rows=seq, tile=seq, ncol=ncol, dup16=[0])

    _, dmu_r, dpr16 = lerp_bwd("r", p_main, rb0, mu_main, 0, [(dr_a, rc(0)), (dr_b, rc(0))], nb)
    _, dmu_k, dpk16 = lerp_bwd("k", p_main, rb0 + nb, mu_main, nb, [(dpk, rc(0))], nb)
    _, dmu_v, dpv16 = lerp_bwd("v", p_main, rb0 + 2 * nb, mu_main, 2 * nb, [(dv_a, rc(0)), (dv_b, rc(0))], nb)
    _, dmu_s, dps12_16 = lerp_bwd("small", p_small, eb_l1, mu_small, 0, [(dpsl, rc(0))], 2)

    do_raw, dz, d_dn_nw, dz16 = rowvjp("gdn_post_bwd", fn_gpost, gpost_ins, [[(do_cat, rc(0))]], [0, 1, 2],
                                       rows=seq, ncol=nh, dup16=[1])
    dq_g, dk_g, dv_g, dg_b, dbeta_b = scan_bwd("gdn_scan_bwd", gdn_chunk, gdn_ins, gdn_states, gdn_kept, do_raw, 0,
                                               rows=seq, chunk=GDN_CHUNK, ncol=nh)
    dps0, d_alog, d_dtb, dps0_16 = rowvjp("gdn_gate_bwd", fn_ggate, gate_ins, [[(dg_b, "row")], [(dbeta_b, "row")]],
                                          [0, 1, 2], rows=seq, dup16=[0])
    dqkv = jnp.concatenate([dq_g, dk_g, dv_g], axis=1)
    _, d_conv_w, dqkv16 = rowvjp("gdn_conv_bwd", fn_gconv, [(p_main, rc(0)), (conv_w, cc(LANES, 0))],
                                 [[(dqkv, rc(0))]], [0, 1], rows=seq, tile=seq, ncol=3 * nh, dup16=[0])

    dp16 = jnp.concatenate([dqkv16, dz16, dps0_16, dpr16, dpk16, dpv16, dps12_16,
                            jnp.zeros((seq, (n_ext - eb_l1 - 2) * LANES), bf16)], axis=1)
    g_w_ext = mm("in_proj_dw", u16, dp16, ta=True, out_dtype=bf16)
    g_w_in = lane_regroup(
        "w_in_grad_regroup", g_w_ext, tbl_bwd,
        lambda p, b: pl.BlockSpec((d, LANES), lambda i, j, t: (0, t[REGROUP_FIELDS * p + 1, j * sub_b + b])),
        pl.BlockSpec((None, d, sub_b * LANES),
                     lambda i, j, t: (j // (per_shard // sub_b), 0, j % (per_shard // sub_b))),
        (4, d, pad_w), (1, 4 * per_shard // sub_b), n_ext * LANES, sub_b)
    first_grads = dict(w_in=g_w_in, dn_conv_w=d_conv_w, rw_w2=d_w2p[:lw_dim], rw_a2=d_a2p[lw_dim:], rw_g2=d_g2)
    first_g, tok = chips_start("scatter_first_start", [by_chip(n, first_grads[n]) for n in first], _chip_block)

    du = mm("in_proj_dx", dp16, w_ext, tb=True, after=tok)
    grad_x, d_mix_nw = rowvjp("norm_mix_bwd", fn_rms, [(xs, "row"), (mix_norm_w, "const")], [[(du, "row")]],
                              [0, 1], rows=seq, adds=[(0, dh1, "row")])
    received, sent = {}, {}

    def arrived(group, name, state, after):
        lands, sources = chips_wait(name, state, after, with_sources=True)
        received.update(zip(group, lands))
        sent.update(zip(group, sources))

    arrived(mid, "scatter_mid_wait", mid_g, grad_x)
    arrived(("ffn_w2",), "scatter_ffn_w2_wait", ffn_w2_g, grad_x)
    arrived(("ffn_w1",), "scatter_ffn_w1_wait", ffn_w1_g, grad_x)

    out = {}

    def reduce_and_update(tag, group):
        partial = [sum_slots("sum_chips_" + n, received[n], sent[n]) for n in group]
        other = swap_sibling("swap_sibling_" + tag, partial)
        for n, p_mine, p_other in zip(group, partial, other):
            if weights[n].shape[2] % LANES:
                rows, cols = weights[n].shape[1:]
                lin = lambda a: jnp.swapaxes(a, 1, 2).reshape(-1, LANES)
                lin_g = lambda p: p.T[:cols].reshape(-1, LANES)
                res = adamw("adamw_" + n, lin(weights[n]), [lin_g(p_mine), lin_g(p_other)], lin(mom_m[n]),
                            lin(mom_v[n]))
                out[n] = [jnp.swapaxes(r.reshape(1, cols, rows), 1, 2) for r in res]
            else:
                out[n] = adamw("adamw_" + n, weights[n], [p_mine, p_other], mom_m[n], mom_v[n])

    reduce_and_update("rest", mid + late)
    arrived(first, "scatter_first_wait", first_g, [out[n][1] for n in mid + late])
    reduce_and_update("first", first)

    small_names = [n for n in names if n not in sharded]
    small_local = dict(
        mix_norm_w=d_mix_nw, dn_a_log=d_alog[:, :nh], dn_dt_bias=d_dtb[:, :nh], dn_norm_w=d_dn_nw,
        rw_mu=jnp.concatenate([dmu_r, dmu_k, dmu_v, dmu_s], axis=1), rw_w0=d_w0, rw_a0=d_a0, rw_k_k=d_k_k,
        rw_k_a=d_k_a, rw_r_k=d_r_k, rw_ln_w=d_ln_w, rw_ln_b=d_ln_b, xa_norm_w=d_xa_nw, mem_norm_w=d_mem_nw,
        ffn_norm_w=d_ffn_nw, final_norm_w=d_fnw)
    loss_vec = jnp.where(jnp.arange(LANES) == 0, loss_rows[0], 0.0)
    (g8,) = gather_all([_pack([small_local[n] for n in small_names] + [loss_vec])], after=[out["w_in"][1]])

    packed_like = [weights[n] for n in small_names] + [loss_vec]
    zero = jnp.zeros((LANES,), f32)
    res = adamw_packed(_pack([weights[n] for n in small_names] + [zero]), g8,
                       _pack([mom_m[n] for n in small_names] + [zero]),
                       _pack([mom_v[n] for n in small_names] + [zero]))
    unpacked = [_unpack(r, packed_like) for r in res]
    for i, n in enumerate(small_names):
        out[n] = [u[i] for u in unpacked]
    loss = unpacked[0][-1][0]

    return (loss, grad_x.reshape(x.shape), *[out[n][0] for n in names], *[out[n][1] for n in names],
            *[out[n][2] for n in names], *[out[n][3] for n in names])
```

```python
import functools

import jax
import jax.numpy as jnp
from jax import lax
from jax.experimental import pallas as pl
from jax.experimental.pallas import tpu as pltpu

f32 = jnp.float32
bf16 = jnp.bfloat16
HI = lax.Precision.HIGHEST
MESH = pl.DeviceIdType.MESH

LANES = 128
VMEM_LIMIT = 56 << 20
TOK_TILE = 256
TOK_TILE_BLOCKED = 1024
MM_TILE = 1024
MM_TILE_K = 2048
MM_WHOLE_K_BYTES = 55 << 20
GDN_CHUNK = 128
WKV_CHUNK = 64
SCAN_GROUP = 8
P_BULK = 1
P_INV = 1
P_RESID = 3
P_CUMSUM = 3
RMS_EPS = 1e-6
RW_GN_EPS = 64e-5
RW_HEAD = 64

ADAM_LR, ADAM_B1, ADAM_B2, ADAM_EPS, ADAM_WD, ADAM_STEP = 0.001, 0.9, 0.999, 1e-08, 0.01, 10


def _params(n_grid):
    return pltpu.CompilerParams(dimension_semantics=("arbitrary",) * n_grid, vmem_limit_bytes=VMEM_LIMIT)


_DIMS = {"nn": (((1,), (0,)), ((), ())), "nt": (((1,), (1,)), ((), ())), "tn": (((0,), (0,)), ((), ()))}
_DIMS_BATCHED = {"nn": (((2,), (1,)), ((0,), (0,))), "nt": (((2,), (2,)), ((0,), (0,))),
                 "tn": (((1,), (1,)), ((0,), (0,)))}


def _raw_dot(a, b, mode, passes):
    dims = (_DIMS if a.ndim == 2 else _DIMS_BATCHED)[mode]
    if passes == 6:
        return lax.dot_general(a.astype(f32), b.astype(f32), dims, precision=HI, preferred_element_type=f32)
    ah, bh = a.astype(bf16), b.astype(bf16)
    r = lax.dot_general(ah, bh, dims, preferred_element_type=f32)
    if passes == 3:
        al = (a - ah.astype(f32)).astype(bf16)
        bl = (b - bh.astype(f32)).astype(bf16)
        r = r + lax.dot_general(al, bh, dims, preferred_element_type=f32)
        r = r + lax.dot_general(ah, bl, dims, preferred_element_type=f32)
    return r


@functools.partial(jax.custom_vjp, nondiff_argnums=(2, 3))
def pdot(a, b, mode, passes):
    return _raw_dot(a, b, mode, passes)


def _pdot_bwd(mode, passes, res, g):
    a, b = res
    if mode == "nn":
        da, db = _raw_dot(g, b, "nt", passes), _raw_dot(a, g, "tn", passes)
    elif mode == "nt":
        da, db = _raw_dot(g, b, "nn", passes), _raw_dot(g, a, "tn", passes)
    else:
        da, db = _raw_dot(b, g, "nt", passes), _raw_dot(a, g, "nn", passes)
    return da.astype(a.dtype), db.astype(b.dtype)


pdot.defvjp(lambda a, b, mode, passes: (_raw_dot(a, b, mode, passes), (a, b)), _pdot_bwd)


def bdot(a, b):
    return pdot(a, b, "nn", 1)


def bdot_nt(a, b):
    return pdot(a, b, "nt", 1)


def _shift_rows(x, k):
    row = lax.broadcasted_iota(jnp.int32, x.shape, 0)
    return jnp.where(row < k, 0.0, pltpu.roll(x, k, axis=0))


def _unshift_rows(g, k):
    n = g.shape[0]
    row = lax.broadcasted_iota(jnp.int32, g.shape, 0)
    return jnp.where(row >= n - k, 0.0, pltpu.roll(g, n - k, axis=0))


@functools.partial(jax.custom_vjp, nondiff_argnums=(1,))
def tshift(x, k):
    return _shift_rows(x, k)


tshift.defvjp(lambda x, k: (_shift_rows(x, k), None), lambda k, _, g: (_unshift_rows(g, k),))


def rms(x, w):
    x = x.astype(f32)
    return x * lax.rsqrt(jnp.mean(x * x, axis=-1, keepdims=True) + RMS_EPS) * w


def softplus(x):
    return jnp.maximum(x, 0.0) + jnp.log(1.0 + jnp.exp(-jnp.abs(x)))


def seg2sum(x):
    lo = lax.broadcasted_iota(jnp.int32, x.shape, 1) < RW_HEAD
    s_lo = jnp.sum(jnp.where(lo, x, 0.0), axis=-1, keepdims=True)
    s_hi = jnp.sum(jnp.where(lo, 0.0, x), axis=-1, keepdims=True)
    return jnp.where(lo, s_lo, s_hi)


def _tile(n, pref):
    if n <= pref:
        return n
    t = pref
    while t >= LANES:
        if n % t == 0:
            return t
        t -= LANES
    return n


def mm(name, a, b, *, ta=False, tb=False, add=None, out_dtype=f32, by_chip=False, epilogue=None, extra=(),
       out_dtypes=None, after=None):
    (k, m) = a.shape if ta else a.shape[::-1]
    (n, kb) = b.shape if tb else b.shape[::-1]
    assert k == kb, (name, a.shape, b.shape)
    tm, tk = _tile(m, MM_TILE), _tile(k, MM_TILE_K)
    tn = _tile(n // 4, MM_TILE) if by_chip else _tile(n, MM_TILE)
    vmem = VMEM_LIMIT
    narrow = _tile(n, MM_TILE // 2)
    whole_k = 2 * 2 * (tm + narrow) * k + 2 * 4 * tm * narrow * (1 + len(extra) + (add is not None))
    if k > tk and not by_chip and whole_k <= MM_WHOLE_K_BYTES:
        tn, tk, vmem = narrow, k, MM_WHOLE_K_BYTES + (7 << 20)
    nk = k // tk
    dims = (((0,) if ta else (1,), (1,) if tb else (0,)), ((), ()))
    extra = list(extra) + ([] if add is None else [add])
    out_dtypes = [out_dtype] if out_dtypes is None else list(out_dtypes)
    n_extra, n_out = len(extra), len(out_dtypes)
    n_after = 0 if after is None else 1

    def body(*refs):
        a_ref, b_ref = refs[:2]
        x_refs = refs[2:2 + n_extra]
        o_refs = refs[2 + n_extra + n_after:2 + n_extra + n_after + n_out]
        part = lax.dot_general(a_ref[...].astype(bf16), b_ref[...].astype(bf16), dims, preferred_element_type=f32)

        def finish(r):
            xs = [x[...] for x in x_refs]
            if add is not None:
                r = r + xs.pop().astype(f32)
            outs = (r,) if epilogue is None else epilogue(r, *xs)
            for o_ref, o in zip(o_refs, outs):
                o_ref[...] = o.astype(o_ref.dtype)

        if nk == 1:
            finish(part)
            return
        acc = refs[-1]
        kk = pl.program_id(2)

        @pl.when(kk == 0)
        def _():
            acc[...] = part

        @pl.when(kk > 0)
        def _():
            acc[...] += part

        @pl.when(kk == nk - 1)
        def _():
            finish(acc[...])

    a_spec = pl.BlockSpec((tk, tm), lambda i, j, q: (q, i)) if ta else pl.BlockSpec((tm, tk), lambda i, j, q: (i, q))
    b_spec = pl.BlockSpec((tn, tk), lambda i, j, q: (j, q)) if tb else pl.BlockSpec((tk, tn), lambda i, j, q: (q, j))
    x_spec = pl.BlockSpec((tm, tn), lambda i, j, q: (i, j))
    if by_chip:
        per_chip = n // 4 // tn
        o_spec = pl.BlockSpec((None, tm, tn), lambda i, j, q: (j // per_chip, i, j % per_chip))
        o_shape = (4, m, n // 4)
    else:
        o_spec, o_shape = x_spec, (m, n)
    afters = [] if after is None else [jnp.reshape(after, (1, 1))]
    res = pl.pallas_call(
        body, name=name, grid=(m // tm, n // tn, nk),
        in_specs=[a_spec, b_spec] + [x_spec] * n_extra + [pl.BlockSpec(memory_space=pl.ANY)] * n_after,
        out_specs=[o_spec] * n_out, out_shape=[jax.ShapeDtypeStruct(o_shape, dt) for dt in out_dtypes],
        scratch_shapes=[pltpu.VMEM((tm, tn), f32)] if nk > 1 else [],
        compiler_params=pltpu.CompilerParams(dimension_semantics=("arbitrary",) * 3, vmem_limit_bytes=vmem))(
            a, b, *extra, *afters)
    return res[0] if n_out == 1 else res


GATHER_PARTS = 4
REGROUP_PIECES = 4


REGROUP_SUB = 5
REGROUP_FIELDS = 5


def _regroup_table(n_out, sources_of):
    import numpy as np
    tbl = np.zeros((REGROUP_FIELDS * REGROUP_PIECES, n_out), np.int32)
    for j in range(n_out):
        groups = sorted(sources_of(j).items())
        assert len(groups) <= REGROUP_PIECES, (j, len(groups))
        for p in range(REGROUP_PIECES):
            if p < len(groups):
                key, lanes = groups[p]
                shifts = {q - s for s, q in lanes}
                qs = sorted(q for _, q in lanes)
                assert len(shifts) == 1 and qs == list(range(qs[0], qs[-1] + 1)), (j, key)
                row = (key[0], key[1], shifts.pop(), qs[0], qs[-1] + 1)
            else:
                row = (0, 0, 0, 0, 0)
            tbl[REGROUP_FIELDS * p:REGROUP_FIELDS * (p + 1), j] = row
    return jnp.asarray(tbl)


def lane_regroup(name, src, table, src_spec, out_spec, out_shape, grid, src_width, sub):
    rows = src.shape[-2]
    n_src = REGROUP_PIECES * sub

    def body(tbl, *refs):
        o_ref = refs[n_src]
        step = pl.program_id(1)

        def moved(b, p):
            j = step * sub + b
            blk, shift, lo, hi = (tbl[REGROUP_FIELDS * p + f, j] for f in range(1, REGROUP_FIELDS))
            x = refs[b * REGROUP_PIECES + p][...]
            if src_width % LANES:
                lane = lax.broadcasted_iota(jnp.int32, (rows, LANES), 1)
                x = jnp.where(lane < src_width - blk * LANES, x, jnp.zeros((), src.dtype))
            pi = lax.broadcasted_iota(jnp.int32, (LANES, LANES), 0)
            qi = lax.broadcasted_iota(jnp.int32, (LANES, LANES), 1)
            sel = jnp.logical_and(qi - pi == shift, jnp.logical_and(qi >= lo, qi < hi))
            return jnp.dot(x, sel.astype(src.dtype), preferred_element_type=f32).astype(o_ref.dtype)

        for b in range(sub):
            lanes = slice(b * LANES, (b + 1) * LANES)
            o_ref[:, lanes] = moved(b, 0)
            for p in range(1, REGROUP_PIECES):
                j = step * sub + b

                @pl.when(tbl[REGROUP_FIELDS * p + 4, j] > tbl[REGROUP_FIELDS * p + 3, j])
                def _(b=b, p=p, lanes=lanes):
                    o_ref[:, lanes] += moved(b, p)

    return pl.pallas_call(
        body, name=name, out_shape=jax.ShapeDtypeStruct(out_shape, src.dtype),
        grid_spec=pltpu.PrefetchScalarGridSpec(
            num_scalar_prefetch=1, grid=grid,
            in_specs=[src_spec(p, b) for b in range(sub) for p in range(REGROUP_PIECES)], out_specs=out_spec),
        compiler_params=_params(2))(table, *[src] * n_src)


def w_in_layout(d_cols, shard_w, c_split, gap, n_blocks):
    def ext_of(c):
        return c if c < c_split else c + gap

    def fwd_sources(j):
        groups = {}
        for q in range(LANES):
            e = j * LANES + q
            c = e if e < c_split else e - gap
            if (c_split <= e < c_split + gap) or c >= d_cols:
                continue
            s, l = divmod(c, shard_w)
            groups.setdefault((s, l // LANES), []).append((l % LANES, q))
        return groups

    per_shard = -(-shard_w // LANES)

    def bwd_sources(j):
        s, b = divmod(j, per_shard)
        groups = {}
        for q in range(LANES):
            l = b * LANES + q
            if l >= shard_w:
                continue
            e = ext_of(s * shard_w + l)
            groups.setdefault((0, e // LANES), []).append((e % LANES, q))
        return groups

    return _regroup_table(n_blocks, fwd_sources), _regroup_table(4 * per_shard, bwd_sources), per_shard


def _in_spec(a, kind, tile):
    if kind == "row":
        return pl.BlockSpec((tile, a.shape[1]), lambda j, i: (i, 0))
    if kind == "const":
        return pl.BlockSpec(a.shape, lambda j, i: (0, 0))
    tag, cw, off = kind
    if tag == "rowc":
        return pl.BlockSpec((tile, cw), lambda j, i: (i, j + off))
    assert tag == "constc", kind
    return pl.BlockSpec((a.shape[0], cw), lambda j, i: (0, j + off))


def rowcall(name, fn, ins, outs, *, rows, tile=None, ncol=1):
    if tile is None:
        tile = min(TOK_TILE_BLOCKED if ncol > 1 else TOK_TILE, rows)
    n_in = len(ins)
    kinds = [o[3] for o in outs]

    def body(*refs):
        j, i = pl.program_id(0), pl.program_id(1)
        res = fn(*[r[...] for r in refs[:n_in]])
        for ref, val, kind in zip(refs[n_in:], res, kinds):
            if kind in ("row", "rowc"):
                ref[...] = val.astype(ref.dtype)
            else:
                first = (i == 0) if kind == "accc" else jnp.logical_and(i == 0, j == 0)

                @pl.when(first)
                def _(ref=ref, val=val):
                    ref[...] = val.astype(ref.dtype)

                @pl.when(jnp.logical_not(first))
                def _(ref=ref, val=val):
                    ref[...] += val.astype(ref.dtype)

    out_shape, out_specs = [], []
    for nr, nc, dtype, kind in outs:
        out_shape.append(jax.ShapeDtypeStruct((nr, nc), dtype))
        if kind == "row":
            out_specs.append(pl.BlockSpec((tile, nc), lambda j, i: (i, 0)))
        elif kind == "rowc":
            out_specs.append(pl.BlockSpec((tile, nc // ncol), lambda j, i: (i, j)))
        elif kind == "acc":
            out_specs.append(pl.BlockSpec((nr, nc), lambda j, i: (0, 0)))
        else:
            out_specs.append(pl.BlockSpec((nr, nc // ncol), lambda j, i: (0, j)))
    return pl.pallas_call(
        body, name=name, grid=(ncol, rows // tile), in_specs=[_in_spec(a, k, tile) for a, k in ins],
        out_specs=out_specs, out_shape=out_shape, compiler_params=_params(2))(*[a for a, _ in ins])


def rowvjp(name, fn, ins, cts, grads, *, rows, tile=None, ncol=1, adds=(), dup16=()):
    n_in = len(ins)
    ct_sizes = [len(c) for c in cts]
    flat_cts = [m for c in cts for m in c]
    n_ct = len(flat_cts)

    def wrapped(*vals):
        xs = list(vals[:n_in])
        gs = vals[n_in:n_in + n_ct]
        extra = vals[n_in + n_ct:]

        def f(*dvars):
            full = list(xs)
            for k, v in zip(grads, dvars):
                full[k] = v
            return fn(*full)

        outs, pull = jax.vjp(f, *[xs[k] for k in grads])
        cot, p = [], 0
        for o, size in zip(outs, ct_sizes):
            g = gs[p].astype(f32)
            for q in range(1, size):
                g = g + gs[p + q].astype(f32)
            cot.append(g.astype(o.dtype))
            p += size
        gv = list(pull(tuple(cot)))
        for (pos, _, _), e in zip(adds, extra):
            gv[pos] = gv[pos] + e.astype(gv[pos].dtype)
        return tuple(gv) + tuple(gv[pos] for pos in dup16)

    outs = []
    for k in grads:
        a, kind = ins[k]
        if kind == "row":
            outs.append((rows, a.shape[1] * ncol, f32, "rowc") if ncol > 1 else (rows, a.shape[1], f32, "row"))
        elif kind == "const":
            outs.append((a.shape[0], a.shape[1], f32, "acc"))
        elif kind[0] == "rowc":
            outs.append((rows, kind[1] * ncol, f32, "rowc"))
        else:
            outs.append((a.shape[0], kind[1] * ncol, f32, "accc"))
    for pos in dup16:
        nr, nc, _, kind = outs[pos]
        outs.append((nr, nc, bf16, kind))
    all_ins = list(ins) + flat_cts + [(a, kind) for _, a, kind in adds]
    return rowcall(name, wrapped, all_ins, outs, rows=rows, tile=tile, ncol=ncol)


def fn_rms(x, w):
    return (rms(x, w),)


def make_fn_gconv(n_norm_blocks):
    def fn(p, cw):
        c = cw[3:4] * p
        for jj in range(3):
            c = c + cw[jj:jj + 1] * tshift(p, 3 - jj)
        s = c * jax.nn.sigmoid(c)
        nrm = s * lax.rsqrt(jnp.sum(s * s, axis=-1, keepdims=True) + 1e-6)
        return (jnp.where(pl.program_id(0) < n_norm_blocks, nrm, s),)
    return fn


def fn_ggate(ps0, alog, dtb, e_g, e_b):
    g = -jnp.exp(alog) * softplus(ps0 + dtb)
    beta = jax.nn.sigmoid(ps0)
    return pdot(g, e_g, "nn", 6), pdot(beta, e_b, "nn", 6)


def fn_gpost(o, z, nw):
    return (rms(o, nw) * (z * jax.nn.sigmoid(z)),)


def fn_lerp(p, mu):
    return (p + (tshift(p, 1) - p) * mu,)


def fn_rprep(pk, psl, w0, a0, k_k, k_a, w2p, a2p, g2):
    g1, g2in = psl[:, :LANES], psl[:, LANES:]
    log_w = -softplus(-(w0 + bdot(jnp.tanh(g1), w2p))) - 0.5
    lw = -jnp.exp(log_w)
    a = jax.nn.sigmoid(a0 + bdot(g1, a2p))
    gate = bdot(jax.nn.sigmoid(g2in), g2)
    kkr = pk * k_k
    kk = kkr / jnp.maximum(jnp.sqrt(seg2sum(kkr * kkr)), 1e-12)
    kmod = pk * (1.0 + (a - 1.0) * k_a)
    return lw, kmod, kk, a, gate


def fn_rpost(y, r, kmod, v, gate, r_k, ln_w, ln_b):
    inv_n = 1.0 / RW_HEAD
    mean = seg2sum(y) * inv_n
    d = y - mean
    var = seg2sum(d * d) * inv_n
    yn = d * lax.rsqrt(var + RW_GN_EPS) * ln_w + ln_b
    bonus = seg2sum(r * kmod * r_k) * v
    return ((yn + bonus) * gate,)


def fn_xcore(q, k, v):
    s = bdot_nt(q, k) * (LANES ** -0.5)
    p = jax.nn.softmax(s, axis=-1)
    return (bdot(p, v),)


def fn_final(h, tgt, w):
    def loss_fn(h, w):
        return 0.5 * jnp.sum(jnp.mean(jnp.square(rms(h, w) - tgt), axis=-1))

    val, (dh, dw) = jax.value_and_grad(loss_fn, argnums=(0, 1))(h, w)
    return dh, dh.astype(bf16), dw, jnp.full((8, LANES), val, f32)


def fn_sumcols(n):
    def fn(x):
        w = x.shape[1] // n
        s = x[:, :w]
        for q in range(1, n):
            s = s + x[:, q * w:(q + 1) * w]
        return (s,)
    return fn


def _tri(c):
    ii = lax.broadcasted_iota(jnp.int32, (c, c), 0)
    jj = lax.broadcasted_iota(jnp.int32, (c, c), 1)
    return ii, jj


def _neumann_raw(m, steps):
    c = m.shape[-1]
    ii, jj = _tri(c)
    eye = (ii == jj).astype(f32)
    t, p = eye + m, m
    for _ in range(steps):
        p = _raw_dot(p, p, "nn", P_INV)
        t = _raw_dot(t, eye + p, "nn", P_INV)
    resid = eye - t + _raw_dot(m, t, "nn", P_RESID)
    return t + _raw_dot(t, resid, "nn", P_INV)


@functools.partial(jax.custom_vjp, nondiff_argnums=(1,))
def _neumann_inverse(m, steps):
    return _neumann_raw(m, steps)


def _neumann_fwd(m, steps):
    t = _neumann_raw(m, steps)
    return t, t


def _neumann_bwd(steps, t, g):
    return (_raw_dot(_raw_dot(t, g, "tn", P_RESID), t, "nt", P_RESID),)


_neumann_inverse.defvjp(_neumann_fwd, _neumann_bwd)


@jax.custom_vjp
def _known_inverse(m, t):
    return t


_known_inverse.defvjp(lambda m, t: (t, t), lambda t, g: (_neumann_bwd(0, t, g)[0], jnp.zeros_like(t)))


def _inverse(m, steps, kept):
    return _neumann_inverse(m, steps) if kept is None else _known_inverse(m, kept)


def cumsum_rows(x):
    t = x.shape[1]
    ii, jj = _tri(t)
    tri = jnp.broadcast_to((ii >= jj).astype(f32), (x.shape[0], t, t))
    return pdot(tri, x, "nn", P_CUMSUM)


def gdn_chunk(q, k, v, gb, bb, s, kept=None):
    c = q.shape[1]
    ii, jj = _tri(c)
    low = ii >= jj
    gcb = cumsum_rows(gb)
    gl = jnp.sum(gb, axis=1, keepdims=True)
    gc_col = gcb[:, :, :c]
    diff = gc_col - jnp.swapaxes(gc_col, 1, 2)
    decay = jnp.where(low, jnp.exp(jnp.where(low, diff, 0.0)), 0.0)
    qs = q * (q.shape[2] ** -0.5)
    kb = k * bb
    with_k = pdot(jnp.concatenate([kb, qs], axis=1), k, "nt", P_BULK)
    a = jnp.where(ii > jj, with_k[:, :c] * decay, 0.0)
    attn = with_k[:, c:] * decay
    t = _inverse(-a, c.bit_length() - 2, kept)
    eg = jnp.exp(gcb)
    uw = pdot(t, jnp.concatenate([v * bb, kb * eg], axis=2), "nn", P_BULK)
    u, w = uw[:, :, :LANES], uw[:, :, LANES:]
    kd = k * jnp.exp(gl - gcb)
    from_state = pdot(jnp.concatenate([w, qs * eg], axis=1), s, "nn", P_BULK)
    v_new = u - from_state[:, :c]
    o = from_state[:, c:] + pdot(attn, v_new, "nn", P_BULK)
    s_new = s * jnp.exp(gl) + pdot(kd, v_new, "tn", P_BULK)
    return o, s_new, t


def wkv_chunk(r, lw, k, v, kk, a, s, kept=None):
    t = r.shape[1]
    ii, jj = _tri(t)
    lo = lax.broadcasted_iota(jnp.int32, r.shape, 2) < RW_HEAD
    cl = cumsum_rows(lw)
    cl_last = jnp.sum(lw, axis=1, keepdims=True)
    al = -kk * jnp.exp(cl - lw)
    be = (a * kk) * jnp.exp(-cl)
    kt = k * jnp.exp(-cl)
    rt = r * jnp.exp(cl)

    def dot(xa, xb, mode="nn"):
        return pdot(xa, xb, mode, P_BULK)

    def sel(x_lo, x_hi):
        return jnp.where(lo, x_lo, x_hi)

    left = jnp.concatenate([jnp.where(lo, al, 0.0), jnp.where(lo, 0.0, al),
                            jnp.where(lo, rt, 0.0), jnp.where(lo, 0.0, rt)], axis=1)
    with_be, with_kt = dot(left, be, "nt"), dot(left, kt, "nt")

    def blocks(prod):
        below, upto = ii > jj, ii >= jj
        return [jnp.where(msk, prod[:, q * t:(q + 1) * t], 0.0) for q, msk in enumerate((below, below, upto, upto))]

    ab_lo, ab_hi, rb_lo, rb_hi = blocks(with_be)
    ak_lo, ak_hi, rk_lo, rk_hi = blocks(with_kt)
    from_state = dot(jnp.concatenate([al, rt], axis=1), s, "nt")
    x = from_state[:, :t] + sel(dot(ak_lo, v), dot(ak_hi, v))
    steps = t.bit_length() - 2
    inv_lo = _inverse(ab_lo, steps, None if kept is None else kept[:, :t])
    inv_hi = _inverse(ab_hi, steps, None if kept is None else kept[:, t:])
    u = sel(dot(inv_lo, x), dot(inv_hi, x))
    y = from_state[:, t:] + sel(dot(rb_lo, u) + dot(rk_lo, v), dot(rb_hi, u) + dot(rk_hi, v))
    vi = lax.broadcasted_iota(jnp.int32, s.shape, 1) < RW_HEAD
    ki = lax.broadcasted_iota(jnp.int32, s.shape, 2) < RW_HEAD
    s_new = jnp.where(vi == ki, (s + dot(u, be, "tn") + dot(v, kt, "tn")) * jnp.exp(cl_last), 0.0)
    return y, s_new, jnp.concatenate([inv_lo, inv_hi], axis=1)


def _scan_group(ncol, offs):
    g = SCAN_GROUP
    while g > 1 and (ncol % g or any(o % g for o in offs)):
        g //= 2
    return g


def scan_fwd(name, chunk_fn, ins, *, rows, chunk, ncol):
    n = rows // chunk
    n_in = len(ins)
    grp = _scan_group(ncol, [off for _, off in ins])
    tile = jax.ShapeDtypeStruct((grp, chunk, LANES), f32)
    kept = jax.eval_shape(chunk_fn, *[tile] * n_in, jax.ShapeDtypeStruct((grp, LANES, LANES), f32))[2].shape[1:]

    def body(*refs):
        o_ref, st_ref, kept_ref, s_scr = refs[n_in:]

        @pl.when(pl.program_id(1) == 0)
        def _():
            s_scr[...] = jnp.zeros_like(s_scr)

        cols = [slice(b * LANES, (b + 1) * LANES) for b in range(grp)]
        s = s_scr[...]
        st_ref[...] = s
        o, s_new, inv = chunk_fn(*[jnp.stack([r[:, c] for c in cols]) for r in refs[:n_in]], s)
        for b, c in enumerate(cols):
            o_ref[:, c] = o[b]
        kept_ref[...] = inv
        s_scr[...] = s_new

    def spec(off):
        return pl.BlockSpec((chunk, grp * LANES), lambda h, c: (c, h + off // grp))

    def per_chunk(shape):
        return pl.BlockSpec((grp, None) + tuple(shape), lambda h, c: (h, c, 0, 0))

    return pl.pallas_call(
        body, name=name, grid=(ncol // grp, n), in_specs=[spec(off) for _, off in ins],
        out_specs=[spec(0), per_chunk((LANES, LANES)), per_chunk(kept)],
        out_shape=[jax.ShapeDtypeStruct((rows, ncol * LANES), f32),
                   jax.ShapeDtypeStruct((ncol, n, LANES, LANES), f32),
                   jax.ShapeDtypeStruct((ncol, n) + tuple(kept), f32)],
        scratch_shapes=[pltpu.VMEM((grp, LANES, LANES), f32)], compiler_params=_params(2))(*[a for a, _ in ins])


def scan_bwd(name, chunk_fn, ins, states, kept, d_out, d_off, *, rows, chunk, ncol):
    n = rows // chunk
    n_in = len(ins)
    grp = _scan_group(ncol, [off for _, off in ins] + [d_off])

    def body(*refs):
        st_ref, kept_ref, do_ref = refs[n_in:n_in + 3]
        g_refs = refs[n_in + 3:2 * n_in + 3]
        ds_scr = refs[-1]

        @pl.when(pl.program_id(1) == 0)
        def _():
            ds_scr[...] = jnp.zeros_like(ds_scr)

        cols = [slice(b * LANES, (b + 1) * LANES) for b in range(grp)]

        def batch(ref):
            return jnp.stack([ref[:, c] for c in cols])

        inv = kept_ref[...]
        _, pull = jax.vjp(lambda *a: chunk_fn(*a, kept=inv)[:2], *[batch(r) for r in refs[:n_in]], st_ref[...])
        gs = pull((batch(do_ref), ds_scr[...]))
        for ref, g in zip(g_refs, gs[:n_in]):
            for b, c in enumerate(cols):
                ref[:, c] = g[b]
        ds_scr[...] = gs[n_in]

    def spec(off):
        return pl.BlockSpec((chunk, grp * LANES), lambda h, c: (n - 1 - c, h + off // grp))

    def per_chunk(shape):
        return pl.BlockSpec((grp, None) + tuple(shape), lambda h, c: (h, n - 1 - c, 0, 0))

    return pl.pallas_call(
        body, name=name, grid=(ncol // grp, n),
        in_specs=[spec(off) for _, off in ins] + [per_chunk(states.shape[2:]), per_chunk(kept.shape[2:]), spec(d_off)],
        out_specs=[spec(0)] * n_in, out_shape=[jax.ShapeDtypeStruct((rows, ncol * LANES), f32)] * n_in,
        scratch_shapes=[pltpu.VMEM((grp, LANES, LANES), f32)],
        compiler_params=_params(2))(*[a for a, _ in ins], states, kept, d_out)


def flip_exchange(name, arrs, flips, n_slots, slot_of, src_of, with_self, after=()):
    n = len(arrs)
    nf = len(flips)
    n_after = len(after)

    def body(*refs):
        ins, outs = refs[:n], refs[n + n_after:2 * n + n_after]
        send, recv, lsem = refs[2 * n + n_after:]
        me = (lax.axis_index("x"), lax.axis_index("y"), lax.axis_index("c"))
        copies = []
        for k in range(n):
            if with_self:
                cp = pltpu.make_async_copy(src_of(ins[k], me), outs[k].at[slot_of(me)], lsem.at[k])
                cp.start()
                copies.append(cp)
            for j, fl in enumerate(flips):
                peer = tuple(1 - m if f else m for m, f in zip(me, fl))
                cp = pltpu.make_async_remote_copy(
                    src_ref=src_of(ins[k], peer), dst_ref=outs[k].at[slot_of(me)], send_sem=send.at[k, j],
                    recv_sem=recv.at[k, j], device_id=peer, device_id_type=MESH)
                cp.start()
                copies.append(cp)
        for cp in copies:
            cp.wait()

    def out_sds(a):
        blk = src_of(jax.ShapeDtypeStruct(a.shape, a.dtype), None)
        return jax.ShapeDtypeStruct((n_slots,) + tuple(blk), a.dtype)

    any_spec = pl.BlockSpec(memory_space=pl.ANY)
    return pl.pallas_call(
        body, name=name, in_specs=[any_spec] * (n + n_after), out_specs=[any_spec] * n,
        out_shape=[out_sds(a) for a in arrs],
        scratch_shapes=[pltpu.SemaphoreType.DMA((n, nf)), pltpu.SemaphoreType.DMA((n, nf)),
                        pltpu.SemaphoreType.DMA((n,))],
        compiler_params=pltpu.CompilerParams(has_side_effects=True))(*arrs, *after)


_CHIP_FLIPS = ((1, 0, 0), (0, 1, 0), (1, 1, 0))
_ALL_FLIPS = ((0, 0, 1), (0, 1, 0), (0, 1, 1), (1, 0, 0), (1, 0, 1), (1, 1, 0), (1, 1, 1))


def _whole(ref, pos):
    return ref.shape if pos is None else ref


def _chip_block(ref, pos):
    return ref.shape[1:] if pos is None else ref.at[2 * pos[0] + pos[1]]


def _chip_slot(p):
    return 2 * p[0] + p[1]


_HBM = pl.BlockSpec(memory_space=pltpu.HBM)
_SEM = pl.BlockSpec(memory_space=pltpu.SEMAPHORE)
_DATAFLOW = pltpu.SideEffectType.DATAFLOW_SIDE_EFFECTING


def _split_copies(mode, refs, n, send, recv):
    me = (lax.axis_index("x"), lax.axis_index("y"), lax.axis_index("c"))
    sib = (me[0], me[1], 1 - me[2])
    lands = refs[:n] if mode == "handover" else refs[n:2 * n]
    copies = []
    for k, land in enumerate(lands):
        half = land.shape[1] // 2
        mine = pl.ds(pl.multiple_of(me[2] * half, 16), half)
        for j, fl in enumerate(_CHIP_FLIPS):
            peer = tuple(1 - m if f else m for m, f in zip(me, fl))
            if mode == "gather":
                src, dst, to = refs[k], land.at[_chip_slot(me)], peer
            elif mode == "scatter":
                src, dst, to = refs[k].at[_chip_slot(peer)], land.at[_chip_slot(me)], peer
            elif mode == "gather_half":
                src, dst, to = refs[k].at[mine], land.at[_chip_slot(me), mine], peer
            else:
                src = dst = land.at[_chip_slot(peer), mine]
                to = sib
            q = k * len(_CHIP_FLIPS) + j
            copies.append(pltpu.make_async_remote_copy(src_ref=src, dst_ref=dst, send_sem=send.at[q],
                                                       recv_sem=recv.at[q], device_id=to, device_id_type=MESH))
    return copies


def split_start(name, mode, ops, n):
    ops = [pltpu.with_memory_space_constraint(a, pltpu.HBM) for a in ops]
    m = len(ops)

    def body(*refs):
        for cp in _split_copies(mode, refs[:m], n, refs[m], refs[m + 1]):
            cp.start()
        refs[-1][...] = jnp.zeros_like(refs[-1])

    sems = pltpu.SemaphoreType.DMA((n * len(_CHIP_FLIPS),))
    outs = pl.pallas_call(
        body, name=name, in_specs=[_HBM] * m,
        out_shape=(sems, sems, *[pltpu.HBM(a.shape, a.dtype) for a in ops], jax.ShapeDtypeStruct((8, LANES), f32)),
        out_specs=(_SEM, _SEM, *[_HBM] * m, pl.BlockSpec(memory_space=pltpu.VMEM)),
        input_output_aliases={i: 2 + i for i in range(m)},
        compiler_params=pltpu.CompilerParams(has_side_effects=_DATAFLOW))(*ops)
    return (outs[0], outs[1], list(outs[2:2 + m]), mode, n), outs[-1][0, 0]


def split_wait(name, state, after, with_sources=False):
    send, recv, ops, mode, n = state
    m = len(ops)
    afters = list(after) if isinstance(after, (list, tuple)) else [after]

    def body(*refs):
        for cp in _split_copies(mode, refs[:m], n, refs[m], refs[m + 1]):
            cp.wait_send()
            cp.wait_recv()

    outs = pl.pallas_call(
        body, name=name, in_specs=[_HBM] * m + [_SEM, _SEM] + [pl.BlockSpec(memory_space=pl.ANY)] * len(afters),
        out_shape=tuple(pltpu.HBM(a.shape, a.dtype) for a in ops), out_specs=tuple([_HBM] * m),
        input_output_aliases={i: i for i in range(m)},
        compiler_params=pltpu.CompilerParams(has_side_effects=_DATAFLOW))(*ops, send, recv, *afters)
    return (list(outs[m - n:]), list(outs[:m - n])) if with_sources else list(outs[m - n:])


def chips_start(name, arrs, src_of, halves=False):
    me = _chip_slot((lax.axis_index("x"), lax.axis_index("y")))
    lands = []
    for a in arrs:
        blk = tuple(src_of(jax.ShapeDtypeStruct(a.shape, a.dtype), None))
        land = lax.empty((4,) + blk, a.dtype)
        lands.append(lax.dynamic_update_index_in_dim(land, a, me, 0) if src_of is _whole else land)
    mode = "scatter" if src_of is _chip_block else ("gather_half" if halves else "gather")
    return split_start(name, mode, list(arrs) + lands, len(arrs))


chips_wait = split_wait


def handover_start(name, lands):
    return split_start(name, "handover", lands, len(lands))


def gather_chips_halves(name, arrs):
    n = len(arrs)
    nf = len(_CHIP_FLIPS)
    split = [a.shape[0] % 32 == 0 for a in arrs]
    parts = [GATHER_PARTS if s and a.shape[0] % (32 * GATHER_PARTS) == 0 and a.size >= (1 << 18) else 1
             for a, s in zip(arrs, split)]

    def body(*refs):
        ins, outs = refs[:n], refs[n:2 * n]
        send1, recv1, send2, recv2, lsem = refs[2 * n:]
        me = (lax.axis_index("x"), lax.axis_index("y"), lax.axis_index("c"))
        sib = (me[0], me[1], 1 - me[2])
        peers = [tuple(1 - m if f else m for m, f in zip(me, fl)) for fl in _CHIP_FLIPS]
        local, first, second = [], [], []
        for k in range(n):
            cp = pltpu.make_async_copy(ins[k], outs[k].at[_chip_slot(me)], lsem.at[k])
            cp.start()
            local.append(cp)
        def rows_of(k, q):
            piece = ins[k].shape[0] // 2 // parts[k]
            if not split[k]:
                return pl.ds(0, ins[k].shape[0])
            return pl.ds(pl.multiple_of(me[2] * (piece * parts[k]) + q * piece, 16), piece)

        def push(k, src, dst_slot, rows, sem, to):
            cp = pltpu.make_async_remote_copy(
                src_ref=src, dst_ref=outs[k].at[dst_slot, rows], send_sem=send1.at[k, sem], recv_sem=recv1.at[k, sem],
                device_id=to, device_id_type=MESH)
            cp.start()
            return cp

        def hand_over(k, j, rows, sem):
            if split[k]:
                got = outs[k].at[_chip_slot(peers[j]), rows]
                fwd = pltpu.make_async_remote_copy(src_ref=got, dst_ref=got, send_sem=send2.at[k, sem],
                                                   recv_sem=recv2.at[k, sem], device_id=sib, device_id_type=MESH)
                fwd.start()
                second.append(fwd)

        relayed = [parts[k] > 1 for k in range(n)]
        sent = []
        for q in range(GATHER_PARTS):
            for k in range(n):
                if q < parts[k]:
                    for j in range(nf if not relayed[k] else 2):
                        sem = j * GATHER_PARTS + q
                        sent.append(push(k, ins[k].at[rows_of(k, q)], _chip_slot(me), rows_of(k, q), sem, peers[j]))
                        first.append((k, j, sem, q))
        for k, j, sem, q in first:
            rows = rows_of(k, q)
            probe = pltpu.make_async_remote_copy(
                src_ref=ins[k].at[rows], dst_ref=outs[k].at[_chip_slot(peers[j]), rows], send_sem=send1.at[k, sem],
                recv_sem=recv1.at[k, sem], device_id=peers[j], device_id_type=MESH)
            probe.wait_recv()
            if relayed[k] and j == (1 - q % 2):
                onward = 2 * GATHER_PARTS + q
                sent.append(push(k, outs[k].at[_chip_slot(peers[j]), rows], _chip_slot(peers[j]), rows, onward,
                                 peers[q % 2]))
            hand_over(k, j, rows, sem)
        for k in range(n):
            if relayed[k]:
                for q in range(parts[k]):
                    rows, sem = rows_of(k, q), 2 * GATHER_PARTS + q
                    probe = pltpu.make_async_remote_copy(
                        src_ref=ins[k].at[rows], dst_ref=outs[k].at[_chip_slot(peers[2]), rows],
                        send_sem=send1.at[k, sem], recv_sem=recv1.at[k, sem], device_id=peers[2], device_id_type=MESH)
                    probe.wait_recv()
                    hand_over(k, 2, rows, sem)
        for cp in sent:
            cp.wait_send()
        for cp in second:
            cp.wait()
        for cp in local:
            cp.wait()

    any_spec = pl.BlockSpec(memory_space=pl.ANY)
    sems = pltpu.SemaphoreType.DMA((n, nf * GATHER_PARTS))
    return pl.pallas_call(
        body, name=name, in_specs=[any_spec] * n, out_specs=[any_spec] * n,
        out_shape=[jax.ShapeDtypeStruct((4,) + a.shape, a.dtype) for a in arrs],
        scratch_shapes=[sems, sems, sems, sems, pltpu.SemaphoreType.DMA((n,))],
        compiler_params=pltpu.CompilerParams(has_side_effects=True))(*arrs)


def swap_sibling(name, arrs):
    outs = flip_exchange(name, arrs, ((0, 0, 1),), 1, lambda p: 0, _whole, False)
    return [o[0] for o in outs]


def gather_all(arrs, after=()):
    return flip_exchange("gather_all", arrs, _ALL_FLIPS, 8, lambda p: 4 * p[0] + 2 * p[1] + p[2], _whole, True,
                         after=after)


def _row_tile(nr, nc, n_arrays):
    budget = (20 << 20) // (n_arrays * 2 * 4 * max(nc, LANES))
    t = min(nr, budget) // 16 * 16
    while t > 0 and nr % t:
        t -= 16
    return t if t > 0 else nr


def sum_slots(name, x, own):
    ns, nr, nc = x.shape
    tile = _row_tile(nr, nc, ns + 1)
    me = jnp.reshape(_chip_slot((lax.axis_index("x"), lax.axis_index("y"))), (1,)).astype(jnp.int32)

    def body(me_ref, own_ref, *refs):
        o_ref = refs[-1]
        s = own_ref[...].astype(f32)
        for r in refs[:-1]:
            s = s + r[...].astype(f32)
        o_ref[...] = s.astype(o_ref.dtype)

    def slot(q):
        return pl.BlockSpec((None, tile, nc), lambda i, t: ((t[0] + q) % ns, i, 0))

    return pl.pallas_call(
        body, name=name, out_shape=jax.ShapeDtypeStruct((nr, nc), x.dtype),
        grid_spec=pltpu.PrefetchScalarGridSpec(
            num_scalar_prefetch=1, grid=(nr // tile,), in_specs=[slot(q) for q in range(ns)],
            out_specs=pl.BlockSpec((tile, nc), lambda i, t: (i, 0))),
        compiler_params=_params(1))(me, own, *[x] * (ns - 1))


def adamw(name, w, g_parts, m, v):
    nr, nc = w.shape[-2:]
    n_g = len(g_parts)
    tile = _row_tile(nr, nc, 7 + n_g)

    def body(*refs):
        w_ref, m_ref, v_ref = refs[:3]
        g = refs[3][...].astype(f32)
        for r in refs[4:3 + n_g]:
            g = g + r[...].astype(f32)
        g_ref, d_ref, nm_ref, nv_ref = refs[3 + n_g:]
        nm = ADAM_B1 * m_ref[...] + (1.0 - ADAM_B1) * g
        nv = ADAM_B2 * v_ref[...] + (1.0 - ADAM_B2) * jnp.square(g)
        m_hat = nm / (1.0 - ADAM_B1 ** ADAM_STEP)
        v_hat = nv / (1.0 - ADAM_B2 ** ADAM_STEP)
        g_ref[...] = g
        d_ref[...] = -ADAM_LR * (m_hat / (jnp.sqrt(v_hat) + ADAM_EPS) + ADAM_WD * w_ref[...])
        nm_ref[...] = nm
        nv_ref[...] = nv

    spec = pl.BlockSpec((tile, nc), lambda i: (i, 0))
    spec3 = pl.BlockSpec((None, tile, nc), lambda i: (0, i, 0)) if w.ndim == 3 else spec
    return pl.pallas_call(
        body, name=name, grid=(nr // tile,), in_specs=[spec3] * 3 + [spec] * n_g, out_specs=[spec3] * 4,
        out_shape=[jax.ShapeDtypeStruct(w.shape, f32)] * 4, compiler_params=_params(1))(w, m, v, *g_parts)


def adamw_packed(w, g8, m, v):
    nr, nc = w.shape

    def body(w_ref, g_ref, m_ref, v_ref, go_ref, d_ref, nm_ref, nv_ref):
        g = g_ref[0]
        for q in range(1, 8):
            g = g + g_ref[q]
        nm = ADAM_B1 * m_ref[...] + (1.0 - ADAM_B1) * g
        nv = ADAM_B2 * v_ref[...] + (1.0 - ADAM_B2) * jnp.square(g)
        m_hat = nm / (1.0 - ADAM_B1 ** ADAM_STEP)
        v_hat = nv / (1.0 - ADAM_B2 ** ADAM_STEP)
        go_ref[...] = g
        d_ref[...] = -ADAM_LR * (m_hat / (jnp.sqrt(v_hat) + ADAM_EPS) + ADAM_WD * w_ref[...])
        nm_ref[...] = nm
        nv_ref[...] = nv

    return pl.pallas_call(body, name="adamw_packed", out_shape=[jax.ShapeDtypeStruct((nr, nc), f32)] * 4,
                          compiler_params=pltpu.CompilerParams(vmem_limit_bytes=VMEM_LIMIT))(w, g8, m, v)


def _pack(vectors):
    rows = []
    for a in vectors:
        flat = a.reshape(-1).astype(f32)
        pad = (-flat.shape[0]) % LANES
        rows.append(jnp.pad(flat, (0, pad)).reshape(-1, LANES))
    packed = jnp.concatenate(rows, axis=0)
    return jnp.pad(packed, ((0, (-packed.shape[0]) % 8), (0, 0)))


def _unpack(packed, like):
    out, r = [], 0
    for a in like:
        n = a.size
        nr = -(-n // LANES)
        out.append(packed[r:r + nr].reshape(-1)[:n].reshape(a.shape))
        r += nr
    return out


def kernel(x, mem, mix_norm_w, w_in, dn_conv_w, dn_a_log, dn_dt_bias, dn_norm_w, rw_mu, rw_w0, rw_w2, rw_a0, rw_a2, rw_g2, rw_k_k, rw_k_a, rw_r_k, rw_ln_w, rw_ln_b, w_out, xa_norm_w, mem_norm_w, xa_wq, xa_wk, xa_wv, xa_wo, ffn_norm_w, ffn_w1, ffn_w2, final_norm_w, loss_target, m_mix_norm_w, m_w_in, m_dn_conv_w, m_dn_a_log, m_dn_dt_bias, m_dn_norm_w, m_rw_mu, m_rw_w0, m_rw_w2, m_rw_a0, m_rw_a2, m_rw_g2, m_rw_k_k, m_rw_k_a, m_rw_r_k, m_rw_ln_w, m_rw_ln_b, m_w_out, m_xa_norm_w, m_mem_norm_w, m_xa_wq, m_xa_wk, m_xa_wv, m_xa_wo, m_ffn_norm_w, m_ffn_w1, m_ffn_w2, m_final_norm_w, v_mix_norm_w, v_w_in, v_dn_conv_w, v_dn_a_log, v_dn_dt_bias, v_dn_norm_w, v_rw_mu, v_rw_w0, v_rw_w2, v_rw_a0, v_rw_a2, v_rw_g2, v_rw_k_k, v_rw_k_a, v_rw_r_k, v_rw_ln_w, v_rw_ln_b, v_w_out, v_xa_norm_w, v_mem_norm_w, v_xa_wq, v_xa_wk, v_xa_wv, v_xa_wo, v_ffn_norm_w, v_ffn_w1, v_ffn_w2, v_final_norm_w):
    weights = dict(mix_norm_w=mix_norm_w, w_in=w_in, dn_conv_w=dn_conv_w, dn_a_log=dn_a_log, dn_dt_bias=dn_dt_bias, dn_norm_w=dn_norm_w, rw_mu=rw_mu, rw_w0=rw_w0, rw_w2=rw_w2, rw_a0=rw_a0, rw_a2=rw_a2, rw_g2=rw_g2, rw_k_k=rw_k_k, rw_k_a=rw_k_a, rw_r_k=rw_r_k, rw_ln_w=rw_ln_w, rw_ln_b=rw_ln_b, w_out=w_out, xa_norm_w=xa_norm_w, mem_norm_w=mem_norm_w, xa_wq=xa_wq, xa_wk=xa_wk, xa_wv=xa_wv, xa_wo=xa_wo, ffn_norm_w=ffn_norm_w, ffn_w1=ffn_w1, ffn_w2=ffn_w2, final_norm_w=final_norm_w)
    mom_m = dict(mix_norm_w=m_mix_norm_w, w_in=m_w_in, dn_conv_w=m_dn_conv_w, dn_a_log=m_dn_a_log, dn_dt_bias=m_dn_dt_bias, dn_norm_w=m_dn_norm_w, rw_mu=m_rw_mu, rw_w0=m_rw_w0, rw_w2=m_rw_w2, rw_a0=m_rw_a0, rw_a2=m_rw_a2, rw_g2=m_rw_g2, rw_k_k=m_rw_k_k, rw_k_a=m_rw_k_a, rw_r_k=m_rw_r_k, rw_ln_w=m_rw_ln_w, rw_ln_b=m_rw_ln_b, w_out=m_w_out, xa_norm_w=m_xa_norm_w, mem_norm_w=m_mem_norm_w, xa_wq=m_xa_wq, xa_wk=m_xa_wk, xa_wv=m_xa_wv, xa_wo=m_xa_wo, ffn_norm_w=m_ffn_norm_w, ffn_w1=m_ffn_w1, ffn_w2=m_ffn_w2, final_norm_w=m_final_norm_w)
    mom_v = dict(mix_norm_w=v_mix_norm_w, w_in=v_w_in, dn_conv_w=v_dn_conv_w, dn_a_log=v_dn_a_log, dn_dt_bias=v_dn_dt_bias, dn_norm_w=v_dn_norm_w, rw_mu=v_rw_mu, rw_w0=v_rw_w0, rw_w2=v_rw_w2, rw_a0=v_rw_a0, rw_a2=v_rw_a2, rw_g2=v_rw_g2, rw_k_k=v_rw_k_k, rw_k_a=v_rw_k_a, rw_r_k=v_rw_r_k, rw_ln_w=v_rw_ln_w, rw_ln_b=v_rw_ln_b, w_out=v_w_out, xa_norm_w=v_xa_norm_w, mem_norm_w=v_mem_norm_w, xa_wq=v_xa_wq, xa_wk=v_xa_wk, xa_wv=v_xa_wv, xa_wo=v_xa_wo, ffn_norm_w=v_ffn_norm_w, ffn_w1=v_ffn_w1, ffn_w2=v_ffn_w2, final_norm_w=v_final_norm_w)
    names = list(weights)

    seq, d = x.shape[1], x.shape[2]
    dnw = d // 2
    rww = d - dnw
    nh, nb = dnw // LANES, rww // LANES
    n_mem = mem.shape[1]
    lw_dim, la_dim, lg_dim = rw_w2.shape[1], rw_a2.shape[1], rw_g2.shape[1]
    assert lw_dim + la_dim == LANES and lg_dim == LANES and dnw % LANES == 0 and rww % LANES == 0
    xs, mems, tgt = x[0], mem[0], loss_target[0]

    col_sharded = ("w_in", "xa_wo", "ffn_w1", "dn_conv_w", "rw_w2", "rw_a2", "rw_g2")
    row_sharded = ("w_out", "xa_wq", "xa_wk", "xa_wv", "ffn_w2")
    f32_payload = ("dn_conv_w", "rw_w2", "rw_a2", "rw_g2")
    sharded = col_sharded + row_sharded
    payload = {n: weights[n][0].astype(f32 if n in f32_payload else bf16) for n in sharded}
    shard_w = w_in.shape[2]
    pad_w = -(-shard_w // LANES) * LANES
    first = ("w_in", "dn_conv_w", "rw_w2", "rw_a2", "rw_g2")
    mid = ("w_out", "xa_wq", "xa_wk", "xa_wv", "xa_wo")
    late = ("ffn_w1", "ffn_w2")
    gathered = dict(zip(first, gather_chips_halves("gather_first", [payload[n] for n in first])))
    ordered = lax.optimization_barrier(([gathered[n] for n in first], [payload[n] for n in mid + late]))
    gathered = dict(zip(first, ordered[0]))
    payload.update(zip(mid + late, ordered[1]))
    mid_state, tok_mid = chips_start("gather_mid_start", [payload[n] for n in mid], _whole, halves=True)
    late_state, tok_late = chips_start("gather_late_start", [payload[n] for n in late], _whole, halves=True)
    mix_norm_w_in = mix_norm_w + (tok_mid + tok_late)

    def full(n):
        g = gathered[n]
        if n in col_sharded:
            return g.transpose(1, 0, 2).reshape(g.shape[1], 4 * g.shape[2])
        return g.reshape(4 * g.shape[1], g.shape[2])

    c_rw0 = 4 * dnw + 2 * nh
    c_rw = 3 * rww
    eb_ab, eb_r = 4 * nh, 4 * nh + 1
    eb_l1 = eb_r + 3 * nb
    n_ext = -(-(eb_l1 + 2) // 4) * 4
    tbl_fwd, tbl_bwd, per_shard = w_in_layout(4 * shard_w, shard_w, c_rw0, LANES - 2 * nh, n_ext)
    sub_f = max(s for s in range(1, REGROUP_SUB + 1) if n_ext % s == 0)
    sub_b = max(s for s in range(1, REGROUP_SUB + 1) if per_shard % s == 0)
    w_ext = lane_regroup(
        "w_in_regroup", gathered["w_in"], tbl_fwd,
        lambda p, b: pl.BlockSpec(
            (None, d, LANES), lambda i, j, t: (t[REGROUP_FIELDS * p, j * sub_f + b], 0,
                                               t[REGROUP_FIELDS * p + 1, j * sub_f + b])),
        pl.BlockSpec((d, sub_f * LANES), lambda i, j, t: (0, j)), (d, n_ext * LANES), (1, n_ext // sub_f), shard_w,
        sub_f)
    conv_w = full("dn_conv_w")
    w2p = jnp.concatenate([full("rw_w2"), jnp.zeros((la_dim, rww), f32)], axis=0)
    a2p = jnp.concatenate([jnp.zeros((lw_dim, rww), f32), full("rw_a2")], axis=0)
    g2 = full("rw_g2")
    xaw = xa_wq.shape[2]
    nxh = xaw // LANES
    ffn = 4 * ffn_w1.shape[2]

    def lane_row(vec):
        return jnp.pad(vec.reshape(1, -1), ((0, 0), (0, LANES - vec.size)))

    alog_row, dtb_row = lane_row(dn_a_log), lane_row(dn_dt_bias)
    head_of_col = jnp.arange(dnw)[None, :] // LANES
    e_g = (jnp.arange(LANES)[:, None] == head_of_col).astype(f32)
    e_b = (jnp.arange(LANES)[:, None] == head_of_col + nh).astype(f32)
    mu_main, mu_small = rw_mu[:, :c_rw], rw_mu[:, c_rw:]
    r_k_row = rw_r_k.reshape(1, rww)
    fnw = final_norm_w.reshape(1, d)
    qb, kb_, vb, zb = 0, nh, 2 * nh, 3 * nh
    rb0 = eb_r
    cc = lambda w_, o_: ("constc", w_, o_)
    rc = lambda o_: ("rowc", LANES, o_)

    (u16,) = rowcall("norm_mix", fn_rms, [(xs, "row"), (mix_norm_w_in, "const")], [(seq, d, bf16, "row")], rows=seq)
    p_main = p_small = mm("in_proj", u16, w_ext)

    fn_gconv = make_fn_gconv(2 * nh)
    (qkv,) = rowcall("gdn_conv", fn_gconv, [(p_main, rc(0)), (conv_w, cc(LANES, 0))],
                     [(seq, 3 * dnw, f32, "rowc")], rows=seq, tile=seq, ncol=3 * nh)
    gate_ins = [(p_small, rc(eb_ab)), (alog_row, "const"), (dtb_row, "const"), (e_g, "const"), (e_b, "const")]
    g_b, beta_b = rowcall("gdn_gate", fn_ggate, gate_ins, [(seq, dnw, f32, "row")] * 2, rows=seq)
    gdn_ins = [(qkv, qb), (qkv, kb_), (qkv, vb), (g_b, 0), (beta_b, 0)]
    o_raw, gdn_states, gdn_kept = scan_fwd("gdn_scan", gdn_chunk, gdn_ins, rows=seq, chunk=GDN_CHUNK, ncol=nh)
    mid_state, tok = handover_start("gather_mid_pass", chips_wait("gather_mid_wait", mid_state, o_raw))
    gpost_ins = [(o_raw, rc(0)), (p_main, rc(zb)), (dn_norm_w + tok, "const")]
    (o_dn,) = rowcall("gdn_post", fn_gpost, gpost_ins, [(seq, dnw, bf16, "rowc")], rows=seq, ncol=nh)

    (prw,) = rowcall("rw_lerp_main", fn_lerp, [(p_main, rc(rb0)), (mu_main, cc(LANES, 0))],
                     [(seq, c_rw, f32, "rowc")], rows=seq, tile=seq, ncol=3 * nb)
    (psl,) = rowcall("rw_lerp_small", fn_lerp, [(p_small, rc(eb_l1)), (mu_small, cc(LANES, 0))],
                     [(seq, 2 * LANES, f32, "rowc")], rows=seq, tile=seq, ncol=2)
    rprep_ins = [(prw, rc(nb)), (psl, "row"), (rw_w0, cc(LANES, 0)), (rw_a0, cc(LANES, 0)), (rw_k_k, cc(LANES, 0)),
                 (rw_k_a, cc(LANES, 0)), (w2p, cc(LANES, 0)), (a2p, cc(LANES, 0)), (g2, cc(LANES, 0))]
    lw, kmod, kk, a_rw, gate = rowcall("rw_prep", fn_rprep, rprep_ins, [(seq, rww, f32, "rowc")] * 5,
                                        rows=seq, ncol=nb)
    wkv_ins = [(prw, 0), (lw, 0), (kmod, 0), (prw, 2 * nb), (kk, 0), (a_rw, 0)]
    y_rw, wkv_states, wkv_kept = scan_fwd("wkv_scan", wkv_chunk, wkv_ins, rows=seq, chunk=WKV_CHUNK, ncol=nb)
    rpost_ins = [(y_rw, rc(0)), (prw, rc(0)), (kmod, rc(0)), (prw, rc(2 * nb)), (gate, rc(0)),
                 (r_k_row, cc(LANES, 0)), (rw_ln_w, cc(LANES, 0)), (rw_ln_b, cc(LANES, 0))]
    (o_rw,) = rowcall("rw_post", fn_rpost, rpost_ins, [(seq, rww, bf16, "rowc")], rows=seq, ncol=nb)

    o_cat = jnp.concatenate([o_dn, o_rw], axis=1)
    late_state, tok = handover_start("gather_late_pass", chips_wait("gather_late_wait", late_state, o_cat))
    gathered.update(zip(mid, chips_wait("gather_mid_pass_wait", mid_state, o_cat)))
    w_out_f, wq_f, wk_f, wv_f, wo_f = full("w_out"), full("xa_wq"), full("xa_wk"), full("xa_wv"), full("xa_wo")
    h1 = mm("out_proj", o_cat, w_out_f, add=xs)

    (hn16,) = rowcall("norm_xa", fn_rms, [(h1, "row"), (xa_norm_w + tok, "const")], [(seq, d, bf16, "row")],
                      rows=seq)
    (mn16,) = rowcall("norm_mem", fn_rms, [(mems, "row"), (mem_norm_w, "const")], [(n_mem, d, bf16, "row")],
                      rows=n_mem)
    q_xa = mm("xa_q", hn16, wq_f)
    k_xa = mm("xa_k", mn16, wk_f)
    v_xa = mm("xa_v", mn16, wv_f)
    xcore_ins = [(q_xa, rc(0)), (k_xa, cc(LANES, 0)), (v_xa, cc(LANES, 0))]
    (o_xa,) = rowcall("xa_core", fn_xcore, xcore_ins, [(seq, xaw, bf16, "rowc")], rows=seq, ncol=nxh)
    h2 = mm("xa_o", o_xa, wo_f, add=h1)

    (fn16,) = rowcall("norm_ffn", fn_rms, [(h2, "row"), (ffn_norm_w, "const")], [(seq, d, bf16, "row")], rows=seq)
    gathered.update(zip(late, chips_wait("gather_late_pass_wait", late_state, fn16)))
    w1_f, w2_f = full("ffn_w1"), full("ffn_w2")
    a1_16, hid16 = mm("ffn_up", fn16, w1_f, epilogue=lambda r: (r, jnp.square(jnp.maximum(r, 0.0))),
                      out_dtypes=(bf16, bf16))
    h3 = mm("ffn_down", hid16, w2_f, add=h2)

    dh3, dh3_16, d_fnw, loss_rows = rowcall(
        "loss_head", fn_final, [(h3, "row"), (tgt, "row"), (fnw, "const")],
        [(seq, d, f32, "row"), (seq, d, bf16, "row"), (1, d, f32, "acc"), (8, LANES, f32, "acc")], rows=seq)

    da1_16 = mm("ffn_down_dx", dh3_16, w2_f, tb=True, extra=[a1_16], out_dtype=bf16,
                epilogue=lambda r, a1: (r * (2.0 * jnp.maximum(a1.astype(f32), 0.0)),))
    def by_chip(n, g):
        if g.ndim == 3:
            return g
        if n in col_sharded:
            return g.reshape(g.shape[0], 4, g.shape[1] // 4).transpose(1, 0, 2)
        return g.reshape(4, g.shape[0] // 4, g.shape[1])

    g_ffn_w2 = mm("ffn_down_dw", hid16, dh3_16, ta=True, out_dtype=bf16)
    ffn_w2_g, tok = chips_start("scatter_ffn_w2_start", [by_chip("ffn_w2", g_ffn_w2)], _chip_block)
    g_ffn_w1 = mm("ffn_up_dw", fn16, da1_16, ta=True, out_dtype=bf16, by_chip=True, after=tok)
    ffn_w1_g, tok = chips_start("scatter_ffn_w1_start", [g_ffn_w1], _chip_block)
    dfn = mm("ffn_up_dx", da1_16, w1_f, tb=True, after=tok)
    dh2, d_ffn_nw, dh2_16 = rowvjp("norm_ffn_bwd", fn_rms, [(h2, "row"), (ffn_norm_w, "const")],
                                   [[(dfn, "row")]], [0, 1], rows=seq, adds=[(0, dh3, "row")], dup16=[0])

    do_xa = mm("xa_o_dx", dh2_16, wo_f, tb=True)
    g_xa_wo = mm("xa_o_dw", o_xa, dh2_16, ta=True, out_dtype=bf16, by_chip=True)
    dq_xa, dk_xa, dv_xa, dq16 = rowvjp("xa_core_bwd", fn_xcore, xcore_ins, [[(do_xa, rc(0))]], [0, 1, 2],
                                       rows=seq, ncol=nxh, dup16=[0])
    g_xa_wq = mm("xa_q_dw", hn16, dq16, ta=True, out_dtype=bf16)
    dhn = mm("xa_q_dx", dq16, wq_f, tb=True)
    dh1, d_xa_nw, dh1_16 = rowvjp("norm_xa_bwd", fn_rms, [(h1, "row"), (xa_norm_w, "const")], [[(dhn, "row")]],
                                  [0, 1], rows=seq, adds=[(0, dh2, "row")], dup16=[0])
    dk16, dv16 = dk_xa.astype(bf16), dv_xa.astype(bf16)
    g_xa_wk = mm("xa_k_dw", mn16, dk16, ta=True, out_dtype=bf16)
    g_xa_wv = mm("xa_v_dw", mn16, dv16, ta=True, out_dtype=bf16)
    dmn = mm("xa_v_dx", dv16, wv_f, tb=True, add=mm("xa_k_dx", dk16, wk_f, tb=True))
    (d_mem_nw,) = rowvjp("norm_mem_bwd", fn_rms, [(mems, "row"), (mem_norm_w, "const")], [[(dmn, "row")]], [1],
                         rows=n_mem)

    g_w_out = mm("out_proj_dw", o_cat, dh1_16, ta=True, out_dtype=bf16)
    mid_grads = dict(w_out=g_w_out, xa_wq=g_xa_wq, xa_wk=g_xa_wk, xa_wv=g_xa_wv, xa_wo=g_xa_wo)
    mid_g, tok = chips_start("scatter_mid_start", [by_chip(n, mid_grads[n]) for n in mid], _chip_block)
    do_cat = mm("out_proj_dx", dh1_16, w_out_f, tb=True, after=tok)

    dy, dr_a, dkmod_a, dv_a, dgate, d_r_k, d_ln_w, d_ln_b = rowvjp(
        "rw_post_bwd", fn_rpost, rpost_ins, [[(do_cat, rc(nh))]], [0, 1, 2, 3, 4, 5, 6, 7], rows=seq, ncol=nb)
    dr_b, dlw, dkmod_b, dv_b, dkk, da_rw = scan_bwd("wkv_scan_bwd", wkv_chunk, wkv_ins, wkv_states, wkv_kept, dy, 0,
                                                    rows=seq, chunk=WKV_CHUNK, ncol=nb)
    rprep_cts = [[(dlw, rc(0))], [(dkmod_a, rc(0)), (dkmod_b, rc(0))], [(dkk, rc(0))], [(da_rw, rc(0))],
                 [(dgate, rc(0))]]
    dpk, dpsl_parts, d_w0, d_a0, d_k_k, d_k_a, d_w2p, d_a2p, d_g2 = rowvjp(
        "rw_prep_bwd", fn_rprep, rprep_ins, rprep_cts, [0, 1, 2, 3, 4, 5, 6, 7, 8], rows=seq, ncol=nb)
    (dpsl,) = rowcall("rw_prep_sum", fn_sumcols(nb), [(dpsl_parts, "row")], [(seq, 2 * LANES, f32, "row")], rows=seq)

    def lerp_bwd(tag, p, p_off, mu, mu_off, ct_lists, ncol):
        return rowvjp("rw_lerp_bwd_" + tag, fn_lerp, [(p, rc(p_off)), (mu, cc(LANES, mu_off))], [ct_lists], [0, 1],
                      rows=seq, tile=seq, ncol=ncol, dup16=[0])

    _, dmu_r, dpr16 = lerp_bwd("r", p_main, rb0, mu_main, 0, [(dr_a, rc(0)), (dr_b, rc(0))], nb)
    _, dmu_k, dpk16 = lerp_bwd("k", p_main, rb0 + nb, mu_main, nb, [(dpk, rc(0))], nb)
    _, dmu_v, dpv16 = lerp_bwd("v", p_main, rb0 + 2 * nb, mu_main, 2 * nb, [(dv_a, rc(0)), (dv_b, rc(0))], nb)
    _, dmu_s, dps12_16 = lerp_bwd("small", p_small, eb_l1, mu_small, 0, [(dpsl, rc(0))], 2)

    do_raw, dz, d_dn_nw, dz16 = rowvjp("gdn_post_bwd", fn_gpost, gpost_ins, [[(do_cat, rc(0))]], [0, 1, 2],
                                       rows=seq, ncol=nh, dup16=[1])
    dq_g, dk_g, dv_g, dg_b, dbeta_b = scan_bwd("gdn_scan_bwd", gdn_chunk, gdn_ins, gdn_states, gdn_kept, do_raw, 0,
                                               rows=seq, chunk=GDN_CHUNK, ncol=nh)
    dps0, d_alog, d_dtb, dps0_16 = rowvjp("gdn_gate_bwd", fn_ggate, gate_ins, [[(dg_b, "row")], [(dbeta_b, "row")]],
                                          [0, 1, 2], rows=seq, dup16=[0])
    dqkv = jnp.concatenate([dq_g, dk_g, dv_g], axis=1)
    _, d_conv_w, dqkv16 = rowvjp("gdn_conv_bwd", fn_gconv, [(p_main, rc(0)), (conv_w, cc(LANES, 0))],
                                 [[(dqkv, rc(0))]], [0, 1], rows=seq, tile=seq, ncol=3 * nh, dup16=[0])

    dp16 = jnp.concatenate([dqkv16, dz16, dps0_16, dpr16, dpk16, dpv16, dps12_16,
                            jnp.zeros((seq, (n_ext - eb_l1 - 2) * LANES), bf16)], axis=1)
    g_w_ext = mm("in_proj_dw", u16, dp16, ta=True, out_dtype=bf16)
    g_w_in = lane_regroup(
        "w_in_grad_regroup", g_w_ext, tbl_bwd,
        lambda p, b: pl.BlockSpec((d, LANES), lambda i, j, t: (0, t[REGROUP_FIELDS * p + 1, j * sub_b + b])),
        pl.BlockSpec((None, d, sub_b * LANES),
                     lambda i, j, t: (j // (per_shard // sub_b), 0, j % (per_shard // sub_b))),
        (4, d, pad_w), (1, 4 * per_shard // sub_b), n_ext * LANES, sub_b)
    first_grads = dict(w_in=g_w_in, dn_conv_w=d_conv_w, rw_w2=d_w2p[:lw_dim], rw_a2=d_a2p[lw_dim:], rw_g2=d_g2)
    first_g, tok = chips_start("scatter_first_start", [by_chip(n, first_grads[n]) for n in first], _chip_block)

    du = mm("in_proj_dx", dp16, w_ext, tb=True, after=tok)
    grad_x, d_mix_nw = rowvjp("norm_mix_bwd", fn_rms, [(xs, "row"), (mix_norm_w, "const")], [[(du, "row")]],
                              [0, 1], rows=seq, adds=[(0, dh1, "row")])
    received, sent = {}, {}

    def arrived(group, name, state, after):
        lands, sources = chips_wait(name, state, after, with_sources=True)
        received.update(zip(group, lands))
        sent.update(zip(group, sources))

    arrived(mid, "scatter_mid_wait", mid_g, grad_x)
    arrived(("ffn_w2",), "scatter_ffn_w2_wait", ffn_w2_g, grad_x)
    arrived(("ffn_w1",), "scatter_ffn_w1_wait", ffn_w1_g, grad_x)

    out = {}

    def reduce_and_update(tag, group):
        partial = [sum_slots("sum_chips_" + n, received[n], sent[n]) for n in group]
        other = swap_sibling("swap_sibling_" + tag, partial)
        for n, p_mine, p_other in zip(group, partial, other):
            if weights[n].shape[2] % LANES:
                rows, cols = weights[n].shape[1:]
                lin = lambda a: jnp.swapaxes(a, 1, 2).reshape(-1, LANES)
                lin_g = lambda p: p.T[:cols].reshape(-1, LANES)
                res = adamw("adamw_" + n, lin(weights[n]), [lin_g(p_mine), lin_g(p_other)], lin(mom_m[n]),
                            lin(mom_v[n]))
                out[n] = [jnp.swapaxes(r.reshape(1, cols, rows), 1, 2) for r in res]
            else:
                out[n] = adamw("adamw_" + n, weights[n], [p_mine, p_other], mom_m[n], mom_v[n])

    reduce_and_update("rest", mid + late)
    arrived(first, "scatter_first_wait", first_g, [out[n][1] for n in mid + late])
    reduce_and_update("first", first)

    small_names = [n for n in names if n not in sharded]
    small_local = dict(
        mix_norm_w=d_mix_nw, dn_a_log=d_alog[:, :nh], dn_dt_bias=d_dtb[:, :nh], dn_norm_w=d_dn_nw,
        rw_mu=jnp.concatenate([dmu_r, dmu_k, dmu_v, dmu_s], axis=1), rw_w0=d_w0, rw_a0=d_a0, rw_k_k=d_k_k,
        rw_k_a=d_k_a, rw_r_k=d_r_k, rw_ln_w=d_ln_w, rw_ln_b=d_ln_b, xa_norm_w=d_xa_nw, mem_norm_w=d_mem_nw,
        ffn_norm_w=d_ffn_nw, final_norm_w=d_fnw)
    loss_vec = jnp.where(jnp.arange(LANES) == 0, loss_rows[0], 0.0)
    (g8,) = gather_all([_pack([small_local[n] for n in small_names] + [loss_vec])], after=[out["w_in"][1]])

    packed_like = [weights[n] for n in small_names] + [loss_vec]
    zero = jnp.zeros((LANES,), f32)
    res = adamw_packed(_pack([weights[n] for n in small_names] + [zero]), g8,
                       _pack([mom_m[n] for n in small_names] + [zero]),
                       _pack([mom_v[n] for n in small_names] + [zero]))
    unpacked = [_unpack(r, packed_like) for r in res]
    for i, n in enumerate(small_names):
        out[n] = [u[i] for u in unpacked]
    loss = unpacked[0][-1][0]

    return (loss, grad_x.reshape(x.shape), *[out[n][0] for n in names], *[out[n][1] for n in names],
            *[out[n][2] for n in names], *[out[n][3] for n in names])
```

```python
import functools

import jax
import jax.numpy as jnp
from jax import lax
from jax.experimental import pallas as pl
from jax.experimental.pallas import tpu as pltpu

f32 = jnp.float32
bf16 = jnp.bfloat16
HI = lax.Precision.HIGHEST
MESH = pl.DeviceIdType.MESH

LANES = 128
VMEM_LIMIT = 56 << 20
TOK_TILE = 256
TOK_TILE_BLOCKED = 1024
MM_TILE = 1024
MM_TILE_K = 2048
MM_WHOLE_K_BYTES = 55 << 20
GDN_CHUNK = 128
WKV_CHUNK = 64
SCAN_GROUP = 8
P_BULK = 1
P_INV = 1
P_RESID = 3
P_CUMSUM = 3
RMS_EPS = 1e-6
RW_GN_EPS = 64e-5
RW_HEAD = 64

ADAM_LR, ADAM_B1, ADAM_B2, ADAM_EPS, ADAM_WD, ADAM_STEP = 0.001, 0.9, 0.999, 1e-08, 0.01, 10


def _params(n_grid):
    return pltpu.CompilerParams(dimension_semantics=("arbitrary",) * n_grid, vmem_limit_bytes=VMEM_LIMIT)


_DIMS = {"nn": (((1,), (0,)), ((), ())), "nt": (((1,), (1,)), ((), ())), "tn": (((0,), (0,)), ((), ()))}
_DIMS_BATCHED = {"nn": (((2,), (1,)), ((0,), (0,))), "nt": (((2,), (2,)), ((0,), (0,))),
                 "tn": (((1,), (1,)), ((0,), (0,)))}


def _raw_dot(a, b, mode, passes):
    dims = (_DIMS if a.ndim == 2 else _DIMS_BATCHED)[mode]
    if passes == 6:
        return lax.dot_general(a.astype(f32), b.astype(f32), dims, precision=HI, preferred_element_type=f32)
    ah, bh = a.astype(bf16), b.astype(bf16)
    r = lax.dot_general(ah, bh, dims, preferred_element_type=f32)
    if passes == 3:
        al = (a - ah.astype(f32)).astype(bf16)
        bl = (b - bh.astype(f32)).astype(bf16)
        r = r + lax.dot_general(al, bh, dims, preferred_element_type=f32)
        r = r + lax.dot_general(ah, bl, dims, preferred_element_type=f32)
    return r


@functools.partial(jax.custom_vjp, nondiff_argnums=(2, 3))
def pdot(a, b, mode, passes):
    return _raw_dot(a, b, mode, passes)


def _pdot_bwd(mode, passes, res, g):
    a, b = res
    if mode == "nn":
        da, db = _raw_dot(g, b, "nt", passes), _raw_dot(a, g, "tn", passes)
    elif mode == "nt":
        da, db = _raw_dot(g, b, "nn", passes), _raw_dot(g, a, "tn", passes)
    else:
        da, db = _raw_dot(b, g, "nt", passes), _raw_dot(a, g, "nn", passes)
    return da.astype(a.dtype), db.astype(b.dtype)


pdot.defvjp(lambda a, b, mode, passes: (_raw_dot(a, b, mode, passes), (a, b)), _pdot_bwd)


def bdot(a, b):
    return pdot(a, b, "nn", 1)


def bdot_nt(a, b):
    return pdot(a, b, "nt", 1)


def _shift_rows(x, k):
    row = lax.broadcasted_iota(jnp.int32, x.shape, 0)
    return jnp.where(row < k, 0.0, pltpu.roll(x, k, axis=0))


def _unshift_rows(g, k):
    n = g.shape[0]
    row = lax.broadcasted_iota(jnp.int32, g.shape, 0)
    return jnp.where(row >= n - k, 0.0, pltpu.roll(g, n - k, axis=0))


@functools.partial(jax.custom_vjp, nondiff_argnums=(1,))
def tshift(x, k):
    return _shift_rows(x, k)


tshift.defvjp(lambda x, k: (_shift_rows(x, k), None), lambda k, _, g: (_unshift_rows(g, k),))


def rms(x, w):
    x = x.astype(f32)
    return x * lax.rsqrt(jnp.mean(x * x, axis=-1, keepdims=True) + RMS_EPS) * w


def softplus(x):
    return jnp.maximum(x, 0.0) + jnp.log(1.0 + jnp.exp(-jnp.abs(x)))


def seg2sum(x):
    lo = lax.broadcasted_iota(jnp.int32, x.shape, 1) < RW_HEAD
    s_lo = jnp.sum(jnp.where(lo, x, 0.0), axis=-1, keepdims=True)
    s_hi = jnp.sum(jnp.where(lo, 0.0, x), axis=-1, keepdims=True)
    return jnp.where(lo, s_lo, s_hi)


def _tile(n, pref):
    if n <= pref:
        return n
    t = pref
    while t >= LANES:
        if n % t == 0:
            return t
        t -= LANES
    return n


def mm(name, a, b, *, ta=False, tb=False, add=None, out_dtype=f32, by_chip=False, epilogue=None, extra=(),
       out_dtypes=None, after=None):
    (k, m) = a.shape if ta else a.shape[::-1]
    (n, kb) = b.shape if tb else b.shape[::-1]
    assert k == kb, (name, a.shape, b.shape)
    tm, tk = _tile(m, MM_TILE), _tile(k, MM_TILE_K)
    tn = _tile(n // 4, MM_TILE) if by_chip else _tile(n, MM_TILE)
    vmem = VMEM_LIMIT
    narrow = _tile(n, MM_TILE // 2)
    whole_k = 2 * 2 * (tm + narrow) * k + 2 * 4 * tm * narrow * (1 + len(extra) + (add is not None))
    if k > tk and not by_chip and whole_k <= MM_WHOLE_K_BYTES:
        tn, tk, vmem = narrow, k, MM_WHOLE_K_BYTES + (7 << 20)
    nk = k // tk
    dims = (((0,) if ta else (1,), (1,) if tb else (0,)), ((), ()))
    extra = list(extra) + ([] if add is None else [add])
    out_dtypes = [out_dtype] if out_dtypes is None else list(out_dtypes)
    n_extra, n_out = len(extra), len(out_dtypes)
    n_after = 0 if after is None else 1

    def body(*refs):
        a_ref, b_ref = refs[:2]
        x_refs = refs[2:2 + n_extra]
        o_refs = refs[2 + n_extra + n_after:2 + n_extra + n_after + n_out]
        part = lax.dot_general(a_ref[...].astype(bf16), b_ref[...].astype(bf16), dims, preferred_element_type=f32)

        def finish(r):
            xs = [x[...] for x in x_refs]
            if add is not None:
                r = r + xs.pop().astype(f32)
            outs = (r,) if epilogue is None else epilogue(r, *xs)
            for o_ref, o in zip(o_refs, outs):
                o_ref[...] = o.astype(o_ref.dtype)

        if nk == 1:
            finish(part)
            return
        acc = refs[-1]
        kk = pl.program_id(2)

        @pl.when(kk == 0)
        def _():
            acc[...] = part

        @pl.when(kk > 0)
        def _():
            acc[...] += part

        @pl.when(kk == nk - 1)
        def _():
            finish(acc[...])

    a_spec = pl.BlockSpec((tk, tm), lambda i, j, q: (q, i)) if ta else pl.BlockSpec((tm, tk), lambda i, j, q: (i, q))
    b_spec = pl.BlockSpec((tn, tk), lambda i, j, q: (j, q)) if tb else pl.BlockSpec((tk, tn), lambda i, j, q: (q, j))
    x_spec = pl.BlockSpec((tm, tn), lambda i, j, q: (i, j))
    if by_chip:
        per_chip = n // 4 // tn
        o_spec = pl.BlockSpec((None, tm, tn), lambda i, j, q: (j // per_chip, i, j % per_chip))
        o_shape = (4, m, n // 4)
    else:
        o_spec, o_shape = x_spec, (m, n)
    afters = [] if after is None else [jnp.reshape(after, (1, 1))]
    res = pl.pallas_call(
        body, name=name, grid=(m // tm, n // tn, nk),
        in_specs=[a_spec, b_spec] + [x_spec] * n_extra + [pl.BlockSpec(memory_space=pl.ANY)] * n_after,
        out_specs=[o_spec] * n_out, out_shape=[jax.ShapeDtypeStruct(o_shape, dt) for dt in out_dtypes],
        scratch_shapes=[pltpu.VMEM((tm, tn), f32)] if nk > 1 else [],
        compiler_params=pltpu.CompilerParams(dimension_semantics=("arbitrary",) * 3, vmem_limit_bytes=vmem))(
            a, b, *extra, *afters)
    return res[0] if n_out == 1 else res


REGROUP_PIECES = 4


REGROUP_SUB = 5
REGROUP_FIELDS = 5


def _regroup_table(n_out, sources_of):
    import numpy as np
    tbl = np.zeros((REGROUP_FIELDS * REGROUP_PIECES, n_out), np.int32)
    for j in range(n_out):
        groups = sorted(sources_of(j).items())
        assert len(groups) <= REGROUP_PIECES, (j, len(groups))
        for p in range(REGROUP_PIECES):
            if p < len(groups):
                key, lanes = groups[p]
                shifts = {q - s for s, q in lanes}
                qs = sorted(q for _, q in lanes)
                assert len(shifts) == 1 and qs == list(range(qs[0], qs[-1] + 1)), (j, key)
                row = (key[0], key[1], shifts.pop(), qs[0], qs[-1] + 1)
            else:
                row = (0, 0, 0, 0, 0)
            tbl[REGROUP_FIELDS * p:REGROUP_FIELDS * (p + 1), j] = row
    return jnp.asarray(tbl)


def lane_regroup(name, src, table, src_spec, out_spec, out_shape, grid, src_width, sub):
    rows = src.shape[-2]
    n_src = REGROUP_PIECES * sub

    def body(tbl, *refs):
        o_ref = refs[n_src]
        step = pl.program_id(1)

        def moved(b, p):
            j = step * sub + b
            blk, shift, lo, hi = (tbl[REGROUP_FIELDS * p + f, j] for f in range(1, REGROUP_FIELDS))
            x = refs[b * REGROUP_PIECES + p][...]
            if src_width % LANES:
                lane = lax.broadcasted_iota(jnp.int32, (rows, LANES), 1)
                x = jnp.where(lane < src_width - blk * LANES, x, jnp.zeros((), src.dtype))
            pi = lax.broadcasted_iota(jnp.int32, (LANES, LANES), 0)
            qi = lax.broadcasted_iota(jnp.int32, (LANES, LANES), 1)
            sel = jnp.logical_and(qi - pi == shift, jnp.logical_and(qi >= lo, qi < hi))
            return jnp.dot(x, sel.astype(src.dtype), preferred_element_type=f32).astype(o_ref.dtype)

        for b in range(sub):
            lanes = slice(b * LANES, (b + 1) * LANES)
            o_ref[:, lanes] = moved(b, 0)
            for p in range(1, REGROUP_PIECES):
                j = step * sub + b

                @pl.when(tbl[REGROUP_FIELDS * p + 4, j] > tbl[REGROUP_FIELDS * p + 3, j])
                def _(b=b, p=p, lanes=lanes):
                    o_ref[:, lanes] += moved(b, p)

    return pl.pallas_call(
        body, name=name, out_shape=jax.ShapeDtypeStruct(out_shape, src.dtype),
        grid_spec=pltpu.PrefetchScalarGridSpec(
            num_scalar_prefetch=1, grid=grid,
            in_specs=[src_spec(p, b) for b in range(sub) for p in range(REGROUP_PIECES)], out_specs=out_spec),
        compiler_params=_params(2))(table, *[src] * n_src)


def w_in_layout(d_cols, shard_w, c_split, gap, n_blocks):
    def ext_of(c):
        return c if c < c_split else c + gap

    def fwd_sources(j):
        groups = {}
        for q in range(LANES):
            e = j * LANES + q
            c = e if e < c_split else e - gap
            if (c_split <= e < c_split + gap) or c >= d_cols:
                continue
            s, l = divmod(c, shard_w)
            groups.setdefault((s, l // LANES), []).append((l % LANES, q))
        return groups

    per_shard = -(-shard_w // LANES)

    def bwd_sources(j):
        s, b = divmod(j, per_shard)
        groups = {}
        for q in range(LANES):
            l = b * LANES + q
            if l >= shard_w:
                continue
            e = ext_of(s * shard_w + l)
            groups.setdefault((0, e // LANES), []).append((e % LANES, q))
        return groups

    return _regroup_table(n_blocks, fwd_sources), _regroup_table(4 * per_shard, bwd_sources), per_shard


def _in_spec(a, kind, tile):
    if kind == "row":
        return pl.BlockSpec((tile, a.shape[1]), lambda j, i: (i, 0))
    if kind == "const":
        return pl.BlockSpec(a.shape, lambda j, i: (0, 0))
    tag, cw, off = kind
    if tag == "rowc":
        return pl.BlockSpec((tile, cw), lambda j, i: (i, j + off))
    assert tag == "constc", kind
    return pl.BlockSpec((a.shape[0], cw), lambda j, i: (0, j + off))


def rowcall(name, fn, ins, outs, *, rows, tile=None, ncol=1):
    if tile is None:
        tile = min(TOK_TILE_BLOCKED if ncol > 1 else TOK_TILE, rows)
    n_in = len(ins)
    kinds = [o[3] for o in outs]

    def body(*refs):
        j, i = pl.program_id(0), pl.program_id(1)
        res = fn(*[r[...] for r in refs[:n_in]])
        for ref, val, kind in zip(refs[n_in:], res, kinds):
            if kind in ("row", "rowc"):
                ref[...] = val.astype(ref.dtype)
            else:
                first = (i == 0) if kind == "accc" else jnp.logical_and(i == 0, j == 0)

                @pl.when(first)
                def _(ref=ref, val=val):
                    ref[...] = val.astype(ref.dtype)

                @pl.when(jnp.logical_not(first))
                def _(ref=ref, val=val):
                    ref[...] += val.astype(ref.dtype)

    out_shape, out_specs = [], []
    for nr, nc, dtype, kind in outs:
        out_shape.append(jax.ShapeDtypeStruct((nr, nc), dtype))
        if kind == "row":
            out_specs.append(pl.BlockSpec((tile, nc), lambda j, i: (i, 0)))
        elif kind == "rowc":
            out_specs.append(pl.BlockSpec((tile, nc // ncol), lambda j, i: (i, j)))
        elif kind == "acc":
            out_specs.append(pl.BlockSpec((nr, nc), lambda j, i: (0, 0)))
        else:
            out_specs.append(pl.BlockSpec((nr, nc // ncol), lambda j, i: (0, j)))
    return pl.pallas_call(
        body, name=name, grid=(ncol, rows // tile), in_specs=[_in_spec(a, k, tile) for a, k in ins],
        out_specs=out_specs, out_shape=out_shape, compiler_params=_params(2))(*[a for a, _ in ins])


def rowvjp(name, fn, ins, cts, grads, *, rows, tile=None, ncol=1, adds=(), dup16=()):
    n_in = len(ins)
    ct_sizes = [len(c) for c in cts]
    flat_cts = [m for c in cts for m in c]
    n_ct = len(flat_cts)

    def wrapped(*vals):
        xs = list(vals[:n_in])
        gs = vals[n_in:n_in + n_ct]
        extra = vals[n_in + n_ct:]

        def f(*dvars):
            full = list(xs)
            for k, v in zip(grads, dvars):
                full[k] = v
            return fn(*full)

        outs, pull = jax.vjp(f, *[xs[k] for k in grads])
        cot, p = [], 0
        for o, size in zip(outs, ct_sizes):
            g = gs[p].astype(f32)
            for q in range(1, size):
                g = g + gs[p + q].astype(f32)
            cot.append(g.astype(o.dtype))
            p += size
        gv = list(pull(tuple(cot)))
        for (pos, _, _), e in zip(adds, extra):
            gv[pos] = gv[pos] + e.astype(gv[pos].dtype)
        return tuple(gv) + tuple(gv[pos] for pos in dup16)

    outs = []
    for k in grads:
        a, kind = ins[k]
        if kind == "row":
            outs.append((rows, a.shape[1] * ncol, f32, "rowc") if ncol > 1 else (rows, a.shape[1], f32, "row"))
        elif kind == "const":
            outs.append((a.shape[0], a.shape[1], f32, "acc"))
        elif kind[0] == "rowc":
            outs.append((rows, kind[1] * ncol, f32, "rowc"))
        else:
            outs.append((a.shape[0], kind[1] * ncol, f32, "accc"))
    for pos in dup16:
        nr, nc, _, kind = outs[pos]
        outs.append((nr, nc, bf16, kind))
    all_ins = list(ins) + flat_cts + [(a, kind) for _, a, kind in adds]
    return rowcall(name, wrapped, all_ins, outs, rows=rows, tile=tile, ncol=ncol)


def fn_rms(x, w):
    return (rms(x, w),)


def make_fn_gconv(n_norm_blocks):
    def fn(p, cw):
        c = cw[3:4] * p
        for jj in range(3):
            c = c + cw[jj:jj + 1] * tshift(p, 3 - jj)
        s = c * jax.nn.sigmoid(c)
        nrm = s * lax.rsqrt(jnp.sum(s * s, axis=-1, keepdims=True) + 1e-6)
        return (jnp.where(pl.program_id(0) < n_norm_blocks, nrm, s),)
    return fn


def fn_ggate(ps0, alog, dtb, e_g, e_b):
    g = -jnp.exp(alog) * softplus(ps0 + dtb)
    beta = jax.nn.sigmoid(ps0)
    return pdot(g, e_g, "nn", 6), pdot(beta, e_b, "nn", 6)


def fn_gpost(o, z, nw):
    return (rms(o, nw) * (z * jax.nn.sigmoid(z)),)


def fn_lerp(p, mu):
    return (p + (tshift(p, 1) - p) * mu,)


def fn_rprep(pk, psl, w0, a0, k_k, k_a, w2p, a2p, g2):
    g1, g2in = psl[:, :LANES], psl[:, LANES:]
    log_w = -softplus(-(w0 + bdot(jnp.tanh(g1), w2p))) - 0.5
    lw = -jnp.exp(log_w)
    a = jax.nn.sigmoid(a0 + bdot(g1, a2p))
    gate = bdot(jax.nn.sigmoid(g2in), g2)
    kkr = pk * k_k
    kk = kkr / jnp.maximum(jnp.sqrt(seg2sum(kkr * kkr)), 1e-12)
    kmod = pk * (1.0 + (a - 1.0) * k_a)
    return lw, kmod, kk, a, gate


def fn_rpost(y, r, kmod, v, gate, r_k, ln_w, ln_b):
    inv_n = 1.0 / RW_HEAD
    mean = seg2sum(y) * inv_n
    d = y - mean
    var = seg2sum(d * d) * inv_n
    yn = d * lax.rsqrt(var + RW_GN_EPS) * ln_w + ln_b
    bonus = seg2sum(r * kmod * r_k) * v
    return ((yn + bonus) * gate,)


def fn_xcore(q, k, v):
    s = bdot_nt(q, k) * (LANES ** -0.5)
    p = jax.nn.softmax(s, axis=-1)
    return (bdot(p, v),)


def fn_final(h, tgt, w):
    def loss_fn(h, w):
        return 0.5 * jnp.sum(jnp.mean(jnp.square(rms(h, w) - tgt), axis=-1))

    val, (dh, dw) = jax.value_and_grad(loss_fn, argnums=(0, 1))(h, w)
    return dh, dh.astype(bf16), dw, jnp.full((8, LANES), val, f32)


def fn_sumcols(n):
    def fn(x):
        w = x.shape[1] // n
        s = x[:, :w]
        for q in range(1, n):
            s = s + x[:, q * w:(q + 1) * w]
        return (s,)
    return fn


def _tri(c):
    ii = lax.broadcasted_iota(jnp.int32, (c, c), 0)
    jj = lax.broadcasted_iota(jnp.int32, (c, c), 1)
    return ii, jj


def _neumann_raw(m, steps):
    c = m.shape[-1]
    ii, jj = _tri(c)
    eye = (ii == jj).astype(f32)
    t, p = eye + m, m
    for _ in range(steps):
        p = _raw_dot(p, p, "nn", P_INV)
        t = _raw_dot(t, eye + p, "nn", P_INV)
    resid = eye - t + _raw_dot(m, t, "nn", P_RESID)
    return t + _raw_dot(t, resid, "nn", P_INV)


@functools.partial(jax.custom_vjp, nondiff_argnums=(1,))
def _neumann_inverse(m, steps):
    return _neumann_raw(m, steps)


def _neumann_fwd(m, steps):
    t = _neumann_raw(m, steps)
    return t, t


def _neumann_bwd(steps, t, g):
    return (_raw_dot(_raw_dot(t, g, "tn", P_RESID), t, "nt", P_RESID),)


_neumann_inverse.defvjp(_neumann_fwd, _neumann_bwd)


@jax.custom_vjp
def _known_inverse(m, t):
    return t


_known_inverse.defvjp(lambda m, t: (t, t), lambda t, g: (_neumann_bwd(0, t, g)[0], jnp.zeros_like(t)))


def _inverse(m, steps, kept):
    return _neumann_inverse(m, steps) if kept is None else _known_inverse(m, kept)


def cumsum_rows(x):
    t = x.shape[1]
    ii, jj = _tri(t)
    tri = jnp.broadcast_to((ii >= jj).astype(f32), (x.shape[0], t, t))
    return pdot(tri, x, "nn", P_CUMSUM)


def gdn_chunk(q, k, v, gb, bb, s, kept=None):
    c = q.shape[1]
    ii, jj = _tri(c)
    low = ii >= jj
    gcb = cumsum_rows(gb)
    gl = jnp.sum(gb, axis=1, keepdims=True)
    gc_col = gcb[:, :, :c]
    diff = gc_col - jnp.swapaxes(gc_col, 1, 2)
    decay = jnp.where(low, jnp.exp(jnp.where(low, diff, 0.0)), 0.0)
    qs = q * (q.shape[2] ** -0.5)
    kb = k * bb
    with_k = pdot(jnp.concatenate([kb, qs], axis=1), k, "nt", P_BULK)
    a = jnp.where(ii > jj, with_k[:, :c] * decay, 0.0)
    attn = with_k[:, c:] * decay
    t = _inverse(-a, c.bit_length() - 2, kept)
    eg = jnp.exp(gcb)
    uw = pdot(t, jnp.concatenate([v * bb, kb * eg], axis=2), "nn", P_BULK)
    u, w = uw[:, :, :LANES], uw[:, :, LANES:]
    kd = k * jnp.exp(gl - gcb)
    from_state = pdot(jnp.concatenate([w, qs * eg], axis=1), s, "nn", P_BULK)
    v_new = u - from_state[:, :c]
    o = from_state[:, c:] + pdot(attn, v_new, "nn", P_BULK)
    s_new = s * jnp.exp(gl) + pdot(kd, v_new, "tn", P_BULK)
    return o, s_new, t


def wkv_chunk(r, lw, k, v, kk, a, s, kept=None):
    t = r.shape[1]
    ii, jj = _tri(t)
    lo = lax.broadcasted_iota(jnp.int32, r.shape, 2) < RW_HEAD
    cl = cumsum_rows(lw)
    cl_last = jnp.sum(lw, axis=1, keepdims=True)
    al = -kk * jnp.exp(cl - lw)
    be = (a * kk) * jnp.exp(-cl)
    kt = k * jnp.exp(-cl)
    rt = r * jnp.exp(cl)

    def dot(xa, xb, mode="nn"):
        return pdot(xa, xb, mode, P_BULK)

    def sel(x_lo, x_hi):
        return jnp.where(lo, x_lo, x_hi)

    left = jnp.concatenate([jnp.where(lo, al, 0.0), jnp.where(lo, 0.0, al),
                            jnp.where(lo, rt, 0.0), jnp.where(lo, 0.0, rt)], axis=1)
    with_be, with_kt = dot(left, be, "nt"), dot(left, kt, "nt")

    def blocks(prod):
        below, upto = ii > jj, ii >= jj
        return [jnp.where(msk, prod[:, q * t:(q + 1) * t], 0.0) for q, msk in enumerate((below, below, upto, upto))]

    ab_lo, ab_hi, rb_lo, rb_hi = blocks(with_be)
    ak_lo, ak_hi, rk_lo, rk_hi = blocks(with_kt)
    from_state = dot(jnp.concatenate([al, rt], axis=1), s, "nt")
    x = from_state[:, :t] + sel(dot(ak_lo, v), dot(ak_hi, v))
    steps = t.bit_length() - 2
    inv_lo = _inverse(ab_lo, steps, None if kept is None else kept[:, :t])
    inv_hi = _inverse(ab_hi, steps, None if kept is None else kept[:, t:])
    u = sel(dot(inv_lo, x), dot(inv_hi, x))
    y = from_state[:, t:] + sel(dot(rb_lo, u) + dot(rk_lo, v), dot(rb_hi, u) + dot(rk_hi, v))
    vi = lax.broadcasted_iota(jnp.int32, s.shape, 1) < RW_HEAD
    ki = lax.broadcasted_iota(jnp.int32, s.shape, 2) < RW_HEAD
    s_new = jnp.where(vi == ki, (s + dot(u, be, "tn") + dot(v, kt, "tn")) * jnp.exp(cl_last), 0.0)
    return y, s_new, jnp.concatenate([inv_lo, inv_hi], axis=1)


def _scan_group(ncol, offs):
    g = SCAN_GROUP
    while g > 1 and (ncol % g or any(o % g for o in offs)):
        g //= 2
    return g


def scan_fwd(name, chunk_fn, ins, *, rows, chunk, ncol):
    n = rows // chunk
    n_in = len(ins)
    grp = _scan_group(ncol, [off for _, off in ins])
    tile = jax.ShapeDtypeStruct((grp, chunk, LANES), f32)
    kept = jax.eval_shape(chunk_fn, *[tile] * n_in, jax.ShapeDtypeStruct((grp, LANES, LANES), f32))[2].shape[1:]

    def body(*refs):
        o_ref, st_ref, kept_ref, s_scr = refs[n_in:]

        @pl.when(pl.program_id(1) == 0)
        def _():
            s_scr[...] = jnp.zeros_like(s_scr)

        cols = [slice(b * LANES, (b + 1) * LANES) for b in range(grp)]
        s = s_scr[...]
        st_ref[...] = s
        o, s_new, inv = chunk_fn(*[jnp.stack([r[:, c] for c in cols]) for r in refs[:n_in]], s)
        for b, c in enumerate(cols):
            o_ref[:, c] = o[b]
        kept_ref[...] = inv
        s_scr[...] = s_new

    def spec(off):
        return pl.BlockSpec((chunk, grp * LANES), lambda h, c: (c, h + off // grp))

    def per_chunk(shape):
        return pl.BlockSpec((grp, None) + tuple(shape), lambda h, c: (h, c, 0, 0))

    return pl.pallas_call(
        body, name=name, grid=(ncol // grp, n), in_specs=[spec(off) for _, off in ins],
        out_specs=[spec(0), per_chunk((LANES, LANES)), per_chunk(kept)],
        out_shape=[jax.ShapeDtypeStruct((rows, ncol * LANES), f32),
                   jax.ShapeDtypeStruct((ncol, n, LANES, LANES), f32),
                   jax.ShapeDtypeStruct((ncol, n) + tuple(kept), f32)],
        scratch_shapes=[pltpu.VMEM((grp, LANES, LANES), f32)], compiler_params=_params(2))(*[a for a, _ in ins])


def scan_bwd(name, chunk_fn, ins, states, kept, d_out, d_off, *, rows, chunk, ncol):
    n = rows // chunk
    n_in = len(ins)
    grp = _scan_group(ncol, [off for _, off in ins] + [d_off])

    def body(*refs):
        st_ref, kept_ref, do_ref = refs[n_in:n_in + 3]
        g_refs = refs[n_in + 3:2 * n_in + 3]
        ds_scr = refs[-1]

        @pl.when(pl.program_id(1) == 0)
        def _():
            ds_scr[...] = jnp.zeros_like(ds_scr)

        cols = [slice(b * LANES, (b + 1) * LANES) for b in range(grp)]

        def batch(ref):
            return jnp.stack([ref[:, c] for c in cols])

        inv = kept_ref[...]
        _, pull = jax.vjp(lambda *a: chunk_fn(*a, kept=inv)[:2], *[batch(r) for r in refs[:n_in]], st_ref[...])
        gs = pull((batch(do_ref), ds_scr[...]))
        for ref, g in zip(g_refs, gs[:n_in]):
            for b, c in enumerate(cols):
                ref[:, c] = g[b]
        ds_scr[...] = gs[n_in]

    def spec(off):
        return pl.BlockSpec((chunk, grp * LANES), lambda h, c: (n - 1 - c, h + off // grp))

    def per_chunk(shape):
        return pl.BlockSpec((grp, None) + tuple(shape), lambda h, c: (h, n - 1 - c, 0, 0))

    return pl.pallas_call(
        body, name=name, grid=(ncol // grp, n),
        in_specs=[spec(off) for _, off in ins] + [per_chunk(states.shape[2:]), per_chunk(kept.shape[2:]), spec(d_off)],
        out_specs=[spec(0)] * n_in, out_shape=[jax.ShapeDtypeStruct((rows, ncol * LANES), f32)] * n_in,
        scratch_shapes=[pltpu.VMEM((grp, LANES, LANES), f32)],
        compiler_params=_params(2))(*[a for a, _ in ins], states, kept, d_out)


def flip_exchange(name, arrs, flips, n_slots, slot_of, src_of, with_self, after=()):
    n = len(arrs)
    nf = len(flips)
    n_after = len(after)

    def body(*refs):
        ins, outs = refs[:n], refs[n + n_after:2 * n + n_after]
        send, recv, lsem = refs[2 * n + n_after:]
        me = (lax.axis_index("x"), lax.axis_index("y"), lax.axis_index("c"))
        copies = []
        for k in range(n):
            if with_self:
                cp = pltpu.make_async_copy(src_of(ins[k], me), outs[k].at[slot_of(me)], lsem.at[k])
                cp.start()
                copies.append(cp)
            for j, fl in enumerate(flips):
                peer = tuple(1 - m if f else m for m, f in zip(me, fl))
                cp = pltpu.make_async_remote_copy(
                    src_ref=src_of(ins[k], peer), dst_ref=outs[k].at[slot_of(me)], send_sem=send.at[k, j],
                    recv_sem=recv.at[k, j], device_id=peer, device_id_type=MESH)
                cp.start()
                copies.append(cp)
        for cp in copies:
            cp.wait()

    def out_sds(a):
        blk = src_of(jax.ShapeDtypeStruct(a.shape, a.dtype), None)
        return jax.ShapeDtypeStruct((n_slots,) + tuple(blk), a.dtype)

    any_spec = pl.BlockSpec(memory_space=pl.ANY)
    return pl.pallas_call(
        body, name=name, in_specs=[any_spec] * (n + n_after), out_specs=[any_spec] * n,
        out_shape=[out_sds(a) for a in arrs],
        scratch_shapes=[pltpu.SemaphoreType.DMA((n, nf)), pltpu.SemaphoreType.DMA((n, nf)),
                        pltpu.SemaphoreType.DMA((n,))],
        compiler_params=pltpu.CompilerParams(has_side_effects=True))(*arrs, *after)


_CHIP_FLIPS = ((1, 0, 0), (0, 1, 0), (1, 1, 0))
_ALL_FLIPS = ((0, 0, 1), (0, 1, 0), (0, 1, 1), (1, 0, 0), (1, 0, 1), (1, 1, 0), (1, 1, 1))


def _whole(ref, pos):
    return ref.shape if pos is None else ref


def _chip_block(ref, pos):
    return ref.shape[1:] if pos is None else ref.at[2 * pos[0] + pos[1]]


def _chip_slot(p):
    return 2 * p[0] + p[1]


_HBM = pl.BlockSpec(memory_space=pltpu.HBM)
_SEM = pl.BlockSpec(memory_space=pltpu.SEMAPHORE)
_DATAFLOW = pltpu.SideEffectType.DATAFLOW_SIDE_EFFECTING


def _split_copies(mode, refs, n, send, recv):
    me = (lax.axis_index("x"), lax.axis_index("y"), lax.axis_index("c"))
    sib = (me[0], me[1], 1 - me[2])
    lands = refs[:n] if mode == "handover" else refs[n:2 * n]
    copies = []
    for k, land in enumerate(lands):
        half = land.shape[1] // 2
        mine = pl.ds(pl.multiple_of(me[2] * half, 16), half)
        for j, fl in enumerate(_CHIP_FLIPS):
            peer = tuple(1 - m if f else m for m, f in zip(me, fl))
            if mode == "gather":
                src, dst, to = refs[k], land.at[_chip_slot(me)], peer
            elif mode == "scatter":
                src, dst, to = refs[k].at[_chip_slot(peer)], land.at[_chip_slot(me)], peer
            elif mode == "gather_half":
                src, dst, to = refs[k].at[mine], land.at[_chip_slot(me), mine], peer
            else:
                src = dst = land.at[_chip_slot(peer), mine]
                to = sib
            q = k * len(_CHIP_FLIPS) + j
            copies.append(pltpu.make_async_remote_copy(src_ref=src, dst_ref=dst, send_sem=send.at[q],
                                                       recv_sem=recv.at[q], device_id=to, device_id_type=MESH))
    return copies


def split_start(name, mode, ops, n):
    ops = [pltpu.with_memory_space_constraint(a, pltpu.HBM) for a in ops]
    m = len(ops)

    def body(*refs):
        for cp in _split_copies(mode, refs[:m], n, refs[m], refs[m + 1]):
            cp.start()
        refs[-1][...] = jnp.zeros_like(refs[-1])

    sems = pltpu.SemaphoreType.DMA((n * len(_CHIP_FLIPS),))
    outs = pl.pallas_call(
        body, name=name, in_specs=[_HBM] * m,
        out_shape=(sems, sems, *[pltpu.HBM(a.shape, a.dtype) for a in ops], jax.ShapeDtypeStruct((8, LANES), f32)),
        out_specs=(_SEM, _SEM, *[_HBM] * m, pl.BlockSpec(memory_space=pltpu.VMEM)),
        input_output_aliases={i: 2 + i for i in range(m)},
        compiler_params=pltpu.CompilerParams(has_side_effects=_DATAFLOW))(*ops)
    return (outs[0], outs[1], list(outs[2:2 + m]), mode, n), outs[-1][0, 0]


def split_wait(name, state, after, with_sources=False):
    send, recv, ops, mode, n = state
    m = len(ops)
    afters = list(after) if isinstance(after, (list, tuple)) else [after]

    def body(*refs):
        for cp in _split_copies(mode, refs[:m], n, refs[m], refs[m + 1]):
            cp.wait_send()
            cp.wait_recv()

    outs = pl.pallas_call(
        body, name=name, in_specs=[_HBM] * m + [_SEM, _SEM] + [pl.BlockSpec(memory_space=pl.ANY)] * len(afters),
        out_shape=tuple(pltpu.HBM(a.shape, a.dtype) for a in ops), out_specs=tuple([_HBM] * m),
        input_output_aliases={i: i for i in range(m)},
        compiler_params=pltpu.CompilerParams(has_side_effects=_DATAFLOW))(*ops, send, recv, *afters)
    return (list(outs[m - n:]), list(outs[:m - n])) if with_sources else list(outs[m - n:])


def chips_start(name, arrs, src_of, halves=False):
    me = _chip_slot((lax.axis_index("x"), lax.axis_index("y")))
    lands = []
    for a in arrs:
        blk = tuple(src_of(jax.ShapeDtypeStruct(a.shape, a.dtype), None))
        land = lax.empty((4,) + blk, a.dtype)
        lands.append(lax.dynamic_update_index_in_dim(land, a, me, 0) if src_of is _whole else land)
    mode = "scatter" if src_of is _chip_block else ("gather_half" if halves else "gather")
    return split_start(name, mode, list(arrs) + lands, len(arrs))


chips_wait = split_wait


def handover_start(name, lands):
    return split_start(name, "handover", lands, len(lands))


def gather_chips_halves(name, arrs):
    n = len(arrs)
    nf = len(_CHIP_FLIPS)
    split = [a.shape[0] % 32 == 0 for a in arrs]

    own = _chip_slot((lax.axis_index("x"), lax.axis_index("y")))
    lands = [lax.dynamic_update_index_in_dim(lax.empty((4,) + a.shape, a.dtype), a, own, 0) for a in arrs]

    def body(*refs):
        ins, outs = refs[:n], refs[2 * n:3 * n]
        send1, recv1, send2, recv2 = refs[3 * n:]
        me = (lax.axis_index("x"), lax.axis_index("y"), lax.axis_index("c"))
        sib = (me[0], me[1], 1 - me[2])
        peers = [tuple(1 - m if f else m for m, f in zip(me, fl)) for fl in _CHIP_FLIPS]
        local, first, second = [], [], []
        for k in range(n):
            half = ins[k].shape[0] // 2
            rows = pl.ds(pl.multiple_of(me[2] * half, 16), half) if split[k] else pl.ds(0, ins[k].shape[0])
            for j, peer in enumerate(peers):
                cp = pltpu.make_async_remote_copy(
                    src_ref=ins[k].at[rows], dst_ref=outs[k].at[_chip_slot(me), rows], send_sem=send1.at[k, j],
                    recv_sem=recv1.at[k, j], device_id=peer, device_id_type=MESH)
                cp.start()
                first.append((k, j, rows, cp))
        for k, j, rows, cp in first:
            cp.wait_recv()
            if split[k]:
                got = outs[k].at[_chip_slot(peers[j]), rows]
                fwd = pltpu.make_async_remote_copy(src_ref=got, dst_ref=got, send_sem=send2.at[k, j],
                                                   recv_sem=recv2.at[k, j], device_id=sib, device_id_type=MESH)
                fwd.start()
                second.append(fwd)
        for _, _, _, cp in first:
            cp.wait_send()
        for cp in second:
            cp.wait()
        for cp in local:
            cp.wait()

    any_spec = pl.BlockSpec(memory_space=pl.ANY)
    sems = pltpu.SemaphoreType.DMA((n, nf))
    return pl.pallas_call(
        body, name=name, in_specs=[any_spec] * (2 * n), out_specs=[any_spec] * n,
        out_shape=[jax.ShapeDtypeStruct((4,) + a.shape, a.dtype) for a in arrs],
        input_output_aliases={n + k: k for k in range(n)},
        scratch_shapes=[sems, sems, sems, sems],
        compiler_params=pltpu.CompilerParams(has_side_effects=True))(*arrs, *lands)


def swap_sibling(name, arrs):
    outs = flip_exchange(name, arrs, ((0, 0, 1),), 1, lambda p: 0, _whole, False)
    return [o[0] for o in outs]


def gather_all(arrs, after=()):
    return flip_exchange("gather_all", arrs, _ALL_FLIPS, 8, lambda p: 4 * p[0] + 2 * p[1] + p[2], _whole, True,
                         after=after)


def _row_tile(nr, nc, n_arrays):
    budget = (20 << 20) // (n_arrays * 2 * 4 * max(nc, LANES))
    t = min(nr, budget) // 16 * 16
    while t > 0 and nr % t:
        t -= 16
    return t if t > 0 else nr


def sum_slots(name, x, own):
    ns, nr, nc = x.shape
    tile = _row_tile(nr, nc, ns + 1)
    me = jnp.reshape(_chip_slot((lax.axis_index("x"), lax.axis_index("y"))), (1,)).astype(jnp.int32)

    def body(me_ref, own_ref, *refs):
        o_ref = refs[-1]
        s = own_ref[...].astype(f32)
        for r in refs[:-1]:
            s = s + r[...].astype(f32)
        o_ref[...] = s.astype(o_ref.dtype)

    def slot(q):
        return pl.BlockSpec((None, tile, nc), lambda i, t: ((t[0] + q) % ns, i, 0))

    return pl.pallas_call(
        body, name=name, out_shape=jax.ShapeDtypeStruct((nr, nc), x.dtype),
        grid_spec=pltpu.PrefetchScalarGridSpec(
            num_scalar_prefetch=1, grid=(nr // tile,), in_specs=[slot(q) for q in range(ns)],
            out_specs=pl.BlockSpec((tile, nc), lambda i, t: (i, 0))),
        compiler_params=_params(1))(me, own, *[x] * (ns - 1))


def adamw(name, w, g_parts, m, v):
    nr, nc = w.shape[-2:]
    n_g = len(g_parts)
    tile = _row_tile(nr, nc, 7 + n_g)

    def body(*refs):
        w_ref, m_ref, v_ref = refs[:3]
        g = refs[3][...].astype(f32)
        for r in refs[4:3 + n_g]:
            g = g + r[...].astype(f32)
        g_ref, d_ref, nm_ref, nv_ref = refs[3 + n_g:]
        nm = ADAM_B1 * m_ref[...] + (1.0 - ADAM_B1) * g
        nv = ADAM_B2 * v_ref[...] + (1.0 - ADAM_B2) * jnp.square(g)
        m_hat = nm / (1.0 - ADAM_B1 ** ADAM_STEP)
        v_hat = nv / (1.0 - ADAM_B2 ** ADAM_STEP)
        g_ref[...] = g
        d_ref[...] = -ADAM_LR * (m_hat / (jnp.sqrt(v_hat) + ADAM_EPS) + ADAM_WD * w_ref[...])
        nm_ref[...] = nm
        nv_ref[...] = nv

    spec = pl.BlockSpec((tile, nc), lambda i: (i, 0))
    spec3 = pl.BlockSpec((None, tile, nc), lambda i: (0, i, 0)) if w.ndim == 3 else spec
    return pl.pallas_call(
        body, name=name, grid=(nr // tile,), in_specs=[spec3] * 3 + [spec] * n_g, out_specs=[spec3] * 4,
        out_shape=[jax.ShapeDtypeStruct(w.shape, f32)] * 4, compiler_params=_params(1))(w, m, v, *g_parts)


def adamw_packed(w, g8, m, v):
    nr, nc = w.shape

    def body(w_ref, g_ref, m_ref, v_ref, go_ref, d_ref, nm_ref, nv_ref):
        g = g_ref[0]
        for q in range(1, 8):
            g = g + g_ref[q]
        nm = ADAM_B1 * m_ref[...] + (1.0 - ADAM_B1) * g
        nv = ADAM_B2 * v_ref[...] + (1.0 - ADAM_B2) * jnp.square(g)
        m_hat = nm / (1.0 - ADAM_B1 ** ADAM_STEP)
        v_hat = nv / (1.0 - ADAM_B2 ** ADAM_STEP)
        go_ref[...] = g
        d_ref[...] = -ADAM_LR * (m_hat / (jnp.sqrt(v_hat) + ADAM_EPS) + ADAM_WD * w_ref[...])
        nm_ref[...] = nm
        nv_ref[...] = nv

    return pl.pallas_call(body, name="adamw_packed", out_shape=[jax.ShapeDtypeStruct((nr, nc), f32)] * 4,
                          compiler_params=pltpu.CompilerParams(vmem_limit_bytes=VMEM_LIMIT))(w, g8, m, v)


def _pack(vectors):
    rows = []
    for a in vectors:
        flat = a.reshape(-1).astype(f32)
        pad = (-flat.shape[0]) % LANES
        rows.append(jnp.pad(flat, (0, pad)).reshape(-1, LANES))
    packed = jnp.concatenate(rows, axis=0)
    return jnp.pad(packed, ((0, (-packed.shape[0]) % 8), (0, 0)))


def _unpack(packed, like):
    out, r = [], 0
    for a in like:
        n = a.size
        nr = -(-n // LANES)
        out.append(packed[r:r + nr].reshape(-1)[:n].reshape(a.shape))
        r += nr
    return out


def kernel(x, mem, mix_norm_w, w_in, dn_conv_w, dn_a_log, dn_dt_bias, dn_norm_w, rw_mu, rw_w0, rw_w2, rw_a0, rw_a2, rw_g2, rw_k_k, rw_k_a, rw_r_k, rw_ln_w, rw_ln_b, w_out, xa_norm_w, mem_norm_w, xa_wq, xa_wk, xa_wv, xa_wo, ffn_norm_w, ffn_w1, ffn_w2, final_norm_w, loss_target, m_mix_norm_w, m_w_in, m_dn_conv_w, m_dn_a_log, m_dn_dt_bias, m_dn_norm_w, m_rw_mu, m_rw_w0, m_rw_w2, m_rw_a0, m_rw_a2, m_rw_g2, m_rw_k_k, m_rw_k_a, m_rw_r_k, m_rw_ln_w, m_rw_ln_b, m_w_out, m_xa_norm_w, m_mem_norm_w, m_xa_wq, m_xa_wk, m_xa_wv, m_xa_wo, m_ffn_norm_w, m_ffn_w1, m_ffn_w2, m_final_norm_w, v_mix_norm_w, v_w_in, v_dn_conv_w, v_dn_a_log, v_dn_dt_bias, v_dn_norm_w, v_rw_mu, v_rw_w0, v_rw_w2, v_rw_a0, v_rw_a2, v_rw_g2, v_rw_k_k, v_rw_k_a, v_rw_r_k, v_rw_ln_w, v_rw_ln_b, v_w_out, v_xa_norm_w, v_mem_norm_w, v_xa_wq, v_xa_wk, v_xa_wv, v_xa_wo, v_ffn_norm_w, v_ffn_w1, v_ffn_w2, v_final_norm_w):
    weights = dict(mix_norm_w=mix_norm_w, w_in=w_in, dn_conv_w=dn_conv_w, dn_a_log=dn_a_log, dn_dt_bias=dn_dt_bias, dn_norm_w=dn_norm_w, rw_mu=rw_mu, rw_w0=rw_w0, rw_w2=rw_w2, rw_a0=rw_a0, rw_a2=rw_a2, rw_g2=rw_g2, rw_k_k=rw_k_k, rw_k_a=rw_k_a, rw_r_k=rw_r_k, rw_ln_w=rw_ln_w, rw_ln_b=rw_ln_b, w_out=w_out, xa_norm_w=xa_norm_w, mem_norm_w=mem_norm_w, xa_wq=xa_wq, xa_wk=xa_wk, xa_wv=xa_wv, xa_wo=xa_wo, ffn_norm_w=ffn_norm_w, ffn_w1=ffn_w1, ffn_w2=ffn_w2, final_norm_w=final_norm_w)
    mom_m = dict(mix_norm_w=m_mix_norm_w, w_in=m_w_in, dn_conv_w=m_dn_conv_w, dn_a_log=m_dn_a_log, dn_dt_bias=m_dn_dt_bias, dn_norm_w=m_dn_norm_w, rw_mu=m_rw_mu, rw_w0=m_rw_w0, rw_w2=m_rw_w2, rw_a0=m_rw_a0, rw_a2=m_rw_a2, rw_g2=m_rw_g2, rw_k_k=m_rw_k_k, rw_k_a=m_rw_k_a, rw_r_k=m_rw_r_k, rw_ln_w=m_rw_ln_w, rw_ln_b=m_rw_ln_b, w_out=m_w_out, xa_norm_w=m_xa_norm_w, mem_norm_w=m_mem_norm_w, xa_wq=m_xa_wq, xa_wk=m_xa_wk, xa_wv=m_xa_wv, xa_wo=m_xa_wo, ffn_norm_w=m_ffn_norm_w, ffn_w1=m_ffn_w1, ffn_w2=m_ffn_w2, final_norm_w=m_final_norm_w)
    mom_v = dict(mix_norm_w=v_mix_norm_w, w_in=v_w_in, dn_conv_w=v_dn_conv_w, dn_a_log=v_dn_a_log, dn_dt_bias=v_dn_dt_bias, dn_norm_w=v_dn_norm_w, rw_mu=v_rw_mu, rw_w0=v_rw_w0, rw_w2=v_rw_w2, rw_a0=v_rw_a0, rw_a2=v_rw_a2, rw_g2=v_rw_g2, rw_k_k=v_rw_k_k, rw_k_a=v_rw_k_a, rw_r_k=v_rw_r_k, rw_ln_w=v_rw_ln_w, rw_ln_b=v_rw_ln_b, w_out=v_w_out, xa_norm_w=v_xa_norm_w, mem_norm_w=v_mem_norm_w, xa_wq=v_xa_wq, xa_wk=v_xa_wk, xa_wv=v_xa_wv, xa_wo=v_xa_wo, ffn_norm_w=v_ffn_norm_w, ffn_w1=v_ffn_w1, ffn_w2=v_ffn_w2, final_norm_w=v_final_norm_w)
    names = list(weights)

    seq, d = x.shape[1], x.shape[2]
    dnw = d // 2
    rww = d - dnw
    nh, nb = dnw // LANES, rww // LANES
    n_mem = mem.shape[1]
    lw_dim, la_dim, lg_dim = rw_w2.shape[1], rw_a2.shape[1], rw_g2.shape[1]
    assert lw_dim + la_dim == LANES and lg_dim == LANES and dnw % LANES == 0 and rww % LANES == 0
    xs, mems, tgt = x[0], mem[0], loss_target[0]

    col_sharded = ("w_in", "xa_wo", "ffn_w1", "dn_conv_w", "rw_w2", "rw_a2", "rw_g2")
    row_sharded = ("w_out", "xa_wq", "xa_wk", "xa_wv", "ffn_w2")
    f32_payload = ("dn_conv_w", "rw_w2", "rw_a2", "rw_g2")
    sharded = col_sharded + row_sharded
    payload = {n: weights[n][0].astype(f32 if n in f32_payload else bf16) for n in sharded}
    shard_w = w_in.shape[2]
    pad_w = -(-shard_w // LANES) * LANES
    first = ("w_in", "dn_conv_w", "rw_w2", "rw_a2", "rw_g2")
    mid = ("w_out", "xa_wq", "xa_wk", "xa_wv", "xa_wo")
    late = ("ffn_w1", "ffn_w2")
    gathered = dict(zip(first, gather_chips_halves("gather_first", [payload[n] for n in first])))
    ordered = lax.optimization_barrier(([gathered[n] for n in first], [payload[n] for n in mid + late]))
    gathered = dict(zip(first, ordered[0]))
    payload.update(zip(mid + late, ordered[1]))
    mid_state, tok_mid = chips_start("gather_mid_start", [payload[n] for n in mid], _whole, halves=True)
    late_state, tok_late = chips_start("gather_late_start", [payload[n] for n in late], _whole, halves=True)
    mix_norm_w_in = mix_norm_w + (tok_mid + tok_late)

    def full(n):
        g = gathered[n]
        if n in col_sharded:
            return g.transpose(1, 0, 2).reshape(g.shape[1], 4 * g.shape[2])
        return g.reshape(4 * g.shape[1], g.shape[2])

    c_rw0 = 4 * dnw + 2 * nh
    c_rw = 3 * rww
    eb_ab, eb_r = 4 * nh, 4 * nh + 1
    eb_l1 = eb_r + 3 * nb
    n_ext = -(-(eb_l1 + 2) // 4) * 4
    tbl_fwd, tbl_bwd, per_shard = w_in_layout(4 * shard_w, shard_w, c_rw0, LANES - 2 * nh, n_ext)
    sub_f = max(s for s in range(1, REGROUP_SUB + 1) if n_ext % s == 0)
    sub_b = max(s for s in range(1, REGROUP_SUB + 1) if per_shard % s == 0)
    w_ext = lane_regroup(
        "w_in_regroup", gathered["w_in"], tbl_fwd,
        lambda p, b: pl.BlockSpec(
            (None, d, LANES), lambda i, j, t: (t[REGROUP_FIELDS * p, j * sub_f + b], 0,
                                               t[REGROUP_FIELDS * p + 1, j * sub_f + b])),
        pl.BlockSpec((d, sub_f * LANES), lambda i, j, t: (0, j)), (d, n_ext * LANES), (1, n_ext // sub_f), shard_w,
        sub_f)
    conv_w = full("dn_conv_w")
    w2p = jnp.concatenate([full("rw_w2"), jnp.zeros((la_dim, rww), f32)], axis=0)
    a2p = jnp.concatenate([jnp.zeros((lw_dim, rww), f32), full("rw_a2")], axis=0)
    g2 = full("rw_g2")
    xaw = xa_wq.shape[2]
    nxh = xaw // LANES
    ffn = 4 * ffn_w1.shape[2]

    def lane_row(vec):
        return jnp.pad(vec.reshape(1, -1), ((0, 0), (0, LANES - vec.size)))

    alog_row, dtb_row = lane_row(dn_a_log), lane_row(dn_dt_bias)
    head_of_col = jnp.arange(dnw)[None, :] // LANES
    e_g = (jnp.arange(LANES)[:, None] == head_of_col).astype(f32)
    e_b = (jnp.arange(LANES)[:, None] == head_of_col + nh).astype(f32)
    mu_main, mu_small = rw_mu[:, :c_rw], rw_mu[:, c_rw:]
    r_k_row = rw_r_k.reshape(1, rww)
    fnw = final_norm_w.reshape(1, d)
    qb, kb_, vb, zb = 0, nh, 2 * nh, 3 * nh
    rb0 = eb_r
    cc = lambda w_, o_: ("constc", w_, o_)
    rc = lambda o_: ("rowc", LANES, o_)

    (u16,) = rowcall("norm_mix", fn_rms, [(xs, "row"), (mix_norm_w_in, "const")], [(seq, d, bf16, "row")], rows=seq)
    p_main = p_small = mm("in_proj", u16, w_ext)

    fn_gconv = make_fn_gconv(2 * nh)
    (qkv,) = rowcall("gdn_conv", fn_gconv, [(p_main, rc(0)), (conv_w, cc(LANES, 0))],
                     [(seq, 3 * dnw, f32, "rowc")], rows=seq, tile=seq, ncol=3 * nh)
    gate_ins = [(p_small, rc(eb_ab)), (alog_row, "const"), (dtb_row, "const"), (e_g, "const"), (e_b, "const")]
    g_b, beta_b = rowcall("gdn_gate", fn_ggate, gate_ins, [(seq, dnw, f32, "row")] * 2, rows=seq)
    gdn_ins = [(qkv, qb), (qkv, kb_), (qkv, vb), (g_b, 0), (beta_b, 0)]
    o_raw, gdn_states, gdn_kept = scan_fwd("gdn_scan", gdn_chunk, gdn_ins, rows=seq, chunk=GDN_CHUNK, ncol=nh)
    mid_state, tok = handover_start("gather_mid_pass", chips_wait("gather_mid_wait", mid_state, o_raw))
    gpost_ins = [(o_raw, rc(0)), (p_main, rc(zb)), (dn_norm_w + tok, "const")]
    (o_dn,) = rowcall("gdn_post", fn_gpost, gpost_ins, [(seq, dnw, bf16, "rowc")], rows=seq, ncol=nh)

    (prw,) = rowcall("rw_lerp_main", fn_lerp, [(p_main, rc(rb0)), (mu_main, cc(LANES, 0))],
                     [(seq, c_rw, f32, "rowc")], rows=seq, tile=seq, ncol=3 * nb)
    (psl,) = rowcall("rw_lerp_small", fn_lerp, [(p_small, rc(eb_l1)), (mu_small, cc(LANES, 0))],
                     [(seq, 2 * LANES, f32, "rowc")], rows=seq, tile=seq, ncol=2)
    rprep_ins = [(prw, rc(nb)), (psl, "row"), (rw_w0, cc(LANES, 0)), (rw_a0, cc(LANES, 0)), (rw_k_k, cc(LANES, 0)),
                 (rw_k_a, cc(LANES, 0)), (w2p, cc(LANES, 0)), (a2p, cc(LANES, 0)), (g2, cc(LANES, 0))]
    lw, kmod, kk, a_rw, gate = rowcall("rw_prep", fn_rprep, rprep_ins, [(seq, rww, f32, "rowc")] * 5,
                                        rows=seq, ncol=nb)
    wkv_ins = [(prw, 0), (lw, 0), (kmod, 0), (prw, 2 * nb), (kk, 0), (a_rw, 0)]
    y_rw, wkv_states, wkv_kept = scan_fwd("wkv_scan", wkv_chunk, wkv_ins, rows=seq, chunk=WKV_CHUNK, ncol=nb)
    rpost_ins = [(y_rw, rc(0)), (prw, rc(0)), (kmod, rc(0)), (prw, rc(2 * nb)), (gate, rc(0)),
                 (r_k_row, cc(LANES, 0)), (rw_ln_w, cc(LANES, 0)), (rw_ln_b, cc(LANES, 0))]
    (o_rw,) = rowcall("rw_post", fn_rpost, rpost_ins, [(seq, rww, bf16, "rowc")], rows=seq, ncol=nb)

    o_cat = jnp.concatenate([o_dn, o_rw], axis=1)
    late_state, tok = handover_start("gather_late_pass", chips_wait("gather_late_wait", late_state, o_cat))
    gathered.update(zip(mid, chips_wait("gather_mid_pass_wait", mid_state, o_cat)))
    w_out_f, wq_f, wk_f, wv_f, wo_f = full("w_out"), full("xa_wq"), full("xa_wk"), full("xa_wv"), full("xa_wo")
    h1 = mm("out_proj", o_cat, w_out_f, add=xs)

    (hn16,) = rowcall("norm_xa", fn_rms, [(h1, "row"), (xa_norm_w + tok, "const")], [(seq, d, bf16, "row")],
                      rows=seq)
    (mn16,) = rowcall("norm_mem", fn_rms, [(mems, "row"), (mem_norm_w, "const")], [(n_mem, d, bf16, "row")],
                      rows=n_mem)
    q_xa = mm("xa_q", hn16, wq_f)
    k_xa = mm("xa_k", mn16, wk_f)
    v_xa = mm("xa_v", mn16, wv_f)
    xcore_ins = [(q_xa, rc(0)), (k_xa, cc(LANES, 0)), (v_xa, cc(LANES, 0))]
    (o_xa,) = rowcall("xa_core", fn_xcore, xcore_ins, [(seq, xaw, bf16, "rowc")], rows=seq, ncol=nxh)
    h2 = mm("xa_o", o_xa, wo_f, add=h1)

    (fn16,) = rowcall("norm_ffn", fn_rms, [(h2, "row"), (ffn_norm_w, "const")], [(seq, d, bf16, "row")], rows=seq)
    gathered.update(zip(late, chips_wait("gather_late_pass_wait", late_state, fn16)))
    w1_f, w2_f = full("ffn_w1"), full("ffn_w2")
    a1_16, hid16 = mm("ffn_up", fn16, w1_f, epilogue=lambda r: (r, jnp.square(jnp.maximum(r, 0.0))),
                      out_dtypes=(bf16, bf16))
    h3 = mm("ffn_down", hid16, w2_f, add=h2)

    dh3, dh3_16, d_fnw, loss_rows = rowcall(
        "loss_head", fn_final, [(h3, "row"), (tgt, "row"), (fnw, "const")],
        [(seq, d, f32, "row"), (seq, d, bf16, "row"), (1, d, f32, "acc"), (8, LANES, f32, "acc")], rows=seq)

    da1_16 = mm("ffn_down_dx", dh3_16, w2_f, tb=True, extra=[a1_16], out_dtype=bf16,
                epilogue=lambda r, a1: (r * (2.0 * jnp.maximum(a1.astype(f32), 0.0)),))
    def by_chip(n, g):
        if g.ndim == 3:
            return g
        if n in col_sharded:
            return g.reshape(g.shape[0], 4, g.shape[1] // 4).transpose(1, 0, 2)
        return g.reshape(4, g.shape[0] // 4, g.shape[1])

    g_ffn_w2 = mm("ffn_down_dw", hid16, dh3_16, ta=True, out_dtype=bf16)
    ffn_w2_g, tok = chips_start("scatter_ffn_w2_start", [by_chip("ffn_w2", g_ffn_w2)], _chip_block)
    g_ffn_w1 = mm("ffn_up_dw", fn16, da1_16, ta=True, out_dtype=bf16, by_chip=True, after=tok)
    ffn_w1_g, tok = chips_start("scatter_ffn_w1_start", [g_ffn_w1], _chip_block)
    dfn = mm("ffn_up_dx", da1_16, w1_f, tb=True, after=tok)
    dh2, d_ffn_nw, dh2_16 = rowvjp("norm_ffn_bwd", fn_rms, [(h2, "row"), (ffn_norm_w, "const")],
                                   [[(dfn, "row")]], [0, 1], rows=seq, adds=[(0, dh3, "row")], dup16=[0])

    do_xa = mm("xa_o_dx", dh2_16, wo_f, tb=True)
    g_xa_wo = mm("xa_o_dw", o_xa, dh2_16, ta=True, out_dtype=bf16, by_chip=True)
    dq_xa, dk_xa, dv_xa, dq16 = rowvjp("xa_core_bwd", fn_xcore, xcore_ins, [[(do_xa, rc(0))]], [0, 1, 2],
                                       rows=seq, ncol=nxh, dup16=[0])
    g_xa_wq = mm("xa_q_dw", hn16, dq16, ta=True, out_dtype=bf16)
    dhn = mm("xa_q_dx", dq16, wq_f, tb=True)
    dh1, d_xa_nw, dh1_16 = rowvjp("norm_xa_bwd", fn_rms, [(h1, "row"), (xa_norm_w, "const")], [[(dhn, "row")]],
                                  [0, 1], rows=seq, adds=[(0, dh2, "row")], dup16=[0])
    dk16, dv16 = dk_xa.astype(bf16), dv_xa.astype(bf16)
    g_xa_wk = mm("xa_k_dw", mn16, dk16, ta=True, out_dtype=bf16)
    g_xa_wv = mm("xa_v_dw", mn16, dv16, ta=True, out_dtype=bf16)
    dmn = mm("xa_v_dx", dv16, wv_f, tb=True, add=mm("xa_k_dx", dk16, wk_f, tb=True))
    (d_mem_nw,) = rowvjp("norm_mem_bwd", fn_rms, [(mems, "row"), (mem_norm_w, "const")], [[(dmn, "row")]], [1],
                         rows=n_mem)

    g_w_out = mm("out_proj_dw", o_cat, dh1_16, ta=True, out_dtype=bf16)
    mid_grads = dict(w_out=g_w_out, xa_wq=g_xa_wq, xa_wk=g_xa_wk, xa_wv=g_xa_wv, xa_wo=g_xa_wo)
    mid_g, tok = chips_start("scatter_mid_start", [by_chip(n, mid_grads[n]) for n in mid], _chip_block)
    do_cat = mm("out_proj_dx", dh1_16, w_out_f, tb=True, after=tok)

    dy, dr_a, dkmod_a, dv_a, dgate, d_r_k, d_ln_w, d_ln_b = rowvjp(
        "rw_post_bwd", fn_rpost, rpost_ins, [[(do_cat, rc(nh))]], [0, 1, 2, 3, 4, 5, 6, 7], rows=seq, ncol=nb)
    dr_b, dlw, dkmod_b, dv_b, dkk, da_rw = scan_bwd("wkv_scan_bwd", wkv_chunk, wkv_ins, wkv_states, wkv_kept, dy, 0,
                                                    rows=seq, chunk=WKV_CHUNK, ncol=nb)
    rprep_cts = [[(dlw, rc(0))], [(dkmod_a, rc(0)), (dkmod_b, rc(0))], [(dkk, rc(0))], [(da_rw, rc(0))],
                 [(dgate, rc(0))]]
    dpk, dpsl_parts, d_w0, d_a0, d_k_k, d_k_a, d_w2p, d_a2p, d_g2 = rowvjp(
        "rw_prep_bwd", fn_rprep, rprep_ins, rprep_cts, [0, 1, 2, 3, 4, 5, 6, 7, 8], rows=seq, ncol=nb)
    (dpsl,) = rowcall("rw_prep_sum", fn_sumcols(nb), [(dpsl_parts, "row")], [(seq, 2 * LANES, f32, "row")], rows=seq)

    def lerp_bwd(tag, p, p_off, mu, mu_off, ct_lists, ncol):
        return rowvjp("rw_lerp_bwd_" + tag, fn_lerp, [(p, rc(p_off)), (mu, cc(LANES, mu_off))], [ct_lists], [0, 1],
                      rows=seq, tile=seq, ncol=ncol, dup16=[0])

    _, dmu_r, dpr16 = lerp_bwd("r", p_main, rb0, mu_main, 0, [(dr_a, rc(0)), (dr_b, rc(0))], nb)
    _, dmu_k, dpk16 = lerp_bwd("k", p_main, rb0 + nb, mu_main, nb, [(dpk, rc(0))], nb)
    _, dmu_v, dpv16 = lerp_bwd("v", p_main, rb0 + 2 * nb, mu_main, 2 * nb, [(dv_a, rc(0)), (dv_b, rc(0))], nb)
    _, dmu_s, dps12_16 = lerp_bwd("small", p_small, eb_l1, mu_small, 0, [(dpsl, rc(0))], 2)

    do_raw, dz, d_dn_nw, dz16 = rowvjp("gdn_post_bwd", fn_gpost, gpost_ins, [[(do_cat, rc(0))]], [0, 1, 2],
                                       rows=seq, ncol=nh, dup16=[1])
    dq_g, dk_g, dv_g, dg_b, dbeta_b = scan_bwd("gdn_scan_bwd", gdn_chunk, gdn_ins, gdn_states, gdn_kept, do_raw, 0,
                                               rows=seq, chunk=GDN_CHUNK, ncol=nh)
    dps0, d_alog, d_dtb, dps0_16 = rowvjp("gdn_gate_bwd", fn_ggate, gate_ins, [[(dg_b, "row")], [(dbeta_b, "row")]],
                                          [0, 1, 2], rows=seq, dup16=[0])
    dqkv = jnp.concatenate([dq_g, dk_g, dv_g], axis=1)
    _, d_conv_w, dqkv16 = rowvjp("gdn_conv_bwd", fn_gconv, [(p_main, rc(0)), (conv_w, cc(LANES, 0))],
                                 [[(dqkv, rc(0))]], [0, 1], rows=seq, tile=seq, ncol=3 * nh, dup16=[0])

    dp16 = jnp.concatenate([dqkv16, dz16, dps0_16, dpr16, dpk16, dpv16, dps12_16,
                            jnp.zeros((seq, (n_ext - eb_l1 - 2) * LANES), bf16)], axis=1)
    g_w_ext = mm("in_proj_dw", u16, dp16, ta=True, out_dtype=bf16)
    g_w_in = lane_regroup(
        "w_in_grad_regroup", g_w_ext, tbl_bwd,
        lambda p, b: pl.BlockSpec((d, LANES), lambda i, j, t: (0, t[REGROUP_FIELDS * p + 1, j * sub_b + b])),
        pl.BlockSpec((None, d, sub_b * LANES),
                     lambda i, j, t: (j // (per_shard // sub_b), 0, j % (per_shard // sub_b))),
        (4, d, pad_w), (1, 4 * per_shard // sub_b), n_ext * LANES, sub_b)
    first_grads = dict(w_in=g_w_in, dn_conv_w=d_conv_w, rw_w2=d_w2p[:lw_dim], rw_a2=d_a2p[lw_dim:], rw_g2=d_g2)
    first_g, tok = chips_start("scatter_first_start", [by_chip(n, first_grads[n]) for n in first], _chip_block)

    du = mm("in_proj_dx", dp16, w_ext, tb=True, after=tok)
    grad_x, d_mix_nw = rowvjp("norm_mix_bwd", fn_rms, [(xs, "row"), (mix_norm_w, "const")], [[(du, "row")]],
                              [0, 1], rows=seq, adds=[(0, dh1, "row")])
    received, sent = {}, {}

    def arrived(group, name, state, after):
        lands, sources = chips_wait(name, state, after, with_sources=True)
        received.update(zip(group, lands))
        sent.update(zip(group, sources))

    arrived(mid, "scatter_mid_wait", mid_g, grad_x)
    arrived(("ffn_w2",), "scatter_ffn_w2_wait", ffn_w2_g, grad_x)
    arrived(("ffn_w1",), "scatter_ffn_w1_wait", ffn_w1_g, grad_x)

    out = {}

    def reduce_and_update(tag, group):
        partial = [sum_slots("sum_chips_" + n, received[n], sent[n]) for n in group]
        other = swap_sibling("swap_sibling_" + tag, partial)
        for n, p_mine, p_other in zip(group, partial, other):
            if weights[n].shape[2] % LANES:
                rows, cols = weights[n].shape[1:]
                lin = lambda a: jnp.swapaxes(a, 1, 2).reshape(-1, LANES)
                lin_g = lambda p: p.T[:cols].reshape(-1, LANES)
                res = adamw("adamw_" + n, lin(weights[n]), [lin_g(p_mine), lin_g(p_other)], lin(mom_m[n]),
                            lin(mom_v[n]))
                out[n] = [jnp.swapaxes(r.reshape(1, cols, rows), 1, 2) for r in res]
            else:
                out[n] = adamw("adamw_" + n, weights[n], [p_mine, p_other], mom_m[n], mom_v[n])

    reduce_and_update("rest", mid + late)
    arrived(first, "scatter_first_wait", first_g, [out[n][1] for n in mid + late])
    reduce_and_update("first", first)

    small_names = [n for n in names if n not in sharded]
    small_local = dict(
        mix_norm_w=d_mix_nw, dn_a_log=d_alog[:, :nh], dn_dt_bias=d_dtb[:, :nh], dn_norm_w=d_dn_nw,
        rw_mu=jnp.concatenate([dmu_r, dmu_k, dmu_v, dmu_s], axis=1), rw_w0=d_w0, rw_a0=d_a0, rw_k_k=d_k_k,
        rw_k_a=d_k_a, rw_r_k=d_r_k, rw_ln_w=d_ln_w, rw_ln_b=d_ln_b, xa_norm_w=d_xa_nw, mem_norm_w=d_mem_nw,
        ffn_norm_w=d_ffn_nw, final_norm_w=d_fnw)
    loss_vec = jnp.where(jnp.arange(LANES) == 0, loss_rows[0], 0.0)
    (g8,) = gather_all([_pack([small_local[n] for n in small_names] + [loss_vec])], after=[out["w_in"][1]])

    packed_like = [weights[n] for n in small_names] + [loss_vec]
    zero = jnp.zeros((LANES,), f32)
    res = adamw_packed(_pack([weights[n] for n in small_names] + [zero]), g8,
                       _pack([mom_m[n] for n in small_names] + [zero]),
                       _pack([mom_v[n] for n in small_names] + [zero]))
    unpacked = [_unpack(r, packed_like) for r in res]
    for i, n in enumerate(small_names):
        out[n] = [u[i] for u in unpacked]
    loss = unpacked[0][-1][0]

    return (loss, grad_x.reshape(x.shape), *[out[n][0] for n in names], *[out[n][1] for n in names],
            *[out[n][2] for n in names], *[out[n][3] for n in names])
```
